```python
import math
import jax, jax.numpy as jnp
from jax import lax
import numpy as np

D_MODEL = 1024
BATCH = 8
SEQ = 4096
DEPTH = 1

POOL_WINDOWS = (2, 4, 8, 16)
POOL_GROUPS = len(POOL_WINDOWS)
POOL_WIDTH = D_MODEL // 2
POOL_GROUP_DIM = POOL_WIDTH // POOL_GROUPS
RNN_WIDTH = D_MODEL
RNN_HEADS = 16
RNN_HEAD_DIM = RNN_WIDTH // RNN_HEADS
RNN_CONV = 4
LRU_C = 8.0
D_FF = 3 * D_MODEL
FFN_CONV = 3
PLE_DIM = 256
N_BRANCH = 2
RMS_EPS = 1e-6
IN_POOL = POOL_WIDTH
IN_RNN_X = RNN_WIDTH
IN_RNN_G = RNN_WIDTH
IN_GATES = N_BRANCH * D_MODEL
IN_TOTAL = IN_POOL + IN_RNN_X + IN_RNN_G + IN_GATES

kernel_name = "hybrid_pool_rglru_gated_block"


def rms_norm(x, g):
    xf = x.astype(jnp.float32)
    y = xf * lax.rsqrt(jnp.mean(xf * xf, axis=-1, keepdims=True) + RMS_EPS) * g.astype(jnp.float32)
    return y.astype(x.dtype)


def causal_dwconv(u, w, b):
    k_w = w.shape[0]
    s = u.shape[1]
    up = jnp.pad(u, ((0, 0), (k_w - 1, 0), (0, 0)))
    out = b + up[:, 0:s] * w[0]
    for k in range(1, k_w):
        out = out + up[:, k:k + s] * w[k]
    return out


def pool_mixer(u, pool_w, pool_scale):
    b, s, _ = u.shape
    uf = u.astype(jnp.float32)
    cs = jnp.pad(jnp.cumsum(uf, axis=1), ((0, 0), (1, 0), (0, 0)))
    t = jnp.arange(s, dtype=jnp.int32)
    outs = []
    for g, w in enumerate(POOL_WINDOWS):
        sl = cs[..., g * POOL_GROUP_DIM:(g + 1) * POOL_GROUP_DIM]
        prev = jnp.pad(sl, ((0, 0), (w - 1, 0), (0, 0)))[:, :s]
        count = jnp.minimum(t + 1, w).astype(jnp.float32)[None, :, None]
        outs.append((sl[:, 1:] - prev) / count)
    mean = jnp.concatenate(outs, axis=-1)
    d = (mean - uf).astype(u.dtype).reshape(b, s, POOL_GROUPS, POOL_GROUP_DIM)
    y = jnp.einsum('bsgc,gcd->bsgd', d, pool_w).reshape(b, s, POOL_WIDTH)
    return y * pool_scale


def rg_lru_branch(xb, gb, conv_w, conv_b, w_gates, b_gates, lru_lambda):
    b, s, _ = xb.shape
    xc = causal_dwconv(xb, conv_w, conv_b)
    xh = xc.reshape(b, s, RNN_HEADS, RNN_HEAD_DIM)
    gates = jnp.einsum('bshi,ghij->gbshj', xh, w_gates).reshape(2, b, s, RNN_WIDTH)
    gates = gates.astype(jnp.float32) + b_gates.astype(jnp.float32)[:, None, None, :]
    r = jax.nn.sigmoid(gates[0])
    i = jax.nn.sigmoid(gates[1])
    log_a = -LRU_C * r * jax.nn.softplus(-lru_lambda.astype(jnp.float32))
    a = jnp.exp(log_a)
    mult = jnp.sqrt(-jnp.expm1(2.0 * log_a))
    t0 = (jnp.arange(s) == 0)[None, :, None]
    mult = jnp.where(t0, 1.0, mult)
    u = mult * i * xc.astype(jnp.float32)

    def combine(left, right):
        a_l, h_l = left
        a_r, h_r = right
        return a_l * a_r, a_r * h_l + h_r

    _, h = lax.associative_scan(combine, (a, u), axis=1)
    return h.astype(xb.dtype) * jax.nn.gelu(gb)


def _fwd_setup_inputs(seed: int = 0) -> dict:
    key = jax.random.key(seed)
    ks = jax.random.split(key, 32)
    f32 = jnp.float32
    L = DEPTH

    def nrm(k, shape, fan_in):
        return jax.random.normal(k, shape, f32) * (fan_in ** -0.5)

    def gain(k, shape):
        return 1.0 + 0.05 * jax.random.normal(k, shape, f32)

    a_c = jax.random.uniform(ks[10], (L, RNN_WIDTH), f32, 0.9, 0.999)
    s_l = a_c ** (1.0 / LRU_C)
    lru_lambda = jnp.log(s_l) - jnp.log1p(-s_l)

    return {
        "x": jax.random.normal(ks[0], (BATCH, SEQ, D_MODEL), f32),
        "p": jax.random.normal(ks[1], (DEPTH, BATCH, SEQ, PLE_DIM), f32),
        "g_mix_pre": gain(ks[2], (L, D_MODEL)),
        "g_mix_post": gain(ks[3], (L, D_MODEL)),
        "w_in": nrm(ks[4], (L, D_MODEL, IN_TOTAL), D_MODEL),
        "pool_w": nrm(ks[5], (L, POOL_GROUPS, POOL_GROUP_DIM, POOL_GROUP_DIM), POOL_GROUP_DIM),
        "pool_scale": gain(ks[6], (L, POOL_WIDTH)),
        "w_pool_out": nrm(ks[7], (L, POOL_WIDTH, D_MODEL), POOL_WIDTH),
        "conv_w": nrm(ks[8], (L, RNN_CONV, RNN_WIDTH), RNN_CONV),
        "conv_b": 0.02 * jax.random.normal(ks[9], (L, RNN_WIDTH), f32),
        "w_rg_gates": nrm(ks[11], (L, 2, RNN_HEADS, RNN_HEAD_DIM, RNN_HEAD_DIM), RNN_HEAD_DIM),
        "b_rg_gates": 0.02 * jax.random.normal(ks[12], (L, 2, RNN_WIDTH), f32),
        "lru_lambda": lru_lambda,
        "w_rg_out": nrm(ks[13], (L, RNN_WIDTH, D_MODEL), RNN_WIDTH),
        "w_o": nrm(ks[14], (L, D_MODEL, D_MODEL), D_MODEL),
        "g_ffn_pre": gain(ks[15], (L, D_MODEL)),
        "g_ffn_post": gain(ks[16], (L, D_MODEL)),
        "w_up": nrm(ks[17], (L, D_MODEL, 2 * D_FF), D_MODEL),
        "ffn_conv_w": nrm(ks[18], (L, FFN_CONV, D_FF), FFN_CONV),
        "ffn_conv_b": 0.02 * jax.random.normal(ks[19], (L, D_FF), f32),
        "w_down": nrm(ks[20], (L, D_FF, D_MODEL), D_FF),
        "g_ple_gate": gain(ks[21], (L, D_MODEL)),
        "w_ple_gate": nrm(ks[22], (L, D_MODEL, D_MODEL), D_MODEL),
        "w_ple_proj": nrm(ks[23], (L, PLE_DIM, D_MODEL), PLE_DIM),
        "g_ple_post": gain(ks[24], (L, D_MODEL)),
    }


def _fwd_reference(x, p, g_mix_pre, g_mix_post, w_in, pool_w, pool_scale, w_pool_out,
              conv_w, conv_b, w_rg_gates, b_rg_gates, lru_lambda, w_rg_out, w_o,
              g_ffn_pre, g_ffn_post, w_up, ffn_conv_w, ffn_conv_b, w_down,
              g_ple_gate, w_ple_gate, w_ple_proj, g_ple_post):
    for l in range(DEPTH):
        h = rms_norm(x, g_mix_pre[l])
        z = h @ w_in[l]
        c0 = IN_POOL
        c1 = c0 + IN_RNN_X
        c2 = c1 + IN_RNN_G
        u_pool = z[..., :c0]
        u_rx = z[..., c0:c1]
        u_rg = z[..., c1:c2]
        gate_pool = jax.nn.sigmoid(z[..., c2:c2 + D_MODEL])
        gate_rnn = jax.nn.sigmoid(z[..., c2 + D_MODEL:])

        y_pool = pool_mixer(u_pool, pool_w[l], pool_scale[l]) @ w_pool_out[l]
        y_rnn = rg_lru_branch(u_rx, u_rg, conv_w[l], conv_b[l], w_rg_gates[l],
                              b_rg_gates[l], lru_lambda[l]) @ w_rg_out[l]
        merged = gate_pool * y_pool + gate_rnn * y_rnn
        x = x + rms_norm(merged @ w_o[l], g_mix_post[l])

        h = rms_norm(x, g_ffn_pre[l])
        up = h @ w_up[l]
        gate_h = causal_dwconv(up[..., :D_FF], ffn_conv_w[l], ffn_conv_b[l])
        hid = jax.nn.gelu(gate_h) * up[..., D_FF:]
        x = x + rms_norm(hid @ w_down[l], g_ffn_post[l])

        ple_gate = jax.nn.sigmoid(rms_norm(x, g_ple_gate[l]) @ w_ple_gate[l])
        ple = rms_norm(p[l].astype(x.dtype) @ w_ple_proj[l], g_ple_post[l])
        x = x + ple_gate * ple
    return x


import jax as _jax
import jax.numpy as _jnp

TWIN_FORMAT = 'train_step'
FWD_PARAMS = ['x', 'p', 'g_mix_pre', 'g_mix_post', 'w_in', 'pool_w', 'pool_scale', 'w_pool_out', 'conv_w', 'conv_b', 'w_rg_gates', 'b_rg_gates', 'lru_lambda', 'w_rg_out', 'w_o', 'g_ffn_pre', 'g_ffn_post', 'w_up', 'ffn_conv_w', 'ffn_conv_b', 'w_down', 'g_ple_gate', 'w_ple_gate', 'w_ple_proj', 'g_ple_post']
TWIN_WEIGHTS = ['g_mix_pre', 'g_mix_post', 'w_in', 'pool_w', 'pool_scale', 'w_pool_out', 'conv_w', 'conv_b', 'w_rg_gates', 'b_rg_gates', 'lru_lambda', 'w_rg_out', 'w_o', 'g_ffn_pre', 'g_ffn_post', 'w_up', 'ffn_conv_w', 'ffn_conv_b', 'w_down', 'g_ple_gate', 'w_ple_gate', 'w_ple_proj', 'g_ple_post']
TWIN_DIFF_INPUT = 'x'
TWIN_INPUTS = ['x', 'p', 'g_mix_pre', 'g_mix_post', 'w_in', 'pool_w', 'pool_scale', 'w_pool_out', 'conv_w', 'conv_b', 'w_rg_gates', 'b_rg_gates', 'lru_lambda', 'w_rg_out', 'w_o', 'g_ffn_pre', 'g_ffn_post', 'w_up', 'ffn_conv_w', 'ffn_conv_b', 'w_down', 'g_ple_gate', 'w_ple_gate', 'w_ple_proj', 'g_ple_post', 'loss_target', 'm_g_mix_pre', 'm_g_mix_post', 'm_w_in', 'm_pool_w', 'm_pool_scale', 'm_w_pool_out', 'm_conv_w', 'm_conv_b', 'm_w_rg_gates', 'm_b_rg_gates', 'm_lru_lambda', 'm_w_rg_out', 'm_w_o', 'm_g_ffn_pre', 'm_g_ffn_post', 'm_w_up', 'm_ffn_conv_w', 'm_ffn_conv_b', 'm_w_down', 'm_g_ple_gate', 'm_w_ple_gate', 'm_w_ple_proj', 'm_g_ple_post', 'v_g_mix_pre', 'v_g_mix_post', 'v_w_in', 'v_pool_w', 'v_pool_scale', 'v_w_pool_out', 'v_conv_w', 'v_conv_b', 'v_w_rg_gates', 'v_b_rg_gates', 'v_lru_lambda', 'v_w_rg_out', 'v_w_o', 'v_g_ffn_pre', 'v_g_ffn_post', 'v_w_up', 'v_ffn_conv_w', 'v_ffn_conv_b', 'v_w_down', 'v_g_ple_gate', 'v_w_ple_gate', 'v_w_ple_proj', 'v_g_ple_post']
TWIN_OUTPUTS = ['loss', 'grad_x', 'grad_g_mix_pre', 'grad_g_mix_post', 'grad_w_in', 'grad_pool_w', 'grad_pool_scale', 'grad_w_pool_out', 'grad_conv_w', 'grad_conv_b', 'grad_w_rg_gates', 'grad_b_rg_gates', 'grad_lru_lambda', 'grad_w_rg_out', 'grad_w_o', 'grad_g_ffn_pre', 'grad_g_ffn_post', 'grad_w_up', 'grad_ffn_conv_w', 'grad_ffn_conv_b', 'grad_w_down', 'grad_g_ple_gate', 'grad_w_ple_gate', 'grad_w_ple_proj', 'grad_g_ple_post', 'delta_g_mix_pre', 'delta_g_mix_post', 'delta_w_in', 'delta_pool_w', 'delta_pool_scale', 'delta_w_pool_out', 'delta_conv_w', 'delta_conv_b', 'delta_w_rg_gates', 'delta_b_rg_gates', 'delta_lru_lambda', 'delta_w_rg_out', 'delta_w_o', 'delta_g_ffn_pre', 'delta_g_ffn_post', 'delta_w_up', 'delta_ffn_conv_w', 'delta_ffn_conv_b', 'delta_w_down', 'delta_g_ple_gate', 'delta_w_ple_gate', 'delta_w_ple_proj', 'delta_g_ple_post', 'new_m_g_mix_pre', 'new_m_g_mix_post', 'new_m_w_in', 'new_m_pool_w', 'new_m_pool_scale', 'new_m_w_pool_out', 'new_m_conv_w', 'new_m_conv_b', 'new_m_w_rg_gates', 'new_m_b_rg_gates', 'new_m_lru_lambda', 'new_m_w_rg_out', 'new_m_w_o', 'new_m_g_ffn_pre', 'new_m_g_ffn_post', 'new_m_w_up', 'new_m_ffn_conv_w', 'new_m_ffn_conv_b', 'new_m_w_down', 'new_m_g_ple_gate', 'new_m_w_ple_gate', 'new_m_w_ple_proj', 'new_m_g_ple_post', 'new_v_g_mix_pre', 'new_v_g_mix_post', 'new_v_w_in', 'new_v_pool_w', 'new_v_pool_scale', 'new_v_w_pool_out', 'new_v_conv_w', 'new_v_conv_b', 'new_v_w_rg_gates', 'new_v_b_rg_gates', 'new_v_lru_lambda', 'new_v_w_rg_out', 'new_v_w_o', 'new_v_g_ffn_pre', 'new_v_g_ffn_post', 'new_v_w_up', 'new_v_ffn_conv_w', 'new_v_ffn_conv_b', 'new_v_w_down', 'new_v_g_ple_gate', 'new_v_w_ple_gate', 'new_v_w_ple_proj', 'new_v_g_ple_post']
TWIN_LEAF_KINDS = {'loss': 'loss', 'grad_x': 'grad_x', 'grad_g_mix_pre': 'grad_w', 'grad_g_mix_post': 'grad_w', 'grad_w_in': 'grad_w', 'grad_pool_w': 'grad_w', 'grad_pool_scale': 'grad_w', 'grad_w_pool_out': 'grad_w', 'grad_conv_w': 'grad_w', 'grad_conv_b': 'grad_w', 'grad_w_rg_gates': 'grad_w', 'grad_b_rg_gates': 'grad_w', 'grad_lru_lambda': 'grad_w', 'grad_w_rg_out': 'grad_w', 'grad_w_o': 'grad_w', 'grad_g_ffn_pre': 'grad_w', 'grad_g_ffn_post': 'grad_w', 'grad_w_up': 'grad_w', 'grad_ffn_conv_w': 'grad_w', 'grad_ffn_conv_b': 'grad_w', 'grad_w_down': 'grad_w', 'grad_g_ple_gate': 'grad_w', 'grad_w_ple_gate': 'grad_w', 'grad_w_ple_proj': 'grad_w', 'grad_g_ple_post': 'grad_w', 'delta_g_mix_pre': 'delta_w', 'delta_g_mix_post': 'delta_w', 'delta_w_in': 'delta_w', 'delta_pool_w': 'delta_w', 'delta_pool_scale': 'delta_w', 'delta_w_pool_out': 'delta_w', 'delta_conv_w': 'delta_w', 'delta_conv_b': 'delta_w', 'delta_w_rg_gates': 'delta_w', 'delta_b_rg_gates': 'delta_w', 'delta_lru_lambda': 'delta_w', 'delta_w_rg_out': 'delta_w', 'delta_w_o': 'delta_w', 'delta_g_ffn_pre': 'delta_w', 'delta_g_ffn_post': 'delta_w', 'delta_w_up': 'delta_w', 'delta_ffn_conv_w': 'delta_w', 'delta_ffn_conv_b': 'delta_w', 'delta_w_down': 'delta_w', 'delta_g_ple_gate': 'delta_w', 'delta_w_ple_gate': 'delta_w', 'delta_w_ple_proj': 'delta_w', 'delta_g_ple_post': 'delta_w', 'new_m_g_mix_pre': 'new_m', 'new_m_g_mix_post': 'new_m', 'new_m_w_in': 'new_m', 'new_m_pool_w': 'new_m', 'new_m_pool_scale': 'new_m', 'new_m_w_pool_out': 'new_m', 'new_m_conv_w': 'new_m', 'new_m_conv_b': 'new_m', 'new_m_w_rg_gates': 'new_m', 'new_m_b_rg_gates': 'new_m', 'new_m_lru_lambda': 'new_m', 'new_m_w_rg_out': 'new_m', 'new_m_w_o': 'new_m', 'new_m_g_ffn_pre': 'new_m', 'new_m_g_ffn_post': 'new_m', 'new_m_w_up': 'new_m', 'new_m_ffn_conv_w': 'new_m', 'new_m_ffn_conv_b': 'new_m', 'new_m_w_down': 'new_m', 'new_m_g_ple_gate': 'new_m', 'new_m_w_ple_gate': 'new_m', 'new_m_w_ple_proj': 'new_m', 'new_m_g_ple_post': 'new_m', 'new_v_g_mix_pre': 'new_v', 'new_v_g_mix_post': 'new_v', 'new_v_w_in': 'new_v', 'new_v_pool_w': 'new_v', 'new_v_pool_scale': 'new_v', 'new_v_w_pool_out': 'new_v', 'new_v_conv_w': 'new_v', 'new_v_conv_b': 'new_v', 'new_v_w_rg_gates': 'new_v', 'new_v_b_rg_gates': 'new_v', 'new_v_lru_lambda': 'new_v', 'new_v_w_rg_out': 'new_v', 'new_v_w_o': 'new_v', 'new_v_g_ffn_pre': 'new_v', 'new_v_g_ffn_post': 'new_v', 'new_v_w_up': 'new_v', 'new_v_ffn_conv_w': 'new_v', 'new_v_ffn_conv_b': 'new_v', 'new_v_w_down': 'new_v', 'new_v_g_ple_gate': 'new_v', 'new_v_w_ple_gate': 'new_v', 'new_v_w_ple_proj': 'new_v', 'new_v_g_ple_post': 'new_v'}


def _forward(args):
    return _fwd_reference(*[args[k] for k in FWD_PARAMS])


def _output_shape():
    out = _jax.eval_shape(lambda: _forward(_fwd_setup_inputs(0)))
    return out.shape, out.dtype

N_MICROBATCH = 1
ADAM_LR = 0.001
ADAM_B1 = 0.9
ADAM_B2 = 0.999
ADAM_EPS = 1e-08
ADAM_WD = 0.01
ADAM_STEP = 10
PER_EXAMPLE_BATCH_AXIS = {'x': 0, 'p': 1, 'loss_target': 0}
SHARED_INPUTS = []
_WEIGHT_DTYPES = {'g_mix_pre': _jnp.float32, 'g_mix_post': _jnp.float32, 'w_in': _jnp.float32, 'pool_w': _jnp.float32, 'pool_scale': _jnp.float32, 'w_pool_out': _jnp.float32, 'conv_w': _jnp.float32, 'conv_b': _jnp.float32, 'w_rg_gates': _jnp.float32, 'b_rg_gates': _jnp.float32, 'lru_lambda': _jnp.float32, 'w_rg_out': _jnp.float32, 'w_o': _jnp.float32, 'g_ffn_pre': _jnp.float32, 'g_ffn_post': _jnp.float32, 'w_up': _jnp.float32, 'ffn_conv_w': _jnp.float32, 'ffn_conv_b': _jnp.float32, 'w_down': _jnp.float32, 'g_ple_gate': _jnp.float32, 'w_ple_gate': _jnp.float32, 'w_ple_proj': _jnp.float32, 'g_ple_post': _jnp.float32}
MOMENT_SCALE = {'g_mix_pre': 7.177321e-01, 'g_mix_post': 3.217724e+01, 'w_in': 3.456737e-01, 'pool_w': 9.970362e-01, 'pool_scale': 1.144881e+00, 'w_pool_out': 7.679115e-01, 'conv_w': 2.756009e-01, 'conv_b': 5.188426e+00, 'w_rg_gates': 2.560077e-01, 'b_rg_gates': 8.821592e-02, 'lru_lambda': 1.381116e-01, 'w_rg_out': 3.674973e-01, 'w_o': 9.075575e-01, 'g_ffn_pre': 7.062390e-01, 'g_ffn_post': 3.203342e+01, 'w_up': 3.024942e-01, 'ffn_conv_w': 3.243170e-01, 'ffn_conv_b': 6.987890e-01, 'w_down': 6.973650e-01, 'g_ple_gate': 9.722069e-01, 'w_ple_gate': 1.614556e-01, 'w_ple_proj': 1.980693e-01, 'g_ple_post': 9.397216e+00}


def _to_microbatches(a, axis):
    t = _jnp.moveaxis(a, axis, 0)
    t = t.reshape((N_MICROBATCH, t.shape[0] // N_MICROBATCH) + t.shape[1:])
    return _jnp.moveaxis(t, 1, axis + 1)


def setup_inputs(seed: int = 0) -> dict:
    inp = _fwd_setup_inputs(seed)
    key = _jax.random.fold_in(_jax.random.key(seed), 7919)
    shape, _ = _output_shape()
    out = dict(inp)
    out["loss_target"] = _jax.random.normal(_jax.random.fold_in(key, 0), shape, _jnp.float32)
    for i, name in enumerate(TWIN_WEIGHTS):
        w = inp[name].astype(_jnp.float32)
        if MOMENT_SCALE is None:
            s = _jnp.sqrt(_jnp.mean(_jnp.square(w)) + 1e-30)
        else:
            s = MOMENT_SCALE[name]
        km, kv = _jax.random.split(_jax.random.fold_in(key, i + 1))
        out[name] = w
        out["m_" + name] = s * _jax.random.normal(km, w.shape, _jnp.float32)
        out["v_" + name] = (s * s) * _jax.random.uniform(kv, w.shape, _jnp.float32, 0.5, 1.5)
    if N_MICROBATCH > 1:
        for name, axis in PER_EXAMPLE_BATCH_AXIS.items():
            out[name] = _to_microbatches(out[name], axis)
    return {'x': out['x'], 'p': out['p'], 'g_mix_pre': out['g_mix_pre'], 'g_mix_post': out['g_mix_post'], 'w_in': out['w_in'], 'pool_w': out['pool_w'], 'pool_scale': out['pool_scale'], 'w_pool_out': out['w_pool_out'], 'conv_w': out['conv_w'], 'conv_b': out['conv_b'], 'w_rg_gates': out['w_rg_gates'], 'b_rg_gates': out['b_rg_gates'], 'lru_lambda': out['lru_lambda'], 'w_rg_out': out['w_rg_out'], 'w_o': out['w_o'], 'g_ffn_pre': out['g_ffn_pre'], 'g_ffn_post': out['g_ffn_post'], 'w_up': out['w_up'], 'ffn_conv_w': out['ffn_conv_w'], 'ffn_conv_b': out['ffn_conv_b'], 'w_down': out['w_down'], 'g_ple_gate': out['g_ple_gate'], 'w_ple_gate': out['w_ple_gate'], 'w_ple_proj': out['w_ple_proj'], 'g_ple_post': out['g_ple_post'], 'loss_target': out['loss_target'], 'm_g_mix_pre': out['m_g_mix_pre'], 'm_g_mix_post': out['m_g_mix_post'], 'm_w_in': out['m_w_in'], 'm_pool_w': out['m_pool_w'], 'm_pool_scale': out['m_pool_scale'], 'm_w_pool_out': out['m_w_pool_out'], 'm_conv_w': out['m_conv_w'], 'm_conv_b': out['m_conv_b'], 'm_w_rg_gates': out['m_w_rg_gates'], 'm_b_rg_gates': out['m_b_rg_gates'], 'm_lru_lambda': out['m_lru_lambda'], 'm_w_rg_out': out['m_w_rg_out'], 'm_w_o': out['m_w_o'], 'm_g_ffn_pre': out['m_g_ffn_pre'], 'm_g_ffn_post': out['m_g_ffn_post'], 'm_w_up': out['m_w_up'], 'm_ffn_conv_w': out['m_ffn_conv_w'], 'm_ffn_conv_b': out['m_ffn_conv_b'], 'm_w_down': out['m_w_down'], 'm_g_ple_gate': out['m_g_ple_gate'], 'm_w_ple_gate': out['m_w_ple_gate'], 'm_w_ple_proj': out['m_w_ple_proj'], 'm_g_ple_post': out['m_g_ple_post'], 'v_g_mix_pre': out['v_g_mix_pre'], 'v_g_mix_post': out['v_g_mix_post'], 'v_w_in': out['v_w_in'], 'v_pool_w': out['v_pool_w'], 'v_pool_scale': out['v_pool_scale'], 'v_w_pool_out': out['v_w_pool_out'], 'v_conv_w': out['v_conv_w'], 'v_conv_b': out['v_conv_b'], 'v_w_rg_gates': out['v_w_rg_gates'], 'v_b_rg_gates': out['v_b_rg_gates'], 'v_lru_lambda': out['v_lru_lambda'], 'v_w_rg_out': out['v_w_rg_out'], 'v_w_o': out['v_w_o'], 'v_g_ffn_pre': out['v_g_ffn_pre'], 'v_g_ffn_post': out['v_g_ffn_post'], 'v_w_up': out['v_w_up'], 'v_ffn_conv_w': out['v_ffn_conv_w'], 'v_ffn_conv_b': out['v_ffn_conv_b'], 'v_w_down': out['v_w_down'], 'v_g_ple_gate': out['v_g_ple_gate'], 'v_w_ple_gate': out['v_w_ple_gate'], 'v_w_ple_proj': out['v_w_ple_proj'], 'v_g_ple_post': out['v_g_ple_post']}


def _loss(weights, diff, rest, loss_target):
    with _jax.named_scope("forward"):
        args = {**rest, TWIN_DIFF_INPUT: diff, **{k: w.astype(_WEIGHT_DTYPES[k]) for k, w in weights.items()}}
        y = _forward(args)
    with _jax.named_scope("loss_head"):
        err = _jnp.square(y.astype(_jnp.float32) - loss_target)
        return 0.5 * _jnp.sum(_jnp.mean(err, axis=-1)) if err.ndim else 0.5 * err


def _adamw(w, g, m, v):
    m = ADAM_B1 * m + (1.0 - ADAM_B1) * g
    v = ADAM_B2 * v + (1.0 - ADAM_B2) * _jnp.square(g)
    m_hat = m / (1.0 - ADAM_B1 ** ADAM_STEP)
    v_hat = v / (1.0 - ADAM_B2 ** ADAM_STEP)
    delta = -ADAM_LR * (m_hat / (_jnp.sqrt(v_hat) + ADAM_EPS) + ADAM_WD * w)
    return delta, m, v


def reference(x, p, g_mix_pre, g_mix_post, w_in, pool_w, pool_scale, w_pool_out, conv_w, conv_b, w_rg_gates, b_rg_gates, lru_lambda, w_rg_out, w_o, g_ffn_pre, g_ffn_post, w_up, ffn_conv_w, ffn_conv_b, w_down, g_ple_gate, w_ple_gate, w_ple_proj, g_ple_post, loss_target, m_g_mix_pre, m_g_mix_post, m_w_in, m_pool_w, m_pool_scale, m_w_pool_out, m_conv_w, m_conv_b, m_w_rg_gates, m_b_rg_gates, m_lru_lambda, m_w_rg_out, m_w_o, m_g_ffn_pre, m_g_ffn_post, m_w_up, m_ffn_conv_w, m_ffn_conv_b, m_w_down, m_g_ple_gate, m_w_ple_gate, m_w_ple_proj, m_g_ple_post, v_g_mix_pre, v_g_mix_post, v_w_in, v_pool_w, v_pool_scale, v_w_pool_out, v_conv_w, v_conv_b, v_w_rg_gates, v_b_rg_gates, v_lru_lambda, v_w_rg_out, v_w_o, v_g_ffn_pre, v_g_ffn_post, v_w_up, v_ffn_conv_w, v_ffn_conv_b, v_w_down, v_g_ple_gate, v_w_ple_gate, v_w_ple_proj, v_g_ple_post):
    given = dict(x=x, p=p, g_mix_pre=g_mix_pre, g_mix_post=g_mix_post, w_in=w_in, pool_w=pool_w, pool_scale=pool_scale, w_pool_out=w_pool_out, conv_w=conv_w, conv_b=conv_b, w_rg_gates=w_rg_gates, b_rg_gates=b_rg_gates, lru_lambda=lru_lambda, w_rg_out=w_rg_out, w_o=w_o, g_ffn_pre=g_ffn_pre, g_ffn_post=g_ffn_post, w_up=w_up, ffn_conv_w=ffn_conv_w, ffn_conv_b=ffn_conv_b, w_down=w_down, g_ple_gate=g_ple_gate, w_ple_gate=w_ple_gate, w_ple_proj=w_ple_proj, g_ple_post=g_ple_post, loss_target=loss_target, m_g_mix_pre=m_g_mix_pre, m_g_mix_post=m_g_mix_post, m_w_in=m_w_in, m_pool_w=m_pool_w, m_pool_scale=m_pool_scale, m_w_pool_out=m_w_pool_out, m_conv_w=m_conv_w, m_conv_b=m_conv_b, m_w_rg_gates=m_w_rg_gates, m_b_rg_gates=m_b_rg_gates, m_lru_lambda=m_lru_lambda, m_w_rg_out=m_w_rg_out, m_w_o=m_w_o, m_g_ffn_pre=m_g_ffn_pre, m_g_ffn_post=m_g_ffn_post, m_w_up=m_w_up, m_ffn_conv_w=m_ffn_conv_w, m_ffn_conv_b=m_ffn_conv_b, m_w_down=m_w_down, m_g_ple_gate=m_g_ple_gate, m_w_ple_gate=m_w_ple_gate, m_w_ple_proj=m_w_ple_proj, m_g_ple_post=m_g_ple_post, v_g_mix_pre=v_g_mix_pre, v_g_mix_post=v_g_mix_post, v_w_in=v_w_in, v_pool_w=v_pool_w, v_pool_scale=v_pool_scale, v_w_pool_out=v_w_pool_out, v_conv_w=v_conv_w, v_conv_b=v_conv_b, v_w_rg_gates=v_w_rg_gates, v_b_rg_gates=v_b_rg_gates, v_lru_lambda=v_lru_lambda, v_w_rg_out=v_w_rg_out, v_w_o=v_w_o, v_g_ffn_pre=v_g_ffn_pre, v_g_ffn_post=v_g_ffn_post, v_w_up=v_w_up, v_ffn_conv_w=v_ffn_conv_w, v_ffn_conv_b=v_ffn_conv_b, v_w_down=v_w_down, v_g_ple_gate=v_g_ple_gate, v_w_ple_gate=v_w_ple_gate, v_w_ple_proj=v_w_ple_proj, v_g_ple_post=v_g_ple_post)
    weights = {n: given[n] for n in TWIN_WEIGHTS}
    shared = {n: given[n] for n in SHARED_INPUTS}
    per_example = {n: given[n] for n in ['x', 'p']}
    grad_fn = _jax.value_and_grad(_loss, argnums=(0, 1))

    def one_microbatch(ex, loss_target):
        ex = dict(ex)
        diff = ex.pop(TWIN_DIFF_INPUT)
        return grad_fn(weights, diff, {**shared, **ex}, loss_target)

    if N_MICROBATCH == 1:
        loss, (grad_w, grad_x) = one_microbatch(per_example, given["loss_target"])
    else:
        def body(carry, xs):
            loss_sum, grad_sum = carry
            l_k, (gw_k, gx_k) = one_microbatch(xs[0], xs[1])
            with _jax.named_scope("update"):
                return (loss_sum + l_k, _jax.tree.map(_jnp.add, grad_sum, gw_k)), gx_k

        init = (_jnp.zeros((), _jnp.float32), _jax.tree.map(_jnp.zeros_like, weights))
        (loss, grad_w), grad_x = _jax.lax.scan(body, init, (per_example, given["loss_target"]))
    with _jax.named_scope("update"):
        delta_w, new_m, new_v = {}, {}, {}
        for n in TWIN_WEIGHTS:
            delta_w[n], new_m[n], new_v[n] = _adamw(weights[n], grad_w[n], given["m_" + n], given["v_" + n])
    return (loss, grad_x, *[grad_w[n] for n in TWIN_WEIGHTS], *[delta_w[n] for n in TWIN_WEIGHTS],
            *[new_m[n] for n in TWIN_WEIGHTS], *[new_v[n] for n in TWIN_WEIGHTS])
```

```python
import functools

import jax
import jax.numpy as jnp
from jax import lax
from jax.experimental import pallas as pl
from jax.experimental.pallas import tpu as pltpu

F32 = jnp.float32
BF16 = jnp.bfloat16

D_MODEL = 1024
POOL_WINDOWS = (2, 4, 8, 16)
POOL_GROUPS = 4
POOL_WIDTH = 512
POOL_GROUP_DIM = 128
RNN_HEADS = 16
RNN_HEAD_DIM = 64
GATE_BLOCK = 256
GATE_BLOCKS = D_MODEL // GATE_BLOCK
LRU_C = 8.0
D_FF = 3072
PLE_DIM = 256
RMS_EPS = 1e-6
IN_TOTAL = 4608
N_CHIPS = 4
IN_SHARD = IN_TOTAL // N_CHIPS
UP_SHARD = 2 * D_FF // N_CHIPS
POOL_HALO = 16
CONV_HALO = 8

ADAM_LR = 0.001
ADAM_B1 = 0.9
ADAM_B2 = 0.999
ADAM_EPS = 1e-08
ADAM_WD = 0.01
ADAM_STEP = 10

VMEM_LIMIT = 56 * 1024 * 1024
MESH_AXES = ("x", "y", "c")
MESH = pl.DeviceIdType.MESH

_GELU_C = 0.7978845608028654
_GELU_A = 0.044715


def _dot(a, b):
    return jnp.dot(a.astype(BF16), b.astype(BF16), preferred_element_type=F32)


def _dot_nt(a, b):
    return lax.dot_general(a.astype(BF16), b.astype(BF16), (((1,), (1,)), ((), ())), preferred_element_type=F32)


def _dot_tn(a, b):
    return lax.dot_general(a.astype(BF16), b.astype(BF16), (((0,), (0,)), ((), ())), preferred_element_type=F32)


def _rms_fwd(x, g):
    r = lax.rsqrt(jnp.mean(x * x, axis=-1, keepdims=True) + RMS_EPS)
    xh = x * r
    return xh * g, xh, r


def _rms_bwd(xh, r, g, dy):
    dxh = dy * g
    dg = jnp.sum(dy * xh, axis=0, keepdims=True)
    dx = r * (dxh - xh * jnp.mean(dxh * xh, axis=-1, keepdims=True))
    return dx, dg


def _gelu(x):
    t = jnp.tanh(_GELU_C * (x + _GELU_A * x * x * x))
    return 0.5 * x * (1.0 + t), t


def _gelu_grad(x, t):
    return 0.5 * (1.0 + t) + 0.5 * x * (1.0 - t * t) * _GELU_C * (1.0 + 3.0 * _GELU_A * x * x)


def _softplus_neg(lam):
    nl = -lam
    return jnp.maximum(nl, 0.0) + jnp.log(1.0 + jnp.exp(-jnp.abs(nl)))


def _lru_coeffs(r, lam, first_row):
    c8 = LRU_C * _softplus_neg(lam)
    la = -(c8 * r)
    a = jnp.exp(la)
    m2 = jnp.tanh(-la) * (1.0 + a * a)
    mult = jnp.where(first_row, 1.0, jnp.sqrt(m2))
    return c8, a, m2, mult


def _scan_fwd(a, u, carry):
    n = a.shape[0]
    row = lax.broadcasted_iota(jnp.int32, (n, 1), 0)
    acc_a, acc_h = a, u
    s = 1
    while s < n:
        m = row >= s
        h_s = jnp.where(m, pltpu.roll(acc_h, s, 0), 0.0)
        a_s = jnp.where(m, pltpu.roll(acc_a, s, 0), 1.0)
        acc_h = acc_a * h_s + acc_h
        acc_a = acc_a * a_s
        s *= 2
    return acc_h + acc_a * carry


def _scan_bwd(b, g, carry):
    n = b.shape[0]
    row = lax.broadcasted_iota(jnp.int32, (n, 1), 0)
    acc_b, acc_l = b, g
    s = 1
    while s < n:
        m = row < n - s
        l_s = jnp.where(m, pltpu.roll(acc_l, n - s, 0), 0.0)
        b_s = jnp.where(m, pltpu.roll(acc_b, n - s, 0), 1.0)
        acc_l = acc_b * l_s + acc_l
        acc_b = acc_b * b_s
        s *= 2
    return acc_l + acc_b * carry


def _shift_down(ext, k, halo):
    return pltpu.roll(ext, k, 0)[halo:] if k else ext[halo:]


def _shift_up(ext, k, ts):
    return pltpu.roll(ext, ext.shape[0] - k, 0)[:ts] if k else ext[:ts]


def _rows(ts, width, nt=None, col=0):
    if nt is None:
        return pl.BlockSpec((ts, width), lambda i: (i, col))
    return pl.BlockSpec((ts, width), lambda i: (nt - 1 - i, col))


def _resident(shape):
    zeros = (0,) * len(shape)
    return pl.BlockSpec(shape, lambda i: zeros, pipeline_mode=pl.Buffered(1))


def _acc(shape):
    zeros = (0,) * len(shape)
    return pl.BlockSpec(shape, lambda i: zeros)


def _params():
    return pltpu.CompilerParams(dimension_semantics=("arbitrary",), vmem_limit_bytes=VMEM_LIMIT)


def _sds(shape, dtype=F32):
    return jax.ShapeDtypeStruct(shape, dtype)


def _fwd_in_pool(x, g_pre, w_in, pool_w, pool_scale, w_pool_out, ts):
    s = x.shape[0]

    def body(x_ref, g_ref, win_ref, pw_ref, ps_ref, wpo_ref,
             urx_ref, urg_ref, gp_ref, gr_ref, d_ref, yp_ref, z_scr, halo_scr):
        i = pl.program_id(0)

        @pl.when(i == 0)
        def _():
            halo_scr[...] = jnp.zeros_like(halo_scr)

        h1, _, _ = _rms_fwd(x_ref[...], g_ref[...])
        h1 = h1.astype(BF16)
        for j in range(N_CHIPS):
            z_scr[:, j * IN_SHARD:(j + 1) * IN_SHARD] = jnp.dot(h1, win_ref[j], preferred_element_type=F32)
        urx_ref[...] = z_scr[:, 512:1536]
        urg_ref[...] = z_scr[:, 1536:2560]
        gp_ref[...] = z_scr[:, 2560:3584]
        gr_ref[...] = z_scr[:, 3584:4608]
        u = z_scr[:, 0:POOL_WIDTH]
        ext = jnp.concatenate([halo_scr[...], u], axis=0)
        halo_scr[...] = u[ts - POOL_HALO:, :]
        t = i * ts + lax.broadcasted_iota(jnp.int32, (ts, 1), 0)
        y4 = []
        for g, w in enumerate(POOL_WINDOWS):
            lanes = slice(g * POOL_GROUP_DIM, (g + 1) * POOL_GROUP_DIM)
            acc = ext[:, lanes]
            sh = 1
            while sh < w:
                acc = acc + pltpu.roll(acc, sh, 0)
                sh *= 2
            inv = 1.0 / jnp.minimum(t + 1, w).astype(F32)
            dg = acc[POOL_HALO:, :] * inv - u[:, lanes]
            d_ref[:, lanes] = dg
            y4.append(_dot(dg, pw_ref[g]))
        ypre = jnp.concatenate(y4, axis=1) * ps_ref[...]
        ypre = ypre.astype(BF16)
        for j in range(N_CHIPS):
            yp_ref[:, j * 256:(j + 1) * 256] = jnp.dot(ypre, wpo_ref[j], preferred_element_type=F32)

    return pl.pallas_call(
        body, name="fwd_in_pool", grid=(s // ts,),
        in_specs=[_rows(ts, D_MODEL), _resident((1, D_MODEL)), _resident(w_in.shape), _resident(pool_w.shape),
                  _resident((1, POOL_WIDTH)), _resident(w_pool_out.shape)],
        out_specs=[_rows(ts, D_MODEL)] * 4 + [_rows(ts, POOL_WIDTH), _rows(ts, D_MODEL)],
        out_shape=[_sds((s, D_MODEL))] * 4 + [_sds((s, POOL_WIDTH)), _sds((s, D_MODEL))],
        scratch_shapes=[pltpu.VMEM((ts, IN_TOTAL), F32), pltpu.VMEM((POOL_HALO, POOL_WIDTH), F32)],
        compiler_params=_params(),
    )(x, g_pre, w_in, pool_w, pool_scale, w_pool_out)


def _fwd_rnn_merge(urx, urg, gp, gr, ypool, x, conv_w, conv_b, wg, bg, lam, w_rg_out, w_o, g_post, ts):
    s = x.shape[0]

    def body(urx_ref, urg_ref, gp_ref, gr_ref, yp_ref, x_ref, cw_ref, cb_ref, wg_ref, bg_ref, lam_ref, wrg_ref, wo_ref,
             gpost_ref, xc_ref, r_ref, ig_ref, h_ref, yr_ref, mo_ref, x1_ref, halo_scr, carry_scr):
        i = pl.program_id(0)

        @pl.when(i == 0)
        def _():
            halo_scr[...] = jnp.zeros_like(halo_scr)
            carry_scr[...] = jnp.zeros_like(carry_scr)

        urx_v = urx_ref[...]
        ext = jnp.concatenate([halo_scr[...], urx_v], axis=0)
        halo_scr[...] = urx_v[ts - CONV_HALO:, :]
        cw = cw_ref[...]
        xc = (cb_ref[...] + cw[3:4] * urx_v + cw[2:3] * _shift_down(ext, 1, CONV_HALO)
              + cw[1:2] * _shift_down(ext, 2, CONV_HALO) + cw[0:1] * _shift_down(ext, 3, CONV_HALO))
        xc_ref[...] = xc
        xcb = xc.astype(BF16)
        lin = []
        for gate in range(2):
            parts = [jnp.dot(xcb[:, q * GATE_BLOCK:(q + 1) * GATE_BLOCK], wg_ref[gate, q], preferred_element_type=F32)
                     for q in range(GATE_BLOCKS)]
            lin.append(jnp.concatenate(parts, axis=1) + bg_ref[gate:gate + 1, :])
        r = jax.nn.sigmoid(lin[0])
        ig = jax.nn.sigmoid(lin[1])
        r_ref[...] = r
        ig_ref[...] = ig
        first_row = (i * ts + lax.broadcasted_iota(jnp.int32, (ts, 1), 0)) == 0
        _, a, _, mult = _lru_coeffs(r, lam_ref[...], first_row)
        h = _scan_fwd(a, mult * ig * xc, carry_scr[0:1, :])
        carry_scr[0:1, :] = h[ts - 1:ts, :]
        h_ref[...] = h
        gl, _ = _gelu(urg_ref[...])
        yr = _dot(h * gl, wrg_ref[...])
        yr_ref[...] = yr
        merged = jax.nn.sigmoid(gp_ref[...]) * yp_ref[...] + jax.nn.sigmoid(gr_ref[...]) * yr
        mo = _dot(merged, wo_ref[...])
        mo_ref[...] = mo
        y, _, _ = _rms_fwd(mo, gpost_ref[...])
        x1_ref[...] = x_ref[...] + y

    row = _rows(ts, D_MODEL)
    return pl.pallas_call(
        body, name="fwd_rnn_merge", grid=(s // ts,),
        in_specs=[row] * 6 + [_resident(conv_w.shape), _resident((1, D_MODEL)), _resident(wg.shape), _resident(bg.shape),
                              _resident((1, D_MODEL)), _resident(w_rg_out.shape), _resident(w_o.shape),
                              _resident((1, D_MODEL))],
        out_specs=[row] * 7,
        out_shape=[_sds((s, D_MODEL))] * 7,
        scratch_shapes=[pltpu.VMEM((CONV_HALO, D_MODEL), F32), pltpu.VMEM((8, D_MODEL), F32)],
        compiler_params=_params(),
    )(urx, urg, gp, gr, ypool, x, conv_w, conv_b, wg, bg, lam, w_rg_out, w_o, g_post)


def _fwd_ffn(x1, g_pre, w_up, fcw, fcb, w_down, g_post, ts):
    s = x1.shape[0]

    def body(x1_ref, g_ref, wup_ref, fcw_ref, fcb_ref, wd_ref, gpost_ref, up_ref, h2_ref, dn_ref, x2_ref, halo_scr):
        i = pl.program_id(0)

        @pl.when(i == 0)
        def _():
            halo_scr[...] = jnp.zeros_like(halo_scr)

        x1_v = x1_ref[...]
        h2, _, _ = _rms_fwd(x1_v, g_ref[...])
        h2 = h2.astype(BF16)
        h2_ref[...] = h2
        for j in range(N_CHIPS):
            up_ref[:, j * UP_SHARD:(j + 1) * UP_SHARD] = jnp.dot(h2, wup_ref[j], preferred_element_type=F32)
        ug = up_ref[:, 0:D_FF]
        ext = jnp.concatenate([halo_scr[...], ug], axis=0)
        halo_scr[...] = ug[ts - CONV_HALO:, :]
        w = fcw_ref[...]
        gh = (fcb_ref[...] + w[2:3] * ug + w[1:2] * _shift_down(ext, 1, CONV_HALO)
              + w[0:1] * _shift_down(ext, 2, CONV_HALO))
        gl, _ = _gelu(gh)
        dn = _dot(gl * up_ref[:, D_FF:], wd_ref[...])
        dn_ref[...] = dn
        y, _, _ = _rms_fwd(dn, gpost_ref[...])
        x2_ref[...] = x1_v + y

    row = _rows(ts, D_MODEL)
    return pl.pallas_call(
        body, name="fwd_ffn", grid=(s // ts,),
        in_specs=[row, _resident((1, D_MODEL)), _resident(w_up.shape), _resident(fcw.shape), _resident((1, D_FF)),
                  _resident(w_down.shape), _resident((1, D_MODEL))],
        out_specs=[_rows(ts, 2 * D_FF), row, row, row],
        out_shape=[_sds((s, 2 * D_FF)), _sds((s, D_MODEL), BF16), _sds((s, D_MODEL)), _sds((s, D_MODEL))],
        scratch_shapes=[pltpu.VMEM((CONV_HALO, D_FF), F32)],
        compiler_params=_params(),
    )(x1, g_pre, w_up, fcw, fcb, w_down, g_post)


def _ple_loss(x2, p, tgt, g_gate, w_gate, w_proj, g_post, ts):
    s = x2.shape[0]

    def body(x2_ref, p_ref, t_ref, gg_ref, wg_ref, wp_ref, gp_ref, dx2_ref, loss_ref, dwg_ref, dwp_ref, dgg_ref, dgp_ref):
        @pl.when(pl.program_id(0) == 0)
        def _():
            loss_ref[...] = jnp.zeros_like(loss_ref)
            dwg_ref[...] = jnp.zeros_like(dwg_ref)
            dwp_ref[...] = jnp.zeros_like(dwp_ref)
            dgg_ref[...] = jnp.zeros_like(dgg_ref)
            dgp_ref[...] = jnp.zeros_like(dgp_ref)

        x2_v = x2_ref[...]
        n3, xh3, r3 = _rms_fwd(x2_v, gg_ref[...])
        pg = jax.nn.sigmoid(_dot(n3, wg_ref[...]))
        pb = p_ref[...].astype(BF16)
        q = jnp.concatenate([jnp.dot(pb, wp_ref[j], preferred_element_type=F32) for j in range(N_CHIPS)], axis=1)
        ple, qh, rq = _rms_fwd(q, gp_ref[...])
        e = x2_v + pg * ple - t_ref[...]
        loss_ref[...] += 0.5 * jnp.sum(jnp.mean(e * e, axis=-1, keepdims=True), axis=0, keepdims=True)
        dy = e * (1.0 / D_MODEL)
        dpgl = dy * ple * pg * (1.0 - pg)
        dwg_ref[...] += _dot_tn(n3, dpgl)
        dx3, dgg = _rms_bwd(xh3, r3, gg_ref[...], _dot_nt(dpgl, wg_ref[...]))
        dgg_ref[...] += dgg
        dq, dgp = _rms_bwd(qh, rq, gp_ref[...], dy * pg)
        dgp_ref[...] += dgp
        for j in range(N_CHIPS):
            dwp_ref[j] += _dot_tn(pb, dq[:, j * 256:(j + 1) * 256])
        dx2_ref[...] = dy + dx3

    row = _rows(ts, D_MODEL)
    vec = _acc((1, D_MODEL))
    return pl.pallas_call(
        body, name="ple_loss", grid=(s // ts,),
        in_specs=[row, _rows(ts, PLE_DIM), row, _resident((1, D_MODEL)), _resident(w_gate.shape), _resident(w_proj.shape),
                  _resident((1, D_MODEL))],
        out_specs=[row, _acc((1, 128)), _acc(w_gate.shape), _acc(w_proj.shape), vec, vec],
        out_shape=[_sds((s, D_MODEL)), _sds((1, 128)), _sds(w_gate.shape), _sds(w_proj.shape), _sds((1, D_MODEL)),
                   _sds((1, D_MODEL))],
        compiler_params=_params(),
    )(x2, p, tgt, g_gate, w_gate, w_proj, g_post)


def _bwd_ffn_down(dx2, dn, up, fcw, fcb, w_down, g_post, ts):
    s = dx2.shape[0]
    nt = s // ts
    halo_blocks = ts // CONV_HALO

    def body(dx2_ref, dn_ref, up_ref, uph_ref, fcw_ref, fcb_ref, wd_ref, gpost_ref,
             dup_ref, dwd_ref, dfcw_ref, dfcb_ref, dgp_ref, carry_scr):
        i = pl.program_id(0)
        k = nt - 1 - i

        @pl.when(i == 0)
        def _():
            carry_scr[...] = jnp.zeros_like(carry_scr)
            dwd_ref[...] = jnp.zeros_like(dwd_ref)
            dfcw_ref[...] = jnp.zeros_like(dfcw_ref)
            dfcb_ref[...] = jnp.zeros_like(dfcb_ref)
            dgp_ref[...] = jnp.zeros_like(dgp_ref)

        _, xh, r = _rms_fwd(dn_ref[...], gpost_ref[...])
        ddn, dgp = _rms_bwd(xh, r, gpost_ref[...], dx2_ref[...])
        dgp_ref[...] += dgp
        dhid = _dot_nt(ddn, wd_ref[...])
        ug = up_ref[:, 0:D_FF]
        uv = up_ref[:, D_FF:]
        halo = jnp.where(k > 0, uph_ref[...], 0.0)
        ext = jnp.concatenate([halo, ug], axis=0)
        w = fcw_ref[...]
        gh = (fcb_ref[...] + w[2:3] * ug + w[1:2] * _shift_down(ext, 1, CONV_HALO)
              + w[0:1] * _shift_down(ext, 2, CONV_HALO))
        gl, t = _gelu(gh)
        dwd_ref[...] += _dot_tn(gl * uv, ddn)
        dgh = dhid * uv * _gelu_grad(gh, t)
        dup_ref[:, D_FF:] = (dhid * gl).astype(BF16)
        extd = jnp.concatenate([dgh, carry_scr[...]], axis=0)
        carry_scr[...] = dgh[0:CONV_HALO, :]
        d1 = _shift_up(extd, 1, ts)
        d2 = _shift_up(extd, 2, ts)
        dup_ref[:, 0:D_FF] = (w[2:3] * dgh + w[1:2] * d1 + w[0:1] * d2).astype(BF16)
        dfcw_ref[2:3, :] += jnp.sum(ug * dgh, axis=0, keepdims=True)
        dfcw_ref[1:2, :] += jnp.sum(ug * d1, axis=0, keepdims=True)
        dfcw_ref[0:1, :] += jnp.sum(ug * d2, axis=0, keepdims=True)
        dfcb_ref[...] += jnp.sum(dgh, axis=0, keepdims=True)

    row = _rows(ts, D_MODEL, nt)
    halo_spec = pl.BlockSpec((CONV_HALO, D_FF), lambda i: (jnp.maximum((nt - 1 - i) * halo_blocks - 1, 0), 0))
    return pl.pallas_call(
        body, name="bwd_ffn_down", grid=(nt,),
        in_specs=[row, row, _rows(ts, 2 * D_FF, nt), halo_spec, _resident(fcw.shape), _resident((1, D_FF)),
                  _resident(w_down.shape), _resident((1, D_MODEL))],
        out_specs=[_rows(ts, 2 * D_FF, nt), _acc(w_down.shape), _acc(fcw.shape), _acc((1, D_FF)), _acc((1, D_MODEL))],
        out_shape=[_sds((s, 2 * D_FF), BF16), _sds(w_down.shape), _sds(fcw.shape), _sds((1, D_FF)), _sds((1, D_MODEL))],
        scratch_shapes=[pltpu.VMEM((CONV_HALO, D_FF), F32)],
        compiler_params=_params(),
    )(dx2, dn, up, up, fcw, fcb, w_down, g_post)


def _bwd_ffn_up(dup, x1, dx2, g_pre, w_up, ts):
    s = x1.shape[0]

    def body(dup_ref, x1_ref, dx2_ref, g_ref, wup_ref, dx1_ref, dg_ref):
        @pl.when(pl.program_id(0) == 0)
        def _():
            dg_ref[...] = jnp.zeros_like(dg_ref)

        _, xh, r = _rms_fwd(x1_ref[...], g_ref[...])
        dh2 = _dot_nt(dup_ref[:, 0:UP_SHARD], wup_ref[0])
        for j in range(1, N_CHIPS):
            dh2 = dh2 + _dot_nt(dup_ref[:, j * UP_SHARD:(j + 1) * UP_SHARD], wup_ref[j])
        dx, dg = _rms_bwd(xh, r, g_ref[...], dh2)
        dg_ref[...] += dg
        dx1_ref[...] = dx2_ref[...] + dx

    row = _rows(ts, D_MODEL)
    return pl.pallas_call(
        body, name="bwd_ffn_up", grid=(s // ts,),
        in_specs=[_rows(ts, 2 * D_FF), row, row, _resident((1, D_MODEL)), _resident(w_up.shape)],
        out_specs=[row, _acc((1, D_MODEL))],
        out_shape=[_sds((s, D_MODEL)), _sds((1, D_MODEL))],
        compiler_params=_params(),
    )(dup, x1, dx2, g_pre, w_up)


def _dw_up(h2, dup, ts):
    s = h2.shape[0]

    def body(h2_ref, dup_ref, out_ref):
        @pl.when(pl.program_id(1) == 0)
        def _():
            out_ref[...] = jnp.zeros_like(out_ref)

        out_ref[0] += _dot_tn(h2_ref[...], dup_ref[...])

    return pl.pallas_call(
        body, name="dw_up", grid=(N_CHIPS, s // ts),
        in_specs=[pl.BlockSpec((ts, D_MODEL), lambda j, i: (i, 0)), pl.BlockSpec((ts, UP_SHARD), lambda j, i: (i, j))],
        out_specs=pl.BlockSpec((1, D_MODEL, UP_SHARD), lambda j, i: (j, 0, 0)),
        out_shape=_sds((N_CHIPS, D_MODEL, UP_SHARD)),
        compiler_params=pltpu.CompilerParams(dimension_semantics=("arbitrary", "arbitrary"), vmem_limit_bytes=VMEM_LIMIT),
    )(h2, dup)


def _bwd_merge(dx1, mo, gp, gr, ypool, yrnn, g_post, w_o, ts):
    s = dx1.shape[0]

    def body(dx1_ref, mo_ref, gp_ref, gr_ref, yp_ref, yr_ref, g_ref, wo_ref,
             dgp_ref, dgr_ref, dyp_ref, dyr_ref, dwo_ref, dg_ref):
        @pl.when(pl.program_id(0) == 0)
        def _():
            dwo_ref[...] = jnp.zeros_like(dwo_ref)
            dg_ref[...] = jnp.zeros_like(dg_ref)

        _, xh, r = _rms_fwd(mo_ref[...], g_ref[...])
        dmo, dg = _rms_bwd(xh, r, g_ref[...], dx1_ref[...])
        dg_ref[...] += dg
        dmerged = _dot_nt(dmo, wo_ref[...])
        sp = jax.nn.sigmoid(gp_ref[...])
        sr = jax.nn.sigmoid(gr_ref[...])
        yp = yp_ref[...]
        yr = yr_ref[...]
        dwo_ref[...] += _dot_tn(sp * yp + sr * yr, dmo)
        dgp_ref[...] = (dmerged * yp * sp * (1.0 - sp)).astype(BF16)
        dgr_ref[...] = (dmerged * yr * sr * (1.0 - sr)).astype(BF16)
        dyp_ref[...] = (dmerged * sp).astype(BF16)
        dyr_ref[...] = (dmerged * sr).astype(BF16)

    row = _rows(ts, D_MODEL)
    return pl.pallas_call(
        body, name="bwd_merge", grid=(s // ts,),
        in_specs=[row] * 6 + [_resident((1, D_MODEL)), _resident(w_o.shape)],
        out_specs=[row] * 4 + [_acc(w_o.shape), _acc((1, D_MODEL))],
        out_shape=[_sds((s, D_MODEL), BF16)] * 4 + [_sds(w_o.shape), _sds((1, D_MODEL))],
        compiler_params=_params(),
    )(dx1, mo, gp, gr, ypool, yrnn, g_post, w_o)


def _bwd_rnn(dyr, urx, urg, xc, r, ig, h, conv_w, wg, lam, w_rg_out, ts):
    s = urx.shape[0]
    nt = s // ts
    halo_blocks = ts // CONV_HALO

    def body(dyr_ref, urx_ref, urg_ref, xc_ref, r_ref, ig_ref, h_ref, hh_ref, cw_ref, wg_ref, lam_ref, wrg_ref,
             durx_ref, durg_ref, dwrg_ref, dwg_ref, dcw_ref, dcb_ref, dbg_ref, dlam_ref, mu_scr, carry_scr):
        i = pl.program_id(0)
        k = nt - 1 - i

        @pl.when(i == 0)
        def _():
            mu_scr[...] = jnp.zeros_like(mu_scr)
            carry_scr[...] = jnp.zeros_like(carry_scr)
            dwrg_ref[...] = jnp.zeros_like(dwrg_ref)
            dwg_ref[...] = jnp.zeros_like(dwg_ref)
            dcw_ref[...] = jnp.zeros_like(dcw_ref)
            dcb_ref[...] = jnp.zeros_like(dcb_ref)
            dbg_ref[...] = jnp.zeros_like(dbg_ref)
            dlam_ref[...] = jnp.zeros_like(dlam_ref)

        row = lax.broadcasted_iota(jnp.int32, (ts, 1), 0)
        first_row = (k * ts + row) == 0
        h = h_ref[...]
        urg_v = urg_ref[...]
        dyr_v = dyr_ref[...]
        dhr = _dot_nt(dyr_v, wrg_ref[...])
        gl, t = _gelu(urg_v)
        dwrg_ref[...] += _dot_tn(h * gl, dyr_v)
        durg_ref[...] = (dhr * h * _gelu_grad(urg_v, t)).astype(BF16)
        r_v = r_ref[...]
        ig_v = ig_ref[...]
        xc_v = xc_ref[...]
        lam_v = lam_ref[...]
        c8, a, m2, mult = _lru_coeffs(r_v, lam_v, first_row)
        b = jnp.where(row == ts - 1, 1.0, pltpu.roll(a, ts - 1, 0))
        lt = _scan_bwd(b, dhr * gl, mu_scr[0:1, :])
        mu_scr[0:1, :] = a[0:1, :] * lt[0:1, :]
        h_before = jnp.where(k > 0, hh_ref[CONV_HALO - 1:CONV_HALO, :], 0.0)
        hprev = jnp.where(row == 0, h_before, pltpu.roll(h, 1, 0))
        dmult = lt * ig_v * xc_v
        da = lt * hprev - jnp.where(first_row, 0.0, dmult * a * lax.rsqrt(m2))
        dla = da * a
        dlam_ref[...] += jnp.sum(dla * r_v, axis=0, keepdims=True)
        dlr = (dla * (-c8)) * r_v * (1.0 - r_v)
        dli = (lt * mult * xc_v) * ig_v * (1.0 - ig_v)
        dbg_ref[0:1, :] += jnp.sum(dlr, axis=0, keepdims=True)
        dbg_ref[1:2, :] += jnp.sum(dli, axis=0, keepdims=True)
        xcb = xc_v.astype(BF16)
        parts = []
        for q in range(GATE_BLOCKS):
            blk = slice(q * GATE_BLOCK, (q + 1) * GATE_BLOCK)
            dlr_q = dlr[:, blk].astype(BF16)
            dli_q = dli[:, blk].astype(BF16)
            parts.append(_dot_nt(dlr_q, wg_ref[0, q]) + _dot_nt(dli_q, wg_ref[1, q]))
            dwg_ref[0, q] += _dot_tn(xcb[:, blk], dlr_q)
            dwg_ref[1, q] += _dot_tn(xcb[:, blk], dli_q)
        dxc = lt * mult * ig_v + jnp.concatenate(parts, axis=1)
        extd = jnp.concatenate([dxc, carry_scr[...]], axis=0)
        carry_scr[...] = dxc[0:CONV_HALO, :]
        cw = cw_ref[...]
        urx_v = urx_ref[...]
        durx = cw[3:4] * dxc
        dcw_ref[3:4, :] += jnp.sum(urx_v * dxc, axis=0, keepdims=True)
        for j in (1, 2, 3):
            dj = _shift_up(extd, j, ts)
            durx = durx + cw[3 - j:4 - j] * dj
            dcw_ref[3 - j:4 - j, :] += jnp.sum(urx_v * dj, axis=0, keepdims=True)
        durx_ref[...] = durx.astype(BF16)
        dcb_ref[...] += jnp.sum(dxc, axis=0, keepdims=True)

        @pl.when(i == nt - 1)
        def _():
            dlam_ref[...] = dlam_ref[...] * (LRU_C * jax.nn.sigmoid(-lam_v))

    row_spec = _rows(ts, D_MODEL, nt)
    halo_spec = pl.BlockSpec((CONV_HALO, D_MODEL), lambda i: (jnp.maximum((nt - 1 - i) * halo_blocks - 1, 0), 0))
    vec = _acc((1, D_MODEL))
    return pl.pallas_call(
        body, name="bwd_rnn", grid=(nt,),
        in_specs=[row_spec] * 7 + [halo_spec, _resident(conv_w.shape), _resident(wg.shape), _resident((1, D_MODEL)),
                                   _resident(w_rg_out.shape)],
        out_specs=[row_spec, row_spec, _acc(w_rg_out.shape), _acc(wg.shape), _acc(conv_w.shape), vec,
                   _acc((2, D_MODEL)), vec],
        out_shape=[_sds((s, D_MODEL), BF16), _sds((s, D_MODEL), BF16), _sds(w_rg_out.shape), _sds(wg.shape),
                   _sds(conv_w.shape), _sds((1, D_MODEL)), _sds((2, D_MODEL)), _sds((1, D_MODEL))],
        scratch_shapes=[pltpu.VMEM((8, D_MODEL), F32), pltpu.VMEM((CONV_HALO, D_MODEL), F32)],
        compiler_params=_params(),
    )(dyr, urx, urg, xc, r, ig, h, h, conv_w, wg, lam, w_rg_out)


def _bwd_pool_in(dyp, d, durx, durg, dgp, dgr, x, dx1, g_pre, w_in, pool_w, pool_scale, w_pool_out, ts):
    s = x.shape[0]
    nt = s // ts

    def body(dyp_ref, d_ref, durx_ref, durg_ref, dgp_ref, dgr_ref, x_ref, dx1_ref, g_ref, win_ref, pw_ref, ps_ref,
             wpo_ref, gx_ref, dwin_ref, dwpo_ref, dpw_ref, dps_ref, dg_ref, dz_scr, carry_scr):
        i = pl.program_id(0)
        k = nt - 1 - i

        @pl.when(i == 0)
        def _():
            carry_scr[...] = jnp.zeros_like(carry_scr)
            dwin_ref[...] = jnp.zeros_like(dwin_ref)
            dwpo_ref[...] = jnp.zeros_like(dwpo_ref)
            dpw_ref[...] = jnp.zeros_like(dpw_ref)
            dps_ref[...] = jnp.zeros_like(dps_ref)
            dg_ref[...] = jnp.zeros_like(dg_ref)

        dyp_v = dyp_ref[...]
        d_v = d_ref[...]
        ps = ps_ref[...]
        dypre = _dot_nt(dyp_v[:, 0:256], wpo_ref[0])
        for j in range(1, N_CHIPS):
            dypre = dypre + _dot_nt(dyp_v[:, j * 256:(j + 1) * 256], wpo_ref[j])
        y4 = jnp.concatenate([_dot(d_v[:, g * 128:(g + 1) * 128], pw_ref[g]) for g in range(POOL_GROUPS)], axis=1)
        ypre = (y4 * ps).astype(BF16)
        for j in range(N_CHIPS):
            dwpo_ref[j] += _dot_tn(ypre, dyp_v[:, j * 256:(j + 1) * 256])
        dps_ref[...] += jnp.sum(dypre * y4, axis=0, keepdims=True)
        dy4 = dypre * ps
        t = k * ts + lax.broadcasted_iota(jnp.int32, (ts, 1), 0)
        for g, w in enumerate(POOL_WINDOWS):
            lanes = slice(g * POOL_GROUP_DIM, (g + 1) * POOL_GROUP_DIM)
            dd = _dot_nt(dy4[:, lanes], pw_ref[g])
            dpw_ref[g] += _dot_tn(d_v[:, lanes], dy4[:, lanes])
            e = dd * (1.0 / jnp.minimum(t + 1, w).astype(F32))
            acc = jnp.concatenate([e, carry_scr[:, lanes]], axis=0)
            carry_scr[:, lanes] = e[0:POOL_HALO, :]
            n = ts + POOL_HALO
            sh = 1
            while sh < w:
                acc = acc + pltpu.roll(acc, n - sh, 0)
                sh *= 2
            dz_scr[:, lanes] = (acc[:ts, :] - dd).astype(BF16)
        dz_scr[:, 512:1536] = durx_ref[...]
        dz_scr[:, 1536:2560] = durg_ref[...]
        dz_scr[:, 2560:3584] = dgp_ref[...]
        dz_scr[:, 3584:4608] = dgr_ref[...]
        h1, xh, r = _rms_fwd(x_ref[...], g_ref[...])
        h1 = h1.astype(BF16)
        dh1 = _dot_nt(dz_scr[:, 0:IN_SHARD], win_ref[0])
        dwin_ref[0] += _dot_tn(h1, dz_scr[:, 0:IN_SHARD])
        for j in range(1, N_CHIPS):
            cols = slice(j * IN_SHARD, (j + 1) * IN_SHARD)
            dh1 = dh1 + _dot_nt(dz_scr[:, cols], win_ref[j])
            dwin_ref[j] += _dot_tn(h1, dz_scr[:, cols])
        dx, dg = _rms_bwd(xh, r, g_ref[...], dh1)
        dg_ref[...] += dg
        gx_ref[...] = dx1_ref[...] + dx

    row = _rows(ts, D_MODEL, nt)
    return pl.pallas_call(
        body, name="bwd_pool_in", grid=(nt,),
        in_specs=[row, _rows(ts, POOL_WIDTH, nt)] + [row] * 6 + [_resident((1, D_MODEL)), _resident(w_in.shape),
                                                                _resident(pool_w.shape), _resident((1, POOL_WIDTH)),
                                                                _resident(w_pool_out.shape)],
        out_specs=[row, _acc(w_in.shape), _acc(w_pool_out.shape), _acc(pool_w.shape), _acc((1, POOL_WIDTH)),
                   _acc((1, D_MODEL))],
        out_shape=[_sds((s, D_MODEL)), _sds(w_in.shape), _sds(w_pool_out.shape), _sds(pool_w.shape), _sds((1, POOL_WIDTH)),
                   _sds((1, D_MODEL))],
        scratch_shapes=[pltpu.VMEM((ts, IN_TOTAL), BF16), pltpu.VMEM((POOL_HALO, POOL_WIDTH), F32)],
        compiler_params=_params(),
    )(dyp, d, durx, durg, dgp, dgr, x, dx1, g_pre, w_in, pool_w, pool_scale, w_pool_out)


def _place():
    x, y, c = lax.axis_index("x"), lax.axis_index("y"), lax.axis_index("c")
    others = [(1 - x, y), (x, 1 - y), (1 - x, 1 - y)]
    return x, y, c, 2 * x + y, others


def _remote(src, dst, send_sem, recv_sem, to):
    return pltpu.make_async_remote_copy(src_ref=src, dst_ref=dst, send_sem=send_sem, recv_sem=recv_sem,
                                        device_id=to, device_id_type=MESH)


_ANY = pl.BlockSpec(memory_space=pl.ANY)


def _all_gather(shards):
    n = len(shards)

    def body(*refs):
        srcs, outs = refs[:n], refs[n:2 * n]
        send_sems, recv_sems, local_sems = refs[2 * n:]
        x, y, c, me, others = _place()
        sibling = (x, y, 1 - c)
        first, passed, local = [], [], []
        for w, (src, out) in enumerate(zip(srcs, outs)):
            ah = src.shape[0] // 2
            mine = pl.ds(c * ah, ah)
            cp = pltpu.make_async_copy(src, out.at[me], local_sems.at[w])
            cp.start()
            local.append(cp)
            for j, (ox, oy) in enumerate(others):
                cp = _remote(src.at[mine], out.at[me, mine], send_sems.at[w, j], recv_sems.at[w, j], (ox, oy, c))
                cp.start()
                first.append(cp)
        for w, out in enumerate(outs):
            ah = out.shape[1] // 2
            mine = pl.ds(c * ah, ah)
            for j, (ox, oy) in enumerate(others):
                slot = out.at[2 * ox + oy, mine]
                _remote(slot, slot, send_sems.at[w, j], recv_sems.at[w, j], (ox, oy, c)).wait_recv()
                cp = _remote(slot, slot, send_sems.at[w, 3 + j], recv_sems.at[w, 3 + j], sibling)
                cp.start()
                passed.append(cp)
        for w, out in enumerate(outs):
            ah = out.shape[1] // 2
            theirs = pl.ds((1 - c) * ah, ah)
            for j, (ox, oy) in enumerate(others):
                slot = out.at[2 * ox + oy, theirs]
                _remote(slot, slot, send_sems.at[w, 3 + j], recv_sems.at[w, 3 + j], sibling).wait_recv()
        for cp in first + passed:
            cp.wait_send()
        for cp in local:
            cp.wait()

    return pl.pallas_call(
        body, name="all_gather_weights",
        in_specs=[_ANY] * n, out_specs=[_ANY] * n,
        out_shape=[_sds((N_CHIPS,) + s.shape, s.dtype) for s in shards],
        scratch_shapes=[pltpu.SemaphoreType.DMA((n, 6)), pltpu.SemaphoreType.DMA((n, 6)), pltpu.SemaphoreType.DMA((n,))],
    )(*shards)


def _sibling_halves(grads):
    n = len(grads)

    def body(*refs):
        srcs, outs = refs[:n], refs[n:2 * n]
        send_sems, recv_sems = refs[2 * n:]
        x, y, c, _, _ = _place()
        sibling = (x, y, 1 - c)
        copies = []
        for w, (src, out) in enumerate(zip(srcs, outs)):
            ah = out.shape[1]
            cp = _remote(src.at[:, pl.ds((1 - c) * ah, ah)], out, send_sems.at[w], recv_sems.at[w], sibling)
            cp.start()
            copies.append(cp)
        for cp in copies:
            cp.wait()

    return pl.pallas_call(
        body, name="grad_sibling_halves",
        in_specs=[_ANY] * n, out_specs=[_ANY] * n,
        out_shape=[_sds((g.shape[0], g.shape[1] // 2, g.shape[2]), g.dtype) for g in grads],
        scratch_shapes=[pltpu.SemaphoreType.DMA((n,)), pltpu.SemaphoreType.DMA((n,))],
    )(*grads)


def _chip_exchange(parts):
    n = len(parts)

    def body(*refs):
        srcs, outs = refs[:n], refs[n:2 * n]
        send_sems, recv_sems, local_sems = refs[2 * n:]
        x, y, c, me, others = _place()
        sends, local = [], []
        for w, (src, out) in enumerate(zip(srcs, outs)):
            sharded = src.shape[0] == N_CHIPS
            cp = pltpu.make_async_copy(src.at[me if sharded else 0], out.at[me], local_sems.at[w])
            cp.start()
            local.append(cp)
            for j, (ox, oy) in enumerate(others):
                cp = _remote(src.at[2 * ox + oy if sharded else 0], out.at[me], send_sems.at[w, j], recv_sems.at[w, j],
                             (ox, oy, c))
                cp.start()
                sends.append(cp)
        for w, out in enumerate(outs):
            for j, (ox, oy) in enumerate(others):
                slot = out.at[2 * ox + oy]
                _remote(slot, slot, send_sems.at[w, j], recv_sems.at[w, j], (ox, oy, c)).wait_recv()
        for cp in sends:
            cp.wait_send()
        for cp in local:
            cp.wait()

    return pl.pallas_call(
        body, name="grad_chip_exchange",
        in_specs=[_ANY] * n, out_specs=[_ANY] * n,
        out_shape=[_sds((N_CHIPS,) + p.shape[1:], p.dtype) for p in parts],
        scratch_shapes=[pltpu.SemaphoreType.DMA((n, 3)), pltpu.SemaphoreType.DMA((n, 3)), pltpu.SemaphoreType.DMA((n,))],
    )(*parts)


def _sibling_share(halves):
    n = len(halves)

    def body(*refs):
        srcs, outs = refs[:n], refs[n:2 * n]
        send_sems, recv_sems, local_sems = refs[2 * n:]
        x, y, c, _, _ = _place()
        sibling = (x, y, 1 - c)
        sends, local = [], []
        for w, (src, out) in enumerate(zip(srcs, outs)):
            cp = pltpu.make_async_copy(src, out.at[c], local_sems.at[w])
            cp.start()
            local.append(cp)
            cp = _remote(src, out.at[c], send_sems.at[w], recv_sems.at[w], sibling)
            cp.start()
            sends.append(cp)
        for w, out in enumerate(outs):
            slot = out.at[1 - c]
            _remote(slot, slot, send_sems.at[w], recv_sems.at[w], sibling).wait_recv()
        for cp in sends:
            cp.wait_send()
        for cp in local:
            cp.wait()

    return pl.pallas_call(
        body, name="grad_sibling_share",
        in_specs=[_ANY] * n, out_specs=[_ANY] * n,
        out_shape=[_sds((2,) + h.shape, h.dtype) for h in halves],
        scratch_shapes=[pltpu.SemaphoreType.DMA((n,)), pltpu.SemaphoreType.DMA((n,)), pltpu.SemaphoreType.DMA((n,))],
    )(*halves)


TILE_BYTES = 2 * 1024 * 1024


def _row_tile(rows, cols):
    best = 8
    for tr in range(8, rows + 1, 8):
        if rows % tr == 0 and tr * cols * 4 <= TILE_BYTES:
            best = tr
    assert rows % best == 0, (rows, cols)
    return best


def _chip_partial(g, got, wire_dtype):
    ns, ah, b = got.shape
    tr = _row_tile(ah, b)
    nb = ah // tr

    def body(c_ref, g_ref, got_ref, out_ref):
        out_ref[...] = (g_ref[...] + got_ref[...]).astype(wire_dtype)

    blk = (1, tr, b)
    return pl.pallas_call(
        body, name="grad_chip_partial",
        grid_spec=pltpu.PrefetchScalarGridSpec(
            num_scalar_prefetch=1, grid=(ns, nb),
            in_specs=[pl.BlockSpec(blk, lambda s, i, c: (s, c[0] * nb + i, 0)), pl.BlockSpec(blk, lambda s, i, c: (s, i, 0))],
            out_specs=pl.BlockSpec(blk, lambda s, i, c: (s, i, 0))),
        out_shape=_sds((ns, ah, b), wire_dtype),
        compiler_params=pltpu.CompilerParams(dimension_semantics=("arbitrary", "arbitrary")),
    )(lax.axis_index("c").reshape(1).astype(jnp.int32), g, got)


def _chip_sum(parts):
    _, ah, b = parts.shape
    tr = _row_tile(ah, b)

    def body(p_ref, out_ref):
        acc = p_ref[0].astype(F32) + p_ref[1].astype(F32)
        acc = acc + p_ref[2].astype(F32)
        out_ref[...] = acc + p_ref[3].astype(F32)

    return pl.pallas_call(
        body, name="grad_chip_sum", grid=(ah // tr,),
        in_specs=[pl.BlockSpec((N_CHIPS, tr, b), lambda i: (0, i, 0))],
        out_specs=pl.BlockSpec((tr, b), lambda i: (i, 0)),
        out_shape=_sds((ah, b)),
        compiler_params=pltpu.CompilerParams(dimension_semantics=("arbitrary",)),
    )(parts)


def _adamw(w, g, m, v):
    a, b = w.shape
    tr = _row_tile(a, b)

    def body(w_ref, g_ref, m_ref, v_ref, d_ref, nm_ref, nv_ref):
        g_v = g_ref[...]
        nm = ADAM_B1 * m_ref[...] + (1.0 - ADAM_B1) * g_v
        nv = ADAM_B2 * v_ref[...] + (1.0 - ADAM_B2) * (g_v * g_v)
        m_hat = nm / (1.0 - ADAM_B1 ** ADAM_STEP)
        v_hat = nv / (1.0 - ADAM_B2 ** ADAM_STEP)
        d_ref[...] = -ADAM_LR * (m_hat / (jnp.sqrt(v_hat) + ADAM_EPS) + ADAM_WD * w_ref[...])
        nm_ref[...] = nm
        nv_ref[...] = nv

    blk = pl.BlockSpec((tr, b), lambda i: (i, 0))
    return pl.pallas_call(
        body, name="adamw", grid=(a // tr,),
        in_specs=[blk] * 4, out_specs=[blk] * 3, out_shape=[_sds((a, b))] * 3,
        compiler_params=pltpu.CompilerParams(dimension_semantics=("arbitrary",)),
    )(w, g, m, v)


TINY_ROWS, TINY_COLS = 16, 768
SMALL_COLS = 128
SMALL_ROWS = 1632


def _pack_tiny(conv_w, b_gates, fcw):
    ns = conv_w.shape[0]
    pad = lambda t: jnp.pad(t, ((0, 0), (0, 0), (0, TINY_COLS - t.shape[2])))
    z = lambda rows: jnp.zeros((ns, rows, TINY_COLS), F32)
    return jnp.concatenate([pad(conv_w), pad(b_gates), z(2), fcw, z(5)], axis=1)


def _unpack_tiny(t):
    return t[:, 0:4, 0:256], t[:, 4:6, 0:256], t[:, 8:11, :]


def _cols_to_shards(t, n):
    return t.reshape(t.shape[0], N_CHIPS, n).transpose(1, 0, 2)


def _shards_to_cols(t):
    return t.transpose(1, 0, 2).reshape(t.shape[1], -1)


_SMALL_ORDER = ("g_mix_pre", "g_mix_post", "conv_b", "lru_lambda", "g_ffn_pre", "g_ffn_post", "g_ple_gate", "g_ple_post",
                "pool_scale", "ffn_conv_b", "pool_w", "w_rg_gates")


def _pack_small(parts):
    flat = jnp.concatenate([parts[k].reshape(-1).astype(F32) for k in _SMALL_ORDER])
    flat = jnp.pad(flat, (0, SMALL_ROWS * SMALL_COLS - flat.shape[0]))
    return flat.reshape(SMALL_ROWS, SMALL_COLS)


def _unpack_small(packed, shapes):
    flat = packed.reshape(-1)
    out, off = {}, 0
    for k in _SMALL_ORDER:
        size = 1
        for dim in shapes[k]:
            size *= dim
        out[k] = flat[off:off + size].reshape(shapes[k])
        off += size
    return out


def _gates_block_diag(w):
    w4 = w.reshape(2, GATE_BLOCKS, 4, RNN_HEAD_DIM, RNN_HEAD_DIM)
    eye = jnp.eye(4, dtype=w.dtype)
    return jnp.einsum("gqhij,hk->gqhikj", w4, eye).reshape(2, GATE_BLOCKS, GATE_BLOCK, GATE_BLOCK)


def _gates_from_block_diag(dw):
    d6 = dw.reshape(2, GATE_BLOCKS, 4, RNN_HEAD_DIM, 4, RNN_HEAD_DIM)
    blocks = [d6[:, :, hh, :, hh, :] for hh in range(4)]
    return jnp.stack(blocks, axis=2).reshape(2, RNN_HEADS, RNN_HEAD_DIM, RNN_HEAD_DIM)


ROW_TILE = 256

_SHARDED = ("w_in", "w_pool_out", "w_rg_out", "w_o", "w_up", "w_down", "w_ple_gate", "w_ple_proj")
_WEIGHTS = ("g_mix_pre", "g_mix_post", "w_in", "pool_w", "pool_scale", "w_pool_out", "conv_w", "conv_b", "w_rg_gates",
            "b_rg_gates", "lru_lambda", "w_rg_out", "w_o", "g_ffn_pre", "g_ffn_post", "w_up", "ffn_conv_w", "ffn_conv_b",
            "w_down", "g_ple_gate", "w_ple_gate", "w_ple_proj", "g_ple_post")


def _local_step(x, p, tgt, full, rep, ts):
    conv_w_s, b_gates_s, fcw_s = _unpack_tiny(full["tiny"])
    conv_w, b_gates, fcw = _shards_to_cols(conv_w_s), _shards_to_cols(b_gates_s), _shards_to_cols(fcw_s)
    vec = lambda k: rep[k].reshape(1, -1)
    pool_w = rep["pool_w"].astype(BF16)
    wg = _gates_block_diag(rep["w_rg_gates"]).astype(BF16)
    w_rg_out = full["w_rg_out"].reshape(D_MODEL, D_MODEL)
    w_o = full["w_o"].reshape(D_MODEL, D_MODEL)
    w_down = full["w_down"].reshape(D_FF, D_MODEL)
    w_gate = full["w_ple_gate"].reshape(D_MODEL, D_MODEL)

    urx, urg, gp, gr, d, ypool = _fwd_in_pool(x, vec("g_mix_pre"), full["w_in"], pool_w, vec("pool_scale"),
                                              full["w_pool_out"], ts)
    xc, r, ig, h, yrnn, mo, x1 = _fwd_rnn_merge(urx, urg, gp, gr, ypool, x, conv_w, vec("conv_b"), wg, b_gates,
                                                vec("lru_lambda"), w_rg_out, w_o, vec("g_mix_post"), ts)
    up, h2, dn, x2 = _fwd_ffn(x1, vec("g_ffn_pre"), full["w_up"], fcw, vec("ffn_conv_b"), w_down, vec("g_ffn_post"), ts)
    dx2, loss, d_w_gate, d_w_proj, d_g_ple_gate, d_g_ple_post = _ple_loss(
        x2, p, tgt, vec("g_ple_gate"), w_gate, full["w_ple_proj"], vec("g_ple_post"), ts)
    dup, d_w_down, d_fcw, d_fcb, d_g_ffn_post = _bwd_ffn_down(dx2, dn, up, fcw, vec("ffn_conv_b"), w_down,
                                                               vec("g_ffn_post"), ts)
    dx1, d_g_ffn_pre = _bwd_ffn_up(dup, x1, dx2, vec("g_ffn_pre"), full["w_up"], ts)
    d_w_up = _dw_up(h2, dup, ts)
    dgp, dgr, dyp, dyr, d_w_o, d_g_mix_post = _bwd_merge(dx1, mo, gp, gr, ypool, yrnn, vec("g_mix_post"), w_o, ts)
    durx, durg, d_w_rg_out, d_wg, d_conv_w, d_conv_b, d_b_gates, d_lam = _bwd_rnn(
        dyr, urx, urg, xc, r, ig, h, conv_w, wg, vec("lru_lambda"), w_rg_out, ts)
    grad_x, d_w_in, d_w_pool_out, d_pool_w, d_pool_scale, d_g_mix_pre = _bwd_pool_in(
        dyp, d, durx, durg, dgp, dgr, x, dx1, vec("g_mix_pre"), full["w_in"], pool_w, vec("pool_scale"),
        full["w_pool_out"], ts)

    sharded = [d_w_in, d_w_pool_out, d_w_rg_out.reshape(N_CHIPS, -1, D_MODEL), d_w_o.reshape(N_CHIPS, -1, D_MODEL),
               d_w_up, d_w_down.reshape(N_CHIPS, -1, D_MODEL), d_w_gate.reshape(N_CHIPS, -1, D_MODEL), d_w_proj,
               _pack_tiny(_cols_to_shards(d_conv_w, 256), _cols_to_shards(d_b_gates, 256), _cols_to_shards(d_fcw, 768))]
    replicated = {"g_mix_pre": d_g_mix_pre, "g_mix_post": d_g_mix_post, "conv_b": d_conv_b, "lru_lambda": d_lam,
                  "g_ffn_pre": d_g_ffn_pre, "g_ffn_post": d_g_ffn_post, "g_ple_gate": d_g_ple_gate,
                  "g_ple_post": d_g_ple_post, "pool_scale": d_pool_scale, "ffn_conv_b": d_fcb, "pool_w": d_pool_w,
                  "w_rg_gates": _gates_from_block_diag(d_wg)}
    return loss, grad_x, sharded, replicated


def _reduce_gradients(grads):
    got = _sibling_halves(grads)
    wire = [BF16 if g.shape[1] >= 64 and g.shape[2] > SMALL_COLS else F32 for g in grads]
    parts = [_chip_partial(g, r, dt) for g, r, dt in zip(grads, got, wire)]
    slots = _chip_exchange(parts)
    halves = [_chip_sum(s) for s in slots]
    both = _sibling_share(halves)
    return [b.reshape(2 * b.shape[1], b.shape[2]) for b in both]


def kernel(x, p, g_mix_pre, g_mix_post, w_in, pool_w, pool_scale, w_pool_out, conv_w, conv_b, w_rg_gates, b_rg_gates, lru_lambda, w_rg_out, w_o, g_ffn_pre, g_ffn_post, w_up, ffn_conv_w, ffn_conv_b, w_down, g_ple_gate, w_ple_gate, w_ple_proj, g_ple_post, loss_target, m_g_mix_pre, m_g_mix_post, m_w_in, m_pool_w, m_pool_scale, m_w_pool_out, m_conv_w, m_conv_b, m_w_rg_gates, m_b_rg_gates, m_lru_lambda, m_w_rg_out, m_w_o, m_g_ffn_pre, m_g_ffn_post, m_w_up, m_ffn_conv_w, m_ffn_conv_b, m_w_down, m_g_ple_gate, m_w_ple_gate, m_w_ple_proj, m_g_ple_post, v_g_mix_pre, v_g_mix_post, v_w_in, v_pool_w, v_pool_scale, v_w_pool_out, v_conv_w, v_conv_b, v_w_rg_gates, v_b_rg_gates, v_lru_lambda, v_w_rg_out, v_w_o, v_g_ffn_pre, v_g_ffn_post, v_w_up, v_ffn_conv_w, v_ffn_conv_b, v_w_down, v_g_ple_gate, v_w_ple_gate, v_w_ple_proj, v_g_ple_post):
    args = dict(locals())
    w = {k: args[k][0] for k in _WEIGHTS}
    m = {k: args["m_" + k][0] for k in _WEIGHTS}
    v = {k: args["v_" + k][0] for k in _WEIGHTS}
    small_shapes = {k: w[k].shape for k in _SMALL_ORDER}
    tiny = lambda t: _pack_tiny(t["conv_w"][None], t["b_rg_gates"][None], t["ffn_conv_w"][None])[0]

    gathered = _all_gather([w[k].astype(BF16) for k in _SHARDED] + [tiny(w)])
    full = dict(zip(_SHARDED + ("tiny",), gathered))
    loss, grad_x, sharded, replicated = _local_step(x[0], p[0, 0], loss_target[0], full, w, ROW_TILE)

    reduced = _reduce_gradients(sharded + [_pack_small(replicated)[None]])
    params = [(w[k], m[k], v[k]) for k in _SHARDED] + [(tiny(w), tiny(m), tiny(v))]
    params.append((_pack_small(w), _pack_small(m), _pack_small(v)))
    updates = [_adamw(pw, g, pm, pv) for (pw, pm, pv), g in zip(params, reduced)]

    out = {"grad": {}, "delta": {}, "new_m": {}, "new_v": {}}
    kinds = ("grad", "delta", "new_m", "new_v")
    for i, k in enumerate(_SHARDED):
        for kind, val in zip(kinds, (reduced[i],) + tuple(updates[i])):
            out[kind][k] = val
    for kind, val in zip(kinds, (reduced[-2],) + tuple(updates[-2])):
        cw, bg, fcw = _unpack_tiny(val[None])
        out[kind].update(conv_w=cw[0], b_rg_gates=bg[0], ffn_conv_w=fcw[0])
    for kind, val in zip(kinds, (reduced[-1],) + tuple(updates[-1])):
        out[kind].update(_unpack_small(val, small_shapes))

    total = lax.psum(loss[0, 0], MESH_AXES)
    result = [total, grad_x[None]]
    for kind in kinds:
        result += [out[kind][k][None] for k in _WEIGHTS]
    return tuple(result)
```

```python
import functools

import jax
import jax.numpy as jnp
from jax import lax
from jax.experimental import pallas as pl
from jax.experimental.pallas import tpu as pltpu

F32 = jnp.float32
BF16 = jnp.bfloat16

D_MODEL = 1024
POOL_WINDOWS = (2, 4, 8, 16)
POOL_GROUPS = 4
POOL_WIDTH = 512
POOL_GROUP_DIM = 128
RNN_HEADS = 16
RNN_HEAD_DIM = 64
GATE_BLOCK = 256
GATE_BLOCKS = D_MODEL // GATE_BLOCK
LRU_C = 8.0
D_FF = 3072
PLE_DIM = 256
RMS_EPS = 1e-6
IN_TOTAL = 4608
N_CHIPS = 4
IN_SHARD = IN_TOTAL // N_CHIPS
UP_SHARD = 2 * D_FF // N_CHIPS
POOL_HALO = 16
CONV_HALO = 8

ADAM_LR = 0.001
ADAM_B1 = 0.9
ADAM_B2 = 0.999
ADAM_EPS = 1e-08
ADAM_WD = 0.01
ADAM_STEP = 10

VMEM_LIMIT = 56 * 1024 * 1024
MESH_AXES = ("x", "y", "c")
MESH = pl.DeviceIdType.MESH

_GELU_C = 0.7978845608028654
_GELU_A = 0.044715


def _dot(a, b):
    return jnp.dot(a.astype(BF16), b.astype(BF16), preferred_element_type=F32)


def _dot_nt(a, b):
    return lax.dot_general(a.astype(BF16), b.astype(BF16), (((1,), (1,)), ((), ())), preferred_element_type=F32)


def _dot_tn(a, b):
    return lax.dot_general(a.astype(BF16), b.astype(BF16), (((0,), (0,)), ((), ())), preferred_element_type=F32)


def _rms_fwd(x, g):
    r = lax.rsqrt(jnp.mean(x * x, axis=-1, keepdims=True) + RMS_EPS)
    xh = x * r
    return xh * g, xh, r


def _rms_bwd(xh, r, g, dy):
    dxh = dy * g
    dg = jnp.sum(dy * xh, axis=0, keepdims=True)
    dx = r * (dxh - xh * jnp.mean(dxh * xh, axis=-1, keepdims=True))
    return dx, dg


def _gelu(x):
    t = jnp.tanh(_GELU_C * (x + _GELU_A * x * x * x))
    return 0.5 * x * (1.0 + t), t


def _gelu_grad(x, t):
    return 0.5 * (1.0 + t) + 0.5 * x * (1.0 - t * t) * _GELU_C * (1.0 + 3.0 * _GELU_A * x * x)


def _softplus_neg(lam):
    nl = -lam
    return jnp.maximum(nl, 0.0) + jnp.log(1.0 + jnp.exp(-jnp.abs(nl)))


def _lru_coeffs(r, lam, first_row):
    c8 = LRU_C * _softplus_neg(lam)
    la = -(c8 * r)
    a = jnp.exp(la)
    m2 = jnp.tanh(-la) * (1.0 + a * a)
    mult = jnp.where(first_row, 1.0, jnp.sqrt(m2))
    return c8, a, m2, mult


def _scan_fwd(a, u, carry):
    n = a.shape[0]
    row = lax.broadcasted_iota(jnp.int32, (n, 1), 0)
    acc_a, acc_h = a, u
    s = 1
    while s < n:
        m = row >= s
        h_s = jnp.where(m, pltpu.roll(acc_h, s, 0), 0.0)
        a_s = jnp.where(m, pltpu.roll(acc_a, s, 0), 1.0)
        acc_h = acc_a * h_s + acc_h
        acc_a = acc_a * a_s
        s *= 2
    return acc_h + acc_a * carry


def _scan_bwd(b, g, carry):
    n = b.shape[0]
    row = lax.broadcasted_iota(jnp.int32, (n, 1), 0)
    acc_b, acc_l = b, g
    s = 1
    while s < n:
        m = row < n - s
        l_s = jnp.where(m, pltpu.roll(acc_l, n - s, 0), 0.0)
        b_s = jnp.where(m, pltpu.roll(acc_b, n - s, 0), 1.0)
        acc_l = acc_b * l_s + acc_l
        acc_b = acc_b * b_s
        s *= 2
    return acc_l + acc_b * carry


def _shift_down(ext, k, halo):
    return pltpu.roll(ext, k, 0)[halo:] if k else ext[halo:]


def _shift_up(ext, k, ts):
    return pltpu.roll(ext, ext.shape[0] - k, 0)[:ts] if k else ext[:ts]


def _rows(ts, width, nt=None, col=0):
    if nt is None:
        return pl.BlockSpec((ts, width), lambda i: (i, col))
    return pl.BlockSpec((ts, width), lambda i: (nt - 1 - i, col))


def _resident(shape):
    zeros = (0,) * len(shape)
    return pl.BlockSpec(shape, lambda i: zeros, pipeline_mode=pl.Buffered(1))


def _acc(shape):
    zeros = (0,) * len(shape)
    return pl.BlockSpec(shape, lambda i: zeros)


def _params():
    return pltpu.CompilerParams(dimension_semantics=("arbitrary",), vmem_limit_bytes=VMEM_LIMIT)


def _sds(shape, dtype=F32):
    return jax.ShapeDtypeStruct(shape, dtype)


def _fwd_in_pool(x, g_pre, w_in, pool_w, pool_scale, w_pool_out, ts):
    s = x.shape[0]

    def body(x_ref, g_ref, win_ref, pw_ref, ps_ref, wpo_ref,
             urx_ref, urg_ref, gp_ref, gr_ref, d_ref, yp_ref, z_scr, halo_scr):
        i = pl.program_id(0)

        @pl.when(i == 0)
        def _():
            halo_scr[...] = jnp.zeros_like(halo_scr)

        h1, _, _ = _rms_fwd(x_ref[...], g_ref[...])
        h1 = h1.astype(BF16)
        for j in range(N_CHIPS):
            z_scr[:, j * IN_SHARD:(j + 1) * IN_SHARD] = jnp.dot(h1, win_ref[j], preferred_element_type=F32)
        urx_ref[...] = z_scr[:, 512:1536]
        urg_ref[...] = z_scr[:, 1536:2560]
        gp_ref[...] = z_scr[:, 2560:3584]
        gr_ref[...] = z_scr[:, 3584:4608]
        u = z_scr[:, 0:POOL_WIDTH]
        ext = jnp.concatenate([halo_scr[...], u], axis=0)
        halo_scr[...] = u[ts - POOL_HALO:, :]
        t = i * ts + lax.broadcasted_iota(jnp.int32, (ts, 1), 0)
        y4 = []
        for g, w in enumerate(POOL_WINDOWS):
            lanes = slice(g * POOL_GROUP_DIM, (g + 1) * POOL_GROUP_DIM)
            acc = ext[:, lanes]
            sh = 1
            while sh < w:
                acc = acc + pltpu.roll(acc, sh, 0)
                sh *= 2
            inv = 1.0 / jnp.minimum(t + 1, w).astype(F32)
            dg = acc[POOL_HALO:, :] * inv - u[:, lanes]
            d_ref[:, lanes] = dg
            y4.append(_dot(dg, pw_ref[g]))
        ypre = jnp.concatenate(y4, axis=1) * ps_ref[...]
        ypre = ypre.astype(BF16)
        for j in range(N_CHIPS):
            yp_ref[:, j * 256:(j + 1) * 256] = jnp.dot(ypre, wpo_ref[j], preferred_element_type=F32)

    return pl.pallas_call(
        body, name="fwd_in_pool", grid=(s // ts,),
        in_specs=[_rows(ts, D_MODEL), _resident((1, D_MODEL)), _resident(w_in.shape), _resident(pool_w.shape),
                  _resident((1, POOL_WIDTH)), _resident(w_pool_out.shape)],
        out_specs=[_rows(ts, D_MODEL)] * 4 + [_rows(ts, POOL_WIDTH), _rows(ts, D_MODEL)],
        out_shape=[_sds((s, D_MODEL))] * 4 + [_sds((s, POOL_WIDTH)), _sds((s, D_MODEL))],
        scratch_shapes=[pltpu.VMEM((ts, IN_TOTAL), F32), pltpu.VMEM((POOL_HALO, POOL_WIDTH), F32)],
        compiler_params=_params(),
    )(x, g_pre, w_in, pool_w, pool_scale, w_pool_out)


def _fwd_rnn_merge(urx, urg, gp, gr, ypool, x, conv_w, conv_b, wg, bg, lam, w_rg_out, w_o, g_post, ts):
    s = x.shape[0]

    def body(urx_ref, urg_ref, gp_ref, gr_ref, yp_ref, x_ref, cw_ref, cb_ref, wg_ref, bg_ref, lam_ref, wrg_ref, wo_ref,
             gpost_ref, xc_ref, r_ref, ig_ref, h_ref, yr_ref, mo_ref, x1_ref, halo_scr, carry_scr):
        i = pl.program_id(0)

        @pl.when(i == 0)
        def _():
            halo_scr[...] = jnp.zeros_like(halo_scr)
            carry_scr[...] = jnp.zeros_like(carry_scr)

        urx_v = urx_ref[...]
        ext = jnp.concatenate([halo_scr[...], urx_v], axis=0)
        halo_scr[...] = urx_v[ts - CONV_HALO:, :]
        cw = cw_ref[...]
        xc = (cb_ref[...] + cw[3:4] * urx_v + cw[2:3] * _shift_down(ext, 1, CONV_HALO)
              + cw[1:2] * _shift_down(ext, 2, CONV_HALO) + cw[0:1] * _shift_down(ext, 3, CONV_HALO))
        xc_ref[...] = xc
        xcb = xc.astype(BF16)
        lin = []
        for gate in range(2):
            parts = [jnp.dot(xcb[:, q * GATE_BLOCK:(q + 1) * GATE_BLOCK], wg_ref[gate, q], preferred_element_type=F32)
                     for q in range(GATE_BLOCKS)]
            lin.append(jnp.concatenate(parts, axis=1) + bg_ref[gate:gate + 1, :])
        r = jax.nn.sigmoid(lin[0])
        ig = jax.nn.sigmoid(lin[1])
        r_ref[...] = r
        ig_ref[...] = ig
        first_row = (i * ts + lax.broadcasted_iota(jnp.int32, (ts, 1), 0)) == 0
        _, a, _, mult = _lru_coeffs(r, lam_ref[...], first_row)
        h = _scan_fwd(a, mult * ig * xc, carry_scr[0:1, :])
        carry_scr[0:1, :] = h[ts - 1:ts, :]
        h_ref[...] = h
        gl, _ = _gelu(urg_ref[...])
        yr = _dot(h * gl, wrg_ref[...])
        yr_ref[...] = yr
        merged = jax.nn.sigmoid(gp_ref[...]) * yp_ref[...] + jax.nn.sigmoid(gr_ref[...]) * yr
        mo = _dot(merged, wo_ref[...])
        mo_ref[...] = mo
        y, _, _ = _rms_fwd(mo, gpost_ref[...])
        x1_ref[...] = x_ref[...] + y

    row = _rows(ts, D_MODEL)
    return pl.pallas_call(
        body, name="fwd_rnn_merge", grid=(s // ts,),
        in_specs=[row] * 6 + [_resident(conv_w.shape), _resident((1, D_MODEL)), _resident(wg.shape), _resident(bg.shape),
                              _resident((1, D_MODEL)), _resident(w_rg_out.shape), _resident(w_o.shape),
                              _resident((1, D_MODEL))],
        out_specs=[row] * 7,
        out_shape=[_sds((s, D_MODEL))] * 7,
        scratch_shapes=[pltpu.VMEM((CONV_HALO, D_MODEL), F32), pltpu.VMEM((8, D_MODEL), F32)],
        compiler_params=_params(),
    )(urx, urg, gp, gr, ypool, x, conv_w, conv_b, wg, bg, lam, w_rg_out, w_o, g_post)


def _fwd_ffn(x1, g_pre, w_up, fcw, fcb, w_down, g_post, ts):
    s = x1.shape[0]

    def body(x1_ref, g_ref, wup_ref, fcw_ref, fcb_ref, wd_ref, gpost_ref, up_ref, h2_ref, dn_ref, x2_ref, halo_scr):
        i = pl.program_id(0)

        @pl.when(i == 0)
        def _():
            halo_scr[...] = jnp.zeros_like(halo_scr)

        x1_v = x1_ref[...]
        h2, _, _ = _rms_fwd(x1_v, g_ref[...])
        h2 = h2.astype(BF16)
        h2_ref[...] = h2
        for j in range(N_CHIPS):
            up_ref[:, j * UP_SHARD:(j + 1) * UP_SHARD] = jnp.dot(h2, wup_ref[j], preferred_element_type=F32)
        ug = up_ref[:, 0:D_FF]
        ext = jnp.concatenate([halo_scr[...], ug], axis=0)
        halo_scr[...] = ug[ts - CONV_HALO:, :]
        w = fcw_ref[...]
        gh = (fcb_ref[...] + w[2:3] * ug + w[1:2] * _shift_down(ext, 1, CONV_HALO)
              + w[0:1] * _shift_down(ext, 2, CONV_HALO))
        gl, _ = _gelu(gh)
        dn = _dot(gl * up_ref[:, D_FF:], wd_ref[...])
        dn_ref[...] = dn
        y, _, _ = _rms_fwd(dn, gpost_ref[...])
        x2_ref[...] = x1_v + y

    row = _rows(ts, D_MODEL)
    return pl.pallas_call(
        body, name="fwd_ffn", grid=(s // ts,),
        in_specs=[row, _resident((1, D_MODEL)), _resident(w_up.shape), _resident(fcw.shape), _resident((1, D_FF)),
                  _resident(w_down.shape), _resident((1, D_MODEL))],
        out_specs=[_rows(ts, 2 * D_FF), row, row, row],
        out_shape=[_sds((s, 2 * D_FF)), _sds((s, D_MODEL), BF16), _sds((s, D_MODEL)), _sds((s, D_MODEL))],
        scratch_shapes=[pltpu.VMEM((CONV_HALO, D_FF), F32)],
        compiler_params=_params(),
    )(x1, g_pre, w_up, fcw, fcb, w_down, g_post)


def _ple_loss(x2, p, tgt, g_gate, w_gate, w_proj, g_post, ts):
    s = x2.shape[0]

    def body(x2_ref, p_ref, t_ref, gg_ref, wg_ref, wp_ref, gp_ref, dx2_ref, loss_ref, dwg_ref, dwp_ref, dgg_ref, dgp_ref):
        @pl.when(pl.program_id(0) == 0)
        def _():
            loss_ref[...] = jnp.zeros_like(loss_ref)
            dwg_ref[...] = jnp.zeros_like(dwg_ref)
            dwp_ref[...] = jnp.zeros_like(dwp_ref)
            dgg_ref[...] = jnp.zeros_like(dgg_ref)
            dgp_ref[...] = jnp.zeros_like(dgp_ref)

        x2_v = x2_ref[...]
        n3, xh3, r3 = _rms_fwd(x2_v, gg_ref[...])
        pg = jax.nn.sigmoid(_dot(n3, wg_ref[...]))
        pb = p_ref[...].astype(BF16)
        q = jnp.concatenate([jnp.dot(pb, wp_ref[j], preferred_element_type=F32) for j in range(N_CHIPS)], axis=1)
        ple, qh, rq = _rms_fwd(q, gp_ref[...])
        e = x2_v + pg * ple - t_ref[...]
        loss_ref[...] += 0.5 * jnp.sum(jnp.mean(e * e, axis=-1, keepdims=True), axis=0, keepdims=True)
        dy = e * (1.0 / D_MODEL)
        dpgl = dy * ple * pg * (1.0 - pg)
        dwg_ref[...] += _dot_tn(n3, dpgl)
        dx3, dgg = _rms_bwd(xh3, r3, gg_ref[...], _dot_nt(dpgl, wg_ref[...]))
        dgg_ref[...] += dgg
        dq, dgp = _rms_bwd(qh, rq, gp_ref[...], dy * pg)
        dgp_ref[...] += dgp
        for j in range(N_CHIPS):
            dwp_ref[j] += _dot_tn(pb, dq[:, j * 256:(j + 1) * 256])
        dx2_ref[...] = dy + dx3

    row = _rows(ts, D_MODEL)
    vec = _acc((1, D_MODEL))
    return pl.pallas_call(
        body, name="ple_loss", grid=(s // ts,),
        in_specs=[row, _rows(ts, PLE_DIM), row, _resident((1, D_MODEL)), _resident(w_gate.shape), _resident(w_proj.shape),
                  _resident((1, D_MODEL))],
        out_specs=[row, _acc((1, 128)), _acc(w_gate.shape), _acc(w_proj.shape), vec, vec],
        out_shape=[_sds((s, D_MODEL)), _sds((1, 128)), _sds(w_gate.shape), _sds(w_proj.shape), _sds((1, D_MODEL)),
                   _sds((1, D_MODEL))],
        compiler_params=_params(),
    )(x2, p, tgt, g_gate, w_gate, w_proj, g_post)


def _bwd_ffn_down(dx2, dn, up, fcw, fcb, w_down, g_post, ts):
    s = dx2.shape[0]
    nt = s // ts
    halo_blocks = ts // CONV_HALO

    def body(dx2_ref, dn_ref, up_ref, uph_ref, fcw_ref, fcb_ref, wd_ref, gpost_ref,
             dup_ref, dwd_ref, dfcw_ref, dfcb_ref, dgp_ref, carry_scr):
        i = pl.program_id(0)
        k = nt - 1 - i

        @pl.when(i == 0)
        def _():
            carry_scr[...] = jnp.zeros_like(carry_scr)
            dwd_ref[...] = jnp.zeros_like(dwd_ref)
            dfcw_ref[...] = jnp.zeros_like(dfcw_ref)
            dfcb_ref[...] = jnp.zeros_like(dfcb_ref)
            dgp_ref[...] = jnp.zeros_like(dgp_ref)

        _, xh, r = _rms_fwd(dn_ref[...], gpost_ref[...])
        ddn, dgp = _rms_bwd(xh, r, gpost_ref[...], dx2_ref[...])
        dgp_ref[...] += dgp
        dhid = _dot_nt(ddn, wd_ref[...])
        ug = up_ref[:, 0:D_FF]
        uv = up_ref[:, D_FF:]
        halo = jnp.where(k > 0, uph_ref[...], 0.0)
        ext = jnp.concatenate([halo, ug], axis=0)
        w = fcw_ref[...]
        gh = (fcb_ref[...] + w[2:3] * ug + w[1:2] * _shift_down(ext, 1, CONV_HALO)
              + w[0:1] * _shift_down(ext, 2, CONV_HALO))
        gl, t = _gelu(gh)
        dwd_ref[...] += _dot_tn(gl * uv, ddn)
        dgh = dhid * uv * _gelu_grad(gh, t)
        dup_ref[:, D_FF:] = (dhid * gl).astype(BF16)
        extd = jnp.concatenate([dgh, carry_scr[...]], axis=0)
        carry_scr[...] = dgh[0:CONV_HALO, :]
        d1 = _shift_up(extd, 1, ts)
        d2 = _shift_up(extd, 2, ts)
        dup_ref[:, 0:D_FF] = (w[2:3] * dgh + w[1:2] * d1 + w[0:1] * d2).astype(BF16)
        dfcw_ref[2:3, :] += jnp.sum(ug * dgh, axis=0, keepdims=True)
        dfcw_ref[1:2, :] += jnp.sum(ug * d1, axis=0, keepdims=True)
        dfcw_ref[0:1, :] += jnp.sum(ug * d2, axis=0, keepdims=True)
        dfcb_ref[...] += jnp.sum(dgh, axis=0, keepdims=True)

    row = _rows(ts, D_MODEL, nt)
    halo_spec = pl.BlockSpec((CONV_HALO, D_FF), lambda i: (jnp.maximum((nt - 1 - i) * halo_blocks - 1, 0), 0))
    return pl.pallas_call(
        body, name="bwd_ffn_down", grid=(nt,),
        in_specs=[row, row, _rows(ts, 2 * D_FF, nt), halo_spec, _resident(fcw.shape), _resident((1, D_FF)),
                  _resident(w_down.shape), _resident((1, D_MODEL))],
        out_specs=[_rows(ts, 2 * D_FF, nt), _acc(w_down.shape), _acc(fcw.shape), _acc((1, D_FF)), _acc((1, D_MODEL))],
        out_shape=[_sds((s, 2 * D_FF), BF16), _sds(w_down.shape), _sds(fcw.shape), _sds((1, D_FF)), _sds((1, D_MODEL))],
        scratch_shapes=[pltpu.VMEM((CONV_HALO, D_FF), F32)],
        compiler_params=_params(),
    )(dx2, dn, up, up, fcw, fcb, w_down, g_post)


def _bwd_ffn_up(dup, x1, dx2, g_pre, w_up, ts):
    s = x1.shape[0]

    def body(dup_ref, x1_ref, dx2_ref, g_ref, wup_ref, dx1_ref, dg_ref):
        @pl.when(pl.program_id(0) == 0)
        def _():
            dg_ref[...] = jnp.zeros_like(dg_ref)

        _, xh, r = _rms_fwd(x1_ref[...], g_ref[...])
        dh2 = _dot_nt(dup_ref[:, 0:UP_SHARD], wup_ref[0])
        for j in range(1, N_CHIPS):
            dh2 = dh2 + _dot_nt(dup_ref[:, j * UP_SHARD:(j + 1) * UP_SHARD], wup_ref[j])
        dx, dg = _rms_bwd(xh, r, g_ref[...], dh2)
        dg_ref[...] += dg
        dx1_ref[...] = dx2_ref[...] + dx

    row = _rows(ts, D_MODEL)
    return pl.pallas_call(
        body, name="bwd_ffn_up", grid=(s // ts,),
        in_specs=[_rows(ts, 2 * D_FF), row, row, _resident((1, D_MODEL)), _resident(w_up.shape)],
        out_specs=[row, _acc((1, D_MODEL))],
        out_shape=[_sds((s, D_MODEL)), _sds((1, D_MODEL))],
        compiler_params=_params(),
    )(dup, x1, dx2, g_pre, w_up)


def _dw_up(h2, dup, ts):
    s = h2.shape[0]

    def body(h2_ref, dup_ref, out_ref):
        @pl.when(pl.program_id(1) == 0)
        def _():
            out_ref[...] = jnp.zeros_like(out_ref)

        out_ref[0] += _dot_tn(h2_ref[...], dup_ref[...])

    return pl.pallas_call(
        body, name="dw_up", grid=(N_CHIPS, s // ts),
        in_specs=[pl.BlockSpec((ts, D_MODEL), lambda j, i: (i, 0)), pl.BlockSpec((ts, UP_SHARD), lambda j, i: (i, j))],
        out_specs=pl.BlockSpec((1, D_MODEL, UP_SHARD), lambda j, i: (j, 0, 0)),
        out_shape=_sds((N_CHIPS, D_MODEL, UP_SHARD)),
        compiler_params=pltpu.CompilerParams(dimension_semantics=("arbitrary", "arbitrary"), vmem_limit_bytes=VMEM_LIMIT),
    )(h2, dup)


def _bwd_merge(dx1, mo, gp, gr, ypool, yrnn, g_post, w_o, ts):
    s = dx1.shape[0]

    def body(dx1_ref, mo_ref, gp_ref, gr_ref, yp_ref, yr_ref, g_ref, wo_ref,
             dgp_ref, dgr_ref, dyp_ref, dyr_ref, dwo_ref, dg_ref):
        @pl.when(pl.program_id(0) == 0)
        def _():
            dwo_ref[...] = jnp.zeros_like(dwo_ref)
            dg_ref[...] = jnp.zeros_like(dg_ref)

        _, xh, r = _rms_fwd(mo_ref[...], g_ref[...])
        dmo, dg = _rms_bwd(xh, r, g_ref[...], dx1_ref[...])
        dg_ref[...] += dg
        dmerged = _dot_nt(dmo, wo_ref[...])
        sp = jax.nn.sigmoid(gp_ref[...])
        sr = jax.nn.sigmoid(gr_ref[...])
        yp = yp_ref[...]
        yr = yr_ref[...]
        dwo_ref[...] += _dot_tn(sp * yp + sr * yr, dmo)
        dgp_ref[...] = (dmerged * yp * sp * (1.0 - sp)).astype(BF16)
        dgr_ref[...] = (dmerged * yr * sr * (1.0 - sr)).astype(BF16)
        dyp_ref[...] = (dmerged * sp).astype(BF16)
        dyr_ref[...] = (dmerged * sr).astype(BF16)

    row = _rows(ts, D_MODEL)
    return pl.pallas_call(
        body, name="bwd_merge", grid=(s // ts,),
        in_specs=[row] * 6 + [_resident((1, D_MODEL)), _resident(w_o.shape)],
        out_specs=[row] * 4 + [_acc(w_o.shape), _acc((1, D_MODEL))],
        out_shape=[_sds((s, D_MODEL), BF16)] * 4 + [_sds(w_o.shape), _sds((1, D_MODEL))],
        compiler_params=_params(),
    )(dx1, mo, gp, gr, ypool, yrnn, g_post, w_o)


def _bwd_rnn(dyr, urx, urg, xc, r, ig, h, conv_w, wg, lam, w_rg_out, ts):
    s = urx.shape[0]
    nt = s // ts
    halo_blocks = ts // CONV_HALO

    def body(dyr_ref, urx_ref, urg_ref, xc_ref, r_ref, ig_ref, h_ref, hh_ref, cw_ref, wg_ref, lam_ref, wrg_ref,
             durx_ref, durg_ref, dwrg_ref, dwg_ref, dcw_ref, dcb_ref, dbg_ref, dlam_ref, mu_scr, carry_scr):
        i = pl.program_id(0)
        k = nt - 1 - i

        @pl.when(i == 0)
        def _():
            mu_scr[...] = jnp.zeros_like(mu_scr)
            carry_scr[...] = jnp.zeros_like(carry_scr)
            dwrg_ref[...] = jnp.zeros_like(dwrg_ref)
            dwg_ref[...] = jnp.zeros_like(dwg_ref)
            dcw_ref[...] = jnp.zeros_like(dcw_ref)
            dcb_ref[...] = jnp.zeros_like(dcb_ref)
            dbg_ref[...] = jnp.zeros_like(dbg_ref)
            dlam_ref[...] = jnp.zeros_like(dlam_ref)

        row = lax.broadcasted_iota(jnp.int32, (ts, 1), 0)
        first_row = (k * ts + row) == 0
        h = h_ref[...]
        urg_v = urg_ref[...]
        dyr_v = dyr_ref[...]
        dhr = _dot_nt(dyr_v, wrg_ref[...])
        gl, t = _gelu(urg_v)
        dwrg_ref[...] += _dot_tn(h * gl, dyr_v)
        durg_ref[...] = (dhr * h * _gelu_grad(urg_v, t)).astype(BF16)
        r_v = r_ref[...]
        ig_v = ig_ref[...]
        xc_v = xc_ref[...]
        lam_v = lam_ref[...]
        c8, a, m2, mult = _lru_coeffs(r_v, lam_v, first_row)
        b = jnp.where(row == ts - 1, 1.0, pltpu.roll(a, ts - 1, 0))
        lt = _scan_bwd(b, dhr * gl, mu_scr[0:1, :])
        mu_scr[0:1, :] = a[0:1, :] * lt[0:1, :]
        h_before = jnp.where(k > 0, hh_ref[CONV_HALO - 1:CONV_HALO, :], 0.0)
        hprev = jnp.where(row == 0, h_before, pltpu.roll(h, 1, 0))
        dmult = lt * ig_v * xc_v
        da = lt * hprev - jnp.where(first_row, 0.0, dmult * a * lax.rsqrt(m2))
        dla = da * a
        dlam_ref[...] += jnp.sum(dla * r_v, axis=0, keepdims=True)
        dlr = (dla * (-c8)) * r_v * (1.0 - r_v)
        dli = (lt * mult * xc_v) * ig_v * (1.0 - ig_v)
        dbg_ref[0:1, :] += jnp.sum(dlr, axis=0, keepdims=True)
        dbg_ref[1:2, :] += jnp.sum(dli, axis=0, keepdims=True)
        xcb = xc_v.astype(BF16)
        parts = []
        for q in range(GATE_BLOCKS):
            blk = slice(q * GATE_BLOCK, (q + 1) * GATE_BLOCK)
            dlr_q = dlr[:, blk].astype(BF16)
            dli_q = dli[:, blk].astype(BF16)
            parts.append(_dot_nt(dlr_q, wg_ref[0, q]) + _dot_nt(dli_q, wg_ref[1, q]))
            dwg_ref[0, q] += _dot_tn(xcb[:, blk], dlr_q)
            dwg_ref[1, q] += _dot_tn(xcb[:, blk], dli_q)
        dxc = lt * mult * ig_v + jnp.concatenate(parts, axis=1)
        extd = jnp.concatenate([dxc, carry_scr[...]], axis=0)
        carry_scr[...] = dxc[0:CONV_HALO, :]
        cw = cw_ref[...]
        urx_v = urx_ref[...]
        durx = cw[3:4] * dxc
        dcw_ref[3:4, :] += jnp.sum(urx_v * dxc, axis=0, keepdims=True)
        for j in (1, 2, 3):
            dj = _shift_up(extd, j, ts)
            durx = durx + cw[3 - j:4 - j] * dj
            dcw_ref[3 - j:4 - j, :] += jnp.sum(urx_v * dj, axis=0, keepdims=True)
        durx_ref[...] = durx.astype(BF16)
        dcb_ref[...] += jnp.sum(dxc, axis=0, keepdims=True)

        @pl.when(i == nt - 1)
        def _():
            dlam_ref[...] = dlam_ref[...] * (LRU_C * jax.nn.sigmoid(-lam_v))

    row_spec = _rows(ts, D_MODEL, nt)
    halo_spec = pl.BlockSpec((CONV_HALO, D_MODEL), lambda i: (jnp.maximum((nt - 1 - i) * halo_blocks - 1, 0), 0))
    vec = _acc((1, D_MODEL))
    return pl.pallas_call(
        body, name="bwd_rnn", grid=(nt,),
        in_specs=[row_spec] * 7 + [halo_spec, _resident(conv_w.shape), _resident(wg.shape), _resident((1, D_MODEL)),
                                   _resident(w_rg_out.shape)],
        out_specs=[row_spec, row_spec, _acc(w_rg_out.shape), _acc(wg.shape), _acc(conv_w.shape), vec,
                   _acc((2, D_MODEL)), vec],
        out_shape=[_sds((s, D_MODEL), BF16), _sds((s, D_MODEL), BF16), _sds(w_rg_out.shape), _sds(wg.shape),
                   _sds(conv_w.shape), _sds((1, D_MODEL)), _sds((2, D_MODEL)), _sds((1, D_MODEL))],
        scratch_shapes=[pltpu.VMEM((8, D_MODEL), F32), pltpu.VMEM((CONV_HALO, D_MODEL), F32)],
        compiler_params=_params(),
    )(dyr, urx, urg, xc, r, ig, h, h, conv_w, wg, lam, w_rg_out)


def _bwd_pool_in(dyp, d, durx, durg, dgp, dgr, x, dx1, g_pre, w_in, pool_w, pool_scale, w_pool_out, ts):
    s = x.shape[0]
    nt = s // ts

    def body(dyp_ref, d_ref, durx_ref, durg_ref, dgp_ref, dgr_ref, x_ref, dx1_ref, g_ref, win_ref, pw_ref, ps_ref,
             wpo_ref, gx_ref, dwin_ref, dwpo_ref, dpw_ref, dps_ref, dg_ref, dz_scr, carry_scr):
        i = pl.program_id(0)
        k = nt - 1 - i

        @pl.when(i == 0)
        def _():
            carry_scr[...] = jnp.zeros_like(carry_scr)
            dwin_ref[...] = jnp.zeros_like(dwin_ref)
            dwpo_ref[...] = jnp.zeros_like(dwpo_ref)
            dpw_ref[...] = jnp.zeros_like(dpw_ref)
            dps_ref[...] = jnp.zeros_like(dps_ref)
            dg_ref[...] = jnp.zeros_like(dg_ref)

        dyp_v = dyp_ref[...]
        d_v = d_ref[...]
        ps = ps_ref[...]
        dypre = _dot_nt(dyp_v[:, 0:256], wpo_ref[0])
        for j in range(1, N_CHIPS):
            dypre = dypre + _dot_nt(dyp_v[:, j * 256:(j + 1) * 256], wpo_ref[j])
        y4 = jnp.concatenate([_dot(d_v[:, g * 128:(g + 1) * 128], pw_ref[g]) for g in range(POOL_GROUPS)], axis=1)
        ypre = (y4 * ps).astype(BF16)
        for j in range(N_CHIPS):
            dwpo_ref[j] += _dot_tn(ypre, dyp_v[:, j * 256:(j + 1) * 256])
        dps_ref[...] += jnp.sum(dypre * y4, axis=0, keepdims=True)
        dy4 = dypre * ps
        t = k * ts + lax.broadcasted_iota(jnp.int32, (ts, 1), 0)
        for g, w in enumerate(POOL_WINDOWS):
            lanes = slice(g * POOL_GROUP_DIM, (g + 1) * POOL_GROUP_DIM)
            dd = _dot_nt(dy4[:, lanes], pw_ref[g])
            dpw_ref[g] += _dot_tn(d_v[:, lanes], dy4[:, lanes])
            e = dd * (1.0 / jnp.minimum(t + 1, w).astype(F32))
            acc = jnp.concatenate([e, carry_scr[:, lanes]], axis=0)
            carry_scr[:, lanes] = e[0:POOL_HALO, :]
            n = ts + POOL_HALO
            sh = 1
            while sh < w:
                acc = acc + pltpu.roll(acc, n - sh, 0)
                sh *= 2
            dz_scr[:, lanes] = (acc[:ts, :] - dd).astype(BF16)
        dz_scr[:, 512:1536] = durx_ref[...]
        dz_scr[:, 1536:2560] = durg_ref[...]
        dz_scr[:, 2560:3584] = dgp_ref[...]
        dz_scr[:, 3584:4608] = dgr_ref[...]
        h1, xh, r = _rms_fwd(x_ref[...], g_ref[...])
        h1 = h1.astype(BF16)
        dh1 = _dot_nt(dz_scr[:, 0:IN_SHARD], win_ref[0])
        dwin_ref[0] += _dot_tn(h1, dz_scr[:, 0:IN_SHARD])
        for j in range(1, N_CHIPS):
            cols = slice(j * IN_SHARD, (j + 1) * IN_SHARD)
            dh1 = dh1 + _dot_nt(dz_scr[:, cols], win_ref[j])
            dwin_ref[j] += _dot_tn(h1, dz_scr[:, cols])
        dx, dg = _rms_bwd(xh, r, g_ref[...], dh1)
        dg_ref[...] += dg
        gx_ref[...] = dx1_ref[...] + dx

    row = _rows(ts, D_MODEL, nt)
    return pl.pallas_call(
        body, name="bwd_pool_in", grid=(nt,),
        in_specs=[row, _rows(ts, POOL_WIDTH, nt)] + [row] * 6 + [_resident((1, D_MODEL)), _resident(w_in.shape),
                                                                _resident(pool_w.shape), _resident((1, POOL_WIDTH)),
                                                                _resident(w_pool_out.shape)],
        out_specs=[row, _acc(w_in.shape), _acc(w_pool_out.shape), _acc(pool_w.shape), _acc((1, POOL_WIDTH)),
                   _acc((1, D_MODEL))],
        out_shape=[_sds((s, D_MODEL)), _sds(w_in.shape), _sds(w_pool_out.shape), _sds(pool_w.shape), _sds((1, POOL_WIDTH)),
                   _sds((1, D_MODEL))],
        scratch_shapes=[pltpu.VMEM((ts, IN_TOTAL), BF16), pltpu.VMEM((POOL_HALO, POOL_WIDTH), F32)],
        compiler_params=_params(),
    )(dyp, d, durx, durg, dgp, dgr, x, dx1, g_pre, w_in, pool_w, pool_scale, w_pool_out)


def _place():
    x, y, c = lax.axis_index("x"), lax.axis_index("y"), lax.axis_index("c")
    others = [(1 - x, y), (x, 1 - y), (1 - x, 1 - y)]
    return x, y, c, 2 * x + y, others


def _remote(src, dst, send_sem, recv_sem, to):
    return pltpu.make_async_remote_copy(src_ref=src, dst_ref=dst, send_sem=send_sem, recv_sem=recv_sem,
                                        device_id=to, device_id_type=MESH)


_ANY = pl.BlockSpec(memory_space=pl.ANY)


def _own_slot(w, place, dtype):
    a, b = w.shape
    tr = _row_tile(a, b)

    def body(place_ref, w_ref, out_ref):
        out_ref[0] = w_ref[...].astype(dtype)

    return pl.pallas_call(
        body, name="own_slot",
        grid_spec=pltpu.PrefetchScalarGridSpec(
            num_scalar_prefetch=1, grid=(a // tr,),
            in_specs=[pl.BlockSpec((tr, b), lambda i, s: (i, 0))],
            out_specs=pl.BlockSpec((1, tr, b), lambda i, s: (s[0], i, 0))),
        out_shape=_sds((N_CHIPS, a, b), dtype),
        compiler_params=pltpu.CompilerParams(dimension_semantics=("arbitrary",)),
    )(place, w)


def _all_gather(bufs):
    n = len(bufs)

    def body(*refs):
        outs = refs[n:2 * n]
        send_sems, recv_sems = refs[2 * n:]
        x, y, c, me, others = _place()
        sibling = (x, y, 1 - c)
        first, passed = [], []
        for w, out in enumerate(outs):
            ah = out.shape[1] // 2
            mine = out.at[me, pl.ds(c * ah, ah)]
            for j, (ox, oy) in enumerate(others):
                cp = _remote(mine, mine, send_sems.at[w, j], recv_sems.at[w, j], (ox, oy, c))
                cp.start()
                first.append(cp)
        for w, out in enumerate(outs):
            ah = out.shape[1] // 2
            for j, (ox, oy) in enumerate(others):
                slot = out.at[2 * ox + oy, pl.ds(c * ah, ah)]
                _remote(slot, slot, send_sems.at[w, j], recv_sems.at[w, j], (ox, oy, c)).wait_recv()
                cp = _remote(slot, slot, send_sems.at[w, 3 + j], recv_sems.at[w, 3 + j], sibling)
                cp.start()
                passed.append(cp)
        for w, out in enumerate(outs):
            ah = out.shape[1] // 2
            for j, (ox, oy) in enumerate(others):
                slot = out.at[2 * ox + oy, pl.ds((1 - c) * ah, ah)]
                _remote(slot, slot, send_sems.at[w, 3 + j], recv_sems.at[w, 3 + j], sibling).wait_recv()
        for cp in first + passed:
            cp.wait_send()

    return pl.pallas_call(
        body, name="all_gather_weights",
        in_specs=[_ANY] * n, out_specs=[_ANY] * n,
        out_shape=[_sds(b.shape, b.dtype) for b in bufs],
        input_output_aliases={i: i for i in range(n)},
        scratch_shapes=[pltpu.SemaphoreType.DMA((n, 6)), pltpu.SemaphoreType.DMA((n, 6))],
    )(*bufs)


def _sibling_halves(grads):
    n = len(grads)

    def body(*refs):
        srcs, outs = refs[:n], refs[n:2 * n]
        send_sems, recv_sems = refs[2 * n:]
        x, y, c, _, _ = _place()
        sibling = (x, y, 1 - c)
        copies = []
        for w, (src, out) in enumerate(zip(srcs, outs)):
            ah = out.shape[1]
            cp = _remote(src.at[:, pl.ds((1 - c) * ah, ah)], out, send_sems.at[w], recv_sems.at[w], sibling)
            cp.start()
            copies.append(cp)
        for cp in copies:
            cp.wait()

    return pl.pallas_call(
        body, name="grad_sibling_halves",
        in_specs=[_ANY] * n, out_specs=[_ANY] * n,
        out_shape=[_sds((g.shape[0], g.shape[1] // 2, g.shape[2]), g.dtype) for g in grads],
        scratch_shapes=[pltpu.SemaphoreType.DMA((n,)), pltpu.SemaphoreType.DMA((n,))],
    )(*grads)


def _chip_exchange(sends, accs):
    n = len(accs)
    given = [s for s in sends if s is not None]

    def body(*refs):
        send_refs = iter(refs[:len(given)])
        srcs = [next(send_refs) if s is not None else None for s in sends]
        outs = refs[len(given) + n:len(given) + 2 * n]
        send_sems, recv_sems = refs[len(given) + 2 * n:]
        x, y, c, me, others = _place()
        copies = []
        for w, out in enumerate(outs):
            for j, (ox, oy) in enumerate(others):
                src = out.at[me] if srcs[w] is None else srcs[w].at[2 * ox + oy]
                cp = _remote(src, out.at[me], send_sems.at[w, j], recv_sems.at[w, j], (ox, oy, c))
                cp.start()
                copies.append(cp)
        for w, out in enumerate(outs):
            for j, (ox, oy) in enumerate(others):
                slot = out.at[2 * ox + oy]
                _remote(slot, slot, send_sems.at[w, j], recv_sems.at[w, j], (ox, oy, c)).wait_recv()
        for cp in copies:
            cp.wait_send()

    return pl.pallas_call(
        body, name="grad_chip_exchange",
        in_specs=[_ANY] * (len(given) + n), out_specs=[_ANY] * n,
        out_shape=[_sds(a.shape, a.dtype) for a in accs],
        input_output_aliases={len(given) + i: i for i in range(n)},
        scratch_shapes=[pltpu.SemaphoreType.DMA((n, 3)), pltpu.SemaphoreType.DMA((n, 3))],
    )(*given, *accs)


def _sibling_share(shares):
    n = len(shares)

    def body(*refs):
        outs = refs[n:2 * n]
        send_sems, recv_sems = refs[2 * n:]
        x, y, c, _, _ = _place()
        sibling = (x, y, 1 - c)
        copies = []
        for w, out in enumerate(outs):
            cp = _remote(out.at[c], out.at[c], send_sems.at[w], recv_sems.at[w], sibling)
            cp.start()
            copies.append(cp)
        for w, out in enumerate(outs):
            slot = out.at[1 - c]
            _remote(slot, slot, send_sems.at[w], recv_sems.at[w], sibling).wait_recv()
        for cp in copies:
            cp.wait_send()

    return pl.pallas_call(
        body, name="grad_sibling_share",
        in_specs=[_ANY] * n, out_specs=[_ANY] * n,
        out_shape=[_sds(s.shape, s.dtype) for s in shares],
        input_output_aliases={i: i for i in range(n)},
        scratch_shapes=[pltpu.SemaphoreType.DMA((n,)), pltpu.SemaphoreType.DMA((n,))],
    )(*shares)


TILE_BYTES = 2 * 1024 * 1024
PARTIAL_TILE_BYTES = 512 * 1024


def _row_tile(rows, cols, limit=TILE_BYTES):
    best = 8
    for tr in range(8, rows + 1, 8):
        if rows % tr == 0 and tr * cols * 4 <= limit:
            best = tr
    assert rows % best == 0, (rows, cols)
    return best


def _chip_partial(g, got, place, wire_dtype):
    ns, ah, b = got.shape
    sharded = ns == N_CHIPS
    tr = _row_tile(ah, b, PARTIAL_TILE_BYTES)
    nb = ah // tr

    def body(place_ref, *refs):
        g_refs, got_refs, outs = refs[:ns], refs[ns:2 * ns], refs[2 * ns:]
        parts = [g_refs[k][0] + got_refs[k][0] for k in range(ns)]
        own = parts[0]
        if sharded:
            for k in range(ns):
                outs[0][k] = parts[k].astype(wire_dtype)
                if k:
                    own = jnp.where(place_ref[0] == k, parts[k], own)
        outs[-1][0] = own.astype(wire_dtype)

    blk = (1, tr, b)
    in_specs = ([pl.BlockSpec(blk, lambda i, s, k=k: (k, s[1] * nb + i, 0)) for k in range(ns)]
                + [pl.BlockSpec(blk, lambda i, s, k=k: (k, i, 0)) for k in range(ns)])
    acc_spec = pl.BlockSpec(blk, lambda i, s: (s[0], i, 0))
    acc_shape = _sds((N_CHIPS, ah, b), wire_dtype)
    out = pl.pallas_call(
        body, name="grad_chip_partial",
        grid_spec=pltpu.PrefetchScalarGridSpec(
            num_scalar_prefetch=1, grid=(nb,), in_specs=in_specs,
            out_specs=[pl.BlockSpec((ns, tr, b), lambda i, s: (0, i, 0)), acc_spec] if sharded else [acc_spec]),
        out_shape=[acc_shape, acc_shape] if sharded else [acc_shape],
        compiler_params=pltpu.CompilerParams(dimension_semantics=("arbitrary",), vmem_limit_bytes=32 * 1024 * 1024),
    )(place, *([g] * ns), *([got] * ns))
    return (out[0], out[1]) if sharded else (None, out[0])


def _chip_sum(acc, place):
    _, ah, b = acc.shape
    tr = _row_tile(ah, b)

    def body(place_ref, p_ref, out_ref):
        total = p_ref[0].astype(F32) + p_ref[1].astype(F32)
        total = total + p_ref[2].astype(F32)
        out_ref[0] = total + p_ref[3].astype(F32)

    return pl.pallas_call(
        body, name="grad_chip_sum",
        grid_spec=pltpu.PrefetchScalarGridSpec(
            num_scalar_prefetch=1, grid=(ah // tr,),
            in_specs=[pl.BlockSpec((N_CHIPS, tr, b), lambda i, s: (0, i, 0))],
            out_specs=pl.BlockSpec((1, tr, b), lambda i, s: (s[1], i, 0))),
        out_shape=_sds((2, ah, b)),
        compiler_params=pltpu.CompilerParams(dimension_semantics=("arbitrary",)),
    )(place, acc)


def _adamw(w, g, m, v):
    a, b = w.shape
    tr = _row_tile(a, b)

    def body(w_ref, g_ref, m_ref, v_ref, d_ref, nm_ref, nv_ref):
        g_v = g_ref[...]
        nm = ADAM_B1 * m_ref[...] + (1.0 - ADAM_B1) * g_v
        nv = ADAM_B2 * v_ref[...] + (1.0 - ADAM_B2) * (g_v * g_v)
        m_hat = nm / (1.0 - ADAM_B1 ** ADAM_STEP)
        v_hat = nv / (1.0 - ADAM_B2 ** ADAM_STEP)
        d_ref[...] = -ADAM_LR * (m_hat / (jnp.sqrt(v_hat) + ADAM_EPS) + ADAM_WD * w_ref[...])
        nm_ref[...] = nm
        nv_ref[...] = nv

    blk = pl.BlockSpec((tr, b), lambda i: (i, 0))
    return pl.pallas_call(
        body, name="adamw", grid=(a // tr,),
        in_specs=[blk] * 4, out_specs=[blk] * 3, out_shape=[_sds((a, b))] * 3,
        compiler_params=pltpu.CompilerParams(dimension_semantics=("arbitrary",)),
    )(w, g, m, v)


TINY_ROWS, TINY_COLS = 16, 768
SMALL_COLS = 128
SMALL_ROWS = 1632


def _pack_tiny(conv_w, b_gates, fcw):
    ns = conv_w.shape[0]
    pad = lambda t: jnp.pad(t, ((0, 0), (0, 0), (0, TINY_COLS - t.shape[2])))
    z = lambda rows: jnp.zeros((ns, rows, TINY_COLS), F32)
    return jnp.concatenate([pad(conv_w), pad(b_gates), z(2), fcw, z(5)], axis=1)


def _unpack_tiny(t):
    return t[:, 0:4, 0:256], t[:, 4:6, 0:256], t[:, 8:11, :]


def _cols_to_shards(t, n):
    return t.reshape(t.shape[0], N_CHIPS, n).transpose(1, 0, 2)


def _shards_to_cols(t):
    return t.transpose(1, 0, 2).reshape(t.shape[1], -1)


_SMALL_ORDER = ("g_mix_pre", "g_mix_post", "conv_b", "lru_lambda", "g_ffn_pre", "g_ffn_post", "g_ple_gate", "g_ple_post",
                "pool_scale", "ffn_conv_b", "pool_w", "w_rg_gates")


def _pack_small(parts):
    flat = jnp.concatenate([parts[k].reshape(-1).astype(F32) for k in _SMALL_ORDER])
    flat = jnp.pad(flat, (0, SMALL_ROWS * SMALL_COLS - flat.shape[0]))
    return flat.reshape(SMALL_ROWS, SMALL_COLS)


def _unpack_small(packed, shapes):
    flat = packed.reshape(-1)
    out, off = {}, 0
    for k in _SMALL_ORDER:
        size = 1
        for dim in shapes[k]:
            size *= dim
        out[k] = flat[off:off + size].reshape(shapes[k])
        off += size
    return out


def _gates_block_diag(w):
    w4 = w.reshape(2, GATE_BLOCKS, 4, RNN_HEAD_DIM, RNN_HEAD_DIM)
    eye = jnp.eye(4, dtype=w.dtype)
    return jnp.einsum("gqhij,hk->gqhikj", w4, eye).reshape(2, GATE_BLOCKS, GATE_BLOCK, GATE_BLOCK)


def _gates_from_block_diag(dw):
    d6 = dw.reshape(2, GATE_BLOCKS, 4, RNN_HEAD_DIM, 4, RNN_HEAD_DIM)
    blocks = [d6[:, :, hh, :, hh, :] for hh in range(4)]
    return jnp.stack(blocks, axis=2).reshape(2, RNN_HEADS, RNN_HEAD_DIM, RNN_HEAD_DIM)


ROW_TILE = 256

_SHARDED = ("w_in", "w_pool_out", "w_rg_out", "w_o", "w_up", "w_down", "w_ple_gate", "w_ple_proj")
_WEIGHTS = ("g_mix_pre", "g_mix_post", "w_in", "pool_w", "pool_scale", "w_pool_out", "conv_w", "conv_b", "w_rg_gates",
            "b_rg_gates", "lru_lambda", "w_rg_out", "w_o", "g_ffn_pre", "g_ffn_post", "w_up", "ffn_conv_w", "ffn_conv_b",
            "w_down", "g_ple_gate", "w_ple_gate", "w_ple_proj", "g_ple_post")


def _local_step(x, p, tgt, full, rep, ts):
    conv_w_s, b_gates_s, fcw_s = _unpack_tiny(full["tiny"])
    conv_w, b_gates, fcw = _shards_to_cols(conv_w_s), _shards_to_cols(b_gates_s), _shards_to_cols(fcw_s)
    vec = lambda k: rep[k].reshape(1, -1)
    pool_w = rep["pool_w"].astype(BF16)
    wg = _gates_block_diag(rep["w_rg_gates"]).astype(BF16)
    w_rg_out = full["w_rg_out"].reshape(D_MODEL, D_MODEL)
    w_o = full["w_o"].reshape(D_MODEL, D_MODEL)
    w_down = full["w_down"].reshape(D_FF, D_MODEL)
    w_gate = full["w_ple_gate"].reshape(D_MODEL, D_MODEL)

    urx, urg, gp, gr, d, ypool = _fwd_in_pool(x, vec("g_mix_pre"), full["w_in"], pool_w, vec("pool_scale"),
                                              full["w_pool_out"], ts)
    xc, r, ig, h, yrnn, mo, x1 = _fwd_rnn_merge(urx, urg, gp, gr, ypool, x, conv_w, vec("conv_b"), wg, b_gates,
                                                vec("lru_lambda"), w_rg_out, w_o, vec("g_mix_post"), ts)
    up, h2, dn, x2 = _fwd_ffn(x1, vec("g_ffn_pre"), full["w_up"], fcw, vec("ffn_conv_b"), w_down, vec("g_ffn_post"), ts)
    dx2, loss, d_w_gate, d_w_proj, d_g_ple_gate, d_g_ple_post = _ple_loss(
        x2, p, tgt, vec("g_ple_gate"), w_gate, full["w_ple_proj"], vec("g_ple_post"), ts)
    dup, d_w_down, d_fcw, d_fcb, d_g_ffn_post = _bwd_ffn_down(dx2, dn, up, fcw, vec("ffn_conv_b"), w_down,
                                                               vec("g_ffn_post"), ts)
    dx1, d_g_ffn_pre = _bwd_ffn_up(dup, x1, dx2, vec("g_ffn_pre"), full["w_up"], ts)
    d_w_up = _dw_up(h2, dup, ts)
    dgp, dgr, dyp, dyr, d_w_o, d_g_mix_post = _bwd_merge(dx1, mo, gp, gr, ypool, yrnn, vec("g_mix_post"), w_o, ts)
    durx, durg, d_w_rg_out, d_wg, d_conv_w, d_conv_b, d_b_gates, d_lam = _bwd_rnn(
        dyr, urx, urg, xc, r, ig, h, conv_w, wg, vec("lru_lambda"), w_rg_out, ts)
    grad_x, d_w_in, d_w_pool_out, d_pool_w, d_pool_scale, d_g_mix_pre = _bwd_pool_in(
        dyp, d, durx, durg, dgp, dgr, x, dx1, vec("g_mix_pre"), full["w_in"], pool_w, vec("pool_scale"),
        full["w_pool_out"], ts)

    sharded = [d_w_in, d_w_pool_out, d_w_rg_out.reshape(N_CHIPS, -1, D_MODEL), d_w_o.reshape(N_CHIPS, -1, D_MODEL),
               d_w_up, d_w_down.reshape(N_CHIPS, -1, D_MODEL), d_w_gate.reshape(N_CHIPS, -1, D_MODEL), d_w_proj,
               _pack_tiny(_cols_to_shards(d_conv_w, 256), _cols_to_shards(d_b_gates, 256), _cols_to_shards(d_fcw, 768))]
    replicated = {"g_mix_pre": d_g_mix_pre, "g_mix_post": d_g_mix_post, "conv_b": d_conv_b, "lru_lambda": d_lam,
                  "g_ffn_pre": d_g_ffn_pre, "g_ffn_post": d_g_ffn_post, "g_ple_gate": d_g_ple_gate,
                  "g_ple_post": d_g_ple_post, "pool_scale": d_pool_scale, "ffn_conv_b": d_fcb, "pool_w": d_pool_w,
                  "w_rg_gates": _gates_from_block_diag(d_wg)}
    return loss, grad_x, sharded, replicated


def _reduce_gradients(grads, place):
    got = _sibling_halves(grads)
    wire = [BF16 if g.shape[1] >= 64 and g.shape[2] > SMALL_COLS else F32 for g in grads]
    parts = [_chip_partial(g, r, place, dt) for g, r, dt in zip(grads, got, wire)]
    accs = _chip_exchange([send for send, _ in parts], [acc for _, acc in parts])
    both = _sibling_share([_chip_sum(acc, place) for acc in accs])
    return [b.reshape(2 * b.shape[1], b.shape[2]) for b in both]


def kernel(x, p, g_mix_pre, g_mix_post, w_in, pool_w, pool_scale, w_pool_out, conv_w, conv_b, w_rg_gates, b_rg_gates, lru_lambda, w_rg_out, w_o, g_ffn_pre, g_ffn_post, w_up, ffn_conv_w, ffn_conv_b, w_down, g_ple_gate, w_ple_gate, w_ple_proj, g_ple_post, loss_target, m_g_mix_pre, m_g_mix_post, m_w_in, m_pool_w, m_pool_scale, m_w_pool_out, m_conv_w, m_conv_b, m_w_rg_gates, m_b_rg_gates, m_lru_lambda, m_w_rg_out, m_w_o, m_g_ffn_pre, m_g_ffn_post, m_w_up, m_ffn_conv_w, m_ffn_conv_b, m_w_down, m_g_ple_gate, m_w_ple_gate, m_w_ple_proj, m_g_ple_post, v_g_mix_pre, v_g_mix_post, v_w_in, v_pool_w, v_pool_scale, v_w_pool_out, v_conv_w, v_conv_b, v_w_rg_gates, v_b_rg_gates, v_lru_lambda, v_w_rg_out, v_w_o, v_g_ffn_pre, v_g_ffn_post, v_w_up, v_ffn_conv_w, v_ffn_conv_b, v_w_down, v_g_ple_gate, v_w_ple_gate, v_w_ple_proj, v_g_ple_post):
    args = dict(locals())
    w = {k: args[k][0] for k in _WEIGHTS}
    m = {k: args["m_" + k][0] for k in _WEIGHTS}
    v = {k: args["v_" + k][0] for k in _WEIGHTS}
    small_shapes = {k: w[k].shape for k in _SMALL_ORDER}
    tiny = lambda t: _pack_tiny(t["conv_w"][None], t["b_rg_gates"][None], t["ffn_conv_w"][None])[0]

    place = jnp.stack([2 * lax.axis_index("x") + lax.axis_index("y"), lax.axis_index("c")]).astype(jnp.int32)
    gathered = _all_gather([_own_slot(w[k], place, BF16) for k in _SHARDED] + [_own_slot(tiny(w), place, F32)])
    full = dict(zip(_SHARDED + ("tiny",), gathered))
    loss, grad_x, sharded, replicated = _local_step(x[0], p[0, 0], loss_target[0], full, w, ROW_TILE)

    reduced = _reduce_gradients(sharded + [_pack_small(replicated)[None]], place)
    params = [(w[k], m[k], v[k]) for k in _SHARDED] + [(tiny(w), tiny(m), tiny(v))]
    params.append((_pack_small(w), _pack_small(m), _pack_small(v)))
    updates = [_adamw(pw, g, pm, pv) for (pw, pm, pv), g in zip(params, reduced)]

    out = {"grad": {}, "delta": {}, "new_m": {}, "new_v": {}}
    kinds = ("grad", "delta", "new_m", "new_v")
    for i, k in enumerate(_SHARDED):
        for kind, val in zip(kinds, (reduced[i],) + tuple(updates[i])):
            out[kind][k] = val
    for kind, val in zip(kinds, (reduced[-2],) + tuple(updates[-2])):
        cw, bg, fcw = _unpack_tiny(val[None])
        out[kind].update(conv_w=cw[0], b_rg_gates=bg[0], ffn_conv_w=fcw[0])
    for kind, val in zip(kinds, (reduced[-1],) + tuple(updates[-1])):
        out[kind].update(_unpack_small(val, small_shapes))

    total = lax.psum(loss[0, 0], MESH_AXES)
    result = [total, grad_x[None]]
    for kind in kinds:
        result += [out[kind][k][None] for k in _WEIGHTS]
    return tuple(result)
```

```python
import functools

import jax
import jax.numpy as jnp
from jax import lax
from jax.experimental import pallas as pl
from jax.experimental.pallas import tpu as pltpu

F32 = jnp.float32
BF16 = jnp.bfloat16

D_MODEL = 1024
POOL_WINDOWS = (2, 4, 8, 16)
POOL_GROUPS = 4
POOL_WIDTH = 512
POOL_GROUP_DIM = 128
RNN_HEADS = 16
RNN_HEAD_DIM = 64
GATE_BLOCK = 256
GATE_BLOCKS = D_MODEL // GATE_BLOCK
LRU_C = 8.0
D_FF = 3072
PLE_DIM = 256
RMS_EPS = 1e-6
IN_TOTAL = 4608
N_CHIPS = 4
IN_SHARD = IN_TOTAL // N_CHIPS
UP_SHARD = 2 * D_FF // N_CHIPS
POOL_HALO = 16
CONV_HALO = 8

ADAM_LR = 0.001
ADAM_B1 = 0.9
ADAM_B2 = 0.999
ADAM_EPS = 1e-08
ADAM_WD = 0.01
ADAM_STEP = 10

VMEM_LIMIT = 56 * 1024 * 1024
MESH_AXES = ("x", "y", "c")
MESH = pl.DeviceIdType.MESH

_GELU_C = 0.7978845608028654
_GELU_A = 0.044715


def _dot(a, b):
    return jnp.dot(a.astype(BF16), b.astype(BF16), preferred_element_type=F32)


def _dot_nt(a, b):
    return lax.dot_general(a.astype(BF16), b.astype(BF16), (((1,), (1,)), ((), ())), preferred_element_type=F32)


def _dot_tn(a, b):
    return lax.dot_general(a.astype(BF16), b.astype(BF16), (((0,), (0,)), ((), ())), preferred_element_type=F32)


def _rms_fwd(x, g):
    r = lax.rsqrt(jnp.mean(x * x, axis=-1, keepdims=True) + RMS_EPS)
    xh = x * r
    return xh * g, xh, r


def _rms_bwd(xh, r, g, dy):
    dxh = dy * g
    dg = jnp.sum(dy * xh, axis=0, keepdims=True)
    dx = r * (dxh - xh * jnp.mean(dxh * xh, axis=-1, keepdims=True))
    return dx, dg


def _gelu(x):
    t = jnp.tanh(_GELU_C * (x + _GELU_A * x * x * x))
    return 0.5 * x * (1.0 + t), t


def _gelu_grad(x, t):
    return 0.5 * (1.0 + t) + 0.5 * x * (1.0 - t * t) * _GELU_C * (1.0 + 3.0 * _GELU_A * x * x)


def _softplus_neg(lam):
    nl = -lam
    return jnp.maximum(nl, 0.0) + jnp.log(1.0 + jnp.exp(-jnp.abs(nl)))


def _lru_coeffs(r, lam, first_row):
    c8 = LRU_C * _softplus_neg(lam)
    la = -(c8 * r)
    a = jnp.exp(la)
    m2 = jnp.tanh(-la) * (1.0 + a * a)
    mult = jnp.where(first_row, 1.0, jnp.sqrt(m2))
    return c8, a, m2, mult


def _scan_fwd(a, u, carry):
    n = a.shape[0]
    row = lax.broadcasted_iota(jnp.int32, (n, 1), 0)
    acc_a, acc_h = a, u
    s = 1
    while s < n:
        m = row >= s
        h_s = jnp.where(m, pltpu.roll(acc_h, s, 0), 0.0)
        a_s = jnp.where(m, pltpu.roll(acc_a, s, 0), 1.0)
        acc_h = acc_a * h_s + acc_h
        acc_a = acc_a * a_s
        s *= 2
    return acc_h + acc_a * carry


def _scan_bwd(b, g, carry):
    n = b.shape[0]
    row = lax.broadcasted_iota(jnp.int32, (n, 1), 0)
    acc_b, acc_l = b, g
    s = 1
    while s < n:
        m = row < n - s
        l_s = jnp.where(m, pltpu.roll(acc_l, n - s, 0), 0.0)
        b_s = jnp.where(m, pltpu.roll(acc_b, n - s, 0), 1.0)
        acc_l = acc_b * l_s + acc_l
        acc_b = acc_b * b_s
        s *= 2
    return acc_l + acc_b * carry


def _shift_down(ext, k, halo):
    return pltpu.roll(ext, k, 0)[halo:] if k else ext[halo:]


def _shift_up(ext, k, ts):
    return pltpu.roll(ext, ext.shape[0] - k, 0)[:ts] if k else ext[:ts]


def _rows(ts, width, nt=None, col=0):
    if nt is None:
        return pl.BlockSpec((ts, width), lambda i: (i, col))
    return pl.BlockSpec((ts, width), lambda i: (nt - 1 - i, col))


def _resident(shape):
    zeros = (0,) * len(shape)
    return pl.BlockSpec(shape, lambda i: zeros, pipeline_mode=pl.Buffered(1))


def _acc(shape):
    zeros = (0,) * len(shape)
    return pl.BlockSpec(shape, lambda i: zeros)


def _params():
    return pltpu.CompilerParams(dimension_semantics=("arbitrary",), vmem_limit_bytes=VMEM_LIMIT)


def _sds(shape, dtype=F32):
    return jax.ShapeDtypeStruct(shape, dtype)


class _Task:
    def __init__(self, ins, out_shapes, aliases, sems, start, finish):
        self.ins, self.out_shapes, self.aliases, self.sems = list(ins), list(out_shapes), dict(aliases), list(sems)
        self.start, self.finish = start, finish


def _call(body, name, grid, in_specs, out_specs, out_shape, scratch_shapes, args, tasks=()):
    n_in, n_out, n_scr = len(in_specs), len(out_specs), len(scratch_shapes)
    t_in = [len(t.ins) for t in tasks]
    t_out = [len(t.out_shapes) for t in tasks]
    t_sem = [len(t.sems) for t in tasks]
    steps = 1
    for g in grid:
        steps *= g

    def take(refs, pos, counts):
        groups = []
        for c in counts:
            groups.append(refs[pos:pos + c])
            pos += c
        return groups, pos

    def wrapped(*refs):
        (cin,), pos = take(refs, 0, [n_in])
        tin, pos = take(refs, pos, t_in)
        (cout,), pos = take(refs, pos, [n_out])
        tout, pos = take(refs, pos, t_out)
        (cscr,), pos = take(refs, pos, [n_scr])
        tsem, pos = take(refs, pos, t_sem)
        if not grid:
            for t, a, b, c in zip(tasks, tin, tout, tsem):
                t.start(a, b, c)
            for t, a, b, c in zip(tasks, tin, tout, tsem):
                t.finish(a, b, c)
            return
        step = pl.program_id(0)
        for axis in range(1, len(grid)):
            step = step * grid[axis] + pl.program_id(axis)
        if tasks:
            @pl.when(step == 0)
            def _():
                for t, a, b, c in zip(tasks, tin, tout, tsem):
                    t.start(a, b, c)

        body(*cin, *cout, *cscr)
        if tasks:
            @pl.when(step == steps - 1)
            def _():
                for t, a, b, c in zip(tasks, tin, tout, tsem):
                    t.finish(a, b, c)

    aliases, in_pos, out_pos = {}, n_in, n_out
    for t, ni, no in zip(tasks, t_in, t_out):
        aliases.update({in_pos + a: out_pos + b for a, b in t.aliases.items()})
        in_pos, out_pos = in_pos + ni, out_pos + no
    any_spec = pl.BlockSpec(memory_space=pl.ANY)
    kwargs = dict(grid=grid, compiler_params=pltpu.CompilerParams(
        dimension_semantics=("arbitrary",) * len(grid), vmem_limit_bytes=VMEM_LIMIT)) if grid else {}
    out = pl.pallas_call(
        wrapped, name=name,
        in_specs=list(in_specs) + [any_spec] * sum(t_in),
        out_specs=list(out_specs) + [any_spec] * sum(t_out),
        out_shape=list(out_shape) + [s for t in tasks for s in t.out_shapes],
        scratch_shapes=list(scratch_shapes) + [s for t in tasks for s in t.sems],
        input_output_aliases=aliases, **kwargs,
    )(*args, *[a for t in tasks for a in t.ins])
    task_outs, pos = take(list(out), n_out, t_out)
    return list(out[:n_out]), task_outs


def _fwd_in_pool(x, g_pre, w_in, pool_w, pool_scale, w_pool_out, ts, tasks=()):
    s = x.shape[0]

    def body(x_ref, g_ref, win_ref, pw_ref, ps_ref, wpo_ref,
             urx_ref, urg_ref, gp_ref, gr_ref, d_ref, yp_ref, z_scr, halo_scr):
        i = pl.program_id(0)

        @pl.when(i == 0)
        def _():
            halo_scr[...] = jnp.zeros_like(halo_scr)

        h1, _, _ = _rms_fwd(x_ref[...], g_ref[...])
        h1 = h1.astype(BF16)
        for j in range(N_CHIPS):
            z_scr[:, j * IN_SHARD:(j + 1) * IN_SHARD] = jnp.dot(h1, win_ref[j], preferred_element_type=F32)
        urx_ref[...] = z_scr[:, 512:1536]
        urg_ref[...] = z_scr[:, 1536:2560]
        gp_ref[...] = z_scr[:, 2560:3584]
        gr_ref[...] = z_scr[:, 3584:4608]
        u = z_scr[:, 0:POOL_WIDTH]
        ext = jnp.concatenate([halo_scr[...], u], axis=0)
        halo_scr[...] = u[ts - POOL_HALO:, :]
        t = i * ts + lax.broadcasted_iota(jnp.int32, (ts, 1), 0)
        y4 = []
        for g, w in enumerate(POOL_WINDOWS):
            lanes = slice(g * POOL_GROUP_DIM, (g + 1) * POOL_GROUP_DIM)
            acc = ext[:, lanes]
            sh = 1
            while sh < w:
                acc = acc + pltpu.roll(acc, sh, 0)
                sh *= 2
            inv = 1.0 / jnp.minimum(t + 1, w).astype(F32)
            dg = acc[POOL_HALO:, :] * inv - u[:, lanes]
            d_ref[:, lanes] = dg
            y4.append(_dot(dg, pw_ref[g]))
        ypre = jnp.concatenate(y4, axis=1) * ps_ref[...]
        ypre = ypre.astype(BF16)
        for j in range(N_CHIPS):
            yp_ref[:, j * 256:(j + 1) * 256] = jnp.dot(ypre, wpo_ref[j], preferred_element_type=F32)

    return _call(
        body, "fwd_in_pool", (s // ts,),
        [_rows(ts, D_MODEL), _resident((1, D_MODEL)), _resident(w_in.shape), _resident(pool_w.shape),
         _resident((1, POOL_WIDTH)), _resident(w_pool_out.shape)],
        [_rows(ts, D_MODEL)] * 4 + [_rows(ts, POOL_WIDTH), _rows(ts, D_MODEL)],
        [_sds((s, D_MODEL))] * 4 + [_sds((s, POOL_WIDTH)), _sds((s, D_MODEL))],
        [pltpu.VMEM((ts, IN_TOTAL), F32), pltpu.VMEM((POOL_HALO, POOL_WIDTH), F32)],
        (x, g_pre, w_in, pool_w, pool_scale, w_pool_out), tasks)


def _fwd_rnn_merge(urx, urg, gp, gr, ypool, x, conv_w, conv_b, wg, bg, lam, w_rg_out, w_o, g_post, ts, tasks=()):
    s = x.shape[0]

    def body(urx_ref, urg_ref, gp_ref, gr_ref, yp_ref, x_ref, cw_ref, cb_ref, wg_ref, bg_ref, lam_ref, wrg_ref, wo_ref,
             gpost_ref, xc_ref, r_ref, ig_ref, h_ref, yr_ref, mo_ref, x1_ref, halo_scr, carry_scr):
        i = pl.program_id(0)

        @pl.when(i == 0)
        def _():
            halo_scr[...] = jnp.zeros_like(halo_scr)
            carry_scr[...] = jnp.zeros_like(carry_scr)

        urx_v = urx_ref[...]
        ext = jnp.concatenate([halo_scr[...], urx_v], axis=0)
        halo_scr[...] = urx_v[ts - CONV_HALO:, :]
        cw = cw_ref[...]
        xc = (cb_ref[...] + cw[3:4] * urx_v + cw[2:3] * _shift_down(ext, 1, CONV_HALO)
              + cw[1:2] * _shift_down(ext, 2, CONV_HALO) + cw[0:1] * _shift_down(ext, 3, CONV_HALO))
        xc_ref[...] = xc
        xcb = xc.astype(BF16)
        lin = []
        for gate in range(2):
            parts = [jnp.dot(xcb[:, q * GATE_BLOCK:(q + 1) * GATE_BLOCK], wg_ref[gate, q], preferred_element_type=F32)
                     for q in range(GATE_BLOCKS)]
            lin.append(jnp.concatenate(parts, axis=1) + bg_ref[gate:gate + 1, :])
        r = jax.nn.sigmoid(lin[0])
        ig = jax.nn.sigmoid(lin[1])
        r_ref[...] = r
        ig_ref[...] = ig
        first_row = (i * ts + lax.broadcasted_iota(jnp.int32, (ts, 1), 0)) == 0
        _, a, _, mult = _lru_coeffs(r, lam_ref[...], first_row)
        h = _scan_fwd(a, mult * ig * xc, carry_scr[0:1, :])
        carry_scr[0:1, :] = h[ts - 1:ts, :]
        h_ref[...] = h
        gl, _ = _gelu(urg_ref[...])
        yr = _dot(h * gl, wrg_ref[...])
        yr_ref[...] = yr
        merged = jax.nn.sigmoid(gp_ref[...]) * yp_ref[...] + jax.nn.sigmoid(gr_ref[...]) * yr
        mo = _dot(merged, wo_ref[...])
        mo_ref[...] = mo
        y, _, _ = _rms_fwd(mo, gpost_ref[...])
        x1_ref[...] = x_ref[...] + y

    row = _rows(ts, D_MODEL)
    return _call(
        body, "fwd_rnn_merge", (s // ts,),
        [row] * 6 + [_resident(conv_w.shape), _resident((1, D_MODEL)), _resident(wg.shape), _resident(bg.shape),
                     _resident((1, D_MODEL)), _resident(w_rg_out.shape), _resident(w_o.shape), _resident((1, D_MODEL))],
        [row] * 7, [_sds((s, D_MODEL))] * 7,
        [pltpu.VMEM((CONV_HALO, D_MODEL), F32), pltpu.VMEM((8, D_MODEL), F32)],
        (urx, urg, gp, gr, ypool, x, conv_w, conv_b, wg, bg, lam, w_rg_out, w_o, g_post), tasks)


def _fwd_ffn(x1, g_pre, w_up, fcw, fcb, w_down, g_post, ts):
    s = x1.shape[0]

    def body(x1_ref, g_ref, wup_ref, fcw_ref, fcb_ref, wd_ref, gpost_ref, up_ref, h2_ref, dn_ref, x2_ref, halo_scr):
        i = pl.program_id(0)

        @pl.when(i == 0)
        def _():
            halo_scr[...] = jnp.zeros_like(halo_scr)

        x1_v = x1_ref[...]
        h2, _, _ = _rms_fwd(x1_v, g_ref[...])
        h2 = h2.astype(BF16)
        h2_ref[...] = h2
        for j in range(N_CHIPS):
            up_ref[:, j * UP_SHARD:(j + 1) * UP_SHARD] = jnp.dot(h2, wup_ref[j], preferred_element_type=F32)
        ug = up_ref[:, 0:D_FF]
        ext = jnp.concatenate([halo_scr[...], ug], axis=0)
        halo_scr[...] = ug[ts - CONV_HALO:, :]
        w = fcw_ref[...]
        gh = (fcb_ref[...] + w[2:3] * ug + w[1:2] * _shift_down(ext, 1, CONV_HALO)
              + w[0:1] * _shift_down(ext, 2, CONV_HALO))
        gl, _ = _gelu(gh)
        dn = _dot(gl * up_ref[:, D_FF:], wd_ref[...])
        dn_ref[...] = dn
        y, _, _ = _rms_fwd(dn, gpost_ref[...])
        x2_ref[...] = x1_v + y

    row = _rows(ts, D_MODEL)
    return pl.pallas_call(
        body, name="fwd_ffn", grid=(s // ts,),
        in_specs=[row, _resident((1, D_MODEL)), _resident(w_up.shape), _resident(fcw.shape), _resident((1, D_FF)),
                  _resident(w_down.shape), _resident((1, D_MODEL))],
        out_specs=[_rows(ts, 2 * D_FF), row, row, row],
        out_shape=[_sds((s, 2 * D_FF)), _sds((s, D_MODEL), BF16), _sds((s, D_MODEL)), _sds((s, D_MODEL))],
        scratch_shapes=[pltpu.VMEM((CONV_HALO, D_FF), F32)],
        compiler_params=_params(),
    )(x1, g_pre, w_up, fcw, fcb, w_down, g_post)


def _ple_loss(x2, p, tgt, g_gate, w_gate, w_proj, g_post, ts):
    s = x2.shape[0]

    def body(x2_ref, p_ref, t_ref, gg_ref, wg_ref, wp_ref, gp_ref, dx2_ref, loss_ref, dwg_ref, dwp_ref, dgg_ref, dgp_ref):
        @pl.when(pl.program_id(0) == 0)
        def _():
            loss_ref[...] = jnp.zeros_like(loss_ref)
            dwg_ref[...] = jnp.zeros_like(dwg_ref)
            dwp_ref[...] = jnp.zeros_like(dwp_ref)
            dgg_ref[...] = jnp.zeros_like(dgg_ref)
            dgp_ref[...] = jnp.zeros_like(dgp_ref)

        x2_v = x2_ref[...]
        n3, xh3, r3 = _rms_fwd(x2_v, gg_ref[...])
        pg = jax.nn.sigmoid(_dot(n3, wg_ref[...]))
        pb = p_ref[...].astype(BF16)
        q = jnp.concatenate([jnp.dot(pb, wp_ref[j], preferred_element_type=F32) for j in range(N_CHIPS)], axis=1)
        ple, qh, rq = _rms_fwd(q, gp_ref[...])
        e = x2_v + pg * ple - t_ref[...]
        loss_ref[...] += 0.5 * jnp.sum(jnp.mean(e * e, axis=-1, keepdims=True), axis=0, keepdims=True)
        dy = e * (1.0 / D_MODEL)
        dpgl = dy * ple * pg * (1.0 - pg)
        dwg_ref[...] += _dot_tn(n3, dpgl)
        dx3, dgg = _rms_bwd(xh3, r3, gg_ref[...], _dot_nt(dpgl, wg_ref[...]))
        dgg_ref[...] += dgg
        dq, dgp = _rms_bwd(qh, rq, gp_ref[...], dy * pg)
        dgp_ref[...] += dgp
        for j in range(N_CHIPS):
            dwp_ref[j] += _dot_tn(pb, dq[:, j * 256:(j + 1) * 256])
        dx2_ref[...] = dy + dx3

    row = _rows(ts, D_MODEL)
    vec = _acc((1, D_MODEL))
    return pl.pallas_call(
        body, name="ple_loss", grid=(s // ts,),
        in_specs=[row, _rows(ts, PLE_DIM), row, _resident((1, D_MODEL)), _resident(w_gate.shape), _resident(w_proj.shape),
                  _resident((1, D_MODEL))],
        out_specs=[row, _acc((1, 128)), _acc(w_gate.shape), _acc(w_proj.shape), vec, vec],
        out_shape=[_sds((s, D_MODEL)), _sds((1, 128)), _sds(w_gate.shape), _sds(w_proj.shape), _sds((1, D_MODEL)),
                   _sds((1, D_MODEL))],
        compiler_params=_params(),
    )(x2, p, tgt, g_gate, w_gate, w_proj, g_post)


def _bwd_ffn_down(dx2, dn, up, fcw, fcb, w_down, g_post, ts):
    s = dx2.shape[0]
    nt = s // ts
    halo_blocks = ts // CONV_HALO

    def body(dx2_ref, dn_ref, up_ref, uph_ref, fcw_ref, fcb_ref, wd_ref, gpost_ref,
             dup_ref, dwd_ref, dfcw_ref, dfcb_ref, dgp_ref, carry_scr):
        i = pl.program_id(0)
        k = nt - 1 - i

        @pl.when(i == 0)
        def _():
            carry_scr[...] = jnp.zeros_like(carry_scr)
            dwd_ref[...] = jnp.zeros_like(dwd_ref)
            dfcw_ref[...] = jnp.zeros_like(dfcw_ref)
            dfcb_ref[...] = jnp.zeros_like(dfcb_ref)
            dgp_ref[...] = jnp.zeros_like(dgp_ref)

        _, xh, r = _rms_fwd(dn_ref[...], gpost_ref[...])
        ddn, dgp = _rms_bwd(xh, r, gpost_ref[...], dx2_ref[...])
        dgp_ref[...] += dgp
        dhid = _dot_nt(ddn, wd_ref[...])
        ug = up_ref[:, 0:D_FF]
        uv = up_ref[:, D_FF:]
        halo = jnp.where(k > 0, uph_ref[...], 0.0)
        ext = jnp.concatenate([halo, ug], axis=0)
        w = fcw_ref[...]
        gh = (fcb_ref[...] + w[2:3] * ug + w[1:2] * _shift_down(ext, 1, CONV_HALO)
              + w[0:1] * _shift_down(ext, 2, CONV_HALO))
        gl, t = _gelu(gh)
        dwd_ref[...] += _dot_tn(gl * uv, ddn)
        dgh = dhid * uv * _gelu_grad(gh, t)
        dup_ref[:, D_FF:] = (dhid * gl).astype(BF16)
        extd = jnp.concatenate([dgh, carry_scr[...]], axis=0)
        carry_scr[...] = dgh[0:CONV_HALO, :]
        d1 = _shift_up(extd, 1, ts)
        d2 = _shift_up(extd, 2, ts)
        dup_ref[:, 0:D_FF] = (w[2:3] * dgh + w[1:2] * d1 + w[0:1] * d2).astype(BF16)
        dfcw_ref[2:3, :] += jnp.sum(ug * dgh, axis=0, keepdims=True)
        dfcw_ref[1:2, :] += jnp.sum(ug * d1, axis=0, keepdims=True)
        dfcw_ref[0:1, :] += jnp.sum(ug * d2, axis=0, keepdims=True)
        dfcb_ref[...] += jnp.sum(dgh, axis=0, keepdims=True)

    row = _rows(ts, D_MODEL, nt)
    halo_spec = pl.BlockSpec((CONV_HALO, D_FF), lambda i: (jnp.maximum((nt - 1 - i) * halo_blocks - 1, 0), 0))
    return pl.pallas_call(
        body, name="bwd_ffn_down", grid=(nt,),
        in_specs=[row, row, _rows(ts, 2 * D_FF, nt), halo_spec, _resident(fcw.shape), _resident((1, D_FF)),
                  _resident(w_down.shape), _resident((1, D_MODEL))],
        out_specs=[_rows(ts, 2 * D_FF, nt), _acc(w_down.shape), _acc(fcw.shape), _acc((1, D_FF)), _acc((1, D_MODEL))],
        out_shape=[_sds((s, 2 * D_FF), BF16), _sds(w_down.shape), _sds(fcw.shape), _sds((1, D_FF)), _sds((1, D_MODEL))],
        scratch_shapes=[pltpu.VMEM((CONV_HALO, D_FF), F32)],
        compiler_params=_params(),
    )(dx2, dn, up, up, fcw, fcb, w_down, g_post)


def _bwd_ffn_up(dup, x1, dx2, g_pre, w_up, ts, tasks=()):
    s = x1.shape[0]

    def body(dup_ref, x1_ref, dx2_ref, g_ref, wup_ref, dx1_ref, dg_ref):
        @pl.when(pl.program_id(0) == 0)
        def _():
            dg_ref[...] = jnp.zeros_like(dg_ref)

        _, xh, r = _rms_fwd(x1_ref[...], g_ref[...])
        dh2 = _dot_nt(dup_ref[:, 0:UP_SHARD], wup_ref[0])
        for j in range(1, N_CHIPS):
            dh2 = dh2 + _dot_nt(dup_ref[:, j * UP_SHARD:(j + 1) * UP_SHARD], wup_ref[j])
        dx, dg = _rms_bwd(xh, r, g_ref[...], dh2)
        dg_ref[...] += dg
        dx1_ref[...] = dx2_ref[...] + dx

    row = _rows(ts, D_MODEL)
    return _call(
        body, "bwd_ffn_up", (s // ts,),
        [_rows(ts, 2 * D_FF), row, row, _resident((1, D_MODEL)), _resident(w_up.shape)],
        [row, _acc((1, D_MODEL))], [_sds((s, D_MODEL)), _sds((1, D_MODEL))], [],
        (dup, x1, dx2, g_pre, w_up), tasks)


def _dw_up(h2, dup, ts, tasks=()):
    s = h2.shape[0]

    def body(h2_ref, dup_ref, out_ref):
        @pl.when(pl.program_id(1) == 0)
        def _():
            out_ref[...] = jnp.zeros_like(out_ref)

        out_ref[0] += _dot_tn(h2_ref[...], dup_ref[...])

    return _call(
        body, "dw_up", (N_CHIPS, s // ts),
        [pl.BlockSpec((ts, D_MODEL), lambda j, i: (i, 0)), pl.BlockSpec((ts, UP_SHARD), lambda j, i: (i, j))],
        [pl.BlockSpec((1, D_MODEL, UP_SHARD), lambda j, i: (j, 0, 0))], [_sds((N_CHIPS, D_MODEL, UP_SHARD))], [],
        (h2, dup), tasks)


def _bwd_merge(dx1, mo, gp, gr, ypool, yrnn, g_post, w_o, ts, tasks=()):
    s = dx1.shape[0]

    def body(dx1_ref, mo_ref, gp_ref, gr_ref, yp_ref, yr_ref, g_ref, wo_ref,
             dgp_ref, dgr_ref, dyp_ref, dyr_ref, dwo_ref, dg_ref):
        @pl.when(pl.program_id(0) == 0)
        def _():
            dwo_ref[...] = jnp.zeros_like(dwo_ref)
            dg_ref[...] = jnp.zeros_like(dg_ref)

        _, xh, r = _rms_fwd(mo_ref[...], g_ref[...])
        dmo, dg = _rms_bwd(xh, r, g_ref[...], dx1_ref[...])
        dg_ref[...] += dg
        dmerged = _dot_nt(dmo, wo_ref[...])
        sp = jax.nn.sigmoid(gp_ref[...])
        sr = jax.nn.sigmoid(gr_ref[...])
        yp = yp_ref[...]
        yr = yr_ref[...]
        dwo_ref[...] += _dot_tn(sp * yp + sr * yr, dmo)
        dgp_ref[...] = (dmerged * yp * sp * (1.0 - sp)).astype(BF16)
        dgr_ref[...] = (dmerged * yr * sr * (1.0 - sr)).astype(BF16)
        dyp_ref[...] = (dmerged * sp).astype(BF16)
        dyr_ref[...] = (dmerged * sr).astype(BF16)

    row = _rows(ts, D_MODEL)
    return _call(
        body, "bwd_merge", (s // ts,),
        [row] * 6 + [_resident((1, D_MODEL)), _resident(w_o.shape)],
        [row] * 4 + [_acc(w_o.shape), _acc((1, D_MODEL))],
        [_sds((s, D_MODEL), BF16)] * 4 + [_sds(w_o.shape), _sds((1, D_MODEL))], [],
        (dx1, mo, gp, gr, ypool, yrnn, g_post, w_o), tasks)


def _bwd_rnn(dyr, urx, urg, xc, r, ig, h, conv_w, wg, lam, w_rg_out, ts, tasks=()):
    s = urx.shape[0]
    nt = s // ts
    halo_blocks = ts // CONV_HALO

    def body(dyr_ref, urx_ref, urg_ref, xc_ref, r_ref, ig_ref, h_ref, hh_ref, cw_ref, wg_ref, lam_ref, wrg_ref,
             durx_ref, durg_ref, dwrg_ref, dwg_ref, dcw_ref, dcb_ref, dbg_ref, dlam_ref, mu_scr, carry_scr):
        i = pl.program_id(0)
        k = nt - 1 - i

        @pl.when(i == 0)
        def _():
            mu_scr[...] = jnp.zeros_like(mu_scr)
            carry_scr[...] = jnp.zeros_like(carry_scr)
            dwrg_ref[...] = jnp.zeros_like(dwrg_ref)
            dwg_ref[...] = jnp.zeros_like(dwg_ref)
            dcw_ref[...] = jnp.zeros_like(dcw_ref)
            dcb_ref[...] = jnp.zeros_like(dcb_ref)
            dbg_ref[...] = jnp.zeros_like(dbg_ref)
            dlam_ref[...] = jnp.zeros_like(dlam_ref)

        row = lax.broadcasted_iota(jnp.int32, (ts, 1), 0)
        first_row = (k * ts + row) == 0
        h = h_ref[...]
        urg_v = urg_ref[...]
        dyr_v = dyr_ref[...]
        dhr = _dot_nt(dyr_v, wrg_ref[...])
        gl, t = _gelu(urg_v)
        dwrg_ref[...] += _dot_tn(h * gl, dyr_v)
        durg_ref[...] = (dhr * h * _gelu_grad(urg_v, t)).astype(BF16)
        r_v = r_ref[...]
        ig_v = ig_ref[...]
        xc_v = xc_ref[...]
        lam_v = lam_ref[...]
        c8, a, m2, mult = _lru_coeffs(r_v, lam_v, first_row)
        b = jnp.where(row == ts - 1, 1.0, pltpu.roll(a, ts - 1, 0))
        lt = _scan_bwd(b, dhr * gl, mu_scr[0:1, :])
        mu_scr[0:1, :] = a[0:1, :] * lt[0:1, :]
        h_before = jnp.where(k > 0, hh_ref[CONV_HALO - 1:CONV_HALO, :], 0.0)
        hprev = jnp.where(row == 0, h_before, pltpu.roll(h, 1, 0))
        dmult = lt * ig_v * xc_v
        da = lt * hprev - jnp.where(first_row, 0.0, dmult * a * lax.rsqrt(m2))
        dla = da * a
        dlam_ref[...] += jnp.sum(dla * r_v, axis=0, keepdims=True)
        dlr = (dla * (-c8)) * r_v * (1.0 - r_v)
        dli = (lt * mult * xc_v) * ig_v * (1.0 - ig_v)
        dbg_ref[0:1, :] += jnp.sum(dlr, axis=0, keepdims=True)
        dbg_ref[1:2, :] += jnp.sum(dli, axis=0, keepdims=True)
        xcb = xc_v.astype(BF16)
        parts = []
        for q in range(GATE_BLOCKS):
            blk = slice(q * GATE_BLOCK, (q + 1) * GATE_BLOCK)
            dlr_q = dlr[:, blk].astype(BF16)
            dli_q = dli[:, blk].astype(BF16)
            parts.append(_dot_nt(dlr_q, wg_ref[0, q]) + _dot_nt(dli_q, wg_ref[1, q]))
            dwg_ref[0, q] += _dot_tn(xcb[:, blk], dlr_q)
            dwg_ref[1, q] += _dot_tn(xcb[:, blk], dli_q)
        dxc = lt * mult * ig_v + jnp.concatenate(parts, axis=1)
        extd = jnp.concatenate([dxc, carry_scr[...]], axis=0)
        carry_scr[...] = dxc[0:CONV_HALO, :]
        cw = cw_ref[...]
        urx_v = urx_ref[...]
        durx = cw[3:4] * dxc
        dcw_ref[3:4, :] += jnp.sum(urx_v * dxc, axis=0, keepdims=True)
        for j in (1, 2, 3):
            dj = _shift_up(extd, j, ts)
            durx = durx + cw[3 - j:4 - j] * dj
            dcw_ref[3 - j:4 - j, :] += jnp.sum(urx_v * dj, axis=0, keepdims=True)
        durx_ref[...] = durx.astype(BF16)
        dcb_ref[...] += jnp.sum(dxc, axis=0, keepdims=True)

        @pl.when(i == nt - 1)
        def _():
            dlam_ref[...] = dlam_ref[...] * (LRU_C * jax.nn.sigmoid(-lam_v))

    row_spec = _rows(ts, D_MODEL, nt)
    halo_spec = pl.BlockSpec((CONV_HALO, D_MODEL), lambda i: (jnp.maximum((nt - 1 - i) * halo_blocks - 1, 0), 0))
    vec = _acc((1, D_MODEL))
    return _call(
        body, "bwd_rnn", (nt,),
        [row_spec] * 7 + [halo_spec, _resident(conv_w.shape), _resident(wg.shape), _resident((1, D_MODEL)),
                          _resident(w_rg_out.shape)],
        [row_spec, row_spec, _acc(w_rg_out.shape), _acc(wg.shape), _acc(conv_w.shape), vec, _acc((2, D_MODEL)), vec],
        [_sds((s, D_MODEL), BF16), _sds((s, D_MODEL), BF16), _sds(w_rg_out.shape), _sds(wg.shape), _sds(conv_w.shape),
         _sds((1, D_MODEL)), _sds((2, D_MODEL)), _sds((1, D_MODEL))],
        [pltpu.VMEM((8, D_MODEL), F32), pltpu.VMEM((CONV_HALO, D_MODEL), F32)],
        (dyr, urx, urg, xc, r, ig, h, h, conv_w, wg, lam, w_rg_out), tasks)


def _bwd_pool_in(dyp, d, durx, durg, dgp, dgr, x, dx1, g_pre, w_in, pool_w, pool_scale, w_pool_out, ts, tasks=()):
    s = x.shape[0]
    nt = s // ts

    def body(dyp_ref, d_ref, durx_ref, durg_ref, dgp_ref, dgr_ref, x_ref, dx1_ref, g_ref, win_ref, pw_ref, ps_ref,
             wpo_ref, gx_ref, dwin_ref, dwpo_ref, dpw_ref, dps_ref, dg_ref, dz_scr, carry_scr):
        i = pl.program_id(0)
        k = nt - 1 - i

        @pl.when(i == 0)
        def _():
            carry_scr[...] = jnp.zeros_like(carry_scr)
            dwin_ref[...] = jnp.zeros_like(dwin_ref)
            dwpo_ref[...] = jnp.zeros_like(dwpo_ref)
            dpw_ref[...] = jnp.zeros_like(dpw_ref)
            dps_ref[...] = jnp.zeros_like(dps_ref)
            dg_ref[...] = jnp.zeros_like(dg_ref)

        dyp_v = dyp_ref[...]
        d_v = d_ref[...]
        ps = ps_ref[...]
        dypre = _dot_nt(dyp_v[:, 0:256], wpo_ref[0])
        for j in range(1, N_CHIPS):
            dypre = dypre + _dot_nt(dyp_v[:, j * 256:(j + 1) * 256], wpo_ref[j])
        y4 = jnp.concatenate([_dot(d_v[:, g * 128:(g + 1) * 128], pw_ref[g]) for g in range(POOL_GROUPS)], axis=1)
        ypre = (y4 * ps).astype(BF16)
        for j in range(N_CHIPS):
            dwpo_ref[j] += _dot_tn(ypre, dyp_v[:, j * 256:(j + 1) * 256])
        dps_ref[...] += jnp.sum(dypre * y4, axis=0, keepdims=True)
        dy4 = dypre * ps
        t = k * ts + lax.broadcasted_iota(jnp.int32, (ts, 1), 0)
        for g, w in enumerate(POOL_WINDOWS):
            lanes = slice(g * POOL_GROUP_DIM, (g + 1) * POOL_GROUP_DIM)
            dd = _dot_nt(dy4[:, lanes], pw_ref[g])
            dpw_ref[g] += _dot_tn(d_v[:, lanes], dy4[:, lanes])
            e = dd * (1.0 / jnp.minimum(t + 1, w).astype(F32))
            acc = jnp.concatenate([e, carry_scr[:, lanes]], axis=0)
            carry_scr[:, lanes] = e[0:POOL_HALO, :]
            n = ts + POOL_HALO
            sh = 1
            while sh < w:
                acc = acc + pltpu.roll(acc, n - sh, 0)
                sh *= 2
            dz_scr[:, lanes] = (acc[:ts, :] - dd).astype(BF16)
        dz_scr[:, 512:1536] = durx_ref[...]
        dz_scr[:, 1536:2560] = durg_ref[...]
        dz_scr[:, 2560:3584] = dgp_ref[...]
        dz_scr[:, 3584:4608] = dgr_ref[...]
        h1, xh, r = _rms_fwd(x_ref[...], g_ref[...])
        h1 = h1.astype(BF16)
        dh1 = _dot_nt(dz_scr[:, 0:IN_SHARD], win_ref[0])
        dwin_ref[0] += _dot_tn(h1, dz_scr[:, 0:IN_SHARD])
        for j in range(1, N_CHIPS):
            cols = slice(j * IN_SHARD, (j + 1) * IN_SHARD)
            dh1 = dh1 + _dot_nt(dz_scr[:, cols], win_ref[j])
            dwin_ref[j] += _dot_tn(h1, dz_scr[:, cols])
        dx, dg = _rms_bwd(xh, r, g_ref[...], dh1)
        dg_ref[...] += dg
        gx_ref[...] = dx1_ref[...] + dx

    row = _rows(ts, D_MODEL, nt)
    return _call(
        body, "bwd_pool_in", (nt,),
        [row, _rows(ts, POOL_WIDTH, nt)] + [row] * 6 + [_resident((1, D_MODEL)), _resident(w_in.shape),
                                                       _resident(pool_w.shape), _resident((1, POOL_WIDTH)),
                                                       _resident(w_pool_out.shape)],
        [row, _acc(w_in.shape), _acc(w_pool_out.shape), _acc(pool_w.shape), _acc((1, POOL_WIDTH)), _acc((1, D_MODEL))],
        [_sds((s, D_MODEL)), _sds(w_in.shape), _sds(w_pool_out.shape), _sds(pool_w.shape), _sds((1, POOL_WIDTH)),
         _sds((1, D_MODEL))],
        [pltpu.VMEM((ts, IN_TOTAL), BF16), pltpu.VMEM((POOL_HALO, POOL_WIDTH), F32)],
        (dyp, d, durx, durg, dgp, dgr, x, dx1, g_pre, w_in, pool_w, pool_scale, w_pool_out), tasks)


def _place():
    x, y, c = lax.axis_index("x"), lax.axis_index("y"), lax.axis_index("c")
    others = [(1 - x, y), (x, 1 - y), (1 - x, 1 - y)]
    return x, y, c, 2 * x + y, others


def _remote(src, dst, send_sem, recv_sem, to):
    return pltpu.make_async_remote_copy(src_ref=src, dst_ref=dst, send_sem=send_sem, recv_sem=recv_sem,
                                        device_id=to, device_id_type=MESH)


def _own_slot(w, place, dtype):
    a, b = w.shape
    tr = _row_tile(a, b)

    def body(place_ref, w_ref, out_ref):
        out_ref[0] = w_ref[...].astype(dtype)

    return pl.pallas_call(
        body, name="own_slot",
        grid_spec=pltpu.PrefetchScalarGridSpec(
            num_scalar_prefetch=1, grid=(a // tr,),
            in_specs=[pl.BlockSpec((tr, b), lambda i, s: (i, 0))],
            out_specs=pl.BlockSpec((1, tr, b), lambda i, s: (s[0], i, 0))),
        out_shape=_sds((N_CHIPS, a, b), dtype),
        compiler_params=pltpu.CompilerParams(dimension_semantics=("arbitrary",)),
    )(place, w)


def _run(task, name):
    return _call(None, name, (), [], [], [], [], (), (task,))[1][0]


def _gather_task(bufs):
    n = len(bufs)

    def first_copy(out, w, j, ox, oy, sems):
        x, y, c, me, _ = _place()
        ah = out.shape[1] // 2
        mine = out.at[me, pl.ds(c * ah, ah)]
        return _remote(mine, mine, sems[0].at[w, j], sems[1].at[w, j], (ox, oy, c))

    def passed_copy(out, w, j, ox, oy, sems, half):
        x, y, c, _, _ = _place()
        ah = out.shape[1] // 2
        slot = out.at[2 * ox + oy, pl.ds(half * ah, ah)]
        return _remote(slot, slot, sems[0].at[w, 3 + j], sems[1].at[w, 3 + j], (x, y, 1 - c))

    def start(ins, outs, sems):
        others = _place()[4]
        for w, out in enumerate(outs):
            for j, (ox, oy) in enumerate(others):
                first_copy(out, w, j, ox, oy, sems).start()

    def finish(ins, outs, sems):
        x, y, c, _, others = _place()
        for w, out in enumerate(outs):
            ah = out.shape[1] // 2
            for j, (ox, oy) in enumerate(others):
                slot = out.at[2 * ox + oy, pl.ds(c * ah, ah)]
                _remote(slot, slot, sems[0].at[w, j], sems[1].at[w, j], (ox, oy, c)).wait_recv()
                passed_copy(out, w, j, ox, oy, sems, c).start()
        for w, out in enumerate(outs):
            for j, (ox, oy) in enumerate(others):
                passed_copy(out, w, j, ox, oy, sems, 1 - c).wait_recv()
        for w, out in enumerate(outs):
            for j, (ox, oy) in enumerate(others):
                first_copy(out, w, j, ox, oy, sems).wait_send()
                passed_copy(out, w, j, ox, oy, sems, c).wait_send()

    return _Task(bufs, [_sds(b.shape, b.dtype) for b in bufs], {i: i for i in range(n)},
                 [pltpu.SemaphoreType.DMA((n, 6)), pltpu.SemaphoreType.DMA((n, 6))], start, finish)


def _halves_task(grads):
    n = len(grads)

    def copy(src, out, w, sems):
        x, y, c, _, _ = _place()
        ah = out.shape[1]
        return _remote(src.at[:, pl.ds((1 - c) * ah, ah)], out, sems[0].at[w], sems[1].at[w], (x, y, 1 - c))

    def start(ins, outs, sems):
        for w, (src, out) in enumerate(zip(ins, outs)):
            copy(src, out, w, sems).start()

    def finish(ins, outs, sems):
        for w, (src, out) in enumerate(zip(ins, outs)):
            copy(src, out, w, sems).wait()

    return _Task(grads, [_sds((g.shape[0], g.shape[1] // 2, g.shape[2]), g.dtype) for g in grads], {},
                 [pltpu.SemaphoreType.DMA((n,)), pltpu.SemaphoreType.DMA((n,))], start, finish)


def _exchange_task(sends, accs):
    n = len(accs)
    given = [s for s in sends if s is not None]

    def copies(ins, outs, sems):
        send_refs = iter(ins[:len(given)])
        srcs = [next(send_refs) if s is not None else None for s in sends]
        x, y, c, me, others = _place()
        for w, out in enumerate(outs):
            for j, (ox, oy) in enumerate(others):
                src = out.at[me] if srcs[w] is None else srcs[w].at[2 * ox + oy]
                yield _remote(src, out.at[me], sems[0].at[w, j], sems[1].at[w, j], (ox, oy, c))

    def start(ins, outs, sems):
        for cp in copies(ins, outs, sems):
            cp.start()

    def finish(ins, outs, sems):
        x, y, c, _, others = _place()
        for w, out in enumerate(outs):
            for j, (ox, oy) in enumerate(others):
                slot = out.at[2 * ox + oy]
                _remote(slot, slot, sems[0].at[w, j], sems[1].at[w, j], (ox, oy, c)).wait_recv()
        for cp in copies(ins, outs, sems):
            cp.wait_send()

    return _Task(given + list(accs), [_sds(a.shape, a.dtype) for a in accs], {len(given) + i: i for i in range(n)},
                 [pltpu.SemaphoreType.DMA((n, 3)), pltpu.SemaphoreType.DMA((n, 3))], start, finish)


def _share_task(shares):
    n = len(shares)

    def copy(out, w, sems, slot):
        x, y, c, _, _ = _place()
        return _remote(out.at[slot], out.at[slot], sems[0].at[w], sems[1].at[w], (x, y, 1 - c))

    def start(ins, outs, sems):
        c = _place()[2]
        for w, out in enumerate(outs):
            copy(out, w, sems, c).start()

    def finish(ins, outs, sems):
        c = _place()[2]
        for w, out in enumerate(outs):
            copy(out, w, sems, 1 - c).wait_recv()
        for w, out in enumerate(outs):
            copy(out, w, sems, c).wait_send()

    return _Task(shares, [_sds(s.shape, s.dtype) for s in shares], {i: i for i in range(n)},
                 [pltpu.SemaphoreType.DMA((n,)), pltpu.SemaphoreType.DMA((n,))], start, finish)


TILE_BYTES = 2 * 1024 * 1024
PARTIAL_TILE_BYTES = 512 * 1024


def _row_tile(rows, cols, limit=TILE_BYTES):
    best = 8
    for tr in range(8, rows + 1, 8):
        if rows % tr == 0 and tr * cols * 4 <= limit:
            best = tr
    assert rows % best == 0, (rows, cols)
    return best


def _chip_partial(g, got, place, wire_dtype):
    ns, ah, b = got.shape
    sharded = ns == N_CHIPS
    tr = _row_tile(ah, b, PARTIAL_TILE_BYTES)
    nb = ah // tr

    def body(place_ref, *refs):
        g_refs, got_refs, outs = refs[:ns], refs[ns:2 * ns], refs[2 * ns:]
        parts = [g_refs[k][0] + got_refs[k][0] for k in range(ns)]
        own = parts[0]
        if sharded:
            for k in range(ns):
                outs[0][k] = parts[k].astype(wire_dtype)
                if k:
                    own = jnp.where(place_ref[0] == k, parts[k], own)
        outs[-1][0] = own.astype(wire_dtype)

    blk = (1, tr, b)
    in_specs = ([pl.BlockSpec(blk, lambda i, s, k=k: (k, s[1] * nb + i, 0)) for k in range(ns)]
                + [pl.BlockSpec(blk, lambda i, s, k=k: (k, i, 0)) for k in range(ns)])
    acc_spec = pl.BlockSpec(blk, lambda i, s: (s[0], i, 0))
    acc_shape = _sds((N_CHIPS, ah, b), wire_dtype)
    out = pl.pallas_call(
        body, name="grad_chip_partial",
        grid_spec=pltpu.PrefetchScalarGridSpec(
            num_scalar_prefetch=1, grid=(nb,), in_specs=in_specs,
            out_specs=[pl.BlockSpec((ns, tr, b), lambda i, s: (0, i, 0)), acc_spec] if sharded else [acc_spec]),
        out_shape=[acc_shape, acc_shape] if sharded else [acc_shape],
        compiler_params=pltpu.CompilerParams(dimension_semantics=("arbitrary",), vmem_limit_bytes=32 * 1024 * 1024),
    )(place, *([g] * ns), *([got] * ns))
    return (out[0], out[1]) if sharded else (None, out[0])


def _chip_sum(acc, place):
    _, ah, b = acc.shape
    tr = _row_tile(ah, b)

    def body(place_ref, p_ref, out_ref):
        total = p_ref[0].astype(F32) + p_ref[1].astype(F32)
        total = total + p_ref[2].astype(F32)
        out_ref[0] = total + p_ref[3].astype(F32)

    return pl.pallas_call(
        body, name="grad_chip_sum",
        grid_spec=pltpu.PrefetchScalarGridSpec(
            num_scalar_prefetch=1, grid=(ah // tr,),
            in_specs=[pl.BlockSpec((N_CHIPS, tr, b), lambda i, s: (0, i, 0))],
            out_specs=pl.BlockSpec((1, tr, b), lambda i, s: (s[1], i, 0))),
        out_shape=_sds((2, ah, b)),
        compiler_params=pltpu.CompilerParams(dimension_semantics=("arbitrary",)),
    )(place, acc)


def _adamw(w, g, m, v):
    a, b = w.shape
    tr = _row_tile(a, b)

    def body(w_ref, g_ref, m_ref, v_ref, d_ref, nm_ref, nv_ref):
        g_v = g_ref[...]
        nm = ADAM_B1 * m_ref[...] + (1.0 - ADAM_B1) * g_v
        nv = ADAM_B2 * v_ref[...] + (1.0 - ADAM_B2) * (g_v * g_v)
        m_hat = nm / (1.0 - ADAM_B1 ** ADAM_STEP)
        v_hat = nv / (1.0 - ADAM_B2 ** ADAM_STEP)
        d_ref[...] = -ADAM_LR * (m_hat / (jnp.sqrt(v_hat) + ADAM_EPS) + ADAM_WD * w_ref[...])
        nm_ref[...] = nm
        nv_ref[...] = nv

    blk = pl.BlockSpec((tr, b), lambda i: (i, 0))
    return pl.pallas_call(
        body, name="adamw", grid=(a // tr,),
        in_specs=[blk] * 4, out_specs=[blk] * 3, out_shape=[_sds((a, b))] * 3,
        compiler_params=pltpu.CompilerParams(dimension_semantics=("arbitrary",)),
    )(w, g, m, v)


TINY_ROWS, TINY_COLS = 16, 768
SMALL_COLS = 128
SMALL_ROWS = 1632


def _pack_tiny(conv_w, b_gates, fcw):
    ns = conv_w.shape[0]
    pad = lambda t: jnp.pad(t, ((0, 0), (0, 0), (0, TINY_COLS - t.shape[2])))
    z = lambda rows: jnp.zeros((ns, rows, TINY_COLS), F32)
    return jnp.concatenate([pad(conv_w), pad(b_gates), z(2), fcw, z(5)], axis=1)


def _unpack_tiny(t):
    return t[:, 0:4, 0:256], t[:, 4:6, 0:256], t[:, 8:11, :]


def _cols_to_shards(t, n):
    return t.reshape(t.shape[0], N_CHIPS, n).transpose(1, 0, 2)


def _shards_to_cols(t):
    return t.transpose(1, 0, 2).reshape(t.shape[1], -1)


_SMALL_ORDER = ("g_mix_pre", "g_mix_post", "conv_b", "lru_lambda", "g_ffn_pre", "g_ffn_post", "g_ple_gate", "g_ple_post",
                "pool_scale", "ffn_conv_b", "pool_w", "w_rg_gates")


def _pack_small(parts):
    flat = jnp.concatenate([parts[k].reshape(-1).astype(F32) for k in _SMALL_ORDER])
    flat = jnp.pad(flat, (0, SMALL_ROWS * SMALL_COLS - flat.shape[0]))
    return flat.reshape(SMALL_ROWS, SMALL_COLS)


def _unpack_small(packed, shapes):
    flat = packed.reshape(-1)
    out, off = {}, 0
    for k in _SMALL_ORDER:
        size = 1
        for dim in shapes[k]:
            size *= dim
        out[k] = flat[off:off + size].reshape(shapes[k])
        off += size
    return out


def _gates_block_diag(w):
    w4 = w.reshape(2, GATE_BLOCKS, 4, RNN_HEAD_DIM, RNN_HEAD_DIM)
    eye = jnp.eye(4, dtype=w.dtype)
    return jnp.einsum("gqhij,hk->gqhikj", w4, eye).reshape(2, GATE_BLOCKS, GATE_BLOCK, GATE_BLOCK)


def _gates_from_block_diag(dw):
    d6 = dw.reshape(2, GATE_BLOCKS, 4, RNN_HEAD_DIM, 4, RNN_HEAD_DIM)
    blocks = [d6[:, :, hh, :, hh, :] for hh in range(4)]
    return jnp.stack(blocks, axis=2).reshape(2, RNN_HEADS, RNN_HEAD_DIM, RNN_HEAD_DIM)


ROW_TILE = 256

_SHARDED = ("w_in", "w_pool_out", "w_rg_out", "w_o", "w_up", "w_down", "w_ple_gate", "w_ple_proj")
_WEIGHTS = ("g_mix_pre", "g_mix_post", "w_in", "pool_w", "pool_scale", "w_pool_out", "conv_w", "conv_b", "w_rg_gates",
            "b_rg_gates", "lru_lambda", "w_rg_out", "w_o", "g_ffn_pre", "g_ffn_post", "w_up", "ffn_conv_w", "ffn_conv_b",
            "w_down", "g_ple_gate", "w_ple_gate", "w_ple_proj", "g_ple_post")


def _wire_dtype(g):
    return BF16 if g.shape[1] >= 64 and g.shape[2] > SMALL_COLS else F32


def _partials(grads, got, place):
    parts = [_chip_partial(g, r, place, _wire_dtype(g)) for g, r in zip(grads, got)]
    return [send for send, _ in parts], [acc for _, acc in parts]


def _whole(both):
    return [b.reshape(2 * b.shape[1], b.shape[2]) for b in both]


def _step(x, p, tgt, own, rep, place, ts):
    vec = lambda k: rep[k].reshape(1, -1)
    pool_w = rep["pool_w"].astype(BF16)
    wg = _gates_block_diag(rep["w_rg_gates"]).astype(BF16)
    sq = lambda t: t.reshape(D_MODEL, D_MODEL)
    by4 = lambda t: t.reshape(N_CHIPS, -1, D_MODEL)

    first, ride1, ride2 = ("w_in", "w_pool_out", "tiny"), ("w_rg_out", "w_o", "w_down", "w_ple_gate", "w_ple_proj"), ("w_up",)
    full = dict(zip(first, _run(_gather_task([own[k] for k in first]), "gather_first")))
    conv_w, b_gates, fcw = [_shards_to_cols(t) for t in _unpack_tiny(full["tiny"])]

    (urx, urg, gp, gr, d, ypool), (got,) = _fwd_in_pool(
        x, vec("g_mix_pre"), full["w_in"], pool_w, vec("pool_scale"), full["w_pool_out"], ts,
        [_gather_task([own[k] for k in ride1])])
    full.update(zip(ride1, got))
    w_rg_out, w_o, w_gate = sq(full["w_rg_out"]), sq(full["w_o"]), sq(full["w_ple_gate"])
    w_down = full["w_down"].reshape(D_FF, D_MODEL)
    (xc, r, ig, h, yrnn, mo, x1), (got,) = _fwd_rnn_merge(
        urx, urg, gp, gr, ypool, x, conv_w, vec("conv_b"), wg, b_gates, vec("lru_lambda"), w_rg_out, w_o,
        vec("g_mix_post"), ts, [_gather_task([own[k] for k in ride2])])
    full.update(zip(ride2, got))
    up, h2, dn, x2 = _fwd_ffn(x1, vec("g_ffn_pre"), full["w_up"], fcw, vec("ffn_conv_b"), w_down, vec("g_ffn_post"), ts)
    dx2, loss, d_w_gate, d_w_proj, d_g_ple_gate, d_g_ple_post = _ple_loss(
        x2, p, tgt, vec("g_ple_gate"), w_gate, full["w_ple_proj"], vec("g_ple_post"), ts)
    dup, d_w_down, d_fcw, d_fcb, d_g_ffn_post = _bwd_ffn_down(dx2, dn, up, fcw, vec("ffn_conv_b"), w_down,
                                                               vec("g_ffn_post"), ts)

    names1, grads1 = ("w_ple_gate", "w_ple_proj", "w_down"), [by4(d_w_gate), d_w_proj, by4(d_w_down)]
    (dx1, d_g_ffn_pre), (got1,) = _bwd_ffn_up(dup, x1, dx2, vec("g_ffn_pre"), full["w_up"], ts, [_halves_task(grads1)])
    (d_w_up,), (accs1,) = _dw_up(h2, dup, ts, [_exchange_task(*_partials(grads1, got1, place))])
    shares1 = [_chip_sum(acc, place) for acc in accs1]
    (dgp, dgr, dyp, dyr, d_w_o, d_g_mix_post), (got2, both1) = _bwd_merge(
        dx1, mo, gp, gr, ypool, yrnn, vec("g_mix_post"), w_o, ts, [_halves_task([d_w_up]), _share_task(shares1)])
    (durx, durg, d_w_rg_out, d_wg, d_conv_w, d_conv_b, d_b_gates, d_lam), (accs2,) = _bwd_rnn(
        dyr, urx, urg, xc, r, ig, h, conv_w, wg, vec("lru_lambda"), w_rg_out, ts,
        [_exchange_task(*_partials([d_w_up], got2, place))])
    shares2 = [_chip_sum(acc, place) for acc in accs2]
    (grad_x, d_w_in, d_w_pool_out, d_pool_w, d_pool_scale, d_g_mix_pre), (both2,) = _bwd_pool_in(
        dyp, d, durx, durg, dgp, dgr, x, dx1, vec("g_mix_pre"), full["w_in"], pool_w, vec("pool_scale"),
        full["w_pool_out"], ts, [_share_task(shares2)])

    replicated = {"g_mix_pre": d_g_mix_pre, "g_mix_post": d_g_mix_post, "conv_b": d_conv_b, "lru_lambda": d_lam,
                  "g_ffn_pre": d_g_ffn_pre, "g_ffn_post": d_g_ffn_post, "g_ple_gate": d_g_ple_gate,
                  "g_ple_post": d_g_ple_post, "pool_scale": d_pool_scale, "ffn_conv_b": d_fcb, "pool_w": d_pool_w,
                  "w_rg_gates": _gates_from_block_diag(d_wg)}
    names3 = ("w_in", "w_pool_out", "w_rg_out", "w_o", "tiny", "small")
    grads3 = [d_w_in, d_w_pool_out, by4(d_w_rg_out), by4(d_w_o),
              _pack_tiny(_cols_to_shards(d_conv_w, 256), _cols_to_shards(d_b_gates, 256), _cols_to_shards(d_fcw, 768)),
              _pack_small(replicated)[None]]
    got3 = _run(_halves_task(grads3), "grad_sibling_halves")
    accs3 = _run(_exchange_task(*_partials(grads3, got3, place)), "grad_chip_exchange")
    both3 = _run(_share_task([_chip_sum(acc, place) for acc in accs3]), "grad_sibling_share")

    reduced = dict(zip(names1 + ("w_up",) + names3, _whole(both1) + _whole(both2) + _whole(both3)))
    return loss, grad_x, reduced


def kernel(x, p, g_mix_pre, g_mix_post, w_in, pool_w, pool_scale, w_pool_out, conv_w, conv_b, w_rg_gates, b_rg_gates, lru_lambda, w_rg_out, w_o, g_ffn_pre, g_ffn_post, w_up, ffn_conv_w, ffn_conv_b, w_down, g_ple_gate, w_ple_gate, w_ple_proj, g_ple_post, loss_target, m_g_mix_pre, m_g_mix_post, m_w_in, m_pool_w, m_pool_scale, m_w_pool_out, m_conv_w, m_conv_b, m_w_rg_gates, m_b_rg_gates, m_lru_lambda, m_w_rg_out, m_w_o, m_g_ffn_pre, m_g_ffn_post, m_w_up, m_ffn_conv_w, m_ffn_conv_b, m_w_down, m_g_ple_gate, m_w_ple_gate, m_w_ple_proj, m_g_ple_post, v_g_mix_pre, v_g_mix_post, v_w_in, v_pool_w, v_pool_scale, v_w_pool_out, v_conv_w, v_conv_b, v_w_rg_gates, v_b_rg_gates, v_lru_lambda, v_w_rg_out, v_w_o, v_g_ffn_pre, v_g_ffn_post, v_w_up, v_ffn_conv_w, v_ffn_conv_b, v_w_down, v_g_ple_gate, v_w_ple_gate, v_w_ple_proj, v_g_ple_post):
    args = dict(locals())
    w = {k: args[k][0] for k in _WEIGHTS}
    m = {k: args["m_" + k][0] for k in _WEIGHTS}
    v = {k: args["v_" + k][0] for k in _WEIGHTS}
    small_shapes = {k: w[k].shape for k in _SMALL_ORDER}
    tiny = lambda t: _pack_tiny(t["conv_w"][None], t["b_rg_gates"][None], t["ffn_conv_w"][None])[0]

    place = jnp.stack([2 * lax.axis_index("x") + lax.axis_index("y"), lax.axis_index("c")]).astype(jnp.int32)
    own = {k: _own_slot(w[k], place, BF16) for k in _SHARDED}
    own["tiny"] = _own_slot(tiny(w), place, F32)
    loss, grad_x, reduced = _step(x[0], p[0, 0], loss_target[0], own, w, place, ROW_TILE)

    params = {k: (w[k], m[k], v[k]) for k in _SHARDED}
    params["tiny"] = (tiny(w), tiny(m), tiny(v))
    params["small"] = (_pack_small(w), _pack_small(m), _pack_small(v))
    out = {"grad": {}, "delta": {}, "new_m": {}, "new_v": {}}
    kinds = ("grad", "delta", "new_m", "new_v")
    for k, (pw, pm, pv) in params.items():
        for kind, val in zip(kinds, [reduced[k]] + list(_adamw(pw, reduced[k], pm, pv))):
            if k == "tiny":
                cw, bg, fcw = _unpack_tiny(val[None])
                out[kind].update(conv_w=cw[0], b_rg_gates=bg[0], ffn_conv_w=fcw[0])
            elif k == "small":
                out[kind].update(_unpack_small(val, small_shapes))
            else:
                out[kind][k] = val

    total = lax.psum(loss[0, 0], MESH_AXES)
    result = [total, grad_x[None]]
    for kind in kinds:
        result += [out[kind][k][None] for k in _WEIGHTS]
    return tuple(result)
```

```python
import functools

import jax
import jax.numpy as jnp
from jax import lax
from jax.experimental import pallas as pl
from jax.experimental.pallas import tpu as pltpu

F32 = jnp.float32
BF16 = jnp.bfloat16

D_MODEL = 1024
POOL_WINDOWS = (2, 4, 8, 16)
POOL_GROUPS = 4
POOL_WIDTH = 512
POOL_GROUP_DIM = 128
RNN_HEADS = 16
RNN_HEAD_DIM = 64
GATE_BLOCK = 256
GATE_BLOCKS = D_MODEL // GATE_BLOCK
LRU_C = 8.0
D_FF = 3072
PLE_DIM = 256
RMS_EPS = 1e-6
IN_TOTAL = 4608
N_CHIPS = 4
IN_SHARD = IN_TOTAL // N_CHIPS
UP_SHARD = 2 * D_FF // N_CHIPS
POOL_HALO = 16
CONV_HALO = 8

ADAM_LR = 0.001
ADAM_B1 = 0.9
ADAM_B2 = 0.999
ADAM_EPS = 1e-08
ADAM_WD = 0.01
ADAM_STEP = 10

VMEM_LIMIT = 56 * 1024 * 1024
MESH = pl.DeviceIdType.MESH

_GELU_C = 0.7978845608028654
_GELU_A = 0.044715


def _dot(a, b):
    return jnp.dot(a.astype(BF16), b.astype(BF16), preferred_element_type=F32)


def _dot_nt(a, b):
    return lax.dot_general(a.astype(BF16), b.astype(BF16), (((1,), (1,)), ((), ())), preferred_element_type=F32)


def _dot_tn(a, b):
    return lax.dot_general(a.astype(BF16), b.astype(BF16), (((0,), (0,)), ((), ())), preferred_element_type=F32)


def _rms_fwd(x, g):
    r = lax.rsqrt(jnp.mean(x * x, axis=-1, keepdims=True) + RMS_EPS)
    xh = x * r
    return xh * g, xh, r


def _rms_bwd(xh, r, g, dy):
    dxh = dy * g
    dg = jnp.sum(dy * xh, axis=0, keepdims=True)
    dx = r * (dxh - xh * jnp.mean(dxh * xh, axis=-1, keepdims=True))
    return dx, dg


def _gelu(x):
    t = jnp.tanh(_GELU_C * (x + _GELU_A * x * x * x))
    return 0.5 * x * (1.0 + t), t


def _gelu_grad(x, t):
    return 0.5 * (1.0 + t) + 0.5 * x * (1.0 - t * t) * _GELU_C * (1.0 + 3.0 * _GELU_A * x * x)


def _softplus_neg(lam):
    nl = -lam
    return jnp.maximum(nl, 0.0) + jnp.log(1.0 + jnp.exp(-jnp.abs(nl)))


def _lru_coeffs(r, lam, first_row):
    c8 = LRU_C * _softplus_neg(lam)
    la = -(c8 * r)
    a = jnp.exp(la)
    m2 = jnp.tanh(-la) * (1.0 + a * a)
    mult = jnp.where(first_row, 1.0, jnp.sqrt(m2))
    return c8, a, m2, mult


def _scan_fwd(a, u, carry):
    n = a.shape[0]
    row = lax.broadcasted_iota(jnp.int32, (n, 1), 0)
    acc_a, acc_h = a, u
    s = 1
    while s < n:
        m = row >= s
        h_s = jnp.where(m, pltpu.roll(acc_h, s, 0), 0.0)
        a_s = jnp.where(m, pltpu.roll(acc_a, s, 0), 1.0)
        acc_h = acc_a * h_s + acc_h
        acc_a = acc_a * a_s
        s *= 2
    return acc_h + acc_a * carry


def _scan_bwd(b, g, carry):
    n = b.shape[0]
    row = lax.broadcasted_iota(jnp.int32, (n, 1), 0)
    acc_b, acc_l = b, g
    s = 1
    while s < n:
        m = row < n - s
        l_s = jnp.where(m, pltpu.roll(acc_l, n - s, 0), 0.0)
        b_s = jnp.where(m, pltpu.roll(acc_b, n - s, 0), 1.0)
        acc_l = acc_b * l_s + acc_l
        acc_b = acc_b * b_s
        s *= 2
    return acc_l + acc_b * carry


def _shift_down(ext, k, halo):
    return pltpu.roll(ext, k, 0)[halo:] if k else ext[halo:]


def _shift_up(ext, k, ts):
    return pltpu.roll(ext, ext.shape[0] - k, 0)[:ts] if k else ext[:ts]


def _rows(ts, width, nt=None, col=0):
    if nt is None:
        return pl.BlockSpec((ts, width), lambda i: (i, col))
    return pl.BlockSpec((ts, width), lambda i: (nt - 1 - i, col))


def _resident(shape):
    zeros = (0,) * len(shape)
    return pl.BlockSpec(shape, lambda i: zeros, pipeline_mode=pl.Buffered(1))


def _acc(shape):
    zeros = (0,) * len(shape)
    return pl.BlockSpec(shape, lambda i: zeros)


def _params():
    return pltpu.CompilerParams(dimension_semantics=("arbitrary",), vmem_limit_bytes=VMEM_LIMIT)


def _sds(shape, dtype=F32):
    return jax.ShapeDtypeStruct(shape, dtype)


class _Task:
    def __init__(self, ins, out_shapes, aliases, sems, start, finish):
        self.ins, self.out_shapes, self.aliases, self.sems = list(ins), list(out_shapes), dict(aliases), list(sems)
        self.start, self.finish = start, finish


def _call(body, name, grid, in_specs, out_specs, out_shape, scratch_shapes, args, tasks=()):
    n_in, n_out, n_scr = len(in_specs), len(out_specs), len(scratch_shapes)
    t_in = [len(t.ins) for t in tasks]
    t_out = [len(t.out_shapes) for t in tasks]
    t_sem = [len(t.sems) for t in tasks]
    steps = 1
    for g in grid:
        steps *= g

    def take(refs, pos, counts):
        groups = []
        for c in counts:
            groups.append(refs[pos:pos + c])
            pos += c
        return groups, pos

    def wrapped(*refs):
        (cin,), pos = take(refs, 0, [n_in])
        tin, pos = take(refs, pos, t_in)
        (cout,), pos = take(refs, pos, [n_out])
        tout, pos = take(refs, pos, t_out)
        (cscr,), pos = take(refs, pos, [n_scr])
        tsem, pos = take(refs, pos, t_sem)
        if not grid:
            for t, a, b, c in zip(tasks, tin, tout, tsem):
                t.start(a, b, c)
            for t, a, b, c in zip(tasks, tin, tout, tsem):
                t.finish(a, b, c)
            return
        step = pl.program_id(0)
        for axis in range(1, len(grid)):
            step = step * grid[axis] + pl.program_id(axis)
        if tasks:
            @pl.when(step == 0)
            def _():
                for t, a, b, c in zip(tasks, tin, tout, tsem):
                    t.start(a, b, c)

        body(*cin, *cout, *cscr)
        if tasks:
            @pl.when(step == steps - 1)
            def _():
                for t, a, b, c in zip(tasks, tin, tout, tsem):
                    t.finish(a, b, c)

    aliases, in_pos, out_pos = {}, n_in, n_out
    for t, ni, no in zip(tasks, t_in, t_out):
        aliases.update({in_pos + a: out_pos + b for a, b in t.aliases.items()})
        in_pos, out_pos = in_pos + ni, out_pos + no
    any_spec = pl.BlockSpec(memory_space=pl.ANY)
    kwargs = dict(grid=grid, compiler_params=pltpu.CompilerParams(
        dimension_semantics=("arbitrary",) * len(grid), vmem_limit_bytes=VMEM_LIMIT)) if grid else {}
    out = pl.pallas_call(
        wrapped, name=name,
        in_specs=list(in_specs) + [any_spec] * sum(t_in),
        out_specs=list(out_specs) + [any_spec] * sum(t_out),
        out_shape=list(out_shape) + [s for t in tasks for s in t.out_shapes],
        scratch_shapes=list(scratch_shapes) + [s for t in tasks for s in t.sems],
        input_output_aliases=aliases, **kwargs,
    )(*args, *[a for t in tasks for a in t.ins])
    task_outs, pos = take(list(out), n_out, t_out)
    return list(out[:n_out]), task_outs


def _fwd_in_pool(x, g_pre, w_in, pool_w, pool_scale, w_pool_out, ts, tasks=()):
    s = x.shape[0]

    def body(x_ref, g_ref, win_ref, pw_ref, ps_ref, wpo_ref,
             urx_ref, urg_ref, gp_ref, gr_ref, d_ref, yp_ref, z_scr, halo_scr):
        i = pl.program_id(0)

        @pl.when(i == 0)
        def _():
            halo_scr[...] = jnp.zeros_like(halo_scr)

        h1, _, _ = _rms_fwd(x_ref[...], g_ref[...])
        h1 = h1.astype(BF16)
        for j in range(N_CHIPS):
            z_scr[:, j * IN_SHARD:(j + 1) * IN_SHARD] = jnp.dot(h1, win_ref[j], preferred_element_type=F32)
        urx_ref[...] = z_scr[:, 512:1536]
        urg_ref[...] = z_scr[:, 1536:2560]
        gp_ref[...] = z_scr[:, 2560:3584]
        gr_ref[...] = z_scr[:, 3584:4608]
        u = z_scr[:, 0:POOL_WIDTH]
        ext = jnp.concatenate([halo_scr[...], u], axis=0)
        halo_scr[...] = u[ts - POOL_HALO:, :]
        t = i * ts + lax.broadcasted_iota(jnp.int32, (ts, 1), 0)
        y4 = []
        for g, w in enumerate(POOL_WINDOWS):
            lanes = slice(g * POOL_GROUP_DIM, (g + 1) * POOL_GROUP_DIM)
            acc = ext[:, lanes]
            sh = 1
            while sh < w:
                acc = acc + pltpu.roll(acc, sh, 0)
                sh *= 2
            inv = 1.0 / jnp.minimum(t + 1, w).astype(F32)
            dg = acc[POOL_HALO:, :] * inv - u[:, lanes]
            d_ref[:, lanes] = dg
            y4.append(_dot(dg, pw_ref[g]))
        ypre = jnp.concatenate(y4, axis=1) * ps_ref[...]
        ypre = ypre.astype(BF16)
        for j in range(N_CHIPS):
            yp_ref[:, j * 256:(j + 1) * 256] = jnp.dot(ypre, wpo_ref[j], preferred_element_type=F32)

    return _call(
        body, "fwd_in_pool", (s // ts,),
        [_rows(ts, D_MODEL), _resident((1, D_MODEL)), _resident(w_in.shape), _resident(pool_w.shape),
         _resident((1, POOL_WIDTH)), _resident(w_pool_out.shape)],
        [_rows(ts, D_MODEL)] * 4 + [_rows(ts, POOL_WIDTH), _rows(ts, D_MODEL)],
        [_sds((s, D_MODEL))] * 4 + [_sds((s, POOL_WIDTH)), _sds((s, D_MODEL))],
        [pltpu.VMEM((ts, IN_TOTAL), F32), pltpu.VMEM((POOL_HALO, POOL_WIDTH), F32)],
        (x, g_pre, w_in, pool_w, pool_scale, w_pool_out), tasks)


def _fwd_rnn_merge(urx, urg, gp, gr, ypool, x, conv_w, conv_b, wg, bg, lam, w_rg_out, w_o, g_post, ts, tasks=()):
    s = x.shape[0]

    def body(urx_ref, urg_ref, gp_ref, gr_ref, yp_ref, x_ref, cw_ref, cb_ref, wg_ref, bg_ref, lam_ref, wrg_ref, wo_ref,
             gpost_ref, xc_ref, r_ref, ig_ref, h_ref, yr_ref, mo_ref, x1_ref, halo_scr, carry_scr):
        i = pl.program_id(0)

        @pl.when(i == 0)
        def _():
            halo_scr[...] = jnp.zeros_like(halo_scr)
            carry_scr[...] = jnp.zeros_like(carry_scr)

        urx_v = urx_ref[...]
        ext = jnp.concatenate([halo_scr[...], urx_v], axis=0)
        halo_scr[...] = urx_v[ts - CONV_HALO:, :]
        cw = cw_ref[...]
        xc = (cb_ref[...] + cw[3:4] * urx_v + cw[2:3] * _shift_down(ext, 1, CONV_HALO)
              + cw[1:2] * _shift_down(ext, 2, CONV_HALO) + cw[0:1] * _shift_down(ext, 3, CONV_HALO))
        xc_ref[...] = xc
        xcb = xc.astype(BF16)
        lin = []
        for gate in range(2):
            parts = [jnp.dot(xcb[:, q * GATE_BLOCK:(q + 1) * GATE_BLOCK], wg_ref[gate, q], preferred_element_type=F32)
                     for q in range(GATE_BLOCKS)]
            lin.append(jnp.concatenate(parts, axis=1) + bg_ref[gate:gate + 1, :])
        r = jax.nn.sigmoid(lin[0])
        ig = jax.nn.sigmoid(lin[1])
        r_ref[...] = r
        ig_ref[...] = ig
        first_row = (i * ts + lax.broadcasted_iota(jnp.int32, (ts, 1), 0)) == 0
        _, a, _, mult = _lru_coeffs(r, lam_ref[...], first_row)
        h = _scan_fwd(a, mult * ig * xc, carry_scr[0:1, :])
        carry_scr[0:1, :] = h[ts - 1:ts, :]
        h_ref[...] = h
        gl, _ = _gelu(urg_ref[...])
        yr = _dot(h * gl, wrg_ref[...])
        yr_ref[...] = yr
        merged = jax.nn.sigmoid(gp_ref[...]) * yp_ref[...] + jax.nn.sigmoid(gr_ref[...]) * yr
        mo = _dot(merged, wo_ref[...])
        mo_ref[...] = mo
        y, _, _ = _rms_fwd(mo, gpost_ref[...])
        x1_ref[...] = x_ref[...] + y

    row = _rows(ts, D_MODEL)
    return _call(
        body, "fwd_rnn_merge", (s // ts,),
        [row] * 6 + [_resident(conv_w.shape), _resident((1, D_MODEL)), _resident(wg.shape), _resident(bg.shape),
                     _resident((1, D_MODEL)), _resident(w_rg_out.shape), _resident(w_o.shape), _resident((1, D_MODEL))],
        [row] * 7, [_sds((s, D_MODEL))] * 7,
        [pltpu.VMEM((CONV_HALO, D_MODEL), F32), pltpu.VMEM((8, D_MODEL), F32)],
        (urx, urg, gp, gr, ypool, x, conv_w, conv_b, wg, bg, lam, w_rg_out, w_o, g_post), tasks)


def _fwd_ffn(x1, g_pre, w_up, fcw, fcb, w_down, g_post, ts):
    s = x1.shape[0]

    def body(x1_ref, g_ref, wup_ref, fcw_ref, fcb_ref, wd_ref, gpost_ref, up_ref, h2_ref, dn_ref, x2_ref, halo_scr):
        i = pl.program_id(0)

        @pl.when(i == 0)
        def _():
            halo_scr[...] = jnp.zeros_like(halo_scr)

        x1_v = x1_ref[...]
        h2, _, _ = _rms_fwd(x1_v, g_ref[...])
        h2 = h2.astype(BF16)
        h2_ref[...] = h2
        for j in range(N_CHIPS):
            up_ref[:, j * UP_SHARD:(j + 1) * UP_SHARD] = jnp.dot(h2, wup_ref[j], preferred_element_type=F32)
        ug = up_ref[:, 0:D_FF]
        ext = jnp.concatenate([halo_scr[...], ug], axis=0)
        halo_scr[...] = ug[ts - CONV_HALO:, :]
        w = fcw_ref[...]
        gh = (fcb_ref[...] + w[2:3] * ug + w[1:2] * _shift_down(ext, 1, CONV_HALO)
              + w[0:1] * _shift_down(ext, 2, CONV_HALO))
        gl, _ = _gelu(gh)
        dn = _dot(gl * up_ref[:, D_FF:], wd_ref[...])
        dn_ref[...] = dn
        y, _, _ = _rms_fwd(dn, gpost_ref[...])
        x2_ref[...] = x1_v + y

    row = _rows(ts, D_MODEL)
    return pl.pallas_call(
        body, name="fwd_ffn", grid=(s // ts,),
        in_specs=[row, _resident((1, D_MODEL)), _resident(w_up.shape), _resident(fcw.shape), _resident((1, D_FF)),
                  _resident(w_down.shape), _resident((1, D_MODEL))],
        out_specs=[_rows(ts, 2 * D_FF), row, row, row],
        out_shape=[_sds((s, 2 * D_FF)), _sds((s, D_MODEL), BF16), _sds((s, D_MODEL)), _sds((s, D_MODEL))],
        scratch_shapes=[pltpu.VMEM((CONV_HALO, D_FF), F32)],
        compiler_params=_params(),
    )(x1, g_pre, w_up, fcw, fcb, w_down, g_post)


def _ple_loss(x2, p, tgt, g_gate, w_gate, w_proj, g_post, ts):
    s = x2.shape[0]

    def body(x2_ref, p_ref, t_ref, gg_ref, wg_ref, wp_ref, gp_ref, dx2_ref, loss_ref, dwg_ref, dwp_ref, dgg_ref, dgp_ref):
        @pl.when(pl.program_id(0) == 0)
        def _():
            loss_ref[...] = jnp.zeros_like(loss_ref)
            dwg_ref[...] = jnp.zeros_like(dwg_ref)
            dwp_ref[...] = jnp.zeros_like(dwp_ref)
            dgg_ref[...] = jnp.zeros_like(dgg_ref)
            dgp_ref[...] = jnp.zeros_like(dgp_ref)

        x2_v = x2_ref[...]
        n3, xh3, r3 = _rms_fwd(x2_v, gg_ref[...])
        pg = jax.nn.sigmoid(_dot(n3, wg_ref[...]))
        pb = p_ref[...].astype(BF16)
        q = jnp.concatenate([jnp.dot(pb, wp_ref[j], preferred_element_type=F32) for j in range(N_CHIPS)], axis=1)
        ple, qh, rq = _rms_fwd(q, gp_ref[...])
        e = x2_v + pg * ple - t_ref[...]
        loss_ref[...] += 0.5 * jnp.sum(jnp.mean(e * e, axis=-1, keepdims=True), axis=0, keepdims=True)
        dy = e * (1.0 / D_MODEL)
        dpgl = dy * ple * pg * (1.0 - pg)
        dwg_ref[...] += _dot_tn(n3, dpgl)
        dx3, dgg = _rms_bwd(xh3, r3, gg_ref[...], _dot_nt(dpgl, wg_ref[...]))
        dgg_ref[...] += dgg
        dq, dgp = _rms_bwd(qh, rq, gp_ref[...], dy * pg)
        dgp_ref[...] += dgp
        for j in range(N_CHIPS):
            dwp_ref[j] += _dot_tn(pb, dq[:, j * 256:(j + 1) * 256])
        dx2_ref[...] = dy + dx3

    row = _rows(ts, D_MODEL)
    vec = _acc((1, D_MODEL))
    return pl.pallas_call(
        body, name="ple_loss", grid=(s // ts,),
        in_specs=[row, _rows(ts, PLE_DIM), row, _resident((1, D_MODEL)), _resident(w_gate.shape), _resident(w_proj.shape),
                  _resident((1, D_MODEL))],
        out_specs=[row, _acc((1, 128)), _acc(w_gate.shape), _acc(w_proj.shape), vec, vec],
        out_shape=[_sds((s, D_MODEL)), _sds((1, 128)), _sds(w_gate.shape), _sds(w_proj.shape), _sds((1, D_MODEL)),
                   _sds((1, D_MODEL))],
        compiler_params=_params(),
    )(x2, p, tgt, g_gate, w_gate, w_proj, g_post)


def _bwd_ffn_down(dx2, dn, up, fcw, fcb, w_down, g_post, ts):
    s = dx2.shape[0]
    nt = s // ts
    halo_blocks = ts // CONV_HALO

    def body(dx2_ref, dn_ref, up_ref, uph_ref, fcw_ref, fcb_ref, wd_ref, gpost_ref,
             dup_ref, dwd_ref, dfcw_ref, dfcb_ref, dgp_ref, carry_scr):
        i = pl.program_id(0)
        k = nt - 1 - i

        @pl.when(i == 0)
        def _():
            carry_scr[...] = jnp.zeros_like(carry_scr)
            dwd_ref[...] = jnp.zeros_like(dwd_ref)
            dfcw_ref[...] = jnp.zeros_like(dfcw_ref)
            dfcb_ref[...] = jnp.zeros_like(dfcb_ref)
            dgp_ref[...] = jnp.zeros_like(dgp_ref)

        _, xh, r = _rms_fwd(dn_ref[...], gpost_ref[...])
        ddn, dgp = _rms_bwd(xh, r, gpost_ref[...], dx2_ref[...])
        dgp_ref[...] += dgp
        dhid = _dot_nt(ddn, wd_ref[...])
        ug = up_ref[:, 0:D_FF]
        uv = up_ref[:, D_FF:]
        halo = jnp.where(k > 0, uph_ref[...], 0.0)
        ext = jnp.concatenate([halo, ug], axis=0)
        w = fcw_ref[...]
        gh = (fcb_ref[...] + w[2:3] * ug + w[1:2] * _shift_down(ext, 1, CONV_HALO)
              + w[0:1] * _shift_down(ext, 2, CONV_HALO))
        gl, t = _gelu(gh)
        dwd_ref[...] += _dot_tn(gl * uv, ddn)
        dgh = dhid * uv * _gelu_grad(gh, t)
        dup_ref[:, D_FF:] = (dhid * gl).astype(BF16)
        extd = jnp.concatenate([dgh, carry_scr[...]], axis=0)
        carry_scr[...] = dgh[0:CONV_HALO, :]
        d1 = _shift_up(extd, 1, ts)
        d2 = _shift_up(extd, 2, ts)
        dup_ref[:, 0:D_FF] = (w[2:3] * dgh + w[1:2] * d1 + w[0:1] * d2).astype(BF16)
        dfcw_ref[2:3, :] += jnp.sum(ug * dgh, axis=0, keepdims=True)
        dfcw_ref[1:2, :] += jnp.sum(ug * d1, axis=0, keepdims=True)
        dfcw_ref[0:1, :] += jnp.sum(ug * d2, axis=0, keepdims=True)
        dfcb_ref[...] += jnp.sum(dgh, axis=0, keepdims=True)

    row = _rows(ts, D_MODEL, nt)
    halo_spec = pl.BlockSpec((CONV_HALO, D_FF), lambda i: (jnp.maximum((nt - 1 - i) * halo_blocks - 1, 0), 0))
    return pl.pallas_call(
        body, name="bwd_ffn_down", grid=(nt,),
        in_specs=[row, row, _rows(ts, 2 * D_FF, nt), halo_spec, _resident(fcw.shape), _resident((1, D_FF)),
                  _resident(w_down.shape), _resident((1, D_MODEL))],
        out_specs=[_rows(ts, 2 * D_FF, nt), _acc(w_down.shape), _acc(fcw.shape), _acc((1, D_FF)), _acc((1, D_MODEL))],
        out_shape=[_sds((s, 2 * D_FF), BF16), _sds(w_down.shape), _sds(fcw.shape), _sds((1, D_FF)), _sds((1, D_MODEL))],
        scratch_shapes=[pltpu.VMEM((CONV_HALO, D_FF), F32)],
        compiler_params=_params(),
    )(dx2, dn, up, up, fcw, fcb, w_down, g_post)


def _bwd_ffn_up(dup, x1, dx2, g_pre, w_up, ts, tasks=()):
    s = x1.shape[0]

    def body(dup_ref, x1_ref, dx2_ref, g_ref, wup_ref, dx1_ref, dg_ref):
        @pl.when(pl.program_id(0) == 0)
        def _():
            dg_ref[...] = jnp.zeros_like(dg_ref)

        _, xh, r = _rms_fwd(x1_ref[...], g_ref[...])
        dh2 = _dot_nt(dup_ref[:, 0:UP_SHARD], wup_ref[0])
        for j in range(1, N_CHIPS):
            dh2 = dh2 + _dot_nt(dup_ref[:, j * UP_SHARD:(j + 1) * UP_SHARD], wup_ref[j])
        dx, dg = _rms_bwd(xh, r, g_ref[...], dh2)
        dg_ref[...] += dg
        dx1_ref[...] = dx2_ref[...] + dx

    row = _rows(ts, D_MODEL)
    return _call(
        body, "bwd_ffn_up", (s // ts,),
        [_rows(ts, 2 * D_FF), row, row, _resident((1, D_MODEL)), _resident(w_up.shape)],
        [row, _acc((1, D_MODEL))], [_sds((s, D_MODEL)), _sds((1, D_MODEL))], [],
        (dup, x1, dx2, g_pre, w_up), tasks)


def _dw_up(h2, dup, ts, tasks=()):
    s = h2.shape[0]
    ts = min(DW_TILES * ts, s)

    def body(h2_ref, dup_ref, out_ref):
        @pl.when(pl.program_id(1) == 0)
        def _():
            out_ref[...] = jnp.zeros_like(out_ref)

        out_ref[0] += _dot_tn(h2_ref[...], dup_ref[...])

    return _call(
        body, "dw_up", (N_CHIPS, s // ts),
        [pl.BlockSpec((ts, D_MODEL), lambda j, i: (i, 0)), pl.BlockSpec((ts, UP_SHARD), lambda j, i: (i, j))],
        [pl.BlockSpec((1, D_MODEL, UP_SHARD), lambda j, i: (j, 0, 0))], [_sds((N_CHIPS, D_MODEL, UP_SHARD))], [],
        (h2, dup), tasks)


def _bwd_merge(dx1, mo, gp, gr, ypool, yrnn, g_post, w_o, ts, tasks=()):
    s = dx1.shape[0]

    def body(dx1_ref, mo_ref, gp_ref, gr_ref, yp_ref, yr_ref, g_ref, wo_ref,
             dgp_ref, dgr_ref, dyp_ref, dyr_ref, dwo_ref, dg_ref):
        @pl.when(pl.program_id(0) == 0)
        def _():
            dwo_ref[...] = jnp.zeros_like(dwo_ref)
            dg_ref[...] = jnp.zeros_like(dg_ref)

        _, xh, r = _rms_fwd(mo_ref[...], g_ref[...])
        dmo, dg = _rms_bwd(xh, r, g_ref[...], dx1_ref[...])
        dg_ref[...] += dg
        dmerged = _dot_nt(dmo, wo_ref[...])
        sp = jax.nn.sigmoid(gp_ref[...])
        sr = jax.nn.sigmoid(gr_ref[...])
        yp = yp_ref[...]
        yr = yr_ref[...]
        dwo_ref[...] += _dot_tn(sp * yp + sr * yr, dmo)
        dgp_ref[...] = (dmerged * yp * sp * (1.0 - sp)).astype(BF16)
        dgr_ref[...] = (dmerged * yr * sr * (1.0 - sr)).astype(BF16)
        dyp_ref[...] = (dmerged * sp).astype(BF16)
        dyr_ref[...] = (dmerged * sr).astype(BF16)

    row = _rows(ts, D_MODEL)
    return _call(
        body, "bwd_merge", (s // ts,),
        [row] * 6 + [_resident((1, D_MODEL)), _resident(w_o.shape)],
        [row] * 4 + [_acc(w_o.shape), _acc((1, D_MODEL))],
        [_sds((s, D_MODEL), BF16)] * 4 + [_sds(w_o.shape), _sds((1, D_MODEL))], [],
        (dx1, mo, gp, gr, ypool, yrnn, g_post, w_o), tasks)


def _bwd_rnn(dyr, urx, urg, xc, r, ig, h, conv_w, wg, lam, w_rg_out, ts, tasks=()):
    s = urx.shape[0]
    nt = s // ts
    halo_blocks = ts // CONV_HALO

    def body(dyr_ref, urx_ref, urg_ref, xc_ref, r_ref, ig_ref, h_ref, hh_ref, cw_ref, wg_ref, lam_ref, wrg_ref,
             durx_ref, durg_ref, dwrg_ref, dwg_ref, dcw_ref, dcb_ref, dbg_ref, dlam_ref, mu_scr, carry_scr):
        i = pl.program_id(0)
        k = nt - 1 - i

        @pl.when(i == 0)
        def _():
            mu_scr[...] = jnp.zeros_like(mu_scr)
            carry_scr[...] = jnp.zeros_like(carry_scr)
            dwrg_ref[...] = jnp.zeros_like(dwrg_ref)
            dwg_ref[...] = jnp.zeros_like(dwg_ref)
            dcw_ref[...] = jnp.zeros_like(dcw_ref)
            dcb_ref[...] = jnp.zeros_like(dcb_ref)
            dbg_ref[...] = jnp.zeros_like(dbg_ref)
            dlam_ref[...] = jnp.zeros_like(dlam_ref)

        row = lax.broadcasted_iota(jnp.int32, (ts, 1), 0)
        first_row = (k * ts + row) == 0
        h = h_ref[...]
        urg_v = urg_ref[...]
        dyr_v = dyr_ref[...]
        dhr = _dot_nt(dyr_v, wrg_ref[...])
        gl, t = _gelu(urg_v)
        dwrg_ref[...] += _dot_tn(h * gl, dyr_v)
        durg_ref[...] = (dhr * h * _gelu_grad(urg_v, t)).astype(BF16)
        r_v = r_ref[...]
        ig_v = ig_ref[...]
        xc_v = xc_ref[...]
        lam_v = lam_ref[...]
        c8, a, m2, mult = _lru_coeffs(r_v, lam_v, first_row)
        b = jnp.where(row == ts - 1, 1.0, pltpu.roll(a, ts - 1, 0))
        lt = _scan_bwd(b, dhr * gl, mu_scr[0:1, :])
        mu_scr[0:1, :] = a[0:1, :] * lt[0:1, :]
        h_before = jnp.where(k > 0, hh_ref[CONV_HALO - 1:CONV_HALO, :], 0.0)
        hprev = jnp.where(row == 0, h_before, pltpu.roll(h, 1, 0))
        dmult = lt * ig_v * xc_v
        da = lt * hprev - jnp.where(first_row, 0.0, dmult * a * lax.rsqrt(m2))
        dla = da * a
        dlam_ref[...] += jnp.sum(dla * r_v, axis=0, keepdims=True)
        dlr = (dla * (-c8)) * r_v * (1.0 - r_v)
        dli = (lt * mult * xc_v) * ig_v * (1.0 - ig_v)
        dbg_ref[0:1, :] += jnp.sum(dlr, axis=0, keepdims=True)
        dbg_ref[1:2, :] += jnp.sum(dli, axis=0, keepdims=True)
        xcb = xc_v.astype(BF16)
        parts = []
        for q in range(GATE_BLOCKS):
            blk = slice(q * GATE_BLOCK, (q + 1) * GATE_BLOCK)
            dlr_q = dlr[:, blk].astype(BF16)
            dli_q = dli[:, blk].astype(BF16)
            parts.append(_dot_nt(dlr_q, wg_ref[0, q]) + _dot_nt(dli_q, wg_ref[1, q]))
            dwg_ref[0, q] += _dot_tn(xcb[:, blk], dlr_q)
            dwg_ref[1, q] += _dot_tn(xcb[:, blk], dli_q)
        dxc = lt * mult * ig_v + jnp.concatenate(parts, axis=1)
        extd = jnp.concatenate([dxc, carry_scr[...]], axis=0)
        carry_scr[...] = dxc[0:CONV_HALO, :]
        cw = cw_ref[...]
        urx_v = urx_ref[...]
        durx = cw[3:4] * dxc
        dcw_ref[3:4, :] += jnp.sum(urx_v * dxc, axis=0, keepdims=True)
        for j in (1, 2, 3):
            dj = _shift_up(extd, j, ts)
            durx = durx + cw[3 - j:4 - j] * dj
            dcw_ref[3 - j:4 - j, :] += jnp.sum(urx_v * dj, axis=0, keepdims=True)
        durx_ref[...] = durx.astype(BF16)
        dcb_ref[...] += jnp.sum(dxc, axis=0, keepdims=True)

        @pl.when(i == nt - 1)
        def _():
            dlam_ref[...] = dlam_ref[...] * (LRU_C * jax.nn.sigmoid(-lam_v))

    row_spec = _rows(ts, D_MODEL, nt)
    halo_spec = pl.BlockSpec((CONV_HALO, D_MODEL), lambda i: (jnp.maximum((nt - 1 - i) * halo_blocks - 1, 0), 0))
    vec = _acc((1, D_MODEL))
    return _call(
        body, "bwd_rnn", (nt,),
        [row_spec] * 7 + [halo_spec, _resident(conv_w.shape), _resident(wg.shape), _resident((1, D_MODEL)),
                          _resident(w_rg_out.shape)],
        [row_spec, row_spec, _acc(w_rg_out.shape), _acc(wg.shape), _acc(conv_w.shape), vec, _acc((2, D_MODEL)), vec],
        [_sds((s, D_MODEL), BF16), _sds((s, D_MODEL), BF16), _sds(w_rg_out.shape), _sds(wg.shape), _sds(conv_w.shape),
         _sds((1, D_MODEL)), _sds((2, D_MODEL)), _sds((1, D_MODEL))],
        [pltpu.VMEM((8, D_MODEL), F32), pltpu.VMEM((CONV_HALO, D_MODEL), F32)],
        (dyr, urx, urg, xc, r, ig, h, h, conv_w, wg, lam, w_rg_out), tasks)


def _bwd_pool_in(dyp, d, durx, durg, dgp, dgr, x, dx1, g_pre, w_in, pool_w, pool_scale, w_pool_out, ts, tasks=()):
    s = x.shape[0]
    nt = s // ts

    def body(dyp_ref, d_ref, durx_ref, durg_ref, dgp_ref, dgr_ref, x_ref, dx1_ref, g_ref, win_ref, pw_ref, ps_ref,
             wpo_ref, gx_ref, dwin_ref, dwpo_ref, dpw_ref, dps_ref, dg_ref, dz_scr, carry_scr):
        i = pl.program_id(0)
        k = nt - 1 - i

        @pl.when(i == 0)
        def _():
            carry_scr[...] = jnp.zeros_like(carry_scr)
            dwin_ref[...] = jnp.zeros_like(dwin_ref)
            dwpo_ref[...] = jnp.zeros_like(dwpo_ref)
            dpw_ref[...] = jnp.zeros_like(dpw_ref)
            dps_ref[...] = jnp.zeros_like(dps_ref)
            dg_ref[...] = jnp.zeros_like(dg_ref)

        dyp_v = dyp_ref[...]
        d_v = d_ref[...]
        ps = ps_ref[...]
        dypre = _dot_nt(dyp_v[:, 0:256], wpo_ref[0])
        for j in range(1, N_CHIPS):
            dypre = dypre + _dot_nt(dyp_v[:, j * 256:(j + 1) * 256], wpo_ref[j])
        y4 = jnp.concatenate([_dot(d_v[:, g * 128:(g + 1) * 128], pw_ref[g]) for g in range(POOL_GROUPS)], axis=1)
        ypre = (y4 * ps).astype(BF16)
        for j in range(N_CHIPS):
            dwpo_ref[j] += _dot_tn(ypre, dyp_v[:, j * 256:(j + 1) * 256])
        dps_ref[...] += jnp.sum(dypre * y4, axis=0, keepdims=True)
        dy4 = dypre * ps
        t = k * ts + lax.broadcasted_iota(jnp.int32, (ts, 1), 0)
        for g, w in enumerate(POOL_WINDOWS):
            lanes = slice(g * POOL_GROUP_DIM, (g + 1) * POOL_GROUP_DIM)
            dd = _dot_nt(dy4[:, lanes], pw_ref[g])
            dpw_ref[g] += _dot_tn(d_v[:, lanes], dy4[:, lanes])
            e = dd * (1.0 / jnp.minimum(t + 1, w).astype(F32))
            acc = jnp.concatenate([e, carry_scr[:, lanes]], axis=0)
            carry_scr[:, lanes] = e[0:POOL_HALO, :]
            n = ts + POOL_HALO
            sh = 1
            while sh < w:
                acc = acc + pltpu.roll(acc, n - sh, 0)
                sh *= 2
            dz_scr[:, lanes] = (acc[:ts, :] - dd).astype(BF16)
        dz_scr[:, 512:1536] = durx_ref[...]
        dz_scr[:, 1536:2560] = durg_ref[...]
        dz_scr[:, 2560:3584] = dgp_ref[...]
        dz_scr[:, 3584:4608] = dgr_ref[...]
        h1, xh, r = _rms_fwd(x_ref[...], g_ref[...])
        h1 = h1.astype(BF16)
        dh1 = _dot_nt(dz_scr[:, 0:IN_SHARD], win_ref[0])
        dwin_ref[0] += _dot_tn(h1, dz_scr[:, 0:IN_SHARD])
        for j in range(1, N_CHIPS):
            cols = slice(j * IN_SHARD, (j + 1) * IN_SHARD)
            dh1 = dh1 + _dot_nt(dz_scr[:, cols], win_ref[j])
            dwin_ref[j] += _dot_tn(h1, dz_scr[:, cols])
        dx, dg = _rms_bwd(xh, r, g_ref[...], dh1)
        dg_ref[...] += dg
        gx_ref[...] = dx1_ref[...] + dx

    row = _rows(ts, D_MODEL, nt)
    return _call(
        body, "bwd_pool_in", (nt,),
        [row, _rows(ts, POOL_WIDTH, nt)] + [row] * 6 + [_resident((1, D_MODEL)), _resident(w_in.shape),
                                                       _resident(pool_w.shape), _resident((1, POOL_WIDTH)),
                                                       _resident(w_pool_out.shape)],
        [row, _acc(w_in.shape), _acc(w_pool_out.shape), _acc(pool_w.shape), _acc((1, POOL_WIDTH)), _acc((1, D_MODEL))],
        [_sds((s, D_MODEL)), _sds(w_in.shape), _sds(w_pool_out.shape), _sds(pool_w.shape), _sds((1, POOL_WIDTH)),
         _sds((1, D_MODEL))],
        [pltpu.VMEM((ts, IN_TOTAL), BF16), pltpu.VMEM((POOL_HALO, POOL_WIDTH), F32)],
        (dyp, d, durx, durg, dgp, dgr, x, dx1, g_pre, w_in, pool_w, pool_scale, w_pool_out), tasks)


def _place():
    x, y, c = lax.axis_index("x"), lax.axis_index("y"), lax.axis_index("c")
    others = [(1 - x, y), (x, 1 - y), (1 - x, 1 - y)]
    return x, y, c, 2 * x + y, others


def _remote(src, dst, send_sem, recv_sem, to):
    return pltpu.make_async_remote_copy(src_ref=src, dst_ref=dst, send_sem=send_sem, recv_sem=recv_sem,
                                        device_id=to, device_id_type=MESH)


def _own_slot(w, place, dtype):
    a, b = w.shape
    tr = _row_tile(a, b)

    def body(place_ref, w_ref, out_ref):
        out_ref[0] = w_ref[...].astype(dtype)

    return pl.pallas_call(
        body, name="own_slot",
        grid_spec=pltpu.PrefetchScalarGridSpec(
            num_scalar_prefetch=1, grid=(a // tr,),
            in_specs=[pl.BlockSpec((tr, b), lambda i, s: (i, 0))],
            out_specs=pl.BlockSpec((1, tr, b), lambda i, s: (s[0], i, 0))),
        out_shape=_sds((N_CHIPS, a, b), dtype),
        compiler_params=pltpu.CompilerParams(dimension_semantics=("arbitrary",)),
    )(place, w)


def _run(task, name):
    return _call(None, name, (), [], [], [], [], (), (task,))[1][0]


def _gather_task(bufs):
    n = len(bufs)

    def first_copy(out, w, j, ox, oy, sems):
        x, y, c, me, _ = _place()
        ah = out.shape[1] // 2
        mine = out.at[me, pl.ds(c * ah, ah)]
        return _remote(mine, mine, sems[0].at[w, j], sems[1].at[w, j], (ox, oy, c))

    def passed_copy(out, w, j, ox, oy, sems, half):
        x, y, c, _, _ = _place()
        ah = out.shape[1] // 2
        slot = out.at[2 * ox + oy, pl.ds(half * ah, ah)]
        return _remote(slot, slot, sems[0].at[w, 3 + j], sems[1].at[w, 3 + j], (x, y, 1 - c))

    def start(ins, outs, sems):
        others = _place()[4]
        for w, out in enumerate(outs):
            for j, (ox, oy) in enumerate(others):
                first_copy(out, w, j, ox, oy, sems).start()

    def finish(ins, outs, sems):
        x, y, c, _, others = _place()
        for w, out in enumerate(outs):
            ah = out.shape[1] // 2
            for j, (ox, oy) in enumerate(others):
                slot = out.at[2 * ox + oy, pl.ds(c * ah, ah)]
                _remote(slot, slot, sems[0].at[w, j], sems[1].at[w, j], (ox, oy, c)).wait_recv()
                passed_copy(out, w, j, ox, oy, sems, c).start()
        for w, out in enumerate(outs):
            for j, (ox, oy) in enumerate(others):
                passed_copy(out, w, j, ox, oy, sems, 1 - c).wait_recv()
        for w, out in enumerate(outs):
            for j, (ox, oy) in enumerate(others):
                first_copy(out, w, j, ox, oy, sems).wait_send()
                passed_copy(out, w, j, ox, oy, sems, c).wait_send()

    return _Task(bufs, [_sds(b.shape, b.dtype) for b in bufs], {i: i for i in range(n)},
                 [pltpu.SemaphoreType.DMA((n, 6)), pltpu.SemaphoreType.DMA((n, 6))], start, finish)


def _halves_task(grads):
    n = len(grads)

    def copy(src, out, w, sems):
        x, y, c, _, _ = _place()
        ah = out.shape[1]
        return _remote(src.at[:, pl.ds((1 - c) * ah, ah)], out, sems[0].at[w], sems[1].at[w], (x, y, 1 - c))

    def start(ins, outs, sems):
        for w, (src, out) in enumerate(zip(ins, outs)):
            copy(src, out, w, sems).start()

    def finish(ins, outs, sems):
        for w, (src, out) in enumerate(zip(ins, outs)):
            copy(src, out, w, sems).wait()

    return _Task(grads, [_sds((g.shape[0], g.shape[1] // 2, g.shape[2]), g.dtype) for g in grads], {},
                 [pltpu.SemaphoreType.DMA((n,)), pltpu.SemaphoreType.DMA((n,))], start, finish)


def _exchange_task(sends, accs):
    n = len(accs)
    given = [s for s in sends if s is not None]

    def copies(ins, outs, sems):
        send_refs = iter(ins[:len(given)])
        srcs = [next(send_refs) if s is not None else None for s in sends]
        x, y, c, me, others = _place()
        for w, out in enumerate(outs):
            for j, (ox, oy) in enumerate(others):
                src = out.at[me] if srcs[w] is None else srcs[w].at[2 * ox + oy]
                yield _remote(src, out.at[me], sems[0].at[w, j], sems[1].at[w, j], (ox, oy, c))

    def start(ins, outs, sems):
        for cp in copies(ins, outs, sems):
            cp.start()

    def finish(ins, outs, sems):
        x, y, c, _, others = _place()
        for w, out in enumerate(outs):
            for j, (ox, oy) in enumerate(others):
                slot = out.at[2 * ox + oy]
                _remote(slot, slot, sems[0].at[w, j], sems[1].at[w, j], (ox, oy, c)).wait_recv()
        for cp in copies(ins, outs, sems):
            cp.wait_send()

    return _Task(given + list(accs), [_sds(a.shape, a.dtype) for a in accs], {len(given) + i: i for i in range(n)},
                 [pltpu.SemaphoreType.DMA((n, 3)), pltpu.SemaphoreType.DMA((n, 3))], start, finish)


def _share_task(shares):
    n = len(shares)

    def copy(out, w, sems, slot):
        x, y, c, _, _ = _place()
        return _remote(out.at[slot], out.at[slot], sems[0].at[w], sems[1].at[w], (x, y, 1 - c))

    def start(ins, outs, sems):
        c = _place()[2]
        for w, out in enumerate(outs):
            copy(out, w, sems, c).start()

    def finish(ins, outs, sems):
        c = _place()[2]
        for w, out in enumerate(outs):
            copy(out, w, sems, 1 - c).wait_recv()
        for w, out in enumerate(outs):
            copy(out, w, sems, c).wait_send()

    return _Task(shares, [_sds(s.shape, s.dtype) for s in shares], {i: i for i in range(n)},
                 [pltpu.SemaphoreType.DMA((n,)), pltpu.SemaphoreType.DMA((n,))], start, finish)


TILE_BYTES = 2 * 1024 * 1024
PARTIAL_TILE_BYTES = 512 * 1024


def _row_tile(rows, cols, limit=TILE_BYTES):
    best = 8
    for tr in range(8, rows + 1, 8):
        if rows % tr == 0 and tr * cols * 4 <= limit:
            best = tr
    assert rows % best == 0, (rows, cols)
    return best


def _chip_partial(g, got, place, wire_dtype):
    ns, ah, b = got.shape
    sharded = ns == N_CHIPS
    tr = _row_tile(ah, b, PARTIAL_TILE_BYTES)
    nb = ah // tr

    def body(place_ref, *refs):
        g_refs, got_refs, outs = refs[:ns], refs[ns:2 * ns], refs[2 * ns:]
        parts = [g_refs[k][0] + got_refs[k][0] for k in range(ns)]
        own = parts[0]
        if sharded:
            for k in range(ns):
                outs[0][k] = parts[k].astype(wire_dtype)
                if k:
                    own = jnp.where(place_ref[0] == k, parts[k], own)
        outs[-1][0] = own.astype(wire_dtype)

    blk = (1, tr, b)
    in_specs = ([pl.BlockSpec(blk, lambda i, s, k=k: (k, s[1] * nb + i, 0)) for k in range(ns)]
                + [pl.BlockSpec(blk, lambda i, s, k=k: (k, i, 0)) for k in range(ns)])
    acc_spec = pl.BlockSpec(blk, lambda i, s: (s[0], i, 0))
    acc_shape = _sds((N_CHIPS, ah, b), wire_dtype)
    out = pl.pallas_call(
        body, name="grad_chip_partial",
        grid_spec=pltpu.PrefetchScalarGridSpec(
            num_scalar_prefetch=1, grid=(nb,), in_specs=in_specs,
            out_specs=[pl.BlockSpec((ns, tr, b), lambda i, s: (0, i, 0)), acc_spec] if sharded else [acc_spec]),
        out_shape=[acc_shape, acc_shape] if sharded else [acc_shape],
        compiler_params=pltpu.CompilerParams(dimension_semantics=("arbitrary",), vmem_limit_bytes=32 * 1024 * 1024),
    )(place, *([g] * ns), *([got] * ns))
    return (out[0], out[1]) if sharded else (None, out[0])


def _chip_sum(acc, place):
    _, ah, b = acc.shape
    tr = _row_tile(ah, b)

    def body(place_ref, p_ref, out_ref):
        total = p_ref[0].astype(F32) + p_ref[1].astype(F32)
        total = total + p_ref[2].astype(F32)
        out_ref[0] = total + p_ref[3].astype(F32)

    return pl.pallas_call(
        body, name="grad_chip_sum",
        grid_spec=pltpu.PrefetchScalarGridSpec(
            num_scalar_prefetch=1, grid=(ah // tr,),
            in_specs=[pl.BlockSpec((N_CHIPS, tr, b), lambda i, s: (0, i, 0))],
            out_specs=pl.BlockSpec((1, tr, b), lambda i, s: (s[1], i, 0))),
        out_shape=_sds((2, ah, b)),
        compiler_params=pltpu.CompilerParams(dimension_semantics=("arbitrary",)),
    )(place, acc)


def _adam_math(w, g, m, v):
    nm = ADAM_B1 * m + (1.0 - ADAM_B1) * g
    nv = ADAM_B2 * v + (1.0 - ADAM_B2) * (g * g)
    m_hat = nm / (1.0 - ADAM_B1 ** ADAM_STEP)
    v_hat = nv / (1.0 - ADAM_B2 ** ADAM_STEP)
    return -ADAM_LR * (m_hat / (jnp.sqrt(v_hat) + ADAM_EPS) + ADAM_WD * w), nm, nv


def _adamw(w, g, m, v):
    a, b = w.shape
    tr = _row_tile(a, b)

    def body(w_ref, g_ref, m_ref, v_ref, d_ref, nm_ref, nv_ref):
        d_ref[...], nm_ref[...], nv_ref[...] = _adam_math(w_ref[...], g_ref[...], m_ref[...], v_ref[...])

    blk = pl.BlockSpec((tr, b), lambda i: (i, 0))
    return pl.pallas_call(
        body, name="adamw", grid=(a // tr,),
        in_specs=[blk] * 4, out_specs=[blk] * 3, out_shape=[_sds((a, b))] * 3,
        compiler_params=pltpu.CompilerParams(dimension_semantics=("arbitrary",)),
    )(w, g, m, v)


def _adamw_pieces(g, pieces, name):
    n = len(pieces)

    def body(g_ref, *refs):
        ins, outs = refs[:3 * n], refs[3 * n:]
        for i, piece in enumerate(pieces):
            w_ref, m_ref, v_ref = ins[3 * i:3 * i + 3]
            o_g, o_d, o_m, o_v = outs[4 * i:4 * i + 4]
            if len(piece) == 5:
                g_v = g_ref[piece[3], piece[4]]
                o_g[...] = g_v
                o_d[...], o_m[...], o_v[...] = _adam_math(w_ref[...], g_v, m_ref[...], v_ref[...])
            else:
                for r in range(w_ref.shape[1] // SMALL_COLS):
                    lanes = slice(r * SMALL_COLS, (r + 1) * SMALL_COLS)
                    g_v = g_ref[piece[3] + r:piece[3] + r + 1, :]
                    o_g[:, lanes] = g_v
                    o_d[:, lanes], o_m[:, lanes], o_v[:, lanes] = _adam_math(w_ref[:, lanes], g_v, m_ref[:, lanes],
                                                                            v_ref[:, lanes])

    operands = [t for piece in pieces for t in piece[:3]]
    out = pl.pallas_call(
        body, name=name,
        out_shape=[_sds(piece[0].shape) for piece in pieces for _ in range(4)],
    )(g, *operands)
    return [tuple(out[4 * i:4 * i + 4]) for i in range(n)]


TINY_ROWS, TINY_COLS = 16, 768
SMALL_COLS = 128
SMALL_ROWS = 624


def _pack_tiny(conv_w, b_gates, fcw):
    ns = conv_w.shape[0]
    pad = lambda t: jnp.pad(t, ((0, 0), (0, 0), (0, TINY_COLS - t.shape[2])))
    z = lambda rows: jnp.zeros((ns, rows, TINY_COLS), F32)
    return jnp.concatenate([pad(conv_w), pad(b_gates), z(2), fcw, z(5)], axis=1)


def _unpack_tiny(t):
    return t[:, 0:4, 0:256], t[:, 4:6, 0:256], t[:, 8:11, :]


def _cols_to_shards(t, n):
    return t.reshape(t.shape[0], N_CHIPS, n).transpose(1, 0, 2)


def _shards_to_cols(t):
    return t.transpose(1, 0, 2).reshape(t.shape[1], -1)


_VECTORS = ("g_mix_pre", "g_mix_post", "conv_b", "lru_lambda", "g_ffn_pre", "g_ffn_post", "g_ple_gate", "g_ple_post",
            "pool_scale", "ffn_conv_b")
_VECTOR_LEN = {"pool_scale": POOL_WIDTH, "ffn_conv_b": D_FF}
POOL_W_ROWS = POOL_GROUPS * POOL_GROUP_DIM


def _vector_rows():
    rows, row = {}, POOL_W_ROWS
    for k in _VECTORS:
        rows[k] = row
        row += max(8, _VECTOR_LEN.get(k, D_MODEL) // SMALL_COLS)
    return rows, row


def _pack_small(grads, loss):
    tiles = lambda t: jnp.pad(t, ((0, -t.shape[0] % 8), (0, 0)))
    parts = [grads["pool_w"].reshape(POOL_W_ROWS, SMALL_COLS)] + [tiles(grads[k].reshape(-1, SMALL_COLS)) for k in _VECTORS]
    parts.append(tiles(loss))
    used = sum(t.shape[0] for t in parts)
    return jnp.concatenate(parts + [jnp.zeros((SMALL_ROWS - used, SMALL_COLS), F32)], axis=0)


def _gates_block_diag(w):
    w4 = w.reshape(2, GATE_BLOCKS, 4, RNN_HEAD_DIM, RNN_HEAD_DIM)
    eye = jnp.eye(4, dtype=w.dtype)
    return jnp.einsum("gqhij,hk->gqhikj", w4, eye).reshape(2, GATE_BLOCKS, GATE_BLOCK, GATE_BLOCK)


def _gates_from_block_diag(dw):
    d6 = dw.reshape(2, GATE_BLOCKS, 4, RNN_HEAD_DIM, 4, RNN_HEAD_DIM)
    blocks = [d6[:, :, hh, :, hh, :] for hh in range(4)]
    return jnp.stack(blocks, axis=2).reshape(2, RNN_HEADS, RNN_HEAD_DIM, RNN_HEAD_DIM)


ROW_TILE = 256
DW_TILES = 4

_SHARDED = ("w_in", "w_pool_out", "w_rg_out", "w_o", "w_up", "w_down", "w_ple_gate", "w_ple_proj")
_WEIGHTS = ("g_mix_pre", "g_mix_post", "w_in", "pool_w", "pool_scale", "w_pool_out", "conv_w", "conv_b", "w_rg_gates",
            "b_rg_gates", "lru_lambda", "w_rg_out", "w_o", "g_ffn_pre", "g_ffn_post", "w_up", "ffn_conv_w", "ffn_conv_b",
            "w_down", "g_ple_gate", "w_ple_gate", "w_ple_proj", "g_ple_post")


def _wire_dtype(g):
    return BF16 if g.shape[1] >= 64 and g.shape[2] > SMALL_COLS else F32


def _partials(grads, got, place):
    parts = [_chip_partial(g, r, place, _wire_dtype(g)) for g, r in zip(grads, got)]
    return [send for send, _ in parts], [acc for _, acc in parts]


def _whole(both):
    return [b.reshape(2 * b.shape[1], b.shape[2]) for b in both]


def _step(x, p, tgt, own, rep, place, ts):
    vec = lambda k: rep[k].reshape(1, -1)
    pool_w = rep["pool_w"].astype(BF16)
    wg = _gates_block_diag(rep["w_rg_gates"]).astype(BF16)
    sq = lambda t: t.reshape(D_MODEL, D_MODEL)
    by4 = lambda t: t.reshape(N_CHIPS, -1, D_MODEL)

    first, ride1, ride2 = ("w_in", "w_pool_out", "tiny"), ("w_rg_out", "w_o", "w_down", "w_ple_gate", "w_ple_proj"), ("w_up",)
    full = dict(zip(first, _run(_gather_task([own[k] for k in first]), "gather_first")))
    conv_w, b_gates, fcw = [_shards_to_cols(t) for t in _unpack_tiny(full["tiny"])]

    (urx, urg, gp, gr, d, ypool), (got,) = _fwd_in_pool(
        x, vec("g_mix_pre"), full["w_in"], pool_w, vec("pool_scale"), full["w_pool_out"], ts,
        [_gather_task([own[k] for k in ride1])])
    full.update(zip(ride1, got))
    w_rg_out, w_o, w_gate = sq(full["w_rg_out"]), sq(full["w_o"]), sq(full["w_ple_gate"])
    w_down = full["w_down"].reshape(D_FF, D_MODEL)
    (xc, r, ig, h, yrnn, mo, x1), (got,) = _fwd_rnn_merge(
        urx, urg, gp, gr, ypool, x, conv_w, vec("conv_b"), wg, b_gates, vec("lru_lambda"), w_rg_out, w_o,
        vec("g_mix_post"), ts, [_gather_task([own[k] for k in ride2])])
    full.update(zip(ride2, got))
    up, h2, dn, x2 = _fwd_ffn(x1, vec("g_ffn_pre"), full["w_up"], fcw, vec("ffn_conv_b"), w_down, vec("g_ffn_post"), ts)
    dx2, loss, d_w_gate, d_w_proj, d_g_ple_gate, d_g_ple_post = _ple_loss(
        x2, p, tgt, vec("g_ple_gate"), w_gate, full["w_ple_proj"], vec("g_ple_post"), ts)
    dup, d_w_down, d_fcw, d_fcb, d_g_ffn_post = _bwd_ffn_down(dx2, dn, up, fcw, vec("ffn_conv_b"), w_down,
                                                               vec("g_ffn_post"), ts)

    names1, grads1 = ("w_ple_gate", "w_ple_proj", "w_down"), [by4(d_w_gate), d_w_proj, by4(d_w_down)]
    (dx1, d_g_ffn_pre), (got1,) = _bwd_ffn_up(dup, x1, dx2, vec("g_ffn_pre"), full["w_up"], ts, [_halves_task(grads1)])
    (d_w_up,), (accs1,) = _dw_up(h2, dup, ts, [_exchange_task(*_partials(grads1, got1, place))])
    shares1 = [_chip_sum(acc, place) for acc in accs1]
    (dgp, dgr, dyp, dyr, d_w_o, d_g_mix_post), (got2, both1) = _bwd_merge(
        dx1, mo, gp, gr, ypool, yrnn, vec("g_mix_post"), w_o, ts, [_halves_task([d_w_up]), _share_task(shares1)])
    (durx, durg, d_w_rg_out, d_wg, d_conv_w, d_conv_b, d_b_gates, d_lam), (accs2,) = _bwd_rnn(
        dyr, urx, urg, xc, r, ig, h, conv_w, wg, vec("lru_lambda"), w_rg_out, ts,
        [_exchange_task(*_partials([d_w_up], got2, place))])
    shares2 = [_chip_sum(acc, place) for acc in accs2]
    (grad_x, d_w_in, d_w_pool_out, d_pool_w, d_pool_scale, d_g_mix_pre), _ = _bwd_pool_in(
        dyp, d, durx, durg, dgp, dgr, x, dx1, vec("g_mix_pre"), full["w_in"], pool_w, vec("pool_scale"),
        full["w_pool_out"], ts)

    replicated = {"g_mix_pre": d_g_mix_pre, "g_mix_post": d_g_mix_post, "conv_b": d_conv_b, "lru_lambda": d_lam,
                  "g_ffn_pre": d_g_ffn_pre, "g_ffn_post": d_g_ffn_post, "g_ple_gate": d_g_ple_gate,
                  "g_ple_post": d_g_ple_post, "pool_scale": d_pool_scale, "ffn_conv_b": d_fcb, "pool_w": d_pool_w}
    names3 = ("w_in", "w_pool_out", "w_rg_out", "w_o", "tiny", "small", "w_rg_gates")
    grads3 = [d_w_in, d_w_pool_out, by4(d_w_rg_out), by4(d_w_o),
              _pack_tiny(_cols_to_shards(d_conv_w, 256), _cols_to_shards(d_b_gates, 256), _cols_to_shards(d_fcw, 768)),
              _pack_small(replicated, loss)[None],
              _gates_from_block_diag(d_wg).reshape(1, 2 * RNN_HEADS * RNN_HEAD_DIM, RNN_HEAD_DIM)]
    got3 = _run(_halves_task(grads3), "grad_sibling_halves")
    accs3 = _run(_exchange_task(*_partials(grads3, got3, place)), "grad_chip_exchange")
    both = _run(_share_task(shares2 + [_chip_sum(acc, place) for acc in accs3]), "grad_sibling_share")

    return grad_x, dict(zip(names1 + ("w_up",) + names3, _whole(both1) + _whole(both)))


def kernel(x, p, g_mix_pre, g_mix_post, w_in, pool_w, pool_scale, w_pool_out, conv_w, conv_b, w_rg_gates, b_rg_gates, lru_lambda, w_rg_out, w_o, g_ffn_pre, g_ffn_post, w_up, ffn_conv_w, ffn_conv_b, w_down, g_ple_gate, w_ple_gate, w_ple_proj, g_ple_post, loss_target, m_g_mix_pre, m_g_mix_post, m_w_in, m_pool_w, m_pool_scale, m_w_pool_out, m_conv_w, m_conv_b, m_w_rg_gates, m_b_rg_gates, m_lru_lambda, m_w_rg_out, m_w_o, m_g_ffn_pre, m_g_ffn_post, m_w_up, m_ffn_conv_w, m_ffn_conv_b, m_w_down, m_g_ple_gate, m_w_ple_gate, m_w_ple_proj, m_g_ple_post, v_g_mix_pre, v_g_mix_post, v_w_in, v_pool_w, v_pool_scale, v_w_pool_out, v_conv_w, v_conv_b, v_w_rg_gates, v_b_rg_gates, v_lru_lambda, v_w_rg_out, v_w_o, v_g_ffn_pre, v_g_ffn_post, v_w_up, v_ffn_conv_w, v_ffn_conv_b, v_w_down, v_g_ple_gate, v_w_ple_gate, v_w_ple_proj, v_g_ple_post):
    args = dict(locals())
    w = {k: args[k][0] for k in _WEIGHTS}
    m = {k: args["m_" + k][0] for k in _WEIGHTS}
    v = {k: args["v_" + k][0] for k in _WEIGHTS}
    place = jnp.stack([2 * lax.axis_index("x") + lax.axis_index("y"), lax.axis_index("c")]).astype(jnp.int32)
    own = {k: _own_slot(w[k], place, BF16) for k in _SHARDED}
    own["tiny"] = _own_slot(_pack_tiny(w["conv_w"][None], w["b_rg_gates"][None], w["ffn_conv_w"][None])[0], place, F32)
    grad_x, reduced = _step(x[0], p[0, 0], loss_target[0], own, w, place, ROW_TILE)

    gates_2d = (2 * RNN_HEADS * RNN_HEAD_DIM, RNN_HEAD_DIM)
    as2d = lambda k, shape: tuple(t[k].reshape(shape) for t in (w, m, v))
    done = {k: (reduced[k],) + tuple(_adamw(w[k], reduced[k], m[k], v[k])) for k in _SHARDED}
    gates_w, gates_m, gates_v = as2d("w_rg_gates", gates_2d)
    done["w_rg_gates"] = (reduced["w_rg_gates"],) + tuple(_adamw(gates_w, reduced["w_rg_gates"], gates_m, gates_v))
    tiny_names = ("conv_w", "b_rg_gates", "ffn_conv_w")
    tiny_at = ((slice(0, 4), slice(0, 256)), (slice(4, 6), slice(0, 256)), (slice(8, 11), slice(None)))
    done.update(zip(tiny_names, _adamw_pieces(
        reduced["tiny"], [(w[k], m[k], v[k]) + at for k, at in zip(tiny_names, tiny_at)], "adamw_tiny")))
    vector_rows, loss_row = _vector_rows()
    pieces = [as2d("pool_w", (POOL_W_ROWS, SMALL_COLS)) + (slice(0, POOL_W_ROWS), slice(None))]
    pieces += [as2d(k, (1, -1)) + (vector_rows[k],) for k in _VECTORS]
    done.update(zip(("pool_w",) + _VECTORS, _adamw_pieces(reduced["small"], pieces, "adamw_small")))

    result = [reduced["small"][loss_row, 0], grad_x[None]]
    for kind in range(4):
        result += [done[k][kind].reshape(args[k].shape) for k in _WEIGHTS]
    return tuple(result)
```

```python
import functools

import jax
import jax.numpy as jnp
from jax import lax
from jax.experimental import pallas as pl
from jax.experimental.pallas import tpu as pltpu

F32 = jnp.float32
BF16 = jnp.bfloat16

D_MODEL = 1024
POOL_WINDOWS = (2, 4, 8, 16)
POOL_GROUPS = 4
POOL_WIDTH = 512
POOL_GROUP_DIM = 128
RNN_HEADS = 16
RNN_HEAD_DIM = 64
GATE_BLOCK = 256
GATE_BLOCKS = D_MODEL // GATE_BLOCK
LRU_C = 8.0
D_FF = 3072
PLE_DIM = 256
RMS_EPS = 1e-6
IN_TOTAL = 4608
N_CHIPS = 4
IN_SHARD = IN_TOTAL // N_CHIPS
UP_SHARD = 2 * D_FF // N_CHIPS
POOL_HALO = 16
CONV_HALO = 8

ADAM_LR = 0.001
ADAM_B1 = 0.9
ADAM_B2 = 0.999
ADAM_EPS = 1e-08
ADAM_WD = 0.01
ADAM_STEP = 10

VMEM_LIMIT = 56 * 1024 * 1024
MESH = pl.DeviceIdType.MESH

_GELU_C = 0.7978845608028654
_GELU_A = 0.044715


def _dot(a, b):
    return jnp.dot(a.astype(BF16), b.astype(BF16), preferred_element_type=F32)


def _dot_nt(a, b):
    return lax.dot_general(a.astype(BF16), b.astype(BF16), (((1,), (1,)), ((), ())), preferred_element_type=F32)


def _dot_tn(a, b):
    return lax.dot_general(a.astype(BF16), b.astype(BF16), (((0,), (0,)), ((), ())), preferred_element_type=F32)


def _rms_fwd(x, g):
    r = lax.rsqrt(jnp.mean(x * x, axis=-1, keepdims=True) + RMS_EPS)
    xh = x * r
    return xh * g, xh, r


def _rms_bwd(xh, r, g, dy):
    dxh = dy * g
    dg = jnp.sum(dy * xh, axis=0, keepdims=True)
    dx = r * (dxh - xh * jnp.mean(dxh * xh, axis=-1, keepdims=True))
    return dx, dg


def _gelu(x):
    t = jnp.tanh(_GELU_C * (x + _GELU_A * x * x * x))
    return 0.5 * x * (1.0 + t), t


def _gelu_grad(x, t):
    return 0.5 * (1.0 + t) + 0.5 * x * (1.0 - t * t) * _GELU_C * (1.0 + 3.0 * _GELU_A * x * x)


def _softplus_neg(lam):
    nl = -lam
    return jnp.maximum(nl, 0.0) + jnp.log(1.0 + jnp.exp(-jnp.abs(nl)))


def _lru_coeffs(r, lam, first_row):
    c8 = LRU_C * _softplus_neg(lam)
    la = -(c8 * r)
    a = jnp.exp(la)
    m2 = jnp.tanh(-la) * (1.0 + a * a)
    mult = jnp.where(first_row, 1.0, jnp.sqrt(m2))
    return c8, a, m2, mult


SUBLANES = 8


def _scan_fwd(a, u, carry):
    n = a.shape[0]
    sub = lax.broadcasted_iota(jnp.int32, (n, 1), 0) % SUBLANES
    acc_a, acc_h = a, u
    for s in (1, 2, 4):
        m = sub >= s
        h_s = jnp.where(m, pltpu.roll(acc_h, s, 0), 0.0)
        a_s = jnp.where(m, pltpu.roll(acc_a, s, 0), 1.0)
        acc_h = acc_a * h_s + acc_h
        acc_a = acc_a * a_s
    out = []
    for g in range(n // SUBLANES):
        rows = slice(g * SUBLANES, (g + 1) * SUBLANES)
        out.append(acc_h[rows] + acc_a[rows] * carry)
        carry = out[-1][SUBLANES - 1:SUBLANES]
    return jnp.concatenate(out, axis=0)


def _scan_bwd(b, g, carry):
    n = b.shape[0]
    sub = lax.broadcasted_iota(jnp.int32, (n, 1), 0) % SUBLANES
    acc_b, acc_l = b, g
    for s in (1, 2, 4):
        m = sub < SUBLANES - s
        l_s = jnp.where(m, pltpu.roll(acc_l, n - s, 0), 0.0)
        b_s = jnp.where(m, pltpu.roll(acc_b, n - s, 0), 1.0)
        acc_l = acc_b * l_s + acc_l
        acc_b = acc_b * b_s
    out = [None] * (n // SUBLANES)
    for g in reversed(range(n // SUBLANES)):
        rows = slice(g * SUBLANES, (g + 1) * SUBLANES)
        out[g] = acc_l[rows] + acc_b[rows] * carry
        carry = out[g][0:1]
    return jnp.concatenate(out, axis=0)


def _shift_down(ext, k, halo):
    return pltpu.roll(ext, k, 0)[halo:] if k else ext[halo:]


def _shift_up(ext, k, ts):
    return pltpu.roll(ext, ext.shape[0] - k, 0)[:ts] if k else ext[:ts]


def _rows(ts, width, nt=None, col=0):
    if nt is None:
        return pl.BlockSpec((ts, width), lambda i: (i, col))
    return pl.BlockSpec((ts, width), lambda i: (nt - 1 - i, col))


def _resident(shape):
    zeros = (0,) * len(shape)
    return pl.BlockSpec(shape, lambda i: zeros, pipeline_mode=pl.Buffered(1))


def _acc(shape):
    zeros = (0,) * len(shape)
    return pl.BlockSpec(shape, lambda i: zeros)


def _params():
    return pltpu.CompilerParams(dimension_semantics=("arbitrary",), vmem_limit_bytes=VMEM_LIMIT)


def _sds(shape, dtype=F32):
    return jax.ShapeDtypeStruct(shape, dtype)


class _Task:
    def __init__(self, ins, out_shapes, aliases, sems, start, finish):
        self.ins, self.out_shapes, self.aliases, self.sems = list(ins), list(out_shapes), dict(aliases), list(sems)
        self.start, self.finish = start, finish


def _call(body, name, grid, in_specs, out_specs, out_shape, scratch_shapes, args, tasks=()):
    n_in, n_out, n_scr = len(in_specs), len(out_specs), len(scratch_shapes)
    t_in = [len(t.ins) for t in tasks]
    t_out = [len(t.out_shapes) for t in tasks]
    t_sem = [len(t.sems) for t in tasks]
    steps = 1
    for g in grid:
        steps *= g

    def take(refs, pos, counts):
        groups = []
        for c in counts:
            groups.append(refs[pos:pos + c])
            pos += c
        return groups, pos

    def wrapped(*refs):
        (cin,), pos = take(refs, 0, [n_in])
        tin, pos = take(refs, pos, t_in)
        (cout,), pos = take(refs, pos, [n_out])
        tout, pos = take(refs, pos, t_out)
        (cscr,), pos = take(refs, pos, [n_scr])
        tsem, pos = take(refs, pos, t_sem)
        if not grid:
            for t, a, b, c in zip(tasks, tin, tout, tsem):
                t.start(a, b, c)
            for t, a, b, c in zip(tasks, tin, tout, tsem):
                t.finish(a, b, c)
            return
        step = pl.program_id(0)
        for axis in range(1, len(grid)):
            step = step * grid[axis] + pl.program_id(axis)
        if tasks:
            @pl.when(step == 0)
            def _():
                for t, a, b, c in zip(tasks, tin, tout, tsem):
                    t.start(a, b, c)

        body(*cin, *cout, *cscr)
        if tasks:
            @pl.when(step == steps - 1)
            def _():
                for t, a, b, c in zip(tasks, tin, tout, tsem):
                    t.finish(a, b, c)

    aliases, in_pos, out_pos = {}, n_in, n_out
    for t, ni, no in zip(tasks, t_in, t_out):
        aliases.update({in_pos + a: out_pos + b for a, b in t.aliases.items()})
        in_pos, out_pos = in_pos + ni, out_pos + no
    any_spec = pl.BlockSpec(memory_space=pl.ANY)
    kwargs = dict(grid=grid, compiler_params=pltpu.CompilerParams(
        dimension_semantics=("arbitrary",) * len(grid), vmem_limit_bytes=VMEM_LIMIT)) if grid else {}
    out = pl.pallas_call(
        wrapped, name=name,
        in_specs=list(in_specs) + [any_spec] * sum(t_in),
        out_specs=list(out_specs) + [any_spec] * sum(t_out),
        out_shape=list(out_shape) + [s for t in tasks for s in t.out_shapes],
        scratch_shapes=list(scratch_shapes) + [s for t in tasks for s in t.sems],
        input_output_aliases=aliases, **kwargs,
    )(*args, *[a for t in tasks for a in t.ins])
    task_outs, pos = take(list(out), n_out, t_out)
    return list(out[:n_out]), task_outs


def _fwd_in_pool(x, g_pre, w_in, pool_w, pool_scale, w_pool_out, ts, tasks=()):
    s = x.shape[0]

    def body(x_ref, g_ref, win_ref, pw_ref, ps_ref, wpo_ref,
             urx_ref, urg_ref, gp_ref, gr_ref, d_ref, yp_ref, z_scr, halo_scr):
        i = pl.program_id(0)

        @pl.when(i == 0)
        def _():
            halo_scr[...] = jnp.zeros_like(halo_scr)

        h1, _, _ = _rms_fwd(x_ref[...], g_ref[...])
        h1 = h1.astype(BF16)
        for j in range(N_CHIPS):
            z_scr[:, j * IN_SHARD:(j + 1) * IN_SHARD] = jnp.dot(h1, win_ref[j], preferred_element_type=F32)
        urx_ref[...] = z_scr[:, 512:1536]
        urg_ref[...] = z_scr[:, 1536:2560]
        gp_ref[...] = z_scr[:, 2560:3584]
        gr_ref[...] = z_scr[:, 3584:4608]
        u = z_scr[:, 0:POOL_WIDTH]
        ext = jnp.concatenate([halo_scr[...], u], axis=0)
        halo_scr[...] = u[ts - POOL_HALO:, :]
        t = i * ts + lax.broadcasted_iota(jnp.int32, (ts, 1), 0)
        y4 = []
        for g, w in enumerate(POOL_WINDOWS):
            lanes = slice(g * POOL_GROUP_DIM, (g + 1) * POOL_GROUP_DIM)
            acc = ext[:, lanes]
            sh = 1
            while sh < w:
                acc = acc + pltpu.roll(acc, sh, 0)
                sh *= 2
            inv = 1.0 / jnp.minimum(t + 1, w).astype(F32)
            dg = acc[POOL_HALO:, :] * inv - u[:, lanes]
            d_ref[:, lanes] = dg
            y4.append(_dot(dg, pw_ref[g]))
        ypre = jnp.concatenate(y4, axis=1) * ps_ref[...]
        ypre = ypre.astype(BF16)
        for j in range(N_CHIPS):
            yp_ref[:, j * 256:(j + 1) * 256] = jnp.dot(ypre, wpo_ref[j], preferred_element_type=F32)

    return _call(
        body, "fwd_in_pool", (s // ts,),
        [_rows(ts, D_MODEL), _resident((1, D_MODEL)), _resident(w_in.shape), _resident(pool_w.shape),
         _resident((1, POOL_WIDTH)), _resident(w_pool_out.shape)],
        [_rows(ts, D_MODEL)] * 4 + [_rows(ts, POOL_WIDTH), _rows(ts, D_MODEL)],
        [_sds((s, D_MODEL))] * 4 + [_sds((s, POOL_WIDTH)), _sds((s, D_MODEL))],
        [pltpu.VMEM((ts, IN_TOTAL), F32), pltpu.VMEM((POOL_HALO, POOL_WIDTH), F32)],
        (x, g_pre, w_in, pool_w, pool_scale, w_pool_out), tasks)


def _fwd_rnn_merge(urx, urg, gp, gr, ypool, x, conv_w, conv_b, wg, bg, lam, w_rg_out, w_o, g_post, ts, tasks=()):
    s = x.shape[0]

    def body(urx_ref, urg_ref, gp_ref, gr_ref, yp_ref, x_ref, cw_ref, cb_ref, wg_ref, bg_ref, lam_ref, wrg_ref, wo_ref,
             gpost_ref, xc_ref, r_ref, ig_ref, h_ref, yr_ref, mo_ref, x1_ref, halo_scr, carry_scr):
        i = pl.program_id(0)

        @pl.when(i == 0)
        def _():
            halo_scr[...] = jnp.zeros_like(halo_scr)
            carry_scr[...] = jnp.zeros_like(carry_scr)

        urx_v = urx_ref[...]
        ext = jnp.concatenate([halo_scr[...], urx_v], axis=0)
        halo_scr[...] = urx_v[ts - CONV_HALO:, :]
        cw = cw_ref[...]
        xc = (cb_ref[...] + cw[3:4] * urx_v + cw[2:3] * _shift_down(ext, 1, CONV_HALO)
              + cw[1:2] * _shift_down(ext, 2, CONV_HALO) + cw[0:1] * _shift_down(ext, 3, CONV_HALO))
        xc_ref[...] = xc
        xcb = xc.astype(BF16)
        lin = []
        for gate in range(2):
            parts = [jnp.dot(xcb[:, q * GATE_BLOCK:(q + 1) * GATE_BLOCK], wg_ref[gate, q], preferred_element_type=F32)
                     for q in range(GATE_BLOCKS)]
            lin.append(jnp.concatenate(parts, axis=1) + bg_ref[gate:gate + 1, :])
        r = jax.nn.sigmoid(lin[0])
        ig = jax.nn.sigmoid(lin[1])
        r_ref[...] = r
        ig_ref[...] = ig
        first_row = (i * ts + lax.broadcasted_iota(jnp.int32, (ts, 1), 0)) == 0
        _, a, _, mult = _lru_coeffs(r, lam_ref[...], first_row)
        h = _scan_fwd(a, mult * ig * xc, carry_scr[0:1, :])
        carry_scr[0:1, :] = h[ts - 1:ts, :]
        h_ref[...] = h
        gl, _ = _gelu(urg_ref[...])
        yr = _dot(h * gl, wrg_ref[...])
        yr_ref[...] = yr
        merged = jax.nn.sigmoid(gp_ref[...]) * yp_ref[...] + jax.nn.sigmoid(gr_ref[...]) * yr
        mo = _dot(merged, wo_ref[...])
        mo_ref[...] = mo
        y, _, _ = _rms_fwd(mo, gpost_ref[...])
        x1_ref[...] = x_ref[...] + y

    row = _rows(ts, D_MODEL)
    return _call(
        body, "fwd_rnn_merge", (s // ts,),
        [row] * 6 + [_resident(conv_w.shape), _resident((1, D_MODEL)), _resident(wg.shape), _resident(bg.shape),
                     _resident((1, D_MODEL)), _resident(w_rg_out.shape), _resident(w_o.shape), _resident((1, D_MODEL))],
        [row] * 7, [_sds((s, D_MODEL))] * 7,
        [pltpu.VMEM((CONV_HALO, D_MODEL), F32), pltpu.VMEM((8, D_MODEL), F32)],
        (urx, urg, gp, gr, ypool, x, conv_w, conv_b, wg, bg, lam, w_rg_out, w_o, g_post), tasks)


def _fwd_ffn(x1, g_pre, w_up, fcw, fcb, w_down, g_post, ts):
    s = x1.shape[0]

    def body(x1_ref, g_ref, wup_ref, fcw_ref, fcb_ref, wd_ref, gpost_ref,
             up_ref, gl_ref, gg_ref, h2_ref, dn_ref, x2_ref, up_scr, halo_scr):
        i = pl.program_id(0)

        @pl.when(i == 0)
        def _():
            halo_scr[...] = jnp.zeros_like(halo_scr)

        x1_v = x1_ref[...]
        h2, _, _ = _rms_fwd(x1_v, g_ref[...])
        h2 = h2.astype(BF16)
        h2_ref[...] = h2
        for j in range(N_CHIPS):
            up_scr[:, j * UP_SHARD:(j + 1) * UP_SHARD] = jnp.dot(h2, wup_ref[j], preferred_element_type=F32)
        up_ref[...] = up_scr[...].astype(BF16)
        ug = up_scr[:, 0:D_FF]
        ext = jnp.concatenate([halo_scr[...], ug], axis=0)
        halo_scr[...] = ug[ts - CONV_HALO:, :]
        w = fcw_ref[...]
        gh = (fcb_ref[...] + w[2:3] * ug + w[1:2] * _shift_down(ext, 1, CONV_HALO)
              + w[0:1] * _shift_down(ext, 2, CONV_HALO))
        gl, t = _gelu(gh)
        gl_ref[...] = gl.astype(BF16)
        gg_ref[...] = _gelu_grad(gh, t).astype(BF16)
        dn = _dot(gl * up_scr[:, D_FF:], wd_ref[...])
        dn_ref[...] = dn
        y, _, _ = _rms_fwd(dn, gpost_ref[...])
        x2_ref[...] = x1_v + y

    row = _rows(ts, D_MODEL)
    return pl.pallas_call(
        body, name="fwd_ffn", grid=(s // ts,),
        in_specs=[row, _resident((1, D_MODEL)), _resident(w_up.shape), _resident(fcw.shape), _resident((1, D_FF)),
                  _resident(w_down.shape), _resident((1, D_MODEL))],
        out_specs=[_rows(ts, 2 * D_FF), _rows(ts, D_FF), _rows(ts, D_FF), row, row, row],
        out_shape=[_sds((s, 2 * D_FF), BF16), _sds((s, D_FF), BF16), _sds((s, D_FF), BF16), _sds((s, D_MODEL), BF16),
                   _sds((s, D_MODEL)), _sds((s, D_MODEL))],
        scratch_shapes=[pltpu.VMEM((ts, 2 * D_FF), F32), pltpu.VMEM((CONV_HALO, D_FF), F32)],
        compiler_params=_params(),
    )(x1, g_pre, w_up, fcw, fcb, w_down, g_post)


def _ple_loss(x2, p, tgt, g_gate, w_gate, w_proj, g_post, ts):
    s = x2.shape[0]

    def body(x2_ref, p_ref, t_ref, gg_ref, wg_ref, wp_ref, gp_ref, dx2_ref, loss_ref, dwg_ref, dwp_ref, dgg_ref, dgp_ref):
        @pl.when(pl.program_id(0) == 0)
        def _():
            loss_ref[...] = jnp.zeros_like(loss_ref)
            dwg_ref[...] = jnp.zeros_like(dwg_ref)
            dwp_ref[...] = jnp.zeros_like(dwp_ref)
            dgg_ref[...] = jnp.zeros_like(dgg_ref)
            dgp_ref[...] = jnp.zeros_like(dgp_ref)

        x2_v = x2_ref[...]
        n3, xh3, r3 = _rms_fwd(x2_v, gg_ref[...])
        pg = jax.nn.sigmoid(_dot(n3, wg_ref[...]))
        pb = p_ref[...].astype(BF16)
        q = jnp.concatenate([jnp.dot(pb, wp_ref[j], preferred_element_type=F32) for j in range(N_CHIPS)], axis=1)
        ple, qh, rq = _rms_fwd(q, gp_ref[...])
        e = x2_v + pg * ple - t_ref[...]
        loss_ref[...] += 0.5 * jnp.sum(jnp.mean(e * e, axis=-1, keepdims=True), axis=0, keepdims=True)
        dy = e * (1.0 / D_MODEL)
        dpgl = dy * ple * pg * (1.0 - pg)
        dwg_ref[...] += _dot_tn(n3, dpgl)
        dx3, dgg = _rms_bwd(xh3, r3, gg_ref[...], _dot_nt(dpgl, wg_ref[...]))
        dgg_ref[...] += dgg
        dq, dgp = _rms_bwd(qh, rq, gp_ref[...], dy * pg)
        dgp_ref[...] += dgp
        for j in range(N_CHIPS):
            dwp_ref[j] += _dot_tn(pb, dq[:, j * 256:(j + 1) * 256])
        dx2_ref[...] = dy + dx3

    row = _rows(ts, D_MODEL)
    vec = _acc((1, D_MODEL))
    return pl.pallas_call(
        body, name="ple_loss", grid=(s // ts,),
        in_specs=[row, _rows(ts, PLE_DIM), row, _resident((1, D_MODEL)), _resident(w_gate.shape), _resident(w_proj.shape),
                  _resident((1, D_MODEL))],
        out_specs=[row, _acc((1, 128)), _acc(w_gate.shape), _acc(w_proj.shape), vec, vec],
        out_shape=[_sds((s, D_MODEL)), _sds((1, 128)), _sds(w_gate.shape), _sds(w_proj.shape), _sds((1, D_MODEL)),
                   _sds((1, D_MODEL))],
        compiler_params=_params(),
    )(x2, p, tgt, g_gate, w_gate, w_proj, g_post)


def _bwd_ffn_down(dx2, dn, up, gl, gg, fcw, w_down, g_post, ts):
    s = dx2.shape[0]
    nt = s // ts

    def body(dx2_ref, dn_ref, up_ref, gl_ref, gg_ref, fcw_ref, wd_ref, gpost_ref,
             dup_ref, dwd_ref, dfcw_ref, dfcb_ref, dgp_ref, carry_scr):
        i = pl.program_id(0)

        @pl.when(i == 0)
        def _():
            carry_scr[...] = jnp.zeros_like(carry_scr)
            dwd_ref[...] = jnp.zeros_like(dwd_ref)
            dfcw_ref[...] = jnp.zeros_like(dfcw_ref)
            dfcb_ref[...] = jnp.zeros_like(dfcb_ref)
            dgp_ref[...] = jnp.zeros_like(dgp_ref)

        _, xh, r = _rms_fwd(dn_ref[...], gpost_ref[...])
        ddn, dgp = _rms_bwd(xh, r, gpost_ref[...], dx2_ref[...])
        dgp_ref[...] += dgp
        dhid = _dot_nt(ddn, wd_ref[...])
        ug = up_ref[:, 0:D_FF].astype(F32)
        uv = up_ref[:, D_FF:].astype(F32)
        gl = gl_ref[...].astype(F32)
        w = fcw_ref[...]
        dwd_ref[...] += _dot_tn(gl * uv, ddn)
        dgh = dhid * uv * gg_ref[...].astype(F32)
        dup_ref[:, D_FF:] = (dhid * gl).astype(BF16)
        extd = jnp.concatenate([dgh, carry_scr[...]], axis=0)
        carry_scr[...] = dgh[0:CONV_HALO, :]
        d1 = _shift_up(extd, 1, ts)
        d2 = _shift_up(extd, 2, ts)
        dup_ref[:, 0:D_FF] = (w[2:3] * dgh + w[1:2] * d1 + w[0:1] * d2).astype(BF16)
        dfcw_ref[2:3, :] += jnp.sum(ug * dgh, axis=0, keepdims=True)
        dfcw_ref[1:2, :] += jnp.sum(ug * d1, axis=0, keepdims=True)
        dfcw_ref[0:1, :] += jnp.sum(ug * d2, axis=0, keepdims=True)
        dfcb_ref[...] += jnp.sum(dgh, axis=0, keepdims=True)

    row = _rows(ts, D_MODEL, nt)
    wide = _rows(ts, D_FF, nt)
    return pl.pallas_call(
        body, name="bwd_ffn_down", grid=(nt,),
        in_specs=[row, row, _rows(ts, 2 * D_FF, nt), wide, wide, _resident(fcw.shape), _resident(w_down.shape),
                  _resident((1, D_MODEL))],
        out_specs=[_rows(ts, 2 * D_FF, nt), _acc(w_down.shape), _acc(fcw.shape), _acc((1, D_FF)), _acc((1, D_MODEL))],
        out_shape=[_sds((s, 2 * D_FF), BF16), _sds(w_down.shape), _sds(fcw.shape), _sds((1, D_FF)), _sds((1, D_MODEL))],
        scratch_shapes=[pltpu.VMEM((CONV_HALO, D_FF), F32)],
        compiler_params=_params(),
    )(dx2, dn, up, gl, gg, fcw, w_down, g_post)


def _bwd_ffn_up(dup, x1, dx2, g_pre, w_up, ts, tasks=()):
    s = x1.shape[0]

    def body(dup_ref, x1_ref, dx2_ref, g_ref, wup_ref, dx1_ref, dg_ref):
        @pl.when(pl.program_id(0) == 0)
        def _():
            dg_ref[...] = jnp.zeros_like(dg_ref)

        _, xh, r = _rms_fwd(x1_ref[...], g_ref[...])
        dh2 = _dot_nt(dup_ref[:, 0:UP_SHARD], wup_ref[0])
        for j in range(1, N_CHIPS):
            dh2 = dh2 + _dot_nt(dup_ref[:, j * UP_SHARD:(j + 1) * UP_SHARD], wup_ref[j])
        dx, dg = _rms_bwd(xh, r, g_ref[...], dh2)
        dg_ref[...] += dg
        dx1_ref[...] = dx2_ref[...] + dx

    row = _rows(ts, D_MODEL)
    return _call(
        body, "bwd_ffn_up", (s // ts,),
        [_rows(ts, 2 * D_FF), row, row, _resident((1, D_MODEL)), _resident(w_up.shape)],
        [row, _acc((1, D_MODEL))], [_sds((s, D_MODEL)), _sds((1, D_MODEL))], [],
        (dup, x1, dx2, g_pre, w_up), tasks)


def _dw_up(h2, dup, ts, tasks=()):
    s = h2.shape[0]
    ts = min(DW_TILES * ts, s)

    def body(h2_ref, dup_ref, out_ref):
        @pl.when(pl.program_id(1) == 0)
        def _():
            out_ref[...] = jnp.zeros_like(out_ref)

        out_ref[0] += _dot_tn(h2_ref[...], dup_ref[...])

    return _call(
        body, "dw_up", (N_CHIPS, s // ts),
        [pl.BlockSpec((ts, D_MODEL), lambda j, i: (i, 0)), pl.BlockSpec((ts, UP_SHARD), lambda j, i: (i, j))],
        [pl.BlockSpec((1, D_MODEL, UP_SHARD), lambda j, i: (j, 0, 0))], [_sds((N_CHIPS, D_MODEL, UP_SHARD))], [],
        (h2, dup), tasks)


def _bwd_merge(dx1, mo, gp, gr, ypool, yrnn, g_post, w_o, ts, tasks=()):
    s = dx1.shape[0]

    def body(dx1_ref, mo_ref, gp_ref, gr_ref, yp_ref, yr_ref, g_ref, wo_ref,
             dgp_ref, dgr_ref, dyp_ref, dyr_ref, dwo_ref, dg_ref):
        @pl.when(pl.program_id(0) == 0)
        def _():
            dwo_ref[...] = jnp.zeros_like(dwo_ref)
            dg_ref[...] = jnp.zeros_like(dg_ref)

        _, xh, r = _rms_fwd(mo_ref[...], g_ref[...])
        dmo, dg = _rms_bwd(xh, r, g_ref[...], dx1_ref[...])
        dg_ref[...] += dg
        dmerged = _dot_nt(dmo, wo_ref[...])
        sp = jax.nn.sigmoid(gp_ref[...])
        sr = jax.nn.sigmoid(gr_ref[...])
        yp = yp_ref[...]
        yr = yr_ref[...]
        dwo_ref[...] += _dot_tn(sp * yp + sr * yr, dmo)
        dgp_ref[...] = (dmerged * yp * sp * (1.0 - sp)).astype(BF16)
        dgr_ref[...] = (dmerged * yr * sr * (1.0 - sr)).astype(BF16)
        dyp_ref[...] = (dmerged * sp).astype(BF16)
        dyr_ref[...] = (dmerged * sr).astype(BF16)

    row = _rows(ts, D_MODEL)
    return _call(
        body, "bwd_merge", (s // ts,),
        [row] * 6 + [_resident((1, D_MODEL)), _resident(w_o.shape)],
        [row] * 4 + [_acc(w_o.shape), _acc((1, D_MODEL))],
        [_sds((s, D_MODEL), BF16)] * 4 + [_sds(w_o.shape), _sds((1, D_MODEL))], [],
        (dx1, mo, gp, gr, ypool, yrnn, g_post, w_o), tasks)


def _bwd_rnn(dyr, urx, urg, xc, r, ig, h, conv_w, wg, lam, w_rg_out, ts, tasks=()):
    s = urx.shape[0]
    nt = s // ts
    halo_blocks = ts // CONV_HALO

    def body(dyr_ref, urx_ref, urg_ref, xc_ref, r_ref, ig_ref, h_ref, hh_ref, cw_ref, wg_ref, lam_ref, wrg_ref,
             durx_ref, durg_ref, dwrg_ref, dwg_ref, dcw_ref, dcb_ref, dbg_ref, dlam_ref, mu_scr, carry_scr):
        i = pl.program_id(0)
        k = nt - 1 - i

        @pl.when(i == 0)
        def _():
            mu_scr[...] = jnp.zeros_like(mu_scr)
            carry_scr[...] = jnp.zeros_like(carry_scr)
            dwrg_ref[...] = jnp.zeros_like(dwrg_ref)
            dwg_ref[...] = jnp.zeros_like(dwg_ref)
            dcw_ref[...] = jnp.zeros_like(dcw_ref)
            dcb_ref[...] = jnp.zeros_like(dcb_ref)
            dbg_ref[...] = jnp.zeros_like(dbg_ref)
            dlam_ref[...] = jnp.zeros_like(dlam_ref)

        row = lax.broadcasted_iota(jnp.int32, (ts, 1), 0)
        first_row = (k * ts + row) == 0
        h = h_ref[...]
        urg_v = urg_ref[...]
        dyr_v = dyr_ref[...]
        dhr = _dot_nt(dyr_v, wrg_ref[...])
        gl, t = _gelu(urg_v)
        dwrg_ref[...] += _dot_tn(h * gl, dyr_v)
        durg_ref[...] = (dhr * h * _gelu_grad(urg_v, t)).astype(BF16)
        r_v = r_ref[...]
        ig_v = ig_ref[...]
        xc_v = xc_ref[...]
        lam_v = lam_ref[...]
        c8, a, m2, mult = _lru_coeffs(r_v, lam_v, first_row)
        b = jnp.where(row == ts - 1, 1.0, pltpu.roll(a, ts - 1, 0))
        lt = _scan_bwd(b, dhr * gl, mu_scr[0:1, :])
        mu_scr[0:1, :] = a[0:1, :] * lt[0:1, :]
        h_before = jnp.where(k > 0, hh_ref[CONV_HALO - 1:CONV_HALO, :], 0.0)
        hprev = jnp.where(row == 0, h_before, pltpu.roll(h, 1, 0))
        dmult = lt * ig_v * xc_v
        da = lt * hprev - jnp.where(first_row, 0.0, dmult * a * lax.rsqrt(m2))
        dla = da * a
        dlam_ref[...] += jnp.sum(dla * r_v, axis=0, keepdims=True)
        dlr = (dla * (-c8)) * r_v * (1.0 - r_v)
        dli = (lt * mult * xc_v) * ig_v * (1.0 - ig_v)
        dbg_ref[0:1, :] += jnp.sum(dlr, axis=0, keepdims=True)
        dbg_ref[1:2, :] += jnp.sum(dli, axis=0, keepdims=True)
        xcb = xc_v.astype(BF16)
        parts = []
        for q in range(GATE_BLOCKS):
            blk = slice(q * GATE_BLOCK, (q + 1) * GATE_BLOCK)
            dlr_q = dlr[:, blk].astype(BF16)
            dli_q = dli[:, blk].astype(BF16)
            parts.append(_dot_nt(dlr_q, wg_ref[0, q]) + _dot_nt(dli_q, wg_ref[1, q]))
            dwg_ref[0, q] += _dot_tn(xcb[:, blk], dlr_q)
            dwg_ref[1, q] += _dot_tn(xcb[:, blk], dli_q)
        dxc = lt * mult * ig_v + jnp.concatenate(parts, axis=1)
        extd = jnp.concatenate([dxc, carry_scr[...]], axis=0)
        carry_scr[...] = dxc[0:CONV_HALO, :]
        cw = cw_ref[...]
        urx_v = urx_ref[...]
        durx = cw[3:4] * dxc
        dcw_ref[3:4, :] += jnp.sum(urx_v * dxc, axis=0, keepdims=True)
        for j in (1, 2, 3):
            dj = _shift_up(extd, j, ts)
            durx = durx + cw[3 - j:4 - j] * dj
            dcw_ref[3 - j:4 - j, :] += jnp.sum(urx_v * dj, axis=0, keepdims=True)
        durx_ref[...] = durx.astype(BF16)
        dcb_ref[...] += jnp.sum(dxc, axis=0, keepdims=True)

        @pl.when(i == nt - 1)
        def _():
            dlam_ref[...] = dlam_ref[...] * (LRU_C * jax.nn.sigmoid(-lam_v))

    row_spec = _rows(ts, D_MODEL, nt)
    halo_spec = pl.BlockSpec((CONV_HALO, D_MODEL), lambda i: (jnp.maximum((nt - 1 - i) * halo_blocks - 1, 0), 0))
    vec = _acc((1, D_MODEL))
    return _call(
        body, "bwd_rnn", (nt,),
        [row_spec] * 7 + [halo_spec, _resident(conv_w.shape), _resident(wg.shape), _resident((1, D_MODEL)),
                          _resident(w_rg_out.shape)],
        [row_spec, row_spec, _acc(w_rg_out.shape), _acc(wg.shape), _acc(conv_w.shape), vec, _acc((2, D_MODEL)), vec],
        [_sds((s, D_MODEL), BF16), _sds((s, D_MODEL), BF16), _sds(w_rg_out.shape), _sds(wg.shape), _sds(conv_w.shape),
         _sds((1, D_MODEL)), _sds((2, D_MODEL)), _sds((1, D_MODEL))],
        [pltpu.VMEM((8, D_MODEL), F32), pltpu.VMEM((CONV_HALO, D_MODEL), F32)],
        (dyr, urx, urg, xc, r, ig, h, h, conv_w, wg, lam, w_rg_out), tasks)


def _bwd_pool_in(dyp, d, durx, durg, dgp, dgr, x, dx1, g_pre, w_in, pool_w, pool_scale, w_pool_out, ts, tasks=()):
    s = x.shape[0]
    nt = s // ts

    def body(dyp_ref, d_ref, durx_ref, durg_ref, dgp_ref, dgr_ref, x_ref, dx1_ref, g_ref, win_ref, pw_ref, ps_ref,
             wpo_ref, gx_ref, dwin_ref, dwpo_ref, dpw_ref, dps_ref, dg_ref, dz_scr, carry_scr):
        i = pl.program_id(0)
        k = nt - 1 - i

        @pl.when(i == 0)
        def _():
            carry_scr[...] = jnp.zeros_like(carry_scr)
            dwin_ref[...] = jnp.zeros_like(dwin_ref)
            dwpo_ref[...] = jnp.zeros_like(dwpo_ref)
            dpw_ref[...] = jnp.zeros_like(dpw_ref)
            dps_ref[...] = jnp.zeros_like(dps_ref)
            dg_ref[...] = jnp.zeros_like(dg_ref)

        dyp_v = dyp_ref[...]
        d_v = d_ref[...]
        ps = ps_ref[...]
        dypre = _dot_nt(dyp_v[:, 0:256], wpo_ref[0])
        for j in range(1, N_CHIPS):
            dypre = dypre + _dot_nt(dyp_v[:, j * 256:(j + 1) * 256], wpo_ref[j])
        y4 = jnp.concatenate([_dot(d_v[:, g * 128:(g + 1) * 128], pw_ref[g]) for g in range(POOL_GROUPS)], axis=1)
        ypre = (y4 * ps).astype(BF16)
        for j in range(N_CHIPS):
            dwpo_ref[j] += _dot_tn(ypre, dyp_v[:, j * 256:(j + 1) * 256])
        dps_ref[...] += jnp.sum(dypre * y4, axis=0, keepdims=True)
        dy4 = dypre * ps
        t = k * ts + lax.broadcasted_iota(jnp.int32, (ts, 1), 0)
        for g, w in enumerate(POOL_WINDOWS):
            lanes = slice(g * POOL_GROUP_DIM, (g + 1) * POOL_GROUP_DIM)
            dd = _dot_nt(dy4[:, lanes], pw_ref[g])
            dpw_ref[g] += _dot_tn(d_v[:, lanes], dy4[:, lanes])
            e = dd * (1.0 / jnp.minimum(t + 1, w).astype(F32))
            acc = jnp.concatenate([e, carry_scr[:, lanes]], axis=0)
            carry_scr[:, lanes] = e[0:POOL_HALO, :]
            n = ts + POOL_HALO
            sh = 1
            while sh < w:
                acc = acc + pltpu.roll(acc, n - sh, 0)
                sh *= 2
            dz_scr[:, lanes] = (acc[:ts, :] - dd).astype(BF16)
        dz_scr[:, 512:1536] = durx_ref[...]
        dz_scr[:, 1536:2560] = durg_ref[...]
        dz_scr[:, 2560:3584] = dgp_ref[...]
        dz_scr[:, 3584:4608] = dgr_ref[...]
        h1, xh, r = _rms_fwd(x_ref[...], g_ref[...])
        h1 = h1.astype(BF16)
        dh1 = _dot_nt(dz_scr[:, 0:IN_SHARD], win_ref[0])
        dwin_ref[0] += _dot_tn(h1, dz_scr[:, 0:IN_SHARD])
        for j in range(1, N_CHIPS):
            cols = slice(j * IN_SHARD, (j + 1) * IN_SHARD)
            dh1 = dh1 + _dot_nt(dz_scr[:, cols], win_ref[j])
            dwin_ref[j] += _dot_tn(h1, dz_scr[:, cols])
        dx, dg = _rms_bwd(xh, r, g_ref[...], dh1)
        dg_ref[...] += dg
        gx_ref[...] = dx1_ref[...] + dx

    row = _rows(ts, D_MODEL, nt)
    return _call(
        body, "bwd_pool_in", (nt,),
        [row, _rows(ts, POOL_WIDTH, nt)] + [row] * 6 + [_resident((1, D_MODEL)), _resident(w_in.shape),
                                                       _resident(pool_w.shape), _resident((1, POOL_WIDTH)),
                                                       _resident(w_pool_out.shape)],
        [row, _acc(w_in.shape), _acc(w_pool_out.shape), _acc(pool_w.shape), _acc((1, POOL_WIDTH)), _acc((1, D_MODEL))],
        [_sds((s, D_MODEL)), _sds(w_in.shape), _sds(w_pool_out.shape), _sds(pool_w.shape), _sds((1, POOL_WIDTH)),
         _sds((1, D_MODEL))],
        [pltpu.VMEM((ts, IN_TOTAL), BF16), pltpu.VMEM((POOL_HALO, POOL_WIDTH), F32)],
        (dyp, d, durx, durg, dgp, dgr, x, dx1, g_pre, w_in, pool_w, pool_scale, w_pool_out), tasks)


def _place():
    x, y, c = lax.axis_index("x"), lax.axis_index("y"), lax.axis_index("c")
    others = [(1 - x, y), (x, 1 - y), (1 - x, 1 - y)]
    return x, y, c, 2 * x + y, others


def _remote(src, dst, send_sem, recv_sem, to):
    return pltpu.make_async_remote_copy(src_ref=src, dst_ref=dst, send_sem=send_sem, recv_sem=recv_sem,
                                        device_id=to, device_id_type=MESH)


def _own_slot(w, place, dtype):
    a, b = w.shape
    tr = _row_tile(a, b)

    def body(place_ref, w_ref, out_ref):
        out_ref[0] = w_ref[...].astype(dtype)

    return pl.pallas_call(
        body, name="own_slot",
        grid_spec=pltpu.PrefetchScalarGridSpec(
            num_scalar_prefetch=1, grid=(a // tr,),
            in_specs=[pl.BlockSpec((tr, b), lambda i, s: (i, 0))],
            out_specs=pl.BlockSpec((1, tr, b), lambda i, s: (s[0], i, 0))),
        out_shape=_sds((N_CHIPS, a, b), dtype),
        compiler_params=pltpu.CompilerParams(dimension_semantics=("arbitrary",)),
    )(place, w)


def _run(task, name):
    return _call(None, name, (), [], [], [], [], (), (task,))[1][0]


def _gather_task(bufs):
    n = len(bufs)

    def first_copy(out, w, j, ox, oy, sems):
        x, y, c, me, _ = _place()
        ah = out.shape[1] // 2
        mine = out.at[me, pl.ds(c * ah, ah)]
        return _remote(mine, mine, sems[0].at[w, j], sems[1].at[w, j], (ox, oy, c))

    def passed_copy(out, w, j, ox, oy, sems, half):
        x, y, c, _, _ = _place()
        ah = out.shape[1] // 2
        slot = out.at[2 * ox + oy, pl.ds(half * ah, ah)]
        return _remote(slot, slot, sems[0].at[w, 3 + j], sems[1].at[w, 3 + j], (x, y, 1 - c))

    def start(ins, outs, sems):
        others = _place()[4]
        for w, out in enumerate(outs):
            for j, (ox, oy) in enumerate(others):
                first_copy(out, w, j, ox, oy, sems).start()

    def finish(ins, outs, sems):
        x, y, c, _, others = _place()
        for w, out in enumerate(outs):
            ah = out.shape[1] // 2
            for j, (ox, oy) in enumerate(others):
                slot = out.at[2 * ox + oy, pl.ds(c * ah, ah)]
                _remote(slot, slot, sems[0].at[w, j], sems[1].at[w, j], (ox, oy, c)).wait_recv()
                passed_copy(out, w, j, ox, oy, sems, c).start()
        for w, out in enumerate(outs):
            for j, (ox, oy) in enumerate(others):
                passed_copy(out, w, j, ox, oy, sems, 1 - c).wait_recv()
        for w, out in enumerate(outs):
            for j, (ox, oy) in enumerate(others):
                first_copy(out, w, j, ox, oy, sems).wait_send()
                passed_copy(out, w, j, ox, oy, sems, c).wait_send()

    return _Task(bufs, [_sds(b.shape, b.dtype) for b in bufs], {i: i for i in range(n)},
                 [pltpu.SemaphoreType.DMA((n, 6)), pltpu.SemaphoreType.DMA((n, 6))], start, finish)


def _halves_task(grads):
    n = len(grads)

    def copy(src, out, w, sems):
        x, y, c, _, _ = _place()
        ah = out.shape[1]
        return _remote(src.at[:, pl.ds((1 - c) * ah, ah)], out, sems[0].at[w], sems[1].at[w], (x, y, 1 - c))

    def start(ins, outs, sems):
        for w, (src, out) in enumerate(zip(ins, outs)):
            copy(src, out, w, sems).start()

    def finish(ins, outs, sems):
        for w, (src, out) in enumerate(zip(ins, outs)):
            copy(src, out, w, sems).wait()

    return _Task(grads, [_sds((g.shape[0], g.shape[1] // 2, g.shape[2]), g.dtype) for g in grads], {},
                 [pltpu.SemaphoreType.DMA((n,)), pltpu.SemaphoreType.DMA((n,))], start, finish)


def _exchange_task(sends, accs):
    n = len(accs)
    given = [s for s in sends if s is not None]

    def copies(ins, outs, sems):
        send_refs = iter(ins[:len(given)])
        srcs = [next(send_refs) if s is not None else None for s in sends]
        x, y, c, me, others = _place()
        for w, out in enumerate(outs):
            for j, (ox, oy) in enumerate(others):
                src = out.at[me] if srcs[w] is None else srcs[w].at[2 * ox + oy]
                yield _remote(src, out.at[me], sems[0].at[w, j], sems[1].at[w, j], (ox, oy, c))

    def start(ins, outs, sems):
        for cp in copies(ins, outs, sems):
            cp.start()

    def finish(ins, outs, sems):
        x, y, c, _, others = _place()
        for w, out in enumerate(outs):
            for j, (ox, oy) in enumerate(others):
                slot = out.at[2 * ox + oy]
                _remote(slot, slot, sems[0].at[w, j], sems[1].at[w, j], (ox, oy, c)).wait_recv()
        for cp in copies(ins, outs, sems):
            cp.wait_send()

    return _Task(given + list(accs), [_sds(a.shape, a.dtype) for a in accs], {len(given) + i: i for i in range(n)},
                 [pltpu.SemaphoreType.DMA((n, 3)), pltpu.SemaphoreType.DMA((n, 3))], start, finish)


def _share_task(shares):
    n = len(shares)

    def copy(out, w, sems, slot):
        x, y, c, _, _ = _place()
        return _remote(out.at[slot], out.at[slot], sems[0].at[w], sems[1].at[w], (x, y, 1 - c))

    def start(ins, outs, sems):
        c = _place()[2]
        for w, out in enumerate(outs):
            copy(out, w, sems, c).start()

    def finish(ins, outs, sems):
        c = _place()[2]
        for w, out in enumerate(outs):
            copy(out, w, sems, 1 - c).wait_recv()
        for w, out in enumerate(outs):
            copy(out, w, sems, c).wait_send()

    return _Task(shares, [_sds(s.shape, s.dtype) for s in shares], {i: i for i in range(n)},
                 [pltpu.SemaphoreType.DMA((n,)), pltpu.SemaphoreType.DMA((n,))], start, finish)


TILE_BYTES = 2 * 1024 * 1024
PARTIAL_TILE_BYTES = 512 * 1024


def _row_tile(rows, cols, limit=TILE_BYTES):
    best = 8
    for tr in range(8, rows + 1, 8):
        if rows % tr == 0 and tr * cols * 4 <= limit:
            best = tr
    assert rows % best == 0, (rows, cols)
    return best


def _chip_partial(g, got, place, wire_dtype):
    ns, ah, b = got.shape
    sharded = ns == N_CHIPS
    tr = _row_tile(ah, b, PARTIAL_TILE_BYTES)
    nb = ah // tr

    def body(place_ref, *refs):
        g_refs, got_refs, outs = refs[:ns], refs[ns:2 * ns], refs[2 * ns:]
        parts = [g_refs[k][0] + got_refs[k][0] for k in range(ns)]
        own = parts[0]
        if sharded:
            for k in range(ns):
                outs[0][k] = parts[k].astype(wire_dtype)
                if k:
                    own = jnp.where(place_ref[0] == k, parts[k], own)
        outs[-1][0] = own.astype(wire_dtype)

    blk = (1, tr, b)
    in_specs = ([pl.BlockSpec(blk, lambda i, s, k=k: (k, s[1] * nb + i, 0)) for k in range(ns)]
                + [pl.BlockSpec(blk, lambda i, s, k=k: (k, i, 0)) for k in range(ns)])
    acc_spec = pl.BlockSpec(blk, lambda i, s: (s[0], i, 0))
    acc_shape = _sds((N_CHIPS, ah, b), wire_dtype)
    out = pl.pallas_call(
        body, name="grad_chip_partial",
        grid_spec=pltpu.PrefetchScalarGridSpec(
            num_scalar_prefetch=1, grid=(nb,), in_specs=in_specs,
            out_specs=[pl.BlockSpec((ns, tr, b), lambda i, s: (0, i, 0)), acc_spec] if sharded else [acc_spec]),
        out_shape=[acc_shape, acc_shape] if sharded else [acc_shape],
        compiler_params=pltpu.CompilerParams(dimension_semantics=("arbitrary",), vmem_limit_bytes=32 * 1024 * 1024),
    )(place, *([g] * ns), *([got] * ns))
    return (out[0], out[1]) if sharded else (None, out[0])


def _chip_sum(acc, place):
    _, ah, b = acc.shape
    tr = _row_tile(ah, b)

    def body(place_ref, p_ref, out_ref):
        total = p_ref[0].astype(F32) + p_ref[1].astype(F32)
        total = total + p_ref[2].astype(F32)
        out_ref[0] = total + p_ref[3].astype(F32)

    return pl.pallas_call(
        body, name="grad_chip_sum",
        grid_spec=pltpu.PrefetchScalarGridSpec(
            num_scalar_prefetch=1, grid=(ah // tr,),
            in_specs=[pl.BlockSpec((N_CHIPS, tr, b), lambda i, s: (0, i, 0))],
            out_specs=pl.BlockSpec((1, tr, b), lambda i, s: (s[1], i, 0))),
        out_shape=_sds((2, ah, b)),
        compiler_params=pltpu.CompilerParams(dimension_semantics=("arbitrary",)),
    )(place, acc)


def _adam_math(w, g, m, v):
    nm = ADAM_B1 * m + (1.0 - ADAM_B1) * g
    nv = ADAM_B2 * v + (1.0 - ADAM_B2) * (g * g)
    m_hat = nm / (1.0 - ADAM_B1 ** ADAM_STEP)
    v_hat = nv / (1.0 - ADAM_B2 ** ADAM_STEP)
    return -ADAM_LR * (m_hat / (jnp.sqrt(v_hat) + ADAM_EPS) + ADAM_WD * w), nm, nv


def _adamw(w, g, m, v):
    a, b = w.shape
    tr = _row_tile(a, b)

    def body(w_ref, g_ref, m_ref, v_ref, d_ref, nm_ref, nv_ref):
        d_ref[...], nm_ref[...], nv_ref[...] = _adam_math(w_ref[...], g_ref[...], m_ref[...], v_ref[...])

    blk = pl.BlockSpec((tr, b), lambda i: (i, 0))
    return pl.pallas_call(
        body, name="adamw", grid=(a // tr,),
        in_specs=[blk] * 4, out_specs=[blk] * 3, out_shape=[_sds((a, b))] * 3,
        compiler_params=pltpu.CompilerParams(dimension_semantics=("arbitrary",)),
    )(w, g, m, v)


def _adamw_pieces(g, pieces, name):
    n = len(pieces)

    def body(g_ref, *refs):
        ins, outs = refs[:3 * n], refs[3 * n:]
        for i, piece in enumerate(pieces):
            w_ref, m_ref, v_ref = ins[3 * i:3 * i + 3]
            o_g, o_d, o_m, o_v = outs[4 * i:4 * i + 4]
            if len(piece) == 5:
                g_v = g_ref[piece[3], piece[4]]
                o_g[...] = g_v
                o_d[...], o_m[...], o_v[...] = _adam_math(w_ref[...], g_v, m_ref[...], v_ref[...])
            else:
                for r in range(w_ref.shape[1] // SMALL_COLS):
                    lanes = slice(r * SMALL_COLS, (r + 1) * SMALL_COLS)
                    g_v = g_ref[piece[3] + r:piece[3] + r + 1, :]
                    o_g[:, lanes] = g_v
                    o_d[:, lanes], o_m[:, lanes], o_v[:, lanes] = _adam_math(w_ref[:, lanes], g_v, m_ref[:, lanes],
                                                                            v_ref[:, lanes])

    operands = [t for piece in pieces for t in piece[:3]]
    out = pl.pallas_call(
        body, name=name,
        out_shape=[_sds(piece[0].shape) for piece in pieces for _ in range(4)],
    )(g, *operands)
    return [tuple(out[4 * i:4 * i + 4]) for i in range(n)]


TINY_ROWS, TINY_COLS = 16, 768
SMALL_COLS = 128
SMALL_ROWS = 624


def _pack_tiny(conv_w, b_gates, fcw):
    ns = conv_w.shape[0]
    pad = lambda t: jnp.pad(t, ((0, 0), (0, 0), (0, TINY_COLS - t.shape[2])))
    z = lambda rows: jnp.zeros((ns, rows, TINY_COLS), F32)
    return jnp.concatenate([pad(conv_w), pad(b_gates), z(2), fcw, z(5)], axis=1)


def _unpack_tiny(t):
    return t[:, 0:4, 0:256], t[:, 4:6, 0:256], t[:, 8:11, :]


def _cols_to_shards(t, n):
    return t.reshape(t.shape[0], N_CHIPS, n).transpose(1, 0, 2)


def _shards_to_cols(t):
    return t.transpose(1, 0, 2).reshape(t.shape[1], -1)


_VECTORS = ("g_mix_pre", "g_mix_post", "conv_b", "lru_lambda", "g_ffn_pre", "g_ffn_post", "g_ple_gate", "g_ple_post",
            "pool_scale", "ffn_conv_b")
_VECTOR_LEN = {"pool_scale": POOL_WIDTH, "ffn_conv_b": D_FF}
POOL_W_ROWS = POOL_GROUPS * POOL_GROUP_DIM


def _vector_rows():
    rows, row = {}, POOL_W_ROWS
    for k in _VECTORS:
        rows[k] = row
        row += max(8, _VECTOR_LEN.get(k, D_MODEL) // SMALL_COLS)
    return rows, row


def _pack_small(grads, loss):
    tiles = lambda t: jnp.pad(t, ((0, -t.shape[0] % 8), (0, 0)))
    parts = [grads["pool_w"].reshape(POOL_W_ROWS, SMALL_COLS)] + [tiles(grads[k].reshape(-1, SMALL_COLS)) for k in _VECTORS]
    parts.append(tiles(loss))
    used = sum(t.shape[0] for t in parts)
    return jnp.concatenate(parts + [jnp.zeros((SMALL_ROWS - used, SMALL_COLS), F32)], axis=0)


def _gates_block_diag(w):
    w4 = w.reshape(2, GATE_BLOCKS, 4, RNN_HEAD_DIM, RNN_HEAD_DIM)
    eye = jnp.eye(4, dtype=w.dtype)
    return jnp.einsum("gqhij,hk->gqhikj", w4, eye).reshape(2, GATE_BLOCKS, GATE_BLOCK, GATE_BLOCK)


def _gates_from_block_diag(dw):
    d6 = dw.reshape(2, GATE_BLOCKS, 4, RNN_HEAD_DIM, 4, RNN_HEAD_DIM)
    blocks = [d6[:, :, hh, :, hh, :] for hh in range(4)]
    return jnp.stack(blocks, axis=2).reshape(2, RNN_HEADS, RNN_HEAD_DIM, RNN_HEAD_DIM)


ROW_TILE = 256
DW_TILES = 4

_SHARDED = ("w_in", "w_pool_out", "w_rg_out", "w_o", "w_up", "w_down", "w_ple_gate", "w_ple_proj")
_WEIGHTS = ("g_mix_pre", "g_mix_post", "w_in", "pool_w", "pool_scale", "w_pool_out", "conv_w", "conv_b", "w_rg_gates",
            "b_rg_gates", "lru_lambda", "w_rg_out", "w_o", "g_ffn_pre", "g_ffn_post", "w_up", "ffn_conv_w", "ffn_conv_b",
            "w_down", "g_ple_gate", "w_ple_gate", "w_ple_proj", "g_ple_post")


def _wire_dtype(g):
    return BF16 if g.shape[1] >= 64 and g.shape[2] > SMALL_COLS else F32


def _partials(grads, got, place):
    parts = [_chip_partial(g, r, place, _wire_dtype(g)) for g, r in zip(grads, got)]
    return [send for send, _ in parts], [acc for _, acc in parts]


def _whole(both):
    return [b.reshape(2 * b.shape[1], b.shape[2]) for b in both]


def _step(x, p, tgt, own, rep, place, ts):
    vec = lambda k: rep[k].reshape(1, -1)
    pool_w = rep["pool_w"].astype(BF16)
    wg = _gates_block_diag(rep["w_rg_gates"]).astype(BF16)
    sq = lambda t: t.reshape(D_MODEL, D_MODEL)
    by4 = lambda t: t.reshape(N_CHIPS, -1, D_MODEL)

    first, ride1, ride2 = ("w_in", "w_pool_out", "tiny"), ("w_rg_out", "w_o", "w_down", "w_ple_gate", "w_ple_proj"), ("w_up",)
    full = dict(zip(first, _run(_gather_task([own[k] for k in first]), "gather_first")))
    conv_w, b_gates, fcw = [_shards_to_cols(t) for t in _unpack_tiny(full["tiny"])]

    (urx, urg, gp, gr, d, ypool), (got,) = _fwd_in_pool(
        x, vec("g_mix_pre"), full["w_in"], pool_w, vec("pool_scale"), full["w_pool_out"], ts,
        [_gather_task([own[k] for k in ride1])])
    full.update(zip(ride1, got))
    w_rg_out, w_o, w_gate = sq(full["w_rg_out"]), sq(full["w_o"]), sq(full["w_ple_gate"])
    w_down = full["w_down"].reshape(D_FF, D_MODEL)
    (xc, r, ig, h, yrnn, mo, x1), (got,) = _fwd_rnn_merge(
        urx, urg, gp, gr, ypool, x, conv_w, vec("conv_b"), wg, b_gates, vec("lru_lambda"), w_rg_out, w_o,
        vec("g_mix_post"), ts, [_gather_task([own[k] for k in ride2])])
    full.update(zip(ride2, got))
    up, gl, gg, h2, dn, x2 = _fwd_ffn(x1, vec("g_ffn_pre"), full["w_up"], fcw, vec("ffn_conv_b"), w_down,
                                      vec("g_ffn_post"), ts)
    dx2, loss, d_w_gate, d_w_proj, d_g_ple_gate, d_g_ple_post = _ple_loss(
        x2, p, tgt, vec("g_ple_gate"), w_gate, full["w_ple_proj"], vec("g_ple_post"), ts)
    dup, d_w_down, d_fcw, d_fcb, d_g_ffn_post = _bwd_ffn_down(dx2, dn, up, gl, gg, fcw, w_down, vec("g_ffn_post"), ts)

    names1, grads1 = ("w_ple_gate", "w_ple_proj", "w_down"), [by4(d_w_gate), d_w_proj, by4(d_w_down)]
    (dx1, d_g_ffn_pre), (got1,) = _bwd_ffn_up(dup, x1, dx2, vec("g_ffn_pre"), full["w_up"], ts, [_halves_task(grads1)])
    (d_w_up,), (accs1,) = _dw_up(h2, dup, ts, [_exchange_task(*_partials(grads1, got1, place))])
    shares1 = [_chip_sum(acc, place) for acc in accs1]
    (dgp, dgr, dyp, dyr, d_w_o, d_g_mix_post), (got2, both1) = _bwd_merge(
        dx1, mo, gp, gr, ypool, yrnn, vec("g_mix_post"), w_o, ts, [_halves_task([d_w_up]), _share_task(shares1)])
    (durx, durg, d_w_rg_out, d_wg, d_conv_w, d_conv_b, d_b_gates, d_lam), (accs2,) = _bwd_rnn(
        dyr, urx, urg, xc, r, ig, h, conv_w, wg, vec("lru_lambda"), w_rg_out, ts,
        [_exchange_task(*_partials([d_w_up], got2, place))])
    shares2 = [_chip_sum(acc, place) for acc in accs2]
    (grad_x, d_w_in, d_w_pool_out, d_pool_w, d_pool_scale, d_g_mix_pre), _ = _bwd_pool_in(
        dyp, d, durx, durg, dgp, dgr, x, dx1, vec("g_mix_pre"), full["w_in"], pool_w, vec("pool_scale"),
        full["w_pool_out"], ts)

    replicated = {"g_mix_pre": d_g_mix_pre, "g_mix_post": d_g_mix_post, "conv_b": d_conv_b, "lru_lambda": d_lam,
                  "g_ffn_pre": d_g_ffn_pre, "g_ffn_post": d_g_ffn_post, "g_ple_gate": d_g_ple_gate,
                  "g_ple_post": d_g_ple_post, "pool_scale": d_pool_scale, "ffn_conv_b": d_fcb, "pool_w": d_pool_w}
    names3 = ("w_in", "w_pool_out", "w_rg_out", "w_o", "tiny", "small", "w_rg_gates")
    grads3 = [d_w_in, d_w_pool_out, by4(d_w_rg_out), by4(d_w_o),
              _pack_tiny(_cols_to_shards(d_conv_w, 256), _cols_to_shards(d_b_gates, 256), _cols_to_shards(d_fcw, 768)),
              _pack_small(replicated, loss)[None],
              _gates_from_block_diag(d_wg).reshape(1, 2 * RNN_HEADS * RNN_HEAD_DIM, RNN_HEAD_DIM)]
    got3 = _run(_halves_task(grads3), "grad_sibling_halves")
    accs3 = _run(_exchange_task(*_partials(grads3, got3, place)), "grad_chip_exchange")
    both = _run(_share_task(shares2 + [_chip_sum(acc, place) for acc in accs3]), "grad_sibling_share")

    return grad_x, dict(zip(names1 + ("w_up",) + names3, _whole(both1) + _whole(both)))


def kernel(x, p, g_mix_pre, g_mix_post, w_in, pool_w, pool_scale, w_pool_out, conv_w, conv_b, w_rg_gates, b_rg_gates, lru_lambda, w_rg_out, w_o, g_ffn_pre, g_ffn_post, w_up, ffn_conv_w, ffn_conv_b, w_down, g_ple_gate, w_ple_gate, w_ple_proj, g_ple_post, loss_target, m_g_mix_pre, m_g_mix_post, m_w_in, m_pool_w, m_pool_scale, m_w_pool_out, m_conv_w, m_conv_b, m_w_rg_gates, m_b_rg_gates, m_lru_lambda, m_w_rg_out, m_w_o, m_g_ffn_pre, m_g_ffn_post, m_w_up, m_ffn_conv_w, m_ffn_conv_b, m_w_down, m_g_ple_gate, m_w_ple_gate, m_w_ple_proj, m_g_ple_post, v_g_mix_pre, v_g_mix_post, v_w_in, v_pool_w, v_pool_scale, v_w_pool_out, v_conv_w, v_conv_b, v_w_rg_gates, v_b_rg_gates, v_lru_lambda, v_w_rg_out, v_w_o, v_g_ffn_pre, v_g_ffn_post, v_w_up, v_ffn_conv_w, v_ffn_conv_b, v_w_down, v_g_ple_gate, v_w_ple_gate, v_w_ple_proj, v_g_ple_post):
    args = dict(locals())
    w = {k: args[k][0] for k in _WEIGHTS}
    m = {k: args["m_" + k][0] for k in _WEIGHTS}
    v = {k: args["v_" + k][0] for k in _WEIGHTS}
    place = jnp.stack([2 * lax.axis_index("x") + lax.axis_index("y"), lax.axis_index("c")]).astype(jnp.int32)
    own = {k: _own_slot(w[k], place, BF16) for k in _SHARDED}
    own["tiny"] = _own_slot(_pack_tiny(w["conv_w"][None], w["b_rg_gates"][None], w["ffn_conv_w"][None])[0], place, F32)
    grad_x, reduced = _step(x[0], p[0, 0], loss_target[0], own, w, place, ROW_TILE)

    gates_2d = (2 * RNN_HEADS * RNN_HEAD_DIM, RNN_HEAD_DIM)
    as2d = lambda k, shape: tuple(t[k].reshape(shape) for t in (w, m, v))
    done = {k: (reduced[k],) + tuple(_adamw(w[k], reduced[k], m[k], v[k])) for k in _SHARDED}
    gates_w, gates_m, gates_v = as2d("w_rg_gates", gates_2d)
    done["w_rg_gates"] = (reduced["w_rg_gates"],) + tuple(_adamw(gates_w, reduced["w_rg_gates"], gates_m, gates_v))
    tiny_names = ("conv_w", "b_rg_gates", "ffn_conv_w")
    tiny_at = ((slice(0, 4), slice(0, 256)), (slice(4, 6), slice(0, 256)), (slice(8, 11), slice(None)))
    done.update(zip(tiny_names, _adamw_pieces(
        reduced["tiny"], [(w[k], m[k], v[k]) + at for k, at in zip(tiny_names, tiny_at)], "adamw_tiny")))
    vector_rows, loss_row = _vector_rows()
    pieces = [as2d("pool_w", (POOL_W_ROWS, SMALL_COLS)) + (slice(0, POOL_W_ROWS), slice(None))]
    pieces += [as2d(k, (1, -1)) + (vector_rows[k],) for k in _VECTORS]
    done.update(zip(("pool_w",) + _VECTORS, _adamw_pieces(reduced["small"], pieces, "adamw_small")))

    result = [reduced["small"][loss_row, 0], grad_x[None]]
    for kind in range(4):
        result += [done[k][kind].reshape(args[k].shape) for k in _WEIGHTS]
    return tuple(result)
```

```python
import functools

import jax
import jax.numpy as jnp
from jax import lax
from jax.experimental import pallas as pl
from jax.experimental.pallas import tpu as pltpu

F32 = jnp.float32
BF16 = jnp.bfloat16

D_MODEL = 1024
POOL_WINDOWS = (2, 4, 8, 16)
POOL_GROUPS = 4
POOL_WIDTH = 512
POOL_GROUP_DIM = 128
RNN_HEADS = 16
RNN_HEAD_DIM = 64
GATE_BLOCK = 256
GATE_BLOCKS = D_MODEL // GATE_BLOCK
LRU_C = 8.0
D_FF = 3072
PLE_DIM = 256
RMS_EPS = 1e-6
IN_TOTAL = 4608
N_CHIPS = 4
IN_SHARD = IN_TOTAL // N_CHIPS
UP_SHARD = 2 * D_FF // N_CHIPS
POOL_HALO = 16
CONV_HALO = 8

ADAM_LR = 0.001
ADAM_B1 = 0.9
ADAM_B2 = 0.999
ADAM_EPS = 1e-08
ADAM_WD = 0.01
ADAM_STEP = 10

VMEM_LIMIT = 56 * 1024 * 1024
MESH = pl.DeviceIdType.MESH

_GELU_C = 0.7978845608028654
_GELU_A = 0.044715


def _dot(a, b):
    return jnp.dot(a.astype(BF16), b.astype(BF16), preferred_element_type=F32)


def _dot_nt(a, b):
    return lax.dot_general(a.astype(BF16), b.astype(BF16), (((1,), (1,)), ((), ())), preferred_element_type=F32)


def _dot_tn(a, b):
    return lax.dot_general(a.astype(BF16), b.astype(BF16), (((0,), (0,)), ((), ())), preferred_element_type=F32)


def _rms_fwd(x, g):
    r = lax.rsqrt(jnp.mean(x * x, axis=-1, keepdims=True) + RMS_EPS)
    xh = x * r
    return xh * g, xh, r


def _rms_bwd(xh, r, g, dy):
    dxh = dy * g
    dg = jnp.sum(dy * xh, axis=0, keepdims=True)
    dx = r * (dxh - xh * jnp.mean(dxh * xh, axis=-1, keepdims=True))
    return dx, dg


def _gelu(x):
    t = jnp.tanh(_GELU_C * (x + _GELU_A * x * x * x))
    return 0.5 * x * (1.0 + t), t


def _gelu_grad(x, t):
    return 0.5 * (1.0 + t) + 0.5 * x * (1.0 - t * t) * _GELU_C * (1.0 + 3.0 * _GELU_A * x * x)


def _softplus_neg(lam):
    nl = -lam
    return jnp.maximum(nl, 0.0) + jnp.log(1.0 + jnp.exp(-jnp.abs(nl)))


def _lru_coeffs(r, lam, first_row):
    c8 = LRU_C * _softplus_neg(lam)
    la = -(c8 * r)
    a = jnp.exp(la)
    m2 = jnp.tanh(-la) * (1.0 + a * a)
    mult = jnp.where(first_row, 1.0, jnp.sqrt(m2))
    return c8, a, m2, mult


SUBLANES = 8


def _scan_fwd(a, u, carry):
    n = a.shape[0]
    sub = lax.broadcasted_iota(jnp.int32, (n, 1), 0) % SUBLANES
    acc_a, acc_h = a, u
    for s in (1, 2, 4):
        m = sub >= s
        h_s = jnp.where(m, pltpu.roll(acc_h, s, 0), 0.0)
        a_s = jnp.where(m, pltpu.roll(acc_a, s, 0), 1.0)
        acc_h = acc_a * h_s + acc_h
        acc_a = acc_a * a_s
    out = []
    for g in range(n // SUBLANES):
        rows = slice(g * SUBLANES, (g + 1) * SUBLANES)
        out.append(acc_h[rows] + acc_a[rows] * carry)
        carry = out[-1][SUBLANES - 1:SUBLANES]
    return jnp.concatenate(out, axis=0)


def _scan_bwd(b, g, carry):
    n = b.shape[0]
    sub = lax.broadcasted_iota(jnp.int32, (n, 1), 0) % SUBLANES
    acc_b, acc_l = b, g
    for s in (1, 2, 4):
        m = sub < SUBLANES - s
        l_s = jnp.where(m, pltpu.roll(acc_l, n - s, 0), 0.0)
        b_s = jnp.where(m, pltpu.roll(acc_b, n - s, 0), 1.0)
        acc_l = acc_b * l_s + acc_l
        acc_b = acc_b * b_s
    out = [None] * (n // SUBLANES)
    for g in reversed(range(n // SUBLANES)):
        rows = slice(g * SUBLANES, (g + 1) * SUBLANES)
        out[g] = acc_l[rows] + acc_b[rows] * carry
        carry = out[g][0:1]
    return jnp.concatenate(out, axis=0)


def _shift_down(ext, k, halo):
    return pltpu.roll(ext, k, 0)[halo:] if k else ext[halo:]


def _shift_up(ext, k, ts):
    return pltpu.roll(ext, ext.shape[0] - k, 0)[:ts] if k else ext[:ts]


def _rows(ts, width, nt=None, col=0):
    if nt is None:
        return pl.BlockSpec((ts, width), lambda i: (i, col))
    return pl.BlockSpec((ts, width), lambda i: (nt - 1 - i, col))


def _resident(shape):
    zeros = (0,) * len(shape)
    return pl.BlockSpec(shape, lambda i: zeros, pipeline_mode=pl.Buffered(1))


def _acc(shape):
    zeros = (0,) * len(shape)
    return pl.BlockSpec(shape, lambda i: zeros)


def _params():
    return pltpu.CompilerParams(dimension_semantics=("arbitrary",), vmem_limit_bytes=VMEM_LIMIT)


def _sds(shape, dtype=F32):
    return jax.ShapeDtypeStruct(shape, dtype)


class _Task:
    def __init__(self, ins, out_shapes, aliases, sems, start, finish):
        self.ins, self.out_shapes, self.aliases, self.sems = list(ins), list(out_shapes), dict(aliases), list(sems)
        self.start, self.finish = start, finish


def _call(body, name, grid, in_specs, out_specs, out_shape, scratch_shapes, args, tasks=()):
    n_in, n_out, n_scr = len(in_specs), len(out_specs), len(scratch_shapes)
    t_in = [len(t.ins) for t in tasks]
    t_out = [len(t.out_shapes) for t in tasks]
    t_sem = [len(t.sems) for t in tasks]
    steps = 1
    for g in grid:
        steps *= g

    def take(refs, pos, counts):
        groups = []
        for c in counts:
            groups.append(refs[pos:pos + c])
            pos += c
        return groups, pos

    def wrapped(*refs):
        (cin,), pos = take(refs, 0, [n_in])
        tin, pos = take(refs, pos, t_in)
        (cout,), pos = take(refs, pos, [n_out])
        tout, pos = take(refs, pos, t_out)
        (cscr,), pos = take(refs, pos, [n_scr])
        tsem, pos = take(refs, pos, t_sem)
        if not grid:
            for t, a, b, c in zip(tasks, tin, tout, tsem):
                t.start(a, b, c)
            for t, a, b, c in zip(tasks, tin, tout, tsem):
                t.finish(a, b, c)
            return
        step = pl.program_id(0)
        for axis in range(1, len(grid)):
            step = step * grid[axis] + pl.program_id(axis)
        if tasks:
            @pl.when(step == 0)
            def _():
                for t, a, b, c in zip(tasks, tin, tout, tsem):
                    t.start(a, b, c)

        body(*cin, *cout, *cscr)
        if tasks:
            @pl.when(step == steps - 1)
            def _():
                for t, a, b, c in zip(tasks, tin, tout, tsem):
                    t.finish(a, b, c)

    aliases, in_pos, out_pos = {}, n_in, n_out
    for t, ni, no in zip(tasks, t_in, t_out):
        aliases.update({in_pos + a: out_pos + b for a, b in t.aliases.items()})
        in_pos, out_pos = in_pos + ni, out_pos + no
    any_spec = pl.BlockSpec(memory_space=pltpu.HBM)
    kwargs = dict(grid=grid, compiler_params=pltpu.CompilerParams(
        dimension_semantics=("arbitrary",) * len(grid), vmem_limit_bytes=VMEM_LIMIT)) if grid else {}
    out = pl.pallas_call(
        wrapped, name=name,
        in_specs=list(in_specs) + [any_spec] * sum(t_in),
        out_specs=list(out_specs) + [any_spec] * sum(t_out),
        out_shape=list(out_shape) + [s for t in tasks for s in t.out_shapes],
        scratch_shapes=list(scratch_shapes) + [s for t in tasks for s in t.sems],
        input_output_aliases=aliases, **kwargs,
    )(*args, *[pltpu.with_memory_space_constraint(a, pltpu.HBM) for t in tasks for a in t.ins])
    task_outs, pos = take(list(out), n_out, t_out)
    return list(out[:n_out]), task_outs


def _fwd_in_pool(x, g_pre, w_in, pool_w, pool_scale, w_pool_out, ts, tasks=()):
    s = x.shape[0]

    def body(x_ref, g_ref, win_ref, pw_ref, ps_ref, wpo_ref,
             urx_ref, urg_ref, gp_ref, gr_ref, d_ref, yp_ref, z_scr, halo_scr):
        i = pl.program_id(0)

        @pl.when(i == 0)
        def _():
            halo_scr[...] = jnp.zeros_like(halo_scr)

        h1, _, _ = _rms_fwd(x_ref[...], g_ref[...])
        h1 = h1.astype(BF16)
        for j in range(N_CHIPS):
            z_scr[:, j * IN_SHARD:(j + 1) * IN_SHARD] = jnp.dot(h1, win_ref[j], preferred_element_type=F32)
        urx_ref[...] = z_scr[:, 512:1536]
        urg_ref[...] = z_scr[:, 1536:2560]
        gp_ref[...] = z_scr[:, 2560:3584]
        gr_ref[...] = z_scr[:, 3584:4608]
        u = z_scr[:, 0:POOL_WIDTH]
        ext = jnp.concatenate([halo_scr[...], u], axis=0)
        halo_scr[...] = u[ts - POOL_HALO:, :]
        t = i * ts + lax.broadcasted_iota(jnp.int32, (ts, 1), 0)
        y4 = []
        for g, w in enumerate(POOL_WINDOWS):
            lanes = slice(g * POOL_GROUP_DIM, (g + 1) * POOL_GROUP_DIM)
            acc = ext[:, lanes]
            sh = 1
            while sh < w:
                acc = acc + pltpu.roll(acc, sh, 0)
                sh *= 2
            inv = 1.0 / jnp.minimum(t + 1, w).astype(F32)
            dg = acc[POOL_HALO:, :] * inv - u[:, lanes]
            d_ref[:, lanes] = dg
            y4.append(_dot(dg, pw_ref[g]))
        ypre = jnp.concatenate(y4, axis=1) * ps_ref[...]
        ypre = ypre.astype(BF16)
        for j in range(N_CHIPS):
            yp_ref[:, j * 256:(j + 1) * 256] = jnp.dot(ypre, wpo_ref[j], preferred_element_type=F32)

    return _call(
        body, "fwd_in_pool", (s // ts,),
        [_rows(ts, D_MODEL), _resident((1, D_MODEL)), _resident(w_in.shape), _resident(pool_w.shape),
         _resident((1, POOL_WIDTH)), _resident(w_pool_out.shape)],
        [_rows(ts, D_MODEL)] * 4 + [_rows(ts, POOL_WIDTH), _rows(ts, D_MODEL)],
        [_sds((s, D_MODEL))] * 4 + [_sds((s, POOL_WIDTH)), _sds((s, D_MODEL))],
        [pltpu.VMEM((ts, IN_TOTAL), F32), pltpu.VMEM((POOL_HALO, POOL_WIDTH), F32)],
        (x, g_pre, w_in, pool_w, pool_scale, w_pool_out), tasks)


def _fwd_rnn_merge(urx, urg, gp, gr, ypool, x, conv_w, conv_b, wg, bg, lam, w_rg_out, w_o, g_post, ts, tasks=()):
    s = x.shape[0]

    def body(urx_ref, urg_ref, gp_ref, gr_ref, yp_ref, x_ref, cw_ref, cb_ref, wg_ref, bg_ref, lam_ref, wrg_ref, wo_ref,
             gpost_ref, xc_ref, r_ref, ig_ref, h_ref, yr_ref, mo_ref, x1_ref, gl_ref, gg_ref, sp_ref, sr_ref,
             halo_scr, carry_scr):
        i = pl.program_id(0)

        @pl.when(i == 0)
        def _():
            halo_scr[...] = jnp.zeros_like(halo_scr)
            carry_scr[...] = jnp.zeros_like(carry_scr)

        urx_v = urx_ref[...]
        ext = jnp.concatenate([halo_scr[...], urx_v], axis=0)
        halo_scr[...] = urx_v[ts - CONV_HALO:, :]
        cw = cw_ref[...]
        xc = (cb_ref[...] + cw[3:4] * urx_v + cw[2:3] * _shift_down(ext, 1, CONV_HALO)
              + cw[1:2] * _shift_down(ext, 2, CONV_HALO) + cw[0:1] * _shift_down(ext, 3, CONV_HALO))
        xc_ref[...] = xc
        xcb = xc.astype(BF16)
        lin = []
        for gate in range(2):
            parts = [jnp.dot(xcb[:, q * GATE_BLOCK:(q + 1) * GATE_BLOCK], wg_ref[gate, q], preferred_element_type=F32)
                     for q in range(GATE_BLOCKS)]
            lin.append(jnp.concatenate(parts, axis=1) + bg_ref[gate:gate + 1, :])
        r = jax.nn.sigmoid(lin[0])
        ig = jax.nn.sigmoid(lin[1])
        r_ref[...] = r
        ig_ref[...] = ig
        first_row = (i * ts + lax.broadcasted_iota(jnp.int32, (ts, 1), 0)) == 0
        _, a, _, mult = _lru_coeffs(r, lam_ref[...], first_row)
        h = _scan_fwd(a, mult * ig * xc, carry_scr[0:1, :])
        carry_scr[0:1, :] = h[ts - 1:ts, :]
        h_ref[...] = h
        urg_v = urg_ref[...]
        gl, t = _gelu(urg_v)
        gl_ref[...] = gl.astype(BF16)
        gg_ref[...] = _gelu_grad(urg_v, t).astype(BF16)
        yr = _dot(h * gl, wrg_ref[...])
        yr_ref[...] = yr
        sp = jax.nn.sigmoid(gp_ref[...])
        sr = jax.nn.sigmoid(gr_ref[...])
        sp_ref[...] = sp.astype(BF16)
        sr_ref[...] = sr.astype(BF16)
        merged = sp * yp_ref[...] + sr * yr
        mo = _dot(merged, wo_ref[...])
        mo_ref[...] = mo
        y, _, _ = _rms_fwd(mo, gpost_ref[...])
        x1_ref[...] = x_ref[...] + y

    row = _rows(ts, D_MODEL)
    return _call(
        body, "fwd_rnn_merge", (s // ts,),
        [row] * 6 + [_resident(conv_w.shape), _resident((1, D_MODEL)), _resident(wg.shape), _resident(bg.shape),
                     _resident((1, D_MODEL)), _resident(w_rg_out.shape), _resident(w_o.shape), _resident((1, D_MODEL))],
        [row] * 11, [_sds((s, D_MODEL))] * 7 + [_sds((s, D_MODEL), BF16)] * 4,
        [pltpu.VMEM((CONV_HALO, D_MODEL), F32), pltpu.VMEM((8, D_MODEL), F32)],
        (urx, urg, gp, gr, ypool, x, conv_w, conv_b, wg, bg, lam, w_rg_out, w_o, g_post), tasks)


def _fwd_ffn(x1, g_pre, w_up, fcw, fcb, w_down, g_post, ts, tasks=()):
    s = x1.shape[0]

    def body(x1_ref, g_ref, wup_ref, fcw_ref, fcb_ref, wd_ref, gpost_ref,
             up_ref, gl_ref, gg_ref, h2_ref, dn_ref, x2_ref, up_scr, halo_scr):
        i = pl.program_id(0)

        @pl.when(i == 0)
        def _():
            halo_scr[...] = jnp.zeros_like(halo_scr)

        x1_v = x1_ref[...]
        h2, _, _ = _rms_fwd(x1_v, g_ref[...])
        h2 = h2.astype(BF16)
        h2_ref[...] = h2
        for j in range(N_CHIPS):
            up_scr[:, j * UP_SHARD:(j + 1) * UP_SHARD] = jnp.dot(h2, wup_ref[j], preferred_element_type=F32)
        up_ref[...] = up_scr[...].astype(BF16)
        ug = up_scr[:, 0:D_FF]
        ext = jnp.concatenate([halo_scr[...], ug], axis=0)
        halo_scr[...] = ug[ts - CONV_HALO:, :]
        w = fcw_ref[...]
        gh = (fcb_ref[...] + w[2:3] * ug + w[1:2] * _shift_down(ext, 1, CONV_HALO)
              + w[0:1] * _shift_down(ext, 2, CONV_HALO))
        gl, t = _gelu(gh)
        gl_ref[...] = gl.astype(BF16)
        gg_ref[...] = _gelu_grad(gh, t).astype(BF16)
        dn = _dot(gl * up_scr[:, D_FF:], wd_ref[...])
        dn_ref[...] = dn
        y, _, _ = _rms_fwd(dn, gpost_ref[...])
        x2_ref[...] = x1_v + y

    row = _rows(ts, D_MODEL)
    return _call(
        body, "fwd_ffn", (s // ts,),
        [row, _resident((1, D_MODEL)), _resident(w_up.shape), _resident(fcw.shape), _resident((1, D_FF)),
         _resident(w_down.shape), _resident((1, D_MODEL))],
        [_rows(ts, 2 * D_FF), _rows(ts, D_FF), _rows(ts, D_FF), row, row, row],
        [_sds((s, 2 * D_FF), BF16), _sds((s, D_FF), BF16), _sds((s, D_FF), BF16), _sds((s, D_MODEL), BF16),
         _sds((s, D_MODEL)), _sds((s, D_MODEL))],
        [pltpu.VMEM((ts, 2 * D_FF), F32), pltpu.VMEM((CONV_HALO, D_FF), F32)],
        (x1, g_pre, w_up, fcw, fcb, w_down, g_post), tasks)


def _ple_loss(x2, p, tgt, g_gate, w_gate, w_proj, g_post, ts):
    s = x2.shape[0]

    def body(x2_ref, p_ref, t_ref, gg_ref, wg_ref, wp_ref, gp_ref, dx2_ref, loss_ref, dwg_ref, dwp_ref, dgg_ref, dgp_ref):
        @pl.when(pl.program_id(0) == 0)
        def _():
            loss_ref[...] = jnp.zeros_like(loss_ref)
            dwg_ref[...] = jnp.zeros_like(dwg_ref)
            dwp_ref[...] = jnp.zeros_like(dwp_ref)
            dgg_ref[...] = jnp.zeros_like(dgg_ref)
            dgp_ref[...] = jnp.zeros_like(dgp_ref)

        x2_v = x2_ref[...]
        n3, xh3, r3 = _rms_fwd(x2_v, gg_ref[...])
        pg = jax.nn.sigmoid(_dot(n3, wg_ref[...]))
        pb = p_ref[...].astype(BF16)
        q = jnp.concatenate([jnp.dot(pb, wp_ref[j], preferred_element_type=F32) for j in range(N_CHIPS)], axis=1)
        ple, qh, rq = _rms_fwd(q, gp_ref[...])
        e = x2_v + pg * ple - t_ref[...]
        loss_ref[...] += 0.5 * jnp.sum(jnp.mean(e * e, axis=-1, keepdims=True), axis=0, keepdims=True)
        dy = e * (1.0 / D_MODEL)
        dpgl = dy * ple * pg * (1.0 - pg)
        dwg_ref[...] += _dot_tn(n3, dpgl)
        dx3, dgg = _rms_bwd(xh3, r3, gg_ref[...], _dot_nt(dpgl, wg_ref[...]))
        dgg_ref[...] += dgg
        dq, dgp = _rms_bwd(qh, rq, gp_ref[...], dy * pg)
        dgp_ref[...] += dgp
        for j in range(N_CHIPS):
            dwp_ref[j] += _dot_tn(pb, dq[:, j * 256:(j + 1) * 256])
        dx2_ref[...] = dy + dx3

    row = _rows(ts, D_MODEL)
    vec = _acc((1, D_MODEL))
    return pl.pallas_call(
        body, name="ple_loss", grid=(s // ts,),
        in_specs=[row, _rows(ts, PLE_DIM), row, _resident((1, D_MODEL)), _resident(w_gate.shape), _resident(w_proj.shape),
                  _resident((1, D_MODEL))],
        out_specs=[row, _acc((1, 128)), _acc(w_gate.shape), _acc(w_proj.shape), vec, vec],
        out_shape=[_sds((s, D_MODEL)), _sds((1, 128)), _sds(w_gate.shape), _sds(w_proj.shape), _sds((1, D_MODEL)),
                   _sds((1, D_MODEL))],
        compiler_params=_params(),
    )(x2, p, tgt, g_gate, w_gate, w_proj, g_post)


def _bwd_ffn_down(dx2, dn, up, gl, gg, fcw, w_down, g_post, ts):
    s = dx2.shape[0]
    nt = s // ts

    def body(dx2_ref, dn_ref, up_ref, gl_ref, gg_ref, fcw_ref, wd_ref, gpost_ref,
             dup_ref, dwd_ref, dfcw_ref, dfcb_ref, dgp_ref, carry_scr):
        i = pl.program_id(0)

        @pl.when(i == 0)
        def _():
            carry_scr[...] = jnp.zeros_like(carry_scr)
            dwd_ref[...] = jnp.zeros_like(dwd_ref)
            dfcw_ref[...] = jnp.zeros_like(dfcw_ref)
            dfcb_ref[...] = jnp.zeros_like(dfcb_ref)
            dgp_ref[...] = jnp.zeros_like(dgp_ref)

        _, xh, r = _rms_fwd(dn_ref[...], gpost_ref[...])
        ddn, dgp = _rms_bwd(xh, r, gpost_ref[...], dx2_ref[...])
        dgp_ref[...] += dgp
        dhid = _dot_nt(ddn, wd_ref[...])
        ug = up_ref[:, 0:D_FF].astype(F32)
        uv = up_ref[:, D_FF:].astype(F32)
        gl = gl_ref[...].astype(F32)
        w = fcw_ref[...]
        dwd_ref[...] += _dot_tn(gl * uv, ddn)
        dgh = dhid * uv * gg_ref[...].astype(F32)
        dup_ref[:, D_FF:] = (dhid * gl).astype(BF16)
        extd = jnp.concatenate([dgh, carry_scr[...]], axis=0)
        carry_scr[...] = dgh[0:CONV_HALO, :]
        d1 = _shift_up(extd, 1, ts)
        d2 = _shift_up(extd, 2, ts)
        dup_ref[:, 0:D_FF] = (w[2:3] * dgh + w[1:2] * d1 + w[0:1] * d2).astype(BF16)
        dfcw_ref[2:3, :] += jnp.sum(ug * dgh, axis=0, keepdims=True)
        dfcw_ref[1:2, :] += jnp.sum(ug * d1, axis=0, keepdims=True)
        dfcw_ref[0:1, :] += jnp.sum(ug * d2, axis=0, keepdims=True)
        dfcb_ref[...] += jnp.sum(dgh, axis=0, keepdims=True)

    row = _rows(ts, D_MODEL, nt)
    wide = _rows(ts, D_FF, nt)
    return pl.pallas_call(
        body, name="bwd_ffn_down", grid=(nt,),
        in_specs=[row, row, _rows(ts, 2 * D_FF, nt), wide, wide, _resident(fcw.shape), _resident(w_down.shape),
                  _resident((1, D_MODEL))],
        out_specs=[_rows(ts, 2 * D_FF, nt), _acc(w_down.shape), _acc(fcw.shape), _acc((1, D_FF)), _acc((1, D_MODEL))],
        out_shape=[_sds((s, 2 * D_FF), BF16), _sds(w_down.shape), _sds(fcw.shape), _sds((1, D_FF)), _sds((1, D_MODEL))],
        scratch_shapes=[pltpu.VMEM((CONV_HALO, D_FF), F32)],
        compiler_params=_params(),
    )(dx2, dn, up, gl, gg, fcw, w_down, g_post)


def _bwd_ffn_up(dup, x1, dx2, g_pre, w_up, ts, tasks=()):
    s = x1.shape[0]

    def body(dup_ref, x1_ref, dx2_ref, g_ref, wup_ref, dx1_ref, dg_ref):
        @pl.when(pl.program_id(0) == 0)
        def _():
            dg_ref[...] = jnp.zeros_like(dg_ref)

        _, xh, r = _rms_fwd(x1_ref[...], g_ref[...])
        dh2 = _dot_nt(dup_ref[:, 0:UP_SHARD], wup_ref[0])
        for j in range(1, N_CHIPS):
            dh2 = dh2 + _dot_nt(dup_ref[:, j * UP_SHARD:(j + 1) * UP_SHARD], wup_ref[j])
        dx, dg = _rms_bwd(xh, r, g_ref[...], dh2)
        dg_ref[...] += dg
        dx1_ref[...] = dx2_ref[...] + dx

    row = _rows(ts, D_MODEL)
    return _call(
        body, "bwd_ffn_up", (s // ts,),
        [_rows(ts, 2 * D_FF), row, row, _resident((1, D_MODEL)), _resident(w_up.shape)],
        [row, _acc((1, D_MODEL))], [_sds((s, D_MODEL)), _sds((1, D_MODEL))], [],
        (dup, x1, dx2, g_pre, w_up), tasks)


def _dw_up(h2, dup, ts, tasks=()):
    s = h2.shape[0]
    ts = min(DW_TILES * ts, s)

    def body(h2_ref, dup_ref, out_ref):
        @pl.when(pl.program_id(1) == 0)
        def _():
            out_ref[...] = jnp.zeros_like(out_ref)

        out_ref[0] += _dot_tn(h2_ref[...], dup_ref[...])

    return _call(
        body, "dw_up", (N_CHIPS, s // ts),
        [pl.BlockSpec((ts, D_MODEL), lambda j, i: (i, 0)), pl.BlockSpec((ts, UP_SHARD), lambda j, i: (i, j))],
        [pl.BlockSpec((1, D_MODEL, UP_SHARD), lambda j, i: (j, 0, 0))], [_sds((N_CHIPS, D_MODEL, UP_SHARD))], [],
        (h2, dup), tasks)


def _bwd_merge(dx1, mo, sp, sr, ypool, yrnn, g_post, w_o, ts, tasks=()):
    s = dx1.shape[0]

    def body(dx1_ref, mo_ref, sp_ref, sr_ref, yp_ref, yr_ref, g_ref, wo_ref,
             dgp_ref, dgr_ref, dyp_ref, dyr_ref, dwo_ref, dg_ref):
        @pl.when(pl.program_id(0) == 0)
        def _():
            dwo_ref[...] = jnp.zeros_like(dwo_ref)
            dg_ref[...] = jnp.zeros_like(dg_ref)

        _, xh, r = _rms_fwd(mo_ref[...], g_ref[...])
        dmo, dg = _rms_bwd(xh, r, g_ref[...], dx1_ref[...])
        dg_ref[...] += dg
        dmerged = _dot_nt(dmo, wo_ref[...])
        sp = sp_ref[...].astype(F32)
        sr = sr_ref[...].astype(F32)
        yp = yp_ref[...]
        yr = yr_ref[...]
        dwo_ref[...] += _dot_tn(sp * yp + sr * yr, dmo)
        dgp_ref[...] = (dmerged * yp * sp * (1.0 - sp)).astype(BF16)
        dgr_ref[...] = (dmerged * yr * sr * (1.0 - sr)).astype(BF16)
        dyp_ref[...] = (dmerged * sp).astype(BF16)
        dyr_ref[...] = (dmerged * sr).astype(BF16)

    row = _rows(ts, D_MODEL)
    return _call(
        body, "bwd_merge", (s // ts,),
        [row] * 6 + [_resident((1, D_MODEL)), _resident(w_o.shape)],
        [row] * 4 + [_acc(w_o.shape), _acc((1, D_MODEL))],
        [_sds((s, D_MODEL), BF16)] * 4 + [_sds(w_o.shape), _sds((1, D_MODEL))], [],
        (dx1, mo, sp, sr, ypool, yrnn, g_post, w_o), tasks)


def _bwd_rnn(dyr, urx, gl, gg, xc, r, ig, h, conv_w, wg, lam, w_rg_out, ts, tasks=()):
    s = urx.shape[0]
    nt = s // ts
    halo_blocks = ts // CONV_HALO

    def body(dyr_ref, urx_ref, gl_ref, gg_ref, xc_ref, r_ref, ig_ref, h_ref, hh_ref, cw_ref, wg_ref, lam_ref, wrg_ref,
             durx_ref, durg_ref, dwrg_ref, dwg_ref, dcw_ref, dcb_ref, dbg_ref, dlam_ref, mu_scr, carry_scr):
        i = pl.program_id(0)
        k = nt - 1 - i

        @pl.when(i == 0)
        def _():
            mu_scr[...] = jnp.zeros_like(mu_scr)
            carry_scr[...] = jnp.zeros_like(carry_scr)
            dwrg_ref[...] = jnp.zeros_like(dwrg_ref)
            dwg_ref[...] = jnp.zeros_like(dwg_ref)
            dcw_ref[...] = jnp.zeros_like(dcw_ref)
            dcb_ref[...] = jnp.zeros_like(dcb_ref)
            dbg_ref[...] = jnp.zeros_like(dbg_ref)
            dlam_ref[...] = jnp.zeros_like(dlam_ref)

        row = lax.broadcasted_iota(jnp.int32, (ts, 1), 0)
        first_row = (k * ts + row) == 0
        h = h_ref[...]
        dyr_v = dyr_ref[...]
        dhr = _dot_nt(dyr_v, wrg_ref[...])
        gl = gl_ref[...].astype(F32)
        dwrg_ref[...] += _dot_tn(h * gl, dyr_v)
        durg_ref[...] = (dhr * h * gg_ref[...].astype(F32)).astype(BF16)
        r_v = r_ref[...]
        ig_v = ig_ref[...]
        xc_v = xc_ref[...]
        lam_v = lam_ref[...]
        c8, a, m2, mult = _lru_coeffs(r_v, lam_v, first_row)
        b = jnp.where(row == ts - 1, 1.0, pltpu.roll(a, ts - 1, 0))
        lt = _scan_bwd(b, dhr * gl, mu_scr[0:1, :])
        mu_scr[0:1, :] = a[0:1, :] * lt[0:1, :]
        h_before = jnp.where(k > 0, hh_ref[CONV_HALO - 1:CONV_HALO, :], 0.0)
        hprev = jnp.where(row == 0, h_before, pltpu.roll(h, 1, 0))
        dmult = lt * ig_v * xc_v
        da = lt * hprev - jnp.where(first_row, 0.0, dmult * a * lax.rsqrt(m2))
        dla = da * a
        dlam_ref[...] += jnp.sum(dla * r_v, axis=0, keepdims=True)
        dlr = (dla * (-c8)) * r_v * (1.0 - r_v)
        dli = (lt * mult * xc_v) * ig_v * (1.0 - ig_v)
        dbg_ref[0:1, :] += jnp.sum(dlr, axis=0, keepdims=True)
        dbg_ref[1:2, :] += jnp.sum(dli, axis=0, keepdims=True)
        xcb = xc_v.astype(BF16)
        parts = []
        for q in range(GATE_BLOCKS):
            blk = slice(q * GATE_BLOCK, (q + 1) * GATE_BLOCK)
            dlr_q = dlr[:, blk].astype(BF16)
            dli_q = dli[:, blk].astype(BF16)
            parts.append(_dot_nt(dlr_q, wg_ref[0, q]) + _dot_nt(dli_q, wg_ref[1, q]))
            dwg_ref[0, q] += _dot_tn(xcb[:, blk], dlr_q)
            dwg_ref[1, q] += _dot_tn(xcb[:, blk], dli_q)
        dxc = lt * mult * ig_v + jnp.concatenate(parts, axis=1)
        extd = jnp.concatenate([dxc, carry_scr[...]], axis=0)
        carry_scr[...] = dxc[0:CONV_HALO, :]
        cw = cw_ref[...]
        urx_v = urx_ref[...]
        durx = cw[3:4] * dxc
        dcw_ref[3:4, :] += jnp.sum(urx_v * dxc, axis=0, keepdims=True)
        for j in (1, 2, 3):
            dj = _shift_up(extd, j, ts)
            durx = durx + cw[3 - j:4 - j] * dj
            dcw_ref[3 - j:4 - j, :] += jnp.sum(urx_v * dj, axis=0, keepdims=True)
        durx_ref[...] = durx.astype(BF16)
        dcb_ref[...] += jnp.sum(dxc, axis=0, keepdims=True)

        @pl.when(i == nt - 1)
        def _():
            dlam_ref[...] = dlam_ref[...] * (LRU_C * jax.nn.sigmoid(-lam_v))

    row_spec = _rows(ts, D_MODEL, nt)
    halo_spec = pl.BlockSpec((CONV_HALO, D_MODEL), lambda i: (jnp.maximum((nt - 1 - i) * halo_blocks - 1, 0), 0))
    vec = _acc((1, D_MODEL))
    return _call(
        body, "bwd_rnn", (nt,),
        [row_spec] * 8 + [halo_spec, _resident(conv_w.shape), _resident(wg.shape), _resident((1, D_MODEL)),
                          _resident(w_rg_out.shape)],
        [row_spec, row_spec, _acc(w_rg_out.shape), _acc(wg.shape), _acc(conv_w.shape), vec, _acc((2, D_MODEL)), vec],
        [_sds((s, D_MODEL), BF16), _sds((s, D_MODEL), BF16), _sds(w_rg_out.shape), _sds(wg.shape), _sds(conv_w.shape),
         _sds((1, D_MODEL)), _sds((2, D_MODEL)), _sds((1, D_MODEL))],
        [pltpu.VMEM((8, D_MODEL), F32), pltpu.VMEM((CONV_HALO, D_MODEL), F32)],
        (dyr, urx, gl, gg, xc, r, ig, h, h, conv_w, wg, lam, w_rg_out), tasks)


def _bwd_pool_in(dyp, d, durx, durg, dgp, dgr, x, dx1, g_pre, w_in, pool_w, pool_scale, w_pool_out, ts, tasks=()):
    s = x.shape[0]
    nt = s // ts

    def body(dyp_ref, d_ref, durx_ref, durg_ref, dgp_ref, dgr_ref, x_ref, dx1_ref, g_ref, win_ref, pw_ref, ps_ref,
             wpo_ref, gx_ref, dwin_ref, dwpo_ref, dpw_ref, dps_ref, dg_ref, dz_scr, carry_scr):
        i = pl.program_id(0)
        k = nt - 1 - i

        @pl.when(i == 0)
        def _():
            carry_scr[...] = jnp.zeros_like(carry_scr)
            dwin_ref[...] = jnp.zeros_like(dwin_ref)
            dwpo_ref[...] = jnp.zeros_like(dwpo_ref)
            dpw_ref[...] = jnp.zeros_like(dpw_ref)
            dps_ref[...] = jnp.zeros_like(dps_ref)
            dg_ref[...] = jnp.zeros_like(dg_ref)

        dyp_v = dyp_ref[...]
        d_v = d_ref[...]
        ps = ps_ref[...]
        dypre = _dot_nt(dyp_v[:, 0:256], wpo_ref[0])
        for j in range(1, N_CHIPS):
            dypre = dypre + _dot_nt(dyp_v[:, j * 256:(j + 1) * 256], wpo_ref[j])
        y4 = jnp.concatenate([_dot(d_v[:, g * 128:(g + 1) * 128], pw_ref[g]) for g in range(POOL_GROUPS)], axis=1)
        ypre = (y4 * ps).astype(BF16)
        for j in range(N_CHIPS):
            dwpo_ref[j] += _dot_tn(ypre, dyp_v[:, j * 256:(j + 1) * 256])
        dps_ref[...] += jnp.sum(dypre * y4, axis=0, keepdims=True)
        dy4 = dypre * ps
        t = k * ts + lax.broadcasted_iota(jnp.int32, (ts, 1), 0)
        for g, w in enumerate(POOL_WINDOWS):
            lanes = slice(g * POOL_GROUP_DIM, (g + 1) * POOL_GROUP_DIM)
            dd = _dot_nt(dy4[:, lanes], pw_ref[g])
            dpw_ref[g] += _dot_tn(d_v[:, lanes], dy4[:, lanes])
            e = dd * (1.0 / jnp.minimum(t + 1, w).astype(F32))
            acc = jnp.concatenate([e, carry_scr[:, lanes]], axis=0)
            carry_scr[:, lanes] = e[0:POOL_HALO, :]
            n = ts + POOL_HALO
            sh = 1
            while sh < w:
                acc = acc + pltpu.roll(acc, n - sh, 0)
                sh *= 2
            dz_scr[:, lanes] = (acc[:ts, :] - dd).astype(BF16)
        dz_scr[:, 512:1536] = durx_ref[...]
        dz_scr[:, 1536:2560] = durg_ref[...]
        dz_scr[:, 2560:3584] = dgp_ref[...]
        dz_scr[:, 3584:4608] = dgr_ref[...]
        h1, xh, r = _rms_fwd(x_ref[...], g_ref[...])
        h1 = h1.astype(BF16)
        dh1 = _dot_nt(dz_scr[:, 0:IN_SHARD], win_ref[0])
        dwin_ref[0] += _dot_tn(h1, dz_scr[:, 0:IN_SHARD])
        for j in range(1, N_CHIPS):
            cols = slice(j * IN_SHARD, (j + 1) * IN_SHARD)
            dh1 = dh1 + _dot_nt(dz_scr[:, cols], win_ref[j])
            dwin_ref[j] += _dot_tn(h1, dz_scr[:, cols])
        dx, dg = _rms_bwd(xh, r, g_ref[...], dh1)
        dg_ref[...] += dg
        gx_ref[...] = dx1_ref[...] + dx

    row = _rows(ts, D_MODEL, nt)
    return _call(
        body, "bwd_pool_in", (nt,),
        [row, _rows(ts, POOL_WIDTH, nt)] + [row] * 6 + [_resident((1, D_MODEL)), _resident(w_in.shape),
                                                       _resident(pool_w.shape), _resident((1, POOL_WIDTH)),
                                                       _resident(w_pool_out.shape)],
        [row, _acc(w_in.shape), _acc(w_pool_out.shape), _acc(pool_w.shape), _acc((1, POOL_WIDTH)), _acc((1, D_MODEL))],
        [_sds((s, D_MODEL)), _sds(w_in.shape), _sds(w_pool_out.shape), _sds(pool_w.shape), _sds((1, POOL_WIDTH)),
         _sds((1, D_MODEL))],
        [pltpu.VMEM((ts, IN_TOTAL), BF16), pltpu.VMEM((POOL_HALO, POOL_WIDTH), F32)],
        (dyp, d, durx, durg, dgp, dgr, x, dx1, g_pre, w_in, pool_w, pool_scale, w_pool_out), tasks)


def _place():
    x, y, c = lax.axis_index("x"), lax.axis_index("y"), lax.axis_index("c")
    others = [(1 - x, y), (x, 1 - y), (1 - x, 1 - y)]
    return x, y, c, 2 * x + y, others


def _remote(src, dst, send_sem, recv_sem, to):
    return pltpu.make_async_remote_copy(src_ref=src, dst_ref=dst, send_sem=send_sem, recv_sem=recv_sem,
                                        device_id=to, device_id_type=MESH)


def _own_slot(w, place, dtype):
    a, b = w.shape
    tr = _row_tile(a, b)

    def body(place_ref, w_ref, out_ref):
        out_ref[0] = w_ref[...].astype(dtype)

    return pl.pallas_call(
        body, name="own_slot",
        grid_spec=pltpu.PrefetchScalarGridSpec(
            num_scalar_prefetch=1, grid=(a // tr,),
            in_specs=[pl.BlockSpec((tr, b), lambda i, s: (i, 0))],
            out_specs=pl.BlockSpec((1, tr, b), lambda i, s: (s[0], i, 0))),
        out_shape=_sds((N_CHIPS, a, b), dtype),
        compiler_params=pltpu.CompilerParams(dimension_semantics=("arbitrary",)),
    )(place, w)


def _run(task, name):
    return _call(None, name, (), [], [], [], [], (), (task,))[1][0]


def _gather_task(bufs):
    n = len(bufs)

    def first_copy(out, w, j, ox, oy, sems):
        x, y, c, me, _ = _place()
        ah = out.shape[1] // 2
        mine = out.at[me, pl.ds(c * ah, ah)]
        return _remote(mine, mine, sems[0].at[w, j], sems[1].at[w, j], (ox, oy, c))

    def passed_copy(out, w, j, ox, oy, sems, half):
        x, y, c, _, _ = _place()
        ah = out.shape[1] // 2
        slot = out.at[2 * ox + oy, pl.ds(half * ah, ah)]
        return _remote(slot, slot, sems[0].at[w, 3 + j], sems[1].at[w, 3 + j], (x, y, 1 - c))

    def start(ins, outs, sems):
        others = _place()[4]
        for w, out in enumerate(outs):
            for j, (ox, oy) in enumerate(others):
                first_copy(out, w, j, ox, oy, sems).start()

    def finish(ins, outs, sems):
        x, y, c, _, others = _place()
        for w, out in enumerate(outs):
            ah = out.shape[1] // 2
            for j, (ox, oy) in enumerate(others):
                slot = out.at[2 * ox + oy, pl.ds(c * ah, ah)]
                _remote(slot, slot, sems[0].at[w, j], sems[1].at[w, j], (ox, oy, c)).wait_recv()
                passed_copy(out, w, j, ox, oy, sems, c).start()
        for w, out in enumerate(outs):
            for j, (ox, oy) in enumerate(others):
                passed_copy(out, w, j, ox, oy, sems, 1 - c).wait_recv()
        for w, out in enumerate(outs):
            for j, (ox, oy) in enumerate(others):
                first_copy(out, w, j, ox, oy, sems).wait_send()
                passed_copy(out, w, j, ox, oy, sems, c).wait_send()

    return _Task(bufs, [_sds(b.shape, b.dtype) for b in bufs], {i: i for i in range(n)},
                 [pltpu.SemaphoreType.DMA((n, 6)), pltpu.SemaphoreType.DMA((n, 6))], start, finish)


def _halves_task(grads):
    n = len(grads)

    def copy(src, out, w, sems):
        x, y, c, _, _ = _place()
        ah = out.shape[1]
        return _remote(src.at[:, pl.ds((1 - c) * ah, ah)], out, sems[0].at[w], sems[1].at[w], (x, y, 1 - c))

    def start(ins, outs, sems):
        for w, (src, out) in enumerate(zip(ins, outs)):
            copy(src, out, w, sems).start()

    def finish(ins, outs, sems):
        for w, (src, out) in enumerate(zip(ins, outs)):
            copy(src, out, w, sems).wait()

    return _Task(grads, [_sds((g.shape[0], g.shape[1] // 2, g.shape[2]), g.dtype) for g in grads], {},
                 [pltpu.SemaphoreType.DMA((n,)), pltpu.SemaphoreType.DMA((n,))], start, finish)


def _exchange_task(sends, accs):
    n = len(accs)
    given = [s for s in sends if s is not None]

    def copies(ins, outs, sems):
        send_refs = iter(ins[:len(given)])
        srcs = [next(send_refs) if s is not None else None for s in sends]
        x, y, c, me, others = _place()
        for w, out in enumerate(outs):
            for j, (ox, oy) in enumerate(others):
                src = out.at[me] if srcs[w] is None else srcs[w].at[2 * ox + oy]
                yield _remote(src, out.at[me], sems[0].at[w, j], sems[1].at[w, j], (ox, oy, c))

    def start(ins, outs, sems):
        for cp in copies(ins, outs, sems):
            cp.start()

    def finish(ins, outs, sems):
        x, y, c, _, others = _place()
        for w, out in enumerate(outs):
            for j, (ox, oy) in enumerate(others):
                slot = out.at[2 * ox + oy]
                _remote(slot, slot, sems[0].at[w, j], sems[1].at[w, j], (ox, oy, c)).wait_recv()
        for cp in copies(ins, outs, sems):
            cp.wait_send()

    return _Task(given + list(accs), [_sds(a.shape, a.dtype) for a in accs], {len(given) + i: i for i in range(n)},
                 [pltpu.SemaphoreType.DMA((n, 3)), pltpu.SemaphoreType.DMA((n, 3))], start, finish)


def _share_task(shares):
    n = len(shares)

    def copy(out, w, sems, slot):
        x, y, c, _, _ = _place()
        return _remote(out.at[slot], out.at[slot], sems[0].at[w], sems[1].at[w], (x, y, 1 - c))

    def start(ins, outs, sems):
        c = _place()[2]
        for w, out in enumerate(outs):
            copy(out, w, sems, c).start()

    def finish(ins, outs, sems):
        c = _place()[2]
        for w, out in enumerate(outs):
            copy(out, w, sems, 1 - c).wait_recv()
        for w, out in enumerate(outs):
            copy(out, w, sems, c).wait_send()

    return _Task(shares, [_sds(s.shape, s.dtype) for s in shares], {i: i for i in range(n)},
                 [pltpu.SemaphoreType.DMA((n,)), pltpu.SemaphoreType.DMA((n,))], start, finish)


TILE_BYTES = 2 * 1024 * 1024
PARTIAL_TILE_BYTES = 512 * 1024


def _row_tile(rows, cols, limit=TILE_BYTES):
    best = 8
    for tr in range(8, rows + 1, 8):
        if rows % tr == 0 and tr * cols * 4 <= limit:
            best = tr
    assert rows % best == 0, (rows, cols)
    return best


def _chip_partial(g, got, place, wire_dtype):
    ns, ah, b = got.shape
    sharded = ns == N_CHIPS
    tr = _row_tile(ah, b, PARTIAL_TILE_BYTES)
    nb = ah // tr

    def body(place_ref, *refs):
        g_refs, got_refs, outs = refs[:ns], refs[ns:2 * ns], refs[2 * ns:]
        parts = [g_refs[k][0] + got_refs[k][0] for k in range(ns)]
        own = parts[0]
        if sharded:
            for k in range(ns):
                outs[0][k] = parts[k].astype(wire_dtype)
                if k:
                    own = jnp.where(place_ref[0] == k, parts[k], own)
        outs[-1][0] = own.astype(wire_dtype)

    blk = (1, tr, b)
    in_specs = ([pl.BlockSpec(blk, lambda i, s, k=k: (k, s[1] * nb + i, 0)) for k in range(ns)]
                + [pl.BlockSpec(blk, lambda i, s, k=k: (k, i, 0)) for k in range(ns)])
    acc_spec = pl.BlockSpec(blk, lambda i, s: (s[0], i, 0))
    acc_shape = _sds((N_CHIPS, ah, b), wire_dtype)
    out = pl.pallas_call(
        body, name="grad_chip_partial",
        grid_spec=pltpu.PrefetchScalarGridSpec(
            num_scalar_prefetch=1, grid=(nb,), in_specs=in_specs,
            out_specs=[pl.BlockSpec((ns, tr, b), lambda i, s: (0, i, 0)), acc_spec] if sharded else [acc_spec]),
        out_shape=[acc_shape, acc_shape] if sharded else [acc_shape],
        compiler_params=pltpu.CompilerParams(dimension_semantics=("arbitrary",), vmem_limit_bytes=32 * 1024 * 1024),
    )(place, *([g] * ns), *([got] * ns))
    return (out[0], out[1]) if sharded else (None, out[0])


def _chip_sum(acc, place):
    _, ah, b = acc.shape
    tr = _row_tile(ah, b)

    def body(place_ref, p_ref, out_ref):
        total = p_ref[0].astype(F32) + p_ref[1].astype(F32)
        total = total + p_ref[2].astype(F32)
        out_ref[0] = total + p_ref[3].astype(F32)

    return pl.pallas_call(
        body, name="grad_chip_sum",
        grid_spec=pltpu.PrefetchScalarGridSpec(
            num_scalar_prefetch=1, grid=(ah // tr,),
            in_specs=[pl.BlockSpec((N_CHIPS, tr, b), lambda i, s: (0, i, 0))],
            out_specs=pl.BlockSpec((1, tr, b), lambda i, s: (s[1], i, 0))),
        out_shape=_sds((2, ah, b)),
        compiler_params=pltpu.CompilerParams(dimension_semantics=("arbitrary",)),
    )(place, acc)


def _adam_math(w, g, m, v):
    nm = ADAM_B1 * m + (1.0 - ADAM_B1) * g
    nv = ADAM_B2 * v + (1.0 - ADAM_B2) * (g * g)
    m_hat = nm / (1.0 - ADAM_B1 ** ADAM_STEP)
    v_hat = nv / (1.0 - ADAM_B2 ** ADAM_STEP)
    return -ADAM_LR * (m_hat / (jnp.sqrt(v_hat) + ADAM_EPS) + ADAM_WD * w), nm, nv


def _adamw(w, g, m, v):
    a, b = w.shape
    tr = _row_tile(a, b)

    def body(w_ref, g_ref, m_ref, v_ref, g_out, d_ref, nm_ref, nv_ref):
        g_out[...] = g_ref[...]
        d_ref[...], nm_ref[...], nv_ref[...] = _adam_math(w_ref[...], g_ref[...], m_ref[...], v_ref[...])

    blk = pl.BlockSpec((tr, b), lambda i: (i, 0))
    return pl.pallas_call(
        body, name="adamw", grid=(a // tr,),
        in_specs=[blk] * 4, out_specs=[blk] * 4, out_shape=[_sds((a, b))] * 4,
        compiler_params=pltpu.CompilerParams(dimension_semantics=("arbitrary",)),
    )(w, g, m, v)


def _adamw_pieces(g, pieces, name):
    n = len(pieces)

    def body(g_ref, *refs):
        ins, outs = refs[:3 * n], refs[3 * n:]
        for i, piece in enumerate(pieces):
            w_ref, m_ref, v_ref = ins[3 * i:3 * i + 3]
            o_g, o_d, o_m, o_v = outs[4 * i:4 * i + 4]
            if len(piece) == 5:
                g_v = g_ref[piece[3], piece[4]]
                o_g[...] = g_v
                o_d[...], o_m[...], o_v[...] = _adam_math(w_ref[...], g_v, m_ref[...], v_ref[...])
            else:
                for r in range(w_ref.shape[1] // SMALL_COLS):
                    lanes = slice(r * SMALL_COLS, (r + 1) * SMALL_COLS)
                    g_v = g_ref[piece[3] + r:piece[3] + r + 1, :]
                    o_g[:, lanes] = g_v
                    o_d[:, lanes], o_m[:, lanes], o_v[:, lanes] = _adam_math(w_ref[:, lanes], g_v, m_ref[:, lanes],
                                                                            v_ref[:, lanes])

    operands = [t for piece in pieces for t in piece[:3]]
    out = pl.pallas_call(
        body, name=name,
        out_shape=[_sds(piece[0].shape) for piece in pieces for _ in range(4)],
    )(g, *operands)
    return [tuple(out[4 * i:4 * i + 4]) for i in range(n)]


TINY_ROWS, TINY_COLS = 16, 768
SMALL_COLS = 128
SMALL_ROWS = 624


def _pack_tiny(conv_w, b_gates, fcw):
    ns = conv_w.shape[0]
    pad = lambda t: jnp.pad(t, ((0, 0), (0, 0), (0, TINY_COLS - t.shape[2])))
    z = lambda rows: jnp.zeros((ns, rows, TINY_COLS), F32)
    return jnp.concatenate([pad(conv_w), pad(b_gates), z(2), fcw, z(5)], axis=1)


def _unpack_tiny(t):
    return t[:, 0:4, 0:256], t[:, 4:6, 0:256], t[:, 8:11, :]


def _cols_to_shards(t, n):
    return t.reshape(t.shape[0], N_CHIPS, n).transpose(1, 0, 2)


def _shards_to_cols(t):
    return t.transpose(1, 0, 2).reshape(t.shape[1], -1)


_VECTORS = ("g_mix_pre", "g_mix_post", "conv_b", "lru_lambda", "g_ffn_pre", "g_ffn_post", "g_ple_gate", "g_ple_post",
            "pool_scale", "ffn_conv_b")
_VECTOR_LEN = {"pool_scale": POOL_WIDTH, "ffn_conv_b": D_FF}
POOL_W_ROWS = POOL_GROUPS * POOL_GROUP_DIM


def _vector_rows():
    rows, row = {}, POOL_W_ROWS
    for k in _VECTORS:
        rows[k] = row
        row += max(8, _VECTOR_LEN.get(k, D_MODEL) // SMALL_COLS)
    return rows, row


def _pack_small(grads, loss):
    tiles = lambda t: jnp.pad(t, ((0, -t.shape[0] % 8), (0, 0)))
    parts = [grads["pool_w"].reshape(POOL_W_ROWS, SMALL_COLS)] + [tiles(grads[k].reshape(-1, SMALL_COLS)) for k in _VECTORS]
    parts.append(tiles(loss))
    used = sum(t.shape[0] for t in parts)
    return jnp.concatenate(parts + [jnp.zeros((SMALL_ROWS - used, SMALL_COLS), F32)], axis=0)


def _gates_block_diag(w):
    w4 = w.reshape(2, GATE_BLOCKS, 4, RNN_HEAD_DIM, RNN_HEAD_DIM)
    eye = jnp.eye(4, dtype=w.dtype)
    return jnp.einsum("gqhij,hk->gqhikj", w4, eye).reshape(2, GATE_BLOCKS, GATE_BLOCK, GATE_BLOCK)


def _gates_from_block_diag(dw):
    d6 = dw.reshape(2, GATE_BLOCKS, 4, RNN_HEAD_DIM, 4, RNN_HEAD_DIM)
    blocks = [d6[:, :, hh, :, hh, :] for hh in range(4)]
    return jnp.stack(blocks, axis=2).reshape(2, RNN_HEADS, RNN_HEAD_DIM, RNN_HEAD_DIM)


ROW_TILE = 256
DW_TILES = 4

_SHARDED = ("w_in", "w_pool_out", "w_rg_out", "w_o", "w_up", "w_down", "w_ple_gate", "w_ple_proj")
_WEIGHTS = ("g_mix_pre", "g_mix_post", "w_in", "pool_w", "pool_scale", "w_pool_out", "conv_w", "conv_b", "w_rg_gates",
            "b_rg_gates", "lru_lambda", "w_rg_out", "w_o", "g_ffn_pre", "g_ffn_post", "w_up", "ffn_conv_w", "ffn_conv_b",
            "w_down", "g_ple_gate", "w_ple_gate", "w_ple_proj", "g_ple_post")


def _wire_dtype(g):
    return BF16 if g.shape[1] >= 64 and g.shape[2] > SMALL_COLS else F32


def _partials(grads, got, place):
    parts = [_chip_partial(g, r, place, _wire_dtype(g)) for g, r in zip(grads, got)]
    return [send for send, _ in parts], [acc for _, acc in parts]


def _whole(both):
    return [b.reshape(2 * b.shape[1], b.shape[2]) for b in both]


def _step(x, p, tgt, own, rep, place, ts):
    vec = lambda k: rep[k].reshape(1, -1)
    pool_w = rep["pool_w"].astype(BF16)
    wg = _gates_block_diag(rep["w_rg_gates"]).astype(BF16)
    sq = lambda t: t.reshape(D_MODEL, D_MODEL)
    by4 = lambda t: t.reshape(N_CHIPS, -1, D_MODEL)

    first, ride1, ride2, ride3 = (("w_in", "w_pool_out", "tiny"), ("w_rg_out", "w_o", "w_down"), ("w_up",),
                                  ("w_ple_gate", "w_ple_proj"))
    full = dict(zip(first, _run(_gather_task([own[k] for k in first]), "gather_first")))
    conv_w, b_gates, fcw = [_shards_to_cols(t) for t in _unpack_tiny(full["tiny"])]

    (urx, urg, gp, gr, d, ypool), (got,) = _fwd_in_pool(
        x, vec("g_mix_pre"), full["w_in"], pool_w, vec("pool_scale"), full["w_pool_out"], ts,
        [_gather_task([own[k] for k in ride1])])
    full.update(zip(ride1, got))
    w_rg_out, w_o, w_down = sq(full["w_rg_out"]), sq(full["w_o"]), full["w_down"].reshape(D_FF, D_MODEL)
    (xc, r, ig, h, yrnn, mo, x1, glr, ggr, sp, sr), (got,) = _fwd_rnn_merge(
        urx, urg, gp, gr, ypool, x, conv_w, vec("conv_b"), wg, b_gates, vec("lru_lambda"), w_rg_out, w_o,
        vec("g_mix_post"), ts, [_gather_task([own[k] for k in ride2])])
    full.update(zip(ride2, got))
    (up, gl, gg, h2, dn, x2), (got,) = _fwd_ffn(x1, vec("g_ffn_pre"), full["w_up"], fcw, vec("ffn_conv_b"), w_down,
                                               vec("g_ffn_post"), ts, [_gather_task([own[k] for k in ride3])])
    full.update(zip(ride3, got))
    dx2, loss, d_w_gate, d_w_proj, d_g_ple_gate, d_g_ple_post = _ple_loss(
        x2, p, tgt, vec("g_ple_gate"), sq(full["w_ple_gate"]), full["w_ple_proj"], vec("g_ple_post"), ts)
    dup, d_w_down, d_fcw, d_fcb, d_g_ffn_post = _bwd_ffn_down(dx2, dn, up, gl, gg, fcw, w_down, vec("g_ffn_post"), ts)

    names1, grads1 = ("w_ple_gate", "w_ple_proj", "w_down"), [by4(d_w_gate), d_w_proj, by4(d_w_down)]
    (dx1, d_g_ffn_pre), (got1,) = _bwd_ffn_up(dup, x1, dx2, vec("g_ffn_pre"), full["w_up"], ts, [_halves_task(grads1)])
    (d_w_up,), (accs1,) = _dw_up(h2, dup, ts, [_exchange_task(*_partials(grads1, got1, place))])
    shares1 = [_chip_sum(acc, place) for acc in accs1]
    (dgp, dgr, dyp, dyr, d_w_o, d_g_mix_post), (got2, both1) = _bwd_merge(
        dx1, mo, sp, sr, ypool, yrnn, vec("g_mix_post"), w_o, ts, [_halves_task([d_w_up]), _share_task(shares1)])
    (durx, durg, d_w_rg_out, d_wg, d_conv_w, d_conv_b, d_b_gates, d_lam), (accs2,) = _bwd_rnn(
        dyr, urx, glr, ggr, xc, r, ig, h, conv_w, wg, vec("lru_lambda"), w_rg_out, ts,
        [_exchange_task(*_partials([d_w_up], got2, place))])
    shares2 = [_chip_sum(acc, place) for acc in accs2]
    (grad_x, d_w_in, d_w_pool_out, d_pool_w, d_pool_scale, d_g_mix_pre), _ = _bwd_pool_in(
        dyp, d, durx, durg, dgp, dgr, x, dx1, vec("g_mix_pre"), full["w_in"], pool_w, vec("pool_scale"),
        full["w_pool_out"], ts)

    replicated = {"g_mix_pre": d_g_mix_pre, "g_mix_post": d_g_mix_post, "conv_b": d_conv_b, "lru_lambda": d_lam,
                  "g_ffn_pre": d_g_ffn_pre, "g_ffn_post": d_g_ffn_post, "g_ple_gate": d_g_ple_gate,
                  "g_ple_post": d_g_ple_post, "pool_scale": d_pool_scale, "ffn_conv_b": d_fcb, "pool_w": d_pool_w}
    names3 = ("w_in", "w_pool_out", "w_rg_out", "w_o", "tiny", "small", "w_rg_gates")
    grads3 = [d_w_in, d_w_pool_out, by4(d_w_rg_out), by4(d_w_o),
              _pack_tiny(_cols_to_shards(d_conv_w, 256), _cols_to_shards(d_b_gates, 256), _cols_to_shards(d_fcw, 768)),
              _pack_small(replicated, loss)[None],
              _gates_from_block_diag(d_wg).reshape(1, 2 * RNN_HEADS * RNN_HEAD_DIM, RNN_HEAD_DIM)]
    got3 = _run(_halves_task(grads3), "grad_sibling_halves")
    accs3 = _run(_exchange_task(*_partials(grads3, got3, place)), "grad_chip_exchange")
    both = _run(_share_task(shares2 + [_chip_sum(acc, place) for acc in accs3]), "grad_sibling_share")

    return grad_x, dict(zip(names1 + ("w_up",) + names3, _whole(both1) + _whole(both)))


def kernel(x, p, g_mix_pre, g_mix_post, w_in, pool_w, pool_scale, w_pool_out, conv_w, conv_b, w_rg_gates, b_rg_gates, lru_lambda, w_rg_out, w_o, g_ffn_pre, g_ffn_post, w_up, ffn_conv_w, ffn_conv_b, w_down, g_ple_gate, w_ple_gate, w_ple_proj, g_ple_post, loss_target, m_g_mix_pre, m_g_mix_post, m_w_in, m_pool_w, m_pool_scale, m_w_pool_out, m_conv_w, m_conv_b, m_w_rg_gates, m_b_rg_gates, m_lru_lambda, m_w_rg_out, m_w_o, m_g_ffn_pre, m_g_ffn_post, m_w_up, m_ffn_conv_w, m_ffn_conv_b, m_w_down, m_g_ple_gate, m_w_ple_gate, m_w_ple_proj, m_g_ple_post, v_g_mix_pre, v_g_mix_post, v_w_in, v_pool_w, v_pool_scale, v_w_pool_out, v_conv_w, v_conv_b, v_w_rg_gates, v_b_rg_gates, v_lru_lambda, v_w_rg_out, v_w_o, v_g_ffn_pre, v_g_ffn_post, v_w_up, v_ffn_conv_w, v_ffn_conv_b, v_w_down, v_g_ple_gate, v_w_ple_gate, v_w_ple_proj, v_g_ple_post):
    args = dict(locals())
    w = {k: args[k][0] for k in _WEIGHTS}
    m = {k: args["m_" + k][0] for k in _WEIGHTS}
    v = {k: args["v_" + k][0] for k in _WEIGHTS}
    place = jnp.stack([2 * lax.axis_index("x") + lax.axis_index("y"), lax.axis_index("c")]).astype(jnp.int32)
    own = {k: _own_slot(w[k], place, BF16) for k in _SHARDED}
    own["tiny"] = _own_slot(_pack_tiny(w["conv_w"][None], w["b_rg_gates"][None], w["ffn_conv_w"][None])[0], place, F32)
    grad_x, reduced = _step(x[0], p[0, 0], loss_target[0], own, w, place, ROW_TILE)

    gates_2d = (2 * RNN_HEADS * RNN_HEAD_DIM, RNN_HEAD_DIM)
    as2d = lambda k, shape: tuple(t[k].reshape(shape) for t in (w, m, v))
    done = {k: tuple(_adamw(w[k], reduced[k], m[k], v[k])) for k in _SHARDED}
    gates_w, gates_m, gates_v = as2d("w_rg_gates", gates_2d)
    done["w_rg_gates"] = tuple(_adamw(gates_w, reduced["w_rg_gates"], gates_m, gates_v))
    tiny_names = ("conv_w", "b_rg_gates", "ffn_conv_w")
    tiny_at = ((slice(0, 4), slice(0, 256)), (slice(4, 6), slice(0, 256)), (slice(8, 11), slice(None)))
    done.update(zip(tiny_names, _adamw_pieces(
        reduced["tiny"], [(w[k], m[k], v[k]) + at for k, at in zip(tiny_names, tiny_at)], "adamw_tiny")))
    vector_rows, loss_row = _vector_rows()
    pieces = [as2d("pool_w", (POOL_W_ROWS, SMALL_COLS)) + (slice(0, POOL_W_ROWS), slice(None))]
    pieces += [as2d(k, (1, -1)) + (vector_rows[k],) for k in _VECTORS]
    done.update(zip(("pool_w",) + _VECTORS, _adamw_pieces(reduced["small"], pieces, "adamw_small")))

    result = [reduced["small"][loss_row, 0], grad_x[None]]
    for kind in range(4):
        result += [done[k][kind].reshape(args[k].shape) for k in _WEIGHTS]
    return tuple(result)
```

```python
import functools

import jax
import jax.numpy as jnp
from jax import lax
from jax.experimental import pallas as pl
from jax.experimental.pallas import tpu as pltpu

F32 = jnp.float32
BF16 = jnp.bfloat16

D_MODEL = 1024
POOL_WINDOWS = (2, 4, 8, 16)
POOL_GROUPS = 4
POOL_WIDTH = 512
POOL_GROUP_DIM = 128
RNN_HEADS = 16
RNN_HEAD_DIM = 64
GATE_BLOCK = 256
GATE_BLOCKS = D_MODEL // GATE_BLOCK
LRU_C = 8.0
D_FF = 3072
PLE_DIM = 256
RMS_EPS = 1e-6
IN_TOTAL = 4608
N_CHIPS = 4
IN_SHARD = IN_TOTAL // N_CHIPS
UP_SHARD = 2 * D_FF // N_CHIPS
POOL_HALO = 16
CONV_HALO = 8

ADAM_LR = 0.001
ADAM_B1 = 0.9
ADAM_B2 = 0.999
ADAM_EPS = 1e-08
ADAM_WD = 0.01
ADAM_STEP = 10

VMEM_LIMIT = 56 * 1024 * 1024
MESH = pl.DeviceIdType.MESH

_GELU_C = 0.7978845608028654
_GELU_A = 0.044715


def _dot(a, b):
    return jnp.dot(a.astype(BF16), b.astype(BF16), preferred_element_type=F32)


def _dot_nt(a, b):
    return lax.dot_general(a.astype(BF16), b.astype(BF16), (((1,), (1,)), ((), ())), preferred_element_type=F32)


def _dot_tn(a, b):
    return lax.dot_general(a.astype(BF16), b.astype(BF16), (((0,), (0,)), ((), ())), preferred_element_type=F32)


def _rms_fwd(x, g):
    r = lax.rsqrt(jnp.mean(x * x, axis=-1, keepdims=True) + RMS_EPS)
    xh = x * r
    return xh * g, xh, r


def _rms_bwd(xh, r, g, dy):
    dxh = dy * g
    dg = jnp.sum(dy * xh, axis=0, keepdims=True)
    dx = r * (dxh - xh * jnp.mean(dxh * xh, axis=-1, keepdims=True))
    return dx, dg


def _sigmoid(x):
    return 0.5 * jnp.tanh(0.5 * x) + 0.5


def _gelu(x):
    t = jnp.tanh(_GELU_C * (x + _GELU_A * x * x * x))
    return 0.5 * x * (1.0 + t), t


def _gelu_grad(x, t):
    return 0.5 * (1.0 + t) + 0.5 * x * (1.0 - t * t) * _GELU_C * (1.0 + 3.0 * _GELU_A * x * x)


def _softplus_neg(lam):
    nl = -lam
    return jnp.maximum(nl, 0.0) + jnp.log(1.0 + jnp.exp(-jnp.abs(nl)))


def _lru_coeffs(r, lam, first_row):
    c8 = LRU_C * _softplus_neg(lam)
    la = -(c8 * r)
    a = jnp.exp(la)
    m2 = jnp.tanh(-la) * (1.0 + a * a)
    mult = jnp.where(first_row, 1.0, jnp.sqrt(m2))
    return c8, a, m2, mult


SUBLANES = 8


def _scan_fwd(a, u, carry):
    n = a.shape[0]
    sub = lax.broadcasted_iota(jnp.int32, (n, 1), 0) % SUBLANES
    acc_a, acc_h = a, u
    for s in (1, 2, 4):
        m = sub >= s
        h_s = jnp.where(m, pltpu.roll(acc_h, s, 0), 0.0)
        a_s = jnp.where(m, pltpu.roll(acc_a, s, 0), 1.0)
        acc_h = acc_a * h_s + acc_h
        acc_a = acc_a * a_s
    out = []
    for g in range(n // SUBLANES):
        rows = slice(g * SUBLANES, (g + 1) * SUBLANES)
        out.append(acc_h[rows] + acc_a[rows] * carry)
        carry = out[-1][SUBLANES - 1:SUBLANES]
    return jnp.concatenate(out, axis=0)


def _scan_bwd(b, g, carry):
    n = b.shape[0]
    sub = lax.broadcasted_iota(jnp.int32, (n, 1), 0) % SUBLANES
    acc_b, acc_l = b, g
    for s in (1, 2, 4):
        m = sub < SUBLANES - s
        l_s = jnp.where(m, pltpu.roll(acc_l, n - s, 0), 0.0)
        b_s = jnp.where(m, pltpu.roll(acc_b, n - s, 0), 1.0)
        acc_l = acc_b * l_s + acc_l
        acc_b = acc_b * b_s
    out = [None] * (n // SUBLANES)
    for g in reversed(range(n // SUBLANES)):
        rows = slice(g * SUBLANES, (g + 1) * SUBLANES)
        out[g] = acc_l[rows] + acc_b[rows] * carry
        carry = out[g][0:1]
    return jnp.concatenate(out, axis=0)


def _shift_down(ext, k, halo):
    return pltpu.roll(ext, k, 0)[halo:] if k else ext[halo:]


def _shift_up(ext, k, ts):
    return pltpu.roll(ext, ext.shape[0] - k, 0)[:ts] if k else ext[:ts]


def _rows(ts, width, nt=None, col=0):
    if nt is None:
        return pl.BlockSpec((ts, width), lambda i: (i, col))
    return pl.BlockSpec((ts, width), lambda i: (nt - 1 - i, col))


def _resident(shape):
    zeros = (0,) * len(shape)
    return pl.BlockSpec(shape, lambda i: zeros, pipeline_mode=pl.Buffered(1))


def _acc(shape):
    zeros = (0,) * len(shape)
    return pl.BlockSpec(shape, lambda i: zeros)


def _params():
    return pltpu.CompilerParams(dimension_semantics=("arbitrary",), vmem_limit_bytes=VMEM_LIMIT)


def _sds(shape, dtype=F32):
    return jax.ShapeDtypeStruct(shape, dtype)


class _Task:
    def __init__(self, ins, out_shapes, aliases, sems, start, finish):
        self.ins, self.out_shapes, self.aliases, self.sems = list(ins), list(out_shapes), dict(aliases), list(sems)
        self.start, self.finish = start, finish


def _call(body, name, grid, in_specs, out_specs, out_shape, scratch_shapes, args, tasks=()):
    n_in, n_out, n_scr = len(in_specs), len(out_specs), len(scratch_shapes)
    t_in = [len(t.ins) for t in tasks]
    t_out = [len(t.out_shapes) for t in tasks]
    t_sem = [len(t.sems) for t in tasks]
    steps = 1
    for g in grid:
        steps *= g

    def take(refs, pos, counts):
        groups = []
        for c in counts:
            groups.append(refs[pos:pos + c])
            pos += c
        return groups, pos

    def wrapped(*refs):
        (cin,), pos = take(refs, 0, [n_in])
        tin, pos = take(refs, pos, t_in)
        (cout,), pos = take(refs, pos, [n_out])
        tout, pos = take(refs, pos, t_out)
        (cscr,), pos = take(refs, pos, [n_scr])
        tsem, pos = take(refs, pos, t_sem)
        if not grid:
            for t, a, b, c in zip(tasks, tin, tout, tsem):
                t.start(a, b, c)
            if body is not None:
                body(*cin, *cout, *cscr)
            for t, a, b, c in zip(tasks, tin, tout, tsem):
                t.finish(a, b, c)
            return
        step = pl.program_id(0)
        for axis in range(1, len(grid)):
            step = step * grid[axis] + pl.program_id(axis)
        if tasks:
            @pl.when(step == 0)
            def _():
                for t, a, b, c in zip(tasks, tin, tout, tsem):
                    t.start(a, b, c)

        body(*cin, *cout, *cscr)
        if tasks:
            @pl.when(step == steps - 1)
            def _():
                for t, a, b, c in zip(tasks, tin, tout, tsem):
                    t.finish(a, b, c)

    aliases, in_pos, out_pos = {}, n_in, n_out
    for t, ni, no in zip(tasks, t_in, t_out):
        aliases.update({in_pos + a: out_pos + b for a, b in t.aliases.items()})
        in_pos, out_pos = in_pos + ni, out_pos + no
    any_spec = pl.BlockSpec(memory_space=pltpu.HBM)
    kwargs = dict(grid=grid, compiler_params=pltpu.CompilerParams(
        dimension_semantics=("arbitrary",) * len(grid), vmem_limit_bytes=VMEM_LIMIT)) if grid else dict(
        compiler_params=pltpu.CompilerParams(vmem_limit_bytes=VMEM_LIMIT))
    out = pl.pallas_call(
        wrapped, name=name,
        in_specs=list(in_specs) + [any_spec] * sum(t_in),
        out_specs=list(out_specs) + [any_spec] * sum(t_out),
        out_shape=list(out_shape) + [s for t in tasks for s in t.out_shapes],
        scratch_shapes=list(scratch_shapes) + [s for t in tasks for s in t.sems],
        input_output_aliases=aliases, **kwargs,
    )(*args, *[pltpu.with_memory_space_constraint(a, pltpu.HBM) for t in tasks for a in t.ins])
    task_outs, pos = take(list(out), n_out, t_out)
    return list(out[:n_out]), task_outs


def _fwd_in_pool(x, g_pre, w_in, pool_w, pool_scale, w_pool_out, ts, tasks=()):
    s = x.shape[0]

    def body(x_ref, g_ref, win_ref, pw_ref, ps_ref, wpo_ref,
             urx_ref, urg_ref, gp_ref, gr_ref, d_ref, yp_ref, z_scr, halo_scr):
        i = pl.program_id(0)

        @pl.when(i == 0)
        def _():
            halo_scr[...] = jnp.zeros_like(halo_scr)

        h1, _, _ = _rms_fwd(x_ref[...], g_ref[...])
        h1 = h1.astype(BF16)
        for j in range(N_CHIPS):
            z_scr[:, j * IN_SHARD:(j + 1) * IN_SHARD] = jnp.dot(h1, win_ref[j], preferred_element_type=F32)
        urx_ref[...] = z_scr[:, 512:1536]
        urg_ref[...] = z_scr[:, 1536:2560]
        gp_ref[...] = z_scr[:, 2560:3584]
        gr_ref[...] = z_scr[:, 3584:4608]
        u = z_scr[:, 0:POOL_WIDTH]
        ext = jnp.concatenate([halo_scr[...], u], axis=0)
        halo_scr[...] = u[ts - POOL_HALO:, :]
        t = i * ts + lax.broadcasted_iota(jnp.int32, (ts, 1), 0)
        y4 = []
        for g, w in enumerate(POOL_WINDOWS):
            lanes = slice(g * POOL_GROUP_DIM, (g + 1) * POOL_GROUP_DIM)
            acc = ext[:, lanes]
            sh = 1
            while sh < w:
                acc = acc + pltpu.roll(acc, sh, 0)
                sh *= 2
            inv = 1.0 / jnp.minimum(t + 1, w).astype(F32)
            dg = acc[POOL_HALO:, :] * inv - u[:, lanes]
            d_ref[:, lanes] = dg
            y4.append(_dot(dg, pw_ref[g]))
        ypre = jnp.concatenate(y4, axis=1) * ps_ref[...]
        ypre = ypre.astype(BF16)
        for j in range(N_CHIPS):
            yp_ref[:, j * 256:(j + 1) * 256] = jnp.dot(ypre, wpo_ref[j], preferred_element_type=F32)

    return _call(
        body, "fwd_in_pool", (s // ts,),
        [_rows(ts, D_MODEL), _resident((1, D_MODEL)), _resident(w_in.shape), _resident(pool_w.shape),
         _resident((1, POOL_WIDTH)), _resident(w_pool_out.shape)],
        [_rows(ts, D_MODEL)] * 4 + [_rows(ts, POOL_WIDTH), _rows(ts, D_MODEL)],
        [_sds((s, D_MODEL))] * 4 + [_sds((s, POOL_WIDTH)), _sds((s, D_MODEL))],
        [pltpu.VMEM((ts, IN_TOTAL), F32), pltpu.VMEM((POOL_HALO, POOL_WIDTH), F32)],
        (x, g_pre, w_in, pool_w, pool_scale, w_pool_out), tasks)


def _fwd_rnn_merge(urx, urg, gp, gr, ypool, x, conv_w, conv_b, wg, bg, lam, w_rg_out, w_o, g_post, ts, tasks=()):
    s = x.shape[0]

    def body(urx_ref, urg_ref, gp_ref, gr_ref, yp_ref, x_ref, cw_ref, cb_ref, wg_ref, bg_ref, lam_ref, wrg_ref, wo_ref,
             gpost_ref, xc_ref, r_ref, ig_ref, h_ref, yr_ref, mo_ref, x1_ref, gl_ref, gg_ref, sp_ref, sr_ref,
             halo_scr, carry_scr):
        i = pl.program_id(0)

        @pl.when(i == 0)
        def _():
            halo_scr[...] = jnp.zeros_like(halo_scr)
            carry_scr[...] = jnp.zeros_like(carry_scr)

        urx_v = urx_ref[...]
        ext = jnp.concatenate([halo_scr[...], urx_v], axis=0)
        halo_scr[...] = urx_v[ts - CONV_HALO:, :]
        cw = cw_ref[...]
        xc = (cb_ref[...] + cw[3:4] * urx_v + cw[2:3] * _shift_down(ext, 1, CONV_HALO)
              + cw[1:2] * _shift_down(ext, 2, CONV_HALO) + cw[0:1] * _shift_down(ext, 3, CONV_HALO))
        xc_ref[...] = xc
        xcb = xc.astype(BF16)
        lin = []
        for gate in range(2):
            parts = [jnp.dot(xcb[:, q * GATE_BLOCK:(q + 1) * GATE_BLOCK], wg_ref[gate, q], preferred_element_type=F32)
                     for q in range(GATE_BLOCKS)]
            lin.append(jnp.concatenate(parts, axis=1) + bg_ref[gate:gate + 1, :])
        r = _sigmoid(lin[0])
        ig = _sigmoid(lin[1])
        r_ref[...] = r
        ig_ref[...] = ig
        first_row = (i * ts + lax.broadcasted_iota(jnp.int32, (ts, 1), 0)) == 0
        _, a, _, mult = _lru_coeffs(r, lam_ref[...], first_row)
        h = _scan_fwd(a, mult * ig * xc, carry_scr[0:1, :])
        carry_scr[0:1, :] = h[ts - 1:ts, :]
        h_ref[...] = h
        urg_v = urg_ref[...]
        gl, t = _gelu(urg_v)
        gl_ref[...] = gl.astype(BF16)
        gg_ref[...] = _gelu_grad(urg_v, t).astype(BF16)
        yr = _dot(h * gl, wrg_ref[...])
        yr_ref[...] = yr
        sp = _sigmoid(gp_ref[...])
        sr = _sigmoid(gr_ref[...])
        sp_ref[...] = sp.astype(BF16)
        sr_ref[...] = sr.astype(BF16)
        merged = sp * yp_ref[...] + sr * yr
        mo = _dot(merged, wo_ref[...])
        mo_ref[...] = mo
        y, _, _ = _rms_fwd(mo, gpost_ref[...])
        x1_ref[...] = x_ref[...] + y

    row = _rows(ts, D_MODEL)
    return _call(
        body, "fwd_rnn_merge", (s // ts,),
        [row] * 6 + [_resident(conv_w.shape), _resident((1, D_MODEL)), _resident(wg.shape), _resident(bg.shape),
                     _resident((1, D_MODEL)), _resident(w_rg_out.shape), _resident(w_o.shape), _resident((1, D_MODEL))],
        [row] * 11, [_sds((s, D_MODEL))] * 7 + [_sds((s, D_MODEL), BF16)] * 4,
        [pltpu.VMEM((CONV_HALO, D_MODEL), F32), pltpu.VMEM((8, D_MODEL), F32)],
        (urx, urg, gp, gr, ypool, x, conv_w, conv_b, wg, bg, lam, w_rg_out, w_o, g_post), tasks)


def _fwd_ffn(x1, g_pre, w_up, fcw, fcb, w_down, g_post, ts, tasks=()):
    s = x1.shape[0]

    def body(x1_ref, g_ref, wup_ref, fcw_ref, fcb_ref, wd_ref, gpost_ref,
             up_ref, gl_ref, gg_ref, h2_ref, dn_ref, x2_ref, up_scr, halo_scr):
        i = pl.program_id(0)

        @pl.when(i == 0)
        def _():
            halo_scr[...] = jnp.zeros_like(halo_scr)

        x1_v = x1_ref[...]
        h2, _, _ = _rms_fwd(x1_v, g_ref[...])
        h2 = h2.astype(BF16)
        h2_ref[...] = h2
        for j in range(N_CHIPS):
            up_scr[:, j * UP_SHARD:(j + 1) * UP_SHARD] = jnp.dot(h2, wup_ref[j], preferred_element_type=F32)
        up_ref[...] = up_scr[...].astype(BF16)
        ug = up_scr[:, 0:D_FF]
        ext = jnp.concatenate([halo_scr[...], ug], axis=0)
        halo_scr[...] = ug[ts - CONV_HALO:, :]
        w = fcw_ref[...]
        gh = (fcb_ref[...] + w[2:3] * ug + w[1:2] * _shift_down(ext, 1, CONV_HALO)
              + w[0:1] * _shift_down(ext, 2, CONV_HALO))
        gl, t = _gelu(gh)
        gl_ref[...] = gl.astype(BF16)
        gg_ref[...] = _gelu_grad(gh, t).astype(BF16)
        dn = _dot(gl * up_scr[:, D_FF:], wd_ref[...])
        dn_ref[...] = dn
        y, _, _ = _rms_fwd(dn, gpost_ref[...])
        x2_ref[...] = x1_v + y

    row = _rows(ts, D_MODEL)
    return _call(
        body, "fwd_ffn", (s // ts,),
        [row, _resident((1, D_MODEL)), _resident(w_up.shape), _resident(fcw.shape), _resident((1, D_FF)),
         _resident(w_down.shape), _resident((1, D_MODEL))],
        [_rows(ts, 2 * D_FF), _rows(ts, D_FF), _rows(ts, D_FF), row, row, row],
        [_sds((s, 2 * D_FF), BF16), _sds((s, D_FF), BF16), _sds((s, D_FF), BF16), _sds((s, D_MODEL), BF16),
         _sds((s, D_MODEL)), _sds((s, D_MODEL))],
        [pltpu.VMEM((ts, 2 * D_FF), F32), pltpu.VMEM((CONV_HALO, D_FF), F32)],
        (x1, g_pre, w_up, fcw, fcb, w_down, g_post), tasks)


def _ple_loss(x2, p, tgt, g_gate, w_gate, w_proj, g_post, ts):
    s = x2.shape[0]

    def body(x2_ref, p_ref, t_ref, gg_ref, wg_ref, wp_ref, gp_ref, dx2_ref, loss_ref, dwg_ref, dwp_ref, dgg_ref, dgp_ref):
        @pl.when(pl.program_id(0) == 0)
        def _():
            loss_ref[...] = jnp.zeros_like(loss_ref)
            dwg_ref[...] = jnp.zeros_like(dwg_ref)
            dwp_ref[...] = jnp.zeros_like(dwp_ref)
            dgg_ref[...] = jnp.zeros_like(dgg_ref)
            dgp_ref[...] = jnp.zeros_like(dgp_ref)

        x2_v = x2_ref[...]
        n3, xh3, r3 = _rms_fwd(x2_v, gg_ref[...])
        pg = _sigmoid(_dot(n3, wg_ref[...]))
        pb = p_ref[...].astype(BF16)
        q = jnp.concatenate([jnp.dot(pb, wp_ref[j], preferred_element_type=F32) for j in range(N_CHIPS)], axis=1)
        ple, qh, rq = _rms_fwd(q, gp_ref[...])
        e = x2_v + pg * ple - t_ref[...]
        loss_ref[...] += 0.5 * jnp.sum(jnp.mean(e * e, axis=-1, keepdims=True), axis=0, keepdims=True)
        dy = e * (1.0 / D_MODEL)
        dpgl = dy * ple * pg * (1.0 - pg)
        dwg_ref[...] += _dot_tn(n3, dpgl)
        dx3, dgg = _rms_bwd(xh3, r3, gg_ref[...], _dot_nt(dpgl, wg_ref[...]))
        dgg_ref[...] += dgg
        dq, dgp = _rms_bwd(qh, rq, gp_ref[...], dy * pg)
        dgp_ref[...] += dgp
        for j in range(N_CHIPS):
            dwp_ref[j] += _dot_tn(pb, dq[:, j * 256:(j + 1) * 256])
        dx2_ref[...] = dy + dx3

    row = _rows(ts, D_MODEL)
    vec = _acc((1, D_MODEL))
    return pl.pallas_call(
        body, name="ple_loss", grid=(s // ts,),
        in_specs=[row, _rows(ts, PLE_DIM), row, _resident((1, D_MODEL)), _resident(w_gate.shape), _resident(w_proj.shape),
                  _resident((1, D_MODEL))],
        out_specs=[row, _acc((1, 128)), _acc(w_gate.shape), _acc(w_proj.shape), vec, vec],
        out_shape=[_sds((s, D_MODEL)), _sds((1, 128)), _sds(w_gate.shape), _sds(w_proj.shape), _sds((1, D_MODEL)),
                   _sds((1, D_MODEL))],
        compiler_params=_params(),
    )(x2, p, tgt, g_gate, w_gate, w_proj, g_post)


def _bwd_ffn_down(dx2, dn, up, gl, gg, fcw, w_down, g_post, ts):
    s = dx2.shape[0]
    nt = s // ts

    def body(dx2_ref, dn_ref, up_ref, gl_ref, gg_ref, fcw_ref, wd_ref, gpost_ref,
             dup_ref, dwd_ref, dfcw_ref, dfcb_ref, dgp_ref, carry_scr):
        i = pl.program_id(0)

        @pl.when(i == 0)
        def _():
            carry_scr[...] = jnp.zeros_like(carry_scr)
            dwd_ref[...] = jnp.zeros_like(dwd_ref)
            dfcw_ref[...] = jnp.zeros_like(dfcw_ref)
            dfcb_ref[...] = jnp.zeros_like(dfcb_ref)
            dgp_ref[...] = jnp.zeros_like(dgp_ref)

        _, xh, r = _rms_fwd(dn_ref[...], gpost_ref[...])
        ddn, dgp = _rms_bwd(xh, r, gpost_ref[...], dx2_ref[...])
        dgp_ref[...] += dgp
        dhid = _dot_nt(ddn, wd_ref[...])
        ug = up_ref[:, 0:D_FF].astype(F32)
        uv = up_ref[:, D_FF:].astype(F32)
        gl = gl_ref[...].astype(F32)
        w = fcw_ref[...]
        dwd_ref[...] += _dot_tn(gl * uv, ddn)
        dgh = dhid * uv * gg_ref[...].astype(F32)
        dup_ref[:, D_FF:] = (dhid * gl).astype(BF16)
        extd = jnp.concatenate([dgh, carry_scr[...]], axis=0)
        carry_scr[...] = dgh[0:CONV_HALO, :]
        d1 = _shift_up(extd, 1, ts)
        d2 = _shift_up(extd, 2, ts)
        dup_ref[:, 0:D_FF] = (w[2:3] * dgh + w[1:2] * d1 + w[0:1] * d2).astype(BF16)
        dfcw_ref[2:3, :] += jnp.sum(ug * dgh, axis=0, keepdims=True)
        dfcw_ref[1:2, :] += jnp.sum(ug * d1, axis=0, keepdims=True)
        dfcw_ref[0:1, :] += jnp.sum(ug * d2, axis=0, keepdims=True)
        dfcb_ref[...] += jnp.sum(dgh, axis=0, keepdims=True)

    row = _rows(ts, D_MODEL, nt)
    wide = _rows(ts, D_FF, nt)
    return pl.pallas_call(
        body, name="bwd_ffn_down", grid=(nt,),
        in_specs=[row, row, _rows(ts, 2 * D_FF, nt), wide, wide, _resident(fcw.shape), _resident(w_down.shape),
                  _resident((1, D_MODEL))],
        out_specs=[_rows(ts, 2 * D_FF, nt), _acc(w_down.shape), _acc(fcw.shape), _acc((1, D_FF)), _acc((1, D_MODEL))],
        out_shape=[_sds((s, 2 * D_FF), BF16), _sds(w_down.shape), _sds(fcw.shape), _sds((1, D_FF)), _sds((1, D_MODEL))],
        scratch_shapes=[pltpu.VMEM((CONV_HALO, D_FF), F32)],
        compiler_params=_params(),
    )(dx2, dn, up, gl, gg, fcw, w_down, g_post)


def _bwd_ffn_up(dup, x1, dx2, g_pre, w_up, ts, tasks=()):
    s = x1.shape[0]

    def body(dup_ref, x1_ref, dx2_ref, g_ref, wup_ref, dx1_ref, dg_ref):
        @pl.when(pl.program_id(0) == 0)
        def _():
            dg_ref[...] = jnp.zeros_like(dg_ref)

        _, xh, r = _rms_fwd(x1_ref[...], g_ref[...])
        dh2 = _dot_nt(dup_ref[:, 0:UP_SHARD], wup_ref[0])
        for j in range(1, N_CHIPS):
            dh2 = dh2 + _dot_nt(dup_ref[:, j * UP_SHARD:(j + 1) * UP_SHARD], wup_ref[j])
        dx, dg = _rms_bwd(xh, r, g_ref[...], dh2)
        dg_ref[...] += dg
        dx1_ref[...] = dx2_ref[...] + dx

    row = _rows(ts, D_MODEL)
    return _call(
        body, "bwd_ffn_up", (s // ts,),
        [_rows(ts, 2 * D_FF), row, row, _resident((1, D_MODEL)), _resident(w_up.shape)],
        [row, _acc((1, D_MODEL))], [_sds((s, D_MODEL)), _sds((1, D_MODEL))], [],
        (dup, x1, dx2, g_pre, w_up), tasks)


def _dw_up(h2, dup, ts, tasks=()):
    s = h2.shape[0]
    ts = min(DW_TILES * ts, s)

    def body(h2_ref, dup_ref, out_ref):
        @pl.when(pl.program_id(1) == 0)
        def _():
            out_ref[...] = jnp.zeros_like(out_ref)

        out_ref[0] += _dot_tn(h2_ref[...], dup_ref[...])

    return _call(
        body, "dw_up", (N_CHIPS, s // ts),
        [pl.BlockSpec((ts, D_MODEL), lambda j, i: (i, 0)), pl.BlockSpec((ts, UP_SHARD), lambda j, i: (i, j))],
        [pl.BlockSpec((1, D_MODEL, UP_SHARD), lambda j, i: (j, 0, 0))], [_sds((N_CHIPS, D_MODEL, UP_SHARD))], [],
        (h2, dup), tasks)


def _bwd_merge(dx1, mo, sp, sr, ypool, yrnn, g_post, w_o, ts, tasks=()):
    s = dx1.shape[0]

    def body(dx1_ref, mo_ref, sp_ref, sr_ref, yp_ref, yr_ref, g_ref, wo_ref,
             dgp_ref, dgr_ref, dyp_ref, dyr_ref, dwo_ref, dg_ref):
        @pl.when(pl.program_id(0) == 0)
        def _():
            dwo_ref[...] = jnp.zeros_like(dwo_ref)
            dg_ref[...] = jnp.zeros_like(dg_ref)

        _, xh, r = _rms_fwd(mo_ref[...], g_ref[...])
        dmo, dg = _rms_bwd(xh, r, g_ref[...], dx1_ref[...])
        dg_ref[...] += dg
        dmerged = _dot_nt(dmo, wo_ref[...])
        sp = sp_ref[...].astype(F32)
        sr = sr_ref[...].astype(F32)
        yp = yp_ref[...]
        yr = yr_ref[...]
        dwo_ref[...] += _dot_tn(sp * yp + sr * yr, dmo)
        dgp_ref[...] = (dmerged * yp * sp * (1.0 - sp)).astype(BF16)
        dgr_ref[...] = (dmerged * yr * sr * (1.0 - sr)).astype(BF16)
        dyp_ref[...] = (dmerged * sp).astype(BF16)
        dyr_ref[...] = (dmerged * sr).astype(BF16)

    row = _rows(ts, D_MODEL)
    return _call(
        body, "bwd_merge", (s // ts,),
        [row] * 6 + [_resident((1, D_MODEL)), _resident(w_o.shape)],
        [row] * 4 + [_acc(w_o.shape), _acc((1, D_MODEL))],
        [_sds((s, D_MODEL), BF16)] * 4 + [_sds(w_o.shape), _sds((1, D_MODEL))], [],
        (dx1, mo, sp, sr, ypool, yrnn, g_post, w_o), tasks)


def _bwd_rnn(dyr, urx, gl, gg, xc, r, ig, h, conv_w, wg, lam, w_rg_out, ts, tasks=()):
    s = urx.shape[0]
    nt = s // ts
    halo_blocks = ts // CONV_HALO

    def body(dyr_ref, urx_ref, gl_ref, gg_ref, xc_ref, r_ref, ig_ref, h_ref, hh_ref, cw_ref, wg_ref, lam_ref, wrg_ref,
             durx_ref, durg_ref, dwrg_ref, dwg_ref, dcw_ref, dcb_ref, dbg_ref, dlam_ref, mu_scr, carry_scr):
        i = pl.program_id(0)
        k = nt - 1 - i

        @pl.when(i == 0)
        def _():
            mu_scr[...] = jnp.zeros_like(mu_scr)
            carry_scr[...] = jnp.zeros_like(carry_scr)
            dwrg_ref[...] = jnp.zeros_like(dwrg_ref)
            dwg_ref[...] = jnp.zeros_like(dwg_ref)
            dcw_ref[...] = jnp.zeros_like(dcw_ref)
            dcb_ref[...] = jnp.zeros_like(dcb_ref)
            dbg_ref[...] = jnp.zeros_like(dbg_ref)
            dlam_ref[...] = jnp.zeros_like(dlam_ref)

        row = lax.broadcasted_iota(jnp.int32, (ts, 1), 0)
        first_row = (k * ts + row) == 0
        h = h_ref[...]
        dyr_v = dyr_ref[...]
        dhr = _dot_nt(dyr_v, wrg_ref[...])
        gl = gl_ref[...].astype(F32)
        dwrg_ref[...] += _dot_tn(h * gl, dyr_v)
        durg_ref[...] = (dhr * h * gg_ref[...].astype(F32)).astype(BF16)
        r_v = r_ref[...]
        ig_v = ig_ref[...]
        xc_v = xc_ref[...]
        lam_v = lam_ref[...]
        c8, a, m2, mult = _lru_coeffs(r_v, lam_v, first_row)
        b = jnp.where(row == ts - 1, 1.0, pltpu.roll(a, ts - 1, 0))
        lt = _scan_bwd(b, dhr * gl, mu_scr[0:1, :])
        mu_scr[0:1, :] = a[0:1, :] * lt[0:1, :]
        h_before = jnp.where(k > 0, hh_ref[CONV_HALO - 1:CONV_HALO, :], 0.0)
        hprev = jnp.where(row == 0, h_before, pltpu.roll(h, 1, 0))
        dmult = lt * ig_v * xc_v
        da = lt * hprev - jnp.where(first_row, 0.0, dmult * a * lax.rsqrt(m2))
        dla = da * a
        dlam_ref[...] += jnp.sum(dla * r_v, axis=0, keepdims=True)
        dlr = (dla * (-c8)) * r_v * (1.0 - r_v)
        dli = (lt * mult * xc_v) * ig_v * (1.0 - ig_v)
        dbg_ref[0:1, :] += jnp.sum(dlr, axis=0, keepdims=True)
        dbg_ref[1:2, :] += jnp.sum(dli, axis=0, keepdims=True)
        xcb = xc_v.astype(BF16)
        parts = []
        for q in range(GATE_BLOCKS):
            blk = slice(q * GATE_BLOCK, (q + 1) * GATE_BLOCK)
            dlr_q = dlr[:, blk].astype(BF16)
            dli_q = dli[:, blk].astype(BF16)
            parts.append(_dot_nt(dlr_q, wg_ref[0, q]) + _dot_nt(dli_q, wg_ref[1, q]))
            dwg_ref[0, q] += _dot_tn(xcb[:, blk], dlr_q)
            dwg_ref[1, q] += _dot_tn(xcb[:, blk], dli_q)
        dxc = lt * mult * ig_v + jnp.concatenate(parts, axis=1)
        extd = jnp.concatenate([dxc, carry_scr[...]], axis=0)
        carry_scr[...] = dxc[0:CONV_HALO, :]
        cw = cw_ref[...]
        urx_v = urx_ref[...]
        durx = cw[3:4] * dxc
        dcw_ref[3:4, :] += jnp.sum(urx_v * dxc, axis=0, keepdims=True)
        for j in (1, 2, 3):
            dj = _shift_up(extd, j, ts)
            durx = durx + cw[3 - j:4 - j] * dj
            dcw_ref[3 - j:4 - j, :] += jnp.sum(urx_v * dj, axis=0, keepdims=True)
        durx_ref[...] = durx.astype(BF16)
        dcb_ref[...] += jnp.sum(dxc, axis=0, keepdims=True)

        @pl.when(i == nt - 1)
        def _():
            dlam_ref[...] = dlam_ref[...] * (LRU_C * jax.nn.sigmoid(-lam_v))

    row_spec = _rows(ts, D_MODEL, nt)
    halo_spec = pl.BlockSpec((CONV_HALO, D_MODEL), lambda i: (jnp.maximum((nt - 1 - i) * halo_blocks - 1, 0), 0))
    vec = _acc((1, D_MODEL))
    return _call(
        body, "bwd_rnn", (nt,),
        [row_spec] * 8 + [halo_spec, _resident(conv_w.shape), _resident(wg.shape), _resident((1, D_MODEL)),
                          _resident(w_rg_out.shape)],
        [row_spec, row_spec, _acc(w_rg_out.shape), _acc(wg.shape), _acc(conv_w.shape), vec, _acc((2, D_MODEL)), vec],
        [_sds((s, D_MODEL), BF16), _sds((s, D_MODEL), BF16), _sds(w_rg_out.shape), _sds(wg.shape), _sds(conv_w.shape),
         _sds((1, D_MODEL)), _sds((2, D_MODEL)), _sds((1, D_MODEL))],
        [pltpu.VMEM((8, D_MODEL), F32), pltpu.VMEM((CONV_HALO, D_MODEL), F32)],
        (dyr, urx, gl, gg, xc, r, ig, h, h, conv_w, wg, lam, w_rg_out), tasks)


def _bwd_pool_in(dyp, d, durx, durg, dgp, dgr, x, dx1, g_pre, w_in, pool_w, pool_scale, w_pool_out, ts, tasks=()):
    s = x.shape[0]
    nt = s // ts

    def body(dyp_ref, d_ref, durx_ref, durg_ref, dgp_ref, dgr_ref, x_ref, dx1_ref, g_ref, win_ref, pw_ref, ps_ref,
             wpo_ref, gx_ref, dwin_ref, dwpo_ref, dpw_ref, dps_ref, dg_ref, dz_scr, carry_scr):
        i = pl.program_id(0)
        k = nt - 1 - i

        @pl.when(i == 0)
        def _():
            carry_scr[...] = jnp.zeros_like(carry_scr)
            dwin_ref[...] = jnp.zeros_like(dwin_ref)
            dwpo_ref[...] = jnp.zeros_like(dwpo_ref)
            dpw_ref[...] = jnp.zeros_like(dpw_ref)
            dps_ref[...] = jnp.zeros_like(dps_ref)
            dg_ref[...] = jnp.zeros_like(dg_ref)

        dyp_v = dyp_ref[...]
        d_v = d_ref[...]
        ps = ps_ref[...]
        dypre = _dot_nt(dyp_v[:, 0:256], wpo_ref[0])
        for j in range(1, N_CHIPS):
            dypre = dypre + _dot_nt(dyp_v[:, j * 256:(j + 1) * 256], wpo_ref[j])
        y4 = jnp.concatenate([_dot(d_v[:, g * 128:(g + 1) * 128], pw_ref[g]) for g in range(POOL_GROUPS)], axis=1)
        ypre = (y4 * ps).astype(BF16)
        for j in range(N_CHIPS):
            dwpo_ref[j] += _dot_tn(ypre, dyp_v[:, j * 256:(j + 1) * 256])
        dps_ref[...] += jnp.sum(dypre * y4, axis=0, keepdims=True)
        dy4 = dypre * ps
        t = k * ts + lax.broadcasted_iota(jnp.int32, (ts, 1), 0)
        for g, w in enumerate(POOL_WINDOWS):
            lanes = slice(g * POOL_GROUP_DIM, (g + 1) * POOL_GROUP_DIM)
            dd = _dot_nt(dy4[:, lanes], pw_ref[g])
            dpw_ref[g] += _dot_tn(d_v[:, lanes], dy4[:, lanes])
            e = dd * (1.0 / jnp.minimum(t + 1, w).astype(F32))
            acc = jnp.concatenate([e, carry_scr[:, lanes]], axis=0)
            carry_scr[:, lanes] = e[0:POOL_HALO, :]
            n = ts + POOL_HALO
            sh = 1
            while sh < w:
                acc = acc + pltpu.roll(acc, n - sh, 0)
                sh *= 2
            dz_scr[:, lanes] = (acc[:ts, :] - dd).astype(BF16)
        dz_scr[:, 512:1536] = durx_ref[...]
        dz_scr[:, 1536:2560] = durg_ref[...]
        dz_scr[:, 2560:3584] = dgp_ref[...]
        dz_scr[:, 3584:4608] = dgr_ref[...]
        h1, xh, r = _rms_fwd(x_ref[...], g_ref[...])
        h1 = h1.astype(BF16)
        dh1 = _dot_nt(dz_scr[:, 0:IN_SHARD], win_ref[0])
        dwin_ref[0] += _dot_tn(h1, dz_scr[:, 0:IN_SHARD])
        for j in range(1, N_CHIPS):
            cols = slice(j * IN_SHARD, (j + 1) * IN_SHARD)
            dh1 = dh1 + _dot_nt(dz_scr[:, cols], win_ref[j])
            dwin_ref[j] += _dot_tn(h1, dz_scr[:, cols])
        dx, dg = _rms_bwd(xh, r, g_ref[...], dh1)
        dg_ref[...] += dg
        gx_ref[...] = dx1_ref[...] + dx

    row = _rows(ts, D_MODEL, nt)
    return _call(
        body, "bwd_pool_in", (nt,),
        [row, _rows(ts, POOL_WIDTH, nt)] + [row] * 6 + [_resident((1, D_MODEL)), _resident(w_in.shape),
                                                       _resident(pool_w.shape), _resident((1, POOL_WIDTH)),
                                                       _resident(w_pool_out.shape)],
        [row, _acc(w_in.shape), _acc(w_pool_out.shape), _acc(pool_w.shape), _acc((1, POOL_WIDTH)), _acc((1, D_MODEL))],
        [_sds((s, D_MODEL)), _sds(w_in.shape), _sds(w_pool_out.shape), _sds(pool_w.shape), _sds((1, POOL_WIDTH)),
         _sds((1, D_MODEL))],
        [pltpu.VMEM((ts, IN_TOTAL), BF16), pltpu.VMEM((POOL_HALO, POOL_WIDTH), F32)],
        (dyp, d, durx, durg, dgp, dgr, x, dx1, g_pre, w_in, pool_w, pool_scale, w_pool_out), tasks)


def _place():
    x, y, c = lax.axis_index("x"), lax.axis_index("y"), lax.axis_index("c")
    others = [(1 - x, y), (x, 1 - y), (1 - x, 1 - y)]
    return x, y, c, 2 * x + y, others


def _remote(src, dst, send_sem, recv_sem, to):
    return pltpu.make_async_remote_copy(src_ref=src, dst_ref=dst, send_sem=send_sem, recv_sem=recv_sem,
                                        device_id=to, device_id_type=MESH)


def _own_slots(ws, dtypes, name, tasks=()):
    n = len(ws)
    hbm = pl.BlockSpec(memory_space=pltpu.HBM)

    def body(*refs):
        srcs, outs, f32_bufs, cast_bufs, sems = refs[:n], refs[n:2 * n], refs[2 * n:3 * n], refs[3 * n:4 * n], refs[4 * n]
        me = _place()[3]
        loads = [pltpu.make_async_copy(srcs[k], f32_bufs[k], sems.at[k, 0]) for k in range(n)]
        stores = [pltpu.make_async_copy(cast_bufs[k], outs[k].at[me], sems.at[k, 1]) for k in range(n)]
        for cp in loads:
            cp.start()
        for k in range(n):
            loads[k].wait()
            cast_bufs[k][...] = f32_bufs[k][...].astype(dtypes[k])
            stores[k].start()
        for cp in stores:
            cp.wait()

    return _call(
        body, name, (), [hbm] * n, [hbm] * n, [_sds((N_CHIPS,) + w.shape, dt) for w, dt in zip(ws, dtypes)],
        [pltpu.VMEM(w.shape, F32) for w in ws] + [pltpu.VMEM(w.shape, dt) for w, dt in zip(ws, dtypes)]
        + [pltpu.SemaphoreType.DMA((n, 2))],
        [pltpu.with_memory_space_constraint(w, pltpu.HBM) for w in ws], tasks)


def _run(task, name):
    return _call(None, name, (), [], [], [], [], (), (task,))[1][0]


def _gather_task(bufs):
    n = len(bufs)

    def first_copy(out, w, j, ox, oy, sems):
        x, y, c, me, _ = _place()
        ah = out.shape[1] // 2
        mine = out.at[me, pl.ds(c * ah, ah)]
        return _remote(mine, mine, sems[0].at[w, j], sems[1].at[w, j], (ox, oy, c))

    def passed_copy(out, w, j, ox, oy, sems, half):
        x, y, c, _, _ = _place()
        ah = out.shape[1] // 2
        slot = out.at[2 * ox + oy, pl.ds(half * ah, ah)]
        return _remote(slot, slot, sems[0].at[w, 3 + j], sems[1].at[w, 3 + j], (x, y, 1 - c))

    def start(ins, outs, sems):
        others = _place()[4]
        for w, out in enumerate(outs):
            for j, (ox, oy) in enumerate(others):
                first_copy(out, w, j, ox, oy, sems).start()

    def finish(ins, outs, sems):
        x, y, c, _, others = _place()
        for w, out in enumerate(outs):
            ah = out.shape[1] // 2
            for j, (ox, oy) in enumerate(others):
                slot = out.at[2 * ox + oy, pl.ds(c * ah, ah)]
                _remote(slot, slot, sems[0].at[w, j], sems[1].at[w, j], (ox, oy, c)).wait_recv()
                passed_copy(out, w, j, ox, oy, sems, c).start()
        for w, out in enumerate(outs):
            for j, (ox, oy) in enumerate(others):
                passed_copy(out, w, j, ox, oy, sems, 1 - c).wait_recv()
        for w, out in enumerate(outs):
            for j, (ox, oy) in enumerate(others):
                first_copy(out, w, j, ox, oy, sems).wait_send()
                passed_copy(out, w, j, ox, oy, sems, c).wait_send()

    return _Task(bufs, [_sds(b.shape, b.dtype) for b in bufs], {i: i for i in range(n)},
                 [pltpu.SemaphoreType.DMA((n, 6)), pltpu.SemaphoreType.DMA((n, 6))], start, finish)


def _halves_task(grads):
    n = len(grads)

    def copy(src, out, w, sems):
        x, y, c, _, _ = _place()
        ah = out.shape[1]
        return _remote(src.at[:, pl.ds((1 - c) * ah, ah)], out, sems[0].at[w], sems[1].at[w], (x, y, 1 - c))

    def start(ins, outs, sems):
        for w, (src, out) in enumerate(zip(ins, outs)):
            copy(src, out, w, sems).start()

    def finish(ins, outs, sems):
        for w, (src, out) in enumerate(zip(ins, outs)):
            copy(src, out, w, sems).wait()

    return _Task(grads, [_sds((g.shape[0], g.shape[1] // 2, g.shape[2]), g.dtype) for g in grads], {},
                 [pltpu.SemaphoreType.DMA((n,)), pltpu.SemaphoreType.DMA((n,))], start, finish)


def _exchange_task(sends, accs):
    n = len(accs)
    given = [s for s in sends if s is not None]

    def copies(ins, outs, sems):
        send_refs = iter(ins[:len(given)])
        srcs = [next(send_refs) if s is not None else None for s in sends]
        x, y, c, me, others = _place()
        for w, out in enumerate(outs):
            for j, (ox, oy) in enumerate(others):
                src = out.at[me] if srcs[w] is None else srcs[w].at[2 * ox + oy]
                yield _remote(src, out.at[me], sems[0].at[w, j], sems[1].at[w, j], (ox, oy, c))

    def start(ins, outs, sems):
        for cp in copies(ins, outs, sems):
            cp.start()

    def finish(ins, outs, sems):
        x, y, c, _, others = _place()
        for w, out in enumerate(outs):
            for j, (ox, oy) in enumerate(others):
                slot = out.at[2 * ox + oy]
                _remote(slot, slot, sems[0].at[w, j], sems[1].at[w, j], (ox, oy, c)).wait_recv()
        for cp in copies(ins, outs, sems):
            cp.wait_send()

    return _Task(given + list(accs), [_sds(a.shape, a.dtype) for a in accs], {len(given) + i: i for i in range(n)},
                 [pltpu.SemaphoreType.DMA((n, 3)), pltpu.SemaphoreType.DMA((n, 3))], start, finish)


def _share_task(shares):
    n = len(shares)

    def copy(out, w, sems, slot):
        x, y, c, _, _ = _place()
        return _remote(out.at[slot], out.at[slot], sems[0].at[w], sems[1].at[w], (x, y, 1 - c))

    def start(ins, outs, sems):
        c = _place()[2]
        for w, out in enumerate(outs):
            copy(out, w, sems, c).start()

    def finish(ins, outs, sems):
        c = _place()[2]
        for w, out in enumerate(outs):
            copy(out, w, sems, 1 - c).wait_recv()
        for w, out in enumerate(outs):
            copy(out, w, sems, c).wait_send()

    return _Task(shares, [_sds(s.shape, s.dtype) for s in shares], {i: i for i in range(n)},
                 [pltpu.SemaphoreType.DMA((n,)), pltpu.SemaphoreType.DMA((n,))], start, finish)


TILE_BYTES = 2 * 1024 * 1024
PARTIAL_TILE_BYTES = 512 * 1024


def _row_tile(rows, cols, limit=TILE_BYTES):
    best = 8
    for tr in range(8, rows + 1, 8):
        if rows % tr == 0 and tr * cols * 4 <= limit:
            best = tr
    assert rows % best == 0, (rows, cols)
    return best


def _chip_partial(g, got, place, wire_dtype):
    ns, ah, b = got.shape
    sharded = ns == N_CHIPS
    tr = _row_tile(ah, b, PARTIAL_TILE_BYTES)
    nb = ah // tr

    def body(place_ref, *refs):
        g_refs, got_refs, outs = refs[:ns], refs[ns:2 * ns], refs[2 * ns:]
        parts = [g_refs[k][0] + got_refs[k][0] for k in range(ns)]
        own = parts[0]
        if sharded:
            for k in range(ns):
                outs[0][k] = parts[k].astype(wire_dtype)
                if k:
                    own = jnp.where(place_ref[0] == k, parts[k], own)
        outs[-1][0] = own.astype(wire_dtype)

    blk = (1, tr, b)
    in_specs = ([pl.BlockSpec(blk, lambda i, s, k=k: (k, s[1] * nb + i, 0)) for k in range(ns)]
                + [pl.BlockSpec(blk, lambda i, s, k=k: (k, i, 0)) for k in range(ns)])
    acc_spec = pl.BlockSpec(blk, lambda i, s: (s[0], i, 0))
    acc_shape = _sds((N_CHIPS, ah, b), wire_dtype)
    out = pl.pallas_call(
        body, name="grad_chip_partial",
        grid_spec=pltpu.PrefetchScalarGridSpec(
            num_scalar_prefetch=1, grid=(nb,), in_specs=in_specs,
            out_specs=[pl.BlockSpec((ns, tr, b), lambda i, s: (0, i, 0)), acc_spec] if sharded else [acc_spec]),
        out_shape=[acc_shape, acc_shape] if sharded else [acc_shape],
        compiler_params=pltpu.CompilerParams(dimension_semantics=("arbitrary",), vmem_limit_bytes=32 * 1024 * 1024),
    )(place, *([g] * ns), *([got] * ns))
    return (out[0], out[1]) if sharded else (None, out[0])


def _chip_sum(acc, place):
    _, ah, b = acc.shape
    tr = _row_tile(ah, b)

    def body(place_ref, p_ref, out_ref):
        total = p_ref[0].astype(F32) + p_ref[1].astype(F32)
        total = total + p_ref[2].astype(F32)
        out_ref[0] = total + p_ref[3].astype(F32)

    return pl.pallas_call(
        body, name="grad_chip_sum",
        grid_spec=pltpu.PrefetchScalarGridSpec(
            num_scalar_prefetch=1, grid=(ah // tr,),
            in_specs=[pl.BlockSpec((N_CHIPS, tr, b), lambda i, s: (0, i, 0))],
            out_specs=pl.BlockSpec((1, tr, b), lambda i, s: (s[1], i, 0))),
        out_shape=_sds((2, ah, b)),
        compiler_params=pltpu.CompilerParams(dimension_semantics=("arbitrary",)),
    )(place, acc)


def _adam_math(w, g, m, v):
    nm = ADAM_B1 * m + (1.0 - ADAM_B1) * g
    nv = ADAM_B2 * v + (1.0 - ADAM_B2) * (g * g)
    m_hat = nm / (1.0 - ADAM_B1 ** ADAM_STEP)
    v_hat = nv / (1.0 - ADAM_B2 ** ADAM_STEP)
    return -ADAM_LR * (m_hat / (jnp.sqrt(v_hat) + ADAM_EPS) + ADAM_WD * w), nm, nv


def _adamw(w, g, m, v):
    a, b = w.shape
    tr = _row_tile(a, b)

    def body(w_ref, g_ref, m_ref, v_ref, g_out, d_ref, nm_ref, nv_ref):
        g_out[...] = g_ref[...]
        d_ref[...], nm_ref[...], nv_ref[...] = _adam_math(w_ref[...], g_ref[...], m_ref[...], v_ref[...])

    blk = pl.BlockSpec((tr, b), lambda i: (i, 0))
    return pl.pallas_call(
        body, name="adamw", grid=(a // tr,),
        in_specs=[blk] * 4, out_specs=[blk] * 4, out_shape=[_sds((a, b))] * 4,
        compiler_params=pltpu.CompilerParams(dimension_semantics=("arbitrary",)),
    )(w, g, m, v)


def _adamw_pieces(g, pieces, name):
    n = len(pieces)

    def body(g_ref, *refs):
        ins, outs = refs[:3 * n], refs[3 * n:]
        for i, piece in enumerate(pieces):
            w_ref, m_ref, v_ref = ins[3 * i:3 * i + 3]
            o_g, o_d, o_m, o_v = outs[4 * i:4 * i + 4]
            if len(piece) == 5:
                g_v = g_ref[piece[3], piece[4]]
                o_g[...] = g_v
                o_d[...], o_m[...], o_v[...] = _adam_math(w_ref[...], g_v, m_ref[...], v_ref[...])
            else:
                for r in range(w_ref.shape[1] // SMALL_COLS):
                    lanes = slice(r * SMALL_COLS, (r + 1) * SMALL_COLS)
                    g_v = g_ref[piece[3] + r:piece[3] + r + 1, :]
                    o_g[:, lanes] = g_v
                    o_d[:, lanes], o_m[:, lanes], o_v[:, lanes] = _adam_math(w_ref[:, lanes], g_v, m_ref[:, lanes],
                                                                            v_ref[:, lanes])

    operands = [t for piece in pieces for t in piece[:3]]
    out = pl.pallas_call(
        body, name=name,
        out_shape=[_sds(piece[0].shape) for piece in pieces for _ in range(4)],
    )(g, *operands)
    return [tuple(out[4 * i:4 * i + 4]) for i in range(n)]


TINY_ROWS, TINY_COLS = 16, 768
SMALL_COLS = 128
SMALL_ROWS = 624


def _pack_tiny(conv_w, b_gates, fcw):
    ns = conv_w.shape[0]
    pad = lambda t: jnp.pad(t, ((0, 0), (0, 0), (0, TINY_COLS - t.shape[2])))
    z = lambda rows: jnp.zeros((ns, rows, TINY_COLS), F32)
    return jnp.concatenate([pad(conv_w), pad(b_gates), z(2), fcw, z(5)], axis=1)


def _unpack_tiny(t):
    return t[:, 0:4, 0:256], t[:, 4:6, 0:256], t[:, 8:11, :]


def _cols_to_shards(t, n):
    return t.reshape(t.shape[0], N_CHIPS, n).transpose(1, 0, 2)


def _shards_to_cols(t):
    return t.transpose(1, 0, 2).reshape(t.shape[1], -1)


_VECTORS = ("g_mix_pre", "g_mix_post", "conv_b", "lru_lambda", "g_ffn_pre", "g_ffn_post", "g_ple_gate", "g_ple_post",
            "pool_scale", "ffn_conv_b")
_VECTOR_LEN = {"pool_scale": POOL_WIDTH, "ffn_conv_b": D_FF}
POOL_W_ROWS = POOL_GROUPS * POOL_GROUP_DIM


def _vector_rows():
    rows, row = {}, POOL_W_ROWS
    for k in _VECTORS:
        rows[k] = row
        row += max(8, _VECTOR_LEN.get(k, D_MODEL) // SMALL_COLS)
    return rows, row


def _pack_small(grads, loss):
    tiles = lambda t: jnp.pad(t, ((0, -t.shape[0] % 8), (0, 0)))
    parts = [grads["pool_w"].reshape(POOL_W_ROWS, SMALL_COLS)] + [tiles(grads[k].reshape(-1, SMALL_COLS)) for k in _VECTORS]
    parts.append(tiles(loss))
    used = sum(t.shape[0] for t in parts)
    return jnp.concatenate(parts + [jnp.zeros((SMALL_ROWS - used, SMALL_COLS), F32)], axis=0)


def _gates_block_diag(w):
    w4 = w.reshape(2, GATE_BLOCKS, 4, RNN_HEAD_DIM, RNN_HEAD_DIM)
    eye = jnp.eye(4, dtype=w.dtype)
    return jnp.einsum("gqhij,hk->gqhikj", w4, eye).reshape(2, GATE_BLOCKS, GATE_BLOCK, GATE_BLOCK)


def _gates_from_block_diag(dw):
    d6 = dw.reshape(2, GATE_BLOCKS, 4, RNN_HEAD_DIM, 4, RNN_HEAD_DIM)
    blocks = [d6[:, :, hh, :, hh, :] for hh in range(4)]
    return jnp.stack(blocks, axis=2).reshape(2, RNN_HEADS, RNN_HEAD_DIM, RNN_HEAD_DIM)


ROW_TILE = 256
DW_TILES = 4

_SHARDED = ("w_in", "w_pool_out", "w_rg_out", "w_o", "w_up", "w_down", "w_ple_gate", "w_ple_proj")
_WEIGHTS = ("g_mix_pre", "g_mix_post", "w_in", "pool_w", "pool_scale", "w_pool_out", "conv_w", "conv_b", "w_rg_gates",
            "b_rg_gates", "lru_lambda", "w_rg_out", "w_o", "g_ffn_pre", "g_ffn_post", "w_up", "ffn_conv_w", "ffn_conv_b",
            "w_down", "g_ple_gate", "w_ple_gate", "w_ple_proj", "g_ple_post")


def _wire_dtype(g):
    return BF16 if g.shape[1] >= 64 and g.shape[2] > SMALL_COLS else F32


def _partials(grads, got, place):
    parts = [_chip_partial(g, r, place, _wire_dtype(g)) for g, r in zip(grads, got)]
    return [send for send, _ in parts], [acc for _, acc in parts]


def _whole(both):
    return [b.reshape(2 * b.shape[1], b.shape[2]) for b in both]


def _step(x, p, tgt, rep, place, ts):
    vec = lambda k: rep[k].reshape(1, -1)
    pool_w = rep["pool_w"].astype(BF16)
    wg = _gates_block_diag(rep["w_rg_gates"]).astype(BF16)
    sq = lambda t: t.reshape(D_MODEL, D_MODEL)
    by4 = lambda t: t.reshape(N_CHIPS, -1, D_MODEL)

    first, ride1, ride2, ride3 = (("w_in", "w_pool_out", "tiny"), ("w_rg_out", "w_o", "w_down"), ("w_up",),
                                  ("w_ple_gate", "w_ple_proj"))
    later = ride1 + ride2 + ride3
    tiny = _pack_tiny(rep["conv_w"][None], rep["b_rg_gates"][None], rep["ffn_conv_w"][None])[0]
    own_first, _ = _own_slots([rep["w_in"], rep["w_pool_out"], tiny], [BF16, BF16, F32], "own_slots_first")
    own_later, (got,) = _own_slots([rep[k] for k in later], [BF16] * len(later), "own_slots_gather_first",
                                   [_gather_task(own_first)])
    own = dict(zip(later, own_later))
    full = dict(zip(first, got))
    conv_w, b_gates, fcw = [_shards_to_cols(t) for t in _unpack_tiny(full["tiny"])]

    (urx, urg, gp, gr, d, ypool), (got,) = _fwd_in_pool(
        x, vec("g_mix_pre"), full["w_in"], pool_w, vec("pool_scale"), full["w_pool_out"], ts,
        [_gather_task([own[k] for k in ride1])])
    full.update(zip(ride1, got))
    w_rg_out, w_o, w_down = sq(full["w_rg_out"]), sq(full["w_o"]), full["w_down"].reshape(D_FF, D_MODEL)
    (xc, r, ig, h, yrnn, mo, x1, glr, ggr, sp, sr), (got,) = _fwd_rnn_merge(
        urx, urg, gp, gr, ypool, x, conv_w, vec("conv_b"), wg, b_gates, vec("lru_lambda"), w_rg_out, w_o,
        vec("g_mix_post"), ts, [_gather_task([own[k] for k in ride2])])
    full.update(zip(ride2, got))
    (up, gl, gg, h2, dn, x2), (got,) = _fwd_ffn(x1, vec("g_ffn_pre"), full["w_up"], fcw, vec("ffn_conv_b"), w_down,
                                               vec("g_ffn_post"), ts, [_gather_task([own[k] for k in ride3])])
    full.update(zip(ride3, got))
    dx2, loss, d_w_gate, d_w_proj, d_g_ple_gate, d_g_ple_post = _ple_loss(
        x2, p, tgt, vec("g_ple_gate"), sq(full["w_ple_gate"]), full["w_ple_proj"], vec("g_ple_post"), ts)
    dup, d_w_down, d_fcw, d_fcb, d_g_ffn_post = _bwd_ffn_down(dx2, dn, up, gl, gg, fcw, w_down, vec("g_ffn_post"), ts)

    names1, grads1 = ("w_ple_gate", "w_ple_proj", "w_down"), [by4(d_w_gate), d_w_proj, by4(d_w_down)]
    (dx1, d_g_ffn_pre), (got1,) = _bwd_ffn_up(dup, x1, dx2, vec("g_ffn_pre"), full["w_up"], ts, [_halves_task(grads1)])
    (d_w_up,), (accs1,) = _dw_up(h2, dup, ts, [_exchange_task(*_partials(grads1, got1, place))])
    shares1 = [_chip_sum(acc, place) for acc in accs1]
    (dgp, dgr, dyp, dyr, d_w_o, d_g_mix_post), (got2, both1) = _bwd_merge(
        dx1, mo, sp, sr, ypool, yrnn, vec("g_mix_post"), w_o, ts, [_halves_task([d_w_up]), _share_task(shares1)])
    (durx, durg, d_w_rg_out, d_wg, d_conv_w, d_conv_b, d_b_gates, d_lam), (accs2,) = _bwd_rnn(
        dyr, urx, glr, ggr, xc, r, ig, h, conv_w, wg, vec("lru_lambda"), w_rg_out, ts,
        [_exchange_task(*_partials([d_w_up], got2, place))])
    shares2 = [_chip_sum(acc, place) for acc in accs2]
    (grad_x, d_w_in, d_w_pool_out, d_pool_w, d_pool_scale, d_g_mix_pre), _ = _bwd_pool_in(
        dyp, d, durx, durg, dgp, dgr, x, dx1, vec("g_mix_pre"), full["w_in"], pool_w, vec("pool_scale"),
        full["w_pool_out"], ts)

    replicated = {"g_mix_pre": d_g_mix_pre, "g_mix_post": d_g_mix_post, "conv_b": d_conv_b, "lru_lambda": d_lam,
                  "g_ffn_pre": d_g_ffn_pre, "g_ffn_post": d_g_ffn_post, "g_ple_gate": d_g_ple_gate,
                  "g_ple_post": d_g_ple_post, "pool_scale": d_pool_scale, "ffn_conv_b": d_fcb, "pool_w": d_pool_w}
    names3 = ("w_in", "w_pool_out", "w_rg_out", "w_o", "tiny", "small", "w_rg_gates")
    grads3 = [d_w_in, d_w_pool_out, by4(d_w_rg_out), by4(d_w_o),
              _pack_tiny(_cols_to_shards(d_conv_w, 256), _cols_to_shards(d_b_gates, 256), _cols_to_shards(d_fcw, 768)),
              _pack_small(replicated, loss)[None],
              _gates_from_block_diag(d_wg).reshape(1, 2 * RNN_HEADS * RNN_HEAD_DIM, RNN_HEAD_DIM)]
    got3 = _run(_halves_task(grads3), "grad_sibling_halves")
    accs3 = _run(_exchange_task(*_partials(grads3, got3, place)), "grad_chip_exchange")
    both = _run(_share_task(shares2 + [_chip_sum(acc, place) for acc in accs3]), "grad_sibling_share")

    return grad_x, dict(zip(names1 + ("w_up",) + names3, _whole(both1) + _whole(both)))


def kernel(x, p, g_mix_pre, g_mix_post, w_in, pool_w, pool_scale, w_pool_out, conv_w, conv_b, w_rg_gates, b_rg_gates, lru_lambda, w_rg_out, w_o, g_ffn_pre, g_ffn_post, w_up, ffn_conv_w, ffn_conv_b, w_down, g_ple_gate, w_ple_gate, w_ple_proj, g_ple_post, loss_target, m_g_mix_pre, m_g_mix_post, m_w_in, m_pool_w, m_pool_scale, m_w_pool_out, m_conv_w, m_conv_b, m_w_rg_gates, m_b_rg_gates, m_lru_lambda, m_w_rg_out, m_w_o, m_g_ffn_pre, m_g_ffn_post, m_w_up, m_ffn_conv_w, m_ffn_conv_b, m_w_down, m_g_ple_gate, m_w_ple_gate, m_w_ple_proj, m_g_ple_post, v_g_mix_pre, v_g_mix_post, v_w_in, v_pool_w, v_pool_scale, v_w_pool_out, v_conv_w, v_conv_b, v_w_rg_gates, v_b_rg_gates, v_lru_lambda, v_w_rg_out, v_w_o, v_g_ffn_pre, v_g_ffn_post, v_w_up, v_ffn_conv_w, v_ffn_conv_b, v_w_down, v_g_ple_gate, v_w_ple_gate, v_w_ple_proj, v_g_ple_post):
    args = dict(locals())
    w = {k: args[k][0] for k in _WEIGHTS}
    m = {k: args["m_" + k][0] for k in _WEIGHTS}
    v = {k: args["v_" + k][0] for k in _WEIGHTS}
    place = jnp.stack([2 * lax.axis_index("x") + lax.axis_index("y"), lax.axis_index("c")]).astype(jnp.int32)
    grad_x, reduced = _step(x[0], p[0, 0], loss_target[0], w, place, ROW_TILE)

    gates_2d = (2 * RNN_HEADS * RNN_HEAD_DIM, RNN_HEAD_DIM)
    as2d = lambda k, shape: tuple(t[k].reshape(shape) for t in (w, m, v))
    done = {k: tuple(_adamw(w[k], reduced[k], m[k], v[k])) for k in _SHARDED}
    gates_w, gates_m, gates_v = as2d("w_rg_gates", gates_2d)
    done["w_rg_gates"] = tuple(_adamw(gates_w, reduced["w_rg_gates"], gates_m, gates_v))
    tiny_names = ("conv_w", "b_rg_gates", "ffn_conv_w")
    tiny_at = ((slice(0, 4), slice(0, 256)), (slice(4, 6), slice(0, 256)), (slice(8, 11), slice(None)))
    done.update(zip(tiny_names, _adamw_pieces(
        reduced["tiny"], [(w[k], m[k], v[k]) + at for k, at in zip(tiny_names, tiny_at)], "adamw_tiny")))
    vector_rows, loss_row = _vector_rows()
    pieces = [as2d("pool_w", (POOL_W_ROWS, SMALL_COLS)) + (slice(0, POOL_W_ROWS), slice(None))]
    pieces += [as2d(k, (1, -1)) + (vector_rows[k],) for k in _VECTORS]
    done.update(zip(("pool_w",) + _VECTORS, _adamw_pieces(reduced["small"], pieces, "adamw_small")))

    result = [reduced["small"][loss_row, 0], grad_x[None]]
    for kind in range(4):
        result += [done[k][kind].reshape(args[k].shape) for k in _WEIGHTS]
    return tuple(result)
```

```python
import functools

import jax
import jax.numpy as jnp
from jax import lax
from jax.experimental import pallas as pl
from jax.experimental.pallas import tpu as pltpu

F32 = jnp.float32
BF16 = jnp.bfloat16

D_MODEL = 1024
POOL_WINDOWS = (2, 4, 8, 16)
POOL_GROUPS = 4
POOL_WIDTH = 512
POOL_GROUP_DIM = 128
RNN_HEADS = 16
RNN_HEAD_DIM = 64
GATE_BLOCK = 256
GATE_BLOCKS = D_MODEL // GATE_BLOCK
LRU_C = 8.0
D_FF = 3072
PLE_DIM = 256
RMS_EPS = 1e-6
IN_TOTAL = 4608
N_CHIPS = 4
IN_SHARD = IN_TOTAL // N_CHIPS
UP_SHARD = 2 * D_FF // N_CHIPS
POOL_HALO = 16
CONV_HALO = 8

ADAM_LR = 0.001
ADAM_B1 = 0.9
ADAM_B2 = 0.999
ADAM_EPS = 1e-08
ADAM_WD = 0.01
ADAM_STEP = 10

VMEM_LIMIT = 56 * 1024 * 1024
MESH = pl.DeviceIdType.MESH

_GELU_C = 0.7978845608028654
_GELU_A = 0.044715


def _dot(a, b):
    return jnp.dot(a.astype(BF16), b.astype(BF16), preferred_element_type=F32)


def _dot_nt(a, b):
    return lax.dot_general(a.astype(BF16), b.astype(BF16), (((1,), (1,)), ((), ())), preferred_element_type=F32)


def _dot_tn(a, b):
    return lax.dot_general(a.astype(BF16), b.astype(BF16), (((0,), (0,)), ((), ())), preferred_element_type=F32)


def _rms_fwd(x, g):
    r = lax.rsqrt(jnp.mean(x * x, axis=-1, keepdims=True) + RMS_EPS)
    xh = x * r
    return xh * g, xh, r


def _rms_bwd(xh, r, g, dy):
    dxh = dy * g
    dg = jnp.sum(dy * xh, axis=0, keepdims=True)
    dx = r * (dxh - xh * jnp.mean(dxh * xh, axis=-1, keepdims=True))
    return dx, dg


def _sigmoid(x):
    return 0.5 * jnp.tanh(0.5 * x) + 0.5


def _gelu(x):
    t = jnp.tanh(_GELU_C * (x + _GELU_A * x * x * x))
    return 0.5 * x * (1.0 + t), t


def _gelu_grad(x, t):
    return 0.5 * (1.0 + t) + 0.5 * x * (1.0 - t * t) * _GELU_C * (1.0 + 3.0 * _GELU_A * x * x)


def _softplus_neg(lam):
    nl = -lam
    return jnp.maximum(nl, 0.0) + jnp.log(1.0 + jnp.exp(-jnp.abs(nl)))


def _lru_coeffs(r, lam, first_row):
    c8 = LRU_C * _softplus_neg(lam)
    la = -(c8 * r)
    a = jnp.exp(la)
    m2 = jnp.tanh(-la) * (1.0 + a * a)
    mult = jnp.where(first_row, 1.0, jnp.sqrt(m2))
    return c8, a, m2, mult


SUBLANES = 8


def _scan_fwd(a, u, carry):
    n = a.shape[0]
    sub = lax.broadcasted_iota(jnp.int32, (n, 1), 0) % SUBLANES
    acc_a, acc_h = a, u
    for s in (1, 2, 4):
        m = sub >= s
        h_s = jnp.where(m, pltpu.roll(acc_h, s, 0), 0.0)
        a_s = jnp.where(m, pltpu.roll(acc_a, s, 0), 1.0)
        acc_h = acc_a * h_s + acc_h
        acc_a = acc_a * a_s
    out = []
    for g in range(n // SUBLANES):
        rows = slice(g * SUBLANES, (g + 1) * SUBLANES)
        out.append(acc_h[rows] + acc_a[rows] * carry)
        carry = out[-1][SUBLANES - 1:SUBLANES]
    return jnp.concatenate(out, axis=0)


def _scan_bwd(b, g, carry):
    n = b.shape[0]
    sub = lax.broadcasted_iota(jnp.int32, (n, 1), 0) % SUBLANES
    acc_b, acc_l = b, g
    for s in (1, 2, 4):
        m = sub < SUBLANES - s
        l_s = jnp.where(m, pltpu.roll(acc_l, n - s, 0), 0.0)
        b_s = jnp.where(m, pltpu.roll(acc_b, n - s, 0), 1.0)
        acc_l = acc_b * l_s + acc_l
        acc_b = acc_b * b_s
    out = [None] * (n // SUBLANES)
    for g in reversed(range(n // SUBLANES)):
        rows = slice(g * SUBLANES, (g + 1) * SUBLANES)
        out[g] = acc_l[rows] + acc_b[rows] * carry
        carry = out[g][0:1]
    return jnp.concatenate(out, axis=0)


def _shift_down(ext, k, halo):
    return pltpu.roll(ext, k, 0)[halo:] if k else ext[halo:]


def _shift_up(ext, k, ts):
    return pltpu.roll(ext, ext.shape[0] - k, 0)[:ts] if k else ext[:ts]


def _rows(ts, width, nt=None, col=0):
    if nt is None:
        return pl.BlockSpec((ts, width), lambda i: (i, col))
    return pl.BlockSpec((ts, width), lambda i: (nt - 1 - i, col))


def _resident(shape):
    zeros = (0,) * len(shape)
    return pl.BlockSpec(shape, lambda i: zeros, pipeline_mode=pl.Buffered(1))


def _acc(shape):
    zeros = (0,) * len(shape)
    return pl.BlockSpec(shape, lambda i: zeros)


def _params():
    return pltpu.CompilerParams(dimension_semantics=("arbitrary",), vmem_limit_bytes=VMEM_LIMIT)


def _sds(shape, dtype=F32):
    return jax.ShapeDtypeStruct(shape, dtype)


class _Task:
    def __init__(self, ins, out_shapes, aliases, sems, start, finish, relay=None, relay_steps=0):
        self.ins, self.out_shapes, self.aliases, self.sems = list(ins), list(out_shapes), dict(aliases), list(sems)
        self.start, self.relay, self.finish, self.relay_steps = start, relay, finish, relay_steps


def _call(body, name, grid, in_specs, out_specs, out_shape, scratch_shapes, args, tasks=()):
    n_in, n_out, n_scr = len(in_specs), len(out_specs), len(scratch_shapes)
    t_in = [len(t.ins) for t in tasks]
    t_out = [len(t.out_shapes) for t in tasks]
    t_sem = [len(t.sems) for t in tasks]
    steps = 1
    for g in grid:
        steps *= g

    def take(refs, pos, counts):
        groups = []
        for c in counts:
            groups.append(refs[pos:pos + c])
            pos += c
        return groups, pos

    def wrapped(*refs):
        (cin,), pos = take(refs, 0, [n_in])
        tin, pos = take(refs, pos, t_in)
        (cout,), pos = take(refs, pos, [n_out])
        tout, pos = take(refs, pos, t_out)
        (cscr,), pos = take(refs, pos, [n_scr])
        tsem, pos = take(refs, pos, t_sem)
        if not grid:
            for t, a, b, c in zip(tasks, tin, tout, tsem):
                t.start(a, b, c)
            if body is not None:
                body(*cin, *cout, *cscr)
            for t, a, b, c in zip(tasks, tin, tout, tsem):
                if t.relay is not None:
                    t.relay(a, b, c)
            for t, a, b, c in zip(tasks, tin, tout, tsem):
                t.finish(a, b, c)
            return
        step = pl.program_id(0)
        for axis in range(1, len(grid)):
            step = step * grid[axis] + pl.program_id(axis)
        if tasks:
            @pl.when(step == 0)
            def _():
                for t, a, b, c in zip(tasks, tin, tout, tsem):
                    t.start(a, b, c)

        body(*cin, *cout, *cscr)
        for t, a, b, c in zip(tasks, tin, tout, tsem):
            if t.relay is not None:
                pl.when(step == max(steps - 1 - t.relay_steps, 0))(functools.partial(t.relay, a, b, c))

        if tasks:
            @pl.when(step == steps - 1)
            def _():
                for t, a, b, c in zip(tasks, tin, tout, tsem):
                    t.finish(a, b, c)

    aliases, in_pos, out_pos = {}, n_in, n_out
    for t, ni, no in zip(tasks, t_in, t_out):
        aliases.update({in_pos + a: out_pos + b for a, b in t.aliases.items()})
        in_pos, out_pos = in_pos + ni, out_pos + no
    any_spec = pl.BlockSpec(memory_space=pltpu.HBM)
    kwargs = dict(grid=grid, compiler_params=pltpu.CompilerParams(
        dimension_semantics=("arbitrary",) * len(grid), vmem_limit_bytes=VMEM_LIMIT)) if grid else dict(
        compiler_params=pltpu.CompilerParams(vmem_limit_bytes=VMEM_LIMIT))
    out = pl.pallas_call(
        wrapped, name=name,
        in_specs=list(in_specs) + [any_spec] * sum(t_in),
        out_specs=list(out_specs) + [any_spec] * sum(t_out),
        out_shape=list(out_shape) + [s for t in tasks for s in t.out_shapes],
        scratch_shapes=list(scratch_shapes) + [s for t in tasks for s in t.sems],
        input_output_aliases=aliases, **kwargs,
    )(*args, *[pltpu.with_memory_space_constraint(a, pltpu.HBM) for t in tasks for a in t.ins])
    task_outs, pos = take(list(out), n_out, t_out)
    return list(out[:n_out]), task_outs


def _fwd_in_pool(x, g_pre, w_in, pool_w, pool_scale, w_pool_out, ts, tasks=()):
    s = x.shape[0]

    def body(x_ref, g_ref, win_ref, pw_ref, ps_ref, wpo_ref,
             urx_ref, urg_ref, gp_ref, gr_ref, d_ref, yp_ref, z_scr, halo_scr):
        i = pl.program_id(0)

        @pl.when(i == 0)
        def _():
            halo_scr[...] = jnp.zeros_like(halo_scr)

        h1, _, _ = _rms_fwd(x_ref[...], g_ref[...])
        h1 = h1.astype(BF16)
        for j in range(N_CHIPS):
            z_scr[:, j * IN_SHARD:(j + 1) * IN_SHARD] = jnp.dot(h1, win_ref[j], preferred_element_type=F32)
        urx_ref[...] = z_scr[:, 512:1536]
        urg_ref[...] = z_scr[:, 1536:2560]
        gp_ref[...] = z_scr[:, 2560:3584]
        gr_ref[...] = z_scr[:, 3584:4608]
        u = z_scr[:, 0:POOL_WIDTH]
        ext = jnp.concatenate([halo_scr[...], u], axis=0)
        halo_scr[...] = u[ts - POOL_HALO:, :]
        t = i * ts + lax.broadcasted_iota(jnp.int32, (ts, 1), 0)
        y4 = []
        for g, w in enumerate(POOL_WINDOWS):
            lanes = slice(g * POOL_GROUP_DIM, (g + 1) * POOL_GROUP_DIM)
            acc = ext[:, lanes]
            sh = 1
            while sh < w:
                acc = acc + pltpu.roll(acc, sh, 0)
                sh *= 2
            inv = 1.0 / jnp.minimum(t + 1, w).astype(F32)
            dg = acc[POOL_HALO:, :] * inv - u[:, lanes]
            d_ref[:, lanes] = dg
            y4.append(_dot(dg, pw_ref[g]))
        ypre = jnp.concatenate(y4, axis=1) * ps_ref[...]
        ypre = ypre.astype(BF16)
        for j in range(N_CHIPS):
            yp_ref[:, j * 256:(j + 1) * 256] = jnp.dot(ypre, wpo_ref[j], preferred_element_type=F32)

    return _call(
        body, "fwd_in_pool", (s // ts,),
        [_rows(ts, D_MODEL), _resident((1, D_MODEL)), _resident(w_in.shape), _resident(pool_w.shape),
         _resident((1, POOL_WIDTH)), _resident(w_pool_out.shape)],
        [_rows(ts, D_MODEL)] * 4 + [_rows(ts, POOL_WIDTH), _rows(ts, D_MODEL)],
        [_sds((s, D_MODEL))] * 4 + [_sds((s, POOL_WIDTH)), _sds((s, D_MODEL))],
        [pltpu.VMEM((ts, IN_TOTAL), F32), pltpu.VMEM((POOL_HALO, POOL_WIDTH), F32)],
        (x, g_pre, w_in, pool_w, pool_scale, w_pool_out), tasks)


def _fwd_rnn_merge(urx, urg, gp, gr, ypool, x, conv_w, conv_b, wg, bg, lam, w_rg_out, w_o, g_post, ts, tasks=()):
    s = x.shape[0]

    def body(urx_ref, urg_ref, gp_ref, gr_ref, yp_ref, x_ref, cw_ref, cb_ref, wg_ref, bg_ref, lam_ref, wrg_ref, wo_ref,
             gpost_ref, xc_ref, r_ref, ig_ref, h_ref, yr_ref, mo_ref, x1_ref, gl_ref, gg_ref, sp_ref, sr_ref,
             halo_scr, carry_scr):
        i = pl.program_id(0)

        @pl.when(i == 0)
        def _():
            halo_scr[...] = jnp.zeros_like(halo_scr)
            carry_scr[...] = jnp.zeros_like(carry_scr)

        urx_v = urx_ref[...]
        ext = jnp.concatenate([halo_scr[...], urx_v], axis=0)
        halo_scr[...] = urx_v[ts - CONV_HALO:, :]
        cw = cw_ref[...]
        xc = (cb_ref[...] + cw[3:4] * urx_v + cw[2:3] * _shift_down(ext, 1, CONV_HALO)
              + cw[1:2] * _shift_down(ext, 2, CONV_HALO) + cw[0:1] * _shift_down(ext, 3, CONV_HALO))
        xc_ref[...] = xc
        xcb = xc.astype(BF16)
        lin = []
        for gate in range(2):
            parts = [jnp.dot(xcb[:, q * GATE_BLOCK:(q + 1) * GATE_BLOCK], wg_ref[gate, q], preferred_element_type=F32)
                     for q in range(GATE_BLOCKS)]
            lin.append(jnp.concatenate(parts, axis=1) + bg_ref[gate:gate + 1, :])
        r = _sigmoid(lin[0])
        ig = _sigmoid(lin[1])
        r_ref[...] = r
        ig_ref[...] = ig
        first_row = (i * ts + lax.broadcasted_iota(jnp.int32, (ts, 1), 0)) == 0
        _, a, _, mult = _lru_coeffs(r, lam_ref[...], first_row)
        h = _scan_fwd(a, mult * ig * xc, carry_scr[0:1, :])
        carry_scr[0:1, :] = h[ts - 1:ts, :]
        h_ref[...] = h
        urg_v = urg_ref[...]
        gl, t = _gelu(urg_v)
        gl_ref[...] = gl.astype(BF16)
        gg_ref[...] = _gelu_grad(urg_v, t).astype(BF16)
        yr = _dot(h * gl, wrg_ref[...])
        yr_ref[...] = yr
        sp = _sigmoid(gp_ref[...])
        sr = _sigmoid(gr_ref[...])
        sp_ref[...] = sp.astype(BF16)
        sr_ref[...] = sr.astype(BF16)
        merged = sp * yp_ref[...] + sr * yr
        mo = _dot(merged, wo_ref[...])
        mo_ref[...] = mo
        y, _, _ = _rms_fwd(mo, gpost_ref[...])
        x1_ref[...] = x_ref[...] + y

    row = _rows(ts, D_MODEL)
    return _call(
        body, "fwd_rnn_merge", (s // ts,),
        [row] * 6 + [_resident(conv_w.shape), _resident((1, D_MODEL)), _resident(wg.shape), _resident(bg.shape),
                     _resident((1, D_MODEL)), _resident(w_rg_out.shape), _resident(w_o.shape), _resident((1, D_MODEL))],
        [row] * 11, [_sds((s, D_MODEL))] * 7 + [_sds((s, D_MODEL), BF16)] * 4,
        [pltpu.VMEM((CONV_HALO, D_MODEL), F32), pltpu.VMEM((8, D_MODEL), F32)],
        (urx, urg, gp, gr, ypool, x, conv_w, conv_b, wg, bg, lam, w_rg_out, w_o, g_post), tasks)


def _fwd_ffn(x1, g_pre, w_up, fcw, fcb, w_down, g_post, ts, tasks=()):
    s = x1.shape[0]

    def body(x1_ref, g_ref, wup_ref, fcw_ref, fcb_ref, wd_ref, gpost_ref,
             up_ref, gl_ref, gg_ref, h2_ref, dn_ref, x2_ref, up_scr, halo_scr):
        i = pl.program_id(0)

        @pl.when(i == 0)
        def _():
            halo_scr[...] = jnp.zeros_like(halo_scr)

        x1_v = x1_ref[...]
        h2, _, _ = _rms_fwd(x1_v, g_ref[...])
        h2 = h2.astype(BF16)
        h2_ref[...] = h2
        for j in range(N_CHIPS):
            up_scr[:, j * UP_SHARD:(j + 1) * UP_SHARD] = jnp.dot(h2, wup_ref[j], preferred_element_type=F32)
        up_ref[...] = up_scr[...].astype(BF16)
        ug = up_scr[:, 0:D_FF]
        ext = jnp.concatenate([halo_scr[...], ug], axis=0)
        halo_scr[...] = ug[ts - CONV_HALO:, :]
        w = fcw_ref[...]
        gh = (fcb_ref[...] + w[2:3] * ug + w[1:2] * _shift_down(ext, 1, CONV_HALO)
              + w[0:1] * _shift_down(ext, 2, CONV_HALO))
        gl, t = _gelu(gh)
        gl_ref[...] = gl.astype(BF16)
        gg_ref[...] = _gelu_grad(gh, t).astype(BF16)
        dn = _dot(gl * up_scr[:, D_FF:], wd_ref[...])
        dn_ref[...] = dn
        y, _, _ = _rms_fwd(dn, gpost_ref[...])
        x2_ref[...] = x1_v + y

    row = _rows(ts, D_MODEL)
    return _call(
        body, "fwd_ffn", (s // ts,),
        [row, _resident((1, D_MODEL)), _resident(w_up.shape), _resident(fcw.shape), _resident((1, D_FF)),
         _resident(w_down.shape), _resident((1, D_MODEL))],
        [_rows(ts, 2 * D_FF), _rows(ts, D_FF), _rows(ts, D_FF), row, row, row],
        [_sds((s, 2 * D_FF), BF16), _sds((s, D_FF), BF16), _sds((s, D_FF), BF16), _sds((s, D_MODEL), BF16),
         _sds((s, D_MODEL)), _sds((s, D_MODEL))],
        [pltpu.VMEM((ts, 2 * D_FF), F32), pltpu.VMEM((CONV_HALO, D_FF), F32)],
        (x1, g_pre, w_up, fcw, fcb, w_down, g_post), tasks)


def _ple_loss(x2, p, tgt, g_gate, w_gate, w_proj, g_post, ts):
    s = x2.shape[0]

    def body(x2_ref, p_ref, t_ref, gg_ref, wg_ref, wp_ref, gp_ref, dx2_ref, loss_ref, dwg_ref, dwp_ref, dgg_ref, dgp_ref):
        @pl.when(pl.program_id(0) == 0)
        def _():
            loss_ref[...] = jnp.zeros_like(loss_ref)
            dwg_ref[...] = jnp.zeros_like(dwg_ref)
            dwp_ref[...] = jnp.zeros_like(dwp_ref)
            dgg_ref[...] = jnp.zeros_like(dgg_ref)
            dgp_ref[...] = jnp.zeros_like(dgp_ref)

        x2_v = x2_ref[...]
        n3, xh3, r3 = _rms_fwd(x2_v, gg_ref[...])
        pg = _sigmoid(_dot(n3, wg_ref[...]))
        pb = p_ref[...].astype(BF16)
        q = jnp.concatenate([jnp.dot(pb, wp_ref[j], preferred_element_type=F32) for j in range(N_CHIPS)], axis=1)
        ple, qh, rq = _rms_fwd(q, gp_ref[...])
        e = x2_v + pg * ple - t_ref[...]
        loss_ref[...] += 0.5 * jnp.sum(jnp.mean(e * e, axis=-1, keepdims=True), axis=0, keepdims=True)
        dy = e * (1.0 / D_MODEL)
        dpgl = dy * ple * pg * (1.0 - pg)
        dwg_ref[...] += _dot_tn(n3, dpgl)
        dx3, dgg = _rms_bwd(xh3, r3, gg_ref[...], _dot_nt(dpgl, wg_ref[...]))
        dgg_ref[...] += dgg
        dq, dgp = _rms_bwd(qh, rq, gp_ref[...], dy * pg)
        dgp_ref[...] += dgp
        for j in range(N_CHIPS):
            dwp_ref[j] += _dot_tn(pb, dq[:, j * 256:(j + 1) * 256])
        dx2_ref[...] = dy + dx3

    row = _rows(ts, D_MODEL)
    vec = _acc((1, D_MODEL))
    return pl.pallas_call(
        body, name="ple_loss", grid=(s // ts,),
        in_specs=[row, _rows(ts, PLE_DIM), row, _resident((1, D_MODEL)), _resident(w_gate.shape), _resident(w_proj.shape),
                  _resident((1, D_MODEL))],
        out_specs=[row, _acc((1, 128)), _acc(w_gate.shape), _acc(w_proj.shape), vec, vec],
        out_shape=[_sds((s, D_MODEL)), _sds((1, 128)), _sds(w_gate.shape), _sds(w_proj.shape), _sds((1, D_MODEL)),
                   _sds((1, D_MODEL))],
        compiler_params=_params(),
    )(x2, p, tgt, g_gate, w_gate, w_proj, g_post)


def _bwd_ffn_down(dx2, dn, up, gl, gg, fcw, w_down, g_post, ts):
    s = dx2.shape[0]
    nt = s // ts

    def body(dx2_ref, dn_ref, up_ref, gl_ref, gg_ref, fcw_ref, wd_ref, gpost_ref,
             dup_ref, dwd_ref, dfcw_ref, dfcb_ref, dgp_ref, carry_scr):
        i = pl.program_id(0)

        @pl.when(i == 0)
        def _():
            carry_scr[...] = jnp.zeros_like(carry_scr)
            dwd_ref[...] = jnp.zeros_like(dwd_ref)
            dfcw_ref[...] = jnp.zeros_like(dfcw_ref)
            dfcb_ref[...] = jnp.zeros_like(dfcb_ref)
            dgp_ref[...] = jnp.zeros_like(dgp_ref)

        _, xh, r = _rms_fwd(dn_ref[...], gpost_ref[...])
        ddn, dgp = _rms_bwd(xh, r, gpost_ref[...], dx2_ref[...])
        dgp_ref[...] += dgp
        dhid = _dot_nt(ddn, wd_ref[...])
        ug = up_ref[:, 0:D_FF].astype(F32)
        uv = up_ref[:, D_FF:].astype(F32)
        gl = gl_ref[...].astype(F32)
        w = fcw_ref[...]
        dwd_ref[...] += _dot_tn(gl * uv, ddn)
        dgh = dhid * uv * gg_ref[...].astype(F32)
        dup_ref[:, D_FF:] = (dhid * gl).astype(BF16)
        extd = jnp.concatenate([dgh, carry_scr[...]], axis=0)
        carry_scr[...] = dgh[0:CONV_HALO, :]
        d1 = _shift_up(extd, 1, ts)
        d2 = _shift_up(extd, 2, ts)
        dup_ref[:, 0:D_FF] = (w[2:3] * dgh + w[1:2] * d1 + w[0:1] * d2).astype(BF16)
        dfcw_ref[2:3, :] += jnp.sum(ug * dgh, axis=0, keepdims=True)
        dfcw_ref[1:2, :] += jnp.sum(ug * d1, axis=0, keepdims=True)
        dfcw_ref[0:1, :] += jnp.sum(ug * d2, axis=0, keepdims=True)
        dfcb_ref[...] += jnp.sum(dgh, axis=0, keepdims=True)

    row = _rows(ts, D_MODEL, nt)
    wide = _rows(ts, D_FF, nt)
    return pl.pallas_call(
        body, name="bwd_ffn_down", grid=(nt,),
        in_specs=[row, row, _rows(ts, 2 * D_FF, nt), wide, wide, _resident(fcw.shape), _resident(w_down.shape),
                  _resident((1, D_MODEL))],
        out_specs=[_rows(ts, 2 * D_FF, nt), _acc(w_down.shape), _acc(fcw.shape), _acc((1, D_FF)), _acc((1, D_MODEL))],
        out_shape=[_sds((s, 2 * D_FF), BF16), _sds(w_down.shape), _sds(fcw.shape), _sds((1, D_FF)), _sds((1, D_MODEL))],
        scratch_shapes=[pltpu.VMEM((CONV_HALO, D_FF), F32)],
        compiler_params=_params(),
    )(dx2, dn, up, gl, gg, fcw, w_down, g_post)


def _bwd_ffn_up(dup, x1, dx2, g_pre, w_up, ts, tasks=()):
    s = x1.shape[0]

    def body(dup_ref, x1_ref, dx2_ref, g_ref, wup_ref, dx1_ref, dg_ref):
        @pl.when(pl.program_id(0) == 0)
        def _():
            dg_ref[...] = jnp.zeros_like(dg_ref)

        _, xh, r = _rms_fwd(x1_ref[...], g_ref[...])
        dh2 = _dot_nt(dup_ref[:, 0:UP_SHARD], wup_ref[0])
        for j in range(1, N_CHIPS):
            dh2 = dh2 + _dot_nt(dup_ref[:, j * UP_SHARD:(j + 1) * UP_SHARD], wup_ref[j])
        dx, dg = _rms_bwd(xh, r, g_ref[...], dh2)
        dg_ref[...] += dg
        dx1_ref[...] = dx2_ref[...] + dx

    row = _rows(ts, D_MODEL)
    return _call(
        body, "bwd_ffn_up", (s // ts,),
        [_rows(ts, 2 * D_FF), row, row, _resident((1, D_MODEL)), _resident(w_up.shape)],
        [row, _acc((1, D_MODEL))], [_sds((s, D_MODEL)), _sds((1, D_MODEL))], [],
        (dup, x1, dx2, g_pre, w_up), tasks)


def _dw_up(h2, dup, ts, tasks=()):
    s = h2.shape[0]
    ts = min(DW_TILES * ts, s)

    def body(h2_ref, dup_ref, out_ref):
        @pl.when(pl.program_id(1) == 0)
        def _():
            out_ref[...] = jnp.zeros_like(out_ref)

        out_ref[0] += _dot_tn(h2_ref[...], dup_ref[...])

    return _call(
        body, "dw_up", (N_CHIPS, s // ts),
        [pl.BlockSpec((ts, D_MODEL), lambda j, i: (i, 0)), pl.BlockSpec((ts, UP_SHARD), lambda j, i: (i, j))],
        [pl.BlockSpec((1, D_MODEL, UP_SHARD), lambda j, i: (j, 0, 0))], [_sds((N_CHIPS, D_MODEL, UP_SHARD))], [],
        (h2, dup), tasks)


def _bwd_merge(dx1, mo, sp, sr, ypool, yrnn, g_post, w_o, ts, tasks=()):
    s = dx1.shape[0]

    def body(dx1_ref, mo_ref, sp_ref, sr_ref, yp_ref, yr_ref, g_ref, wo_ref,
             dgp_ref, dgr_ref, dyp_ref, dyr_ref, dwo_ref, dg_ref):
        @pl.when(pl.program_id(0) == 0)
        def _():
            dwo_ref[...] = jnp.zeros_like(dwo_ref)
            dg_ref[...] = jnp.zeros_like(dg_ref)

        _, xh, r = _rms_fwd(mo_ref[...], g_ref[...])
        dmo, dg = _rms_bwd(xh, r, g_ref[...], dx1_ref[...])
        dg_ref[...] += dg
        dmerged = _dot_nt(dmo, wo_ref[...])
        sp = sp_ref[...].astype(F32)
        sr = sr_ref[...].astype(F32)
        yp = yp_ref[...]
        yr = yr_ref[...]
        dwo_ref[...] += _dot_tn(sp * yp + sr * yr, dmo)
        dgp_ref[...] = (dmerged * yp * sp * (1.0 - sp)).astype(BF16)
        dgr_ref[...] = (dmerged * yr * sr * (1.0 - sr)).astype(BF16)
        dyp_ref[...] = (dmerged * sp).astype(BF16)
        dyr_ref[...] = (dmerged * sr).astype(BF16)

    row = _rows(ts, D_MODEL)
    return _call(
        body, "bwd_merge", (s // ts,),
        [row] * 6 + [_resident((1, D_MODEL)), _resident(w_o.shape)],
        [row] * 4 + [_acc(w_o.shape), _acc((1, D_MODEL))],
        [_sds((s, D_MODEL), BF16)] * 4 + [_sds(w_o.shape), _sds((1, D_MODEL))], [],
        (dx1, mo, sp, sr, ypool, yrnn, g_post, w_o), tasks)


def _bwd_rnn(dyr, urx, gl, gg, xc, r, ig, h, conv_w, wg, lam, w_rg_out, ts, tasks=()):
    s = urx.shape[0]
    nt = s // ts
    halo_blocks = ts // CONV_HALO

    def body(dyr_ref, urx_ref, gl_ref, gg_ref, xc_ref, r_ref, ig_ref, h_ref, hh_ref, cw_ref, wg_ref, lam_ref, wrg_ref,
             durx_ref, durg_ref, dwrg_ref, dwg_ref, dcw_ref, dcb_ref, dbg_ref, dlam_ref, mu_scr, carry_scr):
        i = pl.program_id(0)
        k = nt - 1 - i

        @pl.when(i == 0)
        def _():
            mu_scr[...] = jnp.zeros_like(mu_scr)
            carry_scr[...] = jnp.zeros_like(carry_scr)
            dwrg_ref[...] = jnp.zeros_like(dwrg_ref)
            dwg_ref[...] = jnp.zeros_like(dwg_ref)
            dcw_ref[...] = jnp.zeros_like(dcw_ref)
            dcb_ref[...] = jnp.zeros_like(dcb_ref)
            dbg_ref[...] = jnp.zeros_like(dbg_ref)
            dlam_ref[...] = jnp.zeros_like(dlam_ref)

        row = lax.broadcasted_iota(jnp.int32, (ts, 1), 0)
        first_row = (k * ts + row) == 0
        h = h_ref[...]
        dyr_v = dyr_ref[...]
        dhr = _dot_nt(dyr_v, wrg_ref[...])
        gl = gl_ref[...].astype(F32)
        dwrg_ref[...] += _dot_tn(h * gl, dyr_v)
        durg_ref[...] = (dhr * h * gg_ref[...].astype(F32)).astype(BF16)
        r_v = r_ref[...]
        ig_v = ig_ref[...]
        xc_v = xc_ref[...]
        lam_v = lam_ref[...]
        c8, a, m2, mult = _lru_coeffs(r_v, lam_v, first_row)
        b = jnp.where(row == ts - 1, 1.0, pltpu.roll(a, ts - 1, 0))
        lt = _scan_bwd(b, dhr * gl, mu_scr[0:1, :])
        mu_scr[0:1, :] = a[0:1, :] * lt[0:1, :]
        h_before = jnp.where(k > 0, hh_ref[CONV_HALO - 1:CONV_HALO, :], 0.0)
        hprev = jnp.where(row == 0, h_before, pltpu.roll(h, 1, 0))
        dmult = lt * ig_v * xc_v
        da = lt * hprev - jnp.where(first_row, 0.0, dmult * a * lax.rsqrt(m2))
        dla = da * a
        dlam_ref[...] += jnp.sum(dla * r_v, axis=0, keepdims=True)
        dlr = (dla * (-c8)) * r_v * (1.0 - r_v)
        dli = (lt * mult * xc_v) * ig_v * (1.0 - ig_v)
        dbg_ref[0:1, :] += jnp.sum(dlr, axis=0, keepdims=True)
        dbg_ref[1:2, :] += jnp.sum(dli, axis=0, keepdims=True)
        xcb = xc_v.astype(BF16)
        parts = []
        for q in range(GATE_BLOCKS):
            blk = slice(q * GATE_BLOCK, (q + 1) * GATE_BLOCK)
            dlr_q = dlr[:, blk].astype(BF16)
            dli_q = dli[:, blk].astype(BF16)
            parts.append(_dot_nt(dlr_q, wg_ref[0, q]) + _dot_nt(dli_q, wg_ref[1, q]))
            dwg_ref[0, q] += _dot_tn(xcb[:, blk], dlr_q)
            dwg_ref[1, q] += _dot_tn(xcb[:, blk], dli_q)
        dxc = lt * mult * ig_v + jnp.concatenate(parts, axis=1)
        extd = jnp.concatenate([dxc, carry_scr[...]], axis=0)
        carry_scr[...] = dxc[0:CONV_HALO, :]
        cw = cw_ref[...]
        urx_v = urx_ref[...]
        durx = cw[3:4] * dxc
        dcw_ref[3:4, :] += jnp.sum(urx_v * dxc, axis=0, keepdims=True)
        for j in (1, 2, 3):
            dj = _shift_up(extd, j, ts)
            durx = durx + cw[3 - j:4 - j] * dj
            dcw_ref[3 - j:4 - j, :] += jnp.sum(urx_v * dj, axis=0, keepdims=True)
        durx_ref[...] = durx.astype(BF16)
        dcb_ref[...] += jnp.sum(dxc, axis=0, keepdims=True)

        @pl.when(i == nt - 1)
        def _():
            dlam_ref[...] = dlam_ref[...] * (LRU_C * jax.nn.sigmoid(-lam_v))

    row_spec = _rows(ts, D_MODEL, nt)
    halo_spec = pl.BlockSpec((CONV_HALO, D_MODEL), lambda i: (jnp.maximum((nt - 1 - i) * halo_blocks - 1, 0), 0))
    vec = _acc((1, D_MODEL))
    return _call(
        body, "bwd_rnn", (nt,),
        [row_spec] * 8 + [halo_spec, _resident(conv_w.shape), _resident(wg.shape), _resident((1, D_MODEL)),
                          _resident(w_rg_out.shape)],
        [row_spec, row_spec, _acc(w_rg_out.shape), _acc(wg.shape), _acc(conv_w.shape), vec, _acc((2, D_MODEL)), vec],
        [_sds((s, D_MODEL), BF16), _sds((s, D_MODEL), BF16), _sds(w_rg_out.shape), _sds(wg.shape), _sds(conv_w.shape),
         _sds((1, D_MODEL)), _sds((2, D_MODEL)), _sds((1, D_MODEL))],
        [pltpu.VMEM((8, D_MODEL), F32), pltpu.VMEM((CONV_HALO, D_MODEL), F32)],
        (dyr, urx, gl, gg, xc, r, ig, h, h, conv_w, wg, lam, w_rg_out), tasks)


def _bwd_pool_in(dyp, d, durx, durg, dgp, dgr, x, dx1, g_pre, w_in, pool_w, pool_scale, w_pool_out, ts, tasks=()):
    s = x.shape[0]
    nt = s // ts

    def body(dyp_ref, d_ref, durx_ref, durg_ref, dgp_ref, dgr_ref, x_ref, dx1_ref, g_ref, win_ref, pw_ref, ps_ref,
             wpo_ref, gx_ref, dwin_ref, dwpo_ref, dpw_ref, dps_ref, dg_ref, dz_scr, carry_scr):
        i = pl.program_id(0)
        k = nt - 1 - i

        @pl.when(i == 0)
        def _():
            carry_scr[...] = jnp.zeros_like(carry_scr)
            dwin_ref[...] = jnp.zeros_like(dwin_ref)
            dwpo_ref[...] = jnp.zeros_like(dwpo_ref)
            dpw_ref[...] = jnp.zeros_like(dpw_ref)
            dps_ref[...] = jnp.zeros_like(dps_ref)
            dg_ref[...] = jnp.zeros_like(dg_ref)

        dyp_v = dyp_ref[...]
        d_v = d_ref[...]
        ps = ps_ref[...]
        dypre = _dot_nt(dyp_v[:, 0:256], wpo_ref[0])
        for j in range(1, N_CHIPS):
            dypre = dypre + _dot_nt(dyp_v[:, j * 256:(j + 1) * 256], wpo_ref[j])
        y4 = jnp.concatenate([_dot(d_v[:, g * 128:(g + 1) * 128], pw_ref[g]) for g in range(POOL_GROUPS)], axis=1)
        ypre = (y4 * ps).astype(BF16)
        for j in range(N_CHIPS):
            dwpo_ref[j] += _dot_tn(ypre, dyp_v[:, j * 256:(j + 1) * 256])
        dps_ref[...] += jnp.sum(dypre * y4, axis=0, keepdims=True)
        dy4 = dypre * ps
        t = k * ts + lax.broadcasted_iota(jnp.int32, (ts, 1), 0)
        for g, w in enumerate(POOL_WINDOWS):
            lanes = slice(g * POOL_GROUP_DIM, (g + 1) * POOL_GROUP_DIM)
            dd = _dot_nt(dy4[:, lanes], pw_ref[g])
            dpw_ref[g] += _dot_tn(d_v[:, lanes], dy4[:, lanes])
            e = dd * (1.0 / jnp.minimum(t + 1, w).astype(F32))
            acc = jnp.concatenate([e, carry_scr[:, lanes]], axis=0)
            carry_scr[:, lanes] = e[0:POOL_HALO, :]
            n = ts + POOL_HALO
            sh = 1
            while sh < w:
                acc = acc + pltpu.roll(acc, n - sh, 0)
                sh *= 2
            dz_scr[:, lanes] = (acc[:ts, :] - dd).astype(BF16)
        dz_scr[:, 512:1536] = durx_ref[...]
        dz_scr[:, 1536:2560] = durg_ref[...]
        dz_scr[:, 2560:3584] = dgp_ref[...]
        dz_scr[:, 3584:4608] = dgr_ref[...]
        h1, xh, r = _rms_fwd(x_ref[...], g_ref[...])
        h1 = h1.astype(BF16)
        dh1 = _dot_nt(dz_scr[:, 0:IN_SHARD], win_ref[0])
        dwin_ref[0] += _dot_tn(h1, dz_scr[:, 0:IN_SHARD])
        for j in range(1, N_CHIPS):
            cols = slice(j * IN_SHARD, (j + 1) * IN_SHARD)
            dh1 = dh1 + _dot_nt(dz_scr[:, cols], win_ref[j])
            dwin_ref[j] += _dot_tn(h1, dz_scr[:, cols])
        dx, dg = _rms_bwd(xh, r, g_ref[...], dh1)
        dg_ref[...] += dg
        gx_ref[...] = dx1_ref[...] + dx

    row = _rows(ts, D_MODEL, nt)
    return _call(
        body, "bwd_pool_in", (nt,),
        [row, _rows(ts, POOL_WIDTH, nt)] + [row] * 6 + [_resident((1, D_MODEL)), _resident(w_in.shape),
                                                       _resident(pool_w.shape), _resident((1, POOL_WIDTH)),
                                                       _resident(w_pool_out.shape)],
        [row, _acc(w_in.shape), _acc(w_pool_out.shape), _acc(pool_w.shape), _acc((1, POOL_WIDTH)), _acc((1, D_MODEL))],
        [_sds((s, D_MODEL)), _sds(w_in.shape), _sds(w_pool_out.shape), _sds(pool_w.shape), _sds((1, POOL_WIDTH)),
         _sds((1, D_MODEL))],
        [pltpu.VMEM((ts, IN_TOTAL), BF16), pltpu.VMEM((POOL_HALO, POOL_WIDTH), F32)],
        (dyp, d, durx, durg, dgp, dgr, x, dx1, g_pre, w_in, pool_w, pool_scale, w_pool_out), tasks)


def _place():
    x, y, c = lax.axis_index("x"), lax.axis_index("y"), lax.axis_index("c")
    others = [(1 - x, y), (x, 1 - y), (1 - x, 1 - y)]
    return x, y, c, 2 * x + y, others


def _remote(src, dst, send_sem, recv_sem, to):
    return pltpu.make_async_remote_copy(src_ref=src, dst_ref=dst, send_sem=send_sem, recv_sem=recv_sem,
                                        device_id=to, device_id_type=MESH)


def _own_slots(ws, dtypes, name, tasks=()):
    n = len(ws)
    hbm = pl.BlockSpec(memory_space=pltpu.HBM)

    def body(*refs):
        srcs, outs, f32_bufs, cast_bufs, sems = refs[:n], refs[n:2 * n], refs[2 * n:3 * n], refs[3 * n:4 * n], refs[4 * n]
        me = _place()[3]
        loads = [pltpu.make_async_copy(srcs[k], f32_bufs[k], sems.at[k, 0]) for k in range(n)]
        stores = [pltpu.make_async_copy(cast_bufs[k], outs[k].at[me], sems.at[k, 1]) for k in range(n)]
        for cp in loads:
            cp.start()
        for k in range(n):
            loads[k].wait()
            cast_bufs[k][...] = f32_bufs[k][...].astype(dtypes[k])
            stores[k].start()
        for cp in stores:
            cp.wait()

    return _call(
        body, name, (), [hbm] * n, [hbm] * n, [_sds((N_CHIPS,) + w.shape, dt) for w, dt in zip(ws, dtypes)],
        [pltpu.VMEM(w.shape, F32) for w in ws] + [pltpu.VMEM(w.shape, dt) for w, dt in zip(ws, dtypes)]
        + [pltpu.SemaphoreType.DMA((n, 2))],
        [pltpu.with_memory_space_constraint(w, pltpu.HBM) for w in ws], tasks)


def _run(tasks, name):
    if isinstance(tasks, _Task):
        return _call(None, name, (), [], [], [], [], (), (tasks,))[1][0]
    return _call(None, name, (), [], [], [], [], (), tuple(tasks))[1]


def _gather_task(bufs, relay_steps=0):
    n = len(bufs)

    def first_copy(out, w, j, ox, oy, sems):
        x, y, c, me, _ = _place()
        ah = out.shape[1] // 2
        mine = out.at[me, pl.ds(c * ah, ah)]
        return _remote(mine, mine, sems[0].at[w, j], sems[1].at[w, j], (ox, oy, c))

    def passed_copy(out, w, j, ox, oy, sems, half):
        x, y, c, _, _ = _place()
        ah = out.shape[1] // 2
        slot = out.at[2 * ox + oy, pl.ds(half * ah, ah)]
        return _remote(slot, slot, sems[0].at[w, 3 + j], sems[1].at[w, 3 + j], (x, y, 1 - c))

    def start(ins, outs, sems):
        others = _place()[4]
        for w, out in enumerate(outs):
            for j, (ox, oy) in enumerate(others):
                first_copy(out, w, j, ox, oy, sems).start()

    def relay(ins, outs, sems):
        x, y, c, _, others = _place()
        for w, out in enumerate(outs):
            ah = out.shape[1] // 2
            for j, (ox, oy) in enumerate(others):
                slot = out.at[2 * ox + oy, pl.ds(c * ah, ah)]
                _remote(slot, slot, sems[0].at[w, j], sems[1].at[w, j], (ox, oy, c)).wait_recv()
                passed_copy(out, w, j, ox, oy, sems, c).start()

    def finish(ins, outs, sems):
        x, y, c, _, others = _place()
        for w, out in enumerate(outs):
            for j, (ox, oy) in enumerate(others):
                passed_copy(out, w, j, ox, oy, sems, 1 - c).wait_recv()
        for w, out in enumerate(outs):
            for j, (ox, oy) in enumerate(others):
                first_copy(out, w, j, ox, oy, sems).wait_send()
                passed_copy(out, w, j, ox, oy, sems, c).wait_send()

    return _Task(bufs, [_sds(b.shape, b.dtype) for b in bufs], {i: i for i in range(n)},
                 [pltpu.SemaphoreType.DMA((n, 6)), pltpu.SemaphoreType.DMA((n, 6))], start, finish, relay, relay_steps)


def _halves_task(grads):
    n = len(grads)

    def copy(src, out, w, sems):
        x, y, c, _, _ = _place()
        ah = out.shape[1]
        return _remote(src.at[:, pl.ds((1 - c) * ah, ah)], out, sems[0].at[w], sems[1].at[w], (x, y, 1 - c))

    def start(ins, outs, sems):
        for w, (src, out) in enumerate(zip(ins, outs)):
            copy(src, out, w, sems).start()

    def finish(ins, outs, sems):
        for w, (src, out) in enumerate(zip(ins, outs)):
            copy(src, out, w, sems).wait()

    return _Task(grads, [_sds((g.shape[0], g.shape[1] // 2, g.shape[2]), g.dtype) for g in grads], {},
                 [pltpu.SemaphoreType.DMA((n,)), pltpu.SemaphoreType.DMA((n,))], start, finish)


def _exchange_task(sends, accs):
    n = len(accs)
    given = [s for s in sends if s is not None]

    def copies(ins, outs, sems):
        send_refs = iter(ins[:len(given)])
        srcs = [next(send_refs) if s is not None else None for s in sends]
        x, y, c, me, others = _place()
        for w, out in enumerate(outs):
            for j, (ox, oy) in enumerate(others):
                src = out.at[me] if srcs[w] is None else srcs[w].at[2 * ox + oy]
                yield _remote(src, out.at[me], sems[0].at[w, j], sems[1].at[w, j], (ox, oy, c))

    def start(ins, outs, sems):
        for cp in copies(ins, outs, sems):
            cp.start()

    def finish(ins, outs, sems):
        x, y, c, _, others = _place()
        for w, out in enumerate(outs):
            for j, (ox, oy) in enumerate(others):
                slot = out.at[2 * ox + oy]
                _remote(slot, slot, sems[0].at[w, j], sems[1].at[w, j], (ox, oy, c)).wait_recv()
        for cp in copies(ins, outs, sems):
            cp.wait_send()

    return _Task(given + list(accs), [_sds(a.shape, a.dtype) for a in accs], {len(given) + i: i for i in range(n)},
                 [pltpu.SemaphoreType.DMA((n, 3)), pltpu.SemaphoreType.DMA((n, 3))], start, finish)


def _swap_task(arrays):
    n = len(arrays)

    def copy(src, out, w, sems):
        x, y, c, _, _ = _place()
        return _remote(src, out, sems[0].at[w], sems[1].at[w], (x, y, 1 - c))

    def start(ins, outs, sems):
        for w, (src, out) in enumerate(zip(ins, outs)):
            copy(src, out, w, sems).start()

    def finish(ins, outs, sems):
        for w, (src, out) in enumerate(zip(ins, outs)):
            copy(src, out, w, sems).wait()

    return _Task(arrays, [_sds(a.shape, a.dtype) for a in arrays], {},
                 [pltpu.SemaphoreType.DMA((n,)), pltpu.SemaphoreType.DMA((n,))], start, finish)


def _share_task(shares):
    n = len(shares)

    def copy(out, w, sems, slot):
        x, y, c, _, _ = _place()
        return _remote(out.at[slot], out.at[slot], sems[0].at[w], sems[1].at[w], (x, y, 1 - c))

    def start(ins, outs, sems):
        c = _place()[2]
        for w, out in enumerate(outs):
            copy(out, w, sems, c).start()

    def finish(ins, outs, sems):
        c = _place()[2]
        for w, out in enumerate(outs):
            copy(out, w, sems, 1 - c).wait_recv()
        for w, out in enumerate(outs):
            copy(out, w, sems, c).wait_send()

    return _Task(shares, [_sds(s.shape, s.dtype) for s in shares], {i: i for i in range(n)},
                 [pltpu.SemaphoreType.DMA((n,)), pltpu.SemaphoreType.DMA((n,))], start, finish)


TILE_BYTES = 2 * 1024 * 1024
PARTIAL_TILE_BYTES = 512 * 1024


def _row_tile(rows, cols, limit=TILE_BYTES):
    best = 8
    for tr in range(8, rows + 1, 8):
        if rows % tr == 0 and tr * cols * 4 <= limit:
            best = tr
    assert rows % best == 0, (rows, cols)
    return best


def _chip_partial(g, got, place, wire_dtype):
    ns, ah, b = got.shape
    sharded = ns == N_CHIPS
    tr = _row_tile(ah, b, PARTIAL_TILE_BYTES)
    nb = ah // tr

    def body(place_ref, *refs):
        g_refs, got_refs, outs = refs[:ns], refs[ns:2 * ns], refs[2 * ns:]
        parts = [g_refs[k][0] + got_refs[k][0] for k in range(ns)]
        own = parts[0]
        if sharded:
            for k in range(ns):
                outs[0][k] = parts[k].astype(wire_dtype)
                if k:
                    own = jnp.where(place_ref[0] == k, parts[k], own)
        outs[-1][0] = own.astype(wire_dtype)

    blk = (1, tr, b)
    in_specs = ([pl.BlockSpec(blk, lambda i, s, k=k: (k, s[1] * nb + i, 0)) for k in range(ns)]
                + [pl.BlockSpec(blk, lambda i, s, k=k: (k, i, 0)) for k in range(ns)])
    acc_spec = pl.BlockSpec(blk, lambda i, s: (s[0], i, 0))
    acc_shape = _sds((N_CHIPS, ah, b), wire_dtype)
    out = pl.pallas_call(
        body, name="grad_chip_partial",
        grid_spec=pltpu.PrefetchScalarGridSpec(
            num_scalar_prefetch=1, grid=(nb,), in_specs=in_specs,
            out_specs=[pl.BlockSpec((ns, tr, b), lambda i, s: (0, i, 0)), acc_spec] if sharded else [acc_spec]),
        out_shape=[acc_shape, acc_shape] if sharded else [acc_shape],
        compiler_params=pltpu.CompilerParams(dimension_semantics=("arbitrary",), vmem_limit_bytes=32 * 1024 * 1024),
    )(place, *([g] * ns), *([got] * ns))
    return (out[0], out[1]) if sharded else (None, out[0])


def _chip_sum(acc, place):
    _, ah, b = acc.shape
    tr = _row_tile(ah, b)

    def body(place_ref, p_ref, out_ref):
        total = p_ref[0].astype(F32) + p_ref[1].astype(F32)
        total = total + p_ref[2].astype(F32)
        out_ref[0] = total + p_ref[3].astype(F32)

    return pl.pallas_call(
        body, name="grad_chip_sum",
        grid_spec=pltpu.PrefetchScalarGridSpec(
            num_scalar_prefetch=1, grid=(ah // tr,),
            in_specs=[pl.BlockSpec((N_CHIPS, tr, b), lambda i, s: (0, i, 0))],
            out_specs=pl.BlockSpec((1, tr, b), lambda i, s: (s[1], i, 0))),
        out_shape=_sds((2, ah, b)),
        compiler_params=pltpu.CompilerParams(dimension_semantics=("arbitrary",)),
    )(place, acc)


def _adam_math(w, g, m, v):
    nm = ADAM_B1 * m + (1.0 - ADAM_B1) * g
    nv = ADAM_B2 * v + (1.0 - ADAM_B2) * (g * g)
    m_hat = nm / (1.0 - ADAM_B1 ** ADAM_STEP)
    v_hat = nv / (1.0 - ADAM_B2 ** ADAM_STEP)
    return -ADAM_LR * (m_hat / (jnp.sqrt(v_hat) + ADAM_EPS) + ADAM_WD * w), nm, nv


def _adamw(w, g, m, v):
    a, b = w.shape
    tr = _row_tile(a, b)

    def body(w_ref, g_ref, m_ref, v_ref, g_out, d_ref, nm_ref, nv_ref):
        g_out[...] = g_ref[...]
        d_ref[...], nm_ref[...], nv_ref[...] = _adam_math(w_ref[...], g_ref[...], m_ref[...], v_ref[...])

    blk = pl.BlockSpec((tr, b), lambda i: (i, 0))
    return pl.pallas_call(
        body, name="adamw", grid=(a // tr,),
        in_specs=[blk] * 4, out_specs=[blk] * 4, out_shape=[_sds((a, b))] * 4,
        compiler_params=pltpu.CompilerParams(dimension_semantics=("arbitrary",)),
    )(w, g, m, v)


def _adamw_sum(w, m, v, acc, got, place):
    a, b = w.shape
    ah = a // 2
    tr = _row_tile(ah, b, TILE_BYTES // 2)
    nb = ah // tr

    def body(place_ref, w_ref, m_ref, v_ref, acc_ref, got_ref, g_out, d_ref, nm_ref, nv_ref):
        mine = (pl.program_id(0) // nb) == place_ref[1]
        part = lambda k: jnp.where(mine, acc_ref[k], got_ref[k]).astype(F32)
        g = part(0) + part(1)
        g = g + part(2)
        g = g + part(3)
        g_out[...] = g
        d_ref[...], nm_ref[...], nv_ref[...] = _adam_math(w_ref[...], g, m_ref[...], v_ref[...])

    blk = pl.BlockSpec((tr, b), lambda i, s: (i, 0))
    mine_spec = pl.BlockSpec((N_CHIPS, tr, b), lambda i, s: (0, jnp.where(i // nb == s[1], i % nb, 0), 0))
    got_spec = pl.BlockSpec((N_CHIPS, tr, b), lambda i, s: (0, jnp.where(i // nb == s[1], 0, i % nb), 0))
    return pl.pallas_call(
        body, name="adamw_sum",
        grid_spec=pltpu.PrefetchScalarGridSpec(
            num_scalar_prefetch=1, grid=(a // tr,), in_specs=[blk] * 3 + [mine_spec, got_spec], out_specs=[blk] * 4),
        out_shape=[_sds((a, b))] * 4,
        compiler_params=pltpu.CompilerParams(dimension_semantics=("arbitrary",), vmem_limit_bytes=VMEM_LIMIT),
    )(place, w, m, v, acc, got)


def _adamw_pieces(g, pieces, name):
    n = len(pieces)

    def body(g_ref, *refs):
        ins, outs = refs[:3 * n], refs[3 * n:]
        for i, piece in enumerate(pieces):
            w_ref, m_ref, v_ref = ins[3 * i:3 * i + 3]
            o_g, o_d, o_m, o_v = outs[4 * i:4 * i + 4]
            if len(piece) == 5:
                g_v = g_ref[piece[3], piece[4]]
                o_g[...] = g_v
                o_d[...], o_m[...], o_v[...] = _adam_math(w_ref[...], g_v, m_ref[...], v_ref[...])
            else:
                for r in range(w_ref.shape[1] // SMALL_COLS):
                    lanes = slice(r * SMALL_COLS, (r + 1) * SMALL_COLS)
                    g_v = g_ref[piece[3] + r:piece[3] + r + 1, :]
                    o_g[:, lanes] = g_v
                    o_d[:, lanes], o_m[:, lanes], o_v[:, lanes] = _adam_math(w_ref[:, lanes], g_v, m_ref[:, lanes],
                                                                            v_ref[:, lanes])

    operands = [t for piece in pieces for t in piece[:3]]
    out = pl.pallas_call(
        body, name=name,
        out_shape=[_sds(piece[0].shape) for piece in pieces for _ in range(4)],
    )(g, *operands)
    return [tuple(out[4 * i:4 * i + 4]) for i in range(n)]


TINY_ROWS, TINY_COLS = 16, 768
SMALL_COLS = 128
SMALL_ROWS = 624


def _pack_tiny(conv_w, b_gates, fcw):
    ns = conv_w.shape[0]
    pad = lambda t: jnp.pad(t, ((0, 0), (0, 0), (0, TINY_COLS - t.shape[2])))
    z = lambda rows: jnp.zeros((ns, rows, TINY_COLS), F32)
    return jnp.concatenate([pad(conv_w), pad(b_gates), z(2), fcw, z(5)], axis=1)


def _unpack_tiny(t):
    return t[:, 0:4, 0:256], t[:, 4:6, 0:256], t[:, 8:11, :]


def _cols_to_shards(t, n):
    return t.reshape(t.shape[0], N_CHIPS, n).transpose(1, 0, 2)


def _shards_to_cols(t):
    return t.transpose(1, 0, 2).reshape(t.shape[1], -1)


_VECTORS = ("g_mix_pre", "g_mix_post", "conv_b", "lru_lambda", "g_ffn_pre", "g_ffn_post", "g_ple_gate", "g_ple_post",
            "pool_scale", "ffn_conv_b")
_VECTOR_LEN = {"pool_scale": POOL_WIDTH, "ffn_conv_b": D_FF}
POOL_W_ROWS = POOL_GROUPS * POOL_GROUP_DIM


def _vector_rows():
    rows, row = {}, POOL_W_ROWS
    for k in _VECTORS:
        rows[k] = row
        row += max(8, _VECTOR_LEN.get(k, D_MODEL) // SMALL_COLS)
    return rows, row


def _pack_small(grads, loss):
    tiles = lambda t: jnp.pad(t, ((0, -t.shape[0] % 8), (0, 0)))
    parts = [grads["pool_w"].reshape(POOL_W_ROWS, SMALL_COLS)] + [tiles(grads[k].reshape(-1, SMALL_COLS)) for k in _VECTORS]
    parts.append(tiles(loss))
    used = sum(t.shape[0] for t in parts)
    return jnp.concatenate(parts + [jnp.zeros((SMALL_ROWS - used, SMALL_COLS), F32)], axis=0)


def _gates_block_diag(w):
    w4 = w.reshape(2, GATE_BLOCKS, 4, RNN_HEAD_DIM, RNN_HEAD_DIM)
    eye = jnp.eye(4, dtype=w.dtype)
    return jnp.einsum("gqhij,hk->gqhikj", w4, eye).reshape(2, GATE_BLOCKS, GATE_BLOCK, GATE_BLOCK)


def _gates_from_block_diag(dw):
    d6 = dw.reshape(2, GATE_BLOCKS, 4, RNN_HEAD_DIM, 4, RNN_HEAD_DIM)
    blocks = [d6[:, :, hh, :, hh, :] for hh in range(4)]
    return jnp.stack(blocks, axis=2).reshape(2, RNN_HEADS, RNN_HEAD_DIM, RNN_HEAD_DIM)


ROW_TILE = 256
DW_TILES = 4

_SHARDED = ("w_in", "w_pool_out", "w_rg_out", "w_o", "w_up", "w_down", "w_ple_gate", "w_ple_proj")
_WEIGHTS = ("g_mix_pre", "g_mix_post", "w_in", "pool_w", "pool_scale", "w_pool_out", "conv_w", "conv_b", "w_rg_gates",
            "b_rg_gates", "lru_lambda", "w_rg_out", "w_o", "g_ffn_pre", "g_ffn_post", "w_up", "ffn_conv_w", "ffn_conv_b",
            "w_down", "g_ple_gate", "w_ple_gate", "w_ple_proj", "g_ple_post")


def _wire_dtype(g):
    return BF16 if g.shape[1] >= 64 and g.shape[2] > SMALL_COLS else F32


def _partials(grads, got, place):
    parts = [_chip_partial(g, r, place, _wire_dtype(g)) for g, r in zip(grads, got)]
    return [send for send, _ in parts], [acc for _, acc in parts]


def _whole(both):
    return [b.reshape(2 * b.shape[1], b.shape[2]) for b in both]


def _step(x, p, tgt, rep, place, ts):
    vec = lambda k: rep[k].reshape(1, -1)
    pool_w = rep["pool_w"].astype(BF16)
    wg = _gates_block_diag(rep["w_rg_gates"]).astype(BF16)
    sq = lambda t: t.reshape(D_MODEL, D_MODEL)
    by4 = lambda t: t.reshape(N_CHIPS, -1, D_MODEL)

    first, ride1, ride2, ride3 = (("w_in", "w_pool_out", "tiny"), ("w_rg_out", "w_o", "w_down"), ("w_up",),
                                  ("w_ple_gate", "w_ple_proj"))
    later = ride1 + ride2 + ride3
    tiny = _pack_tiny(rep["conv_w"][None], rep["b_rg_gates"][None], rep["ffn_conv_w"][None])[0]
    own_first, _ = _own_slots([rep["w_in"], rep["w_pool_out"], tiny], [BF16, BF16, F32], "own_slots_first")
    own_later, (got,) = _own_slots([rep[k] for k in later], [BF16] * len(later), "own_slots_gather_first",
                                   [_gather_task(own_first)])
    own = dict(zip(later, own_later))
    full = dict(zip(first, got))
    conv_w, b_gates, fcw = [_shards_to_cols(t) for t in _unpack_tiny(full["tiny"])]

    (urx, urg, gp, gr, d, ypool), (got,) = _fwd_in_pool(
        x, vec("g_mix_pre"), full["w_in"], pool_w, vec("pool_scale"), full["w_pool_out"], ts,
        [_gather_task([own[k] for k in ride1])])
    full.update(zip(ride1, got))
    w_rg_out, w_o, w_down = sq(full["w_rg_out"]), sq(full["w_o"]), full["w_down"].reshape(D_FF, D_MODEL)
    (xc, r, ig, h, yrnn, mo, x1, glr, ggr, sp, sr), (got,) = _fwd_rnn_merge(
        urx, urg, gp, gr, ypool, x, conv_w, vec("conv_b"), wg, b_gates, vec("lru_lambda"), w_rg_out, w_o,
        vec("g_mix_post"), ts, [_gather_task([own[k] for k in ride2], relay_steps=2)])
    full.update(zip(ride2, got))
    (up, gl, gg, h2, dn, x2), (got,) = _fwd_ffn(x1, vec("g_ffn_pre"), full["w_up"], fcw, vec("ffn_conv_b"), w_down,
                                               vec("g_ffn_post"), ts, [_gather_task([own[k] for k in ride3], relay_steps=4)])
    full.update(zip(ride3, got))
    dx2, loss, d_w_gate, d_w_proj, d_g_ple_gate, d_g_ple_post = _ple_loss(
        x2, p, tgt, vec("g_ple_gate"), sq(full["w_ple_gate"]), full["w_ple_proj"], vec("g_ple_post"), ts)
    dup, d_w_down, d_fcw, d_fcb, d_g_ffn_post = _bwd_ffn_down(dx2, dn, up, gl, gg, fcw, w_down, vec("g_ffn_post"), ts)

    names1, grads1 = ("w_ple_gate", "w_ple_proj", "w_down"), [by4(d_w_gate), d_w_proj, by4(d_w_down)]
    (dx1, d_g_ffn_pre), (got1,) = _bwd_ffn_up(dup, x1, dx2, vec("g_ffn_pre"), full["w_up"], ts, [_halves_task(grads1)])
    (d_w_up,), (accs1,) = _dw_up(h2, dup, ts, [_exchange_task(*_partials(grads1, got1, place))])
    (dgp, dgr, dyp, dyr, d_w_o, d_g_mix_post), (got2, theirs1) = _bwd_merge(
        dx1, mo, sp, sr, ypool, yrnn, vec("g_mix_post"), w_o, ts, [_halves_task([d_w_up]), _swap_task(accs1)])
    (durx, durg, d_w_rg_out, d_wg, d_conv_w, d_conv_b, d_b_gates, d_lam), (accs2,) = _bwd_rnn(
        dyr, urx, glr, ggr, xc, r, ig, h, conv_w, wg, vec("lru_lambda"), w_rg_out, ts,
        [_exchange_task(*_partials([d_w_up], got2, place))])
    (grad_x, d_w_in, d_w_pool_out, d_pool_w, d_pool_scale, d_g_mix_pre), _ = _bwd_pool_in(
        dyp, d, durx, durg, dgp, dgr, x, dx1, vec("g_mix_pre"), full["w_in"], pool_w, vec("pool_scale"),
        full["w_pool_out"], ts)

    replicated = {"g_mix_pre": d_g_mix_pre, "g_mix_post": d_g_mix_post, "conv_b": d_conv_b, "lru_lambda": d_lam,
                  "g_ffn_pre": d_g_ffn_pre, "g_ffn_post": d_g_ffn_post, "g_ple_gate": d_g_ple_gate,
                  "g_ple_post": d_g_ple_post, "pool_scale": d_pool_scale, "ffn_conv_b": d_fcb, "pool_w": d_pool_w}
    names3 = ("w_in", "w_pool_out", "w_rg_out", "w_o", "tiny", "small", "w_rg_gates")
    grads3 = [d_w_in, d_w_pool_out, by4(d_w_rg_out), by4(d_w_o),
              _pack_tiny(_cols_to_shards(d_conv_w, 256), _cols_to_shards(d_b_gates, 256), _cols_to_shards(d_fcw, 768)),
              _pack_small(replicated, loss)[None],
              _gates_from_block_diag(d_wg).reshape(1, 2 * RNN_HEADS * RNN_HEAD_DIM, RNN_HEAD_DIM)]
    got3 = _run(_halves_task(grads3), "grad_sibling_halves")
    accs3 = _run(_exchange_task(*_partials(grads3, got3, place)), "grad_chip_exchange")
    theirs, both = _run([_swap_task(accs2 + accs3[:4]), _share_task([_chip_sum(acc, place) for acc in accs3[4:]])],
                        "grad_sibling_share")
    partials = dict(zip(names1 + ("w_up",) + names3[:4], zip(accs1 + accs2 + accs3[:4], theirs1 + theirs)))
    return grad_x, partials, dict(zip(names3[4:], _whole(both)))


def kernel(x, p, g_mix_pre, g_mix_post, w_in, pool_w, pool_scale, w_pool_out, conv_w, conv_b, w_rg_gates, b_rg_gates, lru_lambda, w_rg_out, w_o, g_ffn_pre, g_ffn_post, w_up, ffn_conv_w, ffn_conv_b, w_down, g_ple_gate, w_ple_gate, w_ple_proj, g_ple_post, loss_target, m_g_mix_pre, m_g_mix_post, m_w_in, m_pool_w, m_pool_scale, m_w_pool_out, m_conv_w, m_conv_b, m_w_rg_gates, m_b_rg_gates, m_lru_lambda, m_w_rg_out, m_w_o, m_g_ffn_pre, m_g_ffn_post, m_w_up, m_ffn_conv_w, m_ffn_conv_b, m_w_down, m_g_ple_gate, m_w_ple_gate, m_w_ple_proj, m_g_ple_post, v_g_mix_pre, v_g_mix_post, v_w_in, v_pool_w, v_pool_scale, v_w_pool_out, v_conv_w, v_conv_b, v_w_rg_gates, v_b_rg_gates, v_lru_lambda, v_w_rg_out, v_w_o, v_g_ffn_pre, v_g_ffn_post, v_w_up, v_ffn_conv_w, v_ffn_conv_b, v_w_down, v_g_ple_gate, v_w_ple_gate, v_w_ple_proj, v_g_ple_post):
    args = dict(locals())
    w = {k: args[k][0] for k in _WEIGHTS}
    m = {k: args["m_" + k][0] for k in _WEIGHTS}
    v = {k: args["v_" + k][0] for k in _WEIGHTS}
    place = jnp.stack([2 * lax.axis_index("x") + lax.axis_index("y"), lax.axis_index("c")]).astype(jnp.int32)
    grad_x, partials, reduced = _step(x[0], p[0, 0], loss_target[0], w, place, ROW_TILE)

    gates_2d = (2 * RNN_HEADS * RNN_HEAD_DIM, RNN_HEAD_DIM)
    as2d = lambda k, shape: tuple(t[k].reshape(shape) for t in (w, m, v))
    done = {k: tuple(_adamw_sum(w[k], m[k], v[k], *partials[k], place)) for k in _SHARDED}
    gates_w, gates_m, gates_v = as2d("w_rg_gates", gates_2d)
    done["w_rg_gates"] = tuple(_adamw(gates_w, reduced["w_rg_gates"], gates_m, gates_v))
    tiny_names = ("conv_w", "b_rg_gates", "ffn_conv_w")
    tiny_at = ((slice(0, 4), slice(0, 256)), (slice(4, 6), slice(0, 256)), (slice(8, 11), slice(None)))
    done.update(zip(tiny_names, _adamw_pieces(
        reduced["tiny"], [(w[k], m[k], v[k]) + at for k, at in zip(tiny_names, tiny_at)], "adamw_tiny")))
    vector_rows, loss_row = _vector_rows()
    pieces = [as2d("pool_w", (POOL_W_ROWS, SMALL_COLS)) + (slice(0, POOL_W_ROWS), slice(None))]
    pieces += [as2d(k, (1, -1)) + (vector_rows[k],) for k in _VECTORS]
    done.update(zip(("pool_w",) + _VECTORS, _adamw_pieces(reduced["small"], pieces, "adamw_small")))

    result = [reduced["small"][loss_row, 0], grad_x[None]]
    for kind in range(4):
        result += [done[k][kind].reshape(args[k].shape) for k in _WEIGHTS]
    return tuple(result)
```

```python
import functools

import jax
import jax.numpy as jnp
from jax import lax
from jax.experimental import pallas as pl
from jax.experimental.pallas import tpu as pltpu

F32 = jnp.float32
BF16 = jnp.bfloat16

D_MODEL = 1024
POOL_WINDOWS = (2, 4, 8, 16)
POOL_GROUPS = 4
POOL_WIDTH = 512
POOL_GROUP_DIM = 128
RNN_HEADS = 16
RNN_HEAD_DIM = 64
GATE_BLOCK = 256
GATE_BLOCKS = D_MODEL // GATE_BLOCK
LRU_C = 8.0
D_FF = 3072
PLE_DIM = 256
RMS_EPS = 1e-6
IN_TOTAL = 4608
N_CHIPS = 4
IN_SHARD = IN_TOTAL // N_CHIPS
UP_SHARD = 2 * D_FF // N_CHIPS
POOL_HALO = 16
CONV_HALO = 8

ADAM_LR = 0.001
ADAM_B1 = 0.9
ADAM_B2 = 0.999
ADAM_EPS = 1e-08
ADAM_WD = 0.01
ADAM_STEP = 10

VMEM_LIMIT = 56 * 1024 * 1024
MESH = pl.DeviceIdType.MESH

_GELU_C = 0.7978845608028654
_GELU_A = 0.044715


def _dot(a, b):
    return jnp.dot(a.astype(BF16), b.astype(BF16), preferred_element_type=F32)


def _dot_nt(a, b):
    return lax.dot_general(a.astype(BF16), b.astype(BF16), (((1,), (1,)), ((), ())), preferred_element_type=F32)


def _dot_tn(a, b):
    return lax.dot_general(a.astype(BF16), b.astype(BF16), (((0,), (0,)), ((), ())), preferred_element_type=F32)


def _rms_fwd(x, g):
    r = lax.rsqrt(jnp.mean(x * x, axis=-1, keepdims=True) + RMS_EPS)
    xh = x * r
    return xh * g, xh, r


def _rms_bwd(xh, r, g, dy):
    dxh = dy * g
    dg = jnp.sum(dy * xh, axis=0, keepdims=True)
    dx = r * (dxh - xh * jnp.mean(dxh * xh, axis=-1, keepdims=True))
    return dx, dg


def _sigmoid(x):
    return 0.5 * jnp.tanh(0.5 * x) + 0.5


def _gelu(x):
    t = jnp.tanh(_GELU_C * (x + _GELU_A * x * x * x))
    return 0.5 * x * (1.0 + t), t


def _gelu_grad(x, t):
    return 0.5 * (1.0 + t) + 0.5 * x * (1.0 - t * t) * _GELU_C * (1.0 + 3.0 * _GELU_A * x * x)


def _softplus_neg(lam):
    nl = -lam
    return jnp.maximum(nl, 0.0) + jnp.log(1.0 + jnp.exp(-jnp.abs(nl)))


def _lru_coeffs(r, lam, first_row):
    c8 = LRU_C * _softplus_neg(lam)
    la = -(c8 * r)
    a = jnp.exp(la)
    m2 = jnp.tanh(-la) * (1.0 + a * a)
    mult = jnp.where(first_row, 1.0, jnp.sqrt(m2))
    return c8, a, m2, mult


SUBLANES = 8


def _scan_fwd(a, u, carry):
    n = a.shape[0]
    sub = lax.broadcasted_iota(jnp.int32, (n, 1), 0) % SUBLANES
    acc_a, acc_h = a, u
    for s in (1, 2, 4):
        m = sub >= s
        h_s = jnp.where(m, pltpu.roll(acc_h, s, 0), 0.0)
        a_s = jnp.where(m, pltpu.roll(acc_a, s, 0), 1.0)
        acc_h = acc_a * h_s + acc_h
        acc_a = acc_a * a_s
    out = []
    for g in range(n // SUBLANES):
        rows = slice(g * SUBLANES, (g + 1) * SUBLANES)
        out.append(acc_h[rows] + acc_a[rows] * carry)
        carry = out[-1][SUBLANES - 1:SUBLANES]
    return jnp.concatenate(out, axis=0)


def _scan_bwd(b, g, carry):
    n = b.shape[0]
    sub = lax.broadcasted_iota(jnp.int32, (n, 1), 0) % SUBLANES
    acc_b, acc_l = b, g
    for s in (1, 2, 4):
        m = sub < SUBLANES - s
        l_s = jnp.where(m, pltpu.roll(acc_l, n - s, 0), 0.0)
        b_s = jnp.where(m, pltpu.roll(acc_b, n - s, 0), 1.0)
        acc_l = acc_b * l_s + acc_l
        acc_b = acc_b * b_s
    out = [None] * (n // SUBLANES)
    for g in reversed(range(n // SUBLANES)):
        rows = slice(g * SUBLANES, (g + 1) * SUBLANES)
        out[g] = acc_l[rows] + acc_b[rows] * carry
        carry = out[g][0:1]
    return jnp.concatenate(out, axis=0)


def _shift_down(ext, k, halo):
    return pltpu.roll(ext, k, 0)[halo:] if k else ext[halo:]


def _shift_up(ext, k, ts):
    return pltpu.roll(ext, ext.shape[0] - k, 0)[:ts] if k else ext[:ts]


def _rows(ts, width, nt=None, col=0):
    if nt is None:
        return pl.BlockSpec((ts, width), lambda i: (i, col))
    return pl.BlockSpec((ts, width), lambda i: (nt - 1 - i, col))


def _resident(shape):
    zeros = (0,) * len(shape)
    return pl.BlockSpec(shape, lambda i: zeros, pipeline_mode=pl.Buffered(1))


def _acc(shape):
    zeros = (0,) * len(shape)
    return pl.BlockSpec(shape, lambda i: zeros)


def _params():
    return pltpu.CompilerParams(dimension_semantics=("arbitrary",), vmem_limit_bytes=VMEM_LIMIT)


def _sds(shape, dtype=F32):
    return jax.ShapeDtypeStruct(shape, dtype)


class _Task:
    def __init__(self, ins, out_shapes, aliases, sems, start, finish, relay=None, relay_steps=0):
        self.ins, self.out_shapes, self.aliases, self.sems = list(ins), list(out_shapes), dict(aliases), list(sems)
        self.start, self.relay, self.finish, self.relay_steps = start, relay, finish, relay_steps


def _call(body, name, grid, in_specs, out_specs, out_shape, scratch_shapes, args, tasks=()):
    n_in, n_out, n_scr = len(in_specs), len(out_specs), len(scratch_shapes)
    t_in = [len(t.ins) for t in tasks]
    t_out = [len(t.out_shapes) for t in tasks]
    t_sem = [len(t.sems) for t in tasks]
    steps = 1
    for g in grid:
        steps *= g

    def take(refs, pos, counts):
        groups = []
        for c in counts:
            groups.append(refs[pos:pos + c])
            pos += c
        return groups, pos

    def wrapped(*refs):
        (cin,), pos = take(refs, 0, [n_in])
        tin, pos = take(refs, pos, t_in)
        (cout,), pos = take(refs, pos, [n_out])
        tout, pos = take(refs, pos, t_out)
        (cscr,), pos = take(refs, pos, [n_scr])
        tsem, pos = take(refs, pos, t_sem)
        if not grid:
            for t, a, b, c in zip(tasks, tin, tout, tsem):
                t.start(a, b, c)
            if body is not None:
                body(*cin, *cout, *cscr)
            for t, a, b, c in zip(tasks, tin, tout, tsem):
                if t.relay is not None:
                    t.relay(a, b, c)
            for t, a, b, c in zip(tasks, tin, tout, tsem):
                t.finish(a, b, c)
            return
        step = pl.program_id(0)
        for axis in range(1, len(grid)):
            step = step * grid[axis] + pl.program_id(axis)
        if tasks:
            @pl.when(step == 0)
            def _():
                for t, a, b, c in zip(tasks, tin, tout, tsem):
                    t.start(a, b, c)

        body(*cin, *cout, *cscr)
        for t, a, b, c in zip(tasks, tin, tout, tsem):
            if t.relay is not None:
                pl.when(step == max(steps - 1 - t.relay_steps, 0))(functools.partial(t.relay, a, b, c))

        if tasks:
            @pl.when(step == steps - 1)
            def _():
                for t, a, b, c in zip(tasks, tin, tout, tsem):
                    t.finish(a, b, c)

    aliases, in_pos, out_pos = {}, n_in, n_out
    for t, ni, no in zip(tasks, t_in, t_out):
        aliases.update({in_pos + a: out_pos + b for a, b in t.aliases.items()})
        in_pos, out_pos = in_pos + ni, out_pos + no
    any_spec = pl.BlockSpec(memory_space=pltpu.HBM)
    kwargs = dict(grid=grid, compiler_params=pltpu.CompilerParams(
        dimension_semantics=("arbitrary",) * len(grid), vmem_limit_bytes=VMEM_LIMIT)) if grid else dict(
        compiler_params=pltpu.CompilerParams(vmem_limit_bytes=VMEM_LIMIT))
    out = pl.pallas_call(
        wrapped, name=name,
        in_specs=list(in_specs) + [any_spec] * sum(t_in),
        out_specs=list(out_specs) + [any_spec] * sum(t_out),
        out_shape=list(out_shape) + [s for t in tasks for s in t.out_shapes],
        scratch_shapes=list(scratch_shapes) + [s for t in tasks for s in t.sems],
        input_output_aliases=aliases, **kwargs,
    )(*args, *[pltpu.with_memory_space_constraint(a, pltpu.HBM) for t in tasks for a in t.ins])
    task_outs, pos = take(list(out), n_out, t_out)
    return list(out[:n_out]), task_outs


def _fwd_in_pool(x, g_pre, w_in, pool_w, pool_scale, w_pool_out, ts, tasks=()):
    s = x.shape[0]

    def body(x_ref, g_ref, win_ref, pw_ref, ps_ref, wpo_ref,
             urx_ref, urg_ref, gp_ref, gr_ref, d_ref, yp_ref, h1_ref, z_scr, halo_scr):
        i = pl.program_id(0)

        @pl.when(i == 0)
        def _():
            halo_scr[...] = jnp.zeros_like(halo_scr)

        h1, _, _ = _rms_fwd(x_ref[...], g_ref[...])
        h1 = h1.astype(BF16)
        h1_ref[...] = h1
        for j in range(N_CHIPS):
            z_scr[:, j * IN_SHARD:(j + 1) * IN_SHARD] = jnp.dot(h1, win_ref[j], preferred_element_type=F32)
        urx_ref[...] = z_scr[:, 512:1536]
        urg_ref[...] = z_scr[:, 1536:2560]
        gp_ref[...] = z_scr[:, 2560:3584]
        gr_ref[...] = z_scr[:, 3584:4608]
        u = z_scr[:, 0:POOL_WIDTH]
        ext = jnp.concatenate([halo_scr[...], u], axis=0)
        halo_scr[...] = u[ts - POOL_HALO:, :]
        t = i * ts + lax.broadcasted_iota(jnp.int32, (ts, 1), 0)
        y4 = []
        for g, w in enumerate(POOL_WINDOWS):
            lanes = slice(g * POOL_GROUP_DIM, (g + 1) * POOL_GROUP_DIM)
            acc = ext[:, lanes]
            sh = 1
            while sh < w:
                acc = acc + pltpu.roll(acc, sh, 0)
                sh *= 2
            inv = 1.0 / jnp.minimum(t + 1, w).astype(F32)
            dg = acc[POOL_HALO:, :] * inv - u[:, lanes]
            d_ref[:, lanes] = dg
            y4.append(_dot(dg, pw_ref[g]))
        ypre = jnp.concatenate(y4, axis=1) * ps_ref[...]
        ypre = ypre.astype(BF16)
        for j in range(N_CHIPS):
            yp_ref[:, j * 256:(j + 1) * 256] = jnp.dot(ypre, wpo_ref[j], preferred_element_type=F32)

    return _call(
        body, "fwd_in_pool", (s // ts,),
        [_rows(ts, D_MODEL), _resident((1, D_MODEL)), _resident(w_in.shape), _resident(pool_w.shape),
         _resident((1, POOL_WIDTH)), _resident(w_pool_out.shape)],
        [_rows(ts, D_MODEL)] * 4 + [_rows(ts, POOL_WIDTH), _rows(ts, D_MODEL), _rows(ts, D_MODEL)],
        [_sds((s, D_MODEL))] * 4 + [_sds((s, POOL_WIDTH)), _sds((s, D_MODEL)), _sds((s, D_MODEL), BF16)],
        [pltpu.VMEM((ts, IN_TOTAL), F32), pltpu.VMEM((POOL_HALO, POOL_WIDTH), F32)],
        (x, g_pre, w_in, pool_w, pool_scale, w_pool_out), tasks)


def _fwd_rnn_merge(urx, urg, gp, gr, ypool, x, conv_w, conv_b, wg, bg, lam, w_rg_out, w_o, g_post, ts, tasks=()):
    s = x.shape[0]

    def body(urx_ref, urg_ref, gp_ref, gr_ref, yp_ref, x_ref, cw_ref, cb_ref, wg_ref, bg_ref, lam_ref, wrg_ref, wo_ref,
             gpost_ref, xc_ref, r_ref, ig_ref, h_ref, yr_ref, mo_ref, x1_ref, gl_ref, gg_ref, sp_ref, sr_ref,
             halo_scr, carry_scr):
        i = pl.program_id(0)

        @pl.when(i == 0)
        def _():
            halo_scr[...] = jnp.zeros_like(halo_scr)
            carry_scr[...] = jnp.zeros_like(carry_scr)

        urx_v = urx_ref[...]
        ext = jnp.concatenate([halo_scr[...], urx_v], axis=0)
        halo_scr[...] = urx_v[ts - CONV_HALO:, :]
        cw = cw_ref[...]
        xc = (cb_ref[...] + cw[3:4] * urx_v + cw[2:3] * _shift_down(ext, 1, CONV_HALO)
              + cw[1:2] * _shift_down(ext, 2, CONV_HALO) + cw[0:1] * _shift_down(ext, 3, CONV_HALO))
        xc_ref[...] = xc
        xcb = xc.astype(BF16)
        lin = []
        for gate in range(2):
            parts = [jnp.dot(xcb[:, q * GATE_BLOCK:(q + 1) * GATE_BLOCK], wg_ref[gate, q], preferred_element_type=F32)
                     for q in range(GATE_BLOCKS)]
            lin.append(jnp.concatenate(parts, axis=1) + bg_ref[gate:gate + 1, :])
        r = _sigmoid(lin[0])
        ig = _sigmoid(lin[1])
        r_ref[...] = r
        ig_ref[...] = ig
        first_row = (i * ts + lax.broadcasted_iota(jnp.int32, (ts, 1), 0)) == 0
        _, a, _, mult = _lru_coeffs(r, lam_ref[...], first_row)
        h = _scan_fwd(a, mult * ig * xc, carry_scr[0:1, :])
        carry_scr[0:1, :] = h[ts - 1:ts, :]
        h_ref[...] = h
        urg_v = urg_ref[...]
        gl, t = _gelu(urg_v)
        gl_ref[...] = gl.astype(BF16)
        gg_ref[...] = _gelu_grad(urg_v, t).astype(BF16)
        yr = _dot(h * gl, wrg_ref[...])
        yr_ref[...] = yr
        sp = _sigmoid(gp_ref[...])
        sr = _sigmoid(gr_ref[...])
        sp_ref[...] = sp.astype(BF16)
        sr_ref[...] = sr.astype(BF16)
        merged = sp * yp_ref[...] + sr * yr
        mo = _dot(merged, wo_ref[...])
        mo_ref[...] = mo
        y, _, _ = _rms_fwd(mo, gpost_ref[...])
        x1_ref[...] = x_ref[...] + y

    row = _rows(ts, D_MODEL)
    return _call(
        body, "fwd_rnn_merge", (s // ts,),
        [row] * 6 + [_resident(conv_w.shape), _resident((1, D_MODEL)), _resident(wg.shape), _resident(bg.shape),
                     _resident((1, D_MODEL)), _resident(w_rg_out.shape), _resident(w_o.shape), _resident((1, D_MODEL))],
        [row] * 11, [_sds((s, D_MODEL))] * 7 + [_sds((s, D_MODEL), BF16)] * 4,
        [pltpu.VMEM((CONV_HALO, D_MODEL), F32), pltpu.VMEM((8, D_MODEL), F32)],
        (urx, urg, gp, gr, ypool, x, conv_w, conv_b, wg, bg, lam, w_rg_out, w_o, g_post), tasks)


def _fwd_ffn(x1, g_pre, w_up, fcw, fcb, w_down, g_post, ts, tasks=()):
    s = x1.shape[0]

    def body(x1_ref, g_ref, wup_ref, fcw_ref, fcb_ref, wd_ref, gpost_ref,
             up_ref, gl_ref, gg_ref, h2_ref, dn_ref, x2_ref, up_scr, halo_scr):
        i = pl.program_id(0)

        @pl.when(i == 0)
        def _():
            halo_scr[...] = jnp.zeros_like(halo_scr)

        x1_v = x1_ref[...]
        h2, _, _ = _rms_fwd(x1_v, g_ref[...])
        h2 = h2.astype(BF16)
        h2_ref[...] = h2
        for j in range(N_CHIPS):
            up_scr[:, j * UP_SHARD:(j + 1) * UP_SHARD] = jnp.dot(h2, wup_ref[j], preferred_element_type=F32)
        up_ref[...] = up_scr[...].astype(BF16)
        ug = up_scr[:, 0:D_FF]
        ext = jnp.concatenate([halo_scr[...], ug], axis=0)
        halo_scr[...] = ug[ts - CONV_HALO:, :]
        w = fcw_ref[...]
        gh = (fcb_ref[...] + w[2:3] * ug + w[1:2] * _shift_down(ext, 1, CONV_HALO)
              + w[0:1] * _shift_down(ext, 2, CONV_HALO))
        gl, t = _gelu(gh)
        gl_ref[...] = gl.astype(BF16)
        gg_ref[...] = _gelu_grad(gh, t).astype(BF16)
        dn = _dot(gl * up_scr[:, D_FF:], wd_ref[...])
        dn_ref[...] = dn
        y, _, _ = _rms_fwd(dn, gpost_ref[...])
        x2_ref[...] = x1_v + y

    row = _rows(ts, D_MODEL)
    return _call(
        body, "fwd_ffn", (s // ts,),
        [row, _resident((1, D_MODEL)), _resident(w_up.shape), _resident(fcw.shape), _resident((1, D_FF)),
         _resident(w_down.shape), _resident((1, D_MODEL))],
        [_rows(ts, 2 * D_FF), _rows(ts, D_FF), _rows(ts, D_FF), row, row, row],
        [_sds((s, 2 * D_FF), BF16), _sds((s, D_FF), BF16), _sds((s, D_FF), BF16), _sds((s, D_MODEL), BF16),
         _sds((s, D_MODEL)), _sds((s, D_MODEL))],
        [pltpu.VMEM((ts, 2 * D_FF), F32), pltpu.VMEM((CONV_HALO, D_FF), F32)],
        (x1, g_pre, w_up, fcw, fcb, w_down, g_post), tasks)


def _ple_loss(x2, p, tgt, g_gate, w_gate, w_proj, g_post, ts):
    s = x2.shape[0]

    def body(x2_ref, p_ref, t_ref, gg_ref, wg_ref, wp_ref, gp_ref, dx2_ref, loss_ref, dwg_ref, dwp_ref, dgg_ref, dgp_ref):
        @pl.when(pl.program_id(0) == 0)
        def _():
            loss_ref[...] = jnp.zeros_like(loss_ref)
            dwg_ref[...] = jnp.zeros_like(dwg_ref)
            dwp_ref[...] = jnp.zeros_like(dwp_ref)
            dgg_ref[...] = jnp.zeros_like(dgg_ref)
            dgp_ref[...] = jnp.zeros_like(dgp_ref)

        x2_v = x2_ref[...]
        n3, xh3, r3 = _rms_fwd(x2_v, gg_ref[...])
        pg = _sigmoid(_dot(n3, wg_ref[...]))
        pb = p_ref[...].astype(BF16)
        q = jnp.concatenate([jnp.dot(pb, wp_ref[j], preferred_element_type=F32) for j in range(N_CHIPS)], axis=1)
        ple, qh, rq = _rms_fwd(q, gp_ref[...])
        e = x2_v + pg * ple - t_ref[...]
        loss_ref[...] += 0.5 * jnp.sum(jnp.mean(e * e, axis=-1, keepdims=True), axis=0, keepdims=True)
        dy = e * (1.0 / D_MODEL)
        dpgl = dy * ple * pg * (1.0 - pg)
        dwg_ref[...] += _dot_tn(n3, dpgl)
        dx3, dgg = _rms_bwd(xh3, r3, gg_ref[...], _dot_nt(dpgl, wg_ref[...]))
        dgg_ref[...] += dgg
        dq, dgp = _rms_bwd(qh, rq, gp_ref[...], dy * pg)
        dgp_ref[...] += dgp
        for j in range(N_CHIPS):
            dwp_ref[j] += _dot_tn(pb, dq[:, j * 256:(j + 1) * 256])
        dx2_ref[...] = dy + dx3

    row = _rows(ts, D_MODEL)
    vec = _acc((1, D_MODEL))
    return pl.pallas_call(
        body, name="ple_loss", grid=(s // ts,),
        in_specs=[row, _rows(ts, PLE_DIM), row, _resident((1, D_MODEL)), _resident(w_gate.shape), _resident(w_proj.shape),
                  _resident((1, D_MODEL))],
        out_specs=[row, _acc((1, 128)), _acc(w_gate.shape), _acc(w_proj.shape), vec, vec],
        out_shape=[_sds((s, D_MODEL)), _sds((1, 128)), _sds(w_gate.shape), _sds(w_proj.shape), _sds((1, D_MODEL)),
                   _sds((1, D_MODEL))],
        compiler_params=_params(),
    )(x2, p, tgt, g_gate, w_gate, w_proj, g_post)


def _bwd_ffn_down(dx2, dn, up, gl, gg, fcw, w_down, g_post, ts):
    s = dx2.shape[0]
    nt = s // ts

    def body(dx2_ref, dn_ref, up_ref, gl_ref, gg_ref, fcw_ref, wd_ref, gpost_ref,
             dup_ref, dwd_ref, dfcw_ref, dfcb_ref, dgp_ref, carry_scr):
        i = pl.program_id(0)

        @pl.when(i == 0)
        def _():
            carry_scr[...] = jnp.zeros_like(carry_scr)
            dwd_ref[...] = jnp.zeros_like(dwd_ref)
            dfcw_ref[...] = jnp.zeros_like(dfcw_ref)
            dfcb_ref[...] = jnp.zeros_like(dfcb_ref)
            dgp_ref[...] = jnp.zeros_like(dgp_ref)

        _, xh, r = _rms_fwd(dn_ref[...], gpost_ref[...])
        ddn, dgp = _rms_bwd(xh, r, gpost_ref[...], dx2_ref[...])
        dgp_ref[...] += dgp
        dhid = _dot_nt(ddn, wd_ref[...])
        ug = up_ref[:, 0:D_FF].astype(F32)
        uv = up_ref[:, D_FF:].astype(F32)
        gl = gl_ref[...].astype(F32)
        w = fcw_ref[...]
        dwd_ref[...] += _dot_tn(gl * uv, ddn)
        dgh = dhid * uv * gg_ref[...].astype(F32)
        dup_ref[:, D_FF:] = (dhid * gl).astype(BF16)
        extd = jnp.concatenate([dgh, carry_scr[...]], axis=0)
        carry_scr[...] = dgh[0:CONV_HALO, :]
        d1 = _shift_up(extd, 1, ts)
        d2 = _shift_up(extd, 2, ts)
        dup_ref[:, 0:D_FF] = (w[2:3] * dgh + w[1:2] * d1 + w[0:1] * d2).astype(BF16)
        dfcw_ref[2:3, :] += jnp.sum(ug * dgh, axis=0, keepdims=True)
        dfcw_ref[1:2, :] += jnp.sum(ug * d1, axis=0, keepdims=True)
        dfcw_ref[0:1, :] += jnp.sum(ug * d2, axis=0, keepdims=True)
        dfcb_ref[...] += jnp.sum(dgh, axis=0, keepdims=True)

    row = _rows(ts, D_MODEL, nt)
    wide = _rows(ts, D_FF, nt)
    return pl.pallas_call(
        body, name="bwd_ffn_down", grid=(nt,),
        in_specs=[row, row, _rows(ts, 2 * D_FF, nt), wide, wide, _resident(fcw.shape), _resident(w_down.shape),
                  _resident((1, D_MODEL))],
        out_specs=[_rows(ts, 2 * D_FF, nt), _acc(w_down.shape), _acc(fcw.shape), _acc((1, D_FF)), _acc((1, D_MODEL))],
        out_shape=[_sds((s, 2 * D_FF), BF16), _sds(w_down.shape), _sds(fcw.shape), _sds((1, D_FF)), _sds((1, D_MODEL))],
        scratch_shapes=[pltpu.VMEM((CONV_HALO, D_FF), F32)],
        compiler_params=_params(),
    )(dx2, dn, up, gl, gg, fcw, w_down, g_post)


def _bwd_ffn_up(dup, x1, dx2, g_pre, w_up, ts, tasks=()):
    s = x1.shape[0]

    def body(dup_ref, x1_ref, dx2_ref, g_ref, wup_ref, dx1_ref, dg_ref):
        @pl.when(pl.program_id(0) == 0)
        def _():
            dg_ref[...] = jnp.zeros_like(dg_ref)

        _, xh, r = _rms_fwd(x1_ref[...], g_ref[...])
        dh2 = _dot_nt(dup_ref[:, 0:UP_SHARD], wup_ref[0])
        for j in range(1, N_CHIPS):
            dh2 = dh2 + _dot_nt(dup_ref[:, j * UP_SHARD:(j + 1) * UP_SHARD], wup_ref[j])
        dx, dg = _rms_bwd(xh, r, g_ref[...], dh2)
        dg_ref[...] += dg
        dx1_ref[...] = dx2_ref[...] + dx

    row = _rows(ts, D_MODEL)
    return _call(
        body, "bwd_ffn_up", (s // ts,),
        [_rows(ts, 2 * D_FF), row, row, _resident((1, D_MODEL)), _resident(w_up.shape)],
        [row, _acc((1, D_MODEL))], [_sds((s, D_MODEL)), _sds((1, D_MODEL))], [],
        (dup, x1, dx2, g_pre, w_up), tasks)


def _dw_up(h2, dup, ts, tasks=()):
    s = h2.shape[0]
    ts = min(DW_TILES * ts, s)

    def body(h2_ref, dup_ref, out_ref):
        @pl.when(pl.program_id(1) == 0)
        def _():
            out_ref[...] = jnp.zeros_like(out_ref)

        out_ref[0] += _dot_tn(h2_ref[...], dup_ref[...])

    return _call(
        body, "dw_up", (N_CHIPS, s // ts),
        [pl.BlockSpec((ts, D_MODEL), lambda j, i: (i, 0)), pl.BlockSpec((ts, UP_SHARD), lambda j, i: (i, j))],
        [pl.BlockSpec((1, D_MODEL, UP_SHARD), lambda j, i: (j, 0, 0))], [_sds((N_CHIPS, D_MODEL, UP_SHARD))], [],
        (h2, dup), tasks)


def _bwd_merge(dx1, mo, sp, sr, ypool, yrnn, g_post, w_o, ts, tasks=()):
    s = dx1.shape[0]

    def body(dx1_ref, mo_ref, sp_ref, sr_ref, yp_ref, yr_ref, g_ref, wo_ref,
             dgp_ref, dgr_ref, dyp_ref, dyr_ref, dwo_ref, dg_ref):
        @pl.when(pl.program_id(0) == 0)
        def _():
            dwo_ref[...] = jnp.zeros_like(dwo_ref)
            dg_ref[...] = jnp.zeros_like(dg_ref)

        _, xh, r = _rms_fwd(mo_ref[...], g_ref[...])
        dmo, dg = _rms_bwd(xh, r, g_ref[...], dx1_ref[...])
        dg_ref[...] += dg
        dmerged = _dot_nt(dmo, wo_ref[...])
        sp = sp_ref[...].astype(F32)
        sr = sr_ref[...].astype(F32)
        yp = yp_ref[...]
        yr = yr_ref[...]
        dwo_ref[...] += _dot_tn(sp * yp + sr * yr, dmo)
        dgp_ref[...] = (dmerged * yp * sp * (1.0 - sp)).astype(BF16)
        dgr_ref[...] = (dmerged * yr * sr * (1.0 - sr)).astype(BF16)
        dyp_ref[...] = (dmerged * sp).astype(BF16)
        dyr_ref[...] = (dmerged * sr).astype(BF16)

    row = _rows(ts, D_MODEL)
    return _call(
        body, "bwd_merge", (s // ts,),
        [row] * 6 + [_resident((1, D_MODEL)), _resident(w_o.shape)],
        [row] * 4 + [_acc(w_o.shape), _acc((1, D_MODEL))],
        [_sds((s, D_MODEL), BF16)] * 4 + [_sds(w_o.shape), _sds((1, D_MODEL))], [],
        (dx1, mo, sp, sr, ypool, yrnn, g_post, w_o), tasks)


def _bwd_rnn(dyr, urx, gl, gg, xc, r, ig, h, conv_w, wg, lam, w_rg_out, ts, tasks=()):
    s = urx.shape[0]
    nt = s // ts
    halo_blocks = ts // CONV_HALO

    def body(dyr_ref, urx_ref, gl_ref, gg_ref, xc_ref, r_ref, ig_ref, h_ref, hh_ref, cw_ref, wg_ref, lam_ref, wrg_ref,
             durx_ref, durg_ref, dwrg_ref, dwg_ref, dcw_ref, dcb_ref, dbg_ref, dlam_ref, mu_scr, carry_scr):
        i = pl.program_id(0)
        k = nt - 1 - i

        @pl.when(i == 0)
        def _():
            mu_scr[...] = jnp.zeros_like(mu_scr)
            carry_scr[...] = jnp.zeros_like(carry_scr)
            dwrg_ref[...] = jnp.zeros_like(dwrg_ref)
            dwg_ref[...] = jnp.zeros_like(dwg_ref)
            dcw_ref[...] = jnp.zeros_like(dcw_ref)
            dcb_ref[...] = jnp.zeros_like(dcb_ref)
            dbg_ref[...] = jnp.zeros_like(dbg_ref)
            dlam_ref[...] = jnp.zeros_like(dlam_ref)

        row = lax.broadcasted_iota(jnp.int32, (ts, 1), 0)
        first_row = (k * ts + row) == 0
        h = h_ref[...]
        dyr_v = dyr_ref[...]
        dhr = _dot_nt(dyr_v, wrg_ref[...])
        gl = gl_ref[...].astype(F32)
        dwrg_ref[...] += _dot_tn(h * gl, dyr_v)
        durg_ref[...] = (dhr * h * gg_ref[...].astype(F32)).astype(BF16)
        r_v = r_ref[...]
        ig_v = ig_ref[...]
        xc_v = xc_ref[...]
        lam_v = lam_ref[...]
        c8, a, m2, mult = _lru_coeffs(r_v, lam_v, first_row)
        b = jnp.where(row == ts - 1, 1.0, pltpu.roll(a, ts - 1, 0))
        lt = _scan_bwd(b, dhr * gl, mu_scr[0:1, :])
        mu_scr[0:1, :] = a[0:1, :] * lt[0:1, :]
        h_before = jnp.where(k > 0, hh_ref[CONV_HALO - 1:CONV_HALO, :], 0.0)
        hprev = jnp.where(row == 0, h_before, pltpu.roll(h, 1, 0))
        dmult = lt * ig_v * xc_v
        da = lt * hprev - jnp.where(first_row, 0.0, dmult * a * lax.rsqrt(m2))
        dla = da * a
        dlam_ref[...] += jnp.sum(dla * r_v, axis=0, keepdims=True)
        dlr = (dla * (-c8)) * r_v * (1.0 - r_v)
        dli = (lt * mult * xc_v) * ig_v * (1.0 - ig_v)
        dbg_ref[0:1, :] += jnp.sum(dlr, axis=0, keepdims=True)
        dbg_ref[1:2, :] += jnp.sum(dli, axis=0, keepdims=True)
        xcb = xc_v.astype(BF16)
        parts = []
        for q in range(GATE_BLOCKS):
            blk = slice(q * GATE_BLOCK, (q + 1) * GATE_BLOCK)
            dlr_q = dlr[:, blk].astype(BF16)
            dli_q = dli[:, blk].astype(BF16)
            parts.append(_dot_nt(dlr_q, wg_ref[0, q]) + _dot_nt(dli_q, wg_ref[1, q]))
            dwg_ref[0, q] += _dot_tn(xcb[:, blk], dlr_q)
            dwg_ref[1, q] += _dot_tn(xcb[:, blk], dli_q)
        dxc = lt * mult * ig_v + jnp.concatenate(parts, axis=1)
        extd = jnp.concatenate([dxc, carry_scr[...]], axis=0)
        carry_scr[...] = dxc[0:CONV_HALO, :]
        cw = cw_ref[...]
        urx_v = urx_ref[...]
        durx = cw[3:4] * dxc
        dcw_ref[3:4, :] += jnp.sum(urx_v * dxc, axis=0, keepdims=True)
        for j in (1, 2, 3):
            dj = _shift_up(extd, j, ts)
            durx = durx + cw[3 - j:4 - j] * dj
            dcw_ref[3 - j:4 - j, :] += jnp.sum(urx_v * dj, axis=0, keepdims=True)
        durx_ref[...] = durx.astype(BF16)
        dcb_ref[...] += jnp.sum(dxc, axis=0, keepdims=True)

        @pl.when(i == nt - 1)
        def _():
            dlam_ref[...] = dlam_ref[...] * (LRU_C * jax.nn.sigmoid(-lam_v))

    row_spec = _rows(ts, D_MODEL, nt)
    halo_spec = pl.BlockSpec((CONV_HALO, D_MODEL), lambda i: (jnp.maximum((nt - 1 - i) * halo_blocks - 1, 0), 0))
    vec = _acc((1, D_MODEL))
    return _call(
        body, "bwd_rnn", (nt,),
        [row_spec] * 8 + [halo_spec, _resident(conv_w.shape), _resident(wg.shape), _resident((1, D_MODEL)),
                          _resident(w_rg_out.shape)],
        [row_spec, row_spec, _acc(w_rg_out.shape), _acc(wg.shape), _acc(conv_w.shape), vec, _acc((2, D_MODEL)), vec],
        [_sds((s, D_MODEL), BF16), _sds((s, D_MODEL), BF16), _sds(w_rg_out.shape), _sds(wg.shape), _sds(conv_w.shape),
         _sds((1, D_MODEL)), _sds((2, D_MODEL)), _sds((1, D_MODEL))],
        [pltpu.VMEM((8, D_MODEL), F32), pltpu.VMEM((CONV_HALO, D_MODEL), F32)],
        (dyr, urx, gl, gg, xc, r, ig, h, h, conv_w, wg, lam, w_rg_out), tasks)


def _bwd_pool(dyp, d, pool_w, pool_scale, w_pool_out, ts, tasks=()):
    s = d.shape[0]
    nt = s // ts

    def body(dyp_ref, d_ref, pw_ref, ps_ref, wpo_ref, dzp_ref, dwpo_ref, dpw_ref, dps_ref, carry_scr):
        i = pl.program_id(0)
        k = nt - 1 - i

        @pl.when(i == 0)
        def _():
            carry_scr[...] = jnp.zeros_like(carry_scr)
            dwpo_ref[...] = jnp.zeros_like(dwpo_ref)
            dpw_ref[...] = jnp.zeros_like(dpw_ref)
            dps_ref[...] = jnp.zeros_like(dps_ref)

        dyp_v = dyp_ref[...]
        d_v = d_ref[...]
        ps = ps_ref[...]
        dypre = _dot_nt(dyp_v[:, 0:256], wpo_ref[0])
        for j in range(1, N_CHIPS):
            dypre = dypre + _dot_nt(dyp_v[:, j * 256:(j + 1) * 256], wpo_ref[j])
        y4 = jnp.concatenate([_dot(d_v[:, g * 128:(g + 1) * 128], pw_ref[g]) for g in range(POOL_GROUPS)], axis=1)
        ypre = (y4 * ps).astype(BF16)
        for j in range(N_CHIPS):
            dwpo_ref[j] += _dot_tn(ypre, dyp_v[:, j * 256:(j + 1) * 256])
        dps_ref[...] += jnp.sum(dypre * y4, axis=0, keepdims=True)
        dy4 = dypre * ps
        t = k * ts + lax.broadcasted_iota(jnp.int32, (ts, 1), 0)
        for g, w in enumerate(POOL_WINDOWS):
            lanes = slice(g * POOL_GROUP_DIM, (g + 1) * POOL_GROUP_DIM)
            dd = _dot_nt(dy4[:, lanes], pw_ref[g])
            dpw_ref[g] += _dot_tn(d_v[:, lanes], dy4[:, lanes])
            e = dd * (1.0 / jnp.minimum(t + 1, w).astype(F32))
            acc = jnp.concatenate([e, carry_scr[:, lanes]], axis=0)
            carry_scr[:, lanes] = e[0:POOL_HALO, :]
            n = ts + POOL_HALO
            sh = 1
            while sh < w:
                acc = acc + pltpu.roll(acc, n - sh, 0)
                sh *= 2
            dzp_ref[:, lanes] = (acc[:ts, :] - dd).astype(BF16)

    return _call(
        body, "bwd_pool", (nt,),
        [_rows(ts, D_MODEL, nt), _rows(ts, POOL_WIDTH, nt), _resident(pool_w.shape), _resident((1, POOL_WIDTH)),
         _resident(w_pool_out.shape)],
        [_rows(ts, POOL_WIDTH, nt), _acc(w_pool_out.shape), _acc(pool_w.shape), _acc((1, POOL_WIDTH))],
        [_sds((s, POOL_WIDTH), BF16), _sds(w_pool_out.shape), _sds(pool_w.shape), _sds((1, POOL_WIDTH))],
        [pltpu.VMEM((POOL_HALO, POOL_WIDTH), F32)],
        (dyp, d, pool_w, pool_scale, w_pool_out), tasks)


def _assemble_dz(dz_scr, dzp_ref, durx_ref, durg_ref, dgp_ref, dgr_ref):
    dz_scr[:, 0:512] = dzp_ref[...]
    dz_scr[:, 512:1536] = durx_ref[...]
    dz_scr[:, 1536:2560] = durg_ref[...]
    dz_scr[:, 2560:3584] = dgp_ref[...]
    dz_scr[:, 3584:4608] = dgr_ref[...]


def _dw_in(h1, dzp, durx, durg, dgp, dgr, ts, tasks=()):
    s = h1.shape[0]
    ts = min(2 * ts, s)

    def body(h1_ref, dzp_ref, durx_ref, durg_ref, dgp_ref, dgr_ref, out_ref, dz_scr):
        @pl.when(pl.program_id(0) == 0)
        def _():
            out_ref[...] = jnp.zeros_like(out_ref)

        _assemble_dz(dz_scr, dzp_ref, durx_ref, durg_ref, dgp_ref, dgr_ref)
        for j in range(N_CHIPS):
            out_ref[j] += _dot_tn(h1_ref[...], dz_scr[:, j * IN_SHARD:(j + 1) * IN_SHARD])

    row = _rows(ts, D_MODEL)
    shape = (N_CHIPS, D_MODEL, IN_SHARD)
    return _call(
        body, "dw_in", (s // ts,), [row, _rows(ts, POOL_WIDTH)] + [row] * 4, [_acc(shape)], [_sds(shape)],
        [pltpu.VMEM((ts, IN_TOTAL), BF16)], (h1, dzp, durx, durg, dgp, dgr), tasks)


def _bwd_in(dzp, durx, durg, dgp, dgr, x, dx1, g_pre, w_in, ts, tasks=()):
    s = x.shape[0]

    def body(dzp_ref, durx_ref, durg_ref, dgp_ref, dgr_ref, x_ref, dx1_ref, g_ref, win_ref, gx_ref, dg_ref, dz_scr):
        @pl.when(pl.program_id(0) == 0)
        def _():
            dg_ref[...] = jnp.zeros_like(dg_ref)

        _assemble_dz(dz_scr, dzp_ref, durx_ref, durg_ref, dgp_ref, dgr_ref)
        _, xh, r = _rms_fwd(x_ref[...], g_ref[...])
        dh1 = _dot_nt(dz_scr[:, 0:IN_SHARD], win_ref[0])
        for j in range(1, N_CHIPS):
            dh1 = dh1 + _dot_nt(dz_scr[:, j * IN_SHARD:(j + 1) * IN_SHARD], win_ref[j])
        dx, dg = _rms_bwd(xh, r, g_ref[...], dh1)
        dg_ref[...] += dg
        gx_ref[...] = dx1_ref[...] + dx

    row = _rows(ts, D_MODEL)
    return _call(
        body, "bwd_in", (s // ts,),
        [_rows(ts, POOL_WIDTH)] + [row] * 6 + [_resident((1, D_MODEL)), _resident(w_in.shape)],
        [row, _acc((1, D_MODEL))], [_sds((s, D_MODEL)), _sds((1, D_MODEL))],
        [pltpu.VMEM((ts, IN_TOTAL), BF16)], (dzp, durx, durg, dgp, dgr, x, dx1, g_pre, w_in), tasks)


def _place():
    x, y, c = lax.axis_index("x"), lax.axis_index("y"), lax.axis_index("c")
    others = [(1 - x, y), (x, 1 - y), (1 - x, 1 - y)]
    return x, y, c, 2 * x + y, others


def _remote(src, dst, send_sem, recv_sem, to):
    return pltpu.make_async_remote_copy(src_ref=src, dst_ref=dst, send_sem=send_sem, recv_sem=recv_sem,
                                        device_id=to, device_id_type=MESH)


def _own_slots(ws, dtypes, name, tasks=()):
    n = len(ws)
    hbm = pl.BlockSpec(memory_space=pltpu.HBM)

    def body(*refs):
        srcs, outs, f32_bufs, cast_bufs, sems = refs[:n], refs[n:2 * n], refs[2 * n:3 * n], refs[3 * n:4 * n], refs[4 * n]
        me = _place()[3]
        loads = [pltpu.make_async_copy(srcs[k], f32_bufs[k], sems.at[k, 0]) for k in range(n)]
        stores = [pltpu.make_async_copy(cast_bufs[k], outs[k].at[me], sems.at[k, 1]) for k in range(n)]
        for cp in loads:
            cp.start()
        for k in range(n):
            loads[k].wait()
            cast_bufs[k][...] = f32_bufs[k][...].astype(dtypes[k])
            stores[k].start()
        for cp in stores:
            cp.wait()

    return _call(
        body, name, (), [hbm] * n, [hbm] * n, [_sds((N_CHIPS,) + w.shape, dt) for w, dt in zip(ws, dtypes)],
        [pltpu.VMEM(w.shape, F32) for w in ws] + [pltpu.VMEM(w.shape, dt) for w, dt in zip(ws, dtypes)]
        + [pltpu.SemaphoreType.DMA((n, 2))],
        [pltpu.with_memory_space_constraint(w, pltpu.HBM) for w in ws], tasks)


def _run(tasks, name):
    if isinstance(tasks, _Task):
        return _call(None, name, (), [], [], [], [], (), (tasks,))[1][0]
    return _call(None, name, (), [], [], [], [], (), tuple(tasks))[1]


def _gather_task(bufs, relay_steps=0):
    n = len(bufs)

    def first_copy(out, w, j, ox, oy, sems):
        x, y, c, me, _ = _place()
        ah = out.shape[1] // 2
        mine = out.at[me, pl.ds(c * ah, ah)]
        return _remote(mine, mine, sems[0].at[w, j], sems[1].at[w, j], (ox, oy, c))

    def passed_copy(out, w, j, ox, oy, sems, half):
        x, y, c, _, _ = _place()
        ah = out.shape[1] // 2
        slot = out.at[2 * ox + oy, pl.ds(half * ah, ah)]
        return _remote(slot, slot, sems[0].at[w, 3 + j], sems[1].at[w, 3 + j], (x, y, 1 - c))

    def start(ins, outs, sems):
        others = _place()[4]
        for w, out in enumerate(outs):
            for j, (ox, oy) in enumerate(others):
                first_copy(out, w, j, ox, oy, sems).start()

    def relay(ins, outs, sems):
        x, y, c, _, others = _place()
        for w, out in enumerate(outs):
            ah = out.shape[1] // 2
            for j, (ox, oy) in enumerate(others):
                slot = out.at[2 * ox + oy, pl.ds(c * ah, ah)]
                _remote(slot, slot, sems[0].at[w, j], sems[1].at[w, j], (ox, oy, c)).wait_recv()
                passed_copy(out, w, j, ox, oy, sems, c).start()

    def finish(ins, outs, sems):
        x, y, c, _, others = _place()
        for w, out in enumerate(outs):
            for j, (ox, oy) in enumerate(others):
                passed_copy(out, w, j, ox, oy, sems, 1 - c).wait_recv()
        for w, out in enumerate(outs):
            for j, (ox, oy) in enumerate(others):
                first_copy(out, w, j, ox, oy, sems).wait_send()
                passed_copy(out, w, j, ox, oy, sems, c).wait_send()

    return _Task(bufs, [_sds(b.shape, b.dtype) for b in bufs], {i: i for i in range(n)},
                 [pltpu.SemaphoreType.DMA((n, 6)), pltpu.SemaphoreType.DMA((n, 6))], start, finish, relay, relay_steps)


def _halves_task(grads):
    n = len(grads)

    def copy(src, out, w, sems):
        x, y, c, _, _ = _place()
        ah = out.shape[1]
        return _remote(src.at[:, pl.ds((1 - c) * ah, ah)], out, sems[0].at[w], sems[1].at[w], (x, y, 1 - c))

    def start(ins, outs, sems):
        for w, (src, out) in enumerate(zip(ins, outs)):
            copy(src, out, w, sems).start()

    def finish(ins, outs, sems):
        for w, (src, out) in enumerate(zip(ins, outs)):
            copy(src, out, w, sems).wait()

    return _Task(grads, [_sds((g.shape[0], g.shape[1] // 2, g.shape[2]), g.dtype) for g in grads], {},
                 [pltpu.SemaphoreType.DMA((n,)), pltpu.SemaphoreType.DMA((n,))], start, finish)


def _exchange_task(sends, accs):
    n = len(accs)
    given = [s for s in sends if s is not None]

    def copies(ins, outs, sems):
        send_refs = iter(ins[:len(given)])
        srcs = [next(send_refs) if s is not None else None for s in sends]
        x, y, c, me, others = _place()
        for w, out in enumerate(outs):
            for j, (ox, oy) in enumerate(others):
                src = out.at[me] if srcs[w] is None else srcs[w].at[2 * ox + oy]
                yield _remote(src, out.at[me], sems[0].at[w, j], sems[1].at[w, j], (ox, oy, c))

    def start(ins, outs, sems):
        for cp in copies(ins, outs, sems):
            cp.start()

    def finish(ins, outs, sems):
        x, y, c, _, others = _place()
        for w, out in enumerate(outs):
            for j, (ox, oy) in enumerate(others):
                slot = out.at[2 * ox + oy]
                _remote(slot, slot, sems[0].at[w, j], sems[1].at[w, j], (ox, oy, c)).wait_recv()
        for cp in copies(ins, outs, sems):
            cp.wait_send()

    return _Task(given + list(accs), [_sds(a.shape, a.dtype) for a in accs], {len(given) + i: i for i in range(n)},
                 [pltpu.SemaphoreType.DMA((n, 3)), pltpu.SemaphoreType.DMA((n, 3))], start, finish)


def _swap_task(arrays):
    n = len(arrays)

    def copy(src, out, w, sems):
        x, y, c, _, _ = _place()
        return _remote(src, out, sems[0].at[w], sems[1].at[w], (x, y, 1 - c))

    def start(ins, outs, sems):
        for w, (src, out) in enumerate(zip(ins, outs)):
            copy(src, out, w, sems).start()

    def finish(ins, outs, sems):
        for w, (src, out) in enumerate(zip(ins, outs)):
            copy(src, out, w, sems).wait()

    return _Task(arrays, [_sds(a.shape, a.dtype) for a in arrays], {},
                 [pltpu.SemaphoreType.DMA((n,)), pltpu.SemaphoreType.DMA((n,))], start, finish)


def _all_devices_task(arrays):
    n = len(arrays)
    flips = [(dx, dy, dc) for dx in (0, 1) for dy in (0, 1) for dc in (0, 1)][1:]

    def peers():
        x, y, c, _, _ = _place()
        flip = lambda v, d: 1 - v if d else v
        return 4 * x + 2 * y + c, [(flip(x, dx), flip(y, dy), flip(c, dc)) for dx, dy, dc in flips]

    def start(ins, outs, sems):
        me, others = peers()
        for w, (src, out) in enumerate(zip(ins, outs)):
            pltpu.make_async_copy(src, out.at[me], sems[2].at[w]).start()
            for k, peer in enumerate(others):
                _remote(src, out.at[me], sems[0].at[w, k], sems[1].at[w, k], peer).start()

    def finish(ins, outs, sems):
        me, others = peers()
        for w, (src, out) in enumerate(zip(ins, outs)):
            for k, (px, py, pc) in enumerate(others):
                slot = out.at[4 * px + 2 * py + pc]
                _remote(slot, slot, sems[0].at[w, k], sems[1].at[w, k], (px, py, pc)).wait_recv()
            for k, peer in enumerate(others):
                _remote(src, out.at[me], sems[0].at[w, k], sems[1].at[w, k], peer).wait_send()
            pltpu.make_async_copy(src, out.at[me], sems[2].at[w]).wait()

    return _Task(arrays, [_sds((8,) + a.shape, a.dtype) for a in arrays], {},
                 [pltpu.SemaphoreType.DMA((n, 7)), pltpu.SemaphoreType.DMA((n, 7)), pltpu.SemaphoreType.DMA((n,))],
                 start, finish)


def _share_task(shares):
    n = len(shares)

    def copy(out, w, sems, slot):
        x, y, c, _, _ = _place()
        return _remote(out.at[slot], out.at[slot], sems[0].at[w], sems[1].at[w], (x, y, 1 - c))

    def start(ins, outs, sems):
        c = _place()[2]
        for w, out in enumerate(outs):
            copy(out, w, sems, c).start()

    def finish(ins, outs, sems):
        c = _place()[2]
        for w, out in enumerate(outs):
            copy(out, w, sems, 1 - c).wait_recv()
        for w, out in enumerate(outs):
            copy(out, w, sems, c).wait_send()

    return _Task(shares, [_sds(s.shape, s.dtype) for s in shares], {i: i for i in range(n)},
                 [pltpu.SemaphoreType.DMA((n,)), pltpu.SemaphoreType.DMA((n,))], start, finish)


TILE_BYTES = 2 * 1024 * 1024
PARTIAL_TILE_BYTES = 512 * 1024


def _row_tile(rows, cols, limit=TILE_BYTES):
    best = 8
    for tr in range(8, rows + 1, 8):
        if rows % tr == 0 and tr * cols * 4 <= limit:
            best = tr
    assert rows % best == 0, (rows, cols)
    return best


def _chip_partial(g, got, place, wire_dtype):
    ns, ah, b = got.shape
    sharded = ns == N_CHIPS
    tr = _row_tile(ah, b, PARTIAL_TILE_BYTES)
    nb = ah // tr

    def body(place_ref, *refs):
        g_refs, got_refs, outs = refs[:ns], refs[ns:2 * ns], refs[2 * ns:]
        parts = [g_refs[k][0] + got_refs[k][0] for k in range(ns)]
        own = parts[0]
        if sharded:
            for k in range(ns):
                outs[0][k] = parts[k].astype(wire_dtype)
                if k:
                    own = jnp.where(place_ref[0] == k, parts[k], own)
        outs[-1][0] = own.astype(wire_dtype)

    blk = (1, tr, b)
    in_specs = ([pl.BlockSpec(blk, lambda i, s, k=k: (k, s[1] * nb + i, 0)) for k in range(ns)]
                + [pl.BlockSpec(blk, lambda i, s, k=k: (k, i, 0)) for k in range(ns)])
    acc_spec = pl.BlockSpec(blk, lambda i, s: (s[0], i, 0))
    acc_shape = _sds((N_CHIPS, ah, b), wire_dtype)
    out = pl.pallas_call(
        body, name="grad_chip_partial",
        grid_spec=pltpu.PrefetchScalarGridSpec(
            num_scalar_prefetch=1, grid=(nb,), in_specs=in_specs,
            out_specs=[pl.BlockSpec((ns, tr, b), lambda i, s: (0, i, 0)), acc_spec] if sharded else [acc_spec]),
        out_shape=[acc_shape, acc_shape] if sharded else [acc_shape],
        compiler_params=pltpu.CompilerParams(dimension_semantics=("arbitrary",), vmem_limit_bytes=32 * 1024 * 1024),
    )(place, *([g] * ns), *([got] * ns))
    return (out[0], out[1]) if sharded else (None, out[0])


def _chip_sum(acc, place):
    _, ah, b = acc.shape
    tr = _row_tile(ah, b)

    def body(place_ref, p_ref, out_ref):
        total = p_ref[0].astype(F32) + p_ref[1].astype(F32)
        total = total + p_ref[2].astype(F32)
        out_ref[0] = total + p_ref[3].astype(F32)

    return pl.pallas_call(
        body, name="grad_chip_sum",
        grid_spec=pltpu.PrefetchScalarGridSpec(
            num_scalar_prefetch=1, grid=(ah // tr,),
            in_specs=[pl.BlockSpec((N_CHIPS, tr, b), lambda i, s: (0, i, 0))],
            out_specs=pl.BlockSpec((1, tr, b), lambda i, s: (s[1], i, 0))),
        out_shape=_sds((2, ah, b)),
        compiler_params=pltpu.CompilerParams(dimension_semantics=("arbitrary",)),
    )(place, acc)


def _adam_math(w, g, m, v):
    nm = ADAM_B1 * m + (1.0 - ADAM_B1) * g
    nv = ADAM_B2 * v + (1.0 - ADAM_B2) * (g * g)
    m_hat = nm / (1.0 - ADAM_B1 ** ADAM_STEP)
    v_hat = nv / (1.0 - ADAM_B2 ** ADAM_STEP)
    return -ADAM_LR * (m_hat / (jnp.sqrt(v_hat) + ADAM_EPS) + ADAM_WD * w), nm, nv


def _adamw(w, g, m, v):
    a, b = w.shape
    tr = _row_tile(a, b)

    def body(w_ref, g_ref, m_ref, v_ref, g_out, d_ref, nm_ref, nv_ref):
        g_out[...] = g_ref[...]
        d_ref[...], nm_ref[...], nv_ref[...] = _adam_math(w_ref[...], g_ref[...], m_ref[...], v_ref[...])

    blk = pl.BlockSpec((tr, b), lambda i: (i, 0))
    return pl.pallas_call(
        body, name="adamw", grid=(a // tr,),
        in_specs=[blk] * 4, out_specs=[blk] * 4, out_shape=[_sds((a, b))] * 4,
        compiler_params=pltpu.CompilerParams(dimension_semantics=("arbitrary",)),
    )(w, g, m, v)


def _adamw_sum(w, m, v, acc, got, place):
    a, b = w.shape
    ah = a // 2
    tr = _row_tile(ah, b, TILE_BYTES // 2)
    nb = ah // tr

    def body(place_ref, w_ref, m_ref, v_ref, acc_ref, got_ref, g_out, d_ref, nm_ref, nv_ref):
        mine = (pl.program_id(0) // nb) == place_ref[1]
        part = lambda k: jnp.where(mine, acc_ref[k], got_ref[k]).astype(F32)
        g = part(0) + part(1)
        g = g + part(2)
        g = g + part(3)
        g_out[...] = g
        d_ref[...], nm_ref[...], nv_ref[...] = _adam_math(w_ref[...], g, m_ref[...], v_ref[...])

    blk = pl.BlockSpec((tr, b), lambda i, s: (i, 0))
    mine_spec = pl.BlockSpec((N_CHIPS, tr, b), lambda i, s: (0, jnp.where(i // nb == s[1], i % nb, 0), 0))
    got_spec = pl.BlockSpec((N_CHIPS, tr, b), lambda i, s: (0, jnp.where(i // nb == s[1], 0, i % nb), 0))
    return pl.pallas_call(
        body, name="adamw_sum",
        grid_spec=pltpu.PrefetchScalarGridSpec(
            num_scalar_prefetch=1, grid=(a // tr,), in_specs=[blk] * 3 + [mine_spec, got_spec], out_specs=[blk] * 4),
        out_shape=[_sds((a, b))] * 4,
        compiler_params=pltpu.CompilerParams(dimension_semantics=("arbitrary",), vmem_limit_bytes=VMEM_LIMIT),
    )(place, w, m, v, acc, got)


def _adamw_pieces(g, pieces, name):
    n = len(pieces)

    def body(g_ref, *refs):
        def grad(rows, cols):
            if len(g_ref.shape) == 2:
                return g_ref[rows, cols]
            total = g_ref[0, rows, cols]
            for k in range(1, g_ref.shape[0]):
                total = total + g_ref[k, rows, cols]
            return total

        ins, outs = refs[:3 * n], refs[3 * n:]
        for i, piece in enumerate(pieces):
            w_ref, m_ref, v_ref = ins[3 * i:3 * i + 3]
            o_g, o_d, o_m, o_v = outs[4 * i:4 * i + 4]
            if len(piece) == 5:
                g_v = grad(piece[3], piece[4])
                o_g[...] = g_v
                o_d[...], o_m[...], o_v[...] = _adam_math(w_ref[...], g_v, m_ref[...], v_ref[...])
            else:
                for r in range(w_ref.shape[1] // SMALL_COLS):
                    lanes = slice(r * SMALL_COLS, (r + 1) * SMALL_COLS)
                    g_v = grad(slice(piece[3] + r, piece[3] + r + 1), slice(None))
                    o_g[:, lanes] = g_v
                    o_d[:, lanes], o_m[:, lanes], o_v[:, lanes] = _adam_math(w_ref[:, lanes], g_v, m_ref[:, lanes],
                                                                            v_ref[:, lanes])

    operands = [t for piece in pieces for t in piece[:3]]
    out = pl.pallas_call(
        body, name=name,
        out_shape=[_sds(piece[0].shape) for piece in pieces for _ in range(4)],
    )(g, *operands)
    return [tuple(out[4 * i:4 * i + 4]) for i in range(n)]


TINY_ROWS, TINY_COLS = 16, 768
SMALL_COLS = 128
SMALL_ROWS = 624


def _pack_tiny(conv_w, b_gates, fcw):
    ns = conv_w.shape[0]
    pad = lambda t: jnp.pad(t, ((0, 0), (0, 0), (0, TINY_COLS - t.shape[2])))
    z = lambda rows: jnp.zeros((ns, rows, TINY_COLS), F32)
    return jnp.concatenate([pad(conv_w), pad(b_gates), z(2), fcw, z(5)], axis=1)


def _unpack_tiny(t):
    return t[:, 0:4, 0:256], t[:, 4:6, 0:256], t[:, 8:11, :]


def _cols_to_shards(t, n):
    return t.reshape(t.shape[0], N_CHIPS, n).transpose(1, 0, 2)


def _shards_to_cols(t):
    return t.transpose(1, 0, 2).reshape(t.shape[1], -1)


_VECTORS = ("g_mix_post", "conv_b", "lru_lambda", "g_ffn_pre", "g_ffn_post", "g_ple_gate", "g_ple_post", "pool_scale",
            "ffn_conv_b")
_VECTOR_LEN = {"pool_scale": POOL_WIDTH, "ffn_conv_b": D_FF}
POOL_W_ROWS = POOL_GROUPS * POOL_GROUP_DIM


def _vector_rows():
    rows, row = {}, POOL_W_ROWS
    for k in _VECTORS:
        rows[k] = row
        row += max(8, _VECTOR_LEN.get(k, D_MODEL) // SMALL_COLS)
    return rows, row


def _pack_small(grads, loss):
    tiles = lambda t: jnp.pad(t, ((0, -t.shape[0] % 8), (0, 0)))
    parts = [grads["pool_w"].reshape(POOL_W_ROWS, SMALL_COLS)] + [tiles(grads[k].reshape(-1, SMALL_COLS)) for k in _VECTORS]
    parts.append(tiles(loss))
    used = sum(t.shape[0] for t in parts)
    return jnp.concatenate(parts + [jnp.zeros((SMALL_ROWS - used, SMALL_COLS), F32)], axis=0)


def _gates_block_diag(w):
    w4 = w.reshape(2, GATE_BLOCKS, 4, RNN_HEAD_DIM, RNN_HEAD_DIM)
    eye = jnp.eye(4, dtype=w.dtype)
    return jnp.einsum("gqhij,hk->gqhikj", w4, eye).reshape(2, GATE_BLOCKS, GATE_BLOCK, GATE_BLOCK)


def _gates_from_block_diag(dw):
    d6 = dw.reshape(2, GATE_BLOCKS, 4, RNN_HEAD_DIM, 4, RNN_HEAD_DIM)
    blocks = [d6[:, :, hh, :, hh, :] for hh in range(4)]
    return jnp.stack(blocks, axis=2).reshape(2, RNN_HEADS, RNN_HEAD_DIM, RNN_HEAD_DIM)


ROW_TILE = 256
DW_TILES = 4

_SHARDED = ("w_in", "w_pool_out", "w_rg_out", "w_o", "w_up", "w_down", "w_ple_gate", "w_ple_proj")
_WEIGHTS = ("g_mix_pre", "g_mix_post", "w_in", "pool_w", "pool_scale", "w_pool_out", "conv_w", "conv_b", "w_rg_gates",
            "b_rg_gates", "lru_lambda", "w_rg_out", "w_o", "g_ffn_pre", "g_ffn_post", "w_up", "ffn_conv_w", "ffn_conv_b",
            "w_down", "g_ple_gate", "w_ple_gate", "w_ple_proj", "g_ple_post")


def _wire_dtype(g):
    return BF16 if g.shape[1] >= 64 and g.shape[2] > SMALL_COLS else F32


def _partials(grads, got, place):
    parts = [_chip_partial(g, r, place, _wire_dtype(g)) for g, r in zip(grads, got)]
    return [send for send, _ in parts], [acc for _, acc in parts]


def _whole(both):
    return [b.reshape(2 * b.shape[1], b.shape[2]) for b in both]


def _step(x, p, tgt, rep, place, ts):
    vec = lambda k: rep[k].reshape(1, -1)
    pool_w = rep["pool_w"].astype(BF16)
    wg = _gates_block_diag(rep["w_rg_gates"]).astype(BF16)
    sq = lambda t: t.reshape(D_MODEL, D_MODEL)
    by4 = lambda t: t.reshape(N_CHIPS, -1, D_MODEL)

    first, ride1, ride2, ride3 = (("w_in", "w_pool_out", "tiny"), ("w_rg_out", "w_o", "w_down"), ("w_up",),
                                  ("w_ple_gate", "w_ple_proj"))
    later = ride1 + ride2 + ride3
    tiny = _pack_tiny(rep["conv_w"][None], rep["b_rg_gates"][None], rep["ffn_conv_w"][None])[0]
    own_first, _ = _own_slots([rep["w_in"], rep["w_pool_out"], tiny], [BF16, BF16, F32], "own_slots_first")
    own_later, (got,) = _own_slots([rep[k] for k in later], [BF16] * len(later), "own_slots_gather_first",
                                   [_gather_task(own_first)])
    own = dict(zip(later, own_later))
    full = dict(zip(first, got))
    conv_w, b_gates, fcw = [_shards_to_cols(t) for t in _unpack_tiny(full["tiny"])]

    (urx, urg, gp, gr, d, ypool, h1), (got,) = _fwd_in_pool(
        x, vec("g_mix_pre"), full["w_in"], pool_w, vec("pool_scale"), full["w_pool_out"], ts,
        [_gather_task([own[k] for k in ride1])])
    full.update(zip(ride1, got))
    w_rg_out, w_o, w_down = sq(full["w_rg_out"]), sq(full["w_o"]), full["w_down"].reshape(D_FF, D_MODEL)
    (xc, r, ig, h, yrnn, mo, x1, glr, ggr, sp, sr), (got,) = _fwd_rnn_merge(
        urx, urg, gp, gr, ypool, x, conv_w, vec("conv_b"), wg, b_gates, vec("lru_lambda"), w_rg_out, w_o,
        vec("g_mix_post"), ts, [_gather_task([own[k] for k in ride2], relay_steps=2)])
    full.update(zip(ride2, got))
    (up, gl, gg, h2, dn, x2), (got,) = _fwd_ffn(x1, vec("g_ffn_pre"), full["w_up"], fcw, vec("ffn_conv_b"), w_down,
                                               vec("g_ffn_post"), ts, [_gather_task([own[k] for k in ride3], relay_steps=4)])
    full.update(zip(ride3, got))
    dx2, loss, d_w_gate, d_w_proj, d_g_ple_gate, d_g_ple_post = _ple_loss(
        x2, p, tgt, vec("g_ple_gate"), sq(full["w_ple_gate"]), full["w_ple_proj"], vec("g_ple_post"), ts)
    dup, d_w_down, d_fcw, d_fcb, d_g_ffn_post = _bwd_ffn_down(dx2, dn, up, gl, gg, fcw, w_down, vec("g_ffn_post"), ts)

    names1, grads1 = ("w_ple_gate", "w_ple_proj", "w_down"), [by4(d_w_gate), d_w_proj, by4(d_w_down)]
    (dx1, d_g_ffn_pre), (got1,) = _bwd_ffn_up(dup, x1, dx2, vec("g_ffn_pre"), full["w_up"], ts, [_halves_task(grads1)])
    (d_w_up,), (accs1,) = _dw_up(h2, dup, ts, [_exchange_task(*_partials(grads1, got1, place))])
    (dgp, dgr, dyp, dyr, d_w_o, d_g_mix_post), (got2, theirs1) = _bwd_merge(
        dx1, mo, sp, sr, ypool, yrnn, vec("g_mix_post"), w_o, ts, [_halves_task([d_w_up]), _swap_task(accs1)])
    (durx, durg, d_w_rg_out, d_wg, d_conv_w, d_conv_b, d_b_gates, d_lam), (accs2,) = _bwd_rnn(
        dyr, urx, glr, ggr, xc, r, ig, h, conv_w, wg, vec("lru_lambda"), w_rg_out, ts,
        [_exchange_task(*_partials([d_w_up], got2, place))])
    names3 = ("w_o", "w_rg_out", "tiny", "w_rg_gates")
    grads3 = [by4(d_w_o), by4(d_w_rg_out),
              _pack_tiny(_cols_to_shards(d_conv_w, 256), _cols_to_shards(d_b_gates, 256), _cols_to_shards(d_fcw, 768)),
              _gates_from_block_diag(d_wg).reshape(1, 2 * RNN_HEADS * RNN_HEAD_DIM, RNN_HEAD_DIM)]
    (dzp, d_w_pool_out, d_pool_w, d_pool_scale), (got3,) = _bwd_pool(
        dyp, d, pool_w, vec("pool_scale"), full["w_pool_out"], ts, [_halves_task(grads3)])
    (d_w_in,), (accs3,) = _dw_in(h1, dzp, durx, durg, dgp, dgr, ts, [_exchange_task(*_partials(grads3, got3, place))])

    replicated = {"g_mix_post": d_g_mix_post, "conv_b": d_conv_b, "lru_lambda": d_lam, "g_ffn_pre": d_g_ffn_pre,
                  "g_ffn_post": d_g_ffn_post, "g_ple_gate": d_g_ple_gate, "g_ple_post": d_g_ple_post,
                  "pool_scale": d_pool_scale, "ffn_conv_b": d_fcb, "pool_w": d_pool_w}
    names4 = ("w_in", "w_pool_out", "small")
    grads4 = [d_w_in, d_w_pool_out, _pack_small(replicated, loss)[None]]
    got4 = _run(_halves_task(grads4), "grad_sibling_halves")
    (grad_x, d_g_mix_pre), (accs4,) = _bwd_in(dzp, durx, durg, dgp, dgr, x, dx1, vec("g_mix_pre"), full["w_in"], ts,
                                              [_exchange_task(*_partials(grads4, got4, place))])

    big = accs2 + accs3[:2] + accs4[:2]
    small = [_chip_sum(acc, place) for acc in accs3[2:] + accs4[2:]]
    theirs, both, (g_mix_pre_parts,) = _run(
        [_swap_task(big), _share_task(small), _all_devices_task([d_g_mix_pre.reshape(SUBLANES, SMALL_COLS)])],
        "grad_sibling_share")
    partials = dict(zip(names1 + ("w_up",) + names3[:2] + names4[:2], zip(accs1 + big, theirs1 + theirs)))
    return grad_x, partials, dict(zip(names3[2:] + names4[2:], _whole(both))), g_mix_pre_parts


def kernel(x, p, g_mix_pre, g_mix_post, w_in, pool_w, pool_scale, w_pool_out, conv_w, conv_b, w_rg_gates, b_rg_gates, lru_lambda, w_rg_out, w_o, g_ffn_pre, g_ffn_post, w_up, ffn_conv_w, ffn_conv_b, w_down, g_ple_gate, w_ple_gate, w_ple_proj, g_ple_post, loss_target, m_g_mix_pre, m_g_mix_post, m_w_in, m_pool_w, m_pool_scale, m_w_pool_out, m_conv_w, m_conv_b, m_w_rg_gates, m_b_rg_gates, m_lru_lambda, m_w_rg_out, m_w_o, m_g_ffn_pre, m_g_ffn_post, m_w_up, m_ffn_conv_w, m_ffn_conv_b, m_w_down, m_g_ple_gate, m_w_ple_gate, m_w_ple_proj, m_g_ple_post, v_g_mix_pre, v_g_mix_post, v_w_in, v_pool_w, v_pool_scale, v_w_pool_out, v_conv_w, v_conv_b, v_w_rg_gates, v_b_rg_gates, v_lru_lambda, v_w_rg_out, v_w_o, v_g_ffn_pre, v_g_ffn_post, v_w_up, v_ffn_conv_w, v_ffn_conv_b, v_w_down, v_g_ple_gate, v_w_ple_gate, v_w_ple_proj, v_g_ple_post):
    args = dict(locals())
    w = {k: args[k][0] for k in _WEIGHTS}
    m = {k: args["m_" + k][0] for k in _WEIGHTS}
    v = {k: args["v_" + k][0] for k in _WEIGHTS}
    place = jnp.stack([2 * lax.axis_index("x") + lax.axis_index("y"), lax.axis_index("c")]).astype(jnp.int32)
    grad_x, partials, reduced, g_mix_pre_parts = _step(x[0], p[0, 0], loss_target[0], w, place, ROW_TILE)

    gates_2d = (2 * RNN_HEADS * RNN_HEAD_DIM, RNN_HEAD_DIM)
    as2d = lambda k, shape: tuple(t[k].reshape(shape) for t in (w, m, v))
    done = {k: tuple(_adamw_sum(w[k], m[k], v[k], *partials[k], place)) for k in _SHARDED}
    gates_w, gates_m, gates_v = as2d("w_rg_gates", gates_2d)
    done["w_rg_gates"] = tuple(_adamw(gates_w, reduced["w_rg_gates"], gates_m, gates_v))
    tiny_names = ("conv_w", "b_rg_gates", "ffn_conv_w")
    tiny_at = ((slice(0, 4), slice(0, 256)), (slice(4, 6), slice(0, 256)), (slice(8, 11), slice(None)))
    done.update(zip(tiny_names, _adamw_pieces(
        reduced["tiny"], [(w[k], m[k], v[k]) + at for k, at in zip(tiny_names, tiny_at)], "adamw_tiny")))
    vector_rows, loss_row = _vector_rows()
    pieces = [as2d("pool_w", (POOL_W_ROWS, SMALL_COLS)) + (slice(0, POOL_W_ROWS), slice(None))]
    pieces += [as2d(k, (1, -1)) + (vector_rows[k],) for k in _VECTORS]
    done.update(zip(("pool_w",) + _VECTORS, _adamw_pieces(reduced["small"], pieces, "adamw_small")))
    done["g_mix_pre"] = _adamw_pieces(g_mix_pre_parts, [as2d("g_mix_pre", (1, -1)) + (0,)], "adamw_g_mix_pre")[0]

    result = [reduced["small"][loss_row, 0], grad_x[None]]
    for kind in range(4):
        result += [done[k][kind].reshape(args[k].shape) for k in _WEIGHTS]
    return tuple(result)
```

```python
import functools

import jax
import jax.numpy as jnp
from jax import lax
from jax.experimental import pallas as pl
from jax.experimental.pallas import tpu as pltpu

F32 = jnp.float32
BF16 = jnp.bfloat16

D_MODEL = 1024
POOL_WINDOWS = (2, 4, 8, 16)
POOL_GROUPS = 4
POOL_WIDTH = 512
POOL_GROUP_DIM = 128
RNN_HEADS = 16
RNN_HEAD_DIM = 64
GATE_BLOCK = 256
GATE_BLOCKS = D_MODEL // GATE_BLOCK
LRU_C = 8.0
D_FF = 3072
PLE_DIM = 256
RMS_EPS = 1e-6
IN_TOTAL = 4608
N_CHIPS = 4
IN_SHARD = IN_TOTAL // N_CHIPS
UP_SHARD = 2 * D_FF // N_CHIPS
POOL_HALO = 16
CONV_HALO = 8

ADAM_LR = 0.001
ADAM_B1 = 0.9
ADAM_B2 = 0.999
ADAM_EPS = 1e-08
ADAM_WD = 0.01
ADAM_STEP = 10

VMEM_LIMIT = 56 * 1024 * 1024
MESH = pl.DeviceIdType.MESH

_GELU_C = 0.7978845608028654
_GELU_A = 0.044715


def _dot(a, b):
    return jnp.dot(a.astype(BF16), b.astype(BF16), preferred_element_type=F32)


def _dot_nt(a, b):
    return lax.dot_general(a.astype(BF16), b.astype(BF16), (((1,), (1,)), ((), ())), preferred_element_type=F32)


def _dot_tn(a, b):
    return lax.dot_general(a.astype(BF16), b.astype(BF16), (((0,), (0,)), ((), ())), preferred_element_type=F32)


def _rms_fwd(x, g):
    r = lax.rsqrt(jnp.mean(x * x, axis=-1, keepdims=True) + RMS_EPS)
    xh = x * r
    return xh * g, xh, r


def _rms_bwd(xh, r, g, dy):
    dxh = dy * g
    dg = jnp.sum(dy * xh, axis=0, keepdims=True)
    dx = r * (dxh - xh * jnp.mean(dxh * xh, axis=-1, keepdims=True))
    return dx, dg


def _sigmoid(x):
    return 0.5 * jnp.tanh(0.5 * x) + 0.5


def _gelu(x):
    t = jnp.tanh(_GELU_C * (x + _GELU_A * x * x * x))
    return 0.5 * x * (1.0 + t), t


def _gelu_grad(x, t):
    return 0.5 * (1.0 + t) + 0.5 * x * (1.0 - t * t) * _GELU_C * (1.0 + 3.0 * _GELU_A * x * x)


def _softplus_neg(lam):
    nl = -lam
    return jnp.maximum(nl, 0.0) + jnp.log(1.0 + jnp.exp(-jnp.abs(nl)))


def _lru_coeffs(r, lam, first_row):
    c8 = LRU_C * _softplus_neg(lam)
    la = -(c8 * r)
    a = jnp.exp(la)
    m2 = jnp.tanh(-la) * (1.0 + a * a)
    mult = jnp.where(first_row, 1.0, jnp.sqrt(m2))
    return c8, a, m2, mult


SUBLANES = 8


def _scan_fwd(a, u, carry):
    n = a.shape[0]
    sub = lax.broadcasted_iota(jnp.int32, (n, 1), 0) % SUBLANES
    acc_a, acc_h = a, u
    for s in (1, 2, 4):
        m = sub >= s
        h_s = jnp.where(m, pltpu.roll(acc_h, s, 0), 0.0)
        a_s = jnp.where(m, pltpu.roll(acc_a, s, 0), 1.0)
        acc_h = acc_a * h_s + acc_h
        acc_a = acc_a * a_s
    out = []
    for g in range(n // SUBLANES):
        rows = slice(g * SUBLANES, (g + 1) * SUBLANES)
        out.append(acc_h[rows] + acc_a[rows] * carry)
        carry = out[-1][SUBLANES - 1:SUBLANES]
    return jnp.concatenate(out, axis=0)


def _scan_bwd(b, g, carry):
    n = b.shape[0]
    sub = lax.broadcasted_iota(jnp.int32, (n, 1), 0) % SUBLANES
    acc_b, acc_l = b, g
    for s in (1, 2, 4):
        m = sub < SUBLANES - s
        l_s = jnp.where(m, pltpu.roll(acc_l, n - s, 0), 0.0)
        b_s = jnp.where(m, pltpu.roll(acc_b, n - s, 0), 1.0)
        acc_l = acc_b * l_s + acc_l
        acc_b = acc_b * b_s
    out = [None] * (n // SUBLANES)
    for g in reversed(range(n // SUBLANES)):
        rows = slice(g * SUBLANES, (g + 1) * SUBLANES)
        out[g] = acc_l[rows] + acc_b[rows] * carry
        carry = out[g][0:1]
    return jnp.concatenate(out, axis=0)


def _shift_down(ext, k, halo):
    return pltpu.roll(ext, k, 0)[halo:] if k else ext[halo:]


def _shift_up(ext, k, ts):
    return pltpu.roll(ext, ext.shape[0] - k, 0)[:ts] if k else ext[:ts]


def _rows(ts, width, nt=None, col=0):
    if nt is None:
        return pl.BlockSpec((ts, width), lambda i: (i, col))
    return pl.BlockSpec((ts, width), lambda i: (nt - 1 - i, col))


def _resident(shape):
    zeros = (0,) * len(shape)
    return pl.BlockSpec(shape, lambda i: zeros, pipeline_mode=pl.Buffered(1))


def _acc(shape):
    zeros = (0,) * len(shape)
    return pl.BlockSpec(shape, lambda i: zeros)


def _params():
    return pltpu.CompilerParams(dimension_semantics=("arbitrary",), vmem_limit_bytes=VMEM_LIMIT)


def _sds(shape, dtype=F32):
    return jax.ShapeDtypeStruct(shape, dtype)


class _Task:
    def __init__(self, ins, out_shapes, aliases, sems, start, finish, relay=None, relay_steps=0):
        self.ins, self.out_shapes, self.aliases, self.sems = list(ins), list(out_shapes), dict(aliases), list(sems)
        self.start, self.relay, self.finish, self.relay_steps = start, relay, finish, relay_steps


def _call(body, name, grid, in_specs, out_specs, out_shape, scratch_shapes, args, tasks=()):
    n_in, n_out, n_scr = len(in_specs), len(out_specs), len(scratch_shapes)
    t_in = [len(t.ins) for t in tasks]
    t_out = [len(t.out_shapes) for t in tasks]
    t_sem = [len(t.sems) for t in tasks]
    steps = 1
    for g in grid:
        steps *= g

    def take(refs, pos, counts):
        groups = []
        for c in counts:
            groups.append(refs[pos:pos + c])
            pos += c
        return groups, pos

    def wrapped(*refs):
        (cin,), pos = take(refs, 0, [n_in])
        tin, pos = take(refs, pos, t_in)
        (cout,), pos = take(refs, pos, [n_out])
        tout, pos = take(refs, pos, t_out)
        (cscr,), pos = take(refs, pos, [n_scr])
        tsem, pos = take(refs, pos, t_sem)
        if not grid:
            for t, a, b, c in zip(tasks, tin, tout, tsem):
                t.start(a, b, c)
            if body is not None:
                body(*cin, *cout, *cscr)
            for t, a, b, c in zip(tasks, tin, tout, tsem):
                if t.relay is not None:
                    t.relay(a, b, c)
            for t, a, b, c in zip(tasks, tin, tout, tsem):
                t.finish(a, b, c)
            return
        step = pl.program_id(0)
        for axis in range(1, len(grid)):
            step = step * grid[axis] + pl.program_id(axis)
        if tasks:
            @pl.when(step == 0)
            def _():
                for t, a, b, c in zip(tasks, tin, tout, tsem):
                    t.start(a, b, c)

        body(*cin, *cout, *cscr)
        for t, a, b, c in zip(tasks, tin, tout, tsem):
            if t.relay is not None:
                pl.when(step == max(steps - 1 - t.relay_steps, 0))(functools.partial(t.relay, a, b, c))

        if tasks:
            @pl.when(step == steps - 1)
            def _():
                for t, a, b, c in zip(tasks, tin, tout, tsem):
                    t.finish(a, b, c)

    aliases, in_pos, out_pos = {}, n_in, n_out
    for t, ni, no in zip(tasks, t_in, t_out):
        aliases.update({in_pos + a: out_pos + b for a, b in t.aliases.items()})
        in_pos, out_pos = in_pos + ni, out_pos + no
    any_spec = pl.BlockSpec(memory_space=pltpu.HBM)
    kwargs = dict(grid=grid, compiler_params=pltpu.CompilerParams(
        dimension_semantics=("arbitrary",) * len(grid), vmem_limit_bytes=VMEM_LIMIT)) if grid else dict(
        compiler_params=pltpu.CompilerParams(vmem_limit_bytes=VMEM_LIMIT))
    out = pl.pallas_call(
        wrapped, name=name,
        in_specs=list(in_specs) + [any_spec] * sum(t_in),
        out_specs=list(out_specs) + [any_spec] * sum(t_out),
        out_shape=list(out_shape) + [s for t in tasks for s in t.out_shapes],
        scratch_shapes=list(scratch_shapes) + [s for t in tasks for s in t.sems],
        input_output_aliases=aliases, **kwargs,
    )(*args, *[pltpu.with_memory_space_constraint(a, pltpu.HBM) for t in tasks for a in t.ins])
    task_outs, pos = take(list(out), n_out, t_out)
    return list(out[:n_out]), task_outs


def _fwd_in_pool(x, g_pre, w_in, pool_w, pool_scale, w_pool_out, ts, tasks=()):
    s = x.shape[0]

    def body(x_ref, g_ref, win_ref, pw_ref, ps_ref, wpo_ref,
             urx_ref, urg_ref, gp_ref, gr_ref, d_ref, yp_ref, h1_ref, z_scr, halo_scr):
        i = pl.program_id(0)

        @pl.when(i == 0)
        def _():
            halo_scr[...] = jnp.zeros_like(halo_scr)

        h1, _, _ = _rms_fwd(x_ref[...], g_ref[...])
        h1 = h1.astype(BF16)
        h1_ref[...] = h1
        for j in range(N_CHIPS):
            z_scr[:, j * IN_SHARD:(j + 1) * IN_SHARD] = jnp.dot(h1, win_ref[j], preferred_element_type=F32)
        urx_ref[...] = z_scr[:, 512:1536]
        urg_ref[...] = z_scr[:, 1536:2560]
        gp_ref[...] = z_scr[:, 2560:3584]
        gr_ref[...] = z_scr[:, 3584:4608]
        u = z_scr[:, 0:POOL_WIDTH]
        ext = jnp.concatenate([halo_scr[...], u], axis=0)
        halo_scr[...] = u[ts - POOL_HALO:, :]
        t = i * ts + lax.broadcasted_iota(jnp.int32, (ts, 1), 0)
        y4 = []
        for g, w in enumerate(POOL_WINDOWS):
            lanes = slice(g * POOL_GROUP_DIM, (g + 1) * POOL_GROUP_DIM)
            acc = ext[:, lanes]
            sh = 1
            while sh < w:
                acc = acc + pltpu.roll(acc, sh, 0)
                sh *= 2
            inv = 1.0 / jnp.minimum(t + 1, w).astype(F32)
            dg = acc[POOL_HALO:, :] * inv - u[:, lanes]
            d_ref[:, lanes] = dg
            y4.append(_dot(dg, pw_ref[g]))
        ypre = jnp.concatenate(y4, axis=1) * ps_ref[...]
        ypre = ypre.astype(BF16)
        for j in range(N_CHIPS):
            yp_ref[:, j * 256:(j + 1) * 256] = jnp.dot(ypre, wpo_ref[j], preferred_element_type=F32)

    return _call(
        body, "fwd_in_pool", (s // ts,),
        [_rows(ts, D_MODEL), _resident((1, D_MODEL)), _resident(w_in.shape), _resident(pool_w.shape),
         _resident((1, POOL_WIDTH)), _resident(w_pool_out.shape)],
        [_rows(ts, D_MODEL)] * 4 + [_rows(ts, POOL_WIDTH), _rows(ts, D_MODEL), _rows(ts, D_MODEL)],
        [_sds((s, D_MODEL))] * 4 + [_sds((s, POOL_WIDTH)), _sds((s, D_MODEL)), _sds((s, D_MODEL), BF16)],
        [pltpu.VMEM((ts, IN_TOTAL), F32), pltpu.VMEM((POOL_HALO, POOL_WIDTH), F32)],
        (x, g_pre, w_in, pool_w, pool_scale, w_pool_out), tasks)


def _fwd_rnn_merge(urx, urg, gp, gr, ypool, x, conv_w, conv_b, wg, bg, lam, w_rg_out, w_o, g_post, ts, tasks=()):
    s = x.shape[0]

    def body(urx_ref, urg_ref, gp_ref, gr_ref, yp_ref, x_ref, cw_ref, cb_ref, wg_ref, bg_ref, lam_ref, wrg_ref, wo_ref,
             gpost_ref, xc_ref, r_ref, ig_ref, h_ref, yr_ref, mo_ref, x1_ref, gl_ref, gg_ref, sp_ref, sr_ref,
             halo_scr, carry_scr):
        i = pl.program_id(0)

        @pl.when(i == 0)
        def _():
            halo_scr[...] = jnp.zeros_like(halo_scr)
            carry_scr[...] = jnp.zeros_like(carry_scr)

        urx_v = urx_ref[...]
        ext = jnp.concatenate([halo_scr[...], urx_v], axis=0)
        halo_scr[...] = urx_v[ts - CONV_HALO:, :]
        cw = cw_ref[...]
        xc = (cb_ref[...] + cw[3:4] * urx_v + cw[2:3] * _shift_down(ext, 1, CONV_HALO)
              + cw[1:2] * _shift_down(ext, 2, CONV_HALO) + cw[0:1] * _shift_down(ext, 3, CONV_HALO))
        xc_ref[...] = xc
        xcb = xc.astype(BF16)
        lin = []
        for gate in range(2):
            parts = [jnp.dot(xcb[:, q * GATE_BLOCK:(q + 1) * GATE_BLOCK], wg_ref[gate, q], preferred_element_type=F32)
                     for q in range(GATE_BLOCKS)]
            lin.append(jnp.concatenate(parts, axis=1) + bg_ref[gate:gate + 1, :])
        r = _sigmoid(lin[0])
        ig = _sigmoid(lin[1])
        r_ref[...] = r
        ig_ref[...] = ig
        first_row = (i * ts + lax.broadcasted_iota(jnp.int32, (ts, 1), 0)) == 0
        _, a, _, mult = _lru_coeffs(r, lam_ref[...], first_row)
        h = _scan_fwd(a, mult * ig * xc, carry_scr[0:1, :])
        carry_scr[0:1, :] = h[ts - 1:ts, :]
        h_ref[...] = h
        urg_v = urg_ref[...]
        gl, t = _gelu(urg_v)
        gl_ref[...] = gl.astype(BF16)
        gg_ref[...] = _gelu_grad(urg_v, t).astype(BF16)
        yr = _dot(h * gl, wrg_ref[...])
        yr_ref[...] = yr
        sp = _sigmoid(gp_ref[...])
        sr = _sigmoid(gr_ref[...])
        sp_ref[...] = sp.astype(BF16)
        sr_ref[...] = sr.astype(BF16)
        merged = sp * yp_ref[...] + sr * yr
        mo = _dot(merged, wo_ref[...])
        mo_ref[...] = mo
        y, _, _ = _rms_fwd(mo, gpost_ref[...])
        x1_ref[...] = x_ref[...] + y

    row = _rows(ts, D_MODEL)
    return _call(
        body, "fwd_rnn_merge", (s // ts,),
        [row] * 6 + [_resident(conv_w.shape), _resident((1, D_MODEL)), _resident(wg.shape), _resident(bg.shape),
                     _resident((1, D_MODEL)), _resident(w_rg_out.shape), _resident(w_o.shape), _resident((1, D_MODEL))],
        [row] * 11, [_sds((s, D_MODEL))] * 7 + [_sds((s, D_MODEL), BF16)] * 4,
        [pltpu.VMEM((CONV_HALO, D_MODEL), F32), pltpu.VMEM((8, D_MODEL), F32)],
        (urx, urg, gp, gr, ypool, x, conv_w, conv_b, wg, bg, lam, w_rg_out, w_o, g_post), tasks)


def _fwd_ffn(x1, g_pre, w_up, fcw, fcb, w_down, g_post, ts, tasks=()):
    s = x1.shape[0]

    def body(x1_ref, g_ref, wup_ref, fcw_ref, fcb_ref, wd_ref, gpost_ref,
             up_ref, gl_ref, gg_ref, h2_ref, dn_ref, x2_ref, up_scr, halo_scr):
        i = pl.program_id(0)

        @pl.when(i == 0)
        def _():
            halo_scr[...] = jnp.zeros_like(halo_scr)

        x1_v = x1_ref[...]
        h2, _, _ = _rms_fwd(x1_v, g_ref[...])
        h2 = h2.astype(BF16)
        h2_ref[...] = h2
        for j in range(N_CHIPS):
            up_scr[:, j * UP_SHARD:(j + 1) * UP_SHARD] = jnp.dot(h2, wup_ref[j], preferred_element_type=F32)
        up_ref[...] = up_scr[...].astype(BF16)
        ug = up_scr[:, 0:D_FF]
        ext = jnp.concatenate([halo_scr[...], ug], axis=0)
        halo_scr[...] = ug[ts - CONV_HALO:, :]
        w = fcw_ref[...]
        gh = (fcb_ref[...] + w[2:3] * ug + w[1:2] * _shift_down(ext, 1, CONV_HALO)
              + w[0:1] * _shift_down(ext, 2, CONV_HALO))
        gl, t = _gelu(gh)
        gl_ref[...] = gl.astype(BF16)
        gg_ref[...] = _gelu_grad(gh, t).astype(BF16)
        dn = _dot(gl * up_scr[:, D_FF:], wd_ref[...])
        dn_ref[...] = dn
        y, _, _ = _rms_fwd(dn, gpost_ref[...])
        x2_ref[...] = x1_v + y

    row = _rows(ts, D_MODEL)
    return _call(
        body, "fwd_ffn", (s // ts,),
        [row, _resident((1, D_MODEL)), _resident(w_up.shape), _resident(fcw.shape), _resident((1, D_FF)),
         _resident(w_down.shape), _resident((1, D_MODEL))],
        [_rows(ts, 2 * D_FF), _rows(ts, D_FF), _rows(ts, D_FF), row, row, row],
        [_sds((s, 2 * D_FF), BF16), _sds((s, D_FF), BF16), _sds((s, D_FF), BF16), _sds((s, D_MODEL), BF16),
         _sds((s, D_MODEL)), _sds((s, D_MODEL))],
        [pltpu.VMEM((ts, 2 * D_FF), F32), pltpu.VMEM((CONV_HALO, D_FF), F32)],
        (x1, g_pre, w_up, fcw, fcb, w_down, g_post), tasks)


def _ple_loss(x2, p, tgt, g_gate, w_gate, w_proj, g_post, ts):
    s = x2.shape[0]

    def body(x2_ref, p_ref, t_ref, gg_ref, wg_ref, wp_ref, gp_ref, dx2_ref, loss_ref, dwg_ref, dwp_ref, dgg_ref, dgp_ref):
        @pl.when(pl.program_id(0) == 0)
        def _():
            loss_ref[...] = jnp.zeros_like(loss_ref)
            dwg_ref[...] = jnp.zeros_like(dwg_ref)
            dwp_ref[...] = jnp.zeros_like(dwp_ref)
            dgg_ref[...] = jnp.zeros_like(dgg_ref)
            dgp_ref[...] = jnp.zeros_like(dgp_ref)

        x2_v = x2_ref[...]
        n3, xh3, r3 = _rms_fwd(x2_v, gg_ref[...])
        pg = _sigmoid(_dot(n3, wg_ref[...]))
        pb = p_ref[...].astype(BF16)
        q = jnp.concatenate([jnp.dot(pb, wp_ref[j], preferred_element_type=F32) for j in range(N_CHIPS)], axis=1)
        ple, qh, rq = _rms_fwd(q, gp_ref[...])
        e = x2_v + pg * ple - t_ref[...]
        loss_ref[...] += 0.5 * jnp.sum(jnp.mean(e * e, axis=-1, keepdims=True), axis=0, keepdims=True)
        dy = e * (1.0 / D_MODEL)
        dpgl = dy * ple * pg * (1.0 - pg)
        dwg_ref[...] += _dot_tn(n3, dpgl)
        dx3, dgg = _rms_bwd(xh3, r3, gg_ref[...], _dot_nt(dpgl, wg_ref[...]))
        dgg_ref[...] += dgg
        dq, dgp = _rms_bwd(qh, rq, gp_ref[...], dy * pg)
        dgp_ref[...] += dgp
        for j in range(N_CHIPS):
            dwp_ref[j] += _dot_tn(pb, dq[:, j * 256:(j + 1) * 256])
        dx2_ref[...] = dy + dx3

    row = _rows(ts, D_MODEL)
    vec = _acc((1, D_MODEL))
    return pl.pallas_call(
        body, name="ple_loss", grid=(s // ts,),
        in_specs=[row, _rows(ts, PLE_DIM), row, _resident((1, D_MODEL)), _resident(w_gate.shape), _resident(w_proj.shape),
                  _resident((1, D_MODEL))],
        out_specs=[row, _acc((1, 128)), _acc(w_gate.shape), _acc(w_proj.shape), vec, vec],
        out_shape=[_sds((s, D_MODEL)), _sds((1, 128)), _sds(w_gate.shape), _sds(w_proj.shape), _sds((1, D_MODEL)),
                   _sds((1, D_MODEL))],
        compiler_params=_params(),
    )(x2, p, tgt, g_gate, w_gate, w_proj, g_post)


def _bwd_ffn_down(dx2, dn, up, gl, gg, fcw, w_down, g_post, ts):
    s = dx2.shape[0]
    nt = s // ts

    def body(dx2_ref, dn_ref, up_ref, gl_ref, gg_ref, fcw_ref, wd_ref, gpost_ref,
             dup_ref, dwd_ref, dfcw_ref, dfcb_ref, dgp_ref, carry_scr):
        i = pl.program_id(0)

        @pl.when(i == 0)
        def _():
            carry_scr[...] = jnp.zeros_like(carry_scr)
            dwd_ref[...] = jnp.zeros_like(dwd_ref)
            dfcw_ref[...] = jnp.zeros_like(dfcw_ref)
            dfcb_ref[...] = jnp.zeros_like(dfcb_ref)
            dgp_ref[...] = jnp.zeros_like(dgp_ref)

        _, xh, r = _rms_fwd(dn_ref[...], gpost_ref[...])
        ddn, dgp = _rms_bwd(xh, r, gpost_ref[...], dx2_ref[...])
        dgp_ref[...] += dgp
        dhid = _dot_nt(ddn, wd_ref[...])
        ug = up_ref[:, 0:D_FF].astype(F32)
        uv = up_ref[:, D_FF:].astype(F32)
        gl = gl_ref[...].astype(F32)
        w = fcw_ref[...]
        dwd_ref[...] += _dot_tn(gl * uv, ddn)
        dgh = dhid * uv * gg_ref[...].astype(F32)
        dup_ref[:, D_FF:] = (dhid * gl).astype(BF16)
        extd = jnp.concatenate([dgh, carry_scr[...]], axis=0)
        carry_scr[...] = dgh[0:CONV_HALO, :]
        d1 = _shift_up(extd, 1, ts)
        d2 = _shift_up(extd, 2, ts)
        dup_ref[:, 0:D_FF] = (w[2:3] * dgh + w[1:2] * d1 + w[0:1] * d2).astype(BF16)
        dfcw_ref[2:3, :] += jnp.sum(ug * dgh, axis=0, keepdims=True)
        dfcw_ref[1:2, :] += jnp.sum(ug * d1, axis=0, keepdims=True)
        dfcw_ref[0:1, :] += jnp.sum(ug * d2, axis=0, keepdims=True)
        dfcb_ref[...] += jnp.sum(dgh, axis=0, keepdims=True)

    row = _rows(ts, D_MODEL, nt)
    wide = _rows(ts, D_FF, nt)
    return pl.pallas_call(
        body, name="bwd_ffn_down", grid=(nt,),
        in_specs=[row, row, _rows(ts, 2 * D_FF, nt), wide, wide, _resident(fcw.shape), _resident(w_down.shape),
                  _resident((1, D_MODEL))],
        out_specs=[_rows(ts, 2 * D_FF, nt), _acc(w_down.shape), _acc(fcw.shape), _acc((1, D_FF)), _acc((1, D_MODEL))],
        out_shape=[_sds((s, 2 * D_FF), BF16), _sds(w_down.shape), _sds(fcw.shape), _sds((1, D_FF)), _sds((1, D_MODEL))],
        scratch_shapes=[pltpu.VMEM((CONV_HALO, D_FF), F32)],
        compiler_params=_params(),
    )(dx2, dn, up, gl, gg, fcw, w_down, g_post)


def _bwd_ffn_up(dup, x1, dx2, g_pre, w_up, ts, tasks=()):
    s = x1.shape[0]

    def body(dup_ref, x1_ref, dx2_ref, g_ref, wup_ref, dx1_ref, dg_ref):
        @pl.when(pl.program_id(0) == 0)
        def _():
            dg_ref[...] = jnp.zeros_like(dg_ref)

        _, xh, r = _rms_fwd(x1_ref[...], g_ref[...])
        dh2 = _dot_nt(dup_ref[:, 0:UP_SHARD], wup_ref[0])
        for j in range(1, N_CHIPS):
            dh2 = dh2 + _dot_nt(dup_ref[:, j * UP_SHARD:(j + 1) * UP_SHARD], wup_ref[j])
        dx, dg = _rms_bwd(xh, r, g_ref[...], dh2)
        dg_ref[...] += dg
        dx1_ref[...] = dx2_ref[...] + dx

    row = _rows(ts, D_MODEL)
    return _call(
        body, "bwd_ffn_up", (s // ts,),
        [_rows(ts, 2 * D_FF), row, row, _resident((1, D_MODEL)), _resident(w_up.shape)],
        [row, _acc((1, D_MODEL))], [_sds((s, D_MODEL)), _sds((1, D_MODEL))], [],
        (dup, x1, dx2, g_pre, w_up), tasks)


def _dw_up(h2, dup, ts, tasks=()):
    s = h2.shape[0]
    ts = min(DW_TILES * ts, s)

    def body(h2_ref, dup_ref, out_ref):
        @pl.when(pl.program_id(1) == 0)
        def _():
            out_ref[...] = jnp.zeros_like(out_ref)

        out_ref[0] += _dot_tn(h2_ref[...], dup_ref[...])

    return _call(
        body, "dw_up", (N_CHIPS, s // ts),
        [pl.BlockSpec((ts, D_MODEL), lambda j, i: (i, 0)), pl.BlockSpec((ts, UP_SHARD), lambda j, i: (i, j))],
        [pl.BlockSpec((1, D_MODEL, UP_SHARD), lambda j, i: (j, 0, 0))], [_sds((N_CHIPS, D_MODEL, UP_SHARD))], [],
        (h2, dup), tasks)


def _bwd_merge(dx1, mo, sp, sr, ypool, yrnn, g_post, w_o, ts, tasks=()):
    s = dx1.shape[0]

    def body(dx1_ref, mo_ref, sp_ref, sr_ref, yp_ref, yr_ref, g_ref, wo_ref,
             dgp_ref, dgr_ref, dyp_ref, dyr_ref, dwo_ref, dg_ref):
        @pl.when(pl.program_id(0) == 0)
        def _():
            dwo_ref[...] = jnp.zeros_like(dwo_ref)
            dg_ref[...] = jnp.zeros_like(dg_ref)

        _, xh, r = _rms_fwd(mo_ref[...], g_ref[...])
        dmo, dg = _rms_bwd(xh, r, g_ref[...], dx1_ref[...])
        dg_ref[...] += dg
        dmerged = _dot_nt(dmo, wo_ref[...])
        sp = sp_ref[...].astype(F32)
        sr = sr_ref[...].astype(F32)
        yp = yp_ref[...]
        yr = yr_ref[...]
        dwo_ref[...] += _dot_tn(sp * yp + sr * yr, dmo)
        dgp_ref[...] = (dmerged * yp * sp * (1.0 - sp)).astype(BF16)
        dgr_ref[...] = (dmerged * yr * sr * (1.0 - sr)).astype(BF16)
        dyp_ref[...] = (dmerged * sp).astype(BF16)
        dyr_ref[...] = (dmerged * sr).astype(BF16)

    row = _rows(ts, D_MODEL)
    return _call(
        body, "bwd_merge", (s // ts,),
        [row] * 6 + [_resident((1, D_MODEL)), _resident(w_o.shape)],
        [row] * 4 + [_acc(w_o.shape), _acc((1, D_MODEL))],
        [_sds((s, D_MODEL), BF16)] * 4 + [_sds(w_o.shape), _sds((1, D_MODEL))], [],
        (dx1, mo, sp, sr, ypool, yrnn, g_post, w_o), tasks)


def _bwd_rnn(dyr, urx, gl, gg, xc, r, ig, h, conv_w, wg, lam, w_rg_out, ts, tasks=()):
    s = urx.shape[0]
    nt = s // ts
    halo_blocks = ts // CONV_HALO

    def body(dyr_ref, urx_ref, gl_ref, gg_ref, xc_ref, r_ref, ig_ref, h_ref, hh_ref, cw_ref, wg_ref, lam_ref, wrg_ref,
             durx_ref, durg_ref, dwrg_ref, dwg_ref, dcw_ref, dcb_ref, dbg_ref, dlam_ref, mu_scr, carry_scr):
        i = pl.program_id(0)
        k = nt - 1 - i

        @pl.when(i == 0)
        def _():
            mu_scr[...] = jnp.zeros_like(mu_scr)
            carry_scr[...] = jnp.zeros_like(carry_scr)
            dwrg_ref[...] = jnp.zeros_like(dwrg_ref)
            dwg_ref[...] = jnp.zeros_like(dwg_ref)
            dcw_ref[...] = jnp.zeros_like(dcw_ref)
            dcb_ref[...] = jnp.zeros_like(dcb_ref)
            dbg_ref[...] = jnp.zeros_like(dbg_ref)
            dlam_ref[...] = jnp.zeros_like(dlam_ref)

        row = lax.broadcasted_iota(jnp.int32, (ts, 1), 0)
        first_row = (k * ts + row) == 0
        h = h_ref[...]
        dyr_v = dyr_ref[...]
        dhr = _dot_nt(dyr_v, wrg_ref[...])
        gl = gl_ref[...].astype(F32)
        dwrg_ref[...] += _dot_tn(h * gl, dyr_v)
        durg_ref[...] = (dhr * h * gg_ref[...].astype(F32)).astype(BF16)
        r_v = r_ref[...]
        ig_v = ig_ref[...]
        xc_v = xc_ref[...]
        lam_v = lam_ref[...]
        c8, a, m2, mult = _lru_coeffs(r_v, lam_v, first_row)
        b = jnp.where(row == ts - 1, 1.0, pltpu.roll(a, ts - 1, 0))
        lt = _scan_bwd(b, dhr * gl, mu_scr[0:1, :])
        mu_scr[0:1, :] = a[0:1, :] * lt[0:1, :]
        h_before = jnp.where(k > 0, hh_ref[CONV_HALO - 1:CONV_HALO, :], 0.0)
        hprev = jnp.where(row == 0, h_before, pltpu.roll(h, 1, 0))
        dmult = lt * ig_v * xc_v
        da = lt * hprev - jnp.where(first_row, 0.0, dmult * a * lax.rsqrt(m2))
        dla = da * a
        dlam_ref[...] += jnp.sum(dla * r_v, axis=0, keepdims=True)
        dlr = (dla * (-c8)) * r_v * (1.0 - r_v)
        dli = (lt * mult * xc_v) * ig_v * (1.0 - ig_v)
        dbg_ref[0:1, :] += jnp.sum(dlr, axis=0, keepdims=True)
        dbg_ref[1:2, :] += jnp.sum(dli, axis=0, keepdims=True)
        xcb = xc_v.astype(BF16)
        parts = []
        for q in range(GATE_BLOCKS):
            blk = slice(q * GATE_BLOCK, (q + 1) * GATE_BLOCK)
            dlr_q = dlr[:, blk].astype(BF16)
            dli_q = dli[:, blk].astype(BF16)
            parts.append(_dot_nt(dlr_q, wg_ref[0, q]) + _dot_nt(dli_q, wg_ref[1, q]))
            dwg_ref[0, q] += _dot_tn(xcb[:, blk], dlr_q)
            dwg_ref[1, q] += _dot_tn(xcb[:, blk], dli_q)
        dxc = lt * mult * ig_v + jnp.concatenate(parts, axis=1)
        extd = jnp.concatenate([dxc, carry_scr[...]], axis=0)
        carry_scr[...] = dxc[0:CONV_HALO, :]
        cw = cw_ref[...]
        urx_v = urx_ref[...]
        durx = cw[3:4] * dxc
        dcw_ref[3:4, :] += jnp.sum(urx_v * dxc, axis=0, keepdims=True)
        for j in (1, 2, 3):
            dj = _shift_up(extd, j, ts)
            durx = durx + cw[3 - j:4 - j] * dj
            dcw_ref[3 - j:4 - j, :] += jnp.sum(urx_v * dj, axis=0, keepdims=True)
        durx_ref[...] = durx.astype(BF16)
        dcb_ref[...] += jnp.sum(dxc, axis=0, keepdims=True)

        @pl.when(i == nt - 1)
        def _():
            dlam_ref[...] = dlam_ref[...] * (LRU_C * jax.nn.sigmoid(-lam_v))

    row_spec = _rows(ts, D_MODEL, nt)
    halo_spec = pl.BlockSpec((CONV_HALO, D_MODEL), lambda i: (jnp.maximum((nt - 1 - i) * halo_blocks - 1, 0), 0))
    vec = _acc((1, D_MODEL))
    return _call(
        body, "bwd_rnn", (nt,),
        [row_spec] * 8 + [halo_spec, _resident(conv_w.shape), _resident(wg.shape), _resident((1, D_MODEL)),
                          _resident(w_rg_out.shape)],
        [row_spec, row_spec, _acc(w_rg_out.shape), _acc(wg.shape), _acc(conv_w.shape), vec, _acc((2, D_MODEL)), vec],
        [_sds((s, D_MODEL), BF16), _sds((s, D_MODEL), BF16), _sds(w_rg_out.shape), _sds(wg.shape), _sds(conv_w.shape),
         _sds((1, D_MODEL)), _sds((2, D_MODEL)), _sds((1, D_MODEL))],
        [pltpu.VMEM((8, D_MODEL), F32), pltpu.VMEM((CONV_HALO, D_MODEL), F32)],
        (dyr, urx, gl, gg, xc, r, ig, h, h, conv_w, wg, lam, w_rg_out), tasks)


def _bwd_pool(dyp, d, pool_w, pool_scale, w_pool_out, ts, tasks=()):
    s = d.shape[0]
    nt = s // ts

    def body(dyp_ref, d_ref, pw_ref, ps_ref, wpo_ref, dzp_ref, dwpo_ref, dpw_ref, dps_ref, carry_scr):
        i = pl.program_id(0)
        k = nt - 1 - i

        @pl.when(i == 0)
        def _():
            carry_scr[...] = jnp.zeros_like(carry_scr)
            dwpo_ref[...] = jnp.zeros_like(dwpo_ref)
            dpw_ref[...] = jnp.zeros_like(dpw_ref)
            dps_ref[...] = jnp.zeros_like(dps_ref)

        dyp_v = dyp_ref[...]
        d_v = d_ref[...]
        ps = ps_ref[...]
        dypre = _dot_nt(dyp_v[:, 0:256], wpo_ref[0])
        for j in range(1, N_CHIPS):
            dypre = dypre + _dot_nt(dyp_v[:, j * 256:(j + 1) * 256], wpo_ref[j])
        y4 = jnp.concatenate([_dot(d_v[:, g * 128:(g + 1) * 128], pw_ref[g]) for g in range(POOL_GROUPS)], axis=1)
        ypre = (y4 * ps).astype(BF16)
        for j in range(N_CHIPS):
            dwpo_ref[j] += _dot_tn(ypre, dyp_v[:, j * 256:(j + 1) * 256])
        dps_ref[...] += jnp.sum(dypre * y4, axis=0, keepdims=True)
        dy4 = dypre * ps
        t = k * ts + lax.broadcasted_iota(jnp.int32, (ts, 1), 0)
        for g, w in enumerate(POOL_WINDOWS):
            lanes = slice(g * POOL_GROUP_DIM, (g + 1) * POOL_GROUP_DIM)
            dd = _dot_nt(dy4[:, lanes], pw_ref[g])
            dpw_ref[g] += _dot_tn(d_v[:, lanes], dy4[:, lanes])
            e = dd * (1.0 / jnp.minimum(t + 1, w).astype(F32))
            acc = jnp.concatenate([e, carry_scr[:, lanes]], axis=0)
            carry_scr[:, lanes] = e[0:POOL_HALO, :]
            n = ts + POOL_HALO
            sh = 1
            while sh < w:
                acc = acc + pltpu.roll(acc, n - sh, 0)
                sh *= 2
            dzp_ref[:, lanes] = (acc[:ts, :] - dd).astype(BF16)

    return _call(
        body, "bwd_pool", (nt,),
        [_rows(ts, D_MODEL, nt), _rows(ts, POOL_WIDTH, nt), _resident(pool_w.shape), _resident((1, POOL_WIDTH)),
         _resident(w_pool_out.shape)],
        [_rows(ts, POOL_WIDTH, nt), _acc(w_pool_out.shape), _acc(pool_w.shape), _acc((1, POOL_WIDTH))],
        [_sds((s, POOL_WIDTH), BF16), _sds(w_pool_out.shape), _sds(pool_w.shape), _sds((1, POOL_WIDTH))],
        [pltpu.VMEM((POOL_HALO, POOL_WIDTH), F32)],
        (dyp, d, pool_w, pool_scale, w_pool_out), tasks)


def _assemble_dz(dz_scr, dzp_ref, durx_ref, durg_ref, dgp_ref, dgr_ref):
    dz_scr[:, 0:512] = dzp_ref[...]
    dz_scr[:, 512:1536] = durx_ref[...]
    dz_scr[:, 1536:2560] = durg_ref[...]
    dz_scr[:, 2560:3584] = dgp_ref[...]
    dz_scr[:, 3584:4608] = dgr_ref[...]


def _dw_in(h1, dzp, durx, durg, dgp, dgr, ts, tasks=()):
    s = h1.shape[0]
    ts = min(2 * ts, s)

    def body(h1_ref, dzp_ref, durx_ref, durg_ref, dgp_ref, dgr_ref, out_ref, dz_scr):
        @pl.when(pl.program_id(0) == 0)
        def _():
            out_ref[...] = jnp.zeros_like(out_ref)

        _assemble_dz(dz_scr, dzp_ref, durx_ref, durg_ref, dgp_ref, dgr_ref)
        for j in range(N_CHIPS):
            out_ref[j] += _dot_tn(h1_ref[...], dz_scr[:, j * IN_SHARD:(j + 1) * IN_SHARD])

    row = _rows(ts, D_MODEL)
    shape = (N_CHIPS, D_MODEL, IN_SHARD)
    return _call(
        body, "dw_in", (s // ts,), [row, _rows(ts, POOL_WIDTH)] + [row] * 4, [_acc(shape)], [_sds(shape)],
        [pltpu.VMEM((ts, IN_TOTAL), BF16)], (h1, dzp, durx, durg, dgp, dgr), tasks)


def _bwd_in(dzp, durx, durg, dgp, dgr, x, dx1, g_pre, w_in, ts, tasks=()):
    s = x.shape[0]

    def body(dzp_ref, durx_ref, durg_ref, dgp_ref, dgr_ref, x_ref, dx1_ref, g_ref, win_ref, gx_ref, dg_ref, dz_scr):
        @pl.when(pl.program_id(0) == 0)
        def _():
            dg_ref[...] = jnp.zeros_like(dg_ref)

        _assemble_dz(dz_scr, dzp_ref, durx_ref, durg_ref, dgp_ref, dgr_ref)
        _, xh, r = _rms_fwd(x_ref[...], g_ref[...])
        dh1 = _dot_nt(dz_scr[:, 0:IN_SHARD], win_ref[0])
        for j in range(1, N_CHIPS):
            dh1 = dh1 + _dot_nt(dz_scr[:, j * IN_SHARD:(j + 1) * IN_SHARD], win_ref[j])
        dx, dg = _rms_bwd(xh, r, g_ref[...], dh1)
        dg_ref[...] += dg
        gx_ref[...] = dx1_ref[...] + dx

    row = _rows(ts, D_MODEL)
    return _call(
        body, "bwd_in", (s // ts,),
        [_rows(ts, POOL_WIDTH)] + [row] * 6 + [_resident((1, D_MODEL)), _resident(w_in.shape)],
        [row, _acc((1, D_MODEL))], [_sds((s, D_MODEL)), _sds((1, D_MODEL))],
        [pltpu.VMEM((ts, IN_TOTAL), BF16)], (dzp, durx, durg, dgp, dgr, x, dx1, g_pre, w_in), tasks)


def _place():
    x, y, c = lax.axis_index("x"), lax.axis_index("y"), lax.axis_index("c")
    others = [(1 - x, y), (x, 1 - y), (1 - x, 1 - y)]
    return x, y, c, 2 * x + y, others


def _remote(src, dst, send_sem, recv_sem, to):
    return pltpu.make_async_remote_copy(src_ref=src, dst_ref=dst, send_sem=send_sem, recv_sem=recv_sem,
                                        device_id=to, device_id_type=MESH)


def _own_slots(ws, dtypes, name, tasks=()):
    n = len(ws)
    hbm = pl.BlockSpec(memory_space=pltpu.HBM)

    def body(*refs):
        srcs, outs, f32_bufs, cast_bufs, sems = refs[:n], refs[n:2 * n], refs[2 * n:3 * n], refs[3 * n:4 * n], refs[4 * n]
        me = _place()[3]
        loads = [pltpu.make_async_copy(srcs[k], f32_bufs[k], sems.at[k, 0]) for k in range(n)]
        stores = [pltpu.make_async_copy(cast_bufs[k], outs[k].at[me], sems.at[k, 1]) for k in range(n)]
        for cp in loads:
            cp.start()
        for k in range(n):
            loads[k].wait()
            cast_bufs[k][...] = f32_bufs[k][...].astype(dtypes[k])
            stores[k].start()
        for cp in stores:
            cp.wait()

    return _call(
        body, name, (), [hbm] * n, [hbm] * n, [_sds((N_CHIPS,) + w.shape, dt) for w, dt in zip(ws, dtypes)],
        [pltpu.VMEM(w.shape, F32) for w in ws] + [pltpu.VMEM(w.shape, dt) for w, dt in zip(ws, dtypes)]
        + [pltpu.SemaphoreType.DMA((n, 2))],
        [pltpu.with_memory_space_constraint(w, pltpu.HBM) for w in ws], tasks)


def _run(tasks, name):
    if isinstance(tasks, _Task):
        return _call(None, name, (), [], [], [], [], (), (tasks,))[1][0]
    return _call(None, name, (), [], [], [], [], (), tuple(tasks))[1]


def _gather_task(bufs, relay_steps=0):
    n = len(bufs)

    def first_copy(out, w, j, ox, oy, sems):
        x, y, c, me, _ = _place()
        ah = out.shape[1] // 2
        mine = out.at[me, pl.ds(c * ah, ah)]
        return _remote(mine, mine, sems[0].at[w, j], sems[1].at[w, j], (ox, oy, c))

    def passed_copy(out, w, j, ox, oy, sems, half):
        x, y, c, _, _ = _place()
        ah = out.shape[1] // 2
        slot = out.at[2 * ox + oy, pl.ds(half * ah, ah)]
        return _remote(slot, slot, sems[0].at[w, 3 + j], sems[1].at[w, 3 + j], (x, y, 1 - c))

    def start(ins, outs, sems):
        others = _place()[4]
        for w, out in enumerate(outs):
            for j, (ox, oy) in enumerate(others):
                first_copy(out, w, j, ox, oy, sems).start()

    def relay(ins, outs, sems):
        x, y, c, _, others = _place()
        for w, out in enumerate(outs):
            ah = out.shape[1] // 2
            for j, (ox, oy) in enumerate(others):
                slot = out.at[2 * ox + oy, pl.ds(c * ah, ah)]
                _remote(slot, slot, sems[0].at[w, j], sems[1].at[w, j], (ox, oy, c)).wait_recv()
                passed_copy(out, w, j, ox, oy, sems, c).start()

    def finish(ins, outs, sems):
        x, y, c, _, others = _place()
        for w, out in enumerate(outs):
            for j, (ox, oy) in enumerate(others):
                passed_copy(out, w, j, ox, oy, sems, 1 - c).wait_recv()
        for w, out in enumerate(outs):
            for j, (ox, oy) in enumerate(others):
                first_copy(out, w, j, ox, oy, sems).wait_send()
                passed_copy(out, w, j, ox, oy, sems, c).wait_send()

    return _Task(bufs, [_sds(b.shape, b.dtype) for b in bufs], {i: i for i in range(n)},
                 [pltpu.SemaphoreType.DMA((n, 6)), pltpu.SemaphoreType.DMA((n, 6))], start, finish, relay, relay_steps)


def _halves_task(grads):
    n = len(grads)

    def copy(src, out, w, sems):
        x, y, c, _, _ = _place()
        ah = out.shape[1]
        return _remote(src.at[:, pl.ds((1 - c) * ah, ah)], out, sems[0].at[w], sems[1].at[w], (x, y, 1 - c))

    def start(ins, outs, sems):
        for w, (src, out) in enumerate(zip(ins, outs)):
            copy(src, out, w, sems).start()

    def finish(ins, outs, sems):
        for w, (src, out) in enumerate(zip(ins, outs)):
            copy(src, out, w, sems).wait()

    return _Task(grads, [_sds((g.shape[0], g.shape[1] // 2, g.shape[2]), g.dtype) for g in grads], {},
                 [pltpu.SemaphoreType.DMA((n,)), pltpu.SemaphoreType.DMA((n,))], start, finish)


def _exchange_task(sends, accs):
    n = len(accs)
    given = [s for s in sends if s is not None]

    def copies(ins, outs, sems):
        send_refs = iter(ins[:len(given)])
        srcs = [next(send_refs) if s is not None else None for s in sends]
        x, y, c, me, others = _place()
        for w, out in enumerate(outs):
            for j, (ox, oy) in enumerate(others):
                src = out.at[me] if srcs[w] is None else srcs[w].at[2 * ox + oy]
                yield _remote(src, out.at[me], sems[0].at[w, j], sems[1].at[w, j], (ox, oy, c))

    def start(ins, outs, sems):
        for cp in copies(ins, outs, sems):
            cp.start()

    def finish(ins, outs, sems):
        x, y, c, _, others = _place()
        for w, out in enumerate(outs):
            for j, (ox, oy) in enumerate(others):
                slot = out.at[2 * ox + oy]
                _remote(slot, slot, sems[0].at[w, j], sems[1].at[w, j], (ox, oy, c)).wait_recv()
        for cp in copies(ins, outs, sems):
            cp.wait_send()

    return _Task(given + list(accs), [_sds(a.shape, a.dtype) for a in accs], {len(given) + i: i for i in range(n)},
                 [pltpu.SemaphoreType.DMA((n, 3)), pltpu.SemaphoreType.DMA((n, 3))], start, finish)


def _swap_task(arrays):
    n = len(arrays)

    def copy(src, out, w, sems):
        x, y, c, _, _ = _place()
        return _remote(src, out, sems[0].at[w], sems[1].at[w], (x, y, 1 - c))

    def start(ins, outs, sems):
        for w, (src, out) in enumerate(zip(ins, outs)):
            copy(src, out, w, sems).start()

    def finish(ins, outs, sems):
        for w, (src, out) in enumerate(zip(ins, outs)):
            copy(src, out, w, sems).wait()

    return _Task(arrays, [_sds(a.shape, a.dtype) for a in arrays], {},
                 [pltpu.SemaphoreType.DMA((n,)), pltpu.SemaphoreType.DMA((n,))], start, finish)


def _all_devices_task(arrays):
    n = len(arrays)
    flips = [(dx, dy, dc) for dx in (0, 1) for dy in (0, 1) for dc in (0, 1)][1:]

    def peers():
        x, y, c, _, _ = _place()
        flip = lambda v, d: 1 - v if d else v
        return 4 * x + 2 * y + c, [(flip(x, dx), flip(y, dy), flip(c, dc)) for dx, dy, dc in flips]

    def start(ins, outs, sems):
        me, others = peers()
        for w, (src, out) in enumerate(zip(ins, outs)):
            pltpu.make_async_copy(src, out.at[me], sems[2].at[w]).start()
            for k, peer in enumerate(others):
                _remote(src, out.at[me], sems[0].at[w, k], sems[1].at[w, k], peer).start()

    def finish(ins, outs, sems):
        me, others = peers()
        for w, (src, out) in enumerate(zip(ins, outs)):
            for k, (px, py, pc) in enumerate(others):
                slot = out.at[4 * px + 2 * py + pc]
                _remote(slot, slot, sems[0].at[w, k], sems[1].at[w, k], (px, py, pc)).wait_recv()
            for k, peer in enumerate(others):
                _remote(src, out.at[me], sems[0].at[w, k], sems[1].at[w, k], peer).wait_send()
            pltpu.make_async_copy(src, out.at[me], sems[2].at[w]).wait()

    return _Task(arrays, [_sds((8,) + a.shape, a.dtype) for a in arrays], {},
                 [pltpu.SemaphoreType.DMA((n, 7)), pltpu.SemaphoreType.DMA((n, 7)), pltpu.SemaphoreType.DMA((n,))],
                 start, finish)


def _share_task(shares):
    n = len(shares)

    def copy(out, w, sems, slot):
        x, y, c, _, _ = _place()
        return _remote(out.at[slot], out.at[slot], sems[0].at[w], sems[1].at[w], (x, y, 1 - c))

    def start(ins, outs, sems):
        c = _place()[2]
        for w, out in enumerate(outs):
            copy(out, w, sems, c).start()

    def finish(ins, outs, sems):
        c = _place()[2]
        for w, out in enumerate(outs):
            copy(out, w, sems, 1 - c).wait_recv()
        for w, out in enumerate(outs):
            copy(out, w, sems, c).wait_send()

    return _Task(shares, [_sds(s.shape, s.dtype) for s in shares], {i: i for i in range(n)},
                 [pltpu.SemaphoreType.DMA((n,)), pltpu.SemaphoreType.DMA((n,))], start, finish)


TILE_BYTES = 2 * 1024 * 1024
PARTIAL_TILE_BYTES = 512 * 1024


def _in_hbm(t):
    return pltpu.with_memory_space_constraint(t, pltpu.HBM)


def _row_tile(rows, cols, limit=TILE_BYTES):
    best = 8
    for tr in range(8, rows + 1, 8):
        if rows % tr == 0 and tr * cols * 4 <= limit:
            best = tr
    assert rows % best == 0, (rows, cols)
    return best


def _chip_partial(g, got, place, wire_dtype):
    ns, ah, b = got.shape
    sharded = ns == N_CHIPS
    tr = _row_tile(ah, b, PARTIAL_TILE_BYTES)
    nb = ah // tr

    def body(place_ref, *refs):
        g_refs, got_refs, outs = refs[:ns], refs[ns:2 * ns], refs[2 * ns:]
        parts = [g_refs[k][0] + got_refs[k][0] for k in range(ns)]
        own = parts[0]
        if sharded:
            for k in range(ns):
                outs[0][k] = parts[k].astype(wire_dtype)
                if k:
                    own = jnp.where(place_ref[0] == k, parts[k], own)
        outs[-1][0] = own.astype(wire_dtype)

    blk = (1, tr, b)
    in_specs = ([pl.BlockSpec(blk, lambda i, s, k=k: (k, s[1] * nb + i, 0)) for k in range(ns)]
                + [pl.BlockSpec(blk, lambda i, s, k=k: (k, i, 0)) for k in range(ns)])
    acc_spec = pl.BlockSpec(blk, lambda i, s: (s[0], i, 0))
    acc_shape = _sds((N_CHIPS, ah, b), wire_dtype)
    out = pl.pallas_call(
        body, name="grad_chip_partial",
        grid_spec=pltpu.PrefetchScalarGridSpec(
            num_scalar_prefetch=1, grid=(nb,), in_specs=in_specs,
            out_specs=[pl.BlockSpec((ns, tr, b), lambda i, s: (0, i, 0)), acc_spec] if sharded else [acc_spec]),
        out_shape=[acc_shape, acc_shape] if sharded else [acc_shape],
        compiler_params=pltpu.CompilerParams(dimension_semantics=("arbitrary",), vmem_limit_bytes=32 * 1024 * 1024),
    )(place, *([_in_hbm(g)] * ns), *([_in_hbm(got)] * ns))
    return (out[0], out[1]) if sharded else (None, out[0])


def _chip_sum(acc, place):
    _, ah, b = acc.shape
    tr = _row_tile(ah, b)

    def body(place_ref, p_ref, out_ref):
        total = p_ref[0].astype(F32) + p_ref[1].astype(F32)
        total = total + p_ref[2].astype(F32)
        out_ref[0] = total + p_ref[3].astype(F32)

    return pl.pallas_call(
        body, name="grad_chip_sum",
        grid_spec=pltpu.PrefetchScalarGridSpec(
            num_scalar_prefetch=1, grid=(ah // tr,),
            in_specs=[pl.BlockSpec((N_CHIPS, tr, b), lambda i, s: (0, i, 0))],
            out_specs=pl.BlockSpec((1, tr, b), lambda i, s: (s[1], i, 0))),
        out_shape=_sds((2, ah, b)),
        compiler_params=pltpu.CompilerParams(dimension_semantics=("arbitrary",)),
    )(place, _in_hbm(acc))


def _adam_math(w, g, m, v):
    nm = ADAM_B1 * m + (1.0 - ADAM_B1) * g
    nv = ADAM_B2 * v + (1.0 - ADAM_B2) * (g * g)
    m_hat = nm / (1.0 - ADAM_B1 ** ADAM_STEP)
    v_hat = nv / (1.0 - ADAM_B2 ** ADAM_STEP)
    return -ADAM_LR * (m_hat / (jnp.sqrt(v_hat) + ADAM_EPS) + ADAM_WD * w), nm, nv


def _adamw(w, g, m, v):
    a, b = w.shape
    tr = _row_tile(a, b)

    def body(w_ref, g_ref, m_ref, v_ref, g_out, d_ref, nm_ref, nv_ref):
        g_out[...] = g_ref[...]
        d_ref[...], nm_ref[...], nv_ref[...] = _adam_math(w_ref[...], g_ref[...], m_ref[...], v_ref[...])

    blk = pl.BlockSpec((tr, b), lambda i: (i, 0))
    return pl.pallas_call(
        body, name="adamw", grid=(a // tr,),
        in_specs=[blk] * 4, out_specs=[blk] * 4, out_shape=[_sds((a, b))] * 4,
        compiler_params=pltpu.CompilerParams(dimension_semantics=("arbitrary",)),
    )(w, g, m, v)


def _adamw_sum(w, m, v, acc, got, place):
    a, b = w.shape
    ah = a // 2
    tr = _row_tile(ah, b, TILE_BYTES // 2)
    nb = ah // tr

    def body(place_ref, w_ref, m_ref, v_ref, acc_ref, got_ref, g_out, d_ref, nm_ref, nv_ref):
        mine = (pl.program_id(0) // nb) == place_ref[1]
        part = lambda k: jnp.where(mine, acc_ref[k], got_ref[k]).astype(F32)
        g = part(0) + part(1)
        g = g + part(2)
        g = g + part(3)
        g_out[...] = g
        d_ref[...], nm_ref[...], nv_ref[...] = _adam_math(w_ref[...], g, m_ref[...], v_ref[...])

    blk = pl.BlockSpec((tr, b), lambda i, s: (i, 0))
    mine_spec = pl.BlockSpec((N_CHIPS, tr, b), lambda i, s: (0, jnp.where(i // nb == s[1], i % nb, 0), 0))
    got_spec = pl.BlockSpec((N_CHIPS, tr, b), lambda i, s: (0, jnp.where(i // nb == s[1], 0, i % nb), 0))
    return pl.pallas_call(
        body, name="adamw_sum",
        grid_spec=pltpu.PrefetchScalarGridSpec(
            num_scalar_prefetch=1, grid=(a // tr,), in_specs=[blk] * 3 + [mine_spec, got_spec], out_specs=[blk] * 4),
        out_shape=[_sds((a, b))] * 4,
        compiler_params=pltpu.CompilerParams(dimension_semantics=("arbitrary",), vmem_limit_bytes=VMEM_LIMIT),
    )(place, w, m, v, _in_hbm(acc), _in_hbm(got))


def _adamw_pieces(g, pieces, name):
    n = len(pieces)

    def body(g_ref, *refs):
        def grad(rows, cols):
            if len(g_ref.shape) == 2:
                return g_ref[rows, cols]
            total = g_ref[0, rows, cols]
            for k in range(1, g_ref.shape[0]):
                total = total + g_ref[k, rows, cols]
            return total

        ins, outs = refs[:3 * n], refs[3 * n:]
        for i, piece in enumerate(pieces):
            w_ref, m_ref, v_ref = ins[3 * i:3 * i + 3]
            o_g, o_d, o_m, o_v = outs[4 * i:4 * i + 4]
            if len(piece) == 5:
                g_v = grad(piece[3], piece[4])
                o_g[...] = g_v
                o_d[...], o_m[...], o_v[...] = _adam_math(w_ref[...], g_v, m_ref[...], v_ref[...])
            else:
                for r in range(w_ref.shape[1] // SMALL_COLS):
                    lanes = slice(r * SMALL_COLS, (r + 1) * SMALL_COLS)
                    g_v = grad(slice(piece[3] + r, piece[3] + r + 1), slice(None))
                    o_g[:, lanes] = g_v
                    o_d[:, lanes], o_m[:, lanes], o_v[:, lanes] = _adam_math(w_ref[:, lanes], g_v, m_ref[:, lanes],
                                                                            v_ref[:, lanes])

    operands = [t for piece in pieces for t in piece[:3]]
    out = pl.pallas_call(
        body, name=name,
        out_shape=[_sds(piece[0].shape) for piece in pieces for _ in range(4)],
    )(g, *operands)
    return [tuple(out[4 * i:4 * i + 4]) for i in range(n)]


TINY_ROWS, TINY_COLS = 16, 768
SMALL_COLS = 128
SMALL_ROWS = 624


def _pack_tiny(conv_w, b_gates, fcw):
    ns = conv_w.shape[0]
    pad = lambda t: jnp.pad(t, ((0, 0), (0, 0), (0, TINY_COLS - t.shape[2])))
    z = lambda rows: jnp.zeros((ns, rows, TINY_COLS), F32)
    return jnp.concatenate([pad(conv_w), pad(b_gates), z(2), fcw, z(5)], axis=1)


def _unpack_tiny(t):
    return t[:, 0:4, 0:256], t[:, 4:6, 0:256], t[:, 8:11, :]


def _cols_to_shards(t, n):
    return t.reshape(t.shape[0], N_CHIPS, n).transpose(1, 0, 2)


def _shards_to_cols(t):
    return t.transpose(1, 0, 2).reshape(t.shape[1], -1)


_VECTORS = ("g_mix_post", "conv_b", "lru_lambda", "g_ffn_pre", "g_ffn_post", "g_ple_gate", "g_ple_post", "pool_scale",
            "ffn_conv_b")
_VECTOR_LEN = {"pool_scale": POOL_WIDTH, "ffn_conv_b": D_FF}
POOL_W_ROWS = POOL_GROUPS * POOL_GROUP_DIM


def _vector_rows():
    rows, row = {}, POOL_W_ROWS
    for k in _VECTORS:
        rows[k] = row
        row += max(8, _VECTOR_LEN.get(k, D_MODEL) // SMALL_COLS)
    return rows, row


def _pack_small(grads, loss):
    tiles = lambda t: jnp.pad(t, ((0, -t.shape[0] % 8), (0, 0)))
    parts = [grads["pool_w"].reshape(POOL_W_ROWS, SMALL_COLS)] + [tiles(grads[k].reshape(-1, SMALL_COLS)) for k in _VECTORS]
    parts.append(tiles(loss))
    used = sum(t.shape[0] for t in parts)
    return jnp.concatenate(parts + [jnp.zeros((SMALL_ROWS - used, SMALL_COLS), F32)], axis=0)


def _gates_block_diag(w):
    w4 = w.reshape(2, GATE_BLOCKS, 4, RNN_HEAD_DIM, RNN_HEAD_DIM)
    eye = jnp.eye(4, dtype=w.dtype)
    return jnp.einsum("gqhij,hk->gqhikj", w4, eye).reshape(2, GATE_BLOCKS, GATE_BLOCK, GATE_BLOCK)


def _gates_from_block_diag(dw):
    d6 = dw.reshape(2, GATE_BLOCKS, 4, RNN_HEAD_DIM, 4, RNN_HEAD_DIM)
    blocks = [d6[:, :, hh, :, hh, :] for hh in range(4)]
    return jnp.stack(blocks, axis=2).reshape(2, RNN_HEADS, RNN_HEAD_DIM, RNN_HEAD_DIM)


ROW_TILE = 256
DW_TILES = 4

_SHARDED = ("w_in", "w_pool_out", "w_rg_out", "w_o", "w_up", "w_down", "w_ple_gate", "w_ple_proj")
_WEIGHTS = ("g_mix_pre", "g_mix_post", "w_in", "pool_w", "pool_scale", "w_pool_out", "conv_w", "conv_b", "w_rg_gates",
            "b_rg_gates", "lru_lambda", "w_rg_out", "w_o", "g_ffn_pre", "g_ffn_post", "w_up", "ffn_conv_w", "ffn_conv_b",
            "w_down", "g_ple_gate", "w_ple_gate", "w_ple_proj", "g_ple_post")


def _wire_dtype(g):
    return BF16 if g.shape[1] >= 64 and g.shape[2] > SMALL_COLS else F32


def _partials(grads, got, place):
    parts = [_chip_partial(g, r, place, _wire_dtype(g)) for g, r in zip(grads, got)]
    return [send for send, _ in parts], [acc for _, acc in parts]


def _whole(both):
    return [b.reshape(2 * b.shape[1], b.shape[2]) for b in both]


def _step(x, p, tgt, rep, place, ts):
    vec = lambda k: rep[k].reshape(1, -1)
    pool_w = rep["pool_w"].astype(BF16)
    wg = _gates_block_diag(rep["w_rg_gates"]).astype(BF16)
    sq = lambda t: t.reshape(D_MODEL, D_MODEL)
    by4 = lambda t: t.reshape(N_CHIPS, -1, D_MODEL)

    first, ride1, ride2, ride3 = (("w_in", "w_pool_out", "tiny"), ("w_rg_out", "w_o", "w_down"), ("w_up",),
                                  ("w_ple_gate", "w_ple_proj"))
    later = ride1 + ride2 + ride3
    tiny = _pack_tiny(rep["conv_w"][None], rep["b_rg_gates"][None], rep["ffn_conv_w"][None])[0]
    own_first, _ = _own_slots([rep["w_in"], rep["w_pool_out"], tiny], [BF16, BF16, F32], "own_slots_first")
    own_later, (got,) = _own_slots([rep[k] for k in later], [BF16] * len(later), "own_slots_gather_first",
                                   [_gather_task(own_first)])
    own = dict(zip(later, own_later))
    full = dict(zip(first, got))
    conv_w, b_gates, fcw = [_shards_to_cols(t) for t in _unpack_tiny(full["tiny"])]

    (urx, urg, gp, gr, d, ypool, h1), (got,) = _fwd_in_pool(
        x, vec("g_mix_pre"), full["w_in"], pool_w, vec("pool_scale"), full["w_pool_out"], ts,
        [_gather_task([own[k] for k in ride1])])
    full.update(zip(ride1, got))
    w_rg_out, w_o, w_down = sq(full["w_rg_out"]), sq(full["w_o"]), full["w_down"].reshape(D_FF, D_MODEL)
    (xc, r, ig, h, yrnn, mo, x1, glr, ggr, sp, sr), (got,) = _fwd_rnn_merge(
        urx, urg, gp, gr, ypool, x, conv_w, vec("conv_b"), wg, b_gates, vec("lru_lambda"), w_rg_out, w_o,
        vec("g_mix_post"), ts, [_gather_task([own[k] for k in ride2], relay_steps=2)])
    full.update(zip(ride2, got))
    (up, gl, gg, h2, dn, x2), (got,) = _fwd_ffn(x1, vec("g_ffn_pre"), full["w_up"], fcw, vec("ffn_conv_b"), w_down,
                                               vec("g_ffn_post"), ts, [_gather_task([own[k] for k in ride3], relay_steps=4)])
    full.update(zip(ride3, got))
    dx2, loss, d_w_gate, d_w_proj, d_g_ple_gate, d_g_ple_post = _ple_loss(
        x2, p, tgt, vec("g_ple_gate"), sq(full["w_ple_gate"]), full["w_ple_proj"], vec("g_ple_post"), ts)
    dup, d_w_down, d_fcw, d_fcb, d_g_ffn_post = _bwd_ffn_down(dx2, dn, up, gl, gg, fcw, w_down, vec("g_ffn_post"), ts)

    names1, grads1 = ("w_ple_gate", "w_ple_proj", "w_down"), [by4(d_w_gate), d_w_proj, by4(d_w_down)]
    (dx1, d_g_ffn_pre), (got1,) = _bwd_ffn_up(dup, x1, dx2, vec("g_ffn_pre"), full["w_up"], ts, [_halves_task(grads1)])
    (d_w_up,), (accs1,) = _dw_up(h2, dup, ts, [_exchange_task(*_partials(grads1, got1, place))])
    (dgp, dgr, dyp, dyr, d_w_o, d_g_mix_post), (got2, theirs1) = _bwd_merge(
        dx1, mo, sp, sr, ypool, yrnn, vec("g_mix_post"), w_o, ts, [_halves_task([d_w_up]), _swap_task(accs1)])
    (durx, durg, d_w_rg_out, d_wg, d_conv_w, d_conv_b, d_b_gates, d_lam), (accs2,) = _bwd_rnn(
        dyr, urx, glr, ggr, xc, r, ig, h, conv_w, wg, vec("lru_lambda"), w_rg_out, ts,
        [_exchange_task(*_partials([d_w_up], got2, place))])
    names3 = ("w_o", "w_rg_out", "tiny", "w_rg_gates")
    grads3 = [by4(d_w_o), by4(d_w_rg_out),
              _pack_tiny(_cols_to_shards(d_conv_w, 256), _cols_to_shards(d_b_gates, 256), _cols_to_shards(d_fcw, 768)),
              _gates_from_block_diag(d_wg).reshape(1, 2 * RNN_HEADS * RNN_HEAD_DIM, RNN_HEAD_DIM)]
    (dzp, d_w_pool_out, d_pool_w, d_pool_scale), (got3, theirs2) = _bwd_pool(
        dyp, d, pool_w, vec("pool_scale"), full["w_pool_out"], ts, [_halves_task(grads3), _swap_task(accs2)])
    (d_w_in,), (accs3,) = _dw_in(h1, dzp, durx, durg, dgp, dgr, ts, [_exchange_task(*_partials(grads3, got3, place))])

    replicated = {"g_mix_post": d_g_mix_post, "conv_b": d_conv_b, "lru_lambda": d_lam, "g_ffn_pre": d_g_ffn_pre,
                  "g_ffn_post": d_g_ffn_post, "g_ple_gate": d_g_ple_gate, "g_ple_post": d_g_ple_post,
                  "pool_scale": d_pool_scale, "ffn_conv_b": d_fcb, "pool_w": d_pool_w}
    names4 = ("w_in", "w_pool_out", "small")
    grads4 = [d_w_in, d_w_pool_out, _pack_small(replicated, loss)[None]]
    got4 = _run(_halves_task(grads4), "grad_sibling_halves")
    (grad_x, d_g_mix_pre), (accs4, theirs3, both3) = _bwd_in(
        dzp, durx, durg, dgp, dgr, x, dx1, vec("g_mix_pre"), full["w_in"], ts,
        [_exchange_task(*_partials(grads4, got4, place)), _swap_task(accs3[:2]),
         _share_task([_chip_sum(acc, place) for acc in accs3[2:]])])
    theirs4, both4, (g_mix_pre_parts,) = _run(
        [_swap_task(accs4[:2]), _share_task([_chip_sum(acc, place) for acc in accs4[2:]]),
         _all_devices_task([d_g_mix_pre.reshape(SUBLANES, SMALL_COLS)])], "grad_sibling_share")
    mine = accs1 + accs2 + accs3[:2] + accs4[:2]
    partials = dict(zip(names1 + ("w_up",) + names3[:2] + names4[:2], zip(mine, theirs1 + theirs2 + theirs3 + theirs4)))
    return grad_x, partials, dict(zip(names3[2:] + names4[2:], _whole(both3) + _whole(both4))), g_mix_pre_parts


def kernel(x, p, g_mix_pre, g_mix_post, w_in, pool_w, pool_scale, w_pool_out, conv_w, conv_b, w_rg_gates, b_rg_gates, lru_lambda, w_rg_out, w_o, g_ffn_pre, g_ffn_post, w_up, ffn_conv_w, ffn_conv_b, w_down, g_ple_gate, w_ple_gate, w_ple_proj, g_ple_post, loss_target, m_g_mix_pre, m_g_mix_post, m_w_in, m_pool_w, m_pool_scale, m_w_pool_out, m_conv_w, m_conv_b, m_w_rg_gates, m_b_rg_gates, m_lru_lambda, m_w_rg_out, m_w_o, m_g_ffn_pre, m_g_ffn_post, m_w_up, m_ffn_conv_w, m_ffn_conv_b, m_w_down, m_g_ple_gate, m_w_ple_gate, m_w_ple_proj, m_g_ple_post, v_g_mix_pre, v_g_mix_post, v_w_in, v_pool_w, v_pool_scale, v_w_pool_out, v_conv_w, v_conv_b, v_w_rg_gates, v_b_rg_gates, v_lru_lambda, v_w_rg_out, v_w_o, v_g_ffn_pre, v_g_ffn_post, v_w_up, v_ffn_conv_w, v_ffn_conv_b, v_w_down, v_g_ple_gate, v_w_ple_gate, v_w_ple_proj, v_g_ple_post):
    args = dict(locals())
    w = {k: args[k][0] for k in _WEIGHTS}
    m = {k: args["m_" + k][0] for k in _WEIGHTS}
    v = {k: args["v_" + k][0] for k in _WEIGHTS}
    place = jnp.stack([2 * lax.axis_index("x") + lax.axis_index("y"), lax.axis_index("c")]).astype(jnp.int32)
    grad_x, partials, reduced, g_mix_pre_parts = _step(x[0], p[0, 0], loss_target[0], w, place, ROW_TILE)

    gates_2d = (2 * RNN_HEADS * RNN_HEAD_DIM, RNN_HEAD_DIM)
    as2d = lambda k, shape: tuple(t[k].reshape(shape) for t in (w, m, v))
    done = {k: tuple(_adamw_sum(w[k], m[k], v[k], *partials[k], place)) for k in _SHARDED}
    gates_w, gates_m, gates_v = as2d("w_rg_gates", gates_2d)
    done["w_rg_gates"] = tuple(_adamw(gates_w, reduced["w_rg_gates"], gates_m, gates_v))
    tiny_names = ("conv_w", "b_rg_gates", "ffn_conv_w")
    tiny_at = ((slice(0, 4), slice(0, 256)), (slice(4, 6), slice(0, 256)), (slice(8, 11), slice(None)))
    done.update(zip(tiny_names, _adamw_pieces(
        reduced["tiny"], [(w[k], m[k], v[k]) + at for k, at in zip(tiny_names, tiny_at)], "adamw_tiny")))
    vector_rows, loss_row = _vector_rows()
    pieces = [as2d("pool_w", (POOL_W_ROWS, SMALL_COLS)) + (slice(0, POOL_W_ROWS), slice(None))]
    pieces += [as2d(k, (1, -1)) + (vector_rows[k],) for k in _VECTORS]
    done.update(zip(("pool_w",) + _VECTORS, _adamw_pieces(reduced["small"], pieces, "adamw_small")))
    done["g_mix_pre"] = _adamw_pieces(g_mix_pre_parts, [as2d("g_mix_pre", (1, -1)) + (0,)], "adamw_g_mix_pre")[0]

    result = [reduced["small"][loss_row, 0], grad_x[None]]
    for kind in range(4):
        result += [done[k][kind].reshape(args[k].shape) for k in _WEIGHTS]
    return tuple(result)
```

```python
import functools

import jax
import jax.numpy as jnp
from jax import lax
from jax.experimental import pallas as pl
from jax.experimental.pallas import tpu as pltpu

F32 = jnp.float32
BF16 = jnp.bfloat16

D_MODEL = 1024
POOL_WINDOWS = (2, 4, 8, 16)
POOL_GROUPS = 4
POOL_WIDTH = 512
POOL_GROUP_DIM = 128
RNN_HEADS = 16
RNN_HEAD_DIM = 64
GATE_BLOCK = 256
GATE_BLOCKS = D_MODEL // GATE_BLOCK
LRU_C = 8.0
D_FF = 3072
PLE_DIM = 256
RMS_EPS = 1e-6
IN_TOTAL = 4608
N_CHIPS = 4
IN_SHARD = IN_TOTAL // N_CHIPS
UP_SHARD = 2 * D_FF // N_CHIPS
POOL_HALO = 16
CONV_HALO = 8

ADAM_LR = 0.001
ADAM_B1 = 0.9
ADAM_B2 = 0.999
ADAM_EPS = 1e-08
ADAM_WD = 0.01
ADAM_STEP = 10

VMEM_LIMIT = 56 * 1024 * 1024
MESH = pl.DeviceIdType.MESH

_GELU_C = 0.7978845608028654
_GELU_A = 0.044715


def _dot(a, b):
    return jnp.dot(a.astype(BF16), b.astype(BF16), preferred_element_type=F32)


def _dot_nt(a, b):
    return lax.dot_general(a.astype(BF16), b.astype(BF16), (((1,), (1,)), ((), ())), preferred_element_type=F32)


def _dot_tn(a, b):
    return lax.dot_general(a.astype(BF16), b.astype(BF16), (((0,), (0,)), ((), ())), preferred_element_type=F32)


def _rms_fwd(x, g):
    r = lax.rsqrt(jnp.mean(x * x, axis=-1, keepdims=True) + RMS_EPS)
    xh = x * r
    return xh * g, xh, r


def _rms_bwd(xh, r, g, dy):
    dxh = dy * g
    dg = jnp.sum(dy * xh, axis=0, keepdims=True)
    dx = r * (dxh - xh * jnp.mean(dxh * xh, axis=-1, keepdims=True))
    return dx, dg


def _sigmoid(x):
    return 0.5 * jnp.tanh(0.5 * x) + 0.5


def _gelu(x):
    t = jnp.tanh(_GELU_C * (x + _GELU_A * x * x * x))
    return 0.5 * x * (1.0 + t), t


def _gelu_grad(x, t):
    return 0.5 * (1.0 + t) + 0.5 * x * (1.0 - t * t) * _GELU_C * (1.0 + 3.0 * _GELU_A * x * x)


def _softplus_neg(lam):
    nl = -lam
    return jnp.maximum(nl, 0.0) + jnp.log(1.0 + jnp.exp(-jnp.abs(nl)))


def _lru_coeffs(r, lam, first_row):
    c8 = LRU_C * _softplus_neg(lam)
    la = -(c8 * r)
    a = jnp.exp(la)
    m2 = jnp.tanh(-la) * (1.0 + a * a)
    mult = jnp.where(first_row, 1.0, jnp.sqrt(m2))
    return c8, a, m2, mult


SUBLANES = 8


def _scan_fwd(a, u, carry):
    n = a.shape[0]
    sub = lax.broadcasted_iota(jnp.int32, (n, 1), 0) % SUBLANES
    acc_a, acc_h = a, u
    for s in (1, 2, 4):
        m = sub >= s
        h_s = jnp.where(m, pltpu.roll(acc_h, s, 0), 0.0)
        a_s = jnp.where(m, pltpu.roll(acc_a, s, 0), 1.0)
        acc_h = acc_a * h_s + acc_h
        acc_a = acc_a * a_s
    out = []
    for g in range(n // SUBLANES):
        rows = slice(g * SUBLANES, (g + 1) * SUBLANES)
        out.append(acc_h[rows] + acc_a[rows] * carry)
        carry = out[-1][SUBLANES - 1:SUBLANES]
    return jnp.concatenate(out, axis=0)


def _scan_bwd(b, g, carry):
    n = b.shape[0]
    sub = lax.broadcasted_iota(jnp.int32, (n, 1), 0) % SUBLANES
    acc_b, acc_l = b, g
    for s in (1, 2, 4):
        m = sub < SUBLANES - s
        l_s = jnp.where(m, pltpu.roll(acc_l, n - s, 0), 0.0)
        b_s = jnp.where(m, pltpu.roll(acc_b, n - s, 0), 1.0)
        acc_l = acc_b * l_s + acc_l
        acc_b = acc_b * b_s
    out = [None] * (n // SUBLANES)
    for g in reversed(range(n // SUBLANES)):
        rows = slice(g * SUBLANES, (g + 1) * SUBLANES)
        out[g] = acc_l[rows] + acc_b[rows] * carry
        carry = out[g][0:1]
    return jnp.concatenate(out, axis=0)


def _shift_down(ext, k, halo):
    return pltpu.roll(ext, k, 0)[halo:] if k else ext[halo:]


def _shift_up(ext, k, ts):
    return pltpu.roll(ext, ext.shape[0] - k, 0)[:ts] if k else ext[:ts]


def _rows(ts, width, nt=None, col=0):
    if nt is None:
        return pl.BlockSpec((ts, width), lambda i: (i, col))
    return pl.BlockSpec((ts, width), lambda i: (nt - 1 - i, col))


def _resident(shape):
    zeros = (0,) * len(shape)
    return pl.BlockSpec(shape, lambda i: zeros, pipeline_mode=pl.Buffered(1))


def _acc(shape):
    zeros = (0,) * len(shape)
    return pl.BlockSpec(shape, lambda i: zeros)


def _params():
    return pltpu.CompilerParams(dimension_semantics=("arbitrary",), vmem_limit_bytes=VMEM_LIMIT)


def _sds(shape, dtype=F32):
    return jax.ShapeDtypeStruct(shape, dtype)


class _Task:
    def __init__(self, ins, out_shapes, aliases, sems, start, finish, relays=()):
        self.ins, self.out_shapes, self.aliases, self.sems = list(ins), list(out_shapes), dict(aliases), list(sems)
        self.start, self.relays, self.finish = start, list(relays), finish


def _call(body, name, grid, in_specs, out_specs, out_shape, scratch_shapes, args, tasks=()):
    n_in, n_out, n_scr = len(in_specs), len(out_specs), len(scratch_shapes)
    t_in = [len(t.ins) for t in tasks]
    t_out = [len(t.out_shapes) for t in tasks]
    t_sem = [len(t.sems) for t in tasks]
    steps = 1
    for g in grid:
        steps *= g

    def take(refs, pos, counts):
        groups = []
        for c in counts:
            groups.append(refs[pos:pos + c])
            pos += c
        return groups, pos

    def wrapped(*refs):
        (cin,), pos = take(refs, 0, [n_in])
        tin, pos = take(refs, pos, t_in)
        (cout,), pos = take(refs, pos, [n_out])
        tout, pos = take(refs, pos, t_out)
        (cscr,), pos = take(refs, pos, [n_scr])
        tsem, pos = take(refs, pos, t_sem)
        if not grid:
            for t, a, b, c in zip(tasks, tin, tout, tsem):
                t.start(a, b, c)
            if body is not None:
                body(*cin, *cout, *cscr)
            for t, a, b, c in zip(tasks, tin, tout, tsem):
                for relay, _ in t.relays:
                    relay(a, b, c)
            for t, a, b, c in zip(tasks, tin, tout, tsem):
                t.finish(a, b, c)
            return
        step = pl.program_id(0)
        for axis in range(1, len(grid)):
            step = step * grid[axis] + pl.program_id(axis)
        if tasks:
            @pl.when(step == 0)
            def _():
                for t, a, b, c in zip(tasks, tin, tout, tsem):
                    t.start(a, b, c)

        body(*cin, *cout, *cscr)
        for t, a, b, c in zip(tasks, tin, tout, tsem):
            for relay, before in t.relays:
                pl.when(step == max(steps - 1 - before, 0))(functools.partial(relay, a, b, c))

        if tasks:
            @pl.when(step == steps - 1)
            def _():
                for t, a, b, c in zip(tasks, tin, tout, tsem):
                    t.finish(a, b, c)

    aliases, in_pos, out_pos = {}, n_in, n_out
    for t, ni, no in zip(tasks, t_in, t_out):
        aliases.update({in_pos + a: out_pos + b for a, b in t.aliases.items()})
        in_pos, out_pos = in_pos + ni, out_pos + no
    any_spec = pl.BlockSpec(memory_space=pltpu.HBM)
    kwargs = dict(grid=grid, compiler_params=pltpu.CompilerParams(
        dimension_semantics=("arbitrary",) * len(grid), vmem_limit_bytes=VMEM_LIMIT)) if grid else dict(
        compiler_params=pltpu.CompilerParams(vmem_limit_bytes=VMEM_LIMIT))
    out = pl.pallas_call(
        wrapped, name=name,
        in_specs=list(in_specs) + [any_spec] * sum(t_in),
        out_specs=list(out_specs) + [any_spec] * sum(t_out),
        out_shape=list(out_shape) + [s for t in tasks for s in t.out_shapes],
        scratch_shapes=list(scratch_shapes) + [s for t in tasks for s in t.sems],
        input_output_aliases=aliases, **kwargs,
    )(*args, *[pltpu.with_memory_space_constraint(a, pltpu.HBM) for t in tasks for a in t.ins])
    task_outs, pos = take(list(out), n_out, t_out)
    return list(out[:n_out]), task_outs


def _fwd_in_pool(x, g_pre, w_in, pool_w, pool_scale, w_pool_out, ts, tasks=()):
    s = x.shape[0]

    def body(x_ref, g_ref, win_ref, pw_ref, ps_ref, wpo_ref,
             urx_ref, urg_ref, gp_ref, gr_ref, d_ref, yp_ref, h1_ref, z_scr, halo_scr):
        i = pl.program_id(0)

        @pl.when(i == 0)
        def _():
            halo_scr[...] = jnp.zeros_like(halo_scr)

        h1, _, _ = _rms_fwd(x_ref[...], g_ref[...])
        h1 = h1.astype(BF16)
        h1_ref[...] = h1
        for j in range(N_CHIPS):
            z_scr[:, j * IN_SHARD:(j + 1) * IN_SHARD] = jnp.dot(h1, win_ref[j], preferred_element_type=F32)
        urx_ref[...] = z_scr[:, 512:1536]
        urg_ref[...] = z_scr[:, 1536:2560]
        gp_ref[...] = z_scr[:, 2560:3584]
        gr_ref[...] = z_scr[:, 3584:4608]
        u = z_scr[:, 0:POOL_WIDTH]
        ext = jnp.concatenate([halo_scr[...], u], axis=0)
        halo_scr[...] = u[ts - POOL_HALO:, :]
        t = i * ts + lax.broadcasted_iota(jnp.int32, (ts, 1), 0)
        y4 = []
        for g, w in enumerate(POOL_WINDOWS):
            lanes = slice(g * POOL_GROUP_DIM, (g + 1) * POOL_GROUP_DIM)
            acc = ext[:, lanes]
            sh = 1
            while sh < w:
                acc = acc + pltpu.roll(acc, sh, 0)
                sh *= 2
            inv = 1.0 / jnp.minimum(t + 1, w).astype(F32)
            dg = acc[POOL_HALO:, :] * inv - u[:, lanes]
            d_ref[:, lanes] = dg
            y4.append(_dot(dg, pw_ref[g]))
        ypre = jnp.concatenate(y4, axis=1) * ps_ref[...]
        ypre = ypre.astype(BF16)
        for j in range(N_CHIPS):
            yp_ref[:, j * 256:(j + 1) * 256] = jnp.dot(ypre, wpo_ref[j], preferred_element_type=F32)

    return _call(
        body, "fwd_in_pool", (s // ts,),
        [_rows(ts, D_MODEL), _resident((1, D_MODEL)), _resident(w_in.shape), _resident(pool_w.shape),
         _resident((1, POOL_WIDTH)), _resident(w_pool_out.shape)],
        [_rows(ts, D_MODEL)] * 4 + [_rows(ts, POOL_WIDTH), _rows(ts, D_MODEL), _rows(ts, D_MODEL)],
        [_sds((s, D_MODEL))] * 4 + [_sds((s, POOL_WIDTH)), _sds((s, D_MODEL)), _sds((s, D_MODEL), BF16)],
        [pltpu.VMEM((ts, IN_TOTAL), F32), pltpu.VMEM((POOL_HALO, POOL_WIDTH), F32)],
        (x, g_pre, w_in, pool_w, pool_scale, w_pool_out), tasks)


def _fwd_rnn_merge(urx, urg, gp, gr, ypool, x, conv_w, conv_b, wg, bg, lam, w_rg_out, w_o, g_post, ts, tasks=()):
    s = x.shape[0]

    def body(urx_ref, urg_ref, gp_ref, gr_ref, yp_ref, x_ref, cw_ref, cb_ref, wg_ref, bg_ref, lam_ref, wrg_ref, wo_ref,
             gpost_ref, xc_ref, r_ref, ig_ref, h_ref, yr_ref, mo_ref, x1_ref, gl_ref, gg_ref, sp_ref, sr_ref,
             halo_scr, carry_scr):
        i = pl.program_id(0)

        @pl.when(i == 0)
        def _():
            halo_scr[...] = jnp.zeros_like(halo_scr)
            carry_scr[...] = jnp.zeros_like(carry_scr)

        urx_v = urx_ref[...]
        ext = jnp.concatenate([halo_scr[...], urx_v], axis=0)
        halo_scr[...] = urx_v[ts - CONV_HALO:, :]
        cw = cw_ref[...]
        xc = (cb_ref[...] + cw[3:4] * urx_v + cw[2:3] * _shift_down(ext, 1, CONV_HALO)
              + cw[1:2] * _shift_down(ext, 2, CONV_HALO) + cw[0:1] * _shift_down(ext, 3, CONV_HALO))
        xc_ref[...] = xc
        xcb = xc.astype(BF16)
        lin = []
        for gate in range(2):
            parts = [jnp.dot(xcb[:, q * GATE_BLOCK:(q + 1) * GATE_BLOCK], wg_ref[gate, q], preferred_element_type=F32)
                     for q in range(GATE_BLOCKS)]
            lin.append(jnp.concatenate(parts, axis=1) + bg_ref[gate:gate + 1, :])
        r = _sigmoid(lin[0])
        ig = _sigmoid(lin[1])
        r_ref[...] = r
        ig_ref[...] = ig
        first_row = (i * ts + lax.broadcasted_iota(jnp.int32, (ts, 1), 0)) == 0
        _, a, _, mult = _lru_coeffs(r, lam_ref[...], first_row)
        h = _scan_fwd(a, mult * ig * xc, carry_scr[0:1, :])
        carry_scr[0:1, :] = h[ts - 1:ts, :]
        h_ref[...] = h
        urg_v = urg_ref[...]
        gl, t = _gelu(urg_v)
        gl_ref[...] = gl.astype(BF16)
        gg_ref[...] = _gelu_grad(urg_v, t).astype(BF16)
        yr = _dot(h * gl, wrg_ref[...])
        yr_ref[...] = yr
        sp = _sigmoid(gp_ref[...])
        sr = _sigmoid(gr_ref[...])
        sp_ref[...] = sp.astype(BF16)
        sr_ref[...] = sr.astype(BF16)
        merged = sp * yp_ref[...] + sr * yr
        mo = _dot(merged, wo_ref[...])
        mo_ref[...] = mo
        y, _, _ = _rms_fwd(mo, gpost_ref[...])
        x1_ref[...] = x_ref[...] + y

    row = _rows(ts, D_MODEL)
    return _call(
        body, "fwd_rnn_merge", (s // ts,),
        [row] * 6 + [_resident(conv_w.shape), _resident((1, D_MODEL)), _resident(wg.shape), _resident(bg.shape),
                     _resident((1, D_MODEL)), _resident(w_rg_out.shape), _resident(w_o.shape), _resident((1, D_MODEL))],
        [row] * 11, [_sds((s, D_MODEL))] * 7 + [_sds((s, D_MODEL), BF16)] * 4,
        [pltpu.VMEM((CONV_HALO, D_MODEL), F32), pltpu.VMEM((8, D_MODEL), F32)],
        (urx, urg, gp, gr, ypool, x, conv_w, conv_b, wg, bg, lam, w_rg_out, w_o, g_post), tasks)


def _fwd_ffn(x1, g_pre, w_up, fcw, fcb, w_down, g_post, ts, tasks=()):
    s = x1.shape[0]

    def body(x1_ref, g_ref, wup_ref, fcw_ref, fcb_ref, wd_ref, gpost_ref,
             up_ref, gl_ref, gg_ref, h2_ref, dn_ref, x2_ref, up_scr, halo_scr):
        i = pl.program_id(0)

        @pl.when(i == 0)
        def _():
            halo_scr[...] = jnp.zeros_like(halo_scr)

        x1_v = x1_ref[...]
        h2, _, _ = _rms_fwd(x1_v, g_ref[...])
        h2 = h2.astype(BF16)
        h2_ref[...] = h2
        for j in range(N_CHIPS):
            up_scr[:, j * UP_SHARD:(j + 1) * UP_SHARD] = jnp.dot(h2, wup_ref[j], preferred_element_type=F32)
        up_ref[...] = up_scr[...].astype(BF16)
        ug = up_scr[:, 0:D_FF]
        ext = jnp.concatenate([halo_scr[...], ug], axis=0)
        halo_scr[...] = ug[ts - CONV_HALO:, :]
        w = fcw_ref[...]
        gh = (fcb_ref[...] + w[2:3] * ug + w[1:2] * _shift_down(ext, 1, CONV_HALO)
              + w[0:1] * _shift_down(ext, 2, CONV_HALO))
        gl, t = _gelu(gh)
        gl_ref[...] = gl.astype(BF16)
        gg_ref[...] = _gelu_grad(gh, t).astype(BF16)
        dn = _dot(gl * up_scr[:, D_FF:], wd_ref[...])
        dn_ref[...] = dn
        y, _, _ = _rms_fwd(dn, gpost_ref[...])
        x2_ref[...] = x1_v + y

    row = _rows(ts, D_MODEL)
    return _call(
        body, "fwd_ffn", (s // ts,),
        [row, _resident((1, D_MODEL)), _resident(w_up.shape), _resident(fcw.shape), _resident((1, D_FF)),
         _resident(w_down.shape), _resident((1, D_MODEL))],
        [_rows(ts, 2 * D_FF), _rows(ts, D_FF), _rows(ts, D_FF), row, row, row],
        [_sds((s, 2 * D_FF), BF16), _sds((s, D_FF), BF16), _sds((s, D_FF), BF16), _sds((s, D_MODEL), BF16),
         _sds((s, D_MODEL)), _sds((s, D_MODEL))],
        [pltpu.VMEM((ts, 2 * D_FF), F32), pltpu.VMEM((CONV_HALO, D_FF), F32)],
        (x1, g_pre, w_up, fcw, fcb, w_down, g_post), tasks)


def _ple_loss(x2, p, tgt, g_gate, w_gate, w_proj, g_post, ts):
    s = x2.shape[0]

    def body(x2_ref, p_ref, t_ref, gg_ref, wg_ref, wp_ref, gp_ref, dx2_ref, loss_ref, dwg_ref, dwp_ref, dgg_ref, dgp_ref):
        @pl.when(pl.program_id(0) == 0)
        def _():
            loss_ref[...] = jnp.zeros_like(loss_ref)
            dwg_ref[...] = jnp.zeros_like(dwg_ref)
            dwp_ref[...] = jnp.zeros_like(dwp_ref)
            dgg_ref[...] = jnp.zeros_like(dgg_ref)
            dgp_ref[...] = jnp.zeros_like(dgp_ref)

        x2_v = x2_ref[...]
        n3, xh3, r3 = _rms_fwd(x2_v, gg_ref[...])
        pg = _sigmoid(_dot(n3, wg_ref[...]))
        pb = p_ref[...].astype(BF16)
        q = jnp.concatenate([jnp.dot(pb, wp_ref[j], preferred_element_type=F32) for j in range(N_CHIPS)], axis=1)
        ple, qh, rq = _rms_fwd(q, gp_ref[...])
        e = x2_v + pg * ple - t_ref[...]
        loss_ref[...] += 0.5 * jnp.sum(jnp.mean(e * e, axis=-1, keepdims=True), axis=0, keepdims=True)
        dy = e * (1.0 / D_MODEL)
        dpgl = dy * ple * pg * (1.0 - pg)
        dwg_ref[...] += _dot_tn(n3, dpgl)
        dx3, dgg = _rms_bwd(xh3, r3, gg_ref[...], _dot_nt(dpgl, wg_ref[...]))
        dgg_ref[...] += dgg
        dq, dgp = _rms_bwd(qh, rq, gp_ref[...], dy * pg)
        dgp_ref[...] += dgp
        for j in range(N_CHIPS):
            dwp_ref[j] += _dot_tn(pb, dq[:, j * 256:(j + 1) * 256])
        dx2_ref[...] = dy + dx3

    row = _rows(ts, D_MODEL)
    vec = _acc((1, D_MODEL))
    return pl.pallas_call(
        body, name="ple_loss", grid=(s // ts,),
        in_specs=[row, _rows(ts, PLE_DIM), row, _resident((1, D_MODEL)), _resident(w_gate.shape), _resident(w_proj.shape),
                  _resident((1, D_MODEL))],
        out_specs=[row, _acc((1, 128)), _acc(w_gate.shape), _acc(w_proj.shape), vec, vec],
        out_shape=[_sds((s, D_MODEL)), _sds((1, 128)), _sds(w_gate.shape), _sds(w_proj.shape), _sds((1, D_MODEL)),
                   _sds((1, D_MODEL))],
        compiler_params=_params(),
    )(x2, p, tgt, g_gate, w_gate, w_proj, g_post)


def _bwd_ffn_down(dx2, dn, up, gl, gg, fcw, w_down, g_post, ts):
    s = dx2.shape[0]
    nt = s // ts

    def body(dx2_ref, dn_ref, up_ref, gl_ref, gg_ref, fcw_ref, wd_ref, gpost_ref,
             dup_ref, dwd_ref, dfcw_ref, dfcb_ref, dgp_ref, carry_scr):
        i = pl.program_id(0)

        @pl.when(i == 0)
        def _():
            carry_scr[...] = jnp.zeros_like(carry_scr)
            dwd_ref[...] = jnp.zeros_like(dwd_ref)
            dfcw_ref[...] = jnp.zeros_like(dfcw_ref)
            dfcb_ref[...] = jnp.zeros_like(dfcb_ref)
            dgp_ref[...] = jnp.zeros_like(dgp_ref)

        _, xh, r = _rms_fwd(dn_ref[...], gpost_ref[...])
        ddn, dgp = _rms_bwd(xh, r, gpost_ref[...], dx2_ref[...])
        dgp_ref[...] += dgp
        dhid = _dot_nt(ddn, wd_ref[...])
        ug = up_ref[:, 0:D_FF].astype(F32)
        uv = up_ref[:, D_FF:].astype(F32)
        gl = gl_ref[...].astype(F32)
        w = fcw_ref[...]
        dwd_ref[...] += _dot_tn(gl * uv, ddn)
        dgh = dhid * uv * gg_ref[...].astype(F32)
        dup_ref[:, D_FF:] = (dhid * gl).astype(BF16)
        extd = jnp.concatenate([dgh, carry_scr[...]], axis=0)
        carry_scr[...] = dgh[0:CONV_HALO, :]
        d1 = _shift_up(extd, 1, ts)
        d2 = _shift_up(extd, 2, ts)
        dup_ref[:, 0:D_FF] = (w[2:3] * dgh + w[1:2] * d1 + w[0:1] * d2).astype(BF16)
        dfcw_ref[2:3, :] += jnp.sum(ug * dgh, axis=0, keepdims=True)
        dfcw_ref[1:2, :] += jnp.sum(ug * d1, axis=0, keepdims=True)
        dfcw_ref[0:1, :] += jnp.sum(ug * d2, axis=0, keepdims=True)
        dfcb_ref[...] += jnp.sum(dgh, axis=0, keepdims=True)

    row = _rows(ts, D_MODEL, nt)
    wide = _rows(ts, D_FF, nt)
    return pl.pallas_call(
        body, name="bwd_ffn_down", grid=(nt,),
        in_specs=[row, row, _rows(ts, 2 * D_FF, nt), wide, wide, _resident(fcw.shape), _resident(w_down.shape),
                  _resident((1, D_MODEL))],
        out_specs=[_rows(ts, 2 * D_FF, nt), _acc(w_down.shape), _acc(fcw.shape), _acc((1, D_FF)), _acc((1, D_MODEL))],
        out_shape=[_sds((s, 2 * D_FF), BF16), _sds(w_down.shape), _sds(fcw.shape), _sds((1, D_FF)), _sds((1, D_MODEL))],
        scratch_shapes=[pltpu.VMEM((CONV_HALO, D_FF), F32)],
        compiler_params=_params(),
    )(dx2, dn, up, gl, gg, fcw, w_down, g_post)


def _bwd_ffn_up(dup, x1, dx2, g_pre, w_up, ts, tasks=()):
    s = x1.shape[0]

    def body(dup_ref, x1_ref, dx2_ref, g_ref, wup_ref, dx1_ref, dg_ref):
        @pl.when(pl.program_id(0) == 0)
        def _():
            dg_ref[...] = jnp.zeros_like(dg_ref)

        _, xh, r = _rms_fwd(x1_ref[...], g_ref[...])
        dh2 = _dot_nt(dup_ref[:, 0:UP_SHARD], wup_ref[0])
        for j in range(1, N_CHIPS):
            dh2 = dh2 + _dot_nt(dup_ref[:, j * UP_SHARD:(j + 1) * UP_SHARD], wup_ref[j])
        dx, dg = _rms_bwd(xh, r, g_ref[...], dh2)
        dg_ref[...] += dg
        dx1_ref[...] = dx2_ref[...] + dx

    row = _rows(ts, D_MODEL)
    return _call(
        body, "bwd_ffn_up", (s // ts,),
        [_rows(ts, 2 * D_FF), row, row, _resident((1, D_MODEL)), _resident(w_up.shape)],
        [row, _acc((1, D_MODEL))], [_sds((s, D_MODEL)), _sds((1, D_MODEL))], [],
        (dup, x1, dx2, g_pre, w_up), tasks)


def _dw_up(h2, dup, ts, tasks=()):
    s = h2.shape[0]
    ts = min(DW_TILES * ts, s)

    def body(h2_ref, dup_ref, out_ref):
        @pl.when(pl.program_id(1) == 0)
        def _():
            out_ref[...] = jnp.zeros_like(out_ref)

        out_ref[0] += _dot_tn(h2_ref[...], dup_ref[...])

    return _call(
        body, "dw_up", (N_CHIPS, s // ts),
        [pl.BlockSpec((ts, D_MODEL), lambda j, i: (i, 0)), pl.BlockSpec((ts, UP_SHARD), lambda j, i: (i, j))],
        [pl.BlockSpec((1, D_MODEL, UP_SHARD), lambda j, i: (j, 0, 0))], [_sds((N_CHIPS, D_MODEL, UP_SHARD))], [],
        (h2, dup), tasks)


def _bwd_merge(dx1, mo, sp, sr, ypool, yrnn, g_post, w_o, ts, tasks=()):
    s = dx1.shape[0]

    def body(dx1_ref, mo_ref, sp_ref, sr_ref, yp_ref, yr_ref, g_ref, wo_ref,
             dgp_ref, dgr_ref, dyp_ref, dyr_ref, dwo_ref, dg_ref):
        @pl.when(pl.program_id(0) == 0)
        def _():
            dwo_ref[...] = jnp.zeros_like(dwo_ref)
            dg_ref[...] = jnp.zeros_like(dg_ref)

        _, xh, r = _rms_fwd(mo_ref[...], g_ref[...])
        dmo, dg = _rms_bwd(xh, r, g_ref[...], dx1_ref[...])
        dg_ref[...] += dg
        dmerged = _dot_nt(dmo, wo_ref[...])
        sp = sp_ref[...].astype(F32)
        sr = sr_ref[...].astype(F32)
        yp = yp_ref[...]
        yr = yr_ref[...]
        dwo_ref[...] += _dot_tn(sp * yp + sr * yr, dmo)
        dgp_ref[...] = (dmerged * yp * sp * (1.0 - sp)).astype(BF16)
        dgr_ref[...] = (dmerged * yr * sr * (1.0 - sr)).astype(BF16)
        dyp_ref[...] = (dmerged * sp).astype(BF16)
        dyr_ref[...] = (dmerged * sr).astype(BF16)

    row = _rows(ts, D_MODEL)
    return _call(
        body, "bwd_merge", (s // ts,),
        [row] * 6 + [_resident((1, D_MODEL)), _resident(w_o.shape)],
        [row] * 4 + [_acc(w_o.shape), _acc((1, D_MODEL))],
        [_sds((s, D_MODEL), BF16)] * 4 + [_sds(w_o.shape), _sds((1, D_MODEL))], [],
        (dx1, mo, sp, sr, ypool, yrnn, g_post, w_o), tasks)


def _bwd_rnn(dyr, urx, gl, gg, xc, r, ig, h, conv_w, wg, lam, w_rg_out, ts, tasks=()):
    s = urx.shape[0]
    nt = s // ts
    halo_blocks = ts // CONV_HALO

    def body(dyr_ref, urx_ref, gl_ref, gg_ref, xc_ref, r_ref, ig_ref, h_ref, hh_ref, cw_ref, wg_ref, lam_ref, wrg_ref,
             durx_ref, durg_ref, dwrg_ref, dwg_ref, dcw_ref, dcb_ref, dbg_ref, dlam_ref, mu_scr, carry_scr):
        i = pl.program_id(0)
        k = nt - 1 - i

        @pl.when(i == 0)
        def _():
            mu_scr[...] = jnp.zeros_like(mu_scr)
            carry_scr[...] = jnp.zeros_like(carry_scr)
            dwrg_ref[...] = jnp.zeros_like(dwrg_ref)
            dwg_ref[...] = jnp.zeros_like(dwg_ref)
            dcw_ref[...] = jnp.zeros_like(dcw_ref)
            dcb_ref[...] = jnp.zeros_like(dcb_ref)
            dbg_ref[...] = jnp.zeros_like(dbg_ref)
            dlam_ref[...] = jnp.zeros_like(dlam_ref)

        row = lax.broadcasted_iota(jnp.int32, (ts, 1), 0)
        first_row = (k * ts + row) == 0
        h = h_ref[...]
        dyr_v = dyr_ref[...]
        dhr = _dot_nt(dyr_v, wrg_ref[...])
        gl = gl_ref[...].astype(F32)
        dwrg_ref[...] += _dot_tn(h * gl, dyr_v)
        durg_ref[...] = (dhr * h * gg_ref[...].astype(F32)).astype(BF16)
        r_v = r_ref[...]
        ig_v = ig_ref[...]
        xc_v = xc_ref[...]
        lam_v = lam_ref[...]
        c8, a, m2, mult = _lru_coeffs(r_v, lam_v, first_row)
        b = jnp.where(row == ts - 1, 1.0, pltpu.roll(a, ts - 1, 0))
        lt = _scan_bwd(b, dhr * gl, mu_scr[0:1, :])
        mu_scr[0:1, :] = a[0:1, :] * lt[0:1, :]
        h_before = jnp.where(k > 0, hh_ref[CONV_HALO - 1:CONV_HALO, :], 0.0)
        hprev = jnp.where(row == 0, h_before, pltpu.roll(h, 1, 0))
        dmult = lt * ig_v * xc_v
        da = lt * hprev - jnp.where(first_row, 0.0, dmult * a * lax.rsqrt(m2))
        dla = da * a
        dlam_ref[...] += jnp.sum(dla * r_v, axis=0, keepdims=True)
        dlr = (dla * (-c8)) * r_v * (1.0 - r_v)
        dli = (lt * mult * xc_v) * ig_v * (1.0 - ig_v)
        dbg_ref[0:1, :] += jnp.sum(dlr, axis=0, keepdims=True)
        dbg_ref[1:2, :] += jnp.sum(dli, axis=0, keepdims=True)
        xcb = xc_v.astype(BF16)
        parts = []
        for q in range(GATE_BLOCKS):
            blk = slice(q * GATE_BLOCK, (q + 1) * GATE_BLOCK)
            dlr_q = dlr[:, blk].astype(BF16)
            dli_q = dli[:, blk].astype(BF16)
            parts.append(_dot_nt(dlr_q, wg_ref[0, q]) + _dot_nt(dli_q, wg_ref[1, q]))
            dwg_ref[0, q] += _dot_tn(xcb[:, blk], dlr_q)
            dwg_ref[1, q] += _dot_tn(xcb[:, blk], dli_q)
        dxc = lt * mult * ig_v + jnp.concatenate(parts, axis=1)
        extd = jnp.concatenate([dxc, carry_scr[...]], axis=0)
        carry_scr[...] = dxc[0:CONV_HALO, :]
        cw = cw_ref[...]
        urx_v = urx_ref[...]
        durx = cw[3:4] * dxc
        dcw_ref[3:4, :] += jnp.sum(urx_v * dxc, axis=0, keepdims=True)
        for j in (1, 2, 3):
            dj = _shift_up(extd, j, ts)
            durx = durx + cw[3 - j:4 - j] * dj
            dcw_ref[3 - j:4 - j, :] += jnp.sum(urx_v * dj, axis=0, keepdims=True)
        durx_ref[...] = durx.astype(BF16)
        dcb_ref[...] += jnp.sum(dxc, axis=0, keepdims=True)

        @pl.when(i == nt - 1)
        def _():
            dlam_ref[...] = dlam_ref[...] * (LRU_C * jax.nn.sigmoid(-lam_v))

    row_spec = _rows(ts, D_MODEL, nt)
    halo_spec = pl.BlockSpec((CONV_HALO, D_MODEL), lambda i: (jnp.maximum((nt - 1 - i) * halo_blocks - 1, 0), 0))
    vec = _acc((1, D_MODEL))
    return _call(
        body, "bwd_rnn", (nt,),
        [row_spec] * 8 + [halo_spec, _resident(conv_w.shape), _resident(wg.shape), _resident((1, D_MODEL)),
                          _resident(w_rg_out.shape)],
        [row_spec, row_spec, _acc(w_rg_out.shape), _acc(wg.shape), _acc(conv_w.shape), vec, _acc((2, D_MODEL)), vec],
        [_sds((s, D_MODEL), BF16), _sds((s, D_MODEL), BF16), _sds(w_rg_out.shape), _sds(wg.shape), _sds(conv_w.shape),
         _sds((1, D_MODEL)), _sds((2, D_MODEL)), _sds((1, D_MODEL))],
        [pltpu.VMEM((8, D_MODEL), F32), pltpu.VMEM((CONV_HALO, D_MODEL), F32)],
        (dyr, urx, gl, gg, xc, r, ig, h, h, conv_w, wg, lam, w_rg_out), tasks)


def _bwd_pool(dyp, d, pool_w, pool_scale, w_pool_out, ts, tasks=()):
    s = d.shape[0]
    nt = s // ts

    def body(dyp_ref, d_ref, pw_ref, ps_ref, wpo_ref, dzp_ref, dwpo_ref, dpw_ref, dps_ref, carry_scr):
        i = pl.program_id(0)
        k = nt - 1 - i

        @pl.when(i == 0)
        def _():
            carry_scr[...] = jnp.zeros_like(carry_scr)
            dwpo_ref[...] = jnp.zeros_like(dwpo_ref)
            dpw_ref[...] = jnp.zeros_like(dpw_ref)
            dps_ref[...] = jnp.zeros_like(dps_ref)

        dyp_v = dyp_ref[...]
        d_v = d_ref[...]
        ps = ps_ref[...]
        dypre = _dot_nt(dyp_v[:, 0:256], wpo_ref[0])
        for j in range(1, N_CHIPS):
            dypre = dypre + _dot_nt(dyp_v[:, j * 256:(j + 1) * 256], wpo_ref[j])
        y4 = jnp.concatenate([_dot(d_v[:, g * 128:(g + 1) * 128], pw_ref[g]) for g in range(POOL_GROUPS)], axis=1)
        ypre = (y4 * ps).astype(BF16)
        for j in range(N_CHIPS):
            dwpo_ref[j] += _dot_tn(ypre, dyp_v[:, j * 256:(j + 1) * 256])
        dps_ref[...] += jnp.sum(dypre * y4, axis=0, keepdims=True)
        dy4 = dypre * ps
        t = k * ts + lax.broadcasted_iota(jnp.int32, (ts, 1), 0)
        for g, w in enumerate(POOL_WINDOWS):
            lanes = slice(g * POOL_GROUP_DIM, (g + 1) * POOL_GROUP_DIM)
            dd = _dot_nt(dy4[:, lanes], pw_ref[g])
            dpw_ref[g] += _dot_tn(d_v[:, lanes], dy4[:, lanes])
            e = dd * (1.0 / jnp.minimum(t + 1, w).astype(F32))
            acc = jnp.concatenate([e, carry_scr[:, lanes]], axis=0)
            carry_scr[:, lanes] = e[0:POOL_HALO, :]
            n = ts + POOL_HALO
            sh = 1
            while sh < w:
                acc = acc + pltpu.roll(acc, n - sh, 0)
                sh *= 2
            dzp_ref[:, lanes] = (acc[:ts, :] - dd).astype(BF16)

    return _call(
        body, "bwd_pool", (nt,),
        [_rows(ts, D_MODEL, nt), _rows(ts, POOL_WIDTH, nt), _resident(pool_w.shape), _resident((1, POOL_WIDTH)),
         _resident(w_pool_out.shape)],
        [_rows(ts, POOL_WIDTH, nt), _acc(w_pool_out.shape), _acc(pool_w.shape), _acc((1, POOL_WIDTH))],
        [_sds((s, POOL_WIDTH), BF16), _sds(w_pool_out.shape), _sds(pool_w.shape), _sds((1, POOL_WIDTH))],
        [pltpu.VMEM((POOL_HALO, POOL_WIDTH), F32)],
        (dyp, d, pool_w, pool_scale, w_pool_out), tasks)


def _assemble_dz(dz_scr, dzp_ref, durx_ref, durg_ref, dgp_ref, dgr_ref):
    dz_scr[:, 0:512] = dzp_ref[...]
    dz_scr[:, 512:1536] = durx_ref[...]
    dz_scr[:, 1536:2560] = durg_ref[...]
    dz_scr[:, 2560:3584] = dgp_ref[...]
    dz_scr[:, 3584:4608] = dgr_ref[...]


def _dw_in(h1, dzp, durx, durg, dgp, dgr, ts, tasks=()):
    s = h1.shape[0]
    ts = min(2 * ts, s)

    def body(h1_ref, dzp_ref, durx_ref, durg_ref, dgp_ref, dgr_ref, out_ref, dz_scr):
        @pl.when(pl.program_id(0) == 0)
        def _():
            out_ref[...] = jnp.zeros_like(out_ref)

        _assemble_dz(dz_scr, dzp_ref, durx_ref, durg_ref, dgp_ref, dgr_ref)
        for j in range(N_CHIPS):
            out_ref[j] += _dot_tn(h1_ref[...], dz_scr[:, j * IN_SHARD:(j + 1) * IN_SHARD])

    row = _rows(ts, D_MODEL)
    shape = (N_CHIPS, D_MODEL, IN_SHARD)
    return _call(
        body, "dw_in", (s // ts,), [row, _rows(ts, POOL_WIDTH)] + [row] * 4, [_acc(shape)], [_sds(shape)],
        [pltpu.VMEM((ts, IN_TOTAL), BF16)], (h1, dzp, durx, durg, dgp, dgr), tasks)


def _bwd_in(dzp, durx, durg, dgp, dgr, x, dx1, g_pre, w_in, ts, tasks=()):
    s = x.shape[0]

    def body(dzp_ref, durx_ref, durg_ref, dgp_ref, dgr_ref, x_ref, dx1_ref, g_ref, win_ref, gx_ref, dg_ref, dz_scr):
        @pl.when(pl.program_id(0) == 0)
        def _():
            dg_ref[...] = jnp.zeros_like(dg_ref)

        _assemble_dz(dz_scr, dzp_ref, durx_ref, durg_ref, dgp_ref, dgr_ref)
        _, xh, r = _rms_fwd(x_ref[...], g_ref[...])
        dh1 = _dot_nt(dz_scr[:, 0:IN_SHARD], win_ref[0])
        for j in range(1, N_CHIPS):
            dh1 = dh1 + _dot_nt(dz_scr[:, j * IN_SHARD:(j + 1) * IN_SHARD], win_ref[j])
        dx, dg = _rms_bwd(xh, r, g_ref[...], dh1)
        dg_ref[...] += dg
        gx_ref[...] = dx1_ref[...] + dx

    row = _rows(ts, D_MODEL)
    return _call(
        body, "bwd_in", (s // ts,),
        [_rows(ts, POOL_WIDTH)] + [row] * 6 + [_resident((1, D_MODEL)), _resident(w_in.shape)],
        [row, _acc((1, D_MODEL))], [_sds((s, D_MODEL)), _sds((1, D_MODEL))],
        [pltpu.VMEM((ts, IN_TOTAL), BF16)], (dzp, durx, durg, dgp, dgr, x, dx1, g_pre, w_in), tasks)


def _place():
    x, y, c = lax.axis_index("x"), lax.axis_index("y"), lax.axis_index("c")
    others = [(1 - x, y), (x, 1 - y), (1 - x, 1 - y)]
    return x, y, c, 2 * x + y, others


def _remote(src, dst, send_sem, recv_sem, to):
    return pltpu.make_async_remote_copy(src_ref=src, dst_ref=dst, send_sem=send_sem, recv_sem=recv_sem,
                                        device_id=to, device_id_type=MESH)


def _own_slots(ws, dtypes, name, tasks=()):
    n = len(ws)
    hbm = pl.BlockSpec(memory_space=pltpu.HBM)

    def body(*refs):
        srcs, outs, f32_bufs, cast_bufs, sems = refs[:n], refs[n:2 * n], refs[2 * n:3 * n], refs[3 * n:4 * n], refs[4 * n]
        me = _place()[3]
        loads = [pltpu.make_async_copy(srcs[k], f32_bufs[k], sems.at[k, 0]) for k in range(n)]
        stores = [pltpu.make_async_copy(cast_bufs[k], outs[k].at[me], sems.at[k, 1]) for k in range(n)]
        for cp in loads:
            cp.start()
        for k in range(n):
            loads[k].wait()
            cast_bufs[k][...] = f32_bufs[k][...].astype(dtypes[k])
            stores[k].start()
        for cp in stores:
            cp.wait()

    return _call(
        body, name, (), [hbm] * n, [hbm] * n, [_sds((N_CHIPS,) + w.shape, dt) for w, dt in zip(ws, dtypes)],
        [pltpu.VMEM(w.shape, F32) for w in ws] + [pltpu.VMEM(w.shape, dt) for w, dt in zip(ws, dtypes)]
        + [pltpu.SemaphoreType.DMA((n, 2))],
        [pltpu.with_memory_space_constraint(w, pltpu.HBM) for w in ws], tasks)


def _run(tasks, name):
    if isinstance(tasks, _Task):
        return _call(None, name, (), [], [], [], [], (), (tasks,))[1][0]
    return _call(None, name, (), [], [], [], [], (), tuple(tasks))[1]


def _gather_task(bufs, relay_steps=(0, 0)):
    n = len(bufs)
    NBR_X, NBR_Y, QUARTER_VIA_Y, QUARTER_VIA_X, SIB_X, SIB_Y, SIB_DIAG = range(7)

    def parts(out):
        x, y, c, me, _ = _place()
        ah = out.shape[1] // 2
        q = ah // 2 if (ah // 2) % 16 == 0 else ah
        return c * ah, ah, q

    def copy(out, w, k, chip, row0, rows, to, sems):
        slot = out.at[chip, pl.ds(row0, rows)]
        return _remote(slot, slot, sems[0].at[w, k], sems[1].at[w, k], to)

    def plan(out, w, sems):
        x, y, c, me, _ = _place()
        row0, ah, q = parts(out)
        xn, yn, dg = 2 * (1 - x) + y, 2 * x + (1 - y), 2 * (1 - x) + (1 - y)
        to_x, to_y, sib = (1 - x, y, c), (x, 1 - y, c), (x, y, 1 - c)
        other = (1 - c) * ah
        cp = functools.partial(copy, out, w, sems=sems)
        sends = {NBR_X: cp(NBR_X, me, row0, ah, to_x), NBR_Y: cp(NBR_Y, me, row0, ah, to_y),
                 QUARTER_VIA_Y: cp(QUARTER_VIA_Y, xn, row0, q, to_y), SIB_X: cp(SIB_X, xn, row0, ah, sib),
                 SIB_Y: cp(SIB_Y, yn, row0, ah, sib), SIB_DIAG: cp(SIB_DIAG, dg, row0, ah, sib)}
        lands = {NBR_X: cp(NBR_X, xn, row0, ah, to_x), NBR_Y: cp(NBR_Y, yn, row0, ah, to_y),
                 QUARTER_VIA_Y: cp(QUARTER_VIA_Y, dg, row0, q, to_y), SIB_X: cp(SIB_X, xn, other, ah, sib),
                 SIB_Y: cp(SIB_Y, yn, other, ah, sib), SIB_DIAG: cp(SIB_DIAG, dg, other, ah, sib)}
        if q < ah:
            sends[QUARTER_VIA_X] = cp(QUARTER_VIA_X, yn, row0 + q, ah - q, to_x)
            lands[QUARTER_VIA_X] = cp(QUARTER_VIA_X, dg, row0 + q, ah - q, to_x)
        return sends, lands

    def start(ins, outs, sems):
        for w, out in enumerate(outs):
            sends, _ = plan(out, w, sems)
            sends[NBR_X].start()
            sends[NBR_Y].start()

    def pass_neighbours(ins, outs, sems):
        for w, out in enumerate(outs):
            sends, lands = plan(out, w, sems)
            lands[NBR_X].wait_recv()
            sends[QUARTER_VIA_Y].start()
            sends[SIB_X].start()
            lands[NBR_Y].wait_recv()
            if QUARTER_VIA_X in sends:
                sends[QUARTER_VIA_X].start()
            sends[SIB_Y].start()

    def pass_diagonal(ins, outs, sems):
        for w, out in enumerate(outs):
            sends, lands = plan(out, w, sems)
            lands[QUARTER_VIA_Y].wait_recv()
            if QUARTER_VIA_X in lands:
                lands[QUARTER_VIA_X].wait_recv()
            sends[SIB_DIAG].start()

    def finish(ins, outs, sems):
        for w, out in enumerate(outs):
            sends, lands = plan(out, w, sems)
            for k in (SIB_X, SIB_Y, SIB_DIAG):
                lands[k].wait_recv()
        for w, out in enumerate(outs):
            sends, _ = plan(out, w, sems)
            for cp in sends.values():
                cp.wait_send()

    return _Task(bufs, [_sds(b.shape, b.dtype) for b in bufs], {i: i for i in range(n)},
                 [pltpu.SemaphoreType.DMA((n, 7)), pltpu.SemaphoreType.DMA((n, 7))], start, finish,
                 [(pass_neighbours, relay_steps[0]), (pass_diagonal, relay_steps[1])])


def _halves_task(grads):
    n = len(grads)

    def copy(src, out, w, sems):
        x, y, c, _, _ = _place()
        ah = out.shape[1]
        return _remote(src.at[:, pl.ds((1 - c) * ah, ah)], out, sems[0].at[w], sems[1].at[w], (x, y, 1 - c))

    def start(ins, outs, sems):
        for w, (src, out) in enumerate(zip(ins, outs)):
            copy(src, out, w, sems).start()

    def finish(ins, outs, sems):
        for w, (src, out) in enumerate(zip(ins, outs)):
            copy(src, out, w, sems).wait()

    return _Task(grads, [_sds((g.shape[0], g.shape[1] // 2, g.shape[2]), g.dtype) for g in grads], {},
                 [pltpu.SemaphoreType.DMA((n,)), pltpu.SemaphoreType.DMA((n,))], start, finish)


def _exchange_task(sends, accs):
    n = len(accs)
    given = [s for s in sends if s is not None]

    def copies(ins, outs, sems):
        send_refs = iter(ins[:len(given)])
        srcs = [next(send_refs) if s is not None else None for s in sends]
        x, y, c, me, others = _place()
        for w, out in enumerate(outs):
            for j, (ox, oy) in enumerate(others):
                src = out.at[me] if srcs[w] is None else srcs[w].at[2 * ox + oy]
                yield _remote(src, out.at[me], sems[0].at[w, j], sems[1].at[w, j], (ox, oy, c))

    def start(ins, outs, sems):
        for cp in copies(ins, outs, sems):
            cp.start()

    def finish(ins, outs, sems):
        x, y, c, _, others = _place()
        for w, out in enumerate(outs):
            for j, (ox, oy) in enumerate(others):
                slot = out.at[2 * ox + oy]
                _remote(slot, slot, sems[0].at[w, j], sems[1].at[w, j], (ox, oy, c)).wait_recv()
        for cp in copies(ins, outs, sems):
            cp.wait_send()

    return _Task(given + list(accs), [_sds(a.shape, a.dtype) for a in accs], {len(given) + i: i for i in range(n)},
                 [pltpu.SemaphoreType.DMA((n, 3)), pltpu.SemaphoreType.DMA((n, 3))], start, finish)


def _swap_task(arrays):
    n = len(arrays)

    def copy(src, out, w, sems):
        x, y, c, _, _ = _place()
        return _remote(src, out, sems[0].at[w], sems[1].at[w], (x, y, 1 - c))

    def start(ins, outs, sems):
        for w, (src, out) in enumerate(zip(ins, outs)):
            copy(src, out, w, sems).start()

    def finish(ins, outs, sems):
        for w, (src, out) in enumerate(zip(ins, outs)):
            copy(src, out, w, sems).wait()

    return _Task(arrays, [_sds(a.shape, a.dtype) for a in arrays], {},
                 [pltpu.SemaphoreType.DMA((n,)), pltpu.SemaphoreType.DMA((n,))], start, finish)


def _all_devices_task(arrays):
    n = len(arrays)
    flips = [(dx, dy, dc) for dx in (0, 1) for dy in (0, 1) for dc in (0, 1)][1:]

    def peers():
        x, y, c, _, _ = _place()
        flip = lambda v, d: 1 - v if d else v
        return 4 * x + 2 * y + c, [(flip(x, dx), flip(y, dy), flip(c, dc)) for dx, dy, dc in flips]

    def start(ins, outs, sems):
        me, others = peers()
        for w, (src, out) in enumerate(zip(ins, outs)):
            pltpu.make_async_copy(src, out.at[me], sems[2].at[w]).start()
            for k, peer in enumerate(others):
                _remote(src, out.at[me], sems[0].at[w, k], sems[1].at[w, k], peer).start()

    def finish(ins, outs, sems):
        me, others = peers()
        for w, (src, out) in enumerate(zip(ins, outs)):
            for k, (px, py, pc) in enumerate(others):
                slot = out.at[4 * px + 2 * py + pc]
                _remote(slot, slot, sems[0].at[w, k], sems[1].at[w, k], (px, py, pc)).wait_recv()
            for k, peer in enumerate(others):
                _remote(src, out.at[me], sems[0].at[w, k], sems[1].at[w, k], peer).wait_send()
            pltpu.make_async_copy(src, out.at[me], sems[2].at[w]).wait()

    return _Task(arrays, [_sds((8,) + a.shape, a.dtype) for a in arrays], {},
                 [pltpu.SemaphoreType.DMA((n, 7)), pltpu.SemaphoreType.DMA((n, 7)), pltpu.SemaphoreType.DMA((n,))],
                 start, finish)


def _share_task(shares):
    n = len(shares)

    def copy(out, w, sems, slot):
        x, y, c, _, _ = _place()
        return _remote(out.at[slot], out.at[slot], sems[0].at[w], sems[1].at[w], (x, y, 1 - c))

    def start(ins, outs, sems):
        c = _place()[2]
        for w, out in enumerate(outs):
            copy(out, w, sems, c).start()

    def finish(ins, outs, sems):
        c = _place()[2]
        for w, out in enumerate(outs):
            copy(out, w, sems, 1 - c).wait_recv()
        for w, out in enumerate(outs):
            copy(out, w, sems, c).wait_send()

    return _Task(shares, [_sds(s.shape, s.dtype) for s in shares], {i: i for i in range(n)},
                 [pltpu.SemaphoreType.DMA((n,)), pltpu.SemaphoreType.DMA((n,))], start, finish)


TILE_BYTES = 2 * 1024 * 1024
PARTIAL_TILE_BYTES = 1024 * 1024


def _in_hbm(t):
    return pltpu.with_memory_space_constraint(t, pltpu.HBM)


def _row_tile(rows, cols, limit=TILE_BYTES):
    best = 8
    for tr in range(8, rows + 1, 8):
        if rows % tr == 0 and tr * cols * 4 <= limit:
            best = tr
    assert rows % best == 0, (rows, cols)
    return best


def _chip_partial(g, got, place, wire_dtype):
    ns, ah, b = got.shape
    sharded = ns == N_CHIPS
    tr = _row_tile(ah, b, PARTIAL_TILE_BYTES)
    nb = ah // tr

    def body(place_ref, *refs):
        g_refs, got_refs, outs = refs[:ns], refs[ns:2 * ns], refs[2 * ns:]
        parts = [g_refs[k][0] + got_refs[k][0] for k in range(ns)]
        own = parts[0]
        if sharded:
            for k in range(ns):
                outs[0][k] = parts[k].astype(wire_dtype)
                if k:
                    own = jnp.where(place_ref[0] == k, parts[k], own)
        outs[-1][0] = own.astype(wire_dtype)

    blk = (1, tr, b)
    in_specs = ([pl.BlockSpec(blk, lambda i, s, k=k: (k, s[1] * nb + i, 0)) for k in range(ns)]
                + [pl.BlockSpec(blk, lambda i, s, k=k: (k, i, 0)) for k in range(ns)])
    acc_spec = pl.BlockSpec(blk, lambda i, s: (s[0], i, 0))
    acc_shape = _sds((N_CHIPS, ah, b), wire_dtype)
    out = pl.pallas_call(
        body, name="grad_chip_partial",
        grid_spec=pltpu.PrefetchScalarGridSpec(
            num_scalar_prefetch=1, grid=(nb,), in_specs=in_specs,
            out_specs=[pl.BlockSpec((ns, tr, b), lambda i, s: (0, i, 0)), acc_spec] if sharded else [acc_spec]),
        out_shape=[acc_shape, acc_shape] if sharded else [acc_shape],
        compiler_params=pltpu.CompilerParams(dimension_semantics=("arbitrary",), vmem_limit_bytes=VMEM_LIMIT),
    )(place, *([g] * ns), *([got] * ns))
    return (out[0], out[1]) if sharded else (None, out[0])


def _chip_sum(acc, place):
    _, ah, b = acc.shape
    tr = _row_tile(ah, b)

    def body(place_ref, p_ref, out_ref):
        total = p_ref[0].astype(F32) + p_ref[1].astype(F32)
        total = total + p_ref[2].astype(F32)
        out_ref[0] = total + p_ref[3].astype(F32)

    return pl.pallas_call(
        body, name="grad_chip_sum",
        grid_spec=pltpu.PrefetchScalarGridSpec(
            num_scalar_prefetch=1, grid=(ah // tr,),
            in_specs=[pl.BlockSpec((N_CHIPS, tr, b), lambda i, s: (0, i, 0))],
            out_specs=pl.BlockSpec((1, tr, b), lambda i, s: (s[1], i, 0))),
        out_shape=_sds((2, ah, b)),
        compiler_params=pltpu.CompilerParams(dimension_semantics=("arbitrary",)),
    )(place, _in_hbm(acc))


def _adam_math(w, g, m, v):
    nm = ADAM_B1 * m + (1.0 - ADAM_B1) * g
    nv = ADAM_B2 * v + (1.0 - ADAM_B2) * (g * g)
    m_hat = nm / (1.0 - ADAM_B1 ** ADAM_STEP)
    v_hat = nv / (1.0 - ADAM_B2 ** ADAM_STEP)
    return -ADAM_LR * (m_hat / (jnp.sqrt(v_hat) + ADAM_EPS) + ADAM_WD * w), nm, nv


def _adamw(w, g, m, v):
    a, b = w.shape
    tr = _row_tile(a, b)

    def body(w_ref, g_ref, m_ref, v_ref, g_out, d_ref, nm_ref, nv_ref):
        g_out[...] = g_ref[...]
        d_ref[...], nm_ref[...], nv_ref[...] = _adam_math(w_ref[...], g_ref[...], m_ref[...], v_ref[...])

    blk = pl.BlockSpec((tr, b), lambda i: (i, 0))
    return pl.pallas_call(
        body, name="adamw", grid=(a // tr,),
        in_specs=[blk] * 4, out_specs=[blk] * 4, out_shape=[_sds((a, b))] * 4,
        compiler_params=pltpu.CompilerParams(dimension_semantics=("arbitrary",)),
    )(w, g, m, v)


def _adamw_sum(w, m, v, acc, got, place):
    a, b = w.shape
    ah = a // 2
    tr = _row_tile(ah, b)
    nb = ah // tr

    def body(place_ref, w_ref, m_ref, v_ref, acc_ref, got_ref, g_out, d_ref, nm_ref, nv_ref):
        mine = (pl.program_id(0) // nb) == place_ref[1]
        part = lambda k: jnp.where(mine, acc_ref[k], got_ref[k]).astype(F32)
        g = part(0) + part(1)
        g = g + part(2)
        g = g + part(3)
        g_out[...] = g
        d_ref[...], nm_ref[...], nv_ref[...] = _adam_math(w_ref[...], g, m_ref[...], v_ref[...])

    blk = pl.BlockSpec((tr, b), lambda i, s: (i, 0))
    mine_spec = pl.BlockSpec((N_CHIPS, tr, b), lambda i, s: (0, jnp.where(i // nb == s[1], i % nb, 0), 0))
    got_spec = pl.BlockSpec((N_CHIPS, tr, b), lambda i, s: (0, jnp.where(i // nb == s[1], 0, i % nb), 0))
    return pl.pallas_call(
        body, name="adamw_sum",
        grid_spec=pltpu.PrefetchScalarGridSpec(
            num_scalar_prefetch=1, grid=(a // tr,), in_specs=[blk] * 3 + [mine_spec, got_spec], out_specs=[blk] * 4),
        out_shape=[_sds((a, b))] * 4,
        compiler_params=pltpu.CompilerParams(dimension_semantics=("arbitrary",), vmem_limit_bytes=VMEM_LIMIT),
    )(place, w, m, v, _in_hbm(acc), _in_hbm(got))


def _adamw_pieces(g, pieces, name):
    n = len(pieces)

    def body(g_ref, *refs):
        def grad(rows, cols):
            if len(g_ref.shape) == 2:
                return g_ref[rows, cols]
            total = g_ref[0, rows, cols]
            for k in range(1, g_ref.shape[0]):
                total = total + g_ref[k, rows, cols]
            return total

        ins, outs = refs[:3 * n], refs[3 * n:]
        for i, piece in enumerate(pieces):
            w_ref, m_ref, v_ref = ins[3 * i:3 * i + 3]
            o_g, o_d, o_m, o_v = outs[4 * i:4 * i + 4]
            if len(piece) == 5:
                g_v = grad(piece[3], piece[4])
                o_g[...] = g_v
                o_d[...], o_m[...], o_v[...] = _adam_math(w_ref[...], g_v, m_ref[...], v_ref[...])
            else:
                for r in range(w_ref.shape[1] // SMALL_COLS):
                    lanes = slice(r * SMALL_COLS, (r + 1) * SMALL_COLS)
                    g_v = grad(slice(piece[3] + r, piece[3] + r + 1), slice(None))
                    o_g[:, lanes] = g_v
                    o_d[:, lanes], o_m[:, lanes], o_v[:, lanes] = _adam_math(w_ref[:, lanes], g_v, m_ref[:, lanes],
                                                                            v_ref[:, lanes])

    operands = [t for piece in pieces for t in piece[:3]]
    out = pl.pallas_call(
        body, name=name,
        out_shape=[_sds(piece[0].shape) for piece in pieces for _ in range(4)],
    )(g, *operands)
    return [tuple(out[4 * i:4 * i + 4]) for i in range(n)]


TINY_ROWS, TINY_COLS = 16, 768
SMALL_COLS = 128
SMALL_ROWS = 624


def _pack_tiny(conv_w, b_gates, fcw):
    ns = conv_w.shape[0]
    pad = lambda t: jnp.pad(t, ((0, 0), (0, 0), (0, TINY_COLS - t.shape[2])))
    z = lambda rows: jnp.zeros((ns, rows, TINY_COLS), F32)
    return jnp.concatenate([pad(conv_w), pad(b_gates), z(2), fcw, z(5)], axis=1)


def _unpack_tiny(t):
    return t[:, 0:4, 0:256], t[:, 4:6, 0:256], t[:, 8:11, :]


def _cols_to_shards(t, n):
    return t.reshape(t.shape[0], N_CHIPS, n).transpose(1, 0, 2)


def _shards_to_cols(t):
    return t.transpose(1, 0, 2).reshape(t.shape[1], -1)


_VECTORS = ("g_mix_post", "conv_b", "lru_lambda", "g_ffn_pre", "g_ffn_post", "g_ple_gate", "g_ple_post", "pool_scale",
            "ffn_conv_b")
_VECTOR_LEN = {"pool_scale": POOL_WIDTH, "ffn_conv_b": D_FF}
POOL_W_ROWS = POOL_GROUPS * POOL_GROUP_DIM


def _vector_rows():
    rows, row = {}, POOL_W_ROWS
    for k in _VECTORS:
        rows[k] = row
        row += max(8, _VECTOR_LEN.get(k, D_MODEL) // SMALL_COLS)
    return rows, row


def _pack_small(grads, loss):
    tiles = lambda t: jnp.pad(t, ((0, -t.shape[0] % 8), (0, 0)))
    parts = [grads["pool_w"].reshape(POOL_W_ROWS, SMALL_COLS)] + [tiles(grads[k].reshape(-1, SMALL_COLS)) for k in _VECTORS]
    parts.append(tiles(loss))
    used = sum(t.shape[0] for t in parts)
    return jnp.concatenate(parts + [jnp.zeros((SMALL_ROWS - used, SMALL_COLS), F32)], axis=0)


def _gates_block_diag(w):
    w4 = w.reshape(2, GATE_BLOCKS, 4, RNN_HEAD_DIM, RNN_HEAD_DIM)
    eye = jnp.eye(4, dtype=w.dtype)
    return jnp.einsum("gqhij,hk->gqhikj", w4, eye).reshape(2, GATE_BLOCKS, GATE_BLOCK, GATE_BLOCK)


def _gates_from_block_diag(dw):
    d6 = dw.reshape(2, GATE_BLOCKS, 4, RNN_HEAD_DIM, 4, RNN_HEAD_DIM)
    blocks = [d6[:, :, hh, :, hh, :] for hh in range(4)]
    return jnp.stack(blocks, axis=2).reshape(2, RNN_HEADS, RNN_HEAD_DIM, RNN_HEAD_DIM)


ROW_TILE = 256
DW_TILES = 4

_SHARDED = ("w_in", "w_pool_out", "w_rg_out", "w_o", "w_up", "w_down", "w_ple_gate", "w_ple_proj")
_WEIGHTS = ("g_mix_pre", "g_mix_post", "w_in", "pool_w", "pool_scale", "w_pool_out", "conv_w", "conv_b", "w_rg_gates",
            "b_rg_gates", "lru_lambda", "w_rg_out", "w_o", "g_ffn_pre", "g_ffn_post", "w_up", "ffn_conv_w", "ffn_conv_b",
            "w_down", "g_ple_gate", "w_ple_gate", "w_ple_proj", "g_ple_post")


def _wire_dtype(g):
    return BF16 if g.shape[1] >= 64 and g.shape[2] > SMALL_COLS else F32


def _partials(grads, got, place):
    parts = [_chip_partial(g, r, place, _wire_dtype(g)) for g, r in zip(grads, got)]
    return [send for send, _ in parts], [acc for _, acc in parts]


def _whole(both):
    return [b.reshape(2 * b.shape[1], b.shape[2]) for b in both]


def _step(x, p, tgt, rep, place, ts):
    vec = lambda k: rep[k].reshape(1, -1)
    pool_w = rep["pool_w"].astype(BF16)
    wg = _gates_block_diag(rep["w_rg_gates"]).astype(BF16)
    sq = lambda t: t.reshape(D_MODEL, D_MODEL)
    by4 = lambda t: t.reshape(N_CHIPS, -1, D_MODEL)

    first, ride1, ride2, ride3 = (("w_in", "w_pool_out", "tiny"), ("w_rg_out", "w_o", "w_down"), ("w_up",),
                                  ("w_ple_gate", "w_ple_proj"))
    later = ride1 + ride2 + ride3
    tiny = _pack_tiny(rep["conv_w"][None], rep["b_rg_gates"][None], rep["ffn_conv_w"][None])[0]
    own_first, _ = _own_slots([rep["w_in"], rep["w_pool_out"], tiny], [BF16, BF16, F32], "own_slots_first")
    own_later, (got,) = _own_slots([rep[k] for k in later], [BF16] * len(later), "own_slots_gather_first",
                                   [_gather_task(own_first)])
    own = dict(zip(later, own_later))
    full = dict(zip(first, got))
    conv_w, b_gates, fcw = [_shards_to_cols(t) for t in _unpack_tiny(full["tiny"])]

    (urx, urg, gp, gr, d, ypool, h1), (got,) = _fwd_in_pool(
        x, vec("g_mix_pre"), full["w_in"], pool_w, vec("pool_scale"), full["w_pool_out"], ts,
        [_gather_task([own[k] for k in ride1], relay_steps=(6, 2))])
    full.update(zip(ride1, got))
    w_rg_out, w_o, w_down = sq(full["w_rg_out"]), sq(full["w_o"]), full["w_down"].reshape(D_FF, D_MODEL)
    (xc, r, ig, h, yrnn, mo, x1, glr, ggr, sp, sr), (got,) = _fwd_rnn_merge(
        urx, urg, gp, gr, ypool, x, conv_w, vec("conv_b"), wg, b_gates, vec("lru_lambda"), w_rg_out, w_o,
        vec("g_mix_post"), ts, [_gather_task([own[k] for k in ride2], relay_steps=(7, 3))])
    full.update(zip(ride2, got))
    (up, gl, gg, h2, dn, x2), (got,) = _fwd_ffn(x1, vec("g_ffn_pre"), full["w_up"], fcw, vec("ffn_conv_b"), w_down,
                                               vec("g_ffn_post"), ts, [_gather_task([own[k] for k in ride3], relay_steps=(10, 5))])
    full.update(zip(ride3, got))
    dx2, loss, d_w_gate, d_w_proj, d_g_ple_gate, d_g_ple_post = _ple_loss(
        x2, p, tgt, vec("g_ple_gate"), sq(full["w_ple_gate"]), full["w_ple_proj"], vec("g_ple_post"), ts)
    dup, d_w_down, d_fcw, d_fcb, d_g_ffn_post = _bwd_ffn_down(dx2, dn, up, gl, gg, fcw, w_down, vec("g_ffn_post"), ts)

    names1, grads1 = ("w_ple_gate", "w_ple_proj", "w_down"), [by4(d_w_gate), d_w_proj, by4(d_w_down)]
    (dx1, d_g_ffn_pre), (got1,) = _bwd_ffn_up(dup, x1, dx2, vec("g_ffn_pre"), full["w_up"], ts, [_halves_task(grads1)])
    (d_w_up,), (accs1,) = _dw_up(h2, dup, ts, [_exchange_task(*_partials(grads1, got1, place))])
    (dgp, dgr, dyp, dyr, d_w_o, d_g_mix_post), (got2, theirs1) = _bwd_merge(
        dx1, mo, sp, sr, ypool, yrnn, vec("g_mix_post"), w_o, ts, [_halves_task([d_w_up]), _swap_task(accs1)])
    (durx, durg, d_w_rg_out, d_wg, d_conv_w, d_conv_b, d_b_gates, d_lam), (accs2,) = _bwd_rnn(
        dyr, urx, glr, ggr, xc, r, ig, h, conv_w, wg, vec("lru_lambda"), w_rg_out, ts,
        [_exchange_task(*_partials([d_w_up], got2, place))])
    names3 = ("w_o", "w_rg_out", "tiny", "w_rg_gates")
    grads3 = [by4(d_w_o), by4(d_w_rg_out),
              _pack_tiny(_cols_to_shards(d_conv_w, 256), _cols_to_shards(d_b_gates, 256), _cols_to_shards(d_fcw, 768)),
              _gates_from_block_diag(d_wg).reshape(1, 2 * RNN_HEADS * RNN_HEAD_DIM, RNN_HEAD_DIM)]
    (dzp, d_w_pool_out, d_pool_w, d_pool_scale), (got3, theirs2) = _bwd_pool(
        dyp, d, pool_w, vec("pool_scale"), full["w_pool_out"], ts, [_halves_task(grads3), _swap_task(accs2)])
    (d_w_in,), (accs3,) = _dw_in(h1, dzp, durx, durg, dgp, dgr, ts, [_exchange_task(*_partials(grads3, got3, place))])

    replicated = {"g_mix_post": d_g_mix_post, "conv_b": d_conv_b, "lru_lambda": d_lam, "g_ffn_pre": d_g_ffn_pre,
                  "g_ffn_post": d_g_ffn_post, "g_ple_gate": d_g_ple_gate, "g_ple_post": d_g_ple_post,
                  "pool_scale": d_pool_scale, "ffn_conv_b": d_fcb, "pool_w": d_pool_w}
    names4 = ("w_in", "w_pool_out", "small")
    grads4 = [d_w_in, d_w_pool_out, _pack_small(replicated, loss)[None]]
    got4 = _run(_halves_task(grads4), "grad_sibling_halves")
    (grad_x, d_g_mix_pre), (accs4, theirs3, both3) = _bwd_in(
        dzp, durx, durg, dgp, dgr, x, dx1, vec("g_mix_pre"), full["w_in"], ts,
        [_exchange_task(*_partials(grads4, got4, place)), _swap_task(accs3[:2]),
         _share_task([_chip_sum(acc, place) for acc in accs3[2:]])])
    theirs4, both4, (g_mix_pre_parts,) = _run(
        [_swap_task(accs4[:2]), _share_task([_chip_sum(acc, place) for acc in accs4[2:]]),
         _all_devices_task([d_g_mix_pre.reshape(SUBLANES, SMALL_COLS)])], "grad_sibling_share")
    mine = accs1 + accs2 + accs3[:2] + accs4[:2]
    partials = dict(zip(names1 + ("w_up",) + names3[:2] + names4[:2], zip(mine, theirs1 + theirs2 + theirs3 + theirs4)))
    return grad_x, partials, dict(zip(names3[2:] + names4[2:], _whole(both3) + _whole(both4))), g_mix_pre_parts


def kernel(x, p, g_mix_pre, g_mix_post, w_in, pool_w, pool_scale, w_pool_out, conv_w, conv_b, w_rg_gates, b_rg_gates, lru_lambda, w_rg_out, w_o, g_ffn_pre, g_ffn_post, w_up, ffn_conv_w, ffn_conv_b, w_down, g_ple_gate, w_ple_gate, w_ple_proj, g_ple_post, loss_target, m_g_mix_pre, m_g_mix_post, m_w_in, m_pool_w, m_pool_scale, m_w_pool_out, m_conv_w, m_conv_b, m_w_rg_gates, m_b_rg_gates, m_lru_lambda, m_w_rg_out, m_w_o, m_g_ffn_pre, m_g_ffn_post, m_w_up, m_ffn_conv_w, m_ffn_conv_b, m_w_down, m_g_ple_gate, m_w_ple_gate, m_w_ple_proj, m_g_ple_post, v_g_mix_pre, v_g_mix_post, v_w_in, v_pool_w, v_pool_scale, v_w_pool_out, v_conv_w, v_conv_b, v_w_rg_gates, v_b_rg_gates, v_lru_lambda, v_w_rg_out, v_w_o, v_g_ffn_pre, v_g_ffn_post, v_w_up, v_ffn_conv_w, v_ffn_conv_b, v_w_down, v_g_ple_gate, v_w_ple_gate, v_w_ple_proj, v_g_ple_post):
    args = dict(locals())
    w = {k: args[k][0] for k in _WEIGHTS}
    m = {k: args["m_" + k][0] for k in _WEIGHTS}
    v = {k: args["v_" + k][0] for k in _WEIGHTS}
    place = jnp.stack([2 * lax.axis_index("x") + lax.axis_index("y"), lax.axis_index("c")]).astype(jnp.int32)
    grad_x, partials, reduced, g_mix_pre_parts = _step(x[0], p[0, 0], loss_target[0], w, place, ROW_TILE)

    gates_2d = (2 * RNN_HEADS * RNN_HEAD_DIM, RNN_HEAD_DIM)
    as2d = lambda k, shape: tuple(t[k].reshape(shape) for t in (w, m, v))
    done = {k: tuple(_adamw_sum(w[k], m[k], v[k], *partials[k], place)) for k in _SHARDED}
    gates_w, gates_m, gates_v = as2d("w_rg_gates", gates_2d)
    done["w_rg_gates"] = tuple(_adamw(gates_w, reduced["w_rg_gates"], gates_m, gates_v))
    tiny_names = ("conv_w", "b_rg_gates", "ffn_conv_w")
    tiny_at = ((slice(0, 4), slice(0, 256)), (slice(4, 6), slice(0, 256)), (slice(8, 11), slice(None)))
    done.update(zip(tiny_names, _adamw_pieces(
        reduced["tiny"], [(w[k], m[k], v[k]) + at for k, at in zip(tiny_names, tiny_at)], "adamw_tiny")))
    vector_rows, loss_row = _vector_rows()
    pieces = [as2d("pool_w", (POOL_W_ROWS, SMALL_COLS)) + (slice(0, POOL_W_ROWS), slice(None))]
    pieces += [as2d(k, (1, -1)) + (vector_rows[k],) for k in _VECTORS]
    done.update(zip(("pool_w",) + _VECTORS, _adamw_pieces(reduced["small"], pieces, "adamw_small")))
    done["g_mix_pre"] = _adamw_pieces(g_mix_pre_parts, [as2d("g_mix_pre", (1, -1)) + (0,)], "adamw_g_mix_pre")[0]

    result = [reduced["small"][loss_row, 0], grad_x[None]]
    for kind in range(4):
        result += [done[k][kind].reshape(args[k].shape) for k in _WEIGHTS]
    return tuple(result)
```

```python
import functools

import jax
import jax.numpy as jnp
from jax import lax
from jax.experimental import pallas as pl
from jax.experimental.pallas import tpu as pltpu

F32 = jnp.float32
BF16 = jnp.bfloat16

D_MODEL = 1024
POOL_WINDOWS = (2, 4, 8, 16)
POOL_GROUPS = 4
POOL_WIDTH = 512
POOL_GROUP_DIM = 128
RNN_HEADS = 16
RNN_HEAD_DIM = 64
GATE_BLOCK = 256
GATE_BLOCKS = D_MODEL // GATE_BLOCK
LRU_C = 8.0
D_FF = 3072
PLE_DIM = 256
RMS_EPS = 1e-6
IN_TOTAL = 4608
N_CHIPS = 4
IN_SHARD = IN_TOTAL // N_CHIPS
UP_SHARD = 2 * D_FF // N_CHIPS
POOL_HALO = 16
CONV_HALO = 8

ADAM_LR = 0.001
ADAM_B1 = 0.9
ADAM_B2 = 0.999
ADAM_EPS = 1e-08
ADAM_WD = 0.01
ADAM_STEP = 10

VMEM_LIMIT = 56 * 1024 * 1024
MESH = pl.DeviceIdType.MESH

_GELU_C = 0.7978845608028654
_GELU_A = 0.044715


def _dot(a, b):
    return jnp.dot(a.astype(BF16), b.astype(BF16), preferred_element_type=F32)


def _dot_nt(a, b):
    return lax.dot_general(a.astype(BF16), b.astype(BF16), (((1,), (1,)), ((), ())), preferred_element_type=F32)


def _dot_tn(a, b):
    return lax.dot_general(a.astype(BF16), b.astype(BF16), (((0,), (0,)), ((), ())), preferred_element_type=F32)


def _rms_fwd(x, g):
    r = lax.rsqrt(jnp.mean(x * x, axis=-1, keepdims=True) + RMS_EPS)
    xh = x * r
    return xh * g, xh, r


def _rms_bwd(xh, r, g, dy):
    dxh = dy * g
    dg = jnp.sum(dy * xh, axis=0, keepdims=True)
    dx = r * (dxh - xh * jnp.mean(dxh * xh, axis=-1, keepdims=True))
    return dx, dg


def _sigmoid(x):
    return 0.5 * jnp.tanh(0.5 * x) + 0.5


def _gelu(x):
    t = jnp.tanh(_GELU_C * (x + _GELU_A * x * x * x))
    return 0.5 * x * (1.0 + t), t


def _gelu_grad(x, t):
    return 0.5 * (1.0 + t) + 0.5 * x * (1.0 - t * t) * _GELU_C * (1.0 + 3.0 * _GELU_A * x * x)


def _softplus_neg(lam):
    nl = -lam
    return jnp.maximum(nl, 0.0) + jnp.log(1.0 + jnp.exp(-jnp.abs(nl)))


def _lru_coeffs(r, lam, first_row):
    c8 = LRU_C * _softplus_neg(lam)
    la = -(c8 * r)
    a = jnp.exp(la)
    m2 = jnp.tanh(-la) * (1.0 + a * a)
    mult = jnp.where(first_row, 1.0, jnp.sqrt(m2))
    return c8, a, m2, mult


SUBLANES = 8


def _scan_fwd(a, u, carry):
    n = a.shape[0]
    sub = lax.broadcasted_iota(jnp.int32, (n, 1), 0) % SUBLANES
    acc_a, acc_h = a, u
    for s in (1, 2, 4):
        m = sub >= s
        h_s = jnp.where(m, pltpu.roll(acc_h, s, 0), 0.0)
        a_s = jnp.where(m, pltpu.roll(acc_a, s, 0), 1.0)
        acc_h = acc_a * h_s + acc_h
        acc_a = acc_a * a_s
    out = []
    for g in range(n // SUBLANES):
        rows = slice(g * SUBLANES, (g + 1) * SUBLANES)
        out.append(acc_h[rows] + acc_a[rows] * carry)
        carry = out[-1][SUBLANES - 1:SUBLANES]
    return jnp.concatenate(out, axis=0)


def _scan_bwd(b, g, carry):
    n = b.shape[0]
    sub = lax.broadcasted_iota(jnp.int32, (n, 1), 0) % SUBLANES
    acc_b, acc_l = b, g
    for s in (1, 2, 4):
        m = sub < SUBLANES - s
        l_s = jnp.where(m, pltpu.roll(acc_l, n - s, 0), 0.0)
        b_s = jnp.where(m, pltpu.roll(acc_b, n - s, 0), 1.0)
        acc_l = acc_b * l_s + acc_l
        acc_b = acc_b * b_s
    out = [None] * (n // SUBLANES)
    for g in reversed(range(n // SUBLANES)):
        rows = slice(g * SUBLANES, (g + 1) * SUBLANES)
        out[g] = acc_l[rows] + acc_b[rows] * carry
        carry = out[g][0:1]
    return jnp.concatenate(out, axis=0)


def _shift_down(ext, k, halo):
    return pltpu.roll(ext, k, 0)[halo:] if k else ext[halo:]


def _shift_up(ext, k, ts):
    return pltpu.roll(ext, ext.shape[0] - k, 0)[:ts] if k else ext[:ts]


def _rows(ts, width, nt=None, col=0):
    if nt is None:
        return pl.BlockSpec((ts, width), lambda i: (i, col))
    return pl.BlockSpec((ts, width), lambda i: (nt - 1 - i, col))


def _resident(shape):
    zeros = (0,) * len(shape)
    return pl.BlockSpec(shape, lambda i: zeros, pipeline_mode=pl.Buffered(1))


def _acc(shape):
    zeros = (0,) * len(shape)
    return pl.BlockSpec(shape, lambda i: zeros)


def _params():
    return pltpu.CompilerParams(dimension_semantics=("arbitrary",), vmem_limit_bytes=VMEM_LIMIT)


def _sds(shape, dtype=F32):
    return jax.ShapeDtypeStruct(shape, dtype)


class _Task:
    def __init__(self, ins, out_shapes, aliases, sems, start, finish, relays=()):
        self.ins, self.out_shapes, self.aliases, self.sems = list(ins), list(out_shapes), dict(aliases), list(sems)
        self.start, self.relays, self.finish = start, list(relays), finish


def _call(body, name, grid, in_specs, out_specs, out_shape, scratch_shapes, args, tasks=()):
    n_in, n_out, n_scr = len(in_specs), len(out_specs), len(scratch_shapes)
    t_in = [len(t.ins) for t in tasks]
    t_out = [len(t.out_shapes) for t in tasks]
    t_sem = [len(t.sems) for t in tasks]
    steps = 1
    for g in grid:
        steps *= g

    def take(refs, pos, counts):
        groups = []
        for c in counts:
            groups.append(refs[pos:pos + c])
            pos += c
        return groups, pos

    def wrapped(*refs):
        (cin,), pos = take(refs, 0, [n_in])
        tin, pos = take(refs, pos, t_in)
        (cout,), pos = take(refs, pos, [n_out])
        tout, pos = take(refs, pos, t_out)
        (cscr,), pos = take(refs, pos, [n_scr])
        tsem, pos = take(refs, pos, t_sem)
        if not grid:
            for t, a, b, c in zip(tasks, tin, tout, tsem):
                t.start(a, b, c)
            if body is not None:
                body(*cin, *cout, *cscr)
            for t, a, b, c in zip(tasks, tin, tout, tsem):
                for relay, _ in t.relays:
                    relay(a, b, c)
            for t, a, b, c in zip(tasks, tin, tout, tsem):
                t.finish(a, b, c)
            return
        step = pl.program_id(0)
        for axis in range(1, len(grid)):
            step = step * grid[axis] + pl.program_id(axis)
        if tasks:
            @pl.when(step == 0)
            def _():
                for t, a, b, c in zip(tasks, tin, tout, tsem):
                    t.start(a, b, c)

        body(*cin, *cout, *cscr)
        for t, a, b, c in zip(tasks, tin, tout, tsem):
            for relay, before in t.relays:
                pl.when(step == max(steps - 1 - before, 0))(functools.partial(relay, a, b, c))

        if tasks:
            @pl.when(step == steps - 1)
            def _():
                for t, a, b, c in zip(tasks, tin, tout, tsem):
                    t.finish(a, b, c)

    aliases, in_pos, out_pos = {}, n_in, n_out
    for t, ni, no in zip(tasks, t_in, t_out):
        aliases.update({in_pos + a: out_pos + b for a, b in t.aliases.items()})
        in_pos, out_pos = in_pos + ni, out_pos + no
    any_spec = pl.BlockSpec(memory_space=pltpu.HBM)
    kwargs = dict(grid=grid, compiler_params=pltpu.CompilerParams(
        dimension_semantics=("arbitrary",) * len(grid), vmem_limit_bytes=VMEM_LIMIT)) if grid else dict(
        compiler_params=pltpu.CompilerParams(vmem_limit_bytes=VMEM_LIMIT))
    out = pl.pallas_call(
        wrapped, name=name,
        in_specs=list(in_specs) + [any_spec] * sum(t_in),
        out_specs=list(out_specs) + [any_spec] * sum(t_out),
        out_shape=list(out_shape) + [s for t in tasks for s in t.out_shapes],
        scratch_shapes=list(scratch_shapes) + [s for t in tasks for s in t.sems],
        input_output_aliases=aliases, **kwargs,
    )(*args, *[pltpu.with_memory_space_constraint(a, pltpu.HBM) for t in tasks for a in t.ins])
    task_outs, pos = take(list(out), n_out, t_out)
    return list(out[:n_out]), task_outs


def _fwd_in_pool(x, g_pre, w_in, pool_w, pool_scale, w_pool_out, ts, tasks=()):
    s = x.shape[0]

    def body(x_ref, g_ref, win_ref, pw_ref, ps_ref, wpo_ref,
             urx_ref, urg_ref, gp_ref, gr_ref, d_ref, yp_ref, h1_ref, z_scr, halo_scr):
        i = pl.program_id(0)

        @pl.when(i == 0)
        def _():
            halo_scr[...] = jnp.zeros_like(halo_scr)

        h1, _, _ = _rms_fwd(x_ref[...], g_ref[...])
        h1 = h1.astype(BF16)
        h1_ref[...] = h1
        for j in range(N_CHIPS):
            z_scr[:, j * IN_SHARD:(j + 1) * IN_SHARD] = jnp.dot(h1, win_ref[j], preferred_element_type=F32)
        urx_ref[...] = z_scr[:, 512:1536]
        urg_ref[...] = z_scr[:, 1536:2560]
        gp_ref[...] = z_scr[:, 2560:3584]
        gr_ref[...] = z_scr[:, 3584:4608]
        u = z_scr[:, 0:POOL_WIDTH]
        ext = jnp.concatenate([halo_scr[...], u], axis=0)
        halo_scr[...] = u[ts - POOL_HALO:, :]
        t = i * ts + lax.broadcasted_iota(jnp.int32, (ts, 1), 0)
        y4 = []
        for g, w in enumerate(POOL_WINDOWS):
            lanes = slice(g * POOL_GROUP_DIM, (g + 1) * POOL_GROUP_DIM)
            acc = ext[:, lanes]
            sh = 1
            while sh < w:
                acc = acc + pltpu.roll(acc, sh, 0)
                sh *= 2
            inv = 1.0 / jnp.minimum(t + 1, w).astype(F32)
            dg = acc[POOL_HALO:, :] * inv - u[:, lanes]
            d_ref[:, lanes] = dg
            y4.append(_dot(dg, pw_ref[g]))
        ypre = jnp.concatenate(y4, axis=1) * ps_ref[...]
        ypre = ypre.astype(BF16)
        for j in range(N_CHIPS):
            yp_ref[:, j * 256:(j + 1) * 256] = jnp.dot(ypre, wpo_ref[j], preferred_element_type=F32)

    return _call(
        body, "fwd_in_pool", (s // ts,),
        [_rows(ts, D_MODEL), _resident((1, D_MODEL)), _resident(w_in.shape), _resident(pool_w.shape),
         _resident((1, POOL_WIDTH)), _resident(w_pool_out.shape)],
        [_rows(ts, D_MODEL)] * 4 + [_rows(ts, POOL_WIDTH), _rows(ts, D_MODEL), _rows(ts, D_MODEL)],
        [_sds((s, D_MODEL))] * 4 + [_sds((s, POOL_WIDTH)), _sds((s, D_MODEL)), _sds((s, D_MODEL), BF16)],
        [pltpu.VMEM((ts, IN_TOTAL), F32), pltpu.VMEM((POOL_HALO, POOL_WIDTH), F32)],
        (x, g_pre, w_in, pool_w, pool_scale, w_pool_out), tasks)


def _fwd_rnn_merge(urx, urg, gp, gr, ypool, x, conv_w, conv_b, wg, bg, lam, w_rg_out, w_o, g_post, ts, tasks=()):
    s = x.shape[0]

    def body(urx_ref, urg_ref, gp_ref, gr_ref, yp_ref, x_ref, cw_ref, cb_ref, wg_ref, bg_ref, lam_ref, wrg_ref, wo_ref,
             gpost_ref, xc_ref, r_ref, ig_ref, h_ref, yr_ref, mo_ref, x1_ref, gl_ref, gg_ref, sp_ref, sr_ref,
             halo_scr, carry_scr):
        i = pl.program_id(0)

        @pl.when(i == 0)
        def _():
            halo_scr[...] = jnp.zeros_like(halo_scr)
            carry_scr[...] = jnp.zeros_like(carry_scr)

        urx_v = urx_ref[...]
        ext = jnp.concatenate([halo_scr[...], urx_v], axis=0)
        halo_scr[...] = urx_v[ts - CONV_HALO:, :]
        cw = cw_ref[...]
        xc = (cb_ref[...] + cw[3:4] * urx_v + cw[2:3] * _shift_down(ext, 1, CONV_HALO)
              + cw[1:2] * _shift_down(ext, 2, CONV_HALO) + cw[0:1] * _shift_down(ext, 3, CONV_HALO))
        xc_ref[...] = xc
        xcb = xc.astype(BF16)
        lin = []
        for gate in range(2):
            parts = [jnp.dot(xcb[:, q * GATE_BLOCK:(q + 1) * GATE_BLOCK], wg_ref[gate, q], preferred_element_type=F32)
                     for q in range(GATE_BLOCKS)]
            lin.append(jnp.concatenate(parts, axis=1) + bg_ref[gate:gate + 1, :])
        r = _sigmoid(lin[0])
        ig = _sigmoid(lin[1])
        r_ref[...] = r
        ig_ref[...] = ig
        first_row = (i * ts + lax.broadcasted_iota(jnp.int32, (ts, 1), 0)) == 0
        _, a, _, mult = _lru_coeffs(r, lam_ref[...], first_row)
        h = _scan_fwd(a, mult * ig * xc, carry_scr[0:1, :])
        carry_scr[0:1, :] = h[ts - 1:ts, :]
        h_ref[...] = h
        urg_v = urg_ref[...]
        gl, t = _gelu(urg_v)
        gl_ref[...] = gl.astype(BF16)
        gg_ref[...] = _gelu_grad(urg_v, t).astype(BF16)
        yr = _dot(h * gl, wrg_ref[...])
        yr_ref[...] = yr
        sp = _sigmoid(gp_ref[...])
        sr = _sigmoid(gr_ref[...])
        sp_ref[...] = sp.astype(BF16)
        sr_ref[...] = sr.astype(BF16)
        merged = sp * yp_ref[...] + sr * yr
        mo = _dot(merged, wo_ref[...])
        mo_ref[...] = mo
        y, _, _ = _rms_fwd(mo, gpost_ref[...])
        x1_ref[...] = x_ref[...] + y

    row = _rows(ts, D_MODEL)
    return _call(
        body, "fwd_rnn_merge", (s // ts,),
        [row] * 6 + [_resident(conv_w.shape), _resident((1, D_MODEL)), _resident(wg.shape), _resident(bg.shape),
                     _resident((1, D_MODEL)), _resident(w_rg_out.shape), _resident(w_o.shape), _resident((1, D_MODEL))],
        [row] * 11, [_sds((s, D_MODEL))] * 7 + [_sds((s, D_MODEL), BF16)] * 4,
        [pltpu.VMEM((CONV_HALO, D_MODEL), F32), pltpu.VMEM((8, D_MODEL), F32)],
        (urx, urg, gp, gr, ypool, x, conv_w, conv_b, wg, bg, lam, w_rg_out, w_o, g_post), tasks)


def _fwd_ffn(x1, g_pre, w_up, fcw, fcb, w_down, g_post, ts, tasks=()):
    s = x1.shape[0]

    def body(x1_ref, g_ref, wup_ref, fcw_ref, fcb_ref, wd_ref, gpost_ref,
             up_ref, gl_ref, gg_ref, h2_ref, dn_ref, x2_ref, up_scr, halo_scr):
        i = pl.program_id(0)

        @pl.when(i == 0)
        def _():
            halo_scr[...] = jnp.zeros_like(halo_scr)

        x1_v = x1_ref[...]
        h2, _, _ = _rms_fwd(x1_v, g_ref[...])
        h2 = h2.astype(BF16)
        h2_ref[...] = h2
        for j in range(N_CHIPS):
            up_scr[:, j * UP_SHARD:(j + 1) * UP_SHARD] = jnp.dot(h2, wup_ref[j], preferred_element_type=F32)
        up_ref[...] = up_scr[...].astype(BF16)
        ug = up_scr[:, 0:D_FF]
        ext = jnp.concatenate([halo_scr[...], ug], axis=0)
        halo_scr[...] = ug[ts - CONV_HALO:, :]
        w = fcw_ref[...]
        gh = (fcb_ref[...] + w[2:3] * ug + w[1:2] * _shift_down(ext, 1, CONV_HALO)
              + w[0:1] * _shift_down(ext, 2, CONV_HALO))
        gl, t = _gelu(gh)
        gl_ref[...] = gl.astype(BF16)
        gg_ref[...] = _gelu_grad(gh, t).astype(BF16)
        dn = _dot(gl * up_scr[:, D_FF:], wd_ref[...])
        dn_ref[...] = dn
        y, _, _ = _rms_fwd(dn, gpost_ref[...])
        x2_ref[...] = x1_v + y

    row = _rows(ts, D_MODEL)
    return _call(
        body, "fwd_ffn", (s // ts,),
        [row, _resident((1, D_MODEL)), _resident(w_up.shape), _resident(fcw.shape), _resident((1, D_FF)),
         _resident(w_down.shape), _resident((1, D_MODEL))],
        [_rows(ts, 2 * D_FF), _rows(ts, D_FF), _rows(ts, D_FF), row, row, row],
        [_sds((s, 2 * D_FF), BF16), _sds((s, D_FF), BF16), _sds((s, D_FF), BF16), _sds((s, D_MODEL), BF16),
         _sds((s, D_MODEL)), _sds((s, D_MODEL))],
        [pltpu.VMEM((ts, 2 * D_FF), F32), pltpu.VMEM((CONV_HALO, D_FF), F32)],
        (x1, g_pre, w_up, fcw, fcb, w_down, g_post), tasks)


def _ple_loss(x2, p, tgt, g_gate, w_gate, w_proj, g_post, ts):
    s = x2.shape[0]

    def body(x2_ref, p_ref, t_ref, gg_ref, wg_ref, wp_ref, gp_ref, dx2_ref, loss_ref, dwg_ref, dwp_ref, dgg_ref, dgp_ref):
        @pl.when(pl.program_id(0) == 0)
        def _():
            loss_ref[...] = jnp.zeros_like(loss_ref)
            dwg_ref[...] = jnp.zeros_like(dwg_ref)
            dwp_ref[...] = jnp.zeros_like(dwp_ref)
            dgg_ref[...] = jnp.zeros_like(dgg_ref)
            dgp_ref[...] = jnp.zeros_like(dgp_ref)

        x2_v = x2_ref[...]
        n3, xh3, r3 = _rms_fwd(x2_v, gg_ref[...])
        pg = _sigmoid(_dot(n3, wg_ref[...]))
        pb = p_ref[...].astype(BF16)
        q = jnp.concatenate([jnp.dot(pb, wp_ref[j], preferred_element_type=F32) for j in range(N_CHIPS)], axis=1)
        ple, qh, rq = _rms_fwd(q, gp_ref[...])
        e = x2_v + pg * ple - t_ref[...]
        loss_ref[...] += 0.5 * jnp.sum(jnp.mean(e * e, axis=-1, keepdims=True), axis=0, keepdims=True)
        dy = e * (1.0 / D_MODEL)
        dpgl = dy * ple * pg * (1.0 - pg)
        dwg_ref[...] += _dot_tn(n3, dpgl)
        dx3, dgg = _rms_bwd(xh3, r3, gg_ref[...], _dot_nt(dpgl, wg_ref[...]))
        dgg_ref[...] += dgg
        dq, dgp = _rms_bwd(qh, rq, gp_ref[...], dy * pg)
        dgp_ref[...] += dgp
        for j in range(N_CHIPS):
            dwp_ref[j] += _dot_tn(pb, dq[:, j * 256:(j + 1) * 256])
        dx2_ref[...] = dy + dx3

    row = _rows(ts, D_MODEL)
    vec = _acc((1, D_MODEL))
    return pl.pallas_call(
        body, name="ple_loss", grid=(s // ts,),
        in_specs=[row, _rows(ts, PLE_DIM), row, _resident((1, D_MODEL)), _resident(w_gate.shape), _resident(w_proj.shape),
                  _resident((1, D_MODEL))],
        out_specs=[row, _acc((1, 128)), _acc(w_gate.shape), _acc(w_proj.shape), vec, vec],
        out_shape=[_sds((s, D_MODEL)), _sds((1, 128)), _sds(w_gate.shape), _sds(w_proj.shape), _sds((1, D_MODEL)),
                   _sds((1, D_MODEL))],
        compiler_params=_params(),
    )(x2, p, tgt, g_gate, w_gate, w_proj, g_post)


def _bwd_ffn_down(dx2, dn, up, gl, gg, fcw, w_down, g_post, ts):
    s = dx2.shape[0]
    nt = s // ts

    def body(dx2_ref, dn_ref, up_ref, gl_ref, gg_ref, fcw_ref, wd_ref, gpost_ref,
             dup_ref, dwd_ref, dfcw_ref, dfcb_ref, dgp_ref, carry_scr):
        i = pl.program_id(0)

        @pl.when(i == 0)
        def _():
            carry_scr[...] = jnp.zeros_like(carry_scr)
            dwd_ref[...] = jnp.zeros_like(dwd_ref)
            dfcw_ref[...] = jnp.zeros_like(dfcw_ref)
            dfcb_ref[...] = jnp.zeros_like(dfcb_ref)
            dgp_ref[...] = jnp.zeros_like(dgp_ref)

        _, xh, r = _rms_fwd(dn_ref[...], gpost_ref[...])
        ddn, dgp = _rms_bwd(xh, r, gpost_ref[...], dx2_ref[...])
        dgp_ref[...] += dgp
        dhid = _dot_nt(ddn, wd_ref[...])
        ug = up_ref[:, 0:D_FF].astype(F32)
        uv = up_ref[:, D_FF:].astype(F32)
        gl = gl_ref[...].astype(F32)
        w = fcw_ref[...]
        dwd_ref[...] += _dot_tn(gl * uv, ddn)
        dgh = dhid * uv * gg_ref[...].astype(F32)
        dup_ref[:, D_FF:] = (dhid * gl).astype(BF16)
        extd = jnp.concatenate([dgh, carry_scr[...]], axis=0)
        carry_scr[...] = dgh[0:CONV_HALO, :]
        d1 = _shift_up(extd, 1, ts)
        d2 = _shift_up(extd, 2, ts)
        dup_ref[:, 0:D_FF] = (w[2:3] * dgh + w[1:2] * d1 + w[0:1] * d2).astype(BF16)
        dfcw_ref[2:3, :] += jnp.sum(ug * dgh, axis=0, keepdims=True)
        dfcw_ref[1:2, :] += jnp.sum(ug * d1, axis=0, keepdims=True)
        dfcw_ref[0:1, :] += jnp.sum(ug * d2, axis=0, keepdims=True)
        dfcb_ref[...] += jnp.sum(dgh, axis=0, keepdims=True)

    row = _rows(ts, D_MODEL, nt)
    wide = _rows(ts, D_FF, nt)
    return pl.pallas_call(
        body, name="bwd_ffn_down", grid=(nt,),
        in_specs=[row, row, _rows(ts, 2 * D_FF, nt), wide, wide, _resident(fcw.shape), _resident(w_down.shape),
                  _resident((1, D_MODEL))],
        out_specs=[_rows(ts, 2 * D_FF, nt), _acc(w_down.shape), _acc(fcw.shape), _acc((1, D_FF)), _acc((1, D_MODEL))],
        out_shape=[_sds((s, 2 * D_FF), BF16), _sds(w_down.shape), _sds(fcw.shape), _sds((1, D_FF)), _sds((1, D_MODEL))],
        scratch_shapes=[pltpu.VMEM((CONV_HALO, D_FF), F32)],
        compiler_params=_params(),
    )(dx2, dn, up, gl, gg, fcw, w_down, g_post)


def _bwd_ffn_up(dup, x1, dx2, g_pre, w_up, ts, tasks=()):
    s = x1.shape[0]

    def body(dup_ref, x1_ref, dx2_ref, g_ref, wup_ref, dx1_ref, dg_ref):
        @pl.when(pl.program_id(0) == 0)
        def _():
            dg_ref[...] = jnp.zeros_like(dg_ref)

        _, xh, r = _rms_fwd(x1_ref[...], g_ref[...])
        dh2 = _dot_nt(dup_ref[:, 0:UP_SHARD], wup_ref[0])
        for j in range(1, N_CHIPS):
            dh2 = dh2 + _dot_nt(dup_ref[:, j * UP_SHARD:(j + 1) * UP_SHARD], wup_ref[j])
        dx, dg = _rms_bwd(xh, r, g_ref[...], dh2)
        dg_ref[...] += dg
        dx1_ref[...] = dx2_ref[...] + dx

    row = _rows(ts, D_MODEL)
    return _call(
        body, "bwd_ffn_up", (s // ts,),
        [_rows(ts, 2 * D_FF), row, row, _resident((1, D_MODEL)), _resident(w_up.shape)],
        [row, _acc((1, D_MODEL))], [_sds((s, D_MODEL)), _sds((1, D_MODEL))], [],
        (dup, x1, dx2, g_pre, w_up), tasks)


def _dw_up(h2, dup, ts, tasks=()):
    s = h2.shape[0]
    ts = min(DW_TILES * ts, s)

    def body(h2_ref, dup_ref, out_ref):
        @pl.when(pl.program_id(1) == 0)
        def _():
            out_ref[...] = jnp.zeros_like(out_ref)

        out_ref[0] += _dot_tn(h2_ref[...], dup_ref[...])

    return _call(
        body, "dw_up", (N_CHIPS, s // ts),
        [pl.BlockSpec((ts, D_MODEL), lambda j, i: (i, 0)), pl.BlockSpec((ts, UP_SHARD), lambda j, i: (i, j))],
        [pl.BlockSpec((1, D_MODEL, UP_SHARD), lambda j, i: (j, 0, 0))], [_sds((N_CHIPS, D_MODEL, UP_SHARD))], [],
        (h2, dup), tasks)


def _bwd_merge(dx1, mo, sp, sr, ypool, yrnn, g_post, w_o, ts, tasks=()):
    s = dx1.shape[0]

    def body(dx1_ref, mo_ref, sp_ref, sr_ref, yp_ref, yr_ref, g_ref, wo_ref,
             dgp_ref, dgr_ref, dyp_ref, dyr_ref, dwo_ref, dg_ref):
        @pl.when(pl.program_id(0) == 0)
        def _():
            dwo_ref[...] = jnp.zeros_like(dwo_ref)
            dg_ref[...] = jnp.zeros_like(dg_ref)

        _, xh, r = _rms_fwd(mo_ref[...], g_ref[...])
        dmo, dg = _rms_bwd(xh, r, g_ref[...], dx1_ref[...])
        dg_ref[...] += dg
        dmerged = _dot_nt(dmo, wo_ref[...])
        sp = sp_ref[...].astype(F32)
        sr = sr_ref[...].astype(F32)
        yp = yp_ref[...]
        yr = yr_ref[...]
        dwo_ref[...] += _dot_tn(sp * yp + sr * yr, dmo)
        dgp_ref[...] = (dmerged * yp * sp * (1.0 - sp)).astype(BF16)
        dgr_ref[...] = (dmerged * yr * sr * (1.0 - sr)).astype(BF16)
        dyp_ref[...] = (dmerged * sp).astype(BF16)
        dyr_ref[...] = (dmerged * sr).astype(BF16)

    row = _rows(ts, D_MODEL)
    return _call(
        body, "bwd_merge", (s // ts,),
        [row] * 6 + [_resident((1, D_MODEL)), _resident(w_o.shape)],
        [row] * 4 + [_acc(w_o.shape), _acc((1, D_MODEL))],
        [_sds((s, D_MODEL), BF16)] * 4 + [_sds(w_o.shape), _sds((1, D_MODEL))], [],
        (dx1, mo, sp, sr, ypool, yrnn, g_post, w_o), tasks)


def _bwd_rnn(dyr, urx, gl, gg, xc, r, ig, h, conv_w, wg, lam, w_rg_out, ts, tasks=()):
    s = urx.shape[0]
    nt = s // ts
    halo_blocks = ts // CONV_HALO

    def body(dyr_ref, urx_ref, gl_ref, gg_ref, xc_ref, r_ref, ig_ref, h_ref, hh_ref, cw_ref, wg_ref, lam_ref, wrg_ref,
             durx_ref, durg_ref, dwrg_ref, dwg_ref, dcw_ref, dcb_ref, dbg_ref, dlam_ref, mu_scr, carry_scr):
        i = pl.program_id(0)
        k = nt - 1 - i

        @pl.when(i == 0)
        def _():
            mu_scr[...] = jnp.zeros_like(mu_scr)
            carry_scr[...] = jnp.zeros_like(carry_scr)
            dwrg_ref[...] = jnp.zeros_like(dwrg_ref)
            dwg_ref[...] = jnp.zeros_like(dwg_ref)
            dcw_ref[...] = jnp.zeros_like(dcw_ref)
            dcb_ref[...] = jnp.zeros_like(dcb_ref)
            dbg_ref[...] = jnp.zeros_like(dbg_ref)
            dlam_ref[...] = jnp.zeros_like(dlam_ref)

        row = lax.broadcasted_iota(jnp.int32, (ts, 1), 0)
        first_row = (k * ts + row) == 0
        h = h_ref[...]
        dyr_v = dyr_ref[...]
        dhr = _dot_nt(dyr_v, wrg_ref[...])
        gl = gl_ref[...].astype(F32)
        dwrg_ref[...] += _dot_tn(h * gl, dyr_v)
        durg_ref[...] = (dhr * h * gg_ref[...].astype(F32)).astype(BF16)
        r_v = r_ref[...]
        ig_v = ig_ref[...]
        xc_v = xc_ref[...]
        lam_v = lam_ref[...]
        c8, a, m2, mult = _lru_coeffs(r_v, lam_v, first_row)
        b = jnp.where(row == ts - 1, 1.0, pltpu.roll(a, ts - 1, 0))
        lt = _scan_bwd(b, dhr * gl, mu_scr[0:1, :])
        mu_scr[0:1, :] = a[0:1, :] * lt[0:1, :]
        h_before = jnp.where(k > 0, hh_ref[CONV_HALO - 1:CONV_HALO, :], 0.0)
        hprev = jnp.where(row == 0, h_before, pltpu.roll(h, 1, 0))
        dmult = lt * ig_v * xc_v
        da = lt * hprev - jnp.where(first_row, 0.0, dmult * a * lax.rsqrt(m2))
        dla = da * a
        dlam_ref[...] += jnp.sum(dla * r_v, axis=0, keepdims=True)
        dlr = (dla * (-c8)) * r_v * (1.0 - r_v)
        dli = (lt * mult * xc_v) * ig_v * (1.0 - ig_v)
        dbg_ref[0:1, :] += jnp.sum(dlr, axis=0, keepdims=True)
        dbg_ref[1:2, :] += jnp.sum(dli, axis=0, keepdims=True)
        xcb = xc_v.astype(BF16)
        parts = []
        for q in range(GATE_BLOCKS):
            blk = slice(q * GATE_BLOCK, (q + 1) * GATE_BLOCK)
            dlr_q = dlr[:, blk].astype(BF16)
            dli_q = dli[:, blk].astype(BF16)
            parts.append(_dot_nt(dlr_q, wg_ref[0, q]) + _dot_nt(dli_q, wg_ref[1, q]))
            dwg_ref[0, q] += _dot_tn(xcb[:, blk], dlr_q)
            dwg_ref[1, q] += _dot_tn(xcb[:, blk], dli_q)
        dxc = lt * mult * ig_v + jnp.concatenate(parts, axis=1)
        extd = jnp.concatenate([dxc, carry_scr[...]], axis=0)
        carry_scr[...] = dxc[0:CONV_HALO, :]
        cw = cw_ref[...]
        urx_v = urx_ref[...]
        durx = cw[3:4] * dxc
        dcw_ref[3:4, :] += jnp.sum(urx_v * dxc, axis=0, keepdims=True)
        for j in (1, 2, 3):
            dj = _shift_up(extd, j, ts)
            durx = durx + cw[3 - j:4 - j] * dj
            dcw_ref[3 - j:4 - j, :] += jnp.sum(urx_v * dj, axis=0, keepdims=True)
        durx_ref[...] = durx.astype(BF16)
        dcb_ref[...] += jnp.sum(dxc, axis=0, keepdims=True)

        @pl.when(i == nt - 1)
        def _():
            dlam_ref[...] = dlam_ref[...] * (LRU_C * jax.nn.sigmoid(-lam_v))

    row_spec = _rows(ts, D_MODEL, nt)
    halo_spec = pl.BlockSpec((CONV_HALO, D_MODEL), lambda i: (jnp.maximum((nt - 1 - i) * halo_blocks - 1, 0), 0))
    vec = _acc((1, D_MODEL))
    return _call(
        body, "bwd_rnn", (nt,),
        [row_spec] * 8 + [halo_spec, _resident(conv_w.shape), _resident(wg.shape), _resident((1, D_MODEL)),
                          _resident(w_rg_out.shape)],
        [row_spec, row_spec, _acc(w_rg_out.shape), _acc(wg.shape), _acc(conv_w.shape), vec, _acc((2, D_MODEL)), vec],
        [_sds((s, D_MODEL), BF16), _sds((s, D_MODEL), BF16), _sds(w_rg_out.shape), _sds(wg.shape), _sds(conv_w.shape),
         _sds((1, D_MODEL)), _sds((2, D_MODEL)), _sds((1, D_MODEL))],
        [pltpu.VMEM((8, D_MODEL), F32), pltpu.VMEM((CONV_HALO, D_MODEL), F32)],
        (dyr, urx, gl, gg, xc, r, ig, h, h, conv_w, wg, lam, w_rg_out), tasks)


def _bwd_pool(dyp, d, pool_w, pool_scale, w_pool_out, ts, tasks=()):
    s = d.shape[0]
    nt = s // ts

    def body(dyp_ref, d_ref, pw_ref, ps_ref, wpo_ref, dzp_ref, dwpo_ref, dpw_ref, dps_ref, carry_scr):
        i = pl.program_id(0)
        k = nt - 1 - i

        @pl.when(i == 0)
        def _():
            carry_scr[...] = jnp.zeros_like(carry_scr)
            dwpo_ref[...] = jnp.zeros_like(dwpo_ref)
            dpw_ref[...] = jnp.zeros_like(dpw_ref)
            dps_ref[...] = jnp.zeros_like(dps_ref)

        dyp_v = dyp_ref[...]
        d_v = d_ref[...]
        ps = ps_ref[...]
        dypre = _dot_nt(dyp_v[:, 0:256], wpo_ref[0])
        for j in range(1, N_CHIPS):
            dypre = dypre + _dot_nt(dyp_v[:, j * 256:(j + 1) * 256], wpo_ref[j])
        y4 = jnp.concatenate([_dot(d_v[:, g * 128:(g + 1) * 128], pw_ref[g]) for g in range(POOL_GROUPS)], axis=1)
        ypre = (y4 * ps).astype(BF16)
        for j in range(N_CHIPS):
            dwpo_ref[j] += _dot_tn(ypre, dyp_v[:, j * 256:(j + 1) * 256])
        dps_ref[...] += jnp.sum(dypre * y4, axis=0, keepdims=True)
        dy4 = dypre * ps
        t = k * ts + lax.broadcasted_iota(jnp.int32, (ts, 1), 0)
        for g, w in enumerate(POOL_WINDOWS):
            lanes = slice(g * POOL_GROUP_DIM, (g + 1) * POOL_GROUP_DIM)
            dd = _dot_nt(dy4[:, lanes], pw_ref[g])
            dpw_ref[g] += _dot_tn(d_v[:, lanes], dy4[:, lanes])
            e = dd * (1.0 / jnp.minimum(t + 1, w).astype(F32))
            acc = jnp.concatenate([e, carry_scr[:, lanes]], axis=0)
            carry_scr[:, lanes] = e[0:POOL_HALO, :]
            n = ts + POOL_HALO
            sh = 1
            while sh < w:
                acc = acc + pltpu.roll(acc, n - sh, 0)
                sh *= 2
            dzp_ref[:, lanes] = (acc[:ts, :] - dd).astype(BF16)

    return _call(
        body, "bwd_pool", (nt,),
        [_rows(ts, D_MODEL, nt), _rows(ts, POOL_WIDTH, nt), _resident(pool_w.shape), _resident((1, POOL_WIDTH)),
         _resident(w_pool_out.shape)],
        [_rows(ts, POOL_WIDTH, nt), _acc(w_pool_out.shape), _acc(pool_w.shape), _acc((1, POOL_WIDTH))],
        [_sds((s, POOL_WIDTH), BF16), _sds(w_pool_out.shape), _sds(pool_w.shape), _sds((1, POOL_WIDTH))],
        [pltpu.VMEM((POOL_HALO, POOL_WIDTH), F32)],
        (dyp, d, pool_w, pool_scale, w_pool_out), tasks)


def _assemble_dz(dz_scr, dzp_ref, durx_ref, durg_ref, dgp_ref, dgr_ref):
    dz_scr[:, 0:512] = dzp_ref[...]
    dz_scr[:, 512:1536] = durx_ref[...]
    dz_scr[:, 1536:2560] = durg_ref[...]
    dz_scr[:, 2560:3584] = dgp_ref[...]
    dz_scr[:, 3584:4608] = dgr_ref[...]


def _dw_in(h1, dzp, durx, durg, dgp, dgr, ts, tasks=()):
    s = h1.shape[0]
    ts = min(2 * ts, s)

    def body(h1_ref, dzp_ref, durx_ref, durg_ref, dgp_ref, dgr_ref, out_ref, dz_scr):
        @pl.when(pl.program_id(0) == 0)
        def _():
            out_ref[...] = jnp.zeros_like(out_ref)

        _assemble_dz(dz_scr, dzp_ref, durx_ref, durg_ref, dgp_ref, dgr_ref)
        for j in range(N_CHIPS):
            out_ref[j] += _dot_tn(h1_ref[...], dz_scr[:, j * IN_SHARD:(j + 1) * IN_SHARD])

    row = _rows(ts, D_MODEL)
    shape = (N_CHIPS, D_MODEL, IN_SHARD)
    return _call(
        body, "dw_in", (s // ts,), [row, _rows(ts, POOL_WIDTH)] + [row] * 4, [_acc(shape)], [_sds(shape)],
        [pltpu.VMEM((ts, IN_TOTAL), BF16)], (h1, dzp, durx, durg, dgp, dgr), tasks)


def _bwd_in(dzp, durx, durg, dgp, dgr, x, dx1, g_pre, w_in, ts, tasks=()):
    s = x.shape[0]

    def body(dzp_ref, durx_ref, durg_ref, dgp_ref, dgr_ref, x_ref, dx1_ref, g_ref, win_ref, gx_ref, dg_ref, dz_scr):
        @pl.when(pl.program_id(0) == 0)
        def _():
            dg_ref[...] = jnp.zeros_like(dg_ref)

        _assemble_dz(dz_scr, dzp_ref, durx_ref, durg_ref, dgp_ref, dgr_ref)
        _, xh, r = _rms_fwd(x_ref[...], g_ref[...])
        dh1 = _dot_nt(dz_scr[:, 0:IN_SHARD], win_ref[0])
        for j in range(1, N_CHIPS):
            dh1 = dh1 + _dot_nt(dz_scr[:, j * IN_SHARD:(j + 1) * IN_SHARD], win_ref[j])
        dx, dg = _rms_bwd(xh, r, g_ref[...], dh1)
        dg_ref[...] += dg
        gx_ref[...] = dx1_ref[...] + dx

    row = _rows(ts, D_MODEL)
    return _call(
        body, "bwd_in", (s // ts,),
        [_rows(ts, POOL_WIDTH)] + [row] * 6 + [_resident((1, D_MODEL)), _resident(w_in.shape)],
        [row, _acc((1, D_MODEL))], [_sds((s, D_MODEL)), _sds((1, D_MODEL))],
        [pltpu.VMEM((ts, IN_TOTAL), BF16)], (dzp, durx, durg, dgp, dgr, x, dx1, g_pre, w_in), tasks)


def _place():
    x, y, c = lax.axis_index("x"), lax.axis_index("y"), lax.axis_index("c")
    others = [(1 - x, y), (x, 1 - y), (1 - x, 1 - y)]
    return x, y, c, 2 * x + y, others


def _remote(src, dst, send_sem, recv_sem, to):
    return pltpu.make_async_remote_copy(src_ref=src, dst_ref=dst, send_sem=send_sem, recv_sem=recv_sem,
                                        device_id=to, device_id_type=MESH)


def _own_slots(ws, dtypes, name, tasks=()):
    n = len(ws)
    hbm = pl.BlockSpec(memory_space=pltpu.HBM)

    def body(*refs):
        srcs, outs, f32_bufs, cast_bufs, sems = refs[:n], refs[n:2 * n], refs[2 * n:3 * n], refs[3 * n:4 * n], refs[4 * n]
        me = _place()[3]
        loads = [pltpu.make_async_copy(srcs[k], f32_bufs[k], sems.at[k, 0]) for k in range(n)]
        stores = [pltpu.make_async_copy(cast_bufs[k], outs[k].at[me], sems.at[k, 1]) for k in range(n)]
        for cp in loads:
            cp.start()
        for k in range(n):
            loads[k].wait()
            cast_bufs[k][...] = f32_bufs[k][...].astype(dtypes[k])
            stores[k].start()
        for cp in stores:
            cp.wait()

    return _call(
        body, name, (), [hbm] * n, [hbm] * n, [_sds((N_CHIPS,) + w.shape, dt) for w, dt in zip(ws, dtypes)],
        [pltpu.VMEM(w.shape, F32) for w in ws] + [pltpu.VMEM(w.shape, dt) for w, dt in zip(ws, dtypes)]
        + [pltpu.SemaphoreType.DMA((n, 2))],
        [pltpu.with_memory_space_constraint(w, pltpu.HBM) for w in ws], tasks)


def _run(tasks, name):
    if isinstance(tasks, _Task):
        return _call(None, name, (), [], [], [], [], (), (tasks,))[1][0]
    return _call(None, name, (), [], [], [], [], (), tuple(tasks))[1]


def _gather_task(bufs, relay_steps=(0, 0)):
    n = len(bufs)
    NBR_X, NBR_Y, QUARTER_VIA_Y, QUARTER_VIA_X, SIB_X, SIB_Y, SIB_DIAG = range(7)

    def parts(out):
        x, y, c, me, _ = _place()
        ah = out.shape[1] // 2
        q = ah // 2 if (ah // 2) % 16 == 0 else ah
        return c * ah, ah, q

    def copy(out, w, k, chip, row0, rows, to, sems):
        slot = out.at[chip, pl.ds(row0, rows)]
        return _remote(slot, slot, sems[0].at[w, k], sems[1].at[w, k], to)

    def plan(out, w, sems):
        x, y, c, me, _ = _place()
        row0, ah, q = parts(out)
        xn, yn, dg = 2 * (1 - x) + y, 2 * x + (1 - y), 2 * (1 - x) + (1 - y)
        to_x, to_y, sib = (1 - x, y, c), (x, 1 - y, c), (x, y, 1 - c)
        other = (1 - c) * ah
        cp = functools.partial(copy, out, w, sems=sems)
        sends = {NBR_X: cp(NBR_X, me, row0, ah, to_x), NBR_Y: cp(NBR_Y, me, row0, ah, to_y),
                 QUARTER_VIA_Y: cp(QUARTER_VIA_Y, xn, row0, q, to_y), SIB_X: cp(SIB_X, xn, row0, ah, sib),
                 SIB_Y: cp(SIB_Y, yn, row0, ah, sib), SIB_DIAG: cp(SIB_DIAG, dg, row0, ah, sib)}
        lands = {NBR_X: cp(NBR_X, xn, row0, ah, to_x), NBR_Y: cp(NBR_Y, yn, row0, ah, to_y),
                 QUARTER_VIA_Y: cp(QUARTER_VIA_Y, dg, row0, q, to_y), SIB_X: cp(SIB_X, xn, other, ah, sib),
                 SIB_Y: cp(SIB_Y, yn, other, ah, sib), SIB_DIAG: cp(SIB_DIAG, dg, other, ah, sib)}
        if q < ah:
            sends[QUARTER_VIA_X] = cp(QUARTER_VIA_X, yn, row0 + q, ah - q, to_x)
            lands[QUARTER_VIA_X] = cp(QUARTER_VIA_X, dg, row0 + q, ah - q, to_x)
        return sends, lands

    def start(ins, outs, sems):
        for w, out in enumerate(outs):
            sends, _ = plan(out, w, sems)
            sends[NBR_X].start()
            sends[NBR_Y].start()

    def pass_neighbours(ins, outs, sems):
        for w, out in enumerate(outs):
            sends, lands = plan(out, w, sems)
            lands[NBR_X].wait_recv()
            sends[QUARTER_VIA_Y].start()
            sends[SIB_X].start()
            lands[NBR_Y].wait_recv()
            if QUARTER_VIA_X in sends:
                sends[QUARTER_VIA_X].start()
            sends[SIB_Y].start()

    def pass_diagonal(ins, outs, sems):
        for w, out in enumerate(outs):
            sends, lands = plan(out, w, sems)
            lands[QUARTER_VIA_Y].wait_recv()
            if QUARTER_VIA_X in lands:
                lands[QUARTER_VIA_X].wait_recv()
            sends[SIB_DIAG].start()

    def finish(ins, outs, sems):
        for w, out in enumerate(outs):
            sends, lands = plan(out, w, sems)
            for k in (SIB_X, SIB_Y, SIB_DIAG):
                lands[k].wait_recv()
        for w, out in enumerate(outs):
            sends, _ = plan(out, w, sems)
            for cp in sends.values():
                cp.wait_send()

    return _Task(bufs, [_sds(b.shape, b.dtype) for b in bufs], {i: i for i in range(n)},
                 [pltpu.SemaphoreType.DMA((n, 7)), pltpu.SemaphoreType.DMA((n, 7))], start, finish,
                 [(pass_neighbours, relay_steps[0]), (pass_diagonal, relay_steps[1])])


def _halves_task(grads):
    n = len(grads)

    def copy(src, out, w, sems):
        x, y, c, _, _ = _place()
        ah = out.shape[1]
        return _remote(src.at[:, pl.ds((1 - c) * ah, ah)], out, sems[0].at[w], sems[1].at[w], (x, y, 1 - c))

    def start(ins, outs, sems):
        for w, (src, out) in enumerate(zip(ins, outs)):
            copy(src, out, w, sems).start()

    def finish(ins, outs, sems):
        for w, (src, out) in enumerate(zip(ins, outs)):
            copy(src, out, w, sems).wait()

    return _Task(grads, [_sds((g.shape[0], g.shape[1] // 2, g.shape[2]), g.dtype) for g in grads], {},
                 [pltpu.SemaphoreType.DMA((n,)), pltpu.SemaphoreType.DMA((n,))], start, finish)


def _exchange_task(sends, accs):
    n = len(accs)
    given = [s for s in sends if s is not None]

    def copies(ins, outs, sems):
        send_refs = iter(ins[:len(given)])
        srcs = [next(send_refs) if s is not None else None for s in sends]
        x, y, c, me, others = _place()
        for w, out in enumerate(outs):
            for j, (ox, oy) in enumerate(others):
                src = out.at[me] if srcs[w] is None else srcs[w].at[2 * ox + oy]
                yield _remote(src, out.at[me], sems[0].at[w, j], sems[1].at[w, j], (ox, oy, c))

    def start(ins, outs, sems):
        for cp in copies(ins, outs, sems):
            cp.start()

    def finish(ins, outs, sems):
        x, y, c, _, others = _place()
        for w, out in enumerate(outs):
            for j, (ox, oy) in enumerate(others):
                slot = out.at[2 * ox + oy]
                _remote(slot, slot, sems[0].at[w, j], sems[1].at[w, j], (ox, oy, c)).wait_recv()
        for cp in copies(ins, outs, sems):
            cp.wait_send()

    return _Task(given + list(accs), [_sds(a.shape, a.dtype) for a in accs], {len(given) + i: i for i in range(n)},
                 [pltpu.SemaphoreType.DMA((n, 3)), pltpu.SemaphoreType.DMA((n, 3))], start, finish)


def _swap_task(arrays):
    n = len(arrays)

    def copy(src, out, w, sems):
        x, y, c, _, _ = _place()
        return _remote(src, out, sems[0].at[w], sems[1].at[w], (x, y, 1 - c))

    def start(ins, outs, sems):
        for w, (src, out) in enumerate(zip(ins, outs)):
            copy(src, out, w, sems).start()

    def finish(ins, outs, sems):
        for w, (src, out) in enumerate(zip(ins, outs)):
            copy(src, out, w, sems).wait()

    return _Task(arrays, [_sds(a.shape, a.dtype) for a in arrays], {},
                 [pltpu.SemaphoreType.DMA((n,)), pltpu.SemaphoreType.DMA((n,))], start, finish)


def _all_devices_task(arrays):
    n = len(arrays)
    flips = [(dx, dy, dc) for dx in (0, 1) for dy in (0, 1) for dc in (0, 1)][1:]

    def peers():
        x, y, c, _, _ = _place()
        flip = lambda v, d: 1 - v if d else v
        return 4 * x + 2 * y + c, [(flip(x, dx), flip(y, dy), flip(c, dc)) for dx, dy, dc in flips]

    def start(ins, outs, sems):
        me, others = peers()
        for w, (src, out) in enumerate(zip(ins, outs)):
            pltpu.make_async_copy(src, out.at[me], sems[2].at[w]).start()
            for k, peer in enumerate(others):
                _remote(src, out.at[me], sems[0].at[w, k], sems[1].at[w, k], peer).start()

    def finish(ins, outs, sems):
        me, others = peers()
        for w, (src, out) in enumerate(zip(ins, outs)):
            for k, (px, py, pc) in enumerate(others):
                slot = out.at[4 * px + 2 * py + pc]
                _remote(slot, slot, sems[0].at[w, k], sems[1].at[w, k], (px, py, pc)).wait_recv()
            for k, peer in enumerate(others):
                _remote(src, out.at[me], sems[0].at[w, k], sems[1].at[w, k], peer).wait_send()
            pltpu.make_async_copy(src, out.at[me], sems[2].at[w]).wait()

    return _Task(arrays, [_sds((8,) + a.shape, a.dtype) for a in arrays], {},
                 [pltpu.SemaphoreType.DMA((n, 7)), pltpu.SemaphoreType.DMA((n, 7)), pltpu.SemaphoreType.DMA((n,))],
                 start, finish)


def _share_task(shares):
    n = len(shares)

    def copy(out, w, sems, slot):
        x, y, c, _, _ = _place()
        return _remote(out.at[slot], out.at[slot], sems[0].at[w], sems[1].at[w], (x, y, 1 - c))

    def start(ins, outs, sems):
        c = _place()[2]
        for w, out in enumerate(outs):
            copy(out, w, sems, c).start()

    def finish(ins, outs, sems):
        c = _place()[2]
        for w, out in enumerate(outs):
            copy(out, w, sems, 1 - c).wait_recv()
        for w, out in enumerate(outs):
            copy(out, w, sems, c).wait_send()

    return _Task(shares, [_sds(s.shape, s.dtype) for s in shares], {i: i for i in range(n)},
                 [pltpu.SemaphoreType.DMA((n,)), pltpu.SemaphoreType.DMA((n,))], start, finish)


TILE_BYTES = 2 * 1024 * 1024
PARTIAL_TILE_BYTES = 1024 * 1024


def _in_hbm(t):
    return pltpu.with_memory_space_constraint(t, pltpu.HBM)


def _row_tile(rows, cols, limit=TILE_BYTES):
    best = 8
    for tr in range(8, rows + 1, 8):
        if rows % tr == 0 and tr * cols * 4 <= limit:
            best = tr
    assert rows % best == 0, (rows, cols)
    return best


def _chip_partial(g, got, place, wire_dtype):
    ns, ah, b = got.shape
    sharded = ns == N_CHIPS
    tr = _row_tile(ah, b, PARTIAL_TILE_BYTES)
    nb = ah // tr

    def body(place_ref, *refs):
        g_refs, got_refs, outs = refs[:ns], refs[ns:2 * ns], refs[2 * ns:]
        parts = [g_refs[k][0] + got_refs[k][0] for k in range(ns)]
        own = parts[0]
        if sharded:
            for k in range(ns):
                outs[0][k] = parts[k].astype(wire_dtype)
                if k:
                    own = jnp.where(place_ref[0] == k, parts[k], own)
        outs[-1][0] = own.astype(wire_dtype)

    blk = (1, tr, b)
    in_specs = ([pl.BlockSpec(blk, lambda i, s, k=k: (k, s[1] * nb + i, 0)) for k in range(ns)]
                + [pl.BlockSpec(blk, lambda i, s, k=k: (k, i, 0)) for k in range(ns)])
    acc_spec = pl.BlockSpec(blk, lambda i, s: (s[0], i, 0))
    acc_shape = _sds((N_CHIPS, ah, b), wire_dtype)
    out = pl.pallas_call(
        body, name="grad_chip_partial",
        grid_spec=pltpu.PrefetchScalarGridSpec(
            num_scalar_prefetch=1, grid=(nb,), in_specs=in_specs,
            out_specs=[pl.BlockSpec((ns, tr, b), lambda i, s: (0, i, 0)), acc_spec] if sharded else [acc_spec]),
        out_shape=[acc_shape, acc_shape] if sharded else [acc_shape],
        compiler_params=pltpu.CompilerParams(dimension_semantics=("arbitrary",), vmem_limit_bytes=VMEM_LIMIT),
    )(place, *([g] * ns), *([got] * ns))
    return (out[0], out[1]) if sharded else (None, out[0])


def _chip_sum(acc, place):
    _, ah, b = acc.shape
    tr = _row_tile(ah, b)

    def body(place_ref, p_ref, out_ref):
        total = p_ref[0].astype(F32) + p_ref[1].astype(F32)
        total = total + p_ref[2].astype(F32)
        out_ref[0] = total + p_ref[3].astype(F32)

    return pl.pallas_call(
        body, name="grad_chip_sum",
        grid_spec=pltpu.PrefetchScalarGridSpec(
            num_scalar_prefetch=1, grid=(ah // tr,),
            in_specs=[pl.BlockSpec((N_CHIPS, tr, b), lambda i, s: (0, i, 0))],
            out_specs=pl.BlockSpec((1, tr, b), lambda i, s: (s[1], i, 0))),
        out_shape=_sds((2, ah, b)),
        compiler_params=pltpu.CompilerParams(dimension_semantics=("arbitrary",)),
    )(place, _in_hbm(acc))


def _adam_math(w, g, m, v):
    nm = ADAM_B1 * m + (1.0 - ADAM_B1) * g
    nv = ADAM_B2 * v + (1.0 - ADAM_B2) * (g * g)
    m_hat = nm / (1.0 - ADAM_B1 ** ADAM_STEP)
    v_hat = nv / (1.0 - ADAM_B2 ** ADAM_STEP)
    return -ADAM_LR * (m_hat / (jnp.sqrt(v_hat) + ADAM_EPS) + ADAM_WD * w), nm, nv


def _adamw(w, g, m, v):
    a, b = w.shape
    tr = _row_tile(a, b)

    def body(w_ref, g_ref, m_ref, v_ref, g_out, d_ref, nm_ref, nv_ref):
        g_out[...] = g_ref[...]
        d_ref[...], nm_ref[...], nv_ref[...] = _adam_math(w_ref[...], g_ref[...], m_ref[...], v_ref[...])

    blk = pl.BlockSpec((tr, b), lambda i: (i, 0))
    return pl.pallas_call(
        body, name="adamw", grid=(a // tr,),
        in_specs=[blk] * 4, out_specs=[blk] * 4, out_shape=[_sds((a, b))] * 4,
        compiler_params=pltpu.CompilerParams(dimension_semantics=("arbitrary",)),
    )(w, g, m, v)


def _adamw_sum(w, m, v, acc, got, place):
    a, b = w.shape
    ah = a // 2
    tr = _row_tile(ah, b)
    nb = ah // tr

    def body(place_ref, w_ref, m_ref, v_ref, acc_ref, got_ref, g_out, d_ref, nm_ref, nv_ref):
        mine = (pl.program_id(0) // nb) == place_ref[1]
        part = lambda k: jnp.where(mine, acc_ref[k], got_ref[k]).astype(F32)
        g = part(0) + part(1)
        g = g + part(2)
        g = g + part(3)
        g_out[...] = g
        d_ref[...], nm_ref[...], nv_ref[...] = _adam_math(w_ref[...], g, m_ref[...], v_ref[...])

    blk = pl.BlockSpec((tr, b), lambda i, s: (i, 0))
    mine_spec = pl.BlockSpec((N_CHIPS, tr, b), lambda i, s: (0, jnp.where(i // nb == s[1], i % nb, 0), 0))
    got_spec = pl.BlockSpec((N_CHIPS, tr, b), lambda i, s: (0, jnp.where(i // nb == s[1], 0, i % nb), 0))
    return pl.pallas_call(
        body, name="adamw_sum",
        grid_spec=pltpu.PrefetchScalarGridSpec(
            num_scalar_prefetch=1, grid=(a // tr,), in_specs=[blk] * 3 + [mine_spec, got_spec], out_specs=[blk] * 4),
        out_shape=[_sds((a, b))] * 4,
        compiler_params=pltpu.CompilerParams(dimension_semantics=("arbitrary",), vmem_limit_bytes=VMEM_LIMIT),
    )(place, w, m, v, _in_hbm(acc), _in_hbm(got))


def _adamw_pieces(g, pieces, name):
    n = len(pieces)

    def body(g_ref, *refs):
        def grad(rows, cols):
            if len(g_ref.shape) == 2:
                return g_ref[rows, cols]
            total = g_ref[0, rows, cols]
            for k in range(1, g_ref.shape[0]):
                total = total + g_ref[k, rows, cols]
            return total

        ins, outs = refs[:3 * n], refs[3 * n:]
        for i, piece in enumerate(pieces):
            w_ref, m_ref, v_ref = ins[3 * i:3 * i + 3]
            o_g, o_d, o_m, o_v = outs[4 * i:4 * i + 4]
            if len(piece) == 5:
                g_v = grad(piece[3], piece[4])
                o_g[...] = g_v
                o_d[...], o_m[...], o_v[...] = _adam_math(w_ref[...], g_v, m_ref[...], v_ref[...])
            else:
                for r in range(w_ref.shape[1] // SMALL_COLS):
                    lanes = slice(r * SMALL_COLS, (r + 1) * SMALL_COLS)
                    g_v = grad(slice(piece[3] + r, piece[3] + r + 1), slice(None))
                    o_g[:, lanes] = g_v
                    o_d[:, lanes], o_m[:, lanes], o_v[:, lanes] = _adam_math(w_ref[:, lanes], g_v, m_ref[:, lanes],
                                                                            v_ref[:, lanes])

    operands = [t for piece in pieces for t in piece[:3]]
    out = pl.pallas_call(
        body, name=name,
        out_shape=[_sds(piece[0].shape) for piece in pieces for _ in range(4)],
    )(g, *operands)
    return [tuple(out[4 * i:4 * i + 4]) for i in range(n)]


TINY_ROWS, TINY_COLS = 16, 768
SMALL_COLS = 128
SMALL_ROWS = 624


def _pack_tiny(conv_w, b_gates, fcw):
    ns = conv_w.shape[0]
    pad = lambda t: jnp.pad(t, ((0, 0), (0, 0), (0, TINY_COLS - t.shape[2])))
    z = lambda rows: jnp.zeros((ns, rows, TINY_COLS), F32)
    return jnp.concatenate([pad(conv_w), pad(b_gates), z(2), fcw, z(5)], axis=1)


def _unpack_tiny(t):
    return t[:, 0:4, 0:256], t[:, 4:6, 0:256], t[:, 8:11, :]


def _cols_to_shards(t, n):
    return t.reshape(t.shape[0], N_CHIPS, n).transpose(1, 0, 2)


def _shards_to_cols(t):
    return t.transpose(1, 0, 2).reshape(t.shape[1], -1)


_VECTORS = ("g_mix_post", "conv_b", "lru_lambda", "g_ffn_pre", "g_ffn_post", "g_ple_gate", "g_ple_post", "pool_scale",
            "ffn_conv_b")
_VECTOR_LEN = {"pool_scale": POOL_WIDTH, "ffn_conv_b": D_FF}
POOL_W_ROWS = POOL_GROUPS * POOL_GROUP_DIM


def _vector_rows():
    rows, row = {}, POOL_W_ROWS
    for k in _VECTORS:
        rows[k] = row
        row += max(8, _VECTOR_LEN.get(k, D_MODEL) // SMALL_COLS)
    return rows, row


def _pack_small(grads, loss):
    tiles = lambda t: jnp.pad(t, ((0, -t.shape[0] % 8), (0, 0)))
    parts = [grads["pool_w"].reshape(POOL_W_ROWS, SMALL_COLS)] + [tiles(grads[k].reshape(-1, SMALL_COLS)) for k in _VECTORS]
    parts.append(tiles(loss))
    used = sum(t.shape[0] for t in parts)
    return jnp.concatenate(parts + [jnp.zeros((SMALL_ROWS - used, SMALL_COLS), F32)], axis=0)


def _gates_block_diag(w):
    w4 = w.reshape(2, GATE_BLOCKS, 4, RNN_HEAD_DIM, RNN_HEAD_DIM)
    eye = jnp.eye(4, dtype=w.dtype)
    return jnp.einsum("gqhij,hk->gqhikj", w4, eye).reshape(2, GATE_BLOCKS, GATE_BLOCK, GATE_BLOCK)


def _gates_from_block_diag(dw):
    d6 = dw.reshape(2, GATE_BLOCKS, 4, RNN_HEAD_DIM, 4, RNN_HEAD_DIM)
    blocks = [d6[:, :, hh, :, hh, :] for hh in range(4)]
    return jnp.stack(blocks, axis=2).reshape(2, RNN_HEADS, RNN_HEAD_DIM, RNN_HEAD_DIM)


ROW_TILE = 256
DW_TILES = 4

_SHARDED = ("w_in", "w_pool_out", "w_rg_out", "w_o", "w_up", "w_down", "w_ple_gate", "w_ple_proj")
_WEIGHTS = ("g_mix_pre", "g_mix_post", "w_in", "pool_w", "pool_scale", "w_pool_out", "conv_w", "conv_b", "w_rg_gates",
            "b_rg_gates", "lru_lambda", "w_rg_out", "w_o", "g_ffn_pre", "g_ffn_post", "w_up", "ffn_conv_w", "ffn_conv_b",
            "w_down", "g_ple_gate", "w_ple_gate", "w_ple_proj", "g_ple_post")


def _wire_dtype(g):
    return BF16 if g.shape[1] >= 64 and g.shape[2] > SMALL_COLS else F32


def _partials(grads, got, place):
    parts = [_chip_partial(g, r, place, _wire_dtype(g)) for g, r in zip(grads, got)]
    return [send for send, _ in parts], [acc for _, acc in parts]


def _whole(both):
    return [b.reshape(2 * b.shape[1], b.shape[2]) for b in both]


def _step(x, p, tgt, rep, place, ts):
    vec = lambda k: rep[k].reshape(1, -1)
    tall = min(2 * ts, x.shape[0])
    pool_w = rep["pool_w"].astype(BF16)
    wg = _gates_block_diag(rep["w_rg_gates"]).astype(BF16)
    sq = lambda t: t.reshape(D_MODEL, D_MODEL)
    by4 = lambda t: t.reshape(N_CHIPS, -1, D_MODEL)

    first, ride1, ride2, ride3 = (("w_in", "w_pool_out", "tiny"), ("w_rg_out", "w_o", "w_down"), ("w_up",),
                                  ("w_ple_gate", "w_ple_proj"))
    later = ride1 + ride2 + ride3
    tiny = _pack_tiny(rep["conv_w"][None], rep["b_rg_gates"][None], rep["ffn_conv_w"][None])[0]
    own_first, _ = _own_slots([rep["w_in"], rep["w_pool_out"], tiny], [BF16, BF16, F32], "own_slots_first")
    own_later, (got,) = _own_slots([rep[k] for k in later], [BF16] * len(later), "own_slots_gather_first",
                                   [_gather_task(own_first)])
    own = dict(zip(later, own_later))
    full = dict(zip(first, got))
    conv_w, b_gates, fcw = [_shards_to_cols(t) for t in _unpack_tiny(full["tiny"])]

    (urx, urg, gp, gr, d, ypool, h1), (got,) = _fwd_in_pool(
        x, vec("g_mix_pre"), full["w_in"], pool_w, vec("pool_scale"), full["w_pool_out"], ts,
        [_gather_task([own[k] for k in ride1], relay_steps=(6, 2))])
    full.update(zip(ride1, got))
    w_rg_out, w_o, w_down = sq(full["w_rg_out"]), sq(full["w_o"]), full["w_down"].reshape(D_FF, D_MODEL)
    (xc, r, ig, h, yrnn, mo, x1, glr, ggr, sp, sr), (got,) = _fwd_rnn_merge(
        urx, urg, gp, gr, ypool, x, conv_w, vec("conv_b"), wg, b_gates, vec("lru_lambda"), w_rg_out, w_o,
        vec("g_mix_post"), ts, [_gather_task([own[k] for k in ride2], relay_steps=(7, 3))])
    full.update(zip(ride2, got))
    (up, gl, gg, h2, dn, x2), (got,) = _fwd_ffn(x1, vec("g_ffn_pre"), full["w_up"], fcw, vec("ffn_conv_b"), w_down,
                                               vec("g_ffn_post"), ts, [_gather_task([own[k] for k in ride3], relay_steps=(10, 5))])
    full.update(zip(ride3, got))
    dx2, loss, d_w_gate, d_w_proj, d_g_ple_gate, d_g_ple_post = _ple_loss(
        x2, p, tgt, vec("g_ple_gate"), sq(full["w_ple_gate"]), full["w_ple_proj"], vec("g_ple_post"), tall)
    dup, d_w_down, d_fcw, d_fcb, d_g_ffn_post = _bwd_ffn_down(dx2, dn, up, gl, gg, fcw, w_down, vec("g_ffn_post"), ts)

    names1, grads1 = ("w_ple_gate", "w_ple_proj", "w_down"), [by4(d_w_gate), d_w_proj, by4(d_w_down)]
    (dx1, d_g_ffn_pre), (got1,) = _bwd_ffn_up(dup, x1, dx2, vec("g_ffn_pre"), full["w_up"], tall, [_halves_task(grads1)])
    (d_w_up,), (accs1,) = _dw_up(h2, dup, ts, [_exchange_task(*_partials(grads1, got1, place))])
    (dgp, dgr, dyp, dyr, d_w_o, d_g_mix_post), (got2, theirs1) = _bwd_merge(
        dx1, mo, sp, sr, ypool, yrnn, vec("g_mix_post"), w_o, tall, [_halves_task([d_w_up]), _swap_task(accs1)])
    (durx, durg, d_w_rg_out, d_wg, d_conv_w, d_conv_b, d_b_gates, d_lam), (accs2,) = _bwd_rnn(
        dyr, urx, glr, ggr, xc, r, ig, h, conv_w, wg, vec("lru_lambda"), w_rg_out, ts,
        [_exchange_task(*_partials([d_w_up], got2, place))])
    names3 = ("w_o", "w_rg_out", "tiny", "w_rg_gates")
    grads3 = [by4(d_w_o), by4(d_w_rg_out),
              _pack_tiny(_cols_to_shards(d_conv_w, 256), _cols_to_shards(d_b_gates, 256), _cols_to_shards(d_fcw, 768)),
              _gates_from_block_diag(d_wg).reshape(1, 2 * RNN_HEADS * RNN_HEAD_DIM, RNN_HEAD_DIM)]
    (dzp, d_w_pool_out, d_pool_w, d_pool_scale), (got3, theirs2) = _bwd_pool(
        dyp, d, pool_w, vec("pool_scale"), full["w_pool_out"], tall, [_halves_task(grads3), _swap_task(accs2)])
    (d_w_in,), (accs3,) = _dw_in(h1, dzp, durx, durg, dgp, dgr, ts, [_exchange_task(*_partials(grads3, got3, place))])

    replicated = {"g_mix_post": d_g_mix_post, "conv_b": d_conv_b, "lru_lambda": d_lam, "g_ffn_pre": d_g_ffn_pre,
                  "g_ffn_post": d_g_ffn_post, "g_ple_gate": d_g_ple_gate, "g_ple_post": d_g_ple_post,
                  "pool_scale": d_pool_scale, "ffn_conv_b": d_fcb, "pool_w": d_pool_w}
    names4 = ("w_in", "w_pool_out", "small")
    grads4 = [d_w_in, d_w_pool_out, _pack_small(replicated, loss)[None]]
    got4 = _run(_halves_task(grads4), "grad_sibling_halves")
    (grad_x, d_g_mix_pre), (accs4, theirs3, both3) = _bwd_in(
        dzp, durx, durg, dgp, dgr, x, dx1, vec("g_mix_pre"), full["w_in"], tall,
        [_exchange_task(*_partials(grads4, got4, place)), _swap_task(accs3[:2]),
         _share_task([_chip_sum(acc, place) for acc in accs3[2:]])])
    theirs4, both4, (g_mix_pre_parts,) = _run(
        [_swap_task(accs4[:2]), _share_task([_chip_sum(acc, place) for acc in accs4[2:]]),
         _all_devices_task([d_g_mix_pre.reshape(SUBLANES, SMALL_COLS)])], "grad_sibling_share")
    mine = accs1 + accs2 + accs3[:2] + accs4[:2]
    partials = dict(zip(names1 + ("w_up",) + names3[:2] + names4[:2], zip(mine, theirs1 + theirs2 + theirs3 + theirs4)))
    return grad_x, partials, dict(zip(names3[2:] + names4[2:], _whole(both3) + _whole(both4))), g_mix_pre_parts


def kernel(x, p, g_mix_pre, g_mix_post, w_in, pool_w, pool_scale, w_pool_out, conv_w, conv_b, w_rg_gates, b_rg_gates, lru_lambda, w_rg_out, w_o, g_ffn_pre, g_ffn_post, w_up, ffn_conv_w, ffn_conv_b, w_down, g_ple_gate, w_ple_gate, w_ple_proj, g_ple_post, loss_target, m_g_mix_pre, m_g_mix_post, m_w_in, m_pool_w, m_pool_scale, m_w_pool_out, m_conv_w, m_conv_b, m_w_rg_gates, m_b_rg_gates, m_lru_lambda, m_w_rg_out, m_w_o, m_g_ffn_pre, m_g_ffn_post, m_w_up, m_ffn_conv_w, m_ffn_conv_b, m_w_down, m_g_ple_gate, m_w_ple_gate, m_w_ple_proj, m_g_ple_post, v_g_mix_pre, v_g_mix_post, v_w_in, v_pool_w, v_pool_scale, v_w_pool_out, v_conv_w, v_conv_b, v_w_rg_gates, v_b_rg_gates, v_lru_lambda, v_w_rg_out, v_w_o, v_g_ffn_pre, v_g_ffn_post, v_w_up, v_ffn_conv_w, v_ffn_conv_b, v_w_down, v_g_ple_gate, v_w_ple_gate, v_w_ple_proj, v_g_ple_post):
    args = dict(locals())
    w = {k: args[k][0] for k in _WEIGHTS}
    m = {k: args["m_" + k][0] for k in _WEIGHTS}
    v = {k: args["v_" + k][0] for k in _WEIGHTS}
    place = jnp.stack([2 * lax.axis_index("x") + lax.axis_index("y"), lax.axis_index("c")]).astype(jnp.int32)
    grad_x, partials, reduced, g_mix_pre_parts = _step(x[0], p[0, 0], loss_target[0], w, place, ROW_TILE)

    gates_2d = (2 * RNN_HEADS * RNN_HEAD_DIM, RNN_HEAD_DIM)
    as2d = lambda k, shape: tuple(t[k].reshape(shape) for t in (w, m, v))
    done = {k: tuple(_adamw_sum(w[k], m[k], v[k], *partials[k], place)) for k in _SHARDED}
    gates_w, gates_m, gates_v = as2d("w_rg_gates", gates_2d)
    done["w_rg_gates"] = tuple(_adamw(gates_w, reduced["w_rg_gates"], gates_m, gates_v))
    tiny_names = ("conv_w", "b_rg_gates", "ffn_conv_w")
    tiny_at = ((slice(0, 4), slice(0, 256)), (slice(4, 6), slice(0, 256)), (slice(8, 11), slice(None)))
    done.update(zip(tiny_names, _adamw_pieces(
        reduced["tiny"], [(w[k], m[k], v[k]) + at for k, at in zip(tiny_names, tiny_at)], "adamw_tiny")))
    vector_rows, loss_row = _vector_rows()
    pieces = [as2d("pool_w", (POOL_W_ROWS, SMALL_COLS)) + (slice(0, POOL_W_ROWS), slice(None))]
    pieces += [as2d(k, (1, -1)) + (vector_rows[k],) for k in _VECTORS]
    done.update(zip(("pool_w",) + _VECTORS, _adamw_pieces(reduced["small"], pieces, "adamw_small")))
    done["g_mix_pre"] = _adamw_pieces(g_mix_pre_parts, [as2d("g_mix_pre", (1, -1)) + (0,)], "adamw_g_mix_pre")[0]

    result = [reduced["small"][loss_row, 0], grad_x[None]]
    for kind in range(4):
        result += [done[k][kind].reshape(args[k].shape) for k in _WEIGHTS]
    return tuple(result)
```

```python
import functools

import jax
import jax.numpy as jnp
from jax import lax
from jax.experimental import pallas as pl
from jax.experimental.pallas import tpu as pltpu

F32 = jnp.float32
BF16 = jnp.bfloat16

D_MODEL = 1024
POOL_WINDOWS = (2, 4, 8, 16)
POOL_GROUPS = 4
POOL_WIDTH = 512
POOL_GROUP_DIM = 128
RNN_HEADS = 16
RNN_HEAD_DIM = 64
GATE_BLOCK = 256
GATE_BLOCKS = D_MODEL // GATE_BLOCK
LRU_C = 8.0
D_FF = 3072
PLE_DIM = 256
RMS_EPS = 1e-6
IN_TOTAL = 4608
N_CHIPS = 4
IN_SHARD = IN_TOTAL // N_CHIPS
UP_SHARD = 2 * D_FF // N_CHIPS
POOL_HALO = 16
CONV_HALO = 8

ADAM_LR = 0.001
ADAM_B1 = 0.9
ADAM_B2 = 0.999
ADAM_EPS = 1e-08
ADAM_WD = 0.01
ADAM_STEP = 10

VMEM_LIMIT = 56 * 1024 * 1024
MESH = pl.DeviceIdType.MESH

_GELU_C = 0.7978845608028654
_GELU_A = 0.044715


def _dot(a, b):
    return jnp.dot(a.astype(BF16), b.astype(BF16), preferred_element_type=F32)


def _dot_nt(a, b):
    return lax.dot_general(a.astype(BF16), b.astype(BF16), (((1,), (1,)), ((), ())), preferred_element_type=F32)


def _dot_tn(a, b):
    return lax.dot_general(a.astype(BF16), b.astype(BF16), (((0,), (0,)), ((), ())), preferred_element_type=F32)


def _rms_fwd(x, g):
    r = lax.rsqrt(jnp.mean(x * x, axis=-1, keepdims=True) + RMS_EPS)
    xh = x * r
    return xh * g, xh, r


def _rms_bwd(xh, r, g, dy):
    dxh = dy * g
    dg = jnp.sum(dy * xh, axis=0, keepdims=True)
    dx = r * (dxh - xh * jnp.mean(dxh * xh, axis=-1, keepdims=True))
    return dx, dg


def _sigmoid(x):
    return 0.5 * jnp.tanh(0.5 * x) + 0.5


def _gelu(x):
    t = jnp.tanh(_GELU_C * (x + _GELU_A * x * x * x))
    return 0.5 * x * (1.0 + t), t


def _gelu_grad(x, t):
    return 0.5 * (1.0 + t) + 0.5 * x * (1.0 - t * t) * _GELU_C * (1.0 + 3.0 * _GELU_A * x * x)


def _softplus_neg(lam):
    nl = -lam
    return jnp.maximum(nl, 0.0) + jnp.log(1.0 + jnp.exp(-jnp.abs(nl)))


def _lru_coeffs(r, lam, first_row):
    c8 = LRU_C * _softplus_neg(lam)
    la = -(c8 * r)
    a = jnp.exp(la)
    m2 = jnp.tanh(-la) * (1.0 + a * a)
    mult = jnp.where(first_row, 1.0, jnp.sqrt(m2))
    return c8, a, m2, mult


SUBLANES = 8


def _scan_fwd(a, u, carry):
    n = a.shape[0]
    sub = lax.broadcasted_iota(jnp.int32, (n, 1), 0) % SUBLANES
    acc_a, acc_h = a, u
    for s in (1, 2, 4):
        m = sub >= s
        h_s = jnp.where(m, pltpu.roll(acc_h, s, 0), 0.0)
        a_s = jnp.where(m, pltpu.roll(acc_a, s, 0), 1.0)
        acc_h = acc_a * h_s + acc_h
        acc_a = acc_a * a_s
    out = []
    for g in range(n // SUBLANES):
        rows = slice(g * SUBLANES, (g + 1) * SUBLANES)
        out.append(acc_h[rows] + acc_a[rows] * carry)
        carry = out[-1][SUBLANES - 1:SUBLANES]
    return jnp.concatenate(out, axis=0)


def _scan_bwd(b, g, carry):
    n = b.shape[0]
    sub = lax.broadcasted_iota(jnp.int32, (n, 1), 0) % SUBLANES
    acc_b, acc_l = b, g
    for s in (1, 2, 4):
        m = sub < SUBLANES - s
        l_s = jnp.where(m, pltpu.roll(acc_l, n - s, 0), 0.0)
        b_s = jnp.where(m, pltpu.roll(acc_b, n - s, 0), 1.0)
        acc_l = acc_b * l_s + acc_l
        acc_b = acc_b * b_s
    out = [None] * (n // SUBLANES)
    for g in reversed(range(n // SUBLANES)):
        rows = slice(g * SUBLANES, (g + 1) * SUBLANES)
        out[g] = acc_l[rows] + acc_b[rows] * carry
        carry = out[g][0:1]
    return jnp.concatenate(out, axis=0)


def _shift_down(ext, k, halo):
    return pltpu.roll(ext, k, 0)[halo:] if k else ext[halo:]


def _shift_up(ext, k, ts):
    return pltpu.roll(ext, ext.shape[0] - k, 0)[:ts] if k else ext[:ts]


def _rows(ts, width, nt=None, col=0):
    if nt is None:
        return pl.BlockSpec((ts, width), lambda i: (i, col))
    return pl.BlockSpec((ts, width), lambda i: (nt - 1 - i, col))


def _resident(shape):
    zeros = (0,) * len(shape)
    return pl.BlockSpec(shape, lambda i: zeros, pipeline_mode=pl.Buffered(1))


def _acc(shape):
    zeros = (0,) * len(shape)
    return pl.BlockSpec(shape, lambda i: zeros)


def _params():
    return pltpu.CompilerParams(dimension_semantics=("arbitrary",), vmem_limit_bytes=VMEM_LIMIT)


def _sds(shape, dtype=F32):
    return jax.ShapeDtypeStruct(shape, dtype)


class _Task:
    def __init__(self, ins, out_shapes, aliases, sems, start, finish, relays=()):
        self.ins, self.out_shapes, self.aliases, self.sems = list(ins), list(out_shapes), dict(aliases), list(sems)
        self.start, self.relays, self.finish = start, list(relays), finish


def _call(body, name, grid, in_specs, out_specs, out_shape, scratch_shapes, args, tasks=()):
    n_in, n_out, n_scr = len(in_specs), len(out_specs), len(scratch_shapes)
    t_in = [len(t.ins) for t in tasks]
    t_out = [len(t.out_shapes) for t in tasks]
    t_sem = [len(t.sems) for t in tasks]
    steps = 1
    for g in grid:
        steps *= g

    def take(refs, pos, counts):
        groups = []
        for c in counts:
            groups.append(refs[pos:pos + c])
            pos += c
        return groups, pos

    def wrapped(*refs):
        (cin,), pos = take(refs, 0, [n_in])
        tin, pos = take(refs, pos, t_in)
        (cout,), pos = take(refs, pos, [n_out])
        tout, pos = take(refs, pos, t_out)
        (cscr,), pos = take(refs, pos, [n_scr])
        tsem, pos = take(refs, pos, t_sem)
        if not grid:
            for t, a, b, c in zip(tasks, tin, tout, tsem):
                t.start(a, b, c)
            if body is not None:
                body(*cin, *cout, *cscr)
            for t, a, b, c in zip(tasks, tin, tout, tsem):
                for relay, _ in t.relays:
                    relay(a, b, c)
            for t, a, b, c in zip(tasks, tin, tout, tsem):
                t.finish(a, b, c)
            return
        step = pl.program_id(0)
        for axis in range(1, len(grid)):
            step = step * grid[axis] + pl.program_id(axis)
        if tasks:
            @pl.when(step == 0)
            def _():
                for t, a, b, c in zip(tasks, tin, tout, tsem):
                    t.start(a, b, c)

        body(*cin, *cout, *cscr)
        for t, a, b, c in zip(tasks, tin, tout, tsem):
            for relay, before in t.relays:
                pl.when(step == max(steps - 1 - before, 0))(functools.partial(relay, a, b, c))

        if tasks:
            @pl.when(step == steps - 1)
            def _():
                for t, a, b, c in zip(tasks, tin, tout, tsem):
                    t.finish(a, b, c)

    aliases, in_pos, out_pos = {}, n_in, n_out
    for t, ni, no in zip(tasks, t_in, t_out):
        aliases.update({in_pos + a: out_pos + b for a, b in t.aliases.items()})
        in_pos, out_pos = in_pos + ni, out_pos + no
    any_spec = pl.BlockSpec(memory_space=pltpu.HBM)
    kwargs = dict(grid=grid, compiler_params=pltpu.CompilerParams(
        dimension_semantics=("arbitrary",) * len(grid), vmem_limit_bytes=VMEM_LIMIT)) if grid else dict(
        compiler_params=pltpu.CompilerParams(vmem_limit_bytes=VMEM_LIMIT))
    out = pl.pallas_call(
        wrapped, name=name,
        in_specs=list(in_specs) + [any_spec] * sum(t_in),
        out_specs=list(out_specs) + [any_spec] * sum(t_out),
        out_shape=list(out_shape) + [s for t in tasks for s in t.out_shapes],
        scratch_shapes=list(scratch_shapes) + [s for t in tasks for s in t.sems],
        input_output_aliases=aliases, **kwargs,
    )(*args, *[pltpu.with_memory_space_constraint(a, pltpu.HBM) for t in tasks for a in t.ins])
    task_outs, pos = take(list(out), n_out, t_out)
    return list(out[:n_out]), task_outs


def _fwd_in_pool(x, g_pre, w_in, pool_w, pool_scale, w_pool_out, ts, tasks=()):
    s = x.shape[0]

    def body(x_ref, g_ref, win_ref, pw_ref, ps_ref, wpo_ref,
             urx_ref, urg_ref, gp_ref, gr_ref, d_ref, yp_ref, h1_ref, z_scr, halo_scr):
        i = pl.program_id(0)

        @pl.when(i == 0)
        def _():
            halo_scr[...] = jnp.zeros_like(halo_scr)

        h1, _, _ = _rms_fwd(x_ref[...], g_ref[...])
        h1 = h1.astype(BF16)
        h1_ref[...] = h1
        for j in range(N_CHIPS):
            z_scr[:, j * IN_SHARD:(j + 1) * IN_SHARD] = jnp.dot(h1, win_ref[j], preferred_element_type=F32)
        urx_ref[...] = z_scr[:, 512:1536]
        urg_ref[...] = z_scr[:, 1536:2560]
        gp_ref[...] = z_scr[:, 2560:3584]
        gr_ref[...] = z_scr[:, 3584:4608]
        u = z_scr[:, 0:POOL_WIDTH]
        ext = jnp.concatenate([halo_scr[...], u], axis=0)
        halo_scr[...] = u[ts - POOL_HALO:, :]
        t = i * ts + lax.broadcasted_iota(jnp.int32, (ts, 1), 0)
        y4 = []
        for g, w in enumerate(POOL_WINDOWS):
            lanes = slice(g * POOL_GROUP_DIM, (g + 1) * POOL_GROUP_DIM)
            acc = ext[:, lanes]
            sh = 1
            while sh < w:
                acc = acc + pltpu.roll(acc, sh, 0)
                sh *= 2
            inv = 1.0 / jnp.minimum(t + 1, w).astype(F32)
            dg = acc[POOL_HALO:, :] * inv - u[:, lanes]
            d_ref[:, lanes] = dg
            y4.append(_dot(dg, pw_ref[g]))
        ypre = jnp.concatenate(y4, axis=1) * ps_ref[...]
        ypre = ypre.astype(BF16)
        for j in range(N_CHIPS):
            yp_ref[:, j * 256:(j + 1) * 256] = jnp.dot(ypre, wpo_ref[j], preferred_element_type=F32)

    return _call(
        body, "fwd_in_pool", (s // ts,),
        [_rows(ts, D_MODEL), _resident((1, D_MODEL)), _resident(w_in.shape), _resident(pool_w.shape),
         _resident((1, POOL_WIDTH)), _resident(w_pool_out.shape)],
        [_rows(ts, D_MODEL)] * 4 + [_rows(ts, POOL_WIDTH), _rows(ts, D_MODEL), _rows(ts, D_MODEL)],
        [_sds((s, D_MODEL))] * 4 + [_sds((s, POOL_WIDTH)), _sds((s, D_MODEL)), _sds((s, D_MODEL), BF16)],
        [pltpu.VMEM((ts, IN_TOTAL), F32), pltpu.VMEM((POOL_HALO, POOL_WIDTH), F32)],
        (x, g_pre, w_in, pool_w, pool_scale, w_pool_out), tasks)


def _fwd_rnn_merge(urx, urg, gp, gr, ypool, x, conv_w, conv_b, wg, bg, lam, w_rg_out, w_o, g_post, ts, tasks=()):
    s = x.shape[0]

    def body(urx_ref, urg_ref, gp_ref, gr_ref, yp_ref, x_ref, cw_ref, cb_ref, wg_ref, bg_ref, lam_ref, wrg_ref, wo_ref,
             gpost_ref, xc_ref, r_ref, ig_ref, h_ref, yr_ref, mo_ref, x1_ref, gl_ref, gg_ref, sp_ref, sr_ref,
             halo_scr, carry_scr):
        i = pl.program_id(0)

        @pl.when(i == 0)
        def _():
            halo_scr[...] = jnp.zeros_like(halo_scr)
            carry_scr[...] = jnp.zeros_like(carry_scr)

        urx_v = urx_ref[...]
        ext = jnp.concatenate([halo_scr[...], urx_v], axis=0)
        halo_scr[...] = urx_v[ts - CONV_HALO:, :]
        cw = cw_ref[...]
        xc = (cb_ref[...] + cw[3:4] * urx_v + cw[2:3] * _shift_down(ext, 1, CONV_HALO)
              + cw[1:2] * _shift_down(ext, 2, CONV_HALO) + cw[0:1] * _shift_down(ext, 3, CONV_HALO))
        xc_ref[...] = xc
        xcb = xc.astype(BF16)
        lin = []
        for gate in range(2):
            parts = [jnp.dot(xcb[:, q * GATE_BLOCK:(q + 1) * GATE_BLOCK], wg_ref[gate, q], preferred_element_type=F32)
                     for q in range(GATE_BLOCKS)]
            lin.append(jnp.concatenate(parts, axis=1) + bg_ref[gate:gate + 1, :])
        r = _sigmoid(lin[0])
        ig = _sigmoid(lin[1])
        r_ref[...] = r
        ig_ref[...] = ig
        first_row = (i * ts + lax.broadcasted_iota(jnp.int32, (ts, 1), 0)) == 0
        _, a, _, mult = _lru_coeffs(r, lam_ref[...], first_row)
        h = _scan_fwd(a, mult * ig * xc, carry_scr[0:1, :])
        carry_scr[0:1, :] = h[ts - 1:ts, :]
        h_ref[...] = h
        urg_v = urg_ref[...]
        gl, t = _gelu(urg_v)
        gl_ref[...] = gl.astype(BF16)
        gg_ref[...] = _gelu_grad(urg_v, t).astype(BF16)
        yr = _dot(h * gl, wrg_ref[...])
        yr_ref[...] = yr
        sp = _sigmoid(gp_ref[...])
        sr = _sigmoid(gr_ref[...])
        sp_ref[...] = sp.astype(BF16)
        sr_ref[...] = sr.astype(BF16)
        merged = sp * yp_ref[...] + sr * yr
        mo = _dot(merged, wo_ref[...])
        mo_ref[...] = mo
        y, _, _ = _rms_fwd(mo, gpost_ref[...])
        x1_ref[...] = x_ref[...] + y

    row = _rows(ts, D_MODEL)
    return _call(
        body, "fwd_rnn_merge", (s // ts,),
        [row] * 6 + [_resident(conv_w.shape), _resident((1, D_MODEL)), _resident(wg.shape), _resident(bg.shape),
                     _resident((1, D_MODEL)), _resident(w_rg_out.shape), _resident(w_o.shape), _resident((1, D_MODEL))],
        [row] * 11, [_sds((s, D_MODEL))] * 7 + [_sds((s, D_MODEL), BF16)] * 4,
        [pltpu.VMEM((CONV_HALO, D_MODEL), F32), pltpu.VMEM((8, D_MODEL), F32)],
        (urx, urg, gp, gr, ypool, x, conv_w, conv_b, wg, bg, lam, w_rg_out, w_o, g_post), tasks)


def _fwd_ffn(x1, g_pre, w_up, fcw, fcb, w_down, g_post, ts, tasks=()):
    s = x1.shape[0]

    def body(x1_ref, g_ref, wup_ref, fcw_ref, fcb_ref, wd_ref, gpost_ref,
             up_ref, gl_ref, gg_ref, h2_ref, dn_ref, x2_ref, up_scr, halo_scr):
        i = pl.program_id(0)

        @pl.when(i == 0)
        def _():
            halo_scr[...] = jnp.zeros_like(halo_scr)

        x1_v = x1_ref[...]
        h2, _, _ = _rms_fwd(x1_v, g_ref[...])
        h2 = h2.astype(BF16)
        h2_ref[...] = h2
        for j in range(N_CHIPS):
            up_scr[:, j * UP_SHARD:(j + 1) * UP_SHARD] = jnp.dot(h2, wup_ref[j], preferred_element_type=F32)
        up_ref[...] = up_scr[...].astype(BF16)
        ug = up_scr[:, 0:D_FF]
        ext = jnp.concatenate([halo_scr[...], ug], axis=0)
        halo_scr[...] = ug[ts - CONV_HALO:, :]
        w = fcw_ref[...]
        gh = (fcb_ref[...] + w[2:3] * ug + w[1:2] * _shift_down(ext, 1, CONV_HALO)
              + w[0:1] * _shift_down(ext, 2, CONV_HALO))
        gl, t = _gelu(gh)
        gl_ref[...] = gl.astype(BF16)
        gg_ref[...] = _gelu_grad(gh, t).astype(BF16)
        dn = _dot(gl * up_scr[:, D_FF:], wd_ref[...])
        dn_ref[...] = dn
        y, _, _ = _rms_fwd(dn, gpost_ref[...])
        x2_ref[...] = x1_v + y

    row = _rows(ts, D_MODEL)
    return _call(
        body, "fwd_ffn", (s // ts,),
        [row, _resident((1, D_MODEL)), _resident(w_up.shape), _resident(fcw.shape), _resident((1, D_FF)),
         _resident(w_down.shape), _resident((1, D_MODEL))],
        [_rows(ts, 2 * D_FF), _rows(ts, D_FF), _rows(ts, D_FF), row, row, row],
        [_sds((s, 2 * D_FF), BF16), _sds((s, D_FF), BF16), _sds((s, D_FF), BF16), _sds((s, D_MODEL), BF16),
         _sds((s, D_MODEL)), _sds((s, D_MODEL))],
        [pltpu.VMEM((ts, 2 * D_FF), F32), pltpu.VMEM((CONV_HALO, D_FF), F32)],
        (x1, g_pre, w_up, fcw, fcb, w_down, g_post), tasks)


def _ple_loss(x2, p, tgt, g_gate, w_gate, w_proj, g_post, ts):
    s = x2.shape[0]

    def body(x2_ref, p_ref, t_ref, gg_ref, wg_ref, wp_ref, gp_ref, dx2_ref, loss_ref, dwg_ref, dwp_ref, dgg_ref, dgp_ref):
        @pl.when(pl.program_id(0) == 0)
        def _():
            loss_ref[...] = jnp.zeros_like(loss_ref)
            dwg_ref[...] = jnp.zeros_like(dwg_ref)
            dwp_ref[...] = jnp.zeros_like(dwp_ref)
            dgg_ref[...] = jnp.zeros_like(dgg_ref)
            dgp_ref[...] = jnp.zeros_like(dgp_ref)

        x2_v = x2_ref[...]
        n3, xh3, r3 = _rms_fwd(x2_v, gg_ref[...])
        pg = _sigmoid(_dot(n3, wg_ref[...]))
        pb = p_ref[...].astype(BF16)
        q = jnp.concatenate([jnp.dot(pb, wp_ref[j], preferred_element_type=F32) for j in range(N_CHIPS)], axis=1)
        ple, qh, rq = _rms_fwd(q, gp_ref[...])
        e = x2_v + pg * ple - t_ref[...]
        loss_ref[...] += 0.5 * jnp.sum(jnp.mean(e * e, axis=-1, keepdims=True), axis=0, keepdims=True)
        dy = e * (1.0 / D_MODEL)
        dpgl = dy * ple * pg * (1.0 - pg)
        dwg_ref[...] += _dot_tn(n3, dpgl)
        dx3, dgg = _rms_bwd(xh3, r3, gg_ref[...], _dot_nt(dpgl, wg_ref[...]))
        dgg_ref[...] += dgg
        dq, dgp = _rms_bwd(qh, rq, gp_ref[...], dy * pg)
        dgp_ref[...] += dgp
        for j in range(N_CHIPS):
            dwp_ref[j] += _dot_tn(pb, dq[:, j * 256:(j + 1) * 256])
        dx2_ref[...] = dy + dx3

    row = _rows(ts, D_MODEL)
    vec = _acc((1, D_MODEL))
    return pl.pallas_call(
        body, name="ple_loss", grid=(s // ts,),
        in_specs=[row, _rows(ts, PLE_DIM), row, _resident((1, D_MODEL)), _resident(w_gate.shape), _resident(w_proj.shape),
                  _resident((1, D_MODEL))],
        out_specs=[row, _acc((1, 128)), _acc(w_gate.shape), _acc(w_proj.shape), vec, vec],
        out_shape=[_sds((s, D_MODEL)), _sds((1, 128)), _sds(w_gate.shape), _sds(w_proj.shape), _sds((1, D_MODEL)),
                   _sds((1, D_MODEL))],
        compiler_params=_params(),
    )(x2, p, tgt, g_gate, w_gate, w_proj, g_post)


def _bwd_ffn_down(dx2, dn, up, gl, gg, fcw, w_down, g_post, ts):
    s = dx2.shape[0]
    nt = s // ts

    def body(dx2_ref, dn_ref, up_ref, gl_ref, gg_ref, fcw_ref, wd_ref, gpost_ref,
             dup_ref, dwd_ref, dfcw_ref, dfcb_ref, dgp_ref, carry_scr):
        i = pl.program_id(0)

        @pl.when(i == 0)
        def _():
            carry_scr[...] = jnp.zeros_like(carry_scr)
            dwd_ref[...] = jnp.zeros_like(dwd_ref)
            dfcw_ref[...] = jnp.zeros_like(dfcw_ref)
            dfcb_ref[...] = jnp.zeros_like(dfcb_ref)
            dgp_ref[...] = jnp.zeros_like(dgp_ref)

        _, xh, r = _rms_fwd(dn_ref[...], gpost_ref[...])
        ddn, dgp = _rms_bwd(xh, r, gpost_ref[...], dx2_ref[...])
        dgp_ref[...] += dgp
        dhid = _dot_nt(ddn, wd_ref[...])
        ug = up_ref[:, 0:D_FF].astype(F32)
        uv = up_ref[:, D_FF:].astype(F32)
        gl = gl_ref[...].astype(F32)
        w = fcw_ref[...]
        dwd_ref[...] += _dot_tn(gl * uv, ddn)
        dgh = dhid * uv * gg_ref[...].astype(F32)
        dup_ref[:, D_FF:] = (dhid * gl).astype(BF16)
        extd = jnp.concatenate([dgh, carry_scr[...]], axis=0)
        carry_scr[...] = dgh[0:CONV_HALO, :]
        d1 = _shift_up(extd, 1, ts)
        d2 = _shift_up(extd, 2, ts)
        dup_ref[:, 0:D_FF] = (w[2:3] * dgh + w[1:2] * d1 + w[0:1] * d2).astype(BF16)
        dfcw_ref[2:3, :] += jnp.sum(ug * dgh, axis=0, keepdims=True)
        dfcw_ref[1:2, :] += jnp.sum(ug * d1, axis=0, keepdims=True)
        dfcw_ref[0:1, :] += jnp.sum(ug * d2, axis=0, keepdims=True)
        dfcb_ref[...] += jnp.sum(dgh, axis=0, keepdims=True)

    row = _rows(ts, D_MODEL, nt)
    wide = _rows(ts, D_FF, nt)
    return pl.pallas_call(
        body, name="bwd_ffn_down", grid=(nt,),
        in_specs=[row, row, _rows(ts, 2 * D_FF, nt), wide, wide, _resident(fcw.shape), _resident(w_down.shape),
                  _resident((1, D_MODEL))],
        out_specs=[_rows(ts, 2 * D_FF, nt), _acc(w_down.shape), _acc(fcw.shape), _acc((1, D_FF)), _acc((1, D_MODEL))],
        out_shape=[_sds((s, 2 * D_FF), BF16), _sds(w_down.shape), _sds(fcw.shape), _sds((1, D_FF)), _sds((1, D_MODEL))],
        scratch_shapes=[pltpu.VMEM((CONV_HALO, D_FF), F32)],
        compiler_params=_params(),
    )(dx2, dn, up, gl, gg, fcw, w_down, g_post)


def _bwd_ffn_up(dup, x1, dx2, g_pre, w_up, ts, tasks=()):
    s = x1.shape[0]

    def body(dup_ref, x1_ref, dx2_ref, g_ref, wup_ref, dx1_ref, dg_ref):
        @pl.when(pl.program_id(0) == 0)
        def _():
            dg_ref[...] = jnp.zeros_like(dg_ref)

        _, xh, r = _rms_fwd(x1_ref[...], g_ref[...])
        dh2 = _dot_nt(dup_ref[:, 0:UP_SHARD], wup_ref[0])
        for j in range(1, N_CHIPS):
            dh2 = dh2 + _dot_nt(dup_ref[:, j * UP_SHARD:(j + 1) * UP_SHARD], wup_ref[j])
        dx, dg = _rms_bwd(xh, r, g_ref[...], dh2)
        dg_ref[...] += dg
        dx1_ref[...] = dx2_ref[...] + dx

    row = _rows(ts, D_MODEL)
    return _call(
        body, "bwd_ffn_up", (s // ts,),
        [_rows(ts, 2 * D_FF), row, row, _resident((1, D_MODEL)), _resident(w_up.shape)],
        [row, _acc((1, D_MODEL))], [_sds((s, D_MODEL)), _sds((1, D_MODEL))], [],
        (dup, x1, dx2, g_pre, w_up), tasks)


def _dw_up(h2, dup, ts, tasks=()):
    s = h2.shape[0]
    ts = min(DW_TILES * ts, s)

    def body(h2_ref, dup_ref, out_ref):
        @pl.when(pl.program_id(1) == 0)
        def _():
            out_ref[...] = jnp.zeros_like(out_ref)

        out_ref[0] += _dot_tn(h2_ref[...], dup_ref[...])

    return _call(
        body, "dw_up", (N_CHIPS, s // ts),
        [pl.BlockSpec((ts, D_MODEL), lambda j, i: (i, 0)), pl.BlockSpec((ts, UP_SHARD), lambda j, i: (i, j))],
        [pl.BlockSpec((1, D_MODEL, UP_SHARD), lambda j, i: (j, 0, 0))], [_sds((N_CHIPS, D_MODEL, UP_SHARD))], [],
        (h2, dup), tasks)


def _bwd_merge(dx1, mo, sp, sr, ypool, yrnn, g_post, w_o, ts, tasks=()):
    s = dx1.shape[0]

    def body(dx1_ref, mo_ref, sp_ref, sr_ref, yp_ref, yr_ref, g_ref, wo_ref,
             dgp_ref, dgr_ref, dyp_ref, dyr_ref, dwo_ref, dg_ref):
        @pl.when(pl.program_id(0) == 0)
        def _():
            dwo_ref[...] = jnp.zeros_like(dwo_ref)
            dg_ref[...] = jnp.zeros_like(dg_ref)

        _, xh, r = _rms_fwd(mo_ref[...], g_ref[...])
        dmo, dg = _rms_bwd(xh, r, g_ref[...], dx1_ref[...])
        dg_ref[...] += dg
        dmerged = _dot_nt(dmo, wo_ref[...])
        sp = sp_ref[...].astype(F32)
        sr = sr_ref[...].astype(F32)
        yp = yp_ref[...]
        yr = yr_ref[...]
        dwo_ref[...] += _dot_tn(sp * yp + sr * yr, dmo)
        dgp_ref[...] = (dmerged * yp * sp * (1.0 - sp)).astype(BF16)
        dgr_ref[...] = (dmerged * yr * sr * (1.0 - sr)).astype(BF16)
        dyp_ref[...] = (dmerged * sp).astype(BF16)
        dyr_ref[...] = (dmerged * sr).astype(BF16)

    row = _rows(ts, D_MODEL)
    return _call(
        body, "bwd_merge", (s // ts,),
        [row] * 6 + [_resident((1, D_MODEL)), _resident(w_o.shape)],
        [row] * 4 + [_acc(w_o.shape), _acc((1, D_MODEL))],
        [_sds((s, D_MODEL), BF16)] * 4 + [_sds(w_o.shape), _sds((1, D_MODEL))], [],
        (dx1, mo, sp, sr, ypool, yrnn, g_post, w_o), tasks)


def _bwd_rnn(dyr, urx, gl, gg, xc, r, ig, h, conv_w, wg, lam, w_rg_out, ts, tasks=()):
    s = urx.shape[0]
    nt = s // ts
    halo_blocks = ts // CONV_HALO

    def body(dyr_ref, urx_ref, gl_ref, gg_ref, xc_ref, r_ref, ig_ref, h_ref, hh_ref, cw_ref, wg_ref, lam_ref, wrg_ref,
             durx_ref, durg_ref, dwrg_ref, dwg_ref, dcw_ref, dcb_ref, dbg_ref, dlam_ref, mu_scr, carry_scr):
        i = pl.program_id(0)
        k = nt - 1 - i

        @pl.when(i == 0)
        def _():
            mu_scr[...] = jnp.zeros_like(mu_scr)
            carry_scr[...] = jnp.zeros_like(carry_scr)
            dwrg_ref[...] = jnp.zeros_like(dwrg_ref)
            dwg_ref[...] = jnp.zeros_like(dwg_ref)
            dcw_ref[...] = jnp.zeros_like(dcw_ref)
            dcb_ref[...] = jnp.zeros_like(dcb_ref)
            dbg_ref[...] = jnp.zeros_like(dbg_ref)
            dlam_ref[...] = jnp.zeros_like(dlam_ref)

        row = lax.broadcasted_iota(jnp.int32, (ts, 1), 0)
        first_row = (k * ts + row) == 0
        h = h_ref[...]
        dyr_v = dyr_ref[...]
        dhr = _dot_nt(dyr_v, wrg_ref[...])
        gl = gl_ref[...].astype(F32)
        dwrg_ref[...] += _dot_tn(h * gl, dyr_v)
        durg_ref[...] = (dhr * h * gg_ref[...].astype(F32)).astype(BF16)
        r_v = r_ref[...]
        ig_v = ig_ref[...]
        xc_v = xc_ref[...]
        lam_v = lam_ref[...]
        c8, a, m2, mult = _lru_coeffs(r_v, lam_v, first_row)
        b = jnp.where(row == ts - 1, 1.0, pltpu.roll(a, ts - 1, 0))
        lt = _scan_bwd(b, dhr * gl, mu_scr[0:1, :])
        mu_scr[0:1, :] = a[0:1, :] * lt[0:1, :]
        h_before = jnp.where(k > 0, hh_ref[CONV_HALO - 1:CONV_HALO, :], 0.0)
        hprev = jnp.where(row == 0, h_before, pltpu.roll(h, 1, 0))
        dmult = lt * ig_v * xc_v
        da = lt * hprev - jnp.where(first_row, 0.0, dmult * a * lax.rsqrt(m2))
        dla = da * a
        dlam_ref[...] += jnp.sum(dla * r_v, axis=0, keepdims=True)
        dlr = (dla * (-c8)) * r_v * (1.0 - r_v)
        dli = (lt * mult * xc_v) * ig_v * (1.0 - ig_v)
        dbg_ref[0:1, :] += jnp.sum(dlr, axis=0, keepdims=True)
        dbg_ref[1:2, :] += jnp.sum(dli, axis=0, keepdims=True)
        xcb = xc_v.astype(BF16)
        parts = []
        for q in range(GATE_BLOCKS):
            blk = slice(q * GATE_BLOCK, (q + 1) * GATE_BLOCK)
            dlr_q = dlr[:, blk].astype(BF16)
            dli_q = dli[:, blk].astype(BF16)
            parts.append(_dot_nt(dlr_q, wg_ref[0, q]) + _dot_nt(dli_q, wg_ref[1, q]))
            dwg_ref[0, q] += _dot_tn(xcb[:, blk], dlr_q)
            dwg_ref[1, q] += _dot_tn(xcb[:, blk], dli_q)
        dxc = lt * mult * ig_v + jnp.concatenate(parts, axis=1)
        extd = jnp.concatenate([dxc, carry_scr[...]], axis=0)
        carry_scr[...] = dxc[0:CONV_HALO, :]
        cw = cw_ref[...]
        urx_v = urx_ref[...]
        durx = cw[3:4] * dxc
        dcw_ref[3:4, :] += jnp.sum(urx_v * dxc, axis=0, keepdims=True)
        for j in (1, 2, 3):
            dj = _shift_up(extd, j, ts)
            durx = durx + cw[3 - j:4 - j] * dj
            dcw_ref[3 - j:4 - j, :] += jnp.sum(urx_v * dj, axis=0, keepdims=True)
        durx_ref[...] = durx.astype(BF16)
        dcb_ref[...] += jnp.sum(dxc, axis=0, keepdims=True)

        @pl.when(i == nt - 1)
        def _():
            dlam_ref[...] = dlam_ref[...] * (LRU_C * jax.nn.sigmoid(-lam_v))

    row_spec = _rows(ts, D_MODEL, nt)
    halo_spec = pl.BlockSpec((CONV_HALO, D_MODEL), lambda i: (jnp.maximum((nt - 1 - i) * halo_blocks - 1, 0), 0))
    vec = _acc((1, D_MODEL))
    return _call(
        body, "bwd_rnn", (nt,),
        [row_spec] * 8 + [halo_spec, _resident(conv_w.shape), _resident(wg.shape), _resident((1, D_MODEL)),
                          _resident(w_rg_out.shape)],
        [row_spec, row_spec, _acc(w_rg_out.shape), _acc(wg.shape), _acc(conv_w.shape), vec, _acc((2, D_MODEL)), vec],
        [_sds((s, D_MODEL), BF16), _sds((s, D_MODEL), BF16), _sds(w_rg_out.shape), _sds(wg.shape), _sds(conv_w.shape),
         _sds((1, D_MODEL)), _sds((2, D_MODEL)), _sds((1, D_MODEL))],
        [pltpu.VMEM((8, D_MODEL), F32), pltpu.VMEM((CONV_HALO, D_MODEL), F32)],
        (dyr, urx, gl, gg, xc, r, ig, h, h, conv_w, wg, lam, w_rg_out), tasks)


def _bwd_pool(dyp, d, pool_w, pool_scale, w_pool_out, ts, tasks=()):
    s = d.shape[0]
    nt = s // ts

    def body(dyp_ref, d_ref, pw_ref, ps_ref, wpo_ref, dzp_ref, dwpo_ref, dpw_ref, dps_ref, carry_scr):
        i = pl.program_id(0)
        k = nt - 1 - i

        @pl.when(i == 0)
        def _():
            carry_scr[...] = jnp.zeros_like(carry_scr)
            dwpo_ref[...] = jnp.zeros_like(dwpo_ref)
            dpw_ref[...] = jnp.zeros_like(dpw_ref)
            dps_ref[...] = jnp.zeros_like(dps_ref)

        dyp_v = dyp_ref[...]
        d_v = d_ref[...]
        ps = ps_ref[...]
        dypre = _dot_nt(dyp_v[:, 0:256], wpo_ref[0])
        for j in range(1, N_CHIPS):
            dypre = dypre + _dot_nt(dyp_v[:, j * 256:(j + 1) * 256], wpo_ref[j])
        y4 = jnp.concatenate([_dot(d_v[:, g * 128:(g + 1) * 128], pw_ref[g]) for g in range(POOL_GROUPS)], axis=1)
        ypre = (y4 * ps).astype(BF16)
        for j in range(N_CHIPS):
            dwpo_ref[j] += _dot_tn(ypre, dyp_v[:, j * 256:(j + 1) * 256])
        dps_ref[...] += jnp.sum(dypre * y4, axis=0, keepdims=True)
        dy4 = dypre * ps
        t = k * ts + lax.broadcasted_iota(jnp.int32, (ts, 1), 0)
        for g, w in enumerate(POOL_WINDOWS):
            lanes = slice(g * POOL_GROUP_DIM, (g + 1) * POOL_GROUP_DIM)
            dd = _dot_nt(dy4[:, lanes], pw_ref[g])
            dpw_ref[g] += _dot_tn(d_v[:, lanes], dy4[:, lanes])
            e = dd * (1.0 / jnp.minimum(t + 1, w).astype(F32))
            acc = jnp.concatenate([e, carry_scr[:, lanes]], axis=0)
            carry_scr[:, lanes] = e[0:POOL_HALO, :]
            n = ts + POOL_HALO
            sh = 1
            while sh < w:
                acc = acc + pltpu.roll(acc, n - sh, 0)
                sh *= 2
            dzp_ref[:, lanes] = (acc[:ts, :] - dd).astype(BF16)

    return _call(
        body, "bwd_pool", (nt,),
        [_rows(ts, D_MODEL, nt), _rows(ts, POOL_WIDTH, nt), _resident(pool_w.shape), _resident((1, POOL_WIDTH)),
         _resident(w_pool_out.shape)],
        [_rows(ts, POOL_WIDTH, nt), _acc(w_pool_out.shape), _acc(pool_w.shape), _acc((1, POOL_WIDTH))],
        [_sds((s, POOL_WIDTH), BF16), _sds(w_pool_out.shape), _sds(pool_w.shape), _sds((1, POOL_WIDTH))],
        [pltpu.VMEM((POOL_HALO, POOL_WIDTH), F32)],
        (dyp, d, pool_w, pool_scale, w_pool_out), tasks)


def _assemble_dz(dz_scr, dzp_ref, durx_ref, durg_ref, dgp_ref, dgr_ref):
    dz_scr[:, 0:512] = dzp_ref[...]
    dz_scr[:, 512:1536] = durx_ref[...]
    dz_scr[:, 1536:2560] = durg_ref[...]
    dz_scr[:, 2560:3584] = dgp_ref[...]
    dz_scr[:, 3584:4608] = dgr_ref[...]


def _dw_in(h1, dzp, durx, durg, dgp, dgr, ts, tasks=()):
    s = h1.shape[0]
    ts = min(2 * ts, s)
    nt = s // ts
    half = D_MODEL // 2

    def body(h1_ref, dzp_ref, durx_ref, durg_ref, dgp_ref, dgr_ref, theirs_ref, mine_ref, got_ref, dz_scr, sems):
        o, i = pl.program_id(0), pl.program_id(1)

        @pl.when((o == 0) & (i == 0))
        def _():
            theirs_ref[...] = jnp.zeros_like(theirs_ref)
            mine_ref[...] = jnp.zeros_like(mine_ref)

        _assemble_dz(dz_scr, dzp_ref, durx_ref, durg_ref, dgp_ref, dgr_ref)
        for out_ref, which in ((theirs_ref, 0), (mine_ref, 1)):
            @pl.when(o == which)
            def _():
                for j in range(N_CHIPS):
                    out_ref[j] += _dot_tn(h1_ref[...], dz_scr[:, j * IN_SHARD:(j + 1) * IN_SHARD])

        x, y, c, _, _ = _place()
        send = _remote(theirs_ref, got_ref, sems.at[0], sems.at[1], (x, y, 1 - c))
        pl.when((o == 1) & (i == 0))(send.start)
        pl.when((o == 1) & (i == nt - 1))(send.wait)

    def h1_cols(o, i):
        c = lax.axis_index("c")
        return i, jnp.where(o == 0, 1 - c, c)

    rows = lambda width: pl.BlockSpec((ts, width), lambda o, i: (i, 0))
    shape = (N_CHIPS, half, IN_SHARD)
    whole = pl.BlockSpec(shape, lambda o, i: (0, 0, 0))
    return _call(
        body, "dw_in", (2, nt), [pl.BlockSpec((ts, half), h1_cols), rows(POOL_WIDTH)] + [rows(D_MODEL)] * 4,
        [whole, whole, pl.BlockSpec(memory_space=pltpu.HBM)], [_sds(shape)] * 3,
        [pltpu.VMEM((ts, IN_TOTAL), BF16), pltpu.SemaphoreType.DMA((2,))], (h1, dzp, durx, durg, dgp, dgr), tasks)


def _bwd_in(dzp, durx, durg, dgp, dgr, x, dx1, g_pre, w_in, ts, tasks=()):
    s = x.shape[0]

    def body(dzp_ref, durx_ref, durg_ref, dgp_ref, dgr_ref, x_ref, dx1_ref, g_ref, win_ref, gx_ref, dg_ref, dz_scr):
        @pl.when(pl.program_id(0) == 0)
        def _():
            dg_ref[...] = jnp.zeros_like(dg_ref)

        _assemble_dz(dz_scr, dzp_ref, durx_ref, durg_ref, dgp_ref, dgr_ref)
        _, xh, r = _rms_fwd(x_ref[...], g_ref[...])
        dh1 = _dot_nt(dz_scr[:, 0:IN_SHARD], win_ref[0])
        for j in range(1, N_CHIPS):
            dh1 = dh1 + _dot_nt(dz_scr[:, j * IN_SHARD:(j + 1) * IN_SHARD], win_ref[j])
        dx, dg = _rms_bwd(xh, r, g_ref[...], dh1)
        dg_ref[...] += dg
        gx_ref[...] = dx1_ref[...] + dx

    row = _rows(ts, D_MODEL)
    return _call(
        body, "bwd_in", (s // ts,),
        [_rows(ts, POOL_WIDTH)] + [row] * 6 + [_resident((1, D_MODEL)), _resident(w_in.shape)],
        [row, _acc((1, D_MODEL))], [_sds((s, D_MODEL)), _sds((1, D_MODEL))],
        [pltpu.VMEM((ts, IN_TOTAL), BF16)], (dzp, durx, durg, dgp, dgr, x, dx1, g_pre, w_in), tasks)


def _place():
    x, y, c = lax.axis_index("x"), lax.axis_index("y"), lax.axis_index("c")
    others = [(1 - x, y), (x, 1 - y), (1 - x, 1 - y)]
    return x, y, c, 2 * x + y, others


def _remote(src, dst, send_sem, recv_sem, to):
    return pltpu.make_async_remote_copy(src_ref=src, dst_ref=dst, send_sem=send_sem, recv_sem=recv_sem,
                                        device_id=to, device_id_type=MESH)


def _own_slots(ws, dtypes, name, tasks=()):
    n = len(ws)
    hbm = pl.BlockSpec(memory_space=pltpu.HBM)

    def body(*refs):
        srcs, outs, f32_bufs, cast_bufs, sems = refs[:n], refs[n:2 * n], refs[2 * n:3 * n], refs[3 * n:4 * n], refs[4 * n]
        me = _place()[3]
        loads = [pltpu.make_async_copy(srcs[k], f32_bufs[k], sems.at[k, 0]) for k in range(n)]
        stores = [pltpu.make_async_copy(cast_bufs[k], outs[k].at[me], sems.at[k, 1]) for k in range(n)]
        for cp in loads:
            cp.start()
        for k in range(n):
            loads[k].wait()
            cast_bufs[k][...] = f32_bufs[k][...].astype(dtypes[k])
            stores[k].start()
        for cp in stores:
            cp.wait()

    return _call(
        body, name, (), [hbm] * n, [hbm] * n, [_sds((N_CHIPS,) + w.shape, dt) for w, dt in zip(ws, dtypes)],
        [pltpu.VMEM(w.shape, F32) for w in ws] + [pltpu.VMEM(w.shape, dt) for w, dt in zip(ws, dtypes)]
        + [pltpu.SemaphoreType.DMA((n, 2))],
        [pltpu.with_memory_space_constraint(w, pltpu.HBM) for w in ws], tasks)


def _run(tasks, name):
    if isinstance(tasks, _Task):
        return _call(None, name, (), [], [], [], [], (), (tasks,))[1][0]
    return _call(None, name, (), [], [], [], [], (), tuple(tasks))[1]


def _gather_task(bufs, relay_steps=(0, 0)):
    n = len(bufs)
    NBR_X, NBR_Y, QUARTER_VIA_Y, QUARTER_VIA_X, SIB_X, SIB_Y, SIB_DIAG = range(7)

    def parts(out):
        x, y, c, me, _ = _place()
        ah = out.shape[1] // 2
        q = ah // 2 if (ah // 2) % 16 == 0 else ah
        return c * ah, ah, q

    def copy(out, w, k, chip, row0, rows, to, sems):
        slot = out.at[chip, pl.ds(row0, rows)]
        return _remote(slot, slot, sems[0].at[w, k], sems[1].at[w, k], to)

    def plan(out, w, sems):
        x, y, c, me, _ = _place()
        row0, ah, q = parts(out)
        xn, yn, dg = 2 * (1 - x) + y, 2 * x + (1 - y), 2 * (1 - x) + (1 - y)
        to_x, to_y, sib = (1 - x, y, c), (x, 1 - y, c), (x, y, 1 - c)
        other = (1 - c) * ah
        cp = functools.partial(copy, out, w, sems=sems)
        sends = {NBR_X: cp(NBR_X, me, row0, ah, to_x), NBR_Y: cp(NBR_Y, me, row0, ah, to_y),
                 QUARTER_VIA_Y: cp(QUARTER_VIA_Y, xn, row0, q, to_y), SIB_X: cp(SIB_X, xn, row0, ah, sib),
                 SIB_Y: cp(SIB_Y, yn, row0, ah, sib), SIB_DIAG: cp(SIB_DIAG, dg, row0, ah, sib)}
        lands = {NBR_X: cp(NBR_X, xn, row0, ah, to_x), NBR_Y: cp(NBR_Y, yn, row0, ah, to_y),
                 QUARTER_VIA_Y: cp(QUARTER_VIA_Y, dg, row0, q, to_y), SIB_X: cp(SIB_X, xn, other, ah, sib),
                 SIB_Y: cp(SIB_Y, yn, other, ah, sib), SIB_DIAG: cp(SIB_DIAG, dg, other, ah, sib)}
        if q < ah:
            sends[QUARTER_VIA_X] = cp(QUARTER_VIA_X, yn, row0 + q, ah - q, to_x)
            lands[QUARTER_VIA_X] = cp(QUARTER_VIA_X, dg, row0 + q, ah - q, to_x)
        return sends, lands

    def start(ins, outs, sems):
        for w, out in enumerate(outs):
            sends, _ = plan(out, w, sems)
            sends[NBR_X].start()
            sends[NBR_Y].start()

    def pass_neighbours(ins, outs, sems):
        for w, out in enumerate(outs):
            sends, lands = plan(out, w, sems)
            lands[NBR_X].wait_recv()
            sends[QUARTER_VIA_Y].start()
            sends[SIB_X].start()
            lands[NBR_Y].wait_recv()
            if QUARTER_VIA_X in sends:
                sends[QUARTER_VIA_X].start()
            sends[SIB_Y].start()

    def pass_diagonal(ins, outs, sems):
        for w, out in enumerate(outs):
            sends, lands = plan(out, w, sems)
            lands[QUARTER_VIA_Y].wait_recv()
            if QUARTER_VIA_X in lands:
                lands[QUARTER_VIA_X].wait_recv()
            sends[SIB_DIAG].start()

    def finish(ins, outs, sems):
        for w, out in enumerate(outs):
            sends, lands = plan(out, w, sems)
            for k in (SIB_X, SIB_Y, SIB_DIAG):
                lands[k].wait_recv()
        for w, out in enumerate(outs):
            sends, _ = plan(out, w, sems)
            for cp in sends.values():
                cp.wait_send()

    return _Task(bufs, [_sds(b.shape, b.dtype) for b in bufs], {i: i for i in range(n)},
                 [pltpu.SemaphoreType.DMA((n, 7)), pltpu.SemaphoreType.DMA((n, 7))], start, finish,
                 [(pass_neighbours, relay_steps[0]), (pass_diagonal, relay_steps[1])])


def _halves_task(grads):
    n = len(grads)

    def copy(src, out, w, sems):
        x, y, c, _, _ = _place()
        ah = out.shape[1]
        return _remote(src.at[:, pl.ds((1 - c) * ah, ah)], out, sems[0].at[w], sems[1].at[w], (x, y, 1 - c))

    def start(ins, outs, sems):
        for w, (src, out) in enumerate(zip(ins, outs)):
            copy(src, out, w, sems).start()

    def finish(ins, outs, sems):
        for w, (src, out) in enumerate(zip(ins, outs)):
            copy(src, out, w, sems).wait()

    return _Task(grads, [_sds((g.shape[0], g.shape[1] // 2, g.shape[2]), g.dtype) for g in grads], {},
                 [pltpu.SemaphoreType.DMA((n,)), pltpu.SemaphoreType.DMA((n,))], start, finish)


def _exchange_task(sends, accs):
    n = len(accs)
    given = [s for s in sends if s is not None]

    def copies(ins, outs, sems):
        send_refs = iter(ins[:len(given)])
        srcs = [next(send_refs) if s is not None else None for s in sends]
        x, y, c, me, others = _place()
        for w, out in enumerate(outs):
            for j, (ox, oy) in enumerate(others):
                src = out.at[me] if srcs[w] is None else srcs[w].at[2 * ox + oy]
                yield _remote(src, out.at[me], sems[0].at[w, j], sems[1].at[w, j], (ox, oy, c))

    def start(ins, outs, sems):
        for cp in copies(ins, outs, sems):
            cp.start()

    def finish(ins, outs, sems):
        x, y, c, _, others = _place()
        for w, out in enumerate(outs):
            for j, (ox, oy) in enumerate(others):
                slot = out.at[2 * ox + oy]
                _remote(slot, slot, sems[0].at[w, j], sems[1].at[w, j], (ox, oy, c)).wait_recv()
        for cp in copies(ins, outs, sems):
            cp.wait_send()

    return _Task(given + list(accs), [_sds(a.shape, a.dtype) for a in accs], {len(given) + i: i for i in range(n)},
                 [pltpu.SemaphoreType.DMA((n, 3)), pltpu.SemaphoreType.DMA((n, 3))], start, finish)


def _swap_task(arrays):
    n = len(arrays)

    def copy(src, out, w, sems):
        x, y, c, _, _ = _place()
        return _remote(src, out, sems[0].at[w], sems[1].at[w], (x, y, 1 - c))

    def start(ins, outs, sems):
        for w, (src, out) in enumerate(zip(ins, outs)):
            copy(src, out, w, sems).start()

    def finish(ins, outs, sems):
        for w, (src, out) in enumerate(zip(ins, outs)):
            copy(src, out, w, sems).wait()

    return _Task(arrays, [_sds(a.shape, a.dtype) for a in arrays], {},
                 [pltpu.SemaphoreType.DMA((n,)), pltpu.SemaphoreType.DMA((n,))], start, finish)


def _all_devices_task(arrays):
    n = len(arrays)
    flips = [(dx, dy, dc) for dx in (0, 1) for dy in (0, 1) for dc in (0, 1)][1:]

    def peers():
        x, y, c, _, _ = _place()
        flip = lambda v, d: 1 - v if d else v
        return 4 * x + 2 * y + c, [(flip(x, dx), flip(y, dy), flip(c, dc)) for dx, dy, dc in flips]

    def start(ins, outs, sems):
        me, others = peers()
        for w, (src, out) in enumerate(zip(ins, outs)):
            pltpu.make_async_copy(src, out.at[me], sems[2].at[w]).start()
            for k, peer in enumerate(others):
                _remote(src, out.at[me], sems[0].at[w, k], sems[1].at[w, k], peer).start()

    def finish(ins, outs, sems):
        me, others = peers()
        for w, (src, out) in enumerate(zip(ins, outs)):
            for k, (px, py, pc) in enumerate(others):
                slot = out.at[4 * px + 2 * py + pc]
                _remote(slot, slot, sems[0].at[w, k], sems[1].at[w, k], (px, py, pc)).wait_recv()
            for k, peer in enumerate(others):
                _remote(src, out.at[me], sems[0].at[w, k], sems[1].at[w, k], peer).wait_send()
            pltpu.make_async_copy(src, out.at[me], sems[2].at[w]).wait()

    return _Task(arrays, [_sds((8,) + a.shape, a.dtype) for a in arrays], {},
                 [pltpu.SemaphoreType.DMA((n, 7)), pltpu.SemaphoreType.DMA((n, 7)), pltpu.SemaphoreType.DMA((n,))],
                 start, finish)


def _share_task(shares):
    n = len(shares)

    def copy(out, w, sems, slot):
        x, y, c, _, _ = _place()
        return _remote(out.at[slot], out.at[slot], sems[0].at[w], sems[1].at[w], (x, y, 1 - c))

    def start(ins, outs, sems):
        c = _place()[2]
        for w, out in enumerate(outs):
            copy(out, w, sems, c).start()

    def finish(ins, outs, sems):
        c = _place()[2]
        for w, out in enumerate(outs):
            copy(out, w, sems, 1 - c).wait_recv()
        for w, out in enumerate(outs):
            copy(out, w, sems, c).wait_send()

    return _Task(shares, [_sds(s.shape, s.dtype) for s in shares], {i: i for i in range(n)},
                 [pltpu.SemaphoreType.DMA((n,)), pltpu.SemaphoreType.DMA((n,))], start, finish)


TILE_BYTES = 2 * 1024 * 1024
PARTIAL_TILE_BYTES = 1024 * 1024


def _in_hbm(t):
    return pltpu.with_memory_space_constraint(t, pltpu.HBM)


def _row_tile(rows, cols, limit=TILE_BYTES):
    best = 8
    for tr in range(8, rows + 1, 8):
        if rows % tr == 0 and tr * cols * 4 <= limit:
            best = tr
    assert rows % best == 0, (rows, cols)
    return best


def _chip_partial(g, got, place, wire_dtype):
    ns, ah, b = got.shape
    sharded = ns == N_CHIPS
    tr = _row_tile(ah, b, PARTIAL_TILE_BYTES)
    nb = ah // tr
    first = 0 if g.shape[1] == ah else nb

    def body(place_ref, *refs):
        g_refs, got_refs, outs = refs[:ns], refs[ns:2 * ns], refs[2 * ns:]
        parts = [g_refs[k][0] + got_refs[k][0] for k in range(ns)]
        own = parts[0]
        if sharded:
            for k in range(ns):
                outs[0][k] = parts[k].astype(wire_dtype)
                if k:
                    own = jnp.where(place_ref[0] == k, parts[k], own)
        outs[-1][0] = own.astype(wire_dtype)

    blk = (1, tr, b)
    in_specs = ([pl.BlockSpec(blk, lambda i, s, k=k: (k, s[1] * first + i, 0)) for k in range(ns)]
                + [pl.BlockSpec(blk, lambda i, s, k=k: (k, i, 0)) for k in range(ns)])
    acc_spec = pl.BlockSpec(blk, lambda i, s: (s[0], i, 0))
    acc_shape = _sds((N_CHIPS, ah, b), wire_dtype)
    out = pl.pallas_call(
        body, name="grad_chip_partial",
        grid_spec=pltpu.PrefetchScalarGridSpec(
            num_scalar_prefetch=1, grid=(nb,), in_specs=in_specs,
            out_specs=[pl.BlockSpec((ns, tr, b), lambda i, s: (0, i, 0)), acc_spec] if sharded else [acc_spec]),
        out_shape=[acc_shape, acc_shape] if sharded else [acc_shape],
        compiler_params=pltpu.CompilerParams(dimension_semantics=("arbitrary",), vmem_limit_bytes=VMEM_LIMIT),
    )(place, *([g] * ns), *([got] * ns))
    return (out[0], out[1]) if sharded else (None, out[0])


def _chip_sum(acc, place):
    _, ah, b = acc.shape
    tr = _row_tile(ah, b)

    def body(place_ref, p_ref, out_ref):
        total = p_ref[0].astype(F32) + p_ref[1].astype(F32)
        total = total + p_ref[2].astype(F32)
        out_ref[0] = total + p_ref[3].astype(F32)

    return pl.pallas_call(
        body, name="grad_chip_sum",
        grid_spec=pltpu.PrefetchScalarGridSpec(
            num_scalar_prefetch=1, grid=(ah // tr,),
            in_specs=[pl.BlockSpec((N_CHIPS, tr, b), lambda i, s: (0, i, 0))],
            out_specs=pl.BlockSpec((1, tr, b), lambda i, s: (s[1], i, 0))),
        out_shape=_sds((2, ah, b)),
        compiler_params=pltpu.CompilerParams(dimension_semantics=("arbitrary",)),
    )(place, _in_hbm(acc))


def _adam_math(w, g, m, v):
    nm = ADAM_B1 * m + (1.0 - ADAM_B1) * g
    nv = ADAM_B2 * v + (1.0 - ADAM_B2) * (g * g)
    m_hat = nm / (1.0 - ADAM_B1 ** ADAM_STEP)
    v_hat = nv / (1.0 - ADAM_B2 ** ADAM_STEP)
    return -ADAM_LR * (m_hat / (jnp.sqrt(v_hat) + ADAM_EPS) + ADAM_WD * w), nm, nv


def _adamw(w, g, m, v):
    a, b = w.shape
    tr = _row_tile(a, b)

    def body(w_ref, g_ref, m_ref, v_ref, g_out, d_ref, nm_ref, nv_ref):
        g_out[...] = g_ref[...]
        d_ref[...], nm_ref[...], nv_ref[...] = _adam_math(w_ref[...], g_ref[...], m_ref[...], v_ref[...])

    blk = pl.BlockSpec((tr, b), lambda i: (i, 0))
    return pl.pallas_call(
        body, name="adamw", grid=(a // tr,),
        in_specs=[blk] * 4, out_specs=[blk] * 4, out_shape=[_sds((a, b))] * 4,
        compiler_params=pltpu.CompilerParams(dimension_semantics=("arbitrary",)),
    )(w, g, m, v)


def _adamw_sum(w, m, v, acc, got, place):
    a, b = w.shape
    ah = a // 2
    tr = _row_tile(ah, b)
    nb = ah // tr

    def body(place_ref, w_ref, m_ref, v_ref, acc_ref, got_ref, g_out, d_ref, nm_ref, nv_ref):
        mine = (pl.program_id(0) // nb) == place_ref[1]
        part = lambda k: jnp.where(mine, acc_ref[k], got_ref[k]).astype(F32)
        g = part(0) + part(1)
        g = g + part(2)
        g = g + part(3)
        g_out[...] = g
        d_ref[...], nm_ref[...], nv_ref[...] = _adam_math(w_ref[...], g, m_ref[...], v_ref[...])

    blk = pl.BlockSpec((tr, b), lambda i, s: (i, 0))
    mine_spec = pl.BlockSpec((N_CHIPS, tr, b), lambda i, s: (0, jnp.where(i // nb == s[1], i % nb, 0), 0))
    got_spec = pl.BlockSpec((N_CHIPS, tr, b), lambda i, s: (0, jnp.where(i // nb == s[1], 0, i % nb), 0))
    return pl.pallas_call(
        body, name="adamw_sum",
        grid_spec=pltpu.PrefetchScalarGridSpec(
            num_scalar_prefetch=1, grid=(a // tr,), in_specs=[blk] * 3 + [mine_spec, got_spec], out_specs=[blk] * 4),
        out_shape=[_sds((a, b))] * 4,
        compiler_params=pltpu.CompilerParams(dimension_semantics=("arbitrary",), vmem_limit_bytes=VMEM_LIMIT),
    )(place, w, m, v, _in_hbm(acc), _in_hbm(got))


def _adamw_pieces(g, pieces, name):
    n = len(pieces)

    def body(g_ref, *refs):
        def grad(rows, cols):
            if len(g_ref.shape) == 2:
                return g_ref[rows, cols]
            total = g_ref[0, rows, cols]
            for k in range(1, g_ref.shape[0]):
                total = total + g_ref[k, rows, cols]
            return total

        ins, outs = refs[:3 * n], refs[3 * n:]
        for i, piece in enumerate(pieces):
            w_ref, m_ref, v_ref = ins[3 * i:3 * i + 3]
            o_g, o_d, o_m, o_v = outs[4 * i:4 * i + 4]
            if len(piece) == 5:
                g_v = grad(piece[3], piece[4])
                o_g[...] = g_v
                o_d[...], o_m[...], o_v[...] = _adam_math(w_ref[...], g_v, m_ref[...], v_ref[...])
            else:
                for r in range(w_ref.shape[1] // SMALL_COLS):
                    lanes = slice(r * SMALL_COLS, (r + 1) * SMALL_COLS)
                    g_v = grad(slice(piece[3] + r, piece[3] + r + 1), slice(None))
                    o_g[:, lanes] = g_v
                    o_d[:, lanes], o_m[:, lanes], o_v[:, lanes] = _adam_math(w_ref[:, lanes], g_v, m_ref[:, lanes],
                                                                            v_ref[:, lanes])

    operands = [t for piece in pieces for t in piece[:3]]
    out = pl.pallas_call(
        body, name=name,
        out_shape=[_sds(piece[0].shape) for piece in pieces for _ in range(4)],
    )(g, *operands)
    return [tuple(out[4 * i:4 * i + 4]) for i in range(n)]


TINY_ROWS, TINY_COLS = 16, 768
SMALL_COLS = 128
SMALL_ROWS = 624


def _pack_tiny(conv_w, b_gates, fcw):
    ns = conv_w.shape[0]
    pad = lambda t: jnp.pad(t, ((0, 0), (0, 0), (0, TINY_COLS - t.shape[2])))
    z = lambda rows: jnp.zeros((ns, rows, TINY_COLS), F32)
    return jnp.concatenate([pad(conv_w), pad(b_gates), z(2), fcw, z(5)], axis=1)


def _unpack_tiny(t):
    return t[:, 0:4, 0:256], t[:, 4:6, 0:256], t[:, 8:11, :]


def _cols_to_shards(t, n):
    return t.reshape(t.shape[0], N_CHIPS, n).transpose(1, 0, 2)


def _shards_to_cols(t):
    return t.transpose(1, 0, 2).reshape(t.shape[1], -1)


_VECTORS = ("g_mix_post", "conv_b", "lru_lambda", "g_ffn_pre", "g_ffn_post", "g_ple_gate", "g_ple_post", "pool_scale",
            "ffn_conv_b")
_VECTOR_LEN = {"pool_scale": POOL_WIDTH, "ffn_conv_b": D_FF}
POOL_W_ROWS = POOL_GROUPS * POOL_GROUP_DIM


def _vector_rows():
    rows, row = {}, POOL_W_ROWS
    for k in _VECTORS:
        rows[k] = row
        row += max(8, _VECTOR_LEN.get(k, D_MODEL) // SMALL_COLS)
    return rows, row


def _pack_small(grads, loss):
    tiles = lambda t: jnp.pad(t, ((0, -t.shape[0] % 8), (0, 0)))
    parts = [grads["pool_w"].reshape(POOL_W_ROWS, SMALL_COLS)] + [tiles(grads[k].reshape(-1, SMALL_COLS)) for k in _VECTORS]
    parts.append(tiles(loss))
    used = sum(t.shape[0] for t in parts)
    return jnp.concatenate(parts + [jnp.zeros((SMALL_ROWS - used, SMALL_COLS), F32)], axis=0)


def _gates_block_diag(w):
    w4 = w.reshape(2, GATE_BLOCKS, 4, RNN_HEAD_DIM, RNN_HEAD_DIM)
    eye = jnp.eye(4, dtype=w.dtype)
    return jnp.einsum("gqhij,hk->gqhikj", w4, eye).reshape(2, GATE_BLOCKS, GATE_BLOCK, GATE_BLOCK)


def _gates_from_block_diag(dw):
    d6 = dw.reshape(2, GATE_BLOCKS, 4, RNN_HEAD_DIM, 4, RNN_HEAD_DIM)
    blocks = [d6[:, :, hh, :, hh, :] for hh in range(4)]
    return jnp.stack(blocks, axis=2).reshape(2, RNN_HEADS, RNN_HEAD_DIM, RNN_HEAD_DIM)


ROW_TILE = 256
DW_TILES = 4

_SHARDED = ("w_in", "w_pool_out", "w_rg_out", "w_o", "w_up", "w_down", "w_ple_gate", "w_ple_proj")
_WEIGHTS = ("g_mix_pre", "g_mix_post", "w_in", "pool_w", "pool_scale", "w_pool_out", "conv_w", "conv_b", "w_rg_gates",
            "b_rg_gates", "lru_lambda", "w_rg_out", "w_o", "g_ffn_pre", "g_ffn_post", "w_up", "ffn_conv_w", "ffn_conv_b",
            "w_down", "g_ple_gate", "w_ple_gate", "w_ple_proj", "g_ple_post")


def _wire_dtype(g):
    return BF16 if g.shape[1] >= 64 and g.shape[2] > SMALL_COLS else F32


def _partials(grads, got, place):
    parts = [_chip_partial(g, r, place, _wire_dtype(g)) for g, r in zip(grads, got)]
    return [send for send, _ in parts], [acc for _, acc in parts]


def _whole(both):
    return [b.reshape(2 * b.shape[1], b.shape[2]) for b in both]


def _step(x, p, tgt, rep, place, ts):
    vec = lambda k: rep[k].reshape(1, -1)
    tall = min(2 * ts, x.shape[0])
    pool_w = rep["pool_w"].astype(BF16)
    wg = _gates_block_diag(rep["w_rg_gates"]).astype(BF16)
    sq = lambda t: t.reshape(D_MODEL, D_MODEL)
    by4 = lambda t: t.reshape(N_CHIPS, -1, D_MODEL)

    first, ride1, ride2, ride3 = (("w_in", "w_pool_out", "tiny"), ("w_rg_out", "w_o", "w_down"), ("w_up",),
                                  ("w_ple_gate", "w_ple_proj"))
    later = ride1 + ride2 + ride3
    tiny = _pack_tiny(rep["conv_w"][None], rep["b_rg_gates"][None], rep["ffn_conv_w"][None])[0]
    own_first, _ = _own_slots([rep["w_in"], rep["w_pool_out"], tiny], [BF16, BF16, F32], "own_slots_first")
    own_later, (got,) = _own_slots([rep[k] for k in later], [BF16] * len(later), "own_slots_gather_first",
                                   [_gather_task(own_first)])
    own = dict(zip(later, own_later))
    full = dict(zip(first, got))
    conv_w, b_gates, fcw = [_shards_to_cols(t) for t in _unpack_tiny(full["tiny"])]

    (urx, urg, gp, gr, d, ypool, h1), (got,) = _fwd_in_pool(
        x, vec("g_mix_pre"), full["w_in"], pool_w, vec("pool_scale"), full["w_pool_out"], ts,
        [_gather_task([own[k] for k in ride1], relay_steps=(6, 2))])
    full.update(zip(ride1, got))
    w_rg_out, w_o, w_down = sq(full["w_rg_out"]), sq(full["w_o"]), full["w_down"].reshape(D_FF, D_MODEL)
    (xc, r, ig, h, yrnn, mo, x1, glr, ggr, sp, sr), (got,) = _fwd_rnn_merge(
        urx, urg, gp, gr, ypool, x, conv_w, vec("conv_b"), wg, b_gates, vec("lru_lambda"), w_rg_out, w_o,
        vec("g_mix_post"), ts, [_gather_task([own[k] for k in ride2], relay_steps=(7, 3))])
    full.update(zip(ride2, got))
    (up, gl, gg, h2, dn, x2), (got,) = _fwd_ffn(x1, vec("g_ffn_pre"), full["w_up"], fcw, vec("ffn_conv_b"), w_down,
                                               vec("g_ffn_post"), ts, [_gather_task([own[k] for k in ride3], relay_steps=(10, 5))])
    full.update(zip(ride3, got))
    dx2, loss, d_w_gate, d_w_proj, d_g_ple_gate, d_g_ple_post = _ple_loss(
        x2, p, tgt, vec("g_ple_gate"), sq(full["w_ple_gate"]), full["w_ple_proj"], vec("g_ple_post"), tall)
    dup, d_w_down, d_fcw, d_fcb, d_g_ffn_post = _bwd_ffn_down(dx2, dn, up, gl, gg, fcw, w_down, vec("g_ffn_post"), ts)

    names1, grads1 = ("w_ple_gate", "w_ple_proj", "w_down"), [by4(d_w_gate), d_w_proj, by4(d_w_down)]
    (dx1, d_g_ffn_pre), (got1,) = _bwd_ffn_up(dup, x1, dx2, vec("g_ffn_pre"), full["w_up"], tall, [_halves_task(grads1)])
    (d_w_up,), (accs1,) = _dw_up(h2, dup, ts, [_exchange_task(*_partials(grads1, got1, place))])
    (dgp, dgr, dyp, dyr, d_w_o, d_g_mix_post), (got2, theirs1) = _bwd_merge(
        dx1, mo, sp, sr, ypool, yrnn, vec("g_mix_post"), w_o, tall, [_halves_task([d_w_up]), _swap_task(accs1)])
    (durx, durg, d_w_rg_out, d_wg, d_conv_w, d_conv_b, d_b_gates, d_lam), (accs2,) = _bwd_rnn(
        dyr, urx, glr, ggr, xc, r, ig, h, conv_w, wg, vec("lru_lambda"), w_rg_out, ts,
        [_exchange_task(*_partials([d_w_up], got2, place))])
    names3 = ("w_o", "w_rg_out", "tiny", "w_rg_gates")
    grads3 = [by4(d_w_o), by4(d_w_rg_out),
              _pack_tiny(_cols_to_shards(d_conv_w, 256), _cols_to_shards(d_b_gates, 256), _cols_to_shards(d_fcw, 768)),
              _gates_from_block_diag(d_wg).reshape(1, 2 * RNN_HEADS * RNN_HEAD_DIM, RNN_HEAD_DIM)]
    (dzp, d_w_pool_out, d_pool_w, d_pool_scale), (got3, theirs2) = _bwd_pool(
        dyp, d, pool_w, vec("pool_scale"), full["w_pool_out"], tall, [_halves_task(grads3), _swap_task(accs2)])
    replicated = {"g_mix_post": d_g_mix_post, "conv_b": d_conv_b, "lru_lambda": d_lam, "g_ffn_pre": d_g_ffn_pre,
                  "g_ffn_post": d_g_ffn_post, "g_ple_gate": d_g_ple_gate, "g_ple_post": d_g_ple_post,
                  "pool_scale": d_pool_scale, "ffn_conv_b": d_fcb, "pool_w": d_pool_w}
    names4 = ("w_in", "w_pool_out", "small")
    small4 = [d_w_pool_out, _pack_small(replicated, loss)[None]]
    (_, d_w_in_half, got_w_in), (accs3, got_small4) = _dw_in(
        h1, dzp, durx, durg, dgp, dgr, ts, [_exchange_task(*_partials(grads3, got3, place)), _halves_task(small4)])
    grads4, got4 = [d_w_in_half] + small4, [got_w_in] + got_small4
    (grad_x, d_g_mix_pre), (accs4, theirs3, both3) = _bwd_in(
        dzp, durx, durg, dgp, dgr, x, dx1, vec("g_mix_pre"), full["w_in"], tall,
        [_exchange_task(*_partials(grads4, got4, place)), _swap_task(accs3[:2]),
         _share_task([_chip_sum(acc, place) for acc in accs3[2:]])])
    theirs4, both4, (g_mix_pre_parts,) = _run(
        [_swap_task(accs4[:2]), _share_task([_chip_sum(acc, place) for acc in accs4[2:]]),
         _all_devices_task([d_g_mix_pre.reshape(SUBLANES, SMALL_COLS)])], "grad_sibling_share")
    mine = accs1 + accs2 + accs3[:2] + accs4[:2]
    partials = dict(zip(names1 + ("w_up",) + names3[:2] + names4[:2], zip(mine, theirs1 + theirs2 + theirs3 + theirs4)))
    return grad_x, partials, dict(zip(names3[2:] + names4[2:], _whole(both3) + _whole(both4))), g_mix_pre_parts


def kernel(x, p, g_mix_pre, g_mix_post, w_in, pool_w, pool_scale, w_pool_out, conv_w, conv_b, w_rg_gates, b_rg_gates, lru_lambda, w_rg_out, w_o, g_ffn_pre, g_ffn_post, w_up, ffn_conv_w, ffn_conv_b, w_down, g_ple_gate, w_ple_gate, w_ple_proj, g_ple_post, loss_target, m_g_mix_pre, m_g_mix_post, m_w_in, m_pool_w, m_pool_scale, m_w_pool_out, m_conv_w, m_conv_b, m_w_rg_gates, m_b_rg_gates, m_lru_lambda, m_w_rg_out, m_w_o, m_g_ffn_pre, m_g_ffn_post, m_w_up, m_ffn_conv_w, m_ffn_conv_b, m_w_down, m_g_ple_gate, m_w_ple_gate, m_w_ple_proj, m_g_ple_post, v_g_mix_pre, v_g_mix_post, v_w_in, v_pool_w, v_pool_scale, v_w_pool_out, v_conv_w, v_conv_b, v_w_rg_gates, v_b_rg_gates, v_lru_lambda, v_w_rg_out, v_w_o, v_g_ffn_pre, v_g_ffn_post, v_w_up, v_ffn_conv_w, v_ffn_conv_b, v_w_down, v_g_ple_gate, v_w_ple_gate, v_w_ple_proj, v_g_ple_post):
    args = dict(locals())
    w = {k: args[k][0] for k in _WEIGHTS}
    m = {k: args["m_" + k][0] for k in _WEIGHTS}
    v = {k: args["v_" + k][0] for k in _WEIGHTS}
    place = jnp.stack([2 * lax.axis_index("x") + lax.axis_index("y"), lax.axis_index("c")]).astype(jnp.int32)
    grad_x, partials, reduced, g_mix_pre_parts = _step(x[0], p[0, 0], loss_target[0], w, place, ROW_TILE)

    gates_2d = (2 * RNN_HEADS * RNN_HEAD_DIM, RNN_HEAD_DIM)
    as2d = lambda k, shape: tuple(t[k].reshape(shape) for t in (w, m, v))
    done = {k: tuple(_adamw_sum(w[k], m[k], v[k], *partials[k], place)) for k in _SHARDED}
    gates_w, gates_m, gates_v = as2d("w_rg_gates", gates_2d)
    done["w_rg_gates"] = tuple(_adamw(gates_w, reduced["w_rg_gates"], gates_m, gates_v))
    tiny_names = ("conv_w", "b_rg_gates", "ffn_conv_w")
    tiny_at = ((slice(0, 4), slice(0, 256)), (slice(4, 6), slice(0, 256)), (slice(8, 11), slice(None)))
    done.update(zip(tiny_names, _adamw_pieces(
        reduced["tiny"], [(w[k], m[k], v[k]) + at for k, at in zip(tiny_names, tiny_at)], "adamw_tiny")))
    vector_rows, loss_row = _vector_rows()
    pieces = [as2d("pool_w", (POOL_W_ROWS, SMALL_COLS)) + (slice(0, POOL_W_ROWS), slice(None))]
    pieces += [as2d(k, (1, -1)) + (vector_rows[k],) for k in _VECTORS]
    done.update(zip(("pool_w",) + _VECTORS, _adamw_pieces(reduced["small"], pieces, "adamw_small")))
    done["g_mix_pre"] = _adamw_pieces(g_mix_pre_parts, [as2d("g_mix_pre", (1, -1)) + (0,)], "adamw_g_mix_pre")[0]

    result = [reduced["small"][loss_row, 0], grad_x[None]]
    for kind in range(4):
        result += [done[k][kind].reshape(args[k].shape) for k in _WEIGHTS]
    return tuple(result)
```

```python
import functools

import jax
import jax.numpy as jnp
from jax import lax
from jax.experimental import pallas as pl
from jax.experimental.pallas import tpu as pltpu

F32 = jnp.float32
BF16 = jnp.bfloat16

D_MODEL = 1024
POOL_WINDOWS = (2, 4, 8, 16)
POOL_GROUPS = 4
POOL_WIDTH = 512
POOL_GROUP_DIM = 128
RNN_HEADS = 16
RNN_HEAD_DIM = 64
GATE_BLOCK = 256
GATE_BLOCKS = D_MODEL // GATE_BLOCK
LRU_C = 8.0
D_FF = 3072
PLE_DIM = 256
RMS_EPS = 1e-6
IN_TOTAL = 4608
N_CHIPS = 4
IN_SHARD = IN_TOTAL // N_CHIPS
UP_SHARD = 2 * D_FF // N_CHIPS
POOL_HALO = 16
CONV_HALO = 8

ADAM_LR = 0.001
ADAM_B1 = 0.9
ADAM_B2 = 0.999
ADAM_EPS = 1e-08
ADAM_WD = 0.01
ADAM_STEP = 10

VMEM_LIMIT = 56 * 1024 * 1024
MESH = pl.DeviceIdType.MESH

_GELU_C = 0.7978845608028654
_GELU_A = 0.044715


def _dot(a, b):
    return jnp.dot(a.astype(BF16), b.astype(BF16), preferred_element_type=F32)


def _dot_nt(a, b):
    return lax.dot_general(a.astype(BF16), b.astype(BF16), (((1,), (1,)), ((), ())), preferred_element_type=F32)


def _dot_tn(a, b):
    return lax.dot_general(a.astype(BF16), b.astype(BF16), (((0,), (0,)), ((), ())), preferred_element_type=F32)


def _rms_fwd(x, g):
    r = lax.rsqrt(jnp.mean(x * x, axis=-1, keepdims=True) + RMS_EPS)
    xh = x * r
    return xh * g, xh, r


def _rms_bwd(xh, r, g, dy):
    dxh = dy * g
    dg = jnp.sum(dy * xh, axis=0, keepdims=True)
    dx = r * (dxh - xh * jnp.mean(dxh * xh, axis=-1, keepdims=True))
    return dx, dg


def _sigmoid(x):
    return 0.5 * jnp.tanh(0.5 * x) + 0.5


def _gelu(x):
    t = jnp.tanh(_GELU_C * (x + _GELU_A * x * x * x))
    return 0.5 * x * (1.0 + t), t


def _gelu_grad(x, t):
    return 0.5 * (1.0 + t) + 0.5 * x * (1.0 - t * t) * _GELU_C * (1.0 + 3.0 * _GELU_A * x * x)


def _softplus_neg(lam):
    nl = -lam
    return jnp.maximum(nl, 0.0) + jnp.log(1.0 + jnp.exp(-jnp.abs(nl)))


def _lru_coeffs(r, lam, first_row):
    c8 = LRU_C * _softplus_neg(lam)
    la = -(c8 * r)
    a = jnp.exp(la)
    m2 = jnp.tanh(-la) * (1.0 + a * a)
    mult = jnp.where(first_row, 1.0, jnp.sqrt(m2))
    return c8, a, m2, mult


SUBLANES = 8


def _scan_fwd(a, u, carry):
    n = a.shape[0]
    sub = lax.broadcasted_iota(jnp.int32, (n, 1), 0) % SUBLANES
    acc_a, acc_h = a, u
    for s in (1, 2, 4):
        m = sub >= s
        h_s = jnp.where(m, pltpu.roll(acc_h, s, 0), 0.0)
        a_s = jnp.where(m, pltpu.roll(acc_a, s, 0), 1.0)
        acc_h = acc_a * h_s + acc_h
        acc_a = acc_a * a_s
    out = []
    for g in range(n // SUBLANES):
        rows = slice(g * SUBLANES, (g + 1) * SUBLANES)
        out.append(acc_h[rows] + acc_a[rows] * carry)
        carry = out[-1][SUBLANES - 1:SUBLANES]
    return jnp.concatenate(out, axis=0)


def _scan_bwd(b, g, carry):
    n = b.shape[0]
    sub = lax.broadcasted_iota(jnp.int32, (n, 1), 0) % SUBLANES
    acc_b, acc_l = b, g
    for s in (1, 2, 4):
        m = sub < SUBLANES - s
        l_s = jnp.where(m, pltpu.roll(acc_l, n - s, 0), 0.0)
        b_s = jnp.where(m, pltpu.roll(acc_b, n - s, 0), 1.0)
        acc_l = acc_b * l_s + acc_l
        acc_b = acc_b * b_s
    out = [None] * (n // SUBLANES)
    for g in reversed(range(n // SUBLANES)):
        rows = slice(g * SUBLANES, (g + 1) * SUBLANES)
        out[g] = acc_l[rows] + acc_b[rows] * carry
        carry = out[g][0:1]
    return jnp.concatenate(out, axis=0)


def _shift_down(ext, k, halo):
    return pltpu.roll(ext, k, 0)[halo:] if k else ext[halo:]


def _shift_up(ext, k, ts):
    return pltpu.roll(ext, ext.shape[0] - k, 0)[:ts] if k else ext[:ts]


def _rows(ts, width, nt=None, col=0):
    if nt is None:
        return pl.BlockSpec((ts, width), lambda i: (i, col))
    return pl.BlockSpec((ts, width), lambda i: (nt - 1 - i, col))


def _resident(shape):
    zeros = (0,) * len(shape)
    return pl.BlockSpec(shape, lambda i: zeros, pipeline_mode=pl.Buffered(1))


def _acc(shape):
    zeros = (0,) * len(shape)
    return pl.BlockSpec(shape, lambda i: zeros)


def _params():
    return pltpu.CompilerParams(dimension_semantics=("arbitrary",), vmem_limit_bytes=VMEM_LIMIT)


def _sds(shape, dtype=F32):
    return jax.ShapeDtypeStruct(shape, dtype)


class _Task:
    def __init__(self, ins, out_shapes, aliases, sems, start, finish, relays=()):
        self.ins, self.out_shapes, self.aliases, self.sems = list(ins), list(out_shapes), dict(aliases), list(sems)
        self.start, self.relays, self.finish = start, list(relays), finish


def _call(body, name, grid, in_specs, out_specs, out_shape, scratch_shapes, args, tasks=()):
    n_in, n_out, n_scr = len(in_specs), len(out_specs), len(scratch_shapes)
    t_in = [len(t.ins) for t in tasks]
    t_out = [len(t.out_shapes) for t in tasks]
    t_sem = [len(t.sems) for t in tasks]
    steps = 1
    for g in grid:
        steps *= g

    def take(refs, pos, counts):
        groups = []
        for c in counts:
            groups.append(refs[pos:pos + c])
            pos += c
        return groups, pos

    def wrapped(*refs):
        (cin,), pos = take(refs, 0, [n_in])
        tin, pos = take(refs, pos, t_in)
        (cout,), pos = take(refs, pos, [n_out])
        tout, pos = take(refs, pos, t_out)
        (cscr,), pos = take(refs, pos, [n_scr])
        tsem, pos = take(refs, pos, t_sem)
        if not grid:
            for t, a, b, c in zip(tasks, tin, tout, tsem):
                t.start(a, b, c)
            if body is not None:
                body(*cin, *cout, *cscr)
            for t, a, b, c in zip(tasks, tin, tout, tsem):
                for relay, _ in t.relays:
                    relay(a, b, c)
            for t, a, b, c in zip(tasks, tin, tout, tsem):
                t.finish(a, b, c)
            return
        step = pl.program_id(0)
        for axis in range(1, len(grid)):
            step = step * grid[axis] + pl.program_id(axis)
        if tasks:
            @pl.when(step == 0)
            def _():
                for t, a, b, c in zip(tasks, tin, tout, tsem):
                    t.start(a, b, c)

        body(*cin, *cout, *cscr)
        for t, a, b, c in zip(tasks, tin, tout, tsem):
            for relay, before in t.relays:
                pl.when(step == max(steps - 1 - before, 0))(functools.partial(relay, a, b, c))

        if tasks:
            @pl.when(step == steps - 1)
            def _():
                for t, a, b, c in zip(tasks, tin, tout, tsem):
                    t.finish(a, b, c)

    aliases, in_pos, out_pos = {}, n_in, n_out
    for t, ni, no in zip(tasks, t_in, t_out):
        aliases.update({in_pos + a: out_pos + b for a, b in t.aliases.items()})
        in_pos, out_pos = in_pos + ni, out_pos + no
    any_spec = pl.BlockSpec(memory_space=pltpu.HBM)
    kwargs = dict(grid=grid, compiler_params=pltpu.CompilerParams(
        dimension_semantics=("arbitrary",) * len(grid), vmem_limit_bytes=VMEM_LIMIT)) if grid else dict(
        compiler_params=pltpu.CompilerParams(vmem_limit_bytes=VMEM_LIMIT))
    out = pl.pallas_call(
        wrapped, name=name,
        in_specs=list(in_specs) + [any_spec] * sum(t_in),
        out_specs=list(out_specs) + [any_spec] * sum(t_out),
        out_shape=list(out_shape) + [s for t in tasks for s in t.out_shapes],
        scratch_shapes=list(scratch_shapes) + [s for t in tasks for s in t.sems],
        input_output_aliases=aliases, **kwargs,
    )(*args, *[pltpu.with_memory_space_constraint(a, pltpu.HBM) for t in tasks for a in t.ins])
    task_outs, pos = take(list(out), n_out, t_out)
    return list(out[:n_out]), task_outs


def _fwd_in_pool(x, g_pre, w_in, pool_w, pool_scale, w_pool_out, ts, tasks=()):
    s = x.shape[0]

    def body(x_ref, g_ref, win_ref, pw_ref, ps_ref, wpo_ref,
             urx_ref, urg_ref, gp_ref, gr_ref, d_ref, yp_ref, h1_ref, z_scr, halo_scr):
        i = pl.program_id(0)

        @pl.when(i == 0)
        def _():
            halo_scr[...] = jnp.zeros_like(halo_scr)

        h1, _, _ = _rms_fwd(x_ref[...], g_ref[...])
        h1 = h1.astype(BF16)
        h1_ref[...] = h1
        for j in range(N_CHIPS):
            z_scr[:, j * IN_SHARD:(j + 1) * IN_SHARD] = jnp.dot(h1, win_ref[j], preferred_element_type=F32)
        urx_ref[...] = z_scr[:, 512:1536]
        urg_ref[...] = z_scr[:, 1536:2560]
        gp_ref[...] = z_scr[:, 2560:3584]
        gr_ref[...] = z_scr[:, 3584:4608]
        u = z_scr[:, 0:POOL_WIDTH]
        ext = jnp.concatenate([halo_scr[...], u], axis=0)
        halo_scr[...] = u[ts - POOL_HALO:, :]
        t = i * ts + lax.broadcasted_iota(jnp.int32, (ts, 1), 0)
        y4 = []
        for g, w in enumerate(POOL_WINDOWS):
            lanes = slice(g * POOL_GROUP_DIM, (g + 1) * POOL_GROUP_DIM)
            acc = ext[:, lanes]
            sh = 1
            while sh < w:
                acc = acc + pltpu.roll(acc, sh, 0)
                sh *= 2
            inv = 1.0 / jnp.minimum(t + 1, w).astype(F32)
            dg = acc[POOL_HALO:, :] * inv - u[:, lanes]
            d_ref[:, lanes] = dg.astype(BF16)
            y4.append(_dot(dg, pw_ref[g]))
        ypre = jnp.concatenate(y4, axis=1) * ps_ref[...]
        ypre = ypre.astype(BF16)
        for j in range(N_CHIPS):
            yp_ref[:, j * 256:(j + 1) * 256] = jnp.dot(ypre, wpo_ref[j], preferred_element_type=F32)

    return _call(
        body, "fwd_in_pool", (s // ts,),
        [_rows(ts, D_MODEL), _resident((1, D_MODEL)), _resident(w_in.shape), _resident(pool_w.shape),
         _resident((1, POOL_WIDTH)), _resident(w_pool_out.shape)],
        [_rows(ts, D_MODEL)] * 4 + [_rows(ts, POOL_WIDTH), _rows(ts, D_MODEL), _rows(ts, D_MODEL)],
        [_sds((s, D_MODEL))] * 4 + [_sds((s, POOL_WIDTH), BF16), _sds((s, D_MODEL)), _sds((s, D_MODEL), BF16)],
        [pltpu.VMEM((ts, IN_TOTAL), F32), pltpu.VMEM((POOL_HALO, POOL_WIDTH), F32)],
        (x, g_pre, w_in, pool_w, pool_scale, w_pool_out), tasks)


def _fwd_rnn_merge(urx, urg, gp, gr, ypool, x, conv_w, conv_b, wg, bg, lam, w_rg_out, w_o, g_post, ts, tasks=()):
    s = x.shape[0]

    def body(urx_ref, urg_ref, gp_ref, gr_ref, yp_ref, x_ref, cw_ref, cb_ref, wg_ref, bg_ref, lam_ref, wrg_ref, wo_ref,
             gpost_ref, xc_ref, r_ref, ig_ref, h_ref, yr_ref, mo_ref, x1_ref, gl_ref, gg_ref, sp_ref, sr_ref,
             halo_scr, carry_scr):
        i = pl.program_id(0)

        @pl.when(i == 0)
        def _():
            halo_scr[...] = jnp.zeros_like(halo_scr)
            carry_scr[...] = jnp.zeros_like(carry_scr)

        urx_v = urx_ref[...]
        ext = jnp.concatenate([halo_scr[...], urx_v], axis=0)
        halo_scr[...] = urx_v[ts - CONV_HALO:, :]
        cw = cw_ref[...]
        xc = (cb_ref[...] + cw[3:4] * urx_v + cw[2:3] * _shift_down(ext, 1, CONV_HALO)
              + cw[1:2] * _shift_down(ext, 2, CONV_HALO) + cw[0:1] * _shift_down(ext, 3, CONV_HALO))
        xc_ref[...] = xc
        xcb = xc.astype(BF16)
        lin = []
        for gate in range(2):
            parts = [jnp.dot(xcb[:, q * GATE_BLOCK:(q + 1) * GATE_BLOCK], wg_ref[gate, q], preferred_element_type=F32)
                     for q in range(GATE_BLOCKS)]
            lin.append(jnp.concatenate(parts, axis=1) + bg_ref[gate:gate + 1, :])
        r = _sigmoid(lin[0])
        ig = _sigmoid(lin[1])
        r_ref[...] = r
        ig_ref[...] = ig
        first_row = (i * ts + lax.broadcasted_iota(jnp.int32, (ts, 1), 0)) == 0
        _, a, _, mult = _lru_coeffs(r, lam_ref[...], first_row)
        h = _scan_fwd(a, mult * ig * xc, carry_scr[0:1, :])
        carry_scr[0:1, :] = h[ts - 1:ts, :]
        h_ref[...] = h
        urg_v = urg_ref[...]
        gl, t = _gelu(urg_v)
        gl_ref[...] = gl.astype(BF16)
        gg_ref[...] = _gelu_grad(urg_v, t).astype(BF16)
        yr = _dot(h * gl, wrg_ref[...])
        yr_ref[...] = yr
        sp = _sigmoid(gp_ref[...])
        sr = _sigmoid(gr_ref[...])
        sp_ref[...] = sp.astype(BF16)
        sr_ref[...] = sr.astype(BF16)
        merged = sp * yp_ref[...] + sr * yr
        mo = _dot(merged, wo_ref[...])
        mo_ref[...] = mo
        y, _, _ = _rms_fwd(mo, gpost_ref[...])
        x1_ref[...] = x_ref[...] + y

    row = _rows(ts, D_MODEL)
    return _call(
        body, "fwd_rnn_merge", (s // ts,),
        [row] * 6 + [_resident(conv_w.shape), _resident((1, D_MODEL)), _resident(wg.shape), _resident(bg.shape),
                     _resident((1, D_MODEL)), _resident(w_rg_out.shape), _resident(w_o.shape), _resident((1, D_MODEL))],
        [row] * 11, [_sds((s, D_MODEL))] * 7 + [_sds((s, D_MODEL), BF16)] * 4,
        [pltpu.VMEM((CONV_HALO, D_MODEL), F32), pltpu.VMEM((8, D_MODEL), F32)],
        (urx, urg, gp, gr, ypool, x, conv_w, conv_b, wg, bg, lam, w_rg_out, w_o, g_post), tasks)


def _fwd_ffn(x1, g_pre, w_up, fcw, fcb, w_down, g_post, ts, tasks=()):
    s = x1.shape[0]

    def body(x1_ref, g_ref, wup_ref, fcw_ref, fcb_ref, wd_ref, gpost_ref,
             up_ref, gl_ref, gg_ref, h2_ref, dn_ref, x2_ref, up_scr, halo_scr):
        i = pl.program_id(0)

        @pl.when(i == 0)
        def _():
            halo_scr[...] = jnp.zeros_like(halo_scr)

        x1_v = x1_ref[...]
        h2, _, _ = _rms_fwd(x1_v, g_ref[...])
        h2 = h2.astype(BF16)
        h2_ref[...] = h2
        for j in range(N_CHIPS):
            up_scr[:, j * UP_SHARD:(j + 1) * UP_SHARD] = jnp.dot(h2, wup_ref[j], preferred_element_type=F32)
        up_ref[...] = up_scr[...].astype(BF16)
        ug = up_scr[:, 0:D_FF]
        ext = jnp.concatenate([halo_scr[...], ug], axis=0)
        halo_scr[...] = ug[ts - CONV_HALO:, :]
        w = fcw_ref[...]
        gh = (fcb_ref[...] + w[2:3] * ug + w[1:2] * _shift_down(ext, 1, CONV_HALO)
              + w[0:1] * _shift_down(ext, 2, CONV_HALO))
        gl, t = _gelu(gh)
        gl_ref[...] = gl.astype(BF16)
        gg_ref[...] = _gelu_grad(gh, t).astype(BF16)
        dn = _dot(gl * up_scr[:, D_FF:], wd_ref[...])
        dn_ref[...] = dn
        y, _, _ = _rms_fwd(dn, gpost_ref[...])
        x2_ref[...] = x1_v + y

    row = _rows(ts, D_MODEL)
    return _call(
        body, "fwd_ffn", (s // ts,),
        [row, _resident((1, D_MODEL)), _resident(w_up.shape), _resident(fcw.shape), _resident((1, D_FF)),
         _resident(w_down.shape), _resident((1, D_MODEL))],
        [_rows(ts, 2 * D_FF), _rows(ts, D_FF), _rows(ts, D_FF), row, row, row],
        [_sds((s, 2 * D_FF), BF16), _sds((s, D_FF), BF16), _sds((s, D_FF), BF16), _sds((s, D_MODEL), BF16),
         _sds((s, D_MODEL)), _sds((s, D_MODEL))],
        [pltpu.VMEM((ts, 2 * D_FF), F32), pltpu.VMEM((CONV_HALO, D_FF), F32)],
        (x1, g_pre, w_up, fcw, fcb, w_down, g_post), tasks)


def _ple_loss(x2, p, tgt, g_gate, w_gate, w_proj, g_post, ts):
    s = x2.shape[0]

    def body(x2_ref, p_ref, t_ref, gg_ref, wg_ref, wp_ref, gp_ref, dx2_ref, loss_ref, dwg_ref, dwp_ref, dgg_ref, dgp_ref):
        @pl.when(pl.program_id(0) == 0)
        def _():
            loss_ref[...] = jnp.zeros_like(loss_ref)
            dwg_ref[...] = jnp.zeros_like(dwg_ref)
            dwp_ref[...] = jnp.zeros_like(dwp_ref)
            dgg_ref[...] = jnp.zeros_like(dgg_ref)
            dgp_ref[...] = jnp.zeros_like(dgp_ref)

        x2_v = x2_ref[...]
        n3, xh3, r3 = _rms_fwd(x2_v, gg_ref[...])
        pg = _sigmoid(_dot(n3, wg_ref[...]))
        pb = p_ref[...].astype(BF16)
        q = jnp.concatenate([jnp.dot(pb, wp_ref[j], preferred_element_type=F32) for j in range(N_CHIPS)], axis=1)
        ple, qh, rq = _rms_fwd(q, gp_ref[...])
        e = x2_v + pg * ple - t_ref[...]
        loss_ref[...] += 0.5 * jnp.sum(jnp.mean(e * e, axis=-1, keepdims=True), axis=0, keepdims=True)
        dy = e * (1.0 / D_MODEL)
        dpgl = dy * ple * pg * (1.0 - pg)
        dwg_ref[...] += _dot_tn(n3, dpgl)
        dx3, dgg = _rms_bwd(xh3, r3, gg_ref[...], _dot_nt(dpgl, wg_ref[...]))
        dgg_ref[...] += dgg
        dq, dgp = _rms_bwd(qh, rq, gp_ref[...], dy * pg)
        dgp_ref[...] += dgp
        for j in range(N_CHIPS):
            dwp_ref[j] += _dot_tn(pb, dq[:, j * 256:(j + 1) * 256])
        dx2_ref[...] = dy + dx3

    row = _rows(ts, D_MODEL)
    vec = _acc((1, D_MODEL))
    return pl.pallas_call(
        body, name="ple_loss", grid=(s // ts,),
        in_specs=[row, _rows(ts, PLE_DIM), row, _resident((1, D_MODEL)), _resident(w_gate.shape), _resident(w_proj.shape),
                  _resident((1, D_MODEL))],
        out_specs=[row, _acc((1, 128)), _acc(w_gate.shape), _acc(w_proj.shape), vec, vec],
        out_shape=[_sds((s, D_MODEL)), _sds((1, 128)), _sds(w_gate.shape), _sds(w_proj.shape), _sds((1, D_MODEL)),
                   _sds((1, D_MODEL))],
        compiler_params=_params(),
    )(x2, p, tgt, g_gate, w_gate, w_proj, g_post)


def _bwd_ffn_down(dx2, dn, up, gl, gg, fcw, w_down, g_post, ts):
    s = dx2.shape[0]
    nt = s // ts

    def body(dx2_ref, dn_ref, up_ref, gl_ref, gg_ref, fcw_ref, wd_ref, gpost_ref,
             dup_ref, dwd_ref, dfcw_ref, dfcb_ref, dgp_ref, carry_scr):
        i = pl.program_id(0)

        @pl.when(i == 0)
        def _():
            carry_scr[...] = jnp.zeros_like(carry_scr)
            dwd_ref[...] = jnp.zeros_like(dwd_ref)
            dfcw_ref[...] = jnp.zeros_like(dfcw_ref)
            dfcb_ref[...] = jnp.zeros_like(dfcb_ref)
            dgp_ref[...] = jnp.zeros_like(dgp_ref)

        _, xh, r = _rms_fwd(dn_ref[...], gpost_ref[...])
        ddn, dgp = _rms_bwd(xh, r, gpost_ref[...], dx2_ref[...])
        dgp_ref[...] += dgp
        dhid = _dot_nt(ddn, wd_ref[...])
        ug = up_ref[:, 0:D_FF].astype(F32)
        uv = up_ref[:, D_FF:].astype(F32)
        gl = gl_ref[...].astype(F32)
        w = fcw_ref[...]
        dwd_ref[...] += _dot_tn(gl * uv, ddn)
        dgh = dhid * uv * gg_ref[...].astype(F32)
        dup_ref[:, D_FF:] = (dhid * gl).astype(BF16)
        extd = jnp.concatenate([dgh, carry_scr[...]], axis=0)
        carry_scr[...] = dgh[0:CONV_HALO, :]
        d1 = _shift_up(extd, 1, ts)
        d2 = _shift_up(extd, 2, ts)
        dup_ref[:, 0:D_FF] = (w[2:3] * dgh + w[1:2] * d1 + w[0:1] * d2).astype(BF16)
        dfcw_ref[2:3, :] += jnp.sum(ug * dgh, axis=0, keepdims=True)
        dfcw_ref[1:2, :] += jnp.sum(ug * d1, axis=0, keepdims=True)
        dfcw_ref[0:1, :] += jnp.sum(ug * d2, axis=0, keepdims=True)
        dfcb_ref[...] += jnp.sum(dgh, axis=0, keepdims=True)

    row = _rows(ts, D_MODEL, nt)
    wide = _rows(ts, D_FF, nt)
    return pl.pallas_call(
        body, name="bwd_ffn_down", grid=(nt,),
        in_specs=[row, row, _rows(ts, 2 * D_FF, nt), wide, wide, _resident(fcw.shape), _resident(w_down.shape),
                  _resident((1, D_MODEL))],
        out_specs=[_rows(ts, 2 * D_FF, nt), _acc(w_down.shape), _acc(fcw.shape), _acc((1, D_FF)), _acc((1, D_MODEL))],
        out_shape=[_sds((s, 2 * D_FF), BF16), _sds(w_down.shape), _sds(fcw.shape), _sds((1, D_FF)), _sds((1, D_MODEL))],
        scratch_shapes=[pltpu.VMEM((CONV_HALO, D_FF), F32)],
        compiler_params=_params(),
    )(dx2, dn, up, gl, gg, fcw, w_down, g_post)


def _bwd_ffn_up(dup, x1, dx2, g_pre, w_up, ts, tasks=()):
    s = x1.shape[0]

    def body(dup_ref, x1_ref, dx2_ref, g_ref, wup_ref, dx1_ref, dg_ref):
        @pl.when(pl.program_id(0) == 0)
        def _():
            dg_ref[...] = jnp.zeros_like(dg_ref)

        _, xh, r = _rms_fwd(x1_ref[...], g_ref[...])
        dh2 = _dot_nt(dup_ref[:, 0:UP_SHARD], wup_ref[0])
        for j in range(1, N_CHIPS):
            dh2 = dh2 + _dot_nt(dup_ref[:, j * UP_SHARD:(j + 1) * UP_SHARD], wup_ref[j])
        dx, dg = _rms_bwd(xh, r, g_ref[...], dh2)
        dg_ref[...] += dg
        dx1_ref[...] = dx2_ref[...] + dx

    row = _rows(ts, D_MODEL)
    return _call(
        body, "bwd_ffn_up", (s // ts,),
        [_rows(ts, 2 * D_FF), row, row, _resident((1, D_MODEL)), _resident(w_up.shape)],
        [row, _acc((1, D_MODEL))], [_sds((s, D_MODEL)), _sds((1, D_MODEL))], [],
        (dup, x1, dx2, g_pre, w_up), tasks)


def _dw_up(h2, dup, ts, tasks=()):
    s = h2.shape[0]
    ts = min(DW_TILES * ts, s)

    def body(h2_ref, dup_ref, out_ref):
        @pl.when(pl.program_id(1) == 0)
        def _():
            out_ref[...] = jnp.zeros_like(out_ref)

        out_ref[0] += _dot_tn(h2_ref[...], dup_ref[...])

    return _call(
        body, "dw_up", (N_CHIPS, s // ts),
        [pl.BlockSpec((ts, D_MODEL), lambda j, i: (i, 0)), pl.BlockSpec((ts, UP_SHARD), lambda j, i: (i, j))],
        [pl.BlockSpec((1, D_MODEL, UP_SHARD), lambda j, i: (j, 0, 0))], [_sds((N_CHIPS, D_MODEL, UP_SHARD))], [],
        (h2, dup), tasks)


def _bwd_merge(dx1, mo, sp, sr, ypool, yrnn, g_post, w_o, ts, tasks=()):
    s = dx1.shape[0]

    def body(dx1_ref, mo_ref, sp_ref, sr_ref, yp_ref, yr_ref, g_ref, wo_ref,
             dgp_ref, dgr_ref, dyp_ref, dyr_ref, dwo_ref, dg_ref):
        @pl.when(pl.program_id(0) == 0)
        def _():
            dwo_ref[...] = jnp.zeros_like(dwo_ref)
            dg_ref[...] = jnp.zeros_like(dg_ref)

        _, xh, r = _rms_fwd(mo_ref[...], g_ref[...])
        dmo, dg = _rms_bwd(xh, r, g_ref[...], dx1_ref[...])
        dg_ref[...] += dg
        dmerged = _dot_nt(dmo, wo_ref[...])
        sp = sp_ref[...].astype(F32)
        sr = sr_ref[...].astype(F32)
        yp = yp_ref[...]
        yr = yr_ref[...]
        dwo_ref[...] += _dot_tn(sp * yp + sr * yr, dmo)
        dgp_ref[...] = (dmerged * yp * sp * (1.0 - sp)).astype(BF16)
        dgr_ref[...] = (dmerged * yr * sr * (1.0 - sr)).astype(BF16)
        dyp_ref[...] = (dmerged * sp).astype(BF16)
        dyr_ref[...] = (dmerged * sr).astype(BF16)

    row = _rows(ts, D_MODEL)
    return _call(
        body, "bwd_merge", (s // ts,),
        [row] * 6 + [_resident((1, D_MODEL)), _resident(w_o.shape)],
        [row] * 4 + [_acc(w_o.shape), _acc((1, D_MODEL))],
        [_sds((s, D_MODEL), BF16)] * 4 + [_sds(w_o.shape), _sds((1, D_MODEL))], [],
        (dx1, mo, sp, sr, ypool, yrnn, g_post, w_o), tasks)


def _bwd_rnn(dyr, urx, gl, gg, xc, r, ig, h, conv_w, wg, lam, w_rg_out, ts, tasks=()):
    s = urx.shape[0]
    nt = s // ts
    halo_blocks = ts // CONV_HALO

    def body(dyr_ref, urx_ref, gl_ref, gg_ref, xc_ref, r_ref, ig_ref, h_ref, hh_ref, cw_ref, wg_ref, lam_ref, wrg_ref,
             durx_ref, durg_ref, dwrg_ref, dwg_ref, dcw_ref, dcb_ref, dbg_ref, dlam_ref, mu_scr, carry_scr):
        i = pl.program_id(0)
        k = nt - 1 - i

        @pl.when(i == 0)
        def _():
            mu_scr[...] = jnp.zeros_like(mu_scr)
            carry_scr[...] = jnp.zeros_like(carry_scr)
            dwrg_ref[...] = jnp.zeros_like(dwrg_ref)
            dwg_ref[...] = jnp.zeros_like(dwg_ref)
            dcw_ref[...] = jnp.zeros_like(dcw_ref)
            dcb_ref[...] = jnp.zeros_like(dcb_ref)
            dbg_ref[...] = jnp.zeros_like(dbg_ref)
            dlam_ref[...] = jnp.zeros_like(dlam_ref)

        row = lax.broadcasted_iota(jnp.int32, (ts, 1), 0)
        first_row = (k * ts + row) == 0
        h = h_ref[...]
        dyr_v = dyr_ref[...]
        dhr = _dot_nt(dyr_v, wrg_ref[...])
        gl = gl_ref[...].astype(F32)
        dwrg_ref[...] += _dot_tn(h * gl, dyr_v)
        durg_ref[...] = (dhr * h * gg_ref[...].astype(F32)).astype(BF16)
        r_v = r_ref[...]
        ig_v = ig_ref[...]
        xc_v = xc_ref[...]
        lam_v = lam_ref[...]
        c8, a, m2, mult = _lru_coeffs(r_v, lam_v, first_row)
        b = jnp.where(row == ts - 1, 1.0, pltpu.roll(a, ts - 1, 0))
        lt = _scan_bwd(b, dhr * gl, mu_scr[0:1, :])
        mu_scr[0:1, :] = a[0:1, :] * lt[0:1, :]
        h_before = jnp.where(k > 0, hh_ref[CONV_HALO - 1:CONV_HALO, :], 0.0)
        hprev = jnp.where(row == 0, h_before, pltpu.roll(h, 1, 0))
        dmult = lt * ig_v * xc_v
        da = lt * hprev - jnp.where(first_row, 0.0, dmult * a * lax.rsqrt(m2))
        dla = da * a
        dlam_ref[...] += jnp.sum(dla * r_v, axis=0, keepdims=True)
        dlr = (dla * (-c8)) * r_v * (1.0 - r_v)
        dli = (lt * mult * xc_v) * ig_v * (1.0 - ig_v)
        dbg_ref[0:1, :] += jnp.sum(dlr, axis=0, keepdims=True)
        dbg_ref[1:2, :] += jnp.sum(dli, axis=0, keepdims=True)
        xcb = xc_v.astype(BF16)
        parts = []
        for q in range(GATE_BLOCKS):
            blk = slice(q * GATE_BLOCK, (q + 1) * GATE_BLOCK)
            dlr_q = dlr[:, blk].astype(BF16)
            dli_q = dli[:, blk].astype(BF16)
            parts.append(_dot_nt(dlr_q, wg_ref[0, q]) + _dot_nt(dli_q, wg_ref[1, q]))
            dwg_ref[0, q] += _dot_tn(xcb[:, blk], dlr_q)
            dwg_ref[1, q] += _dot_tn(xcb[:, blk], dli_q)
        dxc = lt * mult * ig_v + jnp.concatenate(parts, axis=1)
        extd = jnp.concatenate([dxc, carry_scr[...]], axis=0)
        carry_scr[...] = dxc[0:CONV_HALO, :]
        cw = cw_ref[...]
        urx_v = urx_ref[...]
        durx = cw[3:4] * dxc
        dcw_ref[3:4, :] += jnp.sum(urx_v * dxc, axis=0, keepdims=True)
        for j in (1, 2, 3):
            dj = _shift_up(extd, j, ts)
            durx = durx + cw[3 - j:4 - j] * dj
            dcw_ref[3 - j:4 - j, :] += jnp.sum(urx_v * dj, axis=0, keepdims=True)
        durx_ref[...] = durx.astype(BF16)
        dcb_ref[...] += jnp.sum(dxc, axis=0, keepdims=True)

        @pl.when(i == nt - 1)
        def _():
            dlam_ref[...] = dlam_ref[...] * (LRU_C * jax.nn.sigmoid(-lam_v))

    row_spec = _rows(ts, D_MODEL, nt)
    halo_spec = pl.BlockSpec((CONV_HALO, D_MODEL), lambda i: (jnp.maximum((nt - 1 - i) * halo_blocks - 1, 0), 0))
    vec = _acc((1, D_MODEL))
    return _call(
        body, "bwd_rnn", (nt,),
        [row_spec] * 8 + [halo_spec, _resident(conv_w.shape), _resident(wg.shape), _resident((1, D_MODEL)),
                          _resident(w_rg_out.shape)],
        [row_spec, row_spec, _acc(w_rg_out.shape), _acc(wg.shape), _acc(conv_w.shape), vec, _acc((2, D_MODEL)), vec],
        [_sds((s, D_MODEL), BF16), _sds((s, D_MODEL), BF16), _sds(w_rg_out.shape), _sds(wg.shape), _sds(conv_w.shape),
         _sds((1, D_MODEL)), _sds((2, D_MODEL)), _sds((1, D_MODEL))],
        [pltpu.VMEM((8, D_MODEL), F32), pltpu.VMEM((CONV_HALO, D_MODEL), F32)],
        (dyr, urx, gl, gg, xc, r, ig, h, h, conv_w, wg, lam, w_rg_out), tasks)


def _bwd_pool(dyp, d, pool_w, pool_scale, w_pool_out, ts, tasks=()):
    s = d.shape[0]
    nt = s // ts

    def body(dyp_ref, d_ref, pw_ref, ps_ref, wpo_ref, dzp_ref, dwpo_ref, dpw_ref, dps_ref, carry_scr):
        i = pl.program_id(0)
        k = nt - 1 - i

        @pl.when(i == 0)
        def _():
            carry_scr[...] = jnp.zeros_like(carry_scr)
            dwpo_ref[...] = jnp.zeros_like(dwpo_ref)
            dpw_ref[...] = jnp.zeros_like(dpw_ref)
            dps_ref[...] = jnp.zeros_like(dps_ref)

        dyp_v = dyp_ref[...]
        d_v = d_ref[...]
        ps = ps_ref[...]
        dypre = _dot_nt(dyp_v[:, 0:256], wpo_ref[0])
        for j in range(1, N_CHIPS):
            dypre = dypre + _dot_nt(dyp_v[:, j * 256:(j + 1) * 256], wpo_ref[j])
        y4 = jnp.concatenate([_dot(d_v[:, g * 128:(g + 1) * 128], pw_ref[g]) for g in range(POOL_GROUPS)], axis=1)
        ypre = (y4 * ps).astype(BF16)
        for j in range(N_CHIPS):
            dwpo_ref[j] += _dot_tn(ypre, dyp_v[:, j * 256:(j + 1) * 256])
        dps_ref[...] += jnp.sum(dypre * y4, axis=0, keepdims=True)
        dy4 = dypre * ps
        t = k * ts + lax.broadcasted_iota(jnp.int32, (ts, 1), 0)
        for g, w in enumerate(POOL_WINDOWS):
            lanes = slice(g * POOL_GROUP_DIM, (g + 1) * POOL_GROUP_DIM)
            dd = _dot_nt(dy4[:, lanes], pw_ref[g])
            dpw_ref[g] += _dot_tn(d_v[:, lanes], dy4[:, lanes])
            e = dd * (1.0 / jnp.minimum(t + 1, w).astype(F32))
            acc = jnp.concatenate([e, carry_scr[:, lanes]], axis=0)
            carry_scr[:, lanes] = e[0:POOL_HALO, :]
            n = ts + POOL_HALO
            sh = 1
            while sh < w:
                acc = acc + pltpu.roll(acc, n - sh, 0)
                sh *= 2
            dzp_ref[:, lanes] = (acc[:ts, :] - dd).astype(BF16)

    return _call(
        body, "bwd_pool", (nt,),
        [_rows(ts, D_MODEL, nt), _rows(ts, POOL_WIDTH, nt), _resident(pool_w.shape), _resident((1, POOL_WIDTH)),
         _resident(w_pool_out.shape)],
        [_rows(ts, POOL_WIDTH, nt), _acc(w_pool_out.shape), _acc(pool_w.shape), _acc((1, POOL_WIDTH))],
        [_sds((s, POOL_WIDTH), BF16), _sds(w_pool_out.shape), _sds(pool_w.shape), _sds((1, POOL_WIDTH))],
        [pltpu.VMEM((POOL_HALO, POOL_WIDTH), F32)],
        (dyp, d, pool_w, pool_scale, w_pool_out), tasks)


def _assemble_dz(dz_scr, dzp_ref, durx_ref, durg_ref, dgp_ref, dgr_ref):
    dz_scr[:, 0:512] = dzp_ref[...]
    dz_scr[:, 512:1536] = durx_ref[...]
    dz_scr[:, 1536:2560] = durg_ref[...]
    dz_scr[:, 2560:3584] = dgp_ref[...]
    dz_scr[:, 3584:4608] = dgr_ref[...]


def _dw_in(h1, dzp, durx, durg, dgp, dgr, ts, tasks=()):
    s = h1.shape[0]
    ts = min(2 * ts, s)
    nt = s // ts
    half = D_MODEL // 2

    def body(h1_ref, dzp_ref, durx_ref, durg_ref, dgp_ref, dgr_ref, theirs_ref, mine_ref, got_ref, dz_scr, sems):
        o, i = pl.program_id(0), pl.program_id(1)

        @pl.when((o == 0) & (i == 0))
        def _():
            theirs_ref[...] = jnp.zeros_like(theirs_ref)
            mine_ref[...] = jnp.zeros_like(mine_ref)

        _assemble_dz(dz_scr, dzp_ref, durx_ref, durg_ref, dgp_ref, dgr_ref)
        for out_ref, which in ((theirs_ref, 0), (mine_ref, 1)):
            @pl.when(o == which)
            def _():
                for j in range(N_CHIPS):
                    out_ref[j] += _dot_tn(h1_ref[...], dz_scr[:, j * IN_SHARD:(j + 1) * IN_SHARD])

        x, y, c, _, _ = _place()
        send = _remote(theirs_ref, got_ref, sems.at[0], sems.at[1], (x, y, 1 - c))
        pl.when((o == 1) & (i == 0))(send.start)
        pl.when((o == 1) & (i == nt - 1))(send.wait)

    def h1_cols(o, i):
        c = lax.axis_index("c")
        return i, jnp.where(o == 0, 1 - c, c)

    rows = lambda width: pl.BlockSpec((ts, width), lambda o, i: (i, 0))
    shape = (N_CHIPS, half, IN_SHARD)
    whole = pl.BlockSpec(shape, lambda o, i: (0, 0, 0))
    return _call(
        body, "dw_in", (2, nt), [pl.BlockSpec((ts, half), h1_cols), rows(POOL_WIDTH)] + [rows(D_MODEL)] * 4,
        [whole, whole, pl.BlockSpec(memory_space=pltpu.HBM)], [_sds(shape)] * 3,
        [pltpu.VMEM((ts, IN_TOTAL), BF16), pltpu.SemaphoreType.DMA((2,))], (h1, dzp, durx, durg, dgp, dgr), tasks)


def _bwd_in(dzp, durx, durg, dgp, dgr, x, dx1, g_pre, w_in, ts, tasks=()):
    s = x.shape[0]

    def body(dzp_ref, durx_ref, durg_ref, dgp_ref, dgr_ref, x_ref, dx1_ref, g_ref, win_ref, gx_ref, dg_ref, dz_scr):
        @pl.when(pl.program_id(0) == 0)
        def _():
            dg_ref[...] = jnp.zeros_like(dg_ref)

        _assemble_dz(dz_scr, dzp_ref, durx_ref, durg_ref, dgp_ref, dgr_ref)
        _, xh, r = _rms_fwd(x_ref[...], g_ref[...])
        dh1 = _dot_nt(dz_scr[:, 0:IN_SHARD], win_ref[0])
        for j in range(1, N_CHIPS):
            dh1 = dh1 + _dot_nt(dz_scr[:, j * IN_SHARD:(j + 1) * IN_SHARD], win_ref[j])
        dx, dg = _rms_bwd(xh, r, g_ref[...], dh1)
        dg_ref[...] += dg
        gx_ref[...] = dx1_ref[...] + dx

    row = _rows(ts, D_MODEL)
    return _call(
        body, "bwd_in", (s // ts,),
        [_rows(ts, POOL_WIDTH)] + [row] * 6 + [_resident((1, D_MODEL)), _resident(w_in.shape)],
        [row, _acc((1, D_MODEL))], [_sds((s, D_MODEL)), _sds((1, D_MODEL))],
        [pltpu.VMEM((ts, IN_TOTAL), BF16)], (dzp, durx, durg, dgp, dgr, x, dx1, g_pre, w_in), tasks)


def _place():
    x, y, c = lax.axis_index("x"), lax.axis_index("y"), lax.axis_index("c")
    others = [(1 - x, y), (x, 1 - y), (1 - x, 1 - y)]
    return x, y, c, 2 * x + y, others


def _remote(src, dst, send_sem, recv_sem, to):
    return pltpu.make_async_remote_copy(src_ref=src, dst_ref=dst, send_sem=send_sem, recv_sem=recv_sem,
                                        device_id=to, device_id_type=MESH)


def _own_slots(ws, dtypes, name, tasks=()):
    n = len(ws)
    hbm = pl.BlockSpec(memory_space=pltpu.HBM)

    def body(*refs):
        srcs, outs, f32_bufs, cast_bufs, sems = refs[:n], refs[n:2 * n], refs[2 * n:3 * n], refs[3 * n:4 * n], refs[4 * n]
        me = _place()[3]
        loads = [pltpu.make_async_copy(srcs[k], f32_bufs[k], sems.at[k, 0]) for k in range(n)]
        stores = [pltpu.make_async_copy(cast_bufs[k], outs[k].at[me], sems.at[k, 1]) for k in range(n)]
        for cp in loads:
            cp.start()
        for k in range(n):
            loads[k].wait()
            cast_bufs[k][...] = f32_bufs[k][...].astype(dtypes[k])
            stores[k].start()
        for cp in stores:
            cp.wait()

    return _call(
        body, name, (), [hbm] * n, [hbm] * n, [_sds((N_CHIPS,) + w.shape, dt) for w, dt in zip(ws, dtypes)],
        [pltpu.VMEM(w.shape, F32) for w in ws] + [pltpu.VMEM(w.shape, dt) for w, dt in zip(ws, dtypes)]
        + [pltpu.SemaphoreType.DMA((n, 2))],
        [pltpu.with_memory_space_constraint(w, pltpu.HBM) for w in ws], tasks)


def _run(tasks, name):
    if isinstance(tasks, _Task):
        return _call(None, name, (), [], [], [], [], (), (tasks,))[1][0]
    return _call(None, name, (), [], [], [], [], (), tuple(tasks))[1]


def _gather_task(bufs, relay_steps=(0, 0)):
    n = len(bufs)
    NBR_X, NBR_Y, QUARTER_VIA_Y, QUARTER_VIA_X, SIB_X, SIB_Y, SIB_DIAG = range(7)

    def parts(out):
        x, y, c, me, _ = _place()
        ah = out.shape[1] // 2
        q = ah // 2 if (ah // 2) % 16 == 0 else ah
        return c * ah, ah, q

    def copy(out, w, k, chip, row0, rows, to, sems):
        slot = out.at[chip, pl.ds(row0, rows)]
        return _remote(slot, slot, sems[0].at[w, k], sems[1].at[w, k], to)

    def plan(out, w, sems):
        x, y, c, me, _ = _place()
        row0, ah, q = parts(out)
        xn, yn, dg = 2 * (1 - x) + y, 2 * x + (1 - y), 2 * (1 - x) + (1 - y)
        to_x, to_y, sib = (1 - x, y, c), (x, 1 - y, c), (x, y, 1 - c)
        other = (1 - c) * ah
        cp = functools.partial(copy, out, w, sems=sems)
        sends = {NBR_X: cp(NBR_X, me, row0, ah, to_x), NBR_Y: cp(NBR_Y, me, row0, ah, to_y),
                 QUARTER_VIA_Y: cp(QUARTER_VIA_Y, xn, row0, q, to_y), SIB_X: cp(SIB_X, xn, row0, ah, sib),
                 SIB_Y: cp(SIB_Y, yn, row0, ah, sib), SIB_DIAG: cp(SIB_DIAG, dg, row0, ah, sib)}
        lands = {NBR_X: cp(NBR_X, xn, row0, ah, to_x), NBR_Y: cp(NBR_Y, yn, row0, ah, to_y),
                 QUARTER_VIA_Y: cp(QUARTER_VIA_Y, dg, row0, q, to_y), SIB_X: cp(SIB_X, xn, other, ah, sib),
                 SIB_Y: cp(SIB_Y, yn, other, ah, sib), SIB_DIAG: cp(SIB_DIAG, dg, other, ah, sib)}
        if q < ah:
            sends[QUARTER_VIA_X] = cp(QUARTER_VIA_X, yn, row0 + q, ah - q, to_x)
            lands[QUARTER_VIA_X] = cp(QUARTER_VIA_X, dg, row0 + q, ah - q, to_x)
        return sends, lands

    def start(ins, outs, sems):
        for w, out in enumerate(outs):
            sends, _ = plan(out, w, sems)
            sends[NBR_X].start()
            sends[NBR_Y].start()

    def pass_neighbours(ins, outs, sems):
        for w, out in enumerate(outs):
            sends, lands = plan(out, w, sems)
            lands[NBR_X].wait_recv()
            sends[QUARTER_VIA_Y].start()
            sends[SIB_X].start()
            lands[NBR_Y].wait_recv()
            if QUARTER_VIA_X in sends:
                sends[QUARTER_VIA_X].start()
            sends[SIB_Y].start()

    def pass_diagonal(ins, outs, sems):
        for w, out in enumerate(outs):
            sends, lands = plan(out, w, sems)
            lands[QUARTER_VIA_Y].wait_recv()
            if QUARTER_VIA_X in lands:
                lands[QUARTER_VIA_X].wait_recv()
            sends[SIB_DIAG].start()

    def finish(ins, outs, sems):
        for w, out in enumerate(outs):
            sends, lands = plan(out, w, sems)
            for k in (SIB_X, SIB_Y, SIB_DIAG):
                lands[k].wait_recv()
        for w, out in enumerate(outs):
            sends, _ = plan(out, w, sems)
            for cp in sends.values():
                cp.wait_send()

    return _Task(bufs, [_sds(b.shape, b.dtype) for b in bufs], {i: i for i in range(n)},
                 [pltpu.SemaphoreType.DMA((n, 7)), pltpu.SemaphoreType.DMA((n, 7))], start, finish,
                 [(pass_neighbours, relay_steps[0]), (pass_diagonal, relay_steps[1])])


def _halves_task(grads):
    n = len(grads)

    def copy(src, out, w, sems):
        x, y, c, _, _ = _place()
        ah = out.shape[1]
        return _remote(src.at[:, pl.ds((1 - c) * ah, ah)], out, sems[0].at[w], sems[1].at[w], (x, y, 1 - c))

    def start(ins, outs, sems):
        for w, (src, out) in enumerate(zip(ins, outs)):
            copy(src, out, w, sems).start()

    def finish(ins, outs, sems):
        for w, (src, out) in enumerate(zip(ins, outs)):
            copy(src, out, w, sems).wait()

    return _Task(grads, [_sds((g.shape[0], g.shape[1] // 2, g.shape[2]), g.dtype) for g in grads], {},
                 [pltpu.SemaphoreType.DMA((n,)), pltpu.SemaphoreType.DMA((n,))], start, finish)


def _exchange_task(sends, accs):
    n = len(accs)
    given = [s for s in sends if s is not None]

    def copies(ins, outs, sems):
        send_refs = iter(ins[:len(given)])
        srcs = [next(send_refs) if s is not None else None for s in sends]
        x, y, c, me, others = _place()
        for w, out in enumerate(outs):
            for j, (ox, oy) in enumerate(others):
                src = out.at[me] if srcs[w] is None else srcs[w].at[2 * ox + oy]
                yield _remote(src, out.at[me], sems[0].at[w, j], sems[1].at[w, j], (ox, oy, c))

    def start(ins, outs, sems):
        for cp in copies(ins, outs, sems):
            cp.start()

    def finish(ins, outs, sems):
        x, y, c, _, others = _place()
        for w, out in enumerate(outs):
            for j, (ox, oy) in enumerate(others):
                slot = out.at[2 * ox + oy]
                _remote(slot, slot, sems[0].at[w, j], sems[1].at[w, j], (ox, oy, c)).wait_recv()
        for cp in copies(ins, outs, sems):
            cp.wait_send()

    return _Task(given + list(accs), [_sds(a.shape, a.dtype) for a in accs], {len(given) + i: i for i in range(n)},
                 [pltpu.SemaphoreType.DMA((n, 3)), pltpu.SemaphoreType.DMA((n, 3))], start, finish)


def _swap_task(arrays):
    n = len(arrays)

    def copy(src, out, w, sems):
        x, y, c, _, _ = _place()
        return _remote(src, out, sems[0].at[w], sems[1].at[w], (x, y, 1 - c))

    def start(ins, outs, sems):
        for w, (src, out) in enumerate(zip(ins, outs)):
            copy(src, out, w, sems).start()

    def finish(ins, outs, sems):
        for w, (src, out) in enumerate(zip(ins, outs)):
            copy(src, out, w, sems).wait()

    return _Task(arrays, [_sds(a.shape, a.dtype) for a in arrays], {},
                 [pltpu.SemaphoreType.DMA((n,)), pltpu.SemaphoreType.DMA((n,))], start, finish)


def _all_devices_task(arrays):
    n = len(arrays)
    flips = [(dx, dy, dc) for dx in (0, 1) for dy in (0, 1) for dc in (0, 1)][1:]

    def peers():
        x, y, c, _, _ = _place()
        flip = lambda v, d: 1 - v if d else v
        return 4 * x + 2 * y + c, [(flip(x, dx), flip(y, dy), flip(c, dc)) for dx, dy, dc in flips]

    def start(ins, outs, sems):
        me, others = peers()
        for w, (src, out) in enumerate(zip(ins, outs)):
            pltpu.make_async_copy(src, out.at[me], sems[2].at[w]).start()
            for k, peer in enumerate(others):
                _remote(src, out.at[me], sems[0].at[w, k], sems[1].at[w, k], peer).start()

    def finish(ins, outs, sems):
        me, others = peers()
        for w, (src, out) in enumerate(zip(ins, outs)):
            for k, (px, py, pc) in enumerate(others):
                slot = out.at[4 * px + 2 * py + pc]
                _remote(slot, slot, sems[0].at[w, k], sems[1].at[w, k], (px, py, pc)).wait_recv()
            for k, peer in enumerate(others):
                _remote(src, out.at[me], sems[0].at[w, k], sems[1].at[w, k], peer).wait_send()
            pltpu.make_async_copy(src, out.at[me], sems[2].at[w]).wait()

    return _Task(arrays, [_sds((8,) + a.shape, a.dtype) for a in arrays], {},
                 [pltpu.SemaphoreType.DMA((n, 7)), pltpu.SemaphoreType.DMA((n, 7)), pltpu.SemaphoreType.DMA((n,))],
                 start, finish)


def _share_task(shares):
    n = len(shares)

    def copy(out, w, sems, slot):
        x, y, c, _, _ = _place()
        return _remote(out.at[slot], out.at[slot], sems[0].at[w], sems[1].at[w], (x, y, 1 - c))

    def start(ins, outs, sems):
        c = _place()[2]
        for w, out in enumerate(outs):
            copy(out, w, sems, c).start()

    def finish(ins, outs, sems):
        c = _place()[2]
        for w, out in enumerate(outs):
            copy(out, w, sems, 1 - c).wait_recv()
        for w, out in enumerate(outs):
            copy(out, w, sems, c).wait_send()

    return _Task(shares, [_sds(s.shape, s.dtype) for s in shares], {i: i for i in range(n)},
                 [pltpu.SemaphoreType.DMA((n,)), pltpu.SemaphoreType.DMA((n,))], start, finish)


TILE_BYTES = 2 * 1024 * 1024
PARTIAL_TILE_BYTES = 1024 * 1024


def _in_hbm(t):
    return pltpu.with_memory_space_constraint(t, pltpu.HBM)


def _row_tile(rows, cols, limit=TILE_BYTES):
    best = 8
    for tr in range(8, rows + 1, 8):
        if rows % tr == 0 and tr * cols * 4 <= limit:
            best = tr
    assert rows % best == 0, (rows, cols)
    return best


def _chip_partial(g, got, place, wire_dtype):
    ns, ah, b = got.shape
    sharded = ns == N_CHIPS
    tr = _row_tile(ah, b, PARTIAL_TILE_BYTES)
    nb = ah // tr
    first = 0 if g.shape[1] == ah else nb

    def body(place_ref, *refs):
        g_refs, got_refs, outs = refs[:ns], refs[ns:2 * ns], refs[2 * ns:]
        parts = [g_refs[k][0] + got_refs[k][0] for k in range(ns)]
        own = parts[0]
        if sharded:
            for k in range(ns):
                outs[0][k] = parts[k].astype(wire_dtype)
                if k:
                    own = jnp.where(place_ref[0] == k, parts[k], own)
        outs[-1][0] = own.astype(wire_dtype)

    blk = (1, tr, b)
    in_specs = ([pl.BlockSpec(blk, lambda i, s, k=k: (k, s[1] * first + i, 0)) for k in range(ns)]
                + [pl.BlockSpec(blk, lambda i, s, k=k: (k, i, 0)) for k in range(ns)])
    acc_spec = pl.BlockSpec(blk, lambda i, s: (s[0], i, 0))
    acc_shape = _sds((N_CHIPS, ah, b), wire_dtype)
    out = pl.pallas_call(
        body, name="grad_chip_partial",
        grid_spec=pltpu.PrefetchScalarGridSpec(
            num_scalar_prefetch=1, grid=(nb,), in_specs=in_specs,
            out_specs=[pl.BlockSpec((ns, tr, b), lambda i, s: (0, i, 0)), acc_spec] if sharded else [acc_spec]),
        out_shape=[acc_shape, acc_shape] if sharded else [acc_shape],
        compiler_params=pltpu.CompilerParams(dimension_semantics=("arbitrary",), vmem_limit_bytes=VMEM_LIMIT),
    )(place, *([g] * ns), *([got] * ns))
    return (out[0], out[1]) if sharded else (None, out[0])


def _chip_sum(acc, place):
    _, ah, b = acc.shape
    tr = _row_tile(ah, b)

    def body(place_ref, p_ref, out_ref):
        total = p_ref[0].astype(F32) + p_ref[1].astype(F32)
        total = total + p_ref[2].astype(F32)
        out_ref[0] = total + p_ref[3].astype(F32)

    return pl.pallas_call(
        body, name="grad_chip_sum",
        grid_spec=pltpu.PrefetchScalarGridSpec(
            num_scalar_prefetch=1, grid=(ah // tr,),
            in_specs=[pl.BlockSpec((N_CHIPS, tr, b), lambda i, s: (0, i, 0))],
            out_specs=pl.BlockSpec((1, tr, b), lambda i, s: (s[1], i, 0))),
        out_shape=_sds((2, ah, b)),
        compiler_params=pltpu.CompilerParams(dimension_semantics=("arbitrary",)),
    )(place, _in_hbm(acc))


def _adam_math(w, g, m, v):
    nm = ADAM_B1 * m + (1.0 - ADAM_B1) * g
    nv = ADAM_B2 * v + (1.0 - ADAM_B2) * (g * g)
    m_hat = nm / (1.0 - ADAM_B1 ** ADAM_STEP)
    v_hat = nv / (1.0 - ADAM_B2 ** ADAM_STEP)
    return -ADAM_LR * (m_hat / (jnp.sqrt(v_hat) + ADAM_EPS) + ADAM_WD * w), nm, nv


def _adamw(w, g, m, v):
    a, b = w.shape
    tr = _row_tile(a, b)

    def body(w_ref, g_ref, m_ref, v_ref, g_out, d_ref, nm_ref, nv_ref):
        g_out[...] = g_ref[...]
        d_ref[...], nm_ref[...], nv_ref[...] = _adam_math(w_ref[...], g_ref[...], m_ref[...], v_ref[...])

    blk = pl.BlockSpec((tr, b), lambda i: (i, 0))
    return pl.pallas_call(
        body, name="adamw", grid=(a // tr,),
        in_specs=[blk] * 4, out_specs=[blk] * 4, out_shape=[_sds((a, b))] * 4,
        compiler_params=pltpu.CompilerParams(dimension_semantics=("arbitrary",)),
    )(w, g, m, v)


def _adamw_sum(w, m, v, acc, got, place):
    a, b = w.shape
    ah = a // 2
    tr = _row_tile(ah, b)
    nb = ah // tr

    def body(place_ref, w_ref, m_ref, v_ref, acc_ref, got_ref, g_out, d_ref, nm_ref, nv_ref):
        mine = (pl.program_id(0) // nb) == place_ref[1]
        part = lambda k: jnp.where(mine, acc_ref[k], got_ref[k]).astype(F32)
        g = part(0) + part(1)
        g = g + part(2)
        g = g + part(3)
        g_out[...] = g
        d_ref[...], nm_ref[...], nv_ref[...] = _adam_math(w_ref[...], g, m_ref[...], v_ref[...])

    blk = pl.BlockSpec((tr, b), lambda i, s: (i, 0))
    mine_spec = pl.BlockSpec((N_CHIPS, tr, b), lambda i, s: (0, jnp.where(i // nb == s[1], i % nb, 0), 0))
    got_spec = pl.BlockSpec((N_CHIPS, tr, b), lambda i, s: (0, jnp.where(i // nb == s[1], 0, i % nb), 0))
    return pl.pallas_call(
        body, name="adamw_sum",
        grid_spec=pltpu.PrefetchScalarGridSpec(
            num_scalar_prefetch=1, grid=(a // tr,), in_specs=[blk] * 3 + [mine_spec, got_spec], out_specs=[blk] * 4),
        out_shape=[_sds((a, b))] * 4,
        compiler_params=pltpu.CompilerParams(dimension_semantics=("arbitrary",), vmem_limit_bytes=VMEM_LIMIT),
    )(place, w, m, v, _in_hbm(acc), _in_hbm(got))


def _adamw_pieces(g, pieces, name):
    n = len(pieces)

    def body(g_ref, *refs):
        def grad(rows, cols):
            if len(g_ref.shape) == 2:
                return g_ref[rows, cols]
            total = g_ref[0, rows, cols]
            for k in range(1, g_ref.shape[0]):
                total = total + g_ref[k, rows, cols]
            return total

        ins, outs = refs[:3 * n], refs[3 * n:]
        for i, piece in enumerate(pieces):
            w_ref, m_ref, v_ref = ins[3 * i:3 * i + 3]
            o_g, o_d, o_m, o_v = outs[4 * i:4 * i + 4]
            if len(piece) == 5:
                g_v = grad(piece[3], piece[4])
                o_g[...] = g_v
                o_d[...], o_m[...], o_v[...] = _adam_math(w_ref[...], g_v, m_ref[...], v_ref[...])
            else:
                for r in range(w_ref.shape[1] // SMALL_COLS):
                    lanes = slice(r * SMALL_COLS, (r + 1) * SMALL_COLS)
                    g_v = grad(slice(piece[3] + r, piece[3] + r + 1), slice(None))
                    o_g[:, lanes] = g_v
                    o_d[:, lanes], o_m[:, lanes], o_v[:, lanes] = _adam_math(w_ref[:, lanes], g_v, m_ref[:, lanes],
                                                                            v_ref[:, lanes])

    operands = [t for piece in pieces for t in piece[:3]]
    out = pl.pallas_call(
        body, name=name,
        out_shape=[_sds(piece[0].shape) for piece in pieces for _ in range(4)],
    )(g, *operands)
    return [tuple(out[4 * i:4 * i + 4]) for i in range(n)]


TINY_ROWS, TINY_COLS = 16, 768
SMALL_COLS = 128
SMALL_ROWS = 624


def _pack_tiny(conv_w, b_gates, fcw):
    ns = conv_w.shape[0]
    pad = lambda t: jnp.pad(t, ((0, 0), (0, 0), (0, TINY_COLS - t.shape[2])))
    z = lambda rows: jnp.zeros((ns, rows, TINY_COLS), F32)
    return jnp.concatenate([pad(conv_w), pad(b_gates), z(2), fcw, z(5)], axis=1)


def _unpack_tiny(t):
    return t[:, 0:4, 0:256], t[:, 4:6, 0:256], t[:, 8:11, :]


def _cols_to_shards(t, n):
    return t.reshape(t.shape[0], N_CHIPS, n).transpose(1, 0, 2)


def _shards_to_cols(t):
    return t.transpose(1, 0, 2).reshape(t.shape[1], -1)


_VECTORS = ("g_mix_post", "conv_b", "lru_lambda", "g_ffn_pre", "g_ffn_post", "g_ple_gate", "g_ple_post", "pool_scale",
            "ffn_conv_b")
_VECTOR_LEN = {"pool_scale": POOL_WIDTH, "ffn_conv_b": D_FF}
POOL_W_ROWS = POOL_GROUPS * POOL_GROUP_DIM


def _vector_rows():
    rows, row = {}, POOL_W_ROWS
    for k in _VECTORS:
        rows[k] = row
        row += max(8, _VECTOR_LEN.get(k, D_MODEL) // SMALL_COLS)
    return rows, row


def _pack_small(grads, loss):
    tiles = lambda t: jnp.pad(t, ((0, -t.shape[0] % 8), (0, 0)))
    parts = [grads["pool_w"].reshape(POOL_W_ROWS, SMALL_COLS)] + [tiles(grads[k].reshape(-1, SMALL_COLS)) for k in _VECTORS]
    parts.append(tiles(loss))
    used = sum(t.shape[0] for t in parts)
    return jnp.concatenate(parts + [jnp.zeros((SMALL_ROWS - used, SMALL_COLS), F32)], axis=0)


def _gates_block_diag(w):
    w4 = w.reshape(2, GATE_BLOCKS, 4, RNN_HEAD_DIM, RNN_HEAD_DIM)
    eye = jnp.eye(4, dtype=w.dtype)
    return jnp.einsum("gqhij,hk->gqhikj", w4, eye).reshape(2, GATE_BLOCKS, GATE_BLOCK, GATE_BLOCK)


def _gates_from_block_diag(dw):
    d6 = dw.reshape(2, GATE_BLOCKS, 4, RNN_HEAD_DIM, 4, RNN_HEAD_DIM)
    blocks = [d6[:, :, hh, :, hh, :] for hh in range(4)]
    return jnp.stack(blocks, axis=2).reshape(2, RNN_HEADS, RNN_HEAD_DIM, RNN_HEAD_DIM)


ROW_TILE = 256
DW_TILES = 4

_SHARDED = ("w_in", "w_pool_out", "w_rg_out", "w_o", "w_up", "w_down", "w_ple_gate", "w_ple_proj")
_WEIGHTS = ("g_mix_pre", "g_mix_post", "w_in", "pool_w", "pool_scale", "w_pool_out", "conv_w", "conv_b", "w_rg_gates",
            "b_rg_gates", "lru_lambda", "w_rg_out", "w_o", "g_ffn_pre", "g_ffn_post", "w_up", "ffn_conv_w", "ffn_conv_b",
            "w_down", "g_ple_gate", "w_ple_gate", "w_ple_proj", "g_ple_post")


def _wire_dtype(g):
    return BF16 if g.shape[1] >= 64 and g.shape[2] > SMALL_COLS else F32


def _partials(grads, got, place):
    parts = [_chip_partial(g, r, place, _wire_dtype(g)) for g, r in zip(grads, got)]
    return [send for send, _ in parts], [acc for _, acc in parts]


def _whole(both):
    return [b.reshape(2 * b.shape[1], b.shape[2]) for b in both]


def _step(x, p, tgt, rep, place, ts):
    vec = lambda k: rep[k].reshape(1, -1)
    tall = min(2 * ts, x.shape[0])
    pool_w = rep["pool_w"].astype(BF16)
    wg = _gates_block_diag(rep["w_rg_gates"]).astype(BF16)
    sq = lambda t: t.reshape(D_MODEL, D_MODEL)
    by4 = lambda t: t.reshape(N_CHIPS, -1, D_MODEL)

    first, ride1, ride2, ride3 = (("w_in", "w_pool_out", "tiny"), ("w_rg_out", "w_o", "w_down"), ("w_up",),
                                  ("w_ple_gate", "w_ple_proj"))
    later = ride1 + ride2 + ride3
    tiny = _pack_tiny(rep["conv_w"][None], rep["b_rg_gates"][None], rep["ffn_conv_w"][None])[0]
    own_first, _ = _own_slots([rep["w_in"], rep["w_pool_out"], tiny], [BF16, BF16, F32], "own_slots_first")
    own_later, (got,) = _own_slots([rep[k] for k in later], [BF16] * len(later), "own_slots_gather_first",
                                   [_gather_task(own_first)])
    own = dict(zip(later, own_later))
    full = dict(zip(first, got))
    conv_w, b_gates, fcw = [_shards_to_cols(t) for t in _unpack_tiny(full["tiny"])]

    (urx, urg, gp, gr, d, ypool, h1), (got,) = _fwd_in_pool(
        x, vec("g_mix_pre"), full["w_in"], pool_w, vec("pool_scale"), full["w_pool_out"], tall,
        [_gather_task([own[k] for k in ride1], relay_steps=(3, 1))])
    full.update(zip(ride1, got))
    w_rg_out, w_o, w_down = sq(full["w_rg_out"]), sq(full["w_o"]), full["w_down"].reshape(D_FF, D_MODEL)
    (xc, r, ig, h, yrnn, mo, x1, glr, ggr, sp, sr), (got,) = _fwd_rnn_merge(
        urx, urg, gp, gr, ypool, x, conv_w, vec("conv_b"), wg, b_gates, vec("lru_lambda"), w_rg_out, w_o,
        vec("g_mix_post"), ts, [_gather_task([own[k] for k in ride2], relay_steps=(7, 3))])
    full.update(zip(ride2, got))
    (up, gl, gg, h2, dn, x2), (got,) = _fwd_ffn(x1, vec("g_ffn_pre"), full["w_up"], fcw, vec("ffn_conv_b"), w_down,
                                               vec("g_ffn_post"), ts, [_gather_task([own[k] for k in ride3], relay_steps=(10, 5))])
    full.update(zip(ride3, got))
    dx2, loss, d_w_gate, d_w_proj, d_g_ple_gate, d_g_ple_post = _ple_loss(
        x2, p, tgt, vec("g_ple_gate"), sq(full["w_ple_gate"]), full["w_ple_proj"], vec("g_ple_post"), tall)
    dup, d_w_down, d_fcw, d_fcb, d_g_ffn_post = _bwd_ffn_down(dx2, dn, up, gl, gg, fcw, w_down, vec("g_ffn_post"), ts)

    names1, grads1 = ("w_ple_gate", "w_ple_proj", "w_down"), [by4(d_w_gate), d_w_proj, by4(d_w_down)]
    (dx1, d_g_ffn_pre), (got1,) = _bwd_ffn_up(dup, x1, dx2, vec("g_ffn_pre"), full["w_up"], tall, [_halves_task(grads1)])
    (d_w_up,), (accs1,) = _dw_up(h2, dup, ts, [_exchange_task(*_partials(grads1, got1, place))])
    (dgp, dgr, dyp, dyr, d_w_o, d_g_mix_post), (got2, theirs1) = _bwd_merge(
        dx1, mo, sp, sr, ypool, yrnn, vec("g_mix_post"), w_o, tall, [_halves_task([d_w_up]), _swap_task(accs1)])
    (durx, durg, d_w_rg_out, d_wg, d_conv_w, d_conv_b, d_b_gates, d_lam), (accs2,) = _bwd_rnn(
        dyr, urx, glr, ggr, xc, r, ig, h, conv_w, wg, vec("lru_lambda"), w_rg_out, ts,
        [_exchange_task(*_partials([d_w_up], got2, place))])
    names3 = ("w_o", "w_rg_out", "tiny", "w_rg_gates")
    grads3 = [by4(d_w_o), by4(d_w_rg_out),
              _pack_tiny(_cols_to_shards(d_conv_w, 256), _cols_to_shards(d_b_gates, 256), _cols_to_shards(d_fcw, 768)),
              _gates_from_block_diag(d_wg).reshape(1, 2 * RNN_HEADS * RNN_HEAD_DIM, RNN_HEAD_DIM)]
    (dzp, d_w_pool_out, d_pool_w, d_pool_scale), (got3, theirs2) = _bwd_pool(
        dyp, d, pool_w, vec("pool_scale"), full["w_pool_out"], tall, [_halves_task(grads3), _swap_task(accs2)])
    replicated = {"g_mix_post": d_g_mix_post, "conv_b": d_conv_b, "lru_lambda": d_lam, "g_ffn_pre": d_g_ffn_pre,
                  "g_ffn_post": d_g_ffn_post, "g_ple_gate": d_g_ple_gate, "g_ple_post": d_g_ple_post,
                  "pool_scale": d_pool_scale, "ffn_conv_b": d_fcb, "pool_w": d_pool_w}
    names4 = ("w_in", "w_pool_out", "small")
    small4 = [d_w_pool_out, _pack_small(replicated, loss)[None]]
    (_, d_w_in_half, got_w_in), (accs3, got_small4) = _dw_in(
        h1, dzp, durx, durg, dgp, dgr, ts, [_exchange_task(*_partials(grads3, got3, place)), _halves_task(small4)])
    grads4, got4 = [d_w_in_half] + small4, [got_w_in] + got_small4
    (grad_x, d_g_mix_pre), (accs4, theirs3, both3) = _bwd_in(
        dzp, durx, durg, dgp, dgr, x, dx1, vec("g_mix_pre"), full["w_in"], tall,
        [_exchange_task(*_partials(grads4, got4, place)), _swap_task(accs3[:2]),
         _share_task([_chip_sum(acc, place) for acc in accs3[2:]])])
    theirs4, both4, (g_mix_pre_parts,) = _run(
        [_swap_task(accs4[:2]), _share_task([_chip_sum(acc, place) for acc in accs4[2:]]),
         _all_devices_task([d_g_mix_pre.reshape(SUBLANES, SMALL_COLS)])], "grad_sibling_share")
    mine = accs1 + accs2 + accs3[:2] + accs4[:2]
    partials = dict(zip(names1 + ("w_up",) + names3[:2] + names4[:2], zip(mine, theirs1 + theirs2 + theirs3 + theirs4)))
    return grad_x, partials, dict(zip(names3[2:] + names4[2:], _whole(both3) + _whole(both4))), g_mix_pre_parts


def kernel(x, p, g_mix_pre, g_mix_post, w_in, pool_w, pool_scale, w_pool_out, conv_w, conv_b, w_rg_gates, b_rg_gates, lru_lambda, w_rg_out, w_o, g_ffn_pre, g_ffn_post, w_up, ffn_conv_w, ffn_conv_b, w_down, g_ple_gate, w_ple_gate, w_ple_proj, g_ple_post, loss_target, m_g_mix_pre, m_g_mix_post, m_w_in, m_pool_w, m_pool_scale, m_w_pool_out, m_conv_w, m_conv_b, m_w_rg_gates, m_b_rg_gates, m_lru_lambda, m_w_rg_out, m_w_o, m_g_ffn_pre, m_g_ffn_post, m_w_up, m_ffn_conv_w, m_ffn_conv_b, m_w_down, m_g_ple_gate, m_w_ple_gate, m_w_ple_proj, m_g_ple_post, v_g_mix_pre, v_g_mix_post, v_w_in, v_pool_w, v_pool_scale, v_w_pool_out, v_conv_w, v_conv_b, v_w_rg_gates, v_b_rg_gates, v_lru_lambda, v_w_rg_out, v_w_o, v_g_ffn_pre, v_g_ffn_post, v_w_up, v_ffn_conv_w, v_ffn_conv_b, v_w_down, v_g_ple_gate, v_w_ple_gate, v_w_ple_proj, v_g_ple_post):
    args = dict(locals())
    w = {k: args[k][0] for k in _WEIGHTS}
    m = {k: args["m_" + k][0] for k in _WEIGHTS}
    v = {k: args["v_" + k][0] for k in _WEIGHTS}
    place = jnp.stack([2 * lax.axis_index("x") + lax.axis_index("y"), lax.axis_index("c")]).astype(jnp.int32)
    grad_x, partials, reduced, g_mix_pre_parts = _step(x[0], p[0, 0], loss_target[0], w, place, ROW_TILE)

    gates_2d = (2 * RNN_HEADS * RNN_HEAD_DIM, RNN_HEAD_DIM)
    as2d = lambda k, shape: tuple(t[k].reshape(shape) for t in (w, m, v))
    done = {k: tuple(_adamw_sum(w[k], m[k], v[k], *partials[k], place)) for k in _SHARDED}
    gates_w, gates_m, gates_v = as2d("w_rg_gates", gates_2d)
    done["w_rg_gates"] = tuple(_adamw(gates_w, reduced["w_rg_gates"], gates_m, gates_v))
    tiny_names = ("conv_w", "b_rg_gates", "ffn_conv_w")
    tiny_at = ((slice(0, 4), slice(0, 256)), (slice(4, 6), slice(0, 256)), (slice(8, 11), slice(None)))
    done.update(zip(tiny_names, _adamw_pieces(
        reduced["tiny"], [(w[k], m[k], v[k]) + at for k, at in zip(tiny_names, tiny_at)], "adamw_tiny")))
    vector_rows, loss_row = _vector_rows()
    pieces = [as2d("pool_w", (POOL_W_ROWS, SMALL_COLS)) + (slice(0, POOL_W_ROWS), slice(None))]
    pieces += [as2d(k, (1, -1)) + (vector_rows[k],) for k in _VECTORS]
    done.update(zip(("pool_w",) + _VECTORS, _adamw_pieces(reduced["small"], pieces, "adamw_small")))
    done["g_mix_pre"] = _adamw_pieces(g_mix_pre_parts, [as2d("g_mix_pre", (1, -1)) + (0,)], "adamw_g_mix_pre")[0]

    result = [reduced["small"][loss_row, 0], grad_x[None]]
    for kind in range(4):
        result += [done[k][kind].reshape(args[k].shape) for k in _WEIGHTS]
    return tuple(result)
```

```python
import functools

import jax
import jax.numpy as jnp
from jax import lax
from jax.experimental import pallas as pl
from jax.experimental.pallas import tpu as pltpu

F32 = jnp.float32
BF16 = jnp.bfloat16

D_MODEL = 1024
POOL_WINDOWS = (2, 4, 8, 16)
POOL_GROUPS = 4
POOL_WIDTH = 512
POOL_GROUP_DIM = 128
RNN_HEADS = 16
RNN_HEAD_DIM = 64
GATE_BLOCK = 256
GATE_BLOCKS = D_MODEL // GATE_BLOCK
LRU_C = 8.0
D_FF = 3072
PLE_DIM = 256
RMS_EPS = 1e-6
IN_TOTAL = 4608
N_CHIPS = 4
IN_SHARD = IN_TOTAL // N_CHIPS
UP_SHARD = 2 * D_FF // N_CHIPS
POOL_HALO = 16
CONV_HALO = 8

ADAM_LR = 0.001
ADAM_B1 = 0.9
ADAM_B2 = 0.999
ADAM_EPS = 1e-08
ADAM_WD = 0.01
ADAM_STEP = 10

VMEM_LIMIT = 56 * 1024 * 1024
MESH = pl.DeviceIdType.MESH

_GELU_C = 0.7978845608028654
_GELU_A = 0.044715


def _dot(a, b):
    return jnp.dot(a.astype(BF16), b.astype(BF16), preferred_element_type=F32)


def _dot_nt(a, b):
    return lax.dot_general(a.astype(BF16), b.astype(BF16), (((1,), (1,)), ((), ())), preferred_element_type=F32)


def _dot_tn(a, b):
    return lax.dot_general(a.astype(BF16), b.astype(BF16), (((0,), (0,)), ((), ())), preferred_element_type=F32)


def _rms_fwd(x, g):
    r = lax.rsqrt(jnp.mean(x * x, axis=-1, keepdims=True) + RMS_EPS)
    xh = x * r
    return xh * g, xh, r


def _rms_bwd(xh, r, g, dy):
    dxh = dy * g
    dg = jnp.sum(dy * xh, axis=0, keepdims=True)
    dx = r * (dxh - xh * jnp.mean(dxh * xh, axis=-1, keepdims=True))
    return dx, dg


def _sigmoid(x):
    return 0.5 * jnp.tanh(0.5 * x) + 0.5


def _gelu(x):
    t = jnp.tanh(_GELU_C * (x + _GELU_A * x * x * x))
    return 0.5 * x * (1.0 + t), t


def _gelu_grad(x, t):
    return 0.5 * (1.0 + t) + 0.5 * x * (1.0 - t * t) * _GELU_C * (1.0 + 3.0 * _GELU_A * x * x)


def _softplus_neg(lam):
    nl = -lam
    return jnp.maximum(nl, 0.0) + jnp.log(1.0 + jnp.exp(-jnp.abs(nl)))


def _lru_coeffs(r, lam, first_row):
    c8 = LRU_C * _softplus_neg(lam)
    la = -(c8 * r)
    a = jnp.exp(la)
    m2 = jnp.tanh(-la) * (1.0 + a * a)
    mult = jnp.where(first_row, 1.0, jnp.sqrt(m2))
    return c8, a, m2, mult


SUBLANES = 8


def _scan_fwd(a, u, carry):
    n = a.shape[0]
    sub = lax.broadcasted_iota(jnp.int32, (n, 1), 0) % SUBLANES
    acc_a, acc_h = a, u
    for s in (1, 2, 4):
        m = sub >= s
        h_s = jnp.where(m, pltpu.roll(acc_h, s, 0), 0.0)
        a_s = jnp.where(m, pltpu.roll(acc_a, s, 0), 1.0)
        acc_h = acc_a * h_s + acc_h
        acc_a = acc_a * a_s
    out = []
    for g in range(n // SUBLANES):
        rows = slice(g * SUBLANES, (g + 1) * SUBLANES)
        out.append(acc_h[rows] + acc_a[rows] * carry)
        carry = out[-1][SUBLANES - 1:SUBLANES]
    return jnp.concatenate(out, axis=0)


def _scan_bwd(b, g, carry):
    n = b.shape[0]
    sub = lax.broadcasted_iota(jnp.int32, (n, 1), 0) % SUBLANES
    acc_b, acc_l = b, g
    for s in (1, 2, 4):
        m = sub < SUBLANES - s
        l_s = jnp.where(m, pltpu.roll(acc_l, n - s, 0), 0.0)
        b_s = jnp.where(m, pltpu.roll(acc_b, n - s, 0), 1.0)
        acc_l = acc_b * l_s + acc_l
        acc_b = acc_b * b_s
    out = [None] * (n // SUBLANES)
    for g in reversed(range(n // SUBLANES)):
        rows = slice(g * SUBLANES, (g + 1) * SUBLANES)
        out[g] = acc_l[rows] + acc_b[rows] * carry
        carry = out[g][0:1]
    return jnp.concatenate(out, axis=0)


def _shift_down(ext, k, halo):
    return pltpu.roll(ext, k, 0)[halo:] if k else ext[halo:]


def _shift_up(ext, k, ts):
    return pltpu.roll(ext, ext.shape[0] - k, 0)[:ts] if k else ext[:ts]


def _rows(ts, width, nt=None, col=0):
    if nt is None:
        return pl.BlockSpec((ts, width), lambda i: (i, col))
    return pl.BlockSpec((ts, width), lambda i: (nt - 1 - i, col))


def _resident(shape):
    zeros = (0,) * len(shape)
    return pl.BlockSpec(shape, lambda i: zeros, pipeline_mode=pl.Buffered(1))


def _acc(shape):
    zeros = (0,) * len(shape)
    return pl.BlockSpec(shape, lambda i: zeros)


def _params():
    return pltpu.CompilerParams(dimension_semantics=("arbitrary",), vmem_limit_bytes=VMEM_LIMIT)


def _sds(shape, dtype=F32):
    return jax.ShapeDtypeStruct(shape, dtype)


class _Task:
    def __init__(self, ins, out_shapes, aliases, sems, start, finish, relays=()):
        self.ins, self.out_shapes, self.aliases, self.sems = list(ins), list(out_shapes), dict(aliases), list(sems)
        self.start, self.relays, self.finish = start, list(relays), finish


def _call(body, name, grid, in_specs, out_specs, out_shape, scratch_shapes, args, tasks=()):
    n_in, n_out, n_scr = len(in_specs), len(out_specs), len(scratch_shapes)
    t_in = [len(t.ins) for t in tasks]
    t_out = [len(t.out_shapes) for t in tasks]
    t_sem = [len(t.sems) for t in tasks]
    steps = 1
    for g in grid:
        steps *= g

    def take(refs, pos, counts):
        groups = []
        for c in counts:
            groups.append(refs[pos:pos + c])
            pos += c
        return groups, pos

    def wrapped(*refs):
        (cin,), pos = take(refs, 0, [n_in])
        tin, pos = take(refs, pos, t_in)
        (cout,), pos = take(refs, pos, [n_out])
        tout, pos = take(refs, pos, t_out)
        (cscr,), pos = take(refs, pos, [n_scr])
        tsem, pos = take(refs, pos, t_sem)
        if not grid:
            for t, a, b, c in zip(tasks, tin, tout, tsem):
                t.start(a, b, c)
            if body is not None:
                body(*cin, *cout, *cscr)
            for t, a, b, c in zip(tasks, tin, tout, tsem):
                for relay, _ in t.relays:
                    relay(a, b, c)
            for t, a, b, c in zip(tasks, tin, tout, tsem):
                t.finish(a, b, c)
            return
        step = pl.program_id(0)
        for axis in range(1, len(grid)):
            step = step * grid[axis] + pl.program_id(axis)
        if tasks:
            @pl.when(step == 0)
            def _():
                for t, a, b, c in zip(tasks, tin, tout, tsem):
                    t.start(a, b, c)

        body(*cin, *cout, *cscr)
        for t, a, b, c in zip(tasks, tin, tout, tsem):
            for relay, before in t.relays:
                pl.when(step == max(steps - 1 - before, 0))(functools.partial(relay, a, b, c))

        if tasks:
            @pl.when(step == steps - 1)
            def _():
                for t, a, b, c in zip(tasks, tin, tout, tsem):
                    t.finish(a, b, c)

    aliases, in_pos, out_pos = {}, n_in, n_out
    for t, ni, no in zip(tasks, t_in, t_out):
        aliases.update({in_pos + a: out_pos + b for a, b in t.aliases.items()})
        in_pos, out_pos = in_pos + ni, out_pos + no
    any_spec = pl.BlockSpec(memory_space=pltpu.HBM)
    kwargs = dict(grid=grid, compiler_params=pltpu.CompilerParams(
        dimension_semantics=("arbitrary",) * len(grid), vmem_limit_bytes=VMEM_LIMIT)) if grid else dict(
        compiler_params=pltpu.CompilerParams(vmem_limit_bytes=VMEM_LIMIT))
    out = pl.pallas_call(
        wrapped, name=name,
        in_specs=list(in_specs) + [any_spec] * sum(t_in),
        out_specs=list(out_specs) + [any_spec] * sum(t_out),
        out_shape=list(out_shape) + [s for t in tasks for s in t.out_shapes],
        scratch_shapes=list(scratch_shapes) + [s for t in tasks for s in t.sems],
        input_output_aliases=aliases, **kwargs,
    )(*args, *[pltpu.with_memory_space_constraint(a, pltpu.HBM) for t in tasks for a in t.ins])
    task_outs, pos = take(list(out), n_out, t_out)
    return list(out[:n_out]), task_outs


def _fwd_in_pool(x, g_pre, w_in, pool_w, pool_scale, w_pool_out, ts, tasks=()):
    s = x.shape[0]

    def body(x_ref, g_ref, win_ref, pw_ref, ps_ref, wpo_ref,
             urx_ref, urg_ref, gp_ref, gr_ref, d_ref, yp_ref, h1_ref, z_scr, halo_scr):
        i = pl.program_id(0)

        @pl.when(i == 0)
        def _():
            halo_scr[...] = jnp.zeros_like(halo_scr)

        h1, _, _ = _rms_fwd(x_ref[...], g_ref[...])
        h1 = h1.astype(BF16)
        h1_ref[...] = h1
        for j in range(N_CHIPS):
            z_scr[:, j * IN_SHARD:(j + 1) * IN_SHARD] = jnp.dot(h1, win_ref[j], preferred_element_type=F32)
        urx_ref[...] = z_scr[:, 512:1536]
        urg_ref[...] = z_scr[:, 1536:2560]
        gp_ref[...] = z_scr[:, 2560:3584]
        gr_ref[...] = z_scr[:, 3584:4608]
        u = z_scr[:, 0:POOL_WIDTH]
        ext = jnp.concatenate([halo_scr[...], u], axis=0)
        halo_scr[...] = u[ts - POOL_HALO:, :]
        t = i * ts + lax.broadcasted_iota(jnp.int32, (ts, 1), 0)
        y4 = []
        for g, w in enumerate(POOL_WINDOWS):
            lanes = slice(g * POOL_GROUP_DIM, (g + 1) * POOL_GROUP_DIM)
            acc = ext[:, lanes]
            sh = 1
            while sh < w:
                acc = acc + pltpu.roll(acc, sh, 0)
                sh *= 2
            inv = 1.0 / jnp.minimum(t + 1, w).astype(F32)
            dg = acc[POOL_HALO:, :] * inv - u[:, lanes]
            d_ref[:, lanes] = dg.astype(BF16)
            y4.append(_dot(dg, pw_ref[g]))
        ypre = jnp.concatenate(y4, axis=1) * ps_ref[...]
        ypre = ypre.astype(BF16)
        for j in range(N_CHIPS):
            yp_ref[:, j * 256:(j + 1) * 256] = jnp.dot(ypre, wpo_ref[j], preferred_element_type=F32)

    return _call(
        body, "fwd_in_pool", (s // ts,),
        [_rows(ts, D_MODEL), _resident((1, D_MODEL)), _resident(w_in.shape), _resident(pool_w.shape),
         _resident((1, POOL_WIDTH)), _resident(w_pool_out.shape)],
        [_rows(ts, D_MODEL)] * 4 + [_rows(ts, POOL_WIDTH), _rows(ts, D_MODEL), _rows(ts, D_MODEL)],
        [_sds((s, D_MODEL))] * 4 + [_sds((s, POOL_WIDTH), BF16), _sds((s, D_MODEL)), _sds((s, D_MODEL), BF16)],
        [pltpu.VMEM((ts, IN_TOTAL), F32), pltpu.VMEM((POOL_HALO, POOL_WIDTH), F32)],
        (x, g_pre, w_in, pool_w, pool_scale, w_pool_out), tasks)


def _fwd_rnn_merge(urx, urg, gp, gr, ypool, x, conv_w, conv_b, wg, bg, lam, w_rg_out, w_o, g_post, ts, tasks=()):
    s = x.shape[0]

    def body(urx_ref, urg_ref, gp_ref, gr_ref, yp_ref, x_ref, cw_ref, cb_ref, wg_ref, bg_ref, lam_ref, wrg_ref, wo_ref,
             gpost_ref, xc_ref, r_ref, ig_ref, h_ref, yr_ref, mo_ref, x1_ref, gl_ref, gg_ref, sp_ref, sr_ref,
             halo_scr, carry_scr):
        i = pl.program_id(0)

        @pl.when(i == 0)
        def _():
            halo_scr[...] = jnp.zeros_like(halo_scr)
            carry_scr[...] = jnp.zeros_like(carry_scr)

        urx_v = urx_ref[...]
        ext = jnp.concatenate([halo_scr[...], urx_v], axis=0)
        halo_scr[...] = urx_v[ts - CONV_HALO:, :]
        cw = cw_ref[...]
        xc = (cb_ref[...] + cw[3:4] * urx_v + cw[2:3] * _shift_down(ext, 1, CONV_HALO)
              + cw[1:2] * _shift_down(ext, 2, CONV_HALO) + cw[0:1] * _shift_down(ext, 3, CONV_HALO))
        xc_ref[...] = xc
        xcb = xc.astype(BF16)
        lin = []
        for gate in range(2):
            parts = [jnp.dot(xcb[:, q * GATE_BLOCK:(q + 1) * GATE_BLOCK], wg_ref[gate, q], preferred_element_type=F32)
                     for q in range(GATE_BLOCKS)]
            lin.append(jnp.concatenate(parts, axis=1) + bg_ref[gate:gate + 1, :])
        r = _sigmoid(lin[0])
        ig = _sigmoid(lin[1])
        r_ref[...] = r
        ig_ref[...] = ig
        first_row = (i * ts + lax.broadcasted_iota(jnp.int32, (ts, 1), 0)) == 0
        _, a, _, mult = _lru_coeffs(r, lam_ref[...], first_row)
        h = _scan_fwd(a, mult * ig * xc, carry_scr[0:1, :])
        carry_scr[0:1, :] = h[ts - 1:ts, :]
        h_ref[...] = h
        urg_v = urg_ref[...]
        gl, t = _gelu(urg_v)
        gl_ref[...] = gl.astype(BF16)
        gg_ref[...] = _gelu_grad(urg_v, t).astype(BF16)
        yr = _dot(h * gl, wrg_ref[...])
        yr_ref[...] = yr
        sp = _sigmoid(gp_ref[...])
        sr = _sigmoid(gr_ref[...])
        sp_ref[...] = sp.astype(BF16)
        sr_ref[...] = sr.astype(BF16)
        merged = sp * yp_ref[...] + sr * yr
        mo = _dot(merged, wo_ref[...])
        mo_ref[...] = mo
        y, _, _ = _rms_fwd(mo, gpost_ref[...])
        x1_ref[...] = x_ref[...] + y

    row = _rows(ts, D_MODEL)
    return _call(
        body, "fwd_rnn_merge", (s // ts,),
        [row] * 6 + [_resident(conv_w.shape), _resident((1, D_MODEL)), _resident(wg.shape), _resident(bg.shape),
                     _resident((1, D_MODEL)), _resident(w_rg_out.shape), _resident(w_o.shape), _resident((1, D_MODEL))],
        [row] * 11, [_sds((s, D_MODEL))] * 7 + [_sds((s, D_MODEL), BF16)] * 4,
        [pltpu.VMEM((CONV_HALO, D_MODEL), F32), pltpu.VMEM((8, D_MODEL), F32)],
        (urx, urg, gp, gr, ypool, x, conv_w, conv_b, wg, bg, lam, w_rg_out, w_o, g_post), tasks)


def _fwd_ffn(x1, g_pre, w_up, fcw, fcb, w_down, g_post, ts, tasks=()):
    s = x1.shape[0]

    def body(x1_ref, g_ref, wup_ref, fcw_ref, fcb_ref, wd_ref, gpost_ref,
             up_ref, gl_ref, gg_ref, h2_ref, dn_ref, x2_ref, up_scr, halo_scr):
        i = pl.program_id(0)

        @pl.when(i == 0)
        def _():
            halo_scr[...] = jnp.zeros_like(halo_scr)

        x1_v = x1_ref[...]
        h2, _, _ = _rms_fwd(x1_v, g_ref[...])
        h2 = h2.astype(BF16)
        h2_ref[...] = h2
        for j in range(N_CHIPS):
            up_scr[:, j * UP_SHARD:(j + 1) * UP_SHARD] = jnp.dot(h2, wup_ref[j], preferred_element_type=F32)
        up_ref[...] = up_scr[...].astype(BF16)
        ug = up_scr[:, 0:D_FF]
        ext = jnp.concatenate([halo_scr[...], ug], axis=0)
        halo_scr[...] = ug[ts - CONV_HALO:, :]
        w = fcw_ref[...]
        gh = (fcb_ref[...] + w[2:3] * ug + w[1:2] * _shift_down(ext, 1, CONV_HALO)
              + w[0:1] * _shift_down(ext, 2, CONV_HALO))
        gl, t = _gelu(gh)
        gl_ref[...] = gl.astype(BF16)
        gg_ref[...] = _gelu_grad(gh, t).astype(BF16)
        dn = _dot(gl * up_scr[:, D_FF:], wd_ref[...])
        dn_ref[...] = dn
        y, _, _ = _rms_fwd(dn, gpost_ref[...])
        x2_ref[...] = x1_v + y

    row = _rows(ts, D_MODEL)
    return _call(
        body, "fwd_ffn", (s // ts,),
        [row, _resident((1, D_MODEL)), _resident(w_up.shape), _resident(fcw.shape), _resident((1, D_FF)),
         _resident(w_down.shape), _resident((1, D_MODEL))],
        [_rows(ts, 2 * D_FF), _rows(ts, D_FF), _rows(ts, D_FF), row, row, row],
        [_sds((s, 2 * D_FF), BF16), _sds((s, D_FF), BF16), _sds((s, D_FF), BF16), _sds((s, D_MODEL), BF16),
         _sds((s, D_MODEL)), _sds((s, D_MODEL))],
        [pltpu.VMEM((ts, 2 * D_FF), F32), pltpu.VMEM((CONV_HALO, D_FF), F32)],
        (x1, g_pre, w_up, fcw, fcb, w_down, g_post), tasks)


def _ple_loss(x2, p, tgt, g_gate, w_gate, w_proj, g_post, ts):
    s = x2.shape[0]

    def body(x2_ref, p_ref, t_ref, gg_ref, wg_ref, wp_ref, gp_ref, dx2_ref, loss_ref, dwg_ref, dwp_ref, dgg_ref, dgp_ref):
        @pl.when(pl.program_id(0) == 0)
        def _():
            loss_ref[...] = jnp.zeros_like(loss_ref)
            dwg_ref[...] = jnp.zeros_like(dwg_ref)
            dwp_ref[...] = jnp.zeros_like(dwp_ref)
            dgg_ref[...] = jnp.zeros_like(dgg_ref)
            dgp_ref[...] = jnp.zeros_like(dgp_ref)

        x2_v = x2_ref[...]
        n3, xh3, r3 = _rms_fwd(x2_v, gg_ref[...])
        pg = _sigmoid(_dot(n3, wg_ref[...]))
        pb = p_ref[...].astype(BF16)
        q = jnp.concatenate([jnp.dot(pb, wp_ref[j], preferred_element_type=F32) for j in range(N_CHIPS)], axis=1)
        ple, qh, rq = _rms_fwd(q, gp_ref[...])
        e = x2_v + pg * ple - t_ref[...]
        loss_ref[...] += 0.5 * jnp.sum(jnp.mean(e * e, axis=-1, keepdims=True), axis=0, keepdims=True)
        dy = e * (1.0 / D_MODEL)
        dpgl = dy * ple * pg * (1.0 - pg)
        dwg_ref[...] += _dot_tn(n3, dpgl)
        dx3, dgg = _rms_bwd(xh3, r3, gg_ref[...], _dot_nt(dpgl, wg_ref[...]))
        dgg_ref[...] += dgg
        dq, dgp = _rms_bwd(qh, rq, gp_ref[...], dy * pg)
        dgp_ref[...] += dgp
        for j in range(N_CHIPS):
            dwp_ref[j] += _dot_tn(pb, dq[:, j * 256:(j + 1) * 256])
        dx2_ref[...] = dy + dx3

    row = _rows(ts, D_MODEL)
    vec = _acc((1, D_MODEL))
    return pl.pallas_call(
        body, name="ple_loss", grid=(s // ts,),
        in_specs=[row, _rows(ts, PLE_DIM), row, _resident((1, D_MODEL)), _resident(w_gate.shape), _resident(w_proj.shape),
                  _resident((1, D_MODEL))],
        out_specs=[row, _acc((1, 128)), _acc(w_gate.shape), _acc(w_proj.shape), vec, vec],
        out_shape=[_sds((s, D_MODEL)), _sds((1, 128)), _sds(w_gate.shape), _sds(w_proj.shape), _sds((1, D_MODEL)),
                   _sds((1, D_MODEL))],
        compiler_params=_params(),
    )(x2, p, tgt, g_gate, w_gate, w_proj, g_post)


def _bwd_ffn_down(dx2, dn, up, gl, gg, fcw, w_down, g_post, ts):
    s = dx2.shape[0]
    nt = s // ts

    def body(dx2_ref, dn_ref, up_ref, gl_ref, gg_ref, fcw_ref, wd_ref, gpost_ref,
             dup_ref, dwd_ref, dfcw_ref, dfcb_ref, dgp_ref, carry_scr):
        i = pl.program_id(0)

        @pl.when(i == 0)
        def _():
            carry_scr[...] = jnp.zeros_like(carry_scr)
            dwd_ref[...] = jnp.zeros_like(dwd_ref)
            dfcw_ref[...] = jnp.zeros_like(dfcw_ref)
            dfcb_ref[...] = jnp.zeros_like(dfcb_ref)
            dgp_ref[...] = jnp.zeros_like(dgp_ref)

        _, xh, r = _rms_fwd(dn_ref[...], gpost_ref[...])
        ddn, dgp = _rms_bwd(xh, r, gpost_ref[...], dx2_ref[...])
        dgp_ref[...] += dgp
        dhid = _dot_nt(ddn, wd_ref[...])
        ug = up_ref[:, 0:D_FF].astype(F32)
        uv = up_ref[:, D_FF:].astype(F32)
        gl = gl_ref[...].astype(F32)
        w = fcw_ref[...]
        dwd_ref[...] += _dot_tn(gl * uv, ddn)
        dgh = dhid * uv * gg_ref[...].astype(F32)
        dup_ref[:, D_FF:] = (dhid * gl).astype(BF16)
        extd = jnp.concatenate([dgh, carry_scr[...]], axis=0)
        carry_scr[...] = dgh[0:CONV_HALO, :]
        d1 = _shift_up(extd, 1, ts)
        d2 = _shift_up(extd, 2, ts)
        dup_ref[:, 0:D_FF] = (w[2:3] * dgh + w[1:2] * d1 + w[0:1] * d2).astype(BF16)
        dfcw_ref[2:3, :] += jnp.sum(ug * dgh, axis=0, keepdims=True)
        dfcw_ref[1:2, :] += jnp.sum(ug * d1, axis=0, keepdims=True)
        dfcw_ref[0:1, :] += jnp.sum(ug * d2, axis=0, keepdims=True)
        dfcb_ref[...] += jnp.sum(dgh, axis=0, keepdims=True)

    row = _rows(ts, D_MODEL, nt)
    wide = _rows(ts, D_FF, nt)
    return pl.pallas_call(
        body, name="bwd_ffn_down", grid=(nt,),
        in_specs=[row, row, _rows(ts, 2 * D_FF, nt), wide, wide, _resident(fcw.shape), _resident(w_down.shape),
                  _resident((1, D_MODEL))],
        out_specs=[_rows(ts, 2 * D_FF, nt), _acc(w_down.shape), _acc(fcw.shape), _acc((1, D_FF)), _acc((1, D_MODEL))],
        out_shape=[_sds((s, 2 * D_FF), BF16), _sds(w_down.shape), _sds(fcw.shape), _sds((1, D_FF)), _sds((1, D_MODEL))],
        scratch_shapes=[pltpu.VMEM((CONV_HALO, D_FF), F32)],
        compiler_params=_params(),
    )(dx2, dn, up, gl, gg, fcw, w_down, g_post)


def _bwd_ffn_up(dup, x1, dx2, g_pre, w_up, ts, tasks=()):
    s = x1.shape[0]

    def body(dup_ref, x1_ref, dx2_ref, g_ref, wup_ref, dx1_ref, dg_ref):
        @pl.when(pl.program_id(0) == 0)
        def _():
            dg_ref[...] = jnp.zeros_like(dg_ref)

        _, xh, r = _rms_fwd(x1_ref[...], g_ref[...])
        dh2 = _dot_nt(dup_ref[:, 0:UP_SHARD], wup_ref[0])
        for j in range(1, N_CHIPS):
            dh2 = dh2 + _dot_nt(dup_ref[:, j * UP_SHARD:(j + 1) * UP_SHARD], wup_ref[j])
        dx, dg = _rms_bwd(xh, r, g_ref[...], dh2)
        dg_ref[...] += dg
        dx1_ref[...] = dx2_ref[...] + dx

    row = _rows(ts, D_MODEL)
    return _call(
        body, "bwd_ffn_up", (s // ts,),
        [_rows(ts, 2 * D_FF), row, row, _resident((1, D_MODEL)), _resident(w_up.shape)],
        [row, _acc((1, D_MODEL))], [_sds((s, D_MODEL)), _sds((1, D_MODEL))], [],
        (dup, x1, dx2, g_pre, w_up), tasks)


def _dw_up(h2, dup, ts, tasks=()):
    s = h2.shape[0]
    ts = min(DW_TILES * ts, s)

    def body(h2_ref, dup_ref, out_ref):
        @pl.when(pl.program_id(1) == 0)
        def _():
            out_ref[...] = jnp.zeros_like(out_ref)

        out_ref[0] += _dot_tn(h2_ref[...], dup_ref[...])

    return _call(
        body, "dw_up", (N_CHIPS, s // ts),
        [pl.BlockSpec((ts, D_MODEL), lambda j, i: (i, 0)), pl.BlockSpec((ts, UP_SHARD), lambda j, i: (i, j))],
        [pl.BlockSpec((1, D_MODEL, UP_SHARD), lambda j, i: (j, 0, 0))], [_sds((N_CHIPS, D_MODEL, UP_SHARD))], [],
        (h2, dup), tasks)


def _bwd_merge(dx1, mo, sp, sr, ypool, yrnn, g_post, w_o, ts, tasks=()):
    s = dx1.shape[0]

    def body(dx1_ref, mo_ref, sp_ref, sr_ref, yp_ref, yr_ref, g_ref, wo_ref,
             dgp_ref, dgr_ref, dyp_ref, dyr_ref, dwo_ref, dg_ref):
        @pl.when(pl.program_id(0) == 0)
        def _():
            dwo_ref[...] = jnp.zeros_like(dwo_ref)
            dg_ref[...] = jnp.zeros_like(dg_ref)

        _, xh, r = _rms_fwd(mo_ref[...], g_ref[...])
        dmo, dg = _rms_bwd(xh, r, g_ref[...], dx1_ref[...])
        dg_ref[...] += dg
        dmerged = _dot_nt(dmo, wo_ref[...])
        sp = sp_ref[...].astype(F32)
        sr = sr_ref[...].astype(F32)
        yp = yp_ref[...]
        yr = yr_ref[...]
        dwo_ref[...] += _dot_tn(sp * yp + sr * yr, dmo)
        dgp_ref[...] = (dmerged * yp * sp * (1.0 - sp)).astype(BF16)
        dgr_ref[...] = (dmerged * yr * sr * (1.0 - sr)).astype(BF16)
        dyp_ref[...] = (dmerged * sp).astype(BF16)
        dyr_ref[...] = (dmerged * sr).astype(BF16)

    row = _rows(ts, D_MODEL)
    return _call(
        body, "bwd_merge", (s // ts,),
        [row] * 6 + [_resident((1, D_MODEL)), _resident(w_o.shape)],
        [row] * 4 + [_acc(w_o.shape), _acc((1, D_MODEL))],
        [_sds((s, D_MODEL), BF16)] * 4 + [_sds(w_o.shape), _sds((1, D_MODEL))], [],
        (dx1, mo, sp, sr, ypool, yrnn, g_post, w_o), tasks)


def _bwd_rnn(dyr, urx, gl, gg, xc, r, ig, h, conv_w, wg, lam, w_rg_out, ts, tasks=()):
    s = urx.shape[0]
    nt = s // ts
    halo_blocks = ts // CONV_HALO

    def body(dyr_ref, urx_ref, gl_ref, gg_ref, xc_ref, r_ref, ig_ref, h_ref, hh_ref, cw_ref, wg_ref, lam_ref, wrg_ref,
             durx_ref, durg_ref, dwrg_ref, dwg_ref, dcw_ref, dcb_ref, dbg_ref, dlam_ref, mu_scr, carry_scr):
        i = pl.program_id(0)
        k = nt - 1 - i

        @pl.when(i == 0)
        def _():
            mu_scr[...] = jnp.zeros_like(mu_scr)
            carry_scr[...] = jnp.zeros_like(carry_scr)
            dwrg_ref[...] = jnp.zeros_like(dwrg_ref)
            dwg_ref[...] = jnp.zeros_like(dwg_ref)
            dcw_ref[...] = jnp.zeros_like(dcw_ref)
            dcb_ref[...] = jnp.zeros_like(dcb_ref)
            dbg_ref[...] = jnp.zeros_like(dbg_ref)
            dlam_ref[...] = jnp.zeros_like(dlam_ref)

        row = lax.broadcasted_iota(jnp.int32, (ts, 1), 0)
        first_row = (k * ts + row) == 0
        h = h_ref[...]
        dyr_v = dyr_ref[...]
        dhr = _dot_nt(dyr_v, wrg_ref[...])
        gl = gl_ref[...].astype(F32)
        dwrg_ref[...] += _dot_tn(h * gl, dyr_v)
        durg_ref[...] = (dhr * h * gg_ref[...].astype(F32)).astype(BF16)
        r_v = r_ref[...]
        ig_v = ig_ref[...]
        xc_v = xc_ref[...]
        lam_v = lam_ref[...]
        c8, a, m2, mult = _lru_coeffs(r_v, lam_v, first_row)
        b = jnp.where(row == ts - 1, 1.0, pltpu.roll(a, ts - 1, 0))
        lt = _scan_bwd(b, dhr * gl, mu_scr[0:1, :])
        mu_scr[0:1, :] = a[0:1, :] * lt[0:1, :]
        h_before = jnp.where(k > 0, hh_ref[CONV_HALO - 1:CONV_HALO, :], 0.0)
        hprev = jnp.where(row == 0, h_before, pltpu.roll(h, 1, 0))
        dmult = lt * ig_v * xc_v
        da = lt * hprev - jnp.where(first_row, 0.0, dmult * a * lax.rsqrt(m2))
        dla = da * a
        dlam_ref[...] += jnp.sum(dla * r_v, axis=0, keepdims=True)
        dlr = (dla * (-c8)) * r_v * (1.0 - r_v)
        dli = (lt * mult * xc_v) * ig_v * (1.0 - ig_v)
        dbg_ref[0:1, :] += jnp.sum(dlr, axis=0, keepdims=True)
        dbg_ref[1:2, :] += jnp.sum(dli, axis=0, keepdims=True)
        xcb = xc_v.astype(BF16)
        parts = []
        for q in range(GATE_BLOCKS):
            blk = slice(q * GATE_BLOCK, (q + 1) * GATE_BLOCK)
            dlr_q = dlr[:, blk].astype(BF16)
            dli_q = dli[:, blk].astype(BF16)
            parts.append(_dot_nt(dlr_q, wg_ref[0, q]) + _dot_nt(dli_q, wg_ref[1, q]))
            dwg_ref[0, q] += _dot_tn(xcb[:, blk], dlr_q)
            dwg_ref[1, q] += _dot_tn(xcb[:, blk], dli_q)
        dxc = lt * mult * ig_v + jnp.concatenate(parts, axis=1)
        extd = jnp.concatenate([dxc, carry_scr[...]], axis=0)
        carry_scr[...] = dxc[0:CONV_HALO, :]
        cw = cw_ref[...]
        urx_v = urx_ref[...]
        durx = cw[3:4] * dxc
        dcw_ref[3:4, :] += jnp.sum(urx_v * dxc, axis=0, keepdims=True)
        for j in (1, 2, 3):
            dj = _shift_up(extd, j, ts)
            durx = durx + cw[3 - j:4 - j] * dj
            dcw_ref[3 - j:4 - j, :] += jnp.sum(urx_v * dj, axis=0, keepdims=True)
        durx_ref[...] = durx.astype(BF16)
        dcb_ref[...] += jnp.sum(dxc, axis=0, keepdims=True)

        @pl.when(i == nt - 1)
        def _():
            dlam_ref[...] = dlam_ref[...] * (LRU_C * jax.nn.sigmoid(-lam_v))

    row_spec = _rows(ts, D_MODEL, nt)
    halo_spec = pl.BlockSpec((CONV_HALO, D_MODEL), lambda i: (jnp.maximum((nt - 1 - i) * halo_blocks - 1, 0), 0))
    vec = _acc((1, D_MODEL))
    return _call(
        body, "bwd_rnn", (nt,),
        [row_spec] * 8 + [halo_spec, _resident(conv_w.shape), _resident(wg.shape), _resident((1, D_MODEL)),
                          _resident(w_rg_out.shape)],
        [row_spec, row_spec, _acc(w_rg_out.shape), _acc(wg.shape), _acc(conv_w.shape), vec, _acc((2, D_MODEL)), vec],
        [_sds((s, D_MODEL), BF16), _sds((s, D_MODEL), BF16), _sds(w_rg_out.shape), _sds(wg.shape), _sds(conv_w.shape),
         _sds((1, D_MODEL)), _sds((2, D_MODEL)), _sds((1, D_MODEL))],
        [pltpu.VMEM((8, D_MODEL), F32), pltpu.VMEM((CONV_HALO, D_MODEL), F32)],
        (dyr, urx, gl, gg, xc, r, ig, h, h, conv_w, wg, lam, w_rg_out), tasks)


def _bwd_pool(dyp, d, pool_w, pool_scale, w_pool_out, ts, tasks=()):
    s = d.shape[0]
    nt = s // ts

    def body(dyp_ref, d_ref, pw_ref, ps_ref, wpo_ref, dzp_ref, dwpo_ref, dpw_ref, dps_ref, carry_scr):
        i = pl.program_id(0)
        k = nt - 1 - i

        @pl.when(i == 0)
        def _():
            carry_scr[...] = jnp.zeros_like(carry_scr)
            dwpo_ref[...] = jnp.zeros_like(dwpo_ref)
            dpw_ref[...] = jnp.zeros_like(dpw_ref)
            dps_ref[...] = jnp.zeros_like(dps_ref)

        dyp_v = dyp_ref[...]
        d_v = d_ref[...]
        ps = ps_ref[...]
        dypre = _dot_nt(dyp_v[:, 0:256], wpo_ref[0])
        for j in range(1, N_CHIPS):
            dypre = dypre + _dot_nt(dyp_v[:, j * 256:(j + 1) * 256], wpo_ref[j])
        y4 = jnp.concatenate([_dot(d_v[:, g * 128:(g + 1) * 128], pw_ref[g]) for g in range(POOL_GROUPS)], axis=1)
        ypre = (y4 * ps).astype(BF16)
        for j in range(N_CHIPS):
            dwpo_ref[j] += _dot_tn(ypre, dyp_v[:, j * 256:(j + 1) * 256])
        dps_ref[...] += jnp.sum(dypre * y4, axis=0, keepdims=True)
        dy4 = dypre * ps
        t = k * ts + lax.broadcasted_iota(jnp.int32, (ts, 1), 0)
        for g, w in enumerate(POOL_WINDOWS):
            lanes = slice(g * POOL_GROUP_DIM, (g + 1) * POOL_GROUP_DIM)
            dd = _dot_nt(dy4[:, lanes], pw_ref[g])
            dpw_ref[g] += _dot_tn(d_v[:, lanes], dy4[:, lanes])
            e = dd * (1.0 / jnp.minimum(t + 1, w).astype(F32))
            acc = jnp.concatenate([e, carry_scr[:, lanes]], axis=0)
            carry_scr[:, lanes] = e[0:POOL_HALO, :]
            n = ts + POOL_HALO
            sh = 1
            while sh < w:
                acc = acc + pltpu.roll(acc, n - sh, 0)
                sh *= 2
            dzp_ref[:, lanes] = (acc[:ts, :] - dd).astype(BF16)

    return _call(
        body, "bwd_pool", (nt,),
        [_rows(ts, D_MODEL, nt), _rows(ts, POOL_WIDTH, nt), _resident(pool_w.shape), _resident((1, POOL_WIDTH)),
         _resident(w_pool_out.shape)],
        [_rows(ts, POOL_WIDTH, nt), _acc(w_pool_out.shape), _acc(pool_w.shape), _acc((1, POOL_WIDTH))],
        [_sds((s, POOL_WIDTH), BF16), _sds(w_pool_out.shape), _sds(pool_w.shape), _sds((1, POOL_WIDTH))],
        [pltpu.VMEM((POOL_HALO, POOL_WIDTH), F32)],
        (dyp, d, pool_w, pool_scale, w_pool_out), tasks)


def _assemble_dz(dz_scr, dzp_ref, durx_ref, durg_ref, dgp_ref, dgr_ref):
    dz_scr[:, 0:512] = dzp_ref[...]
    dz_scr[:, 512:1536] = durx_ref[...]
    dz_scr[:, 1536:2560] = durg_ref[...]
    dz_scr[:, 2560:3584] = dgp_ref[...]
    dz_scr[:, 3584:4608] = dgr_ref[...]


def _dw_in(h1, dzp, durx, durg, dgp, dgr, ts, tasks=()):
    s = h1.shape[0]
    ts = min(2 * ts, s)
    nt = s // ts
    half = D_MODEL // 2

    def body(h1_ref, dzp_ref, durx_ref, durg_ref, dgp_ref, dgr_ref, theirs_ref, mine_ref, got_ref, dz_scr, sems):
        o, i = pl.program_id(0), pl.program_id(1)

        @pl.when((o == 0) & (i == 0))
        def _():
            theirs_ref[...] = jnp.zeros_like(theirs_ref)
            mine_ref[...] = jnp.zeros_like(mine_ref)

        _assemble_dz(dz_scr, dzp_ref, durx_ref, durg_ref, dgp_ref, dgr_ref)
        for out_ref, which in ((theirs_ref, 0), (mine_ref, 1)):
            @pl.when(o == which)
            def _():
                for j in range(N_CHIPS):
                    out_ref[j] += _dot_tn(h1_ref[...], dz_scr[:, j * IN_SHARD:(j + 1) * IN_SHARD])

        x, y, c, _, _ = _place()
        send = _remote(theirs_ref, got_ref, sems.at[0], sems.at[1], (x, y, 1 - c))
        pl.when((o == 1) & (i == 0))(send.start)
        pl.when((o == 1) & (i == nt - 1))(send.wait)

    def h1_cols(o, i):
        c = lax.axis_index("c")
        return i, jnp.where(o == 0, 1 - c, c)

    rows = lambda width: pl.BlockSpec((ts, width), lambda o, i: (i, 0))
    shape = (N_CHIPS, half, IN_SHARD)
    whole = pl.BlockSpec(shape, lambda o, i: (0, 0, 0))
    return _call(
        body, "dw_in", (2, nt), [pl.BlockSpec((ts, half), h1_cols), rows(POOL_WIDTH)] + [rows(D_MODEL)] * 4,
        [whole, whole, pl.BlockSpec(memory_space=pltpu.HBM)], [_sds(shape)] * 3,
        [pltpu.VMEM((ts, IN_TOTAL), BF16), pltpu.SemaphoreType.DMA((2,))], (h1, dzp, durx, durg, dgp, dgr), tasks)


def _bwd_in(dzp, durx, durg, dgp, dgr, x, dx1, g_pre, w_in, ts, tasks=()):
    s = x.shape[0]

    def body(dzp_ref, durx_ref, durg_ref, dgp_ref, dgr_ref, x_ref, dx1_ref, g_ref, win_ref, gx_ref, dg_ref, dz_scr):
        @pl.when(pl.program_id(0) == 0)
        def _():
            dg_ref[...] = jnp.zeros_like(dg_ref)

        _assemble_dz(dz_scr, dzp_ref, durx_ref, durg_ref, dgp_ref, dgr_ref)
        _, xh, r = _rms_fwd(x_ref[...], g_ref[...])
        dh1 = _dot_nt(dz_scr[:, 0:IN_SHARD], win_ref[0])
        for j in range(1, N_CHIPS):
            dh1 = dh1 + _dot_nt(dz_scr[:, j * IN_SHARD:(j + 1) * IN_SHARD], win_ref[j])
        dx, dg = _rms_bwd(xh, r, g_ref[...], dh1)
        dg_ref[...] += dg
        gx_ref[...] = dx1_ref[...] + dx

    row = _rows(ts, D_MODEL)
    return _call(
        body, "bwd_in", (s // ts,),
        [_rows(ts, POOL_WIDTH)] + [row] * 6 + [_resident((1, D_MODEL)), _resident(w_in.shape)],
        [row, _acc((1, D_MODEL))], [_sds((s, D_MODEL)), _sds((1, D_MODEL))],
        [pltpu.VMEM((ts, IN_TOTAL), BF16)], (dzp, durx, durg, dgp, dgr, x, dx1, g_pre, w_in), tasks)


def _place():
    x, y, c = lax.axis_index("x"), lax.axis_index("y"), lax.axis_index("c")
    others = [(1 - x, y), (x, 1 - y), (1 - x, 1 - y)]
    return x, y, c, 2 * x + y, others


def _remote(src, dst, send_sem, recv_sem, to):
    return pltpu.make_async_remote_copy(src_ref=src, dst_ref=dst, send_sem=send_sem, recv_sem=recv_sem,
                                        device_id=to, device_id_type=MESH)


def _own_slots(ws, dtypes, name, tasks=()):
    n = len(ws)
    hbm = pl.BlockSpec(memory_space=pltpu.HBM)

    def body(*refs):
        srcs, outs, f32_bufs, cast_bufs, sems = refs[:n], refs[n:2 * n], refs[2 * n:3 * n], refs[3 * n:4 * n], refs[4 * n]
        me = _place()[3]
        loads = [pltpu.make_async_copy(srcs[k], f32_bufs[k], sems.at[k, 0]) for k in range(n)]
        stores = [pltpu.make_async_copy(cast_bufs[k], outs[k].at[me], sems.at[k, 1]) for k in range(n)]
        for cp in loads:
            cp.start()
        for k in range(n):
            loads[k].wait()
            cast_bufs[k][...] = f32_bufs[k][...].astype(dtypes[k])
            stores[k].start()
        for cp in stores:
            cp.wait()

    return _call(
        body, name, (), [hbm] * n, [hbm] * n, [_sds((N_CHIPS,) + w.shape, dt) for w, dt in zip(ws, dtypes)],
        [pltpu.VMEM(w.shape, F32) for w in ws] + [pltpu.VMEM(w.shape, dt) for w, dt in zip(ws, dtypes)]
        + [pltpu.SemaphoreType.DMA((n, 2))],
        [pltpu.with_memory_space_constraint(w, pltpu.HBM) for w in ws], tasks)


def _run(tasks, name):
    if isinstance(tasks, _Task):
        return _call(None, name, (), [], [], [], [], (), (tasks,))[1][0]
    return _call(None, name, (), [], [], [], [], (), tuple(tasks))[1]


def _gather_task(bufs, relay_steps=(0, 0)):
    n = len(bufs)
    NBR_X, NBR_Y, QUARTER_VIA_Y, QUARTER_VIA_X, SIB_X, SIB_Y, SIB_DIAG = range(7)

    def parts(out):
        x, y, c, me, _ = _place()
        ah = out.shape[1] // 2
        q = ah // 2 if (ah // 2) % 16 == 0 else ah
        return c * ah, ah, q

    def copy(out, w, k, chip, row0, rows, to, sems):
        slot = out.at[chip, pl.ds(row0, rows)]
        return _remote(slot, slot, sems[0].at[w, k], sems[1].at[w, k], to)

    def plan(out, w, sems):
        x, y, c, me, _ = _place()
        row0, ah, q = parts(out)
        xn, yn, dg = 2 * (1 - x) + y, 2 * x + (1 - y), 2 * (1 - x) + (1 - y)
        to_x, to_y, sib = (1 - x, y, c), (x, 1 - y, c), (x, y, 1 - c)
        other = (1 - c) * ah
        cp = functools.partial(copy, out, w, sems=sems)
        sends = {NBR_X: cp(NBR_X, me, row0, ah, to_x), NBR_Y: cp(NBR_Y, me, row0, ah, to_y),
                 QUARTER_VIA_Y: cp(QUARTER_VIA_Y, xn, row0, q, to_y), SIB_X: cp(SIB_X, xn, row0, ah, sib),
                 SIB_Y: cp(SIB_Y, yn, row0, ah, sib), SIB_DIAG: cp(SIB_DIAG, dg, row0, ah, sib)}
        lands = {NBR_X: cp(NBR_X, xn, row0, ah, to_x), NBR_Y: cp(NBR_Y, yn, row0, ah, to_y),
                 QUARTER_VIA_Y: cp(QUARTER_VIA_Y, dg, row0, q, to_y), SIB_X: cp(SIB_X, xn, other, ah, sib),
                 SIB_Y: cp(SIB_Y, yn, other, ah, sib), SIB_DIAG: cp(SIB_DIAG, dg, other, ah, sib)}
        if q < ah:
            sends[QUARTER_VIA_X] = cp(QUARTER_VIA_X, yn, row0 + q, ah - q, to_x)
            lands[QUARTER_VIA_X] = cp(QUARTER_VIA_X, dg, row0 + q, ah - q, to_x)
        return sends, lands

    def start(ins, outs, sems):
        for w, out in enumerate(outs):
            sends, _ = plan(out, w, sems)
            sends[NBR_X].start()
            sends[NBR_Y].start()

    def pass_neighbours(ins, outs, sems):
        for w, out in enumerate(outs):
            sends, lands = plan(out, w, sems)
            lands[NBR_X].wait_recv()
            sends[QUARTER_VIA_Y].start()
            sends[SIB_X].start()
            lands[NBR_Y].wait_recv()
            if QUARTER_VIA_X in sends:
                sends[QUARTER_VIA_X].start()
            sends[SIB_Y].start()

    def pass_diagonal(ins, outs, sems):
        for w, out in enumerate(outs):
            sends, lands = plan(out, w, sems)
            lands[QUARTER_VIA_Y].wait_recv()
            if QUARTER_VIA_X in lands:
                lands[QUARTER_VIA_X].wait_recv()
            sends[SIB_DIAG].start()

    def finish(ins, outs, sems):
        for w, out in enumerate(outs):
            sends, lands = plan(out, w, sems)
            for k in (SIB_X, SIB_Y, SIB_DIAG):
                lands[k].wait_recv()
        for w, out in enumerate(outs):
            sends, _ = plan(out, w, sems)
            for cp in sends.values():
                cp.wait_send()

    return _Task(bufs, [_sds(b.shape, b.dtype) for b in bufs], {i: i for i in range(n)},
                 [pltpu.SemaphoreType.DMA((n, 7)), pltpu.SemaphoreType.DMA((n, 7))], start, finish,
                 [(pass_neighbours, relay_steps[0]), (pass_diagonal, relay_steps[1])])


def _halves_task(grads):
    n = len(grads)

    def copy(src, out, w, sems):
        x, y, c, _, _ = _place()
        ah = out.shape[1]
        return _remote(src.at[:, pl.ds((1 - c) * ah, ah)], out, sems[0].at[w], sems[1].at[w], (x, y, 1 - c))

    def start(ins, outs, sems):
        for w, (src, out) in enumerate(zip(ins, outs)):
            copy(src, out, w, sems).start()

    def finish(ins, outs, sems):
        for w, (src, out) in enumerate(zip(ins, outs)):
            copy(src, out, w, sems).wait()

    return _Task(grads, [_sds((g.shape[0], g.shape[1] // 2, g.shape[2]), g.dtype) for g in grads], {},
                 [pltpu.SemaphoreType.DMA((n,)), pltpu.SemaphoreType.DMA((n,))], start, finish)


def _exchange_task(sends, accs):
    n = len(accs)
    given = [s for s in sends if s is not None]

    def copies(ins, outs, sems):
        send_refs = iter(ins[:len(given)])
        srcs = [next(send_refs) if s is not None else None for s in sends]
        x, y, c, me, others = _place()
        for w, out in enumerate(outs):
            for j, (ox, oy) in enumerate(others):
                src = out.at[me] if srcs[w] is None else srcs[w].at[2 * ox + oy]
                yield _remote(src, out.at[me], sems[0].at[w, j], sems[1].at[w, j], (ox, oy, c))

    def start(ins, outs, sems):
        for cp in copies(ins, outs, sems):
            cp.start()

    def finish(ins, outs, sems):
        x, y, c, _, others = _place()
        for w, out in enumerate(outs):
            for j, (ox, oy) in enumerate(others):
                slot = out.at[2 * ox + oy]
                _remote(slot, slot, sems[0].at[w, j], sems[1].at[w, j], (ox, oy, c)).wait_recv()
        for cp in copies(ins, outs, sems):
            cp.wait_send()

    return _Task(given + list(accs), [_sds(a.shape, a.dtype) for a in accs], {len(given) + i: i for i in range(n)},
                 [pltpu.SemaphoreType.DMA((n, 3)), pltpu.SemaphoreType.DMA((n, 3))], start, finish)


def _swap_task(arrays):
    n = len(arrays)

    def copy(src, out, w, sems):
        x, y, c, _, _ = _place()
        return _remote(src, out, sems[0].at[w], sems[1].at[w], (x, y, 1 - c))

    def start(ins, outs, sems):
        for w, (src, out) in enumerate(zip(ins, outs)):
            copy(src, out, w, sems).start()

    def finish(ins, outs, sems):
        for w, (src, out) in enumerate(zip(ins, outs)):
            copy(src, out, w, sems).wait()

    return _Task(arrays, [_sds(a.shape, a.dtype) for a in arrays], {},
                 [pltpu.SemaphoreType.DMA((n,)), pltpu.SemaphoreType.DMA((n,))], start, finish)


def _all_devices_task(arrays):
    n = len(arrays)
    flips = [(dx, dy, dc) for dx in (0, 1) for dy in (0, 1) for dc in (0, 1)][1:]

    def peers():
        x, y, c, _, _ = _place()
        flip = lambda v, d: 1 - v if d else v
        return 4 * x + 2 * y + c, [(flip(x, dx), flip(y, dy), flip(c, dc)) for dx, dy, dc in flips]

    def start(ins, outs, sems):
        me, others = peers()
        for w, (src, out) in enumerate(zip(ins, outs)):
            pltpu.make_async_copy(src, out.at[me], sems[2].at[w]).start()
            for k, peer in enumerate(others):
                _remote(src, out.at[me], sems[0].at[w, k], sems[1].at[w, k], peer).start()

    def finish(ins, outs, sems):
        me, others = peers()
        for w, (src, out) in enumerate(zip(ins, outs)):
            for k, (px, py, pc) in enumerate(others):
                slot = out.at[4 * px + 2 * py + pc]
                _remote(slot, slot, sems[0].at[w, k], sems[1].at[w, k], (px, py, pc)).wait_recv()
            for k, peer in enumerate(others):
                _remote(src, out.at[me], sems[0].at[w, k], sems[1].at[w, k], peer).wait_send()
            pltpu.make_async_copy(src, out.at[me], sems[2].at[w]).wait()

    return _Task(arrays, [_sds((8,) + a.shape, a.dtype) for a in arrays], {},
                 [pltpu.SemaphoreType.DMA((n, 7)), pltpu.SemaphoreType.DMA((n, 7)), pltpu.SemaphoreType.DMA((n,))],
                 start, finish)


def _share_task(shares):
    n = len(shares)

    def copy(out, w, sems, slot):
        x, y, c, _, _ = _place()
        return _remote(out.at[slot], out.at[slot], sems[0].at[w], sems[1].at[w], (x, y, 1 - c))

    def start(ins, outs, sems):
        c = _place()[2]
        for w, out in enumerate(outs):
            copy(out, w, sems, c).start()

    def finish(ins, outs, sems):
        c = _place()[2]
        for w, out in enumerate(outs):
            copy(out, w, sems, 1 - c).wait_recv()
        for w, out in enumerate(outs):
            copy(out, w, sems, c).wait_send()

    return _Task(shares, [_sds(s.shape, s.dtype) for s in shares], {i: i for i in range(n)},
                 [pltpu.SemaphoreType.DMA((n,)), pltpu.SemaphoreType.DMA((n,))], start, finish)


TILE_BYTES = 2 * 1024 * 1024
PARTIAL_TILE_BYTES = 1024 * 1024


def _in_hbm(t):
    return pltpu.with_memory_space_constraint(t, pltpu.HBM)


def _row_tile(rows, cols, limit=TILE_BYTES):
    best = 8
    for tr in range(8, rows + 1, 8):
        if rows % tr == 0 and tr * cols * 4 <= limit:
            best = tr
    assert rows % best == 0, (rows, cols)
    return best


def _chip_partial(g, got, place, wire_dtype):
    ns, ah, b = got.shape
    sharded = ns == N_CHIPS
    tr = _row_tile(ah, b, PARTIAL_TILE_BYTES)
    nb = ah // tr
    first = 0 if g.shape[1] == ah else nb

    def body(place_ref, *refs):
        g_refs, got_refs, outs = refs[:ns], refs[ns:2 * ns], refs[2 * ns:]
        parts = [g_refs[k][0] + got_refs[k][0] for k in range(ns)]
        own = parts[0]
        if sharded:
            for k in range(ns):
                outs[0][k] = parts[k].astype(wire_dtype)
                if k:
                    own = jnp.where(place_ref[0] == k, parts[k], own)
        outs[-1][0] = own.astype(wire_dtype)

    blk = (1, tr, b)
    in_specs = ([pl.BlockSpec(blk, lambda i, s, k=k: (k, s[1] * first + i, 0)) for k in range(ns)]
                + [pl.BlockSpec(blk, lambda i, s, k=k: (k, i, 0)) for k in range(ns)])
    acc_spec = pl.BlockSpec(blk, lambda i, s: (s[0], i, 0))
    acc_shape = _sds((N_CHIPS, ah, b), wire_dtype)
    out = pl.pallas_call(
        body, name="grad_chip_partial",
        grid_spec=pltpu.PrefetchScalarGridSpec(
            num_scalar_prefetch=1, grid=(nb,), in_specs=in_specs,
            out_specs=[pl.BlockSpec((ns, tr, b), lambda i, s: (0, i, 0)), acc_spec] if sharded else [acc_spec]),
        out_shape=[acc_shape, acc_shape] if sharded else [acc_shape],
        compiler_params=pltpu.CompilerParams(dimension_semantics=("arbitrary",), vmem_limit_bytes=VMEM_LIMIT),
    )(place, *([g] * ns), *([got] * ns))
    return (out[0], out[1]) if sharded else (None, out[0])


def _chip_sum(acc, place):
    _, ah, b = acc.shape
    tr = _row_tile(ah, b)

    def body(place_ref, p_ref, out_ref):
        total = p_ref[0].astype(F32) + p_ref[1].astype(F32)
        total = total + p_ref[2].astype(F32)
        out_ref[0] = total + p_ref[3].astype(F32)

    return pl.pallas_call(
        body, name="grad_chip_sum",
        grid_spec=pltpu.PrefetchScalarGridSpec(
            num_scalar_prefetch=1, grid=(ah // tr,),
            in_specs=[pl.BlockSpec((N_CHIPS, tr, b), lambda i, s: (0, i, 0))],
            out_specs=pl.BlockSpec((1, tr, b), lambda i, s: (s[1], i, 0))),
        out_shape=_sds((2, ah, b)),
        compiler_params=pltpu.CompilerParams(dimension_semantics=("arbitrary",)),
    )(place, _in_hbm(acc))


def _adam_math(w, g, m, v):
    nm = ADAM_B1 * m + (1.0 - ADAM_B1) * g
    nv = ADAM_B2 * v + (1.0 - ADAM_B2) * (g * g)
    m_hat = nm / (1.0 - ADAM_B1 ** ADAM_STEP)
    v_hat = nv / (1.0 - ADAM_B2 ** ADAM_STEP)
    return -ADAM_LR * (m_hat / (jnp.sqrt(v_hat) + ADAM_EPS) + ADAM_WD * w), nm, nv


def _adamw(w, g, m, v):
    a, b = w.shape
    tr = _row_tile(a, b)

    def body(w_ref, g_ref, m_ref, v_ref, g_out, d_ref, nm_ref, nv_ref):
        g_out[...] = g_ref[...]
        d_ref[...], nm_ref[...], nv_ref[...] = _adam_math(w_ref[...], g_ref[...], m_ref[...], v_ref[...])

    blk = pl.BlockSpec((tr, b), lambda i: (i, 0))
    return pl.pallas_call(
        body, name="adamw", grid=(a // tr,),
        in_specs=[blk] * 4, out_specs=[blk] * 4, out_shape=[_sds((a, b))] * 4,
        compiler_params=pltpu.CompilerParams(dimension_semantics=("arbitrary",)),
    )(w, g, m, v)


def _adamw_sum(w, m, v, acc, got, place):
    a, b = w.shape
    ah = a // 2
    tr = _row_tile(ah, b)
    nb = ah // tr

    def body(place_ref, w_ref, m_ref, v_ref, acc_ref, got_ref, g_out, d_ref, nm_ref, nv_ref):
        mine = (pl.program_id(0) // nb) == place_ref[1]
        part = lambda k: jnp.where(mine, acc_ref[k], got_ref[k]).astype(F32)
        g = part(0) + part(1)
        g = g + part(2)
        g = g + part(3)
        g_out[...] = g
        d_ref[...], nm_ref[...], nv_ref[...] = _adam_math(w_ref[...], g, m_ref[...], v_ref[...])

    blk = pl.BlockSpec((tr, b), lambda i, s: (i, 0))
    mine_spec = pl.BlockSpec((N_CHIPS, tr, b), lambda i, s: (0, jnp.where(i // nb == s[1], i % nb, 0), 0))
    got_spec = pl.BlockSpec((N_CHIPS, tr, b), lambda i, s: (0, jnp.where(i // nb == s[1], 0, i % nb), 0))
    return pl.pallas_call(
        body, name="adamw_sum",
        grid_spec=pltpu.PrefetchScalarGridSpec(
            num_scalar_prefetch=1, grid=(a // tr,), in_specs=[blk] * 3 + [mine_spec, got_spec], out_specs=[blk] * 4),
        out_shape=[_sds((a, b))] * 4,
        compiler_params=pltpu.CompilerParams(dimension_semantics=("arbitrary",), vmem_limit_bytes=VMEM_LIMIT),
    )(place, w, m, v, _in_hbm(acc), _in_hbm(got))


def _adamw_pieces(g, pieces, name):
    n = len(pieces)

    def body(g_ref, *refs):
        def grad(rows, cols):
            if len(g_ref.shape) == 2:
                return g_ref[rows, cols]
            total = g_ref[0, rows, cols]
            for k in range(1, g_ref.shape[0]):
                total = total + g_ref[k, rows, cols]
            return total

        ins, outs = refs[:3 * n], refs[3 * n:]
        for i, piece in enumerate(pieces):
            w_ref, m_ref, v_ref = ins[3 * i:3 * i + 3]
            o_g, o_d, o_m, o_v = outs[4 * i:4 * i + 4]
            if len(piece) == 5:
                g_v = grad(piece[3], piece[4])
                o_g[...] = g_v
                o_d[...], o_m[...], o_v[...] = _adam_math(w_ref[...], g_v, m_ref[...], v_ref[...])
            else:
                for r in range(w_ref.shape[1] // SMALL_COLS):
                    lanes = slice(r * SMALL_COLS, (r + 1) * SMALL_COLS)
                    g_v = grad(slice(piece[3] + r, piece[3] + r + 1), slice(None))
                    o_g[:, lanes] = g_v
                    o_d[:, lanes], o_m[:, lanes], o_v[:, lanes] = _adam_math(w_ref[:, lanes], g_v, m_ref[:, lanes],
                                                                            v_ref[:, lanes])

    operands = [t for piece in pieces for t in piece[:3]]
    out = pl.pallas_call(
        body, name=name,
        out_shape=[_sds(piece[0].shape) for piece in pieces for _ in range(4)],
    )(g, *operands)
    return [tuple(out[4 * i:4 * i + 4]) for i in range(n)]


TINY_ROWS, TINY_COLS = 16, 768
SMALL_COLS = 128
SMALL_ROWS = 624


def _pack_tiny(conv_w, b_gates, fcw):
    ns = conv_w.shape[0]
    pad = lambda t: jnp.pad(t, ((0, 0), (0, 0), (0, TINY_COLS - t.shape[2])))
    z = lambda rows: jnp.zeros((ns, rows, TINY_COLS), F32)
    return jnp.concatenate([pad(conv_w), pad(b_gates), z(2), fcw, z(5)], axis=1)


def _unpack_tiny(t):
    return t[:, 0:4, 0:256], t[:, 4:6, 0:256], t[:, 8:11, :]


def _cols_to_shards(t, n):
    return t.reshape(t.shape[0], N_CHIPS, n).transpose(1, 0, 2)


def _shards_to_cols(t):
    return t.transpose(1, 0, 2).reshape(t.shape[1], -1)


_VECTORS = ("g_mix_post", "conv_b", "lru_lambda", "g_ffn_pre", "g_ffn_post", "g_ple_gate", "g_ple_post", "pool_scale",
            "ffn_conv_b")
_VECTOR_LEN = {"pool_scale": POOL_WIDTH, "ffn_conv_b": D_FF}
POOL_W_ROWS = POOL_GROUPS * POOL_GROUP_DIM


def _vector_rows():
    rows, row = {}, POOL_W_ROWS
    for k in _VECTORS:
        rows[k] = row
        row += max(8, _VECTOR_LEN.get(k, D_MODEL) // SMALL_COLS)
    return rows, row


def _pack_small(grads, loss):
    tiles = lambda t: jnp.pad(t, ((0, -t.shape[0] % 8), (0, 0)))
    parts = [grads["pool_w"].reshape(POOL_W_ROWS, SMALL_COLS)] + [tiles(grads[k].reshape(-1, SMALL_COLS)) for k in _VECTORS]
    parts.append(tiles(loss))
    used = sum(t.shape[0] for t in parts)
    return jnp.concatenate(parts + [jnp.zeros((SMALL_ROWS - used, SMALL_COLS), F32)], axis=0)


def _gates_block_diag(w):
    w4 = w.reshape(2, GATE_BLOCKS, 4, RNN_HEAD_DIM, RNN_HEAD_DIM)
    eye = jnp.eye(4, dtype=w.dtype)
    return jnp.einsum("gqhij,hk->gqhikj", w4, eye).reshape(2, GATE_BLOCKS, GATE_BLOCK, GATE_BLOCK)


def _gates_from_block_diag(dw):
    d6 = dw.reshape(2, GATE_BLOCKS, 4, RNN_HEAD_DIM, 4, RNN_HEAD_DIM)
    blocks = [d6[:, :, hh, :, hh, :] for hh in range(4)]
    return jnp.stack(blocks, axis=2).reshape(2, RNN_HEADS, RNN_HEAD_DIM, RNN_HEAD_DIM)


ROW_TILE = 256
DW_TILES = 4

_SHARDED = ("w_in", "w_pool_out", "w_rg_out", "w_o", "w_up", "w_down", "w_ple_gate", "w_ple_proj")
_WEIGHTS = ("g_mix_pre", "g_mix_post", "w_in", "pool_w", "pool_scale", "w_pool_out", "conv_w", "conv_b", "w_rg_gates",
            "b_rg_gates", "lru_lambda", "w_rg_out", "w_o", "g_ffn_pre", "g_ffn_post", "w_up", "ffn_conv_w", "ffn_conv_b",
            "w_down", "g_ple_gate", "w_ple_gate", "w_ple_proj", "g_ple_post")


def _wire_dtype(g):
    return BF16 if g.shape[1] >= 64 and g.shape[2] > SMALL_COLS else F32


def _partials(grads, got, place):
    parts = [_chip_partial(g, r, place, _wire_dtype(g)) for g, r in zip(grads, got)]
    return [send for send, _ in parts], [acc for _, acc in parts]


def _whole(both):
    return [b.reshape(2 * b.shape[1], b.shape[2]) for b in both]


def _step(x, p, tgt, rep, place, ts):
    vec = lambda k: rep[k].reshape(1, -1)
    tall = min(2 * ts, x.shape[0])
    pool_w = rep["pool_w"].astype(BF16)
    wg = _gates_block_diag(rep["w_rg_gates"]).astype(BF16)
    sq = lambda t: t.reshape(D_MODEL, D_MODEL)
    by4 = lambda t: t.reshape(N_CHIPS, -1, D_MODEL)

    first, ride1, ride2, ride3 = (("w_in", "w_pool_out", "tiny"), ("w_rg_out", "w_o", "w_down"), ("w_up",),
                                  ("w_ple_gate", "w_ple_proj"))
    later = ride1 + ride2 + ride3
    tiny = _pack_tiny(rep["conv_w"][None], rep["b_rg_gates"][None], rep["ffn_conv_w"][None])[0]
    own_first, _ = _own_slots([rep["w_in"], rep["w_pool_out"], tiny], [BF16, BF16, F32], "own_slots_first")
    own_later, (got,) = _own_slots([rep[k] for k in later], [BF16] * len(later), "own_slots_gather_first",
                                   [_gather_task(own_first)])
    own = dict(zip(later, own_later))
    full = dict(zip(first, got))
    conv_w, b_gates, fcw = [_shards_to_cols(t) for t in _unpack_tiny(full["tiny"])]

    (urx, urg, gp, gr, d, ypool, h1), (got,) = _fwd_in_pool(
        x, vec("g_mix_pre"), full["w_in"], pool_w, vec("pool_scale"), full["w_pool_out"], ts,
        [_gather_task([own[k] for k in ride1], relay_steps=(6, 2))])
    full.update(zip(ride1, got))
    w_rg_out, w_o, w_down = sq(full["w_rg_out"]), sq(full["w_o"]), full["w_down"].reshape(D_FF, D_MODEL)
    (xc, r, ig, h, yrnn, mo, x1, glr, ggr, sp, sr), (got,) = _fwd_rnn_merge(
        urx, urg, gp, gr, ypool, x, conv_w, vec("conv_b"), wg, b_gates, vec("lru_lambda"), w_rg_out, w_o,
        vec("g_mix_post"), ts, [_gather_task([own[k] for k in ride2], relay_steps=(7, 3))])
    full.update(zip(ride2, got))
    (up, gl, gg, h2, dn, x2), (got,) = _fwd_ffn(x1, vec("g_ffn_pre"), full["w_up"], fcw, vec("ffn_conv_b"), w_down,
                                               vec("g_ffn_post"), ts, [_gather_task([own[k] for k in ride3], relay_steps=(10, 5))])
    full.update(zip(ride3, got))
    dx2, loss, d_w_gate, d_w_proj, d_g_ple_gate, d_g_ple_post = _ple_loss(
        x2, p, tgt, vec("g_ple_gate"), sq(full["w_ple_gate"]), full["w_ple_proj"], vec("g_ple_post"), tall)
    dup, d_w_down, d_fcw, d_fcb, d_g_ffn_post = _bwd_ffn_down(dx2, dn, up, gl, gg, fcw, w_down, vec("g_ffn_post"), ts)

    names1, grads1 = ("w_ple_gate", "w_ple_proj", "w_down"), [by4(d_w_gate), d_w_proj, by4(d_w_down)]
    (dx1, d_g_ffn_pre), (got1,) = _bwd_ffn_up(dup, x1, dx2, vec("g_ffn_pre"), full["w_up"], tall, [_halves_task(grads1)])
    (d_w_up,), (accs1,) = _dw_up(h2, dup, ts, [_exchange_task(*_partials(grads1, got1, place))])
    (dgp, dgr, dyp, dyr, d_w_o, d_g_mix_post), (got2, theirs1) = _bwd_merge(
        dx1, mo, sp, sr, ypool, yrnn, vec("g_mix_post"), w_o, tall, [_halves_task([d_w_up]), _swap_task(accs1)])
    (durx, durg, d_w_rg_out, d_wg, d_conv_w, d_conv_b, d_b_gates, d_lam), (accs2,) = _bwd_rnn(
        dyr, urx, glr, ggr, xc, r, ig, h, conv_w, wg, vec("lru_lambda"), w_rg_out, ts,
        [_exchange_task(*_partials([d_w_up], got2, place))])
    names3 = ("w_o", "w_rg_out", "tiny", "w_rg_gates")
    grads3 = [by4(d_w_o), by4(d_w_rg_out),
              _pack_tiny(_cols_to_shards(d_conv_w, 256), _cols_to_shards(d_b_gates, 256), _cols_to_shards(d_fcw, 768)),
              _gates_from_block_diag(d_wg).reshape(1, 2 * RNN_HEADS * RNN_HEAD_DIM, RNN_HEAD_DIM)]
    (dzp, d_w_pool_out, d_pool_w, d_pool_scale), (got3, theirs2) = _bwd_pool(
        dyp, d, pool_w, vec("pool_scale"), full["w_pool_out"], tall, [_halves_task(grads3), _swap_task(accs2)])
    replicated = {"g_mix_post": d_g_mix_post, "conv_b": d_conv_b, "lru_lambda": d_lam, "g_ffn_pre": d_g_ffn_pre,
                  "g_ffn_post": d_g_ffn_post, "g_ple_gate": d_g_ple_gate, "g_ple_post": d_g_ple_post,
                  "pool_scale": d_pool_scale, "ffn_conv_b": d_fcb, "pool_w": d_pool_w}
    names4 = ("w_in", "w_pool_out", "small")
    small4 = [d_w_pool_out, _pack_small(replicated, loss)[None]]
    (_, d_w_in_half, got_w_in), (accs3, got_small4) = _dw_in(
        h1, dzp, durx, durg, dgp, dgr, ts, [_exchange_task(*_partials(grads3, got3, place)), _halves_task(small4)])
    grads4, got4 = [d_w_in_half] + small4, [got_w_in] + got_small4
    (grad_x, d_g_mix_pre), (accs4, theirs3, both3) = _bwd_in(
        dzp, durx, durg, dgp, dgr, x, dx1, vec("g_mix_pre"), full["w_in"], tall,
        [_exchange_task(*_partials(grads4, got4, place)), _swap_task(accs3[:2]),
         _share_task([_chip_sum(acc, place) for acc in accs3[2:]])])
    theirs4, both4, (g_mix_pre_parts,) = _run(
        [_swap_task(accs4[:2]), _share_task([_chip_sum(acc, place) for acc in accs4[2:]]),
         _all_devices_task([d_g_mix_pre.reshape(SUBLANES, SMALL_COLS)])], "grad_sibling_share")
    mine = accs1 + accs2 + accs3[:2] + accs4[:2]
    partials = dict(zip(names1 + ("w_up",) + names3[:2] + names4[:2], zip(mine, theirs1 + theirs2 + theirs3 + theirs4)))
    return grad_x, partials, dict(zip(names3[2:] + names4[2:], _whole(both3) + _whole(both4))), g_mix_pre_parts


def kernel(x, p, g_mix_pre, g_mix_post, w_in, pool_w, pool_scale, w_pool_out, conv_w, conv_b, w_rg_gates, b_rg_gates, lru_lambda, w_rg_out, w_o, g_ffn_pre, g_ffn_post, w_up, ffn_conv_w, ffn_conv_b, w_down, g_ple_gate, w_ple_gate, w_ple_proj, g_ple_post, loss_target, m_g_mix_pre, m_g_mix_post, m_w_in, m_pool_w, m_pool_scale, m_w_pool_out, m_conv_w, m_conv_b, m_w_rg_gates, m_b_rg_gates, m_lru_lambda, m_w_rg_out, m_w_o, m_g_ffn_pre, m_g_ffn_post, m_w_up, m_ffn_conv_w, m_ffn_conv_b, m_w_down, m_g_ple_gate, m_w_ple_gate, m_w_ple_proj, m_g_ple_post, v_g_mix_pre, v_g_mix_post, v_w_in, v_pool_w, v_pool_scale, v_w_pool_out, v_conv_w, v_conv_b, v_w_rg_gates, v_b_rg_gates, v_lru_lambda, v_w_rg_out, v_w_o, v_g_ffn_pre, v_g_ffn_post, v_w_up, v_ffn_conv_w, v_ffn_conv_b, v_w_down, v_g_ple_gate, v_w_ple_gate, v_w_ple_proj, v_g_ple_post):
    args = dict(locals())
    w = {k: args[k][0] for k in _WEIGHTS}
    m = {k: args["m_" + k][0] for k in _WEIGHTS}
    v = {k: args["v_" + k][0] for k in _WEIGHTS}
    place = jnp.stack([2 * lax.axis_index("x") + lax.axis_index("y"), lax.axis_index("c")]).astype(jnp.int32)
    grad_x, partials, reduced, g_mix_pre_parts = _step(x[0], p[0, 0], loss_target[0], w, place, ROW_TILE)

    gates_2d = (2 * RNN_HEADS * RNN_HEAD_DIM, RNN_HEAD_DIM)
    as2d = lambda k, shape: tuple(t[k].reshape(shape) for t in (w, m, v))
    done = {k: tuple(_adamw_sum(w[k], m[k], v[k], *partials[k], place)) for k in _SHARDED}
    gates_w, gates_m, gates_v = as2d("w_rg_gates", gates_2d)
    done["w_rg_gates"] = tuple(_adamw(gates_w, reduced["w_rg_gates"], gates_m, gates_v))
    tiny_names = ("conv_w", "b_rg_gates", "ffn_conv_w")
    tiny_at = ((slice(0, 4), slice(0, 256)), (slice(4, 6), slice(0, 256)), (slice(8, 11), slice(None)))
    done.update(zip(tiny_names, _adamw_pieces(
        reduced["tiny"], [(w[k], m[k], v[k]) + at for k, at in zip(tiny_names, tiny_at)], "adamw_tiny")))
    vector_rows, loss_row = _vector_rows()
    pieces = [as2d("pool_w", (POOL_W_ROWS, SMALL_COLS)) + (slice(0, POOL_W_ROWS), slice(None))]
    pieces += [as2d(k, (1, -1)) + (vector_rows[k],) for k in _VECTORS]
    done.update(zip(("pool_w",) + _VECTORS, _adamw_pieces(reduced["small"], pieces, "adamw_small")))
    done["g_mix_pre"] = _adamw_pieces(g_mix_pre_parts, [as2d("g_mix_pre", (1, -1)) + (0,)], "adamw_g_mix_pre")[0]

    result = [reduced["small"][loss_row, 0], grad_x[None]]
    for kind in range(4):
        result += [done[k][kind].reshape(args[k].shape) for k in _WEIGHTS]
    return tuple(result)
```

```python
import functools

import jax
import jax.numpy as jnp
from jax import lax
from jax.experimental import pallas as pl
from jax.experimental.pallas import tpu as pltpu

F32 = jnp.float32
BF16 = jnp.bfloat16

D_MODEL = 1024
POOL_WINDOWS = (2, 4, 8, 16)
POOL_GROUPS = 4
POOL_WIDTH = 512
POOL_GROUP_DIM = 128
RNN_HEADS = 16
RNN_HEAD_DIM = 64
GATE_BLOCK = 256
GATE_BLOCKS = D_MODEL // GATE_BLOCK
LRU_C = 8.0
D_FF = 3072
PLE_DIM = 256
RMS_EPS = 1e-6
IN_TOTAL = 4608
N_CHIPS = 4
IN_SHARD = IN_TOTAL // N_CHIPS
UP_SHARD = 2 * D_FF // N_CHIPS
Z_GROUPS = ((0, 512), (512, 1536), (1536, 2560), (2560, 3584), (3584, 4608))
POOL_HALO = 16
CONV_HALO = 8

ADAM_LR = 0.001
ADAM_B1 = 0.9
ADAM_B2 = 0.999
ADAM_EPS = 1e-08
ADAM_WD = 0.01
ADAM_STEP = 10

VMEM_LIMIT = 56 * 1024 * 1024
MESH = pl.DeviceIdType.MESH

_GELU_C = 0.7978845608028654
_GELU_A = 0.044715


def _dot(a, b):
    return jnp.dot(a.astype(BF16), b.astype(BF16), preferred_element_type=F32)


def _dot_nt(a, b):
    return lax.dot_general(a.astype(BF16), b.astype(BF16), (((1,), (1,)), ((), ())), preferred_element_type=F32)


def _dot_tn(a, b):
    return lax.dot_general(a.astype(BF16), b.astype(BF16), (((0,), (0,)), ((), ())), preferred_element_type=F32)


def _overlaps(group):
    a, b = Z_GROUPS[group]
    found = []
    for j in range(N_CHIPS):
        lo, hi = max(a, j * IN_SHARD), min(b, (j + 1) * IN_SHARD)
        if lo < hi:
            found.append((j, slice(lo - j * IN_SHARD, hi - j * IN_SHARD), slice(lo - a, hi - a)))
    return found


def _rms_fwd(x, g):
    r = lax.rsqrt(jnp.mean(x * x, axis=-1, keepdims=True) + RMS_EPS)
    xh = x * r
    return xh * g, xh, r


def _rms_bwd(xh, r, g, dy):
    dxh = dy * g
    dg = jnp.sum(dy * xh, axis=0, keepdims=True)
    dx = r * (dxh - xh * jnp.mean(dxh * xh, axis=-1, keepdims=True))
    return dx, dg


def _sigmoid(x):
    return 0.5 * jnp.tanh(0.5 * x) + 0.5


def _gelu(x):
    t = jnp.tanh(_GELU_C * (x + _GELU_A * x * x * x))
    return 0.5 * x * (1.0 + t), t


def _gelu_grad(x, t):
    return 0.5 * (1.0 + t) + 0.5 * x * (1.0 - t * t) * _GELU_C * (1.0 + 3.0 * _GELU_A * x * x)


def _softplus_neg(lam):
    nl = -lam
    return jnp.maximum(nl, 0.0) + jnp.log(1.0 + jnp.exp(-jnp.abs(nl)))


def _lru_coeffs(r, lam, first_row):
    c8 = LRU_C * _softplus_neg(lam)
    la = -(c8 * r)
    a = jnp.exp(la)
    m2 = jnp.tanh(-la) * (1.0 + a * a)
    mult = jnp.where(first_row, 1.0, jnp.sqrt(m2))
    return c8, a, m2, mult


SUBLANES = 8


def _scan_fwd(a, u, carry):
    n = a.shape[0]
    sub = lax.broadcasted_iota(jnp.int32, (n, 1), 0) % SUBLANES
    acc_a, acc_h = a, u
    for s in (1, 2, 4):
        m = sub >= s
        h_s = jnp.where(m, pltpu.roll(acc_h, s, 0), 0.0)
        a_s = jnp.where(m, pltpu.roll(acc_a, s, 0), 1.0)
        acc_h = acc_a * h_s + acc_h
        acc_a = acc_a * a_s
    out = []
    for g in range(n // SUBLANES):
        rows = slice(g * SUBLANES, (g + 1) * SUBLANES)
        out.append(acc_h[rows] + acc_a[rows] * carry)
        carry = out[-1][SUBLANES - 1:SUBLANES]
    return jnp.concatenate(out, axis=0)


def _scan_bwd(b, g, carry):
    n = b.shape[0]
    sub = lax.broadcasted_iota(jnp.int32, (n, 1), 0) % SUBLANES
    acc_b, acc_l = b, g
    for s in (1, 2, 4):
        m = sub < SUBLANES - s
        l_s = jnp.where(m, pltpu.roll(acc_l, n - s, 0), 0.0)
        b_s = jnp.where(m, pltpu.roll(acc_b, n - s, 0), 1.0)
        acc_l = acc_b * l_s + acc_l
        acc_b = acc_b * b_s
    out = [None] * (n // SUBLANES)
    for g in reversed(range(n // SUBLANES)):
        rows = slice(g * SUBLANES, (g + 1) * SUBLANES)
        out[g] = acc_l[rows] + acc_b[rows] * carry
        carry = out[g][0:1]
    return jnp.concatenate(out, axis=0)


def _shift_down(ext, k, halo):
    return pltpu.roll(ext, k, 0)[halo:] if k else ext[halo:]


def _shift_up(ext, k, ts):
    return pltpu.roll(ext, ext.shape[0] - k, 0)[:ts] if k else ext[:ts]


def _rows(ts, width, nt=None, col=0):
    if nt is None:
        return pl.BlockSpec((ts, width), lambda i: (i, col))
    return pl.BlockSpec((ts, width), lambda i: (nt - 1 - i, col))


def _resident(shape):
    zeros = (0,) * len(shape)
    return pl.BlockSpec(shape, lambda i: zeros, pipeline_mode=pl.Buffered(1))


def _acc(shape):
    zeros = (0,) * len(shape)
    return pl.BlockSpec(shape, lambda i: zeros)


def _params():
    return pltpu.CompilerParams(dimension_semantics=("arbitrary",), vmem_limit_bytes=VMEM_LIMIT)


def _sds(shape, dtype=F32):
    return jax.ShapeDtypeStruct(shape, dtype)


class _Task:
    def __init__(self, ins, out_shapes, aliases, sems, start, finish, relays=()):
        self.ins, self.out_shapes, self.aliases, self.sems = list(ins), list(out_shapes), dict(aliases), list(sems)
        self.start, self.relays, self.finish = start, list(relays), finish


def _call(body, name, grid, in_specs, out_specs, out_shape, scratch_shapes, args, tasks=()):
    n_in, n_out, n_scr = len(in_specs), len(out_specs), len(scratch_shapes)
    t_in = [len(t.ins) for t in tasks]
    t_out = [len(t.out_shapes) for t in tasks]
    t_sem = [len(t.sems) for t in tasks]
    steps = 1
    for g in grid:
        steps *= g

    def take(refs, pos, counts):
        groups = []
        for c in counts:
            groups.append(refs[pos:pos + c])
            pos += c
        return groups, pos

    def wrapped(*refs):
        (cin,), pos = take(refs, 0, [n_in])
        tin, pos = take(refs, pos, t_in)
        (cout,), pos = take(refs, pos, [n_out])
        tout, pos = take(refs, pos, t_out)
        (cscr,), pos = take(refs, pos, [n_scr])
        tsem, pos = take(refs, pos, t_sem)
        if not grid:
            for t, a, b, c in zip(tasks, tin, tout, tsem):
                t.start(a, b, c)
            if body is not None:
                body(*cin, *cout, *cscr)
            for t, a, b, c in zip(tasks, tin, tout, tsem):
                for relay, _ in t.relays:
                    relay(a, b, c)
            for t, a, b, c in zip(tasks, tin, tout, tsem):
                t.finish(a, b, c)
            return
        step = pl.program_id(0)
        for axis in range(1, len(grid)):
            step = step * grid[axis] + pl.program_id(axis)
        if tasks:
            @pl.when(step == 0)
            def _():
                for t, a, b, c in zip(tasks, tin, tout, tsem):
                    t.start(a, b, c)

        body(*cin, *cout, *cscr)
        for t, a, b, c in zip(tasks, tin, tout, tsem):
            for relay, before in t.relays:
                pl.when(step == max(steps - 1 - before, 0))(functools.partial(relay, a, b, c))

        if tasks:
            @pl.when(step == steps - 1)
            def _():
                for t, a, b, c in zip(tasks, tin, tout, tsem):
                    t.finish(a, b, c)

    aliases, in_pos, out_pos = {}, n_in, n_out
    for t, ni, no in zip(tasks, t_in, t_out):
        aliases.update({in_pos + a: out_pos + b for a, b in t.aliases.items()})
        in_pos, out_pos = in_pos + ni, out_pos + no
    any_spec = pl.BlockSpec(memory_space=pltpu.HBM)
    kwargs = dict(grid=grid, compiler_params=pltpu.CompilerParams(
        dimension_semantics=("arbitrary",) * len(grid), vmem_limit_bytes=VMEM_LIMIT)) if grid else dict(
        compiler_params=pltpu.CompilerParams(vmem_limit_bytes=VMEM_LIMIT))
    out = pl.pallas_call(
        wrapped, name=name,
        in_specs=list(in_specs) + [any_spec] * sum(t_in),
        out_specs=list(out_specs) + [any_spec] * sum(t_out),
        out_shape=list(out_shape) + [s for t in tasks for s in t.out_shapes],
        scratch_shapes=list(scratch_shapes) + [s for t in tasks for s in t.sems],
        input_output_aliases=aliases, **kwargs,
    )(*args, *[pltpu.with_memory_space_constraint(a, pltpu.HBM) for t in tasks for a in t.ins])
    task_outs, pos = take(list(out), n_out, t_out)
    return list(out[:n_out]), task_outs


def _fwd_in_pool(x, g_pre, w_in, pool_w, pool_scale, w_pool_out, ts, tasks=()):
    s = x.shape[0]

    def body(x_ref, g_ref, win_ref, pw_ref, ps_ref, wpo_ref,
             urx_ref, urg_ref, gp_ref, gr_ref, d_ref, yp_ref, h1_ref, halo_scr):
        i = pl.program_id(0)

        @pl.when(i == 0)
        def _():
            halo_scr[...] = jnp.zeros_like(halo_scr)

        h1, _, _ = _rms_fwd(x_ref[...], g_ref[...])
        h1 = h1.astype(BF16)
        h1_ref[...] = h1
        u = jnp.dot(h1, win_ref[0, :, 0:POOL_WIDTH], preferred_element_type=F32)
        for group, out_ref in zip(range(1, len(Z_GROUPS)), (urx_ref, urg_ref, gp_ref, gr_ref)):
            for j, shard_cols, group_cols in _overlaps(group):
                out_ref[:, group_cols] = jnp.dot(h1, win_ref[j, :, shard_cols], preferred_element_type=F32)
        ext = jnp.concatenate([halo_scr[...], u], axis=0)
        halo_scr[...] = u[ts - POOL_HALO:, :]
        t = i * ts + lax.broadcasted_iota(jnp.int32, (ts, 1), 0)
        y4 = []
        for g, w in enumerate(POOL_WINDOWS):
            lanes = slice(g * POOL_GROUP_DIM, (g + 1) * POOL_GROUP_DIM)
            acc = ext[:, lanes]
            sh = 1
            while sh < w:
                acc = acc + pltpu.roll(acc, sh, 0)
                sh *= 2
            inv = 1.0 / jnp.minimum(t + 1, w).astype(F32)
            dg = acc[POOL_HALO:, :] * inv - u[:, lanes]
            d_ref[:, lanes] = dg.astype(BF16)
            y4.append(_dot(dg, pw_ref[g]))
        ypre = jnp.concatenate(y4, axis=1) * ps_ref[...]
        ypre = ypre.astype(BF16)
        for j in range(N_CHIPS):
            yp_ref[:, j * 256:(j + 1) * 256] = jnp.dot(ypre, wpo_ref[j], preferred_element_type=F32)

    return _call(
        body, "fwd_in_pool", (s // ts,),
        [_rows(ts, D_MODEL), _resident((1, D_MODEL)), _resident(w_in.shape), _resident(pool_w.shape),
         _resident((1, POOL_WIDTH)), _resident(w_pool_out.shape)],
        [_rows(ts, D_MODEL)] * 4 + [_rows(ts, POOL_WIDTH), _rows(ts, D_MODEL), _rows(ts, D_MODEL)],
        [_sds((s, D_MODEL))] * 4 + [_sds((s, POOL_WIDTH), BF16), _sds((s, D_MODEL)), _sds((s, D_MODEL), BF16)],
        [pltpu.VMEM((POOL_HALO, POOL_WIDTH), F32)],
        (x, g_pre, w_in, pool_w, pool_scale, w_pool_out), tasks)


def _fwd_rnn_merge(urx, urg, gp, gr, ypool, x, conv_w, conv_b, wg, bg, lam, w_rg_out, w_o, g_post, ts, tasks=()):
    s = x.shape[0]

    def body(urx_ref, urg_ref, gp_ref, gr_ref, yp_ref, x_ref, cw_ref, cb_ref, wg_ref, bg_ref, lam_ref, wrg_ref, wo_ref,
             gpost_ref, xc_ref, r_ref, ig_ref, h_ref, yr_ref, mo_ref, x1_ref, gl_ref, gg_ref, sp_ref, sr_ref,
             halo_scr, carry_scr):
        i = pl.program_id(0)

        @pl.when(i == 0)
        def _():
            halo_scr[...] = jnp.zeros_like(halo_scr)
            carry_scr[...] = jnp.zeros_like(carry_scr)

        urx_v = urx_ref[...]
        ext = jnp.concatenate([halo_scr[...], urx_v], axis=0)
        halo_scr[...] = urx_v[ts - CONV_HALO:, :]
        cw = cw_ref[...]
        xc = (cb_ref[...] + cw[3:4] * urx_v + cw[2:3] * _shift_down(ext, 1, CONV_HALO)
              + cw[1:2] * _shift_down(ext, 2, CONV_HALO) + cw[0:1] * _shift_down(ext, 3, CONV_HALO))
        xc_ref[...] = xc
        xcb = xc.astype(BF16)
        lin = []
        for gate in range(2):
            parts = [jnp.dot(xcb[:, q * GATE_BLOCK:(q + 1) * GATE_BLOCK], wg_ref[gate, q], preferred_element_type=F32)
                     for q in range(GATE_BLOCKS)]
            lin.append(jnp.concatenate(parts, axis=1) + bg_ref[gate:gate + 1, :])
        r = _sigmoid(lin[0])
        ig = _sigmoid(lin[1])
        r_ref[...] = r
        ig_ref[...] = ig
        first_row = (i * ts + lax.broadcasted_iota(jnp.int32, (ts, 1), 0)) == 0
        _, a, _, mult = _lru_coeffs(r, lam_ref[...], first_row)
        h = _scan_fwd(a, mult * ig * xc, carry_scr[0:1, :])
        carry_scr[0:1, :] = h[ts - 1:ts, :]
        h_ref[...] = h
        urg_v = urg_ref[...]
        gl, t = _gelu(urg_v)
        gl_ref[...] = gl.astype(BF16)
        gg_ref[...] = _gelu_grad(urg_v, t).astype(BF16)
        yr = _dot(h * gl, wrg_ref[...])
        yr_ref[...] = yr
        sp = _sigmoid(gp_ref[...])
        sr = _sigmoid(gr_ref[...])
        sp_ref[...] = sp.astype(BF16)
        sr_ref[...] = sr.astype(BF16)
        merged = sp * yp_ref[...] + sr * yr
        mo = _dot(merged, wo_ref[...])
        mo_ref[...] = mo
        y, _, _ = _rms_fwd(mo, gpost_ref[...])
        x1_ref[...] = x_ref[...] + y

    row = _rows(ts, D_MODEL)
    return _call(
        body, "fwd_rnn_merge", (s // ts,),
        [row] * 6 + [_resident(conv_w.shape), _resident((1, D_MODEL)), _resident(wg.shape), _resident(bg.shape),
                     _resident((1, D_MODEL)), _resident(w_rg_out.shape), _resident(w_o.shape), _resident((1, D_MODEL))],
        [row] * 11, [_sds((s, D_MODEL))] * 7 + [_sds((s, D_MODEL), BF16)] * 4,
        [pltpu.VMEM((CONV_HALO, D_MODEL), F32), pltpu.VMEM((8, D_MODEL), F32)],
        (urx, urg, gp, gr, ypool, x, conv_w, conv_b, wg, bg, lam, w_rg_out, w_o, g_post), tasks)


def _fwd_ffn(x1, g_pre, w_up, fcw, fcb, w_down, g_post, ts, tasks=()):
    s = x1.shape[0]

    def body(x1_ref, g_ref, wup_ref, fcw_ref, fcb_ref, wd_ref, gpost_ref,
             up_ref, gl_ref, gg_ref, h2_ref, dn_ref, x2_ref, up_scr, halo_scr):
        i = pl.program_id(0)

        @pl.when(i == 0)
        def _():
            halo_scr[...] = jnp.zeros_like(halo_scr)

        x1_v = x1_ref[...]
        h2, _, _ = _rms_fwd(x1_v, g_ref[...])
        h2 = h2.astype(BF16)
        h2_ref[...] = h2
        for j in range(N_CHIPS):
            up_scr[:, j * UP_SHARD:(j + 1) * UP_SHARD] = jnp.dot(h2, wup_ref[j], preferred_element_type=F32)
        up_ref[...] = up_scr[...].astype(BF16)
        ug = up_scr[:, 0:D_FF]
        ext = jnp.concatenate([halo_scr[...], ug], axis=0)
        halo_scr[...] = ug[ts - CONV_HALO:, :]
        w = fcw_ref[...]
        gh = (fcb_ref[...] + w[2:3] * ug + w[1:2] * _shift_down(ext, 1, CONV_HALO)
              + w[0:1] * _shift_down(ext, 2, CONV_HALO))
        gl, t = _gelu(gh)
        gl_ref[...] = gl.astype(BF16)
        gg_ref[...] = _gelu_grad(gh, t).astype(BF16)
        dn = _dot(gl * up_scr[:, D_FF:], wd_ref[...])
        dn_ref[...] = dn
        y, _, _ = _rms_fwd(dn, gpost_ref[...])
        x2_ref[...] = x1_v + y

    row = _rows(ts, D_MODEL)
    return _call(
        body, "fwd_ffn", (s // ts,),
        [row, _resident((1, D_MODEL)), _resident(w_up.shape), _resident(fcw.shape), _resident((1, D_FF)),
         _resident(w_down.shape), _resident((1, D_MODEL))],
        [_rows(ts, 2 * D_FF), _rows(ts, D_FF), _rows(ts, D_FF), row, row, row],
        [_sds((s, 2 * D_FF), BF16), _sds((s, D_FF), BF16), _sds((s, D_FF), BF16), _sds((s, D_MODEL), BF16),
         _sds((s, D_MODEL)), _sds((s, D_MODEL))],
        [pltpu.VMEM((ts, 2 * D_FF), F32), pltpu.VMEM((CONV_HALO, D_FF), F32)],
        (x1, g_pre, w_up, fcw, fcb, w_down, g_post), tasks)


def _ple_loss(x2, p, tgt, g_gate, w_gate, w_proj, g_post, ts):
    s = x2.shape[0]

    def body(x2_ref, p_ref, t_ref, gg_ref, wg_ref, wp_ref, gp_ref, dx2_ref, loss_ref, dwg_ref, dwp_ref, dgg_ref, dgp_ref):
        @pl.when(pl.program_id(0) == 0)
        def _():
            loss_ref[...] = jnp.zeros_like(loss_ref)
            dwg_ref[...] = jnp.zeros_like(dwg_ref)
            dwp_ref[...] = jnp.zeros_like(dwp_ref)
            dgg_ref[...] = jnp.zeros_like(dgg_ref)
            dgp_ref[...] = jnp.zeros_like(dgp_ref)

        x2_v = x2_ref[...]
        n3, xh3, r3 = _rms_fwd(x2_v, gg_ref[...])
        pg = _sigmoid(_dot(n3, wg_ref[...]))
        pb = p_ref[...].astype(BF16)
        q = jnp.concatenate([jnp.dot(pb, wp_ref[j], preferred_element_type=F32) for j in range(N_CHIPS)], axis=1)
        ple, qh, rq = _rms_fwd(q, gp_ref[...])
        e = x2_v + pg * ple - t_ref[...]
        loss_ref[...] += 0.5 * jnp.sum(jnp.mean(e * e, axis=-1, keepdims=True), axis=0, keepdims=True)
        dy = e * (1.0 / D_MODEL)
        dpgl = dy * ple * pg * (1.0 - pg)
        dwg_ref[...] += _dot_tn(n3, dpgl)
        dx3, dgg = _rms_bwd(xh3, r3, gg_ref[...], _dot_nt(dpgl, wg_ref[...]))
        dgg_ref[...] += dgg
        dq, dgp = _rms_bwd(qh, rq, gp_ref[...], dy * pg)
        dgp_ref[...] += dgp
        for j in range(N_CHIPS):
            dwp_ref[j] += _dot_tn(pb, dq[:, j * 256:(j + 1) * 256])
        dx2_ref[...] = dy + dx3

    row = _rows(ts, D_MODEL)
    vec = _acc((1, D_MODEL))
    return pl.pallas_call(
        body, name="ple_loss", grid=(s // ts,),
        in_specs=[row, _rows(ts, PLE_DIM), row, _resident((1, D_MODEL)), _resident(w_gate.shape), _resident(w_proj.shape),
                  _resident((1, D_MODEL))],
        out_specs=[row, _acc((1, 128)), _acc(w_gate.shape), _acc(w_proj.shape), vec, vec],
        out_shape=[_sds((s, D_MODEL)), _sds((1, 128)), _sds(w_gate.shape), _sds(w_proj.shape), _sds((1, D_MODEL)),
                   _sds((1, D_MODEL))],
        compiler_params=_params(),
    )(x2, p, tgt, g_gate, w_gate, w_proj, g_post)


def _bwd_ffn_down(dx2, dn, up, gl, gg, fcw, w_down, g_post, ts):
    s = dx2.shape[0]
    nt = s // ts

    def body(dx2_ref, dn_ref, up_ref, gl_ref, gg_ref, fcw_ref, wd_ref, gpost_ref,
             dup_ref, dwd_ref, dfcw_ref, dfcb_ref, dgp_ref, carry_scr):
        i = pl.program_id(0)

        @pl.when(i == 0)
        def _():
            carry_scr[...] = jnp.zeros_like(carry_scr)
            dwd_ref[...] = jnp.zeros_like(dwd_ref)
            dfcw_ref[...] = jnp.zeros_like(dfcw_ref)
            dfcb_ref[...] = jnp.zeros_like(dfcb_ref)
            dgp_ref[...] = jnp.zeros_like(dgp_ref)

        _, xh, r = _rms_fwd(dn_ref[...], gpost_ref[...])
        ddn, dgp = _rms_bwd(xh, r, gpost_ref[...], dx2_ref[...])
        dgp_ref[...] += dgp
        dhid = _dot_nt(ddn, wd_ref[...])
        ug = up_ref[:, 0:D_FF].astype(F32)
        uv = up_ref[:, D_FF:].astype(F32)
        gl = gl_ref[...].astype(F32)
        w = fcw_ref[...]
        dwd_ref[...] += _dot_tn(gl * uv, ddn)
        dgh = dhid * uv * gg_ref[...].astype(F32)
        dup_ref[:, D_FF:] = (dhid * gl).astype(BF16)
        extd = jnp.concatenate([dgh, carry_scr[...]], axis=0)
        carry_scr[...] = dgh[0:CONV_HALO, :]
        d1 = _shift_up(extd, 1, ts)
        d2 = _shift_up(extd, 2, ts)
        dup_ref[:, 0:D_FF] = (w[2:3] * dgh + w[1:2] * d1 + w[0:1] * d2).astype(BF16)
        dfcw_ref[2:3, :] += jnp.sum(ug * dgh, axis=0, keepdims=True)
        dfcw_ref[1:2, :] += jnp.sum(ug * d1, axis=0, keepdims=True)
        dfcw_ref[0:1, :] += jnp.sum(ug * d2, axis=0, keepdims=True)
        dfcb_ref[...] += jnp.sum(dgh, axis=0, keepdims=True)

    row = _rows(ts, D_MODEL, nt)
    wide = _rows(ts, D_FF, nt)
    return pl.pallas_call(
        body, name="bwd_ffn_down", grid=(nt,),
        in_specs=[row, row, _rows(ts, 2 * D_FF, nt), wide, wide, _resident(fcw.shape), _resident(w_down.shape),
                  _resident((1, D_MODEL))],
        out_specs=[_rows(ts, 2 * D_FF, nt), _acc(w_down.shape), _acc(fcw.shape), _acc((1, D_FF)), _acc((1, D_MODEL))],
        out_shape=[_sds((s, 2 * D_FF), BF16), _sds(w_down.shape), _sds(fcw.shape), _sds((1, D_FF)), _sds((1, D_MODEL))],
        scratch_shapes=[pltpu.VMEM((CONV_HALO, D_FF), F32)],
        compiler_params=_params(),
    )(dx2, dn, up, gl, gg, fcw, w_down, g_post)


def _bwd_ffn_up(dup, x1, dx2, g_pre, w_up, ts, tasks=()):
    s = x1.shape[0]

    def body(dup_ref, x1_ref, dx2_ref, g_ref, wup_ref, dx1_ref, dg_ref):
        @pl.when(pl.program_id(0) == 0)
        def _():
            dg_ref[...] = jnp.zeros_like(dg_ref)

        _, xh, r = _rms_fwd(x1_ref[...], g_ref[...])
        dh2 = _dot_nt(dup_ref[:, 0:UP_SHARD], wup_ref[0])
        for j in range(1, N_CHIPS):
            dh2 = dh2 + _dot_nt(dup_ref[:, j * UP_SHARD:(j + 1) * UP_SHARD], wup_ref[j])
        dx, dg = _rms_bwd(xh, r, g_ref[...], dh2)
        dg_ref[...] += dg
        dx1_ref[...] = dx2_ref[...] + dx

    row = _rows(ts, D_MODEL)
    return _call(
        body, "bwd_ffn_up", (s // ts,),
        [_rows(ts, 2 * D_FF), row, row, _resident((1, D_MODEL)), _resident(w_up.shape)],
        [row, _acc((1, D_MODEL))], [_sds((s, D_MODEL)), _sds((1, D_MODEL))], [],
        (dup, x1, dx2, g_pre, w_up), tasks)


def _dw_up(h2, dup, ts, tasks=()):
    s = h2.shape[0]
    ts = min(DW_TILES * ts, s)

    def body(h2_ref, dup_ref, out_ref):
        @pl.when(pl.program_id(1) == 0)
        def _():
            out_ref[...] = jnp.zeros_like(out_ref)

        out_ref[0] += _dot_tn(h2_ref[...], dup_ref[...])

    return _call(
        body, "dw_up", (N_CHIPS, s // ts),
        [pl.BlockSpec((ts, D_MODEL), lambda j, i: (i, 0)), pl.BlockSpec((ts, UP_SHARD), lambda j, i: (i, j))],
        [pl.BlockSpec((1, D_MODEL, UP_SHARD), lambda j, i: (j, 0, 0))], [_sds((N_CHIPS, D_MODEL, UP_SHARD))], [],
        (h2, dup), tasks)


def _bwd_merge(dx1, mo, sp, sr, ypool, yrnn, g_post, w_o, ts, tasks=()):
    s = dx1.shape[0]

    def body(dx1_ref, mo_ref, sp_ref, sr_ref, yp_ref, yr_ref, g_ref, wo_ref,
             dgp_ref, dgr_ref, dyp_ref, dyr_ref, dwo_ref, dg_ref):
        @pl.when(pl.program_id(0) == 0)
        def _():
            dwo_ref[...] = jnp.zeros_like(dwo_ref)
            dg_ref[...] = jnp.zeros_like(dg_ref)

        _, xh, r = _rms_fwd(mo_ref[...], g_ref[...])
        dmo, dg = _rms_bwd(xh, r, g_ref[...], dx1_ref[...])
        dg_ref[...] += dg
        dmerged = _dot_nt(dmo, wo_ref[...])
        sp = sp_ref[...].astype(F32)
        sr = sr_ref[...].astype(F32)
        yp = yp_ref[...]
        yr = yr_ref[...]
        dwo_ref[...] += _dot_tn(sp * yp + sr * yr, dmo)
        dgp_ref[...] = (dmerged * yp * sp * (1.0 - sp)).astype(BF16)
        dgr_ref[...] = (dmerged * yr * sr * (1.0 - sr)).astype(BF16)
        dyp_ref[...] = (dmerged * sp).astype(BF16)
        dyr_ref[...] = (dmerged * sr).astype(BF16)

    row = _rows(ts, D_MODEL)
    return _call(
        body, "bwd_merge", (s // ts,),
        [row] * 6 + [_resident((1, D_MODEL)), _resident(w_o.shape)],
        [row] * 4 + [_acc(w_o.shape), _acc((1, D_MODEL))],
        [_sds((s, D_MODEL), BF16)] * 4 + [_sds(w_o.shape), _sds((1, D_MODEL))], [],
        (dx1, mo, sp, sr, ypool, yrnn, g_post, w_o), tasks)


def _bwd_rnn(dyr, urx, gl, gg, xc, r, ig, h, conv_w, wg, lam, w_rg_out, ts, tasks=()):
    s = urx.shape[0]
    nt = s // ts
    halo_blocks = ts // CONV_HALO

    def body(dyr_ref, urx_ref, gl_ref, gg_ref, xc_ref, r_ref, ig_ref, h_ref, hh_ref, cw_ref, wg_ref, lam_ref, wrg_ref,
             durx_ref, durg_ref, dwrg_ref, dwg_ref, dcw_ref, dcb_ref, dbg_ref, dlam_ref, mu_scr, carry_scr):
        i = pl.program_id(0)
        k = nt - 1 - i

        @pl.when(i == 0)
        def _():
            mu_scr[...] = jnp.zeros_like(mu_scr)
            carry_scr[...] = jnp.zeros_like(carry_scr)
            dwrg_ref[...] = jnp.zeros_like(dwrg_ref)
            dwg_ref[...] = jnp.zeros_like(dwg_ref)
            dcw_ref[...] = jnp.zeros_like(dcw_ref)
            dcb_ref[...] = jnp.zeros_like(dcb_ref)
            dbg_ref[...] = jnp.zeros_like(dbg_ref)
            dlam_ref[...] = jnp.zeros_like(dlam_ref)

        row = lax.broadcasted_iota(jnp.int32, (ts, 1), 0)
        first_row = (k * ts + row) == 0
        h = h_ref[...]
        dyr_v = dyr_ref[...]
        dhr = _dot_nt(dyr_v, wrg_ref[...])
        gl = gl_ref[...].astype(F32)
        dwrg_ref[...] += _dot_tn(h * gl, dyr_v)
        durg_ref[...] = (dhr * h * gg_ref[...].astype(F32)).astype(BF16)
        r_v = r_ref[...]
        ig_v = ig_ref[...]
        xc_v = xc_ref[...]
        lam_v = lam_ref[...]
        c8, a, m2, mult = _lru_coeffs(r_v, lam_v, first_row)
        b = jnp.where(row == ts - 1, 1.0, pltpu.roll(a, ts - 1, 0))
        lt = _scan_bwd(b, dhr * gl, mu_scr[0:1, :])
        mu_scr[0:1, :] = a[0:1, :] * lt[0:1, :]
        h_before = jnp.where(k > 0, hh_ref[CONV_HALO - 1:CONV_HALO, :], 0.0)
        hprev = jnp.where(row == 0, h_before, pltpu.roll(h, 1, 0))
        dmult = lt * ig_v * xc_v
        da = lt * hprev - jnp.where(first_row, 0.0, dmult * a * lax.rsqrt(m2))
        dla = da * a
        dlam_ref[...] += jnp.sum(dla * r_v, axis=0, keepdims=True)
        dlr = (dla * (-c8)) * r_v * (1.0 - r_v)
        dli = (lt * mult * xc_v) * ig_v * (1.0 - ig_v)
        dbg_ref[0:1, :] += jnp.sum(dlr, axis=0, keepdims=True)
        dbg_ref[1:2, :] += jnp.sum(dli, axis=0, keepdims=True)
        xcb = xc_v.astype(BF16)
        parts = []
        for q in range(GATE_BLOCKS):
            blk = slice(q * GATE_BLOCK, (q + 1) * GATE_BLOCK)
            dlr_q = dlr[:, blk].astype(BF16)
            dli_q = dli[:, blk].astype(BF16)
            parts.append(_dot_nt(dlr_q, wg_ref[0, q]) + _dot_nt(dli_q, wg_ref[1, q]))
            dwg_ref[0, q] += _dot_tn(xcb[:, blk], dlr_q)
            dwg_ref[1, q] += _dot_tn(xcb[:, blk], dli_q)
        dxc = lt * mult * ig_v + jnp.concatenate(parts, axis=1)
        extd = jnp.concatenate([dxc, carry_scr[...]], axis=0)
        carry_scr[...] = dxc[0:CONV_HALO, :]
        cw = cw_ref[...]
        urx_v = urx_ref[...]
        durx = cw[3:4] * dxc
        dcw_ref[3:4, :] += jnp.sum(urx_v * dxc, axis=0, keepdims=True)
        for j in (1, 2, 3):
            dj = _shift_up(extd, j, ts)
            durx = durx + cw[3 - j:4 - j] * dj
            dcw_ref[3 - j:4 - j, :] += jnp.sum(urx_v * dj, axis=0, keepdims=True)
        durx_ref[...] = durx.astype(BF16)
        dcb_ref[...] += jnp.sum(dxc, axis=0, keepdims=True)

        @pl.when(i == nt - 1)
        def _():
            dlam_ref[...] = dlam_ref[...] * (LRU_C * jax.nn.sigmoid(-lam_v))

    row_spec = _rows(ts, D_MODEL, nt)
    halo_spec = pl.BlockSpec((CONV_HALO, D_MODEL), lambda i: (jnp.maximum((nt - 1 - i) * halo_blocks - 1, 0), 0))
    vec = _acc((1, D_MODEL))
    return _call(
        body, "bwd_rnn", (nt,),
        [row_spec] * 8 + [halo_spec, _resident(conv_w.shape), _resident(wg.shape), _resident((1, D_MODEL)),
                          _resident(w_rg_out.shape)],
        [row_spec, row_spec, _acc(w_rg_out.shape), _acc(wg.shape), _acc(conv_w.shape), vec, _acc((2, D_MODEL)), vec],
        [_sds((s, D_MODEL), BF16), _sds((s, D_MODEL), BF16), _sds(w_rg_out.shape), _sds(wg.shape), _sds(conv_w.shape),
         _sds((1, D_MODEL)), _sds((2, D_MODEL)), _sds((1, D_MODEL))],
        [pltpu.VMEM((8, D_MODEL), F32), pltpu.VMEM((CONV_HALO, D_MODEL), F32)],
        (dyr, urx, gl, gg, xc, r, ig, h, h, conv_w, wg, lam, w_rg_out), tasks)


def _bwd_pool(dyp, d, pool_w, pool_scale, w_pool_out, ts, tasks=()):
    s = d.shape[0]
    nt = s // ts

    def body(dyp_ref, d_ref, pw_ref, ps_ref, wpo_ref, dzp_ref, dwpo_ref, dpw_ref, dps_ref, carry_scr):
        i = pl.program_id(0)
        k = nt - 1 - i

        @pl.when(i == 0)
        def _():
            carry_scr[...] = jnp.zeros_like(carry_scr)
            dwpo_ref[...] = jnp.zeros_like(dwpo_ref)
            dpw_ref[...] = jnp.zeros_like(dpw_ref)
            dps_ref[...] = jnp.zeros_like(dps_ref)

        dyp_v = dyp_ref[...]
        d_v = d_ref[...]
        ps = ps_ref[...]
        dypre = _dot_nt(dyp_v[:, 0:256], wpo_ref[0])
        for j in range(1, N_CHIPS):
            dypre = dypre + _dot_nt(dyp_v[:, j * 256:(j + 1) * 256], wpo_ref[j])
        y4 = jnp.concatenate([_dot(d_v[:, g * 128:(g + 1) * 128], pw_ref[g]) for g in range(POOL_GROUPS)], axis=1)
        ypre = (y4 * ps).astype(BF16)
        for j in range(N_CHIPS):
            dwpo_ref[j] += _dot_tn(ypre, dyp_v[:, j * 256:(j + 1) * 256])
        dps_ref[...] += jnp.sum(dypre * y4, axis=0, keepdims=True)
        dy4 = dypre * ps
        t = k * ts + lax.broadcasted_iota(jnp.int32, (ts, 1), 0)
        for g, w in enumerate(POOL_WINDOWS):
            lanes = slice(g * POOL_GROUP_DIM, (g + 1) * POOL_GROUP_DIM)
            dd = _dot_nt(dy4[:, lanes], pw_ref[g])
            dpw_ref[g] += _dot_tn(d_v[:, lanes], dy4[:, lanes])
            e = dd * (1.0 / jnp.minimum(t + 1, w).astype(F32))
            acc = jnp.concatenate([e, carry_scr[:, lanes]], axis=0)
            carry_scr[:, lanes] = e[0:POOL_HALO, :]
            n = ts + POOL_HALO
            sh = 1
            while sh < w:
                acc = acc + pltpu.roll(acc, n - sh, 0)
                sh *= 2
            dzp_ref[:, lanes] = (acc[:ts, :] - dd).astype(BF16)

    return _call(
        body, "bwd_pool", (nt,),
        [_rows(ts, D_MODEL, nt), _rows(ts, POOL_WIDTH, nt), _resident(pool_w.shape), _resident((1, POOL_WIDTH)),
         _resident(w_pool_out.shape)],
        [_rows(ts, POOL_WIDTH, nt), _acc(w_pool_out.shape), _acc(pool_w.shape), _acc((1, POOL_WIDTH))],
        [_sds((s, POOL_WIDTH), BF16), _sds(w_pool_out.shape), _sds(pool_w.shape), _sds((1, POOL_WIDTH))],
        [pltpu.VMEM((POOL_HALO, POOL_WIDTH), F32)],
        (dyp, d, pool_w, pool_scale, w_pool_out), tasks)


def _assemble_dz(dz_scr, dzp_ref, durx_ref, durg_ref, dgp_ref, dgr_ref):
    dz_scr[:, 0:512] = dzp_ref[...]
    dz_scr[:, 512:1536] = durx_ref[...]
    dz_scr[:, 1536:2560] = durg_ref[...]
    dz_scr[:, 2560:3584] = dgp_ref[...]
    dz_scr[:, 3584:4608] = dgr_ref[...]


def _dw_in(h1, dzp, durx, durg, dgp, dgr, ts, tasks=()):
    s = h1.shape[0]
    ts = min(2 * ts, s)
    nt = s // ts
    half = D_MODEL // 2

    def body(h1_ref, dzp_ref, durx_ref, durg_ref, dgp_ref, dgr_ref, theirs_ref, mine_ref, got_ref, sems):
        o, i = pl.program_id(0), pl.program_id(1)

        @pl.when((o == 0) & (i == 0))
        def _():
            theirs_ref[...] = jnp.zeros_like(theirs_ref)
            mine_ref[...] = jnp.zeros_like(mine_ref)

        groups = (dzp_ref, durx_ref, durg_ref, dgp_ref, dgr_ref)
        for out_ref, which in ((theirs_ref, 0), (mine_ref, 1)):
            @pl.when(o == which)
            def _():
                for group, dz_ref in enumerate(groups):
                    for j, shard_cols, group_cols in _overlaps(group):
                        out_ref[j, :, shard_cols] += _dot_tn(h1_ref[...], dz_ref[:, group_cols])

        x, y, c, _, _ = _place()
        send = _remote(theirs_ref, got_ref, sems.at[0], sems.at[1], (x, y, 1 - c))
        pl.when((o == 1) & (i == 0))(send.start)
        pl.when((o == 1) & (i == nt - 1))(send.wait)

    def h1_cols(o, i):
        c = lax.axis_index("c")
        return i, jnp.where(o == 0, 1 - c, c)

    rows = lambda width: pl.BlockSpec((ts, width), lambda o, i: (i, 0))
    shape = (N_CHIPS, half, IN_SHARD)
    whole = pl.BlockSpec(shape, lambda o, i: (0, 0, 0))
    return _call(
        body, "dw_in", (2, nt), [pl.BlockSpec((ts, half), h1_cols), rows(POOL_WIDTH)] + [rows(D_MODEL)] * 4,
        [whole, whole, pl.BlockSpec(memory_space=pltpu.HBM)], [_sds(shape)] * 3,
        [pltpu.SemaphoreType.DMA((2,))], (h1, dzp, durx, durg, dgp, dgr), tasks)


def _bwd_in(dzp, durx, durg, dgp, dgr, x, dx1, g_pre, w_in, ts, tasks=()):
    s = x.shape[0]

    def body(dzp_ref, durx_ref, durg_ref, dgp_ref, dgr_ref, x_ref, dx1_ref, g_ref, win_ref, gx_ref, dg_ref, dz_scr):
        @pl.when(pl.program_id(0) == 0)
        def _():
            dg_ref[...] = jnp.zeros_like(dg_ref)

        _assemble_dz(dz_scr, dzp_ref, durx_ref, durg_ref, dgp_ref, dgr_ref)
        _, xh, r = _rms_fwd(x_ref[...], g_ref[...])
        dh1 = _dot_nt(dz_scr[:, 0:IN_SHARD], win_ref[0])
        for j in range(1, N_CHIPS):
            dh1 = dh1 + _dot_nt(dz_scr[:, j * IN_SHARD:(j + 1) * IN_SHARD], win_ref[j])
        dx, dg = _rms_bwd(xh, r, g_ref[...], dh1)
        dg_ref[...] += dg
        gx_ref[...] = dx1_ref[...] + dx

    row = _rows(ts, D_MODEL)
    return _call(
        body, "bwd_in", (s // ts,),
        [_rows(ts, POOL_WIDTH)] + [row] * 6 + [_resident((1, D_MODEL)), _resident(w_in.shape)],
        [row, _acc((1, D_MODEL))], [_sds((s, D_MODEL)), _sds((1, D_MODEL))],
        [pltpu.VMEM((ts, IN_TOTAL), BF16)], (dzp, durx, durg, dgp, dgr, x, dx1, g_pre, w_in), tasks)


def _place():
    x, y, c = lax.axis_index("x"), lax.axis_index("y"), lax.axis_index("c")
    others = [(1 - x, y), (x, 1 - y), (1 - x, 1 - y)]
    return x, y, c, 2 * x + y, others


def _remote(src, dst, send_sem, recv_sem, to):
    return pltpu.make_async_remote_copy(src_ref=src, dst_ref=dst, send_sem=send_sem, recv_sem=recv_sem,
                                        device_id=to, device_id_type=MESH)


def _own_slots(ws, dtypes, name, tasks=()):
    n = len(ws)
    hbm = pl.BlockSpec(memory_space=pltpu.HBM)

    def body(*refs):
        srcs, outs, f32_bufs, cast_bufs, sems = refs[:n], refs[n:2 * n], refs[2 * n:3 * n], refs[3 * n:4 * n], refs[4 * n]
        me = _place()[3]
        loads = [pltpu.make_async_copy(srcs[k], f32_bufs[k], sems.at[k, 0]) for k in range(n)]
        stores = [pltpu.make_async_copy(cast_bufs[k], outs[k].at[me], sems.at[k, 1]) for k in range(n)]
        for cp in loads:
            cp.start()
        for k in range(n):
            loads[k].wait()
            cast_bufs[k][...] = f32_bufs[k][...].astype(dtypes[k])
            stores[k].start()
        for cp in stores:
            cp.wait()

    return _call(
        body, name, (), [hbm] * n, [hbm] * n, [_sds((N_CHIPS,) + w.shape, dt) for w, dt in zip(ws, dtypes)],
        [pltpu.VMEM(w.shape, F32) for w in ws] + [pltpu.VMEM(w.shape, dt) for w, dt in zip(ws, dtypes)]
        + [pltpu.SemaphoreType.DMA((n, 2))],
        [pltpu.with_memory_space_constraint(w, pltpu.HBM) for w in ws], tasks)


def _run(tasks, name):
    if isinstance(tasks, _Task):
        return _call(None, name, (), [], [], [], [], (), (tasks,))[1][0]
    return _call(None, name, (), [], [], [], [], (), tuple(tasks))[1]


def _gather_task(bufs, relay_steps=(0, 0)):
    n = len(bufs)
    NBR_X, NBR_Y, QUARTER_VIA_Y, QUARTER_VIA_X, SIB_X, SIB_Y, SIB_DIAG = range(7)

    def parts(out):
        x, y, c, me, _ = _place()
        ah = out.shape[1] // 2
        q = ah // 2 if (ah // 2) % 16 == 0 else ah
        return c * ah, ah, q

    def copy(out, w, k, chip, row0, rows, to, sems):
        slot = out.at[chip, pl.ds(row0, rows)]
        return _remote(slot, slot, sems[0].at[w, k], sems[1].at[w, k], to)

    def plan(out, w, sems):
        x, y, c, me, _ = _place()
        row0, ah, q = parts(out)
        xn, yn, dg = 2 * (1 - x) + y, 2 * x + (1 - y), 2 * (1 - x) + (1 - y)
        to_x, to_y, sib = (1 - x, y, c), (x, 1 - y, c), (x, y, 1 - c)
        other = (1 - c) * ah
        cp = functools.partial(copy, out, w, sems=sems)
        sends = {NBR_X: cp(NBR_X, me, row0, ah, to_x), NBR_Y: cp(NBR_Y, me, row0, ah, to_y),
                 QUARTER_VIA_Y: cp(QUARTER_VIA_Y, xn, row0, q, to_y), SIB_X: cp(SIB_X, xn, row0, ah, sib),
                 SIB_Y: cp(SIB_Y, yn, row0, ah, sib), SIB_DIAG: cp(SIB_DIAG, dg, row0, ah, sib)}
        lands = {NBR_X: cp(NBR_X, xn, row0, ah, to_x), NBR_Y: cp(NBR_Y, yn, row0, ah, to_y),
                 QUARTER_VIA_Y: cp(QUARTER_VIA_Y, dg, row0, q, to_y), SIB_X: cp(SIB_X, xn, other, ah, sib),
                 SIB_Y: cp(SIB_Y, yn, other, ah, sib), SIB_DIAG: cp(SIB_DIAG, dg, other, ah, sib)}
        if q < ah:
            sends[QUARTER_VIA_X] = cp(QUARTER_VIA_X, yn, row0 + q, ah - q, to_x)
            lands[QUARTER_VIA_X] = cp(QUARTER_VIA_X, dg, row0 + q, ah - q, to_x)
        return sends, lands

    def start(ins, outs, sems):
        for w, out in enumerate(outs):
            sends, _ = plan(out, w, sems)
            sends[NBR_X].start()
            sends[NBR_Y].start()

    def pass_neighbours(ins, outs, sems):
        for w, out in enumerate(outs):
            sends, lands = plan(out, w, sems)
            lands[NBR_X].wait_recv()
            sends[QUARTER_VIA_Y].start()
            sends[SIB_X].start()
            lands[NBR_Y].wait_recv()
            if QUARTER_VIA_X in sends:
                sends[QUARTER_VIA_X].start()
            sends[SIB_Y].start()

    def pass_diagonal(ins, outs, sems):
        for w, out in enumerate(outs):
            sends, lands = plan(out, w, sems)
            lands[QUARTER_VIA_Y].wait_recv()
            if QUARTER_VIA_X in lands:
                lands[QUARTER_VIA_X].wait_recv()
            sends[SIB_DIAG].start()

    def finish(ins, outs, sems):
        for w, out in enumerate(outs):
            sends, lands = plan(out, w, sems)
            for k in (SIB_X, SIB_Y, SIB_DIAG):
                lands[k].wait_recv()
        for w, out in enumerate(outs):
            sends, _ = plan(out, w, sems)
            for cp in sends.values():
                cp.wait_send()

    return _Task(bufs, [_sds(b.shape, b.dtype) for b in bufs], {i: i for i in range(n)},
                 [pltpu.SemaphoreType.DMA((n, 7)), pltpu.SemaphoreType.DMA((n, 7))], start, finish,
                 [(pass_neighbours, relay_steps[0]), (pass_diagonal, relay_steps[1])])


def _halves_task(grads):
    n = len(grads)

    def copy(src, out, w, sems):
        x, y, c, _, _ = _place()
        ah = out.shape[1]
        return _remote(src.at[:, pl.ds((1 - c) * ah, ah)], out, sems[0].at[w], sems[1].at[w], (x, y, 1 - c))

    def start(ins, outs, sems):
        for w, (src, out) in enumerate(zip(ins, outs)):
            copy(src, out, w, sems).start()

    def finish(ins, outs, sems):
        for w, (src, out) in enumerate(zip(ins, outs)):
            copy(src, out, w, sems).wait()

    return _Task(grads, [_sds((g.shape[0], g.shape[1] // 2, g.shape[2]), g.dtype) for g in grads], {},
                 [pltpu.SemaphoreType.DMA((n,)), pltpu.SemaphoreType.DMA((n,))], start, finish)


def _exchange_task(sends, accs):
    n = len(accs)
    given = [s for s in sends if s is not None]

    def copies(ins, outs, sems):
        send_refs = iter(ins[:len(given)])
        srcs = [next(send_refs) if s is not None else None for s in sends]
        x, y, c, me, others = _place()
        for w, out in enumerate(outs):
            for j, (ox, oy) in enumerate(others):
                src = out.at[me] if srcs[w] is None else srcs[w].at[2 * ox + oy]
                yield _remote(src, out.at[me], sems[0].at[w, j], sems[1].at[w, j], (ox, oy, c))

    def start(ins, outs, sems):
        for cp in copies(ins, outs, sems):
            cp.start()

    def finish(ins, outs, sems):
        x, y, c, _, others = _place()
        for w, out in enumerate(outs):
            for j, (ox, oy) in enumerate(others):
                slot = out.at[2 * ox + oy]
                _remote(slot, slot, sems[0].at[w, j], sems[1].at[w, j], (ox, oy, c)).wait_recv()
        for cp in copies(ins, outs, sems):
            cp.wait_send()

    return _Task(given + list(accs), [_sds(a.shape, a.dtype) for a in accs], {len(given) + i: i for i in range(n)},
                 [pltpu.SemaphoreType.DMA((n, 3)), pltpu.SemaphoreType.DMA((n, 3))], start, finish)


def _swap_task(arrays):
    n = len(arrays)

    def copy(src, out, w, sems):
        x, y, c, _, _ = _place()
        return _remote(src, out, sems[0].at[w], sems[1].at[w], (x, y, 1 - c))

    def start(ins, outs, sems):
        for w, (src, out) in enumerate(zip(ins, outs)):
            copy(src, out, w, sems).start()

    def finish(ins, outs, sems):
        for w, (src, out) in enumerate(zip(ins, outs)):
            copy(src, out, w, sems).wait()

    return _Task(arrays, [_sds(a.shape, a.dtype) for a in arrays], {},
                 [pltpu.SemaphoreType.DMA((n,)), pltpu.SemaphoreType.DMA((n,))], start, finish)


def _all_devices_task(arrays):
    n = len(arrays)
    flips = [(dx, dy, dc) for dx in (0, 1) for dy in (0, 1) for dc in (0, 1)][1:]

    def peers():
        x, y, c, _, _ = _place()
        flip = lambda v, d: 1 - v if d else v
        return 4 * x + 2 * y + c, [(flip(x, dx), flip(y, dy), flip(c, dc)) for dx, dy, dc in flips]

    def start(ins, outs, sems):
        me, others = peers()
        for w, (src, out) in enumerate(zip(ins, outs)):
            pltpu.make_async_copy(src, out.at[me], sems[2].at[w]).start()
            for k, peer in enumerate(others):
                _remote(src, out.at[me], sems[0].at[w, k], sems[1].at[w, k], peer).start()

    def finish(ins, outs, sems):
        me, others = peers()
        for w, (src, out) in enumerate(zip(ins, outs)):
            for k, (px, py, pc) in enumerate(others):
                slot = out.at[4 * px + 2 * py + pc]
                _remote(slot, slot, sems[0].at[w, k], sems[1].at[w, k], (px, py, pc)).wait_recv()
            for k, peer in enumerate(others):
                _remote(src, out.at[me], sems[0].at[w, k], sems[1].at[w, k], peer).wait_send()
            pltpu.make_async_copy(src, out.at[me], sems[2].at[w]).wait()

    return _Task(arrays, [_sds((8,) + a.shape, a.dtype) for a in arrays], {},
                 [pltpu.SemaphoreType.DMA((n, 7)), pltpu.SemaphoreType.DMA((n, 7)), pltpu.SemaphoreType.DMA((n,))],
                 start, finish)


def _share_task(shares):
    n = len(shares)

    def copy(out, w, sems, slot):
        x, y, c, _, _ = _place()
        return _remote(out.at[slot], out.at[slot], sems[0].at[w], sems[1].at[w], (x, y, 1 - c))

    def start(ins, outs, sems):
        c = _place()[2]
        for w, out in enumerate(outs):
            copy(out, w, sems, c).start()

    def finish(ins, outs, sems):
        c = _place()[2]
        for w, out in enumerate(outs):
            copy(out, w, sems, 1 - c).wait_recv()
        for w, out in enumerate(outs):
            copy(out, w, sems, c).wait_send()

    return _Task(shares, [_sds(s.shape, s.dtype) for s in shares], {i: i for i in range(n)},
                 [pltpu.SemaphoreType.DMA((n,)), pltpu.SemaphoreType.DMA((n,))], start, finish)


TILE_BYTES = 2 * 1024 * 1024
PARTIAL_TILE_BYTES = 1024 * 1024


def _in_hbm(t):
    return pltpu.with_memory_space_constraint(t, pltpu.HBM)


def _row_tile(rows, cols, limit=TILE_BYTES):
    best = 8
    for tr in range(8, rows + 1, 8):
        if rows % tr == 0 and tr * cols * 4 <= limit:
            best = tr
    assert rows % best == 0, (rows, cols)
    return best


def _chip_partial(g, got, place, wire_dtype):
    ns, ah, b = got.shape
    sharded = ns == N_CHIPS
    tr = _row_tile(ah, b, PARTIAL_TILE_BYTES)
    nb = ah // tr
    first = 0 if g.shape[1] == ah else nb

    def body(place_ref, *refs):
        g_refs, got_refs, outs = refs[:ns], refs[ns:2 * ns], refs[2 * ns:]
        parts = [g_refs[k][0] + got_refs[k][0] for k in range(ns)]
        own = parts[0]
        if sharded:
            for k in range(ns):
                outs[0][k] = parts[k].astype(wire_dtype)
                if k:
                    own = jnp.where(place_ref[0] == k, parts[k], own)
        outs[-1][0] = own.astype(wire_dtype)

    blk = (1, tr, b)
    in_specs = ([pl.BlockSpec(blk, lambda i, s, k=k: (k, s[1] * first + i, 0)) for k in range(ns)]
                + [pl.BlockSpec(blk, lambda i, s, k=k: (k, i, 0)) for k in range(ns)])
    acc_spec = pl.BlockSpec(blk, lambda i, s: (s[0], i, 0))
    acc_shape = _sds((N_CHIPS, ah, b), wire_dtype)
    out = pl.pallas_call(
        body, name="grad_chip_partial",
        grid_spec=pltpu.PrefetchScalarGridSpec(
            num_scalar_prefetch=1, grid=(nb,), in_specs=in_specs,
            out_specs=[pl.BlockSpec((ns, tr, b), lambda i, s: (0, i, 0)), acc_spec] if sharded else [acc_spec]),
        out_shape=[acc_shape, acc_shape] if sharded else [acc_shape],
        compiler_params=pltpu.CompilerParams(dimension_semantics=("arbitrary",), vmem_limit_bytes=VMEM_LIMIT),
    )(place, *([g] * ns), *([got] * ns))
    return (out[0], out[1]) if sharded else (None, out[0])


def _chip_sum(acc, place):
    _, ah, b = acc.shape
    tr = _row_tile(ah, b)

    def body(place_ref, p_ref, out_ref):
        total = p_ref[0].astype(F32) + p_ref[1].astype(F32)
        total = total + p_ref[2].astype(F32)
        out_ref[0] = total + p_ref[3].astype(F32)

    return pl.pallas_call(
        body, name="grad_chip_sum",
        grid_spec=pltpu.PrefetchScalarGridSpec(
            num_scalar_prefetch=1, grid=(ah // tr,),
            in_specs=[pl.BlockSpec((N_CHIPS, tr, b), lambda i, s: (0, i, 0))],
            out_specs=pl.BlockSpec((1, tr, b), lambda i, s: (s[1], i, 0))),
        out_shape=_sds((2, ah, b)),
        compiler_params=pltpu.CompilerParams(dimension_semantics=("arbitrary",)),
    )(place, _in_hbm(acc))


def _adam_math(w, g, m, v):
    nm = ADAM_B1 * m + (1.0 - ADAM_B1) * g
    nv = ADAM_B2 * v + (1.0 - ADAM_B2) * (g * g)
    m_hat = nm / (1.0 - ADAM_B1 ** ADAM_STEP)
    v_hat = nv / (1.0 - ADAM_B2 ** ADAM_STEP)
    return -ADAM_LR * (m_hat / (jnp.sqrt(v_hat) + ADAM_EPS) + ADAM_WD * w), nm, nv


def _adamw(w, g, m, v):
    a, b = w.shape
    tr = _row_tile(a, b)

    def body(w_ref, g_ref, m_ref, v_ref, g_out, d_ref, nm_ref, nv_ref):
        g_out[...] = g_ref[...]
        d_ref[...], nm_ref[...], nv_ref[...] = _adam_math(w_ref[...], g_ref[...], m_ref[...], v_ref[...])

    blk = pl.BlockSpec((tr, b), lambda i: (i, 0))
    return pl.pallas_call(
        body, name="adamw", grid=(a // tr,),
        in_specs=[blk] * 4, out_specs=[blk] * 4, out_shape=[_sds((a, b))] * 4,
        compiler_params=pltpu.CompilerParams(dimension_semantics=("arbitrary",)),
    )(w, g, m, v)


def _adamw_sum(w, m, v, acc, got, place):
    a, b = w.shape
    ah = a // 2
    tr = _row_tile(ah, b)
    nb = ah // tr

    def body(place_ref, w_ref, m_ref, v_ref, acc_ref, got_ref, g_out, d_ref, nm_ref, nv_ref):
        mine = (pl.program_id(0) // nb) == place_ref[1]
        part = lambda k: jnp.where(mine, acc_ref[k], got_ref[k]).astype(F32)
        g = part(0) + part(1)
        g = g + part(2)
        g = g + part(3)
        g_out[...] = g
        d_ref[...], nm_ref[...], nv_ref[...] = _adam_math(w_ref[...], g, m_ref[...], v_ref[...])

    blk = pl.BlockSpec((tr, b), lambda i, s: (i, 0))
    mine_spec = pl.BlockSpec((N_CHIPS, tr, b), lambda i, s: (0, jnp.where(i // nb == s[1], i % nb, 0), 0))
    got_spec = pl.BlockSpec((N_CHIPS, tr, b), lambda i, s: (0, jnp.where(i // nb == s[1], 0, i % nb), 0))
    return pl.pallas_call(
        body, name="adamw_sum",
        grid_spec=pltpu.PrefetchScalarGridSpec(
            num_scalar_prefetch=1, grid=(a // tr,), in_specs=[blk] * 3 + [mine_spec, got_spec], out_specs=[blk] * 4),
        out_shape=[_sds((a, b))] * 4,
        compiler_params=pltpu.CompilerParams(dimension_semantics=("arbitrary",), vmem_limit_bytes=VMEM_LIMIT),
    )(place, w, m, v, _in_hbm(acc), _in_hbm(got))


def _adamw_pieces(g, pieces, name):
    n = len(pieces)

    def body(g_ref, *refs):
        def grad(rows, cols):
            if len(g_ref.shape) == 2:
                return g_ref[rows, cols]
            total = g_ref[0, rows, cols]
            for k in range(1, g_ref.shape[0]):
                total = total + g_ref[k, rows, cols]
            return total

        ins, outs = refs[:3 * n], refs[3 * n:]
        for i, piece in enumerate(pieces):
            w_ref, m_ref, v_ref = ins[3 * i:3 * i + 3]
            o_g, o_d, o_m, o_v = outs[4 * i:4 * i + 4]
            if len(piece) == 5:
                g_v = grad(piece[3], piece[4])
                o_g[...] = g_v
                o_d[...], o_m[...], o_v[...] = _adam_math(w_ref[...], g_v, m_ref[...], v_ref[...])
            else:
                for r in range(w_ref.shape[1] // SMALL_COLS):
                    lanes = slice(r * SMALL_COLS, (r + 1) * SMALL_COLS)
                    g_v = grad(slice(piece[3] + r, piece[3] + r + 1), slice(None))
                    o_g[:, lanes] = g_v
                    o_d[:, lanes], o_m[:, lanes], o_v[:, lanes] = _adam_math(w_ref[:, lanes], g_v, m_ref[:, lanes],
                                                                            v_ref[:, lanes])

    operands = [t for piece in pieces for t in piece[:3]]
    out = pl.pallas_call(
        body, name=name,
        out_shape=[_sds(piece[0].shape) for piece in pieces for _ in range(4)],
    )(g, *operands)
    return [tuple(out[4 * i:4 * i + 4]) for i in range(n)]


TINY_ROWS, TINY_COLS = 16, 768
SMALL_COLS = 128
SMALL_ROWS = 624


def _pack_tiny(conv_w, b_gates, fcw):
    ns = conv_w.shape[0]
    pad = lambda t: jnp.pad(t, ((0, 0), (0, 0), (0, TINY_COLS - t.shape[2])))
    z = lambda rows: jnp.zeros((ns, rows, TINY_COLS), F32)
    return jnp.concatenate([pad(conv_w), pad(b_gates), z(2), fcw, z(TINY_ROWS - 11)], axis=1)


def _unpack_tiny(t):
    return t[:, 0:4, 0:256], t[:, 4:6, 0:256], t[:, 8:11, :]


def _cols_to_shards(t, n):
    return t.reshape(t.shape[0], N_CHIPS, n).transpose(1, 0, 2)


def _shards_to_cols(t):
    return t.transpose(1, 0, 2).reshape(t.shape[1], -1)


_VECTORS = ("g_mix_post", "conv_b", "lru_lambda", "g_ffn_pre", "g_ffn_post", "g_ple_gate", "g_ple_post", "pool_scale",
            "ffn_conv_b")
_VECTOR_LEN = {"pool_scale": POOL_WIDTH, "ffn_conv_b": D_FF}
POOL_W_ROWS = POOL_GROUPS * POOL_GROUP_DIM


def _vector_rows():
    rows, row = {}, POOL_W_ROWS
    for k in _VECTORS:
        rows[k] = row
        row += max(8, _VECTOR_LEN.get(k, D_MODEL) // SMALL_COLS)
    return rows, row


def _pack_small(grads, loss):
    tiles = lambda t: jnp.pad(t, ((0, -t.shape[0] % 8), (0, 0)))
    parts = [grads["pool_w"].reshape(POOL_W_ROWS, SMALL_COLS)] + [tiles(grads[k].reshape(-1, SMALL_COLS)) for k in _VECTORS]
    parts.append(tiles(loss))
    used = sum(t.shape[0] for t in parts)
    return jnp.concatenate(parts + [jnp.zeros((SMALL_ROWS - used, SMALL_COLS), F32)], axis=0)


def _gates_block_diag(w):
    w4 = w.reshape(2, GATE_BLOCKS, 4, RNN_HEAD_DIM, RNN_HEAD_DIM)
    eye = jnp.eye(4, dtype=w.dtype)
    return jnp.einsum("gqhij,hk->gqhikj", w4, eye).reshape(2, GATE_BLOCKS, GATE_BLOCK, GATE_BLOCK)


def _gates_from_block_diag(dw):
    d6 = dw.reshape(2, GATE_BLOCKS, 4, RNN_HEAD_DIM, 4, RNN_HEAD_DIM)
    blocks = [d6[:, :, hh, :, hh, :] for hh in range(4)]
    return jnp.stack(blocks, axis=2).reshape(2, RNN_HEADS, RNN_HEAD_DIM, RNN_HEAD_DIM)


ROW_TILE = 256
DW_TILES = 4

_SHARDED = ("w_in", "w_pool_out", "w_rg_out", "w_o", "w_up", "w_down", "w_ple_gate", "w_ple_proj")
_WEIGHTS = ("g_mix_pre", "g_mix_post", "w_in", "pool_w", "pool_scale", "w_pool_out", "conv_w", "conv_b", "w_rg_gates",
            "b_rg_gates", "lru_lambda", "w_rg_out", "w_o", "g_ffn_pre", "g_ffn_post", "w_up", "ffn_conv_w", "ffn_conv_b",
            "w_down", "g_ple_gate", "w_ple_gate", "w_ple_proj", "g_ple_post")


def _wire_dtype(g):
    return BF16 if g.shape[1] >= 64 and g.shape[2] > SMALL_COLS else F32


def _partials(grads, got, place):
    parts = [_chip_partial(g, r, place, _wire_dtype(g)) for g, r in zip(grads, got)]
    return [send for send, _ in parts], [acc for _, acc in parts]


def _whole(both):
    return [b.reshape(2 * b.shape[1], b.shape[2]) for b in both]


def _step(x, p, tgt, rep, place, ts):
    vec = lambda k: rep[k].reshape(1, -1)
    tall = min(2 * ts, x.shape[0])
    pool_w = rep["pool_w"].astype(BF16)
    wg = _gates_block_diag(rep["w_rg_gates"]).astype(BF16)
    sq = lambda t: t.reshape(D_MODEL, D_MODEL)
    by4 = lambda t: t.reshape(N_CHIPS, -1, D_MODEL)

    first, ride1, ride2 = (("w_in", "w_pool_out", "tiny"), ("w_rg_out", "w_o", "w_down"),
                           ("w_up", "w_ple_gate", "w_ple_proj"))
    later = ride1 + ride2
    tiny = _pack_tiny(rep["conv_w"][None], rep["b_rg_gates"][None], rep["ffn_conv_w"][None])[0]
    own_first, _ = _own_slots([rep["w_in"], rep["w_pool_out"], tiny], [BF16, BF16, F32], "own_slots_first")
    own_later, (got,) = _own_slots([rep[k] for k in later], [BF16] * len(later), "own_slots_gather_first",
                                   [_gather_task(own_first)])
    own = dict(zip(later, own_later))
    full = dict(zip(first, got))
    conv_w, b_gates, fcw = [_shards_to_cols(t) for t in _unpack_tiny(full["tiny"])]

    (urx, urg, gp, gr, d, ypool, h1), (got,) = _fwd_in_pool(
        x, vec("g_mix_pre"), full["w_in"], pool_w, vec("pool_scale"), full["w_pool_out"], ts,
        [_gather_task([own[k] for k in ride1], relay_steps=(6, 2))])
    full.update(zip(ride1, got))
    w_rg_out, w_o, w_down = sq(full["w_rg_out"]), sq(full["w_o"]), full["w_down"].reshape(D_FF, D_MODEL)
    (xc, r, ig, h, yrnn, mo, x1, glr, ggr, sp, sr), (got,) = _fwd_rnn_merge(
        urx, urg, gp, gr, ypool, x, conv_w, vec("conv_b"), wg, b_gates, vec("lru_lambda"), w_rg_out, w_o,
        vec("g_mix_post"), ts, [_gather_task([own[k] for k in ride2], relay_steps=(7, 3))])
    full.update(zip(ride2, got))
    (up, gl, gg, h2, dn, x2), _ = _fwd_ffn(x1, vec("g_ffn_pre"), full["w_up"], fcw, vec("ffn_conv_b"), w_down,
                                           vec("g_ffn_post"), ts)
    dx2, loss, d_w_gate, d_w_proj, d_g_ple_gate, d_g_ple_post = _ple_loss(
        x2, p, tgt, vec("g_ple_gate"), sq(full["w_ple_gate"]), full["w_ple_proj"], vec("g_ple_post"), tall)
    dup, d_w_down, d_fcw, d_fcb, d_g_ffn_post = _bwd_ffn_down(dx2, dn, up, gl, gg, fcw, w_down, vec("g_ffn_post"), ts)

    names1, grads1 = ("w_ple_gate", "w_ple_proj", "w_down"), [by4(d_w_gate), d_w_proj, by4(d_w_down)]
    (dx1, d_g_ffn_pre), (got1,) = _bwd_ffn_up(dup, x1, dx2, vec("g_ffn_pre"), full["w_up"], tall, [_halves_task(grads1)])
    (d_w_up,), (accs1,) = _dw_up(h2, dup, ts, [_exchange_task(*_partials(grads1, got1, place))])
    (dgp, dgr, dyp, dyr, d_w_o, d_g_mix_post), (got2, theirs1) = _bwd_merge(
        dx1, mo, sp, sr, ypool, yrnn, vec("g_mix_post"), w_o, tall, [_halves_task([d_w_up]), _swap_task(accs1)])
    (durx, durg, d_w_rg_out, d_wg, d_conv_w, d_conv_b, d_b_gates, d_lam), (accs2,) = _bwd_rnn(
        dyr, urx, glr, ggr, xc, r, ig, h, conv_w, wg, vec("lru_lambda"), w_rg_out, ts,
        [_exchange_task(*_partials([d_w_up], got2, place))])
    names3 = ("w_o", "w_rg_out", "tiny", "w_rg_gates")
    grads3 = [by4(d_w_o), by4(d_w_rg_out),
              _pack_tiny(_cols_to_shards(d_conv_w, 256), _cols_to_shards(d_b_gates, 256), _cols_to_shards(d_fcw, 768)),
              _gates_from_block_diag(d_wg).reshape(1, 2 * RNN_HEADS * RNN_HEAD_DIM, RNN_HEAD_DIM)]
    (dzp, d_w_pool_out, d_pool_w, d_pool_scale), (got3, theirs2) = _bwd_pool(
        dyp, d, pool_w, vec("pool_scale"), full["w_pool_out"], tall, [_halves_task(grads3), _swap_task(accs2)])
    replicated = {"g_mix_post": d_g_mix_post, "conv_b": d_conv_b, "lru_lambda": d_lam, "g_ffn_pre": d_g_ffn_pre,
                  "g_ffn_post": d_g_ffn_post, "g_ple_gate": d_g_ple_gate, "g_ple_post": d_g_ple_post,
                  "pool_scale": d_pool_scale, "ffn_conv_b": d_fcb, "pool_w": d_pool_w}
    names4 = ("w_in", "w_pool_out", "small")
    small4 = [d_w_pool_out, _pack_small(replicated, loss)[None]]
    (_, d_w_in_half, got_w_in), (accs3, got_small4) = _dw_in(
        h1, dzp, durx, durg, dgp, dgr, ts, [_exchange_task(*_partials(grads3, got3, place)), _halves_task(small4)])
    grads4, got4 = [d_w_in_half] + small4, [got_w_in] + got_small4
    (grad_x, d_g_mix_pre), (accs4, theirs3, both3) = _bwd_in(
        dzp, durx, durg, dgp, dgr, x, dx1, vec("g_mix_pre"), full["w_in"], tall,
        [_exchange_task(*_partials(grads4, got4, place)), _swap_task(accs3[:2]),
         _share_task([_chip_sum(acc, place) for acc in accs3[2:]])])
    theirs4, both4, (g_mix_pre_parts,) = _run(
        [_swap_task(accs4[:2]), _share_task([_chip_sum(acc, place) for acc in accs4[2:]]),
         _all_devices_task([d_g_mix_pre.reshape(SUBLANES, SMALL_COLS)])], "grad_sibling_share")
    mine = accs1 + accs2 + accs3[:2] + accs4[:2]
    partials = dict(zip(names1 + ("w_up",) + names3[:2] + names4[:2], zip(mine, theirs1 + theirs2 + theirs3 + theirs4)))
    return grad_x, partials, dict(zip(names3[2:] + names4[2:], _whole(both3) + _whole(both4))), g_mix_pre_parts


def kernel(x, p, g_mix_pre, g_mix_post, w_in, pool_w, pool_scale, w_pool_out, conv_w, conv_b, w_rg_gates, b_rg_gates, lru_lambda, w_rg_out, w_o, g_ffn_pre, g_ffn_post, w_up, ffn_conv_w, ffn_conv_b, w_down, g_ple_gate, w_ple_gate, w_ple_proj, g_ple_post, loss_target, m_g_mix_pre, m_g_mix_post, m_w_in, m_pool_w, m_pool_scale, m_w_pool_out, m_conv_w, m_conv_b, m_w_rg_gates, m_b_rg_gates, m_lru_lambda, m_w_rg_out, m_w_o, m_g_ffn_pre, m_g_ffn_post, m_w_up, m_ffn_conv_w, m_ffn_conv_b, m_w_down, m_g_ple_gate, m_w_ple_gate, m_w_ple_proj, m_g_ple_post, v_g_mix_pre, v_g_mix_post, v_w_in, v_pool_w, v_pool_scale, v_w_pool_out, v_conv_w, v_conv_b, v_w_rg_gates, v_b_rg_gates, v_lru_lambda, v_w_rg_out, v_w_o, v_g_ffn_pre, v_g_ffn_post, v_w_up, v_ffn_conv_w, v_ffn_conv_b, v_w_down, v_g_ple_gate, v_w_ple_gate, v_w_ple_proj, v_g_ple_post):
    args = dict(locals())
    w = {k: args[k][0] for k in _WEIGHTS}
    m = {k: args["m_" + k][0] for k in _WEIGHTS}
    v = {k: args["v_" + k][0] for k in _WEIGHTS}
    place = jnp.stack([2 * lax.axis_index("x") + lax.axis_index("y"), lax.axis_index("c")]).astype(jnp.int32)
    grad_x, partials, reduced, g_mix_pre_parts = _step(x[0], p[0, 0], loss_target[0], w, place, ROW_TILE)

    gates_2d = (2 * RNN_HEADS * RNN_HEAD_DIM, RNN_HEAD_DIM)
    as2d = lambda k, shape: tuple(t[k].reshape(shape) for t in (w, m, v))
    done = {k: tuple(_adamw_sum(w[k], m[k], v[k], *partials[k], place)) for k in _SHARDED}
    gates_w, gates_m, gates_v = as2d("w_rg_gates", gates_2d)
    done["w_rg_gates"] = tuple(_adamw(gates_w, reduced["w_rg_gates"], gates_m, gates_v))
    tiny_names = ("conv_w", "b_rg_gates", "ffn_conv_w")
    tiny_at = ((slice(0, 4), slice(0, 256)), (slice(4, 6), slice(0, 256)), (slice(8, 11), slice(None)))
    done.update(zip(tiny_names, _adamw_pieces(
        reduced["tiny"], [(w[k], m[k], v[k]) + at for k, at in zip(tiny_names, tiny_at)], "adamw_tiny")))
    vector_rows, loss_row = _vector_rows()
    pieces = [as2d("pool_w", (POOL_W_ROWS, SMALL_COLS)) + (slice(0, POOL_W_ROWS), slice(None))]
    pieces += [as2d(k, (1, -1)) + (vector_rows[k],) for k in _VECTORS]
    done.update(zip(("pool_w",) + _VECTORS, _adamw_pieces(reduced["small"], pieces, "adamw_small")))
    done["g_mix_pre"] = _adamw_pieces(g_mix_pre_parts, [as2d("g_mix_pre", (1, -1)) + (0,)], "adamw_g_mix_pre")[0]

    result = [reduced["small"][loss_row, 0], grad_x[None]]
    for kind in range(4):
        result += [done[k][kind].reshape(args[k].shape) for k in _WEIGHTS]
    return tuple(result)
```

```python
import functools

import jax
import jax.numpy as jnp
from jax import lax
from jax.experimental import pallas as pl
from jax.experimental.pallas import tpu as pltpu

F32 = jnp.float32
BF16 = jnp.bfloat16

D_MODEL = 1024
POOL_WINDOWS = (2, 4, 8, 16)
POOL_GROUPS = 4
POOL_WIDTH = 512
POOL_GROUP_DIM = 128
RNN_HEADS = 16
RNN_HEAD_DIM = 64
GATE_BLOCK = 256
GATE_BLOCKS = D_MODEL // GATE_BLOCK
LRU_C = 8.0
D_FF = 3072
PLE_DIM = 256
RMS_EPS = 1e-6
IN_TOTAL = 4608
N_CHIPS = 4
IN_SHARD = IN_TOTAL // N_CHIPS
UP_SHARD = 2 * D_FF // N_CHIPS
Z_GROUPS = ((0, 512), (512, 1536), (1536, 2560), (2560, 3584), (3584, 4608))
POOL_HALO = 16
CONV_HALO = 8

ADAM_LR = 0.001
ADAM_B1 = 0.9
ADAM_B2 = 0.999
ADAM_EPS = 1e-08
ADAM_WD = 0.01
ADAM_STEP = 10

VMEM_LIMIT = 56 * 1024 * 1024
MESH = pl.DeviceIdType.MESH

_GELU_C = 0.7978845608028654
_GELU_A = 0.044715


def _dot(a, b):
    return jnp.dot(a.astype(BF16), b.astype(BF16), preferred_element_type=F32)


def _dot_nt(a, b):
    return lax.dot_general(a.astype(BF16), b.astype(BF16), (((1,), (1,)), ((), ())), preferred_element_type=F32)


def _dot_tn(a, b):
    return lax.dot_general(a.astype(BF16), b.astype(BF16), (((0,), (0,)), ((), ())), preferred_element_type=F32)


def _overlaps(group):
    a, b = Z_GROUPS[group]
    found = []
    for j in range(N_CHIPS):
        lo, hi = max(a, j * IN_SHARD), min(b, (j + 1) * IN_SHARD)
        if lo < hi:
            found.append((j, slice(lo - j * IN_SHARD, hi - j * IN_SHARD), slice(lo - a, hi - a)))
    return found


def _rms_fwd(x, g):
    r = lax.rsqrt(jnp.mean(x * x, axis=-1, keepdims=True) + RMS_EPS)
    xh = x * r
    return xh * g, xh, r


def _rms_bwd(xh, r, g, dy):
    dxh = dy * g
    dg = jnp.sum(dy * xh, axis=0, keepdims=True)
    dx = r * (dxh - xh * jnp.mean(dxh * xh, axis=-1, keepdims=True))
    return dx, dg


def _sigmoid(x):
    return 0.5 * jnp.tanh(0.5 * x) + 0.5


def _gelu(x):
    t = jnp.tanh(_GELU_C * (x + _GELU_A * x * x * x))
    return 0.5 * x * (1.0 + t), t


def _gelu_grad(x, t):
    return 0.5 * (1.0 + t) + 0.5 * x * (1.0 - t * t) * _GELU_C * (1.0 + 3.0 * _GELU_A * x * x)


def _softplus_neg(lam):
    nl = -lam
    return jnp.maximum(nl, 0.0) + jnp.log(1.0 + jnp.exp(-jnp.abs(nl)))


def _lru_coeffs(r, lam, first_row):
    c8 = LRU_C * _softplus_neg(lam)
    la = -(c8 * r)
    a = jnp.exp(la)
    m2 = jnp.tanh(-la) * (1.0 + a * a)
    mult = jnp.where(first_row, 1.0, jnp.sqrt(m2))
    return c8, a, m2, mult


SUBLANES = 8


def _scan_fwd(a, u, carry):
    n = a.shape[0]
    sub = lax.broadcasted_iota(jnp.int32, (n, 1), 0) % SUBLANES
    acc_a, acc_h = a, u
    for s in (1, 2, 4):
        m = sub >= s
        h_s = jnp.where(m, pltpu.roll(acc_h, s, 0), 0.0)
        a_s = jnp.where(m, pltpu.roll(acc_a, s, 0), 1.0)
        acc_h = acc_a * h_s + acc_h
        acc_a = acc_a * a_s
    out = []
    for g in range(n // SUBLANES):
        rows = slice(g * SUBLANES, (g + 1) * SUBLANES)
        out.append(acc_h[rows] + acc_a[rows] * carry)
        carry = out[-1][SUBLANES - 1:SUBLANES]
    return jnp.concatenate(out, axis=0)


def _scan_bwd(b, g, carry):
    n = b.shape[0]
    sub = lax.broadcasted_iota(jnp.int32, (n, 1), 0) % SUBLANES
    acc_b, acc_l = b, g
    for s in (1, 2, 4):
        m = sub < SUBLANES - s
        l_s = jnp.where(m, pltpu.roll(acc_l, n - s, 0), 0.0)
        b_s = jnp.where(m, pltpu.roll(acc_b, n - s, 0), 1.0)
        acc_l = acc_b * l_s + acc_l
        acc_b = acc_b * b_s
    out = [None] * (n // SUBLANES)
    for g in reversed(range(n // SUBLANES)):
        rows = slice(g * SUBLANES, (g + 1) * SUBLANES)
        out[g] = acc_l[rows] + acc_b[rows] * carry
        carry = out[g][0:1]
    return jnp.concatenate(out, axis=0)


def _shift_down(ext, k, halo):
    return pltpu.roll(ext, k, 0)[halo:] if k else ext[halo:]


def _shift_up(ext, k, ts):
    return pltpu.roll(ext, ext.shape[0] - k, 0)[:ts] if k else ext[:ts]


def _rows(ts, width, nt=None, col=0):
    if nt is None:
        return pl.BlockSpec((ts, width), lambda i: (i, col))
    return pl.BlockSpec((ts, width), lambda i: (nt - 1 - i, col))


def _resident(shape):
    zeros = (0,) * len(shape)
    return pl.BlockSpec(shape, lambda i: zeros, pipeline_mode=pl.Buffered(1))


def _acc(shape):
    zeros = (0,) * len(shape)
    return pl.BlockSpec(shape, lambda i: zeros)


def _params():
    return pltpu.CompilerParams(dimension_semantics=("arbitrary",), vmem_limit_bytes=VMEM_LIMIT)


def _sds(shape, dtype=F32):
    return jax.ShapeDtypeStruct(shape, dtype)


class _Task:
    def __init__(self, ins, out_shapes, aliases, sems, start, finish, relays=()):
        self.ins, self.out_shapes, self.aliases, self.sems = list(ins), list(out_shapes), dict(aliases), list(sems)
        self.start, self.relays, self.finish = start, list(relays), finish


def _call(body, name, grid, in_specs, out_specs, out_shape, scratch_shapes, args, tasks=()):
    n_in, n_out, n_scr = len(in_specs), len(out_specs), len(scratch_shapes)
    t_in = [len(t.ins) for t in tasks]
    t_out = [len(t.out_shapes) for t in tasks]
    t_sem = [len(t.sems) for t in tasks]
    steps = 1
    for g in grid:
        steps *= g

    def take(refs, pos, counts):
        groups = []
        for c in counts:
            groups.append(refs[pos:pos + c])
            pos += c
        return groups, pos

    def wrapped(*refs):
        (cin,), pos = take(refs, 0, [n_in])
        tin, pos = take(refs, pos, t_in)
        (cout,), pos = take(refs, pos, [n_out])
        tout, pos = take(refs, pos, t_out)
        (cscr,), pos = take(refs, pos, [n_scr])
        tsem, pos = take(refs, pos, t_sem)
        if not grid:
            for t, a, b, c in zip(tasks, tin, tout, tsem):
                t.start(a, b, c)
            if body is not None:
                body(*cin, *cout, *cscr)
            for t, a, b, c in zip(tasks, tin, tout, tsem):
                for relay, _ in t.relays:
                    relay(a, b, c)
            for t, a, b, c in zip(tasks, tin, tout, tsem):
                t.finish(a, b, c)
            return
        step = pl.program_id(0)
        for axis in range(1, len(grid)):
            step = step * grid[axis] + pl.program_id(axis)
        if tasks:
            @pl.when(step == 0)
            def _():
                for t, a, b, c in zip(tasks, tin, tout, tsem):
                    t.start(a, b, c)

        body(*cin, *cout, *cscr)
        for t, a, b, c in zip(tasks, tin, tout, tsem):
            for relay, before in t.relays:
                pl.when(step == max(steps - 1 - before, 0))(functools.partial(relay, a, b, c))

        if tasks:
            @pl.when(step == steps - 1)
            def _():
                for t, a, b, c in zip(tasks, tin, tout, tsem):
                    t.finish(a, b, c)

    aliases, in_pos, out_pos = {}, n_in, n_out
    for t, ni, no in zip(tasks, t_in, t_out):
        aliases.update({in_pos + a: out_pos + b for a, b in t.aliases.items()})
        in_pos, out_pos = in_pos + ni, out_pos + no
    any_spec = pl.BlockSpec(memory_space=pltpu.HBM)
    kwargs = dict(grid=grid, compiler_params=pltpu.CompilerParams(
        dimension_semantics=("arbitrary",) * len(grid), vmem_limit_bytes=VMEM_LIMIT)) if grid else dict(
        compiler_params=pltpu.CompilerParams(vmem_limit_bytes=VMEM_LIMIT))
    out = pl.pallas_call(
        wrapped, name=name,
        in_specs=list(in_specs) + [any_spec] * sum(t_in),
        out_specs=list(out_specs) + [any_spec] * sum(t_out),
        out_shape=list(out_shape) + [s for t in tasks for s in t.out_shapes],
        scratch_shapes=list(scratch_shapes) + [s for t in tasks for s in t.sems],
        input_output_aliases=aliases, **kwargs,
    )(*args, *[pltpu.with_memory_space_constraint(a, pltpu.HBM) for t in tasks for a in t.ins])
    task_outs, pos = take(list(out), n_out, t_out)
    return list(out[:n_out]), task_outs


def _fwd_in_pool(x, g_pre, w_in, pool_w, pool_scale, w_pool_out, ts, tasks=()):
    s = x.shape[0]

    def body(x_ref, g_ref, win_ref, pw_ref, ps_ref, wpo_ref,
             urx_ref, urg_ref, gp_ref, gr_ref, d_ref, yp_ref, h1_ref, halo_scr):
        i = pl.program_id(0)

        @pl.when(i == 0)
        def _():
            halo_scr[...] = jnp.zeros_like(halo_scr)

        h1, _, _ = _rms_fwd(x_ref[...], g_ref[...])
        h1 = h1.astype(BF16)
        h1_ref[...] = h1
        u = jnp.dot(h1, win_ref[0, :, 0:POOL_WIDTH], preferred_element_type=F32)
        for group, out_ref in zip(range(1, len(Z_GROUPS)), (urx_ref, urg_ref, gp_ref, gr_ref)):
            for j, shard_cols, group_cols in _overlaps(group):
                out_ref[:, group_cols] = jnp.dot(h1, win_ref[j, :, shard_cols], preferred_element_type=F32)
        ext = jnp.concatenate([halo_scr[...], u], axis=0)
        halo_scr[...] = u[ts - POOL_HALO:, :]
        t = i * ts + lax.broadcasted_iota(jnp.int32, (ts, 1), 0)
        y4 = []
        for g, w in enumerate(POOL_WINDOWS):
            lanes = slice(g * POOL_GROUP_DIM, (g + 1) * POOL_GROUP_DIM)
            acc = ext[:, lanes]
            sh = 1
            while sh < w:
                acc = acc + pltpu.roll(acc, sh, 0)
                sh *= 2
            inv = 1.0 / jnp.minimum(t + 1, w).astype(F32)
            dg = acc[POOL_HALO:, :] * inv - u[:, lanes]
            d_ref[:, lanes] = dg.astype(BF16)
            y4.append(_dot(dg, pw_ref[g]))
        ypre = jnp.concatenate(y4, axis=1) * ps_ref[...]
        ypre = ypre.astype(BF16)
        for j in range(N_CHIPS):
            yp_ref[:, j * 256:(j + 1) * 256] = jnp.dot(ypre, wpo_ref[j], preferred_element_type=F32)

    return _call(
        body, "fwd_in_pool", (s // ts,),
        [_rows(ts, D_MODEL), _resident((1, D_MODEL)), _resident(w_in.shape), _resident(pool_w.shape),
         _resident((1, POOL_WIDTH)), _resident(w_pool_out.shape)],
        [_rows(ts, D_MODEL)] * 4 + [_rows(ts, POOL_WIDTH), _rows(ts, D_MODEL), _rows(ts, D_MODEL)],
        [_sds((s, D_MODEL))] * 4 + [_sds((s, POOL_WIDTH), BF16), _sds((s, D_MODEL)), _sds((s, D_MODEL), BF16)],
        [pltpu.VMEM((POOL_HALO, POOL_WIDTH), F32)],
        (x, g_pre, w_in, pool_w, pool_scale, w_pool_out), tasks)


def _fwd_rnn_merge(urx, urg, gp, gr, ypool, x, conv_w, conv_b, wg, bg, lam, w_rg_out, w_o, g_post, ts, tasks=()):
    s = x.shape[0]

    def body(urx_ref, urg_ref, gp_ref, gr_ref, yp_ref, x_ref, cw_ref, cb_ref, wg_ref, bg_ref, lam_ref, wrg_ref, wo_ref,
             gpost_ref, xc_ref, r_ref, ig_ref, h_ref, yr_ref, mo_ref, x1_ref, gl_ref, gg_ref, sp_ref, sr_ref,
             halo_scr, carry_scr):
        i = pl.program_id(0)

        @pl.when(i == 0)
        def _():
            halo_scr[...] = jnp.zeros_like(halo_scr)
            carry_scr[...] = jnp.zeros_like(carry_scr)

        urx_v = urx_ref[...]
        ext = jnp.concatenate([halo_scr[...], urx_v], axis=0)
        halo_scr[...] = urx_v[ts - CONV_HALO:, :]
        cw = cw_ref[...]
        xc = (cb_ref[...] + cw[3:4] * urx_v + cw[2:3] * _shift_down(ext, 1, CONV_HALO)
              + cw[1:2] * _shift_down(ext, 2, CONV_HALO) + cw[0:1] * _shift_down(ext, 3, CONV_HALO))
        xc_ref[...] = xc
        xcb = xc.astype(BF16)
        lin = []
        for gate in range(2):
            parts = [jnp.dot(xcb[:, q * GATE_BLOCK:(q + 1) * GATE_BLOCK], wg_ref[gate, q], preferred_element_type=F32)
                     for q in range(GATE_BLOCKS)]
            lin.append(jnp.concatenate(parts, axis=1) + bg_ref[gate:gate + 1, :])
        r = _sigmoid(lin[0])
        ig = _sigmoid(lin[1])
        r_ref[...] = r
        ig_ref[...] = ig
        first_row = (i * ts + lax.broadcasted_iota(jnp.int32, (ts, 1), 0)) == 0
        _, a, _, mult = _lru_coeffs(r, lam_ref[...], first_row)
        h = _scan_fwd(a, mult * ig * xc, carry_scr[0:1, :])
        carry_scr[0:1, :] = h[ts - 1:ts, :]
        h_ref[...] = h
        urg_v = urg_ref[...]
        gl, t = _gelu(urg_v)
        gl_ref[...] = gl.astype(BF16)
        gg_ref[...] = _gelu_grad(urg_v, t).astype(BF16)
        yr = _dot(h * gl, wrg_ref[...])
        yr_ref[...] = yr
        sp = _sigmoid(gp_ref[...])
        sr = _sigmoid(gr_ref[...])
        sp_ref[...] = sp.astype(BF16)
        sr_ref[...] = sr.astype(BF16)
        merged = sp * yp_ref[...] + sr * yr
        mo = _dot(merged, wo_ref[...])
        mo_ref[...] = mo
        y, _, _ = _rms_fwd(mo, gpost_ref[...])
        x1_ref[...] = x_ref[...] + y

    row = _rows(ts, D_MODEL)
    return _call(
        body, "fwd_rnn_merge", (s // ts,),
        [row] * 6 + [_resident(conv_w.shape), _resident((1, D_MODEL)), _resident(wg.shape), _resident(bg.shape),
                     _resident((1, D_MODEL)), _resident(w_rg_out.shape), _resident(w_o.shape), _resident((1, D_MODEL))],
        [row] * 11, [_sds((s, D_MODEL))] * 7 + [_sds((s, D_MODEL), BF16)] * 4,
        [pltpu.VMEM((CONV_HALO, D_MODEL), F32), pltpu.VMEM((8, D_MODEL), F32)],
        (urx, urg, gp, gr, ypool, x, conv_w, conv_b, wg, bg, lam, w_rg_out, w_o, g_post), tasks)


def _fwd_ffn(x1, g_pre, w_up, fcw, fcb, w_down, g_post, ts, tasks=()):
    s = x1.shape[0]

    def body(x1_ref, g_ref, wup_ref, fcw_ref, fcb_ref, wd_ref, gpost_ref,
             up_ref, gl_ref, gg_ref, h2_ref, dn_ref, x2_ref, up_scr, halo_scr):
        i = pl.program_id(0)

        @pl.when(i == 0)
        def _():
            halo_scr[...] = jnp.zeros_like(halo_scr)

        x1_v = x1_ref[...]
        h2, _, _ = _rms_fwd(x1_v, g_ref[...])
        h2 = h2.astype(BF16)
        h2_ref[...] = h2
        for j in range(N_CHIPS):
            up_scr[:, j * UP_SHARD:(j + 1) * UP_SHARD] = jnp.dot(h2, wup_ref[j], preferred_element_type=F32)
        up_ref[...] = up_scr[...].astype(BF16)
        ug = up_scr[:, 0:D_FF]
        ext = jnp.concatenate([halo_scr[...], ug], axis=0)
        halo_scr[...] = ug[ts - CONV_HALO:, :]
        w = fcw_ref[...]
        gh = (fcb_ref[...] + w[2:3] * ug + w[1:2] * _shift_down(ext, 1, CONV_HALO)
              + w[0:1] * _shift_down(ext, 2, CONV_HALO))
        gl, t = _gelu(gh)
        gl_ref[...] = gl.astype(BF16)
        gg_ref[...] = _gelu_grad(gh, t).astype(BF16)
        dn = _dot(gl * up_scr[:, D_FF:], wd_ref[...])
        dn_ref[...] = dn
        y, _, _ = _rms_fwd(dn, gpost_ref[...])
        x2_ref[...] = x1_v + y

    row = _rows(ts, D_MODEL)
    return _call(
        body, "fwd_ffn", (s // ts,),
        [row, _resident((1, D_MODEL)), _resident(w_up.shape), _resident(fcw.shape), _resident((1, D_FF)),
         _resident(w_down.shape), _resident((1, D_MODEL))],
        [_rows(ts, 2 * D_FF), _rows(ts, D_FF), _rows(ts, D_FF), row, row, row],
        [_sds((s, 2 * D_FF), BF16), _sds((s, D_FF), BF16), _sds((s, D_FF), BF16), _sds((s, D_MODEL), BF16),
         _sds((s, D_MODEL)), _sds((s, D_MODEL))],
        [pltpu.VMEM((ts, 2 * D_FF), F32), pltpu.VMEM((CONV_HALO, D_FF), F32)],
        (x1, g_pre, w_up, fcw, fcb, w_down, g_post), tasks)


def _ple_loss(x2, p, tgt, g_gate, w_gate, w_proj, g_post, ts):
    s = x2.shape[0]

    def body(x2_ref, p_ref, t_ref, gg_ref, wg_ref, wp_ref, gp_ref, dx2_ref, loss_ref, dwg_ref, dwp_ref, dgg_ref, dgp_ref):
        @pl.when(pl.program_id(0) == 0)
        def _():
            loss_ref[...] = jnp.zeros_like(loss_ref)
            dwg_ref[...] = jnp.zeros_like(dwg_ref)
            dwp_ref[...] = jnp.zeros_like(dwp_ref)
            dgg_ref[...] = jnp.zeros_like(dgg_ref)
            dgp_ref[...] = jnp.zeros_like(dgp_ref)

        x2_v = x2_ref[...]
        n3, xh3, r3 = _rms_fwd(x2_v, gg_ref[...])
        pg = _sigmoid(_dot(n3, wg_ref[...]))
        pb = p_ref[...].astype(BF16)
        q = jnp.concatenate([jnp.dot(pb, wp_ref[j], preferred_element_type=F32) for j in range(N_CHIPS)], axis=1)
        ple, qh, rq = _rms_fwd(q, gp_ref[...])
        e = x2_v + pg * ple - t_ref[...]
        loss_ref[...] += 0.5 * jnp.sum(jnp.mean(e * e, axis=-1, keepdims=True), axis=0, keepdims=True)
        dy = e * (1.0 / D_MODEL)
        dpgl = dy * ple * pg * (1.0 - pg)
        dwg_ref[...] += _dot_tn(n3, dpgl)
        dx3, dgg = _rms_bwd(xh3, r3, gg_ref[...], _dot_nt(dpgl, wg_ref[...]))
        dgg_ref[...] += dgg
        dq, dgp = _rms_bwd(qh, rq, gp_ref[...], dy * pg)
        dgp_ref[...] += dgp
        for j in range(N_CHIPS):
            dwp_ref[j] += _dot_tn(pb, dq[:, j * 256:(j + 1) * 256])
        dx2_ref[...] = dy + dx3

    row = _rows(ts, D_MODEL)
    vec = _acc((1, D_MODEL))
    return pl.pallas_call(
        body, name="ple_loss", grid=(s // ts,),
        in_specs=[row, _rows(ts, PLE_DIM), row, _resident((1, D_MODEL)), _resident(w_gate.shape), _resident(w_proj.shape),
                  _resident((1, D_MODEL))],
        out_specs=[row, _acc((1, 128)), _acc(w_gate.shape), _acc(w_proj.shape), vec, vec],
        out_shape=[_sds((s, D_MODEL)), _sds((1, 128)), _sds(w_gate.shape), _sds(w_proj.shape), _sds((1, D_MODEL)),
                   _sds((1, D_MODEL))],
        compiler_params=_params(),
    )(x2, p, tgt, g_gate, w_gate, w_proj, g_post)


def _bwd_ffn_down(dx2, dn, up, gl, gg, fcw, w_down, g_post, ts):
    s = dx2.shape[0]
    nt = s // ts

    def body(dx2_ref, dn_ref, up_ref, gl_ref, gg_ref, fcw_ref, wd_ref, gpost_ref,
             dup_ref, dwd_ref, dfcw_ref, dfcb_ref, dgp_ref, carry_scr):
        i = pl.program_id(0)

        @pl.when(i == 0)
        def _():
            carry_scr[...] = jnp.zeros_like(carry_scr)
            dwd_ref[...] = jnp.zeros_like(dwd_ref)
            dfcw_ref[...] = jnp.zeros_like(dfcw_ref)
            dfcb_ref[...] = jnp.zeros_like(dfcb_ref)
            dgp_ref[...] = jnp.zeros_like(dgp_ref)

        _, xh, r = _rms_fwd(dn_ref[...], gpost_ref[...])
        ddn, dgp = _rms_bwd(xh, r, gpost_ref[...], dx2_ref[...])
        dgp_ref[...] += dgp
        dhid = _dot_nt(ddn, wd_ref[...])
        ug = up_ref[:, 0:D_FF].astype(F32)
        uv = up_ref[:, D_FF:].astype(F32)
        gl = gl_ref[...].astype(F32)
        w = fcw_ref[...]
        dwd_ref[...] += _dot_tn(gl * uv, ddn)
        dgh = dhid * uv * gg_ref[...].astype(F32)
        dup_ref[:, D_FF:] = (dhid * gl).astype(BF16)
        extd = jnp.concatenate([dgh, carry_scr[...]], axis=0)
        carry_scr[...] = dgh[0:CONV_HALO, :]
        d1 = _shift_up(extd, 1, ts)
        d2 = _shift_up(extd, 2, ts)
        dup_ref[:, 0:D_FF] = (w[2:3] * dgh + w[1:2] * d1 + w[0:1] * d2).astype(BF16)
        dfcw_ref[2:3, :] += jnp.sum(ug * dgh, axis=0, keepdims=True)
        dfcw_ref[1:2, :] += jnp.sum(ug * d1, axis=0, keepdims=True)
        dfcw_ref[0:1, :] += jnp.sum(ug * d2, axis=0, keepdims=True)
        dfcb_ref[...] += jnp.sum(dgh, axis=0, keepdims=True)

    row = _rows(ts, D_MODEL, nt)
    wide = _rows(ts, D_FF, nt)
    return pl.pallas_call(
        body, name="bwd_ffn_down", grid=(nt,),
        in_specs=[row, row, _rows(ts, 2 * D_FF, nt), wide, wide, _resident(fcw.shape), _resident(w_down.shape),
                  _resident((1, D_MODEL))],
        out_specs=[_rows(ts, 2 * D_FF, nt), _acc(w_down.shape), _acc(fcw.shape), _acc((1, D_FF)), _acc((1, D_MODEL))],
        out_shape=[_sds((s, 2 * D_FF), BF16), _sds(w_down.shape), _sds(fcw.shape), _sds((1, D_FF)), _sds((1, D_MODEL))],
        scratch_shapes=[pltpu.VMEM((CONV_HALO, D_FF), F32)],
        compiler_params=_params(),
    )(dx2, dn, up, gl, gg, fcw, w_down, g_post)


def _bwd_ffn_up(dup, x1, dx2, g_pre, w_up, ts, tasks=()):
    s = x1.shape[0]

    def body(dup_ref, x1_ref, dx2_ref, g_ref, wup_ref, dx1_ref, dg_ref):
        @pl.when(pl.program_id(0) == 0)
        def _():
            dg_ref[...] = jnp.zeros_like(dg_ref)

        _, xh, r = _rms_fwd(x1_ref[...], g_ref[...])
        dh2 = _dot_nt(dup_ref[:, 0:UP_SHARD], wup_ref[0])
        for j in range(1, N_CHIPS):
            dh2 = dh2 + _dot_nt(dup_ref[:, j * UP_SHARD:(j + 1) * UP_SHARD], wup_ref[j])
        dx, dg = _rms_bwd(xh, r, g_ref[...], dh2)
        dg_ref[...] += dg
        dx1_ref[...] = dx2_ref[...] + dx

    row = _rows(ts, D_MODEL)
    return _call(
        body, "bwd_ffn_up", (s // ts,),
        [_rows(ts, 2 * D_FF), row, row, _resident((1, D_MODEL)), _resident(w_up.shape)],
        [row, _acc((1, D_MODEL))], [_sds((s, D_MODEL)), _sds((1, D_MODEL))], [],
        (dup, x1, dx2, g_pre, w_up), tasks)


def _dw_up(h2, dup, ts, tasks=()):
    s = h2.shape[0]
    ts = min(DW_TILES * ts, s)

    def body(h2_ref, dup_ref, out_ref):
        @pl.when(pl.program_id(1) == 0)
        def _():
            out_ref[...] = jnp.zeros_like(out_ref)

        out_ref[0] += _dot_tn(h2_ref[...], dup_ref[...])

    return _call(
        body, "dw_up", (N_CHIPS, s // ts),
        [pl.BlockSpec((ts, D_MODEL), lambda j, i: (i, 0)), pl.BlockSpec((ts, UP_SHARD), lambda j, i: (i, j))],
        [pl.BlockSpec((1, D_MODEL, UP_SHARD), lambda j, i: (j, 0, 0))], [_sds((N_CHIPS, D_MODEL, UP_SHARD))], [],
        (h2, dup), tasks)


def _bwd_merge(dx1, mo, sp, sr, ypool, yrnn, g_post, w_o, ts, tasks=()):
    s = dx1.shape[0]

    def body(dx1_ref, mo_ref, sp_ref, sr_ref, yp_ref, yr_ref, g_ref, wo_ref,
             dgp_ref, dgr_ref, dyp_ref, dyr_ref, dwo_ref, dg_ref):
        @pl.when(pl.program_id(0) == 0)
        def _():
            dwo_ref[...] = jnp.zeros_like(dwo_ref)
            dg_ref[...] = jnp.zeros_like(dg_ref)

        _, xh, r = _rms_fwd(mo_ref[...], g_ref[...])
        dmo, dg = _rms_bwd(xh, r, g_ref[...], dx1_ref[...])
        dg_ref[...] += dg
        dmerged = _dot_nt(dmo, wo_ref[...])
        sp = sp_ref[...].astype(F32)
        sr = sr_ref[...].astype(F32)
        yp = yp_ref[...]
        yr = yr_ref[...]
        dwo_ref[...] += _dot_tn(sp * yp + sr * yr, dmo)
        dgp_ref[...] = (dmerged * yp * sp * (1.0 - sp)).astype(BF16)
        dgr_ref[...] = (dmerged * yr * sr * (1.0 - sr)).astype(BF16)
        dyp_ref[...] = (dmerged * sp).astype(BF16)
        dyr_ref[...] = (dmerged * sr).astype(BF16)

    row = _rows(ts, D_MODEL)
    return _call(
        body, "bwd_merge", (s // ts,),
        [row] * 6 + [_resident((1, D_MODEL)), _resident(w_o.shape)],
        [row] * 4 + [_acc(w_o.shape), _acc((1, D_MODEL))],
        [_sds((s, D_MODEL), BF16)] * 4 + [_sds(w_o.shape), _sds((1, D_MODEL))], [],
        (dx1, mo, sp, sr, ypool, yrnn, g_post, w_o), tasks)


def _bwd_rnn(dyr, urx, gl, gg, xc, r, ig, h, conv_w, wg, lam, w_rg_out, ts, tasks=()):
    s = urx.shape[0]
    nt = s // ts
    halo_blocks = ts // CONV_HALO

    def body(dyr_ref, urx_ref, gl_ref, gg_ref, xc_ref, r_ref, ig_ref, h_ref, hh_ref, cw_ref, wg_ref, lam_ref, wrg_ref,
             durx_ref, durg_ref, dwrg_ref, dwg_ref, dcw_ref, dcb_ref, dbg_ref, dlam_ref, mu_scr, carry_scr):
        i = pl.program_id(0)
        k = nt - 1 - i

        @pl.when(i == 0)
        def _():
            mu_scr[...] = jnp.zeros_like(mu_scr)
            carry_scr[...] = jnp.zeros_like(carry_scr)
            dwrg_ref[...] = jnp.zeros_like(dwrg_ref)
            dwg_ref[...] = jnp.zeros_like(dwg_ref)
            dcw_ref[...] = jnp.zeros_like(dcw_ref)
            dcb_ref[...] = jnp.zeros_like(dcb_ref)
            dbg_ref[...] = jnp.zeros_like(dbg_ref)
            dlam_ref[...] = jnp.zeros_like(dlam_ref)

        row = lax.broadcasted_iota(jnp.int32, (ts, 1), 0)
        first_row = (k * ts + row) == 0
        h = h_ref[...]
        dyr_v = dyr_ref[...]
        dhr = _dot_nt(dyr_v, wrg_ref[...])
        gl = gl_ref[...].astype(F32)
        dwrg_ref[...] += _dot_tn(h * gl, dyr_v)
        durg_ref[...] = (dhr * h * gg_ref[...].astype(F32)).astype(BF16)
        r_v = r_ref[...]
        ig_v = ig_ref[...]
        xc_v = xc_ref[...]
        lam_v = lam_ref[...]
        c8, a, m2, mult = _lru_coeffs(r_v, lam_v, first_row)
        b = jnp.where(row == ts - 1, 1.0, pltpu.roll(a, ts - 1, 0))
        lt = _scan_bwd(b, dhr * gl, mu_scr[0:1, :])
        mu_scr[0:1, :] = a[0:1, :] * lt[0:1, :]
        h_before = jnp.where(k > 0, hh_ref[CONV_HALO - 1:CONV_HALO, :], 0.0)
        hprev = jnp.where(row == 0, h_before, pltpu.roll(h, 1, 0))
        dmult = lt * ig_v * xc_v
        da = lt * hprev - jnp.where(first_row, 0.0, dmult * a * lax.rsqrt(m2))
        dla = da * a
        dlam_ref[...] += jnp.sum(dla * r_v, axis=0, keepdims=True)
        dlr = (dla * (-c8)) * r_v * (1.0 - r_v)
        dli = (lt * mult * xc_v) * ig_v * (1.0 - ig_v)
        dbg_ref[0:1, :] += jnp.sum(dlr, axis=0, keepdims=True)
        dbg_ref[1:2, :] += jnp.sum(dli, axis=0, keepdims=True)
        xcb = xc_v.astype(BF16)
        parts = []
        for q in range(GATE_BLOCKS):
            blk = slice(q * GATE_BLOCK, (q + 1) * GATE_BLOCK)
            dlr_q = dlr[:, blk].astype(BF16)
            dli_q = dli[:, blk].astype(BF16)
            parts.append(_dot_nt(dlr_q, wg_ref[0, q]) + _dot_nt(dli_q, wg_ref[1, q]))
            dwg_ref[0, q] += _dot_tn(xcb[:, blk], dlr_q)
            dwg_ref[1, q] += _dot_tn(xcb[:, blk], dli_q)
        dxc = lt * mult * ig_v + jnp.concatenate(parts, axis=1)
        extd = jnp.concatenate([dxc, carry_scr[...]], axis=0)
        carry_scr[...] = dxc[0:CONV_HALO, :]
        cw = cw_ref[...]
        urx_v = urx_ref[...]
        durx = cw[3:4] * dxc
        dcw_ref[3:4, :] += jnp.sum(urx_v * dxc, axis=0, keepdims=True)
        for j in (1, 2, 3):
            dj = _shift_up(extd, j, ts)
            durx = durx + cw[3 - j:4 - j] * dj
            dcw_ref[3 - j:4 - j, :] += jnp.sum(urx_v * dj, axis=0, keepdims=True)
        durx_ref[...] = durx.astype(BF16)
        dcb_ref[...] += jnp.sum(dxc, axis=0, keepdims=True)

        @pl.when(i == nt - 1)
        def _():
            dlam_ref[...] = dlam_ref[...] * (LRU_C * jax.nn.sigmoid(-lam_v))

    row_spec = _rows(ts, D_MODEL, nt)
    halo_spec = pl.BlockSpec((CONV_HALO, D_MODEL), lambda i: (jnp.maximum((nt - 1 - i) * halo_blocks - 1, 0), 0))
    vec = _acc((1, D_MODEL))
    return _call(
        body, "bwd_rnn", (nt,),
        [row_spec] * 8 + [halo_spec, _resident(conv_w.shape), _resident(wg.shape), _resident((1, D_MODEL)),
                          _resident(w_rg_out.shape)],
        [row_spec, row_spec, _acc(w_rg_out.shape), _acc(wg.shape), _acc(conv_w.shape), vec, _acc((2, D_MODEL)), vec],
        [_sds((s, D_MODEL), BF16), _sds((s, D_MODEL), BF16), _sds(w_rg_out.shape), _sds(wg.shape), _sds(conv_w.shape),
         _sds((1, D_MODEL)), _sds((2, D_MODEL)), _sds((1, D_MODEL))],
        [pltpu.VMEM((8, D_MODEL), F32), pltpu.VMEM((CONV_HALO, D_MODEL), F32)],
        (dyr, urx, gl, gg, xc, r, ig, h, h, conv_w, wg, lam, w_rg_out), tasks)


def _bwd_pool(dyp, d, pool_w, pool_scale, w_pool_out, ts, tasks=()):
    s = d.shape[0]
    nt = s // ts

    def body(dyp_ref, d_ref, pw_ref, ps_ref, wpo_ref, dzp_ref, dwpo_ref, dpw_ref, dps_ref, carry_scr):
        i = pl.program_id(0)
        k = nt - 1 - i

        @pl.when(i == 0)
        def _():
            carry_scr[...] = jnp.zeros_like(carry_scr)
            dwpo_ref[...] = jnp.zeros_like(dwpo_ref)
            dpw_ref[...] = jnp.zeros_like(dpw_ref)
            dps_ref[...] = jnp.zeros_like(dps_ref)

        dyp_v = dyp_ref[...]
        d_v = d_ref[...]
        ps = ps_ref[...]
        dypre = _dot_nt(dyp_v[:, 0:256], wpo_ref[0])
        for j in range(1, N_CHIPS):
            dypre = dypre + _dot_nt(dyp_v[:, j * 256:(j + 1) * 256], wpo_ref[j])
        y4 = jnp.concatenate([_dot(d_v[:, g * 128:(g + 1) * 128], pw_ref[g]) for g in range(POOL_GROUPS)], axis=1)
        ypre = (y4 * ps).astype(BF16)
        for j in range(N_CHIPS):
            dwpo_ref[j] += _dot_tn(ypre, dyp_v[:, j * 256:(j + 1) * 256])
        dps_ref[...] += jnp.sum(dypre * y4, axis=0, keepdims=True)
        dy4 = dypre * ps
        t = k * ts + lax.broadcasted_iota(jnp.int32, (ts, 1), 0)
        for g, w in enumerate(POOL_WINDOWS):
            lanes = slice(g * POOL_GROUP_DIM, (g + 1) * POOL_GROUP_DIM)
            dd = _dot_nt(dy4[:, lanes], pw_ref[g])
            dpw_ref[g] += _dot_tn(d_v[:, lanes], dy4[:, lanes])
            e = dd * (1.0 / jnp.minimum(t + 1, w).astype(F32))
            acc = jnp.concatenate([e, carry_scr[:, lanes]], axis=0)
            carry_scr[:, lanes] = e[0:POOL_HALO, :]
            n = ts + POOL_HALO
            sh = 1
            while sh < w:
                acc = acc + pltpu.roll(acc, n - sh, 0)
                sh *= 2
            dzp_ref[:, lanes] = (acc[:ts, :] - dd).astype(BF16)

    return _call(
        body, "bwd_pool", (nt,),
        [_rows(ts, D_MODEL, nt), _rows(ts, POOL_WIDTH, nt), _resident(pool_w.shape), _resident((1, POOL_WIDTH)),
         _resident(w_pool_out.shape)],
        [_rows(ts, POOL_WIDTH, nt), _acc(w_pool_out.shape), _acc(pool_w.shape), _acc((1, POOL_WIDTH))],
        [_sds((s, POOL_WIDTH), BF16), _sds(w_pool_out.shape), _sds(pool_w.shape), _sds((1, POOL_WIDTH))],
        [pltpu.VMEM((POOL_HALO, POOL_WIDTH), F32)],
        (dyp, d, pool_w, pool_scale, w_pool_out), tasks)


def _assemble_dz(dz_scr, dzp_ref, durx_ref, durg_ref, dgp_ref, dgr_ref):
    dz_scr[:, 0:512] = dzp_ref[...]
    dz_scr[:, 512:1536] = durx_ref[...]
    dz_scr[:, 1536:2560] = durg_ref[...]
    dz_scr[:, 2560:3584] = dgp_ref[...]
    dz_scr[:, 3584:4608] = dgr_ref[...]


def _dw_in(h1, dzp, durx, durg, dgp, dgr, ts, tasks=()):
    s = h1.shape[0]
    ts = min(2 * ts, s)
    nt = s // ts
    half = D_MODEL // 2

    def body(h1_ref, dzp_ref, durx_ref, durg_ref, dgp_ref, dgr_ref, send_ref, acc_ref, theirs_ref, mine_ref, got_ref, sems):
        o, i = pl.program_id(0), pl.program_id(1)

        @pl.when((o == 0) & (i == 0))
        def _():
            theirs_ref[...] = jnp.zeros_like(theirs_ref)
            mine_ref[...] = jnp.zeros_like(mine_ref)

        groups = (dzp_ref, durx_ref, durg_ref, dgp_ref, dgr_ref)
        for out_ref, which in ((theirs_ref, 0), (mine_ref, 1)):
            @pl.when(o == which)
            def _():
                for group, dz_ref in enumerate(groups):
                    for j, shard_cols, group_cols in _overlaps(group):
                        out_ref[j, :, shard_cols] += _dot_tn(h1_ref[...], dz_ref[:, group_cols])

        x, y, c, me, _ = _place()
        send = _remote(theirs_ref, got_ref, sems.at[0], sems.at[1], (x, y, 1 - c))
        pl.when((o == 1) & (i == 0))(send.start)

        @pl.when((o == 1) & (i == nt - 1))
        def _():
            send.wait()
            for k in range(N_CHIPS):
                part = (mine_ref[k] + got_ref[k]).astype(BF16)
                send_ref[k] = part

                @pl.when(me == k)
                def _():
                    acc_ref[k] = part

    def h1_cols(o, i):
        c = lax.axis_index("c")
        return i, jnp.where(o == 0, 1 - c, c)

    rows = lambda width: pl.BlockSpec((ts, width), lambda o, i: (i, 0))
    shape = (N_CHIPS, half, IN_SHARD)
    whole = pl.BlockSpec(shape, lambda o, i: (0, 0, 0))
    return _call(
        body, "dw_in", (2, nt), [pl.BlockSpec((ts, half), h1_cols), rows(POOL_WIDTH)] + [rows(D_MODEL)] * 4,
        [whole, whole], [_sds(shape, BF16)] * 2,
        [pltpu.VMEM(shape, F32)] * 3 + [pltpu.SemaphoreType.DMA((2,))], (h1, dzp, durx, durg, dgp, dgr), tasks)


def _bwd_in(dzp, durx, durg, dgp, dgr, x, dx1, g_pre, w_in, ts, tasks=()):
    s = x.shape[0]

    def body(dzp_ref, durx_ref, durg_ref, dgp_ref, dgr_ref, x_ref, dx1_ref, g_ref, win_ref, gx_ref, dg_ref, dz_scr):
        @pl.when(pl.program_id(0) == 0)
        def _():
            dg_ref[...] = jnp.zeros_like(dg_ref)

        _assemble_dz(dz_scr, dzp_ref, durx_ref, durg_ref, dgp_ref, dgr_ref)
        _, xh, r = _rms_fwd(x_ref[...], g_ref[...])
        dh1 = _dot_nt(dz_scr[:, 0:IN_SHARD], win_ref[0])
        for j in range(1, N_CHIPS):
            dh1 = dh1 + _dot_nt(dz_scr[:, j * IN_SHARD:(j + 1) * IN_SHARD], win_ref[j])
        dx, dg = _rms_bwd(xh, r, g_ref[...], dh1)
        dg_ref[...] += dg
        gx_ref[...] = dx1_ref[...] + dx

    row = _rows(ts, D_MODEL)
    return _call(
        body, "bwd_in", (s // ts,),
        [_rows(ts, POOL_WIDTH)] + [row] * 6 + [_resident((1, D_MODEL)), _resident(w_in.shape)],
        [row, _acc((1, D_MODEL))], [_sds((s, D_MODEL)), _sds((1, D_MODEL))],
        [pltpu.VMEM((ts, IN_TOTAL), BF16)], (dzp, durx, durg, dgp, dgr, x, dx1, g_pre, w_in), tasks)


def _place():
    x, y, c = lax.axis_index("x"), lax.axis_index("y"), lax.axis_index("c")
    others = [(1 - x, y), (x, 1 - y), (1 - x, 1 - y)]
    return x, y, c, 2 * x + y, others


def _remote(src, dst, send_sem, recv_sem, to):
    return pltpu.make_async_remote_copy(src_ref=src, dst_ref=dst, send_sem=send_sem, recv_sem=recv_sem,
                                        device_id=to, device_id_type=MESH)


def _own_slots(ws, dtypes, name, tasks=()):
    n = len(ws)
    hbm = pl.BlockSpec(memory_space=pltpu.HBM)

    def body(*refs):
        srcs, outs, f32_bufs, cast_bufs, sems = refs[:n], refs[n:2 * n], refs[2 * n:3 * n], refs[3 * n:4 * n], refs[4 * n]
        me = _place()[3]
        loads = [pltpu.make_async_copy(srcs[k], f32_bufs[k], sems.at[k, 0]) for k in range(n)]
        stores = [pltpu.make_async_copy(cast_bufs[k], outs[k].at[me], sems.at[k, 1]) for k in range(n)]
        for cp in loads:
            cp.start()
        for k in range(n):
            loads[k].wait()
            cast_bufs[k][...] = f32_bufs[k][...].astype(dtypes[k])
            stores[k].start()
        for cp in stores:
            cp.wait()

    return _call(
        body, name, (), [hbm] * n, [hbm] * n, [_sds((N_CHIPS,) + w.shape, dt) for w, dt in zip(ws, dtypes)],
        [pltpu.VMEM(w.shape, F32) for w in ws] + [pltpu.VMEM(w.shape, dt) for w, dt in zip(ws, dtypes)]
        + [pltpu.SemaphoreType.DMA((n, 2))],
        [pltpu.with_memory_space_constraint(w, pltpu.HBM) for w in ws], tasks)


def _run(tasks, name):
    if isinstance(tasks, _Task):
        return _call(None, name, (), [], [], [], [], (), (tasks,))[1][0]
    return _call(None, name, (), [], [], [], [], (), tuple(tasks))[1]


def _gather_task(bufs, relay_steps=(0, 0)):
    n = len(bufs)
    NBR_X, NBR_Y, QUARTER_VIA_Y, QUARTER_VIA_X, SIB_X, SIB_Y, SIB_DIAG = range(7)

    def parts(out):
        x, y, c, me, _ = _place()
        ah = out.shape[1] // 2
        q = ah // 2 if (ah // 2) % 16 == 0 else ah
        return c * ah, ah, q

    def copy(out, w, k, chip, row0, rows, to, sems):
        slot = out.at[chip, pl.ds(row0, rows)]
        return _remote(slot, slot, sems[0].at[w, k], sems[1].at[w, k], to)

    def plan(out, w, sems):
        x, y, c, me, _ = _place()
        row0, ah, q = parts(out)
        xn, yn, dg = 2 * (1 - x) + y, 2 * x + (1 - y), 2 * (1 - x) + (1 - y)
        to_x, to_y, sib = (1 - x, y, c), (x, 1 - y, c), (x, y, 1 - c)
        other = (1 - c) * ah
        cp = functools.partial(copy, out, w, sems=sems)
        sends = {NBR_X: cp(NBR_X, me, row0, ah, to_x), NBR_Y: cp(NBR_Y, me, row0, ah, to_y),
                 QUARTER_VIA_Y: cp(QUARTER_VIA_Y, xn, row0, q, to_y), SIB_X: cp(SIB_X, xn, row0, ah, sib),
                 SIB_Y: cp(SIB_Y, yn, row0, ah, sib), SIB_DIAG: cp(SIB_DIAG, dg, row0, ah, sib)}
        lands = {NBR_X: cp(NBR_X, xn, row0, ah, to_x), NBR_Y: cp(NBR_Y, yn, row0, ah, to_y),
                 QUARTER_VIA_Y: cp(QUARTER_VIA_Y, dg, row0, q, to_y), SIB_X: cp(SIB_X, xn, other, ah, sib),
                 SIB_Y: cp(SIB_Y, yn, other, ah, sib), SIB_DIAG: cp(SIB_DIAG, dg, other, ah, sib)}
        if q < ah:
            sends[QUARTER_VIA_X] = cp(QUARTER_VIA_X, yn, row0 + q, ah - q, to_x)
            lands[QUARTER_VIA_X] = cp(QUARTER_VIA_X, dg, row0 + q, ah - q, to_x)
        return sends, lands

    def start(ins, outs, sems):
        for w, out in enumerate(outs):
            sends, _ = plan(out, w, sems)
            sends[NBR_X].start()
            sends[NBR_Y].start()

    def pass_neighbours(ins, outs, sems):
        for w, out in enumerate(outs):
            sends, lands = plan(out, w, sems)
            lands[NBR_X].wait_recv()
            sends[QUARTER_VIA_Y].start()
            sends[SIB_X].start()
            lands[NBR_Y].wait_recv()
            if QUARTER_VIA_X in sends:
                sends[QUARTER_VIA_X].start()
            sends[SIB_Y].start()

    def pass_diagonal(ins, outs, sems):
        for w, out in enumerate(outs):
            sends, lands = plan(out, w, sems)
            lands[QUARTER_VIA_Y].wait_recv()
            if QUARTER_VIA_X in lands:
                lands[QUARTER_VIA_X].wait_recv()
            sends[SIB_DIAG].start()

    def finish(ins, outs, sems):
        for w, out in enumerate(outs):
            sends, lands = plan(out, w, sems)
            for k in (SIB_X, SIB_Y, SIB_DIAG):
                lands[k].wait_recv()
        for w, out in enumerate(outs):
            sends, _ = plan(out, w, sems)
            for cp in sends.values():
                cp.wait_send()

    return _Task(bufs, [_sds(b.shape, b.dtype) for b in bufs], {i: i for i in range(n)},
                 [pltpu.SemaphoreType.DMA((n, 7)), pltpu.SemaphoreType.DMA((n, 7))], start, finish,
                 [(pass_neighbours, relay_steps[0]), (pass_diagonal, relay_steps[1])])


def _halves_task(grads):
    n = len(grads)

    def copy(src, out, w, sems):
        x, y, c, _, _ = _place()
        ah = out.shape[1]
        return _remote(src.at[:, pl.ds((1 - c) * ah, ah)], out, sems[0].at[w], sems[1].at[w], (x, y, 1 - c))

    def start(ins, outs, sems):
        for w, (src, out) in enumerate(zip(ins, outs)):
            copy(src, out, w, sems).start()

    def finish(ins, outs, sems):
        for w, (src, out) in enumerate(zip(ins, outs)):
            copy(src, out, w, sems).wait()

    return _Task(grads, [_sds((g.shape[0], g.shape[1] // 2, g.shape[2]), g.dtype) for g in grads], {},
                 [pltpu.SemaphoreType.DMA((n,)), pltpu.SemaphoreType.DMA((n,))], start, finish)


def _exchange_task(sends, accs):
    n = len(accs)
    given = [s for s in sends if s is not None]

    def copies(ins, outs, sems):
        send_refs = iter(ins[:len(given)])
        srcs = [next(send_refs) if s is not None else None for s in sends]
        x, y, c, me, others = _place()
        for w, out in enumerate(outs):
            for j, (ox, oy) in enumerate(others):
                src = out.at[me] if srcs[w] is None else srcs[w].at[2 * ox + oy]
                yield _remote(src, out.at[me], sems[0].at[w, j], sems[1].at[w, j], (ox, oy, c))

    def start(ins, outs, sems):
        for cp in copies(ins, outs, sems):
            cp.start()

    def finish(ins, outs, sems):
        x, y, c, _, others = _place()
        for w, out in enumerate(outs):
            for j, (ox, oy) in enumerate(others):
                slot = out.at[2 * ox + oy]
                _remote(slot, slot, sems[0].at[w, j], sems[1].at[w, j], (ox, oy, c)).wait_recv()
        for cp in copies(ins, outs, sems):
            cp.wait_send()

    return _Task(given + list(accs), [_sds(a.shape, a.dtype) for a in accs], {len(given) + i: i for i in range(n)},
                 [pltpu.SemaphoreType.DMA((n, 3)), pltpu.SemaphoreType.DMA((n, 3))], start, finish)


def _swap_task(arrays):
    n = len(arrays)

    def copy(src, out, w, sems):
        x, y, c, _, _ = _place()
        return _remote(src, out, sems[0].at[w], sems[1].at[w], (x, y, 1 - c))

    def start(ins, outs, sems):
        for w, (src, out) in enumerate(zip(ins, outs)):
            copy(src, out, w, sems).start()

    def finish(ins, outs, sems):
        for w, (src, out) in enumerate(zip(ins, outs)):
            copy(src, out, w, sems).wait()

    return _Task(arrays, [_sds(a.shape, a.dtype) for a in arrays], {},
                 [pltpu.SemaphoreType.DMA((n,)), pltpu.SemaphoreType.DMA((n,))], start, finish)


def _all_devices_task(arrays):
    n = len(arrays)
    flips = [(dx, dy, dc) for dx in (0, 1) for dy in (0, 1) for dc in (0, 1)][1:]

    def peers():
        x, y, c, _, _ = _place()
        flip = lambda v, d: 1 - v if d else v
        return 4 * x + 2 * y + c, [(flip(x, dx), flip(y, dy), flip(c, dc)) for dx, dy, dc in flips]

    def start(ins, outs, sems):
        me, others = peers()
        for w, (src, out) in enumerate(zip(ins, outs)):
            pltpu.make_async_copy(src, out.at[me], sems[2].at[w]).start()
            for k, peer in enumerate(others):
                _remote(src, out.at[me], sems[0].at[w, k], sems[1].at[w, k], peer).start()

    def finish(ins, outs, sems):
        me, others = peers()
        for w, (src, out) in enumerate(zip(ins, outs)):
            for k, (px, py, pc) in enumerate(others):
                slot = out.at[4 * px + 2 * py + pc]
                _remote(slot, slot, sems[0].at[w, k], sems[1].at[w, k], (px, py, pc)).wait_recv()
            for k, peer in enumerate(others):
                _remote(src, out.at[me], sems[0].at[w, k], sems[1].at[w, k], peer).wait_send()
            pltpu.make_async_copy(src, out.at[me], sems[2].at[w]).wait()

    return _Task(arrays, [_sds((8,) + a.shape, a.dtype) for a in arrays], {},
                 [pltpu.SemaphoreType.DMA((n, 7)), pltpu.SemaphoreType.DMA((n, 7)), pltpu.SemaphoreType.DMA((n,))],
                 start, finish)


def _share_task(shares):
    n = len(shares)

    def copy(out, w, sems, slot):
        x, y, c, _, _ = _place()
        return _remote(out.at[slot], out.at[slot], sems[0].at[w], sems[1].at[w], (x, y, 1 - c))

    def start(ins, outs, sems):
        c = _place()[2]
        for w, out in enumerate(outs):
            copy(out, w, sems, c).start()

    def finish(ins, outs, sems):
        c = _place()[2]
        for w, out in enumerate(outs):
            copy(out, w, sems, 1 - c).wait_recv()
        for w, out in enumerate(outs):
            copy(out, w, sems, c).wait_send()

    return _Task(shares, [_sds(s.shape, s.dtype) for s in shares], {i: i for i in range(n)},
                 [pltpu.SemaphoreType.DMA((n,)), pltpu.SemaphoreType.DMA((n,))], start, finish)


TILE_BYTES = 2 * 1024 * 1024
PARTIAL_TILE_BYTES = 1024 * 1024


def _in_hbm(t):
    return pltpu.with_memory_space_constraint(t, pltpu.HBM)


def _row_tile(rows, cols, limit=TILE_BYTES):
    best = 8
    for tr in range(8, rows + 1, 8):
        if rows % tr == 0 and tr * cols * 4 <= limit:
            best = tr
    assert rows % best == 0, (rows, cols)
    return best


def _chip_partial(g, got, place, wire_dtype):
    ns, ah, b = got.shape
    sharded = ns == N_CHIPS
    tr = _row_tile(ah, b, PARTIAL_TILE_BYTES)
    nb = ah // tr
    first = 0 if g.shape[1] == ah else nb

    def body(place_ref, *refs):
        g_refs, got_refs, outs = refs[:ns], refs[ns:2 * ns], refs[2 * ns:]
        parts = [g_refs[k][0] + got_refs[k][0] for k in range(ns)]
        own = parts[0]
        if sharded:
            for k in range(ns):
                outs[0][k] = parts[k].astype(wire_dtype)
                if k:
                    own = jnp.where(place_ref[0] == k, parts[k], own)
        outs[-1][0] = own.astype(wire_dtype)

    blk = (1, tr, b)
    in_specs = ([pl.BlockSpec(blk, lambda i, s, k=k: (k, s[1] * first + i, 0)) for k in range(ns)]
                + [pl.BlockSpec(blk, lambda i, s, k=k: (k, i, 0)) for k in range(ns)])
    acc_spec = pl.BlockSpec(blk, lambda i, s: (s[0], i, 0))
    acc_shape = _sds((N_CHIPS, ah, b), wire_dtype)
    out = pl.pallas_call(
        body, name="grad_chip_partial",
        grid_spec=pltpu.PrefetchScalarGridSpec(
            num_scalar_prefetch=1, grid=(nb,), in_specs=in_specs,
            out_specs=[pl.BlockSpec((ns, tr, b), lambda i, s: (0, i, 0)), acc_spec] if sharded else [acc_spec]),
        out_shape=[acc_shape, acc_shape] if sharded else [acc_shape],
        compiler_params=pltpu.CompilerParams(dimension_semantics=("arbitrary",), vmem_limit_bytes=VMEM_LIMIT),
    )(place, *([g] * ns), *([got] * ns))
    return (out[0], out[1]) if sharded else (None, out[0])


def _chip_sum(acc, place):
    _, ah, b = acc.shape
    tr = _row_tile(ah, b)

    def body(place_ref, p_ref, out_ref):
        total = p_ref[0].astype(F32) + p_ref[1].astype(F32)
        total = total + p_ref[2].astype(F32)
        out_ref[0] = total + p_ref[3].astype(F32)

    return pl.pallas_call(
        body, name="grad_chip_sum",
        grid_spec=pltpu.PrefetchScalarGridSpec(
            num_scalar_prefetch=1, grid=(ah // tr,),
            in_specs=[pl.BlockSpec((N_CHIPS, tr, b), lambda i, s: (0, i, 0))],
            out_specs=pl.BlockSpec((1, tr, b), lambda i, s: (s[1], i, 0))),
        out_shape=_sds((2, ah, b)),
        compiler_params=pltpu.CompilerParams(dimension_semantics=("arbitrary",)),
    )(place, _in_hbm(acc))


def _adam_math(w, g, m, v):
    nm = ADAM_B1 * m + (1.0 - ADAM_B1) * g
    nv = ADAM_B2 * v + (1.0 - ADAM_B2) * (g * g)
    m_hat = nm / (1.0 - ADAM_B1 ** ADAM_STEP)
    v_hat = nv / (1.0 - ADAM_B2 ** ADAM_STEP)
    return -ADAM_LR * (m_hat / (jnp.sqrt(v_hat) + ADAM_EPS) + ADAM_WD * w), nm, nv


def _adamw(w, g, m, v):
    a, b = w.shape
    tr = _row_tile(a, b)

    def body(w_ref, g_ref, m_ref, v_ref, g_out, d_ref, nm_ref, nv_ref):
        g_out[...] = g_ref[...]
        d_ref[...], nm_ref[...], nv_ref[...] = _adam_math(w_ref[...], g_ref[...], m_ref[...], v_ref[...])

    blk = pl.BlockSpec((tr, b), lambda i: (i, 0))
    return pl.pallas_call(
        body, name="adamw", grid=(a // tr,),
        in_specs=[blk] * 4, out_specs=[blk] * 4, out_shape=[_sds((a, b))] * 4,
        compiler_params=pltpu.CompilerParams(dimension_semantics=("arbitrary",)),
    )(w, g, m, v)


def _adamw_sum(w, m, v, acc, got, place):
    a, b = w.shape
    ah = a // 2
    tr = _row_tile(ah, b)
    nb = ah // tr

    def body(place_ref, w_ref, m_ref, v_ref, acc_ref, got_ref, g_out, d_ref, nm_ref, nv_ref):
        mine = (pl.program_id(0) // nb) == place_ref[1]
        part = lambda k: jnp.where(mine, acc_ref[k], got_ref[k]).astype(F32)
        g = part(0) + part(1)
        g = g + part(2)
        g = g + part(3)
        g_out[...] = g
        d_ref[...], nm_ref[...], nv_ref[...] = _adam_math(w_ref[...], g, m_ref[...], v_ref[...])

    blk = pl.BlockSpec((tr, b), lambda i, s: (i, 0))
    mine_spec = pl.BlockSpec((N_CHIPS, tr, b), lambda i, s: (0, jnp.where(i // nb == s[1], i % nb, 0), 0))
    got_spec = pl.BlockSpec((N_CHIPS, tr, b), lambda i, s: (0, jnp.where(i // nb == s[1], 0, i % nb), 0))
    return pl.pallas_call(
        body, name="adamw_sum",
        grid_spec=pltpu.PrefetchScalarGridSpec(
            num_scalar_prefetch=1, grid=(a // tr,), in_specs=[blk] * 3 + [mine_spec, got_spec], out_specs=[blk] * 4),
        out_shape=[_sds((a, b))] * 4,
        compiler_params=pltpu.CompilerParams(dimension_semantics=("arbitrary",), vmem_limit_bytes=VMEM_LIMIT),
    )(place, w, m, v, _in_hbm(acc), _in_hbm(got))


def _adamw_pieces(g, pieces, name):
    n = len(pieces)

    def body(g_ref, *refs):
        def grad(rows, cols):
            if len(g_ref.shape) == 2:
                return g_ref[rows, cols]
            total = g_ref[0, rows, cols]
            for k in range(1, g_ref.shape[0]):
                total = total + g_ref[k, rows, cols]
            return total

        ins, outs = refs[:3 * n], refs[3 * n:]
        for i, piece in enumerate(pieces):
            w_ref, m_ref, v_ref = ins[3 * i:3 * i + 3]
            o_g, o_d, o_m, o_v = outs[4 * i:4 * i + 4]
            if len(piece) == 5:
                g_v = grad(piece[3], piece[4])
                o_g[...] = g_v
                o_d[...], o_m[...], o_v[...] = _adam_math(w_ref[...], g_v, m_ref[...], v_ref[...])
            else:
                for r in range(w_ref.shape[1] // SMALL_COLS):
                    lanes = slice(r * SMALL_COLS, (r + 1) * SMALL_COLS)
                    g_v = grad(slice(piece[3] + r, piece[3] + r + 1), slice(None))
                    o_g[:, lanes] = g_v
                    o_d[:, lanes], o_m[:, lanes], o_v[:, lanes] = _adam_math(w_ref[:, lanes], g_v, m_ref[:, lanes],
                                                                            v_ref[:, lanes])

    operands = [t for piece in pieces for t in piece[:3]]
    out = pl.pallas_call(
        body, name=name,
        out_shape=[_sds(piece[0].shape) for piece in pieces for _ in range(4)],
    )(g, *operands)
    return [tuple(out[4 * i:4 * i + 4]) for i in range(n)]


TINY_ROWS, TINY_COLS = 16, 768
SMALL_COLS = 128
SMALL_ROWS = 624


def _pack_tiny(conv_w, b_gates, fcw):
    ns = conv_w.shape[0]
    pad = lambda t: jnp.pad(t, ((0, 0), (0, 0), (0, TINY_COLS - t.shape[2])))
    z = lambda rows: jnp.zeros((ns, rows, TINY_COLS), F32)
    return jnp.concatenate([pad(conv_w), pad(b_gates), z(2), fcw, z(TINY_ROWS - 11)], axis=1)


def _unpack_tiny(t):
    return t[:, 0:4, 0:256], t[:, 4:6, 0:256], t[:, 8:11, :]


def _cols_to_shards(t, n):
    return t.reshape(t.shape[0], N_CHIPS, n).transpose(1, 0, 2)


def _shards_to_cols(t):
    return t.transpose(1, 0, 2).reshape(t.shape[1], -1)


_VECTORS = ("g_mix_post", "conv_b", "lru_lambda", "g_ffn_pre", "g_ffn_post", "g_ple_gate", "g_ple_post", "pool_scale",
            "ffn_conv_b")
_VECTOR_LEN = {"pool_scale": POOL_WIDTH, "ffn_conv_b": D_FF}
POOL_W_ROWS = POOL_GROUPS * POOL_GROUP_DIM


def _vector_rows():
    rows, row = {}, POOL_W_ROWS
    for k in _VECTORS:
        rows[k] = row
        row += max(8, _VECTOR_LEN.get(k, D_MODEL) // SMALL_COLS)
    return rows, row


def _pack_small(grads, loss):
    tiles = lambda t: jnp.pad(t, ((0, -t.shape[0] % 8), (0, 0)))
    parts = [grads["pool_w"].reshape(POOL_W_ROWS, SMALL_COLS)] + [tiles(grads[k].reshape(-1, SMALL_COLS)) for k in _VECTORS]
    parts.append(tiles(loss))
    used = sum(t.shape[0] for t in parts)
    return jnp.concatenate(parts + [jnp.zeros((SMALL_ROWS - used, SMALL_COLS), F32)], axis=0)


def _gates_block_diag(w):
    w4 = w.reshape(2, GATE_BLOCKS, 4, RNN_HEAD_DIM, RNN_HEAD_DIM)
    eye = jnp.eye(4, dtype=w.dtype)
    return jnp.einsum("gqhij,hk->gqhikj", w4, eye).reshape(2, GATE_BLOCKS, GATE_BLOCK, GATE_BLOCK)


def _gates_from_block_diag(dw):
    d6 = dw.reshape(2, GATE_BLOCKS, 4, RNN_HEAD_DIM, 4, RNN_HEAD_DIM)
    blocks = [d6[:, :, hh, :, hh, :] for hh in range(4)]
    return jnp.stack(blocks, axis=2).reshape(2, RNN_HEADS, RNN_HEAD_DIM, RNN_HEAD_DIM)


ROW_TILE = 256
DW_TILES = 4

_SHARDED = ("w_in", "w_pool_out", "w_rg_out", "w_o", "w_up", "w_down", "w_ple_gate", "w_ple_proj")
_WEIGHTS = ("g_mix_pre", "g_mix_post", "w_in", "pool_w", "pool_scale", "w_pool_out", "conv_w", "conv_b", "w_rg_gates",
            "b_rg_gates", "lru_lambda", "w_rg_out", "w_o", "g_ffn_pre", "g_ffn_post", "w_up", "ffn_conv_w", "ffn_conv_b",
            "w_down", "g_ple_gate", "w_ple_gate", "w_ple_proj", "g_ple_post")


def _wire_dtype(g):
    return BF16 if g.shape[1] >= 64 and g.shape[2] > SMALL_COLS else F32


def _partials(grads, got, place):
    parts = [_chip_partial(g, r, place, _wire_dtype(g)) for g, r in zip(grads, got)]
    return [send for send, _ in parts], [acc for _, acc in parts]


def _whole(both):
    return [b.reshape(2 * b.shape[1], b.shape[2]) for b in both]


def _step(x, p, tgt, rep, place, ts):
    vec = lambda k: rep[k].reshape(1, -1)
    tall = min(2 * ts, x.shape[0])
    pool_w = rep["pool_w"].astype(BF16)
    wg = _gates_block_diag(rep["w_rg_gates"]).astype(BF16)
    sq = lambda t: t.reshape(D_MODEL, D_MODEL)
    by4 = lambda t: t.reshape(N_CHIPS, -1, D_MODEL)

    first, ride1, ride2 = (("w_in", "w_pool_out", "tiny"), ("w_rg_out", "w_o", "w_down"),
                           ("w_up", "w_ple_gate", "w_ple_proj"))
    later = ride1 + ride2
    tiny = _pack_tiny(rep["conv_w"][None], rep["b_rg_gates"][None], rep["ffn_conv_w"][None])[0]
    own_first, _ = _own_slots([rep["w_in"], rep["w_pool_out"], tiny], [BF16, BF16, F32], "own_slots_first")
    own_later, (got,) = _own_slots([rep[k] for k in later], [BF16] * len(later), "own_slots_gather_first",
                                   [_gather_task(own_first)])
    own = dict(zip(later, own_later))
    full = dict(zip(first, got))
    conv_w, b_gates, fcw = [_shards_to_cols(t) for t in _unpack_tiny(full["tiny"])]

    (urx, urg, gp, gr, d, ypool, h1), (got,) = _fwd_in_pool(
        x, vec("g_mix_pre"), full["w_in"], pool_w, vec("pool_scale"), full["w_pool_out"], ts,
        [_gather_task([own[k] for k in ride1], relay_steps=(6, 2))])
    full.update(zip(ride1, got))
    w_rg_out, w_o, w_down = sq(full["w_rg_out"]), sq(full["w_o"]), full["w_down"].reshape(D_FF, D_MODEL)
    (xc, r, ig, h, yrnn, mo, x1, glr, ggr, sp, sr), (got,) = _fwd_rnn_merge(
        urx, urg, gp, gr, ypool, x, conv_w, vec("conv_b"), wg, b_gates, vec("lru_lambda"), w_rg_out, w_o,
        vec("g_mix_post"), ts, [_gather_task([own[k] for k in ride2], relay_steps=(7, 3))])
    full.update(zip(ride2, got))
    (up, gl, gg, h2, dn, x2), _ = _fwd_ffn(x1, vec("g_ffn_pre"), full["w_up"], fcw, vec("ffn_conv_b"), w_down,
                                           vec("g_ffn_post"), ts)
    dx2, loss, d_w_gate, d_w_proj, d_g_ple_gate, d_g_ple_post = _ple_loss(
        x2, p, tgt, vec("g_ple_gate"), sq(full["w_ple_gate"]), full["w_ple_proj"], vec("g_ple_post"), tall)
    dup, d_w_down, d_fcw, d_fcb, d_g_ffn_post = _bwd_ffn_down(dx2, dn, up, gl, gg, fcw, w_down, vec("g_ffn_post"), ts)

    names1, grads1 = ("w_ple_gate", "w_ple_proj", "w_down"), [by4(d_w_gate), d_w_proj, by4(d_w_down)]
    (dx1, d_g_ffn_pre), (got1,) = _bwd_ffn_up(dup, x1, dx2, vec("g_ffn_pre"), full["w_up"], tall, [_halves_task(grads1)])
    (d_w_up,), (accs1,) = _dw_up(h2, dup, ts, [_exchange_task(*_partials(grads1, got1, place))])
    (dgp, dgr, dyp, dyr, d_w_o, d_g_mix_post), (got2, theirs1) = _bwd_merge(
        dx1, mo, sp, sr, ypool, yrnn, vec("g_mix_post"), w_o, tall, [_halves_task([d_w_up]), _swap_task(accs1)])
    (durx, durg, d_w_rg_out, d_wg, d_conv_w, d_conv_b, d_b_gates, d_lam), (accs2,) = _bwd_rnn(
        dyr, urx, glr, ggr, xc, r, ig, h, conv_w, wg, vec("lru_lambda"), w_rg_out, ts,
        [_exchange_task(*_partials([d_w_up], got2, place))])
    names3 = ("w_o", "w_rg_out", "tiny", "w_rg_gates")
    grads3 = [by4(d_w_o), by4(d_w_rg_out),
              _pack_tiny(_cols_to_shards(d_conv_w, 256), _cols_to_shards(d_b_gates, 256), _cols_to_shards(d_fcw, 768)),
              _gates_from_block_diag(d_wg).reshape(1, 2 * RNN_HEADS * RNN_HEAD_DIM, RNN_HEAD_DIM)]
    (dzp, d_w_pool_out, d_pool_w, d_pool_scale), (got3, theirs2) = _bwd_pool(
        dyp, d, pool_w, vec("pool_scale"), full["w_pool_out"], tall, [_halves_task(grads3), _swap_task(accs2)])
    replicated = {"g_mix_post": d_g_mix_post, "conv_b": d_conv_b, "lru_lambda": d_lam, "g_ffn_pre": d_g_ffn_pre,
                  "g_ffn_post": d_g_ffn_post, "g_ple_gate": d_g_ple_gate, "g_ple_post": d_g_ple_post,
                  "pool_scale": d_pool_scale, "ffn_conv_b": d_fcb, "pool_w": d_pool_w}
    names4 = ("w_in", "w_pool_out", "small")
    small4 = [d_w_pool_out, _pack_small(replicated, loss)[None]]
    (send_w_in, acc_w_in), (accs3, got_small4) = _dw_in(
        h1, dzp, durx, durg, dgp, dgr, ts, [_exchange_task(*_partials(grads3, got3, place)), _halves_task(small4)])
    sends4, accs4 = _partials(small4, got_small4, place)
    (grad_x, d_g_mix_pre), (accs4, theirs3, both3) = _bwd_in(
        dzp, durx, durg, dgp, dgr, x, dx1, vec("g_mix_pre"), full["w_in"], tall,
        [_exchange_task([send_w_in] + sends4, [acc_w_in] + accs4), _swap_task(accs3[:2]),
         _share_task([_chip_sum(acc, place) for acc in accs3[2:]])])
    theirs4, both4, (g_mix_pre_parts,) = _run(
        [_swap_task(accs4[:2]), _share_task([_chip_sum(acc, place) for acc in accs4[2:]]),
         _all_devices_task([d_g_mix_pre.reshape(SUBLANES, SMALL_COLS)])], "grad_sibling_share")
    mine = accs1 + accs2 + accs3[:2] + accs4[:2]
    partials = dict(zip(names1 + ("w_up",) + names3[:2] + names4[:2], zip(mine, theirs1 + theirs2 + theirs3 + theirs4)))
    return grad_x, partials, dict(zip(names3[2:] + names4[2:], _whole(both3) + _whole(both4))), g_mix_pre_parts


def kernel(x, p, g_mix_pre, g_mix_post, w_in, pool_w, pool_scale, w_pool_out, conv_w, conv_b, w_rg_gates, b_rg_gates, lru_lambda, w_rg_out, w_o, g_ffn_pre, g_ffn_post, w_up, ffn_conv_w, ffn_conv_b, w_down, g_ple_gate, w_ple_gate, w_ple_proj, g_ple_post, loss_target, m_g_mix_pre, m_g_mix_post, m_w_in, m_pool_w, m_pool_scale, m_w_pool_out, m_conv_w, m_conv_b, m_w_rg_gates, m_b_rg_gates, m_lru_lambda, m_w_rg_out, m_w_o, m_g_ffn_pre, m_g_ffn_post, m_w_up, m_ffn_conv_w, m_ffn_conv_b, m_w_down, m_g_ple_gate, m_w_ple_gate, m_w_ple_proj, m_g_ple_post, v_g_mix_pre, v_g_mix_post, v_w_in, v_pool_w, v_pool_scale, v_w_pool_out, v_conv_w, v_conv_b, v_w_rg_gates, v_b_rg_gates, v_lru_lambda, v_w_rg_out, v_w_o, v_g_ffn_pre, v_g_ffn_post, v_w_up, v_ffn_conv_w, v_ffn_conv_b, v_w_down, v_g_ple_gate, v_w_ple_gate, v_w_ple_proj, v_g_ple_post):
    args = dict(locals())
    w = {k: args[k][0] for k in _WEIGHTS}
    m = {k: args["m_" + k][0] for k in _WEIGHTS}
    v = {k: args["v_" + k][0] for k in _WEIGHTS}
    place = jnp.stack([2 * lax.axis_index("x") + lax.axis_index("y"), lax.axis_index("c")]).astype(jnp.int32)
    grad_x, partials, reduced, g_mix_pre_parts = _step(x[0], p[0, 0], loss_target[0], w, place, ROW_TILE)

    gates_2d = (2 * RNN_HEADS * RNN_HEAD_DIM, RNN_HEAD_DIM)
    as2d = lambda k, shape: tuple(t[k].reshape(shape) for t in (w, m, v))
    done = {k: tuple(_adamw_sum(w[k], m[k], v[k], *partials[k], place)) for k in _SHARDED}
    gates_w, gates_m, gates_v = as2d("w_rg_gates", gates_2d)
    done["w_rg_gates"] = tuple(_adamw(gates_w, reduced["w_rg_gates"], gates_m, gates_v))
    tiny_names = ("conv_w", "b_rg_gates", "ffn_conv_w")
    tiny_at = ((slice(0, 4), slice(0, 256)), (slice(4, 6), slice(0, 256)), (slice(8, 11), slice(None)))
    done.update(zip(tiny_names, _adamw_pieces(
        reduced["tiny"], [(w[k], m[k], v[k]) + at for k, at in zip(tiny_names, tiny_at)], "adamw_tiny")))
    vector_rows, loss_row = _vector_rows()
    pieces = [as2d("pool_w", (POOL_W_ROWS, SMALL_COLS)) + (slice(0, POOL_W_ROWS), slice(None))]
    pieces += [as2d(k, (1, -1)) + (vector_rows[k],) for k in _VECTORS]
    done.update(zip(("pool_w",) + _VECTORS, _adamw_pieces(reduced["small"], pieces, "adamw_small")))
    done["g_mix_pre"] = _adamw_pieces(g_mix_pre_parts, [as2d("g_mix_pre", (1, -1)) + (0,)], "adamw_g_mix_pre")[0]

    result = [reduced["small"][loss_row, 0], grad_x[None]]
    for kind in range(4):
        result += [done[k][kind].reshape(args[k].shape) for k in _WEIGHTS]
    return tuple(result)
```

```python
import functools

import jax
import jax.numpy as jnp
from jax import lax
from jax.experimental import pallas as pl
from jax.experimental.pallas import tpu as pltpu

F32 = jnp.float32
BF16 = jnp.bfloat16

D_MODEL = 1024
POOL_WINDOWS = (2, 4, 8, 16)
POOL_GROUPS = 4
POOL_WIDTH = 512
POOL_GROUP_DIM = 128
RNN_HEADS = 16
RNN_HEAD_DIM = 64
GATE_BLOCK = 256
GATE_BLOCKS = D_MODEL // GATE_BLOCK
LRU_C = 8.0
D_FF = 3072
PLE_DIM = 256
RMS_EPS = 1e-6
IN_TOTAL = 4608
N_CHIPS = 4
IN_SHARD = IN_TOTAL // N_CHIPS
UP_SHARD = 2 * D_FF // N_CHIPS
Z_GROUPS = ((0, 512), (512, 1536), (1536, 2560), (2560, 3584), (3584, 4608))
POOL_HALO = 16
CONV_HALO = 8

ADAM_LR = 0.001
ADAM_B1 = 0.9
ADAM_B2 = 0.999
ADAM_EPS = 1e-08
ADAM_WD = 0.01
ADAM_STEP = 10

VMEM_LIMIT = 56 * 1024 * 1024
MESH = pl.DeviceIdType.MESH

_GELU_C = 0.7978845608028654
_GELU_A = 0.044715


def _dot(a, b):
    return jnp.dot(a.astype(BF16), b.astype(BF16), preferred_element_type=F32)


def _dot_nt(a, b):
    return lax.dot_general(a.astype(BF16), b.astype(BF16), (((1,), (1,)), ((), ())), preferred_element_type=F32)


def _dot_tn(a, b):
    return lax.dot_general(a.astype(BF16), b.astype(BF16), (((0,), (0,)), ((), ())), preferred_element_type=F32)


def _overlaps(group):
    a, b = Z_GROUPS[group]
    found = []
    for j in range(N_CHIPS):
        lo, hi = max(a, j * IN_SHARD), min(b, (j + 1) * IN_SHARD)
        if lo < hi:
            found.append((j, slice(lo - j * IN_SHARD, hi - j * IN_SHARD), slice(lo - a, hi - a)))
    return found


def _rms_fwd(x, g):
    r = lax.rsqrt(jnp.mean(x * x, axis=-1, keepdims=True) + RMS_EPS)
    xh = x * r
    return xh * g, xh, r


def _rms_bwd(xh, r, g, dy):
    dxh = dy * g
    dg = jnp.sum(dy * xh, axis=0, keepdims=True)
    dx = r * (dxh - xh * jnp.mean(dxh * xh, axis=-1, keepdims=True))
    return dx, dg


def _sigmoid(x):
    return 0.5 * jnp.tanh(0.5 * x) + 0.5


def _gelu(x):
    t = jnp.tanh(_GELU_C * (x + _GELU_A * x * x * x))
    return 0.5 * x * (1.0 + t), t


def _gelu_grad(x, t):
    return 0.5 * (1.0 + t) + 0.5 * x * (1.0 - t * t) * _GELU_C * (1.0 + 3.0 * _GELU_A * x * x)


def _softplus_neg(lam):
    nl = -lam
    return jnp.maximum(nl, 0.0) + jnp.log(1.0 + jnp.exp(-jnp.abs(nl)))


def _lru_coeffs(r, lam, first_row):
    c8 = LRU_C * _softplus_neg(lam)
    la = -(c8 * r)
    a = jnp.exp(la)
    m2 = jnp.tanh(-la) * (1.0 + a * a)
    mult = jnp.where(first_row, 1.0, jnp.sqrt(m2))
    return c8, a, m2, mult


SUBLANES = 8


def _scan_fwd(a, u, carry):
    n = a.shape[0]
    sub = lax.broadcasted_iota(jnp.int32, (n, 1), 0) % SUBLANES
    acc_a, acc_h = a, u
    for s in (1, 2, 4):
        m = sub >= s
        h_s = jnp.where(m, pltpu.roll(acc_h, s, 0), 0.0)
        a_s = jnp.where(m, pltpu.roll(acc_a, s, 0), 1.0)
        acc_h = acc_a * h_s + acc_h
        acc_a = acc_a * a_s
    out = []
    for g in range(n // SUBLANES):
        rows = slice(g * SUBLANES, (g + 1) * SUBLANES)
        out.append(acc_h[rows] + acc_a[rows] * carry)
        carry = out[-1][SUBLANES - 1:SUBLANES]
    return jnp.concatenate(out, axis=0)


def _scan_bwd(b, g, carry):
    n = b.shape[0]
    sub = lax.broadcasted_iota(jnp.int32, (n, 1), 0) % SUBLANES
    acc_b, acc_l = b, g
    for s in (1, 2, 4):
        m = sub < SUBLANES - s
        l_s = jnp.where(m, pltpu.roll(acc_l, n - s, 0), 0.0)
        b_s = jnp.where(m, pltpu.roll(acc_b, n - s, 0), 1.0)
        acc_l = acc_b * l_s + acc_l
        acc_b = acc_b * b_s
    out = [None] * (n // SUBLANES)
    for g in reversed(range(n // SUBLANES)):
        rows = slice(g * SUBLANES, (g + 1) * SUBLANES)
        out[g] = acc_l[rows] + acc_b[rows] * carry
        carry = out[g][0:1]
    return jnp.concatenate(out, axis=0)


def _shift_down(ext, k, halo):
    return pltpu.roll(ext, k, 0)[halo:] if k else ext[halo:]


def _shift_up(ext, k, ts):
    return pltpu.roll(ext, ext.shape[0] - k, 0)[:ts] if k else ext[:ts]


def _rows(ts, width, nt=None, col=0):
    if nt is None:
        return pl.BlockSpec((ts, width), lambda i: (i, col))
    return pl.BlockSpec((ts, width), lambda i: (nt - 1 - i, col))


def _resident(shape):
    zeros = (0,) * len(shape)
    return pl.BlockSpec(shape, lambda i: zeros, pipeline_mode=pl.Buffered(1))


def _acc(shape):
    zeros = (0,) * len(shape)
    return pl.BlockSpec(shape, lambda i: zeros)


def _params():
    return pltpu.CompilerParams(dimension_semantics=("arbitrary",), vmem_limit_bytes=VMEM_LIMIT)


def _sds(shape, dtype=F32):
    return jax.ShapeDtypeStruct(shape, dtype)


class _Task:
    def __init__(self, ins, out_shapes, aliases, sems, start, finish, relays=()):
        self.ins, self.out_shapes, self.aliases, self.sems = list(ins), list(out_shapes), dict(aliases), list(sems)
        self.start, self.relays, self.finish = start, list(relays), finish


def _call(body, name, grid, in_specs, out_specs, out_shape, scratch_shapes, args, tasks=()):
    n_in, n_out, n_scr = len(in_specs), len(out_specs), len(scratch_shapes)
    t_in = [len(t.ins) for t in tasks]
    t_out = [len(t.out_shapes) for t in tasks]
    t_sem = [len(t.sems) for t in tasks]
    steps = 1
    for g in grid:
        steps *= g

    def take(refs, pos, counts):
        groups = []
        for c in counts:
            groups.append(refs[pos:pos + c])
            pos += c
        return groups, pos

    def wrapped(*refs):
        (cin,), pos = take(refs, 0, [n_in])
        tin, pos = take(refs, pos, t_in)
        (cout,), pos = take(refs, pos, [n_out])
        tout, pos = take(refs, pos, t_out)
        (cscr,), pos = take(refs, pos, [n_scr])
        tsem, pos = take(refs, pos, t_sem)
        if not grid:
            for t, a, b, c in zip(tasks, tin, tout, tsem):
                t.start(a, b, c)
            if body is not None:
                body(*cin, *cout, *cscr)
            for t, a, b, c in zip(tasks, tin, tout, tsem):
                for relay, _ in t.relays:
                    relay(a, b, c)
            for t, a, b, c in zip(tasks, tin, tout, tsem):
                t.finish(a, b, c)
            return
        step = pl.program_id(0)
        for axis in range(1, len(grid)):
            step = step * grid[axis] + pl.program_id(axis)
        if tasks:
            @pl.when(step == 0)
            def _():
                for t, a, b, c in zip(tasks, tin, tout, tsem):
                    t.start(a, b, c)

        body(*cin, *cout, *cscr)
        for t, a, b, c in zip(tasks, tin, tout, tsem):
            for relay, before in t.relays:
                pl.when(step == max(steps - 1 - before, 0))(functools.partial(relay, a, b, c))

        if tasks:
            @pl.when(step == steps - 1)
            def _():
                for t, a, b, c in zip(tasks, tin, tout, tsem):
                    t.finish(a, b, c)

    aliases, in_pos, out_pos = {}, n_in, n_out
    for t, ni, no in zip(tasks, t_in, t_out):
        aliases.update({in_pos + a: out_pos + b for a, b in t.aliases.items()})
        in_pos, out_pos = in_pos + ni, out_pos + no
    any_spec = pl.BlockSpec(memory_space=pltpu.HBM)
    kwargs = dict(grid=grid, compiler_params=pltpu.CompilerParams(
        dimension_semantics=("arbitrary",) * len(grid), vmem_limit_bytes=VMEM_LIMIT)) if grid else dict(
        compiler_params=pltpu.CompilerParams(vmem_limit_bytes=VMEM_LIMIT))
    out = pl.pallas_call(
        wrapped, name=name,
        in_specs=list(in_specs) + [any_spec] * sum(t_in),
        out_specs=list(out_specs) + [any_spec] * sum(t_out),
        out_shape=list(out_shape) + [s for t in tasks for s in t.out_shapes],
        scratch_shapes=list(scratch_shapes) + [s for t in tasks for s in t.sems],
        input_output_aliases=aliases, **kwargs,
    )(*args, *[pltpu.with_memory_space_constraint(a, pltpu.HBM) for t in tasks for a in t.ins])
    task_outs, pos = take(list(out), n_out, t_out)
    return list(out[:n_out]), task_outs


def _fwd_in_pool(x, g_pre, w_in, pool_w, pool_scale, w_pool_out, ts, tasks=()):
    s = x.shape[0]

    def body(x_ref, g_ref, win_ref, pw_ref, ps_ref, wpo_ref,
             urx_ref, urg_ref, gp_ref, gr_ref, d_ref, yp_ref, h1_ref, halo_scr):
        i = pl.program_id(0)

        @pl.when(i == 0)
        def _():
            halo_scr[...] = jnp.zeros_like(halo_scr)

        h1, _, _ = _rms_fwd(x_ref[...], g_ref[...])
        h1 = h1.astype(BF16)
        h1_ref[...] = h1
        u = jnp.dot(h1, win_ref[0, :, 0:POOL_WIDTH], preferred_element_type=F32)
        for group, out_ref in zip(range(1, len(Z_GROUPS)), (urx_ref, urg_ref, gp_ref, gr_ref)):
            for j, shard_cols, group_cols in _overlaps(group):
                out_ref[:, group_cols] = jnp.dot(h1, win_ref[j, :, shard_cols], preferred_element_type=F32)
        ext = jnp.concatenate([halo_scr[...], u], axis=0)
        halo_scr[...] = u[ts - POOL_HALO:, :]
        t = i * ts + lax.broadcasted_iota(jnp.int32, (ts, 1), 0)
        y4 = []
        for g, w in enumerate(POOL_WINDOWS):
            lanes = slice(g * POOL_GROUP_DIM, (g + 1) * POOL_GROUP_DIM)
            acc = ext[:, lanes]
            sh = 1
            while sh < w:
                acc = acc + pltpu.roll(acc, sh, 0)
                sh *= 2
            inv = 1.0 / jnp.minimum(t + 1, w).astype(F32)
            dg = acc[POOL_HALO:, :] * inv - u[:, lanes]
            d_ref[:, lanes] = dg.astype(BF16)
            y4.append(_dot(dg, pw_ref[g]))
        ypre = jnp.concatenate(y4, axis=1) * ps_ref[...]
        ypre = ypre.astype(BF16)
        for j in range(N_CHIPS):
            yp_ref[:, j * 256:(j + 1) * 256] = jnp.dot(ypre, wpo_ref[j], preferred_element_type=F32)

    return _call(
        body, "fwd_in_pool", (s // ts,),
        [_rows(ts, D_MODEL), _resident((1, D_MODEL)), _resident(w_in.shape), _resident(pool_w.shape),
         _resident((1, POOL_WIDTH)), _resident(w_pool_out.shape)],
        [_rows(ts, D_MODEL)] * 4 + [_rows(ts, POOL_WIDTH), _rows(ts, D_MODEL), _rows(ts, D_MODEL)],
        [_sds((s, D_MODEL))] * 4 + [_sds((s, POOL_WIDTH), BF16), _sds((s, D_MODEL)), _sds((s, D_MODEL), BF16)],
        [pltpu.VMEM((POOL_HALO, POOL_WIDTH), F32)],
        (x, g_pre, w_in, pool_w, pool_scale, w_pool_out), tasks)


def _fwd_rnn_merge(urx, urg, gp, gr, ypool, x, conv_w, conv_b, wg, bg, lam, w_rg_out, w_o, g_post, ts, tasks=()):
    s = x.shape[0]

    def body(urx_ref, urg_ref, gp_ref, gr_ref, yp_ref, x_ref, cw_ref, cb_ref, wg_ref, bg_ref, lam_ref, wrg_ref, wo_ref,
             gpost_ref, xc_ref, r_ref, ig_ref, h_ref, yr_ref, mo_ref, x1_ref, gl_ref, gg_ref, sp_ref, sr_ref,
             halo_scr, carry_scr):
        i = pl.program_id(0)

        @pl.when(i == 0)
        def _():
            halo_scr[...] = jnp.zeros_like(halo_scr)
            carry_scr[...] = jnp.zeros_like(carry_scr)

        urx_v = urx_ref[...]
        ext = jnp.concatenate([halo_scr[...], urx_v], axis=0)
        halo_scr[...] = urx_v[ts - CONV_HALO:, :]
        cw = cw_ref[...]
        xc = (cb_ref[...] + cw[3:4] * urx_v + cw[2:3] * _shift_down(ext, 1, CONV_HALO)
              + cw[1:2] * _shift_down(ext, 2, CONV_HALO) + cw[0:1] * _shift_down(ext, 3, CONV_HALO))
        xc_ref[...] = xc
        xcb = xc.astype(BF16)
        lin = []
        for gate in range(2):
            parts = [jnp.dot(xcb[:, q * GATE_BLOCK:(q + 1) * GATE_BLOCK], wg_ref[gate, q], preferred_element_type=F32)
                     for q in range(GATE_BLOCKS)]
            lin.append(jnp.concatenate(parts, axis=1) + bg_ref[gate:gate + 1, :])
        r = _sigmoid(lin[0])
        ig = _sigmoid(lin[1])
        r_ref[...] = r
        ig_ref[...] = ig
        first_row = (i * ts + lax.broadcasted_iota(jnp.int32, (ts, 1), 0)) == 0
        _, a, _, mult = _lru_coeffs(r, lam_ref[...], first_row)
        h = _scan_fwd(a, mult * ig * xc, carry_scr[0:1, :])
        carry_scr[0:1, :] = h[ts - 1:ts, :]
        h_ref[...] = h
        urg_v = urg_ref[...]
        gl, t = _gelu(urg_v)
        gl_ref[...] = gl.astype(BF16)
        gg_ref[...] = _gelu_grad(urg_v, t).astype(BF16)
        yr = _dot(h * gl, wrg_ref[...])
        yr_ref[...] = yr
        sp = _sigmoid(gp_ref[...])
        sr = _sigmoid(gr_ref[...])
        sp_ref[...] = sp.astype(BF16)
        sr_ref[...] = sr.astype(BF16)
        merged = sp * yp_ref[...] + sr * yr
        mo = _dot(merged, wo_ref[...])
        mo_ref[...] = mo
        y, _, _ = _rms_fwd(mo, gpost_ref[...])
        x1_ref[...] = x_ref[...] + y

    row = _rows(ts, D_MODEL)
    return _call(
        body, "fwd_rnn_merge", (s // ts,),
        [row] * 6 + [_resident(conv_w.shape), _resident((1, D_MODEL)), _resident(wg.shape), _resident(bg.shape),
                     _resident((1, D_MODEL)), _resident(w_rg_out.shape), _resident(w_o.shape), _resident((1, D_MODEL))],
        [row] * 11, [_sds((s, D_MODEL))] * 7 + [_sds((s, D_MODEL), BF16)] * 4,
        [pltpu.VMEM((CONV_HALO, D_MODEL), F32), pltpu.VMEM((8, D_MODEL), F32)],
        (urx, urg, gp, gr, ypool, x, conv_w, conv_b, wg, bg, lam, w_rg_out, w_o, g_post), tasks)


def _fwd_ffn(x1, g_pre, w_up, fcw, fcb, w_down, g_post, ts, tasks=()):
    s = x1.shape[0]

    def body(x1_ref, g_ref, wup_ref, fcw_ref, fcb_ref, wd_ref, gpost_ref,
             up_ref, gl_ref, gg_ref, h2_ref, dn_ref, x2_ref, up_scr, halo_scr):
        i = pl.program_id(0)

        @pl.when(i == 0)
        def _():
            halo_scr[...] = jnp.zeros_like(halo_scr)

        x1_v = x1_ref[...]
        h2, _, _ = _rms_fwd(x1_v, g_ref[...])
        h2 = h2.astype(BF16)
        h2_ref[...] = h2
        for j in range(N_CHIPS):
            up_scr[:, j * UP_SHARD:(j + 1) * UP_SHARD] = jnp.dot(h2, wup_ref[j], preferred_element_type=F32)
        up_ref[...] = up_scr[...].astype(BF16)
        ug = up_scr[:, 0:D_FF]
        ext = jnp.concatenate([halo_scr[...], ug], axis=0)
        halo_scr[...] = ug[ts - CONV_HALO:, :]
        w = fcw_ref[...]
        gh = (fcb_ref[...] + w[2:3] * ug + w[1:2] * _shift_down(ext, 1, CONV_HALO)
              + w[0:1] * _shift_down(ext, 2, CONV_HALO))
        gl, t = _gelu(gh)
        gl_ref[...] = gl.astype(BF16)
        gg_ref[...] = _gelu_grad(gh, t).astype(BF16)
        dn = _dot(gl * up_scr[:, D_FF:], wd_ref[...])
        dn_ref[...] = dn
        y, _, _ = _rms_fwd(dn, gpost_ref[...])
        x2_ref[...] = x1_v + y

    row = _rows(ts, D_MODEL)
    return _call(
        body, "fwd_ffn", (s // ts,),
        [row, _resident((1, D_MODEL)), _resident(w_up.shape), _resident(fcw.shape), _resident((1, D_FF)),
         _resident(w_down.shape), _resident((1, D_MODEL))],
        [_rows(ts, 2 * D_FF), _rows(ts, D_FF), _rows(ts, D_FF), row, row, row],
        [_sds((s, 2 * D_FF), BF16), _sds((s, D_FF), BF16), _sds((s, D_FF), BF16), _sds((s, D_MODEL), BF16),
         _sds((s, D_MODEL)), _sds((s, D_MODEL))],
        [pltpu.VMEM((ts, 2 * D_FF), F32), pltpu.VMEM((CONV_HALO, D_FF), F32)],
        (x1, g_pre, w_up, fcw, fcb, w_down, g_post), tasks)


def _ple_loss(x2, p, tgt, g_gate, w_gate, w_proj, g_post, ts):
    s = x2.shape[0]

    def body(x2_ref, p_ref, t_ref, gg_ref, wg_ref, wp_ref, gp_ref, dx2_ref, loss_ref, dwg_ref, dwp_ref, dgg_ref, dgp_ref):
        @pl.when(pl.program_id(0) == 0)
        def _():
            loss_ref[...] = jnp.zeros_like(loss_ref)
            dwg_ref[...] = jnp.zeros_like(dwg_ref)
            dwp_ref[...] = jnp.zeros_like(dwp_ref)
            dgg_ref[...] = jnp.zeros_like(dgg_ref)
            dgp_ref[...] = jnp.zeros_like(dgp_ref)

        x2_v = x2_ref[...]
        n3, xh3, r3 = _rms_fwd(x2_v, gg_ref[...])
        pg = _sigmoid(_dot(n3, wg_ref[...]))
        pb = p_ref[...].astype(BF16)
        q = jnp.concatenate([jnp.dot(pb, wp_ref[j], preferred_element_type=F32) for j in range(N_CHIPS)], axis=1)
        ple, qh, rq = _rms_fwd(q, gp_ref[...])
        e = x2_v + pg * ple - t_ref[...]
        loss_ref[...] += 0.5 * jnp.sum(jnp.mean(e * e, axis=-1, keepdims=True), axis=0, keepdims=True)
        dy = e * (1.0 / D_MODEL)
        dpgl = dy * ple * pg * (1.0 - pg)
        dwg_ref[...] += _dot_tn(n3, dpgl)
        dx3, dgg = _rms_bwd(xh3, r3, gg_ref[...], _dot_nt(dpgl, wg_ref[...]))
        dgg_ref[...] += dgg
        dq, dgp = _rms_bwd(qh, rq, gp_ref[...], dy * pg)
        dgp_ref[...] += dgp
        for j in range(N_CHIPS):
            dwp_ref[j] += _dot_tn(pb, dq[:, j * 256:(j + 1) * 256])
        dx2_ref[...] = dy + dx3

    row = _rows(ts, D_MODEL)
    vec = _acc((1, D_MODEL))
    return pl.pallas_call(
        body, name="ple_loss", grid=(s // ts,),
        in_specs=[row, _rows(ts, PLE_DIM), row, _resident((1, D_MODEL)), _resident(w_gate.shape), _resident(w_proj.shape),
                  _resident((1, D_MODEL))],
        out_specs=[row, _acc((1, 128)), _acc(w_gate.shape), _acc(w_proj.shape), vec, vec],
        out_shape=[_sds((s, D_MODEL)), _sds((1, 128)), _sds(w_gate.shape), _sds(w_proj.shape), _sds((1, D_MODEL)),
                   _sds((1, D_MODEL))],
        compiler_params=_params(),
    )(x2, p, tgt, g_gate, w_gate, w_proj, g_post)


def _bwd_ffn_down(dx2, dn, up, gl, gg, fcw, w_down, g_post, ts):
    s = dx2.shape[0]
    nt = s // ts

    def body(dx2_ref, dn_ref, up_ref, gl_ref, gg_ref, fcw_ref, wd_ref, gpost_ref,
             dup_ref, dwd_ref, dfcw_ref, dfcb_ref, dgp_ref, carry_scr):
        i = pl.program_id(0)

        @pl.when(i == 0)
        def _():
            carry_scr[...] = jnp.zeros_like(carry_scr)
            dwd_ref[...] = jnp.zeros_like(dwd_ref)
            dfcw_ref[...] = jnp.zeros_like(dfcw_ref)
            dfcb_ref[...] = jnp.zeros_like(dfcb_ref)
            dgp_ref[...] = jnp.zeros_like(dgp_ref)

        _, xh, r = _rms_fwd(dn_ref[...], gpost_ref[...])
        ddn, dgp = _rms_bwd(xh, r, gpost_ref[...], dx2_ref[...])
        dgp_ref[...] += dgp
        dhid = _dot_nt(ddn, wd_ref[...])
        ug = up_ref[:, 0:D_FF].astype(F32)
        uv = up_ref[:, D_FF:].astype(F32)
        gl = gl_ref[...].astype(F32)
        w = fcw_ref[...]
        dwd_ref[...] += _dot_tn(gl * uv, ddn)
        dgh = dhid * uv * gg_ref[...].astype(F32)
        dup_ref[:, D_FF:] = (dhid * gl).astype(BF16)
        extd = jnp.concatenate([dgh, carry_scr[...]], axis=0)
        carry_scr[...] = dgh[0:CONV_HALO, :]
        d1 = _shift_up(extd, 1, ts)
        d2 = _shift_up(extd, 2, ts)
        dup_ref[:, 0:D_FF] = (w[2:3] * dgh + w[1:2] * d1 + w[0:1] * d2).astype(BF16)
        dfcw_ref[2:3, :] += jnp.sum(ug * dgh, axis=0, keepdims=True)
        dfcw_ref[1:2, :] += jnp.sum(ug * d1, axis=0, keepdims=True)
        dfcw_ref[0:1, :] += jnp.sum(ug * d2, axis=0, keepdims=True)
        dfcb_ref[...] += jnp.sum(dgh, axis=0, keepdims=True)

    row = _rows(ts, D_MODEL, nt)
    wide = _rows(ts, D_FF, nt)
    return pl.pallas_call(
        body, name="bwd_ffn_down", grid=(nt,),
        in_specs=[row, row, _rows(ts, 2 * D_FF, nt), wide, wide, _resident(fcw.shape), _resident(w_down.shape),
                  _resident((1, D_MODEL))],
        out_specs=[_rows(ts, 2 * D_FF, nt), _acc(w_down.shape), _acc(fcw.shape), _acc((1, D_FF)), _acc((1, D_MODEL))],
        out_shape=[_sds((s, 2 * D_FF), BF16), _sds(w_down.shape), _sds(fcw.shape), _sds((1, D_FF)), _sds((1, D_MODEL))],
        scratch_shapes=[pltpu.VMEM((CONV_HALO, D_FF), F32)],
        compiler_params=_params(),
    )(dx2, dn, up, gl, gg, fcw, w_down, g_post)


def _bwd_ffn_up(dup, x1, dx2, g_pre, w_up, ts, tasks=()):
    s = x1.shape[0]

    def body(dup_ref, x1_ref, dx2_ref, g_ref, wup_ref, dx1_ref, dg_ref):
        @pl.when(pl.program_id(0) == 0)
        def _():
            dg_ref[...] = jnp.zeros_like(dg_ref)

        _, xh, r = _rms_fwd(x1_ref[...], g_ref[...])
        dh2 = _dot_nt(dup_ref[:, 0:UP_SHARD], wup_ref[0])
        for j in range(1, N_CHIPS):
            dh2 = dh2 + _dot_nt(dup_ref[:, j * UP_SHARD:(j + 1) * UP_SHARD], wup_ref[j])
        dx, dg = _rms_bwd(xh, r, g_ref[...], dh2)
        dg_ref[...] += dg
        dx1_ref[...] = dx2_ref[...] + dx

    row = _rows(ts, D_MODEL)
    return _call(
        body, "bwd_ffn_up", (s // ts,),
        [_rows(ts, 2 * D_FF), row, row, _resident((1, D_MODEL)), _resident(w_up.shape)],
        [row, _acc((1, D_MODEL))], [_sds((s, D_MODEL)), _sds((1, D_MODEL))], [],
        (dup, x1, dx2, g_pre, w_up), tasks)


def _dw_up(h2, dup, ts, tasks=()):
    s = h2.shape[0]
    ts = min(DW_TILES * ts, s)
    nt = s // ts
    half = D_MODEL // 2

    def body(h2_ref, dup_ref, send_ref, acc_ref, theirs_scr, mine_scr, got_scr, sems):
        j, o, i = pl.program_id(0), pl.program_id(1), pl.program_id(2)
        x, y, c, me, _ = _place()
        prod = _dot_tn(h2_ref[...], dup_ref[...])
        for scr, which in ((theirs_scr, 0), (mine_scr, 1)):
            @pl.when((o == which) & (i == 0))
            def _():
                scr[...] = prod

            @pl.when((o == which) & (i > 0))
            def _():
                scr[...] += prod

        slot = j % 2
        push = _remote(theirs_scr, got_scr.at[slot], sems.at[0, slot], sems.at[1, slot], (x, y, 1 - c))
        pl.when((o == 1) & (i == 0))(push.start)

        @pl.when((o == 1) & (i == nt - 1))
        def _():
            push.wait()
            part = (mine_scr[...] + got_scr[slot]).astype(BF16)
            send_ref[0] = part

            @pl.when(me == j)
            def _():
                acc_ref[0] = part

    def h2_cols(j, o, i):
        c = lax.axis_index("c")
        return i, jnp.where(o == 0, 1 - c, c)

    def own_slot(j, o, i):
        return 2 * lax.axis_index("x") + lax.axis_index("y"), 0, 0

    block = (1, half, UP_SHARD)
    return _call(
        body, "dw_up", (N_CHIPS, 2, nt),
        [pl.BlockSpec((ts, half), h2_cols), pl.BlockSpec((ts, UP_SHARD), lambda j, o, i: (i, j))],
        [pl.BlockSpec(block, lambda j, o, i: (j, 0, 0)), pl.BlockSpec(block, own_slot)],
        [_sds((N_CHIPS, half, UP_SHARD), BF16)] * 2,
        [pltpu.VMEM((half, UP_SHARD), F32), pltpu.VMEM((half, UP_SHARD), F32), pltpu.VMEM((2, half, UP_SHARD), F32),
         pltpu.SemaphoreType.DMA((2, 2))],
        (h2, dup), tasks)


def _bwd_merge(dx1, mo, sp, sr, ypool, yrnn, g_post, w_o, ts, tasks=()):
    s = dx1.shape[0]

    def body(dx1_ref, mo_ref, sp_ref, sr_ref, yp_ref, yr_ref, g_ref, wo_ref,
             dgp_ref, dgr_ref, dyp_ref, dyr_ref, dwo_ref, dg_ref):
        @pl.when(pl.program_id(0) == 0)
        def _():
            dwo_ref[...] = jnp.zeros_like(dwo_ref)
            dg_ref[...] = jnp.zeros_like(dg_ref)

        _, xh, r = _rms_fwd(mo_ref[...], g_ref[...])
        dmo, dg = _rms_bwd(xh, r, g_ref[...], dx1_ref[...])
        dg_ref[...] += dg
        dmerged = _dot_nt(dmo, wo_ref[...])
        sp = sp_ref[...].astype(F32)
        sr = sr_ref[...].astype(F32)
        yp = yp_ref[...]
        yr = yr_ref[...]
        dwo_ref[...] += _dot_tn(sp * yp + sr * yr, dmo)
        dgp_ref[...] = (dmerged * yp * sp * (1.0 - sp)).astype(BF16)
        dgr_ref[...] = (dmerged * yr * sr * (1.0 - sr)).astype(BF16)
        dyp_ref[...] = (dmerged * sp).astype(BF16)
        dyr_ref[...] = (dmerged * sr).astype(BF16)

    row = _rows(ts, D_MODEL)
    return _call(
        body, "bwd_merge", (s // ts,),
        [row] * 6 + [_resident((1, D_MODEL)), _resident(w_o.shape)],
        [row] * 4 + [_acc(w_o.shape), _acc((1, D_MODEL))],
        [_sds((s, D_MODEL), BF16)] * 4 + [_sds(w_o.shape), _sds((1, D_MODEL))], [],
        (dx1, mo, sp, sr, ypool, yrnn, g_post, w_o), tasks)


def _bwd_rnn(dyr, urx, gl, gg, xc, r, ig, h, conv_w, wg, lam, w_rg_out, ts, tasks=()):
    s = urx.shape[0]
    nt = s // ts
    halo_blocks = ts // CONV_HALO

    def body(dyr_ref, urx_ref, gl_ref, gg_ref, xc_ref, r_ref, ig_ref, h_ref, hh_ref, cw_ref, wg_ref, lam_ref, wrg_ref,
             durx_ref, durg_ref, dwrg_ref, dwg_ref, dcw_ref, dcb_ref, dbg_ref, dlam_ref, mu_scr, carry_scr):
        i = pl.program_id(0)
        k = nt - 1 - i

        @pl.when(i == 0)
        def _():
            mu_scr[...] = jnp.zeros_like(mu_scr)
            carry_scr[...] = jnp.zeros_like(carry_scr)
            dwrg_ref[...] = jnp.zeros_like(dwrg_ref)
            dwg_ref[...] = jnp.zeros_like(dwg_ref)
            dcw_ref[...] = jnp.zeros_like(dcw_ref)
            dcb_ref[...] = jnp.zeros_like(dcb_ref)
            dbg_ref[...] = jnp.zeros_like(dbg_ref)
            dlam_ref[...] = jnp.zeros_like(dlam_ref)

        row = lax.broadcasted_iota(jnp.int32, (ts, 1), 0)
        first_row = (k * ts + row) == 0
        h = h_ref[...]
        dyr_v = dyr_ref[...]
        dhr = _dot_nt(dyr_v, wrg_ref[...])
        gl = gl_ref[...].astype(F32)
        dwrg_ref[...] += _dot_tn(h * gl, dyr_v)
        durg_ref[...] = (dhr * h * gg_ref[...].astype(F32)).astype(BF16)
        r_v = r_ref[...]
        ig_v = ig_ref[...]
        xc_v = xc_ref[...]
        lam_v = lam_ref[...]
        c8, a, m2, mult = _lru_coeffs(r_v, lam_v, first_row)
        b = jnp.where(row == ts - 1, 1.0, pltpu.roll(a, ts - 1, 0))
        lt = _scan_bwd(b, dhr * gl, mu_scr[0:1, :])
        mu_scr[0:1, :] = a[0:1, :] * lt[0:1, :]
        h_before = jnp.where(k > 0, hh_ref[CONV_HALO - 1:CONV_HALO, :], 0.0)
        hprev = jnp.where(row == 0, h_before, pltpu.roll(h, 1, 0))
        dmult = lt * ig_v * xc_v
        da = lt * hprev - jnp.where(first_row, 0.0, dmult * a * lax.rsqrt(m2))
        dla = da * a
        dlam_ref[...] += jnp.sum(dla * r_v, axis=0, keepdims=True)
        dlr = (dla * (-c8)) * r_v * (1.0 - r_v)
        dli = (lt * mult * xc_v) * ig_v * (1.0 - ig_v)
        dbg_ref[0:1, :] += jnp.sum(dlr, axis=0, keepdims=True)
        dbg_ref[1:2, :] += jnp.sum(dli, axis=0, keepdims=True)
        xcb = xc_v.astype(BF16)
        parts = []
        for q in range(GATE_BLOCKS):
            blk = slice(q * GATE_BLOCK, (q + 1) * GATE_BLOCK)
            dlr_q = dlr[:, blk].astype(BF16)
            dli_q = dli[:, blk].astype(BF16)
            parts.append(_dot_nt(dlr_q, wg_ref[0, q]) + _dot_nt(dli_q, wg_ref[1, q]))
            dwg_ref[0, q] += _dot_tn(xcb[:, blk], dlr_q)
            dwg_ref[1, q] += _dot_tn(xcb[:, blk], dli_q)
        dxc = lt * mult * ig_v + jnp.concatenate(parts, axis=1)
        extd = jnp.concatenate([dxc, carry_scr[...]], axis=0)
        carry_scr[...] = dxc[0:CONV_HALO, :]
        cw = cw_ref[...]
        urx_v = urx_ref[...]
        durx = cw[3:4] * dxc
        dcw_ref[3:4, :] += jnp.sum(urx_v * dxc, axis=0, keepdims=True)
        for j in (1, 2, 3):
            dj = _shift_up(extd, j, ts)
            durx = durx + cw[3 - j:4 - j] * dj
            dcw_ref[3 - j:4 - j, :] += jnp.sum(urx_v * dj, axis=0, keepdims=True)
        durx_ref[...] = durx.astype(BF16)
        dcb_ref[...] += jnp.sum(dxc, axis=0, keepdims=True)

        @pl.when(i == nt - 1)
        def _():
            dlam_ref[...] = dlam_ref[...] * (LRU_C * jax.nn.sigmoid(-lam_v))

    row_spec = _rows(ts, D_MODEL, nt)
    halo_spec = pl.BlockSpec((CONV_HALO, D_MODEL), lambda i: (jnp.maximum((nt - 1 - i) * halo_blocks - 1, 0), 0))
    vec = _acc((1, D_MODEL))
    return _call(
        body, "bwd_rnn", (nt,),
        [row_spec] * 8 + [halo_spec, _resident(conv_w.shape), _resident(wg.shape), _resident((1, D_MODEL)),
                          _resident(w_rg_out.shape)],
        [row_spec, row_spec, _acc(w_rg_out.shape), _acc(wg.shape), _acc(conv_w.shape), vec, _acc((2, D_MODEL)), vec],
        [_sds((s, D_MODEL), BF16), _sds((s, D_MODEL), BF16), _sds(w_rg_out.shape), _sds(wg.shape), _sds(conv_w.shape),
         _sds((1, D_MODEL)), _sds((2, D_MODEL)), _sds((1, D_MODEL))],
        [pltpu.VMEM((8, D_MODEL), F32), pltpu.VMEM((CONV_HALO, D_MODEL), F32)],
        (dyr, urx, gl, gg, xc, r, ig, h, h, conv_w, wg, lam, w_rg_out), tasks)


def _bwd_pool(dyp, d, pool_w, pool_scale, w_pool_out, ts, tasks=()):
    s = d.shape[0]
    nt = s // ts

    def body(dyp_ref, d_ref, pw_ref, ps_ref, wpo_ref, dzp_ref, dwpo_ref, dpw_ref, dps_ref, carry_scr):
        i = pl.program_id(0)
        k = nt - 1 - i

        @pl.when(i == 0)
        def _():
            carry_scr[...] = jnp.zeros_like(carry_scr)
            dwpo_ref[...] = jnp.zeros_like(dwpo_ref)
            dpw_ref[...] = jnp.zeros_like(dpw_ref)
            dps_ref[...] = jnp.zeros_like(dps_ref)

        dyp_v = dyp_ref[...]
        d_v = d_ref[...]
        ps = ps_ref[...]
        dypre = _dot_nt(dyp_v[:, 0:256], wpo_ref[0])
        for j in range(1, N_CHIPS):
            dypre = dypre + _dot_nt(dyp_v[:, j * 256:(j + 1) * 256], wpo_ref[j])
        y4 = jnp.concatenate([_dot(d_v[:, g * 128:(g + 1) * 128], pw_ref[g]) for g in range(POOL_GROUPS)], axis=1)
        ypre = (y4 * ps).astype(BF16)
        for j in range(N_CHIPS):
            dwpo_ref[j] += _dot_tn(ypre, dyp_v[:, j * 256:(j + 1) * 256])
        dps_ref[...] += jnp.sum(dypre * y4, axis=0, keepdims=True)
        dy4 = dypre * ps
        t = k * ts + lax.broadcasted_iota(jnp.int32, (ts, 1), 0)
        for g, w in enumerate(POOL_WINDOWS):
            lanes = slice(g * POOL_GROUP_DIM, (g + 1) * POOL_GROUP_DIM)
            dd = _dot_nt(dy4[:, lanes], pw_ref[g])
            dpw_ref[g] += _dot_tn(d_v[:, lanes], dy4[:, lanes])
            e = dd * (1.0 / jnp.minimum(t + 1, w).astype(F32))
            acc = jnp.concatenate([e, carry_scr[:, lanes]], axis=0)
            carry_scr[:, lanes] = e[0:POOL_HALO, :]
            n = ts + POOL_HALO
            sh = 1
            while sh < w:
                acc = acc + pltpu.roll(acc, n - sh, 0)
                sh *= 2
            dzp_ref[:, lanes] = (acc[:ts, :] - dd).astype(BF16)

    return _call(
        body, "bwd_pool", (nt,),
        [_rows(ts, D_MODEL, nt), _rows(ts, POOL_WIDTH, nt), _resident(pool_w.shape), _resident((1, POOL_WIDTH)),
         _resident(w_pool_out.shape)],
        [_rows(ts, POOL_WIDTH, nt), _acc(w_pool_out.shape), _acc(pool_w.shape), _acc((1, POOL_WIDTH))],
        [_sds((s, POOL_WIDTH), BF16), _sds(w_pool_out.shape), _sds(pool_w.shape), _sds((1, POOL_WIDTH))],
        [pltpu.VMEM((POOL_HALO, POOL_WIDTH), F32)],
        (dyp, d, pool_w, pool_scale, w_pool_out), tasks)


def _assemble_dz(dz_scr, dzp_ref, durx_ref, durg_ref, dgp_ref, dgr_ref):
    dz_scr[:, 0:512] = dzp_ref[...]
    dz_scr[:, 512:1536] = durx_ref[...]
    dz_scr[:, 1536:2560] = durg_ref[...]
    dz_scr[:, 2560:3584] = dgp_ref[...]
    dz_scr[:, 3584:4608] = dgr_ref[...]


def _dw_in(h1, dzp, durx, durg, dgp, dgr, ts, tasks=()):
    s = h1.shape[0]
    ts = min(2 * ts, s)
    nt = s // ts
    half = D_MODEL // 2

    def body(h1_ref, dzp_ref, durx_ref, durg_ref, dgp_ref, dgr_ref, send_ref, acc_ref, theirs_ref, mine_ref, got_ref, sems):
        o, i = pl.program_id(0), pl.program_id(1)

        @pl.when((o == 0) & (i == 0))
        def _():
            theirs_ref[...] = jnp.zeros_like(theirs_ref)
            mine_ref[...] = jnp.zeros_like(mine_ref)

        groups = (dzp_ref, durx_ref, durg_ref, dgp_ref, dgr_ref)
        for out_ref, which in ((theirs_ref, 0), (mine_ref, 1)):
            @pl.when(o == which)
            def _():
                for group, dz_ref in enumerate(groups):
                    for j, shard_cols, group_cols in _overlaps(group):
                        out_ref[j, :, shard_cols] += _dot_tn(h1_ref[...], dz_ref[:, group_cols])

        x, y, c, me, _ = _place()
        send = _remote(theirs_ref, got_ref, sems.at[0], sems.at[1], (x, y, 1 - c))
        pl.when((o == 1) & (i == 0))(send.start)

        @pl.when((o == 1) & (i == nt - 1))
        def _():
            send.wait()
            for k in range(N_CHIPS):
                part = (mine_ref[k] + got_ref[k]).astype(BF16)
                send_ref[k] = part

                @pl.when(me == k)
                def _():
                    acc_ref[k] = part

    def h1_cols(o, i):
        c = lax.axis_index("c")
        return i, jnp.where(o == 0, 1 - c, c)

    rows = lambda width: pl.BlockSpec((ts, width), lambda o, i: (i, 0))
    shape = (N_CHIPS, half, IN_SHARD)
    whole = pl.BlockSpec(shape, lambda o, i: (0, 0, 0))
    return _call(
        body, "dw_in", (2, nt), [pl.BlockSpec((ts, half), h1_cols), rows(POOL_WIDTH)] + [rows(D_MODEL)] * 4,
        [whole, whole], [_sds(shape, BF16)] * 2,
        [pltpu.VMEM(shape, F32)] * 3 + [pltpu.SemaphoreType.DMA((2,))], (h1, dzp, durx, durg, dgp, dgr), tasks)


def _bwd_in(dzp, durx, durg, dgp, dgr, x, dx1, g_pre, w_in, ts, tasks=()):
    s = x.shape[0]

    def body(dzp_ref, durx_ref, durg_ref, dgp_ref, dgr_ref, x_ref, dx1_ref, g_ref, win_ref, gx_ref, dg_ref, dz_scr):
        @pl.when(pl.program_id(0) == 0)
        def _():
            dg_ref[...] = jnp.zeros_like(dg_ref)

        _assemble_dz(dz_scr, dzp_ref, durx_ref, durg_ref, dgp_ref, dgr_ref)
        _, xh, r = _rms_fwd(x_ref[...], g_ref[...])
        dh1 = _dot_nt(dz_scr[:, 0:IN_SHARD], win_ref[0])
        for j in range(1, N_CHIPS):
            dh1 = dh1 + _dot_nt(dz_scr[:, j * IN_SHARD:(j + 1) * IN_SHARD], win_ref[j])
        dx, dg = _rms_bwd(xh, r, g_ref[...], dh1)
        dg_ref[...] += dg
        gx_ref[...] = dx1_ref[...] + dx

    row = _rows(ts, D_MODEL)
    return _call(
        body, "bwd_in", (s // ts,),
        [_rows(ts, POOL_WIDTH)] + [row] * 6 + [_resident((1, D_MODEL)), _resident(w_in.shape)],
        [row, _acc((1, D_MODEL))], [_sds((s, D_MODEL)), _sds((1, D_MODEL))],
        [pltpu.VMEM((ts, IN_TOTAL), BF16)], (dzp, durx, durg, dgp, dgr, x, dx1, g_pre, w_in), tasks)


def _place():
    x, y, c = lax.axis_index("x"), lax.axis_index("y"), lax.axis_index("c")
    others = [(1 - x, y), (x, 1 - y), (1 - x, 1 - y)]
    return x, y, c, 2 * x + y, others


def _remote(src, dst, send_sem, recv_sem, to):
    return pltpu.make_async_remote_copy(src_ref=src, dst_ref=dst, send_sem=send_sem, recv_sem=recv_sem,
                                        device_id=to, device_id_type=MESH)


def _own_slots(ws, dtypes, name, tasks=()):
    n = len(ws)
    hbm = pl.BlockSpec(memory_space=pltpu.HBM)

    def body(*refs):
        srcs, outs, f32_bufs, cast_bufs, sems = refs[:n], refs[n:2 * n], refs[2 * n:3 * n], refs[3 * n:4 * n], refs[4 * n]
        me = _place()[3]
        loads = [pltpu.make_async_copy(srcs[k], f32_bufs[k], sems.at[k, 0]) for k in range(n)]
        stores = [pltpu.make_async_copy(cast_bufs[k], outs[k].at[me], sems.at[k, 1]) for k in range(n)]
        for cp in loads:
            cp.start()
        for k in range(n):
            loads[k].wait()
            cast_bufs[k][...] = f32_bufs[k][...].astype(dtypes[k])
            stores[k].start()
        for cp in stores:
            cp.wait()

    return _call(
        body, name, (), [hbm] * n, [hbm] * n, [_sds((N_CHIPS,) + w.shape, dt) for w, dt in zip(ws, dtypes)],
        [pltpu.VMEM(w.shape, F32) for w in ws] + [pltpu.VMEM(w.shape, dt) for w, dt in zip(ws, dtypes)]
        + [pltpu.SemaphoreType.DMA((n, 2))],
        [pltpu.with_memory_space_constraint(w, pltpu.HBM) for w in ws], tasks)


def _run(tasks, name):
    if isinstance(tasks, _Task):
        return _call(None, name, (), [], [], [], [], (), (tasks,))[1][0]
    return _call(None, name, (), [], [], [], [], (), tuple(tasks))[1]


def _gather_task(bufs, relay_steps=(0, 0)):
    n = len(bufs)
    NBR_X, NBR_Y, QUARTER_VIA_Y, QUARTER_VIA_X, SIB_X, SIB_Y, SIB_DIAG = range(7)

    def parts(out):
        x, y, c, me, _ = _place()
        ah = out.shape[1] // 2
        q = ah // 2 if (ah // 2) % 16 == 0 else ah
        return c * ah, ah, q

    def copy(out, w, k, chip, row0, rows, to, sems):
        slot = out.at[chip, pl.ds(row0, rows)]
        return _remote(slot, slot, sems[0].at[w, k], sems[1].at[w, k], to)

    def plan(out, w, sems):
        x, y, c, me, _ = _place()
        row0, ah, q = parts(out)
        xn, yn, dg = 2 * (1 - x) + y, 2 * x + (1 - y), 2 * (1 - x) + (1 - y)
        to_x, to_y, sib = (1 - x, y, c), (x, 1 - y, c), (x, y, 1 - c)
        other = (1 - c) * ah
        cp = functools.partial(copy, out, w, sems=sems)
        sends = {NBR_X: cp(NBR_X, me, row0, ah, to_x), NBR_Y: cp(NBR_Y, me, row0, ah, to_y),
                 QUARTER_VIA_Y: cp(QUARTER_VIA_Y, xn, row0, q, to_y), SIB_X: cp(SIB_X, xn, row0, ah, sib),
                 SIB_Y: cp(SIB_Y, yn, row0, ah, sib), SIB_DIAG: cp(SIB_DIAG, dg, row0, ah, sib)}
        lands = {NBR_X: cp(NBR_X, xn, row0, ah, to_x), NBR_Y: cp(NBR_Y, yn, row0, ah, to_y),
                 QUARTER_VIA_Y: cp(QUARTER_VIA_Y, dg, row0, q, to_y), SIB_X: cp(SIB_X, xn, other, ah, sib),
                 SIB_Y: cp(SIB_Y, yn, other, ah, sib), SIB_DIAG: cp(SIB_DIAG, dg, other, ah, sib)}
        if q < ah:
            sends[QUARTER_VIA_X] = cp(QUARTER_VIA_X, yn, row0 + q, ah - q, to_x)
            lands[QUARTER_VIA_X] = cp(QUARTER_VIA_X, dg, row0 + q, ah - q, to_x)
        return sends, lands

    def start(ins, outs, sems):
        for w, out in enumerate(outs):
            sends, _ = plan(out, w, sems)
            sends[NBR_X].start()
            sends[NBR_Y].start()

    def pass_neighbours(ins, outs, sems):
        for w, out in enumerate(outs):
            sends, lands = plan(out, w, sems)
            lands[NBR_X].wait_recv()
            sends[QUARTER_VIA_Y].start()
            sends[SIB_X].start()
            lands[NBR_Y].wait_recv()
            if QUARTER_VIA_X in sends:
                sends[QUARTER_VIA_X].start()
            sends[SIB_Y].start()

    def pass_diagonal(ins, outs, sems):
        for w, out in enumerate(outs):
            sends, lands = plan(out, w, sems)
            lands[QUARTER_VIA_Y].wait_recv()
            if QUARTER_VIA_X in lands:
                lands[QUARTER_VIA_X].wait_recv()
            sends[SIB_DIAG].start()

    def finish(ins, outs, sems):
        for w, out in enumerate(outs):
            sends, lands = plan(out, w, sems)
            for k in (SIB_X, SIB_Y, SIB_DIAG):
                lands[k].wait_recv()
        for w, out in enumerate(outs):
            sends, _ = plan(out, w, sems)
            for cp in sends.values():
                cp.wait_send()

    return _Task(bufs, [_sds(b.shape, b.dtype) for b in bufs], {i: i for i in range(n)},
                 [pltpu.SemaphoreType.DMA((n, 7)), pltpu.SemaphoreType.DMA((n, 7))], start, finish,
                 [(pass_neighbours, relay_steps[0]), (pass_diagonal, relay_steps[1])])


def _halves_task(grads):
    n = len(grads)

    def copy(src, out, w, sems):
        x, y, c, _, _ = _place()
        ah = out.shape[1]
        return _remote(src.at[:, pl.ds((1 - c) * ah, ah)], out, sems[0].at[w], sems[1].at[w], (x, y, 1 - c))

    def start(ins, outs, sems):
        for w, (src, out) in enumerate(zip(ins, outs)):
            copy(src, out, w, sems).start()

    def finish(ins, outs, sems):
        for w, (src, out) in enumerate(zip(ins, outs)):
            copy(src, out, w, sems).wait()

    return _Task(grads, [_sds((g.shape[0], g.shape[1] // 2, g.shape[2]), g.dtype) for g in grads], {},
                 [pltpu.SemaphoreType.DMA((n,)), pltpu.SemaphoreType.DMA((n,))], start, finish)


def _exchange_task(sends, accs):
    n = len(accs)
    given = [s for s in sends if s is not None]

    def copies(ins, outs, sems):
        send_refs = iter(ins[:len(given)])
        srcs = [next(send_refs) if s is not None else None for s in sends]
        x, y, c, me, others = _place()
        for w, out in enumerate(outs):
            for j, (ox, oy) in enumerate(others):
                src = out.at[me] if srcs[w] is None else srcs[w].at[2 * ox + oy]
                yield _remote(src, out.at[me], sems[0].at[w, j], sems[1].at[w, j], (ox, oy, c))

    def start(ins, outs, sems):
        for cp in copies(ins, outs, sems):
            cp.start()

    def finish(ins, outs, sems):
        x, y, c, _, others = _place()
        for w, out in enumerate(outs):
            for j, (ox, oy) in enumerate(others):
                slot = out.at[2 * ox + oy]
                _remote(slot, slot, sems[0].at[w, j], sems[1].at[w, j], (ox, oy, c)).wait_recv()
        for cp in copies(ins, outs, sems):
            cp.wait_send()

    return _Task(given + list(accs), [_sds(a.shape, a.dtype) for a in accs], {len(given) + i: i for i in range(n)},
                 [pltpu.SemaphoreType.DMA((n, 3)), pltpu.SemaphoreType.DMA((n, 3))], start, finish)


def _swap_task(arrays):
    n = len(arrays)

    def copy(src, out, w, sems):
        x, y, c, _, _ = _place()
        return _remote(src, out, sems[0].at[w], sems[1].at[w], (x, y, 1 - c))

    def start(ins, outs, sems):
        for w, (src, out) in enumerate(zip(ins, outs)):
            copy(src, out, w, sems).start()

    def finish(ins, outs, sems):
        for w, (src, out) in enumerate(zip(ins, outs)):
            copy(src, out, w, sems).wait()

    return _Task(arrays, [_sds(a.shape, a.dtype) for a in arrays], {},
                 [pltpu.SemaphoreType.DMA((n,)), pltpu.SemaphoreType.DMA((n,))], start, finish)


def _all_devices_task(arrays):
    n = len(arrays)
    flips = [(dx, dy, dc) for dx in (0, 1) for dy in (0, 1) for dc in (0, 1)][1:]

    def peers():
        x, y, c, _, _ = _place()
        flip = lambda v, d: 1 - v if d else v
        return 4 * x + 2 * y + c, [(flip(x, dx), flip(y, dy), flip(c, dc)) for dx, dy, dc in flips]

    def start(ins, outs, sems):
        me, others = peers()
        for w, (src, out) in enumerate(zip(ins, outs)):
            pltpu.make_async_copy(src, out.at[me], sems[2].at[w]).start()
            for k, peer in enumerate(others):
                _remote(src, out.at[me], sems[0].at[w, k], sems[1].at[w, k], peer).start()

    def finish(ins, outs, sems):
        me, others = peers()
        for w, (src, out) in enumerate(zip(ins, outs)):
            for k, (px, py, pc) in enumerate(others):
                slot = out.at[4 * px + 2 * py + pc]
                _remote(slot, slot, sems[0].at[w, k], sems[1].at[w, k], (px, py, pc)).wait_recv()
            for k, peer in enumerate(others):
                _remote(src, out.at[me], sems[0].at[w, k], sems[1].at[w, k], peer).wait_send()
            pltpu.make_async_copy(src, out.at[me], sems[2].at[w]).wait()

    return _Task(arrays, [_sds((8,) + a.shape, a.dtype) for a in arrays], {},
                 [pltpu.SemaphoreType.DMA((n, 7)), pltpu.SemaphoreType.DMA((n, 7)), pltpu.SemaphoreType.DMA((n,))],
                 start, finish)


def _share_task(shares):
    n = len(shares)

    def copy(out, w, sems, slot):
        x, y, c, _, _ = _place()
        return _remote(out.at[slot], out.at[slot], sems[0].at[w], sems[1].at[w], (x, y, 1 - c))

    def start(ins, outs, sems):
        c = _place()[2]
        for w, out in enumerate(outs):
            copy(out, w, sems, c).start()

    def finish(ins, outs, sems):
        c = _place()[2]
        for w, out in enumerate(outs):
            copy(out, w, sems, 1 - c).wait_recv()
        for w, out in enumerate(outs):
            copy(out, w, sems, c).wait_send()

    return _Task(shares, [_sds(s.shape, s.dtype) for s in shares], {i: i for i in range(n)},
                 [pltpu.SemaphoreType.DMA((n,)), pltpu.SemaphoreType.DMA((n,))], start, finish)


TILE_BYTES = 2 * 1024 * 1024
PARTIAL_TILE_BYTES = 1024 * 1024


def _in_hbm(t):
    return pltpu.with_memory_space_constraint(t, pltpu.HBM)


def _row_tile(rows, cols, limit=TILE_BYTES):
    best = 8
    for tr in range(8, rows + 1, 8):
        if rows % tr == 0 and tr * cols * 4 <= limit:
            best = tr
    assert rows % best == 0, (rows, cols)
    return best


def _chip_partial(g, got, place, wire_dtype):
    ns, ah, b = got.shape
    sharded = ns == N_CHIPS
    tr = _row_tile(ah, b, PARTIAL_TILE_BYTES)
    nb = ah // tr
    first = 0 if g.shape[1] == ah else nb

    def body(place_ref, *refs):
        g_refs, got_refs, outs = refs[:ns], refs[ns:2 * ns], refs[2 * ns:]
        parts = [g_refs[k][0] + got_refs[k][0] for k in range(ns)]
        own = parts[0]
        if sharded:
            for k in range(ns):
                outs[0][k] = parts[k].astype(wire_dtype)
                if k:
                    own = jnp.where(place_ref[0] == k, parts[k], own)
        outs[-1][0] = own.astype(wire_dtype)

    blk = (1, tr, b)
    in_specs = ([pl.BlockSpec(blk, lambda i, s, k=k: (k, s[1] * first + i, 0)) for k in range(ns)]
                + [pl.BlockSpec(blk, lambda i, s, k=k: (k, i, 0)) for k in range(ns)])
    acc_spec = pl.BlockSpec(blk, lambda i, s: (s[0], i, 0))
    acc_shape = _sds((N_CHIPS, ah, b), wire_dtype)
    out = pl.pallas_call(
        body, name="grad_chip_partial",
        grid_spec=pltpu.PrefetchScalarGridSpec(
            num_scalar_prefetch=1, grid=(nb,), in_specs=in_specs,
            out_specs=[pl.BlockSpec((ns, tr, b), lambda i, s: (0, i, 0)), acc_spec] if sharded else [acc_spec]),
        out_shape=[acc_shape, acc_shape] if sharded else [acc_shape],
        compiler_params=pltpu.CompilerParams(dimension_semantics=("arbitrary",), vmem_limit_bytes=VMEM_LIMIT),
    )(place, *([g] * ns), *([got] * ns))
    return (out[0], out[1]) if sharded else (None, out[0])


def _chip_sum(acc, place):
    _, ah, b = acc.shape
    tr = _row_tile(ah, b)

    def body(place_ref, p_ref, out_ref):
        total = p_ref[0].astype(F32) + p_ref[1].astype(F32)
        total = total + p_ref[2].astype(F32)
        out_ref[0] = total + p_ref[3].astype(F32)

    return pl.pallas_call(
        body, name="grad_chip_sum",
        grid_spec=pltpu.PrefetchScalarGridSpec(
            num_scalar_prefetch=1, grid=(ah // tr,),
            in_specs=[pl.BlockSpec((N_CHIPS, tr, b), lambda i, s: (0, i, 0))],
            out_specs=pl.BlockSpec((1, tr, b), lambda i, s: (s[1], i, 0))),
        out_shape=_sds((2, ah, b)),
        compiler_params=pltpu.CompilerParams(dimension_semantics=("arbitrary",)),
    )(place, _in_hbm(acc))


def _adam_math(w, g, m, v):
    nm = ADAM_B1 * m + (1.0 - ADAM_B1) * g
    nv = ADAM_B2 * v + (1.0 - ADAM_B2) * (g * g)
    m_hat = nm / (1.0 - ADAM_B1 ** ADAM_STEP)
    v_hat = nv / (1.0 - ADAM_B2 ** ADAM_STEP)
    return -ADAM_LR * (m_hat / (jnp.sqrt(v_hat) + ADAM_EPS) + ADAM_WD * w), nm, nv


def _adamw(w, g, m, v):
    a, b = w.shape
    tr = _row_tile(a, b)

    def body(w_ref, g_ref, m_ref, v_ref, g_out, d_ref, nm_ref, nv_ref):
        g_out[...] = g_ref[...]
        d_ref[...], nm_ref[...], nv_ref[...] = _adam_math(w_ref[...], g_ref[...], m_ref[...], v_ref[...])

    blk = pl.BlockSpec((tr, b), lambda i: (i, 0))
    return pl.pallas_call(
        body, name="adamw", grid=(a // tr,),
        in_specs=[blk] * 4, out_specs=[blk] * 4, out_shape=[_sds((a, b))] * 4,
        compiler_params=pltpu.CompilerParams(dimension_semantics=("arbitrary",)),
    )(w, g, m, v)


def _adamw_sum(w, m, v, acc, got, place):
    a, b = w.shape
    ah = a // 2
    tr = _row_tile(ah, b)
    nb = ah // tr

    def body(place_ref, w_ref, m_ref, v_ref, acc_ref, got_ref, g_out, d_ref, nm_ref, nv_ref):
        mine = (pl.program_id(0) // nb) == place_ref[1]
        part = lambda k: jnp.where(mine, acc_ref[k], got_ref[k]).astype(F32)
        g = part(0) + part(1)
        g = g + part(2)
        g = g + part(3)
        g_out[...] = g
        d_ref[...], nm_ref[...], nv_ref[...] = _adam_math(w_ref[...], g, m_ref[...], v_ref[...])

    blk = pl.BlockSpec((tr, b), lambda i, s: (i, 0))
    mine_spec = pl.BlockSpec((N_CHIPS, tr, b), lambda i, s: (0, jnp.where(i // nb == s[1], i % nb, 0), 0))
    got_spec = pl.BlockSpec((N_CHIPS, tr, b), lambda i, s: (0, jnp.where(i // nb == s[1], 0, i % nb), 0))
    return pl.pallas_call(
        body, name="adamw_sum",
        grid_spec=pltpu.PrefetchScalarGridSpec(
            num_scalar_prefetch=1, grid=(a // tr,), in_specs=[blk] * 3 + [mine_spec, got_spec], out_specs=[blk] * 4),
        out_shape=[_sds((a, b))] * 4,
        compiler_params=pltpu.CompilerParams(dimension_semantics=("arbitrary",), vmem_limit_bytes=VMEM_LIMIT),
    )(place, w, m, v, _in_hbm(acc), _in_hbm(got))


def _adamw_pieces(g, pieces, name):
    n = len(pieces)

    def body(g_ref, *refs):
        def grad(rows, cols):
            if len(g_ref.shape) == 2:
                return g_ref[rows, cols]
            total = g_ref[0, rows, cols]
            for k in range(1, g_ref.shape[0]):
                total = total + g_ref[k, rows, cols]
            return total

        ins, outs = refs[:3 * n], refs[3 * n:]
        for i, piece in enumerate(pieces):
            w_ref, m_ref, v_ref = ins[3 * i:3 * i + 3]
            o_g, o_d, o_m, o_v = outs[4 * i:4 * i + 4]
            if len(piece) == 5:
                g_v = grad(piece[3], piece[4])
                o_g[...] = g_v
                o_d[...], o_m[...], o_v[...] = _adam_math(w_ref[...], g_v, m_ref[...], v_ref[...])
            else:
                for r in range(w_ref.shape[1] // SMALL_COLS):
                    lanes = slice(r * SMALL_COLS, (r + 1) * SMALL_COLS)
                    g_v = grad(slice(piece[3] + r, piece[3] + r + 1), slice(None))
                    o_g[:, lanes] = g_v
                    o_d[:, lanes], o_m[:, lanes], o_v[:, lanes] = _adam_math(w_ref[:, lanes], g_v, m_ref[:, lanes],
                                                                            v_ref[:, lanes])

    operands = [t for piece in pieces for t in piece[:3]]
    out = pl.pallas_call(
        body, name=name,
        out_shape=[_sds(piece[0].shape) for piece in pieces for _ in range(4)],
    )(g, *operands)
    return [tuple(out[4 * i:4 * i + 4]) for i in range(n)]


TINY_ROWS, TINY_COLS = 16, 768
SMALL_COLS = 128
SMALL_ROWS = 624


def _pack_tiny(conv_w, b_gates, fcw):
    ns = conv_w.shape[0]
    pad = lambda t: jnp.pad(t, ((0, 0), (0, 0), (0, TINY_COLS - t.shape[2])))
    z = lambda rows: jnp.zeros((ns, rows, TINY_COLS), F32)
    return jnp.concatenate([pad(conv_w), pad(b_gates), z(2), fcw, z(TINY_ROWS - 11)], axis=1)


def _unpack_tiny(t):
    return t[:, 0:4, 0:256], t[:, 4:6, 0:256], t[:, 8:11, :]


def _cols_to_shards(t, n):
    return t.reshape(t.shape[0], N_CHIPS, n).transpose(1, 0, 2)


def _shards_to_cols(t):
    return t.transpose(1, 0, 2).reshape(t.shape[1], -1)


_VECTORS = ("g_mix_post", "conv_b", "lru_lambda", "g_ffn_pre", "g_ffn_post", "g_ple_gate", "g_ple_post", "pool_scale",
            "ffn_conv_b")
_VECTOR_LEN = {"pool_scale": POOL_WIDTH, "ffn_conv_b": D_FF}
POOL_W_ROWS = POOL_GROUPS * POOL_GROUP_DIM


def _vector_rows():
    rows, row = {}, POOL_W_ROWS
    for k in _VECTORS:
        rows[k] = row
        row += max(8, _VECTOR_LEN.get(k, D_MODEL) // SMALL_COLS)
    return rows, row


def _pack_small(grads, loss):
    tiles = lambda t: jnp.pad(t, ((0, -t.shape[0] % 8), (0, 0)))
    parts = [grads["pool_w"].reshape(POOL_W_ROWS, SMALL_COLS)] + [tiles(grads[k].reshape(-1, SMALL_COLS)) for k in _VECTORS]
    parts.append(tiles(loss))
    used = sum(t.shape[0] for t in parts)
    return jnp.concatenate(parts + [jnp.zeros((SMALL_ROWS - used, SMALL_COLS), F32)], axis=0)


def _gates_block_diag(w):
    w4 = w.reshape(2, GATE_BLOCKS, 4, RNN_HEAD_DIM, RNN_HEAD_DIM)
    eye = jnp.eye(4, dtype=w.dtype)
    return jnp.einsum("gqhij,hk->gqhikj", w4, eye).reshape(2, GATE_BLOCKS, GATE_BLOCK, GATE_BLOCK)


def _gates_from_block_diag(dw):
    d6 = dw.reshape(2, GATE_BLOCKS, 4, RNN_HEAD_DIM, 4, RNN_HEAD_DIM)
    blocks = [d6[:, :, hh, :, hh, :] for hh in range(4)]
    return jnp.stack(blocks, axis=2).reshape(2, RNN_HEADS, RNN_HEAD_DIM, RNN_HEAD_DIM)


ROW_TILE = 256
DW_TILES = 4

_SHARDED = ("w_in", "w_pool_out", "w_rg_out", "w_o", "w_up", "w_down", "w_ple_gate", "w_ple_proj")
_WEIGHTS = ("g_mix_pre", "g_mix_post", "w_in", "pool_w", "pool_scale", "w_pool_out", "conv_w", "conv_b", "w_rg_gates",
            "b_rg_gates", "lru_lambda", "w_rg_out", "w_o", "g_ffn_pre", "g_ffn_post", "w_up", "ffn_conv_w", "ffn_conv_b",
            "w_down", "g_ple_gate", "w_ple_gate", "w_ple_proj", "g_ple_post")


def _wire_dtype(g):
    return BF16 if g.shape[1] >= 64 and g.shape[2] > SMALL_COLS else F32


def _partials(grads, got, place):
    parts = [_chip_partial(g, r, place, _wire_dtype(g)) for g, r in zip(grads, got)]
    return [send for send, _ in parts], [acc for _, acc in parts]


def _whole(both):
    return [b.reshape(2 * b.shape[1], b.shape[2]) for b in both]


def _step(x, p, tgt, rep, place, ts):
    vec = lambda k: rep[k].reshape(1, -1)
    tall = min(2 * ts, x.shape[0])
    pool_w = rep["pool_w"].astype(BF16)
    wg = _gates_block_diag(rep["w_rg_gates"]).astype(BF16)
    sq = lambda t: t.reshape(D_MODEL, D_MODEL)
    by4 = lambda t: t.reshape(N_CHIPS, -1, D_MODEL)

    first, ride1, ride2 = (("w_in", "w_pool_out", "tiny"), ("w_rg_out", "w_o", "w_down"),
                           ("w_up", "w_ple_gate", "w_ple_proj"))
    later = ride1 + ride2
    tiny = _pack_tiny(rep["conv_w"][None], rep["b_rg_gates"][None], rep["ffn_conv_w"][None])[0]
    own_first, _ = _own_slots([rep["w_in"], rep["w_pool_out"], tiny], [BF16, BF16, F32], "own_slots_first")
    own_later, (got,) = _own_slots([rep[k] for k in later], [BF16] * len(later), "own_slots_gather_first",
                                   [_gather_task(own_first)])
    own = dict(zip(later, own_later))
    full = dict(zip(first, got))
    conv_w, b_gates, fcw = [_shards_to_cols(t) for t in _unpack_tiny(full["tiny"])]

    (urx, urg, gp, gr, d, ypool, h1), (got,) = _fwd_in_pool(
        x, vec("g_mix_pre"), full["w_in"], pool_w, vec("pool_scale"), full["w_pool_out"], ts,
        [_gather_task([own[k] for k in ride1], relay_steps=(6, 2))])
    full.update(zip(ride1, got))
    w_rg_out, w_o, w_down = sq(full["w_rg_out"]), sq(full["w_o"]), full["w_down"].reshape(D_FF, D_MODEL)
    (xc, r, ig, h, yrnn, mo, x1, glr, ggr, sp, sr), (got,) = _fwd_rnn_merge(
        urx, urg, gp, gr, ypool, x, conv_w, vec("conv_b"), wg, b_gates, vec("lru_lambda"), w_rg_out, w_o,
        vec("g_mix_post"), ts, [_gather_task([own[k] for k in ride2], relay_steps=(7, 3))])
    full.update(zip(ride2, got))
    (up, gl, gg, h2, dn, x2), _ = _fwd_ffn(x1, vec("g_ffn_pre"), full["w_up"], fcw, vec("ffn_conv_b"), w_down,
                                           vec("g_ffn_post"), ts)
    dx2, loss, d_w_gate, d_w_proj, d_g_ple_gate, d_g_ple_post = _ple_loss(
        x2, p, tgt, vec("g_ple_gate"), sq(full["w_ple_gate"]), full["w_ple_proj"], vec("g_ple_post"), tall)
    dup, d_w_down, d_fcw, d_fcb, d_g_ffn_post = _bwd_ffn_down(dx2, dn, up, gl, gg, fcw, w_down, vec("g_ffn_post"), ts)

    names1, grads1 = ("w_ple_gate", "w_ple_proj", "w_down"), [by4(d_w_gate), d_w_proj, by4(d_w_down)]
    (dx1, d_g_ffn_pre), (got1,) = _bwd_ffn_up(dup, x1, dx2, vec("g_ffn_pre"), full["w_up"], tall, [_halves_task(grads1)])
    (send_w_up, acc_w_up), (accs1,) = _dw_up(h2, dup, ts, [_exchange_task(*_partials(grads1, got1, place))])
    (dgp, dgr, dyp, dyr, d_w_o, d_g_mix_post), (theirs1,) = _bwd_merge(
        dx1, mo, sp, sr, ypool, yrnn, vec("g_mix_post"), w_o, tall, [_swap_task(accs1)])
    (durx, durg, d_w_rg_out, d_wg, d_conv_w, d_conv_b, d_b_gates, d_lam), (accs2,) = _bwd_rnn(
        dyr, urx, glr, ggr, xc, r, ig, h, conv_w, wg, vec("lru_lambda"), w_rg_out, ts,
        [_exchange_task([send_w_up], [acc_w_up])])
    names3 = ("w_o", "w_rg_out", "tiny", "w_rg_gates")
    grads3 = [by4(d_w_o), by4(d_w_rg_out),
              _pack_tiny(_cols_to_shards(d_conv_w, 256), _cols_to_shards(d_b_gates, 256), _cols_to_shards(d_fcw, 768)),
              _gates_from_block_diag(d_wg).reshape(1, 2 * RNN_HEADS * RNN_HEAD_DIM, RNN_HEAD_DIM)]
    (dzp, d_w_pool_out, d_pool_w, d_pool_scale), (got3, theirs2) = _bwd_pool(
        dyp, d, pool_w, vec("pool_scale"), full["w_pool_out"], tall, [_halves_task(grads3), _swap_task(accs2)])
    replicated = {"g_mix_post": d_g_mix_post, "conv_b": d_conv_b, "lru_lambda": d_lam, "g_ffn_pre": d_g_ffn_pre,
                  "g_ffn_post": d_g_ffn_post, "g_ple_gate": d_g_ple_gate, "g_ple_post": d_g_ple_post,
                  "pool_scale": d_pool_scale, "ffn_conv_b": d_fcb, "pool_w": d_pool_w}
    names4 = ("w_in", "w_pool_out", "small")
    small4 = [d_w_pool_out, _pack_small(replicated, loss)[None]]
    (send_w_in, acc_w_in), (accs3, got_small4) = _dw_in(
        h1, dzp, durx, durg, dgp, dgr, ts, [_exchange_task(*_partials(grads3, got3, place)), _halves_task(small4)])
    sends4, accs4 = _partials(small4, got_small4, place)
    (grad_x, d_g_mix_pre), (accs4, theirs3, both3) = _bwd_in(
        dzp, durx, durg, dgp, dgr, x, dx1, vec("g_mix_pre"), full["w_in"], tall,
        [_exchange_task([send_w_in] + sends4, [acc_w_in] + accs4), _swap_task(accs3[:2]),
         _share_task([_chip_sum(acc, place) for acc in accs3[2:]])])
    theirs4, both4, (g_mix_pre_parts,) = _run(
        [_swap_task(accs4[:2]), _share_task([_chip_sum(acc, place) for acc in accs4[2:]]),
         _all_devices_task([d_g_mix_pre.reshape(SUBLANES, SMALL_COLS)])], "grad_sibling_share")
    mine = accs1 + accs2 + accs3[:2] + accs4[:2]
    partials = dict(zip(names1 + ("w_up",) + names3[:2] + names4[:2], zip(mine, theirs1 + theirs2 + theirs3 + theirs4)))
    return grad_x, partials, dict(zip(names3[2:] + names4[2:], _whole(both3) + _whole(both4))), g_mix_pre_parts


def kernel(x, p, g_mix_pre, g_mix_post, w_in, pool_w, pool_scale, w_pool_out, conv_w, conv_b, w_rg_gates, b_rg_gates, lru_lambda, w_rg_out, w_o, g_ffn_pre, g_ffn_post, w_up, ffn_conv_w, ffn_conv_b, w_down, g_ple_gate, w_ple_gate, w_ple_proj, g_ple_post, loss_target, m_g_mix_pre, m_g_mix_post, m_w_in, m_pool_w, m_pool_scale, m_w_pool_out, m_conv_w, m_conv_b, m_w_rg_gates, m_b_rg_gates, m_lru_lambda, m_w_rg_out, m_w_o, m_g_ffn_pre, m_g_ffn_post, m_w_up, m_ffn_conv_w, m_ffn_conv_b, m_w_down, m_g_ple_gate, m_w_ple_gate, m_w_ple_proj, m_g_ple_post, v_g_mix_pre, v_g_mix_post, v_w_in, v_pool_w, v_pool_scale, v_w_pool_out, v_conv_w, v_conv_b, v_w_rg_gates, v_b_rg_gates, v_lru_lambda, v_w_rg_out, v_w_o, v_g_ffn_pre, v_g_ffn_post, v_w_up, v_ffn_conv_w, v_ffn_conv_b, v_w_down, v_g_ple_gate, v_w_ple_gate, v_w_ple_proj, v_g_ple_post):
    args = dict(locals())
    w = {k: args[k][0] for k in _WEIGHTS}
    m = {k: args["m_" + k][0] for k in _WEIGHTS}
    v = {k: args["v_" + k][0] for k in _WEIGHTS}
    place = jnp.stack([2 * lax.axis_index("x") + lax.axis_index("y"), lax.axis_index("c")]).astype(jnp.int32)
    grad_x, partials, reduced, g_mix_pre_parts = _step(x[0], p[0, 0], loss_target[0], w, place, ROW_TILE)

    gates_2d = (2 * RNN_HEADS * RNN_HEAD_DIM, RNN_HEAD_DIM)
    as2d = lambda k, shape: tuple(t[k].reshape(shape) for t in (w, m, v))
    done = {k: tuple(_adamw_sum(w[k], m[k], v[k], *partials[k], place)) for k in _SHARDED}
    gates_w, gates_m, gates_v = as2d("w_rg_gates", gates_2d)
    done["w_rg_gates"] = tuple(_adamw(gates_w, reduced["w_rg_gates"], gates_m, gates_v))
    tiny_names = ("conv_w", "b_rg_gates", "ffn_conv_w")
    tiny_at = ((slice(0, 4), slice(0, 256)), (slice(4, 6), slice(0, 256)), (slice(8, 11), slice(None)))
    done.update(zip(tiny_names, _adamw_pieces(
        reduced["tiny"], [(w[k], m[k], v[k]) + at for k, at in zip(tiny_names, tiny_at)], "adamw_tiny")))
    vector_rows, loss_row = _vector_rows()
    pieces = [as2d("pool_w", (POOL_W_ROWS, SMALL_COLS)) + (slice(0, POOL_W_ROWS), slice(None))]
    pieces += [as2d(k, (1, -1)) + (vector_rows[k],) for k in _VECTORS]
    done.update(zip(("pool_w",) + _VECTORS, _adamw_pieces(reduced["small"], pieces, "adamw_small")))
    done["g_mix_pre"] = _adamw_pieces(g_mix_pre_parts, [as2d("g_mix_pre", (1, -1)) + (0,)], "adamw_g_mix_pre")[0]

    result = [reduced["small"][loss_row, 0], grad_x[None]]
    for kind in range(4):
        result += [done[k][kind].reshape(args[k].shape) for k in _WEIGHTS]
    return tuple(result)
```

```python
import functools

import jax
import jax.numpy as jnp
from jax import lax
from jax.experimental import pallas as pl
from jax.experimental.pallas import tpu as pltpu

F32 = jnp.float32
BF16 = jnp.bfloat16

D_MODEL = 1024
POOL_WINDOWS = (2, 4, 8, 16)
POOL_GROUPS = 4
POOL_WIDTH = 512
POOL_GROUP_DIM = 128
RNN_HEADS = 16
RNN_HEAD_DIM = 64
GATE_BLOCK = 256
GATE_BLOCKS = D_MODEL // GATE_BLOCK
LRU_C = 8.0
D_FF = 3072
PLE_DIM = 256
RMS_EPS = 1e-6
IN_TOTAL = 4608
N_CHIPS = 4
IN_SHARD = IN_TOTAL // N_CHIPS
UP_SHARD = 2 * D_FF // N_CHIPS
Z_GROUPS = ((0, 512), (512, 1536), (1536, 2560), (2560, 3584), (3584, 4608))
POOL_HALO = 16
CONV_HALO = 8

ADAM_LR = 0.001
ADAM_B1 = 0.9
ADAM_B2 = 0.999
ADAM_EPS = 1e-08
ADAM_WD = 0.01
ADAM_STEP = 10

VMEM_LIMIT = 56 * 1024 * 1024
MESH = pl.DeviceIdType.MESH

_GELU_C = 0.7978845608028654
_GELU_A = 0.044715


def _dot(a, b):
    return jnp.dot(a.astype(BF16), b.astype(BF16), preferred_element_type=F32)


def _dot_nt(a, b):
    return lax.dot_general(a.astype(BF16), b.astype(BF16), (((1,), (1,)), ((), ())), preferred_element_type=F32)


def _dot_tn(a, b):
    return lax.dot_general(a.astype(BF16), b.astype(BF16), (((0,), (0,)), ((), ())), preferred_element_type=F32)


def _overlaps(group):
    a, b = Z_GROUPS[group]
    found = []
    for j in range(N_CHIPS):
        lo, hi = max(a, j * IN_SHARD), min(b, (j + 1) * IN_SHARD)
        if lo < hi:
            found.append((j, slice(lo - j * IN_SHARD, hi - j * IN_SHARD), slice(lo - a, hi - a)))
    return found


def _rms_fwd(x, g):
    r = lax.rsqrt(jnp.mean(x * x, axis=-1, keepdims=True) + RMS_EPS)
    xh = x * r
    return xh * g, xh, r


def _rms_bwd(xh, r, g, dy):
    dxh = dy * g
    dg = jnp.sum(dy * xh, axis=0, keepdims=True)
    dx = r * (dxh - xh * jnp.mean(dxh * xh, axis=-1, keepdims=True))
    return dx, dg


def _sigmoid(x):
    return 0.5 * jnp.tanh(0.5 * x) + 0.5


def _gelu(x):
    t = jnp.tanh(_GELU_C * (x + _GELU_A * x * x * x))
    return 0.5 * x * (1.0 + t), t


def _gelu_grad(x, t):
    return 0.5 * (1.0 + t) + 0.5 * x * (1.0 - t * t) * _GELU_C * (1.0 + 3.0 * _GELU_A * x * x)


def _softplus_neg(lam):
    nl = -lam
    return jnp.maximum(nl, 0.0) + jnp.log(1.0 + jnp.exp(-jnp.abs(nl)))


def _lru_coeffs(r, lam, first_row):
    c8 = LRU_C * _softplus_neg(lam)
    la = -(c8 * r)
    a = jnp.exp(la)
    m2 = jnp.tanh(-la) * (1.0 + a * a)
    mult = jnp.where(first_row, 1.0, jnp.sqrt(m2))
    return c8, a, m2, mult


SUBLANES = 8


def _scan_fwd(a, u, carry):
    n = a.shape[0]
    sub = lax.broadcasted_iota(jnp.int32, (n, 1), 0) % SUBLANES
    acc_a, acc_h = a, u
    for s in (1, 2, 4):
        m = sub >= s
        h_s = jnp.where(m, pltpu.roll(acc_h, s, 0), 0.0)
        a_s = jnp.where(m, pltpu.roll(acc_a, s, 0), 1.0)
        acc_h = acc_a * h_s + acc_h
        acc_a = acc_a * a_s
    out = []
    for g in range(n // SUBLANES):
        rows = slice(g * SUBLANES, (g + 1) * SUBLANES)
        out.append(acc_h[rows] + acc_a[rows] * carry)
        carry = out[-1][SUBLANES - 1:SUBLANES]
    return jnp.concatenate(out, axis=0)


def _scan_bwd(b, g, carry):
    n = b.shape[0]
    sub = lax.broadcasted_iota(jnp.int32, (n, 1), 0) % SUBLANES
    acc_b, acc_l = b, g
    for s in (1, 2, 4):
        m = sub < SUBLANES - s
        l_s = jnp.where(m, pltpu.roll(acc_l, n - s, 0), 0.0)
        b_s = jnp.where(m, pltpu.roll(acc_b, n - s, 0), 1.0)
        acc_l = acc_b * l_s + acc_l
        acc_b = acc_b * b_s
    out = [None] * (n // SUBLANES)
    for g in reversed(range(n // SUBLANES)):
        rows = slice(g * SUBLANES, (g + 1) * SUBLANES)
        out[g] = acc_l[rows] + acc_b[rows] * carry
        carry = out[g][0:1]
    return jnp.concatenate(out, axis=0)


def _shift_down(ext, k, halo):
    return pltpu.roll(ext, k, 0)[halo:] if k else ext[halo:]


def _shift_up(ext, k, ts):
    return pltpu.roll(ext, ext.shape[0] - k, 0)[:ts] if k else ext[:ts]


def _rows(ts, width, nt=None, col=0):
    if nt is None:
        return pl.BlockSpec((ts, width), lambda i: (i, col))
    return pl.BlockSpec((ts, width), lambda i: (nt - 1 - i, col))


def _resident(shape):
    zeros = (0,) * len(shape)
    return pl.BlockSpec(shape, lambda i: zeros, pipeline_mode=pl.Buffered(1))


def _acc(shape):
    zeros = (0,) * len(shape)
    return pl.BlockSpec(shape, lambda i: zeros)


def _params():
    return pltpu.CompilerParams(dimension_semantics=("arbitrary",), vmem_limit_bytes=VMEM_LIMIT)


def _sds(shape, dtype=F32):
    return jax.ShapeDtypeStruct(shape, dtype)


class _Task:
    def __init__(self, ins, out_shapes, aliases, sems, start, finish, relays=()):
        self.ins, self.out_shapes, self.aliases, self.sems = list(ins), list(out_shapes), dict(aliases), list(sems)
        self.start, self.relays, self.finish = start, list(relays), finish


def _call(body, name, grid, in_specs, out_specs, out_shape, scratch_shapes, args, tasks=()):
    n_in, n_out, n_scr = len(in_specs), len(out_specs), len(scratch_shapes)
    t_in = [len(t.ins) for t in tasks]
    t_out = [len(t.out_shapes) for t in tasks]
    t_sem = [len(t.sems) for t in tasks]
    steps = 1
    for g in grid:
        steps *= g

    def take(refs, pos, counts):
        groups = []
        for c in counts:
            groups.append(refs[pos:pos + c])
            pos += c
        return groups, pos

    def wrapped(*refs):
        (cin,), pos = take(refs, 0, [n_in])
        tin, pos = take(refs, pos, t_in)
        (cout,), pos = take(refs, pos, [n_out])
        tout, pos = take(refs, pos, t_out)
        (cscr,), pos = take(refs, pos, [n_scr])
        tsem, pos = take(refs, pos, t_sem)
        if not grid:
            for t, a, b, c in zip(tasks, tin, tout, tsem):
                t.start(a, b, c)
            if body is not None:
                body(*cin, *cout, *cscr)
            for t, a, b, c in zip(tasks, tin, tout, tsem):
                for relay, _ in t.relays:
                    relay(a, b, c)
            for t, a, b, c in zip(tasks, tin, tout, tsem):
                t.finish(a, b, c)
            return
        step = pl.program_id(0)
        for axis in range(1, len(grid)):
            step = step * grid[axis] + pl.program_id(axis)
        if tasks:
            @pl.when(step == 0)
            def _():
                for t, a, b, c in zip(tasks, tin, tout, tsem):
                    t.start(a, b, c)

        body(*cin, *cout, *cscr)
        for t, a, b, c in zip(tasks, tin, tout, tsem):
            for relay, before in t.relays:
                pl.when(step == max(steps - 1 - before, 0))(functools.partial(relay, a, b, c))

        if tasks:
            @pl.when(step == steps - 1)
            def _():
                for t, a, b, c in zip(tasks, tin, tout, tsem):
                    t.finish(a, b, c)

    aliases, in_pos, out_pos = {}, n_in, n_out
    for t, ni, no in zip(tasks, t_in, t_out):
        aliases.update({in_pos + a: out_pos + b for a, b in t.aliases.items()})
        in_pos, out_pos = in_pos + ni, out_pos + no
    any_spec = pl.BlockSpec(memory_space=pltpu.HBM)
    kwargs = dict(grid=grid, compiler_params=pltpu.CompilerParams(
        dimension_semantics=("arbitrary",) * len(grid), vmem_limit_bytes=VMEM_LIMIT)) if grid else dict(
        compiler_params=pltpu.CompilerParams(vmem_limit_bytes=VMEM_LIMIT))
    out = pl.pallas_call(
        wrapped, name=name,
        in_specs=list(in_specs) + [any_spec] * sum(t_in),
        out_specs=list(out_specs) + [any_spec] * sum(t_out),
        out_shape=list(out_shape) + [s for t in tasks for s in t.out_shapes],
        scratch_shapes=list(scratch_shapes) + [s for t in tasks for s in t.sems],
        input_output_aliases=aliases, **kwargs,
    )(*args, *[pltpu.with_memory_space_constraint(a, pltpu.HBM) for t in tasks for a in t.ins])
    task_outs, pos = take(list(out), n_out, t_out)
    return list(out[:n_out]), task_outs


def _fwd_in_pool(x, g_pre, w_in, pool_w, pool_scale, w_pool_out, ts, tasks=()):
    s = x.shape[0]

    def body(x_ref, g_ref, win_ref, pw_ref, ps_ref, wpo_ref,
             urx_ref, urg_ref, gp_ref, gr_ref, d_ref, yp_ref, h1_ref, halo_scr):
        i = pl.program_id(0)

        @pl.when(i == 0)
        def _():
            halo_scr[...] = jnp.zeros_like(halo_scr)

        h1, _, _ = _rms_fwd(x_ref[...], g_ref[...])
        h1 = h1.astype(BF16)
        h1_ref[...] = h1
        u = jnp.dot(h1, win_ref[0, :, 0:POOL_WIDTH], preferred_element_type=F32)
        for group, out_ref in zip(range(1, len(Z_GROUPS)), (urx_ref, urg_ref, gp_ref, gr_ref)):
            for j, shard_cols, group_cols in _overlaps(group):
                out_ref[:, group_cols] = jnp.dot(h1, win_ref[j, :, shard_cols], preferred_element_type=F32)
        ext = jnp.concatenate([halo_scr[...], u], axis=0)
        halo_scr[...] = u[ts - POOL_HALO:, :]
        t = i * ts + lax.broadcasted_iota(jnp.int32, (ts, 1), 0)
        y4 = []
        for g, w in enumerate(POOL_WINDOWS):
            lanes = slice(g * POOL_GROUP_DIM, (g + 1) * POOL_GROUP_DIM)
            acc = ext[:, lanes]
            sh = 1
            while sh < w:
                acc = acc + pltpu.roll(acc, sh, 0)
                sh *= 2
            inv = 1.0 / jnp.minimum(t + 1, w).astype(F32)
            dg = acc[POOL_HALO:, :] * inv - u[:, lanes]
            d_ref[:, lanes] = dg.astype(BF16)
            y4.append(_dot(dg, pw_ref[g]))
        ypre = jnp.concatenate(y4, axis=1) * ps_ref[...]
        ypre = ypre.astype(BF16)
        for j in range(N_CHIPS):
            yp_ref[:, j * 256:(j + 1) * 256] = jnp.dot(ypre, wpo_ref[j], preferred_element_type=F32)

    return _call(
        body, "fwd_in_pool", (s // ts,),
        [_rows(ts, D_MODEL), _resident((1, D_MODEL)), _resident(w_in.shape), _resident(pool_w.shape),
         _resident((1, POOL_WIDTH)), _resident(w_pool_out.shape)],
        [_rows(ts, D_MODEL)] * 4 + [_rows(ts, POOL_WIDTH), _rows(ts, D_MODEL), _rows(ts, D_MODEL)],
        [_sds((s, D_MODEL))] * 4 + [_sds((s, POOL_WIDTH), BF16), _sds((s, D_MODEL)), _sds((s, D_MODEL), BF16)],
        [pltpu.VMEM((POOL_HALO, POOL_WIDTH), F32)],
        (x, g_pre, w_in, pool_w, pool_scale, w_pool_out), tasks)


def _fwd_rnn_merge(urx, urg, gp, gr, ypool, x, conv_w, conv_b, wg, bg, lam, w_rg_out, w_o, g_post, ts, tasks=()):
    s = x.shape[0]

    def body(urx_ref, urg_ref, gp_ref, gr_ref, yp_ref, x_ref, cw_ref, cb_ref, wg_ref, bg_ref, lam_ref, wrg_ref, wo_ref,
             gpost_ref, xc_ref, r_ref, ig_ref, h_ref, yr_ref, mo_ref, x1_ref, gl_ref, gg_ref, sp_ref, sr_ref,
             halo_scr, carry_scr):
        i = pl.program_id(0)

        @pl.when(i == 0)
        def _():
            halo_scr[...] = jnp.zeros_like(halo_scr)
            carry_scr[...] = jnp.zeros_like(carry_scr)

        urx_v = urx_ref[...]
        ext = jnp.concatenate([halo_scr[...], urx_v], axis=0)
        halo_scr[...] = urx_v[ts - CONV_HALO:, :]
        cw = cw_ref[...]
        xc = (cb_ref[...] + cw[3:4] * urx_v + cw[2:3] * _shift_down(ext, 1, CONV_HALO)
              + cw[1:2] * _shift_down(ext, 2, CONV_HALO) + cw[0:1] * _shift_down(ext, 3, CONV_HALO))
        xc_ref[...] = xc
        xcb = xc.astype(BF16)
        lin = []
        for gate in range(2):
            parts = [jnp.dot(xcb[:, q * GATE_BLOCK:(q + 1) * GATE_BLOCK], wg_ref[gate, q], preferred_element_type=F32)
                     for q in range(GATE_BLOCKS)]
            lin.append(jnp.concatenate(parts, axis=1) + bg_ref[gate:gate + 1, :])
        r = _sigmoid(lin[0])
        ig = _sigmoid(lin[1])
        r_ref[...] = r
        ig_ref[...] = ig
        first_row = (i * ts + lax.broadcasted_iota(jnp.int32, (ts, 1), 0)) == 0
        _, a, _, mult = _lru_coeffs(r, lam_ref[...], first_row)
        h = _scan_fwd(a, mult * ig * xc, carry_scr[0:1, :])
        carry_scr[0:1, :] = h[ts - 1:ts, :]
        h_ref[...] = h
        urg_v = urg_ref[...]
        gl, t = _gelu(urg_v)
        gl_ref[...] = gl.astype(BF16)
        gg_ref[...] = _gelu_grad(urg_v, t).astype(BF16)
        yr = _dot(h * gl, wrg_ref[...])
        yr_ref[...] = yr
        sp = _sigmoid(gp_ref[...])
        sr = _sigmoid(gr_ref[...])
        sp_ref[...] = sp.astype(BF16)
        sr_ref[...] = sr.astype(BF16)
        merged = sp * yp_ref[...] + sr * yr
        mo = _dot(merged, wo_ref[...])
        mo_ref[...] = mo
        y, _, _ = _rms_fwd(mo, gpost_ref[...])
        x1_ref[...] = x_ref[...] + y

    row = _rows(ts, D_MODEL)
    return _call(
        body, "fwd_rnn_merge", (s // ts,),
        [row] * 6 + [_resident(conv_w.shape), _resident((1, D_MODEL)), _resident(wg.shape), _resident(bg.shape),
                     _resident((1, D_MODEL)), _resident(w_rg_out.shape), _resident(w_o.shape), _resident((1, D_MODEL))],
        [row] * 11, [_sds((s, D_MODEL))] * 7 + [_sds((s, D_MODEL), BF16)] * 4,
        [pltpu.VMEM((CONV_HALO, D_MODEL), F32), pltpu.VMEM((8, D_MODEL), F32)],
        (urx, urg, gp, gr, ypool, x, conv_w, conv_b, wg, bg, lam, w_rg_out, w_o, g_post), tasks)


def _fwd_ffn(x1, g_pre, w_up, fcw, fcb, w_down, g_post, ts, tasks=()):
    s = x1.shape[0]

    def body(x1_ref, g_ref, wup_ref, fcw_ref, fcb_ref, wd_ref, gpost_ref,
             up_ref, gl_ref, gg_ref, h2_ref, dn_ref, x2_ref, up_scr, halo_scr):
        i = pl.program_id(0)

        @pl.when(i == 0)
        def _():
            halo_scr[...] = jnp.zeros_like(halo_scr)

        x1_v = x1_ref[...]
        h2, _, _ = _rms_fwd(x1_v, g_ref[...])
        h2 = h2.astype(BF16)
        h2_ref[...] = h2
        for j in range(N_CHIPS):
            up_scr[:, j * UP_SHARD:(j + 1) * UP_SHARD] = jnp.dot(h2, wup_ref[j], preferred_element_type=F32)
        up_ref[...] = up_scr[...].astype(BF16)
        ug = up_scr[:, 0:D_FF]
        ext = jnp.concatenate([halo_scr[...], ug], axis=0)
        halo_scr[...] = ug[ts - CONV_HALO:, :]
        w = fcw_ref[...]
        gh = (fcb_ref[...] + w[2:3] * ug + w[1:2] * _shift_down(ext, 1, CONV_HALO)
              + w[0:1] * _shift_down(ext, 2, CONV_HALO))
        gl, t = _gelu(gh)
        gl_ref[...] = gl.astype(BF16)
        gg_ref[...] = _gelu_grad(gh, t).astype(BF16)
        dn = _dot(gl * up_scr[:, D_FF:], wd_ref[...])
        dn_ref[...] = dn
        y, _, _ = _rms_fwd(dn, gpost_ref[...])
        x2_ref[...] = x1_v + y

    row = _rows(ts, D_MODEL)
    return _call(
        body, "fwd_ffn", (s // ts,),
        [row, _resident((1, D_MODEL)), _resident(w_up.shape), _resident(fcw.shape), _resident((1, D_FF)),
         _resident(w_down.shape), _resident((1, D_MODEL))],
        [_rows(ts, 2 * D_FF), _rows(ts, D_FF), _rows(ts, D_FF), row, row, row],
        [_sds((s, 2 * D_FF), BF16), _sds((s, D_FF), BF16), _sds((s, D_FF), BF16), _sds((s, D_MODEL), BF16),
         _sds((s, D_MODEL)), _sds((s, D_MODEL))],
        [pltpu.VMEM((ts, 2 * D_FF), F32), pltpu.VMEM((CONV_HALO, D_FF), F32)],
        (x1, g_pre, w_up, fcw, fcb, w_down, g_post), tasks)


def _ple_loss(x2, p, tgt, g_gate, w_gate, w_proj, g_post, ts):
    s = x2.shape[0]

    def body(x2_ref, p_ref, t_ref, gg_ref, wg_ref, wp_ref, gp_ref, dx2_ref, loss_ref, dwg_ref, dwp_ref, dgg_ref, dgp_ref):
        @pl.when(pl.program_id(0) == 0)
        def _():
            loss_ref[...] = jnp.zeros_like(loss_ref)
            dwg_ref[...] = jnp.zeros_like(dwg_ref)
            dwp_ref[...] = jnp.zeros_like(dwp_ref)
            dgg_ref[...] = jnp.zeros_like(dgg_ref)
            dgp_ref[...] = jnp.zeros_like(dgp_ref)

        x2_v = x2_ref[...]
        n3, xh3, r3 = _rms_fwd(x2_v, gg_ref[...])
        pg = _sigmoid(_dot(n3, wg_ref[...]))
        pb = p_ref[...].astype(BF16)
        q = jnp.concatenate([jnp.dot(pb, wp_ref[j], preferred_element_type=F32) for j in range(N_CHIPS)], axis=1)
        ple, qh, rq = _rms_fwd(q, gp_ref[...])
        e = x2_v + pg * ple - t_ref[...]
        loss_ref[...] += 0.5 * jnp.sum(jnp.mean(e * e, axis=-1, keepdims=True), axis=0, keepdims=True)
        dy = e * (1.0 / D_MODEL)
        dpgl = dy * ple * pg * (1.0 - pg)
        dwg_ref[...] += _dot_tn(n3, dpgl)
        dx3, dgg = _rms_bwd(xh3, r3, gg_ref[...], _dot_nt(dpgl, wg_ref[...]))
        dgg_ref[...] += dgg
        dq, dgp = _rms_bwd(qh, rq, gp_ref[...], dy * pg)
        dgp_ref[...] += dgp
        for j in range(N_CHIPS):
            dwp_ref[j] += _dot_tn(pb, dq[:, j * 256:(j + 1) * 256])
        dx2_ref[...] = dy + dx3

    row = _rows(ts, D_MODEL)
    vec = _acc((1, D_MODEL))
    return pl.pallas_call(
        body, name="ple_loss", grid=(s // ts,),
        in_specs=[row, _rows(ts, PLE_DIM), row, _resident((1, D_MODEL)), _resident(w_gate.shape), _resident(w_proj.shape),
                  _resident((1, D_MODEL))],
        out_specs=[row, _acc((1, 128)), _acc(w_gate.shape), _acc(w_proj.shape), vec, vec],
        out_shape=[_sds((s, D_MODEL)), _sds((1, 128)), _sds(w_gate.shape), _sds(w_proj.shape), _sds((1, D_MODEL)),
                   _sds((1, D_MODEL))],
        compiler_params=_params(),
    )(x2, p, tgt, g_gate, w_gate, w_proj, g_post)


def _bwd_ffn_down(dx2, dn, up, gl, gg, fcw, w_down, g_post, ts):
    s = dx2.shape[0]
    nt = s // ts

    def body(dx2_ref, dn_ref, up_ref, gl_ref, gg_ref, fcw_ref, wd_ref, gpost_ref,
             dup_ref, dwd_ref, dfcw_ref, dfcb_ref, dgp_ref, carry_scr):
        i = pl.program_id(0)

        @pl.when(i == 0)
        def _():
            carry_scr[...] = jnp.zeros_like(carry_scr)
            dwd_ref[...] = jnp.zeros_like(dwd_ref)
            dfcw_ref[...] = jnp.zeros_like(dfcw_ref)
            dfcb_ref[...] = jnp.zeros_like(dfcb_ref)
            dgp_ref[...] = jnp.zeros_like(dgp_ref)

        _, xh, r = _rms_fwd(dn_ref[...], gpost_ref[...])
        ddn, dgp = _rms_bwd(xh, r, gpost_ref[...], dx2_ref[...])
        dgp_ref[...] += dgp
        dhid = _dot_nt(ddn, wd_ref[...])
        ug = up_ref[:, 0:D_FF].astype(F32)
        uv = up_ref[:, D_FF:].astype(F32)
        gl = gl_ref[...].astype(F32)
        w = fcw_ref[...]
        dwd_ref[...] += _dot_tn(gl * uv, ddn)
        dgh = dhid * uv * gg_ref[...].astype(F32)
        dup_ref[:, D_FF:] = (dhid * gl).astype(BF16)
        extd = jnp.concatenate([dgh, carry_scr[...]], axis=0)
        carry_scr[...] = dgh[0:CONV_HALO, :]
        d1 = _shift_up(extd, 1, ts)
        d2 = _shift_up(extd, 2, ts)
        dup_ref[:, 0:D_FF] = (w[2:3] * dgh + w[1:2] * d1 + w[0:1] * d2).astype(BF16)
        dfcw_ref[2:3, :] += jnp.sum(ug * dgh, axis=0, keepdims=True)
        dfcw_ref[1:2, :] += jnp.sum(ug * d1, axis=0, keepdims=True)
        dfcw_ref[0:1, :] += jnp.sum(ug * d2, axis=0, keepdims=True)
        dfcb_ref[...] += jnp.sum(dgh, axis=0, keepdims=True)

    row = _rows(ts, D_MODEL, nt)
    wide = _rows(ts, D_FF, nt)
    return pl.pallas_call(
        body, name="bwd_ffn_down", grid=(nt,),
        in_specs=[row, row, _rows(ts, 2 * D_FF, nt), wide, wide, _resident(fcw.shape), _resident(w_down.shape),
                  _resident((1, D_MODEL))],
        out_specs=[_rows(ts, 2 * D_FF, nt), _acc(w_down.shape), _acc(fcw.shape), _acc((1, D_FF)), _acc((1, D_MODEL))],
        out_shape=[_sds((s, 2 * D_FF), BF16), _sds(w_down.shape), _sds(fcw.shape), _sds((1, D_FF)), _sds((1, D_MODEL))],
        scratch_shapes=[pltpu.VMEM((CONV_HALO, D_FF), F32)],
        compiler_params=_params(),
    )(dx2, dn, up, gl, gg, fcw, w_down, g_post)


def _bwd_ffn_up(dup, x1, dx2, g_pre, w_up, ts, tasks=()):
    s = x1.shape[0]

    def body(dup_ref, x1_ref, dx2_ref, g_ref, wup_ref, dx1_ref, dg_ref):
        @pl.when(pl.program_id(0) == 0)
        def _():
            dg_ref[...] = jnp.zeros_like(dg_ref)

        _, xh, r = _rms_fwd(x1_ref[...], g_ref[...])
        dh2 = _dot_nt(dup_ref[:, 0:UP_SHARD], wup_ref[0])
        for j in range(1, N_CHIPS):
            dh2 = dh2 + _dot_nt(dup_ref[:, j * UP_SHARD:(j + 1) * UP_SHARD], wup_ref[j])
        dx, dg = _rms_bwd(xh, r, g_ref[...], dh2)
        dg_ref[...] += dg
        dx1_ref[...] = dx2_ref[...] + dx

    row = _rows(ts, D_MODEL)
    return _call(
        body, "bwd_ffn_up", (s // ts,),
        [_rows(ts, 2 * D_FF), row, row, _resident((1, D_MODEL)), _resident(w_up.shape)],
        [row, _acc((1, D_MODEL))], [_sds((s, D_MODEL)), _sds((1, D_MODEL))], [],
        (dup, x1, dx2, g_pre, w_up), tasks)


def _dw_up(h2, dup, ts, tasks=()):
    s = h2.shape[0]
    ts = min(DW_TILES * ts, s)
    nt = s // ts
    half = D_MODEL // 2

    def body(h2_ref, dup_ref, send_ref, acc_ref, theirs_scr, mine_scr, got_scr, sems):
        j, o, i = pl.program_id(0), pl.program_id(1), pl.program_id(2)
        x, y, c, me, _ = _place()
        prod = _dot_tn(h2_ref[...], dup_ref[...])
        for scr, which in ((theirs_scr, 0), (mine_scr, 1)):
            @pl.when((o == which) & (i == 0))
            def _():
                scr[...] = prod

            @pl.when((o == which) & (i > 0))
            def _():
                scr[...] += prod

        slot = j % 2
        push = _remote(theirs_scr, got_scr.at[slot], sems.at[0, slot], sems.at[1, slot], (x, y, 1 - c))
        pl.when((o == 1) & (i == 0))(push.start)

        @pl.when((o == 1) & (i == nt - 1))
        def _():
            push.wait()
            part = (mine_scr[...] + got_scr[slot]).astype(BF16)
            send_ref[0] = part

            @pl.when(me == j)
            def _():
                acc_ref[0] = part

    def h2_cols(j, o, i):
        c = lax.axis_index("c")
        return i, jnp.where(o == 0, 1 - c, c)

    def own_slot(j, o, i):
        return 2 * lax.axis_index("x") + lax.axis_index("y"), 0, 0

    block = (1, half, UP_SHARD)
    return _call(
        body, "dw_up", (N_CHIPS, 2, nt),
        [pl.BlockSpec((ts, half), h2_cols), pl.BlockSpec((ts, UP_SHARD), lambda j, o, i: (i, j))],
        [pl.BlockSpec(block, lambda j, o, i: (j, 0, 0)), pl.BlockSpec(block, own_slot)],
        [_sds((N_CHIPS, half, UP_SHARD), BF16)] * 2,
        [pltpu.VMEM((half, UP_SHARD), F32), pltpu.VMEM((half, UP_SHARD), F32), pltpu.VMEM((2, half, UP_SHARD), F32),
         pltpu.SemaphoreType.DMA((2, 2))],
        (h2, dup), tasks)


def _bwd_merge(dx1, mo, sp, sr, ypool, yrnn, g_post, w_o, ts, tasks=()):
    s = dx1.shape[0]

    def body(dx1_ref, mo_ref, sp_ref, sr_ref, yp_ref, yr_ref, g_ref, wo_ref,
             dgp_ref, dgr_ref, dyp_ref, dyr_ref, dwo_ref, dg_ref):
        @pl.when(pl.program_id(0) == 0)
        def _():
            dwo_ref[...] = jnp.zeros_like(dwo_ref)
            dg_ref[...] = jnp.zeros_like(dg_ref)

        _, xh, r = _rms_fwd(mo_ref[...], g_ref[...])
        dmo, dg = _rms_bwd(xh, r, g_ref[...], dx1_ref[...])
        dg_ref[...] += dg
        dmerged = _dot_nt(dmo, wo_ref[...])
        sp = sp_ref[...].astype(F32)
        sr = sr_ref[...].astype(F32)
        yp = yp_ref[...]
        yr = yr_ref[...]
        dwo_ref[...] += _dot_tn(sp * yp + sr * yr, dmo)
        dgp_ref[...] = (dmerged * yp * sp * (1.0 - sp)).astype(BF16)
        dgr_ref[...] = (dmerged * yr * sr * (1.0 - sr)).astype(BF16)
        dyp_ref[...] = (dmerged * sp).astype(BF16)
        dyr_ref[...] = (dmerged * sr).astype(BF16)

    row = _rows(ts, D_MODEL)
    return _call(
        body, "bwd_merge", (s // ts,),
        [row] * 6 + [_resident((1, D_MODEL)), _resident(w_o.shape)],
        [row] * 4 + [_acc(w_o.shape), _acc((1, D_MODEL))],
        [_sds((s, D_MODEL), BF16)] * 4 + [_sds(w_o.shape), _sds((1, D_MODEL))], [],
        (dx1, mo, sp, sr, ypool, yrnn, g_post, w_o), tasks)


def _bwd_rnn(dyr, urx, gl, gg, xc, r, ig, h, conv_w, wg, lam, w_rg_out, ts, tasks=()):
    s = urx.shape[0]
    nt = s // ts
    halo_blocks = ts // CONV_HALO

    def body(dyr_ref, urx_ref, gl_ref, gg_ref, xc_ref, r_ref, ig_ref, h_ref, hh_ref, cw_ref, wg_ref, lam_ref, wrg_ref,
             durx_ref, durg_ref, dwrg_ref, dwg_ref, dcw_ref, dcb_ref, dbg_ref, dlam_ref, mu_scr, carry_scr):
        i = pl.program_id(0)
        k = nt - 1 - i

        @pl.when(i == 0)
        def _():
            mu_scr[...] = jnp.zeros_like(mu_scr)
            carry_scr[...] = jnp.zeros_like(carry_scr)
            dwrg_ref[...] = jnp.zeros_like(dwrg_ref)
            dwg_ref[...] = jnp.zeros_like(dwg_ref)
            dcw_ref[...] = jnp.zeros_like(dcw_ref)
            dcb_ref[...] = jnp.zeros_like(dcb_ref)
            dbg_ref[...] = jnp.zeros_like(dbg_ref)
            dlam_ref[...] = jnp.zeros_like(dlam_ref)

        row = lax.broadcasted_iota(jnp.int32, (ts, 1), 0)
        first_row = (k * ts + row) == 0
        h = h_ref[...]
        dyr_v = dyr_ref[...]
        dhr = _dot_nt(dyr_v, wrg_ref[...])
        gl = gl_ref[...].astype(F32)
        dwrg_ref[...] += _dot_tn(h * gl, dyr_v)
        durg_ref[...] = (dhr * h * gg_ref[...].astype(F32)).astype(BF16)
        r_v = r_ref[...]
        ig_v = ig_ref[...]
        xc_v = xc_ref[...]
        lam_v = lam_ref[...]
        c8, a, m2, mult = _lru_coeffs(r_v, lam_v, first_row)
        b = jnp.where(row == ts - 1, 1.0, pltpu.roll(a, ts - 1, 0))
        lt = _scan_bwd(b, dhr * gl, mu_scr[0:1, :])
        mu_scr[0:1, :] = a[0:1, :] * lt[0:1, :]
        h_before = jnp.where(k > 0, hh_ref[CONV_HALO - 1:CONV_HALO, :], 0.0)
        hprev = jnp.where(row == 0, h_before, pltpu.roll(h, 1, 0))
        dmult = lt * ig_v * xc_v
        da = lt * hprev - jnp.where(first_row, 0.0, dmult * a * lax.rsqrt(m2))
        dla = da * a
        dlam_ref[...] += jnp.sum(dla * r_v, axis=0, keepdims=True)
        dlr = (dla * (-c8)) * r_v * (1.0 - r_v)
        dli = (lt * mult * xc_v) * ig_v * (1.0 - ig_v)
        dbg_ref[0:1, :] += jnp.sum(dlr, axis=0, keepdims=True)
        dbg_ref[1:2, :] += jnp.sum(dli, axis=0, keepdims=True)
        xcb = xc_v.astype(BF16)
        parts = []
        for q in range(GATE_BLOCKS):
            blk = slice(q * GATE_BLOCK, (q + 1) * GATE_BLOCK)
            dlr_q = dlr[:, blk].astype(BF16)
            dli_q = dli[:, blk].astype(BF16)
            parts.append(_dot_nt(dlr_q, wg_ref[0, q]) + _dot_nt(dli_q, wg_ref[1, q]))
            dwg_ref[0, q] += _dot_tn(xcb[:, blk], dlr_q)
            dwg_ref[1, q] += _dot_tn(xcb[:, blk], dli_q)
        dxc = lt * mult * ig_v + jnp.concatenate(parts, axis=1)
        extd = jnp.concatenate([dxc, carry_scr[...]], axis=0)
        carry_scr[...] = dxc[0:CONV_HALO, :]
        cw = cw_ref[...]
        urx_v = urx_ref[...]
        durx = cw[3:4] * dxc
        dcw_ref[3:4, :] += jnp.sum(urx_v * dxc, axis=0, keepdims=True)
        for j in (1, 2, 3):
            dj = _shift_up(extd, j, ts)
            durx = durx + cw[3 - j:4 - j] * dj
            dcw_ref[3 - j:4 - j, :] += jnp.sum(urx_v * dj, axis=0, keepdims=True)
        durx_ref[...] = durx.astype(BF16)
        dcb_ref[...] += jnp.sum(dxc, axis=0, keepdims=True)

        @pl.when(i == nt - 1)
        def _():
            dlam_ref[...] = dlam_ref[...] * (LRU_C * jax.nn.sigmoid(-lam_v))

    row_spec = _rows(ts, D_MODEL, nt)
    halo_spec = pl.BlockSpec((CONV_HALO, D_MODEL), lambda i: (jnp.maximum((nt - 1 - i) * halo_blocks - 1, 0), 0))
    vec = _acc((1, D_MODEL))
    return _call(
        body, "bwd_rnn", (nt,),
        [row_spec] * 8 + [halo_spec, _resident(conv_w.shape), _resident(wg.shape), _resident((1, D_MODEL)),
                          _resident(w_rg_out.shape)],
        [row_spec, row_spec, _acc(w_rg_out.shape), _acc(wg.shape), _acc(conv_w.shape), vec, _acc((2, D_MODEL)), vec],
        [_sds((s, D_MODEL), BF16), _sds((s, D_MODEL), BF16), _sds(w_rg_out.shape), _sds(wg.shape), _sds(conv_w.shape),
         _sds((1, D_MODEL)), _sds((2, D_MODEL)), _sds((1, D_MODEL))],
        [pltpu.VMEM((8, D_MODEL), F32), pltpu.VMEM((CONV_HALO, D_MODEL), F32)],
        (dyr, urx, gl, gg, xc, r, ig, h, h, conv_w, wg, lam, w_rg_out), tasks)


def _bwd_pool(dyp, d, pool_w, pool_scale, w_pool_out, ts, tasks=()):
    s = d.shape[0]
    nt = s // ts

    def body(dyp_ref, d_ref, pw_ref, ps_ref, wpo_ref, dzp_ref, dwpo_ref, dpw_ref, dps_ref, carry_scr):
        i = pl.program_id(0)
        k = nt - 1 - i

        @pl.when(i == 0)
        def _():
            carry_scr[...] = jnp.zeros_like(carry_scr)
            dwpo_ref[...] = jnp.zeros_like(dwpo_ref)
            dpw_ref[...] = jnp.zeros_like(dpw_ref)
            dps_ref[...] = jnp.zeros_like(dps_ref)

        dyp_v = dyp_ref[...]
        d_v = d_ref[...]
        ps = ps_ref[...]
        dypre = _dot_nt(dyp_v[:, 0:256], wpo_ref[0])
        for j in range(1, N_CHIPS):
            dypre = dypre + _dot_nt(dyp_v[:, j * 256:(j + 1) * 256], wpo_ref[j])
        y4 = jnp.concatenate([_dot(d_v[:, g * 128:(g + 1) * 128], pw_ref[g]) for g in range(POOL_GROUPS)], axis=1)
        ypre = (y4 * ps).astype(BF16)
        for j in range(N_CHIPS):
            dwpo_ref[j] += _dot_tn(ypre, dyp_v[:, j * 256:(j + 1) * 256])
        dps_ref[...] += jnp.sum(dypre * y4, axis=0, keepdims=True)
        dy4 = dypre * ps
        t = k * ts + lax.broadcasted_iota(jnp.int32, (ts, 1), 0)
        for g, w in enumerate(POOL_WINDOWS):
            lanes = slice(g * POOL_GROUP_DIM, (g + 1) * POOL_GROUP_DIM)
            dd = _dot_nt(dy4[:, lanes], pw_ref[g])
            dpw_ref[g] += _dot_tn(d_v[:, lanes], dy4[:, lanes])
            e = dd * (1.0 / jnp.minimum(t + 1, w).astype(F32))
            acc = jnp.concatenate([e, carry_scr[:, lanes]], axis=0)
            carry_scr[:, lanes] = e[0:POOL_HALO, :]
            n = ts + POOL_HALO
            sh = 1
            while sh < w:
                acc = acc + pltpu.roll(acc, n - sh, 0)
                sh *= 2
            dzp_ref[:, lanes] = (acc[:ts, :] - dd).astype(BF16)

    return _call(
        body, "bwd_pool", (nt,),
        [_rows(ts, D_MODEL, nt), _rows(ts, POOL_WIDTH, nt), _resident(pool_w.shape), _resident((1, POOL_WIDTH)),
         _resident(w_pool_out.shape)],
        [_rows(ts, POOL_WIDTH, nt), _acc(w_pool_out.shape), _acc(pool_w.shape), _acc((1, POOL_WIDTH))],
        [_sds((s, POOL_WIDTH), BF16), _sds(w_pool_out.shape), _sds(pool_w.shape), _sds((1, POOL_WIDTH))],
        [pltpu.VMEM((POOL_HALO, POOL_WIDTH), F32)],
        (dyp, d, pool_w, pool_scale, w_pool_out), tasks)


def _assemble_dz(dz_scr, dzp_ref, durx_ref, durg_ref, dgp_ref, dgr_ref):
    dz_scr[:, 0:512] = dzp_ref[...]
    dz_scr[:, 512:1536] = durx_ref[...]
    dz_scr[:, 1536:2560] = durg_ref[...]
    dz_scr[:, 2560:3584] = dgp_ref[...]
    dz_scr[:, 3584:4608] = dgr_ref[...]


def _dw_in(h1, dzp, durx, durg, dgp, dgr, ts, tasks=()):
    s = h1.shape[0]
    ts = min(2 * ts, s)
    nt = s // ts
    half = D_MODEL // 2

    def body(h1_ref, dzp_ref, durx_ref, durg_ref, dgp_ref, dgr_ref, send_ref, acc_ref, theirs_ref, mine_ref, got_ref, sems):
        o, i = pl.program_id(0), pl.program_id(1)

        @pl.when((o == 0) & (i == 0))
        def _():
            theirs_ref[...] = jnp.zeros_like(theirs_ref)
            mine_ref[...] = jnp.zeros_like(mine_ref)

        groups = (dzp_ref, durx_ref, durg_ref, dgp_ref, dgr_ref)
        for out_ref, which in ((theirs_ref, 0), (mine_ref, 1)):
            @pl.when(o == which)
            def _():
                for group, dz_ref in enumerate(groups):
                    for j, shard_cols, group_cols in _overlaps(group):
                        out_ref[j, :, shard_cols] += _dot_tn(h1_ref[...], dz_ref[:, group_cols])

        x, y, c, me, _ = _place()
        send = _remote(theirs_ref, got_ref, sems.at[0], sems.at[1], (x, y, 1 - c))
        pl.when((o == 1) & (i == 0))(send.start)

        @pl.when((o == 1) & (i == nt - 1))
        def _():
            send.wait()
            for k in range(N_CHIPS):
                part = (mine_ref[k] + got_ref[k]).astype(BF16)
                send_ref[k] = part

                @pl.when(me == k)
                def _():
                    acc_ref[k] = part

    def h1_cols(o, i):
        c = lax.axis_index("c")
        return i, jnp.where(o == 0, 1 - c, c)

    rows = lambda width: pl.BlockSpec((ts, width), lambda o, i: (i, 0))
    shape = (N_CHIPS, half, IN_SHARD)
    whole = pl.BlockSpec(shape, lambda o, i: (0, 0, 0))
    return _call(
        body, "dw_in", (2, nt), [pl.BlockSpec((ts, half), h1_cols), rows(POOL_WIDTH)] + [rows(D_MODEL)] * 4,
        [whole, whole], [_sds(shape, BF16)] * 2,
        [pltpu.VMEM(shape, F32)] * 3 + [pltpu.SemaphoreType.DMA((2,))], (h1, dzp, durx, durg, dgp, dgr), tasks)


def _bwd_in(dzp, durx, durg, dgp, dgr, x, dx1, g_pre, w_in, ts, tasks=()):
    s = x.shape[0]

    def body(dzp_ref, durx_ref, durg_ref, dgp_ref, dgr_ref, x_ref, dx1_ref, g_ref, win_ref, gx_ref, dg_ref, dz_scr):
        @pl.when(pl.program_id(0) == 0)
        def _():
            dg_ref[...] = jnp.zeros_like(dg_ref)

        _assemble_dz(dz_scr, dzp_ref, durx_ref, durg_ref, dgp_ref, dgr_ref)
        _, xh, r = _rms_fwd(x_ref[...], g_ref[...])
        dh1 = _dot_nt(dz_scr[:, 0:IN_SHARD], win_ref[0])
        for j in range(1, N_CHIPS):
            dh1 = dh1 + _dot_nt(dz_scr[:, j * IN_SHARD:(j + 1) * IN_SHARD], win_ref[j])
        dx, dg = _rms_bwd(xh, r, g_ref[...], dh1)
        dg_ref[...] += dg
        gx_ref[...] = dx1_ref[...] + dx

    row = _rows(ts, D_MODEL)
    return _call(
        body, "bwd_in", (s // ts,),
        [_rows(ts, POOL_WIDTH)] + [row] * 6 + [_resident((1, D_MODEL)), _resident(w_in.shape)],
        [row, _acc((1, D_MODEL))], [_sds((s, D_MODEL)), _sds((1, D_MODEL))],
        [pltpu.VMEM((ts, IN_TOTAL), BF16)], (dzp, durx, durg, dgp, dgr, x, dx1, g_pre, w_in), tasks)


def _place():
    x, y, c = lax.axis_index("x"), lax.axis_index("y"), lax.axis_index("c")
    others = [(1 - x, y), (x, 1 - y), (1 - x, 1 - y)]
    return x, y, c, 2 * x + y, others


def _remote(src, dst, send_sem, recv_sem, to):
    return pltpu.make_async_remote_copy(src_ref=src, dst_ref=dst, send_sem=send_sem, recv_sem=recv_sem,
                                        device_id=to, device_id_type=MESH)


def _own_slots(ws, dtypes, name, tasks=()):
    n = len(ws)
    hbm = pl.BlockSpec(memory_space=pltpu.HBM)

    def body(*refs):
        srcs, outs, f32_bufs, cast_bufs, sems = refs[:n], refs[n:2 * n], refs[2 * n:3 * n], refs[3 * n:4 * n], refs[4 * n]
        me = _place()[3]
        loads = [pltpu.make_async_copy(srcs[k], f32_bufs[k], sems.at[k, 0]) for k in range(n)]
        stores = [pltpu.make_async_copy(cast_bufs[k], outs[k].at[me], sems.at[k, 1]) for k in range(n)]
        for cp in loads:
            cp.start()
        for k in range(n):
            loads[k].wait()
            cast_bufs[k][...] = f32_bufs[k][...].astype(dtypes[k])
            stores[k].start()
        for cp in stores:
            cp.wait()

    return _call(
        body, name, (), [hbm] * n, [hbm] * n, [_sds((N_CHIPS,) + w.shape, dt) for w, dt in zip(ws, dtypes)],
        [pltpu.VMEM(w.shape, F32) for w in ws] + [pltpu.VMEM(w.shape, dt) for w, dt in zip(ws, dtypes)]
        + [pltpu.SemaphoreType.DMA((n, 2))],
        [pltpu.with_memory_space_constraint(w, pltpu.HBM) for w in ws], tasks)


def _run(tasks, name):
    if isinstance(tasks, _Task):
        return _call(None, name, (), [], [], [], [], (), (tasks,))[1][0]
    return _call(None, name, (), [], [], [], [], (), tuple(tasks))[1]


def _gather_task(bufs, relay_steps=(0, 0)):
    n = len(bufs)
    NBR_X, NBR_Y, QUARTER_VIA_Y, QUARTER_VIA_X, SIB_X, SIB_Y, SIB_DIAG = range(7)

    def parts(out):
        x, y, c, me, _ = _place()
        ah = out.shape[1] // 2
        q = ah // 2 if (ah // 2) % 16 == 0 else ah
        return c * ah, ah, q

    def copy(out, w, k, chip, row0, rows, to, sems):
        slot = out.at[chip, pl.ds(row0, rows)]
        return _remote(slot, slot, sems[0].at[w, k], sems[1].at[w, k], to)

    def plan(out, w, sems):
        x, y, c, me, _ = _place()
        row0, ah, q = parts(out)
        xn, yn, dg = 2 * (1 - x) + y, 2 * x + (1 - y), 2 * (1 - x) + (1 - y)
        to_x, to_y, sib = (1 - x, y, c), (x, 1 - y, c), (x, y, 1 - c)
        other = (1 - c) * ah
        cp = functools.partial(copy, out, w, sems=sems)
        sends = {NBR_X: cp(NBR_X, me, row0, ah, to_x), NBR_Y: cp(NBR_Y, me, row0, ah, to_y),
                 QUARTER_VIA_Y: cp(QUARTER_VIA_Y, xn, row0, q, to_y), SIB_X: cp(SIB_X, xn, row0, ah, sib),
                 SIB_Y: cp(SIB_Y, yn, row0, ah, sib), SIB_DIAG: cp(SIB_DIAG, dg, row0, ah, sib)}
        lands = {NBR_X: cp(NBR_X, xn, row0, ah, to_x), NBR_Y: cp(NBR_Y, yn, row0, ah, to_y),
                 QUARTER_VIA_Y: cp(QUARTER_VIA_Y, dg, row0, q, to_y), SIB_X: cp(SIB_X, xn, other, ah, sib),
                 SIB_Y: cp(SIB_Y, yn, other, ah, sib), SIB_DIAG: cp(SIB_DIAG, dg, other, ah, sib)}
        if q < ah:
            sends[QUARTER_VIA_X] = cp(QUARTER_VIA_X, yn, row0 + q, ah - q, to_x)
            lands[QUARTER_VIA_X] = cp(QUARTER_VIA_X, dg, row0 + q, ah - q, to_x)
        return sends, lands

    def start(ins, outs, sems):
        for w, out in enumerate(outs):
            sends, _ = plan(out, w, sems)
            sends[NBR_X].start()
            sends[NBR_Y].start()

    def pass_neighbours(ins, outs, sems):
        for w, out in enumerate(outs):
            sends, lands = plan(out, w, sems)
            lands[NBR_X].wait_recv()
            sends[QUARTER_VIA_Y].start()
            sends[SIB_X].start()
            lands[NBR_Y].wait_recv()
            if QUARTER_VIA_X in sends:
                sends[QUARTER_VIA_X].start()
            sends[SIB_Y].start()

    def pass_diagonal(ins, outs, sems):
        for w, out in enumerate(outs):
            sends, lands = plan(out, w, sems)
            lands[QUARTER_VIA_Y].wait_recv()
            if QUARTER_VIA_X in lands:
                lands[QUARTER_VIA_X].wait_recv()
            sends[SIB_DIAG].start()

    def finish(ins, outs, sems):
        for w, out in enumerate(outs):
            sends, lands = plan(out, w, sems)
            for k in (SIB_X, SIB_Y, SIB_DIAG):
                lands[k].wait_recv()
        for w, out in enumerate(outs):
            sends, _ = plan(out, w, sems)
            for cp in sends.values():
                cp.wait_send()

    return _Task(bufs, [_sds(b.shape, b.dtype) for b in bufs], {i: i for i in range(n)},
                 [pltpu.SemaphoreType.DMA((n, 7)), pltpu.SemaphoreType.DMA((n, 7))], start, finish,
                 [(pass_neighbours, relay_steps[0]), (pass_diagonal, relay_steps[1])])


def _halves_task(grads):
    n = len(grads)

    def copy(src, out, w, sems):
        x, y, c, _, _ = _place()
        ah = out.shape[1]
        return _remote(src.at[:, pl.ds((1 - c) * ah, ah)], out, sems[0].at[w], sems[1].at[w], (x, y, 1 - c))

    def start(ins, outs, sems):
        for w, (src, out) in enumerate(zip(ins, outs)):
            copy(src, out, w, sems).start()

    def finish(ins, outs, sems):
        for w, (src, out) in enumerate(zip(ins, outs)):
            copy(src, out, w, sems).wait()

    return _Task(grads, [_sds((g.shape[0], g.shape[1] // 2, g.shape[2]), g.dtype) for g in grads], {},
                 [pltpu.SemaphoreType.DMA((n,)), pltpu.SemaphoreType.DMA((n,))], start, finish)


def _exchange_task(sends, accs):
    n = len(accs)
    given = [s for s in sends if s is not None]

    def copies(ins, outs, sems):
        send_refs = iter(ins[:len(given)])
        srcs = [next(send_refs) if s is not None else None for s in sends]
        x, y, c, me, others = _place()
        for w, out in enumerate(outs):
            for j, (ox, oy) in enumerate(others):
                src = out.at[me] if srcs[w] is None else srcs[w].at[2 * ox + oy]
                yield _remote(src, out.at[me], sems[0].at[w, j], sems[1].at[w, j], (ox, oy, c))

    def start(ins, outs, sems):
        for cp in copies(ins, outs, sems):
            cp.start()

    def finish(ins, outs, sems):
        x, y, c, _, others = _place()
        for w, out in enumerate(outs):
            for j, (ox, oy) in enumerate(others):
                slot = out.at[2 * ox + oy]
                _remote(slot, slot, sems[0].at[w, j], sems[1].at[w, j], (ox, oy, c)).wait_recv()
        for cp in copies(ins, outs, sems):
            cp.wait_send()

    return _Task(given + list(accs), [_sds(a.shape, a.dtype) for a in accs], {len(given) + i: i for i in range(n)},
                 [pltpu.SemaphoreType.DMA((n, 3)), pltpu.SemaphoreType.DMA((n, 3))], start, finish)


def _swap_task(arrays):
    n = len(arrays)

    def copy(src, out, w, sems):
        x, y, c, _, _ = _place()
        return _remote(src, out, sems[0].at[w], sems[1].at[w], (x, y, 1 - c))

    def start(ins, outs, sems):
        for w, (src, out) in enumerate(zip(ins, outs)):
            copy(src, out, w, sems).start()

    def finish(ins, outs, sems):
        for w, (src, out) in enumerate(zip(ins, outs)):
            copy(src, out, w, sems).wait()

    return _Task(arrays, [_sds(a.shape, a.dtype) for a in arrays], {},
                 [pltpu.SemaphoreType.DMA((n,)), pltpu.SemaphoreType.DMA((n,))], start, finish)


def _all_devices_task(arrays):
    n = len(arrays)
    flips = [(dx, dy, dc) for dx in (0, 1) for dy in (0, 1) for dc in (0, 1)][1:]

    def peers():
        x, y, c, _, _ = _place()
        flip = lambda v, d: 1 - v if d else v
        return 4 * x + 2 * y + c, [(flip(x, dx), flip(y, dy), flip(c, dc)) for dx, dy, dc in flips]

    def start(ins, outs, sems):
        me, others = peers()
        for w, (src, out) in enumerate(zip(ins, outs)):
            pltpu.make_async_copy(src, out.at[me], sems[2].at[w]).start()
            for k, peer in enumerate(others):
                _remote(src, out.at[me], sems[0].at[w, k], sems[1].at[w, k], peer).start()

    def finish(ins, outs, sems):
        me, others = peers()
        for w, (src, out) in enumerate(zip(ins, outs)):
            for k, (px, py, pc) in enumerate(others):
                slot = out.at[4 * px + 2 * py + pc]
                _remote(slot, slot, sems[0].at[w, k], sems[1].at[w, k], (px, py, pc)).wait_recv()
            for k, peer in enumerate(others):
                _remote(src, out.at[me], sems[0].at[w, k], sems[1].at[w, k], peer).wait_send()
            pltpu.make_async_copy(src, out.at[me], sems[2].at[w]).wait()

    return _Task(arrays, [_sds((8,) + a.shape, a.dtype) for a in arrays], {},
                 [pltpu.SemaphoreType.DMA((n, 7)), pltpu.SemaphoreType.DMA((n, 7)), pltpu.SemaphoreType.DMA((n,))],
                 start, finish)


def _share_task(shares):
    n = len(shares)

    def copy(out, w, sems, slot):
        x, y, c, _, _ = _place()
        return _remote(out.at[slot], out.at[slot], sems[0].at[w], sems[1].at[w], (x, y, 1 - c))

    def start(ins, outs, sems):
        c = _place()[2]
        for w, out in enumerate(outs):
            copy(out, w, sems, c).start()

    def finish(ins, outs, sems):
        c = _place()[2]
        for w, out in enumerate(outs):
            copy(out, w, sems, 1 - c).wait_recv()
        for w, out in enumerate(outs):
            copy(out, w, sems, c).wait_send()

    return _Task(shares, [_sds(s.shape, s.dtype) for s in shares], {i: i for i in range(n)},
                 [pltpu.SemaphoreType.DMA((n,)), pltpu.SemaphoreType.DMA((n,))], start, finish)


TILE_BYTES = 2 * 1024 * 1024
PARTIAL_TILE_BYTES = 1024 * 1024


def _in_hbm(t):
    return pltpu.with_memory_space_constraint(t, pltpu.HBM)


def _row_tile(rows, cols, limit=TILE_BYTES):
    best = 8
    for tr in range(8, rows + 1, 8):
        if rows % tr == 0 and tr * cols * 4 <= limit:
            best = tr
    assert rows % best == 0, (rows, cols)
    return best


def _chip_partial(g, got, place, wire_dtype):
    ns, ah, b = got.shape
    sharded = ns == N_CHIPS
    tr = _row_tile(ah, b, PARTIAL_TILE_BYTES)
    nb = ah // tr
    first = 0 if g.shape[1] == ah else nb

    def body(place_ref, *refs):
        g_refs, got_refs, outs = refs[:ns], refs[ns:2 * ns], refs[2 * ns:]
        parts = [g_refs[k][0] + got_refs[k][0] for k in range(ns)]
        own = parts[0]
        if sharded:
            for k in range(ns):
                outs[0][k] = parts[k].astype(wire_dtype)
                if k:
                    own = jnp.where(place_ref[0] == k, parts[k], own)
        outs[-1][0] = own.astype(wire_dtype)

    blk = (1, tr, b)
    in_specs = ([pl.BlockSpec(blk, lambda i, s, k=k: (k, s[1] * first + i, 0)) for k in range(ns)]
                + [pl.BlockSpec(blk, lambda i, s, k=k: (k, i, 0)) for k in range(ns)])
    acc_spec = pl.BlockSpec(blk, lambda i, s: (s[0], i, 0))
    acc_shape = _sds((N_CHIPS, ah, b), wire_dtype)
    out = pl.pallas_call(
        body, name="grad_chip_partial",
        grid_spec=pltpu.PrefetchScalarGridSpec(
            num_scalar_prefetch=1, grid=(nb,), in_specs=in_specs,
            out_specs=[pl.BlockSpec((ns, tr, b), lambda i, s: (0, i, 0)), acc_spec] if sharded else [acc_spec]),
        out_shape=[acc_shape, acc_shape] if sharded else [acc_shape],
        compiler_params=pltpu.CompilerParams(dimension_semantics=("arbitrary",), vmem_limit_bytes=VMEM_LIMIT),
    )(place, *([g] * ns), *([got] * ns))
    return (out[0], out[1]) if sharded else (None, out[0])


def _chip_sum(acc, place):
    _, ah, b = acc.shape
    tr = _row_tile(ah, b)

    def body(place_ref, p_ref, out_ref):
        total = p_ref[0].astype(F32) + p_ref[1].astype(F32)
        total = total + p_ref[2].astype(F32)
        out_ref[0] = total + p_ref[3].astype(F32)

    return pl.pallas_call(
        body, name="grad_chip_sum",
        grid_spec=pltpu.PrefetchScalarGridSpec(
            num_scalar_prefetch=1, grid=(ah // tr,),
            in_specs=[pl.BlockSpec((N_CHIPS, tr, b), lambda i, s: (0, i, 0))],
            out_specs=pl.BlockSpec((1, tr, b), lambda i, s: (s[1], i, 0))),
        out_shape=_sds((2, ah, b)),
        compiler_params=pltpu.CompilerParams(dimension_semantics=("arbitrary",)),
    )(place, _in_hbm(acc))


def _adam_math(w, g, m, v):
    nm = ADAM_B1 * m + (1.0 - ADAM_B1) * g
    nv = ADAM_B2 * v + (1.0 - ADAM_B2) * (g * g)
    m_hat = nm / (1.0 - ADAM_B1 ** ADAM_STEP)
    v_hat = nv / (1.0 - ADAM_B2 ** ADAM_STEP)
    return -ADAM_LR * (m_hat / (jnp.sqrt(v_hat) + ADAM_EPS) + ADAM_WD * w), nm, nv


def _adamw(w, g, m, v):
    a, b = w.shape
    tr = _row_tile(a, b)

    def body(w_ref, g_ref, m_ref, v_ref, g_out, d_ref, nm_ref, nv_ref):
        g_out[...] = g_ref[...]
        d_ref[...], nm_ref[...], nv_ref[...] = _adam_math(w_ref[...], g_ref[...], m_ref[...], v_ref[...])

    blk = pl.BlockSpec((tr, b), lambda i: (i, 0))
    return pl.pallas_call(
        body, name="adamw", grid=(a // tr,),
        in_specs=[blk] * 4, out_specs=[blk] * 4, out_shape=[_sds((a, b))] * 4,
        compiler_params=pltpu.CompilerParams(dimension_semantics=("arbitrary",)),
    )(w, g, m, v)


def _adamw_sum(w, m, v, acc, got, place):
    a, b = w.shape
    ah = a // 2
    tr = _row_tile(ah, b)
    nb = ah // tr

    def body(place_ref, w_ref, m_ref, v_ref, acc_ref, got_ref, g_out, d_ref, nm_ref, nv_ref):
        mine = (pl.program_id(0) // nb) == place_ref[1]
        part = lambda k: jnp.where(mine, acc_ref[k], got_ref[k]).astype(F32)
        g = part(0) + part(1)
        g = g + part(2)
        g = g + part(3)
        g_out[...] = g
        d_ref[...], nm_ref[...], nv_ref[...] = _adam_math(w_ref[...], g, m_ref[...], v_ref[...])

    blk = pl.BlockSpec((tr, b), lambda i, s: (i, 0))
    mine_spec = pl.BlockSpec((N_CHIPS, tr, b), lambda i, s: (0, jnp.where(i // nb == s[1], i % nb, 0), 0))
    got_spec = pl.BlockSpec((N_CHIPS, tr, b), lambda i, s: (0, jnp.where(i // nb == s[1], 0, i % nb), 0))
    return pl.pallas_call(
        body, name="adamw_sum",
        grid_spec=pltpu.PrefetchScalarGridSpec(
            num_scalar_prefetch=1, grid=(a // tr,), in_specs=[blk] * 3 + [mine_spec, got_spec], out_specs=[blk] * 4),
        out_shape=[_sds((a, b))] * 4,
        compiler_params=pltpu.CompilerParams(dimension_semantics=("arbitrary",), vmem_limit_bytes=VMEM_LIMIT),
    )(place, w, m, v, _in_hbm(acc), _in_hbm(got))


def _adamw_pieces(g, pieces, name):
    n = len(pieces)

    def body(g_ref, *refs):
        def grad(rows, cols):
            if len(g_ref.shape) == 2:
                return g_ref[rows, cols]
            total = g_ref[0, rows, cols]
            for k in range(1, g_ref.shape[0]):
                total = total + g_ref[k, rows, cols]
            return total

        ins, outs = refs[:3 * n], refs[3 * n:]
        for i, piece in enumerate(pieces):
            w_ref, m_ref, v_ref = ins[3 * i:3 * i + 3]
            o_g, o_d, o_m, o_v = outs[4 * i:4 * i + 4]
            if len(piece) == 5:
                g_v = grad(piece[3], piece[4])
                o_g[...] = g_v
                o_d[...], o_m[...], o_v[...] = _adam_math(w_ref[...], g_v, m_ref[...], v_ref[...])
            else:
                for r in range(w_ref.shape[1] // SMALL_COLS):
                    lanes = slice(r * SMALL_COLS, (r + 1) * SMALL_COLS)
                    g_v = grad(slice(piece[3] + r, piece[3] + r + 1), slice(None))
                    o_g[:, lanes] = g_v
                    o_d[:, lanes], o_m[:, lanes], o_v[:, lanes] = _adam_math(w_ref[:, lanes], g_v, m_ref[:, lanes],
                                                                            v_ref[:, lanes])

    operands = [t for piece in pieces for t in piece[:3]]
    out = pl.pallas_call(
        body, name=name,
        out_shape=[_sds(piece[0].shape) for piece in pieces for _ in range(4)],
    )(g, *operands)
    return [tuple(out[4 * i:4 * i + 4]) for i in range(n)]


TINY_ROWS, TINY_COLS = 16, 768
SMALL_COLS = 128
SMALL_ROWS = 624


def _pack_tiny(conv_w, b_gates, fcw):
    ns = conv_w.shape[0]
    pad = lambda t: jnp.pad(t, ((0, 0), (0, 0), (0, TINY_COLS - t.shape[2])))
    z = lambda rows: jnp.zeros((ns, rows, TINY_COLS), F32)
    return jnp.concatenate([pad(conv_w), pad(b_gates), z(2), fcw, z(TINY_ROWS - 11)], axis=1)


def _unpack_tiny(t):
    return t[:, 0:4, 0:256], t[:, 4:6, 0:256], t[:, 8:11, :]


def _cols_to_shards(t, n):
    return t.reshape(t.shape[0], N_CHIPS, n).transpose(1, 0, 2)


def _shards_to_cols(t):
    return t.transpose(1, 0, 2).reshape(t.shape[1], -1)


_VECTORS = ("g_mix_post", "conv_b", "lru_lambda", "g_ffn_pre", "g_ffn_post", "g_ple_gate", "g_ple_post", "pool_scale",
            "ffn_conv_b")
_VECTOR_LEN = {"pool_scale": POOL_WIDTH, "ffn_conv_b": D_FF}
POOL_W_ROWS = POOL_GROUPS * POOL_GROUP_DIM


def _vector_rows():
    rows, row = {}, POOL_W_ROWS
    for k in _VECTORS:
        rows[k] = row
        row += max(8, _VECTOR_LEN.get(k, D_MODEL) // SMALL_COLS)
    return rows, row


def _pack_small(grads, loss):
    tiles = lambda t: jnp.pad(t, ((0, -t.shape[0] % 8), (0, 0)))
    parts = [grads["pool_w"].reshape(POOL_W_ROWS, SMALL_COLS)] + [tiles(grads[k].reshape(-1, SMALL_COLS)) for k in _VECTORS]
    parts.append(tiles(loss))
    used = sum(t.shape[0] for t in parts)
    return jnp.concatenate(parts + [jnp.zeros((SMALL_ROWS - used, SMALL_COLS), F32)], axis=0)


def _gates_block_diag(w):
    w4 = w.reshape(2, GATE_BLOCKS, 4, RNN_HEAD_DIM, RNN_HEAD_DIM)
    eye = jnp.eye(4, dtype=w.dtype)
    return jnp.einsum("gqhij,hk->gqhikj", w4, eye).reshape(2, GATE_BLOCKS, GATE_BLOCK, GATE_BLOCK)


def _gates_from_block_diag(dw):
    d6 = dw.reshape(2, GATE_BLOCKS, 4, RNN_HEAD_DIM, 4, RNN_HEAD_DIM)
    blocks = [d6[:, :, hh, :, hh, :] for hh in range(4)]
    return jnp.stack(blocks, axis=2).reshape(2, RNN_HEADS, RNN_HEAD_DIM, RNN_HEAD_DIM)


ROW_TILE = 256
DW_TILES = 8

_SHARDED = ("w_in", "w_pool_out", "w_rg_out", "w_o", "w_up", "w_down", "w_ple_gate", "w_ple_proj")
_WEIGHTS = ("g_mix_pre", "g_mix_post", "w_in", "pool_w", "pool_scale", "w_pool_out", "conv_w", "conv_b", "w_rg_gates",
            "b_rg_gates", "lru_lambda", "w_rg_out", "w_o", "g_ffn_pre", "g_ffn_post", "w_up", "ffn_conv_w", "ffn_conv_b",
            "w_down", "g_ple_gate", "w_ple_gate", "w_ple_proj", "g_ple_post")


def _wire_dtype(g):
    return BF16 if g.shape[1] >= 64 and g.shape[2] > SMALL_COLS else F32


def _partials(grads, got, place):
    parts = [_chip_partial(g, r, place, _wire_dtype(g)) for g, r in zip(grads, got)]
    return [send for send, _ in parts], [acc for _, acc in parts]


def _whole(both):
    return [b.reshape(2 * b.shape[1], b.shape[2]) for b in both]


def _step(x, p, tgt, rep, place, ts):
    vec = lambda k: rep[k].reshape(1, -1)
    tall = min(2 * ts, x.shape[0])
    pool_w = rep["pool_w"].astype(BF16)
    wg = _gates_block_diag(rep["w_rg_gates"]).astype(BF16)
    sq = lambda t: t.reshape(D_MODEL, D_MODEL)
    by4 = lambda t: t.reshape(N_CHIPS, -1, D_MODEL)

    first, ride1, ride2 = (("w_in", "w_pool_out", "tiny"), ("w_rg_out", "w_o", "w_down"),
                           ("w_up", "w_ple_gate", "w_ple_proj"))
    later = ride1 + ride2
    tiny = _pack_tiny(rep["conv_w"][None], rep["b_rg_gates"][None], rep["ffn_conv_w"][None])[0]
    own_first, _ = _own_slots([rep["w_in"], rep["w_pool_out"], tiny], [BF16, BF16, F32], "own_slots_first")
    own_later, (got,) = _own_slots([rep[k] for k in later], [BF16] * len(later), "own_slots_gather_first",
                                   [_gather_task(own_first)])
    own = dict(zip(later, own_later))
    full = dict(zip(first, got))
    conv_w, b_gates, fcw = [_shards_to_cols(t) for t in _unpack_tiny(full["tiny"])]

    (urx, urg, gp, gr, d, ypool, h1), (got,) = _fwd_in_pool(
        x, vec("g_mix_pre"), full["w_in"], pool_w, vec("pool_scale"), full["w_pool_out"], ts,
        [_gather_task([own[k] for k in ride1], relay_steps=(6, 2))])
    full.update(zip(ride1, got))
    w_rg_out, w_o, w_down = sq(full["w_rg_out"]), sq(full["w_o"]), full["w_down"].reshape(D_FF, D_MODEL)
    (xc, r, ig, h, yrnn, mo, x1, glr, ggr, sp, sr), (got,) = _fwd_rnn_merge(
        urx, urg, gp, gr, ypool, x, conv_w, vec("conv_b"), wg, b_gates, vec("lru_lambda"), w_rg_out, w_o,
        vec("g_mix_post"), ts, [_gather_task([own[k] for k in ride2], relay_steps=(7, 3))])
    full.update(zip(ride2, got))
    (up, gl, gg, h2, dn, x2), _ = _fwd_ffn(x1, vec("g_ffn_pre"), full["w_up"], fcw, vec("ffn_conv_b"), w_down,
                                           vec("g_ffn_post"), ts)
    dx2, loss, d_w_gate, d_w_proj, d_g_ple_gate, d_g_ple_post = _ple_loss(
        x2, p, tgt, vec("g_ple_gate"), sq(full["w_ple_gate"]), full["w_ple_proj"], vec("g_ple_post"), tall)
    dup, d_w_down, d_fcw, d_fcb, d_g_ffn_post = _bwd_ffn_down(dx2, dn, up, gl, gg, fcw, w_down, vec("g_ffn_post"), ts)

    names1, grads1 = ("w_ple_gate", "w_ple_proj", "w_down"), [by4(d_w_gate), d_w_proj, by4(d_w_down)]
    (dx1, d_g_ffn_pre), (got1,) = _bwd_ffn_up(dup, x1, dx2, vec("g_ffn_pre"), full["w_up"], tall, [_halves_task(grads1)])
    (send_w_up, acc_w_up), (accs1,) = _dw_up(h2, dup, ts, [_exchange_task(*_partials(grads1, got1, place))])
    (dgp, dgr, dyp, dyr, d_w_o, d_g_mix_post), (theirs1,) = _bwd_merge(
        dx1, mo, sp, sr, ypool, yrnn, vec("g_mix_post"), w_o, tall, [_swap_task(accs1)])
    (durx, durg, d_w_rg_out, d_wg, d_conv_w, d_conv_b, d_b_gates, d_lam), (accs2,) = _bwd_rnn(
        dyr, urx, glr, ggr, xc, r, ig, h, conv_w, wg, vec("lru_lambda"), w_rg_out, ts,
        [_exchange_task([send_w_up], [acc_w_up])])
    names3 = ("w_o", "w_rg_out", "tiny", "w_rg_gates")
    grads3 = [by4(d_w_o), by4(d_w_rg_out),
              _pack_tiny(_cols_to_shards(d_conv_w, 256), _cols_to_shards(d_b_gates, 256), _cols_to_shards(d_fcw, 768)),
              _gates_from_block_diag(d_wg).reshape(1, 2 * RNN_HEADS * RNN_HEAD_DIM, RNN_HEAD_DIM)]
    (dzp, d_w_pool_out, d_pool_w, d_pool_scale), (got3, theirs2) = _bwd_pool(
        dyp, d, pool_w, vec("pool_scale"), full["w_pool_out"], tall, [_halves_task(grads3), _swap_task(accs2)])
    replicated = {"g_mix_post": d_g_mix_post, "conv_b": d_conv_b, "lru_lambda": d_lam, "g_ffn_pre": d_g_ffn_pre,
                  "g_ffn_post": d_g_ffn_post, "g_ple_gate": d_g_ple_gate, "g_ple_post": d_g_ple_post,
                  "pool_scale": d_pool_scale, "ffn_conv_b": d_fcb, "pool_w": d_pool_w}
    names4 = ("w_in", "w_pool_out", "small")
    small4 = [d_w_pool_out, _pack_small(replicated, loss)[None]]
    (send_w_in, acc_w_in), (accs3, got_small4) = _dw_in(
        h1, dzp, durx, durg, dgp, dgr, ts, [_exchange_task(*_partials(grads3, got3, place)), _halves_task(small4)])
    sends4, accs4 = _partials(small4, got_small4, place)
    (grad_x, d_g_mix_pre), (accs4, theirs3, both3) = _bwd_in(
        dzp, durx, durg, dgp, dgr, x, dx1, vec("g_mix_pre"), full["w_in"], tall,
        [_exchange_task([send_w_in] + sends4, [acc_w_in] + accs4), _swap_task(accs3[:2]),
         _share_task([_chip_sum(acc, place) for acc in accs3[2:]])])
    theirs4, both4, (g_mix_pre_parts,) = _run(
        [_swap_task(accs4[:2]), _share_task([_chip_sum(acc, place) for acc in accs4[2:]]),
         _all_devices_task([d_g_mix_pre.reshape(SUBLANES, SMALL_COLS)])], "grad_sibling_share")
    mine = accs1 + accs2 + accs3[:2] + accs4[:2]
    partials = dict(zip(names1 + ("w_up",) + names3[:2] + names4[:2], zip(mine, theirs1 + theirs2 + theirs3 + theirs4)))
    return grad_x, partials, dict(zip(names3[2:] + names4[2:], _whole(both3) + _whole(both4))), g_mix_pre_parts


def kernel(x, p, g_mix_pre, g_mix_post, w_in, pool_w, pool_scale, w_pool_out, conv_w, conv_b, w_rg_gates, b_rg_gates, lru_lambda, w_rg_out, w_o, g_ffn_pre, g_ffn_post, w_up, ffn_conv_w, ffn_conv_b, w_down, g_ple_gate, w_ple_gate, w_ple_proj, g_ple_post, loss_target, m_g_mix_pre, m_g_mix_post, m_w_in, m_pool_w, m_pool_scale, m_w_pool_out, m_conv_w, m_conv_b, m_w_rg_gates, m_b_rg_gates, m_lru_lambda, m_w_rg_out, m_w_o, m_g_ffn_pre, m_g_ffn_post, m_w_up, m_ffn_conv_w, m_ffn_conv_b, m_w_down, m_g_ple_gate, m_w_ple_gate, m_w_ple_proj, m_g_ple_post, v_g_mix_pre, v_g_mix_post, v_w_in, v_pool_w, v_pool_scale, v_w_pool_out, v_conv_w, v_conv_b, v_w_rg_gates, v_b_rg_gates, v_lru_lambda, v_w_rg_out, v_w_o, v_g_ffn_pre, v_g_ffn_post, v_w_up, v_ffn_conv_w, v_ffn_conv_b, v_w_down, v_g_ple_gate, v_w_ple_gate, v_w_ple_proj, v_g_ple_post):
    args = dict(locals())
    w = {k: args[k][0] for k in _WEIGHTS}
    m = {k: args["m_" + k][0] for k in _WEIGHTS}
    v = {k: args["v_" + k][0] for k in _WEIGHTS}
    place = jnp.stack([2 * lax.axis_index("x") + lax.axis_index("y"), lax.axis_index("c")]).astype(jnp.int32)
    grad_x, partials, reduced, g_mix_pre_parts = _step(x[0], p[0, 0], loss_target[0], w, place, ROW_TILE)

    gates_2d = (2 * RNN_HEADS * RNN_HEAD_DIM, RNN_HEAD_DIM)
    as2d = lambda k, shape: tuple(t[k].reshape(shape) for t in (w, m, v))
    done = {k: tuple(_adamw_sum(w[k], m[k], v[k], *partials[k], place)) for k in _SHARDED}
    gates_w, gates_m, gates_v = as2d("w_rg_gates", gates_2d)
    done["w_rg_gates"] = tuple(_adamw(gates_w, reduced["w_rg_gates"], gates_m, gates_v))
    tiny_names = ("conv_w", "b_rg_gates", "ffn_conv_w")
    tiny_at = ((slice(0, 4), slice(0, 256)), (slice(4, 6), slice(0, 256)), (slice(8, 11), slice(None)))
    done.update(zip(tiny_names, _adamw_pieces(
        reduced["tiny"], [(w[k], m[k], v[k]) + at for k, at in zip(tiny_names, tiny_at)], "adamw_tiny")))
    vector_rows, loss_row = _vector_rows()
    pieces = [as2d("pool_w", (POOL_W_ROWS, SMALL_COLS)) + (slice(0, POOL_W_ROWS), slice(None))]
    pieces += [as2d(k, (1, -1)) + (vector_rows[k],) for k in _VECTORS]
    done.update(zip(("pool_w",) + _VECTORS, _adamw_pieces(reduced["small"], pieces, "adamw_small")))
    done["g_mix_pre"] = _adamw_pieces(g_mix_pre_parts, [as2d("g_mix_pre", (1, -1)) + (0,)], "adamw_g_mix_pre")[0]

    result = [reduced["small"][loss_row, 0], grad_x[None]]
    for kind in range(4):
        result += [done[k][kind].reshape(args[k].shape) for k in _WEIGHTS]
    return tuple(result)
```

```python
import functools

import jax
import jax.numpy as jnp
from jax import lax
from jax.experimental import pallas as pl
from jax.experimental.pallas import tpu as pltpu

F32 = jnp.float32
BF16 = jnp.bfloat16

D_MODEL = 1024
POOL_WINDOWS = (2, 4, 8, 16)
POOL_GROUPS = 4
POOL_WIDTH = 512
POOL_GROUP_DIM = 128
RNN_HEADS = 16
RNN_HEAD_DIM = 64
GATE_BLOCK = 256
GATE_BLOCKS = D_MODEL // GATE_BLOCK
LRU_C = 8.0
D_FF = 3072
PLE_DIM = 256
RMS_EPS = 1e-6
IN_TOTAL = 4608
N_CHIPS = 4
IN_SHARD = IN_TOTAL // N_CHIPS
UP_SHARD = 2 * D_FF // N_CHIPS
Z_GROUPS = ((0, 512), (512, 1536), (1536, 2560), (2560, 3584), (3584, 4608))
POOL_HALO = 16
CONV_HALO = 8

ADAM_LR = 0.001
ADAM_B1 = 0.9
ADAM_B2 = 0.999
ADAM_EPS = 1e-08
ADAM_WD = 0.01
ADAM_STEP = 10

VMEM_LIMIT = 56 * 1024 * 1024
MESH = pl.DeviceIdType.MESH

_GELU_C = 0.7978845608028654
_GELU_A = 0.044715


def _dot(a, b):
    return jnp.dot(a.astype(BF16), b.astype(BF16), preferred_element_type=F32)


def _dot_nt(a, b):
    return lax.dot_general(a.astype(BF16), b.astype(BF16), (((1,), (1,)), ((), ())), preferred_element_type=F32)


def _dot_tn(a, b):
    return lax.dot_general(a.astype(BF16), b.astype(BF16), (((0,), (0,)), ((), ())), preferred_element_type=F32)


def _overlaps(group):
    a, b = Z_GROUPS[group]
    found = []
    for j in range(N_CHIPS):
        lo, hi = max(a, j * IN_SHARD), min(b, (j + 1) * IN_SHARD)
        if lo < hi:
            found.append((j, slice(lo - j * IN_SHARD, hi - j * IN_SHARD), slice(lo - a, hi - a)))
    return found


def _rms_fwd(x, g):
    r = lax.rsqrt(jnp.mean(x * x, axis=-1, keepdims=True) + RMS_EPS)
    xh = x * r
    return xh * g, xh, r


def _rms_bwd(xh, r, g, dy):
    dxh = dy * g
    dg = jnp.sum(dy * xh, axis=0, keepdims=True)
    dx = r * (dxh - xh * jnp.mean(dxh * xh, axis=-1, keepdims=True))
    return dx, dg


def _sigmoid(x):
    return 0.5 * jnp.tanh(0.5 * x) + 0.5


def _gelu(x):
    t = jnp.tanh(_GELU_C * (x + _GELU_A * x * x * x))
    return 0.5 * x * (1.0 + t), t


def _gelu_grad(x, t):
    return 0.5 * (1.0 + t) + 0.5 * x * (1.0 - t * t) * _GELU_C * (1.0 + 3.0 * _GELU_A * x * x)


def _softplus_neg(lam):
    nl = -lam
    return jnp.maximum(nl, 0.0) + jnp.log(1.0 + jnp.exp(-jnp.abs(nl)))


def _lru_coeffs(r, lam, first_row):
    c8 = LRU_C * _softplus_neg(lam)
    la = -(c8 * r)
    a = jnp.exp(la)
    m2 = jnp.tanh(-la) * (1.0 + a * a)
    mult = jnp.where(first_row, 1.0, jnp.sqrt(m2))
    return c8, a, m2, mult


SUBLANES = 8


def _scan_fwd(a, u, carry):
    n = a.shape[0]
    sub = lax.broadcasted_iota(jnp.int32, (n, 1), 0) % SUBLANES
    acc_a, acc_h = a, u
    for s in (1, 2, 4):
        m = sub >= s
        h_s = jnp.where(m, pltpu.roll(acc_h, s, 0), 0.0)
        a_s = jnp.where(m, pltpu.roll(acc_a, s, 0), 1.0)
        acc_h = acc_a * h_s + acc_h
        acc_a = acc_a * a_s
    out = []
    for g in range(n // SUBLANES):
        rows = slice(g * SUBLANES, (g + 1) * SUBLANES)
        out.append(acc_h[rows] + acc_a[rows] * carry)
        carry = out[-1][SUBLANES - 1:SUBLANES]
    return jnp.concatenate(out, axis=0)


def _scan_bwd(b, g, carry):
    n = b.shape[0]
    sub = lax.broadcasted_iota(jnp.int32, (n, 1), 0) % SUBLANES
    acc_b, acc_l = b, g
    for s in (1, 2, 4):
        m = sub < SUBLANES - s
        l_s = jnp.where(m, pltpu.roll(acc_l, n - s, 0), 0.0)
        b_s = jnp.where(m, pltpu.roll(acc_b, n - s, 0), 1.0)
        acc_l = acc_b * l_s + acc_l
        acc_b = acc_b * b_s
    out = [None] * (n // SUBLANES)
    for g in reversed(range(n // SUBLANES)):
        rows = slice(g * SUBLANES, (g + 1) * SUBLANES)
        out[g] = acc_l[rows] + acc_b[rows] * carry
        carry = out[g][0:1]
    return jnp.concatenate(out, axis=0)


def _shift_down(ext, k, halo):
    return pltpu.roll(ext, k, 0)[halo:] if k else ext[halo:]


def _shift_up(ext, k, ts):
    return pltpu.roll(ext, ext.shape[0] - k, 0)[:ts] if k else ext[:ts]


def _rows(ts, width, nt=None, col=0):
    if nt is None:
        return pl.BlockSpec((ts, width), lambda i: (i, col))
    return pl.BlockSpec((ts, width), lambda i: (nt - 1 - i, col))


def _resident(shape):
    zeros = (0,) * len(shape)
    return pl.BlockSpec(shape, lambda i: zeros, pipeline_mode=pl.Buffered(1))


def _acc(shape):
    zeros = (0,) * len(shape)
    return pl.BlockSpec(shape, lambda i: zeros)


def _params():
    return pltpu.CompilerParams(dimension_semantics=("arbitrary",), vmem_limit_bytes=VMEM_LIMIT)


def _sds(shape, dtype=F32):
    return jax.ShapeDtypeStruct(shape, dtype)


class _Task:
    def __init__(self, ins, out_shapes, aliases, sems, start, finish, relays=()):
        self.ins, self.out_shapes, self.aliases, self.sems = list(ins), list(out_shapes), dict(aliases), list(sems)
        self.start, self.relays, self.finish = start, list(relays), finish


def _call(body, name, grid, in_specs, out_specs, out_shape, scratch_shapes, args, tasks=()):
    n_in, n_out, n_scr = len(in_specs), len(out_specs), len(scratch_shapes)
    t_in = [len(t.ins) for t in tasks]
    t_out = [len(t.out_shapes) for t in tasks]
    t_sem = [len(t.sems) for t in tasks]
    steps = 1
    for g in grid:
        steps *= g

    def take(refs, pos, counts):
        groups = []
        for c in counts:
            groups.append(refs[pos:pos + c])
            pos += c
        return groups, pos

    def wrapped(*refs):
        (cin,), pos = take(refs, 0, [n_in])
        tin, pos = take(refs, pos, t_in)
        (cout,), pos = take(refs, pos, [n_out])
        tout, pos = take(refs, pos, t_out)
        (cscr,), pos = take(refs, pos, [n_scr])
        tsem, pos = take(refs, pos, t_sem)
        if not grid:
            for t, a, b, c in zip(tasks, tin, tout, tsem):
                t.start(a, b, c)
            if body is not None:
                body(*cin, *cout, *cscr)
            for t, a, b, c in zip(tasks, tin, tout, tsem):
                for relay, _ in t.relays:
                    relay(a, b, c)
            for t, a, b, c in zip(tasks, tin, tout, tsem):
                t.finish(a, b, c)
            return
        step = pl.program_id(0)
        for axis in range(1, len(grid)):
            step = step * grid[axis] + pl.program_id(axis)
        if tasks:
            @pl.when(step == 0)
            def _():
                for t, a, b, c in zip(tasks, tin, tout, tsem):
                    t.start(a, b, c)

        body(*cin, *cout, *cscr)
        for t, a, b, c in zip(tasks, tin, tout, tsem):
            for relay, before in t.relays:
                pl.when(step == max(steps - 1 - before, 0))(functools.partial(relay, a, b, c))

        if tasks:
            @pl.when(step == steps - 1)
            def _():
                for t, a, b, c in zip(tasks, tin, tout, tsem):
                    t.finish(a, b, c)

    aliases, in_pos, out_pos = {}, n_in, n_out
    for t, ni, no in zip(tasks, t_in, t_out):
        aliases.update({in_pos + a: out_pos + b for a, b in t.aliases.items()})
        in_pos, out_pos = in_pos + ni, out_pos + no
    any_spec = pl.BlockSpec(memory_space=pltpu.HBM)
    kwargs = dict(grid=grid, compiler_params=pltpu.CompilerParams(
        dimension_semantics=("arbitrary",) * len(grid), vmem_limit_bytes=VMEM_LIMIT)) if grid else dict(
        compiler_params=pltpu.CompilerParams(vmem_limit_bytes=VMEM_LIMIT))
    out = pl.pallas_call(
        wrapped, name=name,
        in_specs=list(in_specs) + [any_spec] * sum(t_in),
        out_specs=list(out_specs) + [any_spec] * sum(t_out),
        out_shape=list(out_shape) + [s for t in tasks for s in t.out_shapes],
        scratch_shapes=list(scratch_shapes) + [s for t in tasks for s in t.sems],
        input_output_aliases=aliases, **kwargs,
    )(*args, *[pltpu.with_memory_space_constraint(a, pltpu.HBM) for t in tasks for a in t.ins])
    task_outs, pos = take(list(out), n_out, t_out)
    return list(out[:n_out]), task_outs


def _fwd_in_pool(x, g_pre, w_in, pool_w, pool_scale, w_pool_out, ts, tasks=()):
    s = x.shape[0]

    def body(x_ref, g_ref, win_ref, pw_ref, ps_ref, wpo_ref,
             urx_ref, urg_ref, gp_ref, gr_ref, d_ref, yp_ref, h1_ref, halo_scr):
        i = pl.program_id(0)

        @pl.when(i == 0)
        def _():
            halo_scr[...] = jnp.zeros_like(halo_scr)

        h1, _, _ = _rms_fwd(x_ref[...], g_ref[...])
        h1 = h1.astype(BF16)
        h1_ref[...] = h1
        u = jnp.dot(h1, win_ref[0, :, 0:POOL_WIDTH], preferred_element_type=F32)
        for group, out_ref in zip(range(1, len(Z_GROUPS)), (urx_ref, urg_ref, gp_ref, gr_ref)):
            for j, shard_cols, group_cols in _overlaps(group):
                out_ref[:, group_cols] = jnp.dot(h1, win_ref[j, :, shard_cols], preferred_element_type=F32)
        ext = jnp.concatenate([halo_scr[...], u], axis=0)
        halo_scr[...] = u[ts - POOL_HALO:, :]
        t = i * ts + lax.broadcasted_iota(jnp.int32, (ts, 1), 0)
        y4 = []
        for g, w in enumerate(POOL_WINDOWS):
            lanes = slice(g * POOL_GROUP_DIM, (g + 1) * POOL_GROUP_DIM)
            acc = ext[:, lanes]
            sh = 1
            while sh < w:
                acc = acc + pltpu.roll(acc, sh, 0)
                sh *= 2
            inv = 1.0 / jnp.minimum(t + 1, w).astype(F32)
            dg = acc[POOL_HALO:, :] * inv - u[:, lanes]
            d_ref[:, lanes] = dg.astype(BF16)
            y4.append(_dot(dg, pw_ref[g]))
        ypre = jnp.concatenate(y4, axis=1) * ps_ref[...]
        ypre = ypre.astype(BF16)
        for j in range(N_CHIPS):
            yp_ref[:, j * 256:(j + 1) * 256] = jnp.dot(ypre, wpo_ref[j], preferred_element_type=F32)

    return _call(
        body, "fwd_in_pool", (s // ts,),
        [_rows(ts, D_MODEL), _resident((1, D_MODEL)), _resident(w_in.shape), _resident(pool_w.shape),
         _resident((1, POOL_WIDTH)), _resident(w_pool_out.shape)],
        [_rows(ts, D_MODEL)] * 4 + [_rows(ts, POOL_WIDTH), _rows(ts, D_MODEL), _rows(ts, D_MODEL)],
        [_sds((s, D_MODEL))] * 4 + [_sds((s, POOL_WIDTH), BF16), _sds((s, D_MODEL)), _sds((s, D_MODEL), BF16)],
        [pltpu.VMEM((POOL_HALO, POOL_WIDTH), F32)],
        (x, g_pre, w_in, pool_w, pool_scale, w_pool_out), tasks)


def _fwd_rnn_merge(urx, urg, gp, gr, ypool, x, conv_w, conv_b, wg, bg, lam, w_rg_out, w_o, g_post, ts, tasks=()):
    s = x.shape[0]

    def body(urx_ref, urg_ref, gp_ref, gr_ref, yp_ref, x_ref, cw_ref, cb_ref, wg_ref, bg_ref, lam_ref, wrg_ref, wo_ref,
             gpost_ref, xc_ref, r_ref, ig_ref, h_ref, yr_ref, mo_ref, x1_ref, gl_ref, gg_ref, sp_ref, sr_ref,
             halo_scr, carry_scr):
        i = pl.program_id(0)

        @pl.when(i == 0)
        def _():
            halo_scr[...] = jnp.zeros_like(halo_scr)
            carry_scr[...] = jnp.zeros_like(carry_scr)

        urx_v = urx_ref[...]
        ext = jnp.concatenate([halo_scr[...], urx_v], axis=0)
        halo_scr[...] = urx_v[ts - CONV_HALO:, :]
        cw = cw_ref[...]
        xc = (cb_ref[...] + cw[3:4] * urx_v + cw[2:3] * _shift_down(ext, 1, CONV_HALO)
              + cw[1:2] * _shift_down(ext, 2, CONV_HALO) + cw[0:1] * _shift_down(ext, 3, CONV_HALO))
        xc_ref[...] = xc.astype(BF16)
        xcb = xc.astype(BF16)
        lin = []
        for gate in range(2):
            parts = [jnp.dot(xcb[:, q * GATE_BLOCK:(q + 1) * GATE_BLOCK], wg_ref[gate, q], preferred_element_type=F32)
                     for q in range(GATE_BLOCKS)]
            lin.append(jnp.concatenate(parts, axis=1) + bg_ref[gate:gate + 1, :])
        r = _sigmoid(lin[0])
        ig = _sigmoid(lin[1])
        r_ref[...] = r.astype(BF16)
        ig_ref[...] = ig.astype(BF16)
        first_row = (i * ts + lax.broadcasted_iota(jnp.int32, (ts, 1), 0)) == 0
        _, a, _, mult = _lru_coeffs(r, lam_ref[...], first_row)
        h = _scan_fwd(a, mult * ig * xc, carry_scr[0:1, :])
        carry_scr[0:1, :] = h[ts - 1:ts, :]
        h_ref[...] = h
        urg_v = urg_ref[...]
        gl, t = _gelu(urg_v)
        gl_ref[...] = gl.astype(BF16)
        gg_ref[...] = _gelu_grad(urg_v, t).astype(BF16)
        yr = _dot(h * gl, wrg_ref[...])
        yr_ref[...] = yr.astype(BF16)
        sp = _sigmoid(gp_ref[...])
        sr = _sigmoid(gr_ref[...])
        sp_ref[...] = sp.astype(BF16)
        sr_ref[...] = sr.astype(BF16)
        merged = sp * yp_ref[...] + sr * yr
        mo = _dot(merged, wo_ref[...])
        mo_ref[...] = mo
        y, _, _ = _rms_fwd(mo, gpost_ref[...])
        x1_ref[...] = x_ref[...] + y

    row = _rows(ts, D_MODEL)
    return _call(
        body, "fwd_rnn_merge", (s // ts,),
        [row] * 6 + [_resident(conv_w.shape), _resident((1, D_MODEL)), _resident(wg.shape), _resident(bg.shape),
                     _resident((1, D_MODEL)), _resident(w_rg_out.shape), _resident(w_o.shape), _resident((1, D_MODEL))],
        [row] * 11, [_sds((s, D_MODEL), dt) for dt in (BF16, BF16, BF16, F32, BF16, F32, F32, BF16, BF16, BF16, BF16)],
        [pltpu.VMEM((CONV_HALO, D_MODEL), F32), pltpu.VMEM((8, D_MODEL), F32)],
        (urx, urg, gp, gr, ypool, x, conv_w, conv_b, wg, bg, lam, w_rg_out, w_o, g_post), tasks)


def _fwd_ffn(x1, g_pre, w_up, fcw, fcb, w_down, g_post, ts, tasks=()):
    s = x1.shape[0]

    def body(x1_ref, g_ref, wup_ref, fcw_ref, fcb_ref, wd_ref, gpost_ref,
             up_ref, gl_ref, gg_ref, h2_ref, dn_ref, x2_ref, up_scr, halo_scr):
        i = pl.program_id(0)

        @pl.when(i == 0)
        def _():
            halo_scr[...] = jnp.zeros_like(halo_scr)

        x1_v = x1_ref[...]
        h2, _, _ = _rms_fwd(x1_v, g_ref[...])
        h2 = h2.astype(BF16)
        h2_ref[...] = h2
        for j in range(N_CHIPS):
            up_scr[:, j * UP_SHARD:(j + 1) * UP_SHARD] = jnp.dot(h2, wup_ref[j], preferred_element_type=F32)
        up_ref[...] = up_scr[...].astype(BF16)
        ug = up_scr[:, 0:D_FF]
        ext = jnp.concatenate([halo_scr[...], ug], axis=0)
        halo_scr[...] = ug[ts - CONV_HALO:, :]
        w = fcw_ref[...]
        gh = (fcb_ref[...] + w[2:3] * ug + w[1:2] * _shift_down(ext, 1, CONV_HALO)
              + w[0:1] * _shift_down(ext, 2, CONV_HALO))
        gl, t = _gelu(gh)
        gl_ref[...] = gl.astype(BF16)
        gg_ref[...] = _gelu_grad(gh, t).astype(BF16)
        dn = _dot(gl * up_scr[:, D_FF:], wd_ref[...])
        dn_ref[...] = dn
        y, _, _ = _rms_fwd(dn, gpost_ref[...])
        x2_ref[...] = x1_v + y

    row = _rows(ts, D_MODEL)
    return _call(
        body, "fwd_ffn", (s // ts,),
        [row, _resident((1, D_MODEL)), _resident(w_up.shape), _resident(fcw.shape), _resident((1, D_FF)),
         _resident(w_down.shape), _resident((1, D_MODEL))],
        [_rows(ts, 2 * D_FF), _rows(ts, D_FF), _rows(ts, D_FF), row, row, row],
        [_sds((s, 2 * D_FF), BF16), _sds((s, D_FF), BF16), _sds((s, D_FF), BF16), _sds((s, D_MODEL), BF16),
         _sds((s, D_MODEL)), _sds((s, D_MODEL))],
        [pltpu.VMEM((ts, 2 * D_FF), F32), pltpu.VMEM((CONV_HALO, D_FF), F32)],
        (x1, g_pre, w_up, fcw, fcb, w_down, g_post), tasks)


def _ple_loss(x2, p, tgt, g_gate, w_gate, w_proj, g_post, ts):
    s = x2.shape[0]

    def body(x2_ref, p_ref, t_ref, gg_ref, wg_ref, wp_ref, gp_ref, dx2_ref, loss_ref, dwg_ref, dwp_ref, dgg_ref, dgp_ref):
        @pl.when(pl.program_id(0) == 0)
        def _():
            loss_ref[...] = jnp.zeros_like(loss_ref)
            dwg_ref[...] = jnp.zeros_like(dwg_ref)
            dwp_ref[...] = jnp.zeros_like(dwp_ref)
            dgg_ref[...] = jnp.zeros_like(dgg_ref)
            dgp_ref[...] = jnp.zeros_like(dgp_ref)

        x2_v = x2_ref[...]
        n3, xh3, r3 = _rms_fwd(x2_v, gg_ref[...])
        pg = _sigmoid(_dot(n3, wg_ref[...]))
        pb = p_ref[...].astype(BF16)
        q = jnp.concatenate([jnp.dot(pb, wp_ref[j], preferred_element_type=F32) for j in range(N_CHIPS)], axis=1)
        ple, qh, rq = _rms_fwd(q, gp_ref[...])
        e = x2_v + pg * ple - t_ref[...]
        loss_ref[...] += 0.5 * jnp.sum(jnp.mean(e * e, axis=-1, keepdims=True), axis=0, keepdims=True)
        dy = e * (1.0 / D_MODEL)
        dpgl = dy * ple * pg * (1.0 - pg)
        dwg_ref[...] += _dot_tn(n3, dpgl)
        dx3, dgg = _rms_bwd(xh3, r3, gg_ref[...], _dot_nt(dpgl, wg_ref[...]))
        dgg_ref[...] += dgg
        dq, dgp = _rms_bwd(qh, rq, gp_ref[...], dy * pg)
        dgp_ref[...] += dgp
        for j in range(N_CHIPS):
            dwp_ref[j] += _dot_tn(pb, dq[:, j * 256:(j + 1) * 256])
        dx2_ref[...] = dy + dx3

    row = _rows(ts, D_MODEL)
    vec = _acc((1, D_MODEL))
    return pl.pallas_call(
        body, name="ple_loss", grid=(s // ts,),
        in_specs=[row, _rows(ts, PLE_DIM), row, _resident((1, D_MODEL)), _resident(w_gate.shape), _resident(w_proj.shape),
                  _resident((1, D_MODEL))],
        out_specs=[row, _acc((1, 128)), _acc(w_gate.shape), _acc(w_proj.shape), vec, vec],
        out_shape=[_sds((s, D_MODEL)), _sds((1, 128)), _sds(w_gate.shape), _sds(w_proj.shape), _sds((1, D_MODEL)),
                   _sds((1, D_MODEL))],
        compiler_params=_params(),
    )(x2, p, tgt, g_gate, w_gate, w_proj, g_post)


def _bwd_ffn_down(dx2, dn, up, gl, gg, fcw, w_down, g_post, ts):
    s = dx2.shape[0]
    nt = s // ts

    def body(dx2_ref, dn_ref, up_ref, gl_ref, gg_ref, fcw_ref, wd_ref, gpost_ref,
             dup_ref, dwd_ref, dfcw_ref, dfcb_ref, dgp_ref, carry_scr):
        i = pl.program_id(0)

        @pl.when(i == 0)
        def _():
            carry_scr[...] = jnp.zeros_like(carry_scr)
            dwd_ref[...] = jnp.zeros_like(dwd_ref)
            dfcw_ref[...] = jnp.zeros_like(dfcw_ref)
            dfcb_ref[...] = jnp.zeros_like(dfcb_ref)
            dgp_ref[...] = jnp.zeros_like(dgp_ref)

        _, xh, r = _rms_fwd(dn_ref[...], gpost_ref[...])
        ddn, dgp = _rms_bwd(xh, r, gpost_ref[...], dx2_ref[...])
        dgp_ref[...] += dgp
        dhid = _dot_nt(ddn, wd_ref[...])
        ug = up_ref[:, 0:D_FF].astype(F32)
        uv = up_ref[:, D_FF:].astype(F32)
        gl = gl_ref[...].astype(F32)
        w = fcw_ref[...]
        dwd_ref[...] += _dot_tn(gl * uv, ddn)
        dgh = dhid * uv * gg_ref[...].astype(F32)
        dup_ref[:, D_FF:] = (dhid * gl).astype(BF16)
        extd = jnp.concatenate([dgh, carry_scr[...]], axis=0)
        carry_scr[...] = dgh[0:CONV_HALO, :]
        d1 = _shift_up(extd, 1, ts)
        d2 = _shift_up(extd, 2, ts)
        dup_ref[:, 0:D_FF] = (w[2:3] * dgh + w[1:2] * d1 + w[0:1] * d2).astype(BF16)
        dfcw_ref[2:3, :] += jnp.sum(ug * dgh, axis=0, keepdims=True)
        dfcw_ref[1:2, :] += jnp.sum(ug * d1, axis=0, keepdims=True)
        dfcw_ref[0:1, :] += jnp.sum(ug * d2, axis=0, keepdims=True)
        dfcb_ref[...] += jnp.sum(dgh, axis=0, keepdims=True)

    row = _rows(ts, D_MODEL, nt)
    wide = _rows(ts, D_FF, nt)
    return pl.pallas_call(
        body, name="bwd_ffn_down", grid=(nt,),
        in_specs=[row, row, _rows(ts, 2 * D_FF, nt), wide, wide, _resident(fcw.shape), _resident(w_down.shape),
                  _resident((1, D_MODEL))],
        out_specs=[_rows(ts, 2 * D_FF, nt), _acc(w_down.shape), _acc(fcw.shape), _acc((1, D_FF)), _acc((1, D_MODEL))],
        out_shape=[_sds((s, 2 * D_FF), BF16), _sds(w_down.shape), _sds(fcw.shape), _sds((1, D_FF)), _sds((1, D_MODEL))],
        scratch_shapes=[pltpu.VMEM((CONV_HALO, D_FF), F32)],
        compiler_params=_params(),
    )(dx2, dn, up, gl, gg, fcw, w_down, g_post)


def _bwd_ffn_up(dup, x1, dx2, g_pre, w_up, ts, tasks=()):
    s = x1.shape[0]

    def body(dup_ref, x1_ref, dx2_ref, g_ref, wup_ref, dx1_ref, dg_ref):
        @pl.when(pl.program_id(0) == 0)
        def _():
            dg_ref[...] = jnp.zeros_like(dg_ref)

        _, xh, r = _rms_fwd(x1_ref[...], g_ref[...])
        dh2 = _dot_nt(dup_ref[:, 0:UP_SHARD], wup_ref[0])
        for j in range(1, N_CHIPS):
            dh2 = dh2 + _dot_nt(dup_ref[:, j * UP_SHARD:(j + 1) * UP_SHARD], wup_ref[j])
        dx, dg = _rms_bwd(xh, r, g_ref[...], dh2)
        dg_ref[...] += dg
        dx1_ref[...] = dx2_ref[...] + dx

    row = _rows(ts, D_MODEL)
    return _call(
        body, "bwd_ffn_up", (s // ts,),
        [_rows(ts, 2 * D_FF), row, row, _resident((1, D_MODEL)), _resident(w_up.shape)],
        [row, _acc((1, D_MODEL))], [_sds((s, D_MODEL)), _sds((1, D_MODEL))], [],
        (dup, x1, dx2, g_pre, w_up), tasks)


def _dw_up(h2, dup, ts, tasks=()):
    s = h2.shape[0]
    ts = min(DW_TILES * ts, s)
    nt = s // ts
    half = D_MODEL // 2

    def body(h2_ref, dup_ref, send_ref, acc_ref, theirs_scr, mine_scr, got_scr, sems):
        j, o, i = pl.program_id(0), pl.program_id(1), pl.program_id(2)
        x, y, c, me, _ = _place()
        prod = _dot_tn(h2_ref[...], dup_ref[...])
        for scr, which in ((theirs_scr, 0), (mine_scr, 1)):
            @pl.when((o == which) & (i == 0))
            def _():
                scr[...] = prod

            @pl.when((o == which) & (i > 0))
            def _():
                scr[...] += prod

        slot = j % 2
        push = _remote(theirs_scr, got_scr.at[slot], sems.at[0, slot], sems.at[1, slot], (x, y, 1 - c))
        pl.when((o == 1) & (i == 0))(push.start)

        @pl.when((o == 1) & (i == nt - 1))
        def _():
            push.wait()
            part = (mine_scr[...] + got_scr[slot]).astype(BF16)
            send_ref[0] = part

            @pl.when(me == j)
            def _():
                acc_ref[0] = part

    def h2_cols(j, o, i):
        c = lax.axis_index("c")
        return i, jnp.where(o == 0, 1 - c, c)

    def own_slot(j, o, i):
        return 2 * lax.axis_index("x") + lax.axis_index("y"), 0, 0

    block = (1, half, UP_SHARD)
    return _call(
        body, "dw_up", (N_CHIPS, 2, nt),
        [pl.BlockSpec((ts, half), h2_cols), pl.BlockSpec((ts, UP_SHARD), lambda j, o, i: (i, j))],
        [pl.BlockSpec(block, lambda j, o, i: (j, 0, 0)), pl.BlockSpec(block, own_slot)],
        [_sds((N_CHIPS, half, UP_SHARD), BF16)] * 2,
        [pltpu.VMEM((half, UP_SHARD), F32), pltpu.VMEM((half, UP_SHARD), F32), pltpu.VMEM((2, half, UP_SHARD), F32),
         pltpu.SemaphoreType.DMA((2, 2))],
        (h2, dup), tasks)


def _bwd_merge(dx1, mo, sp, sr, ypool, yrnn, g_post, w_o, ts, tasks=()):
    s = dx1.shape[0]

    def body(dx1_ref, mo_ref, sp_ref, sr_ref, yp_ref, yr_ref, g_ref, wo_ref,
             dgp_ref, dgr_ref, dyp_ref, dyr_ref, dwo_ref, dg_ref):
        @pl.when(pl.program_id(0) == 0)
        def _():
            dwo_ref[...] = jnp.zeros_like(dwo_ref)
            dg_ref[...] = jnp.zeros_like(dg_ref)

        _, xh, r = _rms_fwd(mo_ref[...], g_ref[...])
        dmo, dg = _rms_bwd(xh, r, g_ref[...], dx1_ref[...])
        dg_ref[...] += dg
        dmerged = _dot_nt(dmo, wo_ref[...])
        sp = sp_ref[...].astype(F32)
        sr = sr_ref[...].astype(F32)
        yp = yp_ref[...]
        yr = yr_ref[...].astype(F32)
        dwo_ref[...] += _dot_tn(sp * yp + sr * yr, dmo)
        dgp_ref[...] = (dmerged * yp * sp * (1.0 - sp)).astype(BF16)
        dgr_ref[...] = (dmerged * yr * sr * (1.0 - sr)).astype(BF16)
        dyp_ref[...] = (dmerged * sp).astype(BF16)
        dyr_ref[...] = (dmerged * sr).astype(BF16)

    row = _rows(ts, D_MODEL)
    return _call(
        body, "bwd_merge", (s // ts,),
        [row] * 6 + [_resident((1, D_MODEL)), _resident(w_o.shape)],
        [row] * 4 + [_acc(w_o.shape), _acc((1, D_MODEL))],
        [_sds((s, D_MODEL), BF16)] * 4 + [_sds(w_o.shape), _sds((1, D_MODEL))], [],
        (dx1, mo, sp, sr, ypool, yrnn, g_post, w_o), tasks)


def _bwd_rnn(dyr, urx, gl, gg, xc, r, ig, h, conv_w, wg, lam, w_rg_out, ts, tasks=()):
    s = urx.shape[0]
    nt = s // ts
    halo_blocks = ts // CONV_HALO

    def body(dyr_ref, urx_ref, gl_ref, gg_ref, xc_ref, r_ref, ig_ref, h_ref, hh_ref, cw_ref, wg_ref, lam_ref, wrg_ref,
             durx_ref, durg_ref, dwrg_ref, dwg_ref, dcw_ref, dcb_ref, dbg_ref, dlam_ref, mu_scr, carry_scr):
        i = pl.program_id(0)
        k = nt - 1 - i

        @pl.when(i == 0)
        def _():
            mu_scr[...] = jnp.zeros_like(mu_scr)
            carry_scr[...] = jnp.zeros_like(carry_scr)
            dwrg_ref[...] = jnp.zeros_like(dwrg_ref)
            dwg_ref[...] = jnp.zeros_like(dwg_ref)
            dcw_ref[...] = jnp.zeros_like(dcw_ref)
            dcb_ref[...] = jnp.zeros_like(dcb_ref)
            dbg_ref[...] = jnp.zeros_like(dbg_ref)
            dlam_ref[...] = jnp.zeros_like(dlam_ref)

        row = lax.broadcasted_iota(jnp.int32, (ts, 1), 0)
        first_row = (k * ts + row) == 0
        h = h_ref[...]
        dyr_v = dyr_ref[...]
        dhr = _dot_nt(dyr_v, wrg_ref[...])
        gl = gl_ref[...].astype(F32)
        dwrg_ref[...] += _dot_tn(h * gl, dyr_v)
        durg_ref[...] = (dhr * h * gg_ref[...].astype(F32)).astype(BF16)
        r_v = r_ref[...].astype(F32)
        ig_v = ig_ref[...].astype(F32)
        xc_v = xc_ref[...].astype(F32)
        lam_v = lam_ref[...]
        c8, a, m2, mult = _lru_coeffs(r_v, lam_v, first_row)
        b = jnp.where(row == ts - 1, 1.0, pltpu.roll(a, ts - 1, 0))
        lt = _scan_bwd(b, dhr * gl, mu_scr[0:1, :])
        mu_scr[0:1, :] = a[0:1, :] * lt[0:1, :]
        h_before = jnp.where(k > 0, hh_ref[CONV_HALO - 1:CONV_HALO, :], 0.0)
        hprev = jnp.where(row == 0, h_before, pltpu.roll(h, 1, 0))
        dmult = lt * ig_v * xc_v
        da = lt * hprev - jnp.where(first_row, 0.0, dmult * a * lax.rsqrt(m2))
        dla = da * a
        dlam_ref[...] += jnp.sum(dla * r_v, axis=0, keepdims=True)
        dlr = (dla * (-c8)) * r_v * (1.0 - r_v)
        dli = (lt * mult * xc_v) * ig_v * (1.0 - ig_v)
        dbg_ref[0:1, :] += jnp.sum(dlr, axis=0, keepdims=True)
        dbg_ref[1:2, :] += jnp.sum(dli, axis=0, keepdims=True)
        xcb = xc_v.astype(BF16)
        parts = []
        for q in range(GATE_BLOCKS):
            blk = slice(q * GATE_BLOCK, (q + 1) * GATE_BLOCK)
            dlr_q = dlr[:, blk].astype(BF16)
            dli_q = dli[:, blk].astype(BF16)
            parts.append(_dot_nt(dlr_q, wg_ref[0, q]) + _dot_nt(dli_q, wg_ref[1, q]))
            dwg_ref[0, q] += _dot_tn(xcb[:, blk], dlr_q)
            dwg_ref[1, q] += _dot_tn(xcb[:, blk], dli_q)
        dxc = lt * mult * ig_v + jnp.concatenate(parts, axis=1)
        extd = jnp.concatenate([dxc, carry_scr[...]], axis=0)
        carry_scr[...] = dxc[0:CONV_HALO, :]
        cw = cw_ref[...]
        urx_v = urx_ref[...]
        durx = cw[3:4] * dxc
        dcw_ref[3:4, :] += jnp.sum(urx_v * dxc, axis=0, keepdims=True)
        for j in (1, 2, 3):
            dj = _shift_up(extd, j, ts)
            durx = durx + cw[3 - j:4 - j] * dj
            dcw_ref[3 - j:4 - j, :] += jnp.sum(urx_v * dj, axis=0, keepdims=True)
        durx_ref[...] = durx.astype(BF16)
        dcb_ref[...] += jnp.sum(dxc, axis=0, keepdims=True)

        @pl.when(i == nt - 1)
        def _():
            dlam_ref[...] = dlam_ref[...] * (LRU_C * jax.nn.sigmoid(-lam_v))

    row_spec = _rows(ts, D_MODEL, nt)
    halo_spec = pl.BlockSpec((CONV_HALO, D_MODEL), lambda i: (jnp.maximum((nt - 1 - i) * halo_blocks - 1, 0), 0))
    vec = _acc((1, D_MODEL))
    return _call(
        body, "bwd_rnn", (nt,),
        [row_spec] * 8 + [halo_spec, _resident(conv_w.shape), _resident(wg.shape), _resident((1, D_MODEL)),
                          _resident(w_rg_out.shape)],
        [row_spec, row_spec, _acc(w_rg_out.shape), _acc(wg.shape), _acc(conv_w.shape), vec, _acc((2, D_MODEL)), vec],
        [_sds((s, D_MODEL), BF16), _sds((s, D_MODEL), BF16), _sds(w_rg_out.shape), _sds(wg.shape), _sds(conv_w.shape),
         _sds((1, D_MODEL)), _sds((2, D_MODEL)), _sds((1, D_MODEL))],
        [pltpu.VMEM((8, D_MODEL), F32), pltpu.VMEM((CONV_HALO, D_MODEL), F32)],
        (dyr, urx, gl, gg, xc, r, ig, h, h, conv_w, wg, lam, w_rg_out), tasks)


def _bwd_pool(dyp, d, pool_w, pool_scale, w_pool_out, ts, tasks=()):
    s = d.shape[0]
    nt = s // ts

    def body(dyp_ref, d_ref, pw_ref, ps_ref, wpo_ref, dzp_ref, dwpo_ref, dpw_ref, dps_ref, carry_scr):
        i = pl.program_id(0)
        k = nt - 1 - i

        @pl.when(i == 0)
        def _():
            carry_scr[...] = jnp.zeros_like(carry_scr)
            dwpo_ref[...] = jnp.zeros_like(dwpo_ref)
            dpw_ref[...] = jnp.zeros_like(dpw_ref)
            dps_ref[...] = jnp.zeros_like(dps_ref)

        dyp_v = dyp_ref[...]
        d_v = d_ref[...]
        ps = ps_ref[...]
        dypre = _dot_nt(dyp_v[:, 0:256], wpo_ref[0])
        for j in range(1, N_CHIPS):
            dypre = dypre + _dot_nt(dyp_v[:, j * 256:(j + 1) * 256], wpo_ref[j])
        y4 = jnp.concatenate([_dot(d_v[:, g * 128:(g + 1) * 128], pw_ref[g]) for g in range(POOL_GROUPS)], axis=1)
        ypre = (y4 * ps).astype(BF16)
        for j in range(N_CHIPS):
            dwpo_ref[j] += _dot_tn(ypre, dyp_v[:, j * 256:(j + 1) * 256])
        dps_ref[...] += jnp.sum(dypre * y4, axis=0, keepdims=True)
        dy4 = dypre * ps
        t = k * ts + lax.broadcasted_iota(jnp.int32, (ts, 1), 0)
        for g, w in enumerate(POOL_WINDOWS):
            lanes = slice(g * POOL_GROUP_DIM, (g + 1) * POOL_GROUP_DIM)
            dd = _dot_nt(dy4[:, lanes], pw_ref[g])
            dpw_ref[g] += _dot_tn(d_v[:, lanes], dy4[:, lanes])
            e = dd * (1.0 / jnp.minimum(t + 1, w).astype(F32))
            acc = jnp.concatenate([e, carry_scr[:, lanes]], axis=0)
            carry_scr[:, lanes] = e[0:POOL_HALO, :]
            n = ts + POOL_HALO
            sh = 1
            while sh < w:
                acc = acc + pltpu.roll(acc, n - sh, 0)
                sh *= 2
            dzp_ref[:, lanes] = (acc[:ts, :] - dd).astype(BF16)

    return _call(
        body, "bwd_pool", (nt,),
        [_rows(ts, D_MODEL, nt), _rows(ts, POOL_WIDTH, nt), _resident(pool_w.shape), _resident((1, POOL_WIDTH)),
         _resident(w_pool_out.shape)],
        [_rows(ts, POOL_WIDTH, nt), _acc(w_pool_out.shape), _acc(pool_w.shape), _acc((1, POOL_WIDTH))],
        [_sds((s, POOL_WIDTH), BF16), _sds(w_pool_out.shape), _sds(pool_w.shape), _sds((1, POOL_WIDTH))],
        [pltpu.VMEM((POOL_HALO, POOL_WIDTH), F32)],
        (dyp, d, pool_w, pool_scale, w_pool_out), tasks)


def _assemble_dz(dz_scr, dzp_ref, durx_ref, durg_ref, dgp_ref, dgr_ref):
    dz_scr[:, 0:512] = dzp_ref[...]
    dz_scr[:, 512:1536] = durx_ref[...]
    dz_scr[:, 1536:2560] = durg_ref[...]
    dz_scr[:, 2560:3584] = dgp_ref[...]
    dz_scr[:, 3584:4608] = dgr_ref[...]


def _dw_in(h1, dzp, durx, durg, dgp, dgr, ts, tasks=()):
    s = h1.shape[0]
    ts = min(2 * ts, s)
    nt = s // ts
    half = D_MODEL // 2

    def body(h1_ref, dzp_ref, durx_ref, durg_ref, dgp_ref, dgr_ref, send_ref, acc_ref, theirs_ref, mine_ref, got_ref, sems):
        o, i = pl.program_id(0), pl.program_id(1)

        @pl.when((o == 0) & (i == 0))
        def _():
            theirs_ref[...] = jnp.zeros_like(theirs_ref)
            mine_ref[...] = jnp.zeros_like(mine_ref)

        groups = (dzp_ref, durx_ref, durg_ref, dgp_ref, dgr_ref)
        for out_ref, which in ((theirs_ref, 0), (mine_ref, 1)):
            @pl.when(o == which)
            def _():
                for group, dz_ref in enumerate(groups):
                    for j, shard_cols, group_cols in _overlaps(group):
                        out_ref[j, :, shard_cols] += _dot_tn(h1_ref[...], dz_ref[:, group_cols])

        x, y, c, me, _ = _place()
        send = _remote(theirs_ref, got_ref, sems.at[0], sems.at[1], (x, y, 1 - c))
        pl.when((o == 1) & (i == 0))(send.start)

        @pl.when((o == 1) & (i == nt - 1))
        def _():
            send.wait()
            for k in range(N_CHIPS):
                part = (mine_ref[k] + got_ref[k]).astype(BF16)
                send_ref[k] = part

                @pl.when(me == k)
                def _():
                    acc_ref[k] = part

    def h1_cols(o, i):
        c = lax.axis_index("c")
        return i, jnp.where(o == 0, 1 - c, c)

    rows = lambda width: pl.BlockSpec((ts, width), lambda o, i: (i, 0))
    shape = (N_CHIPS, half, IN_SHARD)
    whole = pl.BlockSpec(shape, lambda o, i: (0, 0, 0))
    return _call(
        body, "dw_in", (2, nt), [pl.BlockSpec((ts, half), h1_cols), rows(POOL_WIDTH)] + [rows(D_MODEL)] * 4,
        [whole, whole], [_sds(shape, BF16)] * 2,
        [pltpu.VMEM(shape, F32)] * 3 + [pltpu.SemaphoreType.DMA((2,))], (h1, dzp, durx, durg, dgp, dgr), tasks)


def _bwd_in(dzp, durx, durg, dgp, dgr, x, dx1, g_pre, w_in, ts, tasks=()):
    s = x.shape[0]

    def body(dzp_ref, durx_ref, durg_ref, dgp_ref, dgr_ref, x_ref, dx1_ref, g_ref, win_ref, gx_ref, dg_ref, dz_scr):
        @pl.when(pl.program_id(0) == 0)
        def _():
            dg_ref[...] = jnp.zeros_like(dg_ref)

        _assemble_dz(dz_scr, dzp_ref, durx_ref, durg_ref, dgp_ref, dgr_ref)
        _, xh, r = _rms_fwd(x_ref[...], g_ref[...])
        dh1 = _dot_nt(dz_scr[:, 0:IN_SHARD], win_ref[0])
        for j in range(1, N_CHIPS):
            dh1 = dh1 + _dot_nt(dz_scr[:, j * IN_SHARD:(j + 1) * IN_SHARD], win_ref[j])
        dx, dg = _rms_bwd(xh, r, g_ref[...], dh1)
        dg_ref[...] += dg
        gx_ref[...] = dx1_ref[...] + dx

    row = _rows(ts, D_MODEL)
    return _call(
        body, "bwd_in", (s // ts,),
        [_rows(ts, POOL_WIDTH)] + [row] * 6 + [_resident((1, D_MODEL)), _resident(w_in.shape)],
        [row, _acc((1, D_MODEL))], [_sds((s, D_MODEL)), _sds((1, D_MODEL))],
        [pltpu.VMEM((ts, IN_TOTAL), BF16)], (dzp, durx, durg, dgp, dgr, x, dx1, g_pre, w_in), tasks)


def _place():
    x, y, c = lax.axis_index("x"), lax.axis_index("y"), lax.axis_index("c")
    others = [(1 - x, y), (x, 1 - y), (1 - x, 1 - y)]
    return x, y, c, 2 * x + y, others


def _remote(src, dst, send_sem, recv_sem, to):
    return pltpu.make_async_remote_copy(src_ref=src, dst_ref=dst, send_sem=send_sem, recv_sem=recv_sem,
                                        device_id=to, device_id_type=MESH)


def _own_slots(ws, dtypes, name, tasks=()):
    n = len(ws)
    hbm = pl.BlockSpec(memory_space=pltpu.HBM)

    def body(*refs):
        srcs, outs, f32_bufs, cast_bufs, sems = refs[:n], refs[n:2 * n], refs[2 * n:3 * n], refs[3 * n:4 * n], refs[4 * n]
        me = _place()[3]
        loads = [pltpu.make_async_copy(srcs[k], f32_bufs[k], sems.at[k, 0]) for k in range(n)]
        stores = [pltpu.make_async_copy(cast_bufs[k], outs[k].at[me], sems.at[k, 1]) for k in range(n)]
        for cp in loads:
            cp.start()
        for k in range(n):
            loads[k].wait()
            cast_bufs[k][...] = f32_bufs[k][...].astype(dtypes[k])
            stores[k].start()
        for cp in stores:
            cp.wait()

    return _call(
        body, name, (), [hbm] * n, [hbm] * n, [_sds((N_CHIPS,) + w.shape, dt) for w, dt in zip(ws, dtypes)],
        [pltpu.VMEM(w.shape, F32) for w in ws] + [pltpu.VMEM(w.shape, dt) for w, dt in zip(ws, dtypes)]
        + [pltpu.SemaphoreType.DMA((n, 2))],
        [pltpu.with_memory_space_constraint(w, pltpu.HBM) for w in ws], tasks)


def _run(tasks, name):
    if isinstance(tasks, _Task):
        return _call(None, name, (), [], [], [], [], (), (tasks,))[1][0]
    return _call(None, name, (), [], [], [], [], (), tuple(tasks))[1]


def _gather_task(bufs, relay_steps=(0, 0)):
    n = len(bufs)
    NBR_X, NBR_Y, QUARTER_VIA_Y, QUARTER_VIA_X, SIB_X, SIB_Y, SIB_DIAG = range(7)

    def parts(out):
        x, y, c, me, _ = _place()
        ah = out.shape[1] // 2
        q = ah // 2 if (ah // 2) % 16 == 0 else ah
        return c * ah, ah, q

    def copy(out, w, k, chip, row0, rows, to, sems):
        slot = out.at[chip, pl.ds(row0, rows)]
        return _remote(slot, slot, sems[0].at[w, k], sems[1].at[w, k], to)

    def plan(out, w, sems):
        x, y, c, me, _ = _place()
        row0, ah, q = parts(out)
        xn, yn, dg = 2 * (1 - x) + y, 2 * x + (1 - y), 2 * (1 - x) + (1 - y)
        to_x, to_y, sib = (1 - x, y, c), (x, 1 - y, c), (x, y, 1 - c)
        other = (1 - c) * ah
        cp = functools.partial(copy, out, w, sems=sems)
        sends = {NBR_X: cp(NBR_X, me, row0, ah, to_x), NBR_Y: cp(NBR_Y, me, row0, ah, to_y),
                 QUARTER_VIA_Y: cp(QUARTER_VIA_Y, xn, row0, q, to_y), SIB_X: cp(SIB_X, xn, row0, ah, sib),
                 SIB_Y: cp(SIB_Y, yn, row0, ah, sib), SIB_DIAG: cp(SIB_DIAG, dg, row0, ah, sib)}
        lands = {NBR_X: cp(NBR_X, xn, row0, ah, to_x), NBR_Y: cp(NBR_Y, yn, row0, ah, to_y),
                 QUARTER_VIA_Y: cp(QUARTER_VIA_Y, dg, row0, q, to_y), SIB_X: cp(SIB_X, xn, other, ah, sib),
                 SIB_Y: cp(SIB_Y, yn, other, ah, sib), SIB_DIAG: cp(SIB_DIAG, dg, other, ah, sib)}
        if q < ah:
            sends[QUARTER_VIA_X] = cp(QUARTER_VIA_X, yn, row0 + q, ah - q, to_x)
            lands[QUARTER_VIA_X] = cp(QUARTER_VIA_X, dg, row0 + q, ah - q, to_x)
        return sends, lands

    def start(ins, outs, sems):
        for w, out in enumerate(outs):
            sends, _ = plan(out, w, sems)
            sends[NBR_X].start()
            sends[NBR_Y].start()

    def pass_neighbours(ins, outs, sems):
        for w, out in enumerate(outs):
            sends, lands = plan(out, w, sems)
            lands[NBR_X].wait_recv()
            sends[QUARTER_VIA_Y].start()
            sends[SIB_X].start()
            lands[NBR_Y].wait_recv()
            if QUARTER_VIA_X in sends:
                sends[QUARTER_VIA_X].start()
            sends[SIB_Y].start()

    def pass_diagonal(ins, outs, sems):
        for w, out in enumerate(outs):
            sends, lands = plan(out, w, sems)
            lands[QUARTER_VIA_Y].wait_recv()
            if QUARTER_VIA_X in lands:
                lands[QUARTER_VIA_X].wait_recv()
            sends[SIB_DIAG].start()

    def finish(ins, outs, sems):
        for w, out in enumerate(outs):
            sends, lands = plan(out, w, sems)
            for k in (SIB_X, SIB_Y, SIB_DIAG):
                lands[k].wait_recv()
        for w, out in enumerate(outs):
            sends, _ = plan(out, w, sems)
            for cp in sends.values():
                cp.wait_send()

    return _Task(bufs, [_sds(b.shape, b.dtype) for b in bufs], {i: i for i in range(n)},
                 [pltpu.SemaphoreType.DMA((n, 7)), pltpu.SemaphoreType.DMA((n, 7))], start, finish,
                 [(pass_neighbours, relay_steps[0]), (pass_diagonal, relay_steps[1])])


def _halves_task(grads):
    n = len(grads)

    def copy(src, out, w, sems):
        x, y, c, _, _ = _place()
        ah = out.shape[1]
        return _remote(src.at[:, pl.ds((1 - c) * ah, ah)], out, sems[0].at[w], sems[1].at[w], (x, y, 1 - c))

    def start(ins, outs, sems):
        for w, (src, out) in enumerate(zip(ins, outs)):
            copy(src, out, w, sems).start()

    def finish(ins, outs, sems):
        for w, (src, out) in enumerate(zip(ins, outs)):
            copy(src, out, w, sems).wait()

    return _Task(grads, [_sds((g.shape[0], g.shape[1] // 2, g.shape[2]), g.dtype) for g in grads], {},
                 [pltpu.SemaphoreType.DMA((n,)), pltpu.SemaphoreType.DMA((n,))], start, finish)


def _exchange_task(sends, accs):
    n = len(accs)
    given = [s for s in sends if s is not None]

    def copies(ins, outs, sems):
        send_refs = iter(ins[:len(given)])
        srcs = [next(send_refs) if s is not None else None for s in sends]
        x, y, c, me, others = _place()
        for w, out in enumerate(outs):
            for j, (ox, oy) in enumerate(others):
                src = out.at[me] if srcs[w] is None else srcs[w].at[2 * ox + oy]
                yield _remote(src, out.at[me], sems[0].at[w, j], sems[1].at[w, j], (ox, oy, c))

    def start(ins, outs, sems):
        for cp in copies(ins, outs, sems):
            cp.start()

    def finish(ins, outs, sems):
        x, y, c, _, others = _place()
        for w, out in enumerate(outs):
            for j, (ox, oy) in enumerate(others):
                slot = out.at[2 * ox + oy]
                _remote(slot, slot, sems[0].at[w, j], sems[1].at[w, j], (ox, oy, c)).wait_recv()
        for cp in copies(ins, outs, sems):
            cp.wait_send()

    return _Task(given + list(accs), [_sds(a.shape, a.dtype) for a in accs], {len(given) + i: i for i in range(n)},
                 [pltpu.SemaphoreType.DMA((n, 3)), pltpu.SemaphoreType.DMA((n, 3))], start, finish)


def _swap_task(arrays):
    n = len(arrays)

    def copy(src, out, w, sems):
        x, y, c, _, _ = _place()
        return _remote(src, out, sems[0].at[w], sems[1].at[w], (x, y, 1 - c))

    def start(ins, outs, sems):
        for w, (src, out) in enumerate(zip(ins, outs)):
            copy(src, out, w, sems).start()

    def finish(ins, outs, sems):
        for w, (src, out) in enumerate(zip(ins, outs)):
            copy(src, out, w, sems).wait()

    return _Task(arrays, [_sds(a.shape, a.dtype) for a in arrays], {},
                 [pltpu.SemaphoreType.DMA((n,)), pltpu.SemaphoreType.DMA((n,))], start, finish)


def _all_devices_task(arrays):
    n = len(arrays)
    flips = [(dx, dy, dc) for dx in (0, 1) for dy in (0, 1) for dc in (0, 1)][1:]

    def peers():
        x, y, c, _, _ = _place()
        flip = lambda v, d: 1 - v if d else v
        return 4 * x + 2 * y + c, [(flip(x, dx), flip(y, dy), flip(c, dc)) for dx, dy, dc in flips]

    def start(ins, outs, sems):
        me, others = peers()
        for w, (src, out) in enumerate(zip(ins, outs)):
            pltpu.make_async_copy(src, out.at[me], sems[2].at[w]).start()
            for k, peer in enumerate(others):
                _remote(src, out.at[me], sems[0].at[w, k], sems[1].at[w, k], peer).start()

    def finish(ins, outs, sems):
        me, others = peers()
        for w, (src, out) in enumerate(zip(ins, outs)):
            for k, (px, py, pc) in enumerate(others):
                slot = out.at[4 * px + 2 * py + pc]
                _remote(slot, slot, sems[0].at[w, k], sems[1].at[w, k], (px, py, pc)).wait_recv()
            for k, peer in enumerate(others):
                _remote(src, out.at[me], sems[0].at[w, k], sems[1].at[w, k], peer).wait_send()
            pltpu.make_async_copy(src, out.at[me], sems[2].at[w]).wait()

    return _Task(arrays, [_sds((8,) + a.shape, a.dtype) for a in arrays], {},
                 [pltpu.SemaphoreType.DMA((n, 7)), pltpu.SemaphoreType.DMA((n, 7)), pltpu.SemaphoreType.DMA((n,))],
                 start, finish)


def _share_task(shares):
    n = len(shares)

    def copy(out, w, sems, slot):
        x, y, c, _, _ = _place()
        return _remote(out.at[slot], out.at[slot], sems[0].at[w], sems[1].at[w], (x, y, 1 - c))

    def start(ins, outs, sems):
        c = _place()[2]
        for w, out in enumerate(outs):
            copy(out, w, sems, c).start()

    def finish(ins, outs, sems):
        c = _place()[2]
        for w, out in enumerate(outs):
            copy(out, w, sems, 1 - c).wait_recv()
        for w, out in enumerate(outs):
            copy(out, w, sems, c).wait_send()

    return _Task(shares, [_sds(s.shape, s.dtype) for s in shares], {i: i for i in range(n)},
                 [pltpu.SemaphoreType.DMA((n,)), pltpu.SemaphoreType.DMA((n,))], start, finish)


TILE_BYTES = 2 * 1024 * 1024
PARTIAL_TILE_BYTES = 1024 * 1024


def _in_hbm(t):
    return pltpu.with_memory_space_constraint(t, pltpu.HBM)


def _row_tile(rows, cols, limit=TILE_BYTES):
    best = 8
    for tr in range(8, rows + 1, 8):
        if rows % tr == 0 and tr * cols * 4 <= limit:
            best = tr
    assert rows % best == 0, (rows, cols)
    return best


def _chip_partial(g, got, place, wire_dtype):
    ns, ah, b = got.shape
    sharded = ns == N_CHIPS
    tr = _row_tile(ah, b, PARTIAL_TILE_BYTES)
    nb = ah // tr
    first = 0 if g.shape[1] == ah else nb

    def body(place_ref, *refs):
        g_refs, got_refs, outs = refs[:ns], refs[ns:2 * ns], refs[2 * ns:]
        parts = [g_refs[k][0] + got_refs[k][0] for k in range(ns)]
        own = parts[0]
        if sharded:
            for k in range(ns):
                outs[0][k] = parts[k].astype(wire_dtype)
                if k:
                    own = jnp.where(place_ref[0] == k, parts[k], own)
        outs[-1][0] = own.astype(wire_dtype)

    blk = (1, tr, b)
    in_specs = ([pl.BlockSpec(blk, lambda i, s, k=k: (k, s[1] * first + i, 0)) for k in range(ns)]
                + [pl.BlockSpec(blk, lambda i, s, k=k: (k, i, 0)) for k in range(ns)])
    acc_spec = pl.BlockSpec(blk, lambda i, s: (s[0], i, 0))
    acc_shape = _sds((N_CHIPS, ah, b), wire_dtype)
    out = pl.pallas_call(
        body, name="grad_chip_partial",
        grid_spec=pltpu.PrefetchScalarGridSpec(
            num_scalar_prefetch=1, grid=(nb,), in_specs=in_specs,
            out_specs=[pl.BlockSpec((ns, tr, b), lambda i, s: (0, i, 0)), acc_spec] if sharded else [acc_spec]),
        out_shape=[acc_shape, acc_shape] if sharded else [acc_shape],
        compiler_params=pltpu.CompilerParams(dimension_semantics=("arbitrary",), vmem_limit_bytes=VMEM_LIMIT),
    )(place, *([g] * ns), *([got] * ns))
    return (out[0], out[1]) if sharded else (None, out[0])


def _chip_sum(acc, place):
    _, ah, b = acc.shape
    tr = _row_tile(ah, b)

    def body(place_ref, p_ref, out_ref):
        total = p_ref[0].astype(F32) + p_ref[1].astype(F32)
        total = total + p_ref[2].astype(F32)
        out_ref[0] = total + p_ref[3].astype(F32)

    return pl.pallas_call(
        body, name="grad_chip_sum",
        grid_spec=pltpu.PrefetchScalarGridSpec(
            num_scalar_prefetch=1, grid=(ah // tr,),
            in_specs=[pl.BlockSpec((N_CHIPS, tr, b), lambda i, s: (0, i, 0))],
            out_specs=pl.BlockSpec((1, tr, b), lambda i, s: (s[1], i, 0))),
        out_shape=_sds((2, ah, b)),
        compiler_params=pltpu.CompilerParams(dimension_semantics=("arbitrary",)),
    )(place, _in_hbm(acc))


def _adam_math(w, g, m, v):
    nm = ADAM_B1 * m + (1.0 - ADAM_B1) * g
    nv = ADAM_B2 * v + (1.0 - ADAM_B2) * (g * g)
    m_hat = nm / (1.0 - ADAM_B1 ** ADAM_STEP)
    v_hat = nv / (1.0 - ADAM_B2 ** ADAM_STEP)
    return -ADAM_LR * (m_hat / (jnp.sqrt(v_hat) + ADAM_EPS) + ADAM_WD * w), nm, nv


def _adamw(w, g, m, v):
    a, b = w.shape
    tr = _row_tile(a, b)

    def body(w_ref, g_ref, m_ref, v_ref, g_out, d_ref, nm_ref, nv_ref):
        g_out[...] = g_ref[...]
        d_ref[...], nm_ref[...], nv_ref[...] = _adam_math(w_ref[...], g_ref[...], m_ref[...], v_ref[...])

    blk = pl.BlockSpec((tr, b), lambda i: (i, 0))
    return pl.pallas_call(
        body, name="adamw", grid=(a // tr,),
        in_specs=[blk] * 4, out_specs=[blk] * 4, out_shape=[_sds((a, b))] * 4,
        compiler_params=pltpu.CompilerParams(dimension_semantics=("arbitrary",)),
    )(w, g, m, v)


def _adamw_sum(w, m, v, acc, got, place):
    a, b = w.shape
    ah = a // 2
    tr = _row_tile(ah, b)
    nb = ah // tr

    def body(place_ref, w_ref, m_ref, v_ref, acc_ref, got_ref, g_out, d_ref, nm_ref, nv_ref):
        mine = (pl.program_id(0) // nb) == place_ref[1]
        part = lambda k: jnp.where(mine, acc_ref[k], got_ref[k]).astype(F32)
        g = part(0) + part(1)
        g = g + part(2)
        g = g + part(3)
        g_out[...] = g
        d_ref[...], nm_ref[...], nv_ref[...] = _adam_math(w_ref[...], g, m_ref[...], v_ref[...])

    blk = pl.BlockSpec((tr, b), lambda i, s: (i, 0))
    mine_spec = pl.BlockSpec((N_CHIPS, tr, b), lambda i, s: (0, jnp.where(i // nb == s[1], i % nb, 0), 0))
    got_spec = pl.BlockSpec((N_CHIPS, tr, b), lambda i, s: (0, jnp.where(i // nb == s[1], 0, i % nb), 0))
    return pl.pallas_call(
        body, name="adamw_sum",
        grid_spec=pltpu.PrefetchScalarGridSpec(
            num_scalar_prefetch=1, grid=(a // tr,), in_specs=[blk] * 3 + [mine_spec, got_spec], out_specs=[blk] * 4),
        out_shape=[_sds((a, b))] * 4,
        compiler_params=pltpu.CompilerParams(dimension_semantics=("arbitrary",), vmem_limit_bytes=VMEM_LIMIT),
    )(place, w, m, v, _in_hbm(acc), _in_hbm(got))


def _adamw_pieces(g, pieces, name):
    n = len(pieces)

    def body(g_ref, *refs):
        def grad(rows, cols):
            if len(g_ref.shape) == 2:
                return g_ref[rows, cols]
            total = g_ref[0, rows, cols]
            for k in range(1, g_ref.shape[0]):
                total = total + g_ref[k, rows, cols]
            return total

        ins, outs = refs[:3 * n], refs[3 * n:]
        for i, piece in enumerate(pieces):
            w_ref, m_ref, v_ref = ins[3 * i:3 * i + 3]
            o_g, o_d, o_m, o_v = outs[4 * i:4 * i + 4]
            if len(piece) == 5:
                g_v = grad(piece[3], piece[4])
                o_g[...] = g_v
                o_d[...], o_m[...], o_v[...] = _adam_math(w_ref[...], g_v, m_ref[...], v_ref[...])
            else:
                for r in range(w_ref.shape[1] // SMALL_COLS):
                    lanes = slice(r * SMALL_COLS, (r + 1) * SMALL_COLS)
                    g_v = grad(slice(piece[3] + r, piece[3] + r + 1), slice(None))
                    o_g[:, lanes] = g_v
                    o_d[:, lanes], o_m[:, lanes], o_v[:, lanes] = _adam_math(w_ref[:, lanes], g_v, m_ref[:, lanes],
                                                                            v_ref[:, lanes])

    operands = [t for piece in pieces for t in piece[:3]]
    out = pl.pallas_call(
        body, name=name,
        out_shape=[_sds(piece[0].shape) for piece in pieces for _ in range(4)],
    )(g, *operands)
    return [tuple(out[4 * i:4 * i + 4]) for i in range(n)]


TINY_ROWS, TINY_COLS = 16, 768
SMALL_COLS = 128
SMALL_ROWS = 624


def _pack_tiny(conv_w, b_gates, fcw):
    ns = conv_w.shape[0]
    pad = lambda t: jnp.pad(t, ((0, 0), (0, 0), (0, TINY_COLS - t.shape[2])))
    z = lambda rows: jnp.zeros((ns, rows, TINY_COLS), F32)
    return jnp.concatenate([pad(conv_w), pad(b_gates), z(2), fcw, z(TINY_ROWS - 11)], axis=1)


def _unpack_tiny(t):
    return t[:, 0:4, 0:256], t[:, 4:6, 0:256], t[:, 8:11, :]


def _cols_to_shards(t, n):
    return t.reshape(t.shape[0], N_CHIPS, n).transpose(1, 0, 2)


def _shards_to_cols(t):
    return t.transpose(1, 0, 2).reshape(t.shape[1], -1)


_VECTORS = ("g_mix_post", "conv_b", "lru_lambda", "g_ffn_pre", "g_ffn_post", "g_ple_gate", "g_ple_post", "pool_scale",
            "ffn_conv_b")
_VECTOR_LEN = {"pool_scale": POOL_WIDTH, "ffn_conv_b": D_FF}
POOL_W_ROWS = POOL_GROUPS * POOL_GROUP_DIM


def _vector_rows():
    rows, row = {}, POOL_W_ROWS
    for k in _VECTORS:
        rows[k] = row
        row += max(8, _VECTOR_LEN.get(k, D_MODEL) // SMALL_COLS)
    return rows, row


def _pack_small(grads, loss):
    tiles = lambda t: jnp.pad(t, ((0, -t.shape[0] % 8), (0, 0)))
    parts = [grads["pool_w"].reshape(POOL_W_ROWS, SMALL_COLS)] + [tiles(grads[k].reshape(-1, SMALL_COLS)) for k in _VECTORS]
    parts.append(tiles(loss))
    used = sum(t.shape[0] for t in parts)
    return jnp.concatenate(parts + [jnp.zeros((SMALL_ROWS - used, SMALL_COLS), F32)], axis=0)


def _gates_block_diag(w):
    w4 = w.reshape(2, GATE_BLOCKS, 4, RNN_HEAD_DIM, RNN_HEAD_DIM)
    eye = jnp.eye(4, dtype=w.dtype)
    return jnp.einsum("gqhij,hk->gqhikj", w4, eye).reshape(2, GATE_BLOCKS, GATE_BLOCK, GATE_BLOCK)


def _gates_from_block_diag(dw):
    d6 = dw.reshape(2, GATE_BLOCKS, 4, RNN_HEAD_DIM, 4, RNN_HEAD_DIM)
    blocks = [d6[:, :, hh, :, hh, :] for hh in range(4)]
    return jnp.stack(blocks, axis=2).reshape(2, RNN_HEADS, RNN_HEAD_DIM, RNN_HEAD_DIM)


ROW_TILE = 256
DW_TILES = 8

_SHARDED = ("w_in", "w_pool_out", "w_rg_out", "w_o", "w_up", "w_down", "w_ple_gate", "w_ple_proj")
_WEIGHTS = ("g_mix_pre", "g_mix_post", "w_in", "pool_w", "pool_scale", "w_pool_out", "conv_w", "conv_b", "w_rg_gates",
            "b_rg_gates", "lru_lambda", "w_rg_out", "w_o", "g_ffn_pre", "g_ffn_post", "w_up", "ffn_conv_w", "ffn_conv_b",
            "w_down", "g_ple_gate", "w_ple_gate", "w_ple_proj", "g_ple_post")


def _wire_dtype(g):
    return BF16 if g.shape[1] >= 64 and g.shape[2] > SMALL_COLS else F32


def _partials(grads, got, place):
    parts = [_chip_partial(g, r, place, _wire_dtype(g)) for g, r in zip(grads, got)]
    return [send for send, _ in parts], [acc for _, acc in parts]


def _whole(both):
    return [b.reshape(2 * b.shape[1], b.shape[2]) for b in both]


def _step(x, p, tgt, rep, place, ts):
    vec = lambda k: rep[k].reshape(1, -1)
    tall = min(2 * ts, x.shape[0])
    pool_w = rep["pool_w"].astype(BF16)
    wg = _gates_block_diag(rep["w_rg_gates"]).astype(BF16)
    sq = lambda t: t.reshape(D_MODEL, D_MODEL)
    by4 = lambda t: t.reshape(N_CHIPS, -1, D_MODEL)

    first, ride1, ride2 = (("w_in", "w_pool_out", "tiny"), ("w_rg_out", "w_o", "w_down"),
                           ("w_up", "w_ple_gate", "w_ple_proj"))
    later = ride1 + ride2
    tiny = _pack_tiny(rep["conv_w"][None], rep["b_rg_gates"][None], rep["ffn_conv_w"][None])[0]
    own_first, _ = _own_slots([rep["w_in"], rep["w_pool_out"], tiny], [BF16, BF16, F32], "own_slots_first")
    own_later, (got,) = _own_slots([rep[k] for k in later], [BF16] * len(later), "own_slots_gather_first",
                                   [_gather_task(own_first)])
    own = dict(zip(later, own_later))
    full = dict(zip(first, got))
    conv_w, b_gates, fcw = [_shards_to_cols(t) for t in _unpack_tiny(full["tiny"])]

    (urx, urg, gp, gr, d, ypool, h1), (got,) = _fwd_in_pool(
        x, vec("g_mix_pre"), full["w_in"], pool_w, vec("pool_scale"), full["w_pool_out"], ts,
        [_gather_task([own[k] for k in ride1], relay_steps=(6, 2))])
    full.update(zip(ride1, got))
    w_rg_out, w_o, w_down = sq(full["w_rg_out"]), sq(full["w_o"]), full["w_down"].reshape(D_FF, D_MODEL)
    (xc, r, ig, h, yrnn, mo, x1, glr, ggr, sp, sr), (got,) = _fwd_rnn_merge(
        urx, urg, gp, gr, ypool, x, conv_w, vec("conv_b"), wg, b_gates, vec("lru_lambda"), w_rg_out, w_o,
        vec("g_mix_post"), ts, [_gather_task([own[k] for k in ride2], relay_steps=(7, 3))])
    full.update(zip(ride2, got))
    (up, gl, gg, h2, dn, x2), _ = _fwd_ffn(x1, vec("g_ffn_pre"), full["w_up"], fcw, vec("ffn_conv_b"), w_down,
                                           vec("g_ffn_post"), ts)
    dx2, loss, d_w_gate, d_w_proj, d_g_ple_gate, d_g_ple_post = _ple_loss(
        x2, p, tgt, vec("g_ple_gate"), sq(full["w_ple_gate"]), full["w_ple_proj"], vec("g_ple_post"), tall)
    dup, d_w_down, d_fcw, d_fcb, d_g_ffn_post = _bwd_ffn_down(dx2, dn, up, gl, gg, fcw, w_down, vec("g_ffn_post"), ts)

    names1, grads1 = ("w_ple_gate", "w_ple_proj", "w_down"), [by4(d_w_gate), d_w_proj, by4(d_w_down)]
    (dx1, d_g_ffn_pre), (got1,) = _bwd_ffn_up(dup, x1, dx2, vec("g_ffn_pre"), full["w_up"], tall, [_halves_task(grads1)])
    (send_w_up, acc_w_up), (accs1,) = _dw_up(h2, dup, ts, [_exchange_task(*_partials(grads1, got1, place))])
    (dgp, dgr, dyp, dyr, d_w_o, d_g_mix_post), (theirs1,) = _bwd_merge(
        dx1, mo, sp, sr, ypool, yrnn, vec("g_mix_post"), w_o, tall, [_swap_task(accs1)])
    (durx, durg, d_w_rg_out, d_wg, d_conv_w, d_conv_b, d_b_gates, d_lam), (accs2,) = _bwd_rnn(
        dyr, urx, glr, ggr, xc, r, ig, h, conv_w, wg, vec("lru_lambda"), w_rg_out, ts,
        [_exchange_task([send_w_up], [acc_w_up])])
    names3 = ("w_o", "w_rg_out", "tiny", "w_rg_gates")
    grads3 = [by4(d_w_o), by4(d_w_rg_out),
              _pack_tiny(_cols_to_shards(d_conv_w, 256), _cols_to_shards(d_b_gates, 256), _cols_to_shards(d_fcw, 768)),
              _gates_from_block_diag(d_wg).reshape(1, 2 * RNN_HEADS * RNN_HEAD_DIM, RNN_HEAD_DIM)]
    (dzp, d_w_pool_out, d_pool_w, d_pool_scale), (got3, theirs2) = _bwd_pool(
        dyp, d, pool_w, vec("pool_scale"), full["w_pool_out"], tall, [_halves_task(grads3), _swap_task(accs2)])
    replicated = {"g_mix_post": d_g_mix_post, "conv_b": d_conv_b, "lru_lambda": d_lam, "g_ffn_pre": d_g_ffn_pre,
                  "g_ffn_post": d_g_ffn_post, "g_ple_gate": d_g_ple_gate, "g_ple_post": d_g_ple_post,
                  "pool_scale": d_pool_scale, "ffn_conv_b": d_fcb, "pool_w": d_pool_w}
    names4 = ("w_in", "w_pool_out", "small")
    small4 = [d_w_pool_out, _pack_small(replicated, loss)[None]]
    (send_w_in, acc_w_in), (accs3, got_small4) = _dw_in(
        h1, dzp, durx, durg, dgp, dgr, ts, [_exchange_task(*_partials(grads3, got3, place)), _halves_task(small4)])
    sends4, accs4 = _partials(small4, got_small4, place)
    (grad_x, d_g_mix_pre), (accs4, theirs3, both3) = _bwd_in(
        dzp, durx, durg, dgp, dgr, x, dx1, vec("g_mix_pre"), full["w_in"], tall,
        [_exchange_task([send_w_in] + sends4, [acc_w_in] + accs4), _swap_task(accs3[:2]),
         _share_task([_chip_sum(acc, place) for acc in accs3[2:]])])
    theirs4, both4, (g_mix_pre_parts,) = _run(
        [_swap_task(accs4[:2]), _share_task([_chip_sum(acc, place) for acc in accs4[2:]]),
         _all_devices_task([d_g_mix_pre.reshape(SUBLANES, SMALL_COLS)])], "grad_sibling_share")
    mine = accs1 + accs2 + accs3[:2] + accs4[:2]
    partials = dict(zip(names1 + ("w_up",) + names3[:2] + names4[:2], zip(mine, theirs1 + theirs2 + theirs3 + theirs4)))
    return grad_x, partials, dict(zip(names3[2:] + names4[2:], _whole(both3) + _whole(both4))), g_mix_pre_parts


def kernel(x, p, g_mix_pre, g_mix_post, w_in, pool_w, pool_scale, w_pool_out, conv_w, conv_b, w_rg_gates, b_rg_gates, lru_lambda, w_rg_out, w_o, g_ffn_pre, g_ffn_post, w_up, ffn_conv_w, ffn_conv_b, w_down, g_ple_gate, w_ple_gate, w_ple_proj, g_ple_post, loss_target, m_g_mix_pre, m_g_mix_post, m_w_in, m_pool_w, m_pool_scale, m_w_pool_out, m_conv_w, m_conv_b, m_w_rg_gates, m_b_rg_gates, m_lru_lambda, m_w_rg_out, m_w_o, m_g_ffn_pre, m_g_ffn_post, m_w_up, m_ffn_conv_w, m_ffn_conv_b, m_w_down, m_g_ple_gate, m_w_ple_gate, m_w_ple_proj, m_g_ple_post, v_g_mix_pre, v_g_mix_post, v_w_in, v_pool_w, v_pool_scale, v_w_pool_out, v_conv_w, v_conv_b, v_w_rg_gates, v_b_rg_gates, v_lru_lambda, v_w_rg_out, v_w_o, v_g_ffn_pre, v_g_ffn_post, v_w_up, v_ffn_conv_w, v_ffn_conv_b, v_w_down, v_g_ple_gate, v_w_ple_gate, v_w_ple_proj, v_g_ple_post):
    args = dict(locals())
    w = {k: args[k][0] for k in _WEIGHTS}
    m = {k: args["m_" + k][0] for k in _WEIGHTS}
    v = {k: args["v_" + k][0] for k in _WEIGHTS}
    place = jnp.stack([2 * lax.axis_index("x") + lax.axis_index("y"), lax.axis_index("c")]).astype(jnp.int32)
    grad_x, partials, reduced, g_mix_pre_parts = _step(x[0], p[0, 0], loss_target[0], w, place, ROW_TILE)

    gates_2d = (2 * RNN_HEADS * RNN_HEAD_DIM, RNN_HEAD_DIM)
    as2d = lambda k, shape: tuple(t[k].reshape(shape) for t in (w, m, v))
    done = {k: tuple(_adamw_sum(w[k], m[k], v[k], *partials[k], place)) for k in _SHARDED}
    gates_w, gates_m, gates_v = as2d("w_rg_gates", gates_2d)
    done["w_rg_gates"] = tuple(_adamw(gates_w, reduced["w_rg_gates"], gates_m, gates_v))
    tiny_names = ("conv_w", "b_rg_gates", "ffn_conv_w")
    tiny_at = ((slice(0, 4), slice(0, 256)), (slice(4, 6), slice(0, 256)), (slice(8, 11), slice(None)))
    done.update(zip(tiny_names, _adamw_pieces(
        reduced["tiny"], [(w[k], m[k], v[k]) + at for k, at in zip(tiny_names, tiny_at)], "adamw_tiny")))
    vector_rows, loss_row = _vector_rows()
    pieces = [as2d("pool_w", (POOL_W_ROWS, SMALL_COLS)) + (slice(0, POOL_W_ROWS), slice(None))]
    pieces += [as2d(k, (1, -1)) + (vector_rows[k],) for k in _VECTORS]
    done.update(zip(("pool_w",) + _VECTORS, _adamw_pieces(reduced["small"], pieces, "adamw_small")))
    done["g_mix_pre"] = _adamw_pieces(g_mix_pre_parts, [as2d("g_mix_pre", (1, -1)) + (0,)], "adamw_g_mix_pre")[0]

    result = [reduced["small"][loss_row, 0], grad_x[None]]
    for kind in range(4):
        result += [done[k][kind].reshape(args[k].shape) for k in _WEIGHTS]
    return tuple(result)
```

```python
import functools

import jax
import jax.numpy as jnp
from jax import lax
from jax.experimental import pallas as pl
from jax.experimental.pallas import tpu as pltpu

F32 = jnp.float32
BF16 = jnp.bfloat16

D_MODEL = 1024
POOL_WINDOWS = (2, 4, 8, 16)
POOL_GROUPS = 4
POOL_WIDTH = 512
POOL_GROUP_DIM = 128
RNN_HEADS = 16
RNN_HEAD_DIM = 64
GATE_BLOCK = 256
GATE_BLOCKS = D_MODEL // GATE_BLOCK
LRU_C = 8.0
D_FF = 3072
PLE_DIM = 256
RMS_EPS = 1e-6
IN_TOTAL = 4608
N_CHIPS = 4
IN_SHARD = IN_TOTAL // N_CHIPS
UP_SHARD = 2 * D_FF // N_CHIPS
Z_GROUPS = ((0, 512), (512, 1536), (1536, 2560), (2560, 3584), (3584, 4608))
POOL_HALO = 16
CONV_HALO = 8

ADAM_LR = 0.001
ADAM_B1 = 0.9
ADAM_B2 = 0.999
ADAM_EPS = 1e-08
ADAM_WD = 0.01
ADAM_STEP = 10

VMEM_LIMIT = 56 * 1024 * 1024
MESH = pl.DeviceIdType.MESH

_GELU_C = 0.7978845608028654
_GELU_A = 0.044715


def _dot(a, b):
    return jnp.dot(a.astype(BF16), b.astype(BF16), preferred_element_type=F32)


def _dot_nt(a, b):
    return lax.dot_general(a.astype(BF16), b.astype(BF16), (((1,), (1,)), ((), ())), preferred_element_type=F32)


def _dot_tn(a, b):
    return lax.dot_general(a.astype(BF16), b.astype(BF16), (((0,), (0,)), ((), ())), preferred_element_type=F32)


def _overlaps(group):
    a, b = Z_GROUPS[group]
    found = []
    for j in range(N_CHIPS):
        lo, hi = max(a, j * IN_SHARD), min(b, (j + 1) * IN_SHARD)
        if lo < hi:
            found.append((j, slice(lo - j * IN_SHARD, hi - j * IN_SHARD), slice(lo - a, hi - a)))
    return found


def _rms_fwd(x, g):
    r = lax.rsqrt(jnp.mean(x * x, axis=-1, keepdims=True) + RMS_EPS)
    xh = x * r
    return xh * g, xh, r


def _rms_bwd(xh, r, g, dy):
    dxh = dy * g
    dg = jnp.sum(dy * xh, axis=0, keepdims=True)
    dx = r * (dxh - xh * jnp.mean(dxh * xh, axis=-1, keepdims=True))
    return dx, dg


def _sigmoid(x):
    return 0.5 * jnp.tanh(0.5 * x) + 0.5


def _gelu(x):
    t = jnp.tanh(_GELU_C * (x + _GELU_A * x * x * x))
    return 0.5 * x * (1.0 + t), t


def _gelu_grad(x, t):
    return 0.5 * (1.0 + t) + 0.5 * x * (1.0 - t * t) * _GELU_C * (1.0 + 3.0 * _GELU_A * x * x)


def _softplus_neg(lam):
    nl = -lam
    return jnp.maximum(nl, 0.0) + jnp.log(1.0 + jnp.exp(-jnp.abs(nl)))


def _lru_coeffs(r, lam, first_row):
    c8 = LRU_C * _softplus_neg(lam)
    la = -(c8 * r)
    a = jnp.exp(la)
    m2 = jnp.tanh(-la) * (1.0 + a * a)
    mult = jnp.where(first_row, 1.0, jnp.sqrt(m2))
    return c8, a, m2, mult


SUBLANES = 8


def _scan_fwd(a, u, carry):
    n = a.shape[0]
    sub = lax.broadcasted_iota(jnp.int32, (n, 1), 0) % SUBLANES
    acc_a, acc_h = a, u
    for s in (1, 2, 4):
        m = sub >= s
        h_s = jnp.where(m, pltpu.roll(acc_h, s, 0), 0.0)
        a_s = jnp.where(m, pltpu.roll(acc_a, s, 0), 1.0)
        acc_h = acc_a * h_s + acc_h
        acc_a = acc_a * a_s
    out = []
    for g in range(n // SUBLANES):
        rows = slice(g * SUBLANES, (g + 1) * SUBLANES)
        out.append(acc_h[rows] + acc_a[rows] * carry)
        carry = out[-1][SUBLANES - 1:SUBLANES]
    return jnp.concatenate(out, axis=0)


def _scan_bwd(b, g, carry):
    n = b.shape[0]
    sub = lax.broadcasted_iota(jnp.int32, (n, 1), 0) % SUBLANES
    acc_b, acc_l = b, g
    for s in (1, 2, 4):
        m = sub < SUBLANES - s
        l_s = jnp.where(m, pltpu.roll(acc_l, n - s, 0), 0.0)
        b_s = jnp.where(m, pltpu.roll(acc_b, n - s, 0), 1.0)
        acc_l = acc_b * l_s + acc_l
        acc_b = acc_b * b_s
    out = [None] * (n // SUBLANES)
    for g in reversed(range(n // SUBLANES)):
        rows = slice(g * SUBLANES, (g + 1) * SUBLANES)
        out[g] = acc_l[rows] + acc_b[rows] * carry
        carry = out[g][0:1]
    return jnp.concatenate(out, axis=0)


def _shift_down(ext, k, halo):
    return pltpu.roll(ext, k, 0)[halo:] if k else ext[halo:]


def _shift_up(ext, k, ts):
    return pltpu.roll(ext, ext.shape[0] - k, 0)[:ts] if k else ext[:ts]


def _rows(ts, width, nt=None, col=0):
    if nt is None:
        return pl.BlockSpec((ts, width), lambda i: (i, col))
    return pl.BlockSpec((ts, width), lambda i: (nt - 1 - i, col))


def _resident(shape):
    zeros = (0,) * len(shape)
    return pl.BlockSpec(shape, lambda i: zeros, pipeline_mode=pl.Buffered(1))


def _acc(shape):
    zeros = (0,) * len(shape)
    return pl.BlockSpec(shape, lambda i: zeros)


def _params():
    return pltpu.CompilerParams(dimension_semantics=("arbitrary",), vmem_limit_bytes=VMEM_LIMIT)


def _sds(shape, dtype=F32):
    return jax.ShapeDtypeStruct(shape, dtype)


class _Task:
    def __init__(self, ins, out_shapes, aliases, sems, start, finish, relays=()):
        self.ins, self.out_shapes, self.aliases, self.sems = list(ins), list(out_shapes), dict(aliases), list(sems)
        self.start, self.relays, self.finish = start, list(relays), finish


def _call(body, name, grid, in_specs, out_specs, out_shape, scratch_shapes, args, tasks=()):
    n_in, n_out, n_scr = len(in_specs), len(out_specs), len(scratch_shapes)
    t_in = [len(t.ins) for t in tasks]
    t_out = [len(t.out_shapes) for t in tasks]
    t_sem = [len(t.sems) for t in tasks]
    steps = 1
    for g in grid:
        steps *= g

    def take(refs, pos, counts):
        groups = []
        for c in counts:
            groups.append(refs[pos:pos + c])
            pos += c
        return groups, pos

    def wrapped(*refs):
        (cin,), pos = take(refs, 0, [n_in])
        tin, pos = take(refs, pos, t_in)
        (cout,), pos = take(refs, pos, [n_out])
        tout, pos = take(refs, pos, t_out)
        (cscr,), pos = take(refs, pos, [n_scr])
        tsem, pos = take(refs, pos, t_sem)
        if not grid:
            for t, a, b, c in zip(tasks, tin, tout, tsem):
                t.start(a, b, c)
            if body is not None:
                body(*cin, *cout, *cscr)
            for t, a, b, c in zip(tasks, tin, tout, tsem):
                for relay, _ in t.relays:
                    relay(a, b, c)
            for t, a, b, c in zip(tasks, tin, tout, tsem):
                t.finish(a, b, c)
            return
        step = pl.program_id(0)
        for axis in range(1, len(grid)):
            step = step * grid[axis] + pl.program_id(axis)
        if tasks:
            @pl.when(step == 0)
            def _():
                for t, a, b, c in zip(tasks, tin, tout, tsem):
                    t.start(a, b, c)

        body(*cin, *cout, *cscr)
        for t, a, b, c in zip(tasks, tin, tout, tsem):
            for relay, before in t.relays:
                pl.when(step == max(steps - 1 - before, 0))(functools.partial(relay, a, b, c))

        if tasks:
            @pl.when(step == steps - 1)
            def _():
                for t, a, b, c in zip(tasks, tin, tout, tsem):
                    t.finish(a, b, c)

    aliases, in_pos, out_pos = {}, n_in, n_out
    for t, ni, no in zip(tasks, t_in, t_out):
        aliases.update({in_pos + a: out_pos + b for a, b in t.aliases.items()})
        in_pos, out_pos = in_pos + ni, out_pos + no
    any_spec = pl.BlockSpec(memory_space=pltpu.HBM)
    kwargs = dict(grid=grid, compiler_params=pltpu.CompilerParams(
        dimension_semantics=("arbitrary",) * len(grid), vmem_limit_bytes=VMEM_LIMIT)) if grid else dict(
        compiler_params=pltpu.CompilerParams(vmem_limit_bytes=VMEM_LIMIT))
    out = pl.pallas_call(
        wrapped, name=name,
        in_specs=list(in_specs) + [any_spec] * sum(t_in),
        out_specs=list(out_specs) + [any_spec] * sum(t_out),
        out_shape=list(out_shape) + [s for t in tasks for s in t.out_shapes],
        scratch_shapes=list(scratch_shapes) + [s for t in tasks for s in t.sems],
        input_output_aliases=aliases, **kwargs,
    )(*args, *[pltpu.with_memory_space_constraint(a, pltpu.HBM) for t in tasks for a in t.ins])
    task_outs, pos = take(list(out), n_out, t_out)
    return list(out[:n_out]), task_outs


def _fwd_in_pool(x, g_pre, w_in, pool_w, pool_scale, w_pool_out, ts, tasks=()):
    s = x.shape[0]

    def body(x_ref, g_ref, win_ref, pw_ref, ps_ref, wpo_ref,
             urx_ref, urg_ref, gp_ref, gr_ref, d_ref, yp_ref, h1_ref, halo_scr):
        i = pl.program_id(0)

        @pl.when(i == 0)
        def _():
            halo_scr[...] = jnp.zeros_like(halo_scr)

        h1, _, _ = _rms_fwd(x_ref[...], g_ref[...])
        h1 = h1.astype(BF16)
        h1_ref[...] = h1
        u = jnp.dot(h1, win_ref[0, :, 0:POOL_WIDTH], preferred_element_type=F32)
        for group, out_ref in zip(range(1, len(Z_GROUPS)), (urx_ref, urg_ref, gp_ref, gr_ref)):
            for j, shard_cols, group_cols in _overlaps(group):
                out_ref[:, group_cols] = jnp.dot(h1, win_ref[j, :, shard_cols],
                                                 preferred_element_type=F32).astype(BF16)
        ext = jnp.concatenate([halo_scr[...], u], axis=0)
        halo_scr[...] = u[ts - POOL_HALO:, :]
        t = i * ts + lax.broadcasted_iota(jnp.int32, (ts, 1), 0)
        y4 = []
        for g, w in enumerate(POOL_WINDOWS):
            lanes = slice(g * POOL_GROUP_DIM, (g + 1) * POOL_GROUP_DIM)
            acc = ext[:, lanes]
            sh = 1
            while sh < w:
                acc = acc + pltpu.roll(acc, sh, 0)
                sh *= 2
            inv = 1.0 / jnp.minimum(t + 1, w).astype(F32)
            dg = acc[POOL_HALO:, :] * inv - u[:, lanes]
            d_ref[:, lanes] = dg.astype(BF16)
            y4.append(_dot(dg, pw_ref[g]))
        ypre = jnp.concatenate(y4, axis=1) * ps_ref[...]
        ypre = ypre.astype(BF16)
        for j in range(N_CHIPS):
            yp_ref[:, j * 256:(j + 1) * 256] = jnp.dot(ypre, wpo_ref[j], preferred_element_type=F32)

    return _call(
        body, "fwd_in_pool", (s // ts,),
        [_rows(ts, D_MODEL), _resident((1, D_MODEL)), _resident(w_in.shape), _resident(pool_w.shape),
         _resident((1, POOL_WIDTH)), _resident(w_pool_out.shape)],
        [_rows(ts, D_MODEL)] * 4 + [_rows(ts, POOL_WIDTH), _rows(ts, D_MODEL), _rows(ts, D_MODEL)],
        [_sds((s, D_MODEL), BF16)] * 4 + [_sds((s, POOL_WIDTH), BF16), _sds((s, D_MODEL)), _sds((s, D_MODEL), BF16)],
        [pltpu.VMEM((POOL_HALO, POOL_WIDTH), F32)],
        (x, g_pre, w_in, pool_w, pool_scale, w_pool_out), tasks)


def _fwd_rnn_merge(urx, urg, gp, gr, ypool, x, conv_w, conv_b, wg, bg, lam, w_rg_out, w_o, g_post, ts, tasks=()):
    s = x.shape[0]

    def body(urx_ref, urg_ref, gp_ref, gr_ref, yp_ref, x_ref, cw_ref, cb_ref, wg_ref, bg_ref, lam_ref, wrg_ref, wo_ref,
             gpost_ref, xc_ref, r_ref, ig_ref, h_ref, yr_ref, mo_ref, x1_ref, gl_ref, gg_ref, sp_ref, sr_ref,
             halo_scr, carry_scr):
        i = pl.program_id(0)

        @pl.when(i == 0)
        def _():
            halo_scr[...] = jnp.zeros_like(halo_scr)
            carry_scr[...] = jnp.zeros_like(carry_scr)

        urx_v = urx_ref[...].astype(F32)
        ext = jnp.concatenate([halo_scr[...], urx_v], axis=0)
        halo_scr[...] = urx_v[ts - CONV_HALO:, :]
        cw = cw_ref[...]
        xc = (cb_ref[...] + cw[3:4] * urx_v + cw[2:3] * _shift_down(ext, 1, CONV_HALO)
              + cw[1:2] * _shift_down(ext, 2, CONV_HALO) + cw[0:1] * _shift_down(ext, 3, CONV_HALO))
        xc_ref[...] = xc.astype(BF16)
        xcb = xc.astype(BF16)
        lin = []
        for gate in range(2):
            parts = [jnp.dot(xcb[:, q * GATE_BLOCK:(q + 1) * GATE_BLOCK], wg_ref[gate, q], preferred_element_type=F32)
                     for q in range(GATE_BLOCKS)]
            lin.append(jnp.concatenate(parts, axis=1) + bg_ref[gate:gate + 1, :])
        r = _sigmoid(lin[0])
        ig = _sigmoid(lin[1])
        r_ref[...] = r.astype(BF16)
        ig_ref[...] = ig.astype(BF16)
        first_row = (i * ts + lax.broadcasted_iota(jnp.int32, (ts, 1), 0)) == 0
        _, a, _, mult = _lru_coeffs(r, lam_ref[...], first_row)
        h = _scan_fwd(a, mult * ig * xc, carry_scr[0:1, :])
        carry_scr[0:1, :] = h[ts - 1:ts, :]
        h_ref[...] = h
        urg_v = urg_ref[...].astype(F32)
        gl, t = _gelu(urg_v)
        gl_ref[...] = gl.astype(BF16)
        gg_ref[...] = _gelu_grad(urg_v, t).astype(BF16)
        yr = _dot(h * gl, wrg_ref[...])
        yr_ref[...] = yr.astype(BF16)
        sp = _sigmoid(gp_ref[...].astype(F32))
        sr = _sigmoid(gr_ref[...].astype(F32))
        sp_ref[...] = sp.astype(BF16)
        sr_ref[...] = sr.astype(BF16)
        merged = sp * yp_ref[...] + sr * yr
        mo = _dot(merged, wo_ref[...])
        mo_ref[...] = mo
        y, _, _ = _rms_fwd(mo, gpost_ref[...])
        x1_ref[...] = x_ref[...] + y

    row = _rows(ts, D_MODEL)
    return _call(
        body, "fwd_rnn_merge", (s // ts,),
        [row] * 6 + [_resident(conv_w.shape), _resident((1, D_MODEL)), _resident(wg.shape), _resident(bg.shape),
                     _resident((1, D_MODEL)), _resident(w_rg_out.shape), _resident(w_o.shape), _resident((1, D_MODEL))],
        [row] * 11, [_sds((s, D_MODEL), dt) for dt in (BF16, BF16, BF16, F32, BF16, F32, F32, BF16, BF16, BF16, BF16)],
        [pltpu.VMEM((CONV_HALO, D_MODEL), F32), pltpu.VMEM((8, D_MODEL), F32)],
        (urx, urg, gp, gr, ypool, x, conv_w, conv_b, wg, bg, lam, w_rg_out, w_o, g_post), tasks)


def _fwd_ffn(x1, g_pre, w_up, fcw, fcb, w_down, g_post, ts, tasks=()):
    s = x1.shape[0]

    def body(x1_ref, g_ref, wup_ref, fcw_ref, fcb_ref, wd_ref, gpost_ref,
             up_ref, gl_ref, gg_ref, h2_ref, dn_ref, x2_ref, up_scr, halo_scr):
        i = pl.program_id(0)

        @pl.when(i == 0)
        def _():
            halo_scr[...] = jnp.zeros_like(halo_scr)

        x1_v = x1_ref[...]
        h2, _, _ = _rms_fwd(x1_v, g_ref[...])
        h2 = h2.astype(BF16)
        h2_ref[...] = h2
        for j in range(N_CHIPS):
            up_scr[:, j * UP_SHARD:(j + 1) * UP_SHARD] = jnp.dot(h2, wup_ref[j], preferred_element_type=F32)
        up_ref[...] = up_scr[...].astype(BF16)
        ug = up_scr[:, 0:D_FF]
        ext = jnp.concatenate([halo_scr[...], ug], axis=0)
        halo_scr[...] = ug[ts - CONV_HALO:, :]
        w = fcw_ref[...]
        gh = (fcb_ref[...] + w[2:3] * ug + w[1:2] * _shift_down(ext, 1, CONV_HALO)
              + w[0:1] * _shift_down(ext, 2, CONV_HALO))
        gl, t = _gelu(gh)
        gl_ref[...] = gl.astype(BF16)
        gg_ref[...] = _gelu_grad(gh, t).astype(BF16)
        dn = _dot(gl * up_scr[:, D_FF:], wd_ref[...])
        dn_ref[...] = dn
        y, _, _ = _rms_fwd(dn, gpost_ref[...])
        x2_ref[...] = x1_v + y

    row = _rows(ts, D_MODEL)
    return _call(
        body, "fwd_ffn", (s // ts,),
        [row, _resident((1, D_MODEL)), _resident(w_up.shape), _resident(fcw.shape), _resident((1, D_FF)),
         _resident(w_down.shape), _resident((1, D_MODEL))],
        [_rows(ts, 2 * D_FF), _rows(ts, D_FF), _rows(ts, D_FF), row, row, row],
        [_sds((s, 2 * D_FF), BF16), _sds((s, D_FF), BF16), _sds((s, D_FF), BF16), _sds((s, D_MODEL), BF16),
         _sds((s, D_MODEL)), _sds((s, D_MODEL))],
        [pltpu.VMEM((ts, 2 * D_FF), F32), pltpu.VMEM((CONV_HALO, D_FF), F32)],
        (x1, g_pre, w_up, fcw, fcb, w_down, g_post), tasks)


def _ple_loss(x2, p, tgt, g_gate, w_gate, w_proj, g_post, ts):
    s = x2.shape[0]

    def body(x2_ref, p_ref, t_ref, gg_ref, wg_ref, wp_ref, gp_ref, dx2_ref, loss_ref, dwg_ref, dwp_ref, dgg_ref, dgp_ref):
        @pl.when(pl.program_id(0) == 0)
        def _():
            loss_ref[...] = jnp.zeros_like(loss_ref)
            dwg_ref[...] = jnp.zeros_like(dwg_ref)
            dwp_ref[...] = jnp.zeros_like(dwp_ref)
            dgg_ref[...] = jnp.zeros_like(dgg_ref)
            dgp_ref[...] = jnp.zeros_like(dgp_ref)

        x2_v = x2_ref[...]
        n3, xh3, r3 = _rms_fwd(x2_v, gg_ref[...])
        pg = _sigmoid(_dot(n3, wg_ref[...]))
        pb = p_ref[...].astype(BF16)
        q = jnp.concatenate([jnp.dot(pb, wp_ref[j], preferred_element_type=F32) for j in range(N_CHIPS)], axis=1)
        ple, qh, rq = _rms_fwd(q, gp_ref[...])
        e = x2_v + pg * ple - t_ref[...]
        loss_ref[...] += 0.5 * jnp.sum(jnp.mean(e * e, axis=-1, keepdims=True), axis=0, keepdims=True)
        dy = e * (1.0 / D_MODEL)
        dpgl = dy * ple * pg * (1.0 - pg)
        dwg_ref[...] += _dot_tn(n3, dpgl)
        dx3, dgg = _rms_bwd(xh3, r3, gg_ref[...], _dot_nt(dpgl, wg_ref[...]))
        dgg_ref[...] += dgg
        dq, dgp = _rms_bwd(qh, rq, gp_ref[...], dy * pg)
        dgp_ref[...] += dgp
        for j in range(N_CHIPS):
            dwp_ref[j] += _dot_tn(pb, dq[:, j * 256:(j + 1) * 256])
        dx2_ref[...] = dy + dx3

    row = _rows(ts, D_MODEL)
    vec = _acc((1, D_MODEL))
    return pl.pallas_call(
        body, name="ple_loss", grid=(s // ts,),
        in_specs=[row, _rows(ts, PLE_DIM), row, _resident((1, D_MODEL)), _resident(w_gate.shape), _resident(w_proj.shape),
                  _resident((1, D_MODEL))],
        out_specs=[row, _acc((1, 128)), _acc(w_gate.shape), _acc(w_proj.shape), vec, vec],
        out_shape=[_sds((s, D_MODEL)), _sds((1, 128)), _sds(w_gate.shape), _sds(w_proj.shape), _sds((1, D_MODEL)),
                   _sds((1, D_MODEL))],
        compiler_params=_params(),
    )(x2, p, tgt, g_gate, w_gate, w_proj, g_post)


def _bwd_ffn_down(dx2, dn, up, gl, gg, fcw, w_down, g_post, ts):
    s = dx2.shape[0]
    nt = s // ts

    def body(dx2_ref, dn_ref, up_ref, gl_ref, gg_ref, fcw_ref, wd_ref, gpost_ref,
             dup_ref, dwd_ref, dfcw_ref, dfcb_ref, dgp_ref, carry_scr):
        i = pl.program_id(0)

        @pl.when(i == 0)
        def _():
            carry_scr[...] = jnp.zeros_like(carry_scr)
            dwd_ref[...] = jnp.zeros_like(dwd_ref)
            dfcw_ref[...] = jnp.zeros_like(dfcw_ref)
            dfcb_ref[...] = jnp.zeros_like(dfcb_ref)
            dgp_ref[...] = jnp.zeros_like(dgp_ref)

        _, xh, r = _rms_fwd(dn_ref[...], gpost_ref[...])
        ddn, dgp = _rms_bwd(xh, r, gpost_ref[...], dx2_ref[...])
        dgp_ref[...] += dgp
        dhid = _dot_nt(ddn, wd_ref[...])
        ug = up_ref[:, 0:D_FF].astype(F32)
        uv = up_ref[:, D_FF:].astype(F32)
        gl = gl_ref[...].astype(F32)
        w = fcw_ref[...]
        dwd_ref[...] += _dot_tn(gl * uv, ddn)
        dgh = dhid * uv * gg_ref[...].astype(F32)
        dup_ref[:, D_FF:] = (dhid * gl).astype(BF16)
        extd = jnp.concatenate([dgh, carry_scr[...]], axis=0)
        carry_scr[...] = dgh[0:CONV_HALO, :]
        d1 = _shift_up(extd, 1, ts)
        d2 = _shift_up(extd, 2, ts)
        dup_ref[:, 0:D_FF] = (w[2:3] * dgh + w[1:2] * d1 + w[0:1] * d2).astype(BF16)
        dfcw_ref[2:3, :] += jnp.sum(ug * dgh, axis=0, keepdims=True)
        dfcw_ref[1:2, :] += jnp.sum(ug * d1, axis=0, keepdims=True)
        dfcw_ref[0:1, :] += jnp.sum(ug * d2, axis=0, keepdims=True)
        dfcb_ref[...] += jnp.sum(dgh, axis=0, keepdims=True)

    row = _rows(ts, D_MODEL, nt)
    wide = _rows(ts, D_FF, nt)
    return pl.pallas_call(
        body, name="bwd_ffn_down", grid=(nt,),
        in_specs=[row, row, _rows(ts, 2 * D_FF, nt), wide, wide, _resident(fcw.shape), _resident(w_down.shape),
                  _resident((1, D_MODEL))],
        out_specs=[_rows(ts, 2 * D_FF, nt), _acc(w_down.shape), _acc(fcw.shape), _acc((1, D_FF)), _acc((1, D_MODEL))],
        out_shape=[_sds((s, 2 * D_FF), BF16), _sds(w_down.shape), _sds(fcw.shape), _sds((1, D_FF)), _sds((1, D_MODEL))],
        scratch_shapes=[pltpu.VMEM((CONV_HALO, D_FF), F32)],
        compiler_params=_params(),
    )(dx2, dn, up, gl, gg, fcw, w_down, g_post)


def _bwd_ffn_up(dup, x1, dx2, g_pre, w_up, ts, tasks=()):
    s = x1.shape[0]

    def body(dup_ref, x1_ref, dx2_ref, g_ref, wup_ref, dx1_ref, dg_ref):
        @pl.when(pl.program_id(0) == 0)
        def _():
            dg_ref[...] = jnp.zeros_like(dg_ref)

        _, xh, r = _rms_fwd(x1_ref[...], g_ref[...])
        dh2 = _dot_nt(dup_ref[:, 0:UP_SHARD], wup_ref[0])
        for j in range(1, N_CHIPS):
            dh2 = dh2 + _dot_nt(dup_ref[:, j * UP_SHARD:(j + 1) * UP_SHARD], wup_ref[j])
        dx, dg = _rms_bwd(xh, r, g_ref[...], dh2)
        dg_ref[...] += dg
        dx1_ref[...] = dx2_ref[...] + dx

    row = _rows(ts, D_MODEL)
    return _call(
        body, "bwd_ffn_up", (s // ts,),
        [_rows(ts, 2 * D_FF), row, row, _resident((1, D_MODEL)), _resident(w_up.shape)],
        [row, _acc((1, D_MODEL))], [_sds((s, D_MODEL)), _sds((1, D_MODEL))], [],
        (dup, x1, dx2, g_pre, w_up), tasks)


def _dw_up(h2, dup, ts, tasks=()):
    s = h2.shape[0]
    ts = min(DW_TILES * ts, s)
    nt = s // ts
    half = D_MODEL // 2

    def body(h2_ref, dup_ref, send_ref, acc_ref, theirs_scr, mine_scr, got_scr, sems):
        j, o, i = pl.program_id(0), pl.program_id(1), pl.program_id(2)
        x, y, c, me, _ = _place()
        prod = _dot_tn(h2_ref[...], dup_ref[...])
        for scr, which in ((theirs_scr, 0), (mine_scr, 1)):
            @pl.when((o == which) & (i == 0))
            def _():
                scr[...] = prod

            @pl.when((o == which) & (i > 0))
            def _():
                scr[...] += prod

        slot = j % 2
        push = _remote(theirs_scr, got_scr.at[slot], sems.at[0, slot], sems.at[1, slot], (x, y, 1 - c))
        pl.when((o == 1) & (i == 0))(push.start)

        @pl.when((o == 1) & (i == nt - 1))
        def _():
            push.wait()
            part = (mine_scr[...] + got_scr[slot]).astype(BF16)
            send_ref[0] = part

            @pl.when(me == j)
            def _():
                acc_ref[0] = part

    def h2_cols(j, o, i):
        c = lax.axis_index("c")
        return i, jnp.where(o == 0, 1 - c, c)

    def own_slot(j, o, i):
        return 2 * lax.axis_index("x") + lax.axis_index("y"), 0, 0

    block = (1, half, UP_SHARD)
    return _call(
        body, "dw_up", (N_CHIPS, 2, nt),
        [pl.BlockSpec((ts, half), h2_cols), pl.BlockSpec((ts, UP_SHARD), lambda j, o, i: (i, j))],
        [pl.BlockSpec(block, lambda j, o, i: (j, 0, 0)), pl.BlockSpec(block, own_slot)],
        [_sds((N_CHIPS, half, UP_SHARD), BF16)] * 2,
        [pltpu.VMEM((half, UP_SHARD), F32), pltpu.VMEM((half, UP_SHARD), F32), pltpu.VMEM((2, half, UP_SHARD), F32),
         pltpu.SemaphoreType.DMA((2, 2))],
        (h2, dup), tasks)


def _bwd_merge(dx1, mo, sp, sr, ypool, yrnn, g_post, w_o, ts, tasks=()):
    s = dx1.shape[0]

    def body(dx1_ref, mo_ref, sp_ref, sr_ref, yp_ref, yr_ref, g_ref, wo_ref,
             dgp_ref, dgr_ref, dyp_ref, dyr_ref, dwo_ref, dg_ref):
        @pl.when(pl.program_id(0) == 0)
        def _():
            dwo_ref[...] = jnp.zeros_like(dwo_ref)
            dg_ref[...] = jnp.zeros_like(dg_ref)

        _, xh, r = _rms_fwd(mo_ref[...], g_ref[...])
        dmo, dg = _rms_bwd(xh, r, g_ref[...], dx1_ref[...])
        dg_ref[...] += dg
        dmerged = _dot_nt(dmo, wo_ref[...])
        sp = sp_ref[...].astype(F32)
        sr = sr_ref[...].astype(F32)
        yp = yp_ref[...]
        yr = yr_ref[...].astype(F32)
        dwo_ref[...] += _dot_tn(sp * yp + sr * yr, dmo)
        dgp_ref[...] = (dmerged * yp * sp * (1.0 - sp)).astype(BF16)
        dgr_ref[...] = (dmerged * yr * sr * (1.0 - sr)).astype(BF16)
        dyp_ref[...] = (dmerged * sp).astype(BF16)
        dyr_ref[...] = (dmerged * sr).astype(BF16)

    row = _rows(ts, D_MODEL)
    return _call(
        body, "bwd_merge", (s // ts,),
        [row] * 6 + [_resident((1, D_MODEL)), _resident(w_o.shape)],
        [row] * 4 + [_acc(w_o.shape), _acc((1, D_MODEL))],
        [_sds((s, D_MODEL), BF16)] * 4 + [_sds(w_o.shape), _sds((1, D_MODEL))], [],
        (dx1, mo, sp, sr, ypool, yrnn, g_post, w_o), tasks)


def _bwd_rnn(dyr, urx, gl, gg, xc, r, ig, h, conv_w, wg, lam, w_rg_out, ts, tasks=()):
    s = urx.shape[0]
    nt = s // ts
    halo_blocks = ts // CONV_HALO

    def body(dyr_ref, urx_ref, gl_ref, gg_ref, xc_ref, r_ref, ig_ref, h_ref, hh_ref, cw_ref, wg_ref, lam_ref, wrg_ref,
             durx_ref, durg_ref, dwrg_ref, dwg_ref, dcw_ref, dcb_ref, dbg_ref, dlam_ref, mu_scr, carry_scr):
        i = pl.program_id(0)
        k = nt - 1 - i

        @pl.when(i == 0)
        def _():
            mu_scr[...] = jnp.zeros_like(mu_scr)
            carry_scr[...] = jnp.zeros_like(carry_scr)
            dwrg_ref[...] = jnp.zeros_like(dwrg_ref)
            dwg_ref[...] = jnp.zeros_like(dwg_ref)
            dcw_ref[...] = jnp.zeros_like(dcw_ref)
            dcb_ref[...] = jnp.zeros_like(dcb_ref)
            dbg_ref[...] = jnp.zeros_like(dbg_ref)
            dlam_ref[...] = jnp.zeros_like(dlam_ref)

        row = lax.broadcasted_iota(jnp.int32, (ts, 1), 0)
        first_row = (k * ts + row) == 0
        h = h_ref[...]
        dyr_v = dyr_ref[...]
        dhr = _dot_nt(dyr_v, wrg_ref[...])
        gl = gl_ref[...].astype(F32)
        dwrg_ref[...] += _dot_tn(h * gl, dyr_v)
        durg_ref[...] = (dhr * h * gg_ref[...].astype(F32)).astype(BF16)
        r_v = r_ref[...].astype(F32)
        ig_v = ig_ref[...].astype(F32)
        xc_v = xc_ref[...].astype(F32)
        lam_v = lam_ref[...]
        c8, a, m2, mult = _lru_coeffs(r_v, lam_v, first_row)
        b = jnp.where(row == ts - 1, 1.0, pltpu.roll(a, ts - 1, 0))
        lt = _scan_bwd(b, dhr * gl, mu_scr[0:1, :])
        mu_scr[0:1, :] = a[0:1, :] * lt[0:1, :]
        h_before = jnp.where(k > 0, hh_ref[CONV_HALO - 1:CONV_HALO, :], 0.0)
        hprev = jnp.where(row == 0, h_before, pltpu.roll(h, 1, 0))
        dmult = lt * ig_v * xc_v
        da = lt * hprev - jnp.where(first_row, 0.0, dmult * a * lax.rsqrt(m2))
        dla = da * a
        dlam_ref[...] += jnp.sum(dla * r_v, axis=0, keepdims=True)
        dlr = (dla * (-c8)) * r_v * (1.0 - r_v)
        dli = (lt * mult * xc_v) * ig_v * (1.0 - ig_v)
        dbg_ref[0:1, :] += jnp.sum(dlr, axis=0, keepdims=True)
        dbg_ref[1:2, :] += jnp.sum(dli, axis=0, keepdims=True)
        xcb = xc_v.astype(BF16)
        parts = []
        for q in range(GATE_BLOCKS):
            blk = slice(q * GATE_BLOCK, (q + 1) * GATE_BLOCK)
            dlr_q = dlr[:, blk].astype(BF16)
            dli_q = dli[:, blk].astype(BF16)
            parts.append(_dot_nt(dlr_q, wg_ref[0, q]) + _dot_nt(dli_q, wg_ref[1, q]))
            dwg_ref[0, q] += _dot_tn(xcb[:, blk], dlr_q)
            dwg_ref[1, q] += _dot_tn(xcb[:, blk], dli_q)
        dxc = lt * mult * ig_v + jnp.concatenate(parts, axis=1)
        extd = jnp.concatenate([dxc, carry_scr[...]], axis=0)
        carry_scr[...] = dxc[0:CONV_HALO, :]
        cw = cw_ref[...]
        urx_v = urx_ref[...].astype(F32)
        durx = cw[3:4] * dxc
        dcw_ref[3:4, :] += jnp.sum(urx_v * dxc, axis=0, keepdims=True)
        for j in (1, 2, 3):
            dj = _shift_up(extd, j, ts)
            durx = durx + cw[3 - j:4 - j] * dj
            dcw_ref[3 - j:4 - j, :] += jnp.sum(urx_v * dj, axis=0, keepdims=True)
        durx_ref[...] = durx.astype(BF16)
        dcb_ref[...] += jnp.sum(dxc, axis=0, keepdims=True)

        @pl.when(i == nt - 1)
        def _():
            dlam_ref[...] = dlam_ref[...] * (LRU_C * jax.nn.sigmoid(-lam_v))

    row_spec = _rows(ts, D_MODEL, nt)
    halo_spec = pl.BlockSpec((CONV_HALO, D_MODEL), lambda i: (jnp.maximum((nt - 1 - i) * halo_blocks - 1, 0), 0))
    vec = _acc((1, D_MODEL))
    return _call(
        body, "bwd_rnn", (nt,),
        [row_spec] * 8 + [halo_spec, _resident(conv_w.shape), _resident(wg.shape), _resident((1, D_MODEL)),
                          _resident(w_rg_out.shape)],
        [row_spec, row_spec, _acc(w_rg_out.shape), _acc(wg.shape), _acc(conv_w.shape), vec, _acc((2, D_MODEL)), vec],
        [_sds((s, D_MODEL), BF16), _sds((s, D_MODEL), BF16), _sds(w_rg_out.shape), _sds(wg.shape), _sds(conv_w.shape),
         _sds((1, D_MODEL)), _sds((2, D_MODEL)), _sds((1, D_MODEL))],
        [pltpu.VMEM((8, D_MODEL), F32), pltpu.VMEM((CONV_HALO, D_MODEL), F32)],
        (dyr, urx, gl, gg, xc, r, ig, h, h, conv_w, wg, lam, w_rg_out), tasks)


def _bwd_pool(dyp, d, pool_w, pool_scale, w_pool_out, ts, tasks=()):
    s = d.shape[0]
    nt = s // ts

    def body(dyp_ref, d_ref, pw_ref, ps_ref, wpo_ref, dzp_ref, dwpo_ref, dpw_ref, dps_ref, carry_scr):
        i = pl.program_id(0)
        k = nt - 1 - i

        @pl.when(i == 0)
        def _():
            carry_scr[...] = jnp.zeros_like(carry_scr)
            dwpo_ref[...] = jnp.zeros_like(dwpo_ref)
            dpw_ref[...] = jnp.zeros_like(dpw_ref)
            dps_ref[...] = jnp.zeros_like(dps_ref)

        dyp_v = dyp_ref[...]
        d_v = d_ref[...]
        ps = ps_ref[...]
        dypre = _dot_nt(dyp_v[:, 0:256], wpo_ref[0])
        for j in range(1, N_CHIPS):
            dypre = dypre + _dot_nt(dyp_v[:, j * 256:(j + 1) * 256], wpo_ref[j])
        y4 = jnp.concatenate([_dot(d_v[:, g * 128:(g + 1) * 128], pw_ref[g]) for g in range(POOL_GROUPS)], axis=1)
        ypre = (y4 * ps).astype(BF16)
        for j in range(N_CHIPS):
            dwpo_ref[j] += _dot_tn(ypre, dyp_v[:, j * 256:(j + 1) * 256])
        dps_ref[...] += jnp.sum(dypre * y4, axis=0, keepdims=True)
        dy4 = dypre * ps
        t = k * ts + lax.broadcasted_iota(jnp.int32, (ts, 1), 0)
        for g, w in enumerate(POOL_WINDOWS):
            lanes = slice(g * POOL_GROUP_DIM, (g + 1) * POOL_GROUP_DIM)
            dd = _dot_nt(dy4[:, lanes], pw_ref[g])
            dpw_ref[g] += _dot_tn(d_v[:, lanes], dy4[:, lanes])
            e = dd * (1.0 / jnp.minimum(t + 1, w).astype(F32))
            acc = jnp.concatenate([e, carry_scr[:, lanes]], axis=0)
            carry_scr[:, lanes] = e[0:POOL_HALO, :]
            n = ts + POOL_HALO
            sh = 1
            while sh < w:
                acc = acc + pltpu.roll(acc, n - sh, 0)
                sh *= 2
            dzp_ref[:, lanes] = (acc[:ts, :] - dd).astype(BF16)

    return _call(
        body, "bwd_pool", (nt,),
        [_rows(ts, D_MODEL, nt), _rows(ts, POOL_WIDTH, nt), _resident(pool_w.shape), _resident((1, POOL_WIDTH)),
         _resident(w_pool_out.shape)],
        [_rows(ts, POOL_WIDTH, nt), _acc(w_pool_out.shape), _acc(pool_w.shape), _acc((1, POOL_WIDTH))],
        [_sds((s, POOL_WIDTH), BF16), _sds(w_pool_out.shape), _sds(pool_w.shape), _sds((1, POOL_WIDTH))],
        [pltpu.VMEM((POOL_HALO, POOL_WIDTH), F32)],
        (dyp, d, pool_w, pool_scale, w_pool_out), tasks)


def _assemble_dz(dz_scr, dzp_ref, durx_ref, durg_ref, dgp_ref, dgr_ref):
    dz_scr[:, 0:512] = dzp_ref[...]
    dz_scr[:, 512:1536] = durx_ref[...]
    dz_scr[:, 1536:2560] = durg_ref[...]
    dz_scr[:, 2560:3584] = dgp_ref[...]
    dz_scr[:, 3584:4608] = dgr_ref[...]


def _dw_in(h1, dzp, durx, durg, dgp, dgr, ts, tasks=()):
    s = h1.shape[0]
    ts = min(2 * ts, s)
    nt = s // ts
    half = D_MODEL // 2

    def body(h1_ref, dzp_ref, durx_ref, durg_ref, dgp_ref, dgr_ref, send_ref, acc_ref, theirs_ref, mine_ref, got_ref, sems):
        o, i = pl.program_id(0), pl.program_id(1)

        @pl.when((o == 0) & (i == 0))
        def _():
            theirs_ref[...] = jnp.zeros_like(theirs_ref)
            mine_ref[...] = jnp.zeros_like(mine_ref)

        groups = (dzp_ref, durx_ref, durg_ref, dgp_ref, dgr_ref)
        for out_ref, which in ((theirs_ref, 0), (mine_ref, 1)):
            @pl.when(o == which)
            def _():
                for group, dz_ref in enumerate(groups):
                    for j, shard_cols, group_cols in _overlaps(group):
                        out_ref[j, :, shard_cols] += _dot_tn(h1_ref[...], dz_ref[:, group_cols])

        x, y, c, me, _ = _place()
        send = _remote(theirs_ref, got_ref, sems.at[0], sems.at[1], (x, y, 1 - c))
        pl.when((o == 1) & (i == 0))(send.start)

        @pl.when((o == 1) & (i == nt - 1))
        def _():
            send.wait()
            for k in range(N_CHIPS):
                part = (mine_ref[k] + got_ref[k]).astype(BF16)
                send_ref[k] = part

                @pl.when(me == k)
                def _():
                    acc_ref[k] = part

    def h1_cols(o, i):
        c = lax.axis_index("c")
        return i, jnp.where(o == 0, 1 - c, c)

    rows = lambda width: pl.BlockSpec((ts, width), lambda o, i: (i, 0))
    shape = (N_CHIPS, half, IN_SHARD)
    whole = pl.BlockSpec(shape, lambda o, i: (0, 0, 0))
    return _call(
        body, "dw_in", (2, nt), [pl.BlockSpec((ts, half), h1_cols), rows(POOL_WIDTH)] + [rows(D_MODEL)] * 4,
        [whole, whole], [_sds(shape, BF16)] * 2,
        [pltpu.VMEM(shape, F32)] * 3 + [pltpu.SemaphoreType.DMA((2,))], (h1, dzp, durx, durg, dgp, dgr), tasks)


def _bwd_in(dzp, durx, durg, dgp, dgr, x, dx1, g_pre, w_in, ts, tasks=()):
    s = x.shape[0]

    def body(dzp_ref, durx_ref, durg_ref, dgp_ref, dgr_ref, x_ref, dx1_ref, g_ref, win_ref, gx_ref, dg_ref, dz_scr):
        @pl.when(pl.program_id(0) == 0)
        def _():
            dg_ref[...] = jnp.zeros_like(dg_ref)

        _assemble_dz(dz_scr, dzp_ref, durx_ref, durg_ref, dgp_ref, dgr_ref)
        _, xh, r = _rms_fwd(x_ref[...], g_ref[...])
        dh1 = _dot_nt(dz_scr[:, 0:IN_SHARD], win_ref[0])
        for j in range(1, N_CHIPS):
            dh1 = dh1 + _dot_nt(dz_scr[:, j * IN_SHARD:(j + 1) * IN_SHARD], win_ref[j])
        dx, dg = _rms_bwd(xh, r, g_ref[...], dh1)
        dg_ref[...] += dg
        gx_ref[...] = dx1_ref[...] + dx

    row = _rows(ts, D_MODEL)
    return _call(
        body, "bwd_in", (s // ts,),
        [_rows(ts, POOL_WIDTH)] + [row] * 6 + [_resident((1, D_MODEL)), _resident(w_in.shape)],
        [row, _acc((1, D_MODEL))], [_sds((s, D_MODEL)), _sds((1, D_MODEL))],
        [pltpu.VMEM((ts, IN_TOTAL), BF16)], (dzp, durx, durg, dgp, dgr, x, dx1, g_pre, w_in), tasks)


def _place():
    x, y, c = lax.axis_index("x"), lax.axis_index("y"), lax.axis_index("c")
    others = [(1 - x, y), (x, 1 - y), (1 - x, 1 - y)]
    return x, y, c, 2 * x + y, others


def _remote(src, dst, send_sem, recv_sem, to):
    return pltpu.make_async_remote_copy(src_ref=src, dst_ref=dst, send_sem=send_sem, recv_sem=recv_sem,
                                        device_id=to, device_id_type=MESH)


def _own_slots(ws, dtypes, name, tasks=()):
    n = len(ws)
    hbm = pl.BlockSpec(memory_space=pltpu.HBM)

    def body(*refs):
        srcs, outs, f32_bufs, cast_bufs, sems = refs[:n], refs[n:2 * n], refs[2 * n:3 * n], refs[3 * n:4 * n], refs[4 * n]
        me = _place()[3]
        loads = [pltpu.make_async_copy(srcs[k], f32_bufs[k], sems.at[k, 0]) for k in range(n)]
        stores = [pltpu.make_async_copy(cast_bufs[k], outs[k].at[me], sems.at[k, 1]) for k in range(n)]
        for cp in loads:
            cp.start()
        for k in range(n):
            loads[k].wait()
            cast_bufs[k][...] = f32_bufs[k][...].astype(dtypes[k])
            stores[k].start()
        for cp in stores:
            cp.wait()

    return _call(
        body, name, (), [hbm] * n, [hbm] * n, [_sds((N_CHIPS,) + w.shape, dt) for w, dt in zip(ws, dtypes)],
        [pltpu.VMEM(w.shape, F32) for w in ws] + [pltpu.VMEM(w.shape, dt) for w, dt in zip(ws, dtypes)]
        + [pltpu.SemaphoreType.DMA((n, 2))],
        [pltpu.with_memory_space_constraint(w, pltpu.HBM) for w in ws], tasks)


def _run(tasks, name):
    if isinstance(tasks, _Task):
        return _call(None, name, (), [], [], [], [], (), (tasks,))[1][0]
    return _call(None, name, (), [], [], [], [], (), tuple(tasks))[1]


def _gather_task(bufs, relay_steps=(0, 0)):
    n = len(bufs)
    NBR_X, NBR_Y, QUARTER_VIA_Y, QUARTER_VIA_X, SIB_X, SIB_Y, SIB_DIAG = range(7)

    def parts(out):
        x, y, c, me, _ = _place()
        ah = out.shape[1] // 2
        q = ah // 2 if (ah // 2) % 16 == 0 else ah
        return c * ah, ah, q

    def copy(out, w, k, chip, row0, rows, to, sems):
        slot = out.at[chip, pl.ds(row0, rows)]
        return _remote(slot, slot, sems[0].at[w, k], sems[1].at[w, k], to)

    def plan(out, w, sems):
        x, y, c, me, _ = _place()
        row0, ah, q = parts(out)
        xn, yn, dg = 2 * (1 - x) + y, 2 * x + (1 - y), 2 * (1 - x) + (1 - y)
        to_x, to_y, sib = (1 - x, y, c), (x, 1 - y, c), (x, y, 1 - c)
        other = (1 - c) * ah
        cp = functools.partial(copy, out, w, sems=sems)
        sends = {NBR_X: cp(NBR_X, me, row0, ah, to_x), NBR_Y: cp(NBR_Y, me, row0, ah, to_y),
                 QUARTER_VIA_Y: cp(QUARTER_VIA_Y, xn, row0, q, to_y), SIB_X: cp(SIB_X, xn, row0, ah, sib),
                 SIB_Y: cp(SIB_Y, yn, row0, ah, sib), SIB_DIAG: cp(SIB_DIAG, dg, row0, ah, sib)}
        lands = {NBR_X: cp(NBR_X, xn, row0, ah, to_x), NBR_Y: cp(NBR_Y, yn, row0, ah, to_y),
                 QUARTER_VIA_Y: cp(QUARTER_VIA_Y, dg, row0, q, to_y), SIB_X: cp(SIB_X, xn, other, ah, sib),
                 SIB_Y: cp(SIB_Y, yn, other, ah, sib), SIB_DIAG: cp(SIB_DIAG, dg, other, ah, sib)}
        if q < ah:
            sends[QUARTER_VIA_X] = cp(QUARTER_VIA_X, yn, row0 + q, ah - q, to_x)
            lands[QUARTER_VIA_X] = cp(QUARTER_VIA_X, dg, row0 + q, ah - q, to_x)
        return sends, lands

    def start(ins, outs, sems):
        for w, out in enumerate(outs):
            sends, _ = plan(out, w, sems)
            sends[NBR_X].start()
            sends[NBR_Y].start()

    def pass_neighbours(ins, outs, sems):
        for w, out in enumerate(outs):
            sends, lands = plan(out, w, sems)
            lands[NBR_X].wait_recv()
            sends[QUARTER_VIA_Y].start()
            sends[SIB_X].start()
            lands[NBR_Y].wait_recv()
            if QUARTER_VIA_X in sends:
                sends[QUARTER_VIA_X].start()
            sends[SIB_Y].start()

    def pass_diagonal(ins, outs, sems):
        for w, out in enumerate(outs):
            sends, lands = plan(out, w, sems)
            lands[QUARTER_VIA_Y].wait_recv()
            if QUARTER_VIA_X in lands:
                lands[QUARTER_VIA_X].wait_recv()
            sends[SIB_DIAG].start()

    def finish(ins, outs, sems):
        for w, out in enumerate(outs):
            sends, lands = plan(out, w, sems)
            for k in (SIB_X, SIB_Y, SIB_DIAG):
                lands[k].wait_recv()
        for w, out in enumerate(outs):
            sends, _ = plan(out, w, sems)
            for cp in sends.values():
                cp.wait_send()

    return _Task(bufs, [_sds(b.shape, b.dtype) for b in bufs], {i: i for i in range(n)},
                 [pltpu.SemaphoreType.DMA((n, 7)), pltpu.SemaphoreType.DMA((n, 7))], start, finish,
                 [(pass_neighbours, relay_steps[0]), (pass_diagonal, relay_steps[1])])


def _halves_task(grads):
    n = len(grads)

    def copy(src, out, w, sems):
        x, y, c, _, _ = _place()
        ah = out.shape[1]
        return _remote(src.at[:, pl.ds((1 - c) * ah, ah)], out, sems[0].at[w], sems[1].at[w], (x, y, 1 - c))

    def start(ins, outs, sems):
        for w, (src, out) in enumerate(zip(ins, outs)):
            copy(src, out, w, sems).start()

    def finish(ins, outs, sems):
        for w, (src, out) in enumerate(zip(ins, outs)):
            copy(src, out, w, sems).wait()

    return _Task(grads, [_sds((g.shape[0], g.shape[1] // 2, g.shape[2]), g.dtype) for g in grads], {},
                 [pltpu.SemaphoreType.DMA((n,)), pltpu.SemaphoreType.DMA((n,))], start, finish)


def _exchange_task(sends, accs):
    n = len(accs)
    given = [s for s in sends if s is not None]

    def copies(ins, outs, sems):
        send_refs = iter(ins[:len(given)])
        srcs = [next(send_refs) if s is not None else None for s in sends]
        x, y, c, me, others = _place()
        for w, out in enumerate(outs):
            for j, (ox, oy) in enumerate(others):
                src = out.at[me] if srcs[w] is None else srcs[w].at[2 * ox + oy]
                yield _remote(src, out.at[me], sems[0].at[w, j], sems[1].at[w, j], (ox, oy, c))

    def start(ins, outs, sems):
        for cp in copies(ins, outs, sems):
            cp.start()

    def finish(ins, outs, sems):
        x, y, c, _, others = _place()
        for w, out in enumerate(outs):
            for j, (ox, oy) in enumerate(others):
                slot = out.at[2 * ox + oy]
                _remote(slot, slot, sems[0].at[w, j], sems[1].at[w, j], (ox, oy, c)).wait_recv()
        for cp in copies(ins, outs, sems):
            cp.wait_send()

    return _Task(given + list(accs), [_sds(a.shape, a.dtype) for a in accs], {len(given) + i: i for i in range(n)},
                 [pltpu.SemaphoreType.DMA((n, 3)), pltpu.SemaphoreType.DMA((n, 3))], start, finish)


def _swap_task(arrays):
    n = len(arrays)

    def copy(src, out, w, sems):
        x, y, c, _, _ = _place()
        return _remote(src, out, sems[0].at[w], sems[1].at[w], (x, y, 1 - c))

    def start(ins, outs, sems):
        for w, (src, out) in enumerate(zip(ins, outs)):
            copy(src, out, w, sems).start()

    def finish(ins, outs, sems):
        for w, (src, out) in enumerate(zip(ins, outs)):
            copy(src, out, w, sems).wait()

    return _Task(arrays, [_sds(a.shape, a.dtype) for a in arrays], {},
                 [pltpu.SemaphoreType.DMA((n,)), pltpu.SemaphoreType.DMA((n,))], start, finish)


def _all_devices_task(arrays):
    n = len(arrays)
    flips = [(dx, dy, dc) for dx in (0, 1) for dy in (0, 1) for dc in (0, 1)][1:]

    def peers():
        x, y, c, _, _ = _place()
        flip = lambda v, d: 1 - v if d else v
        return 4 * x + 2 * y + c, [(flip(x, dx), flip(y, dy), flip(c, dc)) for dx, dy, dc in flips]

    def start(ins, outs, sems):
        me, others = peers()
        for w, (src, out) in enumerate(zip(ins, outs)):
            pltpu.make_async_copy(src, out.at[me], sems[2].at[w]).start()
            for k, peer in enumerate(others):
                _remote(src, out.at[me], sems[0].at[w, k], sems[1].at[w, k], peer).start()

    def finish(ins, outs, sems):
        me, others = peers()
        for w, (src, out) in enumerate(zip(ins, outs)):
            for k, (px, py, pc) in enumerate(others):
                slot = out.at[4 * px + 2 * py + pc]
                _remote(slot, slot, sems[0].at[w, k], sems[1].at[w, k], (px, py, pc)).wait_recv()
            for k, peer in enumerate(others):
                _remote(src, out.at[me], sems[0].at[w, k], sems[1].at[w, k], peer).wait_send()
            pltpu.make_async_copy(src, out.at[me], sems[2].at[w]).wait()

    return _Task(arrays, [_sds((8,) + a.shape, a.dtype) for a in arrays], {},
                 [pltpu.SemaphoreType.DMA((n, 7)), pltpu.SemaphoreType.DMA((n, 7)), pltpu.SemaphoreType.DMA((n,))],
                 start, finish)


def _share_task(shares):
    n = len(shares)

    def copy(out, w, sems, slot):
        x, y, c, _, _ = _place()
        return _remote(out.at[slot], out.at[slot], sems[0].at[w], sems[1].at[w], (x, y, 1 - c))

    def start(ins, outs, sems):
        c = _place()[2]
        for w, out in enumerate(outs):
            copy(out, w, sems, c).start()

    def finish(ins, outs, sems):
        c = _place()[2]
        for w, out in enumerate(outs):
            copy(out, w, sems, 1 - c).wait_recv()
        for w, out in enumerate(outs):
            copy(out, w, sems, c).wait_send()

    return _Task(shares, [_sds(s.shape, s.dtype) for s in shares], {i: i for i in range(n)},
                 [pltpu.SemaphoreType.DMA((n,)), pltpu.SemaphoreType.DMA((n,))], start, finish)


TILE_BYTES = 2 * 1024 * 1024
PARTIAL_TILE_BYTES = 1024 * 1024


def _in_hbm(t):
    return pltpu.with_memory_space_constraint(t, pltpu.HBM)


def _row_tile(rows, cols, limit=TILE_BYTES):
    best = 8
    for tr in range(8, rows + 1, 8):
        if rows % tr == 0 and tr * cols * 4 <= limit:
            best = tr
    assert rows % best == 0, (rows, cols)
    return best


def _chip_partial(g, got, place, wire_dtype):
    ns, ah, b = got.shape
    sharded = ns == N_CHIPS
    tr = _row_tile(ah, b, PARTIAL_TILE_BYTES)
    nb = ah // tr
    first = 0 if g.shape[1] == ah else nb

    def body(place_ref, *refs):
        g_refs, got_refs, outs = refs[:ns], refs[ns:2 * ns], refs[2 * ns:]
        parts = [g_refs[k][0] + got_refs[k][0] for k in range(ns)]
        own = parts[0]
        if sharded:
            for k in range(ns):
                outs[0][k] = parts[k].astype(wire_dtype)
                if k:
                    own = jnp.where(place_ref[0] == k, parts[k], own)
        outs[-1][0] = own.astype(wire_dtype)

    blk = (1, tr, b)
    in_specs = ([pl.BlockSpec(blk, lambda i, s, k=k: (k, s[1] * first + i, 0)) for k in range(ns)]
                + [pl.BlockSpec(blk, lambda i, s, k=k: (k, i, 0)) for k in range(ns)])
    acc_spec = pl.BlockSpec(blk, lambda i, s: (s[0], i, 0))
    acc_shape = _sds((N_CHIPS, ah, b), wire_dtype)
    out = pl.pallas_call(
        body, name="grad_chip_partial",
        grid_spec=pltpu.PrefetchScalarGridSpec(
            num_scalar_prefetch=1, grid=(nb,), in_specs=in_specs,
            out_specs=[pl.BlockSpec((ns, tr, b), lambda i, s: (0, i, 0)), acc_spec] if sharded else [acc_spec]),
        out_shape=[acc_shape, acc_shape] if sharded else [acc_shape],
        compiler_params=pltpu.CompilerParams(dimension_semantics=("arbitrary",), vmem_limit_bytes=VMEM_LIMIT),
    )(place, *([g] * ns), *([got] * ns))
    return (out[0], out[1]) if sharded else (None, out[0])


def _chip_sum(acc, place):
    _, ah, b = acc.shape
    tr = _row_tile(ah, b)

    def body(place_ref, p_ref, out_ref):
        total = p_ref[0].astype(F32) + p_ref[1].astype(F32)
        total = total + p_ref[2].astype(F32)
        out_ref[0] = total + p_ref[3].astype(F32)

    return pl.pallas_call(
        body, name="grad_chip_sum",
        grid_spec=pltpu.PrefetchScalarGridSpec(
            num_scalar_prefetch=1, grid=(ah // tr,),
            in_specs=[pl.BlockSpec((N_CHIPS, tr, b), lambda i, s: (0, i, 0))],
            out_specs=pl.BlockSpec((1, tr, b), lambda i, s: (s[1], i, 0))),
        out_shape=_sds((2, ah, b)),
        compiler_params=pltpu.CompilerParams(dimension_semantics=("arbitrary",)),
    )(place, _in_hbm(acc))


def _adam_math(w, g, m, v):
    nm = ADAM_B1 * m + (1.0 - ADAM_B1) * g
    nv = ADAM_B2 * v + (1.0 - ADAM_B2) * (g * g)
    m_hat = nm / (1.0 - ADAM_B1 ** ADAM_STEP)
    v_hat = nv / (1.0 - ADAM_B2 ** ADAM_STEP)
    return -ADAM_LR * (m_hat / (jnp.sqrt(v_hat) + ADAM_EPS) + ADAM_WD * w), nm, nv


def _adamw(w, g, m, v):
    a, b = w.shape
    tr = _row_tile(a, b)

    def body(w_ref, g_ref, m_ref, v_ref, g_out, d_ref, nm_ref, nv_ref):
        g_out[...] = g_ref[...]
        d_ref[...], nm_ref[...], nv_ref[...] = _adam_math(w_ref[...], g_ref[...], m_ref[...], v_ref[...])

    blk = pl.BlockSpec((tr, b), lambda i: (i, 0))
    return pl.pallas_call(
        body, name="adamw", grid=(a // tr,),
        in_specs=[blk] * 4, out_specs=[blk] * 4, out_shape=[_sds((a, b))] * 4,
        compiler_params=pltpu.CompilerParams(dimension_semantics=("arbitrary",)),
    )(w, g, m, v)


def _adamw_sum(w, m, v, acc, got, place):
    a, b = w.shape
    ah = a // 2
    tr = _row_tile(ah, b)
    nb = ah // tr

    def body(place_ref, w_ref, m_ref, v_ref, acc_ref, got_ref, g_out, d_ref, nm_ref, nv_ref):
        mine = (pl.program_id(0) // nb) == place_ref[1]
        part = lambda k: jnp.where(mine, acc_ref[k], got_ref[k]).astype(F32)
        g = part(0) + part(1)
        g = g + part(2)
        g = g + part(3)
        g_out[...] = g
        d_ref[...], nm_ref[...], nv_ref[...] = _adam_math(w_ref[...], g, m_ref[...], v_ref[...])

    blk = pl.BlockSpec((tr, b), lambda i, s: (i, 0))
    mine_spec = pl.BlockSpec((N_CHIPS, tr, b), lambda i, s: (0, jnp.where(i // nb == s[1], i % nb, 0), 0))
    got_spec = pl.BlockSpec((N_CHIPS, tr, b), lambda i, s: (0, jnp.where(i // nb == s[1], 0, i % nb), 0))
    return pl.pallas_call(
        body, name="adamw_sum",
        grid_spec=pltpu.PrefetchScalarGridSpec(
            num_scalar_prefetch=1, grid=(a // tr,), in_specs=[blk] * 3 + [mine_spec, got_spec], out_specs=[blk] * 4),
        out_shape=[_sds((a, b))] * 4,
        compiler_params=pltpu.CompilerParams(dimension_semantics=("arbitrary",), vmem_limit_bytes=VMEM_LIMIT),
    )(place, w, m, v, _in_hbm(acc), _in_hbm(got))


def _adamw_pieces(g, pieces, name):
    n = len(pieces)

    def body(g_ref, *refs):
        def grad(rows, cols):
            if len(g_ref.shape) == 2:
                return g_ref[rows, cols]
            total = g_ref[0, rows, cols]
            for k in range(1, g_ref.shape[0]):
                total = total + g_ref[k, rows, cols]
            return total

        ins, outs = refs[:3 * n], refs[3 * n:]
        for i, piece in enumerate(pieces):
            w_ref, m_ref, v_ref = ins[3 * i:3 * i + 3]
            o_g, o_d, o_m, o_v = outs[4 * i:4 * i + 4]
            if len(piece) == 5:
                g_v = grad(piece[3], piece[4])
                o_g[...] = g_v
                o_d[...], o_m[...], o_v[...] = _adam_math(w_ref[...], g_v, m_ref[...], v_ref[...])
            else:
                for r in range(w_ref.shape[1] // SMALL_COLS):
                    lanes = slice(r * SMALL_COLS, (r + 1) * SMALL_COLS)
                    g_v = grad(slice(piece[3] + r, piece[3] + r + 1), slice(None))
                    o_g[:, lanes] = g_v
                    o_d[:, lanes], o_m[:, lanes], o_v[:, lanes] = _adam_math(w_ref[:, lanes], g_v, m_ref[:, lanes],
                                                                            v_ref[:, lanes])

    operands = [t for piece in pieces for t in piece[:3]]
    out = pl.pallas_call(
        body, name=name,
        out_shape=[_sds(piece[0].shape) for piece in pieces for _ in range(4)],
    )(g, *operands)
    return [tuple(out[4 * i:4 * i + 4]) for i in range(n)]


TINY_ROWS, TINY_COLS = 16, 768
SMALL_COLS = 128
SMALL_ROWS = 624


def _pack_tiny(conv_w, b_gates, fcw):
    ns = conv_w.shape[0]
    pad = lambda t: jnp.pad(t, ((0, 0), (0, 0), (0, TINY_COLS - t.shape[2])))
    z = lambda rows: jnp.zeros((ns, rows, TINY_COLS), F32)
    return jnp.concatenate([pad(conv_w), pad(b_gates), z(2), fcw, z(TINY_ROWS - 11)], axis=1)


def _unpack_tiny(t):
    return t[:, 0:4, 0:256], t[:, 4:6, 0:256], t[:, 8:11, :]


def _cols_to_shards(t, n):
    return t.reshape(t.shape[0], N_CHIPS, n).transpose(1, 0, 2)


def _shards_to_cols(t):
    return t.transpose(1, 0, 2).reshape(t.shape[1], -1)


_VECTORS = ("g_mix_post", "conv_b", "lru_lambda", "g_ffn_pre", "g_ffn_post", "g_ple_gate", "g_ple_post", "pool_scale",
            "ffn_conv_b")
_VECTOR_LEN = {"pool_scale": POOL_WIDTH, "ffn_conv_b": D_FF}
POOL_W_ROWS = POOL_GROUPS * POOL_GROUP_DIM


def _vector_rows():
    rows, row = {}, POOL_W_ROWS
    for k in _VECTORS:
        rows[k] = row
        row += max(8, _VECTOR_LEN.get(k, D_MODEL) // SMALL_COLS)
    return rows, row


def _pack_small(grads, loss):
    tiles = lambda t: jnp.pad(t, ((0, -t.shape[0] % 8), (0, 0)))
    parts = [grads["pool_w"].reshape(POOL_W_ROWS, SMALL_COLS)] + [tiles(grads[k].reshape(-1, SMALL_COLS)) for k in _VECTORS]
    parts.append(tiles(loss))
    used = sum(t.shape[0] for t in parts)
    return jnp.concatenate(parts + [jnp.zeros((SMALL_ROWS - used, SMALL_COLS), F32)], axis=0)


def _gates_block_diag(w):
    w4 = w.reshape(2, GATE_BLOCKS, 4, RNN_HEAD_DIM, RNN_HEAD_DIM)
    eye = jnp.eye(4, dtype=w.dtype)
    return jnp.einsum("gqhij,hk->gqhikj", w4, eye).reshape(2, GATE_BLOCKS, GATE_BLOCK, GATE_BLOCK)


def _gates_from_block_diag(dw):
    d6 = dw.reshape(2, GATE_BLOCKS, 4, RNN_HEAD_DIM, 4, RNN_HEAD_DIM)
    blocks = [d6[:, :, hh, :, hh, :] for hh in range(4)]
    return jnp.stack(blocks, axis=2).reshape(2, RNN_HEADS, RNN_HEAD_DIM, RNN_HEAD_DIM)


ROW_TILE = 256
DW_TILES = 8

_SHARDED = ("w_in", "w_pool_out", "w_rg_out", "w_o", "w_up", "w_down", "w_ple_gate", "w_ple_proj")
_WEIGHTS = ("g_mix_pre", "g_mix_post", "w_in", "pool_w", "pool_scale", "w_pool_out", "conv_w", "conv_b", "w_rg_gates",
            "b_rg_gates", "lru_lambda", "w_rg_out", "w_o", "g_ffn_pre", "g_ffn_post", "w_up", "ffn_conv_w", "ffn_conv_b",
            "w_down", "g_ple_gate", "w_ple_gate", "w_ple_proj", "g_ple_post")


def _wire_dtype(g):
    return BF16 if g.shape[1] >= 64 and g.shape[2] > SMALL_COLS else F32


def _partials(grads, got, place):
    parts = [_chip_partial(g, r, place, _wire_dtype(g)) for g, r in zip(grads, got)]
    return [send for send, _ in parts], [acc for _, acc in parts]


def _whole(both):
    return [b.reshape(2 * b.shape[1], b.shape[2]) for b in both]


def _step(x, p, tgt, rep, place, ts):
    vec = lambda k: rep[k].reshape(1, -1)
    tall = min(2 * ts, x.shape[0])
    pool_w = rep["pool_w"].astype(BF16)
    wg = _gates_block_diag(rep["w_rg_gates"]).astype(BF16)
    sq = lambda t: t.reshape(D_MODEL, D_MODEL)
    by4 = lambda t: t.reshape(N_CHIPS, -1, D_MODEL)

    first, ride1, ride2 = (("w_in", "w_pool_out", "tiny"), ("w_rg_out", "w_o", "w_down"),
                           ("w_up", "w_ple_gate", "w_ple_proj"))
    later = ride1 + ride2
    tiny = _pack_tiny(rep["conv_w"][None], rep["b_rg_gates"][None], rep["ffn_conv_w"][None])[0]
    own_first, _ = _own_slots([rep["w_in"], rep["w_pool_out"], tiny], [BF16, BF16, F32], "own_slots_first")
    own_later, (got,) = _own_slots([rep[k] for k in later], [BF16] * len(later), "own_slots_gather_first",
                                   [_gather_task(own_first)])
    own = dict(zip(later, own_later))
    full = dict(zip(first, got))
    conv_w, b_gates, fcw = [_shards_to_cols(t) for t in _unpack_tiny(full["tiny"])]

    (urx, urg, gp, gr, d, ypool, h1), (got,) = _fwd_in_pool(
        x, vec("g_mix_pre"), full["w_in"], pool_w, vec("pool_scale"), full["w_pool_out"], ts,
        [_gather_task([own[k] for k in ride1], relay_steps=(6, 2))])
    full.update(zip(ride1, got))
    w_rg_out, w_o, w_down = sq(full["w_rg_out"]), sq(full["w_o"]), full["w_down"].reshape(D_FF, D_MODEL)
    (xc, r, ig, h, yrnn, mo, x1, glr, ggr, sp, sr), (got,) = _fwd_rnn_merge(
        urx, urg, gp, gr, ypool, x, conv_w, vec("conv_b"), wg, b_gates, vec("lru_lambda"), w_rg_out, w_o,
        vec("g_mix_post"), ts, [_gather_task([own[k] for k in ride2], relay_steps=(7, 3))])
    full.update(zip(ride2, got))
    (up, gl, gg, h2, dn, x2), _ = _fwd_ffn(x1, vec("g_ffn_pre"), full["w_up"], fcw, vec("ffn_conv_b"), w_down,
                                           vec("g_ffn_post"), ts)
    dx2, loss, d_w_gate, d_w_proj, d_g_ple_gate, d_g_ple_post = _ple_loss(
        x2, p, tgt, vec("g_ple_gate"), sq(full["w_ple_gate"]), full["w_ple_proj"], vec("g_ple_post"), tall)
    dup, d_w_down, d_fcw, d_fcb, d_g_ffn_post = _bwd_ffn_down(dx2, dn, up, gl, gg, fcw, w_down, vec("g_ffn_post"), ts)

    names1, grads1 = ("w_ple_gate", "w_ple_proj", "w_down"), [by4(d_w_gate), d_w_proj, by4(d_w_down)]
    (dx1, d_g_ffn_pre), (got1,) = _bwd_ffn_up(dup, x1, dx2, vec("g_ffn_pre"), full["w_up"], tall, [_halves_task(grads1)])
    (send_w_up, acc_w_up), (accs1,) = _dw_up(h2, dup, ts, [_exchange_task(*_partials(grads1, got1, place))])
    (dgp, dgr, dyp, dyr, d_w_o, d_g_mix_post), (theirs1,) = _bwd_merge(
        dx1, mo, sp, sr, ypool, yrnn, vec("g_mix_post"), w_o, tall, [_swap_task(accs1)])
    (durx, durg, d_w_rg_out, d_wg, d_conv_w, d_conv_b, d_b_gates, d_lam), (accs2,) = _bwd_rnn(
        dyr, urx, glr, ggr, xc, r, ig, h, conv_w, wg, vec("lru_lambda"), w_rg_out, ts,
        [_exchange_task([send_w_up], [acc_w_up])])
    names3 = ("w_o", "w_rg_out", "tiny", "w_rg_gates")
    grads3 = [by4(d_w_o), by4(d_w_rg_out),
              _pack_tiny(_cols_to_shards(d_conv_w, 256), _cols_to_shards(d_b_gates, 256), _cols_to_shards(d_fcw, 768)),
              _gates_from_block_diag(d_wg).reshape(1, 2 * RNN_HEADS * RNN_HEAD_DIM, RNN_HEAD_DIM)]
    (dzp, d_w_pool_out, d_pool_w, d_pool_scale), (got3, theirs2) = _bwd_pool(
        dyp, d, pool_w, vec("pool_scale"), full["w_pool_out"], tall, [_halves_task(grads3), _swap_task(accs2)])
    replicated = {"g_mix_post": d_g_mix_post, "conv_b": d_conv_b, "lru_lambda": d_lam, "g_ffn_pre": d_g_ffn_pre,
                  "g_ffn_post": d_g_ffn_post, "g_ple_gate": d_g_ple_gate, "g_ple_post": d_g_ple_post,
                  "pool_scale": d_pool_scale, "ffn_conv_b": d_fcb, "pool_w": d_pool_w}
    names4 = ("w_in", "w_pool_out", "small")
    small4 = [d_w_pool_out, _pack_small(replicated, loss)[None]]
    (send_w_in, acc_w_in), (accs3, got_small4) = _dw_in(
        h1, dzp, durx, durg, dgp, dgr, ts, [_exchange_task(*_partials(grads3, got3, place)), _halves_task(small4)])
    sends4, accs4 = _partials(small4, got_small4, place)
    (grad_x, d_g_mix_pre), (accs4, theirs3, both3) = _bwd_in(
        dzp, durx, durg, dgp, dgr, x, dx1, vec("g_mix_pre"), full["w_in"], tall,
        [_exchange_task([send_w_in] + sends4, [acc_w_in] + accs4), _swap_task(accs3[:2]),
         _share_task([_chip_sum(acc, place) for acc in accs3[2:]])])
    theirs4, both4, (g_mix_pre_parts,) = _run(
        [_swap_task(accs4[:2]), _share_task([_chip_sum(acc, place) for acc in accs4[2:]]),
         _all_devices_task([d_g_mix_pre.reshape(SUBLANES, SMALL_COLS)])], "grad_sibling_share")
    mine = accs1 + accs2 + accs3[:2] + accs4[:2]
    partials = dict(zip(names1 + ("w_up",) + names3[:2] + names4[:2], zip(mine, theirs1 + theirs2 + theirs3 + theirs4)))
    return grad_x, partials, dict(zip(names3[2:] + names4[2:], _whole(both3) + _whole(both4))), g_mix_pre_parts


def kernel(x, p, g_mix_pre, g_mix_post, w_in, pool_w, pool_scale, w_pool_out, conv_w, conv_b, w_rg_gates, b_rg_gates, lru_lambda, w_rg_out, w_o, g_ffn_pre, g_ffn_post, w_up, ffn_conv_w, ffn_conv_b, w_down, g_ple_gate, w_ple_gate, w_ple_proj, g_ple_post, loss_target, m_g_mix_pre, m_g_mix_post, m_w_in, m_pool_w, m_pool_scale, m_w_pool_out, m_conv_w, m_conv_b, m_w_rg_gates, m_b_rg_gates, m_lru_lambda, m_w_rg_out, m_w_o, m_g_ffn_pre, m_g_ffn_post, m_w_up, m_ffn_conv_w, m_ffn_conv_b, m_w_down, m_g_ple_gate, m_w_ple_gate, m_w_ple_proj, m_g_ple_post, v_g_mix_pre, v_g_mix_post, v_w_in, v_pool_w, v_pool_scale, v_w_pool_out, v_conv_w, v_conv_b, v_w_rg_gates, v_b_rg_gates, v_lru_lambda, v_w_rg_out, v_w_o, v_g_ffn_pre, v_g_ffn_post, v_w_up, v_ffn_conv_w, v_ffn_conv_b, v_w_down, v_g_ple_gate, v_w_ple_gate, v_w_ple_proj, v_g_ple_post):
    args = dict(locals())
    w = {k: args[k][0] for k in _WEIGHTS}
    m = {k: args["m_" + k][0] for k in _WEIGHTS}
    v = {k: args["v_" + k][0] for k in _WEIGHTS}
    place = jnp.stack([2 * lax.axis_index("x") + lax.axis_index("y"), lax.axis_index("c")]).astype(jnp.int32)
    grad_x, partials, reduced, g_mix_pre_parts = _step(x[0], p[0, 0], loss_target[0], w, place, ROW_TILE)

    gates_2d = (2 * RNN_HEADS * RNN_HEAD_DIM, RNN_HEAD_DIM)
    as2d = lambda k, shape: tuple(t[k].reshape(shape) for t in (w, m, v))
    done = {k: tuple(_adamw_sum(w[k], m[k], v[k], *partials[k], place)) for k in _SHARDED}
    gates_w, gates_m, gates_v = as2d("w_rg_gates", gates_2d)
    done["w_rg_gates"] = tuple(_adamw(gates_w, reduced["w_rg_gates"], gates_m, gates_v))
    tiny_names = ("conv_w", "b_rg_gates", "ffn_conv_w")
    tiny_at = ((slice(0, 4), slice(0, 256)), (slice(4, 6), slice(0, 256)), (slice(8, 11), slice(None)))
    done.update(zip(tiny_names, _adamw_pieces(
        reduced["tiny"], [(w[k], m[k], v[k]) + at for k, at in zip(tiny_names, tiny_at)], "adamw_tiny")))
    vector_rows, loss_row = _vector_rows()
    pieces = [as2d("pool_w", (POOL_W_ROWS, SMALL_COLS)) + (slice(0, POOL_W_ROWS), slice(None))]
    pieces += [as2d(k, (1, -1)) + (vector_rows[k],) for k in _VECTORS]
    done.update(zip(("pool_w",) + _VECTORS, _adamw_pieces(reduced["small"], pieces, "adamw_small")))
    done["g_mix_pre"] = _adamw_pieces(g_mix_pre_parts, [as2d("g_mix_pre", (1, -1)) + (0,)], "adamw_g_mix_pre")[0]

    result = [reduced["small"][loss_row, 0], grad_x[None]]
    for kind in range(4):
        result += [done[k][kind].reshape(args[k].shape) for k in _WEIGHTS]
    return tuple(result)
```

```python
import functools

import jax
import jax.numpy as jnp
from jax import lax
from jax.experimental import pallas as pl
from jax.experimental.pallas import tpu as pltpu

F32 = jnp.float32
BF16 = jnp.bfloat16

D_MODEL = 1024
POOL_WINDOWS = (2, 4, 8, 16)
POOL_GROUPS = 4
POOL_WIDTH = 512
POOL_GROUP_DIM = 128
RNN_HEADS = 16
RNN_HEAD_DIM = 64
GATE_BLOCK = 256
GATE_BLOCKS = D_MODEL // GATE_BLOCK
LRU_C = 8.0
D_FF = 3072
PLE_DIM = 256
RMS_EPS = 1e-6
IN_TOTAL = 4608
N_CHIPS = 4
IN_SHARD = IN_TOTAL // N_CHIPS
UP_SHARD = 2 * D_FF // N_CHIPS
Z_GROUPS = ((0, 512), (512, 1536), (1536, 2560), (2560, 3584), (3584, 4608))
POOL_HALO = 16
CONV_HALO = 8

ADAM_LR = 0.001
ADAM_B1 = 0.9
ADAM_B2 = 0.999
ADAM_EPS = 1e-08
ADAM_WD = 0.01
ADAM_STEP = 10

VMEM_LIMIT = 56 * 1024 * 1024
MESH = pl.DeviceIdType.MESH

_GELU_C = 0.7978845608028654
_GELU_A = 0.044715


def _dot(a, b):
    return jnp.dot(a.astype(BF16), b.astype(BF16), preferred_element_type=F32)


def _dot_nt(a, b):
    return lax.dot_general(a.astype(BF16), b.astype(BF16), (((1,), (1,)), ((), ())), preferred_element_type=F32)


def _dot_tn(a, b):
    return lax.dot_general(a.astype(BF16), b.astype(BF16), (((0,), (0,)), ((), ())), preferred_element_type=F32)


def _overlaps(group):
    a, b = Z_GROUPS[group]
    found = []
    for j in range(N_CHIPS):
        lo, hi = max(a, j * IN_SHARD), min(b, (j + 1) * IN_SHARD)
        if lo < hi:
            found.append((j, slice(lo - j * IN_SHARD, hi - j * IN_SHARD), slice(lo - a, hi - a)))
    return found


def _rms_fwd(x, g):
    r = lax.rsqrt(jnp.mean(x * x, axis=-1, keepdims=True) + RMS_EPS)
    xh = x * r
    return xh * g, xh, r


def _rms_bwd(xh, r, g, dy):
    dxh = dy * g
    dg = jnp.sum(dy * xh, axis=0, keepdims=True)
    dx = r * (dxh - xh * jnp.mean(dxh * xh, axis=-1, keepdims=True))
    return dx, dg


def _sigmoid(x):
    return 0.5 * jnp.tanh(0.5 * x) + 0.5


def _gelu(x):
    t = jnp.tanh(_GELU_C * (x + _GELU_A * x * x * x))
    return 0.5 * x * (1.0 + t), t


def _gelu_grad(x, t):
    return 0.5 * (1.0 + t) + 0.5 * x * (1.0 - t * t) * _GELU_C * (1.0 + 3.0 * _GELU_A * x * x)


def _softplus_neg(lam):
    nl = -lam
    return jnp.maximum(nl, 0.0) + jnp.log(1.0 + jnp.exp(-jnp.abs(nl)))


def _lru_coeffs(r, lam, first_row):
    c8 = LRU_C * _softplus_neg(lam)
    la = -(c8 * r)
    a = jnp.exp(la)
    m2 = jnp.tanh(-la) * (1.0 + a * a)
    mult = jnp.where(first_row, 1.0, jnp.sqrt(m2))
    return c8, a, m2, mult


SUBLANES = 8


def _scan_fwd(a, u, carry):
    n = a.shape[0]
    sub = lax.broadcasted_iota(jnp.int32, (n, 1), 0) % SUBLANES
    acc_a, acc_h = a, u
    for s in (1, 2, 4):
        m = sub >= s
        h_s = jnp.where(m, pltpu.roll(acc_h, s, 0), 0.0)
        a_s = jnp.where(m, pltpu.roll(acc_a, s, 0), 1.0)
        acc_h = acc_a * h_s + acc_h
        acc_a = acc_a * a_s
    out = []
    for g in range(n // SUBLANES):
        rows = slice(g * SUBLANES, (g + 1) * SUBLANES)
        out.append(acc_h[rows] + acc_a[rows] * carry)
        carry = out[-1][SUBLANES - 1:SUBLANES]
    return jnp.concatenate(out, axis=0)


def _scan_bwd(b, g, carry):
    n = b.shape[0]
    sub = lax.broadcasted_iota(jnp.int32, (n, 1), 0) % SUBLANES
    acc_b, acc_l = b, g
    for s in (1, 2, 4):
        m = sub < SUBLANES - s
        l_s = jnp.where(m, pltpu.roll(acc_l, n - s, 0), 0.0)
        b_s = jnp.where(m, pltpu.roll(acc_b, n - s, 0), 1.0)
        acc_l = acc_b * l_s + acc_l
        acc_b = acc_b * b_s
    out = [None] * (n // SUBLANES)
    for g in reversed(range(n // SUBLANES)):
        rows = slice(g * SUBLANES, (g + 1) * SUBLANES)
        out[g] = acc_l[rows] + acc_b[rows] * carry
        carry = out[g][0:1]
    return jnp.concatenate(out, axis=0)


def _shift_down(ext, k, halo):
    return pltpu.roll(ext, k, 0)[halo:] if k else ext[halo:]


def _shift_up(ext, k, ts):
    return pltpu.roll(ext, ext.shape[0] - k, 0)[:ts] if k else ext[:ts]


def _rows(ts, width, nt=None, col=0):
    if nt is None:
        return pl.BlockSpec((ts, width), lambda i: (i, col))
    return pl.BlockSpec((ts, width), lambda i: (nt - 1 - i, col))


def _resident(shape):
    zeros = (0,) * len(shape)
    return pl.BlockSpec(shape, lambda i: zeros, pipeline_mode=pl.Buffered(1))


def _acc(shape):
    zeros = (0,) * len(shape)
    return pl.BlockSpec(shape, lambda i: zeros)


def _params():
    return pltpu.CompilerParams(dimension_semantics=("arbitrary",), vmem_limit_bytes=VMEM_LIMIT)


def _sds(shape, dtype=F32):
    return jax.ShapeDtypeStruct(shape, dtype)


class _Task:
    def __init__(self, ins, out_shapes, aliases, sems, start, finish, relays=()):
        self.ins, self.out_shapes, self.aliases, self.sems = list(ins), list(out_shapes), dict(aliases), list(sems)
        self.start, self.relays, self.finish = start, list(relays), finish


def _call(body, name, grid, in_specs, out_specs, out_shape, scratch_shapes, args, tasks=()):
    n_in, n_out, n_scr = len(in_specs), len(out_specs), len(scratch_shapes)
    t_in = [len(t.ins) for t in tasks]
    t_out = [len(t.out_shapes) for t in tasks]
    t_sem = [len(t.sems) for t in tasks]
    steps = 1
    for g in grid:
        steps *= g

    def take(refs, pos, counts):
        groups = []
        for c in counts:
            groups.append(refs[pos:pos + c])
            pos += c
        return groups, pos

    def wrapped(*refs):
        (cin,), pos = take(refs, 0, [n_in])
        tin, pos = take(refs, pos, t_in)
        (cout,), pos = take(refs, pos, [n_out])
        tout, pos = take(refs, pos, t_out)
        (cscr,), pos = take(refs, pos, [n_scr])
        tsem, pos = take(refs, pos, t_sem)
        if not grid:
            for t, a, b, c in zip(tasks, tin, tout, tsem):
                t.start(a, b, c)
            if body is not None:
                body(*cin, *cout, *cscr)
            for t, a, b, c in zip(tasks, tin, tout, tsem):
                for relay, _ in t.relays:
                    relay(a, b, c)
            for t, a, b, c in zip(tasks, tin, tout, tsem):
                t.finish(a, b, c)
            return
        step = pl.program_id(0)
        for axis in range(1, len(grid)):
            step = step * grid[axis] + pl.program_id(axis)
        if tasks:
            @pl.when(step == 0)
            def _():
                for t, a, b, c in zip(tasks, tin, tout, tsem):
                    t.start(a, b, c)

        body(*cin, *cout, *cscr)
        for t, a, b, c in zip(tasks, tin, tout, tsem):
            for relay, before in t.relays:
                pl.when(step == max(steps - 1 - before, 0))(functools.partial(relay, a, b, c))

        if tasks:
            @pl.when(step == steps - 1)
            def _():
                for t, a, b, c in zip(tasks, tin, tout, tsem):
                    t.finish(a, b, c)

    aliases, in_pos, out_pos = {}, n_in, n_out
    for t, ni, no in zip(tasks, t_in, t_out):
        aliases.update({in_pos + a: out_pos + b for a, b in t.aliases.items()})
        in_pos, out_pos = in_pos + ni, out_pos + no
    any_spec = pl.BlockSpec(memory_space=pltpu.HBM)
    kwargs = dict(grid=grid, compiler_params=pltpu.CompilerParams(
        dimension_semantics=("arbitrary",) * len(grid), vmem_limit_bytes=VMEM_LIMIT)) if grid else dict(
        compiler_params=pltpu.CompilerParams(vmem_limit_bytes=VMEM_LIMIT))
    out = pl.pallas_call(
        wrapped, name=name,
        in_specs=list(in_specs) + [any_spec] * sum(t_in),
        out_specs=list(out_specs) + [any_spec] * sum(t_out),
        out_shape=list(out_shape) + [s for t in tasks for s in t.out_shapes],
        scratch_shapes=list(scratch_shapes) + [s for t in tasks for s in t.sems],
        input_output_aliases=aliases, **kwargs,
    )(*args, *[pltpu.with_memory_space_constraint(a, pltpu.HBM) for t in tasks for a in t.ins])
    task_outs, pos = take(list(out), n_out, t_out)
    return list(out[:n_out]), task_outs


def _fwd_in_pool(x, g_pre, w_in, pool_w, pool_scale, w_pool_out, ts, tasks=()):
    s = x.shape[0]

    def body(x_ref, g_ref, win_ref, pw_ref, ps_ref, wpo_ref,
             urx_ref, urg_ref, gp_ref, gr_ref, d_ref, yp_ref, h1_ref, halo_scr):
        i = pl.program_id(0)

        @pl.when(i == 0)
        def _():
            halo_scr[...] = jnp.zeros_like(halo_scr)

        h1, _, _ = _rms_fwd(x_ref[...], g_ref[...])
        h1 = h1.astype(BF16)
        h1_ref[...] = h1
        u = jnp.dot(h1, win_ref[0, :, 0:POOL_WIDTH], preferred_element_type=F32)
        for group, out_ref in zip(range(1, len(Z_GROUPS)), (urx_ref, urg_ref, gp_ref, gr_ref)):
            for j, shard_cols, group_cols in _overlaps(group):
                out_ref[:, group_cols] = jnp.dot(h1, win_ref[j, :, shard_cols],
                                                 preferred_element_type=F32).astype(BF16)
        ext = jnp.concatenate([halo_scr[...], u], axis=0)
        halo_scr[...] = u[ts - POOL_HALO:, :]
        t = i * ts + lax.broadcasted_iota(jnp.int32, (ts, 1), 0)
        y4 = []
        for g, w in enumerate(POOL_WINDOWS):
            lanes = slice(g * POOL_GROUP_DIM, (g + 1) * POOL_GROUP_DIM)
            acc = ext[:, lanes]
            sh = 1
            while sh < w:
                acc = acc + pltpu.roll(acc, sh, 0)
                sh *= 2
            inv = 1.0 / jnp.minimum(t + 1, w).astype(F32)
            dg = acc[POOL_HALO:, :] * inv - u[:, lanes]
            d_ref[:, lanes] = dg.astype(BF16)
            y4.append(_dot(dg, pw_ref[g]))
        ypre = jnp.concatenate(y4, axis=1) * ps_ref[...]
        ypre = ypre.astype(BF16)
        for j in range(N_CHIPS):
            yp_ref[:, j * 256:(j + 1) * 256] = jnp.dot(ypre, wpo_ref[j], preferred_element_type=F32)

    return _call(
        body, "fwd_in_pool", (s // ts,),
        [_rows(ts, D_MODEL), _resident((1, D_MODEL)), _resident(w_in.shape), _resident(pool_w.shape),
         _resident((1, POOL_WIDTH)), _resident(w_pool_out.shape)],
        [_rows(ts, D_MODEL)] * 4 + [_rows(ts, POOL_WIDTH), _rows(ts, D_MODEL), _rows(ts, D_MODEL)],
        [_sds((s, D_MODEL), BF16)] * 4 + [_sds((s, POOL_WIDTH), BF16), _sds((s, D_MODEL)), _sds((s, D_MODEL), BF16)],
        [pltpu.VMEM((POOL_HALO, POOL_WIDTH), F32)],
        (x, g_pre, w_in, pool_w, pool_scale, w_pool_out), tasks)


def _fwd_rnn_merge(urx, urg, gp, gr, ypool, x, conv_w, conv_b, wg, bg, lam, w_rg_out, w_o, g_post, ts, tasks=()):
    s = x.shape[0]

    def body(urx_ref, urg_ref, gp_ref, gr_ref, yp_ref, x_ref, cw_ref, cb_ref, wg_ref, bg_ref, lam_ref, wrg_ref, wo_ref,
             gpost_ref, xc_ref, r_ref, ig_ref, h_ref, yr_ref, mo_ref, x1_ref, gl_ref, gg_ref, sp_ref, sr_ref,
             halo_scr, carry_scr):
        i = pl.program_id(0)

        @pl.when(i == 0)
        def _():
            halo_scr[...] = jnp.zeros_like(halo_scr)
            carry_scr[...] = jnp.zeros_like(carry_scr)

        urx_v = urx_ref[...].astype(F32)
        ext = jnp.concatenate([halo_scr[...], urx_v], axis=0)
        halo_scr[...] = urx_v[ts - CONV_HALO:, :]
        cw = cw_ref[...]
        xc = (cb_ref[...] + cw[3:4] * urx_v + cw[2:3] * _shift_down(ext, 1, CONV_HALO)
              + cw[1:2] * _shift_down(ext, 2, CONV_HALO) + cw[0:1] * _shift_down(ext, 3, CONV_HALO))
        xc_ref[...] = xc.astype(BF16)
        xcb = xc.astype(BF16)
        lin = []
        for gate in range(2):
            parts = [jnp.dot(xcb[:, q * GATE_BLOCK:(q + 1) * GATE_BLOCK], wg_ref[gate, q], preferred_element_type=F32)
                     for q in range(GATE_BLOCKS)]
            lin.append(jnp.concatenate(parts, axis=1) + bg_ref[gate:gate + 1, :])
        r = _sigmoid(lin[0])
        ig = _sigmoid(lin[1])
        r_ref[...] = r.astype(BF16)
        ig_ref[...] = ig.astype(BF16)
        first_row = (i * ts + lax.broadcasted_iota(jnp.int32, (ts, 1), 0)) == 0
        _, a, _, mult = _lru_coeffs(r, lam_ref[...], first_row)
        h = _scan_fwd(a, mult * ig * xc, carry_scr[0:1, :])
        carry_scr[0:1, :] = h[ts - 1:ts, :]
        h_ref[...] = h
        urg_v = urg_ref[...].astype(F32)
        gl, t = _gelu(urg_v)
        gl_ref[...] = gl.astype(BF16)
        gg_ref[...] = _gelu_grad(urg_v, t).astype(BF16)
        yr = _dot(h * gl, wrg_ref[...])
        yr_ref[...] = yr.astype(BF16)
        sp = _sigmoid(gp_ref[...].astype(F32))
        sr = _sigmoid(gr_ref[...].astype(F32))
        sp_ref[...] = sp.astype(BF16)
        sr_ref[...] = sr.astype(BF16)
        merged = sp * yp_ref[...] + sr * yr
        mo = _dot(merged, wo_ref[...])
        mo_ref[...] = mo
        y, _, _ = _rms_fwd(mo, gpost_ref[...])
        x1_ref[...] = x_ref[...] + y

    row = _rows(ts, D_MODEL)
    return _call(
        body, "fwd_rnn_merge", (s // ts,),
        [row] * 6 + [_resident(conv_w.shape), _resident((1, D_MODEL)), _resident(wg.shape), _resident(bg.shape),
                     _resident((1, D_MODEL)), _resident(w_rg_out.shape), _resident(w_o.shape), _resident((1, D_MODEL))],
        [row] * 11, [_sds((s, D_MODEL), dt) for dt in (BF16, BF16, BF16, F32, BF16, F32, F32, BF16, BF16, BF16, BF16)],
        [pltpu.VMEM((CONV_HALO, D_MODEL), F32), pltpu.VMEM((8, D_MODEL), F32)],
        (urx, urg, gp, gr, ypool, x, conv_w, conv_b, wg, bg, lam, w_rg_out, w_o, g_post), tasks)


def _fwd_ffn(x1, g_pre, w_up, fcw, fcb, w_down, g_post, ts, tasks=()):
    s = x1.shape[0]

    def body(x1_ref, g_ref, wup_ref, fcw_ref, fcb_ref, wd_ref, gpost_ref,
             up_ref, gl_ref, gg_ref, h2_ref, dn_ref, x2_ref, up_scr, halo_scr):
        i = pl.program_id(0)

        @pl.when(i == 0)
        def _():
            halo_scr[...] = jnp.zeros_like(halo_scr)

        x1_v = x1_ref[...]
        h2, _, _ = _rms_fwd(x1_v, g_ref[...])
        h2 = h2.astype(BF16)
        h2_ref[...] = h2
        for j in range(N_CHIPS):
            up_scr[:, j * UP_SHARD:(j + 1) * UP_SHARD] = jnp.dot(h2, wup_ref[j], preferred_element_type=F32)
        up_ref[...] = up_scr[...].astype(BF16)
        ug = up_scr[:, 0:D_FF]
        ext = jnp.concatenate([halo_scr[...], ug], axis=0)
        halo_scr[...] = ug[ts - CONV_HALO:, :]
        w = fcw_ref[...]
        gh = (fcb_ref[...] + w[2:3] * ug + w[1:2] * _shift_down(ext, 1, CONV_HALO)
              + w[0:1] * _shift_down(ext, 2, CONV_HALO))
        gl, t = _gelu(gh)
        gl_ref[...] = gl.astype(BF16)
        gg_ref[...] = _gelu_grad(gh, t).astype(BF16)
        dn = _dot(gl * up_scr[:, D_FF:], wd_ref[...])
        dn_ref[...] = dn
        y, _, _ = _rms_fwd(dn, gpost_ref[...])
        x2_ref[...] = x1_v + y

    row = _rows(ts, D_MODEL)
    return _call(
        body, "fwd_ffn", (s // ts,),
        [row, _resident((1, D_MODEL)), _resident(w_up.shape), _resident(fcw.shape), _resident((1, D_FF)),
         _resident(w_down.shape), _resident((1, D_MODEL))],
        [_rows(ts, 2 * D_FF), _rows(ts, D_FF), _rows(ts, D_FF), row, row, row],
        [_sds((s, 2 * D_FF), BF16), _sds((s, D_FF), BF16), _sds((s, D_FF), BF16), _sds((s, D_MODEL), BF16),
         _sds((s, D_MODEL)), _sds((s, D_MODEL))],
        [pltpu.VMEM((ts, 2 * D_FF), F32), pltpu.VMEM((CONV_HALO, D_FF), F32)],
        (x1, g_pre, w_up, fcw, fcb, w_down, g_post), tasks)


def _ple_loss(x2, p, tgt, g_gate, w_gate, w_proj, g_post, ts):
    s = x2.shape[0]

    def body(x2_ref, p_ref, t_ref, gg_ref, wg_ref, wp_ref, gp_ref, dx2_ref, loss_ref, dwg_ref, dwp_ref, dgg_ref, dgp_ref):
        @pl.when(pl.program_id(0) == 0)
        def _():
            loss_ref[...] = jnp.zeros_like(loss_ref)
            dwg_ref[...] = jnp.zeros_like(dwg_ref)
            dwp_ref[...] = jnp.zeros_like(dwp_ref)
            dgg_ref[...] = jnp.zeros_like(dgg_ref)
            dgp_ref[...] = jnp.zeros_like(dgp_ref)

        x2_v = x2_ref[...]
        n3, xh3, r3 = _rms_fwd(x2_v, gg_ref[...])
        pg = _sigmoid(_dot(n3, wg_ref[...]))
        pb = p_ref[...].astype(BF16)
        q = jnp.concatenate([jnp.dot(pb, wp_ref[j], preferred_element_type=F32) for j in range(N_CHIPS)], axis=1)
        ple, qh, rq = _rms_fwd(q, gp_ref[...])
        e = x2_v + pg * ple - t_ref[...]
        loss_ref[...] += 0.5 * jnp.sum(jnp.mean(e * e, axis=-1, keepdims=True), axis=0, keepdims=True)
        dy = e * (1.0 / D_MODEL)
        dpgl = dy * ple * pg * (1.0 - pg)
        dwg_ref[...] += _dot_tn(n3, dpgl)
        dx3, dgg = _rms_bwd(xh3, r3, gg_ref[...], _dot_nt(dpgl, wg_ref[...]))
        dgg_ref[...] += dgg
        dq, dgp = _rms_bwd(qh, rq, gp_ref[...], dy * pg)
        dgp_ref[...] += dgp
        for j in range(N_CHIPS):
            dwp_ref[j] += _dot_tn(pb, dq[:, j * 256:(j + 1) * 256])
        dx2_ref[...] = dy + dx3

    row = _rows(ts, D_MODEL)
    vec = _acc((1, D_MODEL))
    return pl.pallas_call(
        body, name="ple_loss", grid=(s // ts,),
        in_specs=[row, _rows(ts, PLE_DIM), row, _resident((1, D_MODEL)), _resident(w_gate.shape), _resident(w_proj.shape),
                  _resident((1, D_MODEL))],
        out_specs=[row, _acc((1, 128)), _acc(w_gate.shape), _acc(w_proj.shape), vec, vec],
        out_shape=[_sds((s, D_MODEL)), _sds((1, 128)), _sds(w_gate.shape), _sds(w_proj.shape), _sds((1, D_MODEL)),
                   _sds((1, D_MODEL))],
        compiler_params=_params(),
    )(x2, p, tgt, g_gate, w_gate, w_proj, g_post)


def _bwd_ffn_down(dx2, dn, up, gl, gg, fcw, w_down, g_post, ts):
    s = dx2.shape[0]
    nt = s // ts

    def body(dx2_ref, dn_ref, up_ref, gl_ref, gg_ref, fcw_ref, wd_ref, gpost_ref,
             dup_ref, dwd_ref, dfcw_ref, dfcb_ref, dgp_ref, carry_scr):
        i = pl.program_id(0)

        @pl.when(i == 0)
        def _():
            carry_scr[...] = jnp.zeros_like(carry_scr)
            dwd_ref[...] = jnp.zeros_like(dwd_ref)
            dfcw_ref[...] = jnp.zeros_like(dfcw_ref)
            dfcb_ref[...] = jnp.zeros_like(dfcb_ref)
            dgp_ref[...] = jnp.zeros_like(dgp_ref)

        _, xh, r = _rms_fwd(dn_ref[...], gpost_ref[...])
        ddn, dgp = _rms_bwd(xh, r, gpost_ref[...], dx2_ref[...])
        dgp_ref[...] += dgp
        dhid = _dot_nt(ddn, wd_ref[...])
        ug = up_ref[:, 0:D_FF].astype(F32)
        uv = up_ref[:, D_FF:].astype(F32)
        gl = gl_ref[...].astype(F32)
        w = fcw_ref[...]
        dwd_ref[...] += _dot_tn(gl * uv, ddn)
        dgh = dhid * uv * gg_ref[...].astype(F32)
        dup_ref[:, D_FF:] = (dhid * gl).astype(BF16)
        extd = jnp.concatenate([dgh, carry_scr[...]], axis=0)
        carry_scr[...] = dgh[0:CONV_HALO, :]
        d1 = _shift_up(extd, 1, ts)
        d2 = _shift_up(extd, 2, ts)
        dup_ref[:, 0:D_FF] = (w[2:3] * dgh + w[1:2] * d1 + w[0:1] * d2).astype(BF16)
        dfcw_ref[2:3, :] += jnp.sum(ug * dgh, axis=0, keepdims=True)
        dfcw_ref[1:2, :] += jnp.sum(ug * d1, axis=0, keepdims=True)
        dfcw_ref[0:1, :] += jnp.sum(ug * d2, axis=0, keepdims=True)
        dfcb_ref[...] += jnp.sum(dgh, axis=0, keepdims=True)

    row = _rows(ts, D_MODEL, nt)
    wide = _rows(ts, D_FF, nt)
    return pl.pallas_call(
        body, name="bwd_ffn_down", grid=(nt,),
        in_specs=[row, row, _rows(ts, 2 * D_FF, nt), wide, wide, _resident(fcw.shape), _resident(w_down.shape),
                  _resident((1, D_MODEL))],
        out_specs=[_rows(ts, 2 * D_FF, nt), _acc(w_down.shape), _acc(fcw.shape), _acc((1, D_FF)), _acc((1, D_MODEL))],
        out_shape=[_sds((s, 2 * D_FF), BF16), _sds(w_down.shape), _sds(fcw.shape), _sds((1, D_FF)), _sds((1, D_MODEL))],
        scratch_shapes=[pltpu.VMEM((CONV_HALO, D_FF), F32)],
        compiler_params=_params(),
    )(dx2, dn, up, gl, gg, fcw, w_down, g_post)


def _bwd_ffn_up(dup, x1, dx2, g_pre, w_up, ts, tasks=()):
    s = x1.shape[0]

    def body(dup_ref, x1_ref, dx2_ref, g_ref, wup_ref, dx1_ref, dg_ref):
        @pl.when(pl.program_id(0) == 0)
        def _():
            dg_ref[...] = jnp.zeros_like(dg_ref)

        _, xh, r = _rms_fwd(x1_ref[...], g_ref[...])
        dh2 = _dot_nt(dup_ref[:, 0:UP_SHARD], wup_ref[0])
        for j in range(1, N_CHIPS):
            dh2 = dh2 + _dot_nt(dup_ref[:, j * UP_SHARD:(j + 1) * UP_SHARD], wup_ref[j])
        dx, dg = _rms_bwd(xh, r, g_ref[...], dh2)
        dg_ref[...] += dg
        dx1_ref[...] = dx2_ref[...] + dx

    row = _rows(ts, D_MODEL)
    return _call(
        body, "bwd_ffn_up", (s // ts,),
        [_rows(ts, 2 * D_FF), row, row, _resident((1, D_MODEL)), _resident(w_up.shape)],
        [row, _acc((1, D_MODEL))], [_sds((s, D_MODEL)), _sds((1, D_MODEL))], [],
        (dup, x1, dx2, g_pre, w_up), tasks)


def _dw_up(h2, dup, ts, tasks=()):
    s = h2.shape[0]
    ts = min(DW_TILES * ts, s)
    nt = s // ts
    half = D_MODEL // 2

    def body(h2_ref, dup_ref, send_ref, acc_ref, theirs_scr, mine_scr, got_scr, sems):
        j, o, i = pl.program_id(0), pl.program_id(1), pl.program_id(2)
        x, y, c, me, _ = _place()
        prod = _dot_tn(h2_ref[...], dup_ref[...])
        for scr, which in ((theirs_scr, 0), (mine_scr, 1)):
            @pl.when((o == which) & (i == 0))
            def _():
                scr[...] = prod

            @pl.when((o == which) & (i > 0))
            def _():
                scr[...] += prod

        slot = j % 2
        push = _remote(theirs_scr, got_scr.at[slot], sems.at[0, slot], sems.at[1, slot], (x, y, 1 - c))
        pl.when((o == 1) & (i == 0))(push.start)

        @pl.when((o == 1) & (i == nt - 1))
        def _():
            push.wait()
            part = (mine_scr[...] + got_scr[slot]).astype(BF16)
            send_ref[0] = part

            @pl.when(me == j)
            def _():
                acc_ref[0] = part

    def h2_cols(j, o, i):
        c = lax.axis_index("c")
        return i, jnp.where(o == 0, 1 - c, c)

    def own_slot(j, o, i):
        return 2 * lax.axis_index("x") + lax.axis_index("y"), 0, 0

    block = (1, half, UP_SHARD)
    return _call(
        body, "dw_up", (N_CHIPS, 2, nt),
        [pl.BlockSpec((ts, half), h2_cols), pl.BlockSpec((ts, UP_SHARD), lambda j, o, i: (i, j))],
        [pl.BlockSpec(block, lambda j, o, i: (j, 0, 0)), pl.BlockSpec(block, own_slot)],
        [_sds((N_CHIPS, half, UP_SHARD), BF16)] * 2,
        [pltpu.VMEM((half, UP_SHARD), F32), pltpu.VMEM((half, UP_SHARD), F32), pltpu.VMEM((2, half, UP_SHARD), F32),
         pltpu.SemaphoreType.DMA((2, 2))],
        (h2, dup), tasks)


def _bwd_merge(dx1, mo, sp, sr, ypool, yrnn, g_post, w_o, ts, tasks=()):
    s = dx1.shape[0]

    def body(dx1_ref, mo_ref, sp_ref, sr_ref, yp_ref, yr_ref, g_ref, wo_ref,
             dgp_ref, dgr_ref, dyp_ref, dyr_ref, dwo_ref, dg_ref):
        @pl.when(pl.program_id(0) == 0)
        def _():
            dwo_ref[...] = jnp.zeros_like(dwo_ref)
            dg_ref[...] = jnp.zeros_like(dg_ref)

        _, xh, r = _rms_fwd(mo_ref[...], g_ref[...])
        dmo, dg = _rms_bwd(xh, r, g_ref[...], dx1_ref[...])
        dg_ref[...] += dg
        dmerged = _dot_nt(dmo, wo_ref[...])
        sp = sp_ref[...].astype(F32)
        sr = sr_ref[...].astype(F32)
        yp = yp_ref[...]
        yr = yr_ref[...].astype(F32)
        dwo_ref[...] += _dot_tn(sp * yp + sr * yr, dmo)
        dgp_ref[...] = (dmerged * yp * sp * (1.0 - sp)).astype(BF16)
        dgr_ref[...] = (dmerged * yr * sr * (1.0 - sr)).astype(BF16)
        dyp_ref[...] = (dmerged * sp).astype(BF16)
        dyr_ref[...] = (dmerged * sr).astype(BF16)

    row = _rows(ts, D_MODEL)
    return _call(
        body, "bwd_merge", (s // ts,),
        [row] * 6 + [_resident((1, D_MODEL)), _resident(w_o.shape)],
        [row] * 4 + [_acc(w_o.shape), _acc((1, D_MODEL))],
        [_sds((s, D_MODEL), BF16)] * 4 + [_sds(w_o.shape), _sds((1, D_MODEL))], [],
        (dx1, mo, sp, sr, ypool, yrnn, g_post, w_o), tasks)


def _bwd_rnn(dyr, urx, gl, gg, xc, r, ig, h, conv_w, wg, lam, w_rg_out, ts, tasks=()):
    s = urx.shape[0]
    nt = s // ts
    halo_blocks = ts // CONV_HALO

    def body(dyr_ref, urx_ref, gl_ref, gg_ref, xc_ref, r_ref, ig_ref, h_ref, hh_ref, cw_ref, wg_ref, lam_ref, wrg_ref,
             durx_ref, durg_ref, dwrg_ref, dwg_ref, dcw_ref, dcb_ref, dbg_ref, dlam_ref, mu_scr, carry_scr):
        i = pl.program_id(0)
        k = nt - 1 - i

        @pl.when(i == 0)
        def _():
            mu_scr[...] = jnp.zeros_like(mu_scr)
            carry_scr[...] = jnp.zeros_like(carry_scr)
            dwrg_ref[...] = jnp.zeros_like(dwrg_ref)
            dwg_ref[...] = jnp.zeros_like(dwg_ref)
            dcw_ref[...] = jnp.zeros_like(dcw_ref)
            dcb_ref[...] = jnp.zeros_like(dcb_ref)
            dbg_ref[...] = jnp.zeros_like(dbg_ref)
            dlam_ref[...] = jnp.zeros_like(dlam_ref)

        row = lax.broadcasted_iota(jnp.int32, (ts, 1), 0)
        first_row = (k * ts + row) == 0
        h = h_ref[...]
        dyr_v = dyr_ref[...]
        dhr = _dot_nt(dyr_v, wrg_ref[...])
        gl = gl_ref[...].astype(F32)
        dwrg_ref[...] += _dot_tn(h * gl, dyr_v)
        durg_ref[...] = (dhr * h * gg_ref[...].astype(F32)).astype(BF16)
        r_v = r_ref[...].astype(F32)
        ig_v = ig_ref[...].astype(F32)
        xc_v = xc_ref[...].astype(F32)
        lam_v = lam_ref[...]
        c8, a, m2, mult = _lru_coeffs(r_v, lam_v, first_row)
        b = jnp.where(row == ts - 1, 1.0, pltpu.roll(a, ts - 1, 0))
        lt = _scan_bwd(b, dhr * gl, mu_scr[0:1, :])
        mu_scr[0:1, :] = a[0:1, :] * lt[0:1, :]
        h_before = jnp.where(k > 0, hh_ref[CONV_HALO - 1:CONV_HALO, :], 0.0)
        hprev = jnp.where(row == 0, h_before, pltpu.roll(h, 1, 0))
        dmult = lt * ig_v * xc_v
        da = lt * hprev - jnp.where(first_row, 0.0, dmult * a * lax.rsqrt(m2))
        dla = da * a
        dlam_ref[...] += jnp.sum(dla * r_v, axis=0, keepdims=True)
        dlr = (dla * (-c8)) * r_v * (1.0 - r_v)
        dli = (lt * mult * xc_v) * ig_v * (1.0 - ig_v)
        dbg_ref[0:1, :] += jnp.sum(dlr, axis=0, keepdims=True)
        dbg_ref[1:2, :] += jnp.sum(dli, axis=0, keepdims=True)
        xcb = xc_v.astype(BF16)
        parts = []
        for q in range(GATE_BLOCKS):
            blk = slice(q * GATE_BLOCK, (q + 1) * GATE_BLOCK)
            dlr_q = dlr[:, blk].astype(BF16)
            dli_q = dli[:, blk].astype(BF16)
            parts.append(_dot_nt(dlr_q, wg_ref[0, q]) + _dot_nt(dli_q, wg_ref[1, q]))
            dwg_ref[0, q] += _dot_tn(xcb[:, blk], dlr_q)
            dwg_ref[1, q] += _dot_tn(xcb[:, blk], dli_q)
        dxc = lt * mult * ig_v + jnp.concatenate(parts, axis=1)
        extd = jnp.concatenate([dxc, carry_scr[...]], axis=0)
        carry_scr[...] = dxc[0:CONV_HALO, :]
        cw = cw_ref[...]
        urx_v = urx_ref[...].astype(F32)
        durx = cw[3:4] * dxc
        dcw_ref[3:4, :] += jnp.sum(urx_v * dxc, axis=0, keepdims=True)
        for j in (1, 2, 3):
            dj = _shift_up(extd, j, ts)
            durx = durx + cw[3 - j:4 - j] * dj
            dcw_ref[3 - j:4 - j, :] += jnp.sum(urx_v * dj, axis=0, keepdims=True)
        durx_ref[...] = durx.astype(BF16)
        dcb_ref[...] += jnp.sum(dxc, axis=0, keepdims=True)

        @pl.when(i == nt - 1)
        def _():
            dlam_ref[...] = dlam_ref[...] * (LRU_C * jax.nn.sigmoid(-lam_v))

    row_spec = _rows(ts, D_MODEL, nt)
    halo_spec = pl.BlockSpec((CONV_HALO, D_MODEL), lambda i: (jnp.maximum((nt - 1 - i) * halo_blocks - 1, 0), 0))
    vec = _acc((1, D_MODEL))
    return _call(
        body, "bwd_rnn", (nt,),
        [row_spec] * 8 + [halo_spec, _resident(conv_w.shape), _resident(wg.shape), _resident((1, D_MODEL)),
                          _resident(w_rg_out.shape)],
        [row_spec, row_spec, _acc(w_rg_out.shape), _acc(wg.shape), _acc(conv_w.shape), vec, _acc((2, D_MODEL)), vec],
        [_sds((s, D_MODEL), BF16), _sds((s, D_MODEL), BF16), _sds(w_rg_out.shape), _sds(wg.shape), _sds(conv_w.shape),
         _sds((1, D_MODEL)), _sds((2, D_MODEL)), _sds((1, D_MODEL))],
        [pltpu.VMEM((8, D_MODEL), F32), pltpu.VMEM((CONV_HALO, D_MODEL), F32)],
        (dyr, urx, gl, gg, xc, r, ig, h, h, conv_w, wg, lam, w_rg_out), tasks)


def _bwd_pool(dyp, d, pool_w, pool_scale, w_pool_out, ts, tasks=()):
    s = d.shape[0]
    nt = s // ts

    def body(dyp_ref, d_ref, pw_ref, ps_ref, wpo_ref, dzp_ref, dwpo_ref, dpw_ref, dps_ref, carry_scr):
        i = pl.program_id(0)
        k = nt - 1 - i

        @pl.when(i == 0)
        def _():
            carry_scr[...] = jnp.zeros_like(carry_scr)
            dwpo_ref[...] = jnp.zeros_like(dwpo_ref)
            dpw_ref[...] = jnp.zeros_like(dpw_ref)
            dps_ref[...] = jnp.zeros_like(dps_ref)

        dyp_v = dyp_ref[...]
        d_v = d_ref[...]
        ps = ps_ref[...]
        dypre = _dot_nt(dyp_v[:, 0:256], wpo_ref[0])
        for j in range(1, N_CHIPS):
            dypre = dypre + _dot_nt(dyp_v[:, j * 256:(j + 1) * 256], wpo_ref[j])
        y4 = jnp.concatenate([_dot(d_v[:, g * 128:(g + 1) * 128], pw_ref[g]) for g in range(POOL_GROUPS)], axis=1)
        ypre = (y4 * ps).astype(BF16)
        for j in range(N_CHIPS):
            dwpo_ref[j] += _dot_tn(ypre, dyp_v[:, j * 256:(j + 1) * 256])
        dps_ref[...] += jnp.sum(dypre * y4, axis=0, keepdims=True)
        dy4 = dypre * ps
        t = k * ts + lax.broadcasted_iota(jnp.int32, (ts, 1), 0)
        for g, w in enumerate(POOL_WINDOWS):
            lanes = slice(g * POOL_GROUP_DIM, (g + 1) * POOL_GROUP_DIM)
            dd = _dot_nt(dy4[:, lanes], pw_ref[g])
            dpw_ref[g] += _dot_tn(d_v[:, lanes], dy4[:, lanes])
            e = dd * (1.0 / jnp.minimum(t + 1, w).astype(F32))
            acc = jnp.concatenate([e, carry_scr[:, lanes]], axis=0)
            carry_scr[:, lanes] = e[0:POOL_HALO, :]
            n = ts + POOL_HALO
            sh = 1
            while sh < w:
                acc = acc + pltpu.roll(acc, n - sh, 0)
                sh *= 2
            dzp_ref[:, lanes] = (acc[:ts, :] - dd).astype(BF16)

    return _call(
        body, "bwd_pool", (nt,),
        [_rows(ts, D_MODEL, nt), _rows(ts, POOL_WIDTH, nt), _resident(pool_w.shape), _resident((1, POOL_WIDTH)),
         _resident(w_pool_out.shape)],
        [_rows(ts, POOL_WIDTH, nt), _acc(w_pool_out.shape), _acc(pool_w.shape), _acc((1, POOL_WIDTH))],
        [_sds((s, POOL_WIDTH), BF16), _sds(w_pool_out.shape), _sds(pool_w.shape), _sds((1, POOL_WIDTH))],
        [pltpu.VMEM((POOL_HALO, POOL_WIDTH), F32)],
        (dyp, d, pool_w, pool_scale, w_pool_out), tasks)


def _assemble_dz(dz_scr, dzp_ref, durx_ref, durg_ref, dgp_ref, dgr_ref):
    dz_scr[:, 0:512] = dzp_ref[...]
    dz_scr[:, 512:1536] = durx_ref[...]
    dz_scr[:, 1536:2560] = durg_ref[...]
    dz_scr[:, 2560:3584] = dgp_ref[...]
    dz_scr[:, 3584:4608] = dgr_ref[...]


def _dw_in(h1, dzp, durx, durg, dgp, dgr, ts, tasks=()):
    s = h1.shape[0]
    ts = min(2 * ts, s)
    nt = s // ts
    half = D_MODEL // 2

    def body(h1_ref, dzp_ref, durx_ref, durg_ref, dgp_ref, dgr_ref, send_ref, acc_ref, theirs_ref, mine_ref, got_ref, sems):
        o, i = pl.program_id(0), pl.program_id(1)

        @pl.when((o == 0) & (i == 0))
        def _():
            theirs_ref[...] = jnp.zeros_like(theirs_ref)
            mine_ref[...] = jnp.zeros_like(mine_ref)

        groups = (dzp_ref, durx_ref, durg_ref, dgp_ref, dgr_ref)
        for out_ref, which in ((theirs_ref, 0), (mine_ref, 1)):
            @pl.when(o == which)
            def _():
                for group, dz_ref in enumerate(groups):
                    for j, shard_cols, group_cols in _overlaps(group):
                        out_ref[j, :, shard_cols] += _dot_tn(h1_ref[...], dz_ref[:, group_cols])

        x, y, c, me, _ = _place()
        send = _remote(theirs_ref, got_ref, sems.at[0], sems.at[1], (x, y, 1 - c))
        pl.when((o == 1) & (i == 0))(send.start)

        @pl.when((o == 1) & (i == nt - 1))
        def _():
            send.wait()
            for k in range(N_CHIPS):
                part = (mine_ref[k] + got_ref[k]).astype(BF16)
                send_ref[k] = part

                @pl.when(me == k)
                def _():
                    acc_ref[k] = part

    def h1_cols(o, i):
        c = lax.axis_index("c")
        return i, jnp.where(o == 0, 1 - c, c)

    rows = lambda width: pl.BlockSpec((ts, width), lambda o, i: (i, 0))
    shape = (N_CHIPS, half, IN_SHARD)
    whole = pl.BlockSpec(shape, lambda o, i: (0, 0, 0))
    return _call(
        body, "dw_in", (2, nt), [pl.BlockSpec((ts, half), h1_cols), rows(POOL_WIDTH)] + [rows(D_MODEL)] * 4,
        [whole, whole], [_sds(shape, BF16)] * 2,
        [pltpu.VMEM(shape, F32)] * 3 + [pltpu.SemaphoreType.DMA((2,))], (h1, dzp, durx, durg, dgp, dgr), tasks)


def _bwd_in(dzp, durx, durg, dgp, dgr, x, dx1, g_pre, w_in, ts, tasks=()):
    s = x.shape[0]

    def body(dzp_ref, durx_ref, durg_ref, dgp_ref, dgr_ref, x_ref, dx1_ref, g_ref, win_ref, gx_ref, dg_ref, dz_scr):
        @pl.when(pl.program_id(0) == 0)
        def _():
            dg_ref[...] = jnp.zeros_like(dg_ref)

        _assemble_dz(dz_scr, dzp_ref, durx_ref, durg_ref, dgp_ref, dgr_ref)
        _, xh, r = _rms_fwd(x_ref[...], g_ref[...])
        dh1 = _dot_nt(dz_scr[:, 0:IN_SHARD], win_ref[0])
        for j in range(1, N_CHIPS):
            dh1 = dh1 + _dot_nt(dz_scr[:, j * IN_SHARD:(j + 1) * IN_SHARD], win_ref[j])
        dx, dg = _rms_bwd(xh, r, g_ref[...], dh1)
        dg_ref[...] += dg
        gx_ref[...] = dx1_ref[...] + dx

    row = _rows(ts, D_MODEL)
    return _call(
        body, "bwd_in", (s // ts,),
        [_rows(ts, POOL_WIDTH)] + [row] * 6 + [_resident((1, D_MODEL)), _resident(w_in.shape)],
        [row, _acc((1, D_MODEL))], [_sds((s, D_MODEL)), _sds((1, D_MODEL))],
        [pltpu.VMEM((ts, IN_TOTAL), BF16)], (dzp, durx, durg, dgp, dgr, x, dx1, g_pre, w_in), tasks)


def _place():
    x, y, c = lax.axis_index("x"), lax.axis_index("y"), lax.axis_index("c")
    others = [(1 - x, y), (x, 1 - y), (1 - x, 1 - y)]
    return x, y, c, 2 * x + y, others


def _remote(src, dst, send_sem, recv_sem, to):
    return pltpu.make_async_remote_copy(src_ref=src, dst_ref=dst, send_sem=send_sem, recv_sem=recv_sem,
                                        device_id=to, device_id_type=MESH)


def _own_slots(ws, dtypes, name, tasks=()):
    n = len(ws)
    hbm = pl.BlockSpec(memory_space=pltpu.HBM)

    def body(*refs):
        srcs, outs, f32_bufs, cast_bufs, sems = refs[:n], refs[n:2 * n], refs[2 * n:3 * n], refs[3 * n:4 * n], refs[4 * n]
        me = _place()[3]
        loads = [pltpu.make_async_copy(srcs[k], f32_bufs[k], sems.at[k, 0]) for k in range(n)]
        stores = [pltpu.make_async_copy(cast_bufs[k], outs[k].at[me], sems.at[k, 1]) for k in range(n)]
        for cp in loads:
            cp.start()
        for k in range(n):
            loads[k].wait()
            cast_bufs[k][...] = f32_bufs[k][...].astype(dtypes[k])
            stores[k].start()
        for cp in stores:
            cp.wait()

    return _call(
        body, name, (), [hbm] * n, [hbm] * n, [_sds((N_CHIPS,) + w.shape, dt) for w, dt in zip(ws, dtypes)],
        [pltpu.VMEM(w.shape, F32) for w in ws] + [pltpu.VMEM(w.shape, dt) for w, dt in zip(ws, dtypes)]
        + [pltpu.SemaphoreType.DMA((n, 2))],
        [pltpu.with_memory_space_constraint(w, pltpu.HBM) for w in ws], tasks)


def _run(tasks, name):
    if isinstance(tasks, _Task):
        return _call(None, name, (), [], [], [], [], (), (tasks,))[1][0]
    return _call(None, name, (), [], [], [], [], (), tuple(tasks))[1]


def _gather_task(bufs, relay_steps=(0, 0)):
    n = len(bufs)
    NBR_X, NBR_Y, QUARTER_VIA_Y, QUARTER_VIA_X, SIB_X, SIB_Y, SIB_DIAG = range(7)

    def parts(out):
        x, y, c, me, _ = _place()
        ah = out.shape[1] // 2
        q = ah // 2 if (ah // 2) % 16 == 0 else ah
        return c * ah, ah, q

    def copy(out, w, k, chip, row0, rows, to, sems):
        slot = out.at[chip, pl.ds(row0, rows)]
        return _remote(slot, slot, sems[0].at[w, k], sems[1].at[w, k], to)

    def plan(out, w, sems):
        x, y, c, me, _ = _place()
        row0, ah, q = parts(out)
        xn, yn, dg = 2 * (1 - x) + y, 2 * x + (1 - y), 2 * (1 - x) + (1 - y)
        to_x, to_y, sib = (1 - x, y, c), (x, 1 - y, c), (x, y, 1 - c)
        other = (1 - c) * ah
        cp = functools.partial(copy, out, w, sems=sems)
        sends = {NBR_X: cp(NBR_X, me, row0, ah, to_x), NBR_Y: cp(NBR_Y, me, row0, ah, to_y),
                 QUARTER_VIA_Y: cp(QUARTER_VIA_Y, xn, row0, q, to_y), SIB_X: cp(SIB_X, xn, row0, ah, sib),
                 SIB_Y: cp(SIB_Y, yn, row0, ah, sib), SIB_DIAG: cp(SIB_DIAG, dg, row0, ah, sib)}
        lands = {NBR_X: cp(NBR_X, xn, row0, ah, to_x), NBR_Y: cp(NBR_Y, yn, row0, ah, to_y),
                 QUARTER_VIA_Y: cp(QUARTER_VIA_Y, dg, row0, q, to_y), SIB_X: cp(SIB_X, xn, other, ah, sib),
                 SIB_Y: cp(SIB_Y, yn, other, ah, sib), SIB_DIAG: cp(SIB_DIAG, dg, other, ah, sib)}
        if q < ah:
            sends[QUARTER_VIA_X] = cp(QUARTER_VIA_X, yn, row0 + q, ah - q, to_x)
            lands[QUARTER_VIA_X] = cp(QUARTER_VIA_X, dg, row0 + q, ah - q, to_x)
        return sends, lands

    def start(ins, outs, sems):
        for w, out in enumerate(outs):
            sends, _ = plan(out, w, sems)
            sends[NBR_X].start()
            sends[NBR_Y].start()

    def pass_neighbours(ins, outs, sems):
        for w, out in enumerate(outs):
            sends, lands = plan(out, w, sems)
            lands[NBR_X].wait_recv()
            sends[QUARTER_VIA_Y].start()
            sends[SIB_X].start()
            lands[NBR_Y].wait_recv()
            if QUARTER_VIA_X in sends:
                sends[QUARTER_VIA_X].start()
            sends[SIB_Y].start()

    def pass_diagonal(ins, outs, sems):
        for w, out in enumerate(outs):
            sends, lands = plan(out, w, sems)
            lands[QUARTER_VIA_Y].wait_recv()
            if QUARTER_VIA_X in lands:
                lands[QUARTER_VIA_X].wait_recv()
            sends[SIB_DIAG].start()

    def finish(ins, outs, sems):
        for w, out in enumerate(outs):
            sends, lands = plan(out, w, sems)
            for k in (SIB_X, SIB_Y, SIB_DIAG):
                lands[k].wait_recv()
        for w, out in enumerate(outs):
            sends, _ = plan(out, w, sems)
            for cp in sends.values():
                cp.wait_send()

    return _Task(bufs, [_sds(b.shape, b.dtype) for b in bufs], {i: i for i in range(n)},
                 [pltpu.SemaphoreType.DMA((n, 7)), pltpu.SemaphoreType.DMA((n, 7))], start, finish,
                 [(pass_neighbours, relay_steps[0]), (pass_diagonal, relay_steps[1])])


def _halves_task(grads):
    n = len(grads)

    def copy(src, out, w, sems):
        x, y, c, _, _ = _place()
        ah = out.shape[1]
        return _remote(src.at[:, pl.ds((1 - c) * ah, ah)], out, sems[0].at[w], sems[1].at[w], (x, y, 1 - c))

    def start(ins, outs, sems):
        for w, (src, out) in enumerate(zip(ins, outs)):
            copy(src, out, w, sems).start()

    def finish(ins, outs, sems):
        for w, (src, out) in enumerate(zip(ins, outs)):
            copy(src, out, w, sems).wait()

    return _Task(grads, [_sds((g.shape[0], g.shape[1] // 2, g.shape[2]), g.dtype) for g in grads], {},
                 [pltpu.SemaphoreType.DMA((n,)), pltpu.SemaphoreType.DMA((n,))], start, finish)


def _exchange_task(sends, accs):
    n = len(accs)
    given = [s for s in sends if s is not None]

    def copies(ins, outs, sems):
        send_refs = iter(ins[:len(given)])
        srcs = [next(send_refs) if s is not None else None for s in sends]
        x, y, c, me, others = _place()
        for w, out in enumerate(outs):
            for j, (ox, oy) in enumerate(others):
                src = out.at[me] if srcs[w] is None else srcs[w].at[2 * ox + oy]
                yield _remote(src, out.at[me], sems[0].at[w, j], sems[1].at[w, j], (ox, oy, c))

    def start(ins, outs, sems):
        for cp in copies(ins, outs, sems):
            cp.start()

    def finish(ins, outs, sems):
        x, y, c, _, others = _place()
        for w, out in enumerate(outs):
            for j, (ox, oy) in enumerate(others):
                slot = out.at[2 * ox + oy]
                _remote(slot, slot, sems[0].at[w, j], sems[1].at[w, j], (ox, oy, c)).wait_recv()
        for cp in copies(ins, outs, sems):
            cp.wait_send()

    return _Task(given + list(accs), [_sds(a.shape, a.dtype) for a in accs], {len(given) + i: i for i in range(n)},
                 [pltpu.SemaphoreType.DMA((n, 3)), pltpu.SemaphoreType.DMA((n, 3))], start, finish)


def _swap_task(arrays):
    n = len(arrays)

    def copy(src, out, w, sems):
        x, y, c, _, _ = _place()
        return _remote(src, out, sems[0].at[w], sems[1].at[w], (x, y, 1 - c))

    def start(ins, outs, sems):
        for w, (src, out) in enumerate(zip(ins, outs)):
            copy(src, out, w, sems).start()

    def finish(ins, outs, sems):
        for w, (src, out) in enumerate(zip(ins, outs)):
            copy(src, out, w, sems).wait()

    return _Task(arrays, [_sds(a.shape, a.dtype) for a in arrays], {},
                 [pltpu.SemaphoreType.DMA((n,)), pltpu.SemaphoreType.DMA((n,))], start, finish)


def _all_devices_task(arrays):
    n = len(arrays)
    flips = [(dx, dy, dc) for dx in (0, 1) for dy in (0, 1) for dc in (0, 1)][1:]

    def peers():
        x, y, c, _, _ = _place()
        flip = lambda v, d: 1 - v if d else v
        return 4 * x + 2 * y + c, [(flip(x, dx), flip(y, dy), flip(c, dc)) for dx, dy, dc in flips]

    def start(ins, outs, sems):
        me, others = peers()
        for w, (src, out) in enumerate(zip(ins, outs)):
            pltpu.make_async_copy(src, out.at[me], sems[2].at[w]).start()
            for k, peer in enumerate(others):
                _remote(src, out.at[me], sems[0].at[w, k], sems[1].at[w, k], peer).start()

    def finish(ins, outs, sems):
        me, others = peers()
        for w, (src, out) in enumerate(zip(ins, outs)):
            for k, (px, py, pc) in enumerate(others):
                slot = out.at[4 * px + 2 * py + pc]
                _remote(slot, slot, sems[0].at[w, k], sems[1].at[w, k], (px, py, pc)).wait_recv()
            for k, peer in enumerate(others):
                _remote(src, out.at[me], sems[0].at[w, k], sems[1].at[w, k], peer).wait_send()
            pltpu.make_async_copy(src, out.at[me], sems[2].at[w]).wait()

    return _Task(arrays, [_sds((8,) + a.shape, a.dtype) for a in arrays], {},
                 [pltpu.SemaphoreType.DMA((n, 7)), pltpu.SemaphoreType.DMA((n, 7)), pltpu.SemaphoreType.DMA((n,))],
                 start, finish)


def _share_task(shares):
    n = len(shares)

    def copy(out, w, sems, slot):
        x, y, c, _, _ = _place()
        return _remote(out.at[slot], out.at[slot], sems[0].at[w], sems[1].at[w], (x, y, 1 - c))

    def start(ins, outs, sems):
        c = _place()[2]
        for w, out in enumerate(outs):
            copy(out, w, sems, c).start()

    def finish(ins, outs, sems):
        c = _place()[2]
        for w, out in enumerate(outs):
            copy(out, w, sems, 1 - c).wait_recv()
        for w, out in enumerate(outs):
            copy(out, w, sems, c).wait_send()

    return _Task(shares, [_sds(s.shape, s.dtype) for s in shares], {i: i for i in range(n)},
                 [pltpu.SemaphoreType.DMA((n,)), pltpu.SemaphoreType.DMA((n,))], start, finish)


TILE_BYTES = 2 * 1024 * 1024
ADAM_TILE_BYTES = 384 * 1024
PARTIAL_TILE_BYTES = 256 * 1024
BF16_SUBLANES = 16


def _in_hbm(t):
    return pltpu.with_memory_space_constraint(t, pltpu.HBM)


def _row_tile(rows, cols, limit=TILE_BYTES, min_tiles=1):
    limit = min(limit, rows * cols * 4 // min_tiles)
    fits = [tr for tr in range(BF16_SUBLANES, rows, BF16_SUBLANES) if rows % tr == 0 and tr * cols * 4 <= limit]
    return max(fits) if fits and rows * cols * 4 > limit else rows


def _chip_partial(g, got, place, wire_dtype):
    ns, ah, b = got.shape
    sharded = ns == N_CHIPS
    tr = _row_tile(ah, b, PARTIAL_TILE_BYTES, min_tiles=4)
    nb = ah // tr
    first = 0 if g.shape[1] == ah else nb

    def body(place_ref, *refs):
        g_refs, got_refs, outs = refs[:ns], refs[ns:2 * ns], refs[2 * ns:]
        parts = [g_refs[k][0] + got_refs[k][0] for k in range(ns)]
        own = parts[0]
        if sharded:
            for k in range(ns):
                outs[0][k] = parts[k].astype(wire_dtype)
                if k:
                    own = jnp.where(place_ref[0] == k, parts[k], own)
        outs[-1][0] = own.astype(wire_dtype)

    blk = (1, tr, b)
    in_specs = ([pl.BlockSpec(blk, lambda i, s, k=k: (k, s[1] * first + i, 0)) for k in range(ns)]
                + [pl.BlockSpec(blk, lambda i, s, k=k: (k, i, 0)) for k in range(ns)])
    acc_spec = pl.BlockSpec(blk, lambda i, s: (s[0], i, 0))
    acc_shape = _sds((N_CHIPS, ah, b), wire_dtype)
    out = pl.pallas_call(
        body, name="grad_chip_partial",
        grid_spec=pltpu.PrefetchScalarGridSpec(
            num_scalar_prefetch=1, grid=(nb,), in_specs=in_specs,
            out_specs=[pl.BlockSpec((ns, tr, b), lambda i, s: (0, i, 0)), acc_spec] if sharded else [acc_spec]),
        out_shape=[acc_shape, acc_shape] if sharded else [acc_shape],
        compiler_params=pltpu.CompilerParams(dimension_semantics=("arbitrary",), vmem_limit_bytes=VMEM_LIMIT),
    )(place, *([g] * ns), *([got] * ns))
    return (out[0], out[1]) if sharded else (None, out[0])


def _chip_sum(acc, place):
    _, ah, b = acc.shape
    tr = _row_tile(ah, b)

    def body(place_ref, p_ref, out_ref):
        total = p_ref[0].astype(F32) + p_ref[1].astype(F32)
        total = total + p_ref[2].astype(F32)
        out_ref[0] = total + p_ref[3].astype(F32)

    return pl.pallas_call(
        body, name="grad_chip_sum",
        grid_spec=pltpu.PrefetchScalarGridSpec(
            num_scalar_prefetch=1, grid=(ah // tr,),
            in_specs=[pl.BlockSpec((N_CHIPS, tr, b), lambda i, s: (0, i, 0))],
            out_specs=pl.BlockSpec((1, tr, b), lambda i, s: (s[1], i, 0))),
        out_shape=_sds((2, ah, b)),
        compiler_params=pltpu.CompilerParams(dimension_semantics=("arbitrary",)),
    )(place, _in_hbm(acc))


def _adam_math(w, g, m, v):
    nm = ADAM_B1 * m + (1.0 - ADAM_B1) * g
    nv = ADAM_B2 * v + (1.0 - ADAM_B2) * (g * g)
    m_hat = nm / (1.0 - ADAM_B1 ** ADAM_STEP)
    v_hat = nv / (1.0 - ADAM_B2 ** ADAM_STEP)
    return -ADAM_LR * (m_hat / (jnp.sqrt(v_hat) + ADAM_EPS) + ADAM_WD * w), nm, nv


def _adamw(w, g, m, v):
    a, b = w.shape
    tr = _row_tile(a, b)

    def body(w_ref, g_ref, m_ref, v_ref, g_out, d_ref, nm_ref, nv_ref):
        g_out[...] = g_ref[...]
        d_ref[...], nm_ref[...], nv_ref[...] = _adam_math(w_ref[...], g_ref[...], m_ref[...], v_ref[...])

    blk = pl.BlockSpec((tr, b), lambda i: (i, 0))
    return pl.pallas_call(
        body, name="adamw", grid=(a // tr,),
        in_specs=[blk] * 4, out_specs=[blk] * 4, out_shape=[_sds((a, b))] * 4,
        compiler_params=pltpu.CompilerParams(dimension_semantics=("arbitrary",)),
    )(w, g, m, v)


def _adamw_sum(w, m, v, acc, got, place):
    a, b = w.shape
    ah = a // 2
    tr = _row_tile(ah, b, ADAM_TILE_BYTES, min_tiles=2)
    nb = ah // tr

    def body(place_ref, w_ref, m_ref, v_ref, acc_ref, got_ref, g_out, d_ref, nm_ref, nv_ref):
        mine = (pl.program_id(0) // nb) == place_ref[1]
        part = lambda k: jnp.where(mine, acc_ref[k], got_ref[k]).astype(F32)
        g = part(0) + part(1)
        g = g + part(2)
        g = g + part(3)
        g_out[...] = g
        d_ref[...], nm_ref[...], nv_ref[...] = _adam_math(w_ref[...], g, m_ref[...], v_ref[...])

    blk = pl.BlockSpec((tr, b), lambda i, s: (i, 0))
    mine_spec = pl.BlockSpec((N_CHIPS, tr, b), lambda i, s: (0, jnp.where(i // nb == s[1], i % nb, 0), 0))
    got_spec = pl.BlockSpec((N_CHIPS, tr, b), lambda i, s: (0, jnp.where(i // nb == s[1], 0, i % nb), 0))
    return pl.pallas_call(
        body, name="adamw_sum",
        grid_spec=pltpu.PrefetchScalarGridSpec(
            num_scalar_prefetch=1, grid=(a // tr,), in_specs=[blk] * 3 + [mine_spec, got_spec], out_specs=[blk] * 4),
        out_shape=[_sds((a, b))] * 4,
        compiler_params=pltpu.CompilerParams(dimension_semantics=("arbitrary",), vmem_limit_bytes=VMEM_LIMIT),
    )(place, w, m, v, _in_hbm(acc), _in_hbm(got))


def _adamw_pieces(g, pieces, name):
    n = len(pieces)

    def body(g_ref, *refs):
        def grad(rows, cols):
            if len(g_ref.shape) == 2:
                return g_ref[rows, cols]
            total = g_ref[0, rows, cols]
            for k in range(1, g_ref.shape[0]):
                total = total + g_ref[k, rows, cols]
            return total

        ins, outs = refs[:3 * n], refs[3 * n:]
        for i, piece in enumerate(pieces):
            w_ref, m_ref, v_ref = ins[3 * i:3 * i + 3]
            o_g, o_d, o_m, o_v = outs[4 * i:4 * i + 4]
            if len(piece) == 5:
                g_v = grad(piece[3], piece[4])
                o_g[...] = g_v
                o_d[...], o_m[...], o_v[...] = _adam_math(w_ref[...], g_v, m_ref[...], v_ref[...])
            else:
                for r in range(w_ref.shape[1] // SMALL_COLS):
                    lanes = slice(r * SMALL_COLS, (r + 1) * SMALL_COLS)
                    g_v = grad(slice(piece[3] + r, piece[3] + r + 1), slice(None))
                    o_g[:, lanes] = g_v
                    o_d[:, lanes], o_m[:, lanes], o_v[:, lanes] = _adam_math(w_ref[:, lanes], g_v, m_ref[:, lanes],
                                                                            v_ref[:, lanes])

    operands = [t for piece in pieces for t in piece[:3]]
    out = pl.pallas_call(
        body, name=name,
        out_shape=[_sds(piece[0].shape) for piece in pieces for _ in range(4)],
    )(g, *operands)
    return [tuple(out[4 * i:4 * i + 4]) for i in range(n)]


TINY_ROWS, TINY_COLS = 16, 768
SMALL_COLS = 128
SMALL_ROWS = 624


def _pack_tiny(conv_w, b_gates, fcw):
    ns = conv_w.shape[0]
    pad = lambda t: jnp.pad(t, ((0, 0), (0, 0), (0, TINY_COLS - t.shape[2])))
    z = lambda rows: jnp.zeros((ns, rows, TINY_COLS), F32)
    return jnp.concatenate([pad(conv_w), pad(b_gates), z(2), fcw, z(TINY_ROWS - 11)], axis=1)


def _unpack_tiny(t):
    return t[:, 0:4, 0:256], t[:, 4:6, 0:256], t[:, 8:11, :]


def _cols_to_shards(t, n):
    return t.reshape(t.shape[0], N_CHIPS, n).transpose(1, 0, 2)


def _shards_to_cols(t):
    return t.transpose(1, 0, 2).reshape(t.shape[1], -1)


_VECTORS = ("g_mix_post", "conv_b", "lru_lambda", "g_ffn_pre", "g_ffn_post", "g_ple_gate", "g_ple_post", "pool_scale",
            "ffn_conv_b")
_VECTOR_LEN = {"pool_scale": POOL_WIDTH, "ffn_conv_b": D_FF}
POOL_W_ROWS = POOL_GROUPS * POOL_GROUP_DIM


def _vector_rows():
    rows, row = {}, POOL_W_ROWS
    for k in _VECTORS:
        rows[k] = row
        row += max(8, _VECTOR_LEN.get(k, D_MODEL) // SMALL_COLS)
    return rows, row


def _pack_small(grads, loss):
    tiles = lambda t: jnp.pad(t, ((0, -t.shape[0] % 8), (0, 0)))
    parts = [grads["pool_w"].reshape(POOL_W_ROWS, SMALL_COLS)] + [tiles(grads[k].reshape(-1, SMALL_COLS)) for k in _VECTORS]
    parts.append(tiles(loss))
    used = sum(t.shape[0] for t in parts)
    return jnp.concatenate(parts + [jnp.zeros((SMALL_ROWS - used, SMALL_COLS), F32)], axis=0)


def _gates_block_diag(w):
    w4 = w.reshape(2, GATE_BLOCKS, 4, RNN_HEAD_DIM, RNN_HEAD_DIM)
    eye = jnp.eye(4, dtype=w.dtype)
    return jnp.einsum("gqhij,hk->gqhikj", w4, eye).reshape(2, GATE_BLOCKS, GATE_BLOCK, GATE_BLOCK)


def _gates_from_block_diag(dw):
    d6 = dw.reshape(2, GATE_BLOCKS, 4, RNN_HEAD_DIM, 4, RNN_HEAD_DIM)
    blocks = [d6[:, :, hh, :, hh, :] for hh in range(4)]
    return jnp.stack(blocks, axis=2).reshape(2, RNN_HEADS, RNN_HEAD_DIM, RNN_HEAD_DIM)


ROW_TILE = 256
DW_TILES = 8

_SHARDED = ("w_in", "w_pool_out", "w_rg_out", "w_o", "w_up", "w_down", "w_ple_gate", "w_ple_proj")
_WEIGHTS = ("g_mix_pre", "g_mix_post", "w_in", "pool_w", "pool_scale", "w_pool_out", "conv_w", "conv_b", "w_rg_gates",
            "b_rg_gates", "lru_lambda", "w_rg_out", "w_o", "g_ffn_pre", "g_ffn_post", "w_up", "ffn_conv_w", "ffn_conv_b",
            "w_down", "g_ple_gate", "w_ple_gate", "w_ple_proj", "g_ple_post")


def _wire_dtype(g):
    return BF16 if g.shape[1] >= 64 and g.shape[2] > SMALL_COLS else F32


def _partials(grads, got, place):
    parts = [_chip_partial(g, r, place, _wire_dtype(g)) for g, r in zip(grads, got)]
    return [send for send, _ in parts], [acc for _, acc in parts]


def _whole(both):
    return [b.reshape(2 * b.shape[1], b.shape[2]) for b in both]


def _step(x, p, tgt, rep, place, ts):
    vec = lambda k: rep[k].reshape(1, -1)
    tall = min(2 * ts, x.shape[0])
    pool_w = rep["pool_w"].astype(BF16)
    wg = _gates_block_diag(rep["w_rg_gates"]).astype(BF16)
    sq = lambda t: t.reshape(D_MODEL, D_MODEL)
    by4 = lambda t: t.reshape(N_CHIPS, -1, D_MODEL)

    first, ride1, ride2 = (("w_in", "w_pool_out", "tiny"), ("w_rg_out", "w_o", "w_down"),
                           ("w_up", "w_ple_gate", "w_ple_proj"))
    later = ride1 + ride2
    tiny = _pack_tiny(rep["conv_w"][None], rep["b_rg_gates"][None], rep["ffn_conv_w"][None])[0]
    own_first, _ = _own_slots([rep["w_in"], rep["w_pool_out"], tiny], [BF16, BF16, F32], "own_slots_first")
    own_later, (got,) = _own_slots([rep[k] for k in later], [BF16] * len(later), "own_slots_gather_first",
                                   [_gather_task(own_first)])
    own = dict(zip(later, own_later))
    full = dict(zip(first, got))
    conv_w, b_gates, fcw = [_shards_to_cols(t) for t in _unpack_tiny(full["tiny"])]

    (urx, urg, gp, gr, d, ypool, h1), (got,) = _fwd_in_pool(
        x, vec("g_mix_pre"), full["w_in"], pool_w, vec("pool_scale"), full["w_pool_out"], ts,
        [_gather_task([own[k] for k in ride1], relay_steps=(6, 2))])
    full.update(zip(ride1, got))
    w_rg_out, w_o, w_down = sq(full["w_rg_out"]), sq(full["w_o"]), full["w_down"].reshape(D_FF, D_MODEL)
    (xc, r, ig, h, yrnn, mo, x1, glr, ggr, sp, sr), (got,) = _fwd_rnn_merge(
        urx, urg, gp, gr, ypool, x, conv_w, vec("conv_b"), wg, b_gates, vec("lru_lambda"), w_rg_out, w_o,
        vec("g_mix_post"), ts, [_gather_task([own[k] for k in ride2], relay_steps=(7, 3))])
    full.update(zip(ride2, got))
    (up, gl, gg, h2, dn, x2), _ = _fwd_ffn(x1, vec("g_ffn_pre"), full["w_up"], fcw, vec("ffn_conv_b"), w_down,
                                           vec("g_ffn_post"), ts)
    dx2, loss, d_w_gate, d_w_proj, d_g_ple_gate, d_g_ple_post = _ple_loss(
        x2, p, tgt, vec("g_ple_gate"), sq(full["w_ple_gate"]), full["w_ple_proj"], vec("g_ple_post"), tall)
    dup, d_w_down, d_fcw, d_fcb, d_g_ffn_post = _bwd_ffn_down(dx2, dn, up, gl, gg, fcw, w_down, vec("g_ffn_post"), ts)

    names1, grads1 = ("w_ple_gate", "w_ple_proj", "w_down"), [by4(d_w_gate), d_w_proj, by4(d_w_down)]
    (dx1, d_g_ffn_pre), (got1,) = _bwd_ffn_up(dup, x1, dx2, vec("g_ffn_pre"), full["w_up"], tall, [_halves_task(grads1)])
    (send_w_up, acc_w_up), (accs1,) = _dw_up(h2, dup, ts, [_exchange_task(*_partials(grads1, got1, place))])
    (dgp, dgr, dyp, dyr, d_w_o, d_g_mix_post), (theirs1,) = _bwd_merge(
        dx1, mo, sp, sr, ypool, yrnn, vec("g_mix_post"), w_o, tall, [_swap_task(accs1)])
    (durx, durg, d_w_rg_out, d_wg, d_conv_w, d_conv_b, d_b_gates, d_lam), (accs2,) = _bwd_rnn(
        dyr, urx, glr, ggr, xc, r, ig, h, conv_w, wg, vec("lru_lambda"), w_rg_out, ts,
        [_exchange_task([send_w_up], [acc_w_up])])
    names3 = ("w_o", "w_rg_out", "tiny", "w_rg_gates")
    grads3 = [by4(d_w_o), by4(d_w_rg_out),
              _pack_tiny(_cols_to_shards(d_conv_w, 256), _cols_to_shards(d_b_gates, 256), _cols_to_shards(d_fcw, 768)),
              _gates_from_block_diag(d_wg).reshape(1, 2 * RNN_HEADS * RNN_HEAD_DIM, RNN_HEAD_DIM)]
    (dzp, d_w_pool_out, d_pool_w, d_pool_scale), (got3, theirs2) = _bwd_pool(
        dyp, d, pool_w, vec("pool_scale"), full["w_pool_out"], tall, [_halves_task(grads3), _swap_task(accs2)])
    replicated = {"g_mix_post": d_g_mix_post, "conv_b": d_conv_b, "lru_lambda": d_lam, "g_ffn_pre": d_g_ffn_pre,
                  "g_ffn_post": d_g_ffn_post, "g_ple_gate": d_g_ple_gate, "g_ple_post": d_g_ple_post,
                  "pool_scale": d_pool_scale, "ffn_conv_b": d_fcb, "pool_w": d_pool_w}
    names4 = ("w_in", "w_pool_out", "small")
    small4 = [d_w_pool_out, _pack_small(replicated, loss)[None]]
    (send_w_in, acc_w_in), (accs3, got_small4) = _dw_in(
        h1, dzp, durx, durg, dgp, dgr, ts, [_exchange_task(*_partials(grads3, got3, place)), _halves_task(small4)])
    sends4, accs4 = _partials(small4, got_small4, place)
    (grad_x, d_g_mix_pre), (accs4, theirs3, both3) = _bwd_in(
        dzp, durx, durg, dgp, dgr, x, dx1, vec("g_mix_pre"), full["w_in"], tall,
        [_exchange_task([send_w_in] + sends4, [acc_w_in] + accs4), _swap_task(accs3[:2]),
         _share_task([_chip_sum(acc, place) for acc in accs3[2:]])])
    theirs4, both4, (g_mix_pre_parts,) = _run(
        [_swap_task(accs4[:2]), _share_task([_chip_sum(acc, place) for acc in accs4[2:]]),
         _all_devices_task([d_g_mix_pre.reshape(SUBLANES, SMALL_COLS)])], "grad_sibling_share")
    mine = accs1 + accs2 + accs3[:2] + accs4[:2]
    partials = dict(zip(names1 + ("w_up",) + names3[:2] + names4[:2], zip(mine, theirs1 + theirs2 + theirs3 + theirs4)))
    return grad_x, partials, dict(zip(names3[2:] + names4[2:], _whole(both3) + _whole(both4))), g_mix_pre_parts


def kernel(x, p, g_mix_pre, g_mix_post, w_in, pool_w, pool_scale, w_pool_out, conv_w, conv_b, w_rg_gates, b_rg_gates, lru_lambda, w_rg_out, w_o, g_ffn_pre, g_ffn_post, w_up, ffn_conv_w, ffn_conv_b, w_down, g_ple_gate, w_ple_gate, w_ple_proj, g_ple_post, loss_target, m_g_mix_pre, m_g_mix_post, m_w_in, m_pool_w, m_pool_scale, m_w_pool_out, m_conv_w, m_conv_b, m_w_rg_gates, m_b_rg_gates, m_lru_lambda, m_w_rg_out, m_w_o, m_g_ffn_pre, m_g_ffn_post, m_w_up, m_ffn_conv_w, m_ffn_conv_b, m_w_down, m_g_ple_gate, m_w_ple_gate, m_w_ple_proj, m_g_ple_post, v_g_mix_pre, v_g_mix_post, v_w_in, v_pool_w, v_pool_scale, v_w_pool_out, v_conv_w, v_conv_b, v_w_rg_gates, v_b_rg_gates, v_lru_lambda, v_w_rg_out, v_w_o, v_g_ffn_pre, v_g_ffn_post, v_w_up, v_ffn_conv_w, v_ffn_conv_b, v_w_down, v_g_ple_gate, v_w_ple_gate, v_w_ple_proj, v_g_ple_post):
    args = dict(locals())
    w = {k: args[k][0] for k in _WEIGHTS}
    m = {k: args["m_" + k][0] for k in _WEIGHTS}
    v = {k: args["v_" + k][0] for k in _WEIGHTS}
    place = jnp.stack([2 * lax.axis_index("x") + lax.axis_index("y"), lax.axis_index("c")]).astype(jnp.int32)
    grad_x, partials, reduced, g_mix_pre_parts = _step(x[0], p[0, 0], loss_target[0], w, place, ROW_TILE)

    gates_2d = (2 * RNN_HEADS * RNN_HEAD_DIM, RNN_HEAD_DIM)
    as2d = lambda k, shape: tuple(t[k].reshape(shape) for t in (w, m, v))
    done = {k: tuple(_adamw_sum(w[k], m[k], v[k], *partials[k], place)) for k in _SHARDED}
    gates_w, gates_m, gates_v = as2d("w_rg_gates", gates_2d)
    done["w_rg_gates"] = tuple(_adamw(gates_w, reduced["w_rg_gates"], gates_m, gates_v))
    tiny_names = ("conv_w", "b_rg_gates", "ffn_conv_w")
    tiny_at = ((slice(0, 4), slice(0, 256)), (slice(4, 6), slice(0, 256)), (slice(8, 11), slice(None)))
    done.update(zip(tiny_names, _adamw_pieces(
        reduced["tiny"], [(w[k], m[k], v[k]) + at for k, at in zip(tiny_names, tiny_at)], "adamw_tiny")))
    vector_rows, loss_row = _vector_rows()
    pieces = [as2d("pool_w", (POOL_W_ROWS, SMALL_COLS)) + (slice(0, POOL_W_ROWS), slice(None))]
    pieces += [as2d(k, (1, -1)) + (vector_rows[k],) for k in _VECTORS]
    done.update(zip(("pool_w",) + _VECTORS, _adamw_pieces(reduced["small"], pieces, "adamw_small")))
    done["g_mix_pre"] = _adamw_pieces(g_mix_pre_parts, [as2d("g_mix_pre", (1, -1)) + (0,)], "adamw_g_mix_pre")[0]

    result = [reduced["small"][loss_row, 0], grad_x[None]]
    for kind in range(4):
        result += [done[k][kind].reshape(args[k].shape) for k in _WEIGHTS]
    return tuple(result)
```

```python
import functools

import jax
import jax.numpy as jnp
from jax import lax
from jax.experimental import pallas as pl
from jax.experimental.pallas import tpu as pltpu

F32 = jnp.float32
BF16 = jnp.bfloat16

D_MODEL = 1024
POOL_WINDOWS = (2, 4, 8, 16)
POOL_GROUPS = 4
POOL_WIDTH = 512
POOL_GROUP_DIM = 128
RNN_HEADS = 16
RNN_HEAD_DIM = 64
GATE_BLOCK = 256
GATE_BLOCKS = D_MODEL // GATE_BLOCK
LRU_C = 8.0
D_FF = 3072
PLE_DIM = 256
RMS_EPS = 1e-6
IN_TOTAL = 4608
N_CHIPS = 4
IN_SHARD = IN_TOTAL // N_CHIPS
UP_SHARD = 2 * D_FF // N_CHIPS
Z_GROUPS = ((0, 512), (512, 1536), (1536, 2560), (2560, 3584), (3584, 4608))
POOL_HALO = 16
CONV_HALO = 8

ADAM_LR = 0.001
ADAM_B1 = 0.9
ADAM_B2 = 0.999
ADAM_EPS = 1e-08
ADAM_WD = 0.01
ADAM_STEP = 10

VMEM_LIMIT = 56 * 1024 * 1024
MESH = pl.DeviceIdType.MESH

_GELU_C = 0.7978845608028654
_GELU_A = 0.044715


def _dot(a, b):
    return jnp.dot(a.astype(BF16), b.astype(BF16), preferred_element_type=F32)


def _dot_nt(a, b):
    return lax.dot_general(a.astype(BF16), b.astype(BF16), (((1,), (1,)), ((), ())), preferred_element_type=F32)


def _dot_tn(a, b):
    return lax.dot_general(a.astype(BF16), b.astype(BF16), (((0,), (0,)), ((), ())), preferred_element_type=F32)


def _overlaps(group):
    a, b = Z_GROUPS[group]
    found = []
    for j in range(N_CHIPS):
        lo, hi = max(a, j * IN_SHARD), min(b, (j + 1) * IN_SHARD)
        if lo < hi:
            found.append((j, slice(lo - j * IN_SHARD, hi - j * IN_SHARD), slice(lo - a, hi - a)))
    return found


def _rms_fwd(x, g):
    r = lax.rsqrt(jnp.mean(x * x, axis=-1, keepdims=True) + RMS_EPS)
    xh = x * r
    return xh * g, xh, r


def _rms_bwd(xh, r, g, dy):
    dxh = dy * g
    dg = jnp.sum(dy * xh, axis=0, keepdims=True)
    dx = r * (dxh - xh * jnp.mean(dxh * xh, axis=-1, keepdims=True))
    return dx, dg


def _sigmoid(x):
    return 0.5 * jnp.tanh(0.5 * x) + 0.5


def _gelu(x):
    x2 = x * x
    t = jnp.tanh(x * (_GELU_C + (_GELU_C * _GELU_A) * x2))
    p = 0.5 * t + 0.5
    gl = x * p
    return gl, p + gl * (1.0 - p) * (2.0 * _GELU_C + (6.0 * _GELU_C * _GELU_A) * x2)


def _softplus_neg(lam):
    nl = -lam
    return jnp.maximum(nl, 0.0) + jnp.log(1.0 + jnp.exp(-jnp.abs(nl)))


def _lru_coeffs(r, lam, first_row):
    c8 = LRU_C * _softplus_neg(lam)
    la = -(c8 * r)
    a = jnp.exp(la)
    m2 = jnp.tanh(-la) * (1.0 + a * a)
    mult = jnp.where(first_row, 1.0, jnp.sqrt(m2))
    return c8, a, m2, mult


SUBLANES = 8


def _scan_fwd(a, u, carry):
    n = a.shape[0]
    sub = lax.broadcasted_iota(jnp.int32, (n, 1), 0) % SUBLANES
    acc_a, acc_h = a, u
    for s in (1, 2, 4):
        m = sub >= s
        h_s = jnp.where(m, pltpu.roll(acc_h, s, 0), 0.0)
        a_s = jnp.where(m, pltpu.roll(acc_a, s, 0), 1.0)
        acc_h = acc_a * h_s + acc_h
        acc_a = acc_a * a_s
    out = []
    for g in range(n // SUBLANES):
        rows = slice(g * SUBLANES, (g + 1) * SUBLANES)
        out.append(acc_h[rows] + acc_a[rows] * carry)
        carry = out[-1][SUBLANES - 1:SUBLANES]
    return jnp.concatenate(out, axis=0)


def _scan_bwd(b, g, carry):
    n = b.shape[0]
    sub = lax.broadcasted_iota(jnp.int32, (n, 1), 0) % SUBLANES
    acc_b, acc_l = b, g
    for s in (1, 2, 4):
        m = sub < SUBLANES - s
        l_s = jnp.where(m, pltpu.roll(acc_l, n - s, 0), 0.0)
        b_s = jnp.where(m, pltpu.roll(acc_b, n - s, 0), 1.0)
        acc_l = acc_b * l_s + acc_l
        acc_b = acc_b * b_s
    out = [None] * (n // SUBLANES)
    for g in reversed(range(n // SUBLANES)):
        rows = slice(g * SUBLANES, (g + 1) * SUBLANES)
        out[g] = acc_l[rows] + acc_b[rows] * carry
        carry = out[g][0:1]
    return jnp.concatenate(out, axis=0)


def _shift_down(ext, k, halo):
    return pltpu.roll(ext, k, 0)[halo:] if k else ext[halo:]


def _shift_up(ext, k, ts):
    return pltpu.roll(ext, ext.shape[0] - k, 0)[:ts] if k else ext[:ts]


def _rows(ts, width, nt=None, col=0):
    if nt is None:
        return pl.BlockSpec((ts, width), lambda i: (i, col))
    return pl.BlockSpec((ts, width), lambda i: (nt - 1 - i, col))


def _resident(shape):
    zeros = (0,) * len(shape)
    return pl.BlockSpec(shape, lambda i: zeros, pipeline_mode=pl.Buffered(1))


def _acc(shape):
    zeros = (0,) * len(shape)
    return pl.BlockSpec(shape, lambda i: zeros)


def _params():
    return pltpu.CompilerParams(dimension_semantics=("arbitrary",), vmem_limit_bytes=VMEM_LIMIT)


def _sds(shape, dtype=F32):
    return jax.ShapeDtypeStruct(shape, dtype)


class _Task:
    def __init__(self, ins, out_shapes, aliases, sems, start, finish, relays=()):
        self.ins, self.out_shapes, self.aliases, self.sems = list(ins), list(out_shapes), dict(aliases), list(sems)
        self.start, self.relays, self.finish = start, list(relays), finish


def _call(body, name, grid, in_specs, out_specs, out_shape, scratch_shapes, args, tasks=()):
    n_in, n_out, n_scr = len(in_specs), len(out_specs), len(scratch_shapes)
    t_in = [len(t.ins) for t in tasks]
    t_out = [len(t.out_shapes) for t in tasks]
    t_sem = [len(t.sems) for t in tasks]
    steps = 1
    for g in grid:
        steps *= g

    def take(refs, pos, counts):
        groups = []
        for c in counts:
            groups.append(refs[pos:pos + c])
            pos += c
        return groups, pos

    def wrapped(*refs):
        (cin,), pos = take(refs, 0, [n_in])
        tin, pos = take(refs, pos, t_in)
        (cout,), pos = take(refs, pos, [n_out])
        tout, pos = take(refs, pos, t_out)
        (cscr,), pos = take(refs, pos, [n_scr])
        tsem, pos = take(refs, pos, t_sem)
        if not grid:
            for t, a, b, c in zip(tasks, tin, tout, tsem):
                t.start(a, b, c)
            if body is not None:
                body(*cin, *cout, *cscr)
            for t, a, b, c in zip(tasks, tin, tout, tsem):
                for relay, _ in t.relays:
                    relay(a, b, c)
            for t, a, b, c in zip(tasks, tin, tout, tsem):
                t.finish(a, b, c)
            return
        step = pl.program_id(0)
        for axis in range(1, len(grid)):
            step = step * grid[axis] + pl.program_id(axis)
        if tasks:
            @pl.when(step == 0)
            def _():
                for t, a, b, c in zip(tasks, tin, tout, tsem):
                    t.start(a, b, c)

        body(*cin, *cout, *cscr)
        for t, a, b, c in zip(tasks, tin, tout, tsem):
            for relay, before in t.relays:
                pl.when(step == max(steps - 1 - before, 0))(functools.partial(relay, a, b, c))

        if tasks:
            @pl.when(step == steps - 1)
            def _():
                for t, a, b, c in zip(tasks, tin, tout, tsem):
                    t.finish(a, b, c)

    aliases, in_pos, out_pos = {}, n_in, n_out
    for t, ni, no in zip(tasks, t_in, t_out):
        aliases.update({in_pos + a: out_pos + b for a, b in t.aliases.items()})
        in_pos, out_pos = in_pos + ni, out_pos + no
    any_spec = pl.BlockSpec(memory_space=pltpu.HBM)
    kwargs = dict(grid=grid, compiler_params=pltpu.CompilerParams(
        dimension_semantics=("arbitrary",) * len(grid), vmem_limit_bytes=VMEM_LIMIT)) if grid else dict(
        compiler_params=pltpu.CompilerParams(vmem_limit_bytes=VMEM_LIMIT))
    out = pl.pallas_call(
        wrapped, name=name,
        in_specs=list(in_specs) + [any_spec] * sum(t_in),
        out_specs=list(out_specs) + [any_spec] * sum(t_out),
        out_shape=list(out_shape) + [s for t in tasks for s in t.out_shapes],
        scratch_shapes=list(scratch_shapes) + [s for t in tasks for s in t.sems],
        input_output_aliases=aliases, **kwargs,
    )(*args, *[pltpu.with_memory_space_constraint(a, pltpu.HBM) for t in tasks for a in t.ins])
    task_outs, pos = take(list(out), n_out, t_out)
    return list(out[:n_out]), task_outs


def _fwd_in_pool(x, g_pre, w_in, pool_w, pool_scale, w_pool_out, ts, tasks=()):
    s = x.shape[0]

    def body(x_ref, g_ref, win_ref, pw_ref, ps_ref, wpo_ref,
             urx_ref, urg_ref, gp_ref, gr_ref, d_ref, yp_ref, h1_ref, halo_scr):
        i = pl.program_id(0)

        @pl.when(i == 0)
        def _():
            halo_scr[...] = jnp.zeros_like(halo_scr)

        h1, _, _ = _rms_fwd(x_ref[...], g_ref[...])
        h1 = h1.astype(BF16)
        h1_ref[...] = h1
        u = jnp.dot(h1, win_ref[0, :, 0:POOL_WIDTH], preferred_element_type=F32)
        for group, out_ref in zip(range(1, len(Z_GROUPS)), (urx_ref, urg_ref, gp_ref, gr_ref)):
            for j, shard_cols, group_cols in _overlaps(group):
                out_ref[:, group_cols] = jnp.dot(h1, win_ref[j, :, shard_cols],
                                                 preferred_element_type=F32).astype(BF16)
        ext = jnp.concatenate([halo_scr[...], u], axis=0)
        halo_scr[...] = u[ts - POOL_HALO:, :]
        t = i * ts + lax.broadcasted_iota(jnp.int32, (ts, 1), 0)
        y4 = []
        for g, w in enumerate(POOL_WINDOWS):
            lanes = slice(g * POOL_GROUP_DIM, (g + 1) * POOL_GROUP_DIM)
            acc = ext[:, lanes]
            sh = 1
            while sh < w:
                acc = acc + pltpu.roll(acc, sh, 0)
                sh *= 2
            inv = 1.0 / jnp.minimum(t + 1, w).astype(F32)
            dg = acc[POOL_HALO:, :] * inv - u[:, lanes]
            d_ref[:, lanes] = dg.astype(BF16)
            y4.append(_dot(dg, pw_ref[g]))
        ypre = jnp.concatenate(y4, axis=1) * ps_ref[...]
        ypre = ypre.astype(BF16)
        for j in range(N_CHIPS):
            yp_ref[:, j * 256:(j + 1) * 256] = jnp.dot(ypre, wpo_ref[j], preferred_element_type=F32)

    return _call(
        body, "fwd_in_pool", (s // ts,),
        [_rows(ts, D_MODEL), _resident((1, D_MODEL)), _resident(w_in.shape), _resident(pool_w.shape),
         _resident((1, POOL_WIDTH)), _resident(w_pool_out.shape)],
        [_rows(ts, D_MODEL)] * 4 + [_rows(ts, POOL_WIDTH), _rows(ts, D_MODEL), _rows(ts, D_MODEL)],
        [_sds((s, D_MODEL), BF16)] * 4 + [_sds((s, POOL_WIDTH), BF16), _sds((s, D_MODEL)), _sds((s, D_MODEL), BF16)],
        [pltpu.VMEM((POOL_HALO, POOL_WIDTH), F32)],
        (x, g_pre, w_in, pool_w, pool_scale, w_pool_out), tasks)


def _fwd_rnn_merge(urx, urg, gp, gr, ypool, x, conv_w, conv_b, wg, bg, lam, w_rg_out, w_o, g_post, ts, tasks=()):
    s = x.shape[0]

    def body(urx_ref, urg_ref, gp_ref, gr_ref, yp_ref, x_ref, cw_ref, cb_ref, wg_ref, bg_ref, lam_ref, wrg_ref, wo_ref,
             gpost_ref, xc_ref, r_ref, ig_ref, h_ref, yr_ref, mo_ref, x1_ref, gl_ref, gg_ref, sp_ref, sr_ref,
             halo_scr, carry_scr):
        i = pl.program_id(0)

        @pl.when(i == 0)
        def _():
            halo_scr[...] = jnp.zeros_like(halo_scr)
            carry_scr[...] = jnp.zeros_like(carry_scr)

        urx_v = urx_ref[...].astype(F32)
        ext = jnp.concatenate([halo_scr[...], urx_v], axis=0)
        halo_scr[...] = urx_v[ts - CONV_HALO:, :]
        cw = cw_ref[...]
        xc = (cb_ref[...] + cw[3:4] * urx_v + cw[2:3] * _shift_down(ext, 1, CONV_HALO)
              + cw[1:2] * _shift_down(ext, 2, CONV_HALO) + cw[0:1] * _shift_down(ext, 3, CONV_HALO))
        xc_ref[...] = xc.astype(BF16)
        xcb = xc.astype(BF16)
        lin = []
        for gate in range(2):
            parts = [jnp.dot(xcb[:, q * GATE_BLOCK:(q + 1) * GATE_BLOCK], wg_ref[gate, q], preferred_element_type=F32)
                     for q in range(GATE_BLOCKS)]
            lin.append(jnp.concatenate(parts, axis=1) + bg_ref[gate:gate + 1, :])
        r = _sigmoid(lin[0])
        ig = _sigmoid(lin[1])
        r_ref[...] = r.astype(BF16)
        ig_ref[...] = ig.astype(BF16)
        first_row = (i * ts + lax.broadcasted_iota(jnp.int32, (ts, 1), 0)) == 0
        _, a, _, mult = _lru_coeffs(r, lam_ref[...], first_row)
        h = _scan_fwd(a, mult * ig * xc, carry_scr[0:1, :])
        carry_scr[0:1, :] = h[ts - 1:ts, :]
        h_ref[...] = h
        urg_v = urg_ref[...].astype(F32)
        gl, gg = _gelu(urg_v)
        gl_ref[...] = gl.astype(BF16)
        gg_ref[...] = gg.astype(BF16)
        yr = _dot(h * gl, wrg_ref[...])
        yr_ref[...] = yr.astype(BF16)
        sp = _sigmoid(gp_ref[...].astype(F32))
        sr = _sigmoid(gr_ref[...].astype(F32))
        sp_ref[...] = sp.astype(BF16)
        sr_ref[...] = sr.astype(BF16)
        merged = sp * yp_ref[...] + sr * yr
        mo = _dot(merged, wo_ref[...])
        mo_ref[...] = mo
        y, _, _ = _rms_fwd(mo, gpost_ref[...])
        x1_ref[...] = x_ref[...] + y

    row = _rows(ts, D_MODEL)
    return _call(
        body, "fwd_rnn_merge", (s // ts,),
        [row] * 6 + [_resident(conv_w.shape), _resident((1, D_MODEL)), _resident(wg.shape), _resident(bg.shape),
                     _resident((1, D_MODEL)), _resident(w_rg_out.shape), _resident(w_o.shape), _resident((1, D_MODEL))],
        [row] * 11, [_sds((s, D_MODEL), dt) for dt in (BF16, BF16, BF16, F32, BF16, F32, F32, BF16, BF16, BF16, BF16)],
        [pltpu.VMEM((CONV_HALO, D_MODEL), F32), pltpu.VMEM((8, D_MODEL), F32)],
        (urx, urg, gp, gr, ypool, x, conv_w, conv_b, wg, bg, lam, w_rg_out, w_o, g_post), tasks)


def _fwd_ffn(x1, g_pre, w_up, fcw, fcb, w_down, g_post, ts, tasks=()):
    s = x1.shape[0]

    def body(x1_ref, g_ref, wup_ref, fcw_ref, fcb_ref, wd_ref, gpost_ref,
             up_ref, gl_ref, gg_ref, h2_ref, dn_ref, x2_ref, up_scr, halo_scr):
        i = pl.program_id(0)

        @pl.when(i == 0)
        def _():
            halo_scr[...] = jnp.zeros_like(halo_scr)

        x1_v = x1_ref[...]
        h2, _, _ = _rms_fwd(x1_v, g_ref[...])
        h2 = h2.astype(BF16)
        h2_ref[...] = h2
        for j in range(N_CHIPS):
            up_scr[:, j * UP_SHARD:(j + 1) * UP_SHARD] = jnp.dot(h2, wup_ref[j], preferred_element_type=F32)
        up_ref[...] = up_scr[...].astype(BF16)
        ug = up_scr[:, 0:D_FF]
        ext = jnp.concatenate([halo_scr[...], ug], axis=0)
        halo_scr[...] = ug[ts - CONV_HALO:, :]
        w = fcw_ref[...]
        gh = (fcb_ref[...] + w[2:3] * ug + w[1:2] * _shift_down(ext, 1, CONV_HALO)
              + w[0:1] * _shift_down(ext, 2, CONV_HALO))
        gl, gg = _gelu(gh)
        gl_ref[...] = gl.astype(BF16)
        gg_ref[...] = gg.astype(BF16)
        dn = _dot(gl * up_scr[:, D_FF:], wd_ref[...])
        dn_ref[...] = dn
        y, _, _ = _rms_fwd(dn, gpost_ref[...])
        x2_ref[...] = x1_v + y

    row = _rows(ts, D_MODEL)
    return _call(
        body, "fwd_ffn", (s // ts,),
        [row, _resident((1, D_MODEL)), _resident(w_up.shape), _resident(fcw.shape), _resident((1, D_FF)),
         _resident(w_down.shape), _resident((1, D_MODEL))],
        [_rows(ts, 2 * D_FF), _rows(ts, D_FF), _rows(ts, D_FF), row, row, row],
        [_sds((s, 2 * D_FF), BF16), _sds((s, D_FF), BF16), _sds((s, D_FF), BF16), _sds((s, D_MODEL), BF16),
         _sds((s, D_MODEL)), _sds((s, D_MODEL))],
        [pltpu.VMEM((ts, 2 * D_FF), F32), pltpu.VMEM((CONV_HALO, D_FF), F32)],
        (x1, g_pre, w_up, fcw, fcb, w_down, g_post), tasks)


def _ple_loss(x2, p, tgt, g_gate, w_gate, w_proj, g_post, ts):
    s = x2.shape[0]

    def body(x2_ref, p_ref, t_ref, gg_ref, wg_ref, wp_ref, gp_ref, dx2_ref, loss_ref, dwg_ref, dwp_ref, dgg_ref, dgp_ref):
        @pl.when(pl.program_id(0) == 0)
        def _():
            loss_ref[...] = jnp.zeros_like(loss_ref)
            dwg_ref[...] = jnp.zeros_like(dwg_ref)
            dwp_ref[...] = jnp.zeros_like(dwp_ref)
            dgg_ref[...] = jnp.zeros_like(dgg_ref)
            dgp_ref[...] = jnp.zeros_like(dgp_ref)

        x2_v = x2_ref[...]
        n3, xh3, r3 = _rms_fwd(x2_v, gg_ref[...])
        pg = _sigmoid(_dot(n3, wg_ref[...]))
        pb = p_ref[...].astype(BF16)
        q = jnp.concatenate([jnp.dot(pb, wp_ref[j], preferred_element_type=F32) for j in range(N_CHIPS)], axis=1)
        ple, qh, rq = _rms_fwd(q, gp_ref[...])
        e = x2_v + pg * ple - t_ref[...]
        loss_ref[...] += 0.5 * jnp.sum(jnp.mean(e * e, axis=-1, keepdims=True), axis=0, keepdims=True)
        dy = e * (1.0 / D_MODEL)
        dpgl = dy * ple * pg * (1.0 - pg)
        dwg_ref[...] += _dot_tn(n3, dpgl)
        dx3, dgg = _rms_bwd(xh3, r3, gg_ref[...], _dot_nt(dpgl, wg_ref[...]))
        dgg_ref[...] += dgg
        dq, dgp = _rms_bwd(qh, rq, gp_ref[...], dy * pg)
        dgp_ref[...] += dgp
        for j in range(N_CHIPS):
            dwp_ref[j] += _dot_tn(pb, dq[:, j * 256:(j + 1) * 256])
        dx2_ref[...] = dy + dx3

    row = _rows(ts, D_MODEL)
    vec = _acc((1, D_MODEL))
    return pl.pallas_call(
        body, name="ple_loss", grid=(s // ts,),
        in_specs=[row, _rows(ts, PLE_DIM), row, _resident((1, D_MODEL)), _resident(w_gate.shape), _resident(w_proj.shape),
                  _resident((1, D_MODEL))],
        out_specs=[row, _acc((1, 128)), _acc(w_gate.shape), _acc(w_proj.shape), vec, vec],
        out_shape=[_sds((s, D_MODEL)), _sds((1, 128)), _sds(w_gate.shape), _sds(w_proj.shape), _sds((1, D_MODEL)),
                   _sds((1, D_MODEL))],
        compiler_params=_params(),
    )(x2, p, tgt, g_gate, w_gate, w_proj, g_post)


def _bwd_ffn_down(dx2, dn, up, gl, gg, fcw, w_down, g_post, ts):
    s = dx2.shape[0]
    nt = s // ts

    def body(dx2_ref, dn_ref, up_ref, gl_ref, gg_ref, fcw_ref, wd_ref, gpost_ref,
             dup_ref, dwd_ref, dfcw_ref, dfcb_ref, dgp_ref, carry_scr):
        i = pl.program_id(0)

        @pl.when(i == 0)
        def _():
            carry_scr[...] = jnp.zeros_like(carry_scr)
            dwd_ref[...] = jnp.zeros_like(dwd_ref)
            dfcw_ref[...] = jnp.zeros_like(dfcw_ref)
            dfcb_ref[...] = jnp.zeros_like(dfcb_ref)
            dgp_ref[...] = jnp.zeros_like(dgp_ref)

        _, xh, r = _rms_fwd(dn_ref[...], gpost_ref[...])
        ddn, dgp = _rms_bwd(xh, r, gpost_ref[...], dx2_ref[...])
        dgp_ref[...] += dgp
        dhid = _dot_nt(ddn, wd_ref[...])
        ug = up_ref[:, 0:D_FF].astype(F32)
        uv = up_ref[:, D_FF:].astype(F32)
        gl = gl_ref[...].astype(F32)
        w = fcw_ref[...]
        dwd_ref[...] += _dot_tn(gl * uv, ddn)
        dgh = dhid * uv * gg_ref[...].astype(F32)
        dup_ref[:, D_FF:] = (dhid * gl).astype(BF16)
        extd = jnp.concatenate([dgh, carry_scr[...]], axis=0)
        carry_scr[...] = dgh[0:CONV_HALO, :]
        d1 = _shift_up(extd, 1, ts)
        d2 = _shift_up(extd, 2, ts)
        dup_ref[:, 0:D_FF] = (w[2:3] * dgh + w[1:2] * d1 + w[0:1] * d2).astype(BF16)
        dfcw_ref[2:3, :] += jnp.sum(ug * dgh, axis=0, keepdims=True)
        dfcw_ref[1:2, :] += jnp.sum(ug * d1, axis=0, keepdims=True)
        dfcw_ref[0:1, :] += jnp.sum(ug * d2, axis=0, keepdims=True)
        dfcb_ref[...] += jnp.sum(dgh, axis=0, keepdims=True)

    row = _rows(ts, D_MODEL, nt)
    wide = _rows(ts, D_FF, nt)
    return pl.pallas_call(
        body, name="bwd_ffn_down", grid=(nt,),
        in_specs=[row, row, _rows(ts, 2 * D_FF, nt), wide, wide, _resident(fcw.shape), _resident(w_down.shape),
                  _resident((1, D_MODEL))],
        out_specs=[_rows(ts, 2 * D_FF, nt), _acc(w_down.shape), _acc(fcw.shape), _acc((1, D_FF)), _acc((1, D_MODEL))],
        out_shape=[_sds((s, 2 * D_FF), BF16), _sds(w_down.shape), _sds(fcw.shape), _sds((1, D_FF)), _sds((1, D_MODEL))],
        scratch_shapes=[pltpu.VMEM((CONV_HALO, D_FF), F32)],
        compiler_params=_params(),
    )(dx2, dn, up, gl, gg, fcw, w_down, g_post)


def _bwd_ffn_up(dup, x1, dx2, g_pre, w_up, ts, tasks=()):
    s = x1.shape[0]

    def body(dup_ref, x1_ref, dx2_ref, g_ref, wup_ref, dx1_ref, dg_ref):
        @pl.when(pl.program_id(0) == 0)
        def _():
            dg_ref[...] = jnp.zeros_like(dg_ref)

        _, xh, r = _rms_fwd(x1_ref[...], g_ref[...])
        dh2 = _dot_nt(dup_ref[:, 0:UP_SHARD], wup_ref[0])
        for j in range(1, N_CHIPS):
            dh2 = dh2 + _dot_nt(dup_ref[:, j * UP_SHARD:(j + 1) * UP_SHARD], wup_ref[j])
        dx, dg = _rms_bwd(xh, r, g_ref[...], dh2)
        dg_ref[...] += dg
        dx1_ref[...] = dx2_ref[...] + dx

    row = _rows(ts, D_MODEL)
    return _call(
        body, "bwd_ffn_up", (s // ts,),
        [_rows(ts, 2 * D_FF), row, row, _resident((1, D_MODEL)), _resident(w_up.shape)],
        [row, _acc((1, D_MODEL))], [_sds((s, D_MODEL)), _sds((1, D_MODEL))], [],
        (dup, x1, dx2, g_pre, w_up), tasks)


def _dw_up(h2, dup, ts, tasks=()):
    s = h2.shape[0]
    ts = min(DW_TILES * ts, s)
    nt = s // ts
    half = D_MODEL // 2

    def body(h2_ref, dup_ref, send_ref, acc_ref, theirs_scr, mine_scr, got_scr, sems):
        j, o, i = pl.program_id(0), pl.program_id(1), pl.program_id(2)
        x, y, c, me, _ = _place()
        prod = _dot_tn(h2_ref[...], dup_ref[...])
        for scr, which in ((theirs_scr, 0), (mine_scr, 1)):
            @pl.when((o == which) & (i == 0))
            def _():
                scr[...] = prod

            @pl.when((o == which) & (i > 0))
            def _():
                scr[...] += prod

        slot = j % 2
        push = _remote(theirs_scr, got_scr.at[slot], sems.at[0, slot], sems.at[1, slot], (x, y, 1 - c))
        pl.when((o == 1) & (i == 0))(push.start)

        @pl.when((o == 1) & (i == nt - 1))
        def _():
            push.wait()
            part = (mine_scr[...] + got_scr[slot]).astype(BF16)
            send_ref[0] = part

            @pl.when(me == j)
            def _():
                acc_ref[0] = part

    def h2_cols(j, o, i):
        c = lax.axis_index("c")
        return i, jnp.where(o == 0, 1 - c, c)

    def own_slot(j, o, i):
        return 2 * lax.axis_index("x") + lax.axis_index("y"), 0, 0

    block = (1, half, UP_SHARD)
    return _call(
        body, "dw_up", (N_CHIPS, 2, nt),
        [pl.BlockSpec((ts, half), h2_cols), pl.BlockSpec((ts, UP_SHARD), lambda j, o, i: (i, j))],
        [pl.BlockSpec(block, lambda j, o, i: (j, 0, 0)), pl.BlockSpec(block, own_slot)],
        [_sds((N_CHIPS, half, UP_SHARD), BF16)] * 2,
        [pltpu.VMEM((half, UP_SHARD), F32), pltpu.VMEM((half, UP_SHARD), F32), pltpu.VMEM((2, half, UP_SHARD), F32),
         pltpu.SemaphoreType.DMA((2, 2))],
        (h2, dup), tasks)


def _bwd_merge(dx1, mo, sp, sr, ypool, yrnn, g_post, w_o, ts, tasks=()):
    s = dx1.shape[0]

    def body(dx1_ref, mo_ref, sp_ref, sr_ref, yp_ref, yr_ref, g_ref, wo_ref,
             dgp_ref, dgr_ref, dyp_ref, dyr_ref, dwo_ref, dg_ref):
        @pl.when(pl.program_id(0) == 0)
        def _():
            dwo_ref[...] = jnp.zeros_like(dwo_ref)
            dg_ref[...] = jnp.zeros_like(dg_ref)

        _, xh, r = _rms_fwd(mo_ref[...], g_ref[...])
        dmo, dg = _rms_bwd(xh, r, g_ref[...], dx1_ref[...])
        dg_ref[...] += dg
        dmerged = _dot_nt(dmo, wo_ref[...])
        sp = sp_ref[...].astype(F32)
        sr = sr_ref[...].astype(F32)
        yp = yp_ref[...]
        yr = yr_ref[...].astype(F32)
        dwo_ref[...] += _dot_tn(sp * yp + sr * yr, dmo)
        dgp_ref[...] = (dmerged * yp * sp * (1.0 - sp)).astype(BF16)
        dgr_ref[...] = (dmerged * yr * sr * (1.0 - sr)).astype(BF16)
        dyp_ref[...] = (dmerged * sp).astype(BF16)
        dyr_ref[...] = (dmerged * sr).astype(BF16)

    row = _rows(ts, D_MODEL)
    return _call(
        body, "bwd_merge", (s // ts,),
        [row] * 6 + [_resident((1, D_MODEL)), _resident(w_o.shape)],
        [row] * 4 + [_acc(w_o.shape), _acc((1, D_MODEL))],
        [_sds((s, D_MODEL), BF16)] * 4 + [_sds(w_o.shape), _sds((1, D_MODEL))], [],
        (dx1, mo, sp, sr, ypool, yrnn, g_post, w_o), tasks)


def _bwd_rnn(dyr, urx, gl, gg, xc, r, ig, h, conv_w, wg, lam, w_rg_out, ts, tasks=()):
    s = urx.shape[0]
    nt = s // ts
    halo_blocks = ts // CONV_HALO

    def body(dyr_ref, urx_ref, gl_ref, gg_ref, xc_ref, r_ref, ig_ref, h_ref, hh_ref, cw_ref, wg_ref, lam_ref, wrg_ref,
             durx_ref, durg_ref, dwrg_ref, dwg_ref, dcw_ref, dcb_ref, dbg_ref, dlam_ref, mu_scr, carry_scr):
        i = pl.program_id(0)
        k = nt - 1 - i

        @pl.when(i == 0)
        def _():
            mu_scr[...] = jnp.zeros_like(mu_scr)
            carry_scr[...] = jnp.zeros_like(carry_scr)
            dwrg_ref[...] = jnp.zeros_like(dwrg_ref)
            dwg_ref[...] = jnp.zeros_like(dwg_ref)
            dcw_ref[...] = jnp.zeros_like(dcw_ref)
            dcb_ref[...] = jnp.zeros_like(dcb_ref)
            dbg_ref[...] = jnp.zeros_like(dbg_ref)
            dlam_ref[...] = jnp.zeros_like(dlam_ref)

        row = lax.broadcasted_iota(jnp.int32, (ts, 1), 0)
        first_row = (k * ts + row) == 0
        h = h_ref[...]
        dyr_v = dyr_ref[...]
        dhr = _dot_nt(dyr_v, wrg_ref[...])
        gl = gl_ref[...].astype(F32)
        dwrg_ref[...] += _dot_tn(h * gl, dyr_v)
        durg_ref[...] = (dhr * h * gg_ref[...].astype(F32)).astype(BF16)
        r_v = r_ref[...].astype(F32)
        ig_v = ig_ref[...].astype(F32)
        xc_v = xc_ref[...].astype(F32)
        lam_v = lam_ref[...]
        c8, a, m2, mult = _lru_coeffs(r_v, lam_v, first_row)
        b = jnp.where(row == ts - 1, 1.0, pltpu.roll(a, ts - 1, 0))
        lt = _scan_bwd(b, dhr * gl, mu_scr[0:1, :])
        mu_scr[0:1, :] = a[0:1, :] * lt[0:1, :]
        h_before = jnp.where(k > 0, hh_ref[CONV_HALO - 1:CONV_HALO, :], 0.0)
        hprev = jnp.where(row == 0, h_before, pltpu.roll(h, 1, 0))
        dmult = lt * ig_v * xc_v
        da = lt * hprev - jnp.where(first_row, 0.0, dmult * a * lax.rsqrt(m2))
        dla = da * a
        dlam_ref[...] += jnp.sum(dla * r_v, axis=0, keepdims=True)
        dlr = (dla * (-c8)) * r_v * (1.0 - r_v)
        dli = (lt * mult * xc_v) * ig_v * (1.0 - ig_v)
        dbg_ref[0:1, :] += jnp.sum(dlr, axis=0, keepdims=True)
        dbg_ref[1:2, :] += jnp.sum(dli, axis=0, keepdims=True)
        xcb = xc_v.astype(BF16)
        parts = []
        for q in range(GATE_BLOCKS):
            blk = slice(q * GATE_BLOCK, (q + 1) * GATE_BLOCK)
            dlr_q = dlr[:, blk].astype(BF16)
            dli_q = dli[:, blk].astype(BF16)
            parts.append(_dot_nt(dlr_q, wg_ref[0, q]) + _dot_nt(dli_q, wg_ref[1, q]))
            dwg_ref[0, q] += _dot_tn(xcb[:, blk], dlr_q)
            dwg_ref[1, q] += _dot_tn(xcb[:, blk], dli_q)
        dxc = lt * mult * ig_v + jnp.concatenate(parts, axis=1)
        extd = jnp.concatenate([dxc, carry_scr[...]], axis=0)
        carry_scr[...] = dxc[0:CONV_HALO, :]
        cw = cw_ref[...]
        urx_v = urx_ref[...].astype(F32)
        durx = cw[3:4] * dxc
        dcw_ref[3:4, :] += jnp.sum(urx_v * dxc, axis=0, keepdims=True)
        for j in (1, 2, 3):
            dj = _shift_up(extd, j, ts)
            durx = durx + cw[3 - j:4 - j] * dj
            dcw_ref[3 - j:4 - j, :] += jnp.sum(urx_v * dj, axis=0, keepdims=True)
        durx_ref[...] = durx.astype(BF16)
        dcb_ref[...] += jnp.sum(dxc, axis=0, keepdims=True)

        @pl.when(i == nt - 1)
        def _():
            dlam_ref[...] = dlam_ref[...] * (LRU_C * jax.nn.sigmoid(-lam_v))

    row_spec = _rows(ts, D_MODEL, nt)
    halo_spec = pl.BlockSpec((CONV_HALO, D_MODEL), lambda i: (jnp.maximum((nt - 1 - i) * halo_blocks - 1, 0), 0))
    vec = _acc((1, D_MODEL))
    return _call(
        body, "bwd_rnn", (nt,),
        [row_spec] * 8 + [halo_spec, _resident(conv_w.shape), _resident(wg.shape), _resident((1, D_MODEL)),
                          _resident(w_rg_out.shape)],
        [row_spec, row_spec, _acc(w_rg_out.shape), _acc(wg.shape), _acc(conv_w.shape), vec, _acc((2, D_MODEL)), vec],
        [_sds((s, D_MODEL), BF16), _sds((s, D_MODEL), BF16), _sds(w_rg_out.shape), _sds(wg.shape), _sds(conv_w.shape),
         _sds((1, D_MODEL)), _sds((2, D_MODEL)), _sds((1, D_MODEL))],
        [pltpu.VMEM((8, D_MODEL), F32), pltpu.VMEM((CONV_HALO, D_MODEL), F32)],
        (dyr, urx, gl, gg, xc, r, ig, h, h, conv_w, wg, lam, w_rg_out), tasks)


def _bwd_pool(dyp, d, pool_w, pool_scale, w_pool_out, ts, tasks=()):
    s = d.shape[0]
    nt = s // ts

    def body(dyp_ref, d_ref, pw_ref, ps_ref, wpo_ref, dzp_ref, dwpo_ref, dpw_ref, dps_ref, carry_scr):
        i = pl.program_id(0)
        k = nt - 1 - i

        @pl.when(i == 0)
        def _():
            carry_scr[...] = jnp.zeros_like(carry_scr)
            dwpo_ref[...] = jnp.zeros_like(dwpo_ref)
            dpw_ref[...] = jnp.zeros_like(dpw_ref)
            dps_ref[...] = jnp.zeros_like(dps_ref)

        dyp_v = dyp_ref[...]
        d_v = d_ref[...]
        ps = ps_ref[...]
        dypre = _dot_nt(dyp_v[:, 0:256], wpo_ref[0])
        for j in range(1, N_CHIPS):
            dypre = dypre + _dot_nt(dyp_v[:, j * 256:(j + 1) * 256], wpo_ref[j])
        y4 = jnp.concatenate([_dot(d_v[:, g * 128:(g + 1) * 128], pw_ref[g]) for g in range(POOL_GROUPS)], axis=1)
        ypre = (y4 * ps).astype(BF16)
        for j in range(N_CHIPS):
            dwpo_ref[j] += _dot_tn(ypre, dyp_v[:, j * 256:(j + 1) * 256])
        dps_ref[...] += jnp.sum(dypre * y4, axis=0, keepdims=True)
        dy4 = dypre * ps
        t = k * ts + lax.broadcasted_iota(jnp.int32, (ts, 1), 0)
        for g, w in enumerate(POOL_WINDOWS):
            lanes = slice(g * POOL_GROUP_DIM, (g + 1) * POOL_GROUP_DIM)
            dd = _dot_nt(dy4[:, lanes], pw_ref[g])
            dpw_ref[g] += _dot_tn(d_v[:, lanes], dy4[:, lanes])
            e = dd * (1.0 / jnp.minimum(t + 1, w).astype(F32))
            acc = jnp.concatenate([e, carry_scr[:, lanes]], axis=0)
            carry_scr[:, lanes] = e[0:POOL_HALO, :]
            n = ts + POOL_HALO
            sh = 1
            while sh < w:
                acc = acc + pltpu.roll(acc, n - sh, 0)
                sh *= 2
            dzp_ref[:, lanes] = (acc[:ts, :] - dd).astype(BF16)

    return _call(
        body, "bwd_pool", (nt,),
        [_rows(ts, D_MODEL, nt), _rows(ts, POOL_WIDTH, nt), _resident(pool_w.shape), _resident((1, POOL_WIDTH)),
         _resident(w_pool_out.shape)],
        [_rows(ts, POOL_WIDTH, nt), _acc(w_pool_out.shape), _acc(pool_w.shape), _acc((1, POOL_WIDTH))],
        [_sds((s, POOL_WIDTH), BF16), _sds(w_pool_out.shape), _sds(pool_w.shape), _sds((1, POOL_WIDTH))],
        [pltpu.VMEM((POOL_HALO, POOL_WIDTH), F32)],
        (dyp, d, pool_w, pool_scale, w_pool_out), tasks)


def _assemble_dz(dz_scr, dzp_ref, durx_ref, durg_ref, dgp_ref, dgr_ref):
    dz_scr[:, 0:512] = dzp_ref[...]
    dz_scr[:, 512:1536] = durx_ref[...]
    dz_scr[:, 1536:2560] = durg_ref[...]
    dz_scr[:, 2560:3584] = dgp_ref[...]
    dz_scr[:, 3584:4608] = dgr_ref[...]


def _dw_in(h1, dzp, durx, durg, dgp, dgr, ts, tasks=()):
    s = h1.shape[0]
    ts = min(2 * ts, s)
    nt = s // ts
    half = D_MODEL // 2

    def body(h1_ref, dzp_ref, durx_ref, durg_ref, dgp_ref, dgr_ref, send_ref, acc_ref, theirs_ref, mine_ref, got_ref, sems):
        o, i = pl.program_id(0), pl.program_id(1)

        @pl.when((o == 0) & (i == 0))
        def _():
            theirs_ref[...] = jnp.zeros_like(theirs_ref)
            mine_ref[...] = jnp.zeros_like(mine_ref)

        groups = (dzp_ref, durx_ref, durg_ref, dgp_ref, dgr_ref)
        for out_ref, which in ((theirs_ref, 0), (mine_ref, 1)):
            @pl.when(o == which)
            def _():
                for group, dz_ref in enumerate(groups):
                    for j, shard_cols, group_cols in _overlaps(group):
                        out_ref[j, :, shard_cols] += _dot_tn(h1_ref[...], dz_ref[:, group_cols])

        x, y, c, me, _ = _place()
        send = _remote(theirs_ref, got_ref, sems.at[0], sems.at[1], (x, y, 1 - c))
        pl.when((o == 1) & (i == 0))(send.start)

        @pl.when((o == 1) & (i == nt - 1))
        def _():
            send.wait()
            for k in range(N_CHIPS):
                part = (mine_ref[k] + got_ref[k]).astype(BF16)
                send_ref[k] = part

                @pl.when(me == k)
                def _():
                    acc_ref[k] = part

    def h1_cols(o, i):
        c = lax.axis_index("c")
        return i, jnp.where(o == 0, 1 - c, c)

    rows = lambda width: pl.BlockSpec((ts, width), lambda o, i: (i, 0))
    shape = (N_CHIPS, half, IN_SHARD)
    whole = pl.BlockSpec(shape, lambda o, i: (0, 0, 0))
    return _call(
        body, "dw_in", (2, nt), [pl.BlockSpec((ts, half), h1_cols), rows(POOL_WIDTH)] + [rows(D_MODEL)] * 4,
        [whole, whole], [_sds(shape, BF16)] * 2,
        [pltpu.VMEM(shape, F32)] * 3 + [pltpu.SemaphoreType.DMA((2,))], (h1, dzp, durx, durg, dgp, dgr), tasks)


def _bwd_in(dzp, durx, durg, dgp, dgr, x, dx1, g_pre, w_in, ts, tasks=()):
    s = x.shape[0]

    def body(dzp_ref, durx_ref, durg_ref, dgp_ref, dgr_ref, x_ref, dx1_ref, g_ref, win_ref, gx_ref, dg_ref, dz_scr):
        @pl.when(pl.program_id(0) == 0)
        def _():
            dg_ref[...] = jnp.zeros_like(dg_ref)

        _assemble_dz(dz_scr, dzp_ref, durx_ref, durg_ref, dgp_ref, dgr_ref)
        _, xh, r = _rms_fwd(x_ref[...], g_ref[...])
        dh1 = _dot_nt(dz_scr[:, 0:IN_SHARD], win_ref[0])
        for j in range(1, N_CHIPS):
            dh1 = dh1 + _dot_nt(dz_scr[:, j * IN_SHARD:(j + 1) * IN_SHARD], win_ref[j])
        dx, dg = _rms_bwd(xh, r, g_ref[...], dh1)
        dg_ref[...] += dg
        gx_ref[...] = dx1_ref[...] + dx

    row = _rows(ts, D_MODEL)
    return _call(
        body, "bwd_in", (s // ts,),
        [_rows(ts, POOL_WIDTH)] + [row] * 6 + [_resident((1, D_MODEL)), _resident(w_in.shape)],
        [row, _acc((1, D_MODEL))], [_sds((s, D_MODEL)), _sds((1, D_MODEL))],
        [pltpu.VMEM((ts, IN_TOTAL), BF16)], (dzp, durx, durg, dgp, dgr, x, dx1, g_pre, w_in), tasks)


def _place():
    x, y, c = lax.axis_index("x"), lax.axis_index("y"), lax.axis_index("c")
    others = [(1 - x, y), (x, 1 - y), (1 - x, 1 - y)]
    return x, y, c, 2 * x + y, others


def _remote(src, dst, send_sem, recv_sem, to):
    return pltpu.make_async_remote_copy(src_ref=src, dst_ref=dst, send_sem=send_sem, recv_sem=recv_sem,
                                        device_id=to, device_id_type=MESH)


def _own_slots(ws, dtypes, name, tasks=()):
    n = len(ws)
    hbm = pl.BlockSpec(memory_space=pltpu.HBM)

    def body(*refs):
        srcs, outs, f32_bufs, cast_bufs, sems = refs[:n], refs[n:2 * n], refs[2 * n:3 * n], refs[3 * n:4 * n], refs[4 * n]
        me = _place()[3]
        loads = [pltpu.make_async_copy(srcs[k], f32_bufs[k], sems.at[k, 0]) for k in range(n)]
        stores = [pltpu.make_async_copy(cast_bufs[k], outs[k].at[me], sems.at[k, 1]) for k in range(n)]
        for cp in loads:
            cp.start()
        for k in range(n):
            loads[k].wait()
            cast_bufs[k][...] = f32_bufs[k][...].astype(dtypes[k])
            stores[k].start()
        for cp in stores:
            cp.wait()

    return _call(
        body, name, (), [hbm] * n, [hbm] * n, [_sds((N_CHIPS,) + w.shape, dt) for w, dt in zip(ws, dtypes)],
        [pltpu.VMEM(w.shape, F32) for w in ws] + [pltpu.VMEM(w.shape, dt) for w, dt in zip(ws, dtypes)]
        + [pltpu.SemaphoreType.DMA((n, 2))],
        [pltpu.with_memory_space_constraint(w, pltpu.HBM) for w in ws], tasks)


def _run(tasks, name):
    if isinstance(tasks, _Task):
        return _call(None, name, (), [], [], [], [], (), (tasks,))[1][0]
    return _call(None, name, (), [], [], [], [], (), tuple(tasks))[1]


def _gather_task(bufs, relay_steps=(0, 0)):
    n = len(bufs)
    NBR_X, NBR_Y, QUARTER_VIA_Y, QUARTER_VIA_X, SIB_X, SIB_Y, SIB_DIAG = range(7)

    def parts(out):
        x, y, c, me, _ = _place()
        ah = out.shape[1] // 2
        q = ah // 2 if (ah // 2) % 16 == 0 else ah
        return c * ah, ah, q

    def copy(out, w, k, chip, row0, rows, to, sems):
        slot = out.at[chip, pl.ds(row0, rows)]
        return _remote(slot, slot, sems[0].at[w, k], sems[1].at[w, k], to)

    def plan(out, w, sems):
        x, y, c, me, _ = _place()
        row0, ah, q = parts(out)
        xn, yn, dg = 2 * (1 - x) + y, 2 * x + (1 - y), 2 * (1 - x) + (1 - y)
        to_x, to_y, sib = (1 - x, y, c), (x, 1 - y, c), (x, y, 1 - c)
        other = (1 - c) * ah
        cp = functools.partial(copy, out, w, sems=sems)
        sends = {NBR_X: cp(NBR_X, me, row0, ah, to_x), NBR_Y: cp(NBR_Y, me, row0, ah, to_y),
                 QUARTER_VIA_Y: cp(QUARTER_VIA_Y, xn, row0, q, to_y), SIB_X: cp(SIB_X, xn, row0, ah, sib),
                 SIB_Y: cp(SIB_Y, yn, row0, ah, sib), SIB_DIAG: cp(SIB_DIAG, dg, row0, ah, sib)}
        lands = {NBR_X: cp(NBR_X, xn, row0, ah, to_x), NBR_Y: cp(NBR_Y, yn, row0, ah, to_y),
                 QUARTER_VIA_Y: cp(QUARTER_VIA_Y, dg, row0, q, to_y), SIB_X: cp(SIB_X, xn, other, ah, sib),
                 SIB_Y: cp(SIB_Y, yn, other, ah, sib), SIB_DIAG: cp(SIB_DIAG, dg, other, ah, sib)}
        if q < ah:
            sends[QUARTER_VIA_X] = cp(QUARTER_VIA_X, yn, row0 + q, ah - q, to_x)
            lands[QUARTER_VIA_X] = cp(QUARTER_VIA_X, dg, row0 + q, ah - q, to_x)
        return sends, lands

    def start(ins, outs, sems):
        for w, out in enumerate(outs):
            sends, _ = plan(out, w, sems)
            sends[NBR_X].start()
            sends[NBR_Y].start()

    def pass_neighbours(ins, outs, sems):
        for w, out in enumerate(outs):
            sends, lands = plan(out, w, sems)
            lands[NBR_X].wait_recv()
            sends[QUARTER_VIA_Y].start()
            sends[SIB_X].start()
            lands[NBR_Y].wait_recv()
            if QUARTER_VIA_X in sends:
                sends[QUARTER_VIA_X].start()
            sends[SIB_Y].start()

    def pass_diagonal(ins, outs, sems):
        for w, out in enumerate(outs):
            sends, lands = plan(out, w, sems)
            lands[QUARTER_VIA_Y].wait_recv()
            if QUARTER_VIA_X in lands:
                lands[QUARTER_VIA_X].wait_recv()
            sends[SIB_DIAG].start()

    def finish(ins, outs, sems):
        for w, out in enumerate(outs):
            sends, lands = plan(out, w, sems)
            for k in (SIB_X, SIB_Y, SIB_DIAG):
                lands[k].wait_recv()
        for w, out in enumerate(outs):
            sends, _ = plan(out, w, sems)
            for cp in sends.values():
                cp.wait_send()

    return _Task(bufs, [_sds(b.shape, b.dtype) for b in bufs], {i: i for i in range(n)},
                 [pltpu.SemaphoreType.DMA((n, 7)), pltpu.SemaphoreType.DMA((n, 7))], start, finish,
                 [(pass_neighbours, relay_steps[0]), (pass_diagonal, relay_steps[1])])


def _halves_task(grads):
    n = len(grads)

    def copy(src, out, w, sems):
        x, y, c, _, _ = _place()
        ah = out.shape[1]
        return _remote(src.at[:, pl.ds((1 - c) * ah, ah)], out, sems[0].at[w], sems[1].at[w], (x, y, 1 - c))

    def start(ins, outs, sems):
        for w, (src, out) in enumerate(zip(ins, outs)):
            copy(src, out, w, sems).start()

    def finish(ins, outs, sems):
        for w, (src, out) in enumerate(zip(ins, outs)):
            copy(src, out, w, sems).wait()

    return _Task(grads, [_sds((g.shape[0], g.shape[1] // 2, g.shape[2]), g.dtype) for g in grads], {},
                 [pltpu.SemaphoreType.DMA((n,)), pltpu.SemaphoreType.DMA((n,))], start, finish)


def _exchange_task(sends, accs):
    n = len(accs)
    given = [s for s in sends if s is not None]

    def copies(ins, outs, sems):
        send_refs = iter(ins[:len(given)])
        srcs = [next(send_refs) if s is not None else None for s in sends]
        x, y, c, me, others = _place()
        for w, out in enumerate(outs):
            for j, (ox, oy) in enumerate(others):
                src = out.at[me] if srcs[w] is None else srcs[w].at[2 * ox + oy]
                yield _remote(src, out.at[me], sems[0].at[w, j], sems[1].at[w, j], (ox, oy, c))

    def start(ins, outs, sems):
        for cp in copies(ins, outs, sems):
            cp.start()

    def finish(ins, outs, sems):
        x, y, c, _, others = _place()
        for w, out in enumerate(outs):
            for j, (ox, oy) in enumerate(others):
                slot = out.at[2 * ox + oy]
                _remote(slot, slot, sems[0].at[w, j], sems[1].at[w, j], (ox, oy, c)).wait_recv()
        for cp in copies(ins, outs, sems):
            cp.wait_send()

    return _Task(given + list(accs), [_sds(a.shape, a.dtype) for a in accs], {len(given) + i: i for i in range(n)},
                 [pltpu.SemaphoreType.DMA((n, 3)), pltpu.SemaphoreType.DMA((n, 3))], start, finish)


def _swap_task(arrays):
    n = len(arrays)

    def copy(src, out, w, sems):
        x, y, c, _, _ = _place()
        return _remote(src, out, sems[0].at[w], sems[1].at[w], (x, y, 1 - c))

    def start(ins, outs, sems):
        for w, (src, out) in enumerate(zip(ins, outs)):
            copy(src, out, w, sems).start()

    def finish(ins, outs, sems):
        for w, (src, out) in enumerate(zip(ins, outs)):
            copy(src, out, w, sems).wait()

    return _Task(arrays, [_sds(a.shape, a.dtype) for a in arrays], {},
                 [pltpu.SemaphoreType.DMA((n,)), pltpu.SemaphoreType.DMA((n,))], start, finish)


def _all_devices_task(arrays):
    n = len(arrays)
    flips = [(dx, dy, dc) for dx in (0, 1) for dy in (0, 1) for dc in (0, 1)][1:]

    def peers():
        x, y, c, _, _ = _place()
        flip = lambda v, d: 1 - v if d else v
        return 4 * x + 2 * y + c, [(flip(x, dx), flip(y, dy), flip(c, dc)) for dx, dy, dc in flips]

    def start(ins, outs, sems):
        me, others = peers()
        for w, (src, out) in enumerate(zip(ins, outs)):
            pltpu.make_async_copy(src, out.at[me], sems[2].at[w]).start()
            for k, peer in enumerate(others):
                _remote(src, out.at[me], sems[0].at[w, k], sems[1].at[w, k], peer).start()

    def finish(ins, outs, sems):
        me, others = peers()
        for w, (src, out) in enumerate(zip(ins, outs)):
            for k, (px, py, pc) in enumerate(others):
                slot = out.at[4 * px + 2 * py + pc]
                _remote(slot, slot, sems[0].at[w, k], sems[1].at[w, k], (px, py, pc)).wait_recv()
            for k, peer in enumerate(others):
                _remote(src, out.at[me], sems[0].at[w, k], sems[1].at[w, k], peer).wait_send()
            pltpu.make_async_copy(src, out.at[me], sems[2].at[w]).wait()

    return _Task(arrays, [_sds((8,) + a.shape, a.dtype) for a in arrays], {},
                 [pltpu.SemaphoreType.DMA((n, 7)), pltpu.SemaphoreType.DMA((n, 7)), pltpu.SemaphoreType.DMA((n,))],
                 start, finish)


def _share_task(shares):
    n = len(shares)

    def copy(out, w, sems, slot):
        x, y, c, _, _ = _place()
        return _remote(out.at[slot], out.at[slot], sems[0].at[w], sems[1].at[w], (x, y, 1 - c))

    def start(ins, outs, sems):
        c = _place()[2]
        for w, out in enumerate(outs):
            copy(out, w, sems, c).start()

    def finish(ins, outs, sems):
        c = _place()[2]
        for w, out in enumerate(outs):
            copy(out, w, sems, 1 - c).wait_recv()
        for w, out in enumerate(outs):
            copy(out, w, sems, c).wait_send()

    return _Task(shares, [_sds(s.shape, s.dtype) for s in shares], {i: i for i in range(n)},
                 [pltpu.SemaphoreType.DMA((n,)), pltpu.SemaphoreType.DMA((n,))], start, finish)


TILE_BYTES = 2 * 1024 * 1024
PARTIAL_TILE_BYTES = 1024 * 1024


def _in_hbm(t):
    return pltpu.with_memory_space_constraint(t, pltpu.HBM)


def _row_tile(rows, cols, limit=TILE_BYTES):
    best = 8
    for tr in range(8, rows + 1, 8):
        if rows % tr == 0 and tr * cols * 4 <= limit:
            best = tr
    assert rows % best == 0, (rows, cols)
    return best


def _chip_partial(g, got, place, wire_dtype):
    ns, ah, b = got.shape
    sharded = ns == N_CHIPS
    tr = _row_tile(ah, b, PARTIAL_TILE_BYTES)
    nb = ah // tr
    first = 0 if g.shape[1] == ah else nb

    def body(place_ref, *refs):
        g_refs, got_refs, outs = refs[:ns], refs[ns:2 * ns], refs[2 * ns:]
        parts = [g_refs[k][0] + got_refs[k][0] for k in range(ns)]
        own = parts[0]
        if sharded:
            for k in range(ns):
                outs[0][k] = parts[k].astype(wire_dtype)
                if k:
                    own = jnp.where(place_ref[0] == k, parts[k], own)
        outs[-1][0] = own.astype(wire_dtype)

    blk = (1, tr, b)
    in_specs = ([pl.BlockSpec(blk, lambda i, s, k=k: (k, s[1] * first + i, 0)) for k in range(ns)]
                + [pl.BlockSpec(blk, lambda i, s, k=k: (k, i, 0)) for k in range(ns)])
    acc_spec = pl.BlockSpec(blk, lambda i, s: (s[0], i, 0))
    acc_shape = _sds((N_CHIPS, ah, b), wire_dtype)
    out = pl.pallas_call(
        body, name="grad_chip_partial",
        grid_spec=pltpu.PrefetchScalarGridSpec(
            num_scalar_prefetch=1, grid=(nb,), in_specs=in_specs,
            out_specs=[pl.BlockSpec((ns, tr, b), lambda i, s: (0, i, 0)), acc_spec] if sharded else [acc_spec]),
        out_shape=[acc_shape, acc_shape] if sharded else [acc_shape],
        compiler_params=pltpu.CompilerParams(dimension_semantics=("arbitrary",), vmem_limit_bytes=VMEM_LIMIT),
    )(place, *([g] * ns), *([got] * ns))
    return (out[0], out[1]) if sharded else (None, out[0])


def _chip_sum(acc, place):
    _, ah, b = acc.shape
    tr = _row_tile(ah, b)

    def body(place_ref, p_ref, out_ref):
        total = p_ref[0].astype(F32) + p_ref[1].astype(F32)
        total = total + p_ref[2].astype(F32)
        out_ref[0] = total + p_ref[3].astype(F32)

    return pl.pallas_call(
        body, name="grad_chip_sum",
        grid_spec=pltpu.PrefetchScalarGridSpec(
            num_scalar_prefetch=1, grid=(ah // tr,),
            in_specs=[pl.BlockSpec((N_CHIPS, tr, b), lambda i, s: (0, i, 0))],
            out_specs=pl.BlockSpec((1, tr, b), lambda i, s: (s[1], i, 0))),
        out_shape=_sds((2, ah, b)),
        compiler_params=pltpu.CompilerParams(dimension_semantics=("arbitrary",)),
    )(place, _in_hbm(acc))


def _adam_math(w, g, m, v):
    nm = ADAM_B1 * m + (1.0 - ADAM_B1) * g
    nv = ADAM_B2 * v + (1.0 - ADAM_B2) * (g * g)
    m_hat = nm / (1.0 - ADAM_B1 ** ADAM_STEP)
    v_hat = nv / (1.0 - ADAM_B2 ** ADAM_STEP)
    return -ADAM_LR * (m_hat / (jnp.sqrt(v_hat) + ADAM_EPS) + ADAM_WD * w), nm, nv


def _adamw(w, g, m, v):
    a, b = w.shape
    tr = _row_tile(a, b)

    def body(w_ref, g_ref, m_ref, v_ref, g_out, d_ref, nm_ref, nv_ref):
        g_out[...] = g_ref[...]
        d_ref[...], nm_ref[...], nv_ref[...] = _adam_math(w_ref[...], g_ref[...], m_ref[...], v_ref[...])

    blk = pl.BlockSpec((tr, b), lambda i: (i, 0))
    return pl.pallas_call(
        body, name="adamw", grid=(a // tr,),
        in_specs=[blk] * 4, out_specs=[blk] * 4, out_shape=[_sds((a, b))] * 4,
        compiler_params=pltpu.CompilerParams(dimension_semantics=("arbitrary",)),
    )(w, g, m, v)


def _adamw_sum(w, m, v, acc, got, place):
    a, b = w.shape
    ah = a // 2
    tr = _row_tile(ah, b)
    nb = ah // tr

    def body(place_ref, w_ref, m_ref, v_ref, acc_ref, got_ref, g_out, d_ref, nm_ref, nv_ref):
        mine = (pl.program_id(0) // nb) == place_ref[1]
        part = lambda k: jnp.where(mine, acc_ref[k], got_ref[k]).astype(F32)
        g = part(0) + part(1)
        g = g + part(2)
        g = g + part(3)
        g_out[...] = g
        d_ref[...], nm_ref[...], nv_ref[...] = _adam_math(w_ref[...], g, m_ref[...], v_ref[...])

    blk = pl.BlockSpec((tr, b), lambda i, s: (i, 0))
    mine_spec = pl.BlockSpec((N_CHIPS, tr, b), lambda i, s: (0, jnp.where(i // nb == s[1], i % nb, 0), 0))
    got_spec = pl.BlockSpec((N_CHIPS, tr, b), lambda i, s: (0, jnp.where(i // nb == s[1], 0, i % nb), 0))
    return pl.pallas_call(
        body, name="adamw_sum",
        grid_spec=pltpu.PrefetchScalarGridSpec(
            num_scalar_prefetch=1, grid=(a // tr,), in_specs=[blk] * 3 + [mine_spec, got_spec], out_specs=[blk] * 4),
        out_shape=[_sds((a, b))] * 4,
        compiler_params=pltpu.CompilerParams(dimension_semantics=("arbitrary",), vmem_limit_bytes=VMEM_LIMIT),
    )(place, w, m, v, _in_hbm(acc), _in_hbm(got))


def _adamw_pieces(g, pieces, name):
    n = len(pieces)

    def body(g_ref, *refs):
        def grad(rows, cols):
            if len(g_ref.shape) == 2:
                return g_ref[rows, cols]
            total = g_ref[0, rows, cols]
            for k in range(1, g_ref.shape[0]):
                total = total + g_ref[k, rows, cols]
            return total

        ins, outs = refs[:3 * n], refs[3 * n:]
        for i, piece in enumerate(pieces):
            w_ref, m_ref, v_ref = ins[3 * i:3 * i + 3]
            o_g, o_d, o_m, o_v = outs[4 * i:4 * i + 4]
            if len(piece) == 5:
                g_v = grad(piece[3], piece[4])
                o_g[...] = g_v
                o_d[...], o_m[...], o_v[...] = _adam_math(w_ref[...], g_v, m_ref[...], v_ref[...])
            else:
                for r in range(w_ref.shape[1] // SMALL_COLS):
                    lanes = slice(r * SMALL_COLS, (r + 1) * SMALL_COLS)
                    g_v = grad(slice(piece[3] + r, piece[3] + r + 1), slice(None))
                    o_g[:, lanes] = g_v
                    o_d[:, lanes], o_m[:, lanes], o_v[:, lanes] = _adam_math(w_ref[:, lanes], g_v, m_ref[:, lanes],
                                                                            v_ref[:, lanes])

    operands = [t for piece in pieces for t in piece[:3]]
    out = pl.pallas_call(
        body, name=name,
        out_shape=[_sds(piece[0].shape) for piece in pieces for _ in range(4)],
    )(g, *operands)
    return [tuple(out[4 * i:4 * i + 4]) for i in range(n)]


TINY_ROWS, TINY_COLS = 16, 768
SMALL_COLS = 128
SMALL_ROWS = 624


def _pack_tiny(conv_w, b_gates, fcw):
    ns = conv_w.shape[0]
    pad = lambda t: jnp.pad(t, ((0, 0), (0, 0), (0, TINY_COLS - t.shape[2])))
    z = lambda rows: jnp.zeros((ns, rows, TINY_COLS), F32)
    return jnp.concatenate([pad(conv_w), pad(b_gates), z(2), fcw, z(TINY_ROWS - 11)], axis=1)


def _unpack_tiny(t):
    return t[:, 0:4, 0:256], t[:, 4:6, 0:256], t[:, 8:11, :]


def _cols_to_shards(t, n):
    return t.reshape(t.shape[0], N_CHIPS, n).transpose(1, 0, 2)


def _shards_to_cols(t):
    return t.transpose(1, 0, 2).reshape(t.shape[1], -1)


_VECTORS = ("g_mix_post", "conv_b", "lru_lambda", "g_ffn_pre", "g_ffn_post", "g_ple_gate", "g_ple_post", "pool_scale",
            "ffn_conv_b")
_VECTOR_LEN = {"pool_scale": POOL_WIDTH, "ffn_conv_b": D_FF}
POOL_W_ROWS = POOL_GROUPS * POOL_GROUP_DIM


def _vector_rows():
    rows, row = {}, POOL_W_ROWS
    for k in _VECTORS:
        rows[k] = row
        row += max(8, _VECTOR_LEN.get(k, D_MODEL) // SMALL_COLS)
    return rows, row


def _pack_small(grads, loss):
    tiles = lambda t: jnp.pad(t, ((0, -t.shape[0] % 8), (0, 0)))
    parts = [grads["pool_w"].reshape(POOL_W_ROWS, SMALL_COLS)] + [tiles(grads[k].reshape(-1, SMALL_COLS)) for k in _VECTORS]
    parts.append(tiles(loss))
    used = sum(t.shape[0] for t in parts)
    return jnp.concatenate(parts + [jnp.zeros((SMALL_ROWS - used, SMALL_COLS), F32)], axis=0)


def _gates_block_diag(w):
    w4 = w.reshape(2, GATE_BLOCKS, 4, RNN_HEAD_DIM, RNN_HEAD_DIM)
    eye = jnp.eye(4, dtype=w.dtype)
    return jnp.einsum("gqhij,hk->gqhikj", w4, eye).reshape(2, GATE_BLOCKS, GATE_BLOCK, GATE_BLOCK)


def _gates_from_block_diag(dw):
    d6 = dw.reshape(2, GATE_BLOCKS, 4, RNN_HEAD_DIM, 4, RNN_HEAD_DIM)
    blocks = [d6[:, :, hh, :, hh, :] for hh in range(4)]
    return jnp.stack(blocks, axis=2).reshape(2, RNN_HEADS, RNN_HEAD_DIM, RNN_HEAD_DIM)


ROW_TILE = 256
DW_TILES = 8

_SHARDED = ("w_in", "w_pool_out", "w_rg_out", "w_o", "w_up", "w_down", "w_ple_gate", "w_ple_proj")
_WEIGHTS = ("g_mix_pre", "g_mix_post", "w_in", "pool_w", "pool_scale", "w_pool_out", "conv_w", "conv_b", "w_rg_gates",
            "b_rg_gates", "lru_lambda", "w_rg_out", "w_o", "g_ffn_pre", "g_ffn_post", "w_up", "ffn_conv_w", "ffn_conv_b",
            "w_down", "g_ple_gate", "w_ple_gate", "w_ple_proj", "g_ple_post")


def _wire_dtype(g):
    return BF16 if g.shape[1] >= 64 and g.shape[2] > SMALL_COLS else F32


def _partials(grads, got, place):
    parts = [_chip_partial(g, r, place, _wire_dtype(g)) for g, r in zip(grads, got)]
    return [send for send, _ in parts], [acc for _, acc in parts]


def _whole(both):
    return [b.reshape(2 * b.shape[1], b.shape[2]) for b in both]


def _step(x, p, tgt, rep, place, ts):
    vec = lambda k: rep[k].reshape(1, -1)
    tall = min(2 * ts, x.shape[0])
    pool_w = rep["pool_w"].astype(BF16)
    wg = _gates_block_diag(rep["w_rg_gates"]).astype(BF16)
    sq = lambda t: t.reshape(D_MODEL, D_MODEL)
    by4 = lambda t: t.reshape(N_CHIPS, -1, D_MODEL)

    first, ride1, ride2 = (("w_in", "w_pool_out", "tiny"), ("w_rg_out", "w_o", "w_down"),
                           ("w_up", "w_ple_gate", "w_ple_proj"))
    later = ride1 + ride2
    tiny = _pack_tiny(rep["conv_w"][None], rep["b_rg_gates"][None], rep["ffn_conv_w"][None])[0]
    own_first, _ = _own_slots([rep["w_in"], rep["w_pool_out"], tiny], [BF16, BF16, F32], "own_slots_first")
    own_later, (got,) = _own_slots([rep[k] for k in later], [BF16] * len(later), "own_slots_gather_first",
                                   [_gather_task(own_first)])
    own = dict(zip(later, own_later))
    full = dict(zip(first, got))
    conv_w, b_gates, fcw = [_shards_to_cols(t) for t in _unpack_tiny(full["tiny"])]

    (urx, urg, gp, gr, d, ypool, h1), (got,) = _fwd_in_pool(
        x, vec("g_mix_pre"), full["w_in"], pool_w, vec("pool_scale"), full["w_pool_out"], ts,
        [_gather_task([own[k] for k in ride1], relay_steps=(6, 2))])
    full.update(zip(ride1, got))
    w_rg_out, w_o, w_down = sq(full["w_rg_out"]), sq(full["w_o"]), full["w_down"].reshape(D_FF, D_MODEL)
    (xc, r, ig, h, yrnn, mo, x1, glr, ggr, sp, sr), (got,) = _fwd_rnn_merge(
        urx, urg, gp, gr, ypool, x, conv_w, vec("conv_b"), wg, b_gates, vec("lru_lambda"), w_rg_out, w_o,
        vec("g_mix_post"), ts, [_gather_task([own[k] for k in ride2], relay_steps=(7, 3))])
    full.update(zip(ride2, got))
    (up, gl, gg, h2, dn, x2), _ = _fwd_ffn(x1, vec("g_ffn_pre"), full["w_up"], fcw, vec("ffn_conv_b"), w_down,
                                           vec("g_ffn_post"), ts)
    dx2, loss, d_w_gate, d_w_proj, d_g_ple_gate, d_g_ple_post = _ple_loss(
        x2, p, tgt, vec("g_ple_gate"), sq(full["w_ple_gate"]), full["w_ple_proj"], vec("g_ple_post"), tall)
    dup, d_w_down, d_fcw, d_fcb, d_g_ffn_post = _bwd_ffn_down(dx2, dn, up, gl, gg, fcw, w_down, vec("g_ffn_post"), ts)

    names1, grads1 = ("w_ple_gate", "w_ple_proj", "w_down"), [by4(d_w_gate), d_w_proj, by4(d_w_down)]
    (dx1, d_g_ffn_pre), (got1,) = _bwd_ffn_up(dup, x1, dx2, vec("g_ffn_pre"), full["w_up"], tall, [_halves_task(grads1)])
    (send_w_up, acc_w_up), (accs1,) = _dw_up(h2, dup, ts, [_exchange_task(*_partials(grads1, got1, place))])
    (dgp, dgr, dyp, dyr, d_w_o, d_g_mix_post), (theirs1,) = _bwd_merge(
        dx1, mo, sp, sr, ypool, yrnn, vec("g_mix_post"), w_o, tall, [_swap_task(accs1)])
    (durx, durg, d_w_rg_out, d_wg, d_conv_w, d_conv_b, d_b_gates, d_lam), (accs2,) = _bwd_rnn(
        dyr, urx, glr, ggr, xc, r, ig, h, conv_w, wg, vec("lru_lambda"), w_rg_out, ts,
        [_exchange_task([send_w_up], [acc_w_up])])
    names3 = ("w_o", "w_rg_out", "tiny", "w_rg_gates")
    grads3 = [by4(d_w_o), by4(d_w_rg_out),
              _pack_tiny(_cols_to_shards(d_conv_w, 256), _cols_to_shards(d_b_gates, 256), _cols_to_shards(d_fcw, 768)),
              _gates_from_block_diag(d_wg).reshape(1, 2 * RNN_HEADS * RNN_HEAD_DIM, RNN_HEAD_DIM)]
    (dzp, d_w_pool_out, d_pool_w, d_pool_scale), (got3, theirs2) = _bwd_pool(
        dyp, d, pool_w, vec("pool_scale"), full["w_pool_out"], tall, [_halves_task(grads3), _swap_task(accs2)])
    replicated = {"g_mix_post": d_g_mix_post, "conv_b": d_conv_b, "lru_lambda": d_lam, "g_ffn_pre": d_g_ffn_pre,
                  "g_ffn_post": d_g_ffn_post, "g_ple_gate": d_g_ple_gate, "g_ple_post": d_g_ple_post,
                  "pool_scale": d_pool_scale, "ffn_conv_b": d_fcb, "pool_w": d_pool_w}
    names4 = ("w_in", "w_pool_out", "small")
    small4 = [d_w_pool_out, _pack_small(replicated, loss)[None]]
    (send_w_in, acc_w_in), (accs3, got_small4) = _dw_in(
        h1, dzp, durx, durg, dgp, dgr, ts, [_exchange_task(*_partials(grads3, got3, place)), _halves_task(small4)])
    sends4, accs4 = _partials(small4, got_small4, place)
    (grad_x, d_g_mix_pre), (accs4, theirs3, both3) = _bwd_in(
        dzp, durx, durg, dgp, dgr, x, dx1, vec("g_mix_pre"), full["w_in"], tall,
        [_exchange_task([send_w_in] + sends4, [acc_w_in] + accs4), _swap_task(accs3[:2]),
         _share_task([_chip_sum(acc, place) for acc in accs3[2:]])])
    theirs4, both4, (g_mix_pre_parts,) = _run(
        [_swap_task(accs4[:2]), _share_task([_chip_sum(acc, place) for acc in accs4[2:]]),
         _all_devices_task([d_g_mix_pre.reshape(SUBLANES, SMALL_COLS)])], "grad_sibling_share")
    mine = accs1 + accs2 + accs3[:2] + accs4[:2]
    partials = dict(zip(names1 + ("w_up",) + names3[:2] + names4[:2], zip(mine, theirs1 + theirs2 + theirs3 + theirs4)))
    return grad_x, partials, dict(zip(names3[2:] + names4[2:], _whole(both3) + _whole(both4))), g_mix_pre_parts


def kernel(x, p, g_mix_pre, g_mix_post, w_in, pool_w, pool_scale, w_pool_out, conv_w, conv_b, w_rg_gates, b_rg_gates, lru_lambda, w_rg_out, w_o, g_ffn_pre, g_ffn_post, w_up, ffn_conv_w, ffn_conv_b, w_down, g_ple_gate, w_ple_gate, w_ple_proj, g_ple_post, loss_target, m_g_mix_pre, m_g_mix_post, m_w_in, m_pool_w, m_pool_scale, m_w_pool_out, m_conv_w, m_conv_b, m_w_rg_gates, m_b_rg_gates, m_lru_lambda, m_w_rg_out, m_w_o, m_g_ffn_pre, m_g_ffn_post, m_w_up, m_ffn_conv_w, m_ffn_conv_b, m_w_down, m_g_ple_gate, m_w_ple_gate, m_w_ple_proj, m_g_ple_post, v_g_mix_pre, v_g_mix_post, v_w_in, v_pool_w, v_pool_scale, v_w_pool_out, v_conv_w, v_conv_b, v_w_rg_gates, v_b_rg_gates, v_lru_lambda, v_w_rg_out, v_w_o, v_g_ffn_pre, v_g_ffn_post, v_w_up, v_ffn_conv_w, v_ffn_conv_b, v_w_down, v_g_ple_gate, v_w_ple_gate, v_w_ple_proj, v_g_ple_post):
    args = dict(locals())
    w = {k: args[k][0] for k in _WEIGHTS}
    m = {k: args["m_" + k][0] for k in _WEIGHTS}
    v = {k: args["v_" + k][0] for k in _WEIGHTS}
    place = jnp.stack([2 * lax.axis_index("x") + lax.axis_index("y"), lax.axis_index("c")]).astype(jnp.int32)
    grad_x, partials, reduced, g_mix_pre_parts = _step(x[0], p[0, 0], loss_target[0], w, place, ROW_TILE)

    gates_2d = (2 * RNN_HEADS * RNN_HEAD_DIM, RNN_HEAD_DIM)
    as2d = lambda k, shape: tuple(t[k].reshape(shape) for t in (w, m, v))
    done = {k: tuple(_adamw_sum(w[k], m[k], v[k], *partials[k], place)) for k in _SHARDED}
    gates_w, gates_m, gates_v = as2d("w_rg_gates", gates_2d)
    done["w_rg_gates"] = tuple(_adamw(gates_w, reduced["w_rg_gates"], gates_m, gates_v))
    tiny_names = ("conv_w", "b_rg_gates", "ffn_conv_w")
    tiny_at = ((slice(0, 4), slice(0, 256)), (slice(4, 6), slice(0, 256)), (slice(8, 11), slice(None)))
    done.update(zip(tiny_names, _adamw_pieces(
        reduced["tiny"], [(w[k], m[k], v[k]) + at for k, at in zip(tiny_names, tiny_at)], "adamw_tiny")))
    vector_rows, loss_row = _vector_rows()
    pieces = [as2d("pool_w", (POOL_W_ROWS, SMALL_COLS)) + (slice(0, POOL_W_ROWS), slice(None))]
    pieces += [as2d(k, (1, -1)) + (vector_rows[k],) for k in _VECTORS]
    done.update(zip(("pool_w",) + _VECTORS, _adamw_pieces(reduced["small"], pieces, "adamw_small")))
    done["g_mix_pre"] = _adamw_pieces(g_mix_pre_parts, [as2d("g_mix_pre", (1, -1)) + (0,)], "adamw_g_mix_pre")[0]

    result = [reduced["small"][loss_row, 0], grad_x[None]]
    for kind in range(4):
        result += [done[k][kind].reshape(args[k].shape) for k in _WEIGHTS]
    return tuple(result)
```

```python
import functools

import jax
import jax.numpy as jnp
from jax import lax
from jax.experimental import pallas as pl
from jax.experimental.pallas import tpu as pltpu

F32 = jnp.float32
BF16 = jnp.bfloat16

D_MODEL = 1024
POOL_WINDOWS = (2, 4, 8, 16)
POOL_GROUPS = 4
POOL_WIDTH = 512
POOL_GROUP_DIM = 128
RNN_HEADS = 16
RNN_HEAD_DIM = 64
GATE_BLOCK = 256
GATE_BLOCKS = D_MODEL // GATE_BLOCK
LRU_C = 8.0
D_FF = 3072
PLE_DIM = 256
RMS_EPS = 1e-6
IN_TOTAL = 4608
N_CHIPS = 4
IN_SHARD = IN_TOTAL // N_CHIPS
UP_SHARD = 2 * D_FF // N_CHIPS
Z_GROUPS = ((0, 512), (512, 1536), (1536, 2560), (2560, 3584), (3584, 4608))
POOL_HALO = 16
CONV_HALO = 8

ADAM_LR = 0.001
ADAM_B1 = 0.9
ADAM_B2 = 0.999
ADAM_EPS = 1e-08
ADAM_WD = 0.01
ADAM_STEP = 10

VMEM_LIMIT = 56 * 1024 * 1024
MESH = pl.DeviceIdType.MESH

_GELU_C = 0.7978845608028654
_GELU_A = 0.044715


def _dot(a, b):
    return jnp.dot(a.astype(BF16), b.astype(BF16), preferred_element_type=F32)


def _dot_nt(a, b):
    return lax.dot_general(a.astype(BF16), b.astype(BF16), (((1,), (1,)), ((), ())), preferred_element_type=F32)


def _dot_tn(a, b):
    return lax.dot_general(a.astype(BF16), b.astype(BF16), (((0,), (0,)), ((), ())), preferred_element_type=F32)


def _overlaps(group):
    a, b = Z_GROUPS[group]
    found = []
    for j in range(N_CHIPS):
        lo, hi = max(a, j * IN_SHARD), min(b, (j + 1) * IN_SHARD)
        if lo < hi:
            found.append((j, slice(lo - j * IN_SHARD, hi - j * IN_SHARD), slice(lo - a, hi - a)))
    return found


def _rms_fwd(x, g):
    r = lax.rsqrt(jnp.mean(x * x, axis=-1, keepdims=True) + RMS_EPS)
    xh = x * r
    return xh * g, xh, r


def _rms_bwd(xh, r, g, dy):
    dxh = dy * g
    dg = jnp.sum(dy * xh, axis=0, keepdims=True)
    dx = r * (dxh - xh * jnp.mean(dxh * xh, axis=-1, keepdims=True))
    return dx, dg


def _sigmoid(x):
    return 0.5 * jnp.tanh(0.5 * x) + 0.5


def _gelu(x):
    x2 = x * x
    t = jnp.tanh(x * (_GELU_C + (_GELU_C * _GELU_A) * x2))
    p = 0.5 * t + 0.5
    gl = x * p
    return gl, p + gl * (1.0 - p) * (2.0 * _GELU_C + (6.0 * _GELU_C * _GELU_A) * x2)


def _softplus_neg(lam):
    nl = -lam
    return jnp.maximum(nl, 0.0) + jnp.log(1.0 + jnp.exp(-jnp.abs(nl)))


def _lru_coeffs(r, lam, first_row):
    c8 = LRU_C * _softplus_neg(lam)
    la = -(c8 * r)
    a = jnp.exp(la)
    m2 = jnp.tanh(-la) * (1.0 + a * a)
    mult = jnp.where(first_row, 1.0, jnp.sqrt(m2))
    return c8, a, m2, mult


SUBLANES = 8


def _scan_fwd(a, u, carry):
    n = a.shape[0]
    sub = lax.broadcasted_iota(jnp.int32, (n, 1), 0) % SUBLANES
    acc_a, acc_h = a, u
    for s in (1, 2, 4):
        m = sub >= s
        h_s = jnp.where(m, pltpu.roll(acc_h, s, 0), 0.0)
        a_s = jnp.where(m, pltpu.roll(acc_a, s, 0), 1.0)
        acc_h = acc_a * h_s + acc_h
        acc_a = acc_a * a_s
    out = []
    for g in range(n // SUBLANES):
        rows = slice(g * SUBLANES, (g + 1) * SUBLANES)
        out.append(acc_h[rows] + acc_a[rows] * carry)
        carry = out[-1][SUBLANES - 1:SUBLANES]
    return jnp.concatenate(out, axis=0)


def _scan_bwd(b, g, carry):
    n = b.shape[0]
    sub = lax.broadcasted_iota(jnp.int32, (n, 1), 0) % SUBLANES
    acc_b, acc_l = b, g
    for s in (1, 2, 4):
        m = sub < SUBLANES - s
        l_s = jnp.where(m, pltpu.roll(acc_l, n - s, 0), 0.0)
        b_s = jnp.where(m, pltpu.roll(acc_b, n - s, 0), 1.0)
        acc_l = acc_b * l_s + acc_l
        acc_b = acc_b * b_s
    out = [None] * (n // SUBLANES)
    for g in reversed(range(n // SUBLANES)):
        rows = slice(g * SUBLANES, (g + 1) * SUBLANES)
        out[g] = acc_l[rows] + acc_b[rows] * carry
        carry = out[g][0:1]
    return jnp.concatenate(out, axis=0)


def _shift_down(ext, k, halo):
    return pltpu.roll(ext, k, 0)[halo:] if k else ext[halo:]


def _shift_up(ext, k, ts):
    return pltpu.roll(ext, ext.shape[0] - k, 0)[:ts] if k else ext[:ts]


def _rows(ts, width, nt=None, col=0):
    if nt is None:
        return pl.BlockSpec((ts, width), lambda i: (i, col))
    return pl.BlockSpec((ts, width), lambda i: (nt - 1 - i, col))


def _resident(shape):
    zeros = (0,) * len(shape)
    return pl.BlockSpec(shape, lambda i: zeros, pipeline_mode=pl.Buffered(1))


def _acc(shape):
    zeros = (0,) * len(shape)
    return pl.BlockSpec(shape, lambda i: zeros)


def _params():
    return pltpu.CompilerParams(dimension_semantics=("arbitrary",), vmem_limit_bytes=VMEM_LIMIT)


def _sds(shape, dtype=F32):
    return jax.ShapeDtypeStruct(shape, dtype)


class _Task:
    def __init__(self, ins, out_shapes, aliases, sems, start, finish, relays=()):
        self.ins, self.out_shapes, self.aliases, self.sems = list(ins), list(out_shapes), dict(aliases), list(sems)
        self.start, self.relays, self.finish = start, list(relays), finish


def _call(body, name, grid, in_specs, out_specs, out_shape, scratch_shapes, args, tasks=()):
    n_in, n_out, n_scr = len(in_specs), len(out_specs), len(scratch_shapes)
    t_in = [len(t.ins) for t in tasks]
    t_out = [len(t.out_shapes) for t in tasks]
    t_sem = [len(t.sems) for t in tasks]
    steps = 1
    for g in grid:
        steps *= g

    def take(refs, pos, counts):
        groups = []
        for c in counts:
            groups.append(refs[pos:pos + c])
            pos += c
        return groups, pos

    def wrapped(*refs):
        (cin,), pos = take(refs, 0, [n_in])
        tin, pos = take(refs, pos, t_in)
        (cout,), pos = take(refs, pos, [n_out])
        tout, pos = take(refs, pos, t_out)
        (cscr,), pos = take(refs, pos, [n_scr])
        tsem, pos = take(refs, pos, t_sem)
        if not grid:
            for t, a, b, c in zip(tasks, tin, tout, tsem):
                t.start(a, b, c)
            if body is not None:
                body(*cin, *cout, *cscr)
            for t, a, b, c in zip(tasks, tin, tout, tsem):
                for relay, _ in t.relays:
                    relay(a, b, c)
            for t, a, b, c in zip(tasks, tin, tout, tsem):
                t.finish(a, b, c)
            return
        step = pl.program_id(0)
        for axis in range(1, len(grid)):
            step = step * grid[axis] + pl.program_id(axis)
        if tasks:
            @pl.when(step == 0)
            def _():
                for t, a, b, c in zip(tasks, tin, tout, tsem):
                    t.start(a, b, c)

        body(*cin, *cout, *cscr)
        for t, a, b, c in zip(tasks, tin, tout, tsem):
            for relay, before in t.relays:
                pl.when(step == max(steps - 1 - before, 0))(functools.partial(relay, a, b, c))

        if tasks:
            @pl.when(step == steps - 1)
            def _():
                for t, a, b, c in zip(tasks, tin, tout, tsem):
                    t.finish(a, b, c)

    aliases, in_pos, out_pos = {}, n_in, n_out
    for t, ni, no in zip(tasks, t_in, t_out):
        aliases.update({in_pos + a: out_pos + b for a, b in t.aliases.items()})
        in_pos, out_pos = in_pos + ni, out_pos + no
    any_spec = pl.BlockSpec(memory_space=pltpu.HBM)
    kwargs = dict(grid=grid, compiler_params=pltpu.CompilerParams(
        dimension_semantics=("arbitrary",) * len(grid), vmem_limit_bytes=VMEM_LIMIT)) if grid else dict(
        compiler_params=pltpu.CompilerParams(vmem_limit_bytes=VMEM_LIMIT))
    out = pl.pallas_call(
        wrapped, name=name,
        in_specs=list(in_specs) + [any_spec] * sum(t_in),
        out_specs=list(out_specs) + [any_spec] * sum(t_out),
        out_shape=list(out_shape) + [s for t in tasks for s in t.out_shapes],
        scratch_shapes=list(scratch_shapes) + [s for t in tasks for s in t.sems],
        input_output_aliases=aliases, **kwargs,
    )(*args, *[pltpu.with_memory_space_constraint(a, pltpu.HBM) for t in tasks for a in t.ins])
    task_outs, pos = take(list(out), n_out, t_out)
    return list(out[:n_out]), task_outs


def _fwd_in_pool(x, g_pre, w_in, pool_w, pool_scale, w_pool_out, ts, tasks=()):
    s = x.shape[0]

    def body(x_ref, g_ref, win_ref, pw_ref, ps_ref, wpo_ref,
             urx_ref, urg_ref, gp_ref, gr_ref, d_ref, yp_ref, h1_ref, halo_scr):
        i = pl.program_id(0)

        @pl.when(i == 0)
        def _():
            halo_scr[...] = jnp.zeros_like(halo_scr)

        h1, _, _ = _rms_fwd(x_ref[...], g_ref[...])
        h1 = h1.astype(BF16)
        h1_ref[...] = h1
        u = jnp.dot(h1, win_ref[0, :, 0:POOL_WIDTH], preferred_element_type=F32)
        for group, out_ref in zip(range(1, len(Z_GROUPS)), (urx_ref, urg_ref, gp_ref, gr_ref)):
            for j, shard_cols, group_cols in _overlaps(group):
                out_ref[:, group_cols] = jnp.dot(h1, win_ref[j, :, shard_cols],
                                                 preferred_element_type=F32).astype(BF16)
        ext = jnp.concatenate([halo_scr[...], u], axis=0)
        halo_scr[...] = u[ts - POOL_HALO:, :]
        t = i * ts + lax.broadcasted_iota(jnp.int32, (ts, 1), 0)
        y4 = []
        for g, w in enumerate(POOL_WINDOWS):
            lanes = slice(g * POOL_GROUP_DIM, (g + 1) * POOL_GROUP_DIM)
            acc = ext[:, lanes]
            sh = 1
            while sh < w:
                acc = acc + pltpu.roll(acc, sh, 0)
                sh *= 2
            inv = 1.0 / jnp.minimum(t + 1, w).astype(F32)
            dg = acc[POOL_HALO:, :] * inv - u[:, lanes]
            d_ref[:, lanes] = dg.astype(BF16)
            y4.append(_dot(dg, pw_ref[g]))
        ypre = jnp.concatenate(y4, axis=1) * ps_ref[...]
        ypre = ypre.astype(BF16)
        for j in range(N_CHIPS):
            yp_ref[:, j * 256:(j + 1) * 256] = jnp.dot(ypre, wpo_ref[j], preferred_element_type=F32)

    return _call(
        body, "fwd_in_pool", (s // ts,),
        [_rows(ts, D_MODEL), _resident((1, D_MODEL)), _resident(w_in.shape), _resident(pool_w.shape),
         _resident((1, POOL_WIDTH)), _resident(w_pool_out.shape)],
        [_rows(ts, D_MODEL)] * 4 + [_rows(ts, POOL_WIDTH), _rows(ts, D_MODEL), _rows(ts, D_MODEL)],
        [_sds((s, D_MODEL), BF16)] * 4 + [_sds((s, POOL_WIDTH), BF16), _sds((s, D_MODEL)), _sds((s, D_MODEL), BF16)],
        [pltpu.VMEM((POOL_HALO, POOL_WIDTH), F32)],
        (x, g_pre, w_in, pool_w, pool_scale, w_pool_out), tasks)


def _fwd_rnn_merge(urx, urg, gp, gr, ypool, x, conv_w, conv_b, wg, bg, lam, w_rg_out, w_o, g_post, ts, tasks=()):
    s = x.shape[0]

    def body(urx_ref, urg_ref, gp_ref, gr_ref, yp_ref, x_ref, cw_ref, cb_ref, wg_ref, bg_ref, lam_ref, wrg_ref, wo_ref,
             gpost_ref, xc_ref, r_ref, ig_ref, h_ref, yr_ref, mo_ref, x1_ref, gl_ref, gg_ref, sp_ref, sr_ref,
             halo_scr, carry_scr):
        i = pl.program_id(0)

        @pl.when(i == 0)
        def _():
            halo_scr[...] = jnp.zeros_like(halo_scr)
            carry_scr[...] = jnp.zeros_like(carry_scr)

        urx_v = urx_ref[...].astype(F32)
        ext = jnp.concatenate([halo_scr[...], urx_v], axis=0)
        halo_scr[...] = urx_v[ts - CONV_HALO:, :]
        cw = cw_ref[...]
        xc = (cb_ref[...] + cw[3:4] * urx_v + cw[2:3] * _shift_down(ext, 1, CONV_HALO)
              + cw[1:2] * _shift_down(ext, 2, CONV_HALO) + cw[0:1] * _shift_down(ext, 3, CONV_HALO))
        xc_ref[...] = xc.astype(BF16)
        xcb = xc.astype(BF16)
        lin = []
        for gate in range(2):
            parts = [jnp.dot(xcb[:, q * GATE_BLOCK:(q + 1) * GATE_BLOCK], wg_ref[gate, q], preferred_element_type=F32)
                     for q in range(GATE_BLOCKS)]
            lin.append(jnp.concatenate(parts, axis=1) + bg_ref[gate:gate + 1, :])
        r = _sigmoid(lin[0])
        ig = _sigmoid(lin[1])
        r_ref[...] = r.astype(BF16)
        ig_ref[...] = ig.astype(BF16)
        first_row = (i * ts + lax.broadcasted_iota(jnp.int32, (ts, 1), 0)) == 0
        _, a, _, mult = _lru_coeffs(r, lam_ref[...], first_row)
        h = _scan_fwd(a, mult * ig * xc, carry_scr[0:1, :])
        carry_scr[0:1, :] = h[ts - 1:ts, :]
        h_ref[...] = h
        urg_v = urg_ref[...].astype(F32)
        gl, gg = _gelu(urg_v)
        gl_ref[...] = gl.astype(BF16)
        gg_ref[...] = gg.astype(BF16)
        yr = _dot(h * gl, wrg_ref[...])
        yr_ref[...] = yr.astype(BF16)
        sp = _sigmoid(gp_ref[...].astype(F32))
        sr = _sigmoid(gr_ref[...].astype(F32))
        sp_ref[...] = sp.astype(BF16)
        sr_ref[...] = sr.astype(BF16)
        merged = sp * yp_ref[...] + sr * yr
        mo = _dot(merged, wo_ref[...])
        mo_ref[...] = mo
        y, _, _ = _rms_fwd(mo, gpost_ref[...])
        x1_ref[...] = x_ref[...] + y

    row = _rows(ts, D_MODEL)
    return _call(
        body, "fwd_rnn_merge", (s // ts,),
        [row] * 6 + [_resident(conv_w.shape), _resident((1, D_MODEL)), _resident(wg.shape), _resident(bg.shape),
                     _resident((1, D_MODEL)), _resident(w_rg_out.shape), _resident(w_o.shape), _resident((1, D_MODEL))],
        [row] * 11, [_sds((s, D_MODEL), dt) for dt in (BF16, BF16, BF16, F32, BF16, F32, F32, BF16, BF16, BF16, BF16)],
        [pltpu.VMEM((CONV_HALO, D_MODEL), F32), pltpu.VMEM((8, D_MODEL), F32)],
        (urx, urg, gp, gr, ypool, x, conv_w, conv_b, wg, bg, lam, w_rg_out, w_o, g_post), tasks)


def _fwd_ffn(x1, g_pre, w_up, fcw, fcb, w_down, g_post, ts, tasks=()):
    s = x1.shape[0]

    def body(x1_ref, g_ref, wup_ref, fcw_ref, fcb_ref, wd_ref, gpost_ref,
             up_ref, gl_ref, gg_ref, h2_ref, dn_ref, x2_ref, up_scr, halo_scr):
        i = pl.program_id(0)

        @pl.when(i == 0)
        def _():
            halo_scr[...] = jnp.zeros_like(halo_scr)

        x1_v = x1_ref[...]
        h2, _, _ = _rms_fwd(x1_v, g_ref[...])
        h2 = h2.astype(BF16)
        h2_ref[...] = h2
        for j in range(N_CHIPS):
            up_scr[:, j * UP_SHARD:(j + 1) * UP_SHARD] = jnp.dot(h2, wup_ref[j], preferred_element_type=F32)
        up_ref[...] = up_scr[...].astype(BF16)
        ug = up_scr[:, 0:D_FF]
        ext = jnp.concatenate([halo_scr[...], ug], axis=0)
        halo_scr[...] = ug[ts - CONV_HALO:, :]
        w = fcw_ref[...]
        gh = (fcb_ref[...] + w[2:3] * ug + w[1:2] * _shift_down(ext, 1, CONV_HALO)
              + w[0:1] * _shift_down(ext, 2, CONV_HALO))
        gl, gg = _gelu(gh)
        gl_ref[...] = gl.astype(BF16)
        gg_ref[...] = gg.astype(BF16)
        dn = _dot(gl * up_scr[:, D_FF:], wd_ref[...])
        dn_ref[...] = dn
        y, _, _ = _rms_fwd(dn, gpost_ref[...])
        x2_ref[...] = x1_v + y

    row = _rows(ts, D_MODEL)
    return _call(
        body, "fwd_ffn", (s // ts,),
        [row, _resident((1, D_MODEL)), _resident(w_up.shape), _resident(fcw.shape), _resident((1, D_FF)),
         _resident(w_down.shape), _resident((1, D_MODEL))],
        [_rows(ts, 2 * D_FF), _rows(ts, D_FF), _rows(ts, D_FF), row, row, row],
        [_sds((s, 2 * D_FF), BF16), _sds((s, D_FF), BF16), _sds((s, D_FF), BF16), _sds((s, D_MODEL), BF16),
         _sds((s, D_MODEL)), _sds((s, D_MODEL))],
        [pltpu.VMEM((ts, 2 * D_FF), F32), pltpu.VMEM((CONV_HALO, D_FF), F32)],
        (x1, g_pre, w_up, fcw, fcb, w_down, g_post), tasks)


def _ple_loss(x2, p, tgt, g_gate, w_gate, w_proj, g_post, ts):
    s = x2.shape[0]

    def body(x2_ref, p_ref, t_ref, gg_ref, wg_ref, wp_ref, gp_ref, dx2_ref, loss_ref, dwg_ref, dwp_ref, dgg_ref, dgp_ref):
        @pl.when(pl.program_id(0) == 0)
        def _():
            loss_ref[...] = jnp.zeros_like(loss_ref)
            dwg_ref[...] = jnp.zeros_like(dwg_ref)
            dwp_ref[...] = jnp.zeros_like(dwp_ref)
            dgg_ref[...] = jnp.zeros_like(dgg_ref)
            dgp_ref[...] = jnp.zeros_like(dgp_ref)

        x2_v = x2_ref[...]
        n3, xh3, r3 = _rms_fwd(x2_v, gg_ref[...])
        pg = _sigmoid(_dot(n3, wg_ref[...]))
        pb = p_ref[...].astype(BF16)
        q = jnp.concatenate([jnp.dot(pb, wp_ref[j], preferred_element_type=F32) for j in range(N_CHIPS)], axis=1)
        ple, qh, rq = _rms_fwd(q, gp_ref[...])
        e = x2_v + pg * ple - t_ref[...]
        loss_ref[...] += 0.5 * jnp.sum(jnp.mean(e * e, axis=-1, keepdims=True), axis=0, keepdims=True)
        dy = e * (1.0 / D_MODEL)
        dpgl = dy * ple * pg * (1.0 - pg)
        dwg_ref[...] += _dot_tn(n3, dpgl)
        dx3, dgg = _rms_bwd(xh3, r3, gg_ref[...], _dot_nt(dpgl, wg_ref[...]))
        dgg_ref[...] += dgg
        dq, dgp = _rms_bwd(qh, rq, gp_ref[...], dy * pg)
        dgp_ref[...] += dgp
        for j in range(N_CHIPS):
            dwp_ref[j] += _dot_tn(pb, dq[:, j * 256:(j + 1) * 256])
        dx2_ref[...] = dy + dx3

    row = _rows(ts, D_MODEL)
    vec = _acc((1, D_MODEL))
    return pl.pallas_call(
        body, name="ple_loss", grid=(s // ts,),
        in_specs=[row, _rows(ts, PLE_DIM), row, _resident((1, D_MODEL)), _resident(w_gate.shape), _resident(w_proj.shape),
                  _resident((1, D_MODEL))],
        out_specs=[row, _acc((1, 128)), _acc(w_gate.shape), _acc(w_proj.shape), vec, vec],
        out_shape=[_sds((s, D_MODEL)), _sds((1, 128)), _sds(w_gate.shape), _sds(w_proj.shape), _sds((1, D_MODEL)),
                   _sds((1, D_MODEL))],
        compiler_params=_params(),
    )(x2, p, tgt, g_gate, w_gate, w_proj, g_post)


def _bwd_ffn_down(dx2, dn, up, gl, gg, fcw, w_down, g_post, ts):
    s = dx2.shape[0]
    nt = s // ts

    def body(dx2_ref, dn_ref, up_ref, gl_ref, gg_ref, fcw_ref, wd_ref, gpost_ref,
             dup_ref, dwd_ref, dfcw_ref, dfcb_ref, dgp_ref, carry_scr):
        i = pl.program_id(0)

        @pl.when(i == 0)
        def _():
            carry_scr[...] = jnp.zeros_like(carry_scr)
            dwd_ref[...] = jnp.zeros_like(dwd_ref)
            dfcw_ref[...] = jnp.zeros_like(dfcw_ref)
            dfcb_ref[...] = jnp.zeros_like(dfcb_ref)
            dgp_ref[...] = jnp.zeros_like(dgp_ref)

        _, xh, r = _rms_fwd(dn_ref[...], gpost_ref[...])
        ddn, dgp = _rms_bwd(xh, r, gpost_ref[...], dx2_ref[...])
        dgp_ref[...] += dgp
        dhid = _dot_nt(ddn, wd_ref[...])
        ug = up_ref[:, 0:D_FF].astype(F32)
        uv = up_ref[:, D_FF:]
        gl = gl_ref[...]
        w = fcw_ref[...]
        dwd_ref[...] += _dot_tn(gl * uv, ddn)
        dgh = dhid * (uv * gg_ref[...]).astype(F32)
        dup_ref[:, D_FF:] = dhid.astype(BF16) * gl
        extd = jnp.concatenate([dgh, carry_scr[...]], axis=0)
        carry_scr[...] = dgh[0:CONV_HALO, :]
        d1 = _shift_up(extd, 1, ts)
        d2 = _shift_up(extd, 2, ts)
        dup_ref[:, 0:D_FF] = (w[2:3] * dgh + w[1:2] * d1 + w[0:1] * d2).astype(BF16)
        dfcw_ref[2:3, :] += jnp.sum(ug * dgh, axis=0, keepdims=True)
        dfcw_ref[1:2, :] += jnp.sum(ug * d1, axis=0, keepdims=True)
        dfcw_ref[0:1, :] += jnp.sum(ug * d2, axis=0, keepdims=True)
        dfcb_ref[...] += jnp.sum(dgh, axis=0, keepdims=True)

    row = _rows(ts, D_MODEL, nt)
    wide = _rows(ts, D_FF, nt)
    return pl.pallas_call(
        body, name="bwd_ffn_down", grid=(nt,),
        in_specs=[row, row, _rows(ts, 2 * D_FF, nt), wide, wide, _resident(fcw.shape), _resident(w_down.shape),
                  _resident((1, D_MODEL))],
        out_specs=[_rows(ts, 2 * D_FF, nt), _acc(w_down.shape), _acc(fcw.shape), _acc((1, D_FF)), _acc((1, D_MODEL))],
        out_shape=[_sds((s, 2 * D_FF), BF16), _sds(w_down.shape), _sds(fcw.shape), _sds((1, D_FF)), _sds((1, D_MODEL))],
        scratch_shapes=[pltpu.VMEM((CONV_HALO, D_FF), F32)],
        compiler_params=_params(),
    )(dx2, dn, up, gl, gg, fcw, w_down, g_post)


def _bwd_ffn_up(dup, x1, dx2, g_pre, w_up, ts, tasks=()):
    s = x1.shape[0]

    def body(dup_ref, x1_ref, dx2_ref, g_ref, wup_ref, dx1_ref, dg_ref):
        @pl.when(pl.program_id(0) == 0)
        def _():
            dg_ref[...] = jnp.zeros_like(dg_ref)

        _, xh, r = _rms_fwd(x1_ref[...], g_ref[...])
        dh2 = _dot_nt(dup_ref[:, 0:UP_SHARD], wup_ref[0])
        for j in range(1, N_CHIPS):
            dh2 = dh2 + _dot_nt(dup_ref[:, j * UP_SHARD:(j + 1) * UP_SHARD], wup_ref[j])
        dx, dg = _rms_bwd(xh, r, g_ref[...], dh2)
        dg_ref[...] += dg
        dx1_ref[...] = dx2_ref[...] + dx

    row = _rows(ts, D_MODEL)
    return _call(
        body, "bwd_ffn_up", (s // ts,),
        [_rows(ts, 2 * D_FF), row, row, _resident((1, D_MODEL)), _resident(w_up.shape)],
        [row, _acc((1, D_MODEL))], [_sds((s, D_MODEL)), _sds((1, D_MODEL))], [],
        (dup, x1, dx2, g_pre, w_up), tasks)


def _dw_up(h2, dup, ts, tasks=()):
    s = h2.shape[0]
    ts = min(DW_TILES * ts, s)
    nt = s // ts
    half = D_MODEL // 2

    def body(h2_ref, dup_ref, send_ref, acc_ref, theirs_scr, mine_scr, got_scr, sems):
        j, o, i = pl.program_id(0), pl.program_id(1), pl.program_id(2)
        x, y, c, me, _ = _place()
        prod = _dot_tn(h2_ref[...], dup_ref[...])
        for scr, which in ((theirs_scr, 0), (mine_scr, 1)):
            @pl.when((o == which) & (i == 0))
            def _():
                scr[...] = prod

            @pl.when((o == which) & (i > 0))
            def _():
                scr[...] += prod

        slot = j % 2
        push = _remote(theirs_scr, got_scr.at[slot], sems.at[0, slot], sems.at[1, slot], (x, y, 1 - c))
        pl.when((o == 1) & (i == 0))(push.start)

        @pl.when((o == 1) & (i == nt - 1))
        def _():
            push.wait()
            part = (mine_scr[...] + got_scr[slot]).astype(BF16)
            send_ref[0] = part

            @pl.when(me == j)
            def _():
                acc_ref[0] = part

    def h2_cols(j, o, i):
        c = lax.axis_index("c")
        return i, jnp.where(o == 0, 1 - c, c)

    def own_slot(j, o, i):
        return 2 * lax.axis_index("x") + lax.axis_index("y"), 0, 0

    block = (1, half, UP_SHARD)
    return _call(
        body, "dw_up", (N_CHIPS, 2, nt),
        [pl.BlockSpec((ts, half), h2_cols), pl.BlockSpec((ts, UP_SHARD), lambda j, o, i: (i, j))],
        [pl.BlockSpec(block, lambda j, o, i: (j, 0, 0)), pl.BlockSpec(block, own_slot)],
        [_sds((N_CHIPS, half, UP_SHARD), BF16)] * 2,
        [pltpu.VMEM((half, UP_SHARD), F32), pltpu.VMEM((half, UP_SHARD), F32), pltpu.VMEM((2, half, UP_SHARD), F32),
         pltpu.SemaphoreType.DMA((2, 2))],
        (h2, dup), tasks)


def _bwd_merge(dx1, mo, sp, sr, ypool, yrnn, g_post, w_o, ts, tasks=()):
    s = dx1.shape[0]

    def body(dx1_ref, mo_ref, sp_ref, sr_ref, yp_ref, yr_ref, g_ref, wo_ref,
             dgp_ref, dgr_ref, dyp_ref, dyr_ref, dwo_ref, dg_ref):
        @pl.when(pl.program_id(0) == 0)
        def _():
            dwo_ref[...] = jnp.zeros_like(dwo_ref)
            dg_ref[...] = jnp.zeros_like(dg_ref)

        _, xh, r = _rms_fwd(mo_ref[...], g_ref[...])
        dmo, dg = _rms_bwd(xh, r, g_ref[...], dx1_ref[...])
        dg_ref[...] += dg
        dmerged = _dot_nt(dmo, wo_ref[...])
        sp = sp_ref[...].astype(F32)
        sr = sr_ref[...].astype(F32)
        yp = yp_ref[...]
        yr = yr_ref[...].astype(F32)
        dwo_ref[...] += _dot_tn(sp * yp + sr * yr, dmo)
        dgp_ref[...] = (dmerged * yp * sp * (1.0 - sp)).astype(BF16)
        dgr_ref[...] = (dmerged * yr * sr * (1.0 - sr)).astype(BF16)
        dyp_ref[...] = (dmerged * sp).astype(BF16)
        dyr_ref[...] = (dmerged * sr).astype(BF16)

    row = _rows(ts, D_MODEL)
    return _call(
        body, "bwd_merge", (s // ts,),
        [row] * 6 + [_resident((1, D_MODEL)), _resident(w_o.shape)],
        [row] * 4 + [_acc(w_o.shape), _acc((1, D_MODEL))],
        [_sds((s, D_MODEL), BF16)] * 4 + [_sds(w_o.shape), _sds((1, D_MODEL))], [],
        (dx1, mo, sp, sr, ypool, yrnn, g_post, w_o), tasks)


def _bwd_rnn(dyr, urx, gl, gg, xc, r, ig, h, conv_w, wg, lam, w_rg_out, ts, tasks=()):
    s = urx.shape[0]
    nt = s // ts
    halo_blocks = ts // CONV_HALO

    def body(dyr_ref, urx_ref, gl_ref, gg_ref, xc_ref, r_ref, ig_ref, h_ref, hh_ref, cw_ref, wg_ref, lam_ref, wrg_ref,
             durx_ref, durg_ref, dwrg_ref, dwg_ref, dcw_ref, dcb_ref, dbg_ref, dlam_ref, mu_scr, carry_scr):
        i = pl.program_id(0)
        k = nt - 1 - i

        @pl.when(i == 0)
        def _():
            mu_scr[...] = jnp.zeros_like(mu_scr)
            carry_scr[...] = jnp.zeros_like(carry_scr)
            dwrg_ref[...] = jnp.zeros_like(dwrg_ref)
            dwg_ref[...] = jnp.zeros_like(dwg_ref)
            dcw_ref[...] = jnp.zeros_like(dcw_ref)
            dcb_ref[...] = jnp.zeros_like(dcb_ref)
            dbg_ref[...] = jnp.zeros_like(dbg_ref)
            dlam_ref[...] = jnp.zeros_like(dlam_ref)

        row = lax.broadcasted_iota(jnp.int32, (ts, 1), 0)
        first_row = (k * ts + row) == 0
        h = h_ref[...]
        dyr_v = dyr_ref[...]
        dhr = _dot_nt(dyr_v, wrg_ref[...])
        gl = gl_ref[...].astype(F32)
        dwrg_ref[...] += _dot_tn(h * gl, dyr_v)
        durg_ref[...] = (dhr * h * gg_ref[...].astype(F32)).astype(BF16)
        r_v = r_ref[...].astype(F32)
        ig_v = ig_ref[...].astype(F32)
        xc_v = xc_ref[...].astype(F32)
        lam_v = lam_ref[...]
        c8, a, m2, mult = _lru_coeffs(r_v, lam_v, first_row)
        b = jnp.where(row == ts - 1, 1.0, pltpu.roll(a, ts - 1, 0))
        lt = _scan_bwd(b, dhr * gl, mu_scr[0:1, :])
        mu_scr[0:1, :] = a[0:1, :] * lt[0:1, :]
        h_before = jnp.where(k > 0, hh_ref[CONV_HALO - 1:CONV_HALO, :], 0.0)
        hprev = jnp.where(row == 0, h_before, pltpu.roll(h, 1, 0))
        dmult = lt * ig_v * xc_v
        da = lt * hprev - jnp.where(first_row, 0.0, dmult * a * lax.rsqrt(m2))
        dla = da * a
        dlam_ref[...] += jnp.sum(dla * r_v, axis=0, keepdims=True)
        dlr = (dla * (-c8)) * r_v * (1.0 - r_v)
        dli = (lt * mult * xc_v) * ig_v * (1.0 - ig_v)
        dbg_ref[0:1, :] += jnp.sum(dlr, axis=0, keepdims=True)
        dbg_ref[1:2, :] += jnp.sum(dli, axis=0, keepdims=True)
        xcb = xc_v.astype(BF16)
        parts = []
        for q in range(GATE_BLOCKS):
            blk = slice(q * GATE_BLOCK, (q + 1) * GATE_BLOCK)
            dlr_q = dlr[:, blk].astype(BF16)
            dli_q = dli[:, blk].astype(BF16)
            parts.append(_dot_nt(dlr_q, wg_ref[0, q]) + _dot_nt(dli_q, wg_ref[1, q]))
            dwg_ref[0, q] += _dot_tn(xcb[:, blk], dlr_q)
            dwg_ref[1, q] += _dot_tn(xcb[:, blk], dli_q)
        dxc = lt * mult * ig_v + jnp.concatenate(parts, axis=1)
        extd = jnp.concatenate([dxc, carry_scr[...]], axis=0)
        carry_scr[...] = dxc[0:CONV_HALO, :]
        cw = cw_ref[...]
        urx_v = urx_ref[...].astype(F32)
        durx = cw[3:4] * dxc
        dcw_ref[3:4, :] += jnp.sum(urx_v * dxc, axis=0, keepdims=True)
        for j in (1, 2, 3):
            dj = _shift_up(extd, j, ts)
            durx = durx + cw[3 - j:4 - j] * dj
            dcw_ref[3 - j:4 - j, :] += jnp.sum(urx_v * dj, axis=0, keepdims=True)
        durx_ref[...] = durx.astype(BF16)
        dcb_ref[...] += jnp.sum(dxc, axis=0, keepdims=True)

        @pl.when(i == nt - 1)
        def _():
            dlam_ref[...] = dlam_ref[...] * (LRU_C * jax.nn.sigmoid(-lam_v))

    row_spec = _rows(ts, D_MODEL, nt)
    halo_spec = pl.BlockSpec((CONV_HALO, D_MODEL), lambda i: (jnp.maximum((nt - 1 - i) * halo_blocks - 1, 0), 0))
    vec = _acc((1, D_MODEL))
    return _call(
        body, "bwd_rnn", (nt,),
        [row_spec] * 8 + [halo_spec, _resident(conv_w.shape), _resident(wg.shape), _resident((1, D_MODEL)),
                          _resident(w_rg_out.shape)],
        [row_spec, row_spec, _acc(w_rg_out.shape), _acc(wg.shape), _acc(conv_w.shape), vec, _acc((2, D_MODEL)), vec],
        [_sds((s, D_MODEL), BF16), _sds((s, D_MODEL), BF16), _sds(w_rg_out.shape), _sds(wg.shape), _sds(conv_w.shape),
         _sds((1, D_MODEL)), _sds((2, D_MODEL)), _sds((1, D_MODEL))],
        [pltpu.VMEM((8, D_MODEL), F32), pltpu.VMEM((CONV_HALO, D_MODEL), F32)],
        (dyr, urx, gl, gg, xc, r, ig, h, h, conv_w, wg, lam, w_rg_out), tasks)


def _bwd_pool(dyp, d, pool_w, pool_scale, w_pool_out, ts, tasks=()):
    s = d.shape[0]
    nt = s // ts

    def body(dyp_ref, d_ref, pw_ref, ps_ref, wpo_ref, dzp_ref, dwpo_ref, dpw_ref, dps_ref, carry_scr):
        i = pl.program_id(0)
        k = nt - 1 - i

        @pl.when(i == 0)
        def _():
            carry_scr[...] = jnp.zeros_like(carry_scr)
            dwpo_ref[...] = jnp.zeros_like(dwpo_ref)
            dpw_ref[...] = jnp.zeros_like(dpw_ref)
            dps_ref[...] = jnp.zeros_like(dps_ref)

        dyp_v = dyp_ref[...]
        d_v = d_ref[...]
        ps = ps_ref[...]
        dypre = _dot_nt(dyp_v[:, 0:256], wpo_ref[0])
        for j in range(1, N_CHIPS):
            dypre = dypre + _dot_nt(dyp_v[:, j * 256:(j + 1) * 256], wpo_ref[j])
        y4 = jnp.concatenate([_dot(d_v[:, g * 128:(g + 1) * 128], pw_ref[g]) for g in range(POOL_GROUPS)], axis=1)
        ypre = (y4 * ps).astype(BF16)
        for j in range(N_CHIPS):
            dwpo_ref[j] += _dot_tn(ypre, dyp_v[:, j * 256:(j + 1) * 256])
        dps_ref[...] += jnp.sum(dypre * y4, axis=0, keepdims=True)
        dy4 = dypre * ps
        t = k * ts + lax.broadcasted_iota(jnp.int32, (ts, 1), 0)
        for g, w in enumerate(POOL_WINDOWS):
            lanes = slice(g * POOL_GROUP_DIM, (g + 1) * POOL_GROUP_DIM)
            dd = _dot_nt(dy4[:, lanes], pw_ref[g])
            dpw_ref[g] += _dot_tn(d_v[:, lanes], dy4[:, lanes])
            e = dd * (1.0 / jnp.minimum(t + 1, w).astype(F32))
            acc = jnp.concatenate([e, carry_scr[:, lanes]], axis=0)
            carry_scr[:, lanes] = e[0:POOL_HALO, :]
            n = ts + POOL_HALO
            sh = 1
            while sh < w:
                acc = acc + pltpu.roll(acc, n - sh, 0)
                sh *= 2
            dzp_ref[:, lanes] = (acc[:ts, :] - dd).astype(BF16)

    return _call(
        body, "bwd_pool", (nt,),
        [_rows(ts, D_MODEL, nt), _rows(ts, POOL_WIDTH, nt), _resident(pool_w.shape), _resident((1, POOL_WIDTH)),
         _resident(w_pool_out.shape)],
        [_rows(ts, POOL_WIDTH, nt), _acc(w_pool_out.shape), _acc(pool_w.shape), _acc((1, POOL_WIDTH))],
        [_sds((s, POOL_WIDTH), BF16), _sds(w_pool_out.shape), _sds(pool_w.shape), _sds((1, POOL_WIDTH))],
        [pltpu.VMEM((POOL_HALO, POOL_WIDTH), F32)],
        (dyp, d, pool_w, pool_scale, w_pool_out), tasks)


def _assemble_dz(dz_scr, dzp_ref, durx_ref, durg_ref, dgp_ref, dgr_ref):
    dz_scr[:, 0:512] = dzp_ref[...]
    dz_scr[:, 512:1536] = durx_ref[...]
    dz_scr[:, 1536:2560] = durg_ref[...]
    dz_scr[:, 2560:3584] = dgp_ref[...]
    dz_scr[:, 3584:4608] = dgr_ref[...]


def _dw_in(h1, dzp, durx, durg, dgp, dgr, ts, tasks=()):
    s = h1.shape[0]
    ts = min(2 * ts, s)
    nt = s // ts
    half = D_MODEL // 2

    def body(h1_ref, dzp_ref, durx_ref, durg_ref, dgp_ref, dgr_ref, send_ref, acc_ref, theirs_ref, mine_ref, got_ref, sems):
        o, i = pl.program_id(0), pl.program_id(1)

        @pl.when((o == 0) & (i == 0))
        def _():
            theirs_ref[...] = jnp.zeros_like(theirs_ref)
            mine_ref[...] = jnp.zeros_like(mine_ref)

        groups = (dzp_ref, durx_ref, durg_ref, dgp_ref, dgr_ref)
        for out_ref, which in ((theirs_ref, 0), (mine_ref, 1)):
            @pl.when(o == which)
            def _():
                for group, dz_ref in enumerate(groups):
                    for j, shard_cols, group_cols in _overlaps(group):
                        out_ref[j, :, shard_cols] += _dot_tn(h1_ref[...], dz_ref[:, group_cols])

        x, y, c, me, _ = _place()
        send = _remote(theirs_ref, got_ref, sems.at[0], sems.at[1], (x, y, 1 - c))
        pl.when((o == 1) & (i == 0))(send.start)

        @pl.when((o == 1) & (i == nt - 1))
        def _():
            send.wait()
            for k in range(N_CHIPS):
                part = (mine_ref[k] + got_ref[k]).astype(BF16)
                send_ref[k] = part

                @pl.when(me == k)
                def _():
                    acc_ref[k] = part

    def h1_cols(o, i):
        c = lax.axis_index("c")
        return i, jnp.where(o == 0, 1 - c, c)

    rows = lambda width: pl.BlockSpec((ts, width), lambda o, i: (i, 0))
    shape = (N_CHIPS, half, IN_SHARD)
    whole = pl.BlockSpec(shape, lambda o, i: (0, 0, 0))
    return _call(
        body, "dw_in", (2, nt), [pl.BlockSpec((ts, half), h1_cols), rows(POOL_WIDTH)] + [rows(D_MODEL)] * 4,
        [whole, whole], [_sds(shape, BF16)] * 2,
        [pltpu.VMEM(shape, F32)] * 3 + [pltpu.SemaphoreType.DMA((2,))], (h1, dzp, durx, durg, dgp, dgr), tasks)


def _bwd_in(dzp, durx, durg, dgp, dgr, x, dx1, g_pre, w_in, ts, tasks=()):
    s = x.shape[0]

    def body(dzp_ref, durx_ref, durg_ref, dgp_ref, dgr_ref, x_ref, dx1_ref, g_ref, win_ref, gx_ref, dg_ref, dz_scr):
        @pl.when(pl.program_id(0) == 0)
        def _():
            dg_ref[...] = jnp.zeros_like(dg_ref)

        _assemble_dz(dz_scr, dzp_ref, durx_ref, durg_ref, dgp_ref, dgr_ref)
        _, xh, r = _rms_fwd(x_ref[...], g_ref[...])
        dh1 = _dot_nt(dz_scr[:, 0:IN_SHARD], win_ref[0])
        for j in range(1, N_CHIPS):
            dh1 = dh1 + _dot_nt(dz_scr[:, j * IN_SHARD:(j + 1) * IN_SHARD], win_ref[j])
        dx, dg = _rms_bwd(xh, r, g_ref[...], dh1)
        dg_ref[...] += dg
        gx_ref[...] = dx1_ref[...] + dx

    row = _rows(ts, D_MODEL)
    return _call(
        body, "bwd_in", (s // ts,),
        [_rows(ts, POOL_WIDTH)] + [row] * 6 + [_resident((1, D_MODEL)), _resident(w_in.shape)],
        [row, _acc((1, D_MODEL))], [_sds((s, D_MODEL)), _sds((1, D_MODEL))],
        [pltpu.VMEM((ts, IN_TOTAL), BF16)], (dzp, durx, durg, dgp, dgr, x, dx1, g_pre, w_in), tasks)


def _place():
    x, y, c = lax.axis_index("x"), lax.axis_index("y"), lax.axis_index("c")
    others = [(1 - x, y), (x, 1 - y), (1 - x, 1 - y)]
    return x, y, c, 2 * x + y, others


def _remote(src, dst, send_sem, recv_sem, to):
    return pltpu.make_async_remote_copy(src_ref=src, dst_ref=dst, send_sem=send_sem, recv_sem=recv_sem,
                                        device_id=to, device_id_type=MESH)


def _own_slots(ws, dtypes, name, tasks=()):
    n = len(ws)
    hbm = pl.BlockSpec(memory_space=pltpu.HBM)

    def body(*refs):
        srcs, outs, f32_bufs, cast_bufs, sems = refs[:n], refs[n:2 * n], refs[2 * n:3 * n], refs[3 * n:4 * n], refs[4 * n]
        me = _place()[3]
        loads = [pltpu.make_async_copy(srcs[k], f32_bufs[k], sems.at[k, 0]) for k in range(n)]
        stores = [pltpu.make_async_copy(cast_bufs[k], outs[k].at[me], sems.at[k, 1]) for k in range(n)]
        for cp in loads:
            cp.start()
        for k in range(n):
            loads[k].wait()
            cast_bufs[k][...] = f32_bufs[k][...].astype(dtypes[k])
            stores[k].start()
        for cp in stores:
            cp.wait()

    return _call(
        body, name, (), [hbm] * n, [hbm] * n, [_sds((N_CHIPS,) + w.shape, dt) for w, dt in zip(ws, dtypes)],
        [pltpu.VMEM(w.shape, F32) for w in ws] + [pltpu.VMEM(w.shape, dt) for w, dt in zip(ws, dtypes)]
        + [pltpu.SemaphoreType.DMA((n, 2))],
        [pltpu.with_memory_space_constraint(w, pltpu.HBM) for w in ws], tasks)


def _run(tasks, name):
    if isinstance(tasks, _Task):
        return _call(None, name, (), [], [], [], [], (), (tasks,))[1][0]
    return _call(None, name, (), [], [], [], [], (), tuple(tasks))[1]


def _gather_task(bufs, relay_steps=(0, 0)):
    n = len(bufs)
    NBR_X, NBR_Y, QUARTER_VIA_Y, QUARTER_VIA_X, SIB_X, SIB_Y, SIB_DIAG = range(7)

    def parts(out):
        x, y, c, me, _ = _place()
        ah = out.shape[1] // 2
        q = ah // 2 if (ah // 2) % 16 == 0 else ah
        return c * ah, ah, q

    def copy(out, w, k, chip, row0, rows, to, sems):
        slot = out.at[chip, pl.ds(row0, rows)]
        return _remote(slot, slot, sems[0].at[w, k], sems[1].at[w, k], to)

    def plan(out, w, sems):
        x, y, c, me, _ = _place()
        row0, ah, q = parts(out)
        xn, yn, dg = 2 * (1 - x) + y, 2 * x + (1 - y), 2 * (1 - x) + (1 - y)
        to_x, to_y, sib = (1 - x, y, c), (x, 1 - y, c), (x, y, 1 - c)
        other = (1 - c) * ah
        cp = functools.partial(copy, out, w, sems=sems)
        sends = {NBR_X: cp(NBR_X, me, row0, ah, to_x), NBR_Y: cp(NBR_Y, me, row0, ah, to_y),
                 QUARTER_VIA_Y: cp(QUARTER_VIA_Y, xn, row0, q, to_y), SIB_X: cp(SIB_X, xn, row0, ah, sib),
                 SIB_Y: cp(SIB_Y, yn, row0, ah, sib), SIB_DIAG: cp(SIB_DIAG, dg, row0, ah, sib)}
        lands = {NBR_X: cp(NBR_X, xn, row0, ah, to_x), NBR_Y: cp(NBR_Y, yn, row0, ah, to_y),
                 QUARTER_VIA_Y: cp(QUARTER_VIA_Y, dg, row0, q, to_y), SIB_X: cp(SIB_X, xn, other, ah, sib),
                 SIB_Y: cp(SIB_Y, yn, other, ah, sib), SIB_DIAG: cp(SIB_DIAG, dg, other, ah, sib)}
        if q < ah:
            sends[QUARTER_VIA_X] = cp(QUARTER_VIA_X, yn, row0 + q, ah - q, to_x)
            lands[QUARTER_VIA_X] = cp(QUARTER_VIA_X, dg, row0 + q, ah - q, to_x)
        return sends, lands

    def start(ins, outs, sems):
        for w, out in enumerate(outs):
            sends, _ = plan(out, w, sems)
            sends[NBR_X].start()
            sends[NBR_Y].start()

    def pass_neighbours(ins, outs, sems):
        for w, out in enumerate(outs):
            sends, lands = plan(out, w, sems)
            lands[NBR_X].wait_recv()
            sends[QUARTER_VIA_Y].start()
            sends[SIB_X].start()
            lands[NBR_Y].wait_recv()
            if QUARTER_VIA_X in sends:
                sends[QUARTER_VIA_X].start()
            sends[SIB_Y].start()

    def pass_diagonal(ins, outs, sems):
        for w, out in enumerate(outs):
            sends, lands = plan(out, w, sems)
            lands[QUARTER_VIA_Y].wait_recv()
            if QUARTER_VIA_X in lands:
                lands[QUARTER_VIA_X].wait_recv()
            sends[SIB_DIAG].start()

    def finish(ins, outs, sems):
        for w, out in enumerate(outs):
            sends, lands = plan(out, w, sems)
            for k in (SIB_X, SIB_Y, SIB_DIAG):
                lands[k].wait_recv()
        for w, out in enumerate(outs):
            sends, _ = plan(out, w, sems)
            for cp in sends.values():
                cp.wait_send()

    return _Task(bufs, [_sds(b.shape, b.dtype) for b in bufs], {i: i for i in range(n)},
                 [pltpu.SemaphoreType.DMA((n, 7)), pltpu.SemaphoreType.DMA((n, 7))], start, finish,
                 [(pass_neighbours, relay_steps[0]), (pass_diagonal, relay_steps[1])])


def _halves_task(grads):
    n = len(grads)

    def copy(src, out, w, sems):
        x, y, c, _, _ = _place()
        ah = out.shape[1]
        return _remote(src.at[:, pl.ds((1 - c) * ah, ah)], out, sems[0].at[w], sems[1].at[w], (x, y, 1 - c))

    def start(ins, outs, sems):
        for w, (src, out) in enumerate(zip(ins, outs)):
            copy(src, out, w, sems).start()

    def finish(ins, outs, sems):
        for w, (src, out) in enumerate(zip(ins, outs)):
            copy(src, out, w, sems).wait()

    return _Task(grads, [_sds((g.shape[0], g.shape[1] // 2, g.shape[2]), g.dtype) for g in grads], {},
                 [pltpu.SemaphoreType.DMA((n,)), pltpu.SemaphoreType.DMA((n,))], start, finish)


def _exchange_task(sends, accs):
    n = len(accs)
    given = [s for s in sends if s is not None]

    def copies(ins, outs, sems):
        send_refs = iter(ins[:len(given)])
        srcs = [next(send_refs) if s is not None else None for s in sends]
        x, y, c, me, others = _place()
        for w, out in enumerate(outs):
            for j, (ox, oy) in enumerate(others):
                src = out.at[me] if srcs[w] is None else srcs[w].at[2 * ox + oy]
                yield _remote(src, out.at[me], sems[0].at[w, j], sems[1].at[w, j], (ox, oy, c))

    def start(ins, outs, sems):
        for cp in copies(ins, outs, sems):
            cp.start()

    def finish(ins, outs, sems):
        x, y, c, _, others = _place()
        for w, out in enumerate(outs):
            for j, (ox, oy) in enumerate(others):
                slot = out.at[2 * ox + oy]
                _remote(slot, slot, sems[0].at[w, j], sems[1].at[w, j], (ox, oy, c)).wait_recv()
        for cp in copies(ins, outs, sems):
            cp.wait_send()

    return _Task(given + list(accs), [_sds(a.shape, a.dtype) for a in accs], {len(given) + i: i for i in range(n)},
                 [pltpu.SemaphoreType.DMA((n, 3)), pltpu.SemaphoreType.DMA((n, 3))], start, finish)


def _swap_task(arrays):
    n = len(arrays)

    def copy(src, out, w, sems):
        x, y, c, _, _ = _place()
        return _remote(src, out, sems[0].at[w], sems[1].at[w], (x, y, 1 - c))

    def start(ins, outs, sems):
        for w, (src, out) in enumerate(zip(ins, outs)):
            copy(src, out, w, sems).start()

    def finish(ins, outs, sems):
        for w, (src, out) in enumerate(zip(ins, outs)):
            copy(src, out, w, sems).wait()

    return _Task(arrays, [_sds(a.shape, a.dtype) for a in arrays], {},
                 [pltpu.SemaphoreType.DMA((n,)), pltpu.SemaphoreType.DMA((n,))], start, finish)


def _all_devices_task(arrays):
    n = len(arrays)
    flips = [(dx, dy, dc) for dx in (0, 1) for dy in (0, 1) for dc in (0, 1)][1:]

    def peers():
        x, y, c, _, _ = _place()
        flip = lambda v, d: 1 - v if d else v
        return 4 * x + 2 * y + c, [(flip(x, dx), flip(y, dy), flip(c, dc)) for dx, dy, dc in flips]

    def start(ins, outs, sems):
        me, others = peers()
        for w, (src, out) in enumerate(zip(ins, outs)):
            pltpu.make_async_copy(src, out.at[me], sems[2].at[w]).start()
            for k, peer in enumerate(others):
                _remote(src, out.at[me], sems[0].at[w, k], sems[1].at[w, k], peer).start()

    def finish(ins, outs, sems):
        me, others = peers()
        for w, (src, out) in enumerate(zip(ins, outs)):
            for k, (px, py, pc) in enumerate(others):
                slot = out.at[4 * px + 2 * py + pc]
                _remote(slot, slot, sems[0].at[w, k], sems[1].at[w, k], (px, py, pc)).wait_recv()
            for k, peer in enumerate(others):
                _remote(src, out.at[me], sems[0].at[w, k], sems[1].at[w, k], peer).wait_send()
            pltpu.make_async_copy(src, out.at[me], sems[2].at[w]).wait()

    return _Task(arrays, [_sds((8,) + a.shape, a.dtype) for a in arrays], {},
                 [pltpu.SemaphoreType.DMA((n, 7)), pltpu.SemaphoreType.DMA((n, 7)), pltpu.SemaphoreType.DMA((n,))],
                 start, finish)


def _share_task(shares):
    n = len(shares)

    def copy(out, w, sems, slot):
        x, y, c, _, _ = _place()
        return _remote(out.at[slot], out.at[slot], sems[0].at[w], sems[1].at[w], (x, y, 1 - c))

    def start(ins, outs, sems):
        c = _place()[2]
        for w, out in enumerate(outs):
            copy(out, w, sems, c).start()

    def finish(ins, outs, sems):
        c = _place()[2]
        for w, out in enumerate(outs):
            copy(out, w, sems, 1 - c).wait_recv()
        for w, out in enumerate(outs):
            copy(out, w, sems, c).wait_send()

    return _Task(shares, [_sds(s.shape, s.dtype) for s in shares], {i: i for i in range(n)},
                 [pltpu.SemaphoreType.DMA((n,)), pltpu.SemaphoreType.DMA((n,))], start, finish)


TILE_BYTES = 2 * 1024 * 1024
PARTIAL_TILE_BYTES = 1024 * 1024


def _in_hbm(t):
    return pltpu.with_memory_space_constraint(t, pltpu.HBM)


def _row_tile(rows, cols, limit=TILE_BYTES):
    best = 8
    for tr in range(8, rows + 1, 8):
        if rows % tr == 0 and tr * cols * 4 <= limit:
            best = tr
    assert rows % best == 0, (rows, cols)
    return best


def _chip_partial(g, got, place, wire_dtype):
    ns, ah, b = got.shape
    sharded = ns == N_CHIPS
    tr = _row_tile(ah, b, PARTIAL_TILE_BYTES)
    nb = ah // tr
    first = 0 if g.shape[1] == ah else nb

    def body(place_ref, *refs):
        g_refs, got_refs, outs = refs[:ns], refs[ns:2 * ns], refs[2 * ns:]
        parts = [g_refs[k][0] + got_refs[k][0] for k in range(ns)]
        own = parts[0]
        if sharded:
            for k in range(ns):
                outs[0][k] = parts[k].astype(wire_dtype)
                if k:
                    own = jnp.where(place_ref[0] == k, parts[k], own)
        outs[-1][0] = own.astype(wire_dtype)

    blk = (1, tr, b)
    in_specs = ([pl.BlockSpec(blk, lambda i, s, k=k: (k, s[1] * first + i, 0)) for k in range(ns)]
                + [pl.BlockSpec(blk, lambda i, s, k=k: (k, i, 0)) for k in range(ns)])
    acc_spec = pl.BlockSpec(blk, lambda i, s: (s[0], i, 0))
    acc_shape = _sds((N_CHIPS, ah, b), wire_dtype)
    out = pl.pallas_call(
        body, name="grad_chip_partial",
        grid_spec=pltpu.PrefetchScalarGridSpec(
            num_scalar_prefetch=1, grid=(nb,), in_specs=in_specs,
            out_specs=[pl.BlockSpec((ns, tr, b), lambda i, s: (0, i, 0)), acc_spec] if sharded else [acc_spec]),
        out_shape=[acc_shape, acc_shape] if sharded else [acc_shape],
        compiler_params=pltpu.CompilerParams(dimension_semantics=("arbitrary",), vmem_limit_bytes=VMEM_LIMIT),
    )(place, *([g] * ns), *([got] * ns))
    return (out[0], out[1]) if sharded else (None, out[0])


def _chip_sum(acc, place):
    _, ah, b = acc.shape
    tr = _row_tile(ah, b)

    def body(place_ref, p_ref, out_ref):
        total = p_ref[0].astype(F32) + p_ref[1].astype(F32)
        total = total + p_ref[2].astype(F32)
        out_ref[0] = total + p_ref[3].astype(F32)

    return pl.pallas_call(
        body, name="grad_chip_sum",
        grid_spec=pltpu.PrefetchScalarGridSpec(
            num_scalar_prefetch=1, grid=(ah // tr,),
            in_specs=[pl.BlockSpec((N_CHIPS, tr, b), lambda i, s: (0, i, 0))],
            out_specs=pl.BlockSpec((1, tr, b), lambda i, s: (s[1], i, 0))),
        out_shape=_sds((2, ah, b)),
        compiler_params=pltpu.CompilerParams(dimension_semantics=("arbitrary",)),
    )(place, _in_hbm(acc))


def _adam_math(w, g, m, v):
    nm = ADAM_B1 * m + (1.0 - ADAM_B1) * g
    nv = ADAM_B2 * v + (1.0 - ADAM_B2) * (g * g)
    m_hat = nm / (1.0 - ADAM_B1 ** ADAM_STEP)
    v_hat = nv / (1.0 - ADAM_B2 ** ADAM_STEP)
    return -ADAM_LR * (m_hat / (jnp.sqrt(v_hat) + ADAM_EPS) + ADAM_WD * w), nm, nv


def _adamw(w, g, m, v):
    a, b = w.shape
    tr = _row_tile(a, b)

    def body(w_ref, g_ref, m_ref, v_ref, g_out, d_ref, nm_ref, nv_ref):
        g_out[...] = g_ref[...]
        d_ref[...], nm_ref[...], nv_ref[...] = _adam_math(w_ref[...], g_ref[...], m_ref[...], v_ref[...])

    blk = pl.BlockSpec((tr, b), lambda i: (i, 0))
    return pl.pallas_call(
        body, name="adamw", grid=(a // tr,),
        in_specs=[blk] * 4, out_specs=[blk] * 4, out_shape=[_sds((a, b))] * 4,
        compiler_params=pltpu.CompilerParams(dimension_semantics=("arbitrary",)),
    )(w, g, m, v)


def _adamw_sum(w, m, v, acc, got, place):
    a, b = w.shape
    ah = a // 2
    tr = _row_tile(ah, b)
    nb = ah // tr

    def body(place_ref, w_ref, m_ref, v_ref, acc_ref, got_ref, g_out, d_ref, nm_ref, nv_ref):
        mine = (pl.program_id(0) // nb) == place_ref[1]
        part = lambda k: jnp.where(mine, acc_ref[k], got_ref[k]).astype(F32)
        g = part(0) + part(1)
        g = g + part(2)
        g = g + part(3)
        g_out[...] = g
        d_ref[...], nm_ref[...], nv_ref[...] = _adam_math(w_ref[...], g, m_ref[...], v_ref[...])

    blk = pl.BlockSpec((tr, b), lambda i, s: (i, 0))
    mine_spec = pl.BlockSpec((N_CHIPS, tr, b), lambda i, s: (0, jnp.where(i // nb == s[1], i % nb, 0), 0))
    got_spec = pl.BlockSpec((N_CHIPS, tr, b), lambda i, s: (0, jnp.where(i // nb == s[1], 0, i % nb), 0))
    return pl.pallas_call(
        body, name="adamw_sum",
        grid_spec=pltpu.PrefetchScalarGridSpec(
            num_scalar_prefetch=1, grid=(a // tr,), in_specs=[blk] * 3 + [mine_spec, got_spec], out_specs=[blk] * 4),
        out_shape=[_sds((a, b))] * 4,
        compiler_params=pltpu.CompilerParams(dimension_semantics=("arbitrary",), vmem_limit_bytes=VMEM_LIMIT),
    )(place, w, m, v, _in_hbm(acc), _in_hbm(got))


def _adamw_pieces(g, pieces, name):
    n = len(pieces)

    def body(g_ref, *refs):
        def grad(rows, cols):
            if len(g_ref.shape) == 2:
                return g_ref[rows, cols]
            total = g_ref[0, rows, cols]
            for k in range(1, g_ref.shape[0]):
                total = total + g_ref[k, rows, cols]
            return total

        ins, outs = refs[:3 * n], refs[3 * n:]
        for i, piece in enumerate(pieces):
            w_ref, m_ref, v_ref = ins[3 * i:3 * i + 3]
            o_g, o_d, o_m, o_v = outs[4 * i:4 * i + 4]
            if len(piece) == 5:
                g_v = grad(piece[3], piece[4])
                o_g[...] = g_v
                o_d[...], o_m[...], o_v[...] = _adam_math(w_ref[...], g_v, m_ref[...], v_ref[...])
            else:
                for r in range(w_ref.shape[1] // SMALL_COLS):
                    lanes = slice(r * SMALL_COLS, (r + 1) * SMALL_COLS)
                    g_v = grad(slice(piece[3] + r, piece[3] + r + 1), slice(None))
                    o_g[:, lanes] = g_v
                    o_d[:, lanes], o_m[:, lanes], o_v[:, lanes] = _adam_math(w_ref[:, lanes], g_v, m_ref[:, lanes],
                                                                            v_ref[:, lanes])

    operands = [t for piece in pieces for t in piece[:3]]
    out = pl.pallas_call(
        body, name=name,
        out_shape=[_sds(piece[0].shape) for piece in pieces for _ in range(4)],
    )(g, *operands)
    return [tuple(out[4 * i:4 * i + 4]) for i in range(n)]


TINY_ROWS, TINY_COLS = 16, 768
SMALL_COLS = 128
SMALL_ROWS = 624


def _pack_tiny(conv_w, b_gates, fcw):
    ns = conv_w.shape[0]
    pad = lambda t: jnp.pad(t, ((0, 0), (0, 0), (0, TINY_COLS - t.shape[2])))
    z = lambda rows: jnp.zeros((ns, rows, TINY_COLS), F32)
    return jnp.concatenate([pad(conv_w), pad(b_gates), z(2), fcw, z(TINY_ROWS - 11)], axis=1)


def _unpack_tiny(t):
    return t[:, 0:4, 0:256], t[:, 4:6, 0:256], t[:, 8:11, :]


def _cols_to_shards(t, n):
    return t.reshape(t.shape[0], N_CHIPS, n).transpose(1, 0, 2)


def _shards_to_cols(t):
    return t.transpose(1, 0, 2).reshape(t.shape[1], -1)


_VECTORS = ("g_mix_post", "conv_b", "lru_lambda", "g_ffn_pre", "g_ffn_post", "g_ple_gate", "g_ple_post", "pool_scale",
            "ffn_conv_b")
_VECTOR_LEN = {"pool_scale": POOL_WIDTH, "ffn_conv_b": D_FF}
POOL_W_ROWS = POOL_GROUPS * POOL_GROUP_DIM


def _vector_rows():
    rows, row = {}, POOL_W_ROWS
    for k in _VECTORS:
        rows[k] = row
        row += max(8, _VECTOR_LEN.get(k, D_MODEL) // SMALL_COLS)
    return rows, row


def _pack_small(grads, loss):
    tiles = lambda t: jnp.pad(t, ((0, -t.shape[0] % 8), (0, 0)))
    parts = [grads["pool_w"].reshape(POOL_W_ROWS, SMALL_COLS)] + [tiles(grads[k].reshape(-1, SMALL_COLS)) for k in _VECTORS]
    parts.append(tiles(loss))
    used = sum(t.shape[0] for t in parts)
    return jnp.concatenate(parts + [jnp.zeros((SMALL_ROWS - used, SMALL_COLS), F32)], axis=0)


def _gates_block_diag(w):
    w4 = w.reshape(2, GATE_BLOCKS, 4, RNN_HEAD_DIM, RNN_HEAD_DIM)
    eye = jnp.eye(4, dtype=w.dtype)
    return jnp.einsum("gqhij,hk->gqhikj", w4, eye).reshape(2, GATE_BLOCKS, GATE_BLOCK, GATE_BLOCK)


def _gates_from_block_diag(dw):
    d6 = dw.reshape(2, GATE_BLOCKS, 4, RNN_HEAD_DIM, 4, RNN_HEAD_DIM)
    blocks = [d6[:, :, hh, :, hh, :] for hh in range(4)]
    return jnp.stack(blocks, axis=2).reshape(2, RNN_HEADS, RNN_HEAD_DIM, RNN_HEAD_DIM)


ROW_TILE = 256
DW_TILES = 8

_SHARDED = ("w_in", "w_pool_out", "w_rg_out", "w_o", "w_up", "w_down", "w_ple_gate", "w_ple_proj")
_WEIGHTS = ("g_mix_pre", "g_mix_post", "w_in", "pool_w", "pool_scale", "w_pool_out", "conv_w", "conv_b", "w_rg_gates",
            "b_rg_gates", "lru_lambda", "w_rg_out", "w_o", "g_ffn_pre", "g_ffn_post", "w_up", "ffn_conv_w", "ffn_conv_b",
            "w_down", "g_ple_gate", "w_ple_gate", "w_ple_proj", "g_ple_post")


def _wire_dtype(g):
    return BF16 if g.shape[1] >= 64 and g.shape[2] > SMALL_COLS else F32


def _partials(grads, got, place):
    parts = [_chip_partial(g, r, place, _wire_dtype(g)) for g, r in zip(grads, got)]
    return [send for send, _ in parts], [acc for _, acc in parts]


def _whole(both):
    return [b.reshape(2 * b.shape[1], b.shape[2]) for b in both]


def _step(x, p, tgt, rep, place, ts):
    vec = lambda k: rep[k].reshape(1, -1)
    tall = min(2 * ts, x.shape[0])
    pool_w = rep["pool_w"].astype(BF16)
    wg = _gates_block_diag(rep["w_rg_gates"]).astype(BF16)
    sq = lambda t: t.reshape(D_MODEL, D_MODEL)
    by4 = lambda t: t.reshape(N_CHIPS, -1, D_MODEL)

    first, ride1, ride2 = (("w_in", "w_pool_out", "tiny"), ("w_rg_out", "w_o", "w_down"),
                           ("w_up", "w_ple_gate", "w_ple_proj"))
    later = ride1 + ride2
    tiny = _pack_tiny(rep["conv_w"][None], rep["b_rg_gates"][None], rep["ffn_conv_w"][None])[0]
    own_first, _ = _own_slots([rep["w_in"], rep["w_pool_out"], tiny], [BF16, BF16, F32], "own_slots_first")
    own_later, (got,) = _own_slots([rep[k] for k in later], [BF16] * len(later), "own_slots_gather_first",
                                   [_gather_task(own_first)])
    own = dict(zip(later, own_later))
    full = dict(zip(first, got))
    conv_w, b_gates, fcw = [_shards_to_cols(t) for t in _unpack_tiny(full["tiny"])]

    (urx, urg, gp, gr, d, ypool, h1), (got,) = _fwd_in_pool(
        x, vec("g_mix_pre"), full["w_in"], pool_w, vec("pool_scale"), full["w_pool_out"], ts,
        [_gather_task([own[k] for k in ride1], relay_steps=(6, 2))])
    full.update(zip(ride1, got))
    w_rg_out, w_o, w_down = sq(full["w_rg_out"]), sq(full["w_o"]), full["w_down"].reshape(D_FF, D_MODEL)
    (xc, r, ig, h, yrnn, mo, x1, glr, ggr, sp, sr), (got,) = _fwd_rnn_merge(
        urx, urg, gp, gr, ypool, x, conv_w, vec("conv_b"), wg, b_gates, vec("lru_lambda"), w_rg_out, w_o,
        vec("g_mix_post"), ts, [_gather_task([own[k] for k in ride2], relay_steps=(7, 3))])
    full.update(zip(ride2, got))
    (up, gl, gg, h2, dn, x2), _ = _fwd_ffn(x1, vec("g_ffn_pre"), full["w_up"], fcw, vec("ffn_conv_b"), w_down,
                                           vec("g_ffn_post"), ts)
    dx2, loss, d_w_gate, d_w_proj, d_g_ple_gate, d_g_ple_post = _ple_loss(
        x2, p, tgt, vec("g_ple_gate"), sq(full["w_ple_gate"]), full["w_ple_proj"], vec("g_ple_post"), tall)
    dup, d_w_down, d_fcw, d_fcb, d_g_ffn_post = _bwd_ffn_down(dx2, dn, up, gl, gg, fcw, w_down, vec("g_ffn_post"), ts)

    names1, grads1 = ("w_ple_gate", "w_ple_proj", "w_down"), [by4(d_w_gate), d_w_proj, by4(d_w_down)]
    (dx1, d_g_ffn_pre), (got1,) = _bwd_ffn_up(dup, x1, dx2, vec("g_ffn_pre"), full["w_up"], tall, [_halves_task(grads1)])
    (send_w_up, acc_w_up), (accs1,) = _dw_up(h2, dup, ts, [_exchange_task(*_partials(grads1, got1, place))])
    (dgp, dgr, dyp, dyr, d_w_o, d_g_mix_post), (theirs1,) = _bwd_merge(
        dx1, mo, sp, sr, ypool, yrnn, vec("g_mix_post"), w_o, tall, [_swap_task(accs1)])
    (durx, durg, d_w_rg_out, d_wg, d_conv_w, d_conv_b, d_b_gates, d_lam), (accs2,) = _bwd_rnn(
        dyr, urx, glr, ggr, xc, r, ig, h, conv_w, wg, vec("lru_lambda"), w_rg_out, ts,
        [_exchange_task([send_w_up], [acc_w_up])])
    names3 = ("w_o", "w_rg_out", "tiny", "w_rg_gates")
    grads3 = [by4(d_w_o), by4(d_w_rg_out),
              _pack_tiny(_cols_to_shards(d_conv_w, 256), _cols_to_shards(d_b_gates, 256), _cols_to_shards(d_fcw, 768)),
              _gates_from_block_diag(d_wg).reshape(1, 2 * RNN_HEADS * RNN_HEAD_DIM, RNN_HEAD_DIM)]
    (dzp, d_w_pool_out, d_pool_w, d_pool_scale), (got3, theirs2) = _bwd_pool(
        dyp, d, pool_w, vec("pool_scale"), full["w_pool_out"], tall, [_halves_task(grads3), _swap_task(accs2)])
    replicated = {"g_mix_post": d_g_mix_post, "conv_b": d_conv_b, "lru_lambda": d_lam, "g_ffn_pre": d_g_ffn_pre,
                  "g_ffn_post": d_g_ffn_post, "g_ple_gate": d_g_ple_gate, "g_ple_post": d_g_ple_post,
                  "pool_scale": d_pool_scale, "ffn_conv_b": d_fcb, "pool_w": d_pool_w}
    names4 = ("w_in", "w_pool_out", "small")
    small4 = [d_w_pool_out, _pack_small(replicated, loss)[None]]
    (send_w_in, acc_w_in), (accs3, got_small4) = _dw_in(
        h1, dzp, durx, durg, dgp, dgr, ts, [_exchange_task(*_partials(grads3, got3, place)), _halves_task(small4)])
    sends4, accs4 = _partials(small4, got_small4, place)
    (grad_x, d_g_mix_pre), (accs4, theirs3, both3) = _bwd_in(
        dzp, durx, durg, dgp, dgr, x, dx1, vec("g_mix_pre"), full["w_in"], tall,
        [_exchange_task([send_w_in] + sends4, [acc_w_in] + accs4), _swap_task(accs3[:2]),
         _share_task([_chip_sum(acc, place) for acc in accs3[2:]])])
    theirs4, both4, (g_mix_pre_parts,) = _run(
        [_swap_task(accs4[:2]), _share_task([_chip_sum(acc, place) for acc in accs4[2:]]),
         _all_devices_task([d_g_mix_pre.reshape(SUBLANES, SMALL_COLS)])], "grad_sibling_share")
    mine = accs1 + accs2 + accs3[:2] + accs4[:2]
    partials = dict(zip(names1 + ("w_up",) + names3[:2] + names4[:2], zip(mine, theirs1 + theirs2 + theirs3 + theirs4)))
    return grad_x, partials, dict(zip(names3[2:] + names4[2:], _whole(both3) + _whole(both4))), g_mix_pre_parts


def kernel(x, p, g_mix_pre, g_mix_post, w_in, pool_w, pool_scale, w_pool_out, conv_w, conv_b, w_rg_gates, b_rg_gates, lru_lambda, w_rg_out, w_o, g_ffn_pre, g_ffn_post, w_up, ffn_conv_w, ffn_conv_b, w_down, g_ple_gate, w_ple_gate, w_ple_proj, g_ple_post, loss_target, m_g_mix_pre, m_g_mix_post, m_w_in, m_pool_w, m_pool_scale, m_w_pool_out, m_conv_w, m_conv_b, m_w_rg_gates, m_b_rg_gates, m_lru_lambda, m_w_rg_out, m_w_o, m_g_ffn_pre, m_g_ffn_post, m_w_up, m_ffn_conv_w, m_ffn_conv_b, m_w_down, m_g_ple_gate, m_w_ple_gate, m_w_ple_proj, m_g_ple_post, v_g_mix_pre, v_g_mix_post, v_w_in, v_pool_w, v_pool_scale, v_w_pool_out, v_conv_w, v_conv_b, v_w_rg_gates, v_b_rg_gates, v_lru_lambda, v_w_rg_out, v_w_o, v_g_ffn_pre, v_g_ffn_post, v_w_up, v_ffn_conv_w, v_ffn_conv_b, v_w_down, v_g_ple_gate, v_w_ple_gate, v_w_ple_proj, v_g_ple_post):
    args = dict(locals())
    w = {k: args[k][0] for k in _WEIGHTS}
    m = {k: args["m_" + k][0] for k in _WEIGHTS}
    v = {k: args["v_" + k][0] for k in _WEIGHTS}
    place = jnp.stack([2 * lax.axis_index("x") + lax.axis_index("y"), lax.axis_index("c")]).astype(jnp.int32)
    grad_x, partials, reduced, g_mix_pre_parts = _step(x[0], p[0, 0], loss_target[0], w, place, ROW_TILE)

    gates_2d = (2 * RNN_HEADS * RNN_HEAD_DIM, RNN_HEAD_DIM)
    as2d = lambda k, shape: tuple(t[k].reshape(shape) for t in (w, m, v))
    done = {k: tuple(_adamw_sum(w[k], m[k], v[k], *partials[k], place)) for k in _SHARDED}
    gates_w, gates_m, gates_v = as2d("w_rg_gates", gates_2d)
    done["w_rg_gates"] = tuple(_adamw(gates_w, reduced["w_rg_gates"], gates_m, gates_v))
    tiny_names = ("conv_w", "b_rg_gates", "ffn_conv_w")
    tiny_at = ((slice(0, 4), slice(0, 256)), (slice(4, 6), slice(0, 256)), (slice(8, 11), slice(None)))
    done.update(zip(tiny_names, _adamw_pieces(
        reduced["tiny"], [(w[k], m[k], v[k]) + at for k, at in zip(tiny_names, tiny_at)], "adamw_tiny")))
    vector_rows, loss_row = _vector_rows()
    pieces = [as2d("pool_w", (POOL_W_ROWS, SMALL_COLS)) + (slice(0, POOL_W_ROWS), slice(None))]
    pieces += [as2d(k, (1, -1)) + (vector_rows[k],) for k in _VECTORS]
    done.update(zip(("pool_w",) + _VECTORS, _adamw_pieces(reduced["small"], pieces, "adamw_small")))
    done["g_mix_pre"] = _adamw_pieces(g_mix_pre_parts, [as2d("g_mix_pre", (1, -1)) + (0,)], "adamw_g_mix_pre")[0]

    result = [reduced["small"][loss_row, 0], grad_x[None]]
    for kind in range(4):
        result += [done[k][kind].reshape(args[k].shape) for k in _WEIGHTS]
    return tuple(result)
```

```python
import functools

import jax
import jax.numpy as jnp
from jax import lax
from jax.experimental import pallas as pl
from jax.experimental.pallas import tpu as pltpu

F32 = jnp.float32
BF16 = jnp.bfloat16

D_MODEL = 1024
POOL_WINDOWS = (2, 4, 8, 16)
POOL_GROUPS = 4
POOL_WIDTH = 512
POOL_GROUP_DIM = 128
RNN_HEADS = 16
RNN_HEAD_DIM = 64
GATE_BLOCK = 256
GATE_BLOCKS = D_MODEL // GATE_BLOCK
LRU_C = 8.0
D_FF = 3072
PLE_DIM = 256
RMS_EPS = 1e-6
IN_TOTAL = 4608
N_CHIPS = 4
IN_SHARD = IN_TOTAL // N_CHIPS
UP_SHARD = 2 * D_FF // N_CHIPS
Z_GROUPS = ((0, 512), (512, 1536), (1536, 2560), (2560, 3584), (3584, 4608))
POOL_HALO = 16
CONV_HALO = 8

ADAM_LR = 0.001
ADAM_B1 = 0.9
ADAM_B2 = 0.999
ADAM_EPS = 1e-08
ADAM_WD = 0.01
ADAM_STEP = 10

VMEM_LIMIT = 56 * 1024 * 1024
MESH = pl.DeviceIdType.MESH

_GELU_C = 0.7978845608028654
_GELU_A = 0.044715


def _dot(a, b):
    return jnp.dot(a.astype(BF16), b.astype(BF16), preferred_element_type=F32)


def _dot_nt(a, b):
    return lax.dot_general(a.astype(BF16), b.astype(BF16), (((1,), (1,)), ((), ())), preferred_element_type=F32)


def _dot_tn(a, b):
    return lax.dot_general(a.astype(BF16), b.astype(BF16), (((0,), (0,)), ((), ())), preferred_element_type=F32)


def _overlaps(group):
    a, b = Z_GROUPS[group]
    found = []
    for j in range(N_CHIPS):
        lo, hi = max(a, j * IN_SHARD), min(b, (j + 1) * IN_SHARD)
        if lo < hi:
            found.append((j, slice(lo - j * IN_SHARD, hi - j * IN_SHARD), slice(lo - a, hi - a)))
    return found


def _rms_fwd(x, g):
    r = lax.rsqrt(jnp.mean(x * x, axis=-1, keepdims=True) + RMS_EPS)
    xh = x * r
    return xh * g, xh, r


def _rms_bwd(xh, r, g, dy):
    dxh = dy * g
    dg = jnp.sum(dy * xh, axis=0, keepdims=True)
    dx = r * (dxh - xh * jnp.mean(dxh * xh, axis=-1, keepdims=True))
    return dx, dg


def _sigmoid(x):
    return 0.5 * jnp.tanh(0.5 * x) + 0.5


def _gelu(x):
    x2 = x * x
    t = jnp.tanh(x * (_GELU_C + (_GELU_C * _GELU_A) * x2))
    p = 0.5 * t + 0.5
    gl = x * p
    return gl, p + gl * (1.0 - p) * (2.0 * _GELU_C + (6.0 * _GELU_C * _GELU_A) * x2)


def _softplus_neg(lam):
    nl = -lam
    return jnp.maximum(nl, 0.0) + jnp.log(1.0 + jnp.exp(-jnp.abs(nl)))


def _lru_coeffs(r, lam, first_row):
    nla = (LRU_C * _softplus_neg(lam)) * r
    a = jnp.exp(-nla)
    m2 = jnp.tanh(nla) * (1.0 + a * a)
    mult = jnp.where(first_row, 1.0, jnp.sqrt(m2))
    return nla, a, m2, mult


SUBLANES = 8


def _scan_fwd(a, u, carry):
    n = a.shape[0]
    sub = lax.broadcasted_iota(jnp.int32, (n, 1), 0) % SUBLANES
    acc_a, acc_h = a, u
    for s in (1, 2, 4):
        m = sub >= s
        h_s = jnp.where(m, pltpu.roll(acc_h, s, 0), 0.0)
        a_s = jnp.where(m, pltpu.roll(acc_a, s, 0), 1.0)
        acc_h = acc_a * h_s + acc_h
        acc_a = acc_a * a_s
    out = []
    for g in range(n // SUBLANES):
        rows = slice(g * SUBLANES, (g + 1) * SUBLANES)
        out.append(acc_h[rows] + acc_a[rows] * carry)
        carry = out[-1][SUBLANES - 1:SUBLANES]
    return jnp.concatenate(out, axis=0)


def _scan_bwd(b, g, carry):
    n = b.shape[0]
    sub = lax.broadcasted_iota(jnp.int32, (n, 1), 0) % SUBLANES
    acc_b, acc_l = b, g
    for s in (1, 2, 4):
        m = sub < SUBLANES - s
        l_s = jnp.where(m, pltpu.roll(acc_l, n - s, 0), 0.0)
        b_s = jnp.where(m, pltpu.roll(acc_b, n - s, 0), 1.0)
        acc_l = acc_b * l_s + acc_l
        acc_b = acc_b * b_s
    out = [None] * (n // SUBLANES)
    for g in reversed(range(n // SUBLANES)):
        rows = slice(g * SUBLANES, (g + 1) * SUBLANES)
        out[g] = acc_l[rows] + acc_b[rows] * carry
        carry = out[g][0:1]
    return jnp.concatenate(out, axis=0)


def _shift_down(ext, k, halo):
    return pltpu.roll(ext, k, 0)[halo:] if k else ext[halo:]


def _shift_up(ext, k, ts):
    return pltpu.roll(ext, ext.shape[0] - k, 0)[:ts] if k else ext[:ts]


def _rows(ts, width, nt=None, col=0):
    if nt is None:
        return pl.BlockSpec((ts, width), lambda i: (i, col))
    return pl.BlockSpec((ts, width), lambda i: (nt - 1 - i, col))


def _resident(shape):
    zeros = (0,) * len(shape)
    return pl.BlockSpec(shape, lambda i: zeros, pipeline_mode=pl.Buffered(1))


def _acc(shape):
    zeros = (0,) * len(shape)
    return pl.BlockSpec(shape, lambda i: zeros)


def _params():
    return pltpu.CompilerParams(dimension_semantics=("arbitrary",), vmem_limit_bytes=VMEM_LIMIT)


def _sds(shape, dtype=F32):
    return jax.ShapeDtypeStruct(shape, dtype)


class _Task:
    def __init__(self, ins, out_shapes, aliases, sems, start, finish, relays=()):
        self.ins, self.out_shapes, self.aliases, self.sems = list(ins), list(out_shapes), dict(aliases), list(sems)
        self.start, self.relays, self.finish = start, list(relays), finish


def _call(body, name, grid, in_specs, out_specs, out_shape, scratch_shapes, args, tasks=()):
    n_in, n_out, n_scr = len(in_specs), len(out_specs), len(scratch_shapes)
    t_in = [len(t.ins) for t in tasks]
    t_out = [len(t.out_shapes) for t in tasks]
    t_sem = [len(t.sems) for t in tasks]
    steps = 1
    for g in grid:
        steps *= g

    def take(refs, pos, counts):
        groups = []
        for c in counts:
            groups.append(refs[pos:pos + c])
            pos += c
        return groups, pos

    def wrapped(*refs):
        (cin,), pos = take(refs, 0, [n_in])
        tin, pos = take(refs, pos, t_in)
        (cout,), pos = take(refs, pos, [n_out])
        tout, pos = take(refs, pos, t_out)
        (cscr,), pos = take(refs, pos, [n_scr])
        tsem, pos = take(refs, pos, t_sem)
        if not grid:
            for t, a, b, c in zip(tasks, tin, tout, tsem):
                t.start(a, b, c)
            if body is not None:
                body(*cin, *cout, *cscr)
            for t, a, b, c in zip(tasks, tin, tout, tsem):
                for relay, _ in t.relays:
                    relay(a, b, c)
            for t, a, b, c in zip(tasks, tin, tout, tsem):
                t.finish(a, b, c)
            return
        step = pl.program_id(0)
        for axis in range(1, len(grid)):
            step = step * grid[axis] + pl.program_id(axis)
        if tasks:
            @pl.when(step == 0)
            def _():
                for t, a, b, c in zip(tasks, tin, tout, tsem):
                    t.start(a, b, c)

        body(*cin, *cout, *cscr)
        for t, a, b, c in zip(tasks, tin, tout, tsem):
            for relay, before in t.relays:
                pl.when(step == max(steps - 1 - before, 0))(functools.partial(relay, a, b, c))

        if tasks:
            @pl.when(step == steps - 1)
            def _():
                for t, a, b, c in zip(tasks, tin, tout, tsem):
                    t.finish(a, b, c)

    aliases, in_pos, out_pos = {}, n_in, n_out
    for t, ni, no in zip(tasks, t_in, t_out):
        aliases.update({in_pos + a: out_pos + b for a, b in t.aliases.items()})
        in_pos, out_pos = in_pos + ni, out_pos + no
    any_spec = pl.BlockSpec(memory_space=pltpu.HBM)
    kwargs = dict(grid=grid, compiler_params=pltpu.CompilerParams(
        dimension_semantics=("arbitrary",) * len(grid), vmem_limit_bytes=VMEM_LIMIT)) if grid else dict(
        compiler_params=pltpu.CompilerParams(vmem_limit_bytes=VMEM_LIMIT))
    out = pl.pallas_call(
        wrapped, name=name,
        in_specs=list(in_specs) + [any_spec] * sum(t_in),
        out_specs=list(out_specs) + [any_spec] * sum(t_out),
        out_shape=list(out_shape) + [s for t in tasks for s in t.out_shapes],
        scratch_shapes=list(scratch_shapes) + [s for t in tasks for s in t.sems],
        input_output_aliases=aliases, **kwargs,
    )(*args, *[pltpu.with_memory_space_constraint(a, pltpu.HBM) for t in tasks for a in t.ins])
    task_outs, pos = take(list(out), n_out, t_out)
    return list(out[:n_out]), task_outs


def _fwd_in_pool(x, g_pre, w_in, pool_w, pool_scale, w_pool_out, ts, tasks=()):
    s = x.shape[0]

    def body(x_ref, g_ref, win_ref, pw_ref, ps_ref, wpo_ref,
             urx_ref, urg_ref, gp_ref, gr_ref, d_ref, yp_ref, h1_ref, halo_scr):
        i = pl.program_id(0)

        @pl.when(i == 0)
        def _():
            halo_scr[...] = jnp.zeros_like(halo_scr)

        h1, _, _ = _rms_fwd(x_ref[...], g_ref[...])
        h1 = h1.astype(BF16)
        h1_ref[...] = h1
        u = jnp.dot(h1, win_ref[0, :, 0:POOL_WIDTH], preferred_element_type=F32)
        for group, out_ref in zip(range(1, len(Z_GROUPS)), (urx_ref, urg_ref, gp_ref, gr_ref)):
            for j, shard_cols, group_cols in _overlaps(group):
                out_ref[:, group_cols] = jnp.dot(h1, win_ref[j, :, shard_cols],
                                                 preferred_element_type=F32).astype(BF16)
        ext = jnp.concatenate([halo_scr[...], u], axis=0)
        halo_scr[...] = u[ts - POOL_HALO:, :]
        t = i * ts + lax.broadcasted_iota(jnp.int32, (ts, 1), 0)
        y4 = []
        for g, w in enumerate(POOL_WINDOWS):
            lanes = slice(g * POOL_GROUP_DIM, (g + 1) * POOL_GROUP_DIM)
            acc = ext[:, lanes]
            sh = 1
            while sh < w:
                acc = acc + pltpu.roll(acc, sh, 0)
                sh *= 2
            inv = 1.0 / jnp.minimum(t + 1, w).astype(F32)
            dg = acc[POOL_HALO:, :] * inv - u[:, lanes]
            d_ref[:, lanes] = dg.astype(BF16)
            y4.append(_dot(dg, pw_ref[g]))
        ypre = jnp.concatenate(y4, axis=1) * ps_ref[...]
        ypre = ypre.astype(BF16)
        for j in range(N_CHIPS):
            yp_ref[:, j * 256:(j + 1) * 256] = jnp.dot(ypre, wpo_ref[j], preferred_element_type=F32)

    return _call(
        body, "fwd_in_pool", (s // ts,),
        [_rows(ts, D_MODEL), _resident((1, D_MODEL)), _resident(w_in.shape), _resident(pool_w.shape),
         _resident((1, POOL_WIDTH)), _resident(w_pool_out.shape)],
        [_rows(ts, D_MODEL)] * 4 + [_rows(ts, POOL_WIDTH), _rows(ts, D_MODEL), _rows(ts, D_MODEL)],
        [_sds((s, D_MODEL), BF16)] * 4 + [_sds((s, POOL_WIDTH), BF16), _sds((s, D_MODEL)), _sds((s, D_MODEL), BF16)],
        [pltpu.VMEM((POOL_HALO, POOL_WIDTH), F32)],
        (x, g_pre, w_in, pool_w, pool_scale, w_pool_out), tasks)


def _fwd_rnn_merge(urx, urg, gp, gr, ypool, x, conv_w, conv_b, wg, bg, lam, w_rg_out, w_o, g_post, ts, tasks=()):
    s = x.shape[0]

    def body(urx_ref, urg_ref, gp_ref, gr_ref, yp_ref, x_ref, cw_ref, cb_ref, wg_ref, bg_ref, lam_ref, wrg_ref, wo_ref,
             gpost_ref, xc_ref, r_ref, ig_ref, h_ref, yr_ref, mo_ref, x1_ref, gl_ref, gg_ref, sp_ref, sr_ref,
             halo_scr, carry_scr):
        i = pl.program_id(0)

        @pl.when(i == 0)
        def _():
            halo_scr[...] = jnp.zeros_like(halo_scr)
            carry_scr[...] = jnp.zeros_like(carry_scr)

        urx_v = urx_ref[...].astype(F32)
        ext = jnp.concatenate([halo_scr[...], urx_v], axis=0)
        halo_scr[...] = urx_v[ts - CONV_HALO:, :]
        cw = cw_ref[...]
        xc = (cb_ref[...] + cw[3:4] * urx_v + cw[2:3] * _shift_down(ext, 1, CONV_HALO)
              + cw[1:2] * _shift_down(ext, 2, CONV_HALO) + cw[0:1] * _shift_down(ext, 3, CONV_HALO))
        xc_ref[...] = xc.astype(BF16)
        xcb = xc.astype(BF16)
        lin = []
        for gate in range(2):
            parts = [jnp.dot(xcb[:, q * GATE_BLOCK:(q + 1) * GATE_BLOCK], wg_ref[gate, q], preferred_element_type=F32)
                     for q in range(GATE_BLOCKS)]
            lin.append(jnp.concatenate(parts, axis=1) + bg_ref[gate:gate + 1, :])
        r = _sigmoid(lin[0])
        ig = _sigmoid(lin[1])
        r_ref[...] = r.astype(BF16)
        ig_ref[...] = ig.astype(BF16)
        first_row = (i * ts + lax.broadcasted_iota(jnp.int32, (ts, 1), 0)) == 0
        _, a, _, mult = _lru_coeffs(r, lam_ref[...], first_row)
        h = _scan_fwd(a, mult * ig * xc, carry_scr[0:1, :])
        carry_scr[0:1, :] = h[ts - 1:ts, :]
        h_ref[...] = h
        urg_v = urg_ref[...].astype(F32)
        gl, gg = _gelu(urg_v)
        gl_ref[...] = gl.astype(BF16)
        gg_ref[...] = gg.astype(BF16)
        yr = _dot(h * gl, wrg_ref[...])
        yr_ref[...] = yr.astype(BF16)
        sp = _sigmoid(gp_ref[...].astype(F32))
        sr = _sigmoid(gr_ref[...].astype(F32))
        sp_ref[...] = sp.astype(BF16)
        sr_ref[...] = sr.astype(BF16)
        merged = sp * yp_ref[...] + sr * yr
        mo = _dot(merged, wo_ref[...])
        mo_ref[...] = mo
        y, _, _ = _rms_fwd(mo, gpost_ref[...])
        x1_ref[...] = x_ref[...] + y

    row = _rows(ts, D_MODEL)
    return _call(
        body, "fwd_rnn_merge", (s // ts,),
        [row] * 6 + [_resident(conv_w.shape), _resident((1, D_MODEL)), _resident(wg.shape), _resident(bg.shape),
                     _resident((1, D_MODEL)), _resident(w_rg_out.shape), _resident(w_o.shape), _resident((1, D_MODEL))],
        [row] * 11, [_sds((s, D_MODEL), dt) for dt in (BF16, BF16, BF16, F32, BF16, F32, F32, BF16, BF16, BF16, BF16)],
        [pltpu.VMEM((CONV_HALO, D_MODEL), F32), pltpu.VMEM((8, D_MODEL), F32)],
        (urx, urg, gp, gr, ypool, x, conv_w, conv_b, wg, bg, lam, w_rg_out, w_o, g_post), tasks)


def _fwd_ffn(x1, g_pre, w_up, fcw, fcb, w_down, g_post, ts, tasks=()):
    s = x1.shape[0]

    def body(x1_ref, g_ref, wup_ref, fcw_ref, fcb_ref, wd_ref, gpost_ref,
             up_ref, gl_ref, gg_ref, h2_ref, dn_ref, x2_ref, up_scr, halo_scr):
        i = pl.program_id(0)

        @pl.when(i == 0)
        def _():
            halo_scr[...] = jnp.zeros_like(halo_scr)

        x1_v = x1_ref[...]
        h2, _, _ = _rms_fwd(x1_v, g_ref[...])
        h2 = h2.astype(BF16)
        h2_ref[...] = h2
        for j in range(N_CHIPS):
            up_scr[:, j * UP_SHARD:(j + 1) * UP_SHARD] = jnp.dot(h2, wup_ref[j], preferred_element_type=F32)
        up_ref[...] = up_scr[...].astype(BF16)
        ug = up_scr[:, 0:D_FF]
        ext = jnp.concatenate([halo_scr[...], ug], axis=0)
        halo_scr[...] = ug[ts - CONV_HALO:, :]
        w = fcw_ref[...]
        gh = (fcb_ref[...] + w[2:3] * ug + w[1:2] * _shift_down(ext, 1, CONV_HALO)
              + w[0:1] * _shift_down(ext, 2, CONV_HALO))
        gl, gg = _gelu(gh)
        gl_ref[...] = gl.astype(BF16)
        gg_ref[...] = gg.astype(BF16)
        dn = _dot(gl * up_scr[:, D_FF:], wd_ref[...])
        dn_ref[...] = dn
        y, _, _ = _rms_fwd(dn, gpost_ref[...])
        x2_ref[...] = x1_v + y

    row = _rows(ts, D_MODEL)
    return _call(
        body, "fwd_ffn", (s // ts,),
        [row, _resident((1, D_MODEL)), _resident(w_up.shape), _resident(fcw.shape), _resident((1, D_FF)),
         _resident(w_down.shape), _resident((1, D_MODEL))],
        [_rows(ts, 2 * D_FF), _rows(ts, D_FF), _rows(ts, D_FF), row, row, row],
        [_sds((s, 2 * D_FF), BF16), _sds((s, D_FF), BF16), _sds((s, D_FF), BF16), _sds((s, D_MODEL), BF16),
         _sds((s, D_MODEL)), _sds((s, D_MODEL))],
        [pltpu.VMEM((ts, 2 * D_FF), F32), pltpu.VMEM((CONV_HALO, D_FF), F32)],
        (x1, g_pre, w_up, fcw, fcb, w_down, g_post), tasks)


def _ple_loss(x2, p, tgt, g_gate, w_gate, w_proj, g_post, ts):
    s = x2.shape[0]

    def body(x2_ref, p_ref, t_ref, gg_ref, wg_ref, wp_ref, gp_ref, dx2_ref, loss_ref, dwg_ref, dwp_ref, dgg_ref, dgp_ref):
        @pl.when(pl.program_id(0) == 0)
        def _():
            loss_ref[...] = jnp.zeros_like(loss_ref)
            dwg_ref[...] = jnp.zeros_like(dwg_ref)
            dwp_ref[...] = jnp.zeros_like(dwp_ref)
            dgg_ref[...] = jnp.zeros_like(dgg_ref)
            dgp_ref[...] = jnp.zeros_like(dgp_ref)

        x2_v = x2_ref[...]
        n3, xh3, r3 = _rms_fwd(x2_v, gg_ref[...])
        pg = _sigmoid(_dot(n3, wg_ref[...]))
        pb = p_ref[...].astype(BF16)
        q = jnp.concatenate([jnp.dot(pb, wp_ref[j], preferred_element_type=F32) for j in range(N_CHIPS)], axis=1)
        ple, qh, rq = _rms_fwd(q, gp_ref[...])
        e = x2_v + pg * ple - t_ref[...]
        loss_ref[...] += 0.5 * jnp.sum(jnp.mean(e * e, axis=-1, keepdims=True), axis=0, keepdims=True)
        dy = e * (1.0 / D_MODEL)
        dpgl = dy * ple * pg * (1.0 - pg)
        dwg_ref[...] += _dot_tn(n3, dpgl)
        dx3, dgg = _rms_bwd(xh3, r3, gg_ref[...], _dot_nt(dpgl, wg_ref[...]))
        dgg_ref[...] += dgg
        dq, dgp = _rms_bwd(qh, rq, gp_ref[...], dy * pg)
        dgp_ref[...] += dgp
        for j in range(N_CHIPS):
            dwp_ref[j] += _dot_tn(pb, dq[:, j * 256:(j + 1) * 256])
        dx2_ref[...] = dy + dx3

    row = _rows(ts, D_MODEL)
    vec = _acc((1, D_MODEL))
    return pl.pallas_call(
        body, name="ple_loss", grid=(s // ts,),
        in_specs=[row, _rows(ts, PLE_DIM), row, _resident((1, D_MODEL)), _resident(w_gate.shape), _resident(w_proj.shape),
                  _resident((1, D_MODEL))],
        out_specs=[row, _acc((1, 128)), _acc(w_gate.shape), _acc(w_proj.shape), vec, vec],
        out_shape=[_sds((s, D_MODEL)), _sds((1, 128)), _sds(w_gate.shape), _sds(w_proj.shape), _sds((1, D_MODEL)),
                   _sds((1, D_MODEL))],
        compiler_params=_params(),
    )(x2, p, tgt, g_gate, w_gate, w_proj, g_post)


def _bwd_ffn_down(dx2, dn, up, gl, gg, fcw, w_down, g_post, ts):
    s = dx2.shape[0]
    nt = s // ts

    def body(dx2_ref, dn_ref, up_ref, gl_ref, gg_ref, fcw_ref, wd_ref, gpost_ref,
             dup_ref, dwd_ref, dfcw_ref, dfcb_ref, dgp_ref, carry_scr):
        i = pl.program_id(0)

        @pl.when(i == 0)
        def _():
            carry_scr[...] = jnp.zeros_like(carry_scr)
            dwd_ref[...] = jnp.zeros_like(dwd_ref)
            dfcw_ref[...] = jnp.zeros_like(dfcw_ref)
            dfcb_ref[...] = jnp.zeros_like(dfcb_ref)
            dgp_ref[...] = jnp.zeros_like(dgp_ref)

        _, xh, r = _rms_fwd(dn_ref[...], gpost_ref[...])
        ddn, dgp = _rms_bwd(xh, r, gpost_ref[...], dx2_ref[...])
        dgp_ref[...] += dgp
        dhid = _dot_nt(ddn, wd_ref[...])
        ug = up_ref[:, 0:D_FF].astype(F32)
        uv = up_ref[:, D_FF:]
        gl = gl_ref[...]
        w = fcw_ref[...]
        dwd_ref[...] += _dot_tn(gl * uv, ddn)
        dgh = dhid * (uv * gg_ref[...]).astype(F32)
        dup_ref[:, D_FF:] = dhid.astype(BF16) * gl
        extd = jnp.concatenate([dgh, carry_scr[...]], axis=0)
        carry_scr[...] = dgh[0:CONV_HALO, :]
        d1 = _shift_up(extd, 1, ts)
        d2 = _shift_up(extd, 2, ts)
        dup_ref[:, 0:D_FF] = (w[2:3] * dgh + w[1:2] * d1 + w[0:1] * d2).astype(BF16)
        dfcw_ref[2:3, :] += jnp.sum(ug * dgh, axis=0, keepdims=True)
        dfcw_ref[1:2, :] += jnp.sum(ug * d1, axis=0, keepdims=True)
        dfcw_ref[0:1, :] += jnp.sum(ug * d2, axis=0, keepdims=True)
        dfcb_ref[...] += jnp.sum(dgh, axis=0, keepdims=True)

    row = _rows(ts, D_MODEL, nt)
    wide = _rows(ts, D_FF, nt)
    return pl.pallas_call(
        body, name="bwd_ffn_down", grid=(nt,),
        in_specs=[row, row, _rows(ts, 2 * D_FF, nt), wide, wide, _resident(fcw.shape), _resident(w_down.shape),
                  _resident((1, D_MODEL))],
        out_specs=[_rows(ts, 2 * D_FF, nt), _acc(w_down.shape), _acc(fcw.shape), _acc((1, D_FF)), _acc((1, D_MODEL))],
        out_shape=[_sds((s, 2 * D_FF), BF16), _sds(w_down.shape), _sds(fcw.shape), _sds((1, D_FF)), _sds((1, D_MODEL))],
        scratch_shapes=[pltpu.VMEM((CONV_HALO, D_FF), F32)],
        compiler_params=_params(),
    )(dx2, dn, up, gl, gg, fcw, w_down, g_post)


def _bwd_ffn_up(dup, x1, dx2, g_pre, w_up, ts, tasks=()):
    s = x1.shape[0]

    def body(dup_ref, x1_ref, dx2_ref, g_ref, wup_ref, dx1_ref, dg_ref):
        @pl.when(pl.program_id(0) == 0)
        def _():
            dg_ref[...] = jnp.zeros_like(dg_ref)

        _, xh, r = _rms_fwd(x1_ref[...], g_ref[...])
        dh2 = _dot_nt(dup_ref[:, 0:UP_SHARD], wup_ref[0])
        for j in range(1, N_CHIPS):
            dh2 = dh2 + _dot_nt(dup_ref[:, j * UP_SHARD:(j + 1) * UP_SHARD], wup_ref[j])
        dx, dg = _rms_bwd(xh, r, g_ref[...], dh2)
        dg_ref[...] += dg
        dx1_ref[...] = dx2_ref[...] + dx

    row = _rows(ts, D_MODEL)
    return _call(
        body, "bwd_ffn_up", (s // ts,),
        [_rows(ts, 2 * D_FF), row, row, _resident((1, D_MODEL)), _resident(w_up.shape)],
        [row, _acc((1, D_MODEL))], [_sds((s, D_MODEL)), _sds((1, D_MODEL))], [],
        (dup, x1, dx2, g_pre, w_up), tasks)


def _dw_up(h2, dup, ts, tasks=()):
    s = h2.shape[0]
    ts = min(DW_TILES * ts, s)
    nt = s // ts
    half = D_MODEL // 2

    def body(h2_ref, dup_ref, send_ref, acc_ref, theirs_scr, mine_scr, got_scr, sems):
        j, o, i = pl.program_id(0), pl.program_id(1), pl.program_id(2)
        x, y, c, me, _ = _place()
        prod = _dot_tn(h2_ref[...], dup_ref[...])
        for scr, which in ((theirs_scr, 0), (mine_scr, 1)):
            @pl.when((o == which) & (i == 0))
            def _():
                scr[...] = prod

            @pl.when((o == which) & (i > 0))
            def _():
                scr[...] += prod

        slot = j % 2
        push = _remote(theirs_scr, got_scr.at[slot], sems.at[0, slot], sems.at[1, slot], (x, y, 1 - c))
        pl.when((o == 1) & (i == 0))(push.start)

        @pl.when((o == 1) & (i == nt - 1))
        def _():
            push.wait()
            part = (mine_scr[...] + got_scr[slot]).astype(BF16)
            send_ref[0] = part

            @pl.when(me == j)
            def _():
                acc_ref[0] = part

    def h2_cols(j, o, i):
        c = lax.axis_index("c")
        return i, jnp.where(o == 0, 1 - c, c)

    def own_slot(j, o, i):
        return 2 * lax.axis_index("x") + lax.axis_index("y"), 0, 0

    block = (1, half, UP_SHARD)
    return _call(
        body, "dw_up", (N_CHIPS, 2, nt),
        [pl.BlockSpec((ts, half), h2_cols), pl.BlockSpec((ts, UP_SHARD), lambda j, o, i: (i, j))],
        [pl.BlockSpec(block, lambda j, o, i: (j, 0, 0)), pl.BlockSpec(block, own_slot)],
        [_sds((N_CHIPS, half, UP_SHARD), BF16)] * 2,
        [pltpu.VMEM((half, UP_SHARD), F32), pltpu.VMEM((half, UP_SHARD), F32), pltpu.VMEM((2, half, UP_SHARD), F32),
         pltpu.SemaphoreType.DMA((2, 2))],
        (h2, dup), tasks)


def _bwd_merge(dx1, mo, sp, sr, ypool, yrnn, g_post, w_o, ts, tasks=()):
    s = dx1.shape[0]

    def body(dx1_ref, mo_ref, sp_ref, sr_ref, yp_ref, yr_ref, g_ref, wo_ref,
             dgp_ref, dgr_ref, dyp_ref, dyr_ref, dwo_ref, dg_ref):
        @pl.when(pl.program_id(0) == 0)
        def _():
            dwo_ref[...] = jnp.zeros_like(dwo_ref)
            dg_ref[...] = jnp.zeros_like(dg_ref)

        _, xh, r = _rms_fwd(mo_ref[...], g_ref[...])
        dmo, dg = _rms_bwd(xh, r, g_ref[...], dx1_ref[...])
        dg_ref[...] += dg
        dmerged = _dot_nt(dmo, wo_ref[...])
        sp = sp_ref[...].astype(F32)
        sr = sr_ref[...].astype(F32)
        yp = yp_ref[...]
        yr = yr_ref[...].astype(F32)
        dwo_ref[...] += _dot_tn(sp * yp + sr * yr, dmo)
        dgp_ref[...] = (dmerged * yp * sp * (1.0 - sp)).astype(BF16)
        dgr_ref[...] = (dmerged * yr * sr * (1.0 - sr)).astype(BF16)
        dyp_ref[...] = (dmerged * sp).astype(BF16)
        dyr_ref[...] = (dmerged * sr).astype(BF16)

    row = _rows(ts, D_MODEL)
    return _call(
        body, "bwd_merge", (s // ts,),
        [row] * 6 + [_resident((1, D_MODEL)), _resident(w_o.shape)],
        [row] * 4 + [_acc(w_o.shape), _acc((1, D_MODEL))],
        [_sds((s, D_MODEL), BF16)] * 4 + [_sds(w_o.shape), _sds((1, D_MODEL))], [],
        (dx1, mo, sp, sr, ypool, yrnn, g_post, w_o), tasks)


def _bwd_rnn(dyr, urx, gl, gg, xc, r, ig, h, conv_w, wg, lam, w_rg_out, ts, tasks=()):
    s = urx.shape[0]
    nt = s // ts
    halo_blocks = ts // CONV_HALO

    def body(dyr_ref, urx_ref, gl_ref, gg_ref, xc_ref, r_ref, ig_ref, h_ref, hh_ref, cw_ref, wg_ref, lam_ref, wrg_ref,
             durx_ref, durg_ref, dwrg_ref, dwg_ref, dcw_ref, dcb_ref, dbg_ref, dlam_ref, mu_scr, carry_scr):
        i = pl.program_id(0)
        k = nt - 1 - i

        @pl.when(i == 0)
        def _():
            mu_scr[...] = jnp.zeros_like(mu_scr)
            carry_scr[...] = jnp.zeros_like(carry_scr)
            dwrg_ref[...] = jnp.zeros_like(dwrg_ref)
            dwg_ref[...] = jnp.zeros_like(dwg_ref)
            dcw_ref[...] = jnp.zeros_like(dcw_ref)
            dcb_ref[...] = jnp.zeros_like(dcb_ref)
            dbg_ref[...] = jnp.zeros_like(dbg_ref)
            dlam_ref[...] = jnp.zeros_like(dlam_ref)

        row = lax.broadcasted_iota(jnp.int32, (ts, 1), 0)
        first_row = (k * ts + row) == 0
        h = h_ref[...]
        dyr_v = dyr_ref[...]
        dhr = _dot_nt(dyr_v, wrg_ref[...])
        gl = gl_ref[...].astype(F32)
        dwrg_ref[...] += _dot_tn(h * gl, dyr_v)
        durg_ref[...] = (dhr * h * gg_ref[...].astype(F32)).astype(BF16)
        r_v = r_ref[...].astype(F32)
        ig_v = ig_ref[...].astype(F32)
        xc_v = xc_ref[...].astype(F32)
        lam_v = lam_ref[...]
        nla, a, m2, mult = _lru_coeffs(r_v, lam_v, first_row)
        b = jnp.where(row == ts - 1, 1.0, pltpu.roll(a, ts - 1, 0))
        lt = _scan_bwd(b, dhr * gl, mu_scr[0:1, :])
        mu_scr[0:1, :] = a[0:1, :] * lt[0:1, :]
        h_before = jnp.where(k > 0, hh_ref[CONV_HALO - 1:CONV_HALO, :], 0.0)
        hprev = jnp.where(row == 0, h_before, pltpu.roll(h, 1, 0))
        gated = lt * ig_v
        dmult = gated * xc_v
        da = lt * hprev - jnp.where(first_row, 0.0, dmult * a * lax.rsqrt(m2))
        dla = da * a
        dlam_ref[...] += jnp.sum(dla * r_v, axis=0, keepdims=True)
        dlr = (dla * nla) * (r_v - 1.0)
        dli = (dmult * mult) * (1.0 - ig_v)
        dbg_ref[0:1, :] += jnp.sum(dlr, axis=0, keepdims=True)
        dbg_ref[1:2, :] += jnp.sum(dli, axis=0, keepdims=True)
        xcb = xc_v.astype(BF16)
        parts = []
        for q in range(GATE_BLOCKS):
            blk = slice(q * GATE_BLOCK, (q + 1) * GATE_BLOCK)
            dlr_q = dlr[:, blk].astype(BF16)
            dli_q = dli[:, blk].astype(BF16)
            parts.append(_dot_nt(dlr_q, wg_ref[0, q]) + _dot_nt(dli_q, wg_ref[1, q]))
            dwg_ref[0, q] += _dot_tn(xcb[:, blk], dlr_q)
            dwg_ref[1, q] += _dot_tn(xcb[:, blk], dli_q)
        dxc = gated * mult + jnp.concatenate(parts, axis=1)
        extd = jnp.concatenate([dxc, carry_scr[...]], axis=0)
        carry_scr[...] = dxc[0:CONV_HALO, :]
        cw = cw_ref[...]
        urx_v = urx_ref[...].astype(F32)
        durx = cw[3:4] * dxc
        dcw_ref[3:4, :] += jnp.sum(urx_v * dxc, axis=0, keepdims=True)
        for j in (1, 2, 3):
            dj = _shift_up(extd, j, ts)
            durx = durx + cw[3 - j:4 - j] * dj
            dcw_ref[3 - j:4 - j, :] += jnp.sum(urx_v * dj, axis=0, keepdims=True)
        durx_ref[...] = durx.astype(BF16)
        dcb_ref[...] += jnp.sum(dxc, axis=0, keepdims=True)

        @pl.when(i == nt - 1)
        def _():
            dlam_ref[...] = dlam_ref[...] * (LRU_C * jax.nn.sigmoid(-lam_v))

    row_spec = _rows(ts, D_MODEL, nt)
    halo_spec = pl.BlockSpec((CONV_HALO, D_MODEL), lambda i: (jnp.maximum((nt - 1 - i) * halo_blocks - 1, 0), 0))
    vec = _acc((1, D_MODEL))
    return _call(
        body, "bwd_rnn", (nt,),
        [row_spec] * 8 + [halo_spec, _resident(conv_w.shape), _resident(wg.shape), _resident((1, D_MODEL)),
                          _resident(w_rg_out.shape)],
        [row_spec, row_spec, _acc(w_rg_out.shape), _acc(wg.shape), _acc(conv_w.shape), vec, _acc((2, D_MODEL)), vec],
        [_sds((s, D_MODEL), BF16), _sds((s, D_MODEL), BF16), _sds(w_rg_out.shape), _sds(wg.shape), _sds(conv_w.shape),
         _sds((1, D_MODEL)), _sds((2, D_MODEL)), _sds((1, D_MODEL))],
        [pltpu.VMEM((8, D_MODEL), F32), pltpu.VMEM((CONV_HALO, D_MODEL), F32)],
        (dyr, urx, gl, gg, xc, r, ig, h, h, conv_w, wg, lam, w_rg_out), tasks)


def _bwd_pool(dyp, d, pool_w, pool_scale, w_pool_out, ts, tasks=()):
    s = d.shape[0]
    nt = s // ts

    def body(dyp_ref, d_ref, pw_ref, ps_ref, wpo_ref, dzp_ref, dwpo_ref, dpw_ref, dps_ref, carry_scr):
        i = pl.program_id(0)
        k = nt - 1 - i

        @pl.when(i == 0)
        def _():
            carry_scr[...] = jnp.zeros_like(carry_scr)
            dwpo_ref[...] = jnp.zeros_like(dwpo_ref)
            dpw_ref[...] = jnp.zeros_like(dpw_ref)
            dps_ref[...] = jnp.zeros_like(dps_ref)

        dyp_v = dyp_ref[...]
        d_v = d_ref[...]
        ps = ps_ref[...]
        dypre = _dot_nt(dyp_v[:, 0:256], wpo_ref[0])
        for j in range(1, N_CHIPS):
            dypre = dypre + _dot_nt(dyp_v[:, j * 256:(j + 1) * 256], wpo_ref[j])
        y4 = jnp.concatenate([_dot(d_v[:, g * 128:(g + 1) * 128], pw_ref[g]) for g in range(POOL_GROUPS)], axis=1)
        ypre = (y4 * ps).astype(BF16)
        for j in range(N_CHIPS):
            dwpo_ref[j] += _dot_tn(ypre, dyp_v[:, j * 256:(j + 1) * 256])
        dps_ref[...] += jnp.sum(dypre * y4, axis=0, keepdims=True)
        dy4 = dypre * ps
        t = k * ts + lax.broadcasted_iota(jnp.int32, (ts, 1), 0)
        for g, w in enumerate(POOL_WINDOWS):
            lanes = slice(g * POOL_GROUP_DIM, (g + 1) * POOL_GROUP_DIM)
            dd = _dot_nt(dy4[:, lanes], pw_ref[g])
            dpw_ref[g] += _dot_tn(d_v[:, lanes], dy4[:, lanes])
            e = dd * (1.0 / jnp.minimum(t + 1, w).astype(F32))
            acc = jnp.concatenate([e, carry_scr[:, lanes]], axis=0)
            carry_scr[:, lanes] = e[0:POOL_HALO, :]
            n = ts + POOL_HALO
            sh = 1
            while sh < w:
                acc = acc + pltpu.roll(acc, n - sh, 0)
                sh *= 2
            dzp_ref[:, lanes] = (acc[:ts, :] - dd).astype(BF16)

    return _call(
        body, "bwd_pool", (nt,),
        [_rows(ts, D_MODEL, nt), _rows(ts, POOL_WIDTH, nt), _resident(pool_w.shape), _resident((1, POOL_WIDTH)),
         _resident(w_pool_out.shape)],
        [_rows(ts, POOL_WIDTH, nt), _acc(w_pool_out.shape), _acc(pool_w.shape), _acc((1, POOL_WIDTH))],
        [_sds((s, POOL_WIDTH), BF16), _sds(w_pool_out.shape), _sds(pool_w.shape), _sds((1, POOL_WIDTH))],
        [pltpu.VMEM((POOL_HALO, POOL_WIDTH), F32)],
        (dyp, d, pool_w, pool_scale, w_pool_out), tasks)


def _assemble_dz(dz_scr, dzp_ref, durx_ref, durg_ref, dgp_ref, dgr_ref):
    dz_scr[:, 0:512] = dzp_ref[...]
    dz_scr[:, 512:1536] = durx_ref[...]
    dz_scr[:, 1536:2560] = durg_ref[...]
    dz_scr[:, 2560:3584] = dgp_ref[...]
    dz_scr[:, 3584:4608] = dgr_ref[...]


def _dw_in(h1, dzp, durx, durg, dgp, dgr, ts, tasks=()):
    s = h1.shape[0]
    ts = min(2 * ts, s)
    nt = s // ts
    half = D_MODEL // 2

    def body(h1_ref, dzp_ref, durx_ref, durg_ref, dgp_ref, dgr_ref, send_ref, acc_ref, theirs_ref, mine_ref, got_ref, sems):
        o, i = pl.program_id(0), pl.program_id(1)

        @pl.when((o == 0) & (i == 0))
        def _():
            theirs_ref[...] = jnp.zeros_like(theirs_ref)
            mine_ref[...] = jnp.zeros_like(mine_ref)

        groups = (dzp_ref, durx_ref, durg_ref, dgp_ref, dgr_ref)
        for out_ref, which in ((theirs_ref, 0), (mine_ref, 1)):
            @pl.when(o == which)
            def _():
                for group, dz_ref in enumerate(groups):
                    for j, shard_cols, group_cols in _overlaps(group):
                        out_ref[j, :, shard_cols] += _dot_tn(h1_ref[...], dz_ref[:, group_cols])

        x, y, c, me, _ = _place()
        send = _remote(theirs_ref, got_ref, sems.at[0], sems.at[1], (x, y, 1 - c))
        pl.when((o == 1) & (i == 0))(send.start)

        @pl.when((o == 1) & (i == nt - 1))
        def _():
            send.wait()
            for k in range(N_CHIPS):
                part = (mine_ref[k] + got_ref[k]).astype(BF16)
                send_ref[k] = part

                @pl.when(me == k)
                def _():
                    acc_ref[k] = part

    def h1_cols(o, i):
        c = lax.axis_index("c")
        return i, jnp.where(o == 0, 1 - c, c)

    rows = lambda width: pl.BlockSpec((ts, width), lambda o, i: (i, 0))
    shape = (N_CHIPS, half, IN_SHARD)
    whole = pl.BlockSpec(shape, lambda o, i: (0, 0, 0))
    return _call(
        body, "dw_in", (2, nt), [pl.BlockSpec((ts, half), h1_cols), rows(POOL_WIDTH)] + [rows(D_MODEL)] * 4,
        [whole, whole], [_sds(shape, BF16)] * 2,
        [pltpu.VMEM(shape, F32)] * 3 + [pltpu.SemaphoreType.DMA((2,))], (h1, dzp, durx, durg, dgp, dgr), tasks)


def _bwd_in(dzp, durx, durg, dgp, dgr, x, dx1, g_pre, w_in, ts, tasks=()):
    s = x.shape[0]

    def body(dzp_ref, durx_ref, durg_ref, dgp_ref, dgr_ref, x_ref, dx1_ref, g_ref, win_ref, gx_ref, dg_ref, dz_scr):
        @pl.when(pl.program_id(0) == 0)
        def _():
            dg_ref[...] = jnp.zeros_like(dg_ref)

        _assemble_dz(dz_scr, dzp_ref, durx_ref, durg_ref, dgp_ref, dgr_ref)
        _, xh, r = _rms_fwd(x_ref[...], g_ref[...])
        dh1 = _dot_nt(dz_scr[:, 0:IN_SHARD], win_ref[0])
        for j in range(1, N_CHIPS):
            dh1 = dh1 + _dot_nt(dz_scr[:, j * IN_SHARD:(j + 1) * IN_SHARD], win_ref[j])
        dx, dg = _rms_bwd(xh, r, g_ref[...], dh1)
        dg_ref[...] += dg
        gx_ref[...] = dx1_ref[...] + dx

    row = _rows(ts, D_MODEL)
    return _call(
        body, "bwd_in", (s // ts,),
        [_rows(ts, POOL_WIDTH)] + [row] * 6 + [_resident((1, D_MODEL)), _resident(w_in.shape)],
        [row, _acc((1, D_MODEL))], [_sds((s, D_MODEL)), _sds((1, D_MODEL))],
        [pltpu.VMEM((ts, IN_TOTAL), BF16)], (dzp, durx, durg, dgp, dgr, x, dx1, g_pre, w_in), tasks)


def _place():
    x, y, c = lax.axis_index("x"), lax.axis_index("y"), lax.axis_index("c")
    others = [(1 - x, y), (x, 1 - y), (1 - x, 1 - y)]
    return x, y, c, 2 * x + y, others


def _remote(src, dst, send_sem, recv_sem, to):
    return pltpu.make_async_remote_copy(src_ref=src, dst_ref=dst, send_sem=send_sem, recv_sem=recv_sem,
                                        device_id=to, device_id_type=MESH)


def _own_slots(ws, dtypes, name, tasks=()):
    n = len(ws)
    hbm = pl.BlockSpec(memory_space=pltpu.HBM)

    def body(*refs):
        srcs, outs, f32_bufs, cast_bufs, sems = refs[:n], refs[n:2 * n], refs[2 * n:3 * n], refs[3 * n:4 * n], refs[4 * n]
        me = _place()[3]
        loads = [pltpu.make_async_copy(srcs[k], f32_bufs[k], sems.at[k, 0]) for k in range(n)]
        stores = [pltpu.make_async_copy(cast_bufs[k], outs[k].at[me], sems.at[k, 1]) for k in range(n)]
        for cp in loads:
            cp.start()
        for k in range(n):
            loads[k].wait()
            cast_bufs[k][...] = f32_bufs[k][...].astype(dtypes[k])
            stores[k].start()
        for cp in stores:
            cp.wait()

    return _call(
        body, name, (), [hbm] * n, [hbm] * n, [_sds((N_CHIPS,) + w.shape, dt) for w, dt in zip(ws, dtypes)],
        [pltpu.VMEM(w.shape, F32) for w in ws] + [pltpu.VMEM(w.shape, dt) for w, dt in zip(ws, dtypes)]
        + [pltpu.SemaphoreType.DMA((n, 2))],
        [pltpu.with_memory_space_constraint(w, pltpu.HBM) for w in ws], tasks)


def _run(tasks, name):
    if isinstance(tasks, _Task):
        return _call(None, name, (), [], [], [], [], (), (tasks,))[1][0]
    return _call(None, name, (), [], [], [], [], (), tuple(tasks))[1]


def _gather_task(bufs, relay_steps=(0, 0)):
    n = len(bufs)
    NBR_X, NBR_Y, QUARTER_VIA_Y, QUARTER_VIA_X, SIB_X, SIB_Y, SIB_DIAG = range(7)

    def parts(out):
        x, y, c, me, _ = _place()
        ah = out.shape[1] // 2
        q = ah // 2 if (ah // 2) % 16 == 0 else ah
        return c * ah, ah, q

    def copy(out, w, k, chip, row0, rows, to, sems):
        slot = out.at[chip, pl.ds(row0, rows)]
        return _remote(slot, slot, sems[0].at[w, k], sems[1].at[w, k], to)

    def plan(out, w, sems):
        x, y, c, me, _ = _place()
        row0, ah, q = parts(out)
        xn, yn, dg = 2 * (1 - x) + y, 2 * x + (1 - y), 2 * (1 - x) + (1 - y)
        to_x, to_y, sib = (1 - x, y, c), (x, 1 - y, c), (x, y, 1 - c)
        other = (1 - c) * ah
        cp = functools.partial(copy, out, w, sems=sems)
        sends = {NBR_X: cp(NBR_X, me, row0, ah, to_x), NBR_Y: cp(NBR_Y, me, row0, ah, to_y),
                 QUARTER_VIA_Y: cp(QUARTER_VIA_Y, xn, row0, q, to_y), SIB_X: cp(SIB_X, xn, row0, ah, sib),
                 SIB_Y: cp(SIB_Y, yn, row0, ah, sib), SIB_DIAG: cp(SIB_DIAG, dg, row0, ah, sib)}
        lands = {NBR_X: cp(NBR_X, xn, row0, ah, to_x), NBR_Y: cp(NBR_Y, yn, row0, ah, to_y),
                 QUARTER_VIA_Y: cp(QUARTER_VIA_Y, dg, row0, q, to_y), SIB_X: cp(SIB_X, xn, other, ah, sib),
                 SIB_Y: cp(SIB_Y, yn, other, ah, sib), SIB_DIAG: cp(SIB_DIAG, dg, other, ah, sib)}
        if q < ah:
            sends[QUARTER_VIA_X] = cp(QUARTER_VIA_X, yn, row0 + q, ah - q, to_x)
            lands[QUARTER_VIA_X] = cp(QUARTER_VIA_X, dg, row0 + q, ah - q, to_x)
        return sends, lands

    def start(ins, outs, sems):
        for w, out in enumerate(outs):
            sends, _ = plan(out, w, sems)
            sends[NBR_X].start()
            sends[NBR_Y].start()

    def pass_neighbours(ins, outs, sems):
        for w, out in enumerate(outs):
            sends, lands = plan(out, w, sems)
            lands[NBR_X].wait_recv()
            sends[QUARTER_VIA_Y].start()
            sends[SIB_X].start()
            lands[NBR_Y].wait_recv()
            if QUARTER_VIA_X in sends:
                sends[QUARTER_VIA_X].start()
            sends[SIB_Y].start()

    def pass_diagonal(ins, outs, sems):
        for w, out in enumerate(outs):
            sends, lands = plan(out, w, sems)
            lands[QUARTER_VIA_Y].wait_recv()
            if QUARTER_VIA_X in lands:
                lands[QUARTER_VIA_X].wait_recv()
            sends[SIB_DIAG].start()

    def finish(ins, outs, sems):
        for w, out in enumerate(outs):
            sends, lands = plan(out, w, sems)
            for k in (SIB_X, SIB_Y, SIB_DIAG):
                lands[k].wait_recv()
        for w, out in enumerate(outs):
            sends, _ = plan(out, w, sems)
            for cp in sends.values():
                cp.wait_send()

    return _Task(bufs, [_sds(b.shape, b.dtype) for b in bufs], {i: i for i in range(n)},
                 [pltpu.SemaphoreType.DMA((n, 7)), pltpu.SemaphoreType.DMA((n, 7))], start, finish,
                 [(pass_neighbours, relay_steps[0]), (pass_diagonal, relay_steps[1])])


def _halves_task(grads):
    n = len(grads)

    def copy(src, out, w, sems):
        x, y, c, _, _ = _place()
        ah = out.shape[1]
        return _remote(src.at[:, pl.ds((1 - c) * ah, ah)], out, sems[0].at[w], sems[1].at[w], (x, y, 1 - c))

    def start(ins, outs, sems):
        for w, (src, out) in enumerate(zip(ins, outs)):
            copy(src, out, w, sems).start()

    def finish(ins, outs, sems):
        for w, (src, out) in enumerate(zip(ins, outs)):
            copy(src, out, w, sems).wait()

    return _Task(grads, [_sds((g.shape[0], g.shape[1] // 2, g.shape[2]), g.dtype) for g in grads], {},
                 [pltpu.SemaphoreType.DMA((n,)), pltpu.SemaphoreType.DMA((n,))], start, finish)


def _exchange_task(sends, accs):
    n = len(accs)
    given = [s for s in sends if s is not None]

    def copies(ins, outs, sems):
        send_refs = iter(ins[:len(given)])
        srcs = [next(send_refs) if s is not None else None for s in sends]
        x, y, c, me, others = _place()
        for w, out in enumerate(outs):
            for j, (ox, oy) in enumerate(others):
                src = out.at[me] if srcs[w] is None else srcs[w].at[2 * ox + oy]
                yield _remote(src, out.at[me], sems[0].at[w, j], sems[1].at[w, j], (ox, oy, c))

    def start(ins, outs, sems):
        for cp in copies(ins, outs, sems):
            cp.start()

    def finish(ins, outs, sems):
        x, y, c, _, others = _place()
        for w, out in enumerate(outs):
            for j, (ox, oy) in enumerate(others):
                slot = out.at[2 * ox + oy]
                _remote(slot, slot, sems[0].at[w, j], sems[1].at[w, j], (ox, oy, c)).wait_recv()
        for cp in copies(ins, outs, sems):
            cp.wait_send()

    return _Task(given + list(accs), [_sds(a.shape, a.dtype) for a in accs], {len(given) + i: i for i in range(n)},
                 [pltpu.SemaphoreType.DMA((n, 3)), pltpu.SemaphoreType.DMA((n, 3))], start, finish)


def _swap_task(arrays):
    n = len(arrays)

    def copy(src, out, w, sems):
        x, y, c, _, _ = _place()
        return _remote(src, out, sems[0].at[w], sems[1].at[w], (x, y, 1 - c))

    def start(ins, outs, sems):
        for w, (src, out) in enumerate(zip(ins, outs)):
            copy(src, out, w, sems).start()

    def finish(ins, outs, sems):
        for w, (src, out) in enumerate(zip(ins, outs)):
            copy(src, out, w, sems).wait()

    return _Task(arrays, [_sds(a.shape, a.dtype) for a in arrays], {},
                 [pltpu.SemaphoreType.DMA((n,)), pltpu.SemaphoreType.DMA((n,))], start, finish)


def _all_devices_task(arrays):
    n = len(arrays)
    flips = [(dx, dy, dc) for dx in (0, 1) for dy in (0, 1) for dc in (0, 1)][1:]

    def peers():
        x, y, c, _, _ = _place()
        flip = lambda v, d: 1 - v if d else v
        return 4 * x + 2 * y + c, [(flip(x, dx), flip(y, dy), flip(c, dc)) for dx, dy, dc in flips]

    def start(ins, outs, sems):
        me, others = peers()
        for w, (src, out) in enumerate(zip(ins, outs)):
            pltpu.make_async_copy(src, out.at[me], sems[2].at[w]).start()
            for k, peer in enumerate(others):
                _remote(src, out.at[me], sems[0].at[w, k], sems[1].at[w, k], peer).start()

    def finish(ins, outs, sems):
        me, others = peers()
        for w, (src, out) in enumerate(zip(ins, outs)):
            for k, (px, py, pc) in enumerate(others):
                slot = out.at[4 * px + 2 * py + pc]
                _remote(slot, slot, sems[0].at[w, k], sems[1].at[w, k], (px, py, pc)).wait_recv()
            for k, peer in enumerate(others):
                _remote(src, out.at[me], sems[0].at[w, k], sems[1].at[w, k], peer).wait_send()
            pltpu.make_async_copy(src, out.at[me], sems[2].at[w]).wait()

    return _Task(arrays, [_sds((8,) + a.shape, a.dtype) for a in arrays], {},
                 [pltpu.SemaphoreType.DMA((n, 7)), pltpu.SemaphoreType.DMA((n, 7)), pltpu.SemaphoreType.DMA((n,))],
                 start, finish)


def _share_task(shares):
    n = len(shares)

    def copy(out, w, sems, slot):
        x, y, c, _, _ = _place()
        return _remote(out.at[slot], out.at[slot], sems[0].at[w], sems[1].at[w], (x, y, 1 - c))

    def start(ins, outs, sems):
        c = _place()[2]
        for w, out in enumerate(outs):
            copy(out, w, sems, c).start()

    def finish(ins, outs, sems):
        c = _place()[2]
        for w, out in enumerate(outs):
            copy(out, w, sems, 1 - c).wait_recv()
        for w, out in enumerate(outs):
            copy(out, w, sems, c).wait_send()

    return _Task(shares, [_sds(s.shape, s.dtype) for s in shares], {i: i for i in range(n)},
                 [pltpu.SemaphoreType.DMA((n,)), pltpu.SemaphoreType.DMA((n,))], start, finish)


TILE_BYTES = 2 * 1024 * 1024
PARTIAL_TILE_BYTES = 1024 * 1024


def _in_hbm(t):
    return pltpu.with_memory_space_constraint(t, pltpu.HBM)


def _row_tile(rows, cols, limit=TILE_BYTES):
    best = 8
    for tr in range(8, rows + 1, 8):
        if rows % tr == 0 and tr * cols * 4 <= limit:
            best = tr
    assert rows % best == 0, (rows, cols)
    return best


def _chip_partial(g, got, place, wire_dtype):
    ns, ah, b = got.shape
    sharded = ns == N_CHIPS
    tr = _row_tile(ah, b, PARTIAL_TILE_BYTES)
    nb = ah // tr
    first = 0 if g.shape[1] == ah else nb

    def body(place_ref, *refs):
        g_refs, got_refs, outs = refs[:ns], refs[ns:2 * ns], refs[2 * ns:]
        parts = [g_refs[k][0] + got_refs[k][0] for k in range(ns)]
        own = parts[0]
        if sharded:
            for k in range(ns):
                outs[0][k] = parts[k].astype(wire_dtype)
                if k:
                    own = jnp.where(place_ref[0] == k, parts[k], own)
        outs[-1][0] = own.astype(wire_dtype)

    blk = (1, tr, b)
    in_specs = ([pl.BlockSpec(blk, lambda i, s, k=k: (k, s[1] * first + i, 0)) for k in range(ns)]
                + [pl.BlockSpec(blk, lambda i, s, k=k: (k, i, 0)) for k in range(ns)])
    acc_spec = pl.BlockSpec(blk, lambda i, s: (s[0], i, 0))
    acc_shape = _sds((N_CHIPS, ah, b), wire_dtype)
    out = pl.pallas_call(
        body, name="grad_chip_partial",
        grid_spec=pltpu.PrefetchScalarGridSpec(
            num_scalar_prefetch=1, grid=(nb,), in_specs=in_specs,
            out_specs=[pl.BlockSpec((ns, tr, b), lambda i, s: (0, i, 0)), acc_spec] if sharded else [acc_spec]),
        out_shape=[acc_shape, acc_shape] if sharded else [acc_shape],
        compiler_params=pltpu.CompilerParams(dimension_semantics=("arbitrary",), vmem_limit_bytes=VMEM_LIMIT),
    )(place, *([g] * ns), *([got] * ns))
    return (out[0], out[1]) if sharded else (None, out[0])


def _chip_sum(acc, place):
    _, ah, b = acc.shape
    tr = _row_tile(ah, b)

    def body(place_ref, p_ref, out_ref):
        total = p_ref[0].astype(F32) + p_ref[1].astype(F32)
        total = total + p_ref[2].astype(F32)
        out_ref[0] = total + p_ref[3].astype(F32)

    return pl.pallas_call(
        body, name="grad_chip_sum",
        grid_spec=pltpu.PrefetchScalarGridSpec(
            num_scalar_prefetch=1, grid=(ah // tr,),
            in_specs=[pl.BlockSpec((N_CHIPS, tr, b), lambda i, s: (0, i, 0))],
            out_specs=pl.BlockSpec((1, tr, b), lambda i, s: (s[1], i, 0))),
        out_shape=_sds((2, ah, b)),
        compiler_params=pltpu.CompilerParams(dimension_semantics=("arbitrary",)),
    )(place, _in_hbm(acc))


def _adam_math(w, g, m, v):
    nm = ADAM_B1 * m + (1.0 - ADAM_B1) * g
    nv = ADAM_B2 * v + (1.0 - ADAM_B2) * (g * g)
    m_hat = nm / (1.0 - ADAM_B1 ** ADAM_STEP)
    v_hat = nv / (1.0 - ADAM_B2 ** ADAM_STEP)
    return -ADAM_LR * (m_hat / (jnp.sqrt(v_hat) + ADAM_EPS) + ADAM_WD * w), nm, nv


def _adamw(w, g, m, v):
    a, b = w.shape
    tr = _row_tile(a, b)

    def body(w_ref, g_ref, m_ref, v_ref, g_out, d_ref, nm_ref, nv_ref):
        g_out[...] = g_ref[...]
        d_ref[...], nm_ref[...], nv_ref[...] = _adam_math(w_ref[...], g_ref[...], m_ref[...], v_ref[...])

    blk = pl.BlockSpec((tr, b), lambda i: (i, 0))
    return pl.pallas_call(
        body, name="adamw", grid=(a // tr,),
        in_specs=[blk] * 4, out_specs=[blk] * 4, out_shape=[_sds((a, b))] * 4,
        compiler_params=pltpu.CompilerParams(dimension_semantics=("arbitrary",)),
    )(w, g, m, v)


def _adamw_sum(w, m, v, acc, got, place):
    a, b = w.shape
    ah = a // 2
    tr = _row_tile(ah, b)
    nb = ah // tr

    def body(place_ref, w_ref, m_ref, v_ref, acc_ref, got_ref, g_out, d_ref, nm_ref, nv_ref):
        mine = (pl.program_id(0) // nb) == place_ref[1]
        part = lambda k: jnp.where(mine, acc_ref[k], got_ref[k]).astype(F32)
        g = part(0) + part(1)
        g = g + part(2)
        g = g + part(3)
        g_out[...] = g
        d_ref[...], nm_ref[...], nv_ref[...] = _adam_math(w_ref[...], g, m_ref[...], v_ref[...])

    blk = pl.BlockSpec((tr, b), lambda i, s: (i, 0))
    mine_spec = pl.BlockSpec((N_CHIPS, tr, b), lambda i, s: (0, jnp.where(i // nb == s[1], i % nb, 0), 0))
    got_spec = pl.BlockSpec((N_CHIPS, tr, b), lambda i, s: (0, jnp.where(i // nb == s[1], 0, i % nb), 0))
    return pl.pallas_call(
        body, name="adamw_sum",
        grid_spec=pltpu.PrefetchScalarGridSpec(
            num_scalar_prefetch=1, grid=(a // tr,), in_specs=[blk] * 3 + [mine_spec, got_spec], out_specs=[blk] * 4),
        out_shape=[_sds((a, b))] * 4,
        compiler_params=pltpu.CompilerParams(dimension_semantics=("arbitrary",), vmem_limit_bytes=VMEM_LIMIT),
    )(place, w, m, v, _in_hbm(acc), _in_hbm(got))


def _adamw_pieces(g, pieces, name):
    n = len(pieces)

    def body(g_ref, *refs):
        def grad(rows, cols):
            if len(g_ref.shape) == 2:
                return g_ref[rows, cols]
            total = g_ref[0, rows, cols]
            for k in range(1, g_ref.shape[0]):
                total = total + g_ref[k, rows, cols]
            return total

        ins, outs = refs[:3 * n], refs[3 * n:]
        for i, piece in enumerate(pieces):
            w_ref, m_ref, v_ref = ins[3 * i:3 * i + 3]
            o_g, o_d, o_m, o_v = outs[4 * i:4 * i + 4]
            if len(piece) == 5:
                g_v = grad(piece[3], piece[4])
                o_g[...] = g_v
                o_d[...], o_m[...], o_v[...] = _adam_math(w_ref[...], g_v, m_ref[...], v_ref[...])
            else:
                for r in range(w_ref.shape[1] // SMALL_COLS):
                    lanes = slice(r * SMALL_COLS, (r + 1) * SMALL_COLS)
                    g_v = grad(slice(piece[3] + r, piece[3] + r + 1), slice(None))
                    o_g[:, lanes] = g_v
                    o_d[:, lanes], o_m[:, lanes], o_v[:, lanes] = _adam_math(w_ref[:, lanes], g_v, m_ref[:, lanes],
                                                                            v_ref[:, lanes])

    operands = [t for piece in pieces for t in piece[:3]]
    out = pl.pallas_call(
        body, name=name,
        out_shape=[_sds(piece[0].shape) for piece in pieces for _ in range(4)],
    )(g, *operands)
    return [tuple(out[4 * i:4 * i + 4]) for i in range(n)]


TINY_ROWS, TINY_COLS = 16, 768
SMALL_COLS = 128
SMALL_ROWS = 624


def _pack_tiny(conv_w, b_gates, fcw):
    ns = conv_w.shape[0]
    pad = lambda t: jnp.pad(t, ((0, 0), (0, 0), (0, TINY_COLS - t.shape[2])))
    z = lambda rows: jnp.zeros((ns, rows, TINY_COLS), F32)
    return jnp.concatenate([pad(conv_w), pad(b_gates), z(2), fcw, z(TINY_ROWS - 11)], axis=1)


def _unpack_tiny(t):
    return t[:, 0:4, 0:256], t[:, 4:6, 0:256], t[:, 8:11, :]


def _cols_to_shards(t, n):
    return t.reshape(t.shape[0], N_CHIPS, n).transpose(1, 0, 2)


def _shards_to_cols(t):
    return t.transpose(1, 0, 2).reshape(t.shape[1], -1)


_VECTORS = ("g_mix_post", "conv_b", "lru_lambda", "g_ffn_pre", "g_ffn_post", "g_ple_gate", "g_ple_post", "pool_scale",
            "ffn_conv_b")
_VECTOR_LEN = {"pool_scale": POOL_WIDTH, "ffn_conv_b": D_FF}
POOL_W_ROWS = POOL_GROUPS * POOL_GROUP_DIM


def _vector_rows():
    rows, row = {}, POOL_W_ROWS
    for k in _VECTORS:
        rows[k] = row
        row += max(8, _VECTOR_LEN.get(k, D_MODEL) // SMALL_COLS)
    return rows, row


def _pack_small(grads, loss):
    tiles = lambda t: jnp.pad(t, ((0, -t.shape[0] % 8), (0, 0)))
    parts = [grads["pool_w"].reshape(POOL_W_ROWS, SMALL_COLS)] + [tiles(grads[k].reshape(-1, SMALL_COLS)) for k in _VECTORS]
    parts.append(tiles(loss))
    used = sum(t.shape[0] for t in parts)
    return jnp.concatenate(parts + [jnp.zeros((SMALL_ROWS - used, SMALL_COLS), F32)], axis=0)


def _gates_block_diag(w):
    w4 = w.reshape(2, GATE_BLOCKS, 4, RNN_HEAD_DIM, RNN_HEAD_DIM)
    eye = jnp.eye(4, dtype=w.dtype)
    return jnp.einsum("gqhij,hk->gqhikj", w4, eye).reshape(2, GATE_BLOCKS, GATE_BLOCK, GATE_BLOCK)


def _gates_from_block_diag(dw):
    d6 = dw.reshape(2, GATE_BLOCKS, 4, RNN_HEAD_DIM, 4, RNN_HEAD_DIM)
    blocks = [d6[:, :, hh, :, hh, :] for hh in range(4)]
    return jnp.stack(blocks, axis=2).reshape(2, RNN_HEADS, RNN_HEAD_DIM, RNN_HEAD_DIM)


ROW_TILE = 256
DW_TILES = 8

_SHARDED = ("w_in", "w_pool_out", "w_rg_out", "w_o", "w_up", "w_down", "w_ple_gate", "w_ple_proj")
_WEIGHTS = ("g_mix_pre", "g_mix_post", "w_in", "pool_w", "pool_scale", "w_pool_out", "conv_w", "conv_b", "w_rg_gates",
            "b_rg_gates", "lru_lambda", "w_rg_out", "w_o", "g_ffn_pre", "g_ffn_post", "w_up", "ffn_conv_w", "ffn_conv_b",
            "w_down", "g_ple_gate", "w_ple_gate", "w_ple_proj", "g_ple_post")


def _wire_dtype(g):
    return BF16 if g.shape[1] >= 64 and g.shape[2] > SMALL_COLS else F32


def _partials(grads, got, place):
    parts = [_chip_partial(g, r, place, _wire_dtype(g)) for g, r in zip(grads, got)]
    return [send for send, _ in parts], [acc for _, acc in parts]


def _whole(both):
    return [b.reshape(2 * b.shape[1], b.shape[2]) for b in both]


def _step(x, p, tgt, rep, place, ts):
    vec = lambda k: rep[k].reshape(1, -1)
    tall = min(2 * ts, x.shape[0])
    pool_w = rep["pool_w"].astype(BF16)
    wg = _gates_block_diag(rep["w_rg_gates"]).astype(BF16)
    sq = lambda t: t.reshape(D_MODEL, D_MODEL)
    by4 = lambda t: t.reshape(N_CHIPS, -1, D_MODEL)

    first, ride1, ride2 = (("w_in", "w_pool_out", "tiny"), ("w_rg_out", "w_o", "w_down"),
                           ("w_up", "w_ple_gate", "w_ple_proj"))
    later = ride1 + ride2
    tiny = _pack_tiny(rep["conv_w"][None], rep["b_rg_gates"][None], rep["ffn_conv_w"][None])[0]
    own_first, _ = _own_slots([rep["w_in"], rep["w_pool_out"], tiny], [BF16, BF16, F32], "own_slots_first")
    own_later, (got,) = _own_slots([rep[k] for k in later], [BF16] * len(later), "own_slots_gather_first",
                                   [_gather_task(own_first)])
    own = dict(zip(later, own_later))
    full = dict(zip(first, got))
    conv_w, b_gates, fcw = [_shards_to_cols(t) for t in _unpack_tiny(full["tiny"])]

    (urx, urg, gp, gr, d, ypool, h1), (got,) = _fwd_in_pool(
        x, vec("g_mix_pre"), full["w_in"], pool_w, vec("pool_scale"), full["w_pool_out"], ts,
        [_gather_task([own[k] for k in ride1], relay_steps=(6, 2))])
    full.update(zip(ride1, got))
    w_rg_out, w_o, w_down = sq(full["w_rg_out"]), sq(full["w_o"]), full["w_down"].reshape(D_FF, D_MODEL)
    (xc, r, ig, h, yrnn, mo, x1, glr, ggr, sp, sr), (got,) = _fwd_rnn_merge(
        urx, urg, gp, gr, ypool, x, conv_w, vec("conv_b"), wg, b_gates, vec("lru_lambda"), w_rg_out, w_o,
        vec("g_mix_post"), ts, [_gather_task([own[k] for k in ride2], relay_steps=(7, 3))])
    full.update(zip(ride2, got))
    (up, gl, gg, h2, dn, x2), _ = _fwd_ffn(x1, vec("g_ffn_pre"), full["w_up"], fcw, vec("ffn_conv_b"), w_down,
                                           vec("g_ffn_post"), ts)
    dx2, loss, d_w_gate, d_w_proj, d_g_ple_gate, d_g_ple_post = _ple_loss(
        x2, p, tgt, vec("g_ple_gate"), sq(full["w_ple_gate"]), full["w_ple_proj"], vec("g_ple_post"), tall)
    dup, d_w_down, d_fcw, d_fcb, d_g_ffn_post = _bwd_ffn_down(dx2, dn, up, gl, gg, fcw, w_down, vec("g_ffn_post"), ts)

    names1, grads1 = ("w_ple_gate", "w_ple_proj", "w_down"), [by4(d_w_gate), d_w_proj, by4(d_w_down)]
    (dx1, d_g_ffn_pre), (got1,) = _bwd_ffn_up(dup, x1, dx2, vec("g_ffn_pre"), full["w_up"], tall, [_halves_task(grads1)])
    (send_w_up, acc_w_up), (accs1,) = _dw_up(h2, dup, ts, [_exchange_task(*_partials(grads1, got1, place))])
    (dgp, dgr, dyp, dyr, d_w_o, d_g_mix_post), (theirs1,) = _bwd_merge(
        dx1, mo, sp, sr, ypool, yrnn, vec("g_mix_post"), w_o, tall, [_swap_task(accs1)])
    (durx, durg, d_w_rg_out, d_wg, d_conv_w, d_conv_b, d_b_gates, d_lam), (accs2,) = _bwd_rnn(
        dyr, urx, glr, ggr, xc, r, ig, h, conv_w, wg, vec("lru_lambda"), w_rg_out, ts,
        [_exchange_task([send_w_up], [acc_w_up])])
    names3 = ("w_o", "w_rg_out", "tiny", "w_rg_gates")
    grads3 = [by4(d_w_o), by4(d_w_rg_out),
              _pack_tiny(_cols_to_shards(d_conv_w, 256), _cols_to_shards(d_b_gates, 256), _cols_to_shards(d_fcw, 768)),
              _gates_from_block_diag(d_wg).reshape(1, 2 * RNN_HEADS * RNN_HEAD_DIM, RNN_HEAD_DIM)]
    (dzp, d_w_pool_out, d_pool_w, d_pool_scale), (got3, theirs2) = _bwd_pool(
        dyp, d, pool_w, vec("pool_scale"), full["w_pool_out"], tall, [_halves_task(grads3), _swap_task(accs2)])
    replicated = {"g_mix_post": d_g_mix_post, "conv_b": d_conv_b, "lru_lambda": d_lam, "g_ffn_pre": d_g_ffn_pre,
                  "g_ffn_post": d_g_ffn_post, "g_ple_gate": d_g_ple_gate, "g_ple_post": d_g_ple_post,
                  "pool_scale": d_pool_scale, "ffn_conv_b": d_fcb, "pool_w": d_pool_w}
    names4 = ("w_in", "w_pool_out", "small")
    small4 = [d_w_pool_out, _pack_small(replicated, loss)[None]]
    (send_w_in, acc_w_in), (accs3, got_small4) = _dw_in(
        h1, dzp, durx, durg, dgp, dgr, ts, [_exchange_task(*_partials(grads3, got3, place)), _halves_task(small4)])
    sends4, accs4 = _partials(small4, got_small4, place)
    (grad_x, d_g_mix_pre), (accs4, theirs3, both3) = _bwd_in(
        dzp, durx, durg, dgp, dgr, x, dx1, vec("g_mix_pre"), full["w_in"], tall,
        [_exchange_task([send_w_in] + sends4, [acc_w_in] + accs4), _swap_task(accs3[:2]),
         _share_task([_chip_sum(acc, place) for acc in accs3[2:]])])
    theirs4, both4, (g_mix_pre_parts,) = _run(
        [_swap_task(accs4[:2]), _share_task([_chip_sum(acc, place) for acc in accs4[2:]]),
         _all_devices_task([d_g_mix_pre.reshape(SUBLANES, SMALL_COLS)])], "grad_sibling_share")
    mine = accs1 + accs2 + accs3[:2] + accs4[:2]
    partials = dict(zip(names1 + ("w_up",) + names3[:2] + names4[:2], zip(mine, theirs1 + theirs2 + theirs3 + theirs4)))
    return grad_x, partials, dict(zip(names3[2:] + names4[2:], _whole(both3) + _whole(both4))), g_mix_pre_parts


def kernel(x, p, g_mix_pre, g_mix_post, w_in, pool_w, pool_scale, w_pool_out, conv_w, conv_b, w_rg_gates, b_rg_gates, lru_lambda, w_rg_out, w_o, g_ffn_pre, g_ffn_post, w_up, ffn_conv_w, ffn_conv_b, w_down, g_ple_gate, w_ple_gate, w_ple_proj, g_ple_post, loss_target, m_g_mix_pre, m_g_mix_post, m_w_in, m_pool_w, m_pool_scale, m_w_pool_out, m_conv_w, m_conv_b, m_w_rg_gates, m_b_rg_gates, m_lru_lambda, m_w_rg_out, m_w_o, m_g_ffn_pre, m_g_ffn_post, m_w_up, m_ffn_conv_w, m_ffn_conv_b, m_w_down, m_g_ple_gate, m_w_ple_gate, m_w_ple_proj, m_g_ple_post, v_g_mix_pre, v_g_mix_post, v_w_in, v_pool_w, v_pool_scale, v_w_pool_out, v_conv_w, v_conv_b, v_w_rg_gates, v_b_rg_gates, v_lru_lambda, v_w_rg_out, v_w_o, v_g_ffn_pre, v_g_ffn_post, v_w_up, v_ffn_conv_w, v_ffn_conv_b, v_w_down, v_g_ple_gate, v_w_ple_gate, v_w_ple_proj, v_g_ple_post):
    args = dict(locals())
    w = {k: args[k][0] for k in _WEIGHTS}
    m = {k: args["m_" + k][0] for k in _WEIGHTS}
    v = {k: args["v_" + k][0] for k in _WEIGHTS}
    place = jnp.stack([2 * lax.axis_index("x") + lax.axis_index("y"), lax.axis_index("c")]).astype(jnp.int32)
    grad_x, partials, reduced, g_mix_pre_parts = _step(x[0], p[0, 0], loss_target[0], w, place, ROW_TILE)

    gates_2d = (2 * RNN_HEADS * RNN_HEAD_DIM, RNN_HEAD_DIM)
    as2d = lambda k, shape: tuple(t[k].reshape(shape) for t in (w, m, v))
    done = {k: tuple(_adamw_sum(w[k], m[k], v[k], *partials[k], place)) for k in _SHARDED}
    gates_w, gates_m, gates_v = as2d("w_rg_gates", gates_2d)
    done["w_rg_gates"] = tuple(_adamw(gates_w, reduced["w_rg_gates"], gates_m, gates_v))
    tiny_names = ("conv_w", "b_rg_gates", "ffn_conv_w")
    tiny_at = ((slice(0, 4), slice(0, 256)), (slice(4, 6), slice(0, 256)), (slice(8, 11), slice(None)))
    done.update(zip(tiny_names, _adamw_pieces(
        reduced["tiny"], [(w[k], m[k], v[k]) + at for k, at in zip(tiny_names, tiny_at)], "adamw_tiny")))
    vector_rows, loss_row = _vector_rows()
    pieces = [as2d("pool_w", (POOL_W_ROWS, SMALL_COLS)) + (slice(0, POOL_W_ROWS), slice(None))]
    pieces += [as2d(k, (1, -1)) + (vector_rows[k],) for k in _VECTORS]
    done.update(zip(("pool_w",) + _VECTORS, _adamw_pieces(reduced["small"], pieces, "adamw_small")))
    done["g_mix_pre"] = _adamw_pieces(g_mix_pre_parts, [as2d("g_mix_pre", (1, -1)) + (0,)], "adamw_g_mix_pre")[0]

    result = [reduced["small"][loss_row, 0], grad_x[None]]
    for kind in range(4):
        result += [done[k][kind].reshape(args[k].shape) for k in _WEIGHTS]
    return tuple(result)
```

```python
import functools

import jax
import jax.numpy as jnp
from jax import lax
from jax.experimental import pallas as pl
from jax.experimental.pallas import tpu as pltpu

F32 = jnp.float32
BF16 = jnp.bfloat16

D_MODEL = 1024
POOL_WINDOWS = (2, 4, 8, 16)
POOL_GROUPS = 4
POOL_WIDTH = 512
POOL_GROUP_DIM = 128
RNN_HEADS = 16
RNN_HEAD_DIM = 64
GATE_BLOCK = 256
GATE_BLOCKS = D_MODEL // GATE_BLOCK
LRU_C = 8.0
D_FF = 3072
PLE_DIM = 256
RMS_EPS = 1e-6
IN_TOTAL = 4608
N_CHIPS = 4
IN_SHARD = IN_TOTAL // N_CHIPS
UP_SHARD = 2 * D_FF // N_CHIPS
Z_GROUPS = ((0, 512), (512, 1536), (1536, 2560), (2560, 3584), (3584, 4608))
POOL_HALO = 16
CONV_HALO = 8

ADAM_LR = 0.001
ADAM_B1 = 0.9
ADAM_B2 = 0.999
ADAM_EPS = 1e-08
ADAM_WD = 0.01
ADAM_STEP = 10

VMEM_LIMIT = 56 * 1024 * 1024
MESH = pl.DeviceIdType.MESH

_GELU_C = 0.7978845608028654
_GELU_A = 0.044715


def _dot(a, b):
    return jnp.dot(a.astype(BF16), b.astype(BF16), preferred_element_type=F32)


def _dot_nt(a, b):
    return lax.dot_general(a.astype(BF16), b.astype(BF16), (((1,), (1,)), ((), ())), preferred_element_type=F32)


def _dot_tn(a, b):
    return lax.dot_general(a.astype(BF16), b.astype(BF16), (((0,), (0,)), ((), ())), preferred_element_type=F32)


def _overlaps(group):
    a, b = Z_GROUPS[group]
    found = []
    for j in range(N_CHIPS):
        lo, hi = max(a, j * IN_SHARD), min(b, (j + 1) * IN_SHARD)
        if lo < hi:
            found.append((j, slice(lo - j * IN_SHARD, hi - j * IN_SHARD), slice(lo - a, hi - a)))
    return found


def _rms_fwd(x, g):
    r = lax.rsqrt(jnp.mean(x * x, axis=-1, keepdims=True) + RMS_EPS)
    xh = x * r
    return xh * g, xh, r


def _rms_bwd(xh, r, g, dy):
    dxh = dy * g
    dg = jnp.sum(dy * xh, axis=0, keepdims=True)
    dx = r * (dxh - xh * jnp.mean(dxh * xh, axis=-1, keepdims=True))
    return dx, dg


def _sigmoid(x):
    return pl.reciprocal(1.0 + jnp.exp(-x), approx=True)


def _gelu(x):
    x2 = x * x
    p = pl.reciprocal(1.0 + jnp.exp(x * ((-2.0 * _GELU_C) + (-2.0 * _GELU_C * _GELU_A) * x2)), approx=True)
    gl = x * p
    return gl, p + gl * (1.0 - p) * (2.0 * _GELU_C + (6.0 * _GELU_C * _GELU_A) * x2)


def _softplus_neg(lam):
    nl = -lam
    return jnp.maximum(nl, 0.0) + jnp.log(1.0 + jnp.exp(-jnp.abs(nl)))


def _lru_coeffs(r, lam, first_row):
    c8 = LRU_C * _softplus_neg(lam)
    la = -(c8 * r)
    a = jnp.exp(la)
    m2 = jnp.tanh(-la) * (1.0 + a * a)
    mult = jnp.where(first_row, 1.0, jnp.sqrt(m2))
    return c8, a, m2, mult


SUBLANES = 8


def _scan_fwd(a, u, carry):
    n = a.shape[0]
    sub = lax.broadcasted_iota(jnp.int32, (n, 1), 0) % SUBLANES
    acc_a, acc_h = a, u
    for s in (1, 2, 4):
        m = sub >= s
        h_s = jnp.where(m, pltpu.roll(acc_h, s, 0), 0.0)
        a_s = jnp.where(m, pltpu.roll(acc_a, s, 0), 1.0)
        acc_h = acc_a * h_s + acc_h
        acc_a = acc_a * a_s
    out = []
    for g in range(n // SUBLANES):
        rows = slice(g * SUBLANES, (g + 1) * SUBLANES)
        out.append(acc_h[rows] + acc_a[rows] * carry)
        carry = out[-1][SUBLANES - 1:SUBLANES]
    return jnp.concatenate(out, axis=0)


def _scan_bwd(b, g, carry):
    n = b.shape[0]
    sub = lax.broadcasted_iota(jnp.int32, (n, 1), 0) % SUBLANES
    acc_b, acc_l = b, g
    for s in (1, 2, 4):
        m = sub < SUBLANES - s
        l_s = jnp.where(m, pltpu.roll(acc_l, n - s, 0), 0.0)
        b_s = jnp.where(m, pltpu.roll(acc_b, n - s, 0), 1.0)
        acc_l = acc_b * l_s + acc_l
        acc_b = acc_b * b_s
    out = [None] * (n // SUBLANES)
    for g in reversed(range(n // SUBLANES)):
        rows = slice(g * SUBLANES, (g + 1) * SUBLANES)
        out[g] = acc_l[rows] + acc_b[rows] * carry
        carry = out[g][0:1]
    return jnp.concatenate(out, axis=0)


def _shift_down(ext, k, halo):
    return pltpu.roll(ext, k, 0)[halo:] if k else ext[halo:]


def _shift_up(ext, k, ts):
    return pltpu.roll(ext, ext.shape[0] - k, 0)[:ts] if k else ext[:ts]


def _rows(ts, width, nt=None, col=0):
    if nt is None:
        return pl.BlockSpec((ts, width), lambda i: (i, col))
    return pl.BlockSpec((ts, width), lambda i: (nt - 1 - i, col))


def _resident(shape):
    zeros = (0,) * len(shape)
    return pl.BlockSpec(shape, lambda i: zeros, pipeline_mode=pl.Buffered(1))


def _acc(shape):
    zeros = (0,) * len(shape)
    return pl.BlockSpec(shape, lambda i: zeros)


def _params():
    return pltpu.CompilerParams(dimension_semantics=("arbitrary",), vmem_limit_bytes=VMEM_LIMIT)


def _sds(shape, dtype=F32):
    return jax.ShapeDtypeStruct(shape, dtype)


class _Task:
    def __init__(self, ins, out_shapes, aliases, sems, start, finish, relays=()):
        self.ins, self.out_shapes, self.aliases, self.sems = list(ins), list(out_shapes), dict(aliases), list(sems)
        self.start, self.relays, self.finish = start, list(relays), finish


def _call(body, name, grid, in_specs, out_specs, out_shape, scratch_shapes, args, tasks=()):
    n_in, n_out, n_scr = len(in_specs), len(out_specs), len(scratch_shapes)
    t_in = [len(t.ins) for t in tasks]
    t_out = [len(t.out_shapes) for t in tasks]
    t_sem = [len(t.sems) for t in tasks]
    steps = 1
    for g in grid:
        steps *= g

    def take(refs, pos, counts):
        groups = []
        for c in counts:
            groups.append(refs[pos:pos + c])
            pos += c
        return groups, pos

    def wrapped(*refs):
        (cin,), pos = take(refs, 0, [n_in])
        tin, pos = take(refs, pos, t_in)
        (cout,), pos = take(refs, pos, [n_out])
        tout, pos = take(refs, pos, t_out)
        (cscr,), pos = take(refs, pos, [n_scr])
        tsem, pos = take(refs, pos, t_sem)
        if not grid:
            for t, a, b, c in zip(tasks, tin, tout, tsem):
                t.start(a, b, c)
            if body is not None:
                body(*cin, *cout, *cscr)
            for t, a, b, c in zip(tasks, tin, tout, tsem):
                for relay, _ in t.relays:
                    relay(a, b, c)
            for t, a, b, c in zip(tasks, tin, tout, tsem):
                t.finish(a, b, c)
            return
        step = pl.program_id(0)
        for axis in range(1, len(grid)):
            step = step * grid[axis] + pl.program_id(axis)
        if tasks:
            @pl.when(step == 0)
            def _():
                for t, a, b, c in zip(tasks, tin, tout, tsem):
                    t.start(a, b, c)

        body(*cin, *cout, *cscr)
        for t, a, b, c in zip(tasks, tin, tout, tsem):
            for relay, before in t.relays:
                pl.when(step == max(steps - 1 - before, 0))(functools.partial(relay, a, b, c))

        if tasks:
            @pl.when(step == steps - 1)
            def _():
                for t, a, b, c in zip(tasks, tin, tout, tsem):
                    t.finish(a, b, c)

    aliases, in_pos, out_pos = {}, n_in, n_out
    for t, ni, no in zip(tasks, t_in, t_out):
        aliases.update({in_pos + a: out_pos + b for a, b in t.aliases.items()})
        in_pos, out_pos = in_pos + ni, out_pos + no
    any_spec = pl.BlockSpec(memory_space=pltpu.HBM)
    kwargs = dict(grid=grid, compiler_params=pltpu.CompilerParams(
        dimension_semantics=("arbitrary",) * len(grid), vmem_limit_bytes=VMEM_LIMIT)) if grid else dict(
        compiler_params=pltpu.CompilerParams(vmem_limit_bytes=VMEM_LIMIT))
    out = pl.pallas_call(
        wrapped, name=name,
        in_specs=list(in_specs) + [any_spec] * sum(t_in),
        out_specs=list(out_specs) + [any_spec] * sum(t_out),
        out_shape=list(out_shape) + [s for t in tasks for s in t.out_shapes],
        scratch_shapes=list(scratch_shapes) + [s for t in tasks for s in t.sems],
        input_output_aliases=aliases, **kwargs,
    )(*args, *[pltpu.with_memory_space_constraint(a, pltpu.HBM) for t in tasks for a in t.ins])
    task_outs, pos = take(list(out), n_out, t_out)
    return list(out[:n_out]), task_outs


def _fwd_in_pool(x, g_pre, w_in, pool_w, pool_scale, w_pool_out, ts, tasks=()):
    s = x.shape[0]

    def body(x_ref, g_ref, win_ref, pw_ref, ps_ref, wpo_ref,
             urx_ref, urg_ref, gp_ref, gr_ref, d_ref, yp_ref, h1_ref, halo_scr):
        i = pl.program_id(0)

        @pl.when(i == 0)
        def _():
            halo_scr[...] = jnp.zeros_like(halo_scr)

        h1, _, _ = _rms_fwd(x_ref[...], g_ref[...])
        h1 = h1.astype(BF16)
        h1_ref[...] = h1
        u = jnp.dot(h1, win_ref[0, :, 0:POOL_WIDTH], preferred_element_type=F32)
        for group, out_ref in zip(range(1, len(Z_GROUPS)), (urx_ref, urg_ref, gp_ref, gr_ref)):
            for j, shard_cols, group_cols in _overlaps(group):
                out_ref[:, group_cols] = jnp.dot(h1, win_ref[j, :, shard_cols],
                                                 preferred_element_type=F32).astype(BF16)
        ext = jnp.concatenate([halo_scr[...], u], axis=0)
        halo_scr[...] = u[ts - POOL_HALO:, :]
        t = i * ts + lax.broadcasted_iota(jnp.int32, (ts, 1), 0)
        y4 = []
        for g, w in enumerate(POOL_WINDOWS):
            lanes = slice(g * POOL_GROUP_DIM, (g + 1) * POOL_GROUP_DIM)
            acc = ext[:, lanes]
            sh = 1
            while sh < w:
                acc = acc + pltpu.roll(acc, sh, 0)
                sh *= 2
            inv = 1.0 / jnp.minimum(t + 1, w).astype(F32)
            dg = acc[POOL_HALO:, :] * inv - u[:, lanes]
            d_ref[:, lanes] = dg.astype(BF16)
            y4.append(_dot(dg, pw_ref[g]))
        ypre = jnp.concatenate(y4, axis=1) * ps_ref[...]
        ypre = ypre.astype(BF16)
        for j in range(N_CHIPS):
            yp_ref[:, j * 256:(j + 1) * 256] = jnp.dot(ypre, wpo_ref[j], preferred_element_type=F32)

    return _call(
        body, "fwd_in_pool", (s // ts,),
        [_rows(ts, D_MODEL), _resident((1, D_MODEL)), _resident(w_in.shape), _resident(pool_w.shape),
         _resident((1, POOL_WIDTH)), _resident(w_pool_out.shape)],
        [_rows(ts, D_MODEL)] * 4 + [_rows(ts, POOL_WIDTH), _rows(ts, D_MODEL), _rows(ts, D_MODEL)],
        [_sds((s, D_MODEL), BF16)] * 4 + [_sds((s, POOL_WIDTH), BF16), _sds((s, D_MODEL)), _sds((s, D_MODEL), BF16)],
        [pltpu.VMEM((POOL_HALO, POOL_WIDTH), F32)],
        (x, g_pre, w_in, pool_w, pool_scale, w_pool_out), tasks)


def _fwd_rnn_merge(urx, urg, gp, gr, ypool, x, conv_w, conv_b, wg, bg, lam, w_rg_out, w_o, g_post, ts, tasks=()):
    s = x.shape[0]

    def body(urx_ref, urg_ref, gp_ref, gr_ref, yp_ref, x_ref, cw_ref, cb_ref, wg_ref, bg_ref, lam_ref, wrg_ref, wo_ref,
             gpost_ref, xc_ref, r_ref, ig_ref, h_ref, yr_ref, mo_ref, x1_ref, gl_ref, gg_ref, sp_ref, sr_ref,
             halo_scr, carry_scr):
        i = pl.program_id(0)

        @pl.when(i == 0)
        def _():
            halo_scr[...] = jnp.zeros_like(halo_scr)
            carry_scr[...] = jnp.zeros_like(carry_scr)

        urx_v = urx_ref[...].astype(F32)
        ext = jnp.concatenate([halo_scr[...], urx_v], axis=0)
        halo_scr[...] = urx_v[ts - CONV_HALO:, :]
        cw = cw_ref[...]
        xc = (cb_ref[...] + cw[3:4] * urx_v + cw[2:3] * _shift_down(ext, 1, CONV_HALO)
              + cw[1:2] * _shift_down(ext, 2, CONV_HALO) + cw[0:1] * _shift_down(ext, 3, CONV_HALO))
        xc_ref[...] = xc.astype(BF16)
        xcb = xc.astype(BF16)
        lin = []
        for gate in range(2):
            parts = [jnp.dot(xcb[:, q * GATE_BLOCK:(q + 1) * GATE_BLOCK], wg_ref[gate, q], preferred_element_type=F32)
                     for q in range(GATE_BLOCKS)]
            lin.append(jnp.concatenate(parts, axis=1) + bg_ref[gate:gate + 1, :])
        r = _sigmoid(lin[0])
        ig = _sigmoid(lin[1])
        r_ref[...] = r.astype(BF16)
        ig_ref[...] = ig.astype(BF16)
        first_row = (i * ts + lax.broadcasted_iota(jnp.int32, (ts, 1), 0)) == 0
        _, a, _, mult = _lru_coeffs(r, lam_ref[...], first_row)
        h = _scan_fwd(a, mult * ig * xc, carry_scr[0:1, :])
        carry_scr[0:1, :] = h[ts - 1:ts, :]
        h_ref[...] = h
        urg_v = urg_ref[...].astype(F32)
        gl, gg = _gelu(urg_v)
        gl_ref[...] = gl.astype(BF16)
        gg_ref[...] = gg.astype(BF16)
        yr = _dot(h * gl, wrg_ref[...])
        yr_ref[...] = yr.astype(BF16)
        sp = _sigmoid(gp_ref[...].astype(F32))
        sr = _sigmoid(gr_ref[...].astype(F32))
        sp_ref[...] = sp.astype(BF16)
        sr_ref[...] = sr.astype(BF16)
        merged = sp * yp_ref[...] + sr * yr
        mo = _dot(merged, wo_ref[...])
        mo_ref[...] = mo
        y, _, _ = _rms_fwd(mo, gpost_ref[...])
        x1_ref[...] = x_ref[...] + y

    row = _rows(ts, D_MODEL)
    return _call(
        body, "fwd_rnn_merge", (s // ts,),
        [row] * 6 + [_resident(conv_w.shape), _resident((1, D_MODEL)), _resident(wg.shape), _resident(bg.shape),
                     _resident((1, D_MODEL)), _resident(w_rg_out.shape), _resident(w_o.shape), _resident((1, D_MODEL))],
        [row] * 11, [_sds((s, D_MODEL), dt) for dt in (BF16, BF16, BF16, F32, BF16, F32, F32, BF16, BF16, BF16, BF16)],
        [pltpu.VMEM((CONV_HALO, D_MODEL), F32), pltpu.VMEM((8, D_MODEL), F32)],
        (urx, urg, gp, gr, ypool, x, conv_w, conv_b, wg, bg, lam, w_rg_out, w_o, g_post), tasks)


def _fwd_ffn(x1, g_pre, w_up, fcw, fcb, w_down, g_post, ts, tasks=()):
    s = x1.shape[0]

    def body(x1_ref, g_ref, wup_ref, fcw_ref, fcb_ref, wd_ref, gpost_ref,
             up_ref, gl_ref, gg_ref, h2_ref, dn_ref, x2_ref, up_scr, halo_scr):
        i = pl.program_id(0)

        @pl.when(i == 0)
        def _():
            halo_scr[...] = jnp.zeros_like(halo_scr)

        x1_v = x1_ref[...]
        h2, _, _ = _rms_fwd(x1_v, g_ref[...])
        h2 = h2.astype(BF16)
        h2_ref[...] = h2
        for j in range(N_CHIPS):
            up_scr[:, j * UP_SHARD:(j + 1) * UP_SHARD] = jnp.dot(h2, wup_ref[j], preferred_element_type=F32)
        up_ref[...] = up_scr[...].astype(BF16)
        ug = up_scr[:, 0:D_FF]
        ext = jnp.concatenate([halo_scr[...], ug], axis=0)
        halo_scr[...] = ug[ts - CONV_HALO:, :]
        w = fcw_ref[...]
        gh = (fcb_ref[...] + w[2:3] * ug + w[1:2] * _shift_down(ext, 1, CONV_HALO)
              + w[0:1] * _shift_down(ext, 2, CONV_HALO))
        gl, gg = _gelu(gh)
        gl_ref[...] = gl.astype(BF16)
        gg_ref[...] = gg.astype(BF16)
        dn = _dot(gl * up_scr[:, D_FF:], wd_ref[...])
        dn_ref[...] = dn
        y, _, _ = _rms_fwd(dn, gpost_ref[...])
        x2_ref[...] = x1_v + y

    row = _rows(ts, D_MODEL)
    return _call(
        body, "fwd_ffn", (s // ts,),
        [row, _resident((1, D_MODEL)), _resident(w_up.shape), _resident(fcw.shape), _resident((1, D_FF)),
         _resident(w_down.shape), _resident((1, D_MODEL))],
        [_rows(ts, 2 * D_FF), _rows(ts, D_FF), _rows(ts, D_FF), row, row, row],
        [_sds((s, 2 * D_FF), BF16), _sds((s, D_FF), BF16), _sds((s, D_FF), BF16), _sds((s, D_MODEL), BF16),
         _sds((s, D_MODEL)), _sds((s, D_MODEL))],
        [pltpu.VMEM((ts, 2 * D_FF), F32), pltpu.VMEM((CONV_HALO, D_FF), F32)],
        (x1, g_pre, w_up, fcw, fcb, w_down, g_post), tasks)


def _ple_loss(x2, p, tgt, g_gate, w_gate, w_proj, g_post, ts):
    s = x2.shape[0]

    def body(x2_ref, p_ref, t_ref, gg_ref, wg_ref, wp_ref, gp_ref, dx2_ref, loss_ref, dwg_ref, dwp_ref, dgg_ref, dgp_ref):
        @pl.when(pl.program_id(0) == 0)
        def _():
            loss_ref[...] = jnp.zeros_like(loss_ref)
            dwg_ref[...] = jnp.zeros_like(dwg_ref)
            dwp_ref[...] = jnp.zeros_like(dwp_ref)
            dgg_ref[...] = jnp.zeros_like(dgg_ref)
            dgp_ref[...] = jnp.zeros_like(dgp_ref)

        x2_v = x2_ref[...]
        n3, xh3, r3 = _rms_fwd(x2_v, gg_ref[...])
        pg = _sigmoid(_dot(n3, wg_ref[...]))
        pb = p_ref[...].astype(BF16)
        q = jnp.concatenate([jnp.dot(pb, wp_ref[j], preferred_element_type=F32) for j in range(N_CHIPS)], axis=1)
        ple, qh, rq = _rms_fwd(q, gp_ref[...])
        e = x2_v + pg * ple - t_ref[...]
        loss_ref[...] += 0.5 * jnp.sum(jnp.mean(e * e, axis=-1, keepdims=True), axis=0, keepdims=True)
        dy = e * (1.0 / D_MODEL)
        dpgl = dy * ple * pg * (1.0 - pg)
        dwg_ref[...] += _dot_tn(n3, dpgl)
        dx3, dgg = _rms_bwd(xh3, r3, gg_ref[...], _dot_nt(dpgl, wg_ref[...]))
        dgg_ref[...] += dgg
        dq, dgp = _rms_bwd(qh, rq, gp_ref[...], dy * pg)
        dgp_ref[...] += dgp
        for j in range(N_CHIPS):
            dwp_ref[j] += _dot_tn(pb, dq[:, j * 256:(j + 1) * 256])
        dx2_ref[...] = dy + dx3

    row = _rows(ts, D_MODEL)
    vec = _acc((1, D_MODEL))
    return pl.pallas_call(
        body, name="ple_loss", grid=(s // ts,),
        in_specs=[row, _rows(ts, PLE_DIM), row, _resident((1, D_MODEL)), _resident(w_gate.shape), _resident(w_proj.shape),
                  _resident((1, D_MODEL))],
        out_specs=[row, _acc((1, 128)), _acc(w_gate.shape), _acc(w_proj.shape), vec, vec],
        out_shape=[_sds((s, D_MODEL)), _sds((1, 128)), _sds(w_gate.shape), _sds(w_proj.shape), _sds((1, D_MODEL)),
                   _sds((1, D_MODEL))],
        compiler_params=_params(),
    )(x2, p, tgt, g_gate, w_gate, w_proj, g_post)


def _bwd_ffn_down(dx2, dn, up, gl, gg, fcw, w_down, g_post, ts):
    s = dx2.shape[0]
    nt = s // ts

    def body(dx2_ref, dn_ref, up_ref, gl_ref, gg_ref, fcw_ref, wd_ref, gpost_ref,
             dup_ref, dwd_ref, dfcw_ref, dfcb_ref, dgp_ref, carry_scr):
        i = pl.program_id(0)

        @pl.when(i == 0)
        def _():
            carry_scr[...] = jnp.zeros_like(carry_scr)
            dwd_ref[...] = jnp.zeros_like(dwd_ref)
            dfcw_ref[...] = jnp.zeros_like(dfcw_ref)
            dfcb_ref[...] = jnp.zeros_like(dfcb_ref)
            dgp_ref[...] = jnp.zeros_like(dgp_ref)

        _, xh, r = _rms_fwd(dn_ref[...], gpost_ref[...])
        ddn, dgp = _rms_bwd(xh, r, gpost_ref[...], dx2_ref[...])
        dgp_ref[...] += dgp
        dhid = _dot_nt(ddn, wd_ref[...])
        ug = up_ref[:, 0:D_FF].astype(F32)
        uv = up_ref[:, D_FF:]
        gl = gl_ref[...]
        w = fcw_ref[...]
        dwd_ref[...] += _dot_tn(gl * uv, ddn)
        dgh = dhid * (uv * gg_ref[...]).astype(F32)
        dup_ref[:, D_FF:] = dhid.astype(BF16) * gl
        extd = jnp.concatenate([dgh, carry_scr[...]], axis=0)
        carry_scr[...] = dgh[0:CONV_HALO, :]
        d1 = _shift_up(extd, 1, ts)
        d2 = _shift_up(extd, 2, ts)
        dup_ref[:, 0:D_FF] = (w[2:3] * dgh + w[1:2] * d1 + w[0:1] * d2).astype(BF16)
        dfcw_ref[2:3, :] += jnp.sum(ug * dgh, axis=0, keepdims=True)
        dfcw_ref[1:2, :] += jnp.sum(ug * d1, axis=0, keepdims=True)
        dfcw_ref[0:1, :] += jnp.sum(ug * d2, axis=0, keepdims=True)
        dfcb_ref[...] += jnp.sum(dgh, axis=0, keepdims=True)

    row = _rows(ts, D_MODEL, nt)
    wide = _rows(ts, D_FF, nt)
    return pl.pallas_call(
        body, name="bwd_ffn_down", grid=(nt,),
        in_specs=[row, row, _rows(ts, 2 * D_FF, nt), wide, wide, _resident(fcw.shape), _resident(w_down.shape),
                  _resident((1, D_MODEL))],
        out_specs=[_rows(ts, 2 * D_FF, nt), _acc(w_down.shape), _acc(fcw.shape), _acc((1, D_FF)), _acc((1, D_MODEL))],
        out_shape=[_sds((s, 2 * D_FF), BF16), _sds(w_down.shape), _sds(fcw.shape), _sds((1, D_FF)), _sds((1, D_MODEL))],
        scratch_shapes=[pltpu.VMEM((CONV_HALO, D_FF), F32)],
        compiler_params=_params(),
    )(dx2, dn, up, gl, gg, fcw, w_down, g_post)


def _bwd_ffn_up(dup, x1, dx2, g_pre, w_up, ts, tasks=()):
    s = x1.shape[0]

    def body(dup_ref, x1_ref, dx2_ref, g_ref, wup_ref, dx1_ref, dg_ref):
        @pl.when(pl.program_id(0) == 0)
        def _():
            dg_ref[...] = jnp.zeros_like(dg_ref)

        _, xh, r = _rms_fwd(x1_ref[...], g_ref[...])
        dh2 = _dot_nt(dup_ref[:, 0:UP_SHARD], wup_ref[0])
        for j in range(1, N_CHIPS):
            dh2 = dh2 + _dot_nt(dup_ref[:, j * UP_SHARD:(j + 1) * UP_SHARD], wup_ref[j])
        dx, dg = _rms_bwd(xh, r, g_ref[...], dh2)
        dg_ref[...] += dg
        dx1_ref[...] = dx2_ref[...] + dx

    row = _rows(ts, D_MODEL)
    return _call(
        body, "bwd_ffn_up", (s // ts,),
        [_rows(ts, 2 * D_FF), row, row, _resident((1, D_MODEL)), _resident(w_up.shape)],
        [row, _acc((1, D_MODEL))], [_sds((s, D_MODEL)), _sds((1, D_MODEL))], [],
        (dup, x1, dx2, g_pre, w_up), tasks)


def _dw_up(h2, dup, ts, tasks=()):
    s = h2.shape[0]
    ts = min(DW_TILES * ts, s)
    nt = s // ts
    half = D_MODEL // 2

    def body(h2_ref, dup_ref, send_ref, acc_ref, theirs_scr, mine_scr, got_scr, sems):
        j, o, i = pl.program_id(0), pl.program_id(1), pl.program_id(2)
        x, y, c, me, _ = _place()
        prod = _dot_tn(h2_ref[...], dup_ref[...])
        for scr, which in ((theirs_scr, 0), (mine_scr, 1)):
            @pl.when((o == which) & (i == 0))
            def _():
                scr[...] = prod

            @pl.when((o == which) & (i > 0))
            def _():
                scr[...] += prod

        slot = j % 2
        push = _remote(theirs_scr, got_scr.at[slot], sems.at[0, slot], sems.at[1, slot], (x, y, 1 - c))
        pl.when((o == 1) & (i == 0))(push.start)

        @pl.when((o == 1) & (i == nt - 1))
        def _():
            push.wait()
            part = (mine_scr[...] + got_scr[slot]).astype(BF16)
            send_ref[0] = part

            @pl.when(me == j)
            def _():
                acc_ref[0] = part

    def h2_cols(j, o, i):
        c = lax.axis_index("c")
        return i, jnp.where(o == 0, 1 - c, c)

    def own_slot(j, o, i):
        return 2 * lax.axis_index("x") + lax.axis_index("y"), 0, 0

    block = (1, half, UP_SHARD)
    return _call(
        body, "dw_up", (N_CHIPS, 2, nt),
        [pl.BlockSpec((ts, half), h2_cols), pl.BlockSpec((ts, UP_SHARD), lambda j, o, i: (i, j))],
        [pl.BlockSpec(block, lambda j, o, i: (j, 0, 0)), pl.BlockSpec(block, own_slot)],
        [_sds((N_CHIPS, half, UP_SHARD), BF16)] * 2,
        [pltpu.VMEM((half, UP_SHARD), F32), pltpu.VMEM((half, UP_SHARD), F32), pltpu.VMEM((2, half, UP_SHARD), F32),
         pltpu.SemaphoreType.DMA((2, 2))],
        (h2, dup), tasks)


def _bwd_merge(dx1, mo, sp, sr, ypool, yrnn, g_post, w_o, ts, tasks=()):
    s = dx1.shape[0]

    def body(dx1_ref, mo_ref, sp_ref, sr_ref, yp_ref, yr_ref, g_ref, wo_ref,
             dgp_ref, dgr_ref, dyp_ref, dyr_ref, dwo_ref, dg_ref):
        @pl.when(pl.program_id(0) == 0)
        def _():
            dwo_ref[...] = jnp.zeros_like(dwo_ref)
            dg_ref[...] = jnp.zeros_like(dg_ref)

        _, xh, r = _rms_fwd(mo_ref[...], g_ref[...])
        dmo, dg = _rms_bwd(xh, r, g_ref[...], dx1_ref[...])
        dg_ref[...] += dg
        dmerged = _dot_nt(dmo, wo_ref[...])
        sp = sp_ref[...].astype(F32)
        sr = sr_ref[...].astype(F32)
        yp = yp_ref[...]
        yr = yr_ref[...].astype(F32)
        dwo_ref[...] += _dot_tn(sp * yp + sr * yr, dmo)
        dgp_ref[...] = (dmerged * yp * sp * (1.0 - sp)).astype(BF16)
        dgr_ref[...] = (dmerged * yr * sr * (1.0 - sr)).astype(BF16)
        dyp_ref[...] = (dmerged * sp).astype(BF16)
        dyr_ref[...] = (dmerged * sr).astype(BF16)

    row = _rows(ts, D_MODEL)
    return _call(
        body, "bwd_merge", (s // ts,),
        [row] * 6 + [_resident((1, D_MODEL)), _resident(w_o.shape)],
        [row] * 4 + [_acc(w_o.shape), _acc((1, D_MODEL))],
        [_sds((s, D_MODEL), BF16)] * 4 + [_sds(w_o.shape), _sds((1, D_MODEL))], [],
        (dx1, mo, sp, sr, ypool, yrnn, g_post, w_o), tasks)


def _bwd_rnn(dyr, urx, gl, gg, xc, r, ig, h, conv_w, wg, lam, w_rg_out, ts, tasks=()):
    s = urx.shape[0]
    nt = s // ts
    halo_blocks = ts // CONV_HALO

    def body(dyr_ref, urx_ref, gl_ref, gg_ref, xc_ref, r_ref, ig_ref, h_ref, hh_ref, cw_ref, wg_ref, lam_ref, wrg_ref,
             durx_ref, durg_ref, dwrg_ref, dwg_ref, dcw_ref, dcb_ref, dbg_ref, dlam_ref, mu_scr, carry_scr):
        i = pl.program_id(0)
        k = nt - 1 - i

        @pl.when(i == 0)
        def _():
            mu_scr[...] = jnp.zeros_like(mu_scr)
            carry_scr[...] = jnp.zeros_like(carry_scr)
            dwrg_ref[...] = jnp.zeros_like(dwrg_ref)
            dwg_ref[...] = jnp.zeros_like(dwg_ref)
            dcw_ref[...] = jnp.zeros_like(dcw_ref)
            dcb_ref[...] = jnp.zeros_like(dcb_ref)
            dbg_ref[...] = jnp.zeros_like(dbg_ref)
            dlam_ref[...] = jnp.zeros_like(dlam_ref)

        row = lax.broadcasted_iota(jnp.int32, (ts, 1), 0)
        first_row = (k * ts + row) == 0
        h = h_ref[...]
        dyr_v = dyr_ref[...]
        dhr = _dot_nt(dyr_v, wrg_ref[...])
        gl = gl_ref[...].astype(F32)
        dwrg_ref[...] += _dot_tn(h * gl, dyr_v)
        durg_ref[...] = (dhr * h * gg_ref[...].astype(F32)).astype(BF16)
        r_v = r_ref[...].astype(F32)
        ig_v = ig_ref[...].astype(F32)
        xc_v = xc_ref[...].astype(F32)
        lam_v = lam_ref[...]
        c8, a, m2, mult = _lru_coeffs(r_v, lam_v, first_row)
        b = jnp.where(row == ts - 1, 1.0, pltpu.roll(a, ts - 1, 0))
        lt = _scan_bwd(b, dhr * gl, mu_scr[0:1, :])
        mu_scr[0:1, :] = a[0:1, :] * lt[0:1, :]
        h_before = jnp.where(k > 0, hh_ref[CONV_HALO - 1:CONV_HALO, :], 0.0)
        hprev = jnp.where(row == 0, h_before, pltpu.roll(h, 1, 0))
        dmult = lt * ig_v * xc_v
        da = lt * hprev - jnp.where(first_row, 0.0, dmult * a * lax.rsqrt(m2))
        dla = da * a
        dlam_ref[...] += jnp.sum(dla * r_v, axis=0, keepdims=True)
        dlr = (dla * (-c8)) * r_v * (1.0 - r_v)
        dli = (lt * mult * xc_v) * ig_v * (1.0 - ig_v)
        dbg_ref[0:1, :] += jnp.sum(dlr, axis=0, keepdims=True)
        dbg_ref[1:2, :] += jnp.sum(dli, axis=0, keepdims=True)
        xcb = xc_v.astype(BF16)
        parts = []
        for q in range(GATE_BLOCKS):
            blk = slice(q * GATE_BLOCK, (q + 1) * GATE_BLOCK)
            dlr_q = dlr[:, blk].astype(BF16)
            dli_q = dli[:, blk].astype(BF16)
            parts.append(_dot_nt(dlr_q, wg_ref[0, q]) + _dot_nt(dli_q, wg_ref[1, q]))
            dwg_ref[0, q] += _dot_tn(xcb[:, blk], dlr_q)
            dwg_ref[1, q] += _dot_tn(xcb[:, blk], dli_q)
        dxc = lt * mult * ig_v + jnp.concatenate(parts, axis=1)
        extd = jnp.concatenate([dxc, carry_scr[...]], axis=0)
        carry_scr[...] = dxc[0:CONV_HALO, :]
        cw = cw_ref[...]
        urx_v = urx_ref[...].astype(F32)
        durx = cw[3:4] * dxc
        dcw_ref[3:4, :] += jnp.sum(urx_v * dxc, axis=0, keepdims=True)
        for j in (1, 2, 3):
            dj = _shift_up(extd, j, ts)
            durx = durx + cw[3 - j:4 - j] * dj
            dcw_ref[3 - j:4 - j, :] += jnp.sum(urx_v * dj, axis=0, keepdims=True)
        durx_ref[...] = durx.astype(BF16)
        dcb_ref[...] += jnp.sum(dxc, axis=0, keepdims=True)

        @pl.when(i == nt - 1)
        def _():
            dlam_ref[...] = dlam_ref[...] * (LRU_C * jax.nn.sigmoid(-lam_v))

    row_spec = _rows(ts, D_MODEL, nt)
    halo_spec = pl.BlockSpec((CONV_HALO, D_MODEL), lambda i: (jnp.maximum((nt - 1 - i) * halo_blocks - 1, 0), 0))
    vec = _acc((1, D_MODEL))
    return _call(
        body, "bwd_rnn", (nt,),
        [row_spec] * 8 + [halo_spec, _resident(conv_w.shape), _resident(wg.shape), _resident((1, D_MODEL)),
                          _resident(w_rg_out.shape)],
        [row_spec, row_spec, _acc(w_rg_out.shape), _acc(wg.shape), _acc(conv_w.shape), vec, _acc((2, D_MODEL)), vec],
        [_sds((s, D_MODEL), BF16), _sds((s, D_MODEL), BF16), _sds(w_rg_out.shape), _sds(wg.shape), _sds(conv_w.shape),
         _sds((1, D_MODEL)), _sds((2, D_MODEL)), _sds((1, D_MODEL))],
        [pltpu.VMEM((8, D_MODEL), F32), pltpu.VMEM((CONV_HALO, D_MODEL), F32)],
        (dyr, urx, gl, gg, xc, r, ig, h, h, conv_w, wg, lam, w_rg_out), tasks)


def _bwd_pool(dyp, d, pool_w, pool_scale, w_pool_out, ts, tasks=()):
    s = d.shape[0]
    nt = s // ts

    def body(dyp_ref, d_ref, pw_ref, ps_ref, wpo_ref, dzp_ref, dwpo_ref, dpw_ref, dps_ref, carry_scr):
        i = pl.program_id(0)
        k = nt - 1 - i

        @pl.when(i == 0)
        def _():
            carry_scr[...] = jnp.zeros_like(carry_scr)
            dwpo_ref[...] = jnp.zeros_like(dwpo_ref)
            dpw_ref[...] = jnp.zeros_like(dpw_ref)
            dps_ref[...] = jnp.zeros_like(dps_ref)

        dyp_v = dyp_ref[...]
        d_v = d_ref[...]
        ps = ps_ref[...]
        dypre = _dot_nt(dyp_v[:, 0:256], wpo_ref[0])
        for j in range(1, N_CHIPS):
            dypre = dypre + _dot_nt(dyp_v[:, j * 256:(j + 1) * 256], wpo_ref[j])
        y4 = jnp.concatenate([_dot(d_v[:, g * 128:(g + 1) * 128], pw_ref[g]) for g in range(POOL_GROUPS)], axis=1)
        ypre = (y4 * ps).astype(BF16)
        for j in range(N_CHIPS):
            dwpo_ref[j] += _dot_tn(ypre, dyp_v[:, j * 256:(j + 1) * 256])
        dps_ref[...] += jnp.sum(dypre * y4, axis=0, keepdims=True)
        dy4 = dypre * ps
        t = k * ts + lax.broadcasted_iota(jnp.int32, (ts, 1), 0)
        for g, w in enumerate(POOL_WINDOWS):
            lanes = slice(g * POOL_GROUP_DIM, (g + 1) * POOL_GROUP_DIM)
            dd = _dot_nt(dy4[:, lanes], pw_ref[g])
            dpw_ref[g] += _dot_tn(d_v[:, lanes], dy4[:, lanes])
            e = dd * (1.0 / jnp.minimum(t + 1, w).astype(F32))
            acc = jnp.concatenate([e, carry_scr[:, lanes]], axis=0)
            carry_scr[:, lanes] = e[0:POOL_HALO, :]
            n = ts + POOL_HALO
            sh = 1
            while sh < w:
                acc = acc + pltpu.roll(acc, n - sh, 0)
                sh *= 2
            dzp_ref[:, lanes] = (acc[:ts, :] - dd).astype(BF16)

    return _call(
        body, "bwd_pool", (nt,),
        [_rows(ts, D_MODEL, nt), _rows(ts, POOL_WIDTH, nt), _resident(pool_w.shape), _resident((1, POOL_WIDTH)),
         _resident(w_pool_out.shape)],
        [_rows(ts, POOL_WIDTH, nt), _acc(w_pool_out.shape), _acc(pool_w.shape), _acc((1, POOL_WIDTH))],
        [_sds((s, POOL_WIDTH), BF16), _sds(w_pool_out.shape), _sds(pool_w.shape), _sds((1, POOL_WIDTH))],
        [pltpu.VMEM((POOL_HALO, POOL_WIDTH), F32)],
        (dyp, d, pool_w, pool_scale, w_pool_out), tasks)


def _assemble_dz(dz_scr, dzp_ref, durx_ref, durg_ref, dgp_ref, dgr_ref):
    dz_scr[:, 0:512] = dzp_ref[...]
    dz_scr[:, 512:1536] = durx_ref[...]
    dz_scr[:, 1536:2560] = durg_ref[...]
    dz_scr[:, 2560:3584] = dgp_ref[...]
    dz_scr[:, 3584:4608] = dgr_ref[...]


def _dw_in(h1, dzp, durx, durg, dgp, dgr, ts, tasks=()):
    s = h1.shape[0]
    ts = min(2 * ts, s)
    nt = s // ts
    half = D_MODEL // 2

    def body(h1_ref, dzp_ref, durx_ref, durg_ref, dgp_ref, dgr_ref, send_ref, acc_ref, theirs_ref, mine_ref, got_ref, sems):
        o, i = pl.program_id(0), pl.program_id(1)

        @pl.when((o == 0) & (i == 0))
        def _():
            theirs_ref[...] = jnp.zeros_like(theirs_ref)
            mine_ref[...] = jnp.zeros_like(mine_ref)

        groups = (dzp_ref, durx_ref, durg_ref, dgp_ref, dgr_ref)
        for out_ref, which in ((theirs_ref, 0), (mine_ref, 1)):
            @pl.when(o == which)
            def _():
                for group, dz_ref in enumerate(groups):
                    for j, shard_cols, group_cols in _overlaps(group):
                        out_ref[j, :, shard_cols] += _dot_tn(h1_ref[...], dz_ref[:, group_cols])

        x, y, c, me, _ = _place()
        send = _remote(theirs_ref, got_ref, sems.at[0], sems.at[1], (x, y, 1 - c))
        pl.when((o == 1) & (i == 0))(send.start)

        @pl.when((o == 1) & (i == nt - 1))
        def _():
            send.wait()
            for k in range(N_CHIPS):
                part = (mine_ref[k] + got_ref[k]).astype(BF16)
                send_ref[k] = part

                @pl.when(me == k)
                def _():
                    acc_ref[k] = part

    def h1_cols(o, i):
        c = lax.axis_index("c")
        return i, jnp.where(o == 0, 1 - c, c)

    rows = lambda width: pl.BlockSpec((ts, width), lambda o, i: (i, 0))
    shape = (N_CHIPS, half, IN_SHARD)
    whole = pl.BlockSpec(shape, lambda o, i: (0, 0, 0))
    return _call(
        body, "dw_in", (2, nt), [pl.BlockSpec((ts, half), h1_cols), rows(POOL_WIDTH)] + [rows(D_MODEL)] * 4,
        [whole, whole], [_sds(shape, BF16)] * 2,
        [pltpu.VMEM(shape, F32)] * 3 + [pltpu.SemaphoreType.DMA((2,))], (h1, dzp, durx, durg, dgp, dgr), tasks)


def _bwd_in(dzp, durx, durg, dgp, dgr, x, dx1, g_pre, w_in, ts, tasks=()):
    s = x.shape[0]

    def body(dzp_ref, durx_ref, durg_ref, dgp_ref, dgr_ref, x_ref, dx1_ref, g_ref, win_ref, gx_ref, dg_ref, dz_scr):
        @pl.when(pl.program_id(0) == 0)
        def _():
            dg_ref[...] = jnp.zeros_like(dg_ref)

        _assemble_dz(dz_scr, dzp_ref, durx_ref, durg_ref, dgp_ref, dgr_ref)
        _, xh, r = _rms_fwd(x_ref[...], g_ref[...])
        dh1 = _dot_nt(dz_scr[:, 0:IN_SHARD], win_ref[0])
        for j in range(1, N_CHIPS):
            dh1 = dh1 + _dot_nt(dz_scr[:, j * IN_SHARD:(j + 1) * IN_SHARD], win_ref[j])
        dx, dg = _rms_bwd(xh, r, g_ref[...], dh1)
        dg_ref[...] += dg
        gx_ref[...] = dx1_ref[...] + dx

    row = _rows(ts, D_MODEL)
    return _call(
        body, "bwd_in", (s // ts,),
        [_rows(ts, POOL_WIDTH)] + [row] * 6 + [_resident((1, D_MODEL)), _resident(w_in.shape)],
        [row, _acc((1, D_MODEL))], [_sds((s, D_MODEL)), _sds((1, D_MODEL))],
        [pltpu.VMEM((ts, IN_TOTAL), BF16)], (dzp, durx, durg, dgp, dgr, x, dx1, g_pre, w_in), tasks)


def _place():
    x, y, c = lax.axis_index("x"), lax.axis_index("y"), lax.axis_index("c")
    others = [(1 - x, y), (x, 1 - y), (1 - x, 1 - y)]
    return x, y, c, 2 * x + y, others


def _remote(src, dst, send_sem, recv_sem, to):
    return pltpu.make_async_remote_copy(src_ref=src, dst_ref=dst, send_sem=send_sem, recv_sem=recv_sem,
                                        device_id=to, device_id_type=MESH)


def _own_slots(ws, dtypes, name, tasks=()):
    n = len(ws)
    hbm = pl.BlockSpec(memory_space=pltpu.HBM)

    def body(*refs):
        srcs, outs, f32_bufs, cast_bufs, sems = refs[:n], refs[n:2 * n], refs[2 * n:3 * n], refs[3 * n:4 * n], refs[4 * n]
        me = _place()[3]
        loads = [pltpu.make_async_copy(srcs[k], f32_bufs[k], sems.at[k, 0]) for k in range(n)]
        stores = [pltpu.make_async_copy(cast_bufs[k], outs[k].at[me], sems.at[k, 1]) for k in range(n)]
        for cp in loads:
            cp.start()
        for k in range(n):
            loads[k].wait()
            cast_bufs[k][...] = f32_bufs[k][...].astype(dtypes[k])
            stores[k].start()
        for cp in stores:
            cp.wait()

    return _call(
        body, name, (), [hbm] * n, [hbm] * n, [_sds((N_CHIPS,) + w.shape, dt) for w, dt in zip(ws, dtypes)],
        [pltpu.VMEM(w.shape, F32) for w in ws] + [pltpu.VMEM(w.shape, dt) for w, dt in zip(ws, dtypes)]
        + [pltpu.SemaphoreType.DMA((n, 2))],
        [pltpu.with_memory_space_constraint(w, pltpu.HBM) for w in ws], tasks)


def _run(tasks, name):
    if isinstance(tasks, _Task):
        return _call(None, name, (), [], [], [], [], (), (tasks,))[1][0]
    return _call(None, name, (), [], [], [], [], (), tuple(tasks))[1]


def _gather_task(bufs, relay_steps=(0, 0)):
    n = len(bufs)
    NBR_X, NBR_Y, QUARTER_VIA_Y, QUARTER_VIA_X, SIB_X, SIB_Y, SIB_DIAG = range(7)

    def parts(out):
        x, y, c, me, _ = _place()
        ah = out.shape[1] // 2
        q = ah // 2 if (ah // 2) % 16 == 0 else ah
        return c * ah, ah, q

    def copy(out, w, k, chip, row0, rows, to, sems):
        slot = out.at[chip, pl.ds(row0, rows)]
        return _remote(slot, slot, sems[0].at[w, k], sems[1].at[w, k], to)

    def plan(out, w, sems):
        x, y, c, me, _ = _place()
        row0, ah, q = parts(out)
        xn, yn, dg = 2 * (1 - x) + y, 2 * x + (1 - y), 2 * (1 - x) + (1 - y)
        to_x, to_y, sib = (1 - x, y, c), (x, 1 - y, c), (x, y, 1 - c)
        other = (1 - c) * ah
        cp = functools.partial(copy, out, w, sems=sems)
        sends = {NBR_X: cp(NBR_X, me, row0, ah, to_x), NBR_Y: cp(NBR_Y, me, row0, ah, to_y),
                 QUARTER_VIA_Y: cp(QUARTER_VIA_Y, xn, row0, q, to_y), SIB_X: cp(SIB_X, xn, row0, ah, sib),
                 SIB_Y: cp(SIB_Y, yn, row0, ah, sib), SIB_DIAG: cp(SIB_DIAG, dg, row0, ah, sib)}
        lands = {NBR_X: cp(NBR_X, xn, row0, ah, to_x), NBR_Y: cp(NBR_Y, yn, row0, ah, to_y),
                 QUARTER_VIA_Y: cp(QUARTER_VIA_Y, dg, row0, q, to_y), SIB_X: cp(SIB_X, xn, other, ah, sib),
                 SIB_Y: cp(SIB_Y, yn, other, ah, sib), SIB_DIAG: cp(SIB_DIAG, dg, other, ah, sib)}
        if q < ah:
            sends[QUARTER_VIA_X] = cp(QUARTER_VIA_X, yn, row0 + q, ah - q, to_x)
            lands[QUARTER_VIA_X] = cp(QUARTER_VIA_X, dg, row0 + q, ah - q, to_x)
        return sends, lands

    def start(ins, outs, sems):
        for w, out in enumerate(outs):
            sends, _ = plan(out, w, sems)
            sends[NBR_X].start()
            sends[NBR_Y].start()

    def pass_neighbours(ins, outs, sems):
        for w, out in enumerate(outs):
            sends, lands = plan(out, w, sems)
            lands[NBR_X].wait_recv()
            sends[QUARTER_VIA_Y].start()
            sends[SIB_X].start()
            lands[NBR_Y].wait_recv()
            if QUARTER_VIA_X in sends:
                sends[QUARTER_VIA_X].start()
            sends[SIB_Y].start()

    def pass_diagonal(ins, outs, sems):
        for w, out in enumerate(outs):
            sends, lands = plan(out, w, sems)
            lands[QUARTER_VIA_Y].wait_recv()
            if QUARTER_VIA_X in lands:
                lands[QUARTER_VIA_X].wait_recv()
            sends[SIB_DIAG].start()

    def finish(ins, outs, sems):
        for w, out in enumerate(outs):
            sends, lands = plan(out, w, sems)
            for k in (SIB_X, SIB_Y, SIB_DIAG):
                lands[k].wait_recv()
        for w, out in enumerate(outs):
            sends, _ = plan(out, w, sems)
            for cp in sends.values():
                cp.wait_send()

    return _Task(bufs, [_sds(b.shape, b.dtype) for b in bufs], {i: i for i in range(n)},
                 [pltpu.SemaphoreType.DMA((n, 7)), pltpu.SemaphoreType.DMA((n, 7))], start, finish,
                 [(pass_neighbours, relay_steps[0]), (pass_diagonal, relay_steps[1])])


def _halves_task(grads):
    n = len(grads)

    def copy(src, out, w, sems):
        x, y, c, _, _ = _place()
        ah = out.shape[1]
        return _remote(src.at[:, pl.ds((1 - c) * ah, ah)], out, sems[0].at[w], sems[1].at[w], (x, y, 1 - c))

    def start(ins, outs, sems):
        for w, (src, out) in enumerate(zip(ins, outs)):
            copy(src, out, w, sems).start()

    def finish(ins, outs, sems):
        for w, (src, out) in enumerate(zip(ins, outs)):
            copy(src, out, w, sems).wait()

    return _Task(grads, [_sds((g.shape[0], g.shape[1] // 2, g.shape[2]), g.dtype) for g in grads], {},
                 [pltpu.SemaphoreType.DMA((n,)), pltpu.SemaphoreType.DMA((n,))], start, finish)


def _exchange_task(sends, accs):
    n = len(accs)
    given = [s for s in sends if s is not None]

    def copies(ins, outs, sems):
        send_refs = iter(ins[:len(given)])
        srcs = [next(send_refs) if s is not None else None for s in sends]
        x, y, c, me, others = _place()
        for w, out in enumerate(outs):
            for j, (ox, oy) in enumerate(others):
                src = out.at[me] if srcs[w] is None else srcs[w].at[2 * ox + oy]
                yield _remote(src, out.at[me], sems[0].at[w, j], sems[1].at[w, j], (ox, oy, c))

    def start(ins, outs, sems):
        for cp in copies(ins, outs, sems):
            cp.start()

    def finish(ins, outs, sems):
        x, y, c, _, others = _place()
        for w, out in enumerate(outs):
            for j, (ox, oy) in enumerate(others):
                slot = out.at[2 * ox + oy]
                _remote(slot, slot, sems[0].at[w, j], sems[1].at[w, j], (ox, oy, c)).wait_recv()
        for cp in copies(ins, outs, sems):
            cp.wait_send()

    return _Task(given + list(accs), [_sds(a.shape, a.dtype) for a in accs], {len(given) + i: i for i in range(n)},
                 [pltpu.SemaphoreType.DMA((n, 3)), pltpu.SemaphoreType.DMA((n, 3))], start, finish)


def _swap_task(arrays):
    n = len(arrays)

    def copy(src, out, w, sems):
        x, y, c, _, _ = _place()
        return _remote(src, out, sems[0].at[w], sems[1].at[w], (x, y, 1 - c))

    def start(ins, outs, sems):
        for w, (src, out) in enumerate(zip(ins, outs)):
            copy(src, out, w, sems).start()

    def finish(ins, outs, sems):
        for w, (src, out) in enumerate(zip(ins, outs)):
            copy(src, out, w, sems).wait()

    return _Task(arrays, [_sds(a.shape, a.dtype) for a in arrays], {},
                 [pltpu.SemaphoreType.DMA((n,)), pltpu.SemaphoreType.DMA((n,))], start, finish)


def _all_devices_task(arrays):
    n = len(arrays)
    flips = [(dx, dy, dc) for dx in (0, 1) for dy in (0, 1) for dc in (0, 1)][1:]

    def peers():
        x, y, c, _, _ = _place()
        flip = lambda v, d: 1 - v if d else v
        return 4 * x + 2 * y + c, [(flip(x, dx), flip(y, dy), flip(c, dc)) for dx, dy, dc in flips]

    def start(ins, outs, sems):
        me, others = peers()
        for w, (src, out) in enumerate(zip(ins, outs)):
            pltpu.make_async_copy(src, out.at[me], sems[2].at[w]).start()
            for k, peer in enumerate(others):
                _remote(src, out.at[me], sems[0].at[w, k], sems[1].at[w, k], peer).start()

    def finish(ins, outs, sems):
        me, others = peers()
        for w, (src, out) in enumerate(zip(ins, outs)):
            for k, (px, py, pc) in enumerate(others):
                slot = out.at[4 * px + 2 * py + pc]
                _remote(slot, slot, sems[0].at[w, k], sems[1].at[w, k], (px, py, pc)).wait_recv()
            for k, peer in enumerate(others):
                _remote(src, out.at[me], sems[0].at[w, k], sems[1].at[w, k], peer).wait_send()
            pltpu.make_async_copy(src, out.at[me], sems[2].at[w]).wait()

    return _Task(arrays, [_sds((8,) + a.shape, a.dtype) for a in arrays], {},
                 [pltpu.SemaphoreType.DMA((n, 7)), pltpu.SemaphoreType.DMA((n, 7)), pltpu.SemaphoreType.DMA((n,))],
                 start, finish)


def _share_task(shares):
    n = len(shares)

    def copy(out, w, sems, slot):
        x, y, c, _, _ = _place()
        return _remote(out.at[slot], out.at[slot], sems[0].at[w], sems[1].at[w], (x, y, 1 - c))

    def start(ins, outs, sems):
        c = _place()[2]
        for w, out in enumerate(outs):
            copy(out, w, sems, c).start()

    def finish(ins, outs, sems):
        c = _place()[2]
        for w, out in enumerate(outs):
            copy(out, w, sems, 1 - c).wait_recv()
        for w, out in enumerate(outs):
            copy(out, w, sems, c).wait_send()

    return _Task(shares, [_sds(s.shape, s.dtype) for s in shares], {i: i for i in range(n)},
                 [pltpu.SemaphoreType.DMA((n,)), pltpu.SemaphoreType.DMA((n,))], start, finish)


TILE_BYTES = 2 * 1024 * 1024
PARTIAL_TILE_BYTES = 1024 * 1024


def _in_hbm(t):
    return pltpu.with_memory_space_constraint(t, pltpu.HBM)


def _row_tile(rows, cols, limit=TILE_BYTES):
    best = 8
    for tr in range(8, rows + 1, 8):
        if rows % tr == 0 and tr * cols * 4 <= limit:
            best = tr
    assert rows % best == 0, (rows, cols)
    return best


def _chip_partial(g, got, place, wire_dtype):
    ns, ah, b = got.shape
    sharded = ns == N_CHIPS
    tr = _row_tile(ah, b, PARTIAL_TILE_BYTES)
    nb = ah // tr
    first = 0 if g.shape[1] == ah else nb

    def body(place_ref, *refs):
        g_refs, got_refs, outs = refs[:ns], refs[ns:2 * ns], refs[2 * ns:]
        parts = [g_refs[k][0] + got_refs[k][0] for k in range(ns)]
        own = parts[0]
        if sharded:
            for k in range(ns):
                outs[0][k] = parts[k].astype(wire_dtype)
                if k:
                    own = jnp.where(place_ref[0] == k, parts[k], own)
        outs[-1][0] = own.astype(wire_dtype)

    blk = (1, tr, b)
    in_specs = ([pl.BlockSpec(blk, lambda i, s, k=k: (k, s[1] * first + i, 0)) for k in range(ns)]
                + [pl.BlockSpec(blk, lambda i, s, k=k: (k, i, 0)) for k in range(ns)])
    acc_spec = pl.BlockSpec(blk, lambda i, s: (s[0], i, 0))
    acc_shape = _sds((N_CHIPS, ah, b), wire_dtype)
    out = pl.pallas_call(
        body, name="grad_chip_partial",
        grid_spec=pltpu.PrefetchScalarGridSpec(
            num_scalar_prefetch=1, grid=(nb,), in_specs=in_specs,
            out_specs=[pl.BlockSpec((ns, tr, b), lambda i, s: (0, i, 0)), acc_spec] if sharded else [acc_spec]),
        out_shape=[acc_shape, acc_shape] if sharded else [acc_shape],
        compiler_params=pltpu.CompilerParams(dimension_semantics=("arbitrary",), vmem_limit_bytes=VMEM_LIMIT),
    )(place, *([g] * ns), *([got] * ns))
    return (out[0], out[1]) if sharded else (None, out[0])


def _chip_sum(acc, place):
    _, ah, b = acc.shape
    tr = _row_tile(ah, b)

    def body(place_ref, p_ref, out_ref):
        total = p_ref[0].astype(F32) + p_ref[1].astype(F32)
        total = total + p_ref[2].astype(F32)
        out_ref[0] = total + p_ref[3].astype(F32)

    return pl.pallas_call(
        body, name="grad_chip_sum",
        grid_spec=pltpu.PrefetchScalarGridSpec(
            num_scalar_prefetch=1, grid=(ah // tr,),
            in_specs=[pl.BlockSpec((N_CHIPS, tr, b), lambda i, s: (0, i, 0))],
            out_specs=pl.BlockSpec((1, tr, b), lambda i, s: (s[1], i, 0))),
        out_shape=_sds((2, ah, b)),
        compiler_params=pltpu.CompilerParams(dimension_semantics=("arbitrary",)),
    )(place, _in_hbm(acc))


def _adam_math(w, g, m, v):
    nm = ADAM_B1 * m + (1.0 - ADAM_B1) * g
    nv = ADAM_B2 * v + (1.0 - ADAM_B2) * (g * g)
    m_hat = nm / (1.0 - ADAM_B1 ** ADAM_STEP)
    v_hat = nv / (1.0 - ADAM_B2 ** ADAM_STEP)
    return -ADAM_LR * (m_hat / (jnp.sqrt(v_hat) + ADAM_EPS) + ADAM_WD * w), nm, nv


def _adamw(w, g, m, v):
    a, b = w.shape
    tr = _row_tile(a, b)

    def body(w_ref, g_ref, m_ref, v_ref, g_out, d_ref, nm_ref, nv_ref):
        g_out[...] = g_ref[...]
        d_ref[...], nm_ref[...], nv_ref[...] = _adam_math(w_ref[...], g_ref[...], m_ref[...], v_ref[...])

    blk = pl.BlockSpec((tr, b), lambda i: (i, 0))
    return pl.pallas_call(
        body, name="adamw", grid=(a // tr,),
        in_specs=[blk] * 4, out_specs=[blk] * 4, out_shape=[_sds((a, b))] * 4,
        compiler_params=pltpu.CompilerParams(dimension_semantics=("arbitrary",)),
    )(w, g, m, v)


def _adamw_sum(w, m, v, acc, got, place):
    a, b = w.shape
    ah = a // 2
    tr = _row_tile(ah, b)
    nb = ah // tr

    def body(place_ref, w_ref, m_ref, v_ref, acc_ref, got_ref, g_out, d_ref, nm_ref, nv_ref):
        mine = (pl.program_id(0) // nb) == place_ref[1]
        part = lambda k: jnp.where(mine, acc_ref[k], got_ref[k]).astype(F32)
        g = part(0) + part(1)
        g = g + part(2)
        g = g + part(3)
        g_out[...] = g
        d_ref[...], nm_ref[...], nv_ref[...] = _adam_math(w_ref[...], g, m_ref[...], v_ref[...])

    blk = pl.BlockSpec((tr, b), lambda i, s: (i, 0))
    mine_spec = pl.BlockSpec((N_CHIPS, tr, b), lambda i, s: (0, jnp.where(i // nb == s[1], i % nb, 0), 0))
    got_spec = pl.BlockSpec((N_CHIPS, tr, b), lambda i, s: (0, jnp.where(i // nb == s[1], 0, i % nb), 0))
    return pl.pallas_call(
        body, name="adamw_sum",
        grid_spec=pltpu.PrefetchScalarGridSpec(
            num_scalar_prefetch=1, grid=(a // tr,), in_specs=[blk] * 3 + [mine_spec, got_spec], out_specs=[blk] * 4),
        out_shape=[_sds((a, b))] * 4,
        compiler_params=pltpu.CompilerParams(dimension_semantics=("arbitrary",), vmem_limit_bytes=VMEM_LIMIT),
    )(place, w, m, v, _in_hbm(acc), _in_hbm(got))


def _adamw_pieces(g, pieces, name):
    n = len(pieces)

    def body(g_ref, *refs):
        def grad(rows, cols):
            if len(g_ref.shape) == 2:
                return g_ref[rows, cols]
            total = g_ref[0, rows, cols]
            for k in range(1, g_ref.shape[0]):
                total = total + g_ref[k, rows, cols]
            return total

        ins, outs = refs[:3 * n], refs[3 * n:]
        for i, piece in enumerate(pieces):
            w_ref, m_ref, v_ref = ins[3 * i:3 * i + 3]
            o_g, o_d, o_m, o_v = outs[4 * i:4 * i + 4]
            if len(piece) == 5:
                g_v = grad(piece[3], piece[4])
                o_g[...] = g_v
                o_d[...], o_m[...], o_v[...] = _adam_math(w_ref[...], g_v, m_ref[...], v_ref[...])
            else:
                for r in range(w_ref.shape[1] // SMALL_COLS):
                    lanes = slice(r * SMALL_COLS, (r + 1) * SMALL_COLS)
                    g_v = grad(slice(piece[3] + r, piece[3] + r + 1), slice(None))
                    o_g[:, lanes] = g_v
                    o_d[:, lanes], o_m[:, lanes], o_v[:, lanes] = _adam_math(w_ref[:, lanes], g_v, m_ref[:, lanes],
                                                                            v_ref[:, lanes])

    operands = [t for piece in pieces for t in piece[:3]]
    out = pl.pallas_call(
        body, name=name,
        out_shape=[_sds(piece[0].shape) for piece in pieces for _ in range(4)],
    )(g, *operands)
    return [tuple(out[4 * i:4 * i + 4]) for i in range(n)]


TINY_ROWS, TINY_COLS = 16, 768
SMALL_COLS = 128
SMALL_ROWS = 624


def _pack_tiny(conv_w, b_gates, fcw):
    ns = conv_w.shape[0]
    pad = lambda t: jnp.pad(t, ((0, 0), (0, 0), (0, TINY_COLS - t.shape[2])))
    z = lambda rows: jnp.zeros((ns, rows, TINY_COLS), F32)
    return jnp.concatenate([pad(conv_w), pad(b_gates), z(2), fcw, z(TINY_ROWS - 11)], axis=1)


def _unpack_tiny(t):
    return t[:, 0:4, 0:256], t[:, 4:6, 0:256], t[:, 8:11, :]


def _cols_to_shards(t, n):
    return t.reshape(t.shape[0], N_CHIPS, n).transpose(1, 0, 2)


def _shards_to_cols(t):
    return t.transpose(1, 0, 2).reshape(t.shape[1], -1)


_VECTORS = ("g_mix_post", "conv_b", "lru_lambda", "g_ffn_pre", "g_ffn_post", "g_ple_gate", "g_ple_post", "pool_scale",
            "ffn_conv_b")
_VECTOR_LEN = {"pool_scale": POOL_WIDTH, "ffn_conv_b": D_FF}
POOL_W_ROWS = POOL_GROUPS * POOL_GROUP_DIM


def _vector_rows():
    rows, row = {}, POOL_W_ROWS
    for k in _VECTORS:
        rows[k] = row
        row += max(8, _VECTOR_LEN.get(k, D_MODEL) // SMALL_COLS)
    return rows, row


def _pack_small(grads, loss):
    tiles = lambda t: jnp.pad(t, ((0, -t.shape[0] % 8), (0, 0)))
    parts = [grads["pool_w"].reshape(POOL_W_ROWS, SMALL_COLS)] + [tiles(grads[k].reshape(-1, SMALL_COLS)) for k in _VECTORS]
    parts.append(tiles(loss))
    used = sum(t.shape[0] for t in parts)
    return jnp.concatenate(parts + [jnp.zeros((SMALL_ROWS - used, SMALL_COLS), F32)], axis=0)


def _gates_block_diag(w):
    w4 = w.reshape(2, GATE_BLOCKS, 4, RNN_HEAD_DIM, RNN_HEAD_DIM)
    eye = jnp.eye(4, dtype=w.dtype)
    return jnp.einsum("gqhij,hk->gqhikj", w4, eye).reshape(2, GATE_BLOCKS, GATE_BLOCK, GATE_BLOCK)


def _gates_from_block_diag(dw):
    d6 = dw.reshape(2, GATE_BLOCKS, 4, RNN_HEAD_DIM, 4, RNN_HEAD_DIM)
    blocks = [d6[:, :, hh, :, hh, :] for hh in range(4)]
    return jnp.stack(blocks, axis=2).reshape(2, RNN_HEADS, RNN_HEAD_DIM, RNN_HEAD_DIM)


ROW_TILE = 256
DW_TILES = 8

_SHARDED = ("w_in", "w_pool_out", "w_rg_out", "w_o", "w_up", "w_down", "w_ple_gate", "w_ple_proj")
_WEIGHTS = ("g_mix_pre", "g_mix_post", "w_in", "pool_w", "pool_scale", "w_pool_out", "conv_w", "conv_b", "w_rg_gates",
            "b_rg_gates", "lru_lambda", "w_rg_out", "w_o", "g_ffn_pre", "g_ffn_post", "w_up", "ffn_conv_w", "ffn_conv_b",
            "w_down", "g_ple_gate", "w_ple_gate", "w_ple_proj", "g_ple_post")


def _wire_dtype(g):
    return BF16 if g.shape[1] >= 64 and g.shape[2] > SMALL_COLS else F32


def _partials(grads, got, place):
    parts = [_chip_partial(g, r, place, _wire_dtype(g)) for g, r in zip(grads, got)]
    return [send for send, _ in parts], [acc for _, acc in parts]


def _whole(both):
    return [b.reshape(2 * b.shape[1], b.shape[2]) for b in both]


def _step(x, p, tgt, rep, place, ts):
    vec = lambda k: rep[k].reshape(1, -1)
    tall = min(2 * ts, x.shape[0])
    pool_w = rep["pool_w"].astype(BF16)
    wg = _gates_block_diag(rep["w_rg_gates"]).astype(BF16)
    sq = lambda t: t.reshape(D_MODEL, D_MODEL)
    by4 = lambda t: t.reshape(N_CHIPS, -1, D_MODEL)

    first, ride1, ride2 = (("w_in", "w_pool_out", "tiny"), ("w_rg_out", "w_o", "w_down"),
                           ("w_up", "w_ple_gate", "w_ple_proj"))
    later = ride1 + ride2
    tiny = _pack_tiny(rep["conv_w"][None], rep["b_rg_gates"][None], rep["ffn_conv_w"][None])[0]
    own_first, _ = _own_slots([rep["w_in"], rep["w_pool_out"], tiny], [BF16, BF16, F32], "own_slots_first")
    own_later, (got,) = _own_slots([rep[k] for k in later], [BF16] * len(later), "own_slots_gather_first",
                                   [_gather_task(own_first)])
    own = dict(zip(later, own_later))
    full = dict(zip(first, got))
    conv_w, b_gates, fcw = [_shards_to_cols(t) for t in _unpack_tiny(full["tiny"])]

    (urx, urg, gp, gr, d, ypool, h1), (got,) = _fwd_in_pool(
        x, vec("g_mix_pre"), full["w_in"], pool_w, vec("pool_scale"), full["w_pool_out"], ts,
        [_gather_task([own[k] for k in ride1], relay_steps=(6, 2))])
    full.update(zip(ride1, got))
    w_rg_out, w_o, w_down = sq(full["w_rg_out"]), sq(full["w_o"]), full["w_down"].reshape(D_FF, D_MODEL)
    (xc, r, ig, h, yrnn, mo, x1, glr, ggr, sp, sr), (got,) = _fwd_rnn_merge(
        urx, urg, gp, gr, ypool, x, conv_w, vec("conv_b"), wg, b_gates, vec("lru_lambda"), w_rg_out, w_o,
        vec("g_mix_post"), ts, [_gather_task([own[k] for k in ride2], relay_steps=(7, 3))])
    full.update(zip(ride2, got))
    (up, gl, gg, h2, dn, x2), _ = _fwd_ffn(x1, vec("g_ffn_pre"), full["w_up"], fcw, vec("ffn_conv_b"), w_down,
                                           vec("g_ffn_post"), ts)
    dx2, loss, d_w_gate, d_w_proj, d_g_ple_gate, d_g_ple_post = _ple_loss(
        x2, p, tgt, vec("g_ple_gate"), sq(full["w_ple_gate"]), full["w_ple_proj"], vec("g_ple_post"), tall)
    dup, d_w_down, d_fcw, d_fcb, d_g_ffn_post = _bwd_ffn_down(dx2, dn, up, gl, gg, fcw, w_down, vec("g_ffn_post"), ts)

    names1, grads1 = ("w_ple_gate", "w_ple_proj", "w_down"), [by4(d_w_gate), d_w_proj, by4(d_w_down)]
    (dx1, d_g_ffn_pre), (got1,) = _bwd_ffn_up(dup, x1, dx2, vec("g_ffn_pre"), full["w_up"], tall, [_halves_task(grads1)])
    (send_w_up, acc_w_up), (accs1,) = _dw_up(h2, dup, ts, [_exchange_task(*_partials(grads1, got1, place))])
    (dgp, dgr, dyp, dyr, d_w_o, d_g_mix_post), (theirs1,) = _bwd_merge(
        dx1, mo, sp, sr, ypool, yrnn, vec("g_mix_post"), w_o, tall, [_swap_task(accs1)])
    (durx, durg, d_w_rg_out, d_wg, d_conv_w, d_conv_b, d_b_gates, d_lam), (accs2,) = _bwd_rnn(
        dyr, urx, glr, ggr, xc, r, ig, h, conv_w, wg, vec("lru_lambda"), w_rg_out, ts,
        [_exchange_task([send_w_up], [acc_w_up])])
    names3 = ("w_o", "w_rg_out", "tiny", "w_rg_gates")
    grads3 = [by4(d_w_o), by4(d_w_rg_out),
              _pack_tiny(_cols_to_shards(d_conv_w, 256), _cols_to_shards(d_b_gates, 256), _cols_to_shards(d_fcw, 768)),
              _gates_from_block_diag(d_wg).reshape(1, 2 * RNN_HEADS * RNN_HEAD_DIM, RNN_HEAD_DIM)]
    (dzp, d_w_pool_out, d_pool_w, d_pool_scale), (got3, theirs2) = _bwd_pool(
        dyp, d, pool_w, vec("pool_scale"), full["w_pool_out"], tall, [_halves_task(grads3), _swap_task(accs2)])
    replicated = {"g_mix_post": d_g_mix_post, "conv_b": d_conv_b, "lru_lambda": d_lam, "g_ffn_pre": d_g_ffn_pre,
                  "g_ffn_post": d_g_ffn_post, "g_ple_gate": d_g_ple_gate, "g_ple_post": d_g_ple_post,
                  "pool_scale": d_pool_scale, "ffn_conv_b": d_fcb, "pool_w": d_pool_w}
    names4 = ("w_in", "w_pool_out", "small")
    small4 = [d_w_pool_out, _pack_small(replicated, loss)[None]]
    (send_w_in, acc_w_in), (accs3, got_small4) = _dw_in(
        h1, dzp, durx, durg, dgp, dgr, ts, [_exchange_task(*_partials(grads3, got3, place)), _halves_task(small4)])
    sends4, accs4 = _partials(small4, got_small4, place)
    (grad_x, d_g_mix_pre), (accs4, theirs3, both3) = _bwd_in(
        dzp, durx, durg, dgp, dgr, x, dx1, vec("g_mix_pre"), full["w_in"], tall,
        [_exchange_task([send_w_in] + sends4, [acc_w_in] + accs4), _swap_task(accs3[:2]),
         _share_task([_chip_sum(acc, place) for acc in accs3[2:]])])
    theirs4, both4, (g_mix_pre_parts,) = _run(
        [_swap_task(accs4[:2]), _share_task([_chip_sum(acc, place) for acc in accs4[2:]]),
         _all_devices_task([d_g_mix_pre.reshape(SUBLANES, SMALL_COLS)])], "grad_sibling_share")
    mine = accs1 + accs2 + accs3[:2] + accs4[:2]
    partials = dict(zip(names1 + ("w_up",) + names3[:2] + names4[:2], zip(mine, theirs1 + theirs2 + theirs3 + theirs4)))
    return grad_x, partials, dict(zip(names3[2:] + names4[2:], _whole(both3) + _whole(both4))), g_mix_pre_parts


def kernel(x, p, g_mix_pre, g_mix_post, w_in, pool_w, pool_scale, w_pool_out, conv_w, conv_b, w_rg_gates, b_rg_gates, lru_lambda, w_rg_out, w_o, g_ffn_pre, g_ffn_post, w_up, ffn_conv_w, ffn_conv_b, w_down, g_ple_gate, w_ple_gate, w_ple_proj, g_ple_post, loss_target, m_g_mix_pre, m_g_mix_post, m_w_in, m_pool_w, m_pool_scale, m_w_pool_out, m_conv_w, m_conv_b, m_w_rg_gates, m_b_rg_gates, m_lru_lambda, m_w_rg_out, m_w_o, m_g_ffn_pre, m_g_ffn_post, m_w_up, m_ffn_conv_w, m_ffn_conv_b, m_w_down, m_g_ple_gate, m_w_ple_gate, m_w_ple_proj, m_g_ple_post, v_g_mix_pre, v_g_mix_post, v_w_in, v_pool_w, v_pool_scale, v_w_pool_out, v_conv_w, v_conv_b, v_w_rg_gates, v_b_rg_gates, v_lru_lambda, v_w_rg_out, v_w_o, v_g_ffn_pre, v_g_ffn_post, v_w_up, v_ffn_conv_w, v_ffn_conv_b, v_w_down, v_g_ple_gate, v_w_ple_gate, v_w_ple_proj, v_g_ple_post):
    args = dict(locals())
    w = {k: args[k][0] for k in _WEIGHTS}
    m = {k: args["m_" + k][0] for k in _WEIGHTS}
    v = {k: args["v_" + k][0] for k in _WEIGHTS}
    place = jnp.stack([2 * lax.axis_index("x") + lax.axis_index("y"), lax.axis_index("c")]).astype(jnp.int32)
    grad_x, partials, reduced, g_mix_pre_parts = _step(x[0], p[0, 0], loss_target[0], w, place, ROW_TILE)

    gates_2d = (2 * RNN_HEADS * RNN_HEAD_DIM, RNN_HEAD_DIM)
    as2d = lambda k, shape: tuple(t[k].reshape(shape) for t in (w, m, v))
    done = {k: tuple(_adamw_sum(w[k], m[k], v[k], *partials[k], place)) for k in _SHARDED}
    gates_w, gates_m, gates_v = as2d("w_rg_gates", gates_2d)
    done["w_rg_gates"] = tuple(_adamw(gates_w, reduced["w_rg_gates"], gates_m, gates_v))
    tiny_names = ("conv_w", "b_rg_gates", "ffn_conv_w")
    tiny_at = ((slice(0, 4), slice(0, 256)), (slice(4, 6), slice(0, 256)), (slice(8, 11), slice(None)))
    done.update(zip(tiny_names, _adamw_pieces(
        reduced["tiny"], [(w[k], m[k], v[k]) + at for k, at in zip(tiny_names, tiny_at)], "adamw_tiny")))
    vector_rows, loss_row = _vector_rows()
    pieces = [as2d("pool_w", (POOL_W_ROWS, SMALL_COLS)) + (slice(0, POOL_W_ROWS), slice(None))]
    pieces += [as2d(k, (1, -1)) + (vector_rows[k],) for k in _VECTORS]
    done.update(zip(("pool_w",) + _VECTORS, _adamw_pieces(reduced["small"], pieces, "adamw_small")))
    done["g_mix_pre"] = _adamw_pieces(g_mix_pre_parts, [as2d("g_mix_pre", (1, -1)) + (0,)], "adamw_g_mix_pre")[0]

    result = [reduced["small"][loss_row, 0], grad_x[None]]
    for kind in range(4):
        result += [done[k][kind].reshape(args[k].shape) for k in _WEIGHTS]
    return tuple(result)
```
